```python
import jax, jax.numpy as jnp
from jax import lax
import numpy as np

D_MODEL = 1024
BATCH = 8
SEQ = 4096
DEPTH = 1

MEM_LEN = 256
MIX_WIDTH = D_MODEL
A_WIDTH = MIX_WIDTH // 2
B_WIDTH = MIX_WIDTH - A_WIDTH
A_HEADS = 4
A_HEAD_DIM = A_WIDTH // A_HEADS
B_HEADS = 4
CHUNK = 128
CONV_W = 3
IN_A = 2 * A_WIDTH
IN_B = 3 * B_WIDTH
IN_TOTAL = IN_A + IN_B
X_HEADS = 4
X_HEAD_DIM = D_MODEL // X_HEADS
D_FF = ((8 * D_MODEL // 3 + 255) // 256) * 256
EPS = 1e-6

kernel_name = "hybrid_sgu_shortconv_xattn_layer"


def rms_norm(x, g):
    xf = x.astype(jnp.float32)
    y = xf * lax.rsqrt(jnp.mean(xf * xf, axis=-1, keepdims=True) + EPS)
    return (y * g.astype(jnp.float32)).astype(x.dtype)


def layer_norm(x, g, b):
    xf = x.astype(jnp.float32)
    mu = jnp.mean(xf, axis=-1, keepdims=True)
    var = jnp.mean(jnp.square(xf - mu), axis=-1, keepdims=True)
    y = (xf - mu) * lax.rsqrt(var + EPS)
    return (y * g.astype(jnp.float32) + b.astype(jnp.float32)).astype(x.dtype)


def spatial_gating(a, sgu_ln_g, sgu_ln_b, w_spatial, b_spatial):
    bsz, seq, _ = a.shape
    a = jax.nn.gelu(a)
    u, v = jnp.split(a, 2, axis=-1)
    v = layer_norm(v, sgu_ln_g, sgu_ln_b)
    n_chunks = seq // CHUNK
    v = v.reshape(bsz, n_chunks, CHUNK, A_HEADS, A_HEAD_DIM)
    mask = jnp.tril(jnp.ones((CHUNK, CHUNK), dtype=w_spatial.dtype))
    w = w_spatial * mask[None]
    mixed = jnp.einsum("hts,bnshd->bnthd", w, v)
    mixed = mixed + jnp.transpose(b_spatial)[None, None, :, :, None]
    mixed = mixed.reshape(bsz, seq, A_WIDTH)
    return u * mixed


def short_gated_conv(h, conv_w):
    gate_b, gate_c, val = jnp.split(h, 3, axis=-1)
    z = gate_c * val
    zp = jnp.pad(z, ((0, 0), (CONV_W - 1, 0), (0, 0)))
    seq = z.shape[1]
    conv = (conv_w[0] * zp[:, 0:seq] + conv_w[1] * zp[:, 1:seq + 1]
            + conv_w[2] * zp[:, 2:seq + 2])
    return gate_b * conv


def cross_attention(h, memn, w_q, w_kv, w_o):
    bsz, seq, _ = h.shape
    q = (h @ w_q).reshape(bsz, seq, X_HEADS, X_HEAD_DIM)
    k, v = jnp.split(memn @ w_kv, 2, axis=-1)
    k = k.reshape(bsz, MEM_LEN, X_HEADS, X_HEAD_DIM)
    v = v.reshape(bsz, MEM_LEN, X_HEADS, X_HEAD_DIM)
    scale = X_HEAD_DIM ** -0.5
    s = jnp.einsum("bshd,bmhd->bhsm", q, k).astype(jnp.float32) * scale
    p = jax.nn.softmax(s, axis=-1).astype(v.dtype)
    o = jnp.einsum("bhsm,bmhd->bshd", p, v).reshape(bsz, seq, D_MODEL)
    return o @ w_o


def _fwd_setup_inputs(seed: int = 0) -> dict:
    key = jax.random.key(seed)
    ks = jax.random.split(key, 24)
    f32 = jnp.float32
    nrm = lambda k, shape, scale: jax.random.normal(k, shape, f32) * scale
    gain = lambda k, n: 1.0 + 0.02 * jax.random.normal(k, (n,), f32)
    return {
        "x": jax.random.normal(ks[0], (BATCH, SEQ, D_MODEL), f32),
        "mem": jax.random.normal(ks[1], (BATCH, MEM_LEN, D_MODEL), f32),
        "ln_mix_g": gain(ks[2], D_MODEL),
        "w_in": nrm(ks[3], (D_MODEL, IN_TOTAL), D_MODEL ** -0.5),
        "sgu_ln_g": gain(ks[4], A_WIDTH),
        "sgu_ln_b": nrm(ks[5], (A_WIDTH,), 0.02),
        "w_spatial": nrm(ks[6], (A_HEADS, CHUNK, CHUNK), CHUNK ** -0.5),
        "b_spatial": 1.0 + nrm(ks[7], (A_HEADS, CHUNK), 0.02),
        "conv_w": nrm(ks[8], (CONV_W, B_WIDTH), CONV_W ** -0.5),
        "grp_norm_a": gain(ks[9], A_WIDTH),
        "grp_norm_b": gain(ks[10], B_WIDTH),
        "w_out": nrm(ks[11], (MIX_WIDTH, D_MODEL), MIX_WIDTH ** -0.5),
        "ln_attn_g": gain(ks[12], D_MODEL),
        "ln_mem_g": gain(ks[13], D_MODEL),
        "w_q": nrm(ks[14], (D_MODEL, D_MODEL), D_MODEL ** -0.5),
        "w_kv": nrm(ks[15], (D_MODEL, 2 * D_MODEL), D_MODEL ** -0.5),
        "w_o": nrm(ks[16], (D_MODEL, D_MODEL), D_MODEL ** -0.5),
        "ln_ffn_g": gain(ks[17], D_MODEL),
        "w_gate_up": nrm(ks[18], (D_MODEL, 2 * D_FF), D_MODEL ** -0.5),
        "w_down": nrm(ks[19], (D_FF, D_MODEL), D_FF ** -0.5),
        "ln_final_g": gain(ks[20], D_MODEL),
    }


def _fwd_reference(x, mem, ln_mix_g, w_in, sgu_ln_g, sgu_ln_b, w_spatial, b_spatial,
              conv_w, grp_norm_a, grp_norm_b, w_out, ln_attn_g, ln_mem_g,
              w_q, w_kv, w_o, ln_ffn_g, w_gate_up, w_down, ln_final_g):
    memn = rms_norm(mem, ln_mem_g)
    for _ in range(DEPTH):
        h = rms_norm(x, ln_mix_g) @ w_in
        h_a = h[..., :IN_A]
        h_b = h[..., IN_A:]
        y_a = rms_norm(spatial_gating(h_a, sgu_ln_g, sgu_ln_b, w_spatial, b_spatial), grp_norm_a)
        y_b = rms_norm(short_gated_conv(h_b, conv_w), grp_norm_b)
        x = x + jnp.concatenate([y_a, y_b], axis=-1) @ w_out
        x = x + cross_attention(rms_norm(x, ln_attn_g), memn, w_q, w_kv, w_o)
        g, u = jnp.split(rms_norm(x, ln_ffn_g) @ w_gate_up, 2, axis=-1)
        x = x + (jax.nn.silu(g) * u) @ w_down
    return rms_norm(x, ln_final_g)


import jax as _jax
import jax.numpy as _jnp

TWIN_FORMAT = 'train_step'
FWD_PARAMS = ['x', 'mem', 'ln_mix_g', 'w_in', 'sgu_ln_g', 'sgu_ln_b', 'w_spatial', 'b_spatial', 'conv_w', 'grp_norm_a', 'grp_norm_b', 'w_out', 'ln_attn_g', 'ln_mem_g', 'w_q', 'w_kv', 'w_o', 'ln_ffn_g', 'w_gate_up', 'w_down', 'ln_final_g']
TWIN_WEIGHTS = ['ln_mix_g', 'w_in', 'sgu_ln_g', 'sgu_ln_b', 'w_spatial', 'b_spatial', 'conv_w', 'grp_norm_a', 'grp_norm_b', 'w_out', 'ln_attn_g', 'ln_mem_g', 'w_q', 'w_kv', 'w_o', 'ln_ffn_g', 'w_gate_up', 'w_down', 'ln_final_g']
TWIN_DIFF_INPUT = 'x'
TWIN_INPUTS = ['x', 'mem', 'ln_mix_g', 'w_in', 'sgu_ln_g', 'sgu_ln_b', 'w_spatial', 'b_spatial', 'conv_w', 'grp_norm_a', 'grp_norm_b', 'w_out', 'ln_attn_g', 'ln_mem_g', 'w_q', 'w_kv', 'w_o', 'ln_ffn_g', 'w_gate_up', 'w_down', 'ln_final_g', 'loss_target', 'm_ln_mix_g', 'm_w_in', 'm_sgu_ln_g', 'm_sgu_ln_b', 'm_w_spatial', 'm_b_spatial', 'm_conv_w', 'm_grp_norm_a', 'm_grp_norm_b', 'm_w_out', 'm_ln_attn_g', 'm_ln_mem_g', 'm_w_q', 'm_w_kv', 'm_w_o', 'm_ln_ffn_g', 'm_w_gate_up', 'm_w_down', 'm_ln_final_g', 'v_ln_mix_g', 'v_w_in', 'v_sgu_ln_g', 'v_sgu_ln_b', 'v_w_spatial', 'v_b_spatial', 'v_conv_w', 'v_grp_norm_a', 'v_grp_norm_b', 'v_w_out', 'v_ln_attn_g', 'v_ln_mem_g', 'v_w_q', 'v_w_kv', 'v_w_o', 'v_ln_ffn_g', 'v_w_gate_up', 'v_w_down', 'v_ln_final_g']
TWIN_OUTPUTS = ['loss', 'grad_x', 'grad_ln_mix_g', 'grad_w_in', 'grad_sgu_ln_g', 'grad_sgu_ln_b', 'grad_w_spatial', 'grad_b_spatial', 'grad_conv_w', 'grad_grp_norm_a', 'grad_grp_norm_b', 'grad_w_out', 'grad_ln_attn_g', 'grad_ln_mem_g', 'grad_w_q', 'grad_w_kv', 'grad_w_o', 'grad_ln_ffn_g', 'grad_w_gate_up', 'grad_w_down', 'grad_ln_final_g', 'delta_ln_mix_g', 'delta_w_in', 'delta_sgu_ln_g', 'delta_sgu_ln_b', 'delta_w_spatial', 'delta_b_spatial', 'delta_conv_w', 'delta_grp_norm_a', 'delta_grp_norm_b', 'delta_w_out', 'delta_ln_attn_g', 'delta_ln_mem_g', 'delta_w_q', 'delta_w_kv', 'delta_w_o', 'delta_ln_ffn_g', 'delta_w_gate_up', 'delta_w_down', 'delta_ln_final_g', 'new_m_ln_mix_g', 'new_m_w_in', 'new_m_sgu_ln_g', 'new_m_sgu_ln_b', 'new_m_w_spatial', 'new_m_b_spatial', 'new_m_conv_w', 'new_m_grp_norm_a', 'new_m_grp_norm_b', 'new_m_w_out', 'new_m_ln_attn_g', 'new_m_ln_mem_g', 'new_m_w_q', 'new_m_w_kv', 'new_m_w_o', 'new_m_ln_ffn_g', 'new_m_w_gate_up', 'new_m_w_down', 'new_m_ln_final_g', 'new_v_ln_mix_g', 'new_v_w_in', 'new_v_sgu_ln_g', 'new_v_sgu_ln_b', 'new_v_w_spatial', 'new_v_b_spatial', 'new_v_conv_w', 'new_v_grp_norm_a', 'new_v_grp_norm_b', 'new_v_w_out', 'new_v_ln_attn_g', 'new_v_ln_mem_g', 'new_v_w_q', 'new_v_w_kv', 'new_v_w_o', 'new_v_ln_ffn_g', 'new_v_w_gate_up', 'new_v_w_down', 'new_v_ln_final_g']
TWIN_LEAF_KINDS = {'loss': 'loss', 'grad_x': 'grad_x', 'grad_ln_mix_g': 'grad_w', 'grad_w_in': 'grad_w', 'grad_sgu_ln_g': 'grad_w', 'grad_sgu_ln_b': 'grad_w', 'grad_w_spatial': 'grad_w', 'grad_b_spatial': 'grad_w', 'grad_conv_w': 'grad_w', 'grad_grp_norm_a': 'grad_w', 'grad_grp_norm_b': 'grad_w', 'grad_w_out': 'grad_w', 'grad_ln_attn_g': 'grad_w', 'grad_ln_mem_g': 'grad_w', 'grad_w_q': 'grad_w', 'grad_w_kv': 'grad_w', 'grad_w_o': 'grad_w', 'grad_ln_ffn_g': 'grad_w', 'grad_w_gate_up': 'grad_w', 'grad_w_down': 'grad_w', 'grad_ln_final_g': 'grad_w', 'delta_ln_mix_g': 'delta_w', 'delta_w_in': 'delta_w', 'delta_sgu_ln_g': 'delta_w', 'delta_sgu_ln_b': 'delta_w', 'delta_w_spatial': 'delta_w', 'delta_b_spatial': 'delta_w', 'delta_conv_w': 'delta_w', 'delta_grp_norm_a': 'delta_w', 'delta_grp_norm_b': 'delta_w', 'delta_w_out': 'delta_w', 'delta_ln_attn_g': 'delta_w', 'delta_ln_mem_g': 'delta_w', 'delta_w_q': 'delta_w', 'delta_w_kv': 'delta_w', 'delta_w_o': 'delta_w', 'delta_ln_ffn_g': 'delta_w', 'delta_w_gate_up': 'delta_w', 'delta_w_down': 'delta_w', 'delta_ln_final_g': 'delta_w', 'new_m_ln_mix_g': 'new_m', 'new_m_w_in': 'new_m', 'new_m_sgu_ln_g': 'new_m', 'new_m_sgu_ln_b': 'new_m', 'new_m_w_spatial': 'new_m', 'new_m_b_spatial': 'new_m', 'new_m_conv_w': 'new_m', 'new_m_grp_norm_a': 'new_m', 'new_m_grp_norm_b': 'new_m', 'new_m_w_out': 'new_m', 'new_m_ln_attn_g': 'new_m', 'new_m_ln_mem_g': 'new_m', 'new_m_w_q': 'new_m', 'new_m_w_kv': 'new_m', 'new_m_w_o': 'new_m', 'new_m_ln_ffn_g': 'new_m', 'new_m_w_gate_up': 'new_m', 'new_m_w_down': 'new_m', 'new_m_ln_final_g': 'new_m', 'new_v_ln_mix_g': 'new_v', 'new_v_w_in': 'new_v', 'new_v_sgu_ln_g': 'new_v', 'new_v_sgu_ln_b': 'new_v', 'new_v_w_spatial': 'new_v', 'new_v_b_spatial': 'new_v', 'new_v_conv_w': 'new_v', 'new_v_grp_norm_a': 'new_v', 'new_v_grp_norm_b': 'new_v', 'new_v_w_out': 'new_v', 'new_v_ln_attn_g': 'new_v', 'new_v_ln_mem_g': 'new_v', 'new_v_w_q': 'new_v', 'new_v_w_kv': 'new_v', 'new_v_w_o': 'new_v', 'new_v_ln_ffn_g': 'new_v', 'new_v_w_gate_up': 'new_v', 'new_v_w_down': 'new_v', 'new_v_ln_final_g': 'new_v'}


def _forward(args):
    return _fwd_reference(*[args[k] for k in FWD_PARAMS])


def _output_shape():
    out = _jax.eval_shape(lambda: _forward(_fwd_setup_inputs(0)))
    return out.shape, out.dtype

N_MICROBATCH = 1
ADAM_LR = 0.001
ADAM_B1 = 0.9
ADAM_B2 = 0.999
ADAM_EPS = 1e-08
ADAM_WD = 0.01
ADAM_STEP = 10
PER_EXAMPLE_BATCH_AXIS = {'x': 0, 'mem': 0, 'loss_target': 0}
SHARED_INPUTS = []
_WEIGHT_DTYPES = {'ln_mix_g': _jnp.float32, 'w_in': _jnp.float32, 'sgu_ln_g': _jnp.float32, 'sgu_ln_b': _jnp.float32, 'w_spatial': _jnp.float32, 'b_spatial': _jnp.float32, 'conv_w': _jnp.float32, 'grp_norm_a': _jnp.float32, 'grp_norm_b': _jnp.float32, 'w_out': _jnp.float32, 'ln_attn_g': _jnp.float32, 'ln_mem_g': _jnp.float32, 'w_q': _jnp.float32, 'w_kv': _jnp.float32, 'w_o': _jnp.float32, 'ln_ffn_g': _jnp.float32, 'w_gate_up': _jnp.float32, 'w_down': _jnp.float32, 'ln_final_g': _jnp.float32}
MOMENT_SCALE = {'ln_mix_g': 2.024795e-01, 'w_in': 1.289393e-01, 'sgu_ln_g': 7.485136e-02, 'sgu_ln_b': 8.926593e-02, 'w_spatial': 7.520222e-02, 'b_spatial': 1.111330e-01, 'conv_w': 1.470439e-01, 'grp_norm_a': 1.332131e-01, 'grp_norm_b': 1.335259e-01, 'w_out': 1.365311e-01, 'ln_attn_g': 1.457558e-02, 'ln_mem_g': 2.114628e-02, 'w_q': 1.421687e-02, 'w_kv': 1.423097e-02, 'w_o': 1.443340e-02, 'ln_ffn_g': 1.039631e-01, 'w_gate_up': 4.236743e-02, 'w_down': 6.900552e-02, 'ln_final_g': 3.204279e+01}


def _to_microbatches(a, axis):
    t = _jnp.moveaxis(a, axis, 0)
    t = t.reshape((N_MICROBATCH, t.shape[0] // N_MICROBATCH) + t.shape[1:])
    return _jnp.moveaxis(t, 1, axis + 1)


def setup_inputs(seed: int = 0) -> dict:
    inp = _fwd_setup_inputs(seed)
    key = _jax.random.fold_in(_jax.random.key(seed), 7919)
    shape, _ = _output_shape()
    out = dict(inp)
    out["loss_target"] = _jax.random.normal(_jax.random.fold_in(key, 0), shape, _jnp.float32)
    for i, name in enumerate(TWIN_WEIGHTS):
        w = inp[name].astype(_jnp.float32)
        if MOMENT_SCALE is None:
            s = _jnp.sqrt(_jnp.mean(_jnp.square(w)) + 1e-30)
        else:
            s = MOMENT_SCALE[name]
        km, kv = _jax.random.split(_jax.random.fold_in(key, i + 1))
        out[name] = w
        out["m_" + name] = s * _jax.random.normal(km, w.shape, _jnp.float32)
        out["v_" + name] = (s * s) * _jax.random.uniform(kv, w.shape, _jnp.float32, 0.5, 1.5)
    if N_MICROBATCH > 1:
        for name, axis in PER_EXAMPLE_BATCH_AXIS.items():
            out[name] = _to_microbatches(out[name], axis)
    return {'x': out['x'], 'mem': out['mem'], 'ln_mix_g': out['ln_mix_g'], 'w_in': out['w_in'], 'sgu_ln_g': out['sgu_ln_g'], 'sgu_ln_b': out['sgu_ln_b'], 'w_spatial': out['w_spatial'], 'b_spatial': out['b_spatial'], 'conv_w': out['conv_w'], 'grp_norm_a': out['grp_norm_a'], 'grp_norm_b': out['grp_norm_b'], 'w_out': out['w_out'], 'ln_attn_g': out['ln_attn_g'], 'ln_mem_g': out['ln_mem_g'], 'w_q': out['w_q'], 'w_kv': out['w_kv'], 'w_o': out['w_o'], 'ln_ffn_g': out['ln_ffn_g'], 'w_gate_up': out['w_gate_up'], 'w_down': out['w_down'], 'ln_final_g': out['ln_final_g'], 'loss_target': out['loss_target'], 'm_ln_mix_g': out['m_ln_mix_g'], 'm_w_in': out['m_w_in'], 'm_sgu_ln_g': out['m_sgu_ln_g'], 'm_sgu_ln_b': out['m_sgu_ln_b'], 'm_w_spatial': out['m_w_spatial'], 'm_b_spatial': out['m_b_spatial'], 'm_conv_w': out['m_conv_w'], 'm_grp_norm_a': out['m_grp_norm_a'], 'm_grp_norm_b': out['m_grp_norm_b'], 'm_w_out': out['m_w_out'], 'm_ln_attn_g': out['m_ln_attn_g'], 'm_ln_mem_g': out['m_ln_mem_g'], 'm_w_q': out['m_w_q'], 'm_w_kv': out['m_w_kv'], 'm_w_o': out['m_w_o'], 'm_ln_ffn_g': out['m_ln_ffn_g'], 'm_w_gate_up': out['m_w_gate_up'], 'm_w_down': out['m_w_down'], 'm_ln_final_g': out['m_ln_final_g'], 'v_ln_mix_g': out['v_ln_mix_g'], 'v_w_in': out['v_w_in'], 'v_sgu_ln_g': out['v_sgu_ln_g'], 'v_sgu_ln_b': out['v_sgu_ln_b'], 'v_w_spatial': out['v_w_spatial'], 'v_b_spatial': out['v_b_spatial'], 'v_conv_w': out['v_conv_w'], 'v_grp_norm_a': out['v_grp_norm_a'], 'v_grp_norm_b': out['v_grp_norm_b'], 'v_w_out': out['v_w_out'], 'v_ln_attn_g': out['v_ln_attn_g'], 'v_ln_mem_g': out['v_ln_mem_g'], 'v_w_q': out['v_w_q'], 'v_w_kv': out['v_w_kv'], 'v_w_o': out['v_w_o'], 'v_ln_ffn_g': out['v_ln_ffn_g'], 'v_w_gate_up': out['v_w_gate_up'], 'v_w_down': out['v_w_down'], 'v_ln_final_g': out['v_ln_final_g']}


def _loss(weights, diff, rest, loss_target):
    with _jax.named_scope("forward"):
        args = {**rest, TWIN_DIFF_INPUT: diff, **{k: w.astype(_WEIGHT_DTYPES[k]) for k, w in weights.items()}}
        y = _forward(args)
    with _jax.named_scope("loss_head"):
        err = _jnp.square(y.astype(_jnp.float32) - loss_target)
        return 0.5 * _jnp.sum(_jnp.mean(err, axis=-1)) if err.ndim else 0.5 * err


def _adamw(w, g, m, v):
    m = ADAM_B1 * m + (1.0 - ADAM_B1) * g
    v = ADAM_B2 * v + (1.0 - ADAM_B2) * _jnp.square(g)
    m_hat = m / (1.0 - ADAM_B1 ** ADAM_STEP)
    v_hat = v / (1.0 - ADAM_B2 ** ADAM_STEP)
    delta = -ADAM_LR * (m_hat / (_jnp.sqrt(v_hat) + ADAM_EPS) + ADAM_WD * w)
    return delta, m, v


def reference(x, mem, ln_mix_g, w_in, sgu_ln_g, sgu_ln_b, w_spatial, b_spatial, conv_w, grp_norm_a, grp_norm_b, w_out, ln_attn_g, ln_mem_g, w_q, w_kv, w_o, ln_ffn_g, w_gate_up, w_down, ln_final_g, loss_target, m_ln_mix_g, m_w_in, m_sgu_ln_g, m_sgu_ln_b, m_w_spatial, m_b_spatial, m_conv_w, m_grp_norm_a, m_grp_norm_b, m_w_out, m_ln_attn_g, m_ln_mem_g, m_w_q, m_w_kv, m_w_o, m_ln_ffn_g, m_w_gate_up, m_w_down, m_ln_final_g, v_ln_mix_g, v_w_in, v_sgu_ln_g, v_sgu_ln_b, v_w_spatial, v_b_spatial, v_conv_w, v_grp_norm_a, v_grp_norm_b, v_w_out, v_ln_attn_g, v_ln_mem_g, v_w_q, v_w_kv, v_w_o, v_ln_ffn_g, v_w_gate_up, v_w_down, v_ln_final_g):
    given = dict(x=x, mem=mem, ln_mix_g=ln_mix_g, w_in=w_in, sgu_ln_g=sgu_ln_g, sgu_ln_b=sgu_ln_b, w_spatial=w_spatial, b_spatial=b_spatial, conv_w=conv_w, grp_norm_a=grp_norm_a, grp_norm_b=grp_norm_b, w_out=w_out, ln_attn_g=ln_attn_g, ln_mem_g=ln_mem_g, w_q=w_q, w_kv=w_kv, w_o=w_o, ln_ffn_g=ln_ffn_g, w_gate_up=w_gate_up, w_down=w_down, ln_final_g=ln_final_g, loss_target=loss_target, m_ln_mix_g=m_ln_mix_g, m_w_in=m_w_in, m_sgu_ln_g=m_sgu_ln_g, m_sgu_ln_b=m_sgu_ln_b, m_w_spatial=m_w_spatial, m_b_spatial=m_b_spatial, m_conv_w=m_conv_w, m_grp_norm_a=m_grp_norm_a, m_grp_norm_b=m_grp_norm_b, m_w_out=m_w_out, m_ln_attn_g=m_ln_attn_g, m_ln_mem_g=m_ln_mem_g, m_w_q=m_w_q, m_w_kv=m_w_kv, m_w_o=m_w_o, m_ln_ffn_g=m_ln_ffn_g, m_w_gate_up=m_w_gate_up, m_w_down=m_w_down, m_ln_final_g=m_ln_final_g, v_ln_mix_g=v_ln_mix_g, v_w_in=v_w_in, v_sgu_ln_g=v_sgu_ln_g, v_sgu_ln_b=v_sgu_ln_b, v_w_spatial=v_w_spatial, v_b_spatial=v_b_spatial, v_conv_w=v_conv_w, v_grp_norm_a=v_grp_norm_a, v_grp_norm_b=v_grp_norm_b, v_w_out=v_w_out, v_ln_attn_g=v_ln_attn_g, v_ln_mem_g=v_ln_mem_g, v_w_q=v_w_q, v_w_kv=v_w_kv, v_w_o=v_w_o, v_ln_ffn_g=v_ln_ffn_g, v_w_gate_up=v_w_gate_up, v_w_down=v_w_down, v_ln_final_g=v_ln_final_g)
    weights = {n: given[n] for n in TWIN_WEIGHTS}
    shared = {n: given[n] for n in SHARED_INPUTS}
    per_example = {n: given[n] for n in ['x', 'mem']}
    grad_fn = _jax.value_and_grad(_loss, argnums=(0, 1))

    def one_microbatch(ex, loss_target):
        ex = dict(ex)
        diff = ex.pop(TWIN_DIFF_INPUT)
        return grad_fn(weights, diff, {**shared, **ex}, loss_target)

    if N_MICROBATCH == 1:
        loss, (grad_w, grad_x) = one_microbatch(per_example, given["loss_target"])
    else:
        def body(carry, xs):
            loss_sum, grad_sum = carry
            l_k, (gw_k, gx_k) = one_microbatch(xs[0], xs[1])
            with _jax.named_scope("update"):
                return (loss_sum + l_k, _jax.tree.map(_jnp.add, grad_sum, gw_k)), gx_k

        init = (_jnp.zeros((), _jnp.float32), _jax.tree.map(_jnp.zeros_like, weights))
        (loss, grad_w), grad_x = _jax.lax.scan(body, init, (per_example, given["loss_target"]))
    with _jax.named_scope("update"):
        delta_w, new_m, new_v = {}, {}, {}
        for n in TWIN_WEIGHTS:
            delta_w[n], new_m[n], new_v[n] = _adamw(weights[n], grad_w[n], given["m_" + n], given["v_" + n])
    return (loss, grad_x, *[grad_w[n] for n in TWIN_WEIGHTS], *[delta_w[n] for n in TWIN_WEIGHTS],
            *[new_m[n] for n in TWIN_WEIGHTS], *[new_v[n] for n in TWIN_WEIGHTS])
```

```python
import functools
import math

import jax
import jax.numpy as jnp
from jax import lax
from jax.experimental import pallas as pl
from jax.experimental.pallas import tpu as pltpu

F32 = jnp.float32
BF16 = jnp.bfloat16
EPS = 1e-6
CHUNK = 128
HEADS = 4
N_CHIPS = 4
TM = 256
ADAM_LR, ADAM_B1, ADAM_B2, ADAM_EPS, ADAM_WD, ADAM_STEP = 0.001, 0.9, 0.999, 1e-08, 0.01, 10
GELU_C = math.sqrt(2.0 / math.pi)
GELU_K = 0.044715
SMALL_ROWS = 80
VMEM_LIMIT = 56 * 1024 * 1024
MESH = pl.DeviceIdType.MESH
ANY = pl.BlockSpec(memory_space=pl.ANY)


def _params(*sem):
    return pltpu.CompilerParams(dimension_semantics=sem, vmem_limit_bytes=VMEM_LIMIT)


def _dot(a, b):
    return jnp.dot(a, b, preferred_element_type=F32)


def _dot_nt(a, b):
    return lax.dot_general(a, b, (((1,), (1,)), ((), ())), preferred_element_type=F32)


def _dot_tn(a, b):
    return lax.dot_general(a, b, (((0,), (0,)), ((), ())), preferred_element_type=F32)


def _rms_fwd(x, g):
    r = lax.rsqrt(jnp.mean(x * x, axis=-1, keepdims=True) + EPS)
    xh = x * r
    return xh * g, xh, r


def _rms_bwd(dy, xh, r, g):
    dxh = dy * g
    dx = r * (dxh - xh * jnp.mean(dxh * xh, axis=-1, keepdims=True))
    return dx, jnp.sum(dy * xh, axis=0, keepdims=True)


def _full(shape):
    nd = len(shape)
    return pl.BlockSpec(shape, lambda *_: (0,) * nd, pipeline_mode=pl.Buffered(1))


def _acc(shape):
    nd = len(shape)
    return pl.BlockSpec(shape, lambda *_: (0,) * nd)


def _rows(tm, cols):
    return pl.BlockSpec((tm, cols), lambda i: (i, 0))


def _tril_weights(wsp_ref):
    row = lax.broadcasted_iota(jnp.int32, (CHUNK, CHUNK), 0)
    col = lax.broadcasted_iota(jnp.int32, (CHUNK, CHUNK), 1)
    return [jnp.where(row >= col, wsp_ref[hd], 0.0).astype(BF16) for hd in range(HEADS)]


def _shift_rows(z, zp):
    row = lax.broadcasted_iota(jnp.int32, z.shape, 0)
    zm1 = jnp.where(row == 0, zp[7:8, :], pltpu.roll(z, 1, 0))
    zm2 = jnp.where(row == 0, zp[6:7, :], jnp.where(row == 1, zp[7:8, :], pltpu.roll(z, 2, 0)))
    return zm1, zm2


def _gelu_parts(x):
    t = jnp.tanh(GELU_C * (x + GELU_K * (x * x * x)))
    return 0.5 * x * (1.0 + t), t


def _layer_norm_parts(v, g, b):
    mu = jnp.mean(v, axis=-1, keepdims=True)
    vc = v - mu
    rs = lax.rsqrt(jnp.mean(vc * vc, axis=-1, keepdims=True) + EPS)
    vhat = vc * rs
    return vhat * g + b, vhat, rs


def _kv_fwd(mem, g_mem, w_kv):
    m, d = mem.shape
    ns = w_kv.shape[2]

    def body(mem_ref, g_ref, w_ref, memn_ref, kv_ref):
        y, _, _ = _rms_fwd(mem_ref[...], g_ref[...])
        yb = y.astype(BF16)
        memn_ref[...] = yb
        for k in range(N_CHIPS):
            kv_ref[:, k * ns:(k + 1) * ns] = _dot(yb, w_ref[k]).astype(BF16)

    return pl.pallas_call(
        body, name="kv_fwd",
        out_shape=(jax.ShapeDtypeStruct((m, d), BF16), jax.ShapeDtypeStruct((m, N_CHIPS * ns), BF16)),
        compiler_params=pltpu.CompilerParams(vmem_limit_bytes=VMEM_LIMIT),
    )(mem, g_mem, w_kv)


def _mixer_fwd(x, g1, w_in, lng, lnb, wsp, b_t, cw, ga, gb, w_out):
    s, d = x.shape
    n = s // TM
    nch = TM // CHUNK
    ns = w_in.shape[2]
    nh = N_CHIPS * ns
    aw = d // 2
    hd_w = aw // HEADS

    def body(x_ref, g1_ref, win_ref, lng_ref, lnb_ref, wsp_ref, bt_ref, cw_ref, ga_ref, gb_ref, wout_ref,
             h_ref, x1_ref, y_ref, xn_ref, mix_ref, zp_ref):
        i = pl.program_id(0)

        @pl.when(i == 0)
        def _():
            zp_ref[...] = jnp.zeros_like(zp_ref)

        x = x_ref[...]
        xn, _, _ = _rms_fwd(x, g1_ref[...])
        xnb = xn.astype(BF16)
        xn_ref[...] = xnb
        for k in range(N_CHIPS):
            h_ref[:, k * ns:(k + 1) * ns] = _dot(xnb, win_ref[k])
        a, _ = _gelu_parts(h_ref[:, 0:2 * aw])
        u = a[:, :aw]
        vn, _, _ = _layer_norm_parts(a[:, aw:], lng_ref[...], lnb_ref[...])
        vnb = vn.astype(BF16)
        wm = _tril_weights(wsp_ref)
        for c in range(nch):
            for hd in range(HEADS):
                blk = vnb[c * CHUNK:(c + 1) * CHUNK, hd * hd_w:(hd + 1) * hd_w]
                mix_ref[c * CHUNK:(c + 1) * CHUNK, hd * hd_w:(hd + 1) * hd_w] = _dot(wm[hd], blk) + bt_ref[:, hd:hd + 1]
        ya, _, _ = _rms_fwd(u * mix_ref[...], ga_ref[...])
        g_b = h_ref[:, 2 * aw:3 * aw]
        z = h_ref[:, 3 * aw:4 * aw] * h_ref[:, 4 * aw:5 * aw]
        zm1, zm2 = _shift_rows(z, zp_ref[...])
        conv = cw_ref[0:1, :] * zm2 + cw_ref[1:2, :] * zm1 + cw_ref[2:3, :] * z
        yb, _, _ = _rms_fwd(g_b * conv, gb_ref[...])
        zp_ref[...] = z[TM - 8:TM, :]
        ycat = jnp.concatenate([ya, yb], axis=-1).astype(BF16)
        y_ref[...] = ycat
        x1_ref[...] = x + _dot(ycat, wout_ref[...])

    return pl.pallas_call(
        body, name="mixer_fwd", grid=(n,),
        in_specs=[_rows(TM, d), _full(g1.shape), _full(w_in.shape), _full(lng.shape), _full(lnb.shape),
                  _full(wsp.shape), _full(b_t.shape), _full(cw.shape), _full(ga.shape), _full(gb.shape),
                  _full(w_out.shape)],
        out_specs=[_rows(TM, nh), _rows(TM, d), _rows(TM, d), _rows(TM, d)],
        out_shape=(jax.ShapeDtypeStruct((s, nh), F32), jax.ShapeDtypeStruct((s, d), F32),
                   jax.ShapeDtypeStruct((s, d), BF16), jax.ShapeDtypeStruct((s, d), BF16)),
        scratch_shapes=[pltpu.VMEM((TM, aw), F32), pltpu.VMEM((8, aw), F32)],
        compiler_params=_params("arbitrary"),
    )(x, g1, w_in, lng, lnb, wsp, b_t, cw, ga, gb, w_out)


def _attn_probs(qb, kv_ref, hd, dh, d, scale):
    kh = kv_ref[:, hd * dh:(hd + 1) * dh]
    vh = kv_ref[:, d + hd * dh:d + (hd + 1) * dh]
    sc = _dot_nt(qb, kh) * scale
    sc = sc - jnp.max(sc, axis=-1, keepdims=True)
    e = jnp.exp(sc)
    return e / jnp.sum(e, axis=-1, keepdims=True), kh, vh


def _attn_fwd(x1, g2, w_q, kv, w_o):
    s, d = x1.shape
    n = s // TM
    dh = d // HEADS
    scale = dh ** -0.5

    def body(x1_ref, g2_ref, wq_ref, kv_ref, wo_ref, x2_ref, o_ref, xn_ref):
        x1v = x1_ref[...]
        xn, _, _ = _rms_fwd(x1v, g2_ref[...])
        xnb = xn.astype(BF16)
        xn_ref[...] = xnb
        q = _dot(xnb, wq_ref[...])
        for hd in range(HEADS):
            qb = q[:, hd * dh:(hd + 1) * dh].astype(BF16)
            p, _, vh = _attn_probs(qb, kv_ref, hd, dh, d, scale)
            o_ref[:, hd * dh:(hd + 1) * dh] = _dot(p.astype(BF16), vh).astype(BF16)
        x2_ref[...] = x1v + _dot(o_ref[...], wo_ref[...])

    return pl.pallas_call(
        body, name="attn_fwd", grid=(n,),
        in_specs=[_rows(TM, d), _full(g2.shape), _full(w_q.shape), _full(kv.shape), _full(w_o.shape)],
        out_specs=[_rows(TM, d), _rows(TM, d), _rows(TM, d)],
        out_shape=(jax.ShapeDtypeStruct((s, d), F32), jax.ShapeDtypeStruct((s, d), BF16),
                   jax.ShapeDtypeStruct((s, d), BF16)),
        compiler_params=_params("parallel"),
    )(x1, g2, w_q, kv, w_o)


def _ffn_fwd_bwd(x2, g3, gf, target, w_gu, w_down):
    s, d = x2.shape
    n = s // TM
    ns = w_gu.shape[2]
    ff = 2 * ns

    def body(x2_ref, g3_ref, gf_ref, t_ref, wgu_ref, wd_ref,
             dx2_ref, act_ref, dgu_ref, xn_ref, dx3_ref, loss_ref, dgf_ref, dg3_ref):
        i = pl.program_id(0)

        @pl.when(i == 0)
        def _():
            loss_ref[...] = jnp.zeros_like(loss_ref)
            dgf_ref[...] = jnp.zeros_like(dgf_ref)
            dg3_ref[...] = jnp.zeros_like(dg3_ref)

        x2v = x2_ref[...]
        xn, xh3, r3 = _rms_fwd(x2v, g3_ref[...])
        xnb = xn.astype(BF16)
        xn_ref[...] = xnb
        x3 = x2v
        saved = []
        for j in range(2):
            g = _dot(xnb, wgu_ref[j])
            u = _dot(xnb, wgu_ref[2 + j])
            sg = 1.0 / (1.0 + jnp.exp(-g))
            sl = g * sg
            actb = (sl * u).astype(BF16)
            act_ref[:, j * ns:(j + 1) * ns] = actb
            x3 = x3 + _dot(actb, wd_ref[j * ns:(j + 1) * ns, :])
            saved.append((u, sl, sg * (1.0 + g * (1.0 - sg))))
        gfv = gf_ref[...]
        y, xhf, rf = _rms_fwd(x3, gfv)
        e = y - t_ref[...]
        loss_ref[...] += 0.5 * jnp.sum(jnp.sum(e * e, axis=-1, keepdims=True), axis=0, keepdims=True) / d
        dx3, dgf = _rms_bwd(e / d, xhf, rf, gfv)
        dgf_ref[...] += dgf
        dx3b = dx3.astype(BF16)
        dx3_ref[...] = dx3b
        dxn = jnp.zeros_like(x2v)
        for j in range(2):
            u, sl, dsl = saved[j]
            dact = _dot_nt(dx3b, wd_ref[j * ns:(j + 1) * ns, :])
            dgb = (dact * u * dsl).astype(BF16)
            dub = (dact * sl).astype(BF16)
            dgu_ref[:, j * ns:(j + 1) * ns] = dgb
            dgu_ref[:, ff + j * ns:ff + (j + 1) * ns] = dub
            dxn = dxn + _dot_nt(dgb, wgu_ref[j]) + _dot_nt(dub, wgu_ref[2 + j])
        dxr, dg3 = _rms_bwd(dxn, xh3, r3, g3_ref[...])
        dg3_ref[...] += dg3
        dx2_ref[...] = dx3 + dxr

    vec = jax.ShapeDtypeStruct((1, d), F32)
    return pl.pallas_call(
        body, name="ffn_fwd_bwd", grid=(n,),
        in_specs=[_rows(TM, d), _full(g3.shape), _full(gf.shape), _rows(TM, d), _full(w_gu.shape),
                  _full(w_down.shape)],
        out_specs=[_rows(TM, d), _rows(TM, ff), _rows(TM, 2 * ff), _rows(TM, d), _rows(TM, d),
                   _acc((1, 1)), _acc((1, d)), _acc((1, d))],
        out_shape=(jax.ShapeDtypeStruct((s, d), F32), jax.ShapeDtypeStruct((s, ff), BF16),
                   jax.ShapeDtypeStruct((s, 2 * ff), BF16), jax.ShapeDtypeStruct((s, d), BF16),
                   jax.ShapeDtypeStruct((s, d), BF16), jax.ShapeDtypeStruct((1, 1), F32), vec, vec),
        compiler_params=_params("arbitrary"),
    )(x2, g3, gf, target, w_gu, w_down)


def _attn_bwd(x1, dx2, g2, w_q, kv, w_o):
    s, d = x1.shape
    n = s // TM
    dh = d // HEADS
    scale = dh ** -0.5
    m = kv.shape[0]

    def body(x1_ref, dx2_ref, g2_ref, wq_ref, kv_ref, wo_ref, dx1_ref, dq_ref, dkv_ref, dg2_ref):
        i = pl.program_id(0)

        @pl.when(i == 0)
        def _():
            dkv_ref[...] = jnp.zeros_like(dkv_ref)
            dg2_ref[...] = jnp.zeros_like(dg2_ref)

        xn, xh2, r2 = _rms_fwd(x1_ref[...], g2_ref[...])
        q = _dot(xn.astype(BF16), wq_ref[...])
        dx2v = dx2_ref[...]
        do = _dot_nt(dx2v.astype(BF16), wo_ref[...])
        for hd in range(HEADS):
            qb = q[:, hd * dh:(hd + 1) * dh].astype(BF16)
            p, kh, vh = _attn_probs(qb, kv_ref, hd, dh, d, scale)
            dob = do[:, hd * dh:(hd + 1) * dh].astype(BF16)
            dp = _dot_nt(dob, vh)
            ds = p * (dp - jnp.sum(dp * p, axis=-1, keepdims=True))
            dsb = (ds * scale).astype(BF16)
            dq_ref[:, hd * dh:(hd + 1) * dh] = _dot(dsb, kh).astype(BF16)
            dkv_ref[:, hd * dh:(hd + 1) * dh] += _dot_tn(dsb, qb)
            dkv_ref[:, d + hd * dh:d + (hd + 1) * dh] += _dot_tn(p.astype(BF16), dob)
        dxn = _dot_nt(dq_ref[...], wq_ref[...])
        dxr, dg2 = _rms_bwd(dxn, xh2, r2, g2_ref[...])
        dg2_ref[...] += dg2
        dx1_ref[...] = dx2v + dxr

    return pl.pallas_call(
        body, name="attn_bwd", grid=(n,),
        in_specs=[_rows(TM, d), _rows(TM, d), _full(g2.shape), _full(w_q.shape), _full(kv.shape),
                  _full(w_o.shape)],
        out_specs=[_rows(TM, d), _rows(TM, d), _acc((m, 2 * d)), _acc((1, d))],
        out_shape=(jax.ShapeDtypeStruct((s, d), F32), jax.ShapeDtypeStruct((s, d), BF16),
                   jax.ShapeDtypeStruct((m, 2 * d), F32), jax.ShapeDtypeStruct((1, d), F32)),
        compiler_params=_params("arbitrary"),
    )(x1, dx2, g2, w_q, kv, w_o)


def _kv_bwd(dkv, mem, memn, g_mem, w_kv):
    m, d = mem.shape
    ns = w_kv.shape[2]

    def body(dkv_ref, mem_ref, memn_ref, g_ref, w_ref, gw_ref, dg_ref):
        _, xh, _ = _rms_fwd(mem_ref[...], g_ref[...])
        dmemn = jnp.zeros((m, d), F32)
        for k in range(N_CHIPS):
            dkb = dkv_ref[:, k * ns:(k + 1) * ns].astype(BF16)
            gw_ref[k] = _dot_tn(memn_ref[...], dkb).astype(BF16)
            dmemn = dmemn + _dot_nt(dkb, w_ref[k])
        dg_ref[...] = jnp.sum(dmemn * xh, axis=0, keepdims=True)

    return pl.pallas_call(
        body, name="kv_bwd",
        out_shape=(jax.ShapeDtypeStruct((N_CHIPS, d, ns), BF16), jax.ShapeDtypeStruct((1, d), F32)),
        compiler_params=pltpu.CompilerParams(vmem_limit_bytes=VMEM_LIMIT),
    )(dkv, mem, memn, g_mem, w_kv)


def _mixer_bwd(x, dx1, h, g1, lng, lnb, wsp, b_t, cw, ga, gb, w_out, w_in):
    s, d = x.shape
    n = s // TM
    nch = TM // CHUNK
    ns = w_in.shape[2]
    nh = N_CHIPS * ns
    aw = d // 2
    hd_w = aw // HEADS

    def rev(cols):
        return pl.BlockSpec((TM, cols), lambda i: (n - 1 - i, 0))

    hprev = pl.BlockSpec((8, nh), lambda i: (jnp.maximum((n - 1 - i) * (TM // 8) - 1, 0), 0))

    def body(x_ref, dx1_ref, h_ref, hp_ref, g1_ref, lng_ref, lnb_ref, wsp_ref, bt_ref, cw_ref, ga_ref, gb_ref,
             wout_ref, win_ref,
             dx_ref, dh_ref, dg1_ref, dlng_ref, dlnb_ref, dwsp_ref, dbt_ref, dcw_ref, dga_ref, dgb_ref,
             mix_ref, dvn_ref, dcn_ref):
        i = pl.program_id(0)

        @pl.when(i == 0)
        def _():
            for r in (dg1_ref, dlng_ref, dlnb_ref, dwsp_ref, dbt_ref, dcw_ref, dga_ref, dgb_ref, dcn_ref):
                r[...] = jnp.zeros_like(r)

        dx1v = dx1_ref[...]
        dycat = _dot_nt(dx1v.astype(BF16), wout_ref[...])
        ha = h_ref[:, 0:2 * aw]
        a, th = _gelu_parts(ha)
        u = a[:, :aw]
        lngv = lng_ref[...]
        vn, vhat, rs = _layer_norm_parts(a[:, aw:], lngv, lnb_ref[...])
        vnb = vn.astype(BF16)
        wm = _tril_weights(wsp_ref)
        for c in range(nch):
            for hd in range(HEADS):
                blk = vnb[c * CHUNK:(c + 1) * CHUNK, hd * hd_w:(hd + 1) * hd_w]
                mix_ref[c * CHUNK:(c + 1) * CHUNK, hd * hd_w:(hd + 1) * hd_w] = _dot(wm[hd], blk) + bt_ref[:, hd:hd + 1]
        mixed = mix_ref[...]
        gav = ga_ref[...]
        _, yah, ra = _rms_fwd(u * mixed, gav)
        dya, dga = _rms_bwd(dycat[:, :aw], yah, ra, gav)
        dga_ref[...] += dga
        du = dya * mixed
        dmix = dya * u
        dmb = dmix.astype(BF16)
        tri = lax.broadcasted_iota(jnp.int32, (CHUNK, CHUNK), 0) >= lax.broadcasted_iota(jnp.int32, (CHUNK, CHUNK), 1)
        for hd in range(HEADS):
            dw = jnp.zeros((CHUNK, CHUNK), F32)
            db = jnp.zeros((CHUNK, 1), F32)
            for c in range(nch):
                rows = slice(c * CHUNK, (c + 1) * CHUNK)
                cols = slice(hd * hd_w, (hd + 1) * hd_w)
                dvn_ref[rows, cols] = _dot_tn(wm[hd], dmb[rows, cols])
                dw = dw + _dot_nt(dmb[rows, cols], vnb[rows, cols])
                db = db + jnp.sum(dmix[rows, cols], axis=1, keepdims=True)
            dwsp_ref[hd] += jnp.where(tri, dw, 0.0)
            dbt_ref[:, hd:hd + 1] += db
        dvn = dvn_ref[...]
        dlng_ref[...] += jnp.sum(dvn * vhat, axis=0, keepdims=True)
        dlnb_ref[...] += jnp.sum(dvn, axis=0, keepdims=True)
        dvh = dvn * lngv
        dv = rs * (dvh - jnp.mean(dvh, axis=-1, keepdims=True) - vhat * jnp.mean(dvh * vhat, axis=-1, keepdims=True))
        gprime = 0.5 * (1.0 + th) + 0.5 * ha * (1.0 - th * th) * (GELU_C * (1.0 + 3.0 * GELU_K * (ha * ha)))
        dh_ref[:, 0:2 * aw] = (jnp.concatenate([du, dv], axis=-1) * gprime).astype(BF16)
        g_b = h_ref[:, 2 * aw:3 * aw]
        g_c = h_ref[:, 3 * aw:4 * aw]
        val = h_ref[:, 4 * aw:5 * aw]
        z = g_c * val
        zp = jnp.where(i == n - 1, 0.0, hp_ref[:, 3 * aw:4 * aw] * hp_ref[:, 4 * aw:5 * aw])
        zm1, zm2 = _shift_rows(z, zp)
        cw0, cw1, cw2 = cw_ref[0:1, :], cw_ref[1:2, :], cw_ref[2:3, :]
        conv = cw0 * zm2 + cw1 * zm1 + cw2 * z
        gbv = gb_ref[...]
        _, ybh, rb = _rms_fwd(g_b * conv, gbv)
        dyb, dgb = _rms_bwd(dycat[:, aw:], ybh, rb, gbv)
        dgb_ref[...] += dgb
        dconv = dyb * g_b
        dcw_ref[0:1, :] += jnp.sum(dconv * zm2, axis=0, keepdims=True)
        dcw_ref[1:2, :] += jnp.sum(dconv * zm1, axis=0, keepdims=True)
        dcw_ref[2:3, :] += jnp.sum(dconv * z, axis=0, keepdims=True)
        nxt = dcn_ref[...]
        row = lax.broadcasted_iota(jnp.int32, dconv.shape, 0)
        dcp1 = jnp.where(row == TM - 1, nxt[0:1, :], pltpu.roll(dconv, TM - 1, 0))
        dcp2 = jnp.where(row == TM - 1, nxt[1:2, :],
                         jnp.where(row == TM - 2, nxt[0:1, :], pltpu.roll(dconv, TM - 2, 0)))
        dz = cw2 * dconv + cw1 * dcp1 + cw0 * dcp2
        dcn_ref[...] = dconv[0:8, :]
        dh_ref[:, 2 * aw:3 * aw] = (dyb * conv).astype(BF16)
        dh_ref[:, 3 * aw:4 * aw] = (dz * val).astype(BF16)
        dh_ref[:, 4 * aw:5 * aw] = (dz * g_c).astype(BF16)
        dxn = jnp.zeros((TM, d), F32)
        for k in range(N_CHIPS):
            dxn = dxn + _dot_nt(dh_ref[:, k * ns:(k + 1) * ns], win_ref[k])
        g1v = g1_ref[...]
        _, xh1, r1 = _rms_fwd(x_ref[...], g1v)
        dxr, dg1 = _rms_bwd(dxn, xh1, r1, g1v)
        dg1_ref[...] += dg1
        dx_ref[...] = dx1v + dxr

    ins = (x, dx1, h, h, g1, lng, lnb, wsp, b_t, cw, ga, gb, w_out, w_in)
    acc_shapes = [(1, d), (1, aw), (1, aw), wsp.shape, (CHUNK, CHUNK), cw.shape, (1, aw), (1, aw)]
    return pl.pallas_call(
        body, name="mixer_bwd", grid=(n,),
        in_specs=[rev(d), rev(d), rev(nh), hprev] + [_full(a.shape) for a in ins[4:]],
        out_specs=[rev(d), rev(nh)] + [_acc(sh) for sh in acc_shapes],
        out_shape=(jax.ShapeDtypeStruct((s, d), F32), jax.ShapeDtypeStruct((s, nh), BF16))
        + tuple(jax.ShapeDtypeStruct(sh, F32) for sh in acc_shapes),
        scratch_shapes=[pltpu.VMEM((TM, aw), F32), pltpu.VMEM((TM, aw), F32), pltpu.VMEM((8, aw), F32)],
        compiler_params=_params("arbitrary"),
    )(*ins)


def _weight_grad(a, b, name, tm, tn, col_sharded):
    t, m = a.shape
    n = b.shape[1]

    def body(a_ref, b_ref, o_ref):
        o_ref[...] = _dot_tn(a_ref[...].astype(BF16), b_ref[...].astype(BF16)).astype(BF16)

    if col_sharded:
        ns = n // N_CHIPS
        per = ns // tn
        out_shape = jax.ShapeDtypeStruct((N_CHIPS, m, ns), BF16)
        out_spec = pl.BlockSpec((None, tm, tn), lambda i, j: (j // per, i, j % per))
    else:
        out_shape = jax.ShapeDtypeStruct((m, n), BF16)
        out_spec = pl.BlockSpec((tm, tn), lambda i, j: (i, j))
    out = pl.pallas_call(
        body, name=name, grid=(m // tm, n // tn),
        in_specs=[pl.BlockSpec((t, tm), lambda i, j: (0, i)), pl.BlockSpec((t, tn), lambda i, j: (0, j))],
        out_specs=out_spec, out_shape=out_shape,
        compiler_params=_params("parallel", "parallel"),
    )(a, b)
    return out if col_sharded else out.reshape(N_CHIPS, m // N_CHIPS, n)


def _row_tile(rows, cap=256):
    best = None
    for t in range(16, min(rows, cap) + 1, 16):
        if rows % t == 0:
            best = t
    return best if best is not None else rows


def _sum_arrays(arrs, out_dtype, name):
    r, c = arrs[0].shape
    tr = _row_tile(r)

    def body(*refs):
        acc = refs[0][...].astype(F32)
        for ref in refs[1:-1]:
            acc = acc + ref[...].astype(F32)
        refs[-1][...] = acc.astype(out_dtype)

    return pl.pallas_call(
        body, name=name, grid=(r // tr,),
        in_specs=[_rows(tr, c)] * len(arrs), out_specs=_rows(tr, c),
        out_shape=jax.ShapeDtypeStruct((r, c), out_dtype),
        compiler_params=_params("parallel"),
    )(*arrs)


def _sum_slots(arr, order, name):
    _, r, c = arr.shape
    tr = _row_tile(r)

    def body(*refs):
        acc = refs[0][...].astype(F32)
        for ref in refs[1:-1]:
            acc = acc + ref[...].astype(F32)
        refs[-1][...] = acc

    def slot(k):
        return pl.BlockSpec((None, tr, c), lambda i: (k, i, 0))

    return pl.pallas_call(
        body, name=name, grid=(r // tr,),
        in_specs=[slot(k) for k in order], out_specs=_rows(tr, c),
        out_shape=jax.ShapeDtypeStruct((r, c), F32),
        compiler_params=_params("parallel"),
    )(*([arr] * len(order)))


def _adamw_math(w, g, m, v):
    m2 = ADAM_B1 * m + (1.0 - ADAM_B1) * g
    v2 = ADAM_B2 * v + (1.0 - ADAM_B2) * (g * g)
    m_hat = m2 / (1.0 - ADAM_B1 ** ADAM_STEP)
    v_hat = v2 / (1.0 - ADAM_B2 ** ADAM_STEP)
    delta = -ADAM_LR * (m_hat / (jnp.sqrt(v_hat) + ADAM_EPS) + ADAM_WD * w)
    return delta, m2, v2


def _adamw(w, g, m, v, name):
    r, c = w.shape
    tr = _row_tile(r) if r >= 16 else r

    def body(w_ref, g_ref, m_ref, v_ref, d_ref, m2_ref, v2_ref):
        d_ref[...], m2_ref[...], v2_ref[...] = _adamw_math(w_ref[...], g_ref[...], m_ref[...], v_ref[...])

    sh = jax.ShapeDtypeStruct((r, c), F32)
    return pl.pallas_call(
        body, name=name, grid=(r // tr,),
        in_specs=[_rows(tr, c)] * 4, out_specs=[_rows(tr, c)] * 3, out_shape=(sh, sh, sh),
        compiler_params=_params("parallel"),
    )(w, g, m, v)


def _small_sum_adamw(parts, w, m, v):
    nd, r, c = parts.shape

    def body(p_ref, w_ref, m_ref, v_ref, g_ref, d_ref, m2_ref, v2_ref):
        g = p_ref[0]
        for k in range(1, nd):
            g = g + p_ref[k]
        g_ref[...] = g
        d_ref[...], m2_ref[...], v2_ref[...] = _adamw_math(w_ref[...], g, m_ref[...], v_ref[...])

    sh = jax.ShapeDtypeStruct((r, c), F32)
    return pl.pallas_call(
        body, name="small_sum_adamw", out_shape=(sh, sh, sh, sh),
        compiler_params=pltpu.CompilerParams(vmem_limit_bytes=VMEM_LIMIT),
    )(parts, w, m, v)


def _place():
    x, y, c = lax.axis_index("x"), lax.axis_index("y"), lax.axis_index("c")
    chips = [(1 - x, y), (x, 1 - y), (1 - x, 1 - y)]
    return x, y, c, 2 * x + y, chips


def _remote(src, dst, send_sem, recv_sem, to):
    return pltpu.make_async_remote_copy(src_ref=src, dst_ref=dst, send_sem=send_sem, recv_sem=recv_sem,
                                        device_id=to, device_id_type=MESH)


def _all_gather_weights(shards, conv8):
    nw = len(shards)

    def body(*refs):
        ins, cin = refs[:nw], refs[nw]
        outs, cout = refs[nw + 1:2 * nw + 1], refs[2 * nw + 1]
        send_sems, recv_sems, loc_sems, csend, crecv = refs[2 * nw + 2:]
        x, y, c, me, chips = _place()
        sib = (x, y, 1 - c)
        local = [pltpu.make_async_copy(ins[w], outs[w].at[me], loc_sems.at[w]) for w in range(nw)]
        local.append(pltpu.make_async_copy(cin, cout.at[me], loc_sems.at[nw]))
        for cp in local:
            cp.start()

        def half(w):
            rh = shards[w].shape[0] // 2
            return pl.ds(c * rh, rh)

        sends = []
        for r, (px, py) in enumerate(chips):
            for w in range(nw):
                sends.append(_remote(ins[w].at[half(w)], outs[w].at[me, half(w)],
                                     send_sems.at[w, r], recv_sems.at[w, r], (px, py, c)))
            sends.append(_remote(cin, cout.at[me], csend.at[r], crecv.at[r], (px, py, c)))
        for cp in sends:
            cp.start()
        passed = []
        for r, (px, py) in enumerate(chips):
            pk = 2 * px + py
            for w in range(nw):
                blk = outs[w].at[pk, half(w)]
                _remote(blk, blk, send_sems.at[w, r], recv_sems.at[w, r], (px, py, c)).wait_recv()
                fwd = _remote(blk, blk, send_sems.at[w, 3 + r], recv_sems.at[w, 3 + r], sib)
                fwd.start()
                passed.append(fwd)
            _remote(cin, cout.at[pk], csend.at[r], crecv.at[r], (px, py, c)).wait_recv()
        for r, (px, py) in enumerate(chips):
            pk = 2 * px + py
            for w in range(nw):
                rh = shards[w].shape[0] // 2
                blk = outs[w].at[pk, pl.ds((1 - c) * rh, rh)]
                _remote(blk, blk, send_sems.at[w, 3 + r], recv_sems.at[w, 3 + r], sib).wait_recv()
        for cp in sends + passed:
            cp.wait_send()
        for cp in local:
            cp.wait()

    out_shape = tuple(jax.ShapeDtypeStruct((N_CHIPS,) + s.shape, s.dtype) for s in shards) + (
        jax.ShapeDtypeStruct((N_CHIPS,) + conv8.shape, conv8.dtype),)
    return pl.pallas_call(
        body, name="all_gather_weights", out_shape=out_shape,
        in_specs=[ANY] * (nw + 1), out_specs=[ANY] * (nw + 1),
        scratch_shapes=[pltpu.SemaphoreType.DMA((nw, 6)), pltpu.SemaphoreType.DMA((nw, 6)),
                        pltpu.SemaphoreType.DMA((nw + 1,)), pltpu.SemaphoreType.DMA((3,)),
                        pltpu.SemaphoreType.DMA((3,))],
    )(*shards, conv8)


def _exchange_halves(grads):
    nw = len(grads)

    def body(*refs):
        ins = refs[:nw]
        own, got = refs[nw:2 * nw], refs[2 * nw:3 * nw]
        send_sems, recv_sems, loc_sems = refs[3 * nw:]
        x, y, c, _, _ = _place()
        sib = (x, y, 1 - c)
        cps = []
        for w in range(nw):
            rh = grads[w].shape[1] // 2
            lc = pltpu.make_async_copy(ins[w].at[:, pl.ds(c * rh, rh), :], own[w], loc_sems.at[w])
            lc.start()
            rc = _remote(ins[w].at[:, pl.ds((1 - c) * rh, rh), :], got[w], send_sems.at[w], recv_sems.at[w], sib)
            rc.start()
            cps.append((lc, rc))
        for lc, rc in cps:
            rc.wait()
            lc.wait()

    halves = tuple(jax.ShapeDtypeStruct((g.shape[0], g.shape[1] // 2, g.shape[2]), g.dtype) for g in grads)
    res = pl.pallas_call(
        body, name="rs_exchange_halves", out_shape=halves + halves,
        in_specs=[ANY] * nw, out_specs=[ANY] * (2 * nw),
        scratch_shapes=[pltpu.SemaphoreType.DMA((nw,)), pltpu.SemaphoreType.DMA((nw,)),
                        pltpu.SemaphoreType.DMA((nw,))],
    )(*grads)
    return res[:nw], res[nw:]


def _scatter_partials(parts):
    nw = len(parts)

    def body(*refs):
        ins, outs = refs[:nw], refs[nw:2 * nw]
        send_sems, recv_sems, loc_sems = refs[2 * nw:]
        _, _, c, me, chips = _place()
        cps = []
        for w in range(nw):
            lc = pltpu.make_async_copy(ins[w].at[me], outs[w].at[3], loc_sems.at[w])
            lc.start()
            cps.append(lc)
        rcs = []
        for r, (px, py) in enumerate(chips):
            for w in range(nw):
                rc = _remote(ins[w].at[2 * px + py], outs[w].at[r], send_sems.at[w, r], recv_sems.at[w, r],
                             (px, py, c))
                rc.start()
                rcs.append(rc)
        for rc in rcs:
            rc.wait()
        for lc in cps:
            lc.wait()

    return pl.pallas_call(
        body, name="rs_scatter_partials",
        out_shape=tuple(jax.ShapeDtypeStruct(p.shape, p.dtype) for p in parts),
        in_specs=[ANY] * nw, out_specs=[ANY] * nw,
        scratch_shapes=[pltpu.SemaphoreType.DMA((nw, 3)), pltpu.SemaphoreType.DMA((nw, 3)),
                        pltpu.SemaphoreType.DMA((nw,))],
    )(*parts)


def _join_halves(halves):
    nw = len(halves)

    def body(*refs):
        ins, outs = refs[:nw], refs[nw:2 * nw]
        send_sems, recv_sems, loc_sems = refs[2 * nw:]
        x, y, c, _, _ = _place()
        cps = []
        for w in range(nw):
            rh = halves[w].shape[0]
            dst = outs[w].at[pl.ds(c * rh, rh)]
            lc = pltpu.make_async_copy(ins[w], dst, loc_sems.at[w])
            lc.start()
            rc = _remote(ins[w], dst, send_sems.at[w], recv_sems.at[w], (x, y, 1 - c))
            rc.start()
            cps.append((lc, rc))
        for lc, rc in cps:
            rc.wait_send()
            lc.wait()
        for w in range(nw):
            rh = halves[w].shape[0]
            dst = outs[w].at[pl.ds((1 - c) * rh, rh)]
            _remote(ins[w], dst, send_sems.at[w], recv_sems.at[w], (x, y, 1 - c)).wait_recv()

    return pl.pallas_call(
        body, name="rs_join_halves",
        out_shape=tuple(jax.ShapeDtypeStruct((2 * h.shape[0], h.shape[1]), h.dtype) for h in halves),
        in_specs=[ANY] * nw, out_specs=[ANY] * nw,
        scratch_shapes=[pltpu.SemaphoreType.DMA((nw,)), pltpu.SemaphoreType.DMA((nw,)),
                        pltpu.SemaphoreType.DMA((nw,))],
    )(*halves)


def _gather_small(slab):
    def body(in_ref, out_ref, send_sems, recv_sems, loc_sem):
        x, y, c, _, _ = _place()
        me = 4 * x + 2 * y + c
        lc = pltpu.make_async_copy(in_ref, out_ref.at[me], loc_sem)
        lc.start()
        rcs = []
        for k in range(1, 8):
            px = 1 - x if k & 4 else x
            py = 1 - y if k & 2 else y
            pc = 1 - c if k & 1 else c
            rc = _remote(in_ref, out_ref.at[me], send_sems.at[k - 1], recv_sems.at[k - 1], (px, py, pc))
            rc.start()
            rcs.append((rc, 4 * px + 2 * py + pc))
        for rc, _ in rcs:
            rc.wait_send()
        for k, (_, peer) in enumerate(rcs):
            _remote(in_ref, out_ref.at[peer], send_sems.at[k], recv_sems.at[k], (x, y, c)).wait_recv()
        lc.wait()

    return pl.pallas_call(
        body, name="gather_small", out_shape=jax.ShapeDtypeStruct((8,) + slab.shape, slab.dtype),
        in_specs=[ANY], out_specs=ANY,
        scratch_shapes=[pltpu.SemaphoreType.DMA((7,)), pltpu.SemaphoreType.DMA((7,)), pltpu.SemaphoreType.DMA],
    )(slab)


_SMALL_VECS = ("ln_mix_g", "ln_attn_g", "ln_mem_g", "ln_ffn_g", "ln_final_g")


def _pack_small(p, extra, conv):
    d = p["ln_mix_g"].shape[-1]
    top = [p[k].reshape(1, d) for k in _SMALL_VECS]
    top.append(jnp.concatenate([p["sgu_ln_g"].reshape(-1), p["sgu_ln_b"].reshape(-1)]).reshape(1, d))
    top.append(jnp.concatenate([p["grp_norm_a"].reshape(-1), p["grp_norm_b"].reshape(-1)]).reshape(1, d))
    top.append(jnp.concatenate([p["b_spatial"].reshape(-1), extra]).reshape(1, d))
    mid = jnp.zeros((8, d), F32)
    if conv is not None:
        mid = jnp.pad(conv, ((0, 5), (0, d - conv.shape[1])))
    return jnp.concatenate([jnp.concatenate(top, axis=0), mid, p["w_spatial"].reshape(-1, d)], axis=0)


def _unpack_small(slab):
    d = slab.shape[1]
    hw = d // 2
    out = {k: slab[i] for i, k in enumerate(_SMALL_VECS)}
    out["sgu_ln_g"], out["sgu_ln_b"] = slab[5, :hw], slab[5, hw:]
    out["grp_norm_a"], out["grp_norm_b"] = slab[6, :hw], slab[6, hw:]
    out["b_spatial"] = slab[7, :hw].reshape(HEADS, CHUNK)
    out["w_spatial"] = slab[16:].reshape(HEADS, CHUNK, CHUNK)
    return out


_BIG = ("w_in", "w_kv", "w_gate_up", "w_out", "w_q", "w_o", "w_down")
_WEIGHTS = ("ln_mix_g", "w_in", "sgu_ln_g", "sgu_ln_b", "w_spatial", "b_spatial", "conv_w", "grp_norm_a",
            "grp_norm_b", "w_out", "ln_attn_g", "ln_mem_g", "w_q", "w_kv", "w_o", "ln_ffn_g", "w_gate_up",
            "w_down", "ln_final_g")


def _step(p, m_, v_, x, mem, target):
    s, d = x.shape
    hw = d // 2
    row = lambda a: a.reshape(1, -1)
    x_, y_, c_ = lax.axis_index("x"), lax.axis_index("y"), lax.axis_index("c")
    chip = 2 * x_ + y_

    conv8 = jnp.pad(p["conv_w"], ((0, 5), (0, 0)))
    full = _all_gather_weights([p[k].astype(BF16) for k in _BIG], conv8)
    w_in, w_kv, w_gu, w_out4, w_q4, w_o4, w_down4, conv4 = full
    w_out, w_q, w_o = (a.reshape(-1, d) for a in (w_out4, w_q4, w_o4))
    w_down = w_down4.reshape(-1, d)
    cw = jnp.transpose(conv4[:, :3, :], (1, 0, 2)).reshape(3, hw)
    b_t = jnp.pad(jnp.transpose(p["b_spatial"]), ((0, 0), (0, CHUNK - HEADS)))
    g1, g2, gm, g3, gf = (row(p[k]) for k in _SMALL_VECS)
    lng, lnb, ga, gb = row(p["sgu_ln_g"]), row(p["sgu_ln_b"]), row(p["grp_norm_a"]), row(p["grp_norm_b"])
    wsp = p["w_spatial"]

    memn, kv = _kv_fwd(mem, gm, w_kv)
    h, x1, ycat, xn1 = _mixer_fwd(x, g1, w_in, lng, lnb, wsp, b_t, cw, ga, gb, w_out)
    x2, o, xn2 = _attn_fwd(x1, g2, w_q, kv, w_o)
    dx2, act, dgu, xn3, dx3, loss, dgf, dg3 = _ffn_fwd_bwd(x2, g3, gf, target, w_gu, w_down)
    dx1, dq, dkv, dg2 = _attn_bwd(x1, dx2, g2, w_q, kv, w_o)
    g_kv, dgm = _kv_bwd(dkv, mem, memn, gm, w_kv)
    dx, dh, dg1, dlng, dlnb, dwsp, dbt, dcw, dga, dgb = _mixer_bwd(
        x, dx1, h, g1, lng, lnb, wsp, b_t, cw, ga, gb, w_out, w_in)
    grads = [
        _weight_grad(xn1, dh, "grad_w_in", 512, 640, True),
        g_kv,
        _weight_grad(xn3, dgu, "grad_w_gate_up", 512, 1408, True),
        _weight_grad(ycat, dx1, "grad_w_out", 512, 512, False),
        _weight_grad(xn2, dq, "grad_w_q", 512, 512, False),
        _weight_grad(o, dx2, "grad_w_o", 512, 512, False),
        _weight_grad(act, dx3, "grad_w_down", 1408, 512, False),
    ]

    own, got = _exchange_halves(grads)
    flat = lambda a: a.reshape(-1, a.shape[-1])
    chip_parts = [_sum_arrays([flat(a), flat(b)], BF16, "chip_sum_" + k).reshape(a.shape)
                  for k, a, b in zip(_BIG, own, got)]
    slots = _scatter_partials(chip_parts)
    halves = [_sum_slots(sl, (3, 0, 1, 2), "final_sum_" + k) for k, sl in zip(_BIG, slots)]
    big_g = dict(zip(_BIG, _join_halves(halves)))

    small = {"ln_mix_g": dg1, "ln_attn_g": dg2, "ln_mem_g": dgm, "ln_ffn_g": dg3, "ln_final_g": dgf,
             "sgu_ln_g": dlng, "sgu_ln_b": dlnb, "grp_norm_a": dga, "grp_norm_b": dgb,
             "b_spatial": jnp.transpose(dbt[:, :HEADS]), "w_spatial": dwsp}
    loss_vec = jnp.pad(loss.reshape(1), (0, hw - 1))
    parts = _gather_small(_pack_small(small, loss_vec, dcw))
    zeros = jnp.zeros((hw,), F32)
    sg, sd, sm, sv = _small_sum_adamw(parts, _pack_small(p, zeros, None), _pack_small(m_, zeros, None),
                                      _pack_small(v_, zeros, None))
    out_g, out_d, out_m, out_v = (_unpack_small(a) for a in (sg, sd, sm, sv))
    loss_out = sg[7, hw]
    g_conv = lax.dynamic_slice(sg[8:11, :hw], (0, chip * (hw // N_CHIPS)), (3, hw // N_CHIPS))
    out_g["conv_w"] = g_conv
    out_d["conv_w"], out_m["conv_w"], out_v["conv_w"] = _adamw(p["conv_w"], g_conv, m_["conv_w"], v_["conv_w"],
                                                                "adamw_conv_w")
    for k in _BIG:
        out_g[k] = big_g[k]
        out_d[k], out_m[k], out_v[k] = _adamw(p[k], big_g[k], m_[k], v_[k], "adamw_" + k)
    return loss_out, dx, out_g, out_d, out_m, out_v


def kernel(x, mem, ln_mix_g, w_in, sgu_ln_g, sgu_ln_b, w_spatial, b_spatial, conv_w, grp_norm_a, grp_norm_b, w_out, ln_attn_g, ln_mem_g, w_q, w_kv, w_o, ln_ffn_g, w_gate_up, w_down, ln_final_g, loss_target, m_ln_mix_g, m_w_in, m_sgu_ln_g, m_sgu_ln_b, m_w_spatial, m_b_spatial, m_conv_w, m_grp_norm_a, m_grp_norm_b, m_w_out, m_ln_attn_g, m_ln_mem_g, m_w_q, m_w_kv, m_w_o, m_ln_ffn_g, m_w_gate_up, m_w_down, m_ln_final_g, v_ln_mix_g, v_w_in, v_sgu_ln_g, v_sgu_ln_b, v_w_spatial, v_b_spatial, v_conv_w, v_grp_norm_a, v_grp_norm_b, v_w_out, v_ln_attn_g, v_ln_mem_g, v_w_q, v_w_kv, v_w_o, v_ln_ffn_g, v_w_gate_up, v_w_down, v_ln_final_g):
    p = dict(ln_mix_g=ln_mix_g, w_in=w_in, sgu_ln_g=sgu_ln_g, sgu_ln_b=sgu_ln_b, w_spatial=w_spatial,
             b_spatial=b_spatial, conv_w=conv_w, grp_norm_a=grp_norm_a, grp_norm_b=grp_norm_b, w_out=w_out,
             ln_attn_g=ln_attn_g, ln_mem_g=ln_mem_g, w_q=w_q, w_kv=w_kv, w_o=w_o, ln_ffn_g=ln_ffn_g,
             w_gate_up=w_gate_up, w_down=w_down, ln_final_g=ln_final_g)
    m_ = dict(ln_mix_g=m_ln_mix_g, w_in=m_w_in, sgu_ln_g=m_sgu_ln_g, sgu_ln_b=m_sgu_ln_b, w_spatial=m_w_spatial,
              b_spatial=m_b_spatial, conv_w=m_conv_w, grp_norm_a=m_grp_norm_a, grp_norm_b=m_grp_norm_b,
              w_out=m_w_out, ln_attn_g=m_ln_attn_g, ln_mem_g=m_ln_mem_g, w_q=m_w_q, w_kv=m_w_kv, w_o=m_w_o,
              ln_ffn_g=m_ln_ffn_g, w_gate_up=m_w_gate_up, w_down=m_w_down, ln_final_g=m_ln_final_g)
    v_ = dict(ln_mix_g=v_ln_mix_g, w_in=v_w_in, sgu_ln_g=v_sgu_ln_g, sgu_ln_b=v_sgu_ln_b, w_spatial=v_w_spatial,
              b_spatial=v_b_spatial, conv_w=v_conv_w, grp_norm_a=v_grp_norm_a, grp_norm_b=v_grp_norm_b,
              w_out=v_w_out, ln_attn_g=v_ln_attn_g, ln_mem_g=v_ln_mem_g, w_q=v_w_q, w_kv=v_w_kv, w_o=v_w_o,
              ln_ffn_g=v_ln_ffn_g, w_gate_up=v_w_gate_up, w_down=v_w_down, ln_final_g=v_ln_final_g)
    s, d = x.shape[-2], x.shape[-1]
    loss, dx, g, dl, nm, nv = _step(p, m_, v_, x.reshape(s, d), mem.reshape(-1, d), loss_target.reshape(s, d))
    outs = [loss, dx.reshape(x.shape)]
    for tree in (g, dl, nm, nv):
        outs += [tree[k].reshape(p[k].shape) for k in _WEIGHTS]
    return tuple(outs)
```

```python
import functools
import math

import jax
import jax.numpy as jnp
from jax import lax
from jax.experimental import pallas as pl
from jax.experimental.pallas import tpu as pltpu

F32 = jnp.float32
BF16 = jnp.bfloat16
EPS = 1e-6
CHUNK = 128
HEADS = 4
N_CHIPS = 4
TM = 256
ADAM_LR, ADAM_B1, ADAM_B2, ADAM_EPS, ADAM_WD, ADAM_STEP = 0.001, 0.9, 0.999, 1e-08, 0.01, 10
GELU_C = math.sqrt(2.0 / math.pi)
GELU_K = 0.044715
SMALL_ROWS = 80
VMEM_LIMIT = 56 * 1024 * 1024
MESH = pl.DeviceIdType.MESH
ANY = pl.BlockSpec(memory_space=pl.ANY)


def _params(*sem):
    return pltpu.CompilerParams(dimension_semantics=sem, vmem_limit_bytes=VMEM_LIMIT)


def _dot(a, b):
    return jnp.dot(a, b, preferred_element_type=F32)


def _dot_nt(a, b):
    return lax.dot_general(a, b, (((1,), (1,)), ((), ())), preferred_element_type=F32)


def _dot_tn(a, b):
    return lax.dot_general(a, b, (((0,), (0,)), ((), ())), preferred_element_type=F32)


def _rms_fwd(x, g):
    r = lax.rsqrt(jnp.mean(x * x, axis=-1, keepdims=True) + EPS)
    xh = x * r
    return xh * g, xh, r


def _rms_bwd(dy, xh, r, g):
    dxh = dy * g
    dx = r * (dxh - xh * jnp.mean(dxh * xh, axis=-1, keepdims=True))
    return dx, jnp.sum(dy * xh, axis=0, keepdims=True)


def _full(shape):
    nd = len(shape)
    return pl.BlockSpec(shape, lambda *_: (0,) * nd, pipeline_mode=pl.Buffered(1))


def _acc(shape):
    nd = len(shape)
    return pl.BlockSpec(shape, lambda *_: (0,) * nd)


def _rows(tm, cols):
    return pl.BlockSpec((tm, cols), lambda i: (i, 0))


def _tril_weights(wsp_ref):
    row = lax.broadcasted_iota(jnp.int32, (CHUNK, CHUNK), 0)
    col = lax.broadcasted_iota(jnp.int32, (CHUNK, CHUNK), 1)
    return [jnp.where(row >= col, wsp_ref[hd], 0.0).astype(BF16) for hd in range(HEADS)]


def _shift_rows(z, zp):
    row = lax.broadcasted_iota(jnp.int32, z.shape, 0)
    zm1 = jnp.where(row == 0, zp[7:8, :], pltpu.roll(z, 1, 0))
    zm2 = jnp.where(row == 0, zp[6:7, :], jnp.where(row == 1, zp[7:8, :], pltpu.roll(z, 2, 0)))
    return zm1, zm2


def _gelu_parts(x):
    t = jnp.tanh(GELU_C * (x + GELU_K * (x * x * x)))
    return 0.5 * x * (1.0 + t), t


def _layer_norm_parts(v, g, b):
    mu = jnp.mean(v, axis=-1, keepdims=True)
    vc = v - mu
    rs = lax.rsqrt(jnp.mean(vc * vc, axis=-1, keepdims=True) + EPS)
    vhat = vc * rs
    return vhat * g + b, vhat, rs


def _kv_fwd(mem, g_mem, w_kv):
    m, d = mem.shape
    ns = w_kv.shape[2]

    def body(mem_ref, g_ref, w_ref, memn_ref, kv_ref):
        y, _, _ = _rms_fwd(mem_ref[...], g_ref[...])
        yb = y.astype(BF16)
        memn_ref[...] = yb
        for k in range(N_CHIPS):
            kv_ref[:, k * ns:(k + 1) * ns] = _dot(yb, w_ref[k]).astype(BF16)

    return pl.pallas_call(
        body, name="kv_fwd",
        out_shape=(jax.ShapeDtypeStruct((m, d), BF16), jax.ShapeDtypeStruct((m, N_CHIPS * ns), BF16)),
        compiler_params=pltpu.CompilerParams(vmem_limit_bytes=VMEM_LIMIT),
    )(mem, g_mem, w_kv)


def _mixer_fwd(x, g1, w_in, lng, lnb, wsp, b_t, cw, ga, gb, w_out):
    s, d = x.shape
    n = s // TM
    nch = TM // CHUNK
    ns = w_in.shape[2]
    nh = N_CHIPS * ns
    aw = d // 2
    hd_w = aw // HEADS

    def body(x_ref, g1_ref, win_ref, lng_ref, lnb_ref, wsp_ref, bt_ref, cw_ref, ga_ref, gb_ref, wout_ref,
             h_ref, x1_ref, y_ref, xn_ref, mix_ref, zp_ref):
        i = pl.program_id(0)

        @pl.when(i == 0)
        def _():
            zp_ref[...] = jnp.zeros_like(zp_ref)

        x = x_ref[...]
        xn, _, _ = _rms_fwd(x, g1_ref[...])
        xnb = xn.astype(BF16)
        xn_ref[...] = xnb
        for k in range(N_CHIPS):
            h_ref[:, k * ns:(k + 1) * ns] = _dot(xnb, win_ref[k])
        a, _ = _gelu_parts(h_ref[:, 0:2 * aw])
        u = a[:, :aw]
        vn, _, _ = _layer_norm_parts(a[:, aw:], lng_ref[...], lnb_ref[...])
        vnb = vn.astype(BF16)
        wm = _tril_weights(wsp_ref)
        for c in range(nch):
            for hd in range(HEADS):
                blk = vnb[c * CHUNK:(c + 1) * CHUNK, hd * hd_w:(hd + 1) * hd_w]
                mix_ref[c * CHUNK:(c + 1) * CHUNK, hd * hd_w:(hd + 1) * hd_w] = _dot(wm[hd], blk) + bt_ref[:, hd:hd + 1]
        ya, _, _ = _rms_fwd(u * mix_ref[...], ga_ref[...])
        g_b = h_ref[:, 2 * aw:3 * aw]
        z = h_ref[:, 3 * aw:4 * aw] * h_ref[:, 4 * aw:5 * aw]
        zm1, zm2 = _shift_rows(z, zp_ref[...])
        conv = cw_ref[0:1, :] * zm2 + cw_ref[1:2, :] * zm1 + cw_ref[2:3, :] * z
        yb, _, _ = _rms_fwd(g_b * conv, gb_ref[...])
        zp_ref[...] = z[TM - 8:TM, :]
        ycat = jnp.concatenate([ya, yb], axis=-1).astype(BF16)
        y_ref[...] = ycat
        x1_ref[...] = x + _dot(ycat, wout_ref[...])

    return pl.pallas_call(
        body, name="mixer_fwd", grid=(n,),
        in_specs=[_rows(TM, d), _full(g1.shape), _full(w_in.shape), _full(lng.shape), _full(lnb.shape),
                  _full(wsp.shape), _full(b_t.shape), _full(cw.shape), _full(ga.shape), _full(gb.shape),
                  _full(w_out.shape)],
        out_specs=[_rows(TM, nh), _rows(TM, d), _rows(TM, d), _rows(TM, d)],
        out_shape=(jax.ShapeDtypeStruct((s, nh), F32), jax.ShapeDtypeStruct((s, d), F32),
                   jax.ShapeDtypeStruct((s, d), BF16), jax.ShapeDtypeStruct((s, d), BF16)),
        scratch_shapes=[pltpu.VMEM((TM, aw), F32), pltpu.VMEM((8, aw), F32)],
        compiler_params=_params("arbitrary"),
    )(x, g1, w_in, lng, lnb, wsp, b_t, cw, ga, gb, w_out)


def _attn_probs(qb, kv_ref, hd, dh, d, scale):
    kh = kv_ref[:, hd * dh:(hd + 1) * dh]
    vh = kv_ref[:, d + hd * dh:d + (hd + 1) * dh]
    sc = _dot_nt(qb, kh) * scale
    sc = sc - jnp.max(sc, axis=-1, keepdims=True)
    e = jnp.exp(sc)
    return e / jnp.sum(e, axis=-1, keepdims=True), kh, vh


def _attn_fwd(x1, g2, w_q, kv, w_o):
    s, d = x1.shape
    n = s // TM
    dh = d // HEADS
    scale = dh ** -0.5

    def body(x1_ref, g2_ref, wq_ref, kv_ref, wo_ref, x2_ref, o_ref, xn_ref):
        x1v = x1_ref[...]
        xn, _, _ = _rms_fwd(x1v, g2_ref[...])
        xnb = xn.astype(BF16)
        xn_ref[...] = xnb
        q = _dot(xnb, wq_ref[...])
        for hd in range(HEADS):
            qb = q[:, hd * dh:(hd + 1) * dh].astype(BF16)
            p, _, vh = _attn_probs(qb, kv_ref, hd, dh, d, scale)
            o_ref[:, hd * dh:(hd + 1) * dh] = _dot(p.astype(BF16), vh).astype(BF16)
        x2_ref[...] = x1v + _dot(o_ref[...], wo_ref[...])

    return pl.pallas_call(
        body, name="attn_fwd", grid=(n,),
        in_specs=[_rows(TM, d), _full(g2.shape), _full(w_q.shape), _full(kv.shape), _full(w_o.shape)],
        out_specs=[_rows(TM, d), _rows(TM, d), _rows(TM, d)],
        out_shape=(jax.ShapeDtypeStruct((s, d), F32), jax.ShapeDtypeStruct((s, d), BF16),
                   jax.ShapeDtypeStruct((s, d), BF16)),
        compiler_params=_params("parallel"),
    )(x1, g2, w_q, kv, w_o)


def _ffn_fwd_bwd(x2, g3, gf, target, w_gu, w_down):
    s, d = x2.shape
    n = s // TM
    ns = w_gu.shape[2]
    ff = 2 * ns

    def body(x2_ref, g3_ref, gf_ref, t_ref, wgu_ref, wd_ref,
             dx2_ref, act_ref, dgu_ref, xn_ref, dx3_ref, loss_ref, dgf_ref, dg3_ref):
        i = pl.program_id(0)

        @pl.when(i == 0)
        def _():
            loss_ref[...] = jnp.zeros_like(loss_ref)
            dgf_ref[...] = jnp.zeros_like(dgf_ref)
            dg3_ref[...] = jnp.zeros_like(dg3_ref)

        x2v = x2_ref[...]
        xn, xh3, r3 = _rms_fwd(x2v, g3_ref[...])
        xnb = xn.astype(BF16)
        xn_ref[...] = xnb
        x3 = x2v
        saved = []
        for j in range(2):
            g = _dot(xnb, wgu_ref[j])
            u = _dot(xnb, wgu_ref[2 + j])
            sg = 1.0 / (1.0 + jnp.exp(-g))
            sl = g * sg
            actb = (sl * u).astype(BF16)
            act_ref[:, j * ns:(j + 1) * ns] = actb
            x3 = x3 + _dot(actb, wd_ref[j * ns:(j + 1) * ns, :])
            saved.append((u, sl, sg * (1.0 + g * (1.0 - sg))))
        gfv = gf_ref[...]
        y, xhf, rf = _rms_fwd(x3, gfv)
        e = y - t_ref[...]
        loss_ref[...] += 0.5 * jnp.sum(jnp.sum(e * e, axis=-1, keepdims=True), axis=0, keepdims=True) / d
        dx3, dgf = _rms_bwd(e / d, xhf, rf, gfv)
        dgf_ref[...] += dgf
        dx3b = dx3.astype(BF16)
        dx3_ref[...] = dx3b
        dxn = jnp.zeros_like(x2v)
        for j in range(2):
            u, sl, dsl = saved[j]
            dact = _dot_nt(dx3b, wd_ref[j * ns:(j + 1) * ns, :])
            dgb = (dact * u * dsl).astype(BF16)
            dub = (dact * sl).astype(BF16)
            dgu_ref[:, j * ns:(j + 1) * ns] = dgb
            dgu_ref[:, ff + j * ns:ff + (j + 1) * ns] = dub
            dxn = dxn + _dot_nt(dgb, wgu_ref[j]) + _dot_nt(dub, wgu_ref[2 + j])
        dxr, dg3 = _rms_bwd(dxn, xh3, r3, g3_ref[...])
        dg3_ref[...] += dg3
        dx2_ref[...] = dx3 + dxr

    vec = jax.ShapeDtypeStruct((1, d), F32)
    return pl.pallas_call(
        body, name="ffn_fwd_bwd", grid=(n,),
        in_specs=[_rows(TM, d), _full(g3.shape), _full(gf.shape), _rows(TM, d), _full(w_gu.shape),
                  _full(w_down.shape)],
        out_specs=[_rows(TM, d), _rows(TM, ff), _rows(TM, 2 * ff), _rows(TM, d), _rows(TM, d),
                   _acc((1, 1)), _acc((1, d)), _acc((1, d))],
        out_shape=(jax.ShapeDtypeStruct((s, d), F32), jax.ShapeDtypeStruct((s, ff), BF16),
                   jax.ShapeDtypeStruct((s, 2 * ff), BF16), jax.ShapeDtypeStruct((s, d), BF16),
                   jax.ShapeDtypeStruct((s, d), BF16), jax.ShapeDtypeStruct((1, 1), F32), vec, vec),
        compiler_params=_params("arbitrary"),
    )(x2, g3, gf, target, w_gu, w_down)


def _attn_bwd(x1, dx2, g2, w_q, kv, w_o):
    s, d = x1.shape
    n = s // TM
    dh = d // HEADS
    scale = dh ** -0.5
    m = kv.shape[0]

    def body(x1_ref, dx2_ref, g2_ref, wq_ref, kv_ref, wo_ref, dx1_ref, dq_ref, dkv_ref, dg2_ref):
        i = pl.program_id(0)

        @pl.when(i == 0)
        def _():
            dkv_ref[...] = jnp.zeros_like(dkv_ref)
            dg2_ref[...] = jnp.zeros_like(dg2_ref)

        xn, xh2, r2 = _rms_fwd(x1_ref[...], g2_ref[...])
        q = _dot(xn.astype(BF16), wq_ref[...])
        dx2v = dx2_ref[...]
        do = _dot_nt(dx2v.astype(BF16), wo_ref[...])
        for hd in range(HEADS):
            qb = q[:, hd * dh:(hd + 1) * dh].astype(BF16)
            p, kh, vh = _attn_probs(qb, kv_ref, hd, dh, d, scale)
            dob = do[:, hd * dh:(hd + 1) * dh].astype(BF16)
            dp = _dot_nt(dob, vh)
            ds = p * (dp - jnp.sum(dp * p, axis=-1, keepdims=True))
            dsb = (ds * scale).astype(BF16)
            dq_ref[:, hd * dh:(hd + 1) * dh] = _dot(dsb, kh).astype(BF16)
            dkv_ref[:, hd * dh:(hd + 1) * dh] += _dot_tn(dsb, qb)
            dkv_ref[:, d + hd * dh:d + (hd + 1) * dh] += _dot_tn(p.astype(BF16), dob)
        dxn = _dot_nt(dq_ref[...], wq_ref[...])
        dxr, dg2 = _rms_bwd(dxn, xh2, r2, g2_ref[...])
        dg2_ref[...] += dg2
        dx1_ref[...] = dx2v + dxr

    return pl.pallas_call(
        body, name="attn_bwd", grid=(n,),
        in_specs=[_rows(TM, d), _rows(TM, d), _full(g2.shape), _full(w_q.shape), _full(kv.shape),
                  _full(w_o.shape)],
        out_specs=[_rows(TM, d), _rows(TM, d), _acc((m, 2 * d)), _acc((1, d))],
        out_shape=(jax.ShapeDtypeStruct((s, d), F32), jax.ShapeDtypeStruct((s, d), BF16),
                   jax.ShapeDtypeStruct((m, 2 * d), F32), jax.ShapeDtypeStruct((1, d), F32)),
        compiler_params=_params("arbitrary"),
    )(x1, dx2, g2, w_q, kv, w_o)


def _kv_bwd(dkv, mem, memn, g_mem, w_kv):
    m, d = mem.shape
    ns = w_kv.shape[2]

    def body(dkv_ref, mem_ref, memn_ref, g_ref, w_ref, gw_ref, dg_ref):
        _, xh, _ = _rms_fwd(mem_ref[...], g_ref[...])
        dmemn = jnp.zeros((m, d), F32)
        for k in range(N_CHIPS):
            dkb = dkv_ref[:, k * ns:(k + 1) * ns].astype(BF16)
            gw_ref[k] = _dot_tn(memn_ref[...], dkb).astype(BF16)
            dmemn = dmemn + _dot_nt(dkb, w_ref[k])
        dg_ref[...] = jnp.sum(dmemn * xh, axis=0, keepdims=True)

    return pl.pallas_call(
        body, name="kv_bwd",
        out_shape=(jax.ShapeDtypeStruct((N_CHIPS, d, ns), BF16), jax.ShapeDtypeStruct((1, d), F32)),
        compiler_params=pltpu.CompilerParams(vmem_limit_bytes=VMEM_LIMIT),
    )(dkv, mem, memn, g_mem, w_kv)


def _mixer_bwd(x, dx1, h, g1, lng, lnb, wsp, b_t, cw, ga, gb, w_out, w_in):
    s, d = x.shape
    n = s // TM
    nch = TM // CHUNK
    ns = w_in.shape[2]
    nh = N_CHIPS * ns
    aw = d // 2
    hd_w = aw // HEADS

    def rev(cols):
        return pl.BlockSpec((TM, cols), lambda i: (n - 1 - i, 0))

    hprev = pl.BlockSpec((8, nh), lambda i: (jnp.maximum((n - 1 - i) * (TM // 8) - 1, 0), 0))

    def body(x_ref, dx1_ref, h_ref, hp_ref, g1_ref, lng_ref, lnb_ref, wsp_ref, bt_ref, cw_ref, ga_ref, gb_ref,
             wout_ref, win_ref,
             dx_ref, dh_ref, dg1_ref, dlng_ref, dlnb_ref, dwsp_ref, dbt_ref, dcw_ref, dga_ref, dgb_ref,
             mix_ref, dvn_ref, dcn_ref):
        i = pl.program_id(0)

        @pl.when(i == 0)
        def _():
            for r in (dg1_ref, dlng_ref, dlnb_ref, dwsp_ref, dbt_ref, dcw_ref, dga_ref, dgb_ref, dcn_ref):
                r[...] = jnp.zeros_like(r)

        dx1v = dx1_ref[...]
        dycat = _dot_nt(dx1v.astype(BF16), wout_ref[...])
        ha = h_ref[:, 0:2 * aw]
        a, th = _gelu_parts(ha)
        u = a[:, :aw]
        lngv = lng_ref[...]
        vn, vhat, rs = _layer_norm_parts(a[:, aw:], lngv, lnb_ref[...])
        vnb = vn.astype(BF16)
        wm = _tril_weights(wsp_ref)
        for c in range(nch):
            for hd in range(HEADS):
                blk = vnb[c * CHUNK:(c + 1) * CHUNK, hd * hd_w:(hd + 1) * hd_w]
                mix_ref[c * CHUNK:(c + 1) * CHUNK, hd * hd_w:(hd + 1) * hd_w] = _dot(wm[hd], blk) + bt_ref[:, hd:hd + 1]
        mixed = mix_ref[...]
        gav = ga_ref[...]
        _, yah, ra = _rms_fwd(u * mixed, gav)
        dya, dga = _rms_bwd(dycat[:, :aw], yah, ra, gav)
        dga_ref[...] += dga
        du = dya * mixed
        dmix = dya * u
        dmb = dmix.astype(BF16)
        tri = lax.broadcasted_iota(jnp.int32, (CHUNK, CHUNK), 0) >= lax.broadcasted_iota(jnp.int32, (CHUNK, CHUNK), 1)
        for hd in range(HEADS):
            dw = jnp.zeros((CHUNK, CHUNK), F32)
            db = jnp.zeros((CHUNK, 1), F32)
            for c in range(nch):
                rows = slice(c * CHUNK, (c + 1) * CHUNK)
                cols = slice(hd * hd_w, (hd + 1) * hd_w)
                dvn_ref[rows, cols] = _dot_tn(wm[hd], dmb[rows, cols])
                dw = dw + _dot_nt(dmb[rows, cols], vnb[rows, cols])
                db = db + jnp.sum(dmix[rows, cols], axis=1, keepdims=True)
            dwsp_ref[hd] += jnp.where(tri, dw, 0.0)
            dbt_ref[:, hd:hd + 1] += db
        dvn = dvn_ref[...]
        dlng_ref[...] += jnp.sum(dvn * vhat, axis=0, keepdims=True)
        dlnb_ref[...] += jnp.sum(dvn, axis=0, keepdims=True)
        dvh = dvn * lngv
        dv = rs * (dvh - jnp.mean(dvh, axis=-1, keepdims=True) - vhat * jnp.mean(dvh * vhat, axis=-1, keepdims=True))
        gprime = 0.5 * (1.0 + th) + 0.5 * ha * (1.0 - th * th) * (GELU_C * (1.0 + 3.0 * GELU_K * (ha * ha)))
        dh_ref[:, 0:2 * aw] = (jnp.concatenate([du, dv], axis=-1) * gprime).astype(BF16)
        g_b = h_ref[:, 2 * aw:3 * aw]
        g_c = h_ref[:, 3 * aw:4 * aw]
        val = h_ref[:, 4 * aw:5 * aw]
        z = g_c * val
        zp = jnp.where(i == n - 1, 0.0, hp_ref[:, 3 * aw:4 * aw] * hp_ref[:, 4 * aw:5 * aw])
        zm1, zm2 = _shift_rows(z, zp)
        cw0, cw1, cw2 = cw_ref[0:1, :], cw_ref[1:2, :], cw_ref[2:3, :]
        conv = cw0 * zm2 + cw1 * zm1 + cw2 * z
        gbv = gb_ref[...]
        _, ybh, rb = _rms_fwd(g_b * conv, gbv)
        dyb, dgb = _rms_bwd(dycat[:, aw:], ybh, rb, gbv)
        dgb_ref[...] += dgb
        dconv = dyb * g_b
        dcw_ref[0:1, :] += jnp.sum(dconv * zm2, axis=0, keepdims=True)
        dcw_ref[1:2, :] += jnp.sum(dconv * zm1, axis=0, keepdims=True)
        dcw_ref[2:3, :] += jnp.sum(dconv * z, axis=0, keepdims=True)
        nxt = dcn_ref[...]
        row = lax.broadcasted_iota(jnp.int32, dconv.shape, 0)
        dcp1 = jnp.where(row == TM - 1, nxt[0:1, :], pltpu.roll(dconv, TM - 1, 0))
        dcp2 = jnp.where(row == TM - 1, nxt[1:2, :],
                         jnp.where(row == TM - 2, nxt[0:1, :], pltpu.roll(dconv, TM - 2, 0)))
        dz = cw2 * dconv + cw1 * dcp1 + cw0 * dcp2
        dcn_ref[...] = dconv[0:8, :]
        dh_ref[:, 2 * aw:3 * aw] = (dyb * conv).astype(BF16)
        dh_ref[:, 3 * aw:4 * aw] = (dz * val).astype(BF16)
        dh_ref[:, 4 * aw:5 * aw] = (dz * g_c).astype(BF16)
        dxn = jnp.zeros((TM, d), F32)
        for k in range(N_CHIPS):
            dxn = dxn + _dot_nt(dh_ref[:, k * ns:(k + 1) * ns], win_ref[k])
        g1v = g1_ref[...]
        _, xh1, r1 = _rms_fwd(x_ref[...], g1v)
        dxr, dg1 = _rms_bwd(dxn, xh1, r1, g1v)
        dg1_ref[...] += dg1
        dx_ref[...] = dx1v + dxr

    ins = (x, dx1, h, h, g1, lng, lnb, wsp, b_t, cw, ga, gb, w_out, w_in)
    acc_shapes = [(1, d), (1, aw), (1, aw), wsp.shape, (CHUNK, CHUNK), cw.shape, (1, aw), (1, aw)]
    return pl.pallas_call(
        body, name="mixer_bwd", grid=(n,),
        in_specs=[rev(d), rev(d), rev(nh), hprev] + [_full(a.shape) for a in ins[4:]],
        out_specs=[rev(d), rev(nh)] + [_acc(sh) for sh in acc_shapes],
        out_shape=(jax.ShapeDtypeStruct((s, d), F32), jax.ShapeDtypeStruct((s, nh), BF16))
        + tuple(jax.ShapeDtypeStruct(sh, F32) for sh in acc_shapes),
        scratch_shapes=[pltpu.VMEM((TM, aw), F32), pltpu.VMEM((TM, aw), F32), pltpu.VMEM((8, aw), F32)],
        compiler_params=_params("arbitrary"),
    )(*ins)


def _weight_grad(a, b, name, tm, tn, col_sharded):
    t, m = a.shape
    n = b.shape[1]

    def body(a_ref, b_ref, o_ref):
        o_ref[...] = _dot_tn(a_ref[...].astype(BF16), b_ref[...].astype(BF16)).astype(BF16)

    if col_sharded:
        ns = n // N_CHIPS
        per = ns // tn
        out_shape = jax.ShapeDtypeStruct((N_CHIPS, m, ns), BF16)
        out_spec = pl.BlockSpec((None, tm, tn), lambda i, j: (j // per, i, j % per))
    else:
        out_shape = jax.ShapeDtypeStruct((m, n), BF16)
        out_spec = pl.BlockSpec((tm, tn), lambda i, j: (i, j))
    out = pl.pallas_call(
        body, name=name, grid=(m // tm, n // tn),
        in_specs=[pl.BlockSpec((t, tm), lambda i, j: (0, i)), pl.BlockSpec((t, tn), lambda i, j: (0, j))],
        out_specs=out_spec, out_shape=out_shape,
        compiler_params=_params("parallel", "parallel"),
    )(a, b)
    return out if col_sharded else out.reshape(N_CHIPS, m // N_CHIPS, n)


def _row_tile(rows, cap=256):
    best = None
    for t in range(16, min(rows, cap) + 1, 16):
        if rows % t == 0:
            best = t
    return best if best is not None else rows


def _chip_sum(grad, got, place, name):
    nb, rh, c = got.shape
    tr = _row_tile(rh)
    per = rh // tr

    def body(place_ref, a_ref, b_ref, o_ref):
        o_ref[...] = (a_ref[...].astype(F32) + b_ref[...].astype(F32)).astype(BF16)

    blk = pl.BlockSpec((None, tr, c), lambda k, i, pref: (k, i, 0))
    return pl.pallas_call(
        body, name=name,
        grid_spec=pltpu.PrefetchScalarGridSpec(
            num_scalar_prefetch=1, grid=(nb, per),
            in_specs=[pl.BlockSpec((None, tr, c), lambda k, i, pref: (k, pref[0] * per + i, 0)), blk],
            out_specs=blk),
        out_shape=jax.ShapeDtypeStruct(got.shape, BF16),
        compiler_params=_params("parallel", "parallel"),
    )(place, grad, got)


def _final_sum(part, slots, place, name):
    _, rh, c = slots.shape
    tr = _row_tile(rh)
    per = rh // tr

    def body(place_ref, own_ref, s0_ref, s1_ref, s2_ref, o_ref):
        acc = own_ref[...].astype(F32) + s0_ref[...].astype(F32)
        o_ref[...] = (acc + s1_ref[...].astype(F32)) + s2_ref[...].astype(F32)

    def slot(r):
        return pl.BlockSpec((None, tr, c), lambda i, pref: (r, i, 0))

    return pl.pallas_call(
        body, name=name,
        grid_spec=pltpu.PrefetchScalarGridSpec(
            num_scalar_prefetch=1, grid=(per,),
            in_specs=[pl.BlockSpec((None, tr, c), lambda i, pref: (pref[1], i, 0)), slot(0), slot(1), slot(2)],
            out_specs=pl.BlockSpec((tr, c), lambda i, pref: (pref[0] * per + i, 0))),
        out_shape=jax.ShapeDtypeStruct((2 * rh, c), F32),
        compiler_params=_params("parallel"),
    )(place, part, slots, slots, slots)


def _adamw_math(w, g, m, v):
    m2 = ADAM_B1 * m + (1.0 - ADAM_B1) * g
    v2 = ADAM_B2 * v + (1.0 - ADAM_B2) * (g * g)
    m_hat = m2 / (1.0 - ADAM_B1 ** ADAM_STEP)
    v_hat = v2 / (1.0 - ADAM_B2 ** ADAM_STEP)
    delta = -ADAM_LR * (m_hat / (jnp.sqrt(v_hat) + ADAM_EPS) + ADAM_WD * w)
    return delta, m2, v2


def _adamw(w, g, m, v, name):
    r, c = w.shape
    tr = _row_tile(r) if r >= 16 else r

    def body(w_ref, g_ref, m_ref, v_ref, d_ref, m2_ref, v2_ref):
        d_ref[...], m2_ref[...], v2_ref[...] = _adamw_math(w_ref[...], g_ref[...], m_ref[...], v_ref[...])

    sh = jax.ShapeDtypeStruct((r, c), F32)
    return pl.pallas_call(
        body, name=name, grid=(r // tr,),
        in_specs=[_rows(tr, c)] * 4, out_specs=[_rows(tr, c)] * 3, out_shape=(sh, sh, sh),
        compiler_params=_params("parallel"),
    )(w, g, m, v)


def _small_sum_adamw(parts, w, m, v):
    nd, r, c = parts.shape

    def body(p_ref, w_ref, m_ref, v_ref, g_ref, d_ref, m2_ref, v2_ref):
        g = p_ref[0]
        for k in range(1, nd):
            g = g + p_ref[k]
        g_ref[...] = g
        d_ref[...], m2_ref[...], v2_ref[...] = _adamw_math(w_ref[...], g, m_ref[...], v_ref[...])

    sh = jax.ShapeDtypeStruct((r, c), F32)
    return pl.pallas_call(
        body, name="small_sum_adamw", out_shape=(sh, sh, sh, sh),
        compiler_params=pltpu.CompilerParams(vmem_limit_bytes=VMEM_LIMIT),
    )(parts, w, m, v)


def _place():
    x, y, c = lax.axis_index("x"), lax.axis_index("y"), lax.axis_index("c")
    chips = [(1 - x, y), (x, 1 - y), (1 - x, 1 - y)]
    return x, y, c, 2 * x + y, chips


def _remote(src, dst, send_sem, recv_sem, to):
    return pltpu.make_async_remote_copy(src_ref=src, dst_ref=dst, send_sem=send_sem, recv_sem=recv_sem,
                                        device_id=to, device_id_type=MESH)


def _all_gather_weights(shards, conv8):
    nw = len(shards)

    def body(*refs):
        ins, cin = refs[:nw], refs[nw]
        outs, cout = refs[nw + 1:2 * nw + 1], refs[2 * nw + 1]
        send_sems, recv_sems, csend, crecv = refs[2 * nw + 2:]
        x, y, c, me, chips = _place()
        sib = (x, y, 1 - c)
        local = [_remote(ins[w], outs[w].at[me], send_sems.at[w, 6], recv_sems.at[w, 6], sib) for w in range(nw)]
        local.append(_remote(cin, cout.at[me], csend.at[3], crecv.at[3], sib))
        for cp in local:
            cp.start()

        def half(w):
            rh = shards[w].shape[0] // 2
            return pl.ds(c * rh, rh)

        sends = []
        for r, (px, py) in enumerate(chips):
            for w in range(nw):
                sends.append(_remote(ins[w].at[half(w)], outs[w].at[me, half(w)],
                                     send_sems.at[w, r], recv_sems.at[w, r], (px, py, c)))
            sends.append(_remote(cin, cout.at[me], csend.at[r], crecv.at[r], (px, py, c)))
        for cp in sends:
            cp.start()
        passed = []
        for r, (px, py) in enumerate(chips):
            pk = 2 * px + py
            for w in range(nw):
                blk = outs[w].at[pk, half(w)]
                _remote(blk, blk, send_sems.at[w, r], recv_sems.at[w, r], (px, py, c)).wait_recv()
                fwd = _remote(blk, blk, send_sems.at[w, 3 + r], recv_sems.at[w, 3 + r], sib)
                fwd.start()
                passed.append(fwd)
            _remote(cin, cout.at[pk], csend.at[r], crecv.at[r], (px, py, c)).wait_recv()
        for r, (px, py) in enumerate(chips):
            pk = 2 * px + py
            for w in range(nw):
                rh = shards[w].shape[0] // 2
                blk = outs[w].at[pk, pl.ds((1 - c) * rh, rh)]
                _remote(blk, blk, send_sems.at[w, 3 + r], recv_sems.at[w, 3 + r], sib).wait_recv()
        for cp in sends + passed:
            cp.wait_send()
        for cp in local:
            cp.wait()

    out_shape = tuple(jax.ShapeDtypeStruct((N_CHIPS,) + s.shape, s.dtype) for s in shards) + (
        jax.ShapeDtypeStruct((N_CHIPS,) + conv8.shape, conv8.dtype),)
    return pl.pallas_call(
        body, name="all_gather_weights", out_shape=out_shape,
        in_specs=[ANY] * (nw + 1), out_specs=[ANY] * (nw + 1),
        scratch_shapes=[pltpu.SemaphoreType.DMA((nw, 7)), pltpu.SemaphoreType.DMA((nw, 7)),
                        pltpu.SemaphoreType.DMA((4,)), pltpu.SemaphoreType.DMA((4,))],
    )(*shards, conv8)


def _exchange_halves(grads):
    nw = len(grads)

    def body(*refs):
        ins, got = refs[:nw], refs[nw:2 * nw]
        send_sems, recv_sems = refs[2 * nw:]
        x, y, c, _, _ = _place()
        rcs = []
        for w in range(nw):
            rh = grads[w].shape[1] // 2
            rc = _remote(ins[w].at[:, pl.ds((1 - c) * rh, rh), :], got[w], send_sems.at[w], recv_sems.at[w],
                         (x, y, 1 - c))
            rc.start()
            rcs.append(rc)
        for rc in rcs:
            rc.wait()

    halves = tuple(jax.ShapeDtypeStruct((g.shape[0], g.shape[1] // 2, g.shape[2]), g.dtype) for g in grads)
    return pl.pallas_call(
        body, name="rs_exchange_halves", out_shape=halves,
        in_specs=[ANY] * nw, out_specs=[ANY] * nw,
        scratch_shapes=[pltpu.SemaphoreType.DMA((nw,)), pltpu.SemaphoreType.DMA((nw,))],
    )(*grads)


def _scatter_partials(parts):
    nw = len(parts)

    def body(*refs):
        ins, outs = refs[:nw], refs[nw:2 * nw]
        send_sems, recv_sems = refs[2 * nw:]
        _, _, c, _, chips = _place()
        rcs = []
        for r, (px, py) in enumerate(chips):
            for w in range(nw):
                rc = _remote(ins[w].at[2 * px + py], outs[w].at[r], send_sems.at[w, r], recv_sems.at[w, r],
                             (px, py, c))
                rc.start()
                rcs.append(rc)
        for rc in rcs:
            rc.wait()

    return pl.pallas_call(
        body, name="rs_scatter_partials",
        out_shape=tuple(jax.ShapeDtypeStruct((3,) + p.shape[1:], p.dtype) for p in parts),
        in_specs=[ANY] * nw, out_specs=[ANY] * nw,
        scratch_shapes=[pltpu.SemaphoreType.DMA((nw, 3)), pltpu.SemaphoreType.DMA((nw, 3))],
    )(*parts)


def _join_halves(shards):
    nw = len(shards)

    def body(*refs):
        bufs = refs[nw:2 * nw]
        send_sems, recv_sems = refs[2 * nw:]
        x, y, c, _, _ = _place()
        rcs = []
        for w in range(nw):
            rh = shards[w].shape[0] // 2
            mine = bufs[w].at[pl.ds(c * rh, rh)]
            rc = _remote(mine, mine, send_sems.at[w], recv_sems.at[w], (x, y, 1 - c))
            rc.start()
            rcs.append(rc)
        for rc in rcs:
            rc.wait_send()
        for w in range(nw):
            rh = shards[w].shape[0] // 2
            theirs = bufs[w].at[pl.ds((1 - c) * rh, rh)]
            _remote(theirs, theirs, send_sems.at[w], recv_sems.at[w], (x, y, 1 - c)).wait_recv()

    return pl.pallas_call(
        body, name="rs_join_halves",
        out_shape=tuple(jax.ShapeDtypeStruct(s.shape, s.dtype) for s in shards),
        in_specs=[ANY] * nw, out_specs=[ANY] * nw,
        input_output_aliases={w: w for w in range(nw)},
        scratch_shapes=[pltpu.SemaphoreType.DMA((nw,)), pltpu.SemaphoreType.DMA((nw,))],
    )(*shards)


def _gather_small(slab):
    def body(in_ref, out_ref, send_sems, recv_sems, loc_sem):
        x, y, c, _, _ = _place()
        me = 4 * x + 2 * y + c
        lc = pltpu.make_async_copy(in_ref, out_ref.at[me], loc_sem)
        lc.start()
        rcs = []
        for k in range(1, 8):
            px = 1 - x if k & 4 else x
            py = 1 - y if k & 2 else y
            pc = 1 - c if k & 1 else c
            rc = _remote(in_ref, out_ref.at[me], send_sems.at[k - 1], recv_sems.at[k - 1], (px, py, pc))
            rc.start()
            rcs.append((rc, 4 * px + 2 * py + pc))
        for rc, _ in rcs:
            rc.wait_send()
        for k, (_, peer) in enumerate(rcs):
            _remote(in_ref, out_ref.at[peer], send_sems.at[k], recv_sems.at[k], (x, y, c)).wait_recv()
        lc.wait()

    return pl.pallas_call(
        body, name="gather_small", out_shape=jax.ShapeDtypeStruct((8,) + slab.shape, slab.dtype),
        in_specs=[ANY], out_specs=ANY,
        scratch_shapes=[pltpu.SemaphoreType.DMA((7,)), pltpu.SemaphoreType.DMA((7,)), pltpu.SemaphoreType.DMA],
    )(slab)


_SMALL_VECS = ("ln_mix_g", "ln_attn_g", "ln_mem_g", "ln_ffn_g", "ln_final_g")


def _pack_small(p, extra, conv):
    d = p["ln_mix_g"].shape[-1]
    top = [p[k].reshape(1, d) for k in _SMALL_VECS]
    top.append(jnp.concatenate([p["sgu_ln_g"].reshape(-1), p["sgu_ln_b"].reshape(-1)]).reshape(1, d))
    top.append(jnp.concatenate([p["grp_norm_a"].reshape(-1), p["grp_norm_b"].reshape(-1)]).reshape(1, d))
    top.append(jnp.concatenate([p["b_spatial"].reshape(-1), extra]).reshape(1, d))
    mid = jnp.zeros((8, d), F32)
    if conv is not None:
        mid = jnp.pad(conv, ((0, 5), (0, d - conv.shape[1])))
    return jnp.concatenate([jnp.concatenate(top, axis=0), mid, p["w_spatial"].reshape(-1, d)], axis=0)


def _unpack_small(slab):
    d = slab.shape[1]
    hw = d // 2
    out = {k: slab[i] for i, k in enumerate(_SMALL_VECS)}
    out["sgu_ln_g"], out["sgu_ln_b"] = slab[5, :hw], slab[5, hw:]
    out["grp_norm_a"], out["grp_norm_b"] = slab[6, :hw], slab[6, hw:]
    out["b_spatial"] = slab[7, :hw].reshape(HEADS, CHUNK)
    out["w_spatial"] = slab[16:].reshape(HEADS, CHUNK, CHUNK)
    return out


_BIG = ("w_in", "w_kv", "w_gate_up", "w_out", "w_q", "w_o", "w_down")
_WEIGHTS = ("ln_mix_g", "w_in", "sgu_ln_g", "sgu_ln_b", "w_spatial", "b_spatial", "conv_w", "grp_norm_a",
            "grp_norm_b", "w_out", "ln_attn_g", "ln_mem_g", "w_q", "w_kv", "w_o", "ln_ffn_g", "w_gate_up",
            "w_down", "ln_final_g")


def _step(p, m_, v_, x, mem, target):
    s, d = x.shape
    hw = d // 2
    row = lambda a: a.reshape(1, -1)
    x_, y_, c_ = lax.axis_index("x"), lax.axis_index("y"), lax.axis_index("c")
    chip = 2 * x_ + y_

    conv8 = jnp.pad(p["conv_w"], ((0, 5), (0, 0)))
    full = _all_gather_weights([p[k].astype(BF16) for k in _BIG], conv8)
    w_in, w_kv, w_gu, w_out4, w_q4, w_o4, w_down4, conv4 = full
    w_out, w_q, w_o = (a.reshape(-1, d) for a in (w_out4, w_q4, w_o4))
    w_down = w_down4.reshape(-1, d)
    cw = jnp.transpose(conv4[:, :3, :], (1, 0, 2)).reshape(3, hw)
    b_t = jnp.pad(jnp.transpose(p["b_spatial"]), ((0, 0), (0, CHUNK - HEADS)))
    g1, g2, gm, g3, gf = (row(p[k]) for k in _SMALL_VECS)
    lng, lnb, ga, gb = row(p["sgu_ln_g"]), row(p["sgu_ln_b"]), row(p["grp_norm_a"]), row(p["grp_norm_b"])
    wsp = p["w_spatial"]

    memn, kv = _kv_fwd(mem, gm, w_kv)
    h, x1, ycat, xn1 = _mixer_fwd(x, g1, w_in, lng, lnb, wsp, b_t, cw, ga, gb, w_out)
    x2, o, xn2 = _attn_fwd(x1, g2, w_q, kv, w_o)
    dx2, act, dgu, xn3, dx3, loss, dgf, dg3 = _ffn_fwd_bwd(x2, g3, gf, target, w_gu, w_down)
    dx1, dq, dkv, dg2 = _attn_bwd(x1, dx2, g2, w_q, kv, w_o)
    g_kv, dgm = _kv_bwd(dkv, mem, memn, gm, w_kv)
    dx, dh, dg1, dlng, dlnb, dwsp, dbt, dcw, dga, dgb = _mixer_bwd(
        x, dx1, h, g1, lng, lnb, wsp, b_t, cw, ga, gb, w_out, w_in)
    grads = [
        _weight_grad(xn1, dh, "grad_w_in", 512, 640, True),
        g_kv,
        _weight_grad(xn3, dgu, "grad_w_gate_up", 512, 1408, True),
        _weight_grad(ycat, dx1, "grad_w_out", 512, 512, False),
        _weight_grad(xn2, dq, "grad_w_q", 512, 512, False),
        _weight_grad(o, dx2, "grad_w_o", 512, 512, False),
        _weight_grad(act, dx3, "grad_w_down", 1408, 512, False),
    ]

    place = jnp.stack([c_, chip]).astype(jnp.int32)
    got = _exchange_halves(grads)
    chip_parts = [_chip_sum(g, b, place, "chip_sum_" + k) for k, g, b in zip(_BIG, grads, got)]
    slots = _scatter_partials(chip_parts)
    halves = [_final_sum(a, sl, place, "final_sum_" + k) for k, a, sl in zip(_BIG, chip_parts, slots)]
    big_g = dict(zip(_BIG, _join_halves(halves)))

    small = {"ln_mix_g": dg1, "ln_attn_g": dg2, "ln_mem_g": dgm, "ln_ffn_g": dg3, "ln_final_g": dgf,
             "sgu_ln_g": dlng, "sgu_ln_b": dlnb, "grp_norm_a": dga, "grp_norm_b": dgb,
             "b_spatial": jnp.transpose(dbt[:, :HEADS]), "w_spatial": dwsp}
    loss_vec = jnp.pad(loss.reshape(1), (0, hw - 1))
    parts = _gather_small(_pack_small(small, loss_vec, dcw))
    zeros = jnp.zeros((hw,), F32)
    sg, sd, sm, sv = _small_sum_adamw(parts, _pack_small(p, zeros, None), _pack_small(m_, zeros, None),
                                      _pack_small(v_, zeros, None))
    out_g, out_d, out_m, out_v = (_unpack_small(a) for a in (sg, sd, sm, sv))
    loss_out = sg[7, hw]
    g_conv = lax.dynamic_slice(sg[8:11, :hw], (0, chip * (hw // N_CHIPS)), (3, hw // N_CHIPS))
    out_g["conv_w"] = g_conv
    out_d["conv_w"], out_m["conv_w"], out_v["conv_w"] = _adamw(p["conv_w"], g_conv, m_["conv_w"], v_["conv_w"],
                                                                "adamw_conv_w")
    for k in _BIG:
        out_g[k] = big_g[k]
        out_d[k], out_m[k], out_v[k] = _adamw(p[k], big_g[k], m_[k], v_[k], "adamw_" + k)
    return loss_out, dx, out_g, out_d, out_m, out_v


def kernel(x, mem, ln_mix_g, w_in, sgu_ln_g, sgu_ln_b, w_spatial, b_spatial, conv_w, grp_norm_a, grp_norm_b, w_out, ln_attn_g, ln_mem_g, w_q, w_kv, w_o, ln_ffn_g, w_gate_up, w_down, ln_final_g, loss_target, m_ln_mix_g, m_w_in, m_sgu_ln_g, m_sgu_ln_b, m_w_spatial, m_b_spatial, m_conv_w, m_grp_norm_a, m_grp_norm_b, m_w_out, m_ln_attn_g, m_ln_mem_g, m_w_q, m_w_kv, m_w_o, m_ln_ffn_g, m_w_gate_up, m_w_down, m_ln_final_g, v_ln_mix_g, v_w_in, v_sgu_ln_g, v_sgu_ln_b, v_w_spatial, v_b_spatial, v_conv_w, v_grp_norm_a, v_grp_norm_b, v_w_out, v_ln_attn_g, v_ln_mem_g, v_w_q, v_w_kv, v_w_o, v_ln_ffn_g, v_w_gate_up, v_w_down, v_ln_final_g):
    p = dict(ln_mix_g=ln_mix_g, w_in=w_in, sgu_ln_g=sgu_ln_g, sgu_ln_b=sgu_ln_b, w_spatial=w_spatial,
             b_spatial=b_spatial, conv_w=conv_w, grp_norm_a=grp_norm_a, grp_norm_b=grp_norm_b, w_out=w_out,
             ln_attn_g=ln_attn_g, ln_mem_g=ln_mem_g, w_q=w_q, w_kv=w_kv, w_o=w_o, ln_ffn_g=ln_ffn_g,
             w_gate_up=w_gate_up, w_down=w_down, ln_final_g=ln_final_g)
    m_ = dict(ln_mix_g=m_ln_mix_g, w_in=m_w_in, sgu_ln_g=m_sgu_ln_g, sgu_ln_b=m_sgu_ln_b, w_spatial=m_w_spatial,
              b_spatial=m_b_spatial, conv_w=m_conv_w, grp_norm_a=m_grp_norm_a, grp_norm_b=m_grp_norm_b,
              w_out=m_w_out, ln_attn_g=m_ln_attn_g, ln_mem_g=m_ln_mem_g, w_q=m_w_q, w_kv=m_w_kv, w_o=m_w_o,
              ln_ffn_g=m_ln_ffn_g, w_gate_up=m_w_gate_up, w_down=m_w_down, ln_final_g=m_ln_final_g)
    v_ = dict(ln_mix_g=v_ln_mix_g, w_in=v_w_in, sgu_ln_g=v_sgu_ln_g, sgu_ln_b=v_sgu_ln_b, w_spatial=v_w_spatial,
              b_spatial=v_b_spatial, conv_w=v_conv_w, grp_norm_a=v_grp_norm_a, grp_norm_b=v_grp_norm_b,
              w_out=v_w_out, ln_attn_g=v_ln_attn_g, ln_mem_g=v_ln_mem_g, w_q=v_w_q, w_kv=v_w_kv, w_o=v_w_o,
              ln_ffn_g=v_ln_ffn_g, w_gate_up=v_w_gate_up, w_down=v_w_down, ln_final_g=v_ln_final_g)
    s, d = x.shape[-2], x.shape[-1]
    loss, dx, g, dl, nm, nv = _step(p, m_, v_, x.reshape(s, d), mem.reshape(-1, d), loss_target.reshape(s, d))
    outs = [loss, dx.reshape(x.shape)]
    for tree in (g, dl, nm, nv):
        outs += [tree[k].reshape(p[k].shape) for k in _WEIGHTS]
    return tuple(outs)
```

```python
import functools
import math

import jax
import jax.numpy as jnp
from jax import lax
from jax.experimental import pallas as pl
from jax.experimental.pallas import tpu as pltpu

F32 = jnp.float32
BF16 = jnp.bfloat16
EPS = 1e-6
CHUNK = 128
HEADS = 4
N_CHIPS = 4
TM = 256
ADAM_LR, ADAM_B1, ADAM_B2, ADAM_EPS, ADAM_WD, ADAM_STEP = 0.001, 0.9, 0.999, 1e-08, 0.01, 10
GELU_C = math.sqrt(2.0 / math.pi)
GELU_K = 0.044715
SMALL_ROWS = 80
VMEM_LIMIT = 56 * 1024 * 1024
MESH = pl.DeviceIdType.MESH
ANY = pl.BlockSpec(memory_space=pl.ANY)


def _params(*sem):
    return pltpu.CompilerParams(dimension_semantics=sem, vmem_limit_bytes=VMEM_LIMIT)


def _dot(a, b):
    return jnp.dot(a, b, preferred_element_type=F32)


def _dot_nt(a, b):
    return lax.dot_general(a, b, (((1,), (1,)), ((), ())), preferred_element_type=F32)


def _dot_tn(a, b):
    return lax.dot_general(a, b, (((0,), (0,)), ((), ())), preferred_element_type=F32)


def _rms_fwd(x, g):
    r = lax.rsqrt(jnp.mean(x * x, axis=-1, keepdims=True) + EPS)
    xh = x * r
    return xh * g, xh, r


def _rms_bwd(dy, xh, r, g):
    dxh = dy * g
    dx = r * (dxh - xh * jnp.mean(dxh * xh, axis=-1, keepdims=True))
    return dx, jnp.sum(dy * xh, axis=0, keepdims=True)


def _full(shape):
    nd = len(shape)
    return pl.BlockSpec(shape, lambda *_: (0,) * nd, pipeline_mode=pl.Buffered(1))


def _acc(shape):
    nd = len(shape)
    return pl.BlockSpec(shape, lambda *_: (0,) * nd)


def _rows(tm, cols):
    return pl.BlockSpec((tm, cols), lambda i: (i, 0))


def _tril_weights(wsp_ref):
    row = lax.broadcasted_iota(jnp.int32, (CHUNK, CHUNK), 0)
    col = lax.broadcasted_iota(jnp.int32, (CHUNK, CHUNK), 1)
    return [jnp.where(row >= col, wsp_ref[hd], 0.0).astype(BF16) for hd in range(HEADS)]


def _shift_rows(z, zp):
    row = lax.broadcasted_iota(jnp.int32, z.shape, 0)
    zm1 = jnp.where(row == 0, zp[7:8, :], pltpu.roll(z, 1, 0))
    zm2 = jnp.where(row == 0, zp[6:7, :], jnp.where(row == 1, zp[7:8, :], pltpu.roll(z, 2, 0)))
    return zm1, zm2


def _gelu_parts(x):
    t = jnp.tanh(GELU_C * (x + GELU_K * (x * x * x)))
    return 0.5 * x * (1.0 + t), t


def _layer_norm_parts(v, g, b):
    mu = jnp.mean(v, axis=-1, keepdims=True)
    vc = v - mu
    rs = lax.rsqrt(jnp.mean(vc * vc, axis=-1, keepdims=True) + EPS)
    vhat = vc * rs
    return vhat * g + b, vhat, rs


def _kv_fwd(mem, g_mem, w_kv):
    m, d = mem.shape
    ns = w_kv.shape[2]

    def body(mem_ref, g_ref, w_ref, memn_ref, kv_ref):
        y, _, _ = _rms_fwd(mem_ref[...], g_ref[...])
        yb = y.astype(BF16)
        memn_ref[...] = yb
        for k in range(N_CHIPS):
            kv_ref[:, k * ns:(k + 1) * ns] = _dot(yb, w_ref[k]).astype(BF16)

    return pl.pallas_call(
        body, name="kv_fwd",
        out_shape=(jax.ShapeDtypeStruct((m, d), BF16), jax.ShapeDtypeStruct((m, N_CHIPS * ns), BF16)),
        compiler_params=pltpu.CompilerParams(vmem_limit_bytes=VMEM_LIMIT),
    )(mem, g_mem, w_kv)


def _mixer_fwd(x, g1, w_in, lng, lnb, wsp, b_t, cw, ga, gb, w_out, hosted=None):
    s, d = x.shape
    n = s // TM
    nch = TM // CHUNK
    ns = w_in.shape[2]
    nh = N_CHIPS * ns
    aw = d // 2
    hd_w = aw // HEADS

    def body(x_ref, g1_ref, win_ref, lng_ref, lnb_ref, wsp_ref, bt_ref, cw_ref, ga_ref, gb_ref, wout_ref,
             h_ref, x1_ref, y_ref, xn_ref, mix_ref, zp_ref):
        i = pl.program_id(0)

        @pl.when(i == 0)
        def _():
            zp_ref[...] = jnp.zeros_like(zp_ref)

        x = x_ref[...]
        xn, _, _ = _rms_fwd(x, g1_ref[...])
        xnb = xn.astype(BF16)
        xn_ref[...] = xnb
        for k in range(N_CHIPS):
            h_ref[:, k * ns:(k + 1) * ns] = _dot(xnb, win_ref[k])
        a, _ = _gelu_parts(h_ref[:, 0:2 * aw])
        u = a[:, :aw]
        vn, _, _ = _layer_norm_parts(a[:, aw:], lng_ref[...], lnb_ref[...])
        vnb = vn.astype(BF16)
        wm = _tril_weights(wsp_ref)
        for c in range(nch):
            for hd in range(HEADS):
                blk = vnb[c * CHUNK:(c + 1) * CHUNK, hd * hd_w:(hd + 1) * hd_w]
                mix_ref[c * CHUNK:(c + 1) * CHUNK, hd * hd_w:(hd + 1) * hd_w] = _dot(wm[hd], blk) + bt_ref[:, hd:hd + 1]
        ya, _, _ = _rms_fwd(u * mix_ref[...], ga_ref[...])
        g_b = h_ref[:, 2 * aw:3 * aw]
        z = h_ref[:, 3 * aw:4 * aw] * h_ref[:, 4 * aw:5 * aw]
        zm1, zm2 = _shift_rows(z, zp_ref[...])
        conv = cw_ref[0:1, :] * zm2 + cw_ref[1:2, :] * zm1 + cw_ref[2:3, :] * z
        yb, _, _ = _rms_fwd(g_b * conv, gb_ref[...])
        zp_ref[...] = z[TM - 8:TM, :]
        ycat = jnp.concatenate([ya, yb], axis=-1).astype(BF16)
        y_ref[...] = ycat
        x1_ref[...] = x + _dot(ycat, wout_ref[...])

    return _host_call(
        body, "mixer_fwd", (n,),
        [_rows(TM, d), _full(g1.shape), _full(w_in.shape), _full(lng.shape), _full(lnb.shape),
         _full(wsp.shape), _full(b_t.shape), _full(cw.shape), _full(ga.shape), _full(gb.shape),
         _full(w_out.shape)],
        [_rows(TM, nh), _rows(TM, d), _rows(TM, d), _rows(TM, d)],
        (jax.ShapeDtypeStruct((s, nh), F32), jax.ShapeDtypeStruct((s, d), F32),
         jax.ShapeDtypeStruct((s, d), BF16), jax.ShapeDtypeStruct((s, d), BF16)),
        [pltpu.VMEM((TM, aw), F32), pltpu.VMEM((8, aw), F32)],
        (x, g1, w_in, lng, lnb, wsp, b_t, cw, ga, gb, w_out), ("arbitrary",), hosted)


def _attn_probs(qb, kv_ref, hd, dh, d, scale):
    kh = kv_ref[:, hd * dh:(hd + 1) * dh]
    vh = kv_ref[:, d + hd * dh:d + (hd + 1) * dh]
    sc = _dot_nt(qb, kh) * scale
    sc = sc - jnp.max(sc, axis=-1, keepdims=True)
    e = jnp.exp(sc)
    return e / jnp.sum(e, axis=-1, keepdims=True), kh, vh


def _attn_fwd(x1, g2, w_q, kv, w_o, hosted=None):
    s, d = x1.shape
    n = s // TM
    dh = d // HEADS
    scale = dh ** -0.5

    def body(x1_ref, g2_ref, wq_ref, kv_ref, wo_ref, x2_ref, o_ref, xn_ref):
        x1v = x1_ref[...]
        xn, _, _ = _rms_fwd(x1v, g2_ref[...])
        xnb = xn.astype(BF16)
        xn_ref[...] = xnb
        q = _dot(xnb, wq_ref[...])
        for hd in range(HEADS):
            qb = q[:, hd * dh:(hd + 1) * dh].astype(BF16)
            p, _, vh = _attn_probs(qb, kv_ref, hd, dh, d, scale)
            o_ref[:, hd * dh:(hd + 1) * dh] = _dot(p.astype(BF16), vh).astype(BF16)
        x2_ref[...] = x1v + _dot(o_ref[...], wo_ref[...])

    return _host_call(
        body, "attn_fwd", (n,),
        [_rows(TM, d), _full(g2.shape), _full(w_q.shape), _full(kv.shape), _full(w_o.shape)],
        [_rows(TM, d), _rows(TM, d), _rows(TM, d)],
        (jax.ShapeDtypeStruct((s, d), F32), jax.ShapeDtypeStruct((s, d), BF16), jax.ShapeDtypeStruct((s, d), BF16)),
        [], (x1, g2, w_q, kv, w_o), ("parallel",), hosted)


def _ffn_fwd_bwd(x2, g3, gf, target, w_gu, w_down):
    s, d = x2.shape
    n = s // TM
    ns = w_gu.shape[2]
    ff = 2 * ns

    def body(x2_ref, g3_ref, gf_ref, t_ref, wgu_ref, wd_ref,
             dx2_ref, act_ref, dgu_ref, xn_ref, dx3_ref, loss_ref, dgf_ref, dg3_ref):
        i = pl.program_id(0)

        @pl.when(i == 0)
        def _():
            loss_ref[...] = jnp.zeros_like(loss_ref)
            dgf_ref[...] = jnp.zeros_like(dgf_ref)
            dg3_ref[...] = jnp.zeros_like(dg3_ref)

        x2v = x2_ref[...]
        xn, xh3, r3 = _rms_fwd(x2v, g3_ref[...])
        xnb = xn.astype(BF16)
        xn_ref[...] = xnb
        x3 = x2v
        saved = []
        for j in range(2):
            g = _dot(xnb, wgu_ref[j])
            u = _dot(xnb, wgu_ref[2 + j])
            sg = 1.0 / (1.0 + jnp.exp(-g))
            sl = g * sg
            actb = (sl * u).astype(BF16)
            act_ref[:, j * ns:(j + 1) * ns] = actb
            x3 = x3 + _dot(actb, wd_ref[j * ns:(j + 1) * ns, :])
            saved.append((u, sl, sg * (1.0 + g * (1.0 - sg))))
        gfv = gf_ref[...]
        y, xhf, rf = _rms_fwd(x3, gfv)
        e = y - t_ref[...]
        loss_ref[...] += 0.5 * jnp.sum(jnp.sum(e * e, axis=-1, keepdims=True), axis=0, keepdims=True) / d
        dx3, dgf = _rms_bwd(e / d, xhf, rf, gfv)
        dgf_ref[...] += dgf
        dx3b = dx3.astype(BF16)
        dx3_ref[...] = dx3b
        dxn = jnp.zeros_like(x2v)
        for j in range(2):
            u, sl, dsl = saved[j]
            dact = _dot_nt(dx3b, wd_ref[j * ns:(j + 1) * ns, :])
            dgb = (dact * u * dsl).astype(BF16)
            dub = (dact * sl).astype(BF16)
            dgu_ref[:, j * ns:(j + 1) * ns] = dgb
            dgu_ref[:, ff + j * ns:ff + (j + 1) * ns] = dub
            dxn = dxn + _dot_nt(dgb, wgu_ref[j]) + _dot_nt(dub, wgu_ref[2 + j])
        dxr, dg3 = _rms_bwd(dxn, xh3, r3, g3_ref[...])
        dg3_ref[...] += dg3
        dx2_ref[...] = dx3 + dxr

    vec = jax.ShapeDtypeStruct((1, d), F32)
    return pl.pallas_call(
        body, name="ffn_fwd_bwd", grid=(n,),
        in_specs=[_rows(TM, d), _full(g3.shape), _full(gf.shape), _rows(TM, d), _full(w_gu.shape),
                  _full(w_down.shape)],
        out_specs=[_rows(TM, d), _rows(TM, ff), _rows(TM, 2 * ff), _rows(TM, d), _rows(TM, d),
                   _acc((1, 1)), _acc((1, d)), _acc((1, d))],
        out_shape=(jax.ShapeDtypeStruct((s, d), F32), jax.ShapeDtypeStruct((s, ff), BF16),
                   jax.ShapeDtypeStruct((s, 2 * ff), BF16), jax.ShapeDtypeStruct((s, d), BF16),
                   jax.ShapeDtypeStruct((s, d), BF16), jax.ShapeDtypeStruct((1, 1), F32), vec, vec),
        compiler_params=_params("arbitrary"),
    )(x2, g3, gf, target, w_gu, w_down)


def _attn_bwd(x1, dx2, g2, w_q, kv, w_o, hosted=None):
    s, d = x1.shape
    n = s // TM
    dh = d // HEADS
    scale = dh ** -0.5
    m = kv.shape[0]

    def body(x1_ref, dx2_ref, g2_ref, wq_ref, kv_ref, wo_ref, dx1_ref, dq_ref, dkv_ref, dg2_ref):
        i = pl.program_id(0)

        @pl.when(i == 0)
        def _():
            dkv_ref[...] = jnp.zeros_like(dkv_ref)
            dg2_ref[...] = jnp.zeros_like(dg2_ref)

        xn, xh2, r2 = _rms_fwd(x1_ref[...], g2_ref[...])
        q = _dot(xn.astype(BF16), wq_ref[...])
        dx2v = dx2_ref[...]
        do = _dot_nt(dx2v.astype(BF16), wo_ref[...])
        for hd in range(HEADS):
            qb = q[:, hd * dh:(hd + 1) * dh].astype(BF16)
            p, kh, vh = _attn_probs(qb, kv_ref, hd, dh, d, scale)
            dob = do[:, hd * dh:(hd + 1) * dh].astype(BF16)
            dp = _dot_nt(dob, vh)
            ds = p * (dp - jnp.sum(dp * p, axis=-1, keepdims=True))
            dsb = (ds * scale).astype(BF16)
            dq_ref[:, hd * dh:(hd + 1) * dh] = _dot(dsb, kh).astype(BF16)
            dkv_ref[:, hd * dh:(hd + 1) * dh] += _dot_tn(dsb, qb)
            dkv_ref[:, d + hd * dh:d + (hd + 1) * dh] += _dot_tn(p.astype(BF16), dob)
        dxn = _dot_nt(dq_ref[...], wq_ref[...])
        dxr, dg2 = _rms_bwd(dxn, xh2, r2, g2_ref[...])
        dg2_ref[...] += dg2
        dx1_ref[...] = dx2v + dxr

    return _host_call(
        body, "attn_bwd", (n,),
        [_rows(TM, d), _rows(TM, d), _full(g2.shape), _full(w_q.shape), _full(kv.shape), _full(w_o.shape)],
        [_rows(TM, d), _rows(TM, d), _acc((m, 2 * d)), _acc((1, d))],
        (jax.ShapeDtypeStruct((s, d), F32), jax.ShapeDtypeStruct((s, d), BF16),
         jax.ShapeDtypeStruct((m, 2 * d), F32), jax.ShapeDtypeStruct((1, d), F32)),
        [], (x1, dx2, g2, w_q, kv, w_o), ("arbitrary",), hosted)


def _kv_bwd(dkv, mem, memn, g_mem, w_kv):
    m, d = mem.shape
    ns = w_kv.shape[2]

    def body(dkv_ref, mem_ref, memn_ref, g_ref, w_ref, gw_ref, dg_ref):
        _, xh, _ = _rms_fwd(mem_ref[...], g_ref[...])
        dmemn = jnp.zeros((m, d), F32)
        for k in range(N_CHIPS):
            dkb = dkv_ref[:, k * ns:(k + 1) * ns].astype(BF16)
            gw_ref[k] = _dot_tn(memn_ref[...], dkb).astype(BF16)
            dmemn = dmemn + _dot_nt(dkb, w_ref[k])
        dg_ref[...] = jnp.sum(dmemn * xh, axis=0, keepdims=True)

    return pl.pallas_call(
        body, name="kv_bwd",
        out_shape=(jax.ShapeDtypeStruct((N_CHIPS, d, ns), BF16), jax.ShapeDtypeStruct((1, d), F32)),
        compiler_params=pltpu.CompilerParams(vmem_limit_bytes=VMEM_LIMIT),
    )(dkv, mem, memn, g_mem, w_kv)


def _mixer_bwd(x, dx1, h, g1, lng, lnb, wsp, b_t, cw, ga, gb, w_out, w_in, hosted=None):
    s, d = x.shape
    n = s // TM
    nch = TM // CHUNK
    ns = w_in.shape[2]
    nh = N_CHIPS * ns
    aw = d // 2
    hd_w = aw // HEADS

    def rev(cols):
        return pl.BlockSpec((TM, cols), lambda i: (n - 1 - i, 0))

    hprev = pl.BlockSpec((8, nh), lambda i: (jnp.maximum((n - 1 - i) * (TM // 8) - 1, 0), 0))

    def body(x_ref, dx1_ref, h_ref, hp_ref, g1_ref, lng_ref, lnb_ref, wsp_ref, bt_ref, cw_ref, ga_ref, gb_ref,
             wout_ref, win_ref,
             dx_ref, dh_ref, dg1_ref, dlng_ref, dlnb_ref, dwsp_ref, dbt_ref, dcw_ref, dga_ref, dgb_ref,
             mix_ref, dvn_ref, dcn_ref):
        i = pl.program_id(0)

        @pl.when(i == 0)
        def _():
            for r in (dg1_ref, dlng_ref, dlnb_ref, dwsp_ref, dbt_ref, dcw_ref, dga_ref, dgb_ref, dcn_ref):
                r[...] = jnp.zeros_like(r)

        dx1v = dx1_ref[...]
        dycat = _dot_nt(dx1v.astype(BF16), wout_ref[...])
        ha = h_ref[:, 0:2 * aw]
        a, th = _gelu_parts(ha)
        u = a[:, :aw]
        lngv = lng_ref[...]
        vn, vhat, rs = _layer_norm_parts(a[:, aw:], lngv, lnb_ref[...])
        vnb = vn.astype(BF16)
        wm = _tril_weights(wsp_ref)
        for c in range(nch):
            for hd in range(HEADS):
                blk = vnb[c * CHUNK:(c + 1) * CHUNK, hd * hd_w:(hd + 1) * hd_w]
                mix_ref[c * CHUNK:(c + 1) * CHUNK, hd * hd_w:(hd + 1) * hd_w] = _dot(wm[hd], blk) + bt_ref[:, hd:hd + 1]
        mixed = mix_ref[...]
        gav = ga_ref[...]
        _, yah, ra = _rms_fwd(u * mixed, gav)
        dya, dga = _rms_bwd(dycat[:, :aw], yah, ra, gav)
        dga_ref[...] += dga
        du = dya * mixed
        dmix = dya * u
        dmb = dmix.astype(BF16)
        tri = lax.broadcasted_iota(jnp.int32, (CHUNK, CHUNK), 0) >= lax.broadcasted_iota(jnp.int32, (CHUNK, CHUNK), 1)
        for hd in range(HEADS):
            dw = jnp.zeros((CHUNK, CHUNK), F32)
            db = jnp.zeros((CHUNK, 1), F32)
            for c in range(nch):
                rows = slice(c * CHUNK, (c + 1) * CHUNK)
                cols = slice(hd * hd_w, (hd + 1) * hd_w)
                dvn_ref[rows, cols] = _dot_tn(wm[hd], dmb[rows, cols])
                dw = dw + _dot_nt(dmb[rows, cols], vnb[rows, cols])
                db = db + jnp.sum(dmix[rows, cols], axis=1, keepdims=True)
            dwsp_ref[hd] += jnp.where(tri, dw, 0.0)
            dbt_ref[:, hd:hd + 1] += db
        dvn = dvn_ref[...]
        dlng_ref[...] += jnp.sum(dvn * vhat, axis=0, keepdims=True)
        dlnb_ref[...] += jnp.sum(dvn, axis=0, keepdims=True)
        dvh = dvn * lngv
        dv = rs * (dvh - jnp.mean(dvh, axis=-1, keepdims=True) - vhat * jnp.mean(dvh * vhat, axis=-1, keepdims=True))
        gprime = 0.5 * (1.0 + th) + 0.5 * ha * (1.0 - th * th) * (GELU_C * (1.0 + 3.0 * GELU_K * (ha * ha)))
        dh_ref[:, 0:2 * aw] = (jnp.concatenate([du, dv], axis=-1) * gprime).astype(BF16)
        g_b = h_ref[:, 2 * aw:3 * aw]
        g_c = h_ref[:, 3 * aw:4 * aw]
        val = h_ref[:, 4 * aw:5 * aw]
        z = g_c * val
        zp = jnp.where(i == n - 1, 0.0, hp_ref[:, 3 * aw:4 * aw] * hp_ref[:, 4 * aw:5 * aw])
        zm1, zm2 = _shift_rows(z, zp)
        cw0, cw1, cw2 = cw_ref[0:1, :], cw_ref[1:2, :], cw_ref[2:3, :]
        conv = cw0 * zm2 + cw1 * zm1 + cw2 * z
        gbv = gb_ref[...]
        _, ybh, rb = _rms_fwd(g_b * conv, gbv)
        dyb, dgb = _rms_bwd(dycat[:, aw:], ybh, rb, gbv)
        dgb_ref[...] += dgb
        dconv = dyb * g_b
        dcw_ref[0:1, :] += jnp.sum(dconv * zm2, axis=0, keepdims=True)
        dcw_ref[1:2, :] += jnp.sum(dconv * zm1, axis=0, keepdims=True)
        dcw_ref[2:3, :] += jnp.sum(dconv * z, axis=0, keepdims=True)
        nxt = dcn_ref[...]
        row = lax.broadcasted_iota(jnp.int32, dconv.shape, 0)
        dcp1 = jnp.where(row == TM - 1, nxt[0:1, :], pltpu.roll(dconv, TM - 1, 0))
        dcp2 = jnp.where(row == TM - 1, nxt[1:2, :],
                         jnp.where(row == TM - 2, nxt[0:1, :], pltpu.roll(dconv, TM - 2, 0)))
        dz = cw2 * dconv + cw1 * dcp1 + cw0 * dcp2
        dcn_ref[...] = dconv[0:8, :]
        dh_ref[:, 2 * aw:3 * aw] = (dyb * conv).astype(BF16)
        dh_ref[:, 3 * aw:4 * aw] = (dz * val).astype(BF16)
        dh_ref[:, 4 * aw:5 * aw] = (dz * g_c).astype(BF16)
        dxn = jnp.zeros((TM, d), F32)
        for k in range(N_CHIPS):
            dxn = dxn + _dot_nt(dh_ref[:, k * ns:(k + 1) * ns], win_ref[k])
        g1v = g1_ref[...]
        _, xh1, r1 = _rms_fwd(x_ref[...], g1v)
        dxr, dg1 = _rms_bwd(dxn, xh1, r1, g1v)
        dg1_ref[...] += dg1
        dx_ref[...] = dx1v + dxr

    ins = (x, dx1, h, h, g1, lng, lnb, wsp, b_t, cw, ga, gb, w_out, w_in)
    acc_shapes = [(1, d), (1, aw), (1, aw), wsp.shape, (CHUNK, CHUNK), cw.shape, (1, aw), (1, aw)]
    return _host_call(
        body, "mixer_bwd", (n,),
        [rev(d), rev(d), rev(nh), hprev] + [_full(a.shape) for a in ins[4:]],
        [rev(d), rev(nh)] + [_acc(sh) for sh in acc_shapes],
        (jax.ShapeDtypeStruct((s, d), F32), jax.ShapeDtypeStruct((s, nh), BF16))
        + tuple(jax.ShapeDtypeStruct(sh, F32) for sh in acc_shapes),
        [pltpu.VMEM((TM, aw), F32), pltpu.VMEM((TM, aw), F32), pltpu.VMEM((8, aw), F32)],
        ins, ("arbitrary",), hosted)


def _weight_grad(a, b, name, tm, tn, col_sharded, hosted=None):
    t, m = a.shape
    n = b.shape[1]

    def body(a_ref, b_ref, o_ref):
        o_ref[...] = _dot_tn(a_ref[...].astype(BF16), b_ref[...].astype(BF16)).astype(BF16)

    if col_sharded:
        ns = n // N_CHIPS
        per = ns // tn
        out_shape = jax.ShapeDtypeStruct((N_CHIPS, m, ns), BF16)
        out_spec = pl.BlockSpec((None, tm, tn), lambda i, j: (j // per, i, j % per))
    else:
        out_shape = jax.ShapeDtypeStruct((m, n), BF16)
        out_spec = pl.BlockSpec((tm, tn), lambda i, j: (i, j))
    (out,), extra = _host_call(
        body, name, (m // tm, n // tn),
        [pl.BlockSpec((t, tm), lambda i, j: (0, i)), pl.BlockSpec((t, tn), lambda i, j: (0, j))],
        [out_spec], (out_shape,), [], (a, b), ("parallel", "parallel"), hosted)
    return (out if col_sharded else out.reshape(N_CHIPS, m // N_CHIPS, n)), extra


def _row_tile(rows, cap=256):
    best = None
    for t in range(16, min(rows, cap) + 1, 16):
        if rows % t == 0:
            best = t
    return best if best is not None else rows


def _chip_sum(grad, got, place, name):
    nb, rh, c = got.shape
    tr = _row_tile(rh)
    per = rh // tr

    def body(place_ref, a_ref, b_ref, o_ref):
        o_ref[...] = (a_ref[...].astype(F32) + b_ref[...].astype(F32)).astype(BF16)

    blk = pl.BlockSpec((None, tr, c), lambda k, i, pref: (k, i, 0))
    return pl.pallas_call(
        body, name=name,
        grid_spec=pltpu.PrefetchScalarGridSpec(
            num_scalar_prefetch=1, grid=(nb, per),
            in_specs=[pl.BlockSpec((None, tr, c), lambda k, i, pref: (k, pref[0] * per + i, 0)), blk],
            out_specs=blk),
        out_shape=jax.ShapeDtypeStruct(got.shape, BF16),
        compiler_params=_params("parallel", "parallel"),
    )(place, grad, got)


def _final_sum(part, slots, place, name):
    _, rh, c = slots.shape
    tr = _row_tile(rh)
    per = rh // tr

    def body(place_ref, own_ref, s0_ref, s1_ref, s2_ref, o_ref):
        acc = own_ref[...].astype(F32) + s0_ref[...].astype(F32)
        o_ref[...] = (acc + s1_ref[...].astype(F32)) + s2_ref[...].astype(F32)

    def slot(r):
        return pl.BlockSpec((None, tr, c), lambda i, pref: (r, i, 0))

    return pl.pallas_call(
        body, name=name,
        grid_spec=pltpu.PrefetchScalarGridSpec(
            num_scalar_prefetch=1, grid=(per,),
            in_specs=[pl.BlockSpec((None, tr, c), lambda i, pref: (pref[1], i, 0)), slot(0), slot(1), slot(2)],
            out_specs=pl.BlockSpec((tr, c), lambda i, pref: (pref[0] * per + i, 0))),
        out_shape=jax.ShapeDtypeStruct((2 * rh, c), F32),
        compiler_params=_params("parallel"),
    )(place, part, slots, slots, slots)


def _adamw_math(w, g, m, v):
    m2 = ADAM_B1 * m + (1.0 - ADAM_B1) * g
    v2 = ADAM_B2 * v + (1.0 - ADAM_B2) * (g * g)
    m_hat = m2 / (1.0 - ADAM_B1 ** ADAM_STEP)
    v_hat = v2 / (1.0 - ADAM_B2 ** ADAM_STEP)
    delta = -ADAM_LR * (m_hat / (jnp.sqrt(v_hat) + ADAM_EPS) + ADAM_WD * w)
    return delta, m2, v2


def _adamw(w, g, m, v, name):
    r, c = w.shape
    tr = _row_tile(r) if r >= 16 else r

    def body(w_ref, g_ref, m_ref, v_ref, d_ref, m2_ref, v2_ref):
        d_ref[...], m2_ref[...], v2_ref[...] = _adamw_math(w_ref[...], g_ref[...], m_ref[...], v_ref[...])

    sh = jax.ShapeDtypeStruct((r, c), F32)
    return pl.pallas_call(
        body, name=name, grid=(r // tr,),
        in_specs=[_rows(tr, c)] * 4, out_specs=[_rows(tr, c)] * 3, out_shape=(sh, sh, sh),
        compiler_params=_params("parallel"),
    )(w, g, m, v)


def _small_sum_adamw(parts, w, m, v):
    nd, r, c = parts.shape

    def body(p_ref, w_ref, m_ref, v_ref, g_ref, d_ref, m2_ref, v2_ref):
        g = p_ref[0]
        for k in range(1, nd):
            g = g + p_ref[k]
        g_ref[...] = g
        d_ref[...], m2_ref[...], v2_ref[...] = _adamw_math(w_ref[...], g, m_ref[...], v_ref[...])

    sh = jax.ShapeDtypeStruct((r, c), F32)
    return pl.pallas_call(
        body, name="small_sum_adamw", out_shape=(sh, sh, sh, sh),
        compiler_params=pltpu.CompilerParams(vmem_limit_bytes=VMEM_LIMIT),
    )(parts, w, m, v)


def _place():
    x, y, c = lax.axis_index("x"), lax.axis_index("y"), lax.axis_index("c")
    chips = [(1 - x, y), (x, 1 - y), (1 - x, 1 - y)]
    return x, y, c, 2 * x + y, chips


def _remote(src, dst, send_sem, recv_sem, to):
    return pltpu.make_async_remote_copy(src_ref=src, dst_ref=dst, send_sem=send_sem, recv_sem=recv_sem,
                                        device_id=to, device_id_type=MESH)


class _Exchange:
    def __init__(self, ins, out_shapes, sem_shape, start, finish):
        self.ins, self.out_shapes, self.sem_shape = tuple(ins), tuple(out_shapes), sem_shape
        self.start, self.finish = start, finish


def _run_exchange(ex, name, aliases=None):
    n_in, n_out = len(ex.ins), len(ex.out_shapes)

    def body(*refs):
        ins, outs = refs[:n_in], refs[n_in:n_in + n_out]
        send_sems, recv_sems = refs[n_in + n_out:]
        ex.start(ins, outs, send_sems, recv_sems)
        ex.finish(ins, outs, send_sems, recv_sems)

    sem = pltpu.SemaphoreType.DMA(ex.sem_shape)
    return pl.pallas_call(
        body, name=name, out_shape=ex.out_shapes, in_specs=[ANY] * n_in, out_specs=[ANY] * n_out,
        input_output_aliases=aliases or {}, scratch_shapes=[sem, sem],
    )(*ex.ins)


def _host_call(body, name, grid, in_specs, out_specs, out_shape, scratch_shapes, args, semantics, hosted):
    n_in, n_out, n_scr = len(in_specs), len(out_specs), len(scratch_shapes)
    if hosted is None:
        res = pl.pallas_call(
            body, name=name, grid=grid, in_specs=list(in_specs), out_specs=list(out_specs),
            out_shape=tuple(out_shape), scratch_shapes=list(scratch_shapes), compiler_params=_params(*semantics),
        )(*args)
        return res, ()
    h_in, h_out = len(hosted.ins), len(hosted.out_shapes)

    def wrapped(*refs):
        a, hi = refs[:n_in], refs[n_in:n_in + h_in]
        o = refs[n_in + h_in:n_in + h_in + n_out]
        ho = refs[n_in + h_in + n_out:n_in + h_in + n_out + h_out]
        scr = refs[n_in + h_in + n_out + h_out:]
        send_sems, recv_sems = scr[n_scr], scr[n_scr + 1]
        first = functools.reduce(jnp.logical_and, [pl.program_id(k) == 0 for k in range(len(grid))])
        last = functools.reduce(jnp.logical_and, [pl.program_id(k) == grid[k] - 1 for k in range(len(grid))])

        @pl.when(first)
        def _():
            hosted.start(hi, ho, send_sems, recv_sems)

        body(*a, *o, *scr[:n_scr])

        @pl.when(last)
        def _():
            hosted.finish(hi, ho, send_sems, recv_sems)

    sem = pltpu.SemaphoreType.DMA(hosted.sem_shape)
    res = pl.pallas_call(
        wrapped, name=name, grid=grid, in_specs=list(in_specs) + [ANY] * h_in,
        out_specs=list(out_specs) + [ANY] * h_out, out_shape=tuple(out_shape) + hosted.out_shapes,
        scratch_shapes=list(scratch_shapes) + [sem, sem], compiler_params=_params(*(["arbitrary"] * len(grid))),
    )(*args, *hosted.ins)
    return res[:n_out], res[n_out:]


def _all_gather(shards, small=()):
    items = tuple(shards) + tuple(small)
    nw = len(shards)

    def copies(ins, outs, ss, rs):
        x, y, c, me, chips = _place()
        sib = (x, y, 1 - c)
        swap = [_remote(ins[w], outs[w].at[me], ss.at[w, 6], rs.at[w, 6], sib) for w in range(len(items))]
        ici, landed, fwd, passed = [], [], [], []
        for r, (px, py) in enumerate(chips):
            pk, to = 2 * px + py, (px, py, c)
            for w in range(len(items)):
                if w < nw:
                    rh = items[w].shape[0] // 2
                    mine, theirs = pl.ds(c * rh, rh), pl.ds((1 - c) * rh, rh)
                    ici.append(_remote(ins[w].at[mine], outs[w].at[me, mine], ss.at[w, r], rs.at[w, r], to))
                    got = outs[w].at[pk, mine]
                    landed.append(_remote(got, got, ss.at[w, r], rs.at[w, r], to))
                    fwd.append(_remote(got, got, ss.at[w, 3 + r], rs.at[w, 3 + r], sib))
                    oth = outs[w].at[pk, theirs]
                    passed.append(_remote(oth, oth, ss.at[w, 3 + r], rs.at[w, 3 + r], sib))
                else:
                    ici.append(_remote(ins[w], outs[w].at[me], ss.at[w, r], rs.at[w, r], to))
                    got = outs[w].at[pk]
                    landed.append(_remote(got, got, ss.at[w, r], rs.at[w, r], to))
                    fwd.append(None)
        return swap, ici, landed, fwd, passed

    def start(ins, outs, ss, rs):
        swap, ici, _, _, _ = copies(ins, outs, ss, rs)
        for cp in swap + ici:
            cp.start()

    def finish(ins, outs, ss, rs):
        swap, ici, landed, fwd, passed = copies(ins, outs, ss, rs)
        for arrival, onward in zip(landed, fwd):
            arrival.wait_recv()
            if onward is not None:
                onward.start()
        for cp in passed:
            cp.wait_recv()
        for cp in swap:
            cp.wait()
        for cp in ici + [f for f in fwd if f is not None]:
            cp.wait_send()

    out_shapes = tuple(jax.ShapeDtypeStruct((N_CHIPS,) + a.shape, a.dtype) for a in items)
    return _Exchange(items, out_shapes, (len(items), 7), start, finish)


def _exchange_halves(grads):
    nw = len(grads)

    def copies(ins, outs, ss, rs):
        x, y, c, _, _ = _place()
        res = []
        for w in range(nw):
            rh = grads[w].shape[1] // 2
            res.append(_remote(ins[w].at[:, pl.ds((1 - c) * rh, rh), :], outs[w], ss.at[w], rs.at[w], (x, y, 1 - c)))
        return res

    def start(ins, outs, ss, rs):
        for cp in copies(ins, outs, ss, rs):
            cp.start()

    def finish(ins, outs, ss, rs):
        for cp in copies(ins, outs, ss, rs):
            cp.wait()

    halves = tuple(jax.ShapeDtypeStruct((g.shape[0], g.shape[1] // 2, g.shape[2]), g.dtype) for g in grads)
    return _Exchange(grads, halves, (nw,), start, finish)


def _scatter_partials(parts):
    nw = len(parts)

    def copies(ins, outs, ss, rs):
        _, _, c, _, chips = _place()
        return [_remote(ins[w].at[2 * px + py], outs[w].at[r], ss.at[w, r], rs.at[w, r], (px, py, c))
                for r, (px, py) in enumerate(chips) for w in range(nw)]

    def start(ins, outs, ss, rs):
        for cp in copies(ins, outs, ss, rs):
            cp.start()

    def finish(ins, outs, ss, rs):
        for cp in copies(ins, outs, ss, rs):
            cp.wait()

    out_shapes = tuple(jax.ShapeDtypeStruct((3,) + p.shape[1:], p.dtype) for p in parts)
    return _Exchange(parts, out_shapes, (nw, 3), start, finish)


def _join_halves(shards):
    nw = len(shards)

    def start(ins, outs, ss, rs):
        x, y, c, _, _ = _place()
        for w in range(nw):
            rh = shards[w].shape[0] // 2
            mine = outs[w].at[pl.ds(c * rh, rh)]
            _remote(mine, mine, ss.at[w], rs.at[w], (x, y, 1 - c)).start()

    def finish(ins, outs, ss, rs):
        x, y, c, _, _ = _place()
        for w in range(nw):
            rh = shards[w].shape[0] // 2
            mine = outs[w].at[pl.ds(c * rh, rh)]
            theirs = outs[w].at[pl.ds((1 - c) * rh, rh)]
            _remote(mine, mine, ss.at[w], rs.at[w], (x, y, 1 - c)).wait_send()
            _remote(theirs, theirs, ss.at[w], rs.at[w], (x, y, 1 - c)).wait_recv()

    ex = _Exchange(shards, tuple(jax.ShapeDtypeStruct(s.shape, s.dtype) for s in shards), (nw,), start, finish)
    return _run_exchange(ex, "rs_join_halves", aliases={w: w for w in range(nw)})


def _gather_small(slab):
    def copies(ins, outs, ss, rs):
        x, y, c, _, _ = _place()
        me = 4 * x + 2 * y + c
        own = pltpu.make_async_copy(ins[0], outs[0].at[me], ss.at[7])
        out, arrivals = [], []
        for k in range(1, 8):
            px = 1 - x if k & 4 else x
            py = 1 - y if k & 2 else y
            pc = 1 - c if k & 1 else c
            out.append(_remote(ins[0], outs[0].at[me], ss.at[k - 1], rs.at[k - 1], (px, py, pc)))
            theirs = outs[0].at[4 * px + 2 * py + pc]
            arrivals.append(_remote(theirs, theirs, ss.at[k - 1], rs.at[k - 1], (px, py, pc)))
        return own, out, arrivals

    def start(ins, outs, ss, rs):
        own, out, _ = copies(ins, outs, ss, rs)
        own.start()
        for cp in out:
            cp.start()

    def finish(ins, outs, ss, rs):
        own, out, arrivals = copies(ins, outs, ss, rs)
        for cp in out:
            cp.wait_send()
        for cp in arrivals:
            cp.wait_recv()
        own.wait()

    return _Exchange((slab,), (jax.ShapeDtypeStruct((8,) + slab.shape, slab.dtype),), (8,), start, finish)


_SMALL_VECS = ("ln_mix_g", "ln_attn_g", "ln_mem_g", "ln_ffn_g", "ln_final_g")


def _pack_small(p, extra, conv):
    d = p["ln_mix_g"].shape[-1]
    top = [p[k].reshape(1, d) for k in _SMALL_VECS]
    top.append(jnp.concatenate([p["sgu_ln_g"].reshape(-1), p["sgu_ln_b"].reshape(-1)]).reshape(1, d))
    top.append(jnp.concatenate([p["grp_norm_a"].reshape(-1), p["grp_norm_b"].reshape(-1)]).reshape(1, d))
    top.append(jnp.concatenate([p["b_spatial"].reshape(-1), extra]).reshape(1, d))
    mid = jnp.zeros((8, d), F32)
    if conv is not None:
        mid = jnp.pad(conv, ((0, 5), (0, d - conv.shape[1])))
    return jnp.concatenate([jnp.concatenate(top, axis=0), mid, p["w_spatial"].reshape(-1, d)], axis=0)


def _unpack_small(slab):
    d = slab.shape[1]
    hw = d // 2
    out = {k: slab[i] for i, k in enumerate(_SMALL_VECS)}
    out["sgu_ln_g"], out["sgu_ln_b"] = slab[5, :hw], slab[5, hw:]
    out["grp_norm_a"], out["grp_norm_b"] = slab[6, :hw], slab[6, hw:]
    out["b_spatial"] = slab[7, :hw].reshape(HEADS, CHUNK)
    out["w_spatial"] = slab[16:].reshape(HEADS, CHUNK, CHUNK)
    return out


_BIG = ("w_in", "w_kv", "w_gate_up", "w_out", "w_q", "w_o", "w_down")
_WEIGHTS = ("ln_mix_g", "w_in", "sgu_ln_g", "sgu_ln_b", "w_spatial", "b_spatial", "conv_w", "grp_norm_a",
            "grp_norm_b", "w_out", "ln_attn_g", "ln_mem_g", "w_q", "w_kv", "w_o", "ln_ffn_g", "w_gate_up",
            "w_down", "ln_final_g")


def _step(p, m_, v_, x, mem, target):
    s, d = x.shape
    hw = d // 2
    row = lambda a: a.reshape(1, -1)
    x_, y_, c_ = lax.axis_index("x"), lax.axis_index("y"), lax.axis_index("c")
    chip = 2 * x_ + y_

    bf = {k: p[k].astype(BF16) for k in _BIG}
    conv8 = jnp.pad(p["conv_w"], ((0, 5), (0, 0)))
    w_in, w_out4, conv4 = _run_exchange(_all_gather([bf["w_in"], bf["w_out"]], [conv8]), "all_gather_mixer")
    cw = jnp.transpose(conv4[:, :3, :], (1, 0, 2)).reshape(3, hw)
    b_t = jnp.pad(jnp.transpose(p["b_spatial"]), ((0, 0), (0, CHUNK - HEADS)))
    g1, g2, gm, g3, gf = (row(p[k]) for k in _SMALL_VECS)
    lng, lnb, ga, gb = row(p["sgu_ln_g"]), row(p["sgu_ln_b"]), row(p["grp_norm_a"]), row(p["grp_norm_b"])
    wsp = p["w_spatial"]
    w_out = w_out4.reshape(-1, d)

    (h, x1, ycat, xn1), (w_kv, w_q4, w_o4, w_down4) = _mixer_fwd(
        x, g1, w_in, lng, lnb, wsp, b_t, cw, ga, gb, w_out,
        hosted=_all_gather([bf[k] for k in ("w_kv", "w_q", "w_o", "w_down")]))
    w_q, w_o, w_down = (a.reshape(-1, d) for a in (w_q4, w_o4, w_down4))
    memn, kv = _kv_fwd(mem, gm, w_kv)
    (x2, o, xn2), (w_gu,) = _attn_fwd(x1, g2, w_q, kv, w_o, hosted=_all_gather([bf["w_gate_up"]]))
    dx2, act, dgu, xn3, dx3, loss, dgf, dg3 = _ffn_fwd_bwd(x2, g3, gf, target, w_gu, w_down)

    place = jnp.stack([c_, chip]).astype(jnp.int32)

    def chip_partials(names, grads, tag):
        got = _run_exchange(_exchange_halves(grads), "rs_exchange_" + tag)
        return [_chip_sum(g, b, place, "chip_sum_" + k) for k, g, b in zip(names, grads, got)]

    names_a = ("w_gate_up", "w_down")
    parts_a = chip_partials(names_a, (_weight_grad(xn3, dgu, "grad_w_gate_up", 512, 1408, True)[0],
                                      _weight_grad(act, dx3, "grad_w_down", 1408, 512, False)[0]), "ffn")
    (dx1, dq, dkv, dg2), slots_a = _attn_bwd(x1, dx2, g2, w_q, kv, w_o, hosted=_scatter_partials(parts_a))
    g_kv, dgm = _kv_bwd(dkv, mem, memn, gm, w_kv)
    names_b = ("w_out", "w_q", "w_o", "w_kv")
    parts_b = chip_partials(names_b, (_weight_grad(ycat, dx1, "grad_w_out", 512, 512, False)[0],
                                      _weight_grad(xn2, dq, "grad_w_q", 512, 512, False)[0],
                                      _weight_grad(o, dx2, "grad_w_o", 512, 512, False)[0], g_kv), "attn")
    (dx, dh, dg1, dlng, dlnb, dwsp, dbt, dcw, dga, dgb), slots_b = _mixer_bwd(
        x, dx1, h, g1, lng, lnb, wsp, b_t, cw, ga, gb, w_out, w_in, hosted=_scatter_partials(parts_b))
    small = {"ln_mix_g": dg1, "ln_attn_g": dg2, "ln_mem_g": dgm, "ln_ffn_g": dg3, "ln_final_g": dgf,
             "sgu_ln_g": dlng, "sgu_ln_b": dlnb, "grp_norm_a": dga, "grp_norm_b": dgb,
             "b_spatial": jnp.transpose(dbt[:, :HEADS]), "w_spatial": dwsp}
    loss_vec = jnp.pad(loss.reshape(1), (0, hw - 1))
    g_in, (parts,) = _weight_grad(xn1, dh, "grad_w_in", 512, 640, True,
                                  hosted=_gather_small(_pack_small(small, loss_vec, dcw)))
    names_c = ("w_in",)
    parts_c = chip_partials(names_c, (g_in,), "mixer")
    slots_c = _run_exchange(_scatter_partials(parts_c), "rs_scatter_mixer")
    halves = [_final_sum(a, sl, place, "final_sum_" + k)
              for k, a, sl in zip(names_a + names_b + names_c, parts_a + parts_b + parts_c,
                                  slots_a + slots_b + tuple(slots_c))]
    big_g = dict(zip(names_a + names_b + names_c, _join_halves(halves)))

    zeros = jnp.zeros((hw,), F32)
    sg, sd, sm, sv = _small_sum_adamw(parts, _pack_small(p, zeros, None), _pack_small(m_, zeros, None),
                                      _pack_small(v_, zeros, None))
    out_g, out_d, out_m, out_v = (_unpack_small(a) for a in (sg, sd, sm, sv))
    loss_out = sg[7, hw]
    g_conv = lax.dynamic_slice(sg[8:11, :hw], (0, chip * (hw // N_CHIPS)), (3, hw // N_CHIPS))
    out_g["conv_w"] = g_conv
    out_d["conv_w"], out_m["conv_w"], out_v["conv_w"] = _adamw(p["conv_w"], g_conv, m_["conv_w"], v_["conv_w"],
                                                                "adamw_conv_w")
    for k in _BIG:
        out_g[k] = big_g[k]
        out_d[k], out_m[k], out_v[k] = _adamw(p[k], big_g[k], m_[k], v_[k], "adamw_" + k)
    return loss_out, dx, out_g, out_d, out_m, out_v


def kernel(x, mem, ln_mix_g, w_in, sgu_ln_g, sgu_ln_b, w_spatial, b_spatial, conv_w, grp_norm_a, grp_norm_b, w_out, ln_attn_g, ln_mem_g, w_q, w_kv, w_o, ln_ffn_g, w_gate_up, w_down, ln_final_g, loss_target, m_ln_mix_g, m_w_in, m_sgu_ln_g, m_sgu_ln_b, m_w_spatial, m_b_spatial, m_conv_w, m_grp_norm_a, m_grp_norm_b, m_w_out, m_ln_attn_g, m_ln_mem_g, m_w_q, m_w_kv, m_w_o, m_ln_ffn_g, m_w_gate_up, m_w_down, m_ln_final_g, v_ln_mix_g, v_w_in, v_sgu_ln_g, v_sgu_ln_b, v_w_spatial, v_b_spatial, v_conv_w, v_grp_norm_a, v_grp_norm_b, v_w_out, v_ln_attn_g, v_ln_mem_g, v_w_q, v_w_kv, v_w_o, v_ln_ffn_g, v_w_gate_up, v_w_down, v_ln_final_g):
    p = dict(ln_mix_g=ln_mix_g, w_in=w_in, sgu_ln_g=sgu_ln_g, sgu_ln_b=sgu_ln_b, w_spatial=w_spatial,
             b_spatial=b_spatial, conv_w=conv_w, grp_norm_a=grp_norm_a, grp_norm_b=grp_norm_b, w_out=w_out,
             ln_attn_g=ln_attn_g, ln_mem_g=ln_mem_g, w_q=w_q, w_kv=w_kv, w_o=w_o, ln_ffn_g=ln_ffn_g,
             w_gate_up=w_gate_up, w_down=w_down, ln_final_g=ln_final_g)
    m_ = dict(ln_mix_g=m_ln_mix_g, w_in=m_w_in, sgu_ln_g=m_sgu_ln_g, sgu_ln_b=m_sgu_ln_b, w_spatial=m_w_spatial,
              b_spatial=m_b_spatial, conv_w=m_conv_w, grp_norm_a=m_grp_norm_a, grp_norm_b=m_grp_norm_b,
              w_out=m_w_out, ln_attn_g=m_ln_attn_g, ln_mem_g=m_ln_mem_g, w_q=m_w_q, w_kv=m_w_kv, w_o=m_w_o,
              ln_ffn_g=m_ln_ffn_g, w_gate_up=m_w_gate_up, w_down=m_w_down, ln_final_g=m_ln_final_g)
    v_ = dict(ln_mix_g=v_ln_mix_g, w_in=v_w_in, sgu_ln_g=v_sgu_ln_g, sgu_ln_b=v_sgu_ln_b, w_spatial=v_w_spatial,
              b_spatial=v_b_spatial, conv_w=v_conv_w, grp_norm_a=v_grp_norm_a, grp_norm_b=v_grp_norm_b,
              w_out=v_w_out, ln_attn_g=v_ln_attn_g, ln_mem_g=v_ln_mem_g, w_q=v_w_q, w_kv=v_w_kv, w_o=v_w_o,
              ln_ffn_g=v_ln_ffn_g, w_gate_up=v_w_gate_up, w_down=v_w_down, ln_final_g=v_ln_final_g)
    s, d = x.shape[-2], x.shape[-1]
    loss, dx, g, dl, nm, nv = _step(p, m_, v_, x.reshape(s, d), mem.reshape(-1, d), loss_target.reshape(s, d))
    outs = [loss, dx.reshape(x.shape)]
    for tree in (g, dl, nm, nv):
        outs += [tree[k].reshape(p[k].shape) for k in _WEIGHTS]
    return tuple(outs)
```

```python
import functools
import math

import jax
import jax.numpy as jnp
from jax import lax
from jax.experimental import pallas as pl
from jax.experimental.pallas import tpu as pltpu

F32 = jnp.float32
BF16 = jnp.bfloat16
EPS = 1e-6
CHUNK = 128
HEADS = 4
N_CHIPS = 4
TM = 256
ADAM_LR, ADAM_B1, ADAM_B2, ADAM_EPS, ADAM_WD, ADAM_STEP = 0.001, 0.9, 0.999, 1e-08, 0.01, 10
GELU_C = math.sqrt(2.0 / math.pi)
GELU_K = 0.044715
SMALL_ROWS = 80
VMEM_LIMIT = 56 * 1024 * 1024
MESH = pl.DeviceIdType.MESH
ANY = pl.BlockSpec(memory_space=pl.ANY)


def _params(*sem):
    return pltpu.CompilerParams(dimension_semantics=sem, vmem_limit_bytes=VMEM_LIMIT)


def _dot(a, b):
    return jnp.dot(a, b, preferred_element_type=F32)


def _dot_nt(a, b):
    return lax.dot_general(a, b, (((1,), (1,)), ((), ())), preferred_element_type=F32)


def _dot_tn(a, b):
    return lax.dot_general(a, b, (((0,), (0,)), ((), ())), preferred_element_type=F32)


def _rms_fwd(x, g):
    r = lax.rsqrt(jnp.mean(x * x, axis=-1, keepdims=True) + EPS)
    xh = x * r
    return xh * g, xh, r


def _rms_bwd(dy, xh, r, g):
    dxh = dy * g
    dx = r * (dxh - xh * jnp.mean(dxh * xh, axis=-1, keepdims=True))
    return dx, jnp.sum(dy * xh, axis=0, keepdims=True)


def _full(shape):
    nd = len(shape)
    return pl.BlockSpec(shape, lambda *_: (0,) * nd, pipeline_mode=pl.Buffered(1))


def _acc(shape):
    nd = len(shape)
    return pl.BlockSpec(shape, lambda *_: (0,) * nd)


def _rows(tm, cols):
    return pl.BlockSpec((tm, cols), lambda i: (i, 0))


def _tril_weights(wsp_ref):
    row = lax.broadcasted_iota(jnp.int32, (CHUNK, CHUNK), 0)
    col = lax.broadcasted_iota(jnp.int32, (CHUNK, CHUNK), 1)
    return [jnp.where(row >= col, wsp_ref[hd], 0.0).astype(BF16) for hd in range(HEADS)]


def _shift_rows(z, zp):
    row = lax.broadcasted_iota(jnp.int32, z.shape, 0)
    zm1 = jnp.where(row == 0, zp[7:8, :], pltpu.roll(z, 1, 0))
    zm2 = jnp.where(row == 0, zp[6:7, :], jnp.where(row == 1, zp[7:8, :], pltpu.roll(z, 2, 0)))
    return zm1, zm2


def _gelu_parts(x):
    t = jnp.tanh(GELU_C * (x + GELU_K * (x * x * x)))
    return 0.5 * x * (1.0 + t), t


def _layer_norm_parts(v, g, b):
    mu = jnp.mean(v, axis=-1, keepdims=True)
    vc = v - mu
    rs = lax.rsqrt(jnp.mean(vc * vc, axis=-1, keepdims=True) + EPS)
    vhat = vc * rs
    return vhat * g + b, vhat, rs


def _kv_fwd(mem, g_mem, w_kv):
    m, d = mem.shape
    ns = w_kv.shape[2]

    def body(mem_ref, g_ref, w_ref, memn_ref, kv_ref):
        y, _, _ = _rms_fwd(mem_ref[...], g_ref[...])
        yb = y.astype(BF16)
        memn_ref[...] = yb
        for k in range(N_CHIPS):
            kv_ref[:, k * ns:(k + 1) * ns] = _dot(yb, w_ref[k]).astype(BF16)

    return pl.pallas_call(
        body, name="kv_fwd",
        out_shape=(jax.ShapeDtypeStruct((m, d), BF16), jax.ShapeDtypeStruct((m, N_CHIPS * ns), BF16)),
        compiler_params=pltpu.CompilerParams(vmem_limit_bytes=VMEM_LIMIT),
    )(mem, g_mem, w_kv)


def _mixer_fwd(x, g1, w_in, lng, lnb, wsp, b_t, cw, ga, gb, w_out, hosted=None):
    s, d = x.shape
    n = s // TM
    nch = TM // CHUNK
    ns = w_in.shape[2]
    nh = N_CHIPS * ns
    aw = d // 2
    hd_w = aw // HEADS

    def body(x_ref, g1_ref, win_ref, lng_ref, lnb_ref, wsp_ref, bt_ref, cw_ref, ga_ref, gb_ref, wout_ref,
             h_ref, x1_ref, y_ref, xn_ref, mix_ref, zp_ref):
        i = pl.program_id(0)

        @pl.when(i == 0)
        def _():
            zp_ref[...] = jnp.zeros_like(zp_ref)

        x = x_ref[...]
        xn, _, _ = _rms_fwd(x, g1_ref[...])
        xnb = xn.astype(BF16)
        xn_ref[...] = xnb
        for k in range(N_CHIPS):
            h_ref[:, k * ns:(k + 1) * ns] = _dot(xnb, win_ref[k])
        a, _ = _gelu_parts(h_ref[:, 0:2 * aw])
        u = a[:, :aw]
        vn, _, _ = _layer_norm_parts(a[:, aw:], lng_ref[...], lnb_ref[...])
        vnb = vn.astype(BF16)
        wm = _tril_weights(wsp_ref)
        for c in range(nch):
            for hd in range(HEADS):
                blk = vnb[c * CHUNK:(c + 1) * CHUNK, hd * hd_w:(hd + 1) * hd_w]
                mix_ref[c * CHUNK:(c + 1) * CHUNK, hd * hd_w:(hd + 1) * hd_w] = _dot(wm[hd], blk) + bt_ref[:, hd:hd + 1]
        ya, _, _ = _rms_fwd(u * mix_ref[...], ga_ref[...])
        g_b = h_ref[:, 2 * aw:3 * aw]
        z = h_ref[:, 3 * aw:4 * aw] * h_ref[:, 4 * aw:5 * aw]
        zm1, zm2 = _shift_rows(z, zp_ref[...])
        conv = cw_ref[0:1, :] * zm2 + cw_ref[1:2, :] * zm1 + cw_ref[2:3, :] * z
        yb, _, _ = _rms_fwd(g_b * conv, gb_ref[...])
        zp_ref[...] = z[TM - 8:TM, :]
        ycat = jnp.concatenate([ya, yb], axis=-1).astype(BF16)
        y_ref[...] = ycat
        x1_ref[...] = x + _dot(ycat, wout_ref[...])

    return _host_call(
        body, "mixer_fwd", (n,),
        [_rows(TM, d), _full(g1.shape), _full(w_in.shape), _full(lng.shape), _full(lnb.shape),
         _full(wsp.shape), _full(b_t.shape), _full(cw.shape), _full(ga.shape), _full(gb.shape),
         _full(w_out.shape)],
        [_rows(TM, nh), _rows(TM, d), _rows(TM, d), _rows(TM, d)],
        (jax.ShapeDtypeStruct((s, nh), F32), jax.ShapeDtypeStruct((s, d), F32),
         jax.ShapeDtypeStruct((s, d), BF16), jax.ShapeDtypeStruct((s, d), BF16)),
        [pltpu.VMEM((TM, aw), F32), pltpu.VMEM((8, aw), F32)],
        (x, g1, w_in, lng, lnb, wsp, b_t, cw, ga, gb, w_out), ("arbitrary",), hosted)


def _attn_probs(qb, kv_ref, hd, dh, d, scale):
    kh = kv_ref[:, hd * dh:(hd + 1) * dh]
    vh = kv_ref[:, d + hd * dh:d + (hd + 1) * dh]
    sc = _dot_nt(qb, kh) * scale
    sc = sc - jnp.max(sc, axis=-1, keepdims=True)
    e = jnp.exp(sc)
    return e / jnp.sum(e, axis=-1, keepdims=True), kh, vh


def _attn_fwd(x1, g2, w_q, kv, w_o, hosted=None):
    s, d = x1.shape
    n = s // TM
    dh = d // HEADS
    scale = dh ** -0.5

    def body(x1_ref, g2_ref, wq_ref, kv_ref, wo_ref, x2_ref, o_ref, xn_ref):
        x1v = x1_ref[...]
        xn, _, _ = _rms_fwd(x1v, g2_ref[...])
        xnb = xn.astype(BF16)
        xn_ref[...] = xnb
        q = _dot(xnb, wq_ref[...])
        for hd in range(HEADS):
            qb = q[:, hd * dh:(hd + 1) * dh].astype(BF16)
            p, _, vh = _attn_probs(qb, kv_ref, hd, dh, d, scale)
            o_ref[:, hd * dh:(hd + 1) * dh] = _dot(p.astype(BF16), vh).astype(BF16)
        x2_ref[...] = x1v + _dot(o_ref[...], wo_ref[...])

    return _host_call(
        body, "attn_fwd", (n,),
        [_rows(TM, d), _full(g2.shape), _full(w_q.shape), _full(kv.shape), _full(w_o.shape)],
        [_rows(TM, d), _rows(TM, d), _rows(TM, d)],
        (jax.ShapeDtypeStruct((s, d), F32), jax.ShapeDtypeStruct((s, d), BF16), jax.ShapeDtypeStruct((s, d), BF16)),
        [], (x1, g2, w_q, kv, w_o), ("parallel",), hosted)


def _ffn_fwd_bwd(x2, g3, gf, target, w_gu, w_down):
    s, d = x2.shape
    n = s // TM
    ns = w_gu.shape[2]
    ff = 2 * ns

    def body(x2_ref, g3_ref, gf_ref, t_ref, wgu_ref, wd_ref,
             dx2_ref, act_ref, dgu_ref, xn_ref, dx3_ref, loss_ref, dgf_ref, dg3_ref):
        i = pl.program_id(0)

        @pl.when(i == 0)
        def _():
            loss_ref[...] = jnp.zeros_like(loss_ref)
            dgf_ref[...] = jnp.zeros_like(dgf_ref)
            dg3_ref[...] = jnp.zeros_like(dg3_ref)

        x2v = x2_ref[...]
        xn, xh3, r3 = _rms_fwd(x2v, g3_ref[...])
        xnb = xn.astype(BF16)
        xn_ref[...] = xnb
        x3 = x2v
        saved = []
        for j in range(2):
            g = _dot(xnb, wgu_ref[j])
            u = _dot(xnb, wgu_ref[2 + j])
            sg = 1.0 / (1.0 + jnp.exp(-g))
            sl = g * sg
            actb = (sl * u).astype(BF16)
            act_ref[:, j * ns:(j + 1) * ns] = actb
            x3 = x3 + _dot(actb, wd_ref[j * ns:(j + 1) * ns, :])
            saved.append((u, sl, sg * (1.0 + g * (1.0 - sg))))
        gfv = gf_ref[...]
        y, xhf, rf = _rms_fwd(x3, gfv)
        e = y - t_ref[...]
        loss_ref[...] += 0.5 * jnp.sum(jnp.sum(e * e, axis=-1, keepdims=True), axis=0, keepdims=True) / d
        dx3, dgf = _rms_bwd(e / d, xhf, rf, gfv)
        dgf_ref[...] += dgf
        dx3b = dx3.astype(BF16)
        dx3_ref[...] = dx3b
        dxn = jnp.zeros_like(x2v)
        for j in range(2):
            u, sl, dsl = saved[j]
            dact = _dot_nt(dx3b, wd_ref[j * ns:(j + 1) * ns, :])
            dgb = (dact * u * dsl).astype(BF16)
            dub = (dact * sl).astype(BF16)
            dgu_ref[:, j * ns:(j + 1) * ns] = dgb
            dgu_ref[:, ff + j * ns:ff + (j + 1) * ns] = dub
            dxn = dxn + _dot_nt(dgb, wgu_ref[j]) + _dot_nt(dub, wgu_ref[2 + j])
        dxr, dg3 = _rms_bwd(dxn, xh3, r3, g3_ref[...])
        dg3_ref[...] += dg3
        dx2_ref[...] = dx3 + dxr

    vec = jax.ShapeDtypeStruct((1, d), F32)
    return pl.pallas_call(
        body, name="ffn_fwd_bwd", grid=(n,),
        in_specs=[_rows(TM, d), _full(g3.shape), _full(gf.shape), _rows(TM, d), _full(w_gu.shape),
                  _full(w_down.shape)],
        out_specs=[_rows(TM, d), _rows(TM, ff), _rows(TM, 2 * ff), _rows(TM, d), _rows(TM, d),
                   _acc((1, 1)), _acc((1, d)), _acc((1, d))],
        out_shape=(jax.ShapeDtypeStruct((s, d), F32), jax.ShapeDtypeStruct((s, ff), BF16),
                   jax.ShapeDtypeStruct((s, 2 * ff), BF16), jax.ShapeDtypeStruct((s, d), BF16),
                   jax.ShapeDtypeStruct((s, d), BF16), jax.ShapeDtypeStruct((1, 1), F32), vec, vec),
        compiler_params=_params("arbitrary"),
    )(x2, g3, gf, target, w_gu, w_down)


def _attn_bwd(x1, dx2, g2, w_q, kv, w_o, hosted=None):
    s, d = x1.shape
    n = s // TM
    dh = d // HEADS
    scale = dh ** -0.5
    m = kv.shape[0]

    def body(x1_ref, dx2_ref, g2_ref, wq_ref, kv_ref, wo_ref, dx1_ref, dq_ref, dkv_ref, dg2_ref):
        i = pl.program_id(0)

        @pl.when(i == 0)
        def _():
            dkv_ref[...] = jnp.zeros_like(dkv_ref)
            dg2_ref[...] = jnp.zeros_like(dg2_ref)

        xn, xh2, r2 = _rms_fwd(x1_ref[...], g2_ref[...])
        q = _dot(xn.astype(BF16), wq_ref[...])
        dx2v = dx2_ref[...]
        do = _dot_nt(dx2v.astype(BF16), wo_ref[...])
        for hd in range(HEADS):
            qb = q[:, hd * dh:(hd + 1) * dh].astype(BF16)
            p, kh, vh = _attn_probs(qb, kv_ref, hd, dh, d, scale)
            dob = do[:, hd * dh:(hd + 1) * dh].astype(BF16)
            dp = _dot_nt(dob, vh)
            ds = p * (dp - jnp.sum(dp * p, axis=-1, keepdims=True))
            dsb = (ds * scale).astype(BF16)
            dq_ref[:, hd * dh:(hd + 1) * dh] = _dot(dsb, kh).astype(BF16)
            dkv_ref[:, hd * dh:(hd + 1) * dh] += _dot_tn(dsb, qb)
            dkv_ref[:, d + hd * dh:d + (hd + 1) * dh] += _dot_tn(p.astype(BF16), dob)
        dxn = _dot_nt(dq_ref[...], wq_ref[...])
        dxr, dg2 = _rms_bwd(dxn, xh2, r2, g2_ref[...])
        dg2_ref[...] += dg2
        dx1_ref[...] = dx2v + dxr

    return _host_call(
        body, "attn_bwd", (n,),
        [_rows(TM, d), _rows(TM, d), _full(g2.shape), _full(w_q.shape), _full(kv.shape), _full(w_o.shape)],
        [_rows(TM, d), _rows(TM, d), _acc((m, 2 * d)), _acc((1, d))],
        (jax.ShapeDtypeStruct((s, d), F32), jax.ShapeDtypeStruct((s, d), BF16),
         jax.ShapeDtypeStruct((m, 2 * d), F32), jax.ShapeDtypeStruct((1, d), F32)),
        [], (x1, dx2, g2, w_q, kv, w_o), ("arbitrary",), hosted)


def _kv_bwd(dkv, mem, memn, g_mem, w_kv):
    m, d = mem.shape
    ns = w_kv.shape[2]

    def body(dkv_ref, mem_ref, memn_ref, g_ref, w_ref, gw_ref, dg_ref):
        _, xh, _ = _rms_fwd(mem_ref[...], g_ref[...])
        dmemn = jnp.zeros((m, d), F32)
        for k in range(N_CHIPS):
            dkb = dkv_ref[:, k * ns:(k + 1) * ns].astype(BF16)
            gw_ref[k] = _dot_tn(memn_ref[...], dkb).astype(BF16)
            dmemn = dmemn + _dot_nt(dkb, w_ref[k])
        dg_ref[...] = jnp.sum(dmemn * xh, axis=0, keepdims=True)

    return pl.pallas_call(
        body, name="kv_bwd",
        out_shape=(jax.ShapeDtypeStruct((N_CHIPS, d, ns), BF16), jax.ShapeDtypeStruct((1, d), F32)),
        compiler_params=pltpu.CompilerParams(vmem_limit_bytes=VMEM_LIMIT),
    )(dkv, mem, memn, g_mem, w_kv)


def _mixer_bwd(x, dx1, h, g1, lng, lnb, wsp, b_t, cw, ga, gb, w_out, w_in, hosted=None):
    s, d = x.shape
    n = s // TM
    nch = TM // CHUNK
    ns = w_in.shape[2]
    nh = N_CHIPS * ns
    aw = d // 2
    hd_w = aw // HEADS

    def rev(cols):
        return pl.BlockSpec((TM, cols), lambda i: (n - 1 - i, 0))

    hprev = pl.BlockSpec((8, nh), lambda i: (jnp.maximum((n - 1 - i) * (TM // 8) - 1, 0), 0))

    def body(x_ref, dx1_ref, h_ref, hp_ref, g1_ref, lng_ref, lnb_ref, wsp_ref, bt_ref, cw_ref, ga_ref, gb_ref,
             wout_ref, win_ref,
             dx_ref, dh_ref, dg1_ref, dlng_ref, dlnb_ref, dwsp_ref, dbt_ref, dcw_ref, dga_ref, dgb_ref,
             mix_ref, dvn_ref, dcn_ref):
        i = pl.program_id(0)

        @pl.when(i == 0)
        def _():
            for r in (dg1_ref, dlng_ref, dlnb_ref, dwsp_ref, dbt_ref, dcw_ref, dga_ref, dgb_ref, dcn_ref):
                r[...] = jnp.zeros_like(r)

        dx1v = dx1_ref[...]
        dycat = _dot_nt(dx1v.astype(BF16), wout_ref[...])
        ha = h_ref[:, 0:2 * aw]
        a, th = _gelu_parts(ha)
        u = a[:, :aw]
        lngv = lng_ref[...]
        vn, vhat, rs = _layer_norm_parts(a[:, aw:], lngv, lnb_ref[...])
        vnb = vn.astype(BF16)
        wm = _tril_weights(wsp_ref)
        for c in range(nch):
            for hd in range(HEADS):
                blk = vnb[c * CHUNK:(c + 1) * CHUNK, hd * hd_w:(hd + 1) * hd_w]
                mix_ref[c * CHUNK:(c + 1) * CHUNK, hd * hd_w:(hd + 1) * hd_w] = _dot(wm[hd], blk) + bt_ref[:, hd:hd + 1]
        mixed = mix_ref[...]
        gav = ga_ref[...]
        _, yah, ra = _rms_fwd(u * mixed, gav)
        dya, dga = _rms_bwd(dycat[:, :aw], yah, ra, gav)
        dga_ref[...] += dga
        du = dya * mixed
        dmix = dya * u
        dmb = dmix.astype(BF16)
        tri = lax.broadcasted_iota(jnp.int32, (CHUNK, CHUNK), 0) >= lax.broadcasted_iota(jnp.int32, (CHUNK, CHUNK), 1)
        for hd in range(HEADS):
            dw = jnp.zeros((CHUNK, CHUNK), F32)
            db = jnp.zeros((CHUNK, 1), F32)
            for c in range(nch):
                rows = slice(c * CHUNK, (c + 1) * CHUNK)
                cols = slice(hd * hd_w, (hd + 1) * hd_w)
                dvn_ref[rows, cols] = _dot_tn(wm[hd], dmb[rows, cols])
                dw = dw + _dot_nt(dmb[rows, cols], vnb[rows, cols])
                db = db + jnp.sum(dmix[rows, cols], axis=1, keepdims=True)
            dwsp_ref[hd] += jnp.where(tri, dw, 0.0)
            dbt_ref[:, hd:hd + 1] += db
        dvn = dvn_ref[...]
        dlng_ref[...] += jnp.sum(dvn * vhat, axis=0, keepdims=True)
        dlnb_ref[...] += jnp.sum(dvn, axis=0, keepdims=True)
        dvh = dvn * lngv
        dv = rs * (dvh - jnp.mean(dvh, axis=-1, keepdims=True) - vhat * jnp.mean(dvh * vhat, axis=-1, keepdims=True))
        gprime = 0.5 * (1.0 + th) + 0.5 * ha * (1.0 - th * th) * (GELU_C * (1.0 + 3.0 * GELU_K * (ha * ha)))
        dh_ref[:, 0:2 * aw] = (jnp.concatenate([du, dv], axis=-1) * gprime).astype(BF16)
        g_b = h_ref[:, 2 * aw:3 * aw]
        g_c = h_ref[:, 3 * aw:4 * aw]
        val = h_ref[:, 4 * aw:5 * aw]
        z = g_c * val
        zp = jnp.where(i == n - 1, 0.0, hp_ref[:, 3 * aw:4 * aw] * hp_ref[:, 4 * aw:5 * aw])
        zm1, zm2 = _shift_rows(z, zp)
        cw0, cw1, cw2 = cw_ref[0:1, :], cw_ref[1:2, :], cw_ref[2:3, :]
        conv = cw0 * zm2 + cw1 * zm1 + cw2 * z
        gbv = gb_ref[...]
        _, ybh, rb = _rms_fwd(g_b * conv, gbv)
        dyb, dgb = _rms_bwd(dycat[:, aw:], ybh, rb, gbv)
        dgb_ref[...] += dgb
        dconv = dyb * g_b
        dcw_ref[0:1, :] += jnp.sum(dconv * zm2, axis=0, keepdims=True)
        dcw_ref[1:2, :] += jnp.sum(dconv * zm1, axis=0, keepdims=True)
        dcw_ref[2:3, :] += jnp.sum(dconv * z, axis=0, keepdims=True)
        nxt = dcn_ref[...]
        row = lax.broadcasted_iota(jnp.int32, dconv.shape, 0)
        dcp1 = jnp.where(row == TM - 1, nxt[0:1, :], pltpu.roll(dconv, TM - 1, 0))
        dcp2 = jnp.where(row == TM - 1, nxt[1:2, :],
                         jnp.where(row == TM - 2, nxt[0:1, :], pltpu.roll(dconv, TM - 2, 0)))
        dz = cw2 * dconv + cw1 * dcp1 + cw0 * dcp2
        dcn_ref[...] = dconv[0:8, :]
        dh_ref[:, 2 * aw:3 * aw] = (dyb * conv).astype(BF16)
        dh_ref[:, 3 * aw:4 * aw] = (dz * val).astype(BF16)
        dh_ref[:, 4 * aw:5 * aw] = (dz * g_c).astype(BF16)
        dxn = jnp.zeros((TM, d), F32)
        for k in range(N_CHIPS):
            dxn = dxn + _dot_nt(dh_ref[:, k * ns:(k + 1) * ns], win_ref[k])
        g1v = g1_ref[...]
        _, xh1, r1 = _rms_fwd(x_ref[...], g1v)
        dxr, dg1 = _rms_bwd(dxn, xh1, r1, g1v)
        dg1_ref[...] += dg1
        dx_ref[...] = dx1v + dxr

    ins = (x, dx1, h, h, g1, lng, lnb, wsp, b_t, cw, ga, gb, w_out, w_in)
    acc_shapes = [(1, d), (1, aw), (1, aw), wsp.shape, (CHUNK, CHUNK), cw.shape, (1, aw), (1, aw)]
    return _host_call(
        body, "mixer_bwd", (n,),
        [rev(d), rev(d), rev(nh), hprev] + [_full(a.shape) for a in ins[4:]],
        [rev(d), rev(nh)] + [_acc(sh) for sh in acc_shapes],
        (jax.ShapeDtypeStruct((s, d), F32), jax.ShapeDtypeStruct((s, nh), BF16))
        + tuple(jax.ShapeDtypeStruct(sh, F32) for sh in acc_shapes),
        [pltpu.VMEM((TM, aw), F32), pltpu.VMEM((TM, aw), F32), pltpu.VMEM((8, aw), F32)],
        ins, ("arbitrary",), hosted)


def _weight_grad(a, b, name, tm, tn, col_sharded, hosted=None):
    t, m = a.shape
    n = b.shape[1]

    def body(a_ref, b_ref, o_ref):
        o_ref[...] = _dot_tn(a_ref[...].astype(BF16), b_ref[...].astype(BF16)).astype(BF16)

    if col_sharded:
        ns = n // N_CHIPS
        per = ns // tn
        out_shape = jax.ShapeDtypeStruct((N_CHIPS, m, ns), BF16)
        out_spec = pl.BlockSpec((None, tm, tn), lambda i, j: (j // per, i, j % per))
    else:
        out_shape = jax.ShapeDtypeStruct((m, n), BF16)
        out_spec = pl.BlockSpec((tm, tn), lambda i, j: (i, j))
    (out,), extra = _host_call(
        body, name, (m // tm, n // tn),
        [pl.BlockSpec((t, tm), lambda i, j: (0, i)), pl.BlockSpec((t, tn), lambda i, j: (0, j))],
        [out_spec], (out_shape,), [], (a, b), ("parallel", "parallel"), hosted)
    return (out if col_sharded else out.reshape(N_CHIPS, m // N_CHIPS, n)), extra


def _row_tile(rows, cap=256):
    best = None
    for t in range(16, min(rows, cap) + 1, 16):
        if rows % t == 0:
            best = t
    return best if best is not None else rows


def _chip_sum(grad, got, place, name):
    nb, rh, c = got.shape
    tr = _row_tile(rh)
    per = rh // tr

    def body(place_ref, a_ref, b_ref, o_ref):
        o_ref[...] = (a_ref[...].astype(F32) + b_ref[...].astype(F32)).astype(BF16)

    blk = pl.BlockSpec((None, tr, c), lambda k, i, pref: (k, i, 0))
    return pl.pallas_call(
        body, name=name,
        grid_spec=pltpu.PrefetchScalarGridSpec(
            num_scalar_prefetch=1, grid=(nb, per),
            in_specs=[pl.BlockSpec((None, tr, c), lambda k, i, pref: (k, pref[0] * per + i, 0)), blk],
            out_specs=blk),
        out_shape=jax.ShapeDtypeStruct(got.shape, BF16),
        compiler_params=_params("parallel", "parallel"),
    )(place, grad, got)


def _final_sum(part, slots, place, name):
    _, rh, c = slots.shape
    tr = _row_tile(rh)
    per = rh // tr

    def body(place_ref, own_ref, s0_ref, s1_ref, s2_ref, o_ref):
        acc = own_ref[...].astype(F32) + s0_ref[...].astype(F32)
        o_ref[...] = (acc + s1_ref[...].astype(F32)) + s2_ref[...].astype(F32)

    def slot(r):
        return pl.BlockSpec((None, tr, c), lambda i, pref: (r, i, 0))

    return pl.pallas_call(
        body, name=name,
        grid_spec=pltpu.PrefetchScalarGridSpec(
            num_scalar_prefetch=1, grid=(per,),
            in_specs=[pl.BlockSpec((None, tr, c), lambda i, pref: (pref[1], i, 0)), slot(0), slot(1), slot(2)],
            out_specs=pl.BlockSpec((tr, c), lambda i, pref: (pref[0] * per + i, 0))),
        out_shape=jax.ShapeDtypeStruct((2 * rh, c), F32),
        compiler_params=_params("parallel"),
    )(place, part, slots, slots, slots)


def _adamw_math(w, g, m, v):
    m2 = ADAM_B1 * m + (1.0 - ADAM_B1) * g
    v2 = ADAM_B2 * v + (1.0 - ADAM_B2) * (g * g)
    m_hat = m2 / (1.0 - ADAM_B1 ** ADAM_STEP)
    v_hat = v2 / (1.0 - ADAM_B2 ** ADAM_STEP)
    delta = -ADAM_LR * (m_hat / (jnp.sqrt(v_hat) + ADAM_EPS) + ADAM_WD * w)
    return delta, m2, v2


def _adamw(w, g, m, v, name):
    r, c = w.shape
    tr = _row_tile(r) if r >= 16 else r

    def body(w_ref, g_ref, m_ref, v_ref, d_ref, m2_ref, v2_ref):
        d_ref[...], m2_ref[...], v2_ref[...] = _adamw_math(w_ref[...], g_ref[...], m_ref[...], v_ref[...])

    sh = jax.ShapeDtypeStruct((r, c), F32)
    return pl.pallas_call(
        body, name=name, grid=(r // tr,),
        in_specs=[_rows(tr, c)] * 4, out_specs=[_rows(tr, c)] * 3, out_shape=(sh, sh, sh),
        compiler_params=_params("parallel"),
    )(w, g, m, v)


def _small_sum_adamw(parts, w, m, v):
    nd, r, c = parts.shape

    def body(p_ref, w_ref, m_ref, v_ref, g_ref, d_ref, m2_ref, v2_ref):
        g = p_ref[0]
        for k in range(1, nd):
            g = g + p_ref[k]
        g_ref[...] = g
        d_ref[...], m2_ref[...], v2_ref[...] = _adamw_math(w_ref[...], g, m_ref[...], v_ref[...])

    sh = jax.ShapeDtypeStruct((r, c), F32)
    return pl.pallas_call(
        body, name="small_sum_adamw", out_shape=(sh, sh, sh, sh),
        compiler_params=pltpu.CompilerParams(vmem_limit_bytes=VMEM_LIMIT),
    )(parts, w, m, v)


def _place():
    x, y, c = lax.axis_index("x"), lax.axis_index("y"), lax.axis_index("c")
    chips = [(1 - x, y), (x, 1 - y), (1 - x, 1 - y)]
    return x, y, c, 2 * x + y, chips


def _remote(src, dst, send_sem, recv_sem, to):
    return pltpu.make_async_remote_copy(src_ref=src, dst_ref=dst, send_sem=send_sem, recv_sem=recv_sem,
                                        device_id=to, device_id_type=MESH)


class _Exchange:
    def __init__(self, ins, out_shapes, sem_shape, start, finish, middle=None):
        self.ins, self.out_shapes, self.sem_shape = tuple(ins), tuple(out_shapes), sem_shape
        self.start, self.finish, self.middle = start, finish, middle


def _run_exchange(ex, name, aliases=None):
    n_in, n_out = len(ex.ins), len(ex.out_shapes)

    def body(*refs):
        ins, outs = refs[:n_in], refs[n_in:n_in + n_out]
        send_sems, recv_sems = refs[n_in + n_out:]
        ex.start(ins, outs, send_sems, recv_sems)
        if ex.middle is not None:
            ex.middle(ins, outs, send_sems, recv_sems)
        ex.finish(ins, outs, send_sems, recv_sems)

    sem = pltpu.SemaphoreType.DMA(ex.sem_shape)
    return pl.pallas_call(
        body, name=name, out_shape=ex.out_shapes, in_specs=[ANY] * n_in, out_specs=[ANY] * n_out,
        input_output_aliases=aliases or {}, scratch_shapes=[sem, sem],
    )(*ex.ins)


def _host_call(body, name, grid, in_specs, out_specs, out_shape, scratch_shapes, args, semantics, hosted):
    n_in, n_out, n_scr = len(in_specs), len(out_specs), len(scratch_shapes)
    if hosted is None:
        res = pl.pallas_call(
            body, name=name, grid=grid, in_specs=list(in_specs), out_specs=list(out_specs),
            out_shape=tuple(out_shape), scratch_shapes=list(scratch_shapes), compiler_params=_params(*semantics),
        )(*args)
        return res, ()
    h_in, h_out = len(hosted.ins), len(hosted.out_shapes)

    def wrapped(*refs):
        a, hi = refs[:n_in], refs[n_in:n_in + h_in]
        o = refs[n_in + h_in:n_in + h_in + n_out]
        ho = refs[n_in + h_in + n_out:n_in + h_in + n_out + h_out]
        scr = refs[n_in + h_in + n_out + h_out:]
        send_sems, recv_sems = scr[n_scr], scr[n_scr + 1]
        first = functools.reduce(jnp.logical_and, [pl.program_id(k) == 0 for k in range(len(grid))])
        last = functools.reduce(jnp.logical_and, [pl.program_id(k) == grid[k] - 1 for k in range(len(grid))])

        @pl.when(first)
        def _():
            hosted.start(hi, ho, send_sems, recv_sems)

        if hosted.middle is not None:
            half_way = functools.reduce(jnp.logical_and, [pl.program_id(0) == grid[0] // 2] + [
                pl.program_id(k) == 0 for k in range(1, len(grid))])

            @pl.when(half_way)
            def _():
                hosted.middle(hi, ho, send_sems, recv_sems)

        body(*a, *o, *scr[:n_scr])

        @pl.when(last)
        def _():
            hosted.finish(hi, ho, send_sems, recv_sems)

    sem = pltpu.SemaphoreType.DMA(hosted.sem_shape)
    res = pl.pallas_call(
        wrapped, name=name, grid=grid, in_specs=list(in_specs) + [ANY] * h_in,
        out_specs=list(out_specs) + [ANY] * h_out, out_shape=tuple(out_shape) + hosted.out_shapes,
        scratch_shapes=list(scratch_shapes) + [sem, sem], compiler_params=_params(*(["arbitrary"] * len(grid))),
    )(*args, *hosted.ins)
    return res[:n_out], res[n_out:]


def _all_gather(shards, small=()):
    items = tuple(shards) + tuple(small)
    nw = len(shards)

    def place():
        x, y, c, me, _ = _place()
        first = (x + (1 - c) * (1 - 2 * x), y + c * (1 - 2 * y))
        second = (x + c * (1 - 2 * x), y + (1 - c) * (1 - 2 * y))
        diag = (1 - x, 1 - y)
        return x, y, c, me, (first, second, diag)

    def halves(w, c):
        rh = items[w].shape[0] // 2
        return pl.ds(c * rh, rh), pl.ds((1 - c) * rh, rh)

    def start(ins, outs, ss, rs):
        x, y, c, me, chips = place()
        for w in range(len(items)):
            _remote(ins[w], outs[w].at[me], ss.at[w, 6], rs.at[w, 6], (x, y, 1 - c)).start()
            if w < nw:
                mine, _ = halves(w, c)
                for k in range(2):
                    _remote(ins[w].at[mine], outs[w].at[me, mine], ss.at[w, k], rs.at[w, k], (*chips[k], c)).start()
            else:
                for k in range(3):
                    _remote(ins[w], outs[w].at[me], ss.at[w, k], rs.at[w, k], (*chips[k], c)).start()

    def onward(outs, ss, rs, w, k, x, y, c, chips):
        mine, _ = halves(w, c)
        pk = 2 * chips[k][0] + chips[k][1]
        got = outs[w].at[pk, mine]
        src = chips[1] if k == 2 else chips[k]
        _remote(got, got, ss.at[w, k], rs.at[w, k], (*src, c)).wait_recv()
        if k == 0:
            _remote(got, got, ss.at[w, 2], rs.at[w, 2], (*chips[1], c)).start()
        _remote(got, got, ss.at[w, 3 + k], rs.at[w, 3 + k], (x, y, 1 - c)).start()

    def middle(ins, outs, ss, rs):
        x, y, c, _, chips = place()
        for w in range(nw):
            onward(outs, ss, rs, w, 0, x, y, c, chips)

    def finish(ins, outs, ss, rs):
        x, y, c, me, chips = place()
        sib = (x, y, 1 - c)
        for k in (1, 2):
            for w in range(nw):
                onward(outs, ss, rs, w, k, x, y, c, chips)
        for w in range(len(items)):
            if w < nw:
                mine, theirs = halves(w, c)
                for k, chip in ((3, chips[1]), (4, chips[0]), (5, chips[2])):
                    oth = outs[w].at[2 * chip[0] + chip[1], theirs]
                    _remote(oth, oth, ss.at[w, k], rs.at[w, k], sib).wait_recv()
                own = ins[w].at[mine]
                for k in range(6):
                    _remote(own, own, ss.at[w, k], rs.at[w, k], sib).wait_send()
            else:
                for k in range(3):
                    got = outs[w].at[2 * chips[k][0] + chips[k][1]]
                    _remote(got, got, ss.at[w, k], rs.at[w, k], (*chips[k], c)).wait_recv()
                    _remote(ins[w], ins[w], ss.at[w, k], rs.at[w, k], sib).wait_send()
            _remote(ins[w], outs[w].at[me], ss.at[w, 6], rs.at[w, 6], sib).wait()

    out_shapes = tuple(jax.ShapeDtypeStruct((N_CHIPS,) + a.shape, a.dtype) for a in items)
    return _Exchange(items, out_shapes, (len(items), 7), start, finish, middle if nw else None)


def _exchange_halves(grads):
    nw = len(grads)

    def copies(ins, outs, ss, rs):
        x, y, c, _, _ = _place()
        res = []
        for w in range(nw):
            rh = grads[w].shape[1] // 2
            res.append(_remote(ins[w].at[:, pl.ds((1 - c) * rh, rh), :], outs[w], ss.at[w], rs.at[w], (x, y, 1 - c)))
        return res

    def start(ins, outs, ss, rs):
        for cp in copies(ins, outs, ss, rs):
            cp.start()

    def finish(ins, outs, ss, rs):
        for cp in copies(ins, outs, ss, rs):
            cp.wait()

    halves = tuple(jax.ShapeDtypeStruct((g.shape[0], g.shape[1] // 2, g.shape[2]), g.dtype) for g in grads)
    return _Exchange(grads, halves, (nw,), start, finish)


def _scatter_partials(parts):
    nw = len(parts)

    def copies(ins, outs, ss, rs):
        _, _, c, _, chips = _place()
        return [_remote(ins[w].at[2 * px + py], outs[w].at[r], ss.at[w, r], rs.at[w, r], (px, py, c))
                for r, (px, py) in enumerate(chips) for w in range(nw)]

    def start(ins, outs, ss, rs):
        for cp in copies(ins, outs, ss, rs):
            cp.start()

    def finish(ins, outs, ss, rs):
        for cp in copies(ins, outs, ss, rs):
            cp.wait()

    out_shapes = tuple(jax.ShapeDtypeStruct((3,) + p.shape[1:], p.dtype) for p in parts)
    return _Exchange(parts, out_shapes, (nw, 3), start, finish)


def _join_halves(shards):
    nw = len(shards)

    def start(ins, outs, ss, rs):
        x, y, c, _, _ = _place()
        for w in range(nw):
            rh = shards[w].shape[0] // 2
            mine = outs[w].at[pl.ds(c * rh, rh)]
            _remote(mine, mine, ss.at[w], rs.at[w], (x, y, 1 - c)).start()

    def finish(ins, outs, ss, rs):
        x, y, c, _, _ = _place()
        for w in range(nw):
            rh = shards[w].shape[0] // 2
            mine = outs[w].at[pl.ds(c * rh, rh)]
            theirs = outs[w].at[pl.ds((1 - c) * rh, rh)]
            _remote(mine, mine, ss.at[w], rs.at[w], (x, y, 1 - c)).wait_send()
            _remote(theirs, theirs, ss.at[w], rs.at[w], (x, y, 1 - c)).wait_recv()

    ex = _Exchange(shards, tuple(jax.ShapeDtypeStruct(s.shape, s.dtype) for s in shards), (nw,), start, finish)
    return _run_exchange(ex, "rs_join_halves", aliases={w: w for w in range(nw)})


def _gather_small(slab):
    def copies(ins, outs, ss, rs):
        x, y, c, _, _ = _place()
        me = 4 * x + 2 * y + c
        own = pltpu.make_async_copy(ins[0], outs[0].at[me], ss.at[7])
        out, arrivals = [], []
        for k in range(1, 8):
            px = 1 - x if k & 4 else x
            py = 1 - y if k & 2 else y
            pc = 1 - c if k & 1 else c
            out.append(_remote(ins[0], outs[0].at[me], ss.at[k - 1], rs.at[k - 1], (px, py, pc)))
            theirs = outs[0].at[4 * px + 2 * py + pc]
            arrivals.append(_remote(theirs, theirs, ss.at[k - 1], rs.at[k - 1], (px, py, pc)))
        return own, out, arrivals

    def start(ins, outs, ss, rs):
        own, out, _ = copies(ins, outs, ss, rs)
        own.start()
        for cp in out:
            cp.start()

    def finish(ins, outs, ss, rs):
        own, out, arrivals = copies(ins, outs, ss, rs)
        for cp in out:
            cp.wait_send()
        for cp in arrivals:
            cp.wait_recv()
        own.wait()

    return _Exchange((slab,), (jax.ShapeDtypeStruct((8,) + slab.shape, slab.dtype),), (8,), start, finish)


_SMALL_VECS = ("ln_mix_g", "ln_attn_g", "ln_mem_g", "ln_ffn_g", "ln_final_g")


def _pack_small(p, extra, conv):
    d = p["ln_mix_g"].shape[-1]
    top = [p[k].reshape(1, d) for k in _SMALL_VECS]
    top.append(jnp.concatenate([p["sgu_ln_g"].reshape(-1), p["sgu_ln_b"].reshape(-1)]).reshape(1, d))
    top.append(jnp.concatenate([p["grp_norm_a"].reshape(-1), p["grp_norm_b"].reshape(-1)]).reshape(1, d))
    top.append(jnp.concatenate([p["b_spatial"].reshape(-1), extra]).reshape(1, d))
    mid = jnp.zeros((8, d), F32)
    if conv is not None:
        mid = jnp.pad(conv, ((0, 5), (0, d - conv.shape[1])))
    return jnp.concatenate([jnp.concatenate(top, axis=0), mid, p["w_spatial"].reshape(-1, d)], axis=0)


def _unpack_small(slab):
    d = slab.shape[1]
    hw = d // 2
    out = {k: slab[i] for i, k in enumerate(_SMALL_VECS)}
    out["sgu_ln_g"], out["sgu_ln_b"] = slab[5, :hw], slab[5, hw:]
    out["grp_norm_a"], out["grp_norm_b"] = slab[6, :hw], slab[6, hw:]
    out["b_spatial"] = slab[7, :hw].reshape(HEADS, CHUNK)
    out["w_spatial"] = slab[16:].reshape(HEADS, CHUNK, CHUNK)
    return out


_BIG = ("w_in", "w_kv", "w_gate_up", "w_out", "w_q", "w_o", "w_down")
_WEIGHTS = ("ln_mix_g", "w_in", "sgu_ln_g", "sgu_ln_b", "w_spatial", "b_spatial", "conv_w", "grp_norm_a",
            "grp_norm_b", "w_out", "ln_attn_g", "ln_mem_g", "w_q", "w_kv", "w_o", "ln_ffn_g", "w_gate_up",
            "w_down", "ln_final_g")


def _step(p, m_, v_, x, mem, target):
    s, d = x.shape
    hw = d // 2
    row = lambda a: a.reshape(1, -1)
    x_, y_, c_ = lax.axis_index("x"), lax.axis_index("y"), lax.axis_index("c")
    chip = 2 * x_ + y_

    bf = {k: p[k].astype(BF16) for k in _BIG}
    conv8 = jnp.pad(p["conv_w"], ((0, 5), (0, 0)))
    w_in, w_out4, conv4 = _run_exchange(_all_gather([bf["w_in"], bf["w_out"]], [conv8]), "all_gather_mixer")
    cw = jnp.transpose(conv4[:, :3, :], (1, 0, 2)).reshape(3, hw)
    b_t = jnp.pad(jnp.transpose(p["b_spatial"]), ((0, 0), (0, CHUNK - HEADS)))
    g1, g2, gm, g3, gf = (row(p[k]) for k in _SMALL_VECS)
    lng, lnb, ga, gb = row(p["sgu_ln_g"]), row(p["sgu_ln_b"]), row(p["grp_norm_a"]), row(p["grp_norm_b"])
    wsp = p["w_spatial"]
    w_out = w_out4.reshape(-1, d)

    (h, x1, ycat, xn1), (w_kv, w_q4, w_o4, w_down4) = _mixer_fwd(
        x, g1, w_in, lng, lnb, wsp, b_t, cw, ga, gb, w_out,
        hosted=_all_gather([bf[k] for k in ("w_kv", "w_q", "w_o", "w_down")]))
    w_q, w_o, w_down = (a.reshape(-1, d) for a in (w_q4, w_o4, w_down4))
    memn, kv = _kv_fwd(mem, gm, w_kv)
    (x2, o, xn2), (w_gu,) = _attn_fwd(x1, g2, w_q, kv, w_o, hosted=_all_gather([bf["w_gate_up"]]))
    dx2, act, dgu, xn3, dx3, loss, dgf, dg3 = _ffn_fwd_bwd(x2, g3, gf, target, w_gu, w_down)

    place = jnp.stack([c_, chip]).astype(jnp.int32)

    def chip_partials(names, grads, tag):
        got = _run_exchange(_exchange_halves(grads), "rs_exchange_" + tag)
        return [_chip_sum(g, b, place, "chip_sum_" + k) for k, g, b in zip(names, grads, got)]

    names_d = ("w_down",)
    parts_d = chip_partials(names_d, (_weight_grad(act, dx3, "grad_w_down", 1408, 512, False)[0],), "down")
    g_gu, slots_d = _weight_grad(xn3, dgu, "grad_w_gate_up", 512, 1408, True, hosted=_scatter_partials(parts_d))
    names_a = ("w_gate_up", "w_o")
    parts_a = chip_partials(names_a, (g_gu, _weight_grad(o, dx2, "grad_w_o", 512, 512, False)[0]), "ffn")
    (dx1, dq, dkv, dg2), slots_a = _attn_bwd(x1, dx2, g2, w_q, kv, w_o, hosted=_scatter_partials(parts_a))
    g_kv, dgm = _kv_bwd(dkv, mem, memn, gm, w_kv)
    names_b = ("w_out", "w_q", "w_kv")
    parts_b = chip_partials(names_b, (_weight_grad(ycat, dx1, "grad_w_out", 512, 512, False)[0],
                                      _weight_grad(xn2, dq, "grad_w_q", 512, 512, False)[0], g_kv), "attn")
    (dx, dh, dg1, dlng, dlnb, dwsp, dbt, dcw, dga, dgb), slots_b = _mixer_bwd(
        x, dx1, h, g1, lng, lnb, wsp, b_t, cw, ga, gb, w_out, w_in, hosted=_scatter_partials(parts_b))
    small = {"ln_mix_g": dg1, "ln_attn_g": dg2, "ln_mem_g": dgm, "ln_ffn_g": dg3, "ln_final_g": dgf,
             "sgu_ln_g": dlng, "sgu_ln_b": dlnb, "grp_norm_a": dga, "grp_norm_b": dgb,
             "b_spatial": jnp.transpose(dbt[:, :HEADS]), "w_spatial": dwsp}
    loss_vec = jnp.pad(loss.reshape(1), (0, hw - 1))
    g_in, (parts,) = _weight_grad(xn1, dh, "grad_w_in", 512, 640, True,
                                  hosted=_gather_small(_pack_small(small, loss_vec, dcw)))
    names_c = ("w_in",)
    parts_c = chip_partials(names_c, (g_in,), "mixer")
    slots_c = _run_exchange(_scatter_partials(parts_c), "rs_scatter_mixer")
    names = names_d + names_a + names_b + names_c
    halves = [_final_sum(a, sl, place, "final_sum_" + k)
              for k, a, sl in zip(names, parts_d + parts_a + parts_b + parts_c,
                                  slots_d + slots_a + slots_b + tuple(slots_c))]
    big_g = dict(zip(names, _join_halves(halves)))

    zeros = jnp.zeros((hw,), F32)
    sg, sd, sm, sv = _small_sum_adamw(parts, _pack_small(p, zeros, None), _pack_small(m_, zeros, None),
                                      _pack_small(v_, zeros, None))
    out_g, out_d, out_m, out_v = (_unpack_small(a) for a in (sg, sd, sm, sv))
    loss_out = sg[7, hw]
    g_conv = lax.dynamic_slice(sg[8:11, :hw], (0, chip * (hw // N_CHIPS)), (3, hw // N_CHIPS))
    out_g["conv_w"] = g_conv
    out_d["conv_w"], out_m["conv_w"], out_v["conv_w"] = _adamw(p["conv_w"], g_conv, m_["conv_w"], v_["conv_w"],
                                                                "adamw_conv_w")
    for k in _BIG:
        out_g[k] = big_g[k]
        out_d[k], out_m[k], out_v[k] = _adamw(p[k], big_g[k], m_[k], v_[k], "adamw_" + k)
    return loss_out, dx, out_g, out_d, out_m, out_v


def kernel(x, mem, ln_mix_g, w_in, sgu_ln_g, sgu_ln_b, w_spatial, b_spatial, conv_w, grp_norm_a, grp_norm_b, w_out, ln_attn_g, ln_mem_g, w_q, w_kv, w_o, ln_ffn_g, w_gate_up, w_down, ln_final_g, loss_target, m_ln_mix_g, m_w_in, m_sgu_ln_g, m_sgu_ln_b, m_w_spatial, m_b_spatial, m_conv_w, m_grp_norm_a, m_grp_norm_b, m_w_out, m_ln_attn_g, m_ln_mem_g, m_w_q, m_w_kv, m_w_o, m_ln_ffn_g, m_w_gate_up, m_w_down, m_ln_final_g, v_ln_mix_g, v_w_in, v_sgu_ln_g, v_sgu_ln_b, v_w_spatial, v_b_spatial, v_conv_w, v_grp_norm_a, v_grp_norm_b, v_w_out, v_ln_attn_g, v_ln_mem_g, v_w_q, v_w_kv, v_w_o, v_ln_ffn_g, v_w_gate_up, v_w_down, v_ln_final_g):
    p = dict(ln_mix_g=ln_mix_g, w_in=w_in, sgu_ln_g=sgu_ln_g, sgu_ln_b=sgu_ln_b, w_spatial=w_spatial,
             b_spatial=b_spatial, conv_w=conv_w, grp_norm_a=grp_norm_a, grp_norm_b=grp_norm_b, w_out=w_out,
             ln_attn_g=ln_attn_g, ln_mem_g=ln_mem_g, w_q=w_q, w_kv=w_kv, w_o=w_o, ln_ffn_g=ln_ffn_g,
             w_gate_up=w_gate_up, w_down=w_down, ln_final_g=ln_final_g)
    m_ = dict(ln_mix_g=m_ln_mix_g, w_in=m_w_in, sgu_ln_g=m_sgu_ln_g, sgu_ln_b=m_sgu_ln_b, w_spatial=m_w_spatial,
              b_spatial=m_b_spatial, conv_w=m_conv_w, grp_norm_a=m_grp_norm_a, grp_norm_b=m_grp_norm_b,
              w_out=m_w_out, ln_attn_g=m_ln_attn_g, ln_mem_g=m_ln_mem_g, w_q=m_w_q, w_kv=m_w_kv, w_o=m_w_o,
              ln_ffn_g=m_ln_ffn_g, w_gate_up=m_w_gate_up, w_down=m_w_down, ln_final_g=m_ln_final_g)
    v_ = dict(ln_mix_g=v_ln_mix_g, w_in=v_w_in, sgu_ln_g=v_sgu_ln_g, sgu_ln_b=v_sgu_ln_b, w_spatial=v_w_spatial,
              b_spatial=v_b_spatial, conv_w=v_conv_w, grp_norm_a=v_grp_norm_a, grp_norm_b=v_grp_norm_b,
              w_out=v_w_out, ln_attn_g=v_ln_attn_g, ln_mem_g=v_ln_mem_g, w_q=v_w_q, w_kv=v_w_kv, w_o=v_w_o,
              ln_ffn_g=v_ln_ffn_g, w_gate_up=v_w_gate_up, w_down=v_w_down, ln_final_g=v_ln_final_g)
    s, d = x.shape[-2], x.shape[-1]
    loss, dx, g, dl, nm, nv = _step(p, m_, v_, x.reshape(s, d), mem.reshape(-1, d), loss_target.reshape(s, d))
    outs = [loss, dx.reshape(x.shape)]
    for tree in (g, dl, nm, nv):
        outs += [tree[k].reshape(p[k].shape) for k in _WEIGHTS]
    return tuple(outs)
```

```python
import functools
import math

import jax
import jax.numpy as jnp
from jax import lax
from jax.experimental import pallas as pl
from jax.experimental.pallas import tpu as pltpu

F32 = jnp.float32
BF16 = jnp.bfloat16
EPS = 1e-6
CHUNK = 128
HEADS = 4
N_CHIPS = 4
TM = 256
ADAM_LR, ADAM_B1, ADAM_B2, ADAM_EPS, ADAM_WD, ADAM_STEP = 0.001, 0.9, 0.999, 1e-08, 0.01, 10
GELU_C = math.sqrt(2.0 / math.pi)
GELU_K = 0.044715
SMALL_ROWS = 80
VMEM_LIMIT = 56 * 1024 * 1024
MIDDLE_STEP_16THS = 5
MESH = pl.DeviceIdType.MESH
ANY = pl.BlockSpec(memory_space=pl.ANY)


def _params(*sem):
    return pltpu.CompilerParams(dimension_semantics=sem, vmem_limit_bytes=VMEM_LIMIT)


def _dot(a, b):
    return jnp.dot(a, b, preferred_element_type=F32)


def _dot_nt(a, b):
    return lax.dot_general(a, b, (((1,), (1,)), ((), ())), preferred_element_type=F32)


def _dot_tn(a, b):
    return lax.dot_general(a, b, (((0,), (0,)), ((), ())), preferred_element_type=F32)


def _rms_fwd(x, g):
    r = lax.rsqrt(jnp.mean(x * x, axis=-1, keepdims=True) + EPS)
    xh = x * r
    return xh * g, xh, r


def _rms_bwd(dy, xh, r, g):
    dxh = dy * g
    dx = r * (dxh - xh * jnp.mean(dxh * xh, axis=-1, keepdims=True))
    return dx, jnp.sum(dy * xh, axis=0, keepdims=True)


def _full(shape):
    nd = len(shape)
    return pl.BlockSpec(shape, lambda *_: (0,) * nd, pipeline_mode=pl.Buffered(1))


def _acc(shape):
    nd = len(shape)
    return pl.BlockSpec(shape, lambda *_: (0,) * nd)


def _rows(tm, cols):
    return pl.BlockSpec((tm, cols), lambda i: (i, 0))


def _tril_weights(wsp_ref):
    row = lax.broadcasted_iota(jnp.int32, (CHUNK, CHUNK), 0)
    col = lax.broadcasted_iota(jnp.int32, (CHUNK, CHUNK), 1)
    return [jnp.where(row >= col, wsp_ref[hd], 0.0).astype(BF16) for hd in range(HEADS)]


def _shift_rows(z, zp):
    row = lax.broadcasted_iota(jnp.int32, z.shape, 0)
    zm1 = jnp.where(row == 0, zp[7:8, :], pltpu.roll(z, 1, 0))
    zm2 = jnp.where(row == 0, zp[6:7, :], jnp.where(row == 1, zp[7:8, :], pltpu.roll(z, 2, 0)))
    return zm1, zm2


def _gelu_parts(x):
    t = jnp.tanh(GELU_C * (x + GELU_K * (x * x * x)))
    return 0.5 * x * (1.0 + t), t


def _layer_norm_parts(v, g, b):
    mu = jnp.mean(v, axis=-1, keepdims=True)
    vc = v - mu
    rs = lax.rsqrt(jnp.mean(vc * vc, axis=-1, keepdims=True) + EPS)
    vhat = vc * rs
    return vhat * g + b, vhat, rs


def _kv_fwd(mem, g_mem, w_kv):
    m, d = mem.shape
    ns = w_kv.shape[2]

    def body(mem_ref, g_ref, w_ref, memn_ref, kv_ref):
        y, _, _ = _rms_fwd(mem_ref[...], g_ref[...])
        yb = y.astype(BF16)
        memn_ref[...] = yb
        for k in range(N_CHIPS):
            kv_ref[:, k * ns:(k + 1) * ns] = _dot(yb, w_ref[k]).astype(BF16)

    return pl.pallas_call(
        body, name="kv_fwd",
        out_shape=(jax.ShapeDtypeStruct((m, d), BF16), jax.ShapeDtypeStruct((m, N_CHIPS * ns), BF16)),
        compiler_params=pltpu.CompilerParams(vmem_limit_bytes=VMEM_LIMIT),
    )(mem, g_mem, w_kv)


def _mixer_fwd(x, g1, w_in, lng, lnb, wsp, b_t, cw, ga, gb, w_out, hosted=None):
    s, d = x.shape
    n = s // TM
    nch = TM // CHUNK
    ns = w_in.shape[2]
    nh = N_CHIPS * ns
    aw = d // 2
    hd_w = aw // HEADS

    def body(x_ref, g1_ref, win_ref, lng_ref, lnb_ref, wsp_ref, bt_ref, cw_ref, ga_ref, gb_ref, wout_ref,
             h_ref, x1_ref, y_ref, xn_ref, mix_ref, zp_ref):
        i = pl.program_id(0)

        @pl.when(i == 0)
        def _():
            zp_ref[...] = jnp.zeros_like(zp_ref)

        x = x_ref[...]
        xn, _, _ = _rms_fwd(x, g1_ref[...])
        xnb = xn.astype(BF16)
        xn_ref[...] = xnb
        for k in range(N_CHIPS):
            h_ref[:, k * ns:(k + 1) * ns] = _dot(xnb, win_ref[k])
        a, _ = _gelu_parts(h_ref[:, 0:2 * aw])
        u = a[:, :aw]
        vn, _, _ = _layer_norm_parts(a[:, aw:], lng_ref[...], lnb_ref[...])
        vnb = vn.astype(BF16)
        wm = _tril_weights(wsp_ref)
        for c in range(nch):
            for hd in range(HEADS):
                blk = vnb[c * CHUNK:(c + 1) * CHUNK, hd * hd_w:(hd + 1) * hd_w]
                mix_ref[c * CHUNK:(c + 1) * CHUNK, hd * hd_w:(hd + 1) * hd_w] = _dot(wm[hd], blk) + bt_ref[:, hd:hd + 1]
        ya, _, _ = _rms_fwd(u * mix_ref[...], ga_ref[...])
        g_b = h_ref[:, 2 * aw:3 * aw]
        z = h_ref[:, 3 * aw:4 * aw] * h_ref[:, 4 * aw:5 * aw]
        zm1, zm2 = _shift_rows(z, zp_ref[...])
        conv = cw_ref[0:1, :] * zm2 + cw_ref[1:2, :] * zm1 + cw_ref[2:3, :] * z
        yb, _, _ = _rms_fwd(g_b * conv, gb_ref[...])
        zp_ref[...] = z[TM - 8:TM, :]
        ycat = jnp.concatenate([ya, yb], axis=-1).astype(BF16)
        y_ref[...] = ycat
        x1_ref[...] = x + _dot(ycat, wout_ref[...])

    return _host_call(
        body, "mixer_fwd", (n,),
        [_rows(TM, d), _full(g1.shape), _full(w_in.shape), _full(lng.shape), _full(lnb.shape),
         _full(wsp.shape), _full(b_t.shape), _full(cw.shape), _full(ga.shape), _full(gb.shape),
         _full(w_out.shape)],
        [_rows(TM, nh), _rows(TM, d), _rows(TM, d), _rows(TM, d)],
        (jax.ShapeDtypeStruct((s, nh), F32), jax.ShapeDtypeStruct((s, d), F32),
         jax.ShapeDtypeStruct((s, d), BF16), jax.ShapeDtypeStruct((s, d), BF16)),
        [pltpu.VMEM((TM, aw), F32), pltpu.VMEM((8, aw), F32)],
        (x, g1, w_in, lng, lnb, wsp, b_t, cw, ga, gb, w_out), ("arbitrary",), hosted)


def _attn_probs(qb, kv_ref, hd, dh, d, scale):
    kh = kv_ref[:, hd * dh:(hd + 1) * dh]
    vh = kv_ref[:, d + hd * dh:d + (hd + 1) * dh]
    sc = _dot_nt(qb, kh) * scale
    sc = sc - jnp.max(sc, axis=-1, keepdims=True)
    e = jnp.exp(sc)
    return e / jnp.sum(e, axis=-1, keepdims=True), kh, vh


def _attn_fwd(x1, g2, w_q, kv, w_o, hosted=None):
    s, d = x1.shape
    n = s // TM
    dh = d // HEADS
    scale = dh ** -0.5

    def body(x1_ref, g2_ref, wq_ref, kv_ref, wo_ref, x2_ref, o_ref, xn_ref):
        x1v = x1_ref[...]
        xn, _, _ = _rms_fwd(x1v, g2_ref[...])
        xnb = xn.astype(BF16)
        xn_ref[...] = xnb
        q = _dot(xnb, wq_ref[...])
        for hd in range(HEADS):
            qb = q[:, hd * dh:(hd + 1) * dh].astype(BF16)
            p, _, vh = _attn_probs(qb, kv_ref, hd, dh, d, scale)
            o_ref[:, hd * dh:(hd + 1) * dh] = _dot(p.astype(BF16), vh).astype(BF16)
        x2_ref[...] = x1v + _dot(o_ref[...], wo_ref[...])

    return _host_call(
        body, "attn_fwd", (n,),
        [_rows(TM, d), _full(g2.shape), _full(w_q.shape), _full(kv.shape), _full(w_o.shape)],
        [_rows(TM, d), _rows(TM, d), _rows(TM, d)],
        (jax.ShapeDtypeStruct((s, d), F32), jax.ShapeDtypeStruct((s, d), BF16), jax.ShapeDtypeStruct((s, d), BF16)),
        [], (x1, g2, w_q, kv, w_o), ("parallel",), hosted)


def _ffn_fwd_bwd(x2, g3, gf, target, w_gu, w_down):
    s, d = x2.shape
    n = s // TM
    ns = w_gu.shape[2]
    ff = 2 * ns

    def body(x2_ref, g3_ref, gf_ref, t_ref, wgu_ref, wd_ref,
             dx2_ref, act_ref, dgu_ref, xn_ref, dx3_ref, loss_ref, dgf_ref, dg3_ref, dx2b_ref):
        i = pl.program_id(0)

        @pl.when(i == 0)
        def _():
            loss_ref[...] = jnp.zeros_like(loss_ref)
            dgf_ref[...] = jnp.zeros_like(dgf_ref)
            dg3_ref[...] = jnp.zeros_like(dg3_ref)

        x2v = x2_ref[...]
        xn, xh3, r3 = _rms_fwd(x2v, g3_ref[...])
        xnb = xn.astype(BF16)
        xn_ref[...] = xnb
        x3 = x2v
        saved = []
        for j in range(2):
            g = _dot(xnb, wgu_ref[j])
            u = _dot(xnb, wgu_ref[2 + j])
            sg = 1.0 / (1.0 + jnp.exp(-g))
            sl = g * sg
            actb = (sl * u).astype(BF16)
            act_ref[:, j * ns:(j + 1) * ns] = actb
            x3 = x3 + _dot(actb, wd_ref[j * ns:(j + 1) * ns, :])
            saved.append((u, sl, sg * (1.0 + g * (1.0 - sg))))
        gfv = gf_ref[...]
        y, xhf, rf = _rms_fwd(x3, gfv)
        e = y - t_ref[...]
        loss_ref[...] += 0.5 * jnp.sum(jnp.sum(e * e, axis=-1, keepdims=True), axis=0, keepdims=True) / d
        dx3, dgf = _rms_bwd(e / d, xhf, rf, gfv)
        dgf_ref[...] += dgf
        dx3b = dx3.astype(BF16)
        dx3_ref[...] = dx3b
        dxn = jnp.zeros_like(x2v)
        for j in range(2):
            u, sl, dsl = saved[j]
            dact = _dot_nt(dx3b, wd_ref[j * ns:(j + 1) * ns, :])
            dgb = (dact * u * dsl).astype(BF16)
            dub = (dact * sl).astype(BF16)
            dgu_ref[:, j * ns:(j + 1) * ns] = dgb
            dgu_ref[:, ff + j * ns:ff + (j + 1) * ns] = dub
            dxn = dxn + _dot_nt(dgb, wgu_ref[j]) + _dot_nt(dub, wgu_ref[2 + j])
        dxr, dg3 = _rms_bwd(dxn, xh3, r3, g3_ref[...])
        dg3_ref[...] += dg3
        dx2 = dx3 + dxr
        dx2_ref[...] = dx2
        dx2b_ref[...] = dx2.astype(BF16)

    vec = jax.ShapeDtypeStruct((1, d), F32)
    return pl.pallas_call(
        body, name="ffn_fwd_bwd", grid=(n,),
        in_specs=[_rows(TM, d), _full(g3.shape), _full(gf.shape), _rows(TM, d), _full(w_gu.shape),
                  _full(w_down.shape)],
        out_specs=[_rows(TM, d), _rows(TM, ff), _rows(TM, 2 * ff), _rows(TM, d), _rows(TM, d),
                   _acc((1, 1)), _acc((1, d)), _acc((1, d)), _rows(TM, d)],
        out_shape=(jax.ShapeDtypeStruct((s, d), F32), jax.ShapeDtypeStruct((s, ff), BF16),
                   jax.ShapeDtypeStruct((s, 2 * ff), BF16), jax.ShapeDtypeStruct((s, d), BF16),
                   jax.ShapeDtypeStruct((s, d), BF16), jax.ShapeDtypeStruct((1, 1), F32), vec, vec,
                   jax.ShapeDtypeStruct((s, d), BF16)),
        compiler_params=_params("arbitrary"),
    )(x2, g3, gf, target, w_gu, w_down)


def _attn_bwd(x1, dx2, g2, w_q, kv, w_o, hosted=None):
    s, d = x1.shape
    n = s // TM
    dh = d // HEADS
    scale = dh ** -0.5
    m = kv.shape[0]

    def body(x1_ref, dx2_ref, g2_ref, wq_ref, kv_ref, wo_ref, dx1_ref, dq_ref, dkv_ref, dg2_ref, dx1b_ref):
        i = pl.program_id(0)

        @pl.when(i == 0)
        def _():
            dkv_ref[...] = jnp.zeros_like(dkv_ref)
            dg2_ref[...] = jnp.zeros_like(dg2_ref)

        xn, xh2, r2 = _rms_fwd(x1_ref[...], g2_ref[...])
        q = _dot(xn.astype(BF16), wq_ref[...])
        dx2v = dx2_ref[...]
        do = _dot_nt(dx2v.astype(BF16), wo_ref[...])
        for hd in range(HEADS):
            qb = q[:, hd * dh:(hd + 1) * dh].astype(BF16)
            p, kh, vh = _attn_probs(qb, kv_ref, hd, dh, d, scale)
            dob = do[:, hd * dh:(hd + 1) * dh].astype(BF16)
            dp = _dot_nt(dob, vh)
            ds = p * (dp - jnp.sum(dp * p, axis=-1, keepdims=True))
            dsb = (ds * scale).astype(BF16)
            dq_ref[:, hd * dh:(hd + 1) * dh] = _dot(dsb, kh).astype(BF16)
            dkv_ref[:, hd * dh:(hd + 1) * dh] += _dot_tn(dsb, qb)
            dkv_ref[:, d + hd * dh:d + (hd + 1) * dh] += _dot_tn(p.astype(BF16), dob)
        dxn = _dot_nt(dq_ref[...], wq_ref[...])
        dxr, dg2 = _rms_bwd(dxn, xh2, r2, g2_ref[...])
        dg2_ref[...] += dg2
        dx1 = dx2v + dxr
        dx1_ref[...] = dx1
        dx1b_ref[...] = dx1.astype(BF16)

    return _host_call(
        body, "attn_bwd", (n,),
        [_rows(TM, d), _rows(TM, d), _full(g2.shape), _full(w_q.shape), _full(kv.shape), _full(w_o.shape)],
        [_rows(TM, d), _rows(TM, d), _acc((m, 2 * d)), _acc((1, d)), _rows(TM, d)],
        (jax.ShapeDtypeStruct((s, d), F32), jax.ShapeDtypeStruct((s, d), BF16),
         jax.ShapeDtypeStruct((m, 2 * d), F32), jax.ShapeDtypeStruct((1, d), F32),
         jax.ShapeDtypeStruct((s, d), BF16)),
        [], (x1, dx2, g2, w_q, kv, w_o), ("arbitrary",), hosted)


def _kv_bwd(dkv, mem, memn, g_mem, w_kv):
    m, d = mem.shape
    ns = w_kv.shape[2]

    def body(dkv_ref, mem_ref, memn_ref, g_ref, w_ref, gw_ref, dg_ref):
        _, xh, _ = _rms_fwd(mem_ref[...], g_ref[...])
        dmemn = jnp.zeros((m, d), F32)
        for k in range(N_CHIPS):
            dkb = dkv_ref[:, k * ns:(k + 1) * ns].astype(BF16)
            gw_ref[k] = _dot_tn(memn_ref[...], dkb).astype(BF16)
            dmemn = dmemn + _dot_nt(dkb, w_ref[k])
        dg_ref[...] = jnp.sum(dmemn * xh, axis=0, keepdims=True)

    return pl.pallas_call(
        body, name="kv_bwd",
        out_shape=(jax.ShapeDtypeStruct((N_CHIPS, d, ns), BF16), jax.ShapeDtypeStruct((1, d), F32)),
        compiler_params=pltpu.CompilerParams(vmem_limit_bytes=VMEM_LIMIT),
    )(dkv, mem, memn, g_mem, w_kv)


def _mixer_bwd(x, dx1, h, g1, lng, lnb, wsp, b_t, cw, ga, gb, w_out, w_in, hosted=None):
    s, d = x.shape
    n = s // TM
    nch = TM // CHUNK
    ns = w_in.shape[2]
    nh = N_CHIPS * ns
    aw = d // 2
    hd_w = aw // HEADS

    def rev(cols):
        return pl.BlockSpec((TM, cols), lambda i: (n - 1 - i, 0))

    hprev = pl.BlockSpec((8, nh), lambda i: (jnp.maximum((n - 1 - i) * (TM // 8) - 1, 0), 0))

    def body(x_ref, dx1_ref, h_ref, hp_ref, g1_ref, lng_ref, lnb_ref, wsp_ref, bt_ref, cw_ref, ga_ref, gb_ref,
             wout_ref, win_ref,
             dx_ref, dh_ref, dg1_ref, dlng_ref, dlnb_ref, dwsp_ref, dbt_ref, dcw_ref, dga_ref, dgb_ref,
             mix_ref, dvn_ref, dcn_ref):
        i = pl.program_id(0)

        @pl.when(i == 0)
        def _():
            for r in (dg1_ref, dlng_ref, dlnb_ref, dwsp_ref, dbt_ref, dcw_ref, dga_ref, dgb_ref, dcn_ref):
                r[...] = jnp.zeros_like(r)

        dx1v = dx1_ref[...]
        dycat = _dot_nt(dx1v.astype(BF16), wout_ref[...])
        ha = h_ref[:, 0:2 * aw]
        a, th = _gelu_parts(ha)
        u = a[:, :aw]
        lngv = lng_ref[...]
        vn, vhat, rs = _layer_norm_parts(a[:, aw:], lngv, lnb_ref[...])
        vnb = vn.astype(BF16)
        wm = _tril_weights(wsp_ref)
        for c in range(nch):
            for hd in range(HEADS):
                blk = vnb[c * CHUNK:(c + 1) * CHUNK, hd * hd_w:(hd + 1) * hd_w]
                mix_ref[c * CHUNK:(c + 1) * CHUNK, hd * hd_w:(hd + 1) * hd_w] = _dot(wm[hd], blk) + bt_ref[:, hd:hd + 1]
        mixed = mix_ref[...]
        gav = ga_ref[...]
        _, yah, ra = _rms_fwd(u * mixed, gav)
        dya, dga = _rms_bwd(dycat[:, :aw], yah, ra, gav)
        dga_ref[...] += dga
        du = dya * mixed
        dmix = dya * u
        dmb = dmix.astype(BF16)
        tri = lax.broadcasted_iota(jnp.int32, (CHUNK, CHUNK), 0) >= lax.broadcasted_iota(jnp.int32, (CHUNK, CHUNK), 1)
        for hd in range(HEADS):
            dw = jnp.zeros((CHUNK, CHUNK), F32)
            db = jnp.zeros((CHUNK, 1), F32)
            for c in range(nch):
                rows = slice(c * CHUNK, (c + 1) * CHUNK)
                cols = slice(hd * hd_w, (hd + 1) * hd_w)
                dvn_ref[rows, cols] = _dot_tn(wm[hd], dmb[rows, cols])
                dw = dw + _dot_nt(dmb[rows, cols], vnb[rows, cols])
                db = db + jnp.sum(dmix[rows, cols], axis=1, keepdims=True)
            dwsp_ref[hd] += jnp.where(tri, dw, 0.0)
            dbt_ref[:, hd:hd + 1] += db
        dvn = dvn_ref[...]
        dlng_ref[...] += jnp.sum(dvn * vhat, axis=0, keepdims=True)
        dlnb_ref[...] += jnp.sum(dvn, axis=0, keepdims=True)
        dvh = dvn * lngv
        dv = rs * (dvh - jnp.mean(dvh, axis=-1, keepdims=True) - vhat * jnp.mean(dvh * vhat, axis=-1, keepdims=True))
        gprime = 0.5 * (1.0 + th) + 0.5 * ha * (1.0 - th * th) * (GELU_C * (1.0 + 3.0 * GELU_K * (ha * ha)))
        dh_ref[:, 0:2 * aw] = (jnp.concatenate([du, dv], axis=-1) * gprime).astype(BF16)
        g_b = h_ref[:, 2 * aw:3 * aw]
        g_c = h_ref[:, 3 * aw:4 * aw]
        val = h_ref[:, 4 * aw:5 * aw]
        z = g_c * val
        zp = jnp.where(i == n - 1, 0.0, hp_ref[:, 3 * aw:4 * aw] * hp_ref[:, 4 * aw:5 * aw])
        zm1, zm2 = _shift_rows(z, zp)
        cw0, cw1, cw2 = cw_ref[0:1, :], cw_ref[1:2, :], cw_ref[2:3, :]
        conv = cw0 * zm2 + cw1 * zm1 + cw2 * z
        gbv = gb_ref[...]
        _, ybh, rb = _rms_fwd(g_b * conv, gbv)
        dyb, dgb = _rms_bwd(dycat[:, aw:], ybh, rb, gbv)
        dgb_ref[...] += dgb
        dconv = dyb * g_b
        dcw_ref[0:1, :] += jnp.sum(dconv * zm2, axis=0, keepdims=True)
        dcw_ref[1:2, :] += jnp.sum(dconv * zm1, axis=0, keepdims=True)
        dcw_ref[2:3, :] += jnp.sum(dconv * z, axis=0, keepdims=True)
        nxt = dcn_ref[...]
        row = lax.broadcasted_iota(jnp.int32, dconv.shape, 0)
        dcp1 = jnp.where(row == TM - 1, nxt[0:1, :], pltpu.roll(dconv, TM - 1, 0))
        dcp2 = jnp.where(row == TM - 1, nxt[1:2, :],
                         jnp.where(row == TM - 2, nxt[0:1, :], pltpu.roll(dconv, TM - 2, 0)))
        dz = cw2 * dconv + cw1 * dcp1 + cw0 * dcp2
        dcn_ref[...] = dconv[0:8, :]
        dh_ref[:, 2 * aw:3 * aw] = (dyb * conv).astype(BF16)
        dh_ref[:, 3 * aw:4 * aw] = (dz * val).astype(BF16)
        dh_ref[:, 4 * aw:5 * aw] = (dz * g_c).astype(BF16)
        dxn = jnp.zeros((TM, d), F32)
        for k in range(N_CHIPS):
            dxn = dxn + _dot_nt(dh_ref[:, k * ns:(k + 1) * ns], win_ref[k])
        g1v = g1_ref[...]
        _, xh1, r1 = _rms_fwd(x_ref[...], g1v)
        dxr, dg1 = _rms_bwd(dxn, xh1, r1, g1v)
        dg1_ref[...] += dg1
        dx_ref[...] = dx1v + dxr

    ins = (x, dx1, h, h, g1, lng, lnb, wsp, b_t, cw, ga, gb, w_out, w_in)
    acc_shapes = [(1, d), (1, aw), (1, aw), wsp.shape, (CHUNK, CHUNK), cw.shape, (1, aw), (1, aw)]
    return _host_call(
        body, "mixer_bwd", (n,),
        [rev(d), rev(d), rev(nh), hprev] + [_full(a.shape) for a in ins[4:]],
        [rev(d), rev(nh)] + [_acc(sh) for sh in acc_shapes],
        (jax.ShapeDtypeStruct((s, d), F32), jax.ShapeDtypeStruct((s, nh), BF16))
        + tuple(jax.ShapeDtypeStruct(sh, F32) for sh in acc_shapes),
        [pltpu.VMEM((TM, aw), F32), pltpu.VMEM((TM, aw), F32), pltpu.VMEM((8, aw), F32)],
        ins, ("arbitrary",), hosted)


def _weight_grad(a, b, name, tm, tn, col_sharded, hosted=None):
    t, m = a.shape
    n = b.shape[1]

    def body(a_ref, b_ref, o_ref):
        o_ref[...] = _dot_tn(a_ref[...].astype(BF16), b_ref[...].astype(BF16)).astype(BF16)

    if col_sharded:
        ns = n // N_CHIPS
        per = ns // tn
        out_shape = jax.ShapeDtypeStruct((N_CHIPS, m, ns), BF16)
        out_spec = pl.BlockSpec((None, tm, tn), lambda i, j: (j // per, i, j % per))
    else:
        out_shape = jax.ShapeDtypeStruct((m, n), BF16)
        out_spec = pl.BlockSpec((tm, tn), lambda i, j: (i, j))
    (out,), extra = _host_call(
        body, name, (m // tm, n // tn),
        [pl.BlockSpec((t, tm), lambda i, j: (0, i)), pl.BlockSpec((t, tn), lambda i, j: (0, j))],
        [out_spec], (out_shape,), [], (a, b), ("parallel", "parallel"), hosted)
    return (out if col_sharded else out.reshape(N_CHIPS, m // N_CHIPS, n)), extra


def _row_tile(rows, cap=256):
    best = None
    for t in range(16, min(rows, cap) + 1, 16):
        if rows % t == 0:
            best = t
    return best if best is not None else rows


def _chip_sum(grad, got, place, name):
    nb, rh, c = got.shape
    tr = _row_tile(rh)
    per = rh // tr

    def body(place_ref, a_ref, b_ref, o_ref):
        o_ref[...] = (a_ref[...].astype(F32) + b_ref[...].astype(F32)).astype(BF16)

    blk = pl.BlockSpec((None, tr, c), lambda k, i, pref: (k, i, 0))
    return pl.pallas_call(
        body, name=name,
        grid_spec=pltpu.PrefetchScalarGridSpec(
            num_scalar_prefetch=1, grid=(nb, per),
            in_specs=[pl.BlockSpec((None, tr, c), lambda k, i, pref: (k, pref[0] * per + i, 0)), blk],
            out_specs=blk),
        out_shape=jax.ShapeDtypeStruct(got.shape, BF16),
        compiler_params=_params("parallel", "parallel"),
    )(place, grad, got)


def _final_sum(part, slots, place, name):
    _, rh, c = slots.shape
    tr = _row_tile(rh)
    per = rh // tr

    def body(place_ref, own_ref, s0_ref, s1_ref, s2_ref, o_ref):
        acc = own_ref[...].astype(F32) + s0_ref[...].astype(F32)
        o_ref[...] = (acc + s1_ref[...].astype(F32)) + s2_ref[...].astype(F32)

    def slot(r):
        return pl.BlockSpec((None, tr, c), lambda i, pref: (r, i, 0))

    return pl.pallas_call(
        body, name=name,
        grid_spec=pltpu.PrefetchScalarGridSpec(
            num_scalar_prefetch=1, grid=(per,),
            in_specs=[pl.BlockSpec((None, tr, c), lambda i, pref: (pref[1], i, 0)), slot(0), slot(1), slot(2)],
            out_specs=pl.BlockSpec((tr, c), lambda i, pref: (pref[0] * per + i, 0))),
        out_shape=jax.ShapeDtypeStruct((2 * rh, c), F32),
        compiler_params=_params("parallel"),
    )(place, part, slots, slots, slots)


def _adamw_math(w, g, m, v):
    m2 = ADAM_B1 * m + (1.0 - ADAM_B1) * g
    v2 = ADAM_B2 * v + (1.0 - ADAM_B2) * (g * g)
    m_hat = m2 / (1.0 - ADAM_B1 ** ADAM_STEP)
    v_hat = v2 / (1.0 - ADAM_B2 ** ADAM_STEP)
    delta = -ADAM_LR * (m_hat / (jnp.sqrt(v_hat) + ADAM_EPS) + ADAM_WD * w)
    return delta, m2, v2


def _adamw(w, g, m, v, name):
    r, c = w.shape
    tr = _row_tile(r) if r >= 16 else r

    def body(w_ref, g_ref, m_ref, v_ref, d_ref, m2_ref, v2_ref):
        d_ref[...], m2_ref[...], v2_ref[...] = _adamw_math(w_ref[...], g_ref[...], m_ref[...], v_ref[...])

    sh = jax.ShapeDtypeStruct((r, c), F32)
    return pl.pallas_call(
        body, name=name, grid=(r // tr,),
        in_specs=[_rows(tr, c)] * 4, out_specs=[_rows(tr, c)] * 3, out_shape=(sh, sh, sh),
        compiler_params=_params("parallel"),
    )(w, g, m, v)


def _small_sum_adamw(parts, w, m, v):
    nd, r, c = parts.shape

    def body(p_ref, w_ref, m_ref, v_ref, g_ref, d_ref, m2_ref, v2_ref):
        g = p_ref[0]
        for k in range(1, nd):
            g = g + p_ref[k]
        g_ref[...] = g
        d_ref[...], m2_ref[...], v2_ref[...] = _adamw_math(w_ref[...], g, m_ref[...], v_ref[...])

    sh = jax.ShapeDtypeStruct((r, c), F32)
    return pl.pallas_call(
        body, name="small_sum_adamw", out_shape=(sh, sh, sh, sh),
        compiler_params=pltpu.CompilerParams(vmem_limit_bytes=VMEM_LIMIT),
    )(parts, w, m, v)


def _place():
    x, y, c = lax.axis_index("x"), lax.axis_index("y"), lax.axis_index("c")
    chips = [(1 - x, y), (x, 1 - y), (1 - x, 1 - y)]
    return x, y, c, 2 * x + y, chips


def _remote(src, dst, send_sem, recv_sem, to):
    return pltpu.make_async_remote_copy(src_ref=src, dst_ref=dst, send_sem=send_sem, recv_sem=recv_sem,
                                        device_id=to, device_id_type=MESH)


class _Exchange:
    def __init__(self, ins, out_shapes, sem_shape, start, finish, middle=None):
        self.ins, self.out_shapes, self.sem_shape = tuple(ins), tuple(out_shapes), sem_shape
        self.start, self.finish, self.middle = start, finish, middle


def _run_exchange(ex, name, aliases=None):
    n_in, n_out = len(ex.ins), len(ex.out_shapes)

    def body(*refs):
        ins, outs = refs[:n_in], refs[n_in:n_in + n_out]
        send_sems, recv_sems = refs[n_in + n_out:]
        ex.start(ins, outs, send_sems, recv_sems)
        if ex.middle is not None:
            ex.middle(ins, outs, send_sems, recv_sems)
        ex.finish(ins, outs, send_sems, recv_sems)

    sem = pltpu.SemaphoreType.DMA(ex.sem_shape)
    return pl.pallas_call(
        body, name=name, out_shape=ex.out_shapes, in_specs=[ANY] * n_in, out_specs=[ANY] * n_out,
        input_output_aliases=aliases or {}, scratch_shapes=[sem, sem],
    )(*ex.ins)


def _host_call(body, name, grid, in_specs, out_specs, out_shape, scratch_shapes, args, semantics, hosted):
    n_in, n_out, n_scr = len(in_specs), len(out_specs), len(scratch_shapes)
    if hosted is None:
        res = pl.pallas_call(
            body, name=name, grid=grid, in_specs=list(in_specs), out_specs=list(out_specs),
            out_shape=tuple(out_shape), scratch_shapes=list(scratch_shapes), compiler_params=_params(*semantics),
        )(*args)
        return res, ()
    h_in, h_out = len(hosted.ins), len(hosted.out_shapes)

    def wrapped(*refs):
        a, hi = refs[:n_in], refs[n_in:n_in + h_in]
        o = refs[n_in + h_in:n_in + h_in + n_out]
        ho = refs[n_in + h_in + n_out:n_in + h_in + n_out + h_out]
        scr = refs[n_in + h_in + n_out + h_out:]
        send_sems, recv_sems = scr[n_scr], scr[n_scr + 1]
        first = functools.reduce(jnp.logical_and, [pl.program_id(k) == 0 for k in range(len(grid))])
        last = functools.reduce(jnp.logical_and, [pl.program_id(k) == grid[k] - 1 for k in range(len(grid))])

        @pl.when(first)
        def _():
            hosted.start(hi, ho, send_sems, recv_sems)

        if hosted.middle is not None:
            half_way = functools.reduce(jnp.logical_and, [pl.program_id(0) == max(1, grid[0] * MIDDLE_STEP_16THS // 16)] + [
                pl.program_id(k) == 0 for k in range(1, len(grid))])

            @pl.when(half_way)
            def _():
                hosted.middle(hi, ho, send_sems, recv_sems)

        body(*a, *o, *scr[:n_scr])

        @pl.when(last)
        def _():
            hosted.finish(hi, ho, send_sems, recv_sems)

    sem = pltpu.SemaphoreType.DMA(hosted.sem_shape)
    res = pl.pallas_call(
        wrapped, name=name, grid=grid, in_specs=list(in_specs) + [ANY] * h_in,
        out_specs=list(out_specs) + [ANY] * h_out, out_shape=tuple(out_shape) + hosted.out_shapes,
        scratch_shapes=list(scratch_shapes) + [sem, sem], compiler_params=_params(*(["arbitrary"] * len(grid))),
    )(*args, *hosted.ins)
    return res[:n_out], res[n_out:]


def _all_gather(shards, small=()):
    items = tuple(shards) + tuple(small)
    nw = len(shards)

    def place():
        x, y, c, me, _ = _place()
        first = (x + (1 - c) * (1 - 2 * x), y + c * (1 - 2 * y))
        second = (x + c * (1 - 2 * x), y + (1 - c) * (1 - 2 * y))
        diag = (1 - x, 1 - y)
        return x, y, c, me, (first, second, diag)

    def halves(w, c):
        rh = items[w].shape[0] // 2
        return pl.ds(c * rh, rh), pl.ds((1 - c) * rh, rh)

    def start(ins, outs, ss, rs):
        x, y, c, me, chips = place()
        for w in range(len(items)):
            _remote(ins[w], outs[w].at[me], ss.at[w, 6], rs.at[w, 6], (x, y, 1 - c)).start()
            if w < nw:
                mine, _ = halves(w, c)
                _remote(ins[w].at[mine], outs[w].at[me, mine], ss.at[w, 0], rs.at[w, 0], (*chips[0], c)).start()
            else:
                for k in range(3):
                    _remote(ins[w], outs[w].at[me], ss.at[w, k], rs.at[w, k], (*chips[k], c)).start()

    def onward(outs, ss, rs, w, k, x, y, c, chips):
        mine, _ = halves(w, c)
        pk = 2 * chips[k][0] + chips[k][1]
        got = outs[w].at[pk, mine]
        src = chips[1] if k == 2 else chips[k]
        _remote(got, got, ss.at[w, k], rs.at[w, k], (*src, c)).wait_recv()
        if k == 0:
            _remote(got, got, ss.at[w, 2], rs.at[w, 2], (*chips[1], c)).start()
        _remote(got, got, ss.at[w, 3 + k], rs.at[w, 3 + k], (x, y, 1 - c)).start()

    def middle(ins, outs, ss, rs):
        x, y, c, me, chips = place()
        for w in range(nw):
            mine, _ = halves(w, c)
            _remote(ins[w].at[mine], outs[w].at[me, mine], ss.at[w, 1], rs.at[w, 1], (*chips[1], c)).start()
        for w in range(nw):
            onward(outs, ss, rs, w, 0, x, y, c, chips)

    def finish(ins, outs, ss, rs):
        x, y, c, me, chips = place()
        sib = (x, y, 1 - c)
        for k in (1, 2):
            for w in range(nw):
                onward(outs, ss, rs, w, k, x, y, c, chips)
        for w in range(len(items)):
            if w < nw:
                mine, theirs = halves(w, c)
                for k, chip in ((3, chips[1]), (4, chips[0]), (5, chips[2])):
                    oth = outs[w].at[2 * chip[0] + chip[1], theirs]
                    _remote(oth, oth, ss.at[w, k], rs.at[w, k], sib).wait_recv()
                own = ins[w].at[mine]
                for k in range(6):
                    _remote(own, own, ss.at[w, k], rs.at[w, k], sib).wait_send()
            else:
                for k in range(3):
                    got = outs[w].at[2 * chips[k][0] + chips[k][1]]
                    _remote(got, got, ss.at[w, k], rs.at[w, k], (*chips[k], c)).wait_recv()
                    _remote(ins[w], ins[w], ss.at[w, k], rs.at[w, k], sib).wait_send()
            _remote(ins[w], outs[w].at[me], ss.at[w, 6], rs.at[w, 6], sib).wait()

    out_shapes = tuple(jax.ShapeDtypeStruct((N_CHIPS,) + a.shape, a.dtype) for a in items)
    return _Exchange(items, out_shapes, (len(items), 7), start, finish, middle if nw else None)


def _exchange_halves(grads):
    nw = len(grads)

    def copies(ins, outs, ss, rs):
        x, y, c, _, _ = _place()
        res = []
        for w in range(nw):
            rh = grads[w].shape[1] // 2
            res.append(_remote(ins[w].at[:, pl.ds((1 - c) * rh, rh), :], outs[w], ss.at[w], rs.at[w], (x, y, 1 - c)))
        return res

    def start(ins, outs, ss, rs):
        for cp in copies(ins, outs, ss, rs):
            cp.start()

    def finish(ins, outs, ss, rs):
        for cp in copies(ins, outs, ss, rs):
            cp.wait()

    halves = tuple(jax.ShapeDtypeStruct((g.shape[0], g.shape[1] // 2, g.shape[2]), g.dtype) for g in grads)
    return _Exchange(grads, halves, (nw,), start, finish)


def _scatter_partials(parts):
    nw = len(parts)

    def copies(ins, outs, ss, rs):
        _, _, c, _, chips = _place()
        return [_remote(ins[w].at[2 * px + py], outs[w].at[r], ss.at[w, r], rs.at[w, r], (px, py, c))
                for r, (px, py) in enumerate(chips) for w in range(nw)]

    def start(ins, outs, ss, rs):
        for cp in copies(ins, outs, ss, rs):
            cp.start()

    def finish(ins, outs, ss, rs):
        for cp in copies(ins, outs, ss, rs):
            cp.wait()

    out_shapes = tuple(jax.ShapeDtypeStruct((3,) + p.shape[1:], p.dtype) for p in parts)
    return _Exchange(parts, out_shapes, (nw, 3), start, finish)


def _join_halves(shards):
    nw = len(shards)

    def start(ins, outs, ss, rs):
        x, y, c, _, _ = _place()
        for w in range(nw):
            rh = shards[w].shape[0] // 2
            mine = outs[w].at[pl.ds(c * rh, rh)]
            _remote(mine, mine, ss.at[w], rs.at[w], (x, y, 1 - c)).start()

    def finish(ins, outs, ss, rs):
        x, y, c, _, _ = _place()
        for w in range(nw):
            rh = shards[w].shape[0] // 2
            mine = outs[w].at[pl.ds(c * rh, rh)]
            theirs = outs[w].at[pl.ds((1 - c) * rh, rh)]
            _remote(mine, mine, ss.at[w], rs.at[w], (x, y, 1 - c)).wait_send()
            _remote(theirs, theirs, ss.at[w], rs.at[w], (x, y, 1 - c)).wait_recv()

    ex = _Exchange(shards, tuple(jax.ShapeDtypeStruct(s.shape, s.dtype) for s in shards), (nw,), start, finish)
    return _run_exchange(ex, "rs_join_halves", aliases={w: w for w in range(nw)})


def _gather_small(slab):
    def copies(ins, outs, ss, rs):
        x, y, c, _, _ = _place()
        me = 4 * x + 2 * y + c
        own = pltpu.make_async_copy(ins[0], outs[0].at[me], ss.at[7])
        out, arrivals = [], []
        for k in range(1, 8):
            px = 1 - x if k & 4 else x
            py = 1 - y if k & 2 else y
            pc = 1 - c if k & 1 else c
            out.append(_remote(ins[0], outs[0].at[me], ss.at[k - 1], rs.at[k - 1], (px, py, pc)))
            theirs = outs[0].at[4 * px + 2 * py + pc]
            arrivals.append(_remote(theirs, theirs, ss.at[k - 1], rs.at[k - 1], (px, py, pc)))
        return own, out, arrivals

    def start(ins, outs, ss, rs):
        own, out, _ = copies(ins, outs, ss, rs)
        own.start()
        for cp in out:
            cp.start()

    def finish(ins, outs, ss, rs):
        own, out, arrivals = copies(ins, outs, ss, rs)
        for cp in out:
            cp.wait_send()
        for cp in arrivals:
            cp.wait_recv()
        own.wait()

    return _Exchange((slab,), (jax.ShapeDtypeStruct((8,) + slab.shape, slab.dtype),), (8,), start, finish)


_SMALL_VECS = ("ln_mix_g", "ln_attn_g", "ln_mem_g", "ln_ffn_g", "ln_final_g")


def _pack_small(p, extra, conv):
    d = p["ln_mix_g"].shape[-1]
    top = [p[k].reshape(1, d) for k in _SMALL_VECS]
    top.append(jnp.concatenate([p["sgu_ln_g"].reshape(-1), p["sgu_ln_b"].reshape(-1)]).reshape(1, d))
    top.append(jnp.concatenate([p["grp_norm_a"].reshape(-1), p["grp_norm_b"].reshape(-1)]).reshape(1, d))
    top.append(jnp.concatenate([p["b_spatial"].reshape(-1), extra]).reshape(1, d))
    mid = jnp.zeros((8, d), F32)
    if conv is not None:
        mid = jnp.pad(conv, ((0, 5), (0, d - conv.shape[1])))
    return jnp.concatenate([jnp.concatenate(top, axis=0), mid, p["w_spatial"].reshape(-1, d)], axis=0)


def _unpack_small(slab):
    d = slab.shape[1]
    hw = d // 2
    out = {k: slab[i] for i, k in enumerate(_SMALL_VECS)}
    out["sgu_ln_g"], out["sgu_ln_b"] = slab[5, :hw], slab[5, hw:]
    out["grp_norm_a"], out["grp_norm_b"] = slab[6, :hw], slab[6, hw:]
    out["b_spatial"] = slab[7, :hw].reshape(HEADS, CHUNK)
    out["w_spatial"] = slab[16:].reshape(HEADS, CHUNK, CHUNK)
    return out


_BIG = ("w_in", "w_kv", "w_gate_up", "w_out", "w_q", "w_o", "w_down")
_WEIGHTS = ("ln_mix_g", "w_in", "sgu_ln_g", "sgu_ln_b", "w_spatial", "b_spatial", "conv_w", "grp_norm_a",
            "grp_norm_b", "w_out", "ln_attn_g", "ln_mem_g", "w_q", "w_kv", "w_o", "ln_ffn_g", "w_gate_up",
            "w_down", "ln_final_g")


def _step(p, m_, v_, x, mem, target):
    s, d = x.shape
    hw = d // 2
    row = lambda a: a.reshape(1, -1)
    x_, y_, c_ = lax.axis_index("x"), lax.axis_index("y"), lax.axis_index("c")
    chip = 2 * x_ + y_

    bf = {k: p[k].astype(BF16) for k in _BIG}
    conv8 = jnp.pad(p["conv_w"], ((0, 5), (0, 0)))
    w_in, w_out4, conv4 = _run_exchange(_all_gather([bf["w_in"], bf["w_out"]], [conv8]), "all_gather_mixer")
    cw = jnp.transpose(conv4[:, :3, :], (1, 0, 2)).reshape(3, hw)
    b_t = jnp.pad(jnp.transpose(p["b_spatial"]), ((0, 0), (0, CHUNK - HEADS)))
    g1, g2, gm, g3, gf = (row(p[k]) for k in _SMALL_VECS)
    lng, lnb, ga, gb = row(p["sgu_ln_g"]), row(p["sgu_ln_b"]), row(p["grp_norm_a"]), row(p["grp_norm_b"])
    wsp = p["w_spatial"]
    w_out = w_out4.reshape(-1, d)

    (h, x1, ycat, xn1), (w_kv, w_q4, w_o4, w_down4) = _mixer_fwd(
        x, g1, w_in, lng, lnb, wsp, b_t, cw, ga, gb, w_out,
        hosted=_all_gather([bf[k] for k in ("w_kv", "w_q", "w_o", "w_down")]))
    w_q, w_o, w_down = (a.reshape(-1, d) for a in (w_q4, w_o4, w_down4))
    memn, kv = _kv_fwd(mem, gm, w_kv)
    (x2, o, xn2), (w_gu,) = _attn_fwd(x1, g2, w_q, kv, w_o, hosted=_all_gather([bf["w_gate_up"]]))
    dx2, act, dgu, xn3, dx3, loss, dgf, dg3, dx2b = _ffn_fwd_bwd(x2, g3, gf, target, w_gu, w_down)

    place = jnp.stack([c_, chip]).astype(jnp.int32)

    def chip_partials(names, grads, tag):
        got = _run_exchange(_exchange_halves(grads), "rs_exchange_" + tag)
        return [_chip_sum(g, b, place, "chip_sum_" + k) for k, g, b in zip(names, grads, got)]

    names_d = ("w_down",)
    parts_d = chip_partials(names_d, (_weight_grad(act, dx3, "grad_w_down", 1408, 512, False)[0],), "down")
    g_gu, slots_d = _weight_grad(xn3, dgu, "grad_w_gate_up", 512, 1408, True, hosted=_scatter_partials(parts_d))
    names_a = ("w_gate_up", "w_o")
    parts_a = chip_partials(names_a, (g_gu, _weight_grad(o, dx2b, "grad_w_o", 1024, 512, False)[0]), "ffn")
    (dx1, dq, dkv, dg2, dx1b), slots_a = _attn_bwd(x1, dx2, g2, w_q, kv, w_o, hosted=_scatter_partials(parts_a))
    g_kv, dgm = _kv_bwd(dkv, mem, memn, gm, w_kv)
    names_b = ("w_out", "w_q", "w_kv")
    parts_b = chip_partials(names_b, (_weight_grad(ycat, dx1b, "grad_w_out", 1024, 512, False)[0],
                                      _weight_grad(xn2, dq, "grad_w_q", 1024, 512, False)[0], g_kv), "attn")
    (dx, dh, dg1, dlng, dlnb, dwsp, dbt, dcw, dga, dgb), slots_b = _mixer_bwd(
        x, dx1, h, g1, lng, lnb, wsp, b_t, cw, ga, gb, w_out, w_in, hosted=_scatter_partials(parts_b))
    small = {"ln_mix_g": dg1, "ln_attn_g": dg2, "ln_mem_g": dgm, "ln_ffn_g": dg3, "ln_final_g": dgf,
             "sgu_ln_g": dlng, "sgu_ln_b": dlnb, "grp_norm_a": dga, "grp_norm_b": dgb,
             "b_spatial": jnp.transpose(dbt[:, :HEADS]), "w_spatial": dwsp}
    loss_vec = jnp.pad(loss.reshape(1), (0, hw - 1))
    g_in, (parts,) = _weight_grad(xn1, dh, "grad_w_in", 1024, 640, True,
                                  hosted=_gather_small(_pack_small(small, loss_vec, dcw)))
    names_c = ("w_in",)
    parts_c = chip_partials(names_c, (g_in,), "mixer")
    slots_c = _run_exchange(_scatter_partials(parts_c), "rs_scatter_mixer")
    names = names_d + names_a + names_b + names_c
    halves = [_final_sum(a, sl, place, "final_sum_" + k)
              for k, a, sl in zip(names, parts_d + parts_a + parts_b + parts_c,
                                  slots_d + slots_a + slots_b + tuple(slots_c))]
    big_g = dict(zip(names, _join_halves(halves)))

    zeros = jnp.zeros((hw,), F32)
    sg, sd, sm, sv = _small_sum_adamw(parts, _pack_small(p, zeros, None), _pack_small(m_, zeros, None),
                                      _pack_small(v_, zeros, None))
    out_g, out_d, out_m, out_v = (_unpack_small(a) for a in (sg, sd, sm, sv))
    loss_out = sg[7, hw]
    g_conv = lax.dynamic_slice(sg[8:11, :hw], (0, chip * (hw // N_CHIPS)), (3, hw // N_CHIPS))
    out_g["conv_w"] = g_conv
    out_d["conv_w"], out_m["conv_w"], out_v["conv_w"] = _adamw(p["conv_w"], g_conv, m_["conv_w"], v_["conv_w"],
                                                                "adamw_conv_w")
    for k in _BIG:
        out_g[k] = big_g[k]
        out_d[k], out_m[k], out_v[k] = _adamw(p[k], big_g[k], m_[k], v_[k], "adamw_" + k)
    return loss_out, dx, out_g, out_d, out_m, out_v


def kernel(x, mem, ln_mix_g, w_in, sgu_ln_g, sgu_ln_b, w_spatial, b_spatial, conv_w, grp_norm_a, grp_norm_b, w_out, ln_attn_g, ln_mem_g, w_q, w_kv, w_o, ln_ffn_g, w_gate_up, w_down, ln_final_g, loss_target, m_ln_mix_g, m_w_in, m_sgu_ln_g, m_sgu_ln_b, m_w_spatial, m_b_spatial, m_conv_w, m_grp_norm_a, m_grp_norm_b, m_w_out, m_ln_attn_g, m_ln_mem_g, m_w_q, m_w_kv, m_w_o, m_ln_ffn_g, m_w_gate_up, m_w_down, m_ln_final_g, v_ln_mix_g, v_w_in, v_sgu_ln_g, v_sgu_ln_b, v_w_spatial, v_b_spatial, v_conv_w, v_grp_norm_a, v_grp_norm_b, v_w_out, v_ln_attn_g, v_ln_mem_g, v_w_q, v_w_kv, v_w_o, v_ln_ffn_g, v_w_gate_up, v_w_down, v_ln_final_g):
    p = dict(ln_mix_g=ln_mix_g, w_in=w_in, sgu_ln_g=sgu_ln_g, sgu_ln_b=sgu_ln_b, w_spatial=w_spatial,
             b_spatial=b_spatial, conv_w=conv_w, grp_norm_a=grp_norm_a, grp_norm_b=grp_norm_b, w_out=w_out,
             ln_attn_g=ln_attn_g, ln_mem_g=ln_mem_g, w_q=w_q, w_kv=w_kv, w_o=w_o, ln_ffn_g=ln_ffn_g,
             w_gate_up=w_gate_up, w_down=w_down, ln_final_g=ln_final_g)
    m_ = dict(ln_mix_g=m_ln_mix_g, w_in=m_w_in, sgu_ln_g=m_sgu_ln_g, sgu_ln_b=m_sgu_ln_b, w_spatial=m_w_spatial,
              b_spatial=m_b_spatial, conv_w=m_conv_w, grp_norm_a=m_grp_norm_a, grp_norm_b=m_grp_norm_b,
              w_out=m_w_out, ln_attn_g=m_ln_attn_g, ln_mem_g=m_ln_mem_g, w_q=m_w_q, w_kv=m_w_kv, w_o=m_w_o,
              ln_ffn_g=m_ln_ffn_g, w_gate_up=m_w_gate_up, w_down=m_w_down, ln_final_g=m_ln_final_g)
    v_ = dict(ln_mix_g=v_ln_mix_g, w_in=v_w_in, sgu_ln_g=v_sgu_ln_g, sgu_ln_b=v_sgu_ln_b, w_spatial=v_w_spatial,
              b_spatial=v_b_spatial, conv_w=v_conv_w, grp_norm_a=v_grp_norm_a, grp_norm_b=v_grp_norm_b,
              w_out=v_w_out, ln_attn_g=v_ln_attn_g, ln_mem_g=v_ln_mem_g, w_q=v_w_q, w_kv=v_w_kv, w_o=v_w_o,
              ln_ffn_g=v_ln_ffn_g, w_gate_up=v_w_gate_up, w_down=v_w_down, ln_final_g=v_ln_final_g)
    s, d = x.shape[-2], x.shape[-1]
    loss, dx, g, dl, nm, nv = _step(p, m_, v_, x.reshape(s, d), mem.reshape(-1, d), loss_target.reshape(s, d))
    outs = [loss, dx.reshape(x.shape)]
    for tree in (g, dl, nm, nv):
        outs += [tree[k].reshape(p[k].shape) for k in _WEIGHTS]
    return tuple(outs)
```

```python
import functools
import math

import jax
import jax.numpy as jnp
from jax import lax
from jax.experimental import pallas as pl
from jax.experimental.pallas import tpu as pltpu

F32 = jnp.float32
BF16 = jnp.bfloat16
EPS = 1e-6
CHUNK = 128
HEADS = 4
N_CHIPS = 4
TM = 256
ADAM_LR, ADAM_B1, ADAM_B2, ADAM_EPS, ADAM_WD, ADAM_STEP = 0.001, 0.9, 0.999, 1e-08, 0.01, 10
GELU_C = math.sqrt(2.0 / math.pi)
GELU_K = 0.044715
SMALL_ROWS = 80
VMEM_LIMIT = 56 * 1024 * 1024
MIDDLE_STEP_16THS = 5
MESH = pl.DeviceIdType.MESH
ANY = pl.BlockSpec(memory_space=pl.ANY)


def _params(*sem):
    return pltpu.CompilerParams(dimension_semantics=sem, vmem_limit_bytes=VMEM_LIMIT)


def _dot(a, b):
    return jnp.dot(a, b, preferred_element_type=F32)


def _dot_nt(a, b):
    return lax.dot_general(a, b, (((1,), (1,)), ((), ())), preferred_element_type=F32)


def _dot_tn(a, b):
    return lax.dot_general(a, b, (((0,), (0,)), ((), ())), preferred_element_type=F32)


def _rms_fwd(x, g):
    r = lax.rsqrt(jnp.mean(x * x, axis=-1, keepdims=True) + EPS)
    xh = x * r
    return xh * g, xh, r


def _rms_bwd(dy, xh, r, g):
    dxh = dy * g
    dx = r * (dxh - xh * jnp.mean(dxh * xh, axis=-1, keepdims=True))
    return dx, jnp.sum(dy * xh, axis=0, keepdims=True)


def _full(shape):
    nd = len(shape)
    return pl.BlockSpec(shape, lambda *_: (0,) * nd, pipeline_mode=pl.Buffered(1))


def _acc(shape):
    nd = len(shape)
    return pl.BlockSpec(shape, lambda *_: (0,) * nd)


def _rows(tm, cols):
    return pl.BlockSpec((tm, cols), lambda i: (i, 0))


def _tril_weights(wsp_ref):
    row = lax.broadcasted_iota(jnp.int32, (CHUNK, CHUNK), 0)
    col = lax.broadcasted_iota(jnp.int32, (CHUNK, CHUNK), 1)
    return [jnp.where(row >= col, wsp_ref[hd], 0.0).astype(BF16) for hd in range(HEADS)]


def _shift_rows(z, zp):
    row = lax.broadcasted_iota(jnp.int32, z.shape, 0)
    zm1 = jnp.where(row == 0, zp[7:8, :], pltpu.roll(z, 1, 0))
    zm2 = jnp.where(row == 0, zp[6:7, :], jnp.where(row == 1, zp[7:8, :], pltpu.roll(z, 2, 0)))
    return zm1, zm2


def _gelu_parts(x):
    t = jnp.tanh(GELU_C * (x + GELU_K * (x * x * x)))
    return 0.5 * x * (1.0 + t), t


def _layer_norm_parts(v, g, b):
    mu = jnp.mean(v, axis=-1, keepdims=True)
    vc = v - mu
    rs = lax.rsqrt(jnp.mean(vc * vc, axis=-1, keepdims=True) + EPS)
    vhat = vc * rs
    return vhat * g + b, vhat, rs


def _kv_fwd(mem, g_mem, w_kv):
    m, d = mem.shape
    ns = w_kv.shape[2]

    def body(mem_ref, g_ref, w_ref, memn_ref, kv_ref):
        y, _, _ = _rms_fwd(mem_ref[...], g_ref[...])
        yb = y.astype(BF16)
        memn_ref[...] = yb
        for k in range(N_CHIPS):
            kv_ref[:, k * ns:(k + 1) * ns] = _dot(yb, w_ref[k]).astype(BF16)

    return pl.pallas_call(
        body, name="kv_fwd",
        out_shape=(jax.ShapeDtypeStruct((m, d), BF16), jax.ShapeDtypeStruct((m, N_CHIPS * ns), BF16)),
        compiler_params=pltpu.CompilerParams(vmem_limit_bytes=VMEM_LIMIT),
    )(mem, g_mem, w_kv)


def _mixer_fwd(x, g1, w_in, lng, lnb, wsp, b_t, cw, ga, gb, w_out, hosted=None):
    s, d = x.shape
    n = s // TM
    nch = TM // CHUNK
    ns = w_in.shape[2]
    nh = N_CHIPS * ns
    aw = d // 2
    hd_w = aw // HEADS

    def body(x_ref, g1_ref, win_ref, lng_ref, lnb_ref, wsp_ref, bt_ref, cw_ref, ga_ref, gb_ref, wout_ref,
             h_ref, x1_ref, y_ref, xn_ref, mix_ref, zp_ref):
        i = pl.program_id(0)

        @pl.when(i == 0)
        def _():
            zp_ref[...] = jnp.zeros_like(zp_ref)

        x = x_ref[...]
        xn, _, _ = _rms_fwd(x, g1_ref[...])
        xnb = xn.astype(BF16)
        xn_ref[...] = xnb
        for k in range(N_CHIPS):
            h_ref[:, k * ns:(k + 1) * ns] = _dot(xnb, win_ref[k])
        a, _ = _gelu_parts(h_ref[:, 0:2 * aw])
        u = a[:, :aw]
        vn, _, _ = _layer_norm_parts(a[:, aw:], lng_ref[...], lnb_ref[...])
        vnb = vn.astype(BF16)
        wm = _tril_weights(wsp_ref)
        for c in range(nch):
            for hd in range(HEADS):
                blk = vnb[c * CHUNK:(c + 1) * CHUNK, hd * hd_w:(hd + 1) * hd_w]
                mix_ref[c * CHUNK:(c + 1) * CHUNK, hd * hd_w:(hd + 1) * hd_w] = _dot(wm[hd], blk) + bt_ref[:, hd:hd + 1]
        ya, _, _ = _rms_fwd(u * mix_ref[...], ga_ref[...])
        g_b = h_ref[:, 2 * aw:3 * aw]
        z = h_ref[:, 3 * aw:4 * aw] * h_ref[:, 4 * aw:5 * aw]
        zm1, zm2 = _shift_rows(z, zp_ref[...])
        conv = cw_ref[0:1, :] * zm2 + cw_ref[1:2, :] * zm1 + cw_ref[2:3, :] * z
        yb, _, _ = _rms_fwd(g_b * conv, gb_ref[...])
        zp_ref[...] = z[TM - 8:TM, :]
        ycat = jnp.concatenate([ya, yb], axis=-1).astype(BF16)
        y_ref[...] = ycat
        x1_ref[...] = x + _dot(ycat, wout_ref[...])

    return _host_call(
        body, "mixer_fwd", (n,),
        [_rows(TM, d), _full(g1.shape), _full(w_in.shape), _full(lng.shape), _full(lnb.shape),
         _full(wsp.shape), _full(b_t.shape), _full(cw.shape), _full(ga.shape), _full(gb.shape),
         _full(w_out.shape)],
        [_rows(TM, nh), _rows(TM, d), _rows(TM, d), _rows(TM, d)],
        (jax.ShapeDtypeStruct((s, nh), F32), jax.ShapeDtypeStruct((s, d), F32),
         jax.ShapeDtypeStruct((s, d), BF16), jax.ShapeDtypeStruct((s, d), BF16)),
        [pltpu.VMEM((TM, aw), F32), pltpu.VMEM((8, aw), F32)],
        (x, g1, w_in, lng, lnb, wsp, b_t, cw, ga, gb, w_out), ("arbitrary",), hosted)


def _attn_probs(qb, kv_ref, hd, dh, d, scale):
    kh = kv_ref[:, hd * dh:(hd + 1) * dh]
    vh = kv_ref[:, d + hd * dh:d + (hd + 1) * dh]
    sc = _dot_nt(qb, kh) * scale
    sc = sc - jnp.max(sc, axis=-1, keepdims=True)
    e = jnp.exp(sc)
    return e / jnp.sum(e, axis=-1, keepdims=True), kh, vh


def _attn_fwd(x1, g2, w_q, kv, w_o, hosted=None):
    s, d = x1.shape
    n = s // TM
    dh = d // HEADS
    scale = dh ** -0.5

    def body(x1_ref, g2_ref, wq_ref, kv_ref, wo_ref, x2_ref, o_ref, xn_ref):
        x1v = x1_ref[...]
        xn, _, _ = _rms_fwd(x1v, g2_ref[...])
        xnb = xn.astype(BF16)
        xn_ref[...] = xnb
        q = _dot(xnb, wq_ref[...])
        for hd in range(HEADS):
            qb = q[:, hd * dh:(hd + 1) * dh].astype(BF16)
            p, _, vh = _attn_probs(qb, kv_ref, hd, dh, d, scale)
            o_ref[:, hd * dh:(hd + 1) * dh] = _dot(p.astype(BF16), vh).astype(BF16)
        x2_ref[...] = x1v + _dot(o_ref[...], wo_ref[...])

    return _host_call(
        body, "attn_fwd", (n,),
        [_rows(TM, d), _full(g2.shape), _full(w_q.shape), _full(kv.shape), _full(w_o.shape)],
        [_rows(TM, d), _rows(TM, d), _rows(TM, d)],
        (jax.ShapeDtypeStruct((s, d), F32), jax.ShapeDtypeStruct((s, d), BF16), jax.ShapeDtypeStruct((s, d), BF16)),
        [], (x1, g2, w_q, kv, w_o), ("parallel",), hosted)


def _ffn_fwd_bwd(x2, g3, gf, target, w_gu, w_down):
    s, d = x2.shape
    n = s // TM
    ns = w_gu.shape[2]
    ff = 2 * ns

    def body(x2_ref, g3_ref, gf_ref, t_ref, wgu_ref, wd_ref,
             dx2_ref, act_ref, dgu_ref, xn_ref, dx3_ref, loss_ref, dgf_ref, dg3_ref, dx2b_ref):
        i = pl.program_id(0)

        @pl.when(i == 0)
        def _():
            loss_ref[...] = jnp.zeros_like(loss_ref)
            dgf_ref[...] = jnp.zeros_like(dgf_ref)
            dg3_ref[...] = jnp.zeros_like(dg3_ref)

        x2v = x2_ref[...]
        xn, xh3, r3 = _rms_fwd(x2v, g3_ref[...])
        xnb = xn.astype(BF16)
        xn_ref[...] = xnb
        x3 = x2v
        saved = []
        for j in range(2):
            g = _dot(xnb, wgu_ref[j])
            u = _dot(xnb, wgu_ref[2 + j])
            sg = 1.0 / (1.0 + jnp.exp(-g))
            sl = g * sg
            actb = (sl * u).astype(BF16)
            act_ref[:, j * ns:(j + 1) * ns] = actb
            x3 = x3 + _dot(actb, wd_ref[j * ns:(j + 1) * ns, :])
            saved.append((u, sl, sg * (1.0 + g * (1.0 - sg))))
        gfv = gf_ref[...]
        y, xhf, rf = _rms_fwd(x3, gfv)
        e = y - t_ref[...]
        loss_ref[...] += 0.5 * jnp.sum(jnp.sum(e * e, axis=-1, keepdims=True), axis=0, keepdims=True) / d
        dx3, dgf = _rms_bwd(e / d, xhf, rf, gfv)
        dgf_ref[...] += dgf
        dx3b = dx3.astype(BF16)
        dx3_ref[...] = dx3b
        dxn = jnp.zeros_like(x2v)
        for j in range(2):
            u, sl, dsl = saved[j]
            dact = _dot_nt(dx3b, wd_ref[j * ns:(j + 1) * ns, :])
            dgb = (dact * u * dsl).astype(BF16)
            dub = (dact * sl).astype(BF16)
            dgu_ref[:, j * ns:(j + 1) * ns] = dgb
            dgu_ref[:, ff + j * ns:ff + (j + 1) * ns] = dub
            dxn = dxn + _dot_nt(dgb, wgu_ref[j]) + _dot_nt(dub, wgu_ref[2 + j])
        dxr, dg3 = _rms_bwd(dxn, xh3, r3, g3_ref[...])
        dg3_ref[...] += dg3
        dx2 = dx3 + dxr
        dx2_ref[...] = dx2
        dx2b_ref[...] = dx2.astype(BF16)

    vec = jax.ShapeDtypeStruct((1, d), F32)
    return pl.pallas_call(
        body, name="ffn_fwd_bwd", grid=(n,),
        in_specs=[_rows(TM, d), _full(g3.shape), _full(gf.shape), _rows(TM, d), _full(w_gu.shape),
                  _full(w_down.shape)],
        out_specs=[_rows(TM, d), _rows(TM, ff), _rows(TM, 2 * ff), _rows(TM, d), _rows(TM, d),
                   _acc((1, 1)), _acc((1, d)), _acc((1, d)), _rows(TM, d)],
        out_shape=(jax.ShapeDtypeStruct((s, d), F32), jax.ShapeDtypeStruct((s, ff), BF16),
                   jax.ShapeDtypeStruct((s, 2 * ff), BF16), jax.ShapeDtypeStruct((s, d), BF16),
                   jax.ShapeDtypeStruct((s, d), BF16), jax.ShapeDtypeStruct((1, 1), F32), vec, vec,
                   jax.ShapeDtypeStruct((s, d), BF16)),
        compiler_params=_params("arbitrary"),
    )(x2, g3, gf, target, w_gu, w_down)


def _attn_bwd(x1, dx2, g2, w_q, kv, w_o, hosted=None):
    s, d = x1.shape
    n = s // TM
    dh = d // HEADS
    scale = dh ** -0.5
    m = kv.shape[0]

    def body(x1_ref, dx2_ref, g2_ref, wq_ref, kv_ref, wo_ref, dx1_ref, dq_ref, dkv_ref, dg2_ref, dx1b_ref):
        i = pl.program_id(0)

        @pl.when(i == 0)
        def _():
            dkv_ref[...] = jnp.zeros_like(dkv_ref)
            dg2_ref[...] = jnp.zeros_like(dg2_ref)

        xn, xh2, r2 = _rms_fwd(x1_ref[...], g2_ref[...])
        q = _dot(xn.astype(BF16), wq_ref[...])
        dx2v = dx2_ref[...]
        do = _dot_nt(dx2v.astype(BF16), wo_ref[...])
        for hd in range(HEADS):
            qb = q[:, hd * dh:(hd + 1) * dh].astype(BF16)
            p, kh, vh = _attn_probs(qb, kv_ref, hd, dh, d, scale)
            dob = do[:, hd * dh:(hd + 1) * dh].astype(BF16)
            dp = _dot_nt(dob, vh)
            ds = p * (dp - jnp.sum(dp * p, axis=-1, keepdims=True))
            dsb = (ds * scale).astype(BF16)
            dq_ref[:, hd * dh:(hd + 1) * dh] = _dot(dsb, kh).astype(BF16)
            dkv_ref[:, hd * dh:(hd + 1) * dh] += _dot_tn(dsb, qb)
            dkv_ref[:, d + hd * dh:d + (hd + 1) * dh] += _dot_tn(p.astype(BF16), dob)
        dxn = _dot_nt(dq_ref[...], wq_ref[...])
        dxr, dg2 = _rms_bwd(dxn, xh2, r2, g2_ref[...])
        dg2_ref[...] += dg2
        dx1 = dx2v + dxr
        dx1_ref[...] = dx1
        dx1b_ref[...] = dx1.astype(BF16)

    return _host_call(
        body, "attn_bwd", (n,),
        [_rows(TM, d), _rows(TM, d), _full(g2.shape), _full(w_q.shape), _full(kv.shape), _full(w_o.shape)],
        [_rows(TM, d), _rows(TM, d), _acc((m, 2 * d)), _acc((1, d)), _rows(TM, d)],
        (jax.ShapeDtypeStruct((s, d), F32), jax.ShapeDtypeStruct((s, d), BF16),
         jax.ShapeDtypeStruct((m, 2 * d), F32), jax.ShapeDtypeStruct((1, d), F32),
         jax.ShapeDtypeStruct((s, d), BF16)),
        [], (x1, dx2, g2, w_q, kv, w_o), ("arbitrary",), hosted)


def _kv_bwd(dkv, mem, memn, g_mem, w_kv):
    m, d = mem.shape
    ns = w_kv.shape[2]

    def body(dkv_ref, mem_ref, memn_ref, g_ref, w_ref, gw_ref, dg_ref):
        _, xh, _ = _rms_fwd(mem_ref[...], g_ref[...])
        dmemn = jnp.zeros((m, d), F32)
        for k in range(N_CHIPS):
            dkb = dkv_ref[:, k * ns:(k + 1) * ns].astype(BF16)
            gw_ref[k] = _dot_tn(memn_ref[...], dkb).astype(BF16)
            dmemn = dmemn + _dot_nt(dkb, w_ref[k])
        dg_ref[...] = jnp.sum(dmemn * xh, axis=0, keepdims=True)

    return pl.pallas_call(
        body, name="kv_bwd",
        out_shape=(jax.ShapeDtypeStruct((N_CHIPS, d, ns), BF16), jax.ShapeDtypeStruct((1, d), F32)),
        compiler_params=pltpu.CompilerParams(vmem_limit_bytes=VMEM_LIMIT),
    )(dkv, mem, memn, g_mem, w_kv)


def _mixer_bwd(x, dx1, h, g1, lng, lnb, wsp, b_t, cw, ga, gb, w_out, w_in, hosted=None):
    s, d = x.shape
    n = s // TM
    nch = TM // CHUNK
    ns = w_in.shape[2]
    nh = N_CHIPS * ns
    aw = d // 2
    hd_w = aw // HEADS

    def rev(cols):
        return pl.BlockSpec((TM, cols), lambda i: (n - 1 - i, 0))

    hprev = pl.BlockSpec((8, nh), lambda i: (jnp.maximum((n - 1 - i) * (TM // 8) - 1, 0), 0))

    def body(x_ref, dx1_ref, h_ref, hp_ref, g1_ref, lng_ref, lnb_ref, wsp_ref, bt_ref, cw_ref, ga_ref, gb_ref,
             wout_ref, win_ref,
             dx_ref, dh_ref, dg1_ref, dlng_ref, dlnb_ref, dwsp_ref, dbt_ref, dcw_ref, dga_ref, dgb_ref,
             mix_ref, dvn_ref, dcn_ref):
        i = pl.program_id(0)

        @pl.when(i == 0)
        def _():
            for r in (dg1_ref, dlng_ref, dlnb_ref, dwsp_ref, dbt_ref, dcw_ref, dga_ref, dgb_ref, dcn_ref):
                r[...] = jnp.zeros_like(r)

        dx1v = dx1_ref[...]
        dycat = _dot_nt(dx1v.astype(BF16), wout_ref[...])
        ha = h_ref[:, 0:2 * aw]
        a, th = _gelu_parts(ha)
        u = a[:, :aw]
        lngv = lng_ref[...]
        vn, vhat, rs = _layer_norm_parts(a[:, aw:], lngv, lnb_ref[...])
        vnb = vn.astype(BF16)
        wm = _tril_weights(wsp_ref)
        for c in range(nch):
            for hd in range(HEADS):
                blk = vnb[c * CHUNK:(c + 1) * CHUNK, hd * hd_w:(hd + 1) * hd_w]
                mix_ref[c * CHUNK:(c + 1) * CHUNK, hd * hd_w:(hd + 1) * hd_w] = _dot(wm[hd], blk) + bt_ref[:, hd:hd + 1]
        mixed = mix_ref[...]
        gav = ga_ref[...]
        _, yah, ra = _rms_fwd(u * mixed, gav)
        dya, dga = _rms_bwd(dycat[:, :aw], yah, ra, gav)
        dga_ref[...] += dga
        du = dya * mixed
        dmix = dya * u
        dmb = dmix.astype(BF16)
        tri = lax.broadcasted_iota(jnp.int32, (CHUNK, CHUNK), 0) >= lax.broadcasted_iota(jnp.int32, (CHUNK, CHUNK), 1)
        for hd in range(HEADS):
            dw = jnp.zeros((CHUNK, CHUNK), F32)
            db = jnp.zeros((CHUNK, 1), F32)
            for c in range(nch):
                rows = slice(c * CHUNK, (c + 1) * CHUNK)
                cols = slice(hd * hd_w, (hd + 1) * hd_w)
                dvn_ref[rows, cols] = _dot_tn(wm[hd], dmb[rows, cols])
                dw = dw + _dot_nt(dmb[rows, cols], vnb[rows, cols])
                db = db + jnp.sum(dmix[rows, cols], axis=1, keepdims=True)
            dwsp_ref[hd] += jnp.where(tri, dw, 0.0)
            dbt_ref[:, hd:hd + 1] += db
        dvn = dvn_ref[...]
        dlng_ref[...] += jnp.sum(dvn * vhat, axis=0, keepdims=True)
        dlnb_ref[...] += jnp.sum(dvn, axis=0, keepdims=True)
        dvh = dvn * lngv
        dv = rs * (dvh - jnp.mean(dvh, axis=-1, keepdims=True) - vhat * jnp.mean(dvh * vhat, axis=-1, keepdims=True))
        gprime = 0.5 * (1.0 + th) + 0.5 * ha * (1.0 - th * th) * (GELU_C * (1.0 + 3.0 * GELU_K * (ha * ha)))
        dh_ref[:, 0:2 * aw] = (jnp.concatenate([du, dv], axis=-1) * gprime).astype(BF16)
        g_b = h_ref[:, 2 * aw:3 * aw]
        g_c = h_ref[:, 3 * aw:4 * aw]
        val = h_ref[:, 4 * aw:5 * aw]
        z = g_c * val
        zp = jnp.where(i == n - 1, 0.0, hp_ref[:, 3 * aw:4 * aw] * hp_ref[:, 4 * aw:5 * aw])
        zm1, zm2 = _shift_rows(z, zp)
        cw0, cw1, cw2 = cw_ref[0:1, :], cw_ref[1:2, :], cw_ref[2:3, :]
        conv = cw0 * zm2 + cw1 * zm1 + cw2 * z
        gbv = gb_ref[...]
        _, ybh, rb = _rms_fwd(g_b * conv, gbv)
        dyb, dgb = _rms_bwd(dycat[:, aw:], ybh, rb, gbv)
        dgb_ref[...] += dgb
        dconv = dyb * g_b
        dcw_ref[0:1, :] += jnp.sum(dconv * zm2, axis=0, keepdims=True)
        dcw_ref[1:2, :] += jnp.sum(dconv * zm1, axis=0, keepdims=True)
        dcw_ref[2:3, :] += jnp.sum(dconv * z, axis=0, keepdims=True)
        nxt = dcn_ref[...]
        row = lax.broadcasted_iota(jnp.int32, dconv.shape, 0)
        dcp1 = jnp.where(row == TM - 1, nxt[0:1, :], pltpu.roll(dconv, TM - 1, 0))
        dcp2 = jnp.where(row == TM - 1, nxt[1:2, :],
                         jnp.where(row == TM - 2, nxt[0:1, :], pltpu.roll(dconv, TM - 2, 0)))
        dz = cw2 * dconv + cw1 * dcp1 + cw0 * dcp2
        dcn_ref[...] = dconv[0:8, :]
        dh_ref[:, 2 * aw:3 * aw] = (dyb * conv).astype(BF16)
        dh_ref[:, 3 * aw:4 * aw] = (dz * val).astype(BF16)
        dh_ref[:, 4 * aw:5 * aw] = (dz * g_c).astype(BF16)
        dxn = jnp.zeros((TM, d), F32)
        for k in range(N_CHIPS):
            dxn = dxn + _dot_nt(dh_ref[:, k * ns:(k + 1) * ns], win_ref[k])
        g1v = g1_ref[...]
        _, xh1, r1 = _rms_fwd(x_ref[...], g1v)
        dxr, dg1 = _rms_bwd(dxn, xh1, r1, g1v)
        dg1_ref[...] += dg1
        dx_ref[...] = dx1v + dxr

    ins = (x, dx1, h, h, g1, lng, lnb, wsp, b_t, cw, ga, gb, w_out, w_in)
    acc_shapes = [(1, d), (1, aw), (1, aw), wsp.shape, (CHUNK, CHUNK), cw.shape, (1, aw), (1, aw)]
    return _host_call(
        body, "mixer_bwd", (n,),
        [rev(d), rev(d), rev(nh), hprev] + [_full(a.shape) for a in ins[4:]],
        [rev(d), rev(nh)] + [_acc(sh) for sh in acc_shapes],
        (jax.ShapeDtypeStruct((s, d), F32), jax.ShapeDtypeStruct((s, nh), BF16))
        + tuple(jax.ShapeDtypeStruct(sh, F32) for sh in acc_shapes),
        [pltpu.VMEM((TM, aw), F32), pltpu.VMEM((TM, aw), F32), pltpu.VMEM((8, aw), F32)],
        ins, ("arbitrary",), hosted)


def _weight_grad(a, b, name, tm, tn, col_sharded, hosted=None):
    t, m = a.shape
    n = b.shape[1]

    def body(a_ref, b_ref, o_ref):
        o_ref[...] = _dot_tn(a_ref[...].astype(BF16), b_ref[...].astype(BF16)).astype(BF16)

    if col_sharded:
        ns = n // N_CHIPS
        per = ns // tn
        out_shape = jax.ShapeDtypeStruct((N_CHIPS, m, ns), BF16)
        out_spec = pl.BlockSpec((None, tm, tn), lambda i, j: (j // per, i, j % per))
    else:
        out_shape = jax.ShapeDtypeStruct((m, n), BF16)
        out_spec = pl.BlockSpec((tm, tn), lambda i, j: (i, j))
    (out,), extra = _host_call(
        body, name, (m // tm, n // tn),
        [pl.BlockSpec((t, tm), lambda i, j: (0, i)), pl.BlockSpec((t, tn), lambda i, j: (0, j))],
        [out_spec], (out_shape,), [], (a, b), ("parallel", "parallel"), hosted)
    return (out if col_sharded else out.reshape(N_CHIPS, m // N_CHIPS, n)), extra


def _row_tile(rows, cap=256):
    best = None
    for t in range(16, min(rows, cap) + 1, 16):
        if rows % t == 0:
            best = t
    return best if best is not None else rows


def _chip_sum(grad, got, place, name):
    nb, rh, c = got.shape
    tr = _row_tile(rh)
    per = rh // tr

    def body(place_ref, a_ref, b_ref, o_ref):
        o_ref[...] = (a_ref[...].astype(F32) + b_ref[...].astype(F32)).astype(BF16)

    mine = pl.BlockSpec((None, tr, c), lambda k, i, pref: (k, pref[0] * per + i, 0))
    return pl.pallas_call(
        body, name=name,
        grid_spec=pltpu.PrefetchScalarGridSpec(
            num_scalar_prefetch=1, grid=(nb, per),
            in_specs=[mine, pl.BlockSpec((None, tr, c), lambda k, i, pref: (k, i, 0))],
            out_specs=mine),
        out_shape=jax.ShapeDtypeStruct(grad.shape, BF16),
        compiler_params=_params("parallel", "parallel"),
    )(place, grad, got)


def _adamw_math(w, g, m, v):
    m2 = ADAM_B1 * m + (1.0 - ADAM_B1) * g
    v2 = ADAM_B2 * v + (1.0 - ADAM_B2) * (g * g)
    m_hat = m2 / (1.0 - ADAM_B1 ** ADAM_STEP)
    v_hat = v2 / (1.0 - ADAM_B2 ** ADAM_STEP)
    delta = -ADAM_LR * (m_hat / (jnp.sqrt(v_hat) + ADAM_EPS) + ADAM_WD * w)
    return delta, m2, v2


def _adamw(w, g, m, v, name):
    r, c = w.shape
    tr = _row_tile(r) if r >= 16 else r

    def body(w_ref, g_ref, m_ref, v_ref, d_ref, m2_ref, v2_ref):
        d_ref[...], m2_ref[...], v2_ref[...] = _adamw_math(w_ref[...], g_ref[...], m_ref[...], v_ref[...])

    sh = jax.ShapeDtypeStruct((r, c), F32)
    return pl.pallas_call(
        body, name=name, grid=(r // tr,),
        in_specs=[_rows(tr, c)] * 4, out_specs=[_rows(tr, c)] * 3, out_shape=(sh, sh, sh),
        compiler_params=_params("parallel"),
    )(w, g, m, v)


def _finalize(part, slots, w, m, v, place, name, hosted=None):
    r, c = w.shape
    tr = _row_tile(r)

    def body(place_ref, own_ref, s0_ref, s1_ref, s2_ref, w_ref, m_ref, v_ref, g_ref, d_ref, m2_ref, v2_ref):
        g = own_ref[...].astype(F32) + s0_ref[...].astype(F32)
        g = (g + s1_ref[...].astype(F32)) + s2_ref[...].astype(F32)
        g_ref[...] = g
        d_ref[...], m2_ref[...], v2_ref[...] = _adamw_math(w_ref[...], g, m_ref[...], v_ref[...])

    def slot(k):
        return pl.BlockSpec((None, tr, c), lambda i, pref: (k, i, 0))

    rows = pl.BlockSpec((tr, c), lambda i, pref: (i, 0))
    sh = jax.ShapeDtypeStruct((r, c), F32)
    return _host_call(
        body, name, (r // tr,),
        [pl.BlockSpec((None, tr, c), lambda i, pref: (pref[1], i, 0)), slot(0), slot(1), slot(2), rows, rows, rows],
        [rows] * 4, (sh,) * 4, [], (part, slots, slots, slots, w, m, v), ("parallel",), hosted, prefetch=(place,))


def _small_sum_adamw(parts, w, m, v):
    nd, r, c = parts.shape

    def body(p_ref, w_ref, m_ref, v_ref, g_ref, d_ref, m2_ref, v2_ref):
        g = p_ref[0]
        for k in range(1, nd):
            g = g + p_ref[k]
        g_ref[...] = g
        d_ref[...], m2_ref[...], v2_ref[...] = _adamw_math(w_ref[...], g, m_ref[...], v_ref[...])

    sh = jax.ShapeDtypeStruct((r, c), F32)
    return pl.pallas_call(
        body, name="small_sum_adamw", out_shape=(sh, sh, sh, sh),
        compiler_params=pltpu.CompilerParams(vmem_limit_bytes=VMEM_LIMIT),
    )(parts, w, m, v)


def _place():
    x, y, c = lax.axis_index("x"), lax.axis_index("y"), lax.axis_index("c")
    chips = [(1 - x, y), (x, 1 - y), (1 - x, 1 - y)]
    return x, y, c, 2 * x + y, chips


def _remote(src, dst, send_sem, recv_sem, to):
    return pltpu.make_async_remote_copy(src_ref=src, dst_ref=dst, send_sem=send_sem, recv_sem=recv_sem,
                                        device_id=to, device_id_type=MESH)


class _Exchange:
    def __init__(self, ins, out_shapes, sem_shape, start, finish, middle=None, in_place=False):
        self.ins, self.out_shapes, self.sem_shape = tuple(ins), tuple(out_shapes), sem_shape
        self.start, self.finish, self.middle = start, finish, middle
        self.in_place = in_place


def _run_exchange(ex, name):
    n_in, n_out = len(ex.ins), len(ex.out_shapes)

    def body(*refs):
        ins, outs = refs[:n_in], refs[n_in:n_in + n_out]
        send_sems, recv_sems = refs[n_in + n_out:]
        ex.start(ins, outs, send_sems, recv_sems)
        if ex.middle is not None:
            ex.middle(ins, outs, send_sems, recv_sems)
        ex.finish(ins, outs, send_sems, recv_sems)

    sem = pltpu.SemaphoreType.DMA(ex.sem_shape)
    return pl.pallas_call(
        body, name=name, out_shape=ex.out_shapes, in_specs=[ANY] * n_in, out_specs=[ANY] * n_out,
        input_output_aliases={k: k for k in range(n_in)} if ex.in_place else {}, scratch_shapes=[sem, sem],
    )(*ex.ins)


def _host_call(body, name, grid, in_specs, out_specs, out_shape, scratch_shapes, args, semantics, hosted,
               prefetch=()):
    hosted = [] if hosted is None else (list(hosted) if isinstance(hosted, (list, tuple)) else [hosted])
    n_pre, n_in, n_out, n_scr = len(prefetch), len(in_specs), len(out_specs), len(scratch_shapes)
    h_ins = [a for ex in hosted for a in ex.ins]
    h_outs = [s for ex in hosted for s in ex.out_shapes]
    h_in, h_out = len(h_ins), len(h_outs)

    def wrapped(*refs):
        pre, refs = refs[:n_pre], refs[n_pre:]
        a, hi = refs[:n_in], refs[n_in:n_in + h_in]
        o = refs[n_in + h_in:n_in + h_in + n_out]
        ho = refs[n_in + h_in + n_out:n_in + h_in + n_out + h_out]
        scr = refs[n_in + h_in + n_out + h_out:]

        def run(phase):
            i0 = o0 = 0
            for k, ex in enumerate(hosted):
                fn = getattr(ex, phase)
                if fn is not None:
                    fn(hi[i0:i0 + len(ex.ins)], ho[o0:o0 + len(ex.out_shapes)], scr[n_scr + 2 * k],
                       scr[n_scr + 2 * k + 1])
                i0, o0 = i0 + len(ex.ins), o0 + len(ex.out_shapes)

        if hosted:
            first = functools.reduce(jnp.logical_and, [pl.program_id(k) == 0 for k in range(len(grid))])

            @pl.when(first)
            def _():
                run("start")

        if any(ex.middle is not None for ex in hosted):
            half_way = functools.reduce(jnp.logical_and, [
                pl.program_id(0) == max(1, grid[0] * MIDDLE_STEP_16THS // 16)] + [
                pl.program_id(k) == 0 for k in range(1, len(grid))])

            @pl.when(half_way)
            def _():
                run("middle")

        body(*pre, *a, *o, *scr[:n_scr])

        if hosted:
            last = functools.reduce(jnp.logical_and, [pl.program_id(k) == grid[k] - 1 for k in range(len(grid))])

            @pl.when(last)
            def _():
                run("finish")

    sems = [pltpu.SemaphoreType.DMA(ex.sem_shape) for ex in hosted for _ in range(2)]
    aliases, i0, o0 = {}, n_pre + n_in, n_out
    for ex in hosted:
        if ex.in_place:
            aliases.update({i0 + k: o0 + k for k in range(len(ex.ins))})
        i0, o0 = i0 + len(ex.ins), o0 + len(ex.out_shapes)
    all_in, all_out = list(in_specs) + [ANY] * h_in, list(out_specs) + [ANY] * h_out
    all_scr = list(scratch_shapes) + sems
    params = _params(*(["arbitrary"] * len(grid) if hosted else semantics))
    shapes = tuple(out_shape) + tuple(h_outs)
    if n_pre:
        call = pl.pallas_call(
            wrapped, name=name, out_shape=shapes, input_output_aliases=aliases, compiler_params=params,
            grid_spec=pltpu.PrefetchScalarGridSpec(num_scalar_prefetch=n_pre, grid=grid, in_specs=all_in,
                                                   out_specs=all_out, scratch_shapes=all_scr))
    else:
        call = pl.pallas_call(
            wrapped, name=name, grid=grid, in_specs=all_in, out_specs=all_out, out_shape=shapes,
            scratch_shapes=all_scr, input_output_aliases=aliases, compiler_params=params)
    res = call(*prefetch, *args, *h_ins)
    return res[:n_out], res[n_out:]


def _all_gather(shards, small=()):
    items = tuple(shards) + tuple(small)
    nw = len(shards)

    def place():
        x, y, c, me, _ = _place()
        first = (x + (1 - c) * (1 - 2 * x), y + c * (1 - 2 * y))
        second = (x + c * (1 - 2 * x), y + (1 - c) * (1 - 2 * y))
        diag = (1 - x, 1 - y)
        return x, y, c, me, (first, second, diag)

    def halves(w, c):
        rh = items[w].shape[0] // 2
        return pl.ds(c * rh, rh), pl.ds((1 - c) * rh, rh)

    def start(ins, outs, ss, rs):
        x, y, c, me, chips = place()
        for w in range(len(items)):
            _remote(ins[w], outs[w].at[me], ss.at[w, 6], rs.at[w, 6], (x, y, 1 - c)).start()
            if w < nw:
                mine, _ = halves(w, c)
                _remote(ins[w].at[mine], outs[w].at[me, mine], ss.at[w, 0], rs.at[w, 0], (*chips[0], c)).start()
            else:
                for k in range(3):
                    _remote(ins[w], outs[w].at[me], ss.at[w, k], rs.at[w, k], (*chips[k], c)).start()

    def onward(outs, ss, rs, w, k, x, y, c, chips):
        mine, _ = halves(w, c)
        pk = 2 * chips[k][0] + chips[k][1]
        got = outs[w].at[pk, mine]
        src = chips[1] if k == 2 else chips[k]
        _remote(got, got, ss.at[w, k], rs.at[w, k], (*src, c)).wait_recv()
        if k == 0:
            _remote(got, got, ss.at[w, 2], rs.at[w, 2], (*chips[1], c)).start()
        _remote(got, got, ss.at[w, 3 + k], rs.at[w, 3 + k], (x, y, 1 - c)).start()

    def middle(ins, outs, ss, rs):
        x, y, c, me, chips = place()
        for w in range(nw):
            mine, _ = halves(w, c)
            _remote(ins[w].at[mine], outs[w].at[me, mine], ss.at[w, 1], rs.at[w, 1], (*chips[1], c)).start()
        for w in range(nw):
            onward(outs, ss, rs, w, 0, x, y, c, chips)

    def finish(ins, outs, ss, rs):
        x, y, c, me, chips = place()
        sib = (x, y, 1 - c)
        for k in (1, 2):
            for w in range(nw):
                onward(outs, ss, rs, w, k, x, y, c, chips)
        for w in range(len(items)):
            if w < nw:
                mine, theirs = halves(w, c)
                for k, chip in ((3, chips[1]), (4, chips[0]), (5, chips[2])):
                    oth = outs[w].at[2 * chip[0] + chip[1], theirs]
                    _remote(oth, oth, ss.at[w, k], rs.at[w, k], sib).wait_recv()
                own = ins[w].at[mine]
                for k in range(6):
                    _remote(own, own, ss.at[w, k], rs.at[w, k], sib).wait_send()
            else:
                for k in range(3):
                    got = outs[w].at[2 * chips[k][0] + chips[k][1]]
                    _remote(got, got, ss.at[w, k], rs.at[w, k], (*chips[k], c)).wait_recv()
                    _remote(ins[w], ins[w], ss.at[w, k], rs.at[w, k], sib).wait_send()
            _remote(ins[w], outs[w].at[me], ss.at[w, 6], rs.at[w, 6], sib).wait()

    out_shapes = tuple(jax.ShapeDtypeStruct((N_CHIPS,) + a.shape, a.dtype) for a in items)
    return _Exchange(items, out_shapes, (len(items), 7), start, finish, middle if nw else None)


def _exchange_halves(grads):
    nw = len(grads)

    def copies(ins, outs, ss, rs):
        x, y, c, _, _ = _place()
        res = []
        for w in range(nw):
            rh = grads[w].shape[1] // 2
            res.append(_remote(ins[w].at[:, pl.ds((1 - c) * rh, rh), :], outs[w], ss.at[w], rs.at[w], (x, y, 1 - c)))
        return res

    def start(ins, outs, ss, rs):
        for cp in copies(ins, outs, ss, rs):
            cp.start()

    def finish(ins, outs, ss, rs):
        for cp in copies(ins, outs, ss, rs):
            cp.wait()

    halves = tuple(jax.ShapeDtypeStruct((g.shape[0], g.shape[1] // 2, g.shape[2]), g.dtype) for g in grads)
    return _Exchange(grads, halves, (nw,), start, finish)


def _scatter_partials(parts):
    nw = len(parts)

    def copies(ins, outs, ss, rs):
        _, _, c, _, chips = _place()
        res = []
        for r, (px, py) in enumerate(chips):
            for w in range(nw):
                rh = parts[w].shape[1] // 2
                rows = pl.ds(c * rh, rh)
                res.append(_remote(ins[w].at[2 * px + py, rows], outs[w].at[r, rows], ss.at[w, r], rs.at[w, r],
                                   (px, py, c)))
        return res

    def start(ins, outs, ss, rs):
        for cp in copies(ins, outs, ss, rs):
            cp.start()

    def finish(ins, outs, ss, rs):
        for cp in copies(ins, outs, ss, rs):
            cp.wait()

    out_shapes = tuple(jax.ShapeDtypeStruct((3,) + p.shape[1:], p.dtype) for p in parts)
    return _Exchange(parts, out_shapes, (nw, 3), start, finish)


def _join_partials(parts, slots):
    nw = len(parts)

    def copies(outs, ss, rs, mine):
        x, y, c, me, _ = _place()
        res = []
        for w in range(nw):
            rh = parts[w].shape[1] // 2
            rows = pl.ds((c if mine else 1 - c) * rh, rh)
            own = outs[w].at[me, rows]
            got = outs[nw + w].at[:, rows, :]
            res.append(_remote(own, own, ss.at[w, 0], rs.at[w, 0], (x, y, 1 - c)))
            res.append(_remote(got, got, ss.at[w, 1], rs.at[w, 1], (x, y, 1 - c)))
        return res

    def start(ins, outs, ss, rs):
        for cp in copies(outs, ss, rs, True):
            cp.start()

    def finish(ins, outs, ss, rs):
        for cp in copies(outs, ss, rs, True):
            cp.wait_send()
        for cp in copies(outs, ss, rs, False):
            cp.wait_recv()

    arrays = tuple(parts) + tuple(slots)
    return _Exchange(arrays, tuple(jax.ShapeDtypeStruct(a.shape, a.dtype) for a in arrays), (nw, 2), start, finish,
                     in_place=True)


def _gather_small(slab):
    def copies(ins, outs, ss, rs):
        x, y, c, _, _ = _place()
        me = 4 * x + 2 * y + c
        own = pltpu.make_async_copy(ins[0], outs[0].at[me], ss.at[7])
        out, arrivals = [], []
        for k in range(1, 8):
            px = 1 - x if k & 4 else x
            py = 1 - y if k & 2 else y
            pc = 1 - c if k & 1 else c
            out.append(_remote(ins[0], outs[0].at[me], ss.at[k - 1], rs.at[k - 1], (px, py, pc)))
            theirs = outs[0].at[4 * px + 2 * py + pc]
            arrivals.append(_remote(theirs, theirs, ss.at[k - 1], rs.at[k - 1], (px, py, pc)))
        return own, out, arrivals

    def start(ins, outs, ss, rs):
        own, out, _ = copies(ins, outs, ss, rs)
        own.start()
        for cp in out:
            cp.start()

    def finish(ins, outs, ss, rs):
        own, out, arrivals = copies(ins, outs, ss, rs)
        for cp in out:
            cp.wait_send()
        for cp in arrivals:
            cp.wait_recv()
        own.wait()

    return _Exchange((slab,), (jax.ShapeDtypeStruct((8,) + slab.shape, slab.dtype),), (8,), start, finish)


_SMALL_VECS = ("ln_mix_g", "ln_attn_g", "ln_mem_g", "ln_ffn_g", "ln_final_g")


def _pack_small(p, extra, conv):
    d = p["ln_mix_g"].shape[-1]
    top = [p[k].reshape(1, d) for k in _SMALL_VECS]
    top.append(jnp.concatenate([p["sgu_ln_g"].reshape(-1), p["sgu_ln_b"].reshape(-1)]).reshape(1, d))
    top.append(jnp.concatenate([p["grp_norm_a"].reshape(-1), p["grp_norm_b"].reshape(-1)]).reshape(1, d))
    top.append(jnp.concatenate([p["b_spatial"].reshape(-1), extra]).reshape(1, d))
    mid = jnp.zeros((8, d), F32)
    if conv is not None:
        mid = jnp.pad(conv, ((0, 5), (0, d - conv.shape[1])))
    return jnp.concatenate([jnp.concatenate(top, axis=0), mid, p["w_spatial"].reshape(-1, d)], axis=0)


def _unpack_small(slab):
    d = slab.shape[1]
    hw = d // 2
    out = {k: slab[i] for i, k in enumerate(_SMALL_VECS)}
    out["sgu_ln_g"], out["sgu_ln_b"] = slab[5, :hw], slab[5, hw:]
    out["grp_norm_a"], out["grp_norm_b"] = slab[6, :hw], slab[6, hw:]
    out["b_spatial"] = slab[7, :hw].reshape(HEADS, CHUNK)
    out["w_spatial"] = slab[16:].reshape(HEADS, CHUNK, CHUNK)
    return out


_BIG = ("w_in", "w_kv", "w_gate_up", "w_out", "w_q", "w_o", "w_down")
_WEIGHTS = ("ln_mix_g", "w_in", "sgu_ln_g", "sgu_ln_b", "w_spatial", "b_spatial", "conv_w", "grp_norm_a",
            "grp_norm_b", "w_out", "ln_attn_g", "ln_mem_g", "w_q", "w_kv", "w_o", "ln_ffn_g", "w_gate_up",
            "w_down", "ln_final_g")


def _step(p, m_, v_, x, mem, target):
    s, d = x.shape
    hw = d // 2
    row = lambda a: a.reshape(1, -1)
    x_, y_, c_ = lax.axis_index("x"), lax.axis_index("y"), lax.axis_index("c")
    chip = 2 * x_ + y_

    bf = {k: p[k].astype(BF16) for k in _BIG}
    conv8 = jnp.pad(p["conv_w"], ((0, 5), (0, 0)))
    w_in, w_out4, conv4 = _run_exchange(_all_gather([bf["w_in"], bf["w_out"]], [conv8]), "all_gather_mixer")
    cw = jnp.transpose(conv4[:, :3, :], (1, 0, 2)).reshape(3, hw)
    b_t = jnp.pad(jnp.transpose(p["b_spatial"]), ((0, 0), (0, CHUNK - HEADS)))
    g1, g2, gm, g3, gf = (row(p[k]) for k in _SMALL_VECS)
    lng, lnb, ga, gb = row(p["sgu_ln_g"]), row(p["sgu_ln_b"]), row(p["grp_norm_a"]), row(p["grp_norm_b"])
    wsp = p["w_spatial"]
    w_out = w_out4.reshape(-1, d)

    (h, x1, ycat, xn1), (w_kv, w_q4, w_o4, w_down4) = _mixer_fwd(
        x, g1, w_in, lng, lnb, wsp, b_t, cw, ga, gb, w_out,
        hosted=_all_gather([bf[k] for k in ("w_kv", "w_q", "w_o", "w_down")]))
    w_q, w_o, w_down = (a.reshape(-1, d) for a in (w_q4, w_o4, w_down4))
    memn, kv = _kv_fwd(mem, gm, w_kv)
    (x2, o, xn2), (w_gu,) = _attn_fwd(x1, g2, w_q, kv, w_o, hosted=_all_gather([bf["w_gate_up"]]))
    dx2, act, dgu, xn3, dx3, loss, dgf, dg3, dx2b = _ffn_fwd_bwd(x2, g3, gf, target, w_gu, w_down)

    place = jnp.stack([c_, chip]).astype(jnp.int32)

    def chip_partials(names, grads, tag):
        got = _run_exchange(_exchange_halves(grads), "rs_exchange_" + tag)
        return [_chip_sum(g, b, place, "chip_sum_" + k) for k, g, b in zip(names, grads, got)]

    names_d = ("w_down",)
    parts_d = chip_partials(names_d, (_weight_grad(act, dx3, "grad_w_down", 1408, 512, False)[0],), "down")
    g_gu, slots_d = _weight_grad(xn3, dgu, "grad_w_gate_up", 512, 1408, True, hosted=_scatter_partials(parts_d))
    names_a = ("w_gate_up", "w_o")
    parts_a = chip_partials(names_a, (g_gu, _weight_grad(o, dx2b, "grad_w_o", 1024, 512, False)[0]), "ffn")
    (dx1, dq, dkv, dg2, dx1b), slots_a = _attn_bwd(x1, dx2, g2, w_q, kv, w_o, hosted=_scatter_partials(parts_a))
    g_kv, dgm = _kv_bwd(dkv, mem, memn, gm, w_kv)
    names_b = ("w_out", "w_q", "w_kv")
    parts_b = chip_partials(names_b, (_weight_grad(ycat, dx1b, "grad_w_out", 1024, 512, False)[0],
                                      _weight_grad(xn2, dq, "grad_w_q", 1024, 512, False)[0], g_kv), "attn")
    (dx, dh, dg1, dlng, dlnb, dwsp, dbt, dcw, dga, dgb), slots_b = _mixer_bwd(
        x, dx1, h, g1, lng, lnb, wsp, b_t, cw, ga, gb, w_out, w_in, hosted=_scatter_partials(parts_b))
    small = {"ln_mix_g": dg1, "ln_attn_g": dg2, "ln_mem_g": dgm, "ln_ffn_g": dg3, "ln_final_g": dgf,
             "sgu_ln_g": dlng, "sgu_ln_b": dlnb, "grp_norm_a": dga, "grp_norm_b": dgb,
             "b_spatial": jnp.transpose(dbt[:, :HEADS]), "w_spatial": dwsp}
    loss_vec = jnp.pad(loss.reshape(1), (0, hw - 1))
    names = names_d + names_a + names_b
    n_done = len(names)
    g_in, extra = _weight_grad(
        xn1, dh, "grad_w_in", 1024, 640, True,
        hosted=[_gather_small(_pack_small(small, loss_vec, dcw)),
                _join_partials(parts_d + parts_a + parts_b, slots_d + slots_a + slots_b)])
    parts, whole = extra[0], dict(zip(names, zip(extra[1:1 + n_done], extra[1 + n_done:])))
    (part_in,) = chip_partials(("w_in",), (g_in,), "mixer")
    out_g, out_d, out_m, out_v = {}, {}, {}, {}

    def finalize(k, hosted=None):
        (out_g[k], out_d[k], out_m[k], out_v[k]), res = _finalize(
            whole[k][0], whole[k][1], p[k], m_[k], v_[k], place, "finalize_" + k, hosted)
        return res

    (slots_in,) = finalize("w_gate_up", _scatter_partials([part_in]))
    for k in names:
        if k != "w_gate_up":
            finalize(k)
    whole["w_in"] = _run_exchange(_join_partials([part_in], [slots_in]), "rs_join_mixer")
    finalize("w_in")

    zeros = jnp.zeros((hw,), F32)
    sg, sd, sm, sv = _small_sum_adamw(parts, _pack_small(p, zeros, None), _pack_small(m_, zeros, None),
                                      _pack_small(v_, zeros, None))
    for tree, slab in zip((out_g, out_d, out_m, out_v), (sg, sd, sm, sv)):
        tree.update(_unpack_small(slab))
    loss_out = sg[7, hw]
    g_conv = lax.dynamic_slice(sg[8:11, :hw], (0, chip * (hw // N_CHIPS)), (3, hw // N_CHIPS))
    out_g["conv_w"] = g_conv
    out_d["conv_w"], out_m["conv_w"], out_v["conv_w"] = _adamw(p["conv_w"], g_conv, m_["conv_w"], v_["conv_w"],
                                                                "adamw_conv_w")
    return loss_out, dx, out_g, out_d, out_m, out_v


def kernel(x, mem, ln_mix_g, w_in, sgu_ln_g, sgu_ln_b, w_spatial, b_spatial, conv_w, grp_norm_a, grp_norm_b, w_out, ln_attn_g, ln_mem_g, w_q, w_kv, w_o, ln_ffn_g, w_gate_up, w_down, ln_final_g, loss_target, m_ln_mix_g, m_w_in, m_sgu_ln_g, m_sgu_ln_b, m_w_spatial, m_b_spatial, m_conv_w, m_grp_norm_a, m_grp_norm_b, m_w_out, m_ln_attn_g, m_ln_mem_g, m_w_q, m_w_kv, m_w_o, m_ln_ffn_g, m_w_gate_up, m_w_down, m_ln_final_g, v_ln_mix_g, v_w_in, v_sgu_ln_g, v_sgu_ln_b, v_w_spatial, v_b_spatial, v_conv_w, v_grp_norm_a, v_grp_norm_b, v_w_out, v_ln_attn_g, v_ln_mem_g, v_w_q, v_w_kv, v_w_o, v_ln_ffn_g, v_w_gate_up, v_w_down, v_ln_final_g):
    p = dict(ln_mix_g=ln_mix_g, w_in=w_in, sgu_ln_g=sgu_ln_g, sgu_ln_b=sgu_ln_b, w_spatial=w_spatial,
             b_spatial=b_spatial, conv_w=conv_w, grp_norm_a=grp_norm_a, grp_norm_b=grp_norm_b, w_out=w_out,
             ln_attn_g=ln_attn_g, ln_mem_g=ln_mem_g, w_q=w_q, w_kv=w_kv, w_o=w_o, ln_ffn_g=ln_ffn_g,
             w_gate_up=w_gate_up, w_down=w_down, ln_final_g=ln_final_g)
    m_ = dict(ln_mix_g=m_ln_mix_g, w_in=m_w_in, sgu_ln_g=m_sgu_ln_g, sgu_ln_b=m_sgu_ln_b, w_spatial=m_w_spatial,
              b_spatial=m_b_spatial, conv_w=m_conv_w, grp_norm_a=m_grp_norm_a, grp_norm_b=m_grp_norm_b,
              w_out=m_w_out, ln_attn_g=m_ln_attn_g, ln_mem_g=m_ln_mem_g, w_q=m_w_q, w_kv=m_w_kv, w_o=m_w_o,
              ln_ffn_g=m_ln_ffn_g, w_gate_up=m_w_gate_up, w_down=m_w_down, ln_final_g=m_ln_final_g)
    v_ = dict(ln_mix_g=v_ln_mix_g, w_in=v_w_in, sgu_ln_g=v_sgu_ln_g, sgu_ln_b=v_sgu_ln_b, w_spatial=v_w_spatial,
              b_spatial=v_b_spatial, conv_w=v_conv_w, grp_norm_a=v_grp_norm_a, grp_norm_b=v_grp_norm_b,
              w_out=v_w_out, ln_attn_g=v_ln_attn_g, ln_mem_g=v_ln_mem_g, w_q=v_w_q, w_kv=v_w_kv, w_o=v_w_o,
              ln_ffn_g=v_ln_ffn_g, w_gate_up=v_w_gate_up, w_down=v_w_down, ln_final_g=v_ln_final_g)
    s, d = x.shape[-2], x.shape[-1]
    loss, dx, g, dl, nm, nv = _step(p, m_, v_, x.reshape(s, d), mem.reshape(-1, d), loss_target.reshape(s, d))
    outs = [loss, dx.reshape(x.shape)]
    for tree in (g, dl, nm, nv):
        outs += [tree[k].reshape(p[k].shape) for k in _WEIGHTS]
    return tuple(outs)
```

```python
import functools
import math

import jax
import jax.numpy as jnp
from jax import lax
from jax.experimental import pallas as pl
from jax.experimental.pallas import tpu as pltpu

F32 = jnp.float32
BF16 = jnp.bfloat16
EPS = 1e-6
CHUNK = 128
HEADS = 4
N_CHIPS = 4
TM = 512
TM_ATTN = 512
TM_FFN = 256
ADAM_LR, ADAM_B1, ADAM_B2, ADAM_EPS, ADAM_WD, ADAM_STEP = 0.001, 0.9, 0.999, 1e-08, 0.01, 10
GELU_C = math.sqrt(2.0 / math.pi)
GELU_K = 0.044715
SMALL_ROWS = 80
VMEM_LIMIT = 56 * 1024 * 1024
MIDDLE_STEP_16THS = 5
MESH = pl.DeviceIdType.MESH
ANY = pl.BlockSpec(memory_space=pl.ANY)


def _params(*sem):
    return pltpu.CompilerParams(dimension_semantics=sem, vmem_limit_bytes=VMEM_LIMIT)


def _dot(a, b):
    return jnp.dot(a, b, preferred_element_type=F32)


def _dot_nt(a, b):
    return lax.dot_general(a, b, (((1,), (1,)), ((), ())), preferred_element_type=F32)


def _dot_tn(a, b):
    return lax.dot_general(a, b, (((0,), (0,)), ((), ())), preferred_element_type=F32)


def _rms_fwd(x, g):
    r = lax.rsqrt(jnp.mean(x * x, axis=-1, keepdims=True) + EPS)
    xh = x * r
    return xh * g, xh, r


def _rms_bwd(dy, xh, r, g):
    dxh = dy * g
    dx = r * (dxh - xh * jnp.mean(dxh * xh, axis=-1, keepdims=True))
    return dx, jnp.sum(dy * xh, axis=0, keepdims=True)


def _full(shape):
    nd = len(shape)
    return pl.BlockSpec(shape, lambda *_: (0,) * nd, pipeline_mode=pl.Buffered(1))


def _acc(shape):
    nd = len(shape)
    return pl.BlockSpec(shape, lambda *_: (0,) * nd)


def _rows(tm, cols):
    return pl.BlockSpec((tm, cols), lambda i: (i, 0))


def _tril_weights(wsp_ref):
    row = lax.broadcasted_iota(jnp.int32, (CHUNK, CHUNK), 0)
    col = lax.broadcasted_iota(jnp.int32, (CHUNK, CHUNK), 1)
    return [jnp.where(row >= col, wsp_ref[hd], 0.0).astype(BF16) for hd in range(HEADS)]


def _shift_rows(z, zp):
    row = lax.broadcasted_iota(jnp.int32, z.shape, 0)
    zm1 = jnp.where(row == 0, zp[7:8, :], pltpu.roll(z, 1, 0))
    zm2 = jnp.where(row == 0, zp[6:7, :], jnp.where(row == 1, zp[7:8, :], pltpu.roll(z, 2, 0)))
    return zm1, zm2


def _gelu_parts(x):
    t = jnp.tanh(GELU_C * (x + GELU_K * (x * x * x)))
    return 0.5 * x * (1.0 + t), t


def _layer_norm_parts(v, g, b):
    mu = jnp.mean(v, axis=-1, keepdims=True)
    vc = v - mu
    rs = lax.rsqrt(jnp.mean(vc * vc, axis=-1, keepdims=True) + EPS)
    vhat = vc * rs
    return vhat * g + b, vhat, rs


def _kv_fwd(mem, g_mem, w_kv):
    m, d = mem.shape
    ns = w_kv.shape[2]

    def body(mem_ref, g_ref, w_ref, memn_ref, kv_ref):
        y, _, _ = _rms_fwd(mem_ref[...], g_ref[...])
        yb = y.astype(BF16)
        memn_ref[...] = yb
        for k in range(N_CHIPS):
            kv_ref[:, k * ns:(k + 1) * ns] = _dot(yb, w_ref[k]).astype(BF16)

    return pl.pallas_call(
        body, name="kv_fwd",
        out_shape=(jax.ShapeDtypeStruct((m, d), BF16), jax.ShapeDtypeStruct((m, N_CHIPS * ns), BF16)),
        compiler_params=pltpu.CompilerParams(vmem_limit_bytes=VMEM_LIMIT),
    )(mem, g_mem, w_kv)


def _mixer_fwd(x, g1, w_in, lng, lnb, wsp, b_t, cw, ga, gb, w_out, hosted=None):
    s, d = x.shape
    n = s // TM
    nch = TM // CHUNK
    ns = w_in.shape[2]
    nh = N_CHIPS * ns
    aw = d // 2
    hd_w = aw // HEADS

    def body(x_ref, g1_ref, win_ref, lng_ref, lnb_ref, wsp_ref, bt_ref, cw_ref, ga_ref, gb_ref, wout_ref,
             h_ref, x1_ref, y_ref, xn_ref, mix_ref, zp_ref):
        i = pl.program_id(0)

        @pl.when(i == 0)
        def _():
            zp_ref[...] = jnp.zeros_like(zp_ref)

        x = x_ref[...]
        xn, _, _ = _rms_fwd(x, g1_ref[...])
        xnb = xn.astype(BF16)
        xn_ref[...] = xnb
        for k in range(N_CHIPS):
            h_ref[:, k * ns:(k + 1) * ns] = _dot(xnb, win_ref[k])
        a, _ = _gelu_parts(h_ref[:, 0:2 * aw])
        u = a[:, :aw]
        vn, _, _ = _layer_norm_parts(a[:, aw:], lng_ref[...], lnb_ref[...])
        vnb = vn.astype(BF16)
        wm = _tril_weights(wsp_ref)
        for c in range(nch):
            for hd in range(HEADS):
                blk = vnb[c * CHUNK:(c + 1) * CHUNK, hd * hd_w:(hd + 1) * hd_w]
                mix_ref[c * CHUNK:(c + 1) * CHUNK, hd * hd_w:(hd + 1) * hd_w] = _dot(wm[hd], blk) + bt_ref[:, hd:hd + 1]
        ya, _, _ = _rms_fwd(u * mix_ref[...], ga_ref[...])
        g_b = h_ref[:, 2 * aw:3 * aw]
        z = h_ref[:, 3 * aw:4 * aw] * h_ref[:, 4 * aw:5 * aw]
        zm1, zm2 = _shift_rows(z, zp_ref[...])
        conv = cw_ref[0:1, :] * zm2 + cw_ref[1:2, :] * zm1 + cw_ref[2:3, :] * z
        yb, _, _ = _rms_fwd(g_b * conv, gb_ref[...])
        zp_ref[...] = z[TM - 8:TM, :]
        ycat = jnp.concatenate([ya, yb], axis=-1).astype(BF16)
        y_ref[...] = ycat
        x1_ref[...] = x + _dot(ycat, wout_ref[...])

    return _host_call(
        body, "mixer_fwd", (n,),
        [_rows(TM, d), _full(g1.shape), _full(w_in.shape), _full(lng.shape), _full(lnb.shape),
         _full(wsp.shape), _full(b_t.shape), _full(cw.shape), _full(ga.shape), _full(gb.shape),
         _full(w_out.shape)],
        [_rows(TM, nh), _rows(TM, d), _rows(TM, d), _rows(TM, d)],
        (jax.ShapeDtypeStruct((s, nh), F32), jax.ShapeDtypeStruct((s, d), F32),
         jax.ShapeDtypeStruct((s, d), BF16), jax.ShapeDtypeStruct((s, d), BF16)),
        [pltpu.VMEM((TM, aw), F32), pltpu.VMEM((8, aw), F32)],
        (x, g1, w_in, lng, lnb, wsp, b_t, cw, ga, gb, w_out), ("arbitrary",), hosted)


def _attn_probs(qb, kv_ref, hd, dh, d, scale):
    kh = kv_ref[:, hd * dh:(hd + 1) * dh]
    vh = kv_ref[:, d + hd * dh:d + (hd + 1) * dh]
    sc = _dot_nt(qb, kh) * scale
    sc = sc - jnp.max(sc, axis=-1, keepdims=True)
    e = jnp.exp(sc)
    return e / jnp.sum(e, axis=-1, keepdims=True), kh, vh


def _attn_fwd(x1, g2, w_q, kv, w_o, hosted=None):
    s, d = x1.shape
    tm = min(TM_ATTN, s)
    n = s // tm
    dh = d // HEADS
    scale = dh ** -0.5

    def body(x1_ref, g2_ref, wq_ref, kv_ref, wo_ref, x2_ref, o_ref, xn_ref):
        x1v = x1_ref[...]
        xn, _, _ = _rms_fwd(x1v, g2_ref[...])
        xnb = xn.astype(BF16)
        xn_ref[...] = xnb
        q = _dot(xnb, wq_ref[...])
        for hd in range(HEADS):
            qb = q[:, hd * dh:(hd + 1) * dh].astype(BF16)
            p, _, vh = _attn_probs(qb, kv_ref, hd, dh, d, scale)
            o_ref[:, hd * dh:(hd + 1) * dh] = _dot(p.astype(BF16), vh).astype(BF16)
        x2_ref[...] = x1v + _dot(o_ref[...], wo_ref[...])

    return _host_call(
        body, "attn_fwd", (n,),
        [_rows(tm, d), _full(g2.shape), _full(w_q.shape), _full(kv.shape), _full(w_o.shape)],
        [_rows(tm, d), _rows(tm, d), _rows(tm, d)],
        (jax.ShapeDtypeStruct((s, d), F32), jax.ShapeDtypeStruct((s, d), BF16), jax.ShapeDtypeStruct((s, d), BF16)),
        [], (x1, g2, w_q, kv, w_o), ("parallel",), hosted)


def _ffn_fwd_bwd(x2, g3, gf, target, w_gu, w_down):
    s, d = x2.shape
    tm = min(TM_FFN, s)
    n = s // tm
    ns = w_gu.shape[2]
    ff = 2 * ns

    def body(x2_ref, g3_ref, gf_ref, t_ref, wgu_ref, wd_ref,
             dx2_ref, act_ref, dgu_ref, xn_ref, dx3_ref, loss_ref, dgf_ref, dg3_ref, dx2b_ref):
        i = pl.program_id(0)

        @pl.when(i == 0)
        def _():
            loss_ref[...] = jnp.zeros_like(loss_ref)
            dgf_ref[...] = jnp.zeros_like(dgf_ref)
            dg3_ref[...] = jnp.zeros_like(dg3_ref)

        x2v = x2_ref[...]
        xn, xh3, r3 = _rms_fwd(x2v, g3_ref[...])
        xnb = xn.astype(BF16)
        xn_ref[...] = xnb
        x3 = x2v
        saved = []
        for j in range(2):
            g = _dot(xnb, wgu_ref[j])
            u = _dot(xnb, wgu_ref[2 + j])
            sg = 1.0 / (1.0 + jnp.exp(-g))
            sl = g * sg
            actb = (sl * u).astype(BF16)
            act_ref[:, j * ns:(j + 1) * ns] = actb
            x3 = x3 + _dot(actb, wd_ref[j * ns:(j + 1) * ns, :])
            saved.append((u, sl, sg * (1.0 + g * (1.0 - sg))))
        gfv = gf_ref[...]
        y, xhf, rf = _rms_fwd(x3, gfv)
        e = y - t_ref[...]
        loss_ref[...] += 0.5 * jnp.sum(jnp.sum(e * e, axis=-1, keepdims=True), axis=0, keepdims=True) / d
        dx3, dgf = _rms_bwd(e / d, xhf, rf, gfv)
        dgf_ref[...] += dgf
        dx3b = dx3.astype(BF16)
        dx3_ref[...] = dx3b
        dxn = jnp.zeros_like(x2v)
        for j in range(2):
            u, sl, dsl = saved[j]
            dact = _dot_nt(dx3b, wd_ref[j * ns:(j + 1) * ns, :])
            dgb = (dact * u * dsl).astype(BF16)
            dub = (dact * sl).astype(BF16)
            dgu_ref[:, j * ns:(j + 1) * ns] = dgb
            dgu_ref[:, ff + j * ns:ff + (j + 1) * ns] = dub
            dxn = dxn + _dot_nt(dgb, wgu_ref[j]) + _dot_nt(dub, wgu_ref[2 + j])
        dxr, dg3 = _rms_bwd(dxn, xh3, r3, g3_ref[...])
        dg3_ref[...] += dg3
        dx2 = dx3 + dxr
        dx2_ref[...] = dx2
        dx2b_ref[...] = dx2.astype(BF16)

    vec = jax.ShapeDtypeStruct((1, d), F32)
    return pl.pallas_call(
        body, name="ffn_fwd_bwd", grid=(n,),
        in_specs=[_rows(tm, d), _full(g3.shape), _full(gf.shape), _rows(tm, d), _full(w_gu.shape),
                  _full(w_down.shape)],
        out_specs=[_rows(tm, d), _rows(tm, ff), _rows(tm, 2 * ff), _rows(tm, d), _rows(tm, d),
                   _acc((1, 1)), _acc((1, d)), _acc((1, d)), _rows(tm, d)],
        out_shape=(jax.ShapeDtypeStruct((s, d), F32), jax.ShapeDtypeStruct((s, ff), BF16),
                   jax.ShapeDtypeStruct((s, 2 * ff), BF16), jax.ShapeDtypeStruct((s, d), BF16),
                   jax.ShapeDtypeStruct((s, d), BF16), jax.ShapeDtypeStruct((1, 1), F32), vec, vec,
                   jax.ShapeDtypeStruct((s, d), BF16)),
        compiler_params=_params("arbitrary"),
    )(x2, g3, gf, target, w_gu, w_down)


def _attn_bwd(x1, dx2, g2, w_q, kv, w_o, hosted=None):
    s, d = x1.shape
    tm = min(TM_ATTN, s)
    n = s // tm
    dh = d // HEADS
    scale = dh ** -0.5
    m = kv.shape[0]

    def body(x1_ref, dx2_ref, g2_ref, wq_ref, kv_ref, wo_ref, dx1_ref, dq_ref, dkv_ref, dg2_ref, dx1b_ref):
        i = pl.program_id(0)

        @pl.when(i == 0)
        def _():
            dkv_ref[...] = jnp.zeros_like(dkv_ref)
            dg2_ref[...] = jnp.zeros_like(dg2_ref)

        xn, xh2, r2 = _rms_fwd(x1_ref[...], g2_ref[...])
        q = _dot(xn.astype(BF16), wq_ref[...])
        dx2v = dx2_ref[...]
        do = _dot_nt(dx2v.astype(BF16), wo_ref[...])
        for hd in range(HEADS):
            qb = q[:, hd * dh:(hd + 1) * dh].astype(BF16)
            p, kh, vh = _attn_probs(qb, kv_ref, hd, dh, d, scale)
            dob = do[:, hd * dh:(hd + 1) * dh].astype(BF16)
            dp = _dot_nt(dob, vh)
            ds = p * (dp - jnp.sum(dp * p, axis=-1, keepdims=True))
            dsb = (ds * scale).astype(BF16)
            dq_ref[:, hd * dh:(hd + 1) * dh] = _dot(dsb, kh).astype(BF16)
            dkv_ref[:, hd * dh:(hd + 1) * dh] += _dot_tn(dsb, qb)
            dkv_ref[:, d + hd * dh:d + (hd + 1) * dh] += _dot_tn(p.astype(BF16), dob)
        dxn = _dot_nt(dq_ref[...], wq_ref[...])
        dxr, dg2 = _rms_bwd(dxn, xh2, r2, g2_ref[...])
        dg2_ref[...] += dg2
        dx1 = dx2v + dxr
        dx1_ref[...] = dx1
        dx1b_ref[...] = dx1.astype(BF16)

    return _host_call(
        body, "attn_bwd", (n,),
        [_rows(tm, d), _rows(tm, d), _full(g2.shape), _full(w_q.shape), _full(kv.shape), _full(w_o.shape)],
        [_rows(tm, d), _rows(tm, d), _acc((m, 2 * d)), _acc((1, d)), _rows(tm, d)],
        (jax.ShapeDtypeStruct((s, d), F32), jax.ShapeDtypeStruct((s, d), BF16),
         jax.ShapeDtypeStruct((m, 2 * d), F32), jax.ShapeDtypeStruct((1, d), F32),
         jax.ShapeDtypeStruct((s, d), BF16)),
        [], (x1, dx2, g2, w_q, kv, w_o), ("arbitrary",), hosted)


def _kv_bwd(dkv, mem, memn, g_mem, w_kv):
    m, d = mem.shape
    ns = w_kv.shape[2]

    def body(dkv_ref, mem_ref, memn_ref, g_ref, w_ref, gw_ref, dg_ref):
        _, xh, _ = _rms_fwd(mem_ref[...], g_ref[...])
        dmemn = jnp.zeros((m, d), F32)
        for k in range(N_CHIPS):
            dkb = dkv_ref[:, k * ns:(k + 1) * ns].astype(BF16)
            gw_ref[k] = _dot_tn(memn_ref[...], dkb).astype(BF16)
            dmemn = dmemn + _dot_nt(dkb, w_ref[k])
        dg_ref[...] = jnp.sum(dmemn * xh, axis=0, keepdims=True)

    return pl.pallas_call(
        body, name="kv_bwd",
        out_shape=(jax.ShapeDtypeStruct((N_CHIPS, d, ns), BF16), jax.ShapeDtypeStruct((1, d), F32)),
        compiler_params=pltpu.CompilerParams(vmem_limit_bytes=VMEM_LIMIT),
    )(dkv, mem, memn, g_mem, w_kv)


def _mixer_bwd(x, dx1, h, g1, lng, lnb, wsp, b_t, cw, ga, gb, w_out, w_in, hosted=None):
    s, d = x.shape
    n = s // TM
    nch = TM // CHUNK
    ns = w_in.shape[2]
    nh = N_CHIPS * ns
    aw = d // 2
    hd_w = aw // HEADS

    def rev(cols):
        return pl.BlockSpec((TM, cols), lambda i: (n - 1 - i, 0))

    hprev = pl.BlockSpec((8, nh), lambda i: (jnp.maximum((n - 1 - i) * (TM // 8) - 1, 0), 0))

    def body(x_ref, dx1_ref, h_ref, hp_ref, g1_ref, lng_ref, lnb_ref, wsp_ref, bt_ref, cw_ref, ga_ref, gb_ref,
             wout_ref, win_ref,
             dx_ref, dh_ref, dg1_ref, dlng_ref, dlnb_ref, dwsp_ref, dbt_ref, dcw_ref, dga_ref, dgb_ref,
             mix_ref, dvn_ref, dcn_ref):
        i = pl.program_id(0)

        @pl.when(i == 0)
        def _():
            for r in (dg1_ref, dlng_ref, dlnb_ref, dwsp_ref, dbt_ref, dcw_ref, dga_ref, dgb_ref, dcn_ref):
                r[...] = jnp.zeros_like(r)

        dx1v = dx1_ref[...]
        dycat = _dot_nt(dx1v.astype(BF16), wout_ref[...])
        ha = h_ref[:, 0:2 * aw]
        a, th = _gelu_parts(ha)
        u = a[:, :aw]
        lngv = lng_ref[...]
        vn, vhat, rs = _layer_norm_parts(a[:, aw:], lngv, lnb_ref[...])
        vnb = vn.astype(BF16)
        wm = _tril_weights(wsp_ref)
        for c in range(nch):
            for hd in range(HEADS):
                blk = vnb[c * CHUNK:(c + 1) * CHUNK, hd * hd_w:(hd + 1) * hd_w]
                mix_ref[c * CHUNK:(c + 1) * CHUNK, hd * hd_w:(hd + 1) * hd_w] = _dot(wm[hd], blk) + bt_ref[:, hd:hd + 1]
        mixed = mix_ref[...]
        gav = ga_ref[...]
        _, yah, ra = _rms_fwd(u * mixed, gav)
        dya, dga = _rms_bwd(dycat[:, :aw], yah, ra, gav)
        dga_ref[...] += dga
        du = dya * mixed
        dmix = dya * u
        dmb = dmix.astype(BF16)
        tri = lax.broadcasted_iota(jnp.int32, (CHUNK, CHUNK), 0) >= lax.broadcasted_iota(jnp.int32, (CHUNK, CHUNK), 1)
        for hd in range(HEADS):
            dw = jnp.zeros((CHUNK, CHUNK), F32)
            db = jnp.zeros((CHUNK, 1), F32)
            for c in range(nch):
                rows = slice(c * CHUNK, (c + 1) * CHUNK)
                cols = slice(hd * hd_w, (hd + 1) * hd_w)
                dvn_ref[rows, cols] = _dot_tn(wm[hd], dmb[rows, cols])
                dw = dw + _dot_nt(dmb[rows, cols], vnb[rows, cols])
                db = db + jnp.sum(dmix[rows, cols], axis=1, keepdims=True)
            dwsp_ref[hd] += jnp.where(tri, dw, 0.0)
            dbt_ref[:, hd:hd + 1] += db
        dvn = dvn_ref[...]
        dlng_ref[...] += jnp.sum(dvn * vhat, axis=0, keepdims=True)
        dlnb_ref[...] += jnp.sum(dvn, axis=0, keepdims=True)
        dvh = dvn * lngv
        dv = rs * (dvh - jnp.mean(dvh, axis=-1, keepdims=True) - vhat * jnp.mean(dvh * vhat, axis=-1, keepdims=True))
        gprime = 0.5 * (1.0 + th) + 0.5 * ha * (1.0 - th * th) * (GELU_C * (1.0 + 3.0 * GELU_K * (ha * ha)))
        dh_ref[:, 0:2 * aw] = (jnp.concatenate([du, dv], axis=-1) * gprime).astype(BF16)
        g_b = h_ref[:, 2 * aw:3 * aw]
        g_c = h_ref[:, 3 * aw:4 * aw]
        val = h_ref[:, 4 * aw:5 * aw]
        z = g_c * val
        zp = jnp.where(i == n - 1, 0.0, hp_ref[:, 3 * aw:4 * aw] * hp_ref[:, 4 * aw:5 * aw])
        zm1, zm2 = _shift_rows(z, zp)
        cw0, cw1, cw2 = cw_ref[0:1, :], cw_ref[1:2, :], cw_ref[2:3, :]
        conv = cw0 * zm2 + cw1 * zm1 + cw2 * z
        gbv = gb_ref[...]
        _, ybh, rb = _rms_fwd(g_b * conv, gbv)
        dyb, dgb = _rms_bwd(dycat[:, aw:], ybh, rb, gbv)
        dgb_ref[...] += dgb
        dconv = dyb * g_b
        dcw_ref[0:1, :] += jnp.sum(dconv * zm2, axis=0, keepdims=True)
        dcw_ref[1:2, :] += jnp.sum(dconv * zm1, axis=0, keepdims=True)
        dcw_ref[2:3, :] += jnp.sum(dconv * z, axis=0, keepdims=True)
        nxt = dcn_ref[...]
        row = lax.broadcasted_iota(jnp.int32, dconv.shape, 0)
        dcp1 = jnp.where(row == TM - 1, nxt[0:1, :], pltpu.roll(dconv, TM - 1, 0))
        dcp2 = jnp.where(row == TM - 1, nxt[1:2, :],
                         jnp.where(row == TM - 2, nxt[0:1, :], pltpu.roll(dconv, TM - 2, 0)))
        dz = cw2 * dconv + cw1 * dcp1 + cw0 * dcp2
        dcn_ref[...] = dconv[0:8, :]
        dh_ref[:, 2 * aw:3 * aw] = (dyb * conv).astype(BF16)
        dh_ref[:, 3 * aw:4 * aw] = (dz * val).astype(BF16)
        dh_ref[:, 4 * aw:5 * aw] = (dz * g_c).astype(BF16)
        dxn = jnp.zeros((TM, d), F32)
        for k in range(N_CHIPS):
            dxn = dxn + _dot_nt(dh_ref[:, k * ns:(k + 1) * ns], win_ref[k])
        g1v = g1_ref[...]
        _, xh1, r1 = _rms_fwd(x_ref[...], g1v)
        dxr, dg1 = _rms_bwd(dxn, xh1, r1, g1v)
        dg1_ref[...] += dg1
        dx_ref[...] = dx1v + dxr

    ins = (x, dx1, h, h, g1, lng, lnb, wsp, b_t, cw, ga, gb, w_out, w_in)
    acc_shapes = [(1, d), (1, aw), (1, aw), wsp.shape, (CHUNK, CHUNK), cw.shape, (1, aw), (1, aw)]
    return _host_call(
        body, "mixer_bwd", (n,),
        [rev(d), rev(d), rev(nh), hprev] + [_full(a.shape) for a in ins[4:]],
        [rev(d), rev(nh)] + [_acc(sh) for sh in acc_shapes],
        (jax.ShapeDtypeStruct((s, d), F32), jax.ShapeDtypeStruct((s, nh), BF16))
        + tuple(jax.ShapeDtypeStruct(sh, F32) for sh in acc_shapes),
        [pltpu.VMEM((TM, aw), F32), pltpu.VMEM((TM, aw), F32), pltpu.VMEM((8, aw), F32)],
        ins, ("arbitrary",), hosted)


def _weight_grad(a, b, name, tm, tn, col_sharded, hosted=None):
    t, m = a.shape
    n = b.shape[1]

    def body(a_ref, b_ref, o_ref):
        o_ref[...] = _dot_tn(a_ref[...].astype(BF16), b_ref[...].astype(BF16)).astype(BF16)

    if col_sharded:
        ns = n // N_CHIPS
        per = ns // tn
        out_shape = jax.ShapeDtypeStruct((N_CHIPS, m, ns), BF16)
        out_spec = pl.BlockSpec((None, tm, tn), lambda i, j: (j // per, i, j % per))
    else:
        out_shape = jax.ShapeDtypeStruct((m, n), BF16)
        out_spec = pl.BlockSpec((tm, tn), lambda i, j: (i, j))
    (out,), extra = _host_call(
        body, name, (m // tm, n // tn),
        [pl.BlockSpec((t, tm), lambda i, j: (0, i)), pl.BlockSpec((t, tn), lambda i, j: (0, j))],
        [out_spec], (out_shape,), [], (a, b), ("parallel", "parallel"), hosted)
    return (out if col_sharded else out.reshape(N_CHIPS, m // N_CHIPS, n)), extra


def _row_tile(rows, cap=256):
    best = None
    for t in range(16, min(rows, cap) + 1, 16):
        if rows % t == 0:
            best = t
    return best if best is not None else rows


def _chip_sum(grad, got, place, name):
    nb, rh, c = got.shape
    tr = _row_tile(rh)
    per = rh // tr

    def body(place_ref, a_ref, b_ref, o_ref):
        o_ref[...] = (a_ref[...].astype(F32) + b_ref[...].astype(F32)).astype(BF16)

    mine = pl.BlockSpec((None, tr, c), lambda k, i, pref: (k, pref[0] * per + i, 0))
    return pl.pallas_call(
        body, name=name,
        grid_spec=pltpu.PrefetchScalarGridSpec(
            num_scalar_prefetch=1, grid=(nb, per),
            in_specs=[mine, pl.BlockSpec((None, tr, c), lambda k, i, pref: (k, i, 0))],
            out_specs=mine),
        out_shape=jax.ShapeDtypeStruct(grad.shape, BF16),
        compiler_params=_params("parallel", "parallel"),
    )(place, grad, got)


def _adamw_math(w, g, m, v):
    m2 = ADAM_B1 * m + (1.0 - ADAM_B1) * g
    v2 = ADAM_B2 * v + (1.0 - ADAM_B2) * (g * g)
    m_hat = m2 / (1.0 - ADAM_B1 ** ADAM_STEP)
    v_hat = v2 / (1.0 - ADAM_B2 ** ADAM_STEP)
    delta = -ADAM_LR * (m_hat / (jnp.sqrt(v_hat) + ADAM_EPS) + ADAM_WD * w)
    return delta, m2, v2


def _adamw(w, g, m, v, name):
    r, c = w.shape
    tr = _row_tile(r) if r >= 16 else r

    def body(w_ref, g_ref, m_ref, v_ref, d_ref, m2_ref, v2_ref):
        d_ref[...], m2_ref[...], v2_ref[...] = _adamw_math(w_ref[...], g_ref[...], m_ref[...], v_ref[...])

    sh = jax.ShapeDtypeStruct((r, c), F32)
    return pl.pallas_call(
        body, name=name, grid=(r // tr,),
        in_specs=[_rows(tr, c)] * 4, out_specs=[_rows(tr, c)] * 3, out_shape=(sh, sh, sh),
        compiler_params=_params("parallel"),
    )(w, g, m, v)


def _finalize(part, slots, w, m, v, place, name, hosted=None):
    r, c = w.shape
    tr = _row_tile(r)

    def body(place_ref, own_ref, s0_ref, s1_ref, s2_ref, w_ref, m_ref, v_ref, g_ref, d_ref, m2_ref, v2_ref):
        g = own_ref[...].astype(F32) + s0_ref[...].astype(F32)
        g = (g + s1_ref[...].astype(F32)) + s2_ref[...].astype(F32)
        g_ref[...] = g
        d_ref[...], m2_ref[...], v2_ref[...] = _adamw_math(w_ref[...], g, m_ref[...], v_ref[...])

    def slot(k):
        return pl.BlockSpec((None, tr, c), lambda i, pref: (k, i, 0))

    rows = pl.BlockSpec((tr, c), lambda i, pref: (i, 0))
    sh = jax.ShapeDtypeStruct((r, c), F32)
    return _host_call(
        body, name, (r // tr,),
        [pl.BlockSpec((None, tr, c), lambda i, pref: (pref[1], i, 0)), slot(0), slot(1), slot(2), rows, rows, rows],
        [rows] * 4, (sh,) * 4, [], (part, slots, slots, slots, w, m, v), ("parallel",), hosted, prefetch=(place,))


def _small_sum_adamw(parts, w, m, v):
    nd, r, c = parts.shape

    def body(p_ref, w_ref, m_ref, v_ref, g_ref, d_ref, m2_ref, v2_ref):
        g = p_ref[0]
        for k in range(1, nd):
            g = g + p_ref[k]
        g_ref[...] = g
        d_ref[...], m2_ref[...], v2_ref[...] = _adamw_math(w_ref[...], g, m_ref[...], v_ref[...])

    sh = jax.ShapeDtypeStruct((r, c), F32)
    return pl.pallas_call(
        body, name="small_sum_adamw", out_shape=(sh, sh, sh, sh),
        compiler_params=pltpu.CompilerParams(vmem_limit_bytes=VMEM_LIMIT),
    )(parts, w, m, v)


def _place():
    x, y, c = lax.axis_index("x"), lax.axis_index("y"), lax.axis_index("c")
    chips = [(1 - x, y), (x, 1 - y), (1 - x, 1 - y)]
    return x, y, c, 2 * x + y, chips


def _remote(src, dst, send_sem, recv_sem, to):
    return pltpu.make_async_remote_copy(src_ref=src, dst_ref=dst, send_sem=send_sem, recv_sem=recv_sem,
                                        device_id=to, device_id_type=MESH)


class _Exchange:
    def __init__(self, ins, out_shapes, sem_shape, start, finish, middle=None, in_place=False):
        self.ins, self.out_shapes, self.sem_shape = tuple(ins), tuple(out_shapes), sem_shape
        self.start, self.finish, self.middle = start, finish, middle
        self.in_place = in_place


def _run_exchange(ex, name):
    n_in, n_out = len(ex.ins), len(ex.out_shapes)

    def body(*refs):
        ins, outs = refs[:n_in], refs[n_in:n_in + n_out]
        send_sems, recv_sems = refs[n_in + n_out:]
        ex.start(ins, outs, send_sems, recv_sems)
        if ex.middle is not None:
            ex.middle(ins, outs, send_sems, recv_sems)
        ex.finish(ins, outs, send_sems, recv_sems)

    sem = pltpu.SemaphoreType.DMA(ex.sem_shape)
    return pl.pallas_call(
        body, name=name, out_shape=ex.out_shapes, in_specs=[ANY] * n_in, out_specs=[ANY] * n_out,
        input_output_aliases={k: k for k in range(n_in)} if ex.in_place else {}, scratch_shapes=[sem, sem],
    )(*ex.ins)


def _host_call(body, name, grid, in_specs, out_specs, out_shape, scratch_shapes, args, semantics, hosted,
               prefetch=()):
    hosted = [] if hosted is None else (list(hosted) if isinstance(hosted, (list, tuple)) else [hosted])
    n_pre, n_in, n_out, n_scr = len(prefetch), len(in_specs), len(out_specs), len(scratch_shapes)
    h_ins = [a for ex in hosted for a in ex.ins]
    h_outs = [s for ex in hosted for s in ex.out_shapes]
    h_in, h_out = len(h_ins), len(h_outs)

    def wrapped(*refs):
        pre, refs = refs[:n_pre], refs[n_pre:]
        a, hi = refs[:n_in], refs[n_in:n_in + h_in]
        o = refs[n_in + h_in:n_in + h_in + n_out]
        ho = refs[n_in + h_in + n_out:n_in + h_in + n_out + h_out]
        scr = refs[n_in + h_in + n_out + h_out:]

        def run(phase):
            i0 = o0 = 0
            for k, ex in enumerate(hosted):
                fn = getattr(ex, phase)
                if fn is not None:
                    fn(hi[i0:i0 + len(ex.ins)], ho[o0:o0 + len(ex.out_shapes)], scr[n_scr + 2 * k],
                       scr[n_scr + 2 * k + 1])
                i0, o0 = i0 + len(ex.ins), o0 + len(ex.out_shapes)

        if hosted:
            first = functools.reduce(jnp.logical_and, [pl.program_id(k) == 0 for k in range(len(grid))])

            @pl.when(first)
            def _():
                run("start")

        if any(ex.middle is not None for ex in hosted):
            half_way = functools.reduce(jnp.logical_and, [
                pl.program_id(0) == max(1, grid[0] * MIDDLE_STEP_16THS // 16)] + [
                pl.program_id(k) == 0 for k in range(1, len(grid))])

            @pl.when(half_way)
            def _():
                run("middle")

        body(*pre, *a, *o, *scr[:n_scr])

        if hosted:
            last = functools.reduce(jnp.logical_and, [pl.program_id(k) == grid[k] - 1 for k in range(len(grid))])

            @pl.when(last)
            def _():
                run("finish")

    sems = [pltpu.SemaphoreType.DMA(ex.sem_shape) for ex in hosted for _ in range(2)]
    aliases, i0, o0 = {}, n_pre + n_in, n_out
    for ex in hosted:
        if ex.in_place:
            aliases.update({i0 + k: o0 + k for k in range(len(ex.ins))})
        i0, o0 = i0 + len(ex.ins), o0 + len(ex.out_shapes)
    all_in, all_out = list(in_specs) + [ANY] * h_in, list(out_specs) + [ANY] * h_out
    all_scr = list(scratch_shapes) + sems
    params = _params(*(["arbitrary"] * len(grid) if hosted else semantics))
    shapes = tuple(out_shape) + tuple(h_outs)
    if n_pre:
        call = pl.pallas_call(
            wrapped, name=name, out_shape=shapes, input_output_aliases=aliases, compiler_params=params,
            grid_spec=pltpu.PrefetchScalarGridSpec(num_scalar_prefetch=n_pre, grid=grid, in_specs=all_in,
                                                   out_specs=all_out, scratch_shapes=all_scr))
    else:
        call = pl.pallas_call(
            wrapped, name=name, grid=grid, in_specs=all_in, out_specs=all_out, out_shape=shapes,
            scratch_shapes=all_scr, input_output_aliases=aliases, compiler_params=params)
    res = call(*prefetch, *args, *h_ins)
    return res[:n_out], res[n_out:]


def _all_gather(shards, small=()):
    items = tuple(shards) + tuple(small)
    nw = len(shards)

    def place():
        x, y, c, me, _ = _place()
        first = (x + (1 - c) * (1 - 2 * x), y + c * (1 - 2 * y))
        second = (x + c * (1 - 2 * x), y + (1 - c) * (1 - 2 * y))
        diag = (1 - x, 1 - y)
        return x, y, c, me, (first, second, diag)

    def halves(w, c):
        rh = items[w].shape[0] // 2
        return pl.ds(c * rh, rh), pl.ds((1 - c) * rh, rh)

    def start(ins, outs, ss, rs):
        x, y, c, me, chips = place()
        for w in range(len(items)):
            _remote(ins[w], outs[w].at[me], ss.at[w, 6], rs.at[w, 6], (x, y, 1 - c)).start()
            if w < nw:
                mine, _ = halves(w, c)
                _remote(ins[w].at[mine], outs[w].at[me, mine], ss.at[w, 0], rs.at[w, 0], (*chips[0], c)).start()
            else:
                for k in range(3):
                    _remote(ins[w], outs[w].at[me], ss.at[w, k], rs.at[w, k], (*chips[k], c)).start()

    def onward(outs, ss, rs, w, k, x, y, c, chips):
        mine, _ = halves(w, c)
        pk = 2 * chips[k][0] + chips[k][1]
        got = outs[w].at[pk, mine]
        src = chips[1] if k == 2 else chips[k]
        _remote(got, got, ss.at[w, k], rs.at[w, k], (*src, c)).wait_recv()
        if k == 0:
            _remote(got, got, ss.at[w, 2], rs.at[w, 2], (*chips[1], c)).start()
        _remote(got, got, ss.at[w, 3 + k], rs.at[w, 3 + k], (x, y, 1 - c)).start()

    def middle(ins, outs, ss, rs):
        x, y, c, me, chips = place()
        for w in range(nw):
            mine, _ = halves(w, c)
            _remote(ins[w].at[mine], outs[w].at[me, mine], ss.at[w, 1], rs.at[w, 1], (*chips[1], c)).start()
        for w in range(nw):
            onward(outs, ss, rs, w, 0, x, y, c, chips)

    def finish(ins, outs, ss, rs):
        x, y, c, me, chips = place()
        sib = (x, y, 1 - c)
        for k in (1, 2):
            for w in range(nw):
                onward(outs, ss, rs, w, k, x, y, c, chips)
        for w in range(len(items)):
            if w < nw:
                mine, theirs = halves(w, c)
                for k, chip in ((3, chips[1]), (4, chips[0]), (5, chips[2])):
                    oth = outs[w].at[2 * chip[0] + chip[1], theirs]
                    _remote(oth, oth, ss.at[w, k], rs.at[w, k], sib).wait_recv()
                own = ins[w].at[mine]
                for k in range(6):
                    _remote(own, own, ss.at[w, k], rs.at[w, k], sib).wait_send()
            else:
                for k in range(3):
                    got = outs[w].at[2 * chips[k][0] + chips[k][1]]
                    _remote(got, got, ss.at[w, k], rs.at[w, k], (*chips[k], c)).wait_recv()
                    _remote(ins[w], ins[w], ss.at[w, k], rs.at[w, k], sib).wait_send()
            _remote(ins[w], outs[w].at[me], ss.at[w, 6], rs.at[w, 6], sib).wait()

    out_shapes = tuple(jax.ShapeDtypeStruct((N_CHIPS,) + a.shape, a.dtype) for a in items)
    return _Exchange(items, out_shapes, (len(items), 7), start, finish, middle if nw else None)


def _exchange_halves(grads):
    nw = len(grads)

    def copies(ins, outs, ss, rs):
        x, y, c, _, _ = _place()
        res = []
        for w in range(nw):
            rh = grads[w].shape[1] // 2
            res.append(_remote(ins[w].at[:, pl.ds((1 - c) * rh, rh), :], outs[w], ss.at[w], rs.at[w], (x, y, 1 - c)))
        return res

    def start(ins, outs, ss, rs):
        for cp in copies(ins, outs, ss, rs):
            cp.start()

    def finish(ins, outs, ss, rs):
        for cp in copies(ins, outs, ss, rs):
            cp.wait()

    halves = tuple(jax.ShapeDtypeStruct((g.shape[0], g.shape[1] // 2, g.shape[2]), g.dtype) for g in grads)
    return _Exchange(grads, halves, (nw,), start, finish)


def _scatter_partials(parts):
    nw = len(parts)

    def copies(ins, outs, ss, rs):
        _, _, c, _, chips = _place()
        res = []
        for r, (px, py) in enumerate(chips):
            for w in range(nw):
                rh = parts[w].shape[1] // 2
                rows = pl.ds(c * rh, rh)
                res.append(_remote(ins[w].at[2 * px + py, rows], outs[w].at[r, rows], ss.at[w, r], rs.at[w, r],
                                   (px, py, c)))
        return res

    def start(ins, outs, ss, rs):
        for cp in copies(ins, outs, ss, rs):
            cp.start()

    def finish(ins, outs, ss, rs):
        for cp in copies(ins, outs, ss, rs):
            cp.wait()

    out_shapes = tuple(jax.ShapeDtypeStruct((3,) + p.shape[1:], p.dtype) for p in parts)
    return _Exchange(parts, out_shapes, (nw, 3), start, finish)


def _join_partials(parts, slots):
    nw = len(parts)

    def copies(outs, ss, rs, mine):
        x, y, c, me, _ = _place()
        res = []
        for w in range(nw):
            rh = parts[w].shape[1] // 2
            rows = pl.ds((c if mine else 1 - c) * rh, rh)
            own = outs[w].at[me, rows]
            got = outs[nw + w].at[:, rows, :]
            res.append(_remote(own, own, ss.at[w, 0], rs.at[w, 0], (x, y, 1 - c)))
            res.append(_remote(got, got, ss.at[w, 1], rs.at[w, 1], (x, y, 1 - c)))
        return res

    def start(ins, outs, ss, rs):
        for cp in copies(outs, ss, rs, True):
            cp.start()

    def finish(ins, outs, ss, rs):
        for cp in copies(outs, ss, rs, True):
            cp.wait_send()
        for cp in copies(outs, ss, rs, False):
            cp.wait_recv()

    arrays = tuple(parts) + tuple(slots)
    return _Exchange(arrays, tuple(jax.ShapeDtypeStruct(a.shape, a.dtype) for a in arrays), (nw, 2), start, finish,
                     in_place=True)


def _gather_small(slab):
    def copies(ins, outs, ss, rs):
        x, y, c, _, _ = _place()
        me = 4 * x + 2 * y + c
        own = pltpu.make_async_copy(ins[0], outs[0].at[me], ss.at[7])
        out, arrivals = [], []
        for k in range(1, 8):
            px = 1 - x if k & 4 else x
            py = 1 - y if k & 2 else y
            pc = 1 - c if k & 1 else c
            out.append(_remote(ins[0], outs[0].at[me], ss.at[k - 1], rs.at[k - 1], (px, py, pc)))
            theirs = outs[0].at[4 * px + 2 * py + pc]
            arrivals.append(_remote(theirs, theirs, ss.at[k - 1], rs.at[k - 1], (px, py, pc)))
        return own, out, arrivals

    def start(ins, outs, ss, rs):
        own, out, _ = copies(ins, outs, ss, rs)
        own.start()
        for cp in out:
            cp.start()

    def finish(ins, outs, ss, rs):
        own, out, arrivals = copies(ins, outs, ss, rs)
        for cp in out:
            cp.wait_send()
        for cp in arrivals:
            cp.wait_recv()
        own.wait()

    return _Exchange((slab,), (jax.ShapeDtypeStruct((8,) + slab.shape, slab.dtype),), (8,), start, finish)


_SMALL_VECS = ("ln_mix_g", "ln_attn_g", "ln_mem_g", "ln_ffn_g", "ln_final_g")


def _pack_small(p, extra, conv):
    d = p["ln_mix_g"].shape[-1]
    top = [p[k].reshape(1, d) for k in _SMALL_VECS]
    top.append(jnp.concatenate([p["sgu_ln_g"].reshape(-1), p["sgu_ln_b"].reshape(-1)]).reshape(1, d))
    top.append(jnp.concatenate([p["grp_norm_a"].reshape(-1), p["grp_norm_b"].reshape(-1)]).reshape(1, d))
    top.append(jnp.concatenate([p["b_spatial"].reshape(-1), extra]).reshape(1, d))
    mid = jnp.zeros((8, d), F32)
    if conv is not None:
        mid = jnp.pad(conv, ((0, 5), (0, d - conv.shape[1])))
    return jnp.concatenate([jnp.concatenate(top, axis=0), mid, p["w_spatial"].reshape(-1, d)], axis=0)


def _unpack_small(slab):
    d = slab.shape[1]
    hw = d // 2
    out = {k: slab[i] for i, k in enumerate(_SMALL_VECS)}
    out["sgu_ln_g"], out["sgu_ln_b"] = slab[5, :hw], slab[5, hw:]
    out["grp_norm_a"], out["grp_norm_b"] = slab[6, :hw], slab[6, hw:]
    out["b_spatial"] = slab[7, :hw].reshape(HEADS, CHUNK)
    out["w_spatial"] = slab[16:].reshape(HEADS, CHUNK, CHUNK)
    return out


_BIG = ("w_in", "w_kv", "w_gate_up", "w_out", "w_q", "w_o", "w_down")
_WEIGHTS = ("ln_mix_g", "w_in", "sgu_ln_g", "sgu_ln_b", "w_spatial", "b_spatial", "conv_w", "grp_norm_a",
            "grp_norm_b", "w_out", "ln_attn_g", "ln_mem_g", "w_q", "w_kv", "w_o", "ln_ffn_g", "w_gate_up",
            "w_down", "ln_final_g")


def _step(p, m_, v_, x, mem, target):
    s, d = x.shape
    hw = d // 2
    row = lambda a: a.reshape(1, -1)
    x_, y_, c_ = lax.axis_index("x"), lax.axis_index("y"), lax.axis_index("c")
    chip = 2 * x_ + y_

    bf = {k: p[k].astype(BF16) for k in _BIG}
    conv8 = jnp.pad(p["conv_w"], ((0, 5), (0, 0)))
    w_in, w_out4, conv4 = _run_exchange(_all_gather([bf["w_in"], bf["w_out"]], [conv8]), "all_gather_mixer")
    cw = jnp.transpose(conv4[:, :3, :], (1, 0, 2)).reshape(3, hw)
    b_t = jnp.pad(jnp.transpose(p["b_spatial"]), ((0, 0), (0, CHUNK - HEADS)))
    g1, g2, gm, g3, gf = (row(p[k]) for k in _SMALL_VECS)
    lng, lnb, ga, gb = row(p["sgu_ln_g"]), row(p["sgu_ln_b"]), row(p["grp_norm_a"]), row(p["grp_norm_b"])
    wsp = p["w_spatial"]
    w_out = w_out4.reshape(-1, d)

    (h, x1, ycat, xn1), (w_kv, w_q4, w_o4, w_down4) = _mixer_fwd(
        x, g1, w_in, lng, lnb, wsp, b_t, cw, ga, gb, w_out,
        hosted=_all_gather([bf[k] for k in ("w_kv", "w_q", "w_o", "w_down")]))
    w_q, w_o, w_down = (a.reshape(-1, d) for a in (w_q4, w_o4, w_down4))
    memn, kv = _kv_fwd(mem, gm, w_kv)
    (x2, o, xn2), (w_gu,) = _attn_fwd(x1, g2, w_q, kv, w_o, hosted=_all_gather([bf["w_gate_up"]]))
    dx2, act, dgu, xn3, dx3, loss, dgf, dg3, dx2b = _ffn_fwd_bwd(x2, g3, gf, target, w_gu, w_down)

    place = jnp.stack([c_, chip]).astype(jnp.int32)

    def chip_partials(names, grads, tag):
        got = _run_exchange(_exchange_halves(grads), "rs_exchange_" + tag)
        return [_chip_sum(g, b, place, "chip_sum_" + k) for k, g, b in zip(names, grads, got)]

    names_d = ("w_down",)
    parts_d = chip_partials(names_d, (_weight_grad(act, dx3, "grad_w_down", 1408, 512, False)[0],), "down")
    g_gu, slots_d = _weight_grad(xn3, dgu, "grad_w_gate_up", 512, 1408, True, hosted=_scatter_partials(parts_d))
    names_a = ("w_gate_up", "w_o")
    parts_a = chip_partials(names_a, (g_gu, _weight_grad(o, dx2b, "grad_w_o", 1024, 512, False)[0]), "ffn")
    (dx1, dq, dkv, dg2, dx1b), slots_a = _attn_bwd(x1, dx2, g2, w_q, kv, w_o, hosted=_scatter_partials(parts_a))
    g_kv, dgm = _kv_bwd(dkv, mem, memn, gm, w_kv)
    names_b = ("w_out", "w_q", "w_kv")
    parts_b = chip_partials(names_b, (_weight_grad(ycat, dx1b, "grad_w_out", 1024, 512, False)[0],
                                      _weight_grad(xn2, dq, "grad_w_q", 1024, 512, False)[0], g_kv), "attn")
    (dx, dh, dg1, dlng, dlnb, dwsp, dbt, dcw, dga, dgb), slots_b = _mixer_bwd(
        x, dx1, h, g1, lng, lnb, wsp, b_t, cw, ga, gb, w_out, w_in, hosted=_scatter_partials(parts_b))
    small = {"ln_mix_g": dg1, "ln_attn_g": dg2, "ln_mem_g": dgm, "ln_ffn_g": dg3, "ln_final_g": dgf,
             "sgu_ln_g": dlng, "sgu_ln_b": dlnb, "grp_norm_a": dga, "grp_norm_b": dgb,
             "b_spatial": jnp.transpose(dbt[:, :HEADS]), "w_spatial": dwsp}
    loss_vec = jnp.pad(loss.reshape(1), (0, hw - 1))
    names = names_d + names_a + names_b
    n_done = len(names)
    g_in, extra = _weight_grad(
        xn1, dh, "grad_w_in", 1024, 640, True,
        hosted=[_gather_small(_pack_small(small, loss_vec, dcw)),
                _join_partials(parts_d + parts_a + parts_b, slots_d + slots_a + slots_b)])
    parts, whole = extra[0], dict(zip(names, zip(extra[1:1 + n_done], extra[1 + n_done:])))
    (part_in,) = chip_partials(("w_in",), (g_in,), "mixer")
    out_g, out_d, out_m, out_v = {}, {}, {}, {}

    def finalize(k, hosted=None):
        (out_g[k], out_d[k], out_m[k], out_v[k]), res = _finalize(
            whole[k][0], whole[k][1], p[k], m_[k], v_[k], place, "finalize_" + k, hosted)
        return res

    (slots_in,) = finalize("w_gate_up", _scatter_partials([part_in]))
    for k in names:
        if k != "w_gate_up":
            finalize(k)
    whole["w_in"] = _run_exchange(_join_partials([part_in], [slots_in]), "rs_join_mixer")
    finalize("w_in")

    zeros = jnp.zeros((hw,), F32)
    sg, sd, sm, sv = _small_sum_adamw(parts, _pack_small(p, zeros, None), _pack_small(m_, zeros, None),
                                      _pack_small(v_, zeros, None))
    for tree, slab in zip((out_g, out_d, out_m, out_v), (sg, sd, sm, sv)):
        tree.update(_unpack_small(slab))
    loss_out = sg[7, hw]
    g_conv = lax.dynamic_slice(sg[8:11, :hw], (0, chip * (hw // N_CHIPS)), (3, hw // N_CHIPS))
    out_g["conv_w"] = g_conv
    out_d["conv_w"], out_m["conv_w"], out_v["conv_w"] = _adamw(p["conv_w"], g_conv, m_["conv_w"], v_["conv_w"],
                                                                "adamw_conv_w")
    return loss_out, dx, out_g, out_d, out_m, out_v


def kernel(x, mem, ln_mix_g, w_in, sgu_ln_g, sgu_ln_b, w_spatial, b_spatial, conv_w, grp_norm_a, grp_norm_b, w_out, ln_attn_g, ln_mem_g, w_q, w_kv, w_o, ln_ffn_g, w_gate_up, w_down, ln_final_g, loss_target, m_ln_mix_g, m_w_in, m_sgu_ln_g, m_sgu_ln_b, m_w_spatial, m_b_spatial, m_conv_w, m_grp_norm_a, m_grp_norm_b, m_w_out, m_ln_attn_g, m_ln_mem_g, m_w_q, m_w_kv, m_w_o, m_ln_ffn_g, m_w_gate_up, m_w_down, m_ln_final_g, v_ln_mix_g, v_w_in, v_sgu_ln_g, v_sgu_ln_b, v_w_spatial, v_b_spatial, v_conv_w, v_grp_norm_a, v_grp_norm_b, v_w_out, v_ln_attn_g, v_ln_mem_g, v_w_q, v_w_kv, v_w_o, v_ln_ffn_g, v_w_gate_up, v_w_down, v_ln_final_g):
    p = dict(ln_mix_g=ln_mix_g, w_in=w_in, sgu_ln_g=sgu_ln_g, sgu_ln_b=sgu_ln_b, w_spatial=w_spatial,
             b_spatial=b_spatial, conv_w=conv_w, grp_norm_a=grp_norm_a, grp_norm_b=grp_norm_b, w_out=w_out,
             ln_attn_g=ln_attn_g, ln_mem_g=ln_mem_g, w_q=w_q, w_kv=w_kv, w_o=w_o, ln_ffn_g=ln_ffn_g,
             w_gate_up=w_gate_up, w_down=w_down, ln_final_g=ln_final_g)
    m_ = dict(ln_mix_g=m_ln_mix_g, w_in=m_w_in, sgu_ln_g=m_sgu_ln_g, sgu_ln_b=m_sgu_ln_b, w_spatial=m_w_spatial,
              b_spatial=m_b_spatial, conv_w=m_conv_w, grp_norm_a=m_grp_norm_a, grp_norm_b=m_grp_norm_b,
              w_out=m_w_out, ln_attn_g=m_ln_attn_g, ln_mem_g=m_ln_mem_g, w_q=m_w_q, w_kv=m_w_kv, w_o=m_w_o,
              ln_ffn_g=m_ln_ffn_g, w_gate_up=m_w_gate_up, w_down=m_w_down, ln_final_g=m_ln_final_g)
    v_ = dict(ln_mix_g=v_ln_mix_g, w_in=v_w_in, sgu_ln_g=v_sgu_ln_g, sgu_ln_b=v_sgu_ln_b, w_spatial=v_w_spatial,
              b_spatial=v_b_spatial, conv_w=v_conv_w, grp_norm_a=v_grp_norm_a, grp_norm_b=v_grp_norm_b,
              w_out=v_w_out, ln_attn_g=v_ln_attn_g, ln_mem_g=v_ln_mem_g, w_q=v_w_q, w_kv=v_w_kv, w_o=v_w_o,
              ln_ffn_g=v_ln_ffn_g, w_gate_up=v_w_gate_up, w_down=v_w_down, ln_final_g=v_ln_final_g)
    s, d = x.shape[-2], x.shape[-1]
    loss, dx, g, dl, nm, nv = _step(p, m_, v_, x.reshape(s, d), mem.reshape(-1, d), loss_target.reshape(s, d))
    outs = [loss, dx.reshape(x.shape)]
    for tree in (g, dl, nm, nv):
        outs += [tree[k].reshape(p[k].shape) for k in _WEIGHTS]
    return tuple(outs)
```

```python
import functools
import math

import jax
import jax.numpy as jnp
from jax import lax
from jax.experimental import pallas as pl
from jax.experimental.pallas import tpu as pltpu

F32 = jnp.float32
BF16 = jnp.bfloat16
EPS = 1e-6
CHUNK = 128
HEADS = 4
N_CHIPS = 4
TM = 512
TM_ATTN = 512
TM_FFN = 256
ADAM_LR, ADAM_B1, ADAM_B2, ADAM_EPS, ADAM_WD, ADAM_STEP = 0.001, 0.9, 0.999, 1e-08, 0.01, 10
GELU_C = math.sqrt(2.0 / math.pi)
GELU_K = 0.044715
SMALL_ROWS = 80
VMEM_LIMIT = 56 * 1024 * 1024
MIDDLE_STEP_16THS = 7
MESH = pl.DeviceIdType.MESH
ANY = pl.BlockSpec(memory_space=pl.ANY)


def _params(*sem):
    return pltpu.CompilerParams(dimension_semantics=sem, vmem_limit_bytes=VMEM_LIMIT)


def _dot(a, b):
    return jnp.dot(a, b, preferred_element_type=F32)


def _dot_nt(a, b):
    return lax.dot_general(a, b, (((1,), (1,)), ((), ())), preferred_element_type=F32)


def _dot_tn(a, b):
    return lax.dot_general(a, b, (((0,), (0,)), ((), ())), preferred_element_type=F32)


def _rms_fwd(x, g):
    r = lax.rsqrt(jnp.mean(x * x, axis=-1, keepdims=True) + EPS)
    xh = x * r
    return xh * g, xh, r


def _rms_bwd(dy, xh, r, g):
    dxh = dy * g
    dx = r * (dxh - xh * jnp.mean(dxh * xh, axis=-1, keepdims=True))
    return dx, jnp.sum(dy * xh, axis=0, keepdims=True)


def _full(shape):
    nd = len(shape)
    return pl.BlockSpec(shape, lambda *_: (0,) * nd, pipeline_mode=pl.Buffered(1))


def _acc(shape):
    nd = len(shape)
    return pl.BlockSpec(shape, lambda *_: (0,) * nd)


def _rows(tm, cols):
    return pl.BlockSpec((tm, cols), lambda i: (i, 0))


def _tril_weights(wsp_ref):
    row = lax.broadcasted_iota(jnp.int32, (CHUNK, CHUNK), 0)
    col = lax.broadcasted_iota(jnp.int32, (CHUNK, CHUNK), 1)
    return [jnp.where(row >= col, wsp_ref[hd], 0.0).astype(BF16) for hd in range(HEADS)]


def _shift_rows(z, zp):
    row = lax.broadcasted_iota(jnp.int32, z.shape, 0)
    zm1 = jnp.where(row == 0, zp[7:8, :], pltpu.roll(z, 1, 0))
    zm2 = jnp.where(row == 0, zp[6:7, :], jnp.where(row == 1, zp[7:8, :], pltpu.roll(z, 2, 0)))
    return zm1, zm2


def _gelu_parts(x):
    t = jnp.tanh(GELU_C * (x + GELU_K * (x * x * x)))
    return 0.5 * x * (1.0 + t), t


def _layer_norm_parts(v, g, b):
    mu = jnp.mean(v, axis=-1, keepdims=True)
    vc = v - mu
    rs = lax.rsqrt(jnp.mean(vc * vc, axis=-1, keepdims=True) + EPS)
    vhat = vc * rs
    return vhat * g + b, vhat, rs


def _kv_fwd(mem, g_mem, w_kv):
    m, d = mem.shape
    ns = w_kv.shape[2]

    def body(mem_ref, g_ref, w_ref, memn_ref, kv_ref):
        y, _, _ = _rms_fwd(mem_ref[...], g_ref[...])
        yb = y.astype(BF16)
        memn_ref[...] = yb
        for k in range(N_CHIPS):
            kv_ref[:, k * ns:(k + 1) * ns] = _dot(yb, w_ref[k]).astype(BF16)

    return pl.pallas_call(
        body, name="kv_fwd",
        out_shape=(jax.ShapeDtypeStruct((m, d), BF16), jax.ShapeDtypeStruct((m, N_CHIPS * ns), BF16)),
        compiler_params=pltpu.CompilerParams(vmem_limit_bytes=VMEM_LIMIT),
    )(mem, g_mem, w_kv)


def _mixer_fwd(x, g1, w_in, lng, lnb, wsp, b_t, cw, ga, gb, w_out, hosted=None):
    s, d = x.shape
    n = s // TM
    nch = TM // CHUNK
    ns = w_in.shape[2]
    nh = N_CHIPS * ns
    aw = d // 2
    hd_w = aw // HEADS

    def body(x_ref, g1_ref, win_ref, lng_ref, lnb_ref, wsp_ref, bt_ref, cw_ref, ga_ref, gb_ref, wout_ref,
             h_ref, x1_ref, y_ref, xn_ref, mix_ref, zp_ref):
        i = pl.program_id(0)

        @pl.when(i == 0)
        def _():
            zp_ref[...] = jnp.zeros_like(zp_ref)

        x = x_ref[...]
        xn, _, _ = _rms_fwd(x, g1_ref[...])
        xnb = xn.astype(BF16)
        xn_ref[...] = xnb
        for k in range(N_CHIPS):
            h_ref[:, k * ns:(k + 1) * ns] = _dot(xnb, win_ref[k])
        a, _ = _gelu_parts(h_ref[:, 0:2 * aw])
        u = a[:, :aw]
        vn, _, _ = _layer_norm_parts(a[:, aw:], lng_ref[...], lnb_ref[...])
        vnb = vn.astype(BF16)
        wm = _tril_weights(wsp_ref)
        for c in range(nch):
            for hd in range(HEADS):
                blk = vnb[c * CHUNK:(c + 1) * CHUNK, hd * hd_w:(hd + 1) * hd_w]
                mix_ref[c * CHUNK:(c + 1) * CHUNK, hd * hd_w:(hd + 1) * hd_w] = _dot(wm[hd], blk) + bt_ref[:, hd:hd + 1]
        ya, _, _ = _rms_fwd(u * mix_ref[...], ga_ref[...])
        g_b = h_ref[:, 2 * aw:3 * aw]
        z = h_ref[:, 3 * aw:4 * aw] * h_ref[:, 4 * aw:5 * aw]
        zm1, zm2 = _shift_rows(z, zp_ref[...])
        conv = cw_ref[0:1, :] * zm2 + cw_ref[1:2, :] * zm1 + cw_ref[2:3, :] * z
        yb, _, _ = _rms_fwd(g_b * conv, gb_ref[...])
        zp_ref[...] = z[TM - 8:TM, :]
        ycat = jnp.concatenate([ya, yb], axis=-1).astype(BF16)
        y_ref[...] = ycat
        x1_ref[...] = x + _dot(ycat, wout_ref[...])

    return _host_call(
        body, "mixer_fwd", (n,),
        [_rows(TM, d), _full(g1.shape), _full(w_in.shape), _full(lng.shape), _full(lnb.shape),
         _full(wsp.shape), _full(b_t.shape), _full(cw.shape), _full(ga.shape), _full(gb.shape),
         _full(w_out.shape)],
        [_rows(TM, nh), _rows(TM, d), _rows(TM, d), _rows(TM, d)],
        (jax.ShapeDtypeStruct((s, nh), F32), jax.ShapeDtypeStruct((s, d), F32),
         jax.ShapeDtypeStruct((s, d), BF16), jax.ShapeDtypeStruct((s, d), BF16)),
        [pltpu.VMEM((TM, aw), F32), pltpu.VMEM((8, aw), F32)],
        (x, g1, w_in, lng, lnb, wsp, b_t, cw, ga, gb, w_out), ("arbitrary",), hosted)


def _attn_probs(qb, kv_ref, hd, dh, d, scale):
    kh = kv_ref[:, hd * dh:(hd + 1) * dh]
    vh = kv_ref[:, d + hd * dh:d + (hd + 1) * dh]
    sc = _dot_nt(qb, kh) * scale
    sc = sc - jnp.max(sc, axis=-1, keepdims=True)
    e = jnp.exp(sc)
    return e / jnp.sum(e, axis=-1, keepdims=True), kh, vh


def _attn_fwd(x1, g2, w_q, kv, w_o, hosted=None):
    s, d = x1.shape
    tm = min(TM_ATTN, s)
    n = s // tm
    dh = d // HEADS
    scale = dh ** -0.5

    def body(x1_ref, g2_ref, wq_ref, kv_ref, wo_ref, x2_ref, o_ref, xn_ref):
        x1v = x1_ref[...]
        xn, _, _ = _rms_fwd(x1v, g2_ref[...])
        xnb = xn.astype(BF16)
        xn_ref[...] = xnb
        q = _dot(xnb, wq_ref[...])
        for hd in range(HEADS):
            qb = q[:, hd * dh:(hd + 1) * dh].astype(BF16)
            p, _, vh = _attn_probs(qb, kv_ref, hd, dh, d, scale)
            o_ref[:, hd * dh:(hd + 1) * dh] = _dot(p.astype(BF16), vh).astype(BF16)
        x2_ref[...] = x1v + _dot(o_ref[...], wo_ref[...])

    return _host_call(
        body, "attn_fwd", (n,),
        [_rows(tm, d), _full(g2.shape), _full(w_q.shape), _full(kv.shape), _full(w_o.shape)],
        [_rows(tm, d), _rows(tm, d), _rows(tm, d)],
        (jax.ShapeDtypeStruct((s, d), F32), jax.ShapeDtypeStruct((s, d), BF16), jax.ShapeDtypeStruct((s, d), BF16)),
        [], (x1, g2, w_q, kv, w_o), ("parallel",), hosted)


def _ffn_fwd_bwd(x2, g3, gf, target, w_gu, w_down):
    s, d = x2.shape
    tm = min(TM_FFN, s)
    n = s // tm
    ns = w_gu.shape[2]
    ff = 2 * ns

    def body(x2_ref, g3_ref, gf_ref, t_ref, wgu_ref, wd_ref,
             dx2_ref, act_ref, dgu_ref, xn_ref, dx3_ref, loss_ref, dgf_ref, dg3_ref, dx2b_ref):
        i = pl.program_id(0)

        @pl.when(i == 0)
        def _():
            loss_ref[...] = jnp.zeros_like(loss_ref)
            dgf_ref[...] = jnp.zeros_like(dgf_ref)
            dg3_ref[...] = jnp.zeros_like(dg3_ref)

        x2v = x2_ref[...]
        xn, xh3, r3 = _rms_fwd(x2v, g3_ref[...])
        xnb = xn.astype(BF16)
        xn_ref[...] = xnb
        x3 = x2v
        saved = []
        for j in range(2):
            g = _dot(xnb, wgu_ref[j])
            u = _dot(xnb, wgu_ref[2 + j])
            sg = 1.0 / (1.0 + jnp.exp(-g))
            sl = g * sg
            actb = (sl * u).astype(BF16)
            act_ref[:, j * ns:(j + 1) * ns] = actb
            x3 = x3 + _dot(actb, wd_ref[j * ns:(j + 1) * ns, :])
            saved.append((u, sl, sg * (1.0 + g * (1.0 - sg))))
        gfv = gf_ref[...]
        y, xhf, rf = _rms_fwd(x3, gfv)
        e = y - t_ref[...]
        loss_ref[...] += 0.5 * jnp.sum(jnp.sum(e * e, axis=-1, keepdims=True), axis=0, keepdims=True) / d
        dx3, dgf = _rms_bwd(e / d, xhf, rf, gfv)
        dgf_ref[...] += dgf
        dx3b = dx3.astype(BF16)
        dx3_ref[...] = dx3b
        dxn = jnp.zeros_like(x2v)
        for j in range(2):
            u, sl, dsl = saved[j]
            dact = _dot_nt(dx3b, wd_ref[j * ns:(j + 1) * ns, :])
            dgb = (dact * u * dsl).astype(BF16)
            dub = (dact * sl).astype(BF16)
            dgu_ref[:, j * ns:(j + 1) * ns] = dgb
            dgu_ref[:, ff + j * ns:ff + (j + 1) * ns] = dub
            dxn = dxn + _dot_nt(dgb, wgu_ref[j]) + _dot_nt(dub, wgu_ref[2 + j])
        dxr, dg3 = _rms_bwd(dxn, xh3, r3, g3_ref[...])
        dg3_ref[...] += dg3
        dx2 = dx3 + dxr
        dx2_ref[...] = dx2
        dx2b_ref[...] = dx2.astype(BF16)

    vec = jax.ShapeDtypeStruct((1, d), F32)
    return pl.pallas_call(
        body, name="ffn_fwd_bwd", grid=(n,),
        in_specs=[_rows(tm, d), _full(g3.shape), _full(gf.shape), _rows(tm, d), _full(w_gu.shape),
                  _full(w_down.shape)],
        out_specs=[_rows(tm, d), _rows(tm, ff), _rows(tm, 2 * ff), _rows(tm, d), _rows(tm, d),
                   _acc((1, 1)), _acc((1, d)), _acc((1, d)), _rows(tm, d)],
        out_shape=(jax.ShapeDtypeStruct((s, d), F32), jax.ShapeDtypeStruct((s, ff), BF16),
                   jax.ShapeDtypeStruct((s, 2 * ff), BF16), jax.ShapeDtypeStruct((s, d), BF16),
                   jax.ShapeDtypeStruct((s, d), BF16), jax.ShapeDtypeStruct((1, 1), F32), vec, vec,
                   jax.ShapeDtypeStruct((s, d), BF16)),
        compiler_params=_params("arbitrary"),
    )(x2, g3, gf, target, w_gu, w_down)


def _attn_bwd(x1, dx2, g2, w_q, kv, w_o, hosted=None):
    s, d = x1.shape
    tm = min(TM_ATTN, s)
    n = s // tm
    dh = d // HEADS
    scale = dh ** -0.5
    m = kv.shape[0]

    def body(x1_ref, dx2_ref, g2_ref, wq_ref, kv_ref, wo_ref, dx1_ref, dq_ref, dkv_ref, dg2_ref, dx1b_ref):
        i = pl.program_id(0)

        @pl.when(i == 0)
        def _():
            dkv_ref[...] = jnp.zeros_like(dkv_ref)
            dg2_ref[...] = jnp.zeros_like(dg2_ref)

        xn, xh2, r2 = _rms_fwd(x1_ref[...], g2_ref[...])
        q = _dot(xn.astype(BF16), wq_ref[...])
        dx2v = dx2_ref[...]
        do = _dot_nt(dx2v.astype(BF16), wo_ref[...])
        for hd in range(HEADS):
            qb = q[:, hd * dh:(hd + 1) * dh].astype(BF16)
            p, kh, vh = _attn_probs(qb, kv_ref, hd, dh, d, scale)
            dob = do[:, hd * dh:(hd + 1) * dh].astype(BF16)
            dp = _dot_nt(dob, vh)
            ds = p * (dp - jnp.sum(dp * p, axis=-1, keepdims=True))
            dsb = (ds * scale).astype(BF16)
            dq_ref[:, hd * dh:(hd + 1) * dh] = _dot(dsb, kh).astype(BF16)
            dkv_ref[:, hd * dh:(hd + 1) * dh] += _dot_tn(dsb, qb)
            dkv_ref[:, d + hd * dh:d + (hd + 1) * dh] += _dot_tn(p.astype(BF16), dob)
        dxn = _dot_nt(dq_ref[...], wq_ref[...])
        dxr, dg2 = _rms_bwd(dxn, xh2, r2, g2_ref[...])
        dg2_ref[...] += dg2
        dx1 = dx2v + dxr
        dx1_ref[...] = dx1
        dx1b_ref[...] = dx1.astype(BF16)

    return _host_call(
        body, "attn_bwd", (n,),
        [_rows(tm, d), _rows(tm, d), _full(g2.shape), _full(w_q.shape), _full(kv.shape), _full(w_o.shape)],
        [_rows(tm, d), _rows(tm, d), _acc((m, 2 * d)), _acc((1, d)), _rows(tm, d)],
        (jax.ShapeDtypeStruct((s, d), F32), jax.ShapeDtypeStruct((s, d), BF16),
         jax.ShapeDtypeStruct((m, 2 * d), F32), jax.ShapeDtypeStruct((1, d), F32),
         jax.ShapeDtypeStruct((s, d), BF16)),
        [], (x1, dx2, g2, w_q, kv, w_o), ("arbitrary",), hosted)


def _kv_bwd(dkv, mem, memn, g_mem, w_kv):
    m, d = mem.shape
    ns = w_kv.shape[2]

    def body(dkv_ref, mem_ref, memn_ref, g_ref, w_ref, gw_ref, dg_ref):
        _, xh, _ = _rms_fwd(mem_ref[...], g_ref[...])
        dmemn = jnp.zeros((m, d), F32)
        for k in range(N_CHIPS):
            dkb = dkv_ref[:, k * ns:(k + 1) * ns].astype(BF16)
            gw_ref[k] = _dot_tn(memn_ref[...], dkb).astype(BF16)
            dmemn = dmemn + _dot_nt(dkb, w_ref[k])
        dg_ref[...] = jnp.sum(dmemn * xh, axis=0, keepdims=True)

    return pl.pallas_call(
        body, name="kv_bwd",
        out_shape=(jax.ShapeDtypeStruct((N_CHIPS, d, ns), BF16), jax.ShapeDtypeStruct((1, d), F32)),
        compiler_params=pltpu.CompilerParams(vmem_limit_bytes=VMEM_LIMIT),
    )(dkv, mem, memn, g_mem, w_kv)


def _mixer_bwd(x, dx1, h, g1, lng, lnb, wsp, b_t, cw, ga, gb, w_out, w_in, hosted=None):
    s, d = x.shape
    n = s // TM
    nch = TM // CHUNK
    ns = w_in.shape[2]
    nh = N_CHIPS * ns
    aw = d // 2
    hd_w = aw // HEADS

    def rev(cols):
        return pl.BlockSpec((TM, cols), lambda i: (n - 1 - i, 0))

    hprev = pl.BlockSpec((8, nh), lambda i: (jnp.maximum((n - 1 - i) * (TM // 8) - 1, 0), 0))

    def body(x_ref, dx1_ref, h_ref, hp_ref, g1_ref, lng_ref, lnb_ref, wsp_ref, bt_ref, cw_ref, ga_ref, gb_ref,
             wout_ref, win_ref,
             dx_ref, dh_ref, dg1_ref, dlng_ref, dlnb_ref, dwsp_ref, dbt_ref, dcw_ref, dga_ref, dgb_ref,
             mix_ref, dvn_ref, dcn_ref):
        i = pl.program_id(0)

        @pl.when(i == 0)
        def _():
            for r in (dg1_ref, dlng_ref, dlnb_ref, dwsp_ref, dbt_ref, dcw_ref, dga_ref, dgb_ref, dcn_ref):
                r[...] = jnp.zeros_like(r)

        dx1v = dx1_ref[...]
        dycat = _dot_nt(dx1v.astype(BF16), wout_ref[...])
        ha = h_ref[:, 0:2 * aw]
        a, th = _gelu_parts(ha)
        u = a[:, :aw]
        lngv = lng_ref[...]
        vn, vhat, rs = _layer_norm_parts(a[:, aw:], lngv, lnb_ref[...])
        vnb = vn.astype(BF16)
        wm = _tril_weights(wsp_ref)
        for c in range(nch):
            for hd in range(HEADS):
                blk = vnb[c * CHUNK:(c + 1) * CHUNK, hd * hd_w:(hd + 1) * hd_w]
                mix_ref[c * CHUNK:(c + 1) * CHUNK, hd * hd_w:(hd + 1) * hd_w] = _dot(wm[hd], blk) + bt_ref[:, hd:hd + 1]
        mixed = mix_ref[...]
        gav = ga_ref[...]
        _, yah, ra = _rms_fwd(u * mixed, gav)
        dya, dga = _rms_bwd(dycat[:, :aw], yah, ra, gav)
        dga_ref[...] += dga
        du = dya * mixed
        dmix = dya * u
        dmb = dmix.astype(BF16)
        tri = lax.broadcasted_iota(jnp.int32, (CHUNK, CHUNK), 0) >= lax.broadcasted_iota(jnp.int32, (CHUNK, CHUNK), 1)
        for hd in range(HEADS):
            dw = jnp.zeros((CHUNK, CHUNK), F32)
            db = jnp.zeros((CHUNK, 1), F32)
            for c in range(nch):
                rows = slice(c * CHUNK, (c + 1) * CHUNK)
                cols = slice(hd * hd_w, (hd + 1) * hd_w)
                dvn_ref[rows, cols] = _dot_tn(wm[hd], dmb[rows, cols])
                dw = dw + _dot_nt(dmb[rows, cols], vnb[rows, cols])
                db = db + jnp.sum(dmix[rows, cols], axis=1, keepdims=True)
            dwsp_ref[hd] += jnp.where(tri, dw, 0.0)
            dbt_ref[:, hd:hd + 1] += db
        dvn = dvn_ref[...]
        dlng_ref[...] += jnp.sum(dvn * vhat, axis=0, keepdims=True)
        dlnb_ref[...] += jnp.sum(dvn, axis=0, keepdims=True)
        dvh = dvn * lngv
        dv = rs * (dvh - jnp.mean(dvh, axis=-1, keepdims=True) - vhat * jnp.mean(dvh * vhat, axis=-1, keepdims=True))
        gprime = 0.5 * (1.0 + th) + 0.5 * ha * (1.0 - th * th) * (GELU_C * (1.0 + 3.0 * GELU_K * (ha * ha)))
        dh_ref[:, 0:2 * aw] = (jnp.concatenate([du, dv], axis=-1) * gprime).astype(BF16)
        g_b = h_ref[:, 2 * aw:3 * aw]
        g_c = h_ref[:, 3 * aw:4 * aw]
        val = h_ref[:, 4 * aw:5 * aw]
        z = g_c * val
        zp = jnp.where(i == n - 1, 0.0, hp_ref[:, 3 * aw:4 * aw] * hp_ref[:, 4 * aw:5 * aw])
        zm1, zm2 = _shift_rows(z, zp)
        cw0, cw1, cw2 = cw_ref[0:1, :], cw_ref[1:2, :], cw_ref[2:3, :]
        conv = cw0 * zm2 + cw1 * zm1 + cw2 * z
        gbv = gb_ref[...]
        _, ybh, rb = _rms_fwd(g_b * conv, gbv)
        dyb, dgb = _rms_bwd(dycat[:, aw:], ybh, rb, gbv)
        dgb_ref[...] += dgb
        dconv = dyb * g_b
        dcw_ref[0:1, :] += jnp.sum(dconv * zm2, axis=0, keepdims=True)
        dcw_ref[1:2, :] += jnp.sum(dconv * zm1, axis=0, keepdims=True)
        dcw_ref[2:3, :] += jnp.sum(dconv * z, axis=0, keepdims=True)
        nxt = dcn_ref[...]
        row = lax.broadcasted_iota(jnp.int32, dconv.shape, 0)
        dcp1 = jnp.where(row == TM - 1, nxt[0:1, :], pltpu.roll(dconv, TM - 1, 0))
        dcp2 = jnp.where(row == TM - 1, nxt[1:2, :],
                         jnp.where(row == TM - 2, nxt[0:1, :], pltpu.roll(dconv, TM - 2, 0)))
        dz = cw2 * dconv + cw1 * dcp1 + cw0 * dcp2
        dcn_ref[...] = dconv[0:8, :]
        dh_ref[:, 2 * aw:3 * aw] = (dyb * conv).astype(BF16)
        dh_ref[:, 3 * aw:4 * aw] = (dz * val).astype(BF16)
        dh_ref[:, 4 * aw:5 * aw] = (dz * g_c).astype(BF16)
        dxn = jnp.zeros((TM, d), F32)
        for k in range(N_CHIPS):
            dxn = dxn + _dot_nt(dh_ref[:, k * ns:(k + 1) * ns], win_ref[k])
        g1v = g1_ref[...]
        _, xh1, r1 = _rms_fwd(x_ref[...], g1v)
        dxr, dg1 = _rms_bwd(dxn, xh1, r1, g1v)
        dg1_ref[...] += dg1
        dx_ref[...] = dx1v + dxr

    ins = (x, dx1, h, h, g1, lng, lnb, wsp, b_t, cw, ga, gb, w_out, w_in)
    acc_shapes = [(1, d), (1, aw), (1, aw), wsp.shape, (CHUNK, CHUNK), cw.shape, (1, aw), (1, aw)]
    return _host_call(
        body, "mixer_bwd", (n,),
        [rev(d), rev(d), rev(nh), hprev] + [_full(a.shape) for a in ins[4:]],
        [rev(d), rev(nh)] + [_acc(sh) for sh in acc_shapes],
        (jax.ShapeDtypeStruct((s, d), F32), jax.ShapeDtypeStruct((s, nh), BF16))
        + tuple(jax.ShapeDtypeStruct(sh, F32) for sh in acc_shapes),
        [pltpu.VMEM((TM, aw), F32), pltpu.VMEM((TM, aw), F32), pltpu.VMEM((8, aw), F32)],
        ins, ("arbitrary",), hosted)


def _weight_grad(a, b, name, tm, tn, col_sharded, hosted=None):
    t, m = a.shape
    n = b.shape[1]

    def body(a_ref, b_ref, o_ref):
        o_ref[...] = _dot_tn(a_ref[...].astype(BF16), b_ref[...].astype(BF16)).astype(BF16)

    if col_sharded:
        ns = n // N_CHIPS
        per = ns // tn
        out_shape = jax.ShapeDtypeStruct((N_CHIPS, m, ns), BF16)
        out_spec = pl.BlockSpec((None, tm, tn), lambda i, j: (j // per, i, j % per))
    else:
        out_shape = jax.ShapeDtypeStruct((m, n), BF16)
        out_spec = pl.BlockSpec((tm, tn), lambda i, j: (i, j))
    (out,), extra = _host_call(
        body, name, (m // tm, n // tn),
        [pl.BlockSpec((t, tm), lambda i, j: (0, i)), pl.BlockSpec((t, tn), lambda i, j: (0, j))],
        [out_spec], (out_shape,), [], (a, b), ("parallel", "parallel"), hosted)
    return (out if col_sharded else out.reshape(N_CHIPS, m // N_CHIPS, n)), extra


def _row_tile(rows, cap=256):
    best = None
    for t in range(16, min(rows, cap) + 1, 16):
        if rows % t == 0:
            best = t
    return best if best is not None else rows


def _adamw_math(w, g, m, v):
    m2 = ADAM_B1 * m + (1.0 - ADAM_B1) * g
    v2 = ADAM_B2 * v + (1.0 - ADAM_B2) * (g * g)
    m_hat = m2 / (1.0 - ADAM_B1 ** ADAM_STEP)
    v_hat = v2 / (1.0 - ADAM_B2 ** ADAM_STEP)
    delta = -ADAM_LR * (m_hat / (jnp.sqrt(v_hat) + ADAM_EPS) + ADAM_WD * w)
    return delta, m2, v2


def _adamw(w, g, m, v, name):
    r, c = w.shape
    tr = _row_tile(r) if r >= 16 else r

    def body(w_ref, g_ref, m_ref, v_ref, d_ref, m2_ref, v2_ref):
        d_ref[...], m2_ref[...], v2_ref[...] = _adamw_math(w_ref[...], g_ref[...], m_ref[...], v_ref[...])

    sh = jax.ShapeDtypeStruct((r, c), F32)
    return pl.pallas_call(
        body, name=name, grid=(r // tr,),
        in_specs=[_rows(tr, c)] * 4, out_specs=[_rows(tr, c)] * 3, out_shape=(sh, sh, sh),
        compiler_params=_params("parallel"),
    )(w, g, m, v)


def _finalize(part, slots, w, m, v, place, name, hosted=None):
    r, c = w.shape
    tr = _row_tile(r)

    def body(place_ref, own_ref, s0_ref, s1_ref, s2_ref, w_ref, m_ref, v_ref, g_ref, d_ref, m2_ref, v2_ref):
        g = own_ref[...].astype(F32) + s0_ref[...].astype(F32)
        g = (g + s1_ref[...].astype(F32)) + s2_ref[...].astype(F32)
        g_ref[...] = g
        d_ref[...], m2_ref[...], v2_ref[...] = _adamw_math(w_ref[...], g, m_ref[...], v_ref[...])

    def slot(k):
        return pl.BlockSpec((None, tr, c), lambda i, pref: (k, i, 0))

    rows = pl.BlockSpec((tr, c), lambda i, pref: (i, 0))
    sh = jax.ShapeDtypeStruct((r, c), F32)
    return _host_call(
        body, name, (r // tr,),
        [pl.BlockSpec((None, tr, c), lambda i, pref: (pref[1], i, 0)), slot(0), slot(1), slot(2), rows, rows, rows],
        [rows] * 4, (sh,) * 4, [], (part, slots, slots, slots, w, m, v), ("parallel",), hosted, prefetch=(place,))


def _small_sum_adamw(parts, w, m, v):
    nd, r, c = parts.shape

    def body(p_ref, w_ref, m_ref, v_ref, g_ref, d_ref, m2_ref, v2_ref):
        g = p_ref[0]
        for k in range(1, nd):
            g = g + p_ref[k]
        g_ref[...] = g
        d_ref[...], m2_ref[...], v2_ref[...] = _adamw_math(w_ref[...], g, m_ref[...], v_ref[...])

    sh = jax.ShapeDtypeStruct((r, c), F32)
    return pl.pallas_call(
        body, name="small_sum_adamw", out_shape=(sh, sh, sh, sh),
        compiler_params=pltpu.CompilerParams(vmem_limit_bytes=VMEM_LIMIT),
    )(parts, w, m, v)


def _place():
    x, y, c = lax.axis_index("x"), lax.axis_index("y"), lax.axis_index("c")
    chips = [(1 - x, y), (x, 1 - y), (1 - x, 1 - y)]
    return x, y, c, 2 * x + y, chips


def _remote(src, dst, send_sem, recv_sem, to):
    return pltpu.make_async_remote_copy(src_ref=src, dst_ref=dst, send_sem=send_sem, recv_sem=recv_sem,
                                        device_id=to, device_id_type=MESH)


class _Exchange:
    def __init__(self, ins, out_shapes, sem_shape, start, finish, middle=None, in_place=False):
        self.ins, self.out_shapes, self.sem_shape = tuple(ins), tuple(out_shapes), sem_shape
        self.start, self.finish, self.middle = start, finish, middle
        self.in_place = in_place


def _run_exchange(ex, name):
    n_in, n_out = len(ex.ins), len(ex.out_shapes)

    def body(*refs):
        ins, outs = refs[:n_in], refs[n_in:n_in + n_out]
        send_sems, recv_sems = refs[n_in + n_out:]
        ex.start(ins, outs, send_sems, recv_sems)
        if ex.middle is not None:
            ex.middle(ins, outs, send_sems, recv_sems)
        ex.finish(ins, outs, send_sems, recv_sems)

    sem = pltpu.SemaphoreType.DMA(ex.sem_shape)
    return pl.pallas_call(
        body, name=name, out_shape=ex.out_shapes, in_specs=[ANY] * n_in, out_specs=[ANY] * n_out,
        input_output_aliases={k: k for k in range(n_in)} if ex.in_place else {}, scratch_shapes=[sem, sem],
    )(*ex.ins)


def _host_call(body, name, grid, in_specs, out_specs, out_shape, scratch_shapes, args, semantics, hosted,
               prefetch=()):
    hosted = [] if hosted is None else (list(hosted) if isinstance(hosted, (list, tuple)) else [hosted])
    n_pre, n_in, n_out, n_scr = len(prefetch), len(in_specs), len(out_specs), len(scratch_shapes)
    h_ins = [a for ex in hosted for a in ex.ins]
    h_outs = [s for ex in hosted for s in ex.out_shapes]
    h_in, h_out = len(h_ins), len(h_outs)

    def wrapped(*refs):
        pre, refs = refs[:n_pre], refs[n_pre:]
        a, hi = refs[:n_in], refs[n_in:n_in + h_in]
        o = refs[n_in + h_in:n_in + h_in + n_out]
        ho = refs[n_in + h_in + n_out:n_in + h_in + n_out + h_out]
        scr = refs[n_in + h_in + n_out + h_out:]

        def run(phase):
            i0 = o0 = 0
            for k, ex in enumerate(hosted):
                fn = getattr(ex, phase)
                if fn is not None:
                    fn(hi[i0:i0 + len(ex.ins)], ho[o0:o0 + len(ex.out_shapes)], scr[n_scr + 2 * k],
                       scr[n_scr + 2 * k + 1])
                i0, o0 = i0 + len(ex.ins), o0 + len(ex.out_shapes)

        if hosted:
            first = functools.reduce(jnp.logical_and, [pl.program_id(k) == 0 for k in range(len(grid))])

            @pl.when(first)
            def _():
                run("start")

        if any(ex.middle is not None for ex in hosted):
            half_way = functools.reduce(jnp.logical_and, [
                pl.program_id(0) == max(1, grid[0] * MIDDLE_STEP_16THS // 16)] + [
                pl.program_id(k) == 0 for k in range(1, len(grid))])

            @pl.when(half_way)
            def _():
                run("middle")

        body(*pre, *a, *o, *scr[:n_scr])

        if hosted:
            last = functools.reduce(jnp.logical_and, [pl.program_id(k) == grid[k] - 1 for k in range(len(grid))])

            @pl.when(last)
            def _():
                run("finish")

    sems = [pltpu.SemaphoreType.DMA(ex.sem_shape) for ex in hosted for _ in range(2)]
    aliases, i0, o0 = {}, n_pre + n_in, n_out
    for ex in hosted:
        if ex.in_place:
            aliases.update({i0 + k: o0 + k for k in range(len(ex.ins))})
        i0, o0 = i0 + len(ex.ins), o0 + len(ex.out_shapes)
    all_in, all_out = list(in_specs) + [ANY] * h_in, list(out_specs) + [ANY] * h_out
    all_scr = list(scratch_shapes) + sems
    params = _params(*(["arbitrary"] * len(grid) if hosted else semantics))
    shapes = tuple(out_shape) + tuple(h_outs)
    if n_pre:
        call = pl.pallas_call(
            wrapped, name=name, out_shape=shapes, input_output_aliases=aliases, compiler_params=params,
            grid_spec=pltpu.PrefetchScalarGridSpec(num_scalar_prefetch=n_pre, grid=grid, in_specs=all_in,
                                                   out_specs=all_out, scratch_shapes=all_scr))
    else:
        call = pl.pallas_call(
            wrapped, name=name, grid=grid, in_specs=all_in, out_specs=all_out, out_shape=shapes,
            scratch_shapes=all_scr, input_output_aliases=aliases, compiler_params=params)
    res = call(*prefetch, *args, *h_ins)
    return res[:n_out], res[n_out:]


def _all_gather(shards, small=()):
    items = tuple(shards) + tuple(small)
    nw = len(shards)

    def place():
        x, y, c, me, _ = _place()
        first = (x + (1 - c) * (1 - 2 * x), y + c * (1 - 2 * y))
        second = (x + c * (1 - 2 * x), y + (1 - c) * (1 - 2 * y))
        diag = (1 - x, 1 - y)
        return x, y, c, me, (first, second, diag)

    def halves(w, c):
        rh = items[w].shape[0] // 2
        return pl.ds(c * rh, rh), pl.ds((1 - c) * rh, rh)

    def start(ins, outs, ss, rs):
        x, y, c, me, chips = place()
        for w in range(len(items)):
            _remote(ins[w], outs[w].at[me], ss.at[w, 6], rs.at[w, 6], (x, y, 1 - c)).start()
            if w < nw:
                mine, _ = halves(w, c)
                _remote(ins[w].at[mine], outs[w].at[me, mine], ss.at[w, 0], rs.at[w, 0], (*chips[0], c)).start()
            else:
                for k in range(3):
                    _remote(ins[w], outs[w].at[me], ss.at[w, k], rs.at[w, k], (*chips[k], c)).start()

    def onward(outs, ss, rs, w, k, x, y, c, chips):
        mine, _ = halves(w, c)
        pk = 2 * chips[k][0] + chips[k][1]
        got = outs[w].at[pk, mine]
        src = chips[1] if k == 2 else chips[k]
        _remote(got, got, ss.at[w, k], rs.at[w, k], (*src, c)).wait_recv()
        if k == 0:
            _remote(got, got, ss.at[w, 2], rs.at[w, 2], (*chips[1], c)).start()
        _remote(got, got, ss.at[w, 3 + k], rs.at[w, 3 + k], (x, y, 1 - c)).start()

    def middle(ins, outs, ss, rs):
        x, y, c, me, chips = place()
        for w in range(nw):
            mine, _ = halves(w, c)
            _remote(ins[w].at[mine], outs[w].at[me, mine], ss.at[w, 1], rs.at[w, 1], (*chips[1], c)).start()
        for w in range(nw):
            onward(outs, ss, rs, w, 0, x, y, c, chips)

    def finish(ins, outs, ss, rs):
        x, y, c, me, chips = place()
        sib = (x, y, 1 - c)
        for k in (1, 2):
            for w in range(nw):
                onward(outs, ss, rs, w, k, x, y, c, chips)
        for w in range(len(items)):
            if w < nw:
                mine, theirs = halves(w, c)
                for k, chip in ((3, chips[1]), (4, chips[0]), (5, chips[2])):
                    oth = outs[w].at[2 * chip[0] + chip[1], theirs]
                    _remote(oth, oth, ss.at[w, k], rs.at[w, k], sib).wait_recv()
                own = ins[w].at[mine]
                for k in range(6):
                    _remote(own, own, ss.at[w, k], rs.at[w, k], sib).wait_send()
            else:
                for k in range(3):
                    got = outs[w].at[2 * chips[k][0] + chips[k][1]]
                    _remote(got, got, ss.at[w, k], rs.at[w, k], (*chips[k], c)).wait_recv()
                    _remote(ins[w], ins[w], ss.at[w, k], rs.at[w, k], sib).wait_send()
            _remote(ins[w], outs[w].at[me], ss.at[w, 6], rs.at[w, 6], sib).wait()

    out_shapes = tuple(jax.ShapeDtypeStruct((N_CHIPS,) + a.shape, a.dtype) for a in items)
    return _Exchange(items, out_shapes, (len(items), 7), start, finish, middle if nw else None)


def _chip_reduce(grads, name):
    nw = len(grads)
    step = 64

    def body(*refs):
        ins, outs = refs[:nw], refs[nw:2 * nw]
        own, got = refs[2 * nw:3 * nw], refs[3 * nw:4 * nw]
        send_sems, recv_sems, local_sems = refs[4 * nw:]
        x, y, c, _, _ = _place()
        moves = []
        for w in range(nw):
            rh = grads[w].shape[1] // 2
            away = _remote(ins[w].at[:, pl.ds((1 - c) * rh, rh), :], got[w], send_sems.at[w], recv_sems.at[w],
                           (x, y, 1 - c))
            mine = pltpu.make_async_copy(ins[w].at[:, pl.ds(c * rh, rh), :], own[w], local_sems.at[w])
            away.start()
            mine.start()
            moves.append((away, mine))
        back = []
        for w, (away, mine) in enumerate(moves):
            nb, rh, _ = own[w].shape
            mine.wait()
            away.wait()
            for k in range(nb):
                def add(i, carry, w=w, k=k):
                    rows = pl.ds(pl.multiple_of(i * step, step), step)
                    own[w][k, rows, :] = (own[w][k, rows, :].astype(F32) + got[w][k, rows, :].astype(F32)).astype(BF16)
                    return carry
                lax.fori_loop(0, rh // step, add, 0)
                tail = rh % step
                if tail:
                    rows = slice(rh - tail, rh)
                    own[w][k, rows, :] = (own[w][k, rows, :].astype(F32) + got[w][k, rows, :].astype(F32)).astype(BF16)
            wb = pltpu.make_async_copy(own[w], outs[w].at[:, pl.ds(c * rh, rh), :], local_sems.at[w])
            wb.start()
            back.append(wb)
        for wb in back:
            wb.wait()

    halves = [pltpu.VMEM((g.shape[0], g.shape[1] // 2, g.shape[2]), BF16) for g in grads]
    sem = pltpu.SemaphoreType.DMA((nw,))
    return pl.pallas_call(
        body, name=name, out_shape=tuple(jax.ShapeDtypeStruct(g.shape, BF16) for g in grads),
        in_specs=[ANY] * nw, out_specs=[ANY] * nw, scratch_shapes=halves + halves + [sem, sem, sem],
        compiler_params=pltpu.CompilerParams(vmem_limit_bytes=VMEM_LIMIT),
    )(*grads)


def _scatter_partials(parts):
    nw = len(parts)

    def copies(ins, outs, ss, rs):
        _, _, c, _, chips = _place()
        res = []
        for r, (px, py) in enumerate(chips):
            for w in range(nw):
                rh = parts[w].shape[1] // 2
                rows = pl.ds(c * rh, rh)
                res.append(_remote(ins[w].at[2 * px + py, rows], outs[w].at[r, rows], ss.at[w, r], rs.at[w, r],
                                   (px, py, c)))
        return res

    def start(ins, outs, ss, rs):
        for cp in copies(ins, outs, ss, rs):
            cp.start()

    def finish(ins, outs, ss, rs):
        for cp in copies(ins, outs, ss, rs):
            cp.wait()

    out_shapes = tuple(jax.ShapeDtypeStruct((3,) + p.shape[1:], p.dtype) for p in parts)
    return _Exchange(parts, out_shapes, (nw, 3), start, finish)


def _join_partials(parts, slots):
    nw = len(parts)

    def copies(outs, ss, rs, mine):
        x, y, c, me, _ = _place()
        res = []
        for w in range(nw):
            rh = parts[w].shape[1] // 2
            rows = pl.ds((c if mine else 1 - c) * rh, rh)
            own = outs[w].at[me, rows]
            got = outs[nw + w].at[:, rows, :]
            res.append(_remote(own, own, ss.at[w, 0], rs.at[w, 0], (x, y, 1 - c)))
            res.append(_remote(got, got, ss.at[w, 1], rs.at[w, 1], (x, y, 1 - c)))
        return res

    def start(ins, outs, ss, rs):
        for cp in copies(outs, ss, rs, True):
            cp.start()

    def finish(ins, outs, ss, rs):
        for cp in copies(outs, ss, rs, True):
            cp.wait_send()
        for cp in copies(outs, ss, rs, False):
            cp.wait_recv()

    arrays = tuple(parts) + tuple(slots)
    return _Exchange(arrays, tuple(jax.ShapeDtypeStruct(a.shape, a.dtype) for a in arrays), (nw, 2), start, finish,
                     in_place=True)


def _gather_small(slab):
    def copies(ins, outs, ss, rs):
        x, y, c, _, _ = _place()
        me = 4 * x + 2 * y + c
        own = pltpu.make_async_copy(ins[0], outs[0].at[me], ss.at[7])
        out, arrivals = [], []
        for k in range(1, 8):
            px = 1 - x if k & 4 else x
            py = 1 - y if k & 2 else y
            pc = 1 - c if k & 1 else c
            out.append(_remote(ins[0], outs[0].at[me], ss.at[k - 1], rs.at[k - 1], (px, py, pc)))
            theirs = outs[0].at[4 * px + 2 * py + pc]
            arrivals.append(_remote(theirs, theirs, ss.at[k - 1], rs.at[k - 1], (px, py, pc)))
        return own, out, arrivals

    def start(ins, outs, ss, rs):
        own, out, _ = copies(ins, outs, ss, rs)
        own.start()
        for cp in out:
            cp.start()

    def finish(ins, outs, ss, rs):
        own, out, arrivals = copies(ins, outs, ss, rs)
        for cp in out:
            cp.wait_send()
        for cp in arrivals:
            cp.wait_recv()
        own.wait()

    return _Exchange((slab,), (jax.ShapeDtypeStruct((8,) + slab.shape, slab.dtype),), (8,), start, finish)


_SMALL_VECS = ("ln_mix_g", "ln_attn_g", "ln_mem_g", "ln_ffn_g", "ln_final_g")


def _pack_small(p, extra, conv):
    d = p["ln_mix_g"].shape[-1]
    top = [p[k].reshape(1, d) for k in _SMALL_VECS]
    top.append(jnp.concatenate([p["sgu_ln_g"].reshape(-1), p["sgu_ln_b"].reshape(-1)]).reshape(1, d))
    top.append(jnp.concatenate([p["grp_norm_a"].reshape(-1), p["grp_norm_b"].reshape(-1)]).reshape(1, d))
    top.append(jnp.concatenate([p["b_spatial"].reshape(-1), extra]).reshape(1, d))
    mid = jnp.zeros((8, d), F32)
    if conv is not None:
        mid = jnp.pad(conv, ((0, 5), (0, d - conv.shape[1])))
    return jnp.concatenate([jnp.concatenate(top, axis=0), mid, p["w_spatial"].reshape(-1, d)], axis=0)


def _unpack_small(slab):
    d = slab.shape[1]
    hw = d // 2
    out = {k: slab[i] for i, k in enumerate(_SMALL_VECS)}
    out["sgu_ln_g"], out["sgu_ln_b"] = slab[5, :hw], slab[5, hw:]
    out["grp_norm_a"], out["grp_norm_b"] = slab[6, :hw], slab[6, hw:]
    out["b_spatial"] = slab[7, :hw].reshape(HEADS, CHUNK)
    out["w_spatial"] = slab[16:].reshape(HEADS, CHUNK, CHUNK)
    return out


_BIG = ("w_in", "w_kv", "w_gate_up", "w_out", "w_q", "w_o", "w_down")
_WEIGHTS = ("ln_mix_g", "w_in", "sgu_ln_g", "sgu_ln_b", "w_spatial", "b_spatial", "conv_w", "grp_norm_a",
            "grp_norm_b", "w_out", "ln_attn_g", "ln_mem_g", "w_q", "w_kv", "w_o", "ln_ffn_g", "w_gate_up",
            "w_down", "ln_final_g")


def _step(p, m_, v_, x, mem, target):
    s, d = x.shape
    hw = d // 2
    row = lambda a: a.reshape(1, -1)
    x_, y_, c_ = lax.axis_index("x"), lax.axis_index("y"), lax.axis_index("c")
    chip = 2 * x_ + y_

    bf = {k: p[k].astype(BF16) for k in _BIG}
    conv8 = jnp.pad(p["conv_w"], ((0, 5), (0, 0)))
    w_in, w_out4, conv4 = _run_exchange(_all_gather([bf["w_in"], bf["w_out"]], [conv8]), "all_gather_mixer")
    cw = jnp.transpose(conv4[:, :3, :], (1, 0, 2)).reshape(3, hw)
    b_t = jnp.pad(jnp.transpose(p["b_spatial"]), ((0, 0), (0, CHUNK - HEADS)))
    g1, g2, gm, g3, gf = (row(p[k]) for k in _SMALL_VECS)
    lng, lnb, ga, gb = row(p["sgu_ln_g"]), row(p["sgu_ln_b"]), row(p["grp_norm_a"]), row(p["grp_norm_b"])
    wsp = p["w_spatial"]
    w_out = w_out4.reshape(-1, d)

    (h, x1, ycat, xn1), (w_kv, w_q4, w_o4, w_down4) = _mixer_fwd(
        x, g1, w_in, lng, lnb, wsp, b_t, cw, ga, gb, w_out,
        hosted=_all_gather([bf[k] for k in ("w_kv", "w_q", "w_o", "w_down")]))
    w_q, w_o, w_down = (a.reshape(-1, d) for a in (w_q4, w_o4, w_down4))
    memn, kv = _kv_fwd(mem, gm, w_kv)
    (x2, o, xn2), (w_gu,) = _attn_fwd(x1, g2, w_q, kv, w_o, hosted=_all_gather([bf["w_gate_up"]]))
    dx2, act, dgu, xn3, dx3, loss, dgf, dg3, dx2b = _ffn_fwd_bwd(x2, g3, gf, target, w_gu, w_down)

    place = jnp.stack([c_, chip]).astype(jnp.int32)

    def chip_partials(names, grads, tag):
        return list(_chip_reduce(grads, "chip_reduce_" + tag))

    names_d = ("w_down",)
    parts_d = chip_partials(names_d, (_weight_grad(act, dx3, "grad_w_down", 1408, 512, False)[0],), "down")
    g_gu, slots_d = _weight_grad(xn3, dgu, "grad_w_gate_up", 512, 1408, True, hosted=_scatter_partials(parts_d))
    names_a = ("w_gate_up",)
    parts_a = chip_partials(names_a, (g_gu,), "ffn")
    (dx1, dq, dkv, dg2, dx1b), slots_a = _attn_bwd(x1, dx2, g2, w_q, kv, w_o, hosted=_scatter_partials(parts_a))
    g_kv, dgm = _kv_bwd(dkv, mem, memn, gm, w_kv)
    names_b = ("w_o", "w_out", "w_q", "w_kv")
    parts_b = chip_partials(names_b, (_weight_grad(o, dx2b, "grad_w_o", 1024, 512, False)[0],
                                      _weight_grad(ycat, dx1b, "grad_w_out", 1024, 512, False)[0],
                                      _weight_grad(xn2, dq, "grad_w_q", 1024, 512, False)[0], g_kv), "attn")
    (dx, dh, dg1, dlng, dlnb, dwsp, dbt, dcw, dga, dgb), slots_b = _mixer_bwd(
        x, dx1, h, g1, lng, lnb, wsp, b_t, cw, ga, gb, w_out, w_in, hosted=_scatter_partials(parts_b))
    small = {"ln_mix_g": dg1, "ln_attn_g": dg2, "ln_mem_g": dgm, "ln_ffn_g": dg3, "ln_final_g": dgf,
             "sgu_ln_g": dlng, "sgu_ln_b": dlnb, "grp_norm_a": dga, "grp_norm_b": dgb,
             "b_spatial": jnp.transpose(dbt[:, :HEADS]), "w_spatial": dwsp}
    loss_vec = jnp.pad(loss.reshape(1), (0, hw - 1))
    names = names_d + names_a + names_b
    n_done = len(names)
    g_in, extra = _weight_grad(
        xn1, dh, "grad_w_in", 1024, 640, True,
        hosted=[_gather_small(_pack_small(small, loss_vec, dcw)),
                _join_partials(parts_d + parts_a + parts_b, slots_d + slots_a + slots_b)])
    parts, whole = extra[0], dict(zip(names, zip(extra[1:1 + n_done], extra[1 + n_done:])))
    (part_in,) = chip_partials(("w_in",), (g_in,), "mixer")
    out_g, out_d, out_m, out_v = {}, {}, {}, {}

    def finalize(k, hosted=None):
        (out_g[k], out_d[k], out_m[k], out_v[k]), res = _finalize(
            whole[k][0], whole[k][1], p[k], m_[k], v_[k], place, "finalize_" + k, hosted)
        return res

    (slots_in,) = finalize("w_gate_up", _scatter_partials([part_in]))
    for k in names:
        if k != "w_gate_up":
            finalize(k)
    whole["w_in"] = _run_exchange(_join_partials([part_in], [slots_in]), "rs_join_mixer")
    finalize("w_in")

    zeros = jnp.zeros((hw,), F32)
    sg, sd, sm, sv = _small_sum_adamw(parts, _pack_small(p, zeros, None), _pack_small(m_, zeros, None),
                                      _pack_small(v_, zeros, None))
    for tree, slab in zip((out_g, out_d, out_m, out_v), (sg, sd, sm, sv)):
        tree.update(_unpack_small(slab))
    loss_out = sg[7, hw]
    g_conv = lax.dynamic_slice(sg[8:11, :hw], (0, chip * (hw // N_CHIPS)), (3, hw // N_CHIPS))
    out_g["conv_w"] = g_conv
    out_d["conv_w"], out_m["conv_w"], out_v["conv_w"] = _adamw(p["conv_w"], g_conv, m_["conv_w"], v_["conv_w"],
                                                                "adamw_conv_w")
    return loss_out, dx, out_g, out_d, out_m, out_v


def kernel(x, mem, ln_mix_g, w_in, sgu_ln_g, sgu_ln_b, w_spatial, b_spatial, conv_w, grp_norm_a, grp_norm_b, w_out, ln_attn_g, ln_mem_g, w_q, w_kv, w_o, ln_ffn_g, w_gate_up, w_down, ln_final_g, loss_target, m_ln_mix_g, m_w_in, m_sgu_ln_g, m_sgu_ln_b, m_w_spatial, m_b_spatial, m_conv_w, m_grp_norm_a, m_grp_norm_b, m_w_out, m_ln_attn_g, m_ln_mem_g, m_w_q, m_w_kv, m_w_o, m_ln_ffn_g, m_w_gate_up, m_w_down, m_ln_final_g, v_ln_mix_g, v_w_in, v_sgu_ln_g, v_sgu_ln_b, v_w_spatial, v_b_spatial, v_conv_w, v_grp_norm_a, v_grp_norm_b, v_w_out, v_ln_attn_g, v_ln_mem_g, v_w_q, v_w_kv, v_w_o, v_ln_ffn_g, v_w_gate_up, v_w_down, v_ln_final_g):
    p = dict(ln_mix_g=ln_mix_g, w_in=w_in, sgu_ln_g=sgu_ln_g, sgu_ln_b=sgu_ln_b, w_spatial=w_spatial,
             b_spatial=b_spatial, conv_w=conv_w, grp_norm_a=grp_norm_a, grp_norm_b=grp_norm_b, w_out=w_out,
             ln_attn_g=ln_attn_g, ln_mem_g=ln_mem_g, w_q=w_q, w_kv=w_kv, w_o=w_o, ln_ffn_g=ln_ffn_g,
             w_gate_up=w_gate_up, w_down=w_down, ln_final_g=ln_final_g)
    m_ = dict(ln_mix_g=m_ln_mix_g, w_in=m_w_in, sgu_ln_g=m_sgu_ln_g, sgu_ln_b=m_sgu_ln_b, w_spatial=m_w_spatial,
              b_spatial=m_b_spatial, conv_w=m_conv_w, grp_norm_a=m_grp_norm_a, grp_norm_b=m_grp_norm_b,
              w_out=m_w_out, ln_attn_g=m_ln_attn_g, ln_mem_g=m_ln_mem_g, w_q=m_w_q, w_kv=m_w_kv, w_o=m_w_o,
              ln_ffn_g=m_ln_ffn_g, w_gate_up=m_w_gate_up, w_down=m_w_down, ln_final_g=m_ln_final_g)
    v_ = dict(ln_mix_g=v_ln_mix_g, w_in=v_w_in, sgu_ln_g=v_sgu_ln_g, sgu_ln_b=v_sgu_ln_b, w_spatial=v_w_spatial,
              b_spatial=v_b_spatial, conv_w=v_conv_w, grp_norm_a=v_grp_norm_a, grp_norm_b=v_grp_norm_b,
              w_out=v_w_out, ln_attn_g=v_ln_attn_g, ln_mem_g=v_ln_mem_g, w_q=v_w_q, w_kv=v_w_kv, w_o=v_w_o,
              ln_ffn_g=v_ln_ffn_g, w_gate_up=v_w_gate_up, w_down=v_w_down, ln_final_g=v_ln_final_g)
    s, d = x.shape[-2], x.shape[-1]
    loss, dx, g, dl, nm, nv = _step(p, m_, v_, x.reshape(s, d), mem.reshape(-1, d), loss_target.reshape(s, d))
    outs = [loss, dx.reshape(x.shape)]
    for tree in (g, dl, nm, nv):
        outs += [tree[k].reshape(p[k].shape) for k in _WEIGHTS]
    return tuple(outs)
```

```python
import functools
import math

import jax
import jax.numpy as jnp
from jax import lax
from jax.experimental import pallas as pl
from jax.experimental.pallas import tpu as pltpu

F32 = jnp.float32
BF16 = jnp.bfloat16
EPS = 1e-6
CHUNK = 128
HEADS = 4
N_CHIPS = 4
TM = 512
TM_ATTN = 512
TM_FFN = 256
ADAM_LR, ADAM_B1, ADAM_B2, ADAM_EPS, ADAM_WD, ADAM_STEP = 0.001, 0.9, 0.999, 1e-08, 0.01, 10
GELU_C = math.sqrt(2.0 / math.pi)
GELU_K = 0.044715
SMALL_ROWS = 80
VMEM_LIMIT = 56 * 1024 * 1024
MIDDLE_STEP_16THS = 7
MESH = pl.DeviceIdType.MESH
ANY = pl.BlockSpec(memory_space=pl.ANY)


def _params(*sem):
    return pltpu.CompilerParams(dimension_semantics=sem, vmem_limit_bytes=VMEM_LIMIT)


def _dot(a, b):
    return jnp.dot(a, b, preferred_element_type=F32)


def _dot_nt(a, b):
    return lax.dot_general(a, b, (((1,), (1,)), ((), ())), preferred_element_type=F32)


def _dot_tn(a, b):
    return lax.dot_general(a, b, (((0,), (0,)), ((), ())), preferred_element_type=F32)


def _rms_fwd(x, g):
    r = lax.rsqrt(jnp.mean(x * x, axis=-1, keepdims=True) + EPS)
    xh = x * r
    return xh * g, xh, r


def _rms_bwd(dy, xh, r, g):
    dxh = dy * g
    dx = r * (dxh - xh * jnp.mean(dxh * xh, axis=-1, keepdims=True))
    return dx, jnp.sum(dy * xh, axis=0, keepdims=True)


def _full(shape):
    nd = len(shape)
    return pl.BlockSpec(shape, lambda *_: (0,) * nd, pipeline_mode=pl.Buffered(1))


def _acc(shape):
    nd = len(shape)
    return pl.BlockSpec(shape, lambda *_: (0,) * nd)


def _rows(tm, cols):
    return pl.BlockSpec((tm, cols), lambda i: (i, 0))


def _tril_weights(wsp_ref):
    row = lax.broadcasted_iota(jnp.int32, (CHUNK, CHUNK), 0)
    col = lax.broadcasted_iota(jnp.int32, (CHUNK, CHUNK), 1)
    return [jnp.where(row >= col, wsp_ref[hd], 0.0).astype(BF16) for hd in range(HEADS)]


def _shift_rows(z, zp):
    row = lax.broadcasted_iota(jnp.int32, z.shape, 0)
    zm1 = jnp.where(row == 0, zp[7:8, :], pltpu.roll(z, 1, 0))
    zm2 = jnp.where(row == 0, zp[6:7, :], jnp.where(row == 1, zp[7:8, :], pltpu.roll(z, 2, 0)))
    return zm1, zm2


def _gelu_parts(x):
    t = jnp.tanh(GELU_C * (x + GELU_K * (x * x * x)))
    return 0.5 * x * (1.0 + t), t


def _layer_norm_parts(v, g, b):
    mu = jnp.mean(v, axis=-1, keepdims=True)
    vc = v - mu
    rs = lax.rsqrt(jnp.mean(vc * vc, axis=-1, keepdims=True) + EPS)
    vhat = vc * rs
    return vhat * g + b, vhat, rs


def _kv_fwd(mem, g_mem, w_kv):
    m, d = mem.shape
    ns = w_kv.shape[2]

    def body(mem_ref, g_ref, w_ref, memn_ref, kv_ref):
        y, _, _ = _rms_fwd(mem_ref[...], g_ref[...])
        yb = y.astype(BF16)
        memn_ref[...] = yb
        for k in range(N_CHIPS):
            kv_ref[:, k * ns:(k + 1) * ns] = _dot(yb, w_ref[k]).astype(BF16)

    return pl.pallas_call(
        body, name="kv_fwd",
        out_shape=(jax.ShapeDtypeStruct((m, d), BF16), jax.ShapeDtypeStruct((m, N_CHIPS * ns), BF16)),
        compiler_params=pltpu.CompilerParams(vmem_limit_bytes=VMEM_LIMIT),
    )(mem, g_mem, w_kv)


def _mixer_fwd(x, g1, w_in, lng, lnb, wsp, b_t, cw, ga, gb, w_out, hosted=None):
    s, d = x.shape
    n = s // TM
    nch = TM // CHUNK
    ns = w_in.shape[2]
    nh = N_CHIPS * ns
    aw = d // 2
    hd_w = aw // HEADS

    def body(x_ref, g1_ref, win_ref, lng_ref, lnb_ref, wsp_ref, bt_ref, cw_ref, ga_ref, gb_ref, wout_ref,
             h_ref, x1_ref, y_ref, xn_ref, mix_ref, zp_ref):
        i = pl.program_id(0)

        @pl.when(i == 0)
        def _():
            zp_ref[...] = jnp.zeros_like(zp_ref)

        x = x_ref[...]
        xn, _, _ = _rms_fwd(x, g1_ref[...])
        xnb = xn.astype(BF16)
        xn_ref[...] = xnb
        for k in range(N_CHIPS):
            h_ref[:, k * ns:(k + 1) * ns] = _dot(xnb, win_ref[k])
        a, _ = _gelu_parts(h_ref[:, 0:2 * aw])
        u = a[:, :aw]
        vn, _, _ = _layer_norm_parts(a[:, aw:], lng_ref[...], lnb_ref[...])
        vnb = vn.astype(BF16)
        wm = _tril_weights(wsp_ref)
        for c in range(nch):
            for hd in range(HEADS):
                blk = vnb[c * CHUNK:(c + 1) * CHUNK, hd * hd_w:(hd + 1) * hd_w]
                mix_ref[c * CHUNK:(c + 1) * CHUNK, hd * hd_w:(hd + 1) * hd_w] = _dot(wm[hd], blk) + bt_ref[:, hd:hd + 1]
        ya, _, _ = _rms_fwd(u * mix_ref[...], ga_ref[...])
        g_b = h_ref[:, 2 * aw:3 * aw]
        z = h_ref[:, 3 * aw:4 * aw] * h_ref[:, 4 * aw:5 * aw]
        zm1, zm2 = _shift_rows(z, zp_ref[...])
        conv = cw_ref[0:1, :] * zm2 + cw_ref[1:2, :] * zm1 + cw_ref[2:3, :] * z
        yb, _, _ = _rms_fwd(g_b * conv, gb_ref[...])
        zp_ref[...] = z[TM - 8:TM, :]
        ycat = jnp.concatenate([ya, yb], axis=-1).astype(BF16)
        y_ref[...] = ycat
        x1_ref[...] = x + _dot(ycat, wout_ref[...])

    return _host_call(
        body, "mixer_fwd", (n,),
        [_rows(TM, d), _full(g1.shape), _full(w_in.shape), _full(lng.shape), _full(lnb.shape),
         _full(wsp.shape), _full(b_t.shape), _full(cw.shape), _full(ga.shape), _full(gb.shape),
         _full(w_out.shape)],
        [_rows(TM, nh), _rows(TM, d), _rows(TM, d), _rows(TM, d)],
        (jax.ShapeDtypeStruct((s, nh), F32), jax.ShapeDtypeStruct((s, d), F32),
         jax.ShapeDtypeStruct((s, d), BF16), jax.ShapeDtypeStruct((s, d), BF16)),
        [pltpu.VMEM((TM, aw), F32), pltpu.VMEM((8, aw), F32)],
        (x, g1, w_in, lng, lnb, wsp, b_t, cw, ga, gb, w_out), ("arbitrary",), hosted)


def _attn_probs(qb, kv_ref, hd, dh, d, scale):
    kh = kv_ref[:, hd * dh:(hd + 1) * dh]
    vh = kv_ref[:, d + hd * dh:d + (hd + 1) * dh]
    sc = _dot_nt(qb, kh) * scale
    sc = sc - jnp.max(sc, axis=-1, keepdims=True)
    e = jnp.exp(sc)
    return e / jnp.sum(e, axis=-1, keepdims=True), kh, vh


def _attn_fwd(x1, g2, w_q, kv, w_o, hosted=None):
    s, d = x1.shape
    tm = min(TM_ATTN, s)
    n = s // tm
    dh = d // HEADS
    scale = dh ** -0.5

    def body(x1_ref, g2_ref, wq_ref, kv_ref, wo_ref, x2_ref, o_ref):
        x1v = x1_ref[...]
        xn, _, _ = _rms_fwd(x1v, g2_ref[...])
        q = _dot(xn.astype(BF16), wq_ref[...])
        for hd in range(HEADS):
            qb = q[:, hd * dh:(hd + 1) * dh].astype(BF16)
            p, _, vh = _attn_probs(qb, kv_ref, hd, dh, d, scale)
            o_ref[:, hd * dh:(hd + 1) * dh] = _dot(p.astype(BF16), vh).astype(BF16)
        x2_ref[...] = x1v + _dot(o_ref[...], wo_ref[...])

    return _host_call(
        body, "attn_fwd", (n,),
        [_rows(tm, d), _full(g2.shape), _full(w_q.shape), _full(kv.shape), _full(w_o.shape)],
        [_rows(tm, d), _rows(tm, d)],
        (jax.ShapeDtypeStruct((s, d), F32), jax.ShapeDtypeStruct((s, d), BF16)),
        [], (x1, g2, w_q, kv, w_o), ("parallel",), hosted)


def _ffn_fwd_bwd(x2, g3, gf, target, w_gu, w_down):
    s, d = x2.shape
    tm = min(TM_FFN, s)
    n = s // tm
    ns = w_gu.shape[2]
    ff = 2 * ns

    def body(x2_ref, g3_ref, gf_ref, t_ref, wgu_ref, wd_ref,
             dx2_ref, act_ref, dgu_ref, xn_ref, dx3_ref, loss_ref, dgf_ref, dg3_ref):
        i = pl.program_id(0)

        @pl.when(i == 0)
        def _():
            loss_ref[...] = jnp.zeros_like(loss_ref)
            dgf_ref[...] = jnp.zeros_like(dgf_ref)
            dg3_ref[...] = jnp.zeros_like(dg3_ref)

        x2v = x2_ref[...]
        xn, xh3, r3 = _rms_fwd(x2v, g3_ref[...])
        xnb = xn.astype(BF16)
        xn_ref[...] = xnb
        x3 = x2v
        saved = []
        for j in range(2):
            g = _dot(xnb, wgu_ref[j])
            u = _dot(xnb, wgu_ref[2 + j])
            sg = 1.0 / (1.0 + jnp.exp(-g))
            sl = g * sg
            actb = (sl * u).astype(BF16)
            act_ref[:, j * ns:(j + 1) * ns] = actb
            x3 = x3 + _dot(actb, wd_ref[j * ns:(j + 1) * ns, :])
            saved.append((u, sl, sg * (1.0 + g * (1.0 - sg))))
        gfv = gf_ref[...]
        y, xhf, rf = _rms_fwd(x3, gfv)
        e = y - t_ref[...]
        loss_ref[...] += 0.5 * jnp.sum(jnp.sum(e * e, axis=-1, keepdims=True), axis=0, keepdims=True) / d
        dx3, dgf = _rms_bwd(e / d, xhf, rf, gfv)
        dgf_ref[...] += dgf
        dx3b = dx3.astype(BF16)
        dx3_ref[...] = dx3b
        dxn = jnp.zeros_like(x2v)
        for j in range(2):
            u, sl, dsl = saved[j]
            dact = _dot_nt(dx3b, wd_ref[j * ns:(j + 1) * ns, :])
            dgb = (dact * u * dsl).astype(BF16)
            dub = (dact * sl).astype(BF16)
            dgu_ref[:, j * ns:(j + 1) * ns] = dgb
            dgu_ref[:, ff + j * ns:ff + (j + 1) * ns] = dub
            dxn = dxn + _dot_nt(dgb, wgu_ref[j]) + _dot_nt(dub, wgu_ref[2 + j])
        dxr, dg3 = _rms_bwd(dxn, xh3, r3, g3_ref[...])
        dg3_ref[...] += dg3
        dx2_ref[...] = dx3 + dxr

    vec = jax.ShapeDtypeStruct((1, d), F32)
    return pl.pallas_call(
        body, name="ffn_fwd_bwd", grid=(n,),
        in_specs=[_rows(tm, d), _full(g3.shape), _full(gf.shape), _rows(tm, d), _full(w_gu.shape),
                  _full(w_down.shape)],
        out_specs=[_rows(tm, d), _rows(tm, ff), _rows(tm, 2 * ff), _rows(tm, d), _rows(tm, d),
                   _acc((1, 1)), _acc((1, d)), _acc((1, d))],
        out_shape=(jax.ShapeDtypeStruct((s, d), F32), jax.ShapeDtypeStruct((s, ff), BF16),
                   jax.ShapeDtypeStruct((s, 2 * ff), BF16), jax.ShapeDtypeStruct((s, d), BF16),
                   jax.ShapeDtypeStruct((s, d), BF16), jax.ShapeDtypeStruct((1, 1), F32), vec, vec),
        compiler_params=_params("arbitrary"),
    )(x2, g3, gf, target, w_gu, w_down)


def _attn_bwd(x1, dx2, o, ycat, g2, w_q, kv, w_o, hosted=None):
    s, d = x1.shape
    tm = min(TM_ATTN, s)
    n = s // tm
    dh = d // HEADS
    scale = dh ** -0.5
    m = kv.shape[0]

    def body(x1_ref, dx2_ref, o_ref, y_ref, g2_ref, wq_ref, kv_ref, wo_ref,
             dx1_ref, dkv_ref, dg2_ref, gwo_ref, gwq_ref, gwout_ref, dq_ref):
        i = pl.program_id(0)

        @pl.when(i == 0)
        def _():
            for r in (dkv_ref, dg2_ref, gwo_ref, gwq_ref, gwout_ref):
                r[...] = jnp.zeros_like(r)

        xn, xh2, r2 = _rms_fwd(x1_ref[...], g2_ref[...])
        xnb = xn.astype(BF16)
        q = _dot(xnb, wq_ref[...])
        dx2v = dx2_ref[...]
        dx2b = dx2v.astype(BF16)
        gwo_ref[...] += _dot_tn(o_ref[...], dx2b)
        do = _dot_nt(dx2b, wo_ref[...])
        for hd in range(HEADS):
            qb = q[:, hd * dh:(hd + 1) * dh].astype(BF16)
            p, kh, vh = _attn_probs(qb, kv_ref, hd, dh, d, scale)
            dob = do[:, hd * dh:(hd + 1) * dh].astype(BF16)
            dp = _dot_nt(dob, vh)
            ds = p * (dp - jnp.sum(dp * p, axis=-1, keepdims=True))
            dsb = (ds * scale).astype(BF16)
            dq_ref[:, hd * dh:(hd + 1) * dh] = _dot(dsb, kh).astype(BF16)
            dkv_ref[:, hd * dh:(hd + 1) * dh] += _dot_tn(dsb, qb)
            dkv_ref[:, d + hd * dh:d + (hd + 1) * dh] += _dot_tn(p.astype(BF16), dob)
        dqb = dq_ref[...]
        gwq_ref[...] += _dot_tn(xnb, dqb)
        dxn = _dot_nt(dqb, wq_ref[...])
        dxr, dg2 = _rms_bwd(dxn, xh2, r2, g2_ref[...])
        dg2_ref[...] += dg2
        dx1 = dx2v + dxr
        dx1_ref[...] = dx1
        gwout_ref[...] += _dot_tn(y_ref[...], dx1.astype(BF16))

    sq = jax.ShapeDtypeStruct((d, d), F32)
    return _host_call(
        body, "attn_bwd", (n,),
        [_rows(tm, d), _rows(tm, d), _rows(tm, d), _rows(tm, d), _full(g2.shape), _full(w_q.shape),
         _full(kv.shape), _full(w_o.shape)],
        [_rows(tm, d), _acc((m, 2 * d)), _acc((1, d)), _acc((d, d)), _acc((d, d)), _acc((d, d))],
        (jax.ShapeDtypeStruct((s, d), F32), jax.ShapeDtypeStruct((m, 2 * d), F32),
         jax.ShapeDtypeStruct((1, d), F32), sq, sq, sq),
        [pltpu.VMEM((tm, d), BF16)], (x1, dx2, o, ycat, g2, w_q, kv, w_o), ("arbitrary",), hosted)


def _kv_bwd(dkv, mem, memn, g_mem, w_kv):
    m, d = mem.shape
    ns = w_kv.shape[2]

    def body(dkv_ref, mem_ref, memn_ref, g_ref, w_ref, gw_ref, dg_ref):
        _, xh, _ = _rms_fwd(mem_ref[...], g_ref[...])
        dmemn = jnp.zeros((m, d), F32)
        for k in range(N_CHIPS):
            dkb = dkv_ref[:, k * ns:(k + 1) * ns].astype(BF16)
            gw_ref[k] = _dot_tn(memn_ref[...], dkb).astype(BF16)
            dmemn = dmemn + _dot_nt(dkb, w_ref[k])
        dg_ref[...] = jnp.sum(dmemn * xh, axis=0, keepdims=True)

    return pl.pallas_call(
        body, name="kv_bwd",
        out_shape=(jax.ShapeDtypeStruct((N_CHIPS, d, ns), BF16), jax.ShapeDtypeStruct((1, d), F32)),
        compiler_params=pltpu.CompilerParams(vmem_limit_bytes=VMEM_LIMIT),
    )(dkv, mem, memn, g_mem, w_kv)


def _mixer_bwd(x, dx1, h, g1, lng, lnb, wsp, b_t, cw, ga, gb, w_out, w_in, hosted=None):
    s, d = x.shape
    n = s // TM
    nch = TM // CHUNK
    ns = w_in.shape[2]
    nh = N_CHIPS * ns
    aw = d // 2
    hd_w = aw // HEADS

    def rev(cols):
        return pl.BlockSpec((TM, cols), lambda i: (n - 1 - i, 0))

    hprev = pl.BlockSpec((8, nh), lambda i: (jnp.maximum((n - 1 - i) * (TM // 8) - 1, 0), 0))

    def body(x_ref, dx1_ref, h_ref, hp_ref, g1_ref, lng_ref, lnb_ref, wsp_ref, bt_ref, cw_ref, ga_ref, gb_ref,
             wout_ref, win_ref,
             dx_ref, dh_ref, dg1_ref, dlng_ref, dlnb_ref, dwsp_ref, dbt_ref, dcw_ref, dga_ref, dgb_ref,
             mix_ref, dvn_ref, dcn_ref):
        i = pl.program_id(0)

        @pl.when(i == 0)
        def _():
            for r in (dg1_ref, dlng_ref, dlnb_ref, dwsp_ref, dbt_ref, dcw_ref, dga_ref, dgb_ref, dcn_ref):
                r[...] = jnp.zeros_like(r)

        dx1v = dx1_ref[...]
        dycat = _dot_nt(dx1v.astype(BF16), wout_ref[...])
        ha = h_ref[:, 0:2 * aw]
        a, th = _gelu_parts(ha)
        u = a[:, :aw]
        lngv = lng_ref[...]
        vn, vhat, rs = _layer_norm_parts(a[:, aw:], lngv, lnb_ref[...])
        vnb = vn.astype(BF16)
        wm = _tril_weights(wsp_ref)
        for c in range(nch):
            for hd in range(HEADS):
                blk = vnb[c * CHUNK:(c + 1) * CHUNK, hd * hd_w:(hd + 1) * hd_w]
                mix_ref[c * CHUNK:(c + 1) * CHUNK, hd * hd_w:(hd + 1) * hd_w] = _dot(wm[hd], blk) + bt_ref[:, hd:hd + 1]
        mixed = mix_ref[...]
        gav = ga_ref[...]
        _, yah, ra = _rms_fwd(u * mixed, gav)
        dya, dga = _rms_bwd(dycat[:, :aw], yah, ra, gav)
        dga_ref[...] += dga
        du = dya * mixed
        dmix = dya * u
        dmb = dmix.astype(BF16)
        tri = lax.broadcasted_iota(jnp.int32, (CHUNK, CHUNK), 0) >= lax.broadcasted_iota(jnp.int32, (CHUNK, CHUNK), 1)
        for hd in range(HEADS):
            dw = jnp.zeros((CHUNK, CHUNK), F32)
            db = jnp.zeros((CHUNK, 1), F32)
            for c in range(nch):
                rows = slice(c * CHUNK, (c + 1) * CHUNK)
                cols = slice(hd * hd_w, (hd + 1) * hd_w)
                dvn_ref[rows, cols] = _dot_tn(wm[hd], dmb[rows, cols])
                dw = dw + _dot_nt(dmb[rows, cols], vnb[rows, cols])
                db = db + jnp.sum(dmix[rows, cols], axis=1, keepdims=True)
            dwsp_ref[hd] += jnp.where(tri, dw, 0.0)
            dbt_ref[:, hd:hd + 1] += db
        dvn = dvn_ref[...]
        dlng_ref[...] += jnp.sum(dvn * vhat, axis=0, keepdims=True)
        dlnb_ref[...] += jnp.sum(dvn, axis=0, keepdims=True)
        dvh = dvn * lngv
        dv = rs * (dvh - jnp.mean(dvh, axis=-1, keepdims=True) - vhat * jnp.mean(dvh * vhat, axis=-1, keepdims=True))
        gprime = 0.5 * (1.0 + th) + 0.5 * ha * (1.0 - th * th) * (GELU_C * (1.0 + 3.0 * GELU_K * (ha * ha)))
        dh_ref[:, 0:2 * aw] = (jnp.concatenate([du, dv], axis=-1) * gprime).astype(BF16)
        g_b = h_ref[:, 2 * aw:3 * aw]
        g_c = h_ref[:, 3 * aw:4 * aw]
        val = h_ref[:, 4 * aw:5 * aw]
        z = g_c * val
        zp = jnp.where(i == n - 1, 0.0, hp_ref[:, 3 * aw:4 * aw] * hp_ref[:, 4 * aw:5 * aw])
        zm1, zm2 = _shift_rows(z, zp)
        cw0, cw1, cw2 = cw_ref[0:1, :], cw_ref[1:2, :], cw_ref[2:3, :]
        conv = cw0 * zm2 + cw1 * zm1 + cw2 * z
        gbv = gb_ref[...]
        _, ybh, rb = _rms_fwd(g_b * conv, gbv)
        dyb, dgb = _rms_bwd(dycat[:, aw:], ybh, rb, gbv)
        dgb_ref[...] += dgb
        dconv = dyb * g_b
        dcw_ref[0:1, :] += jnp.sum(dconv * zm2, axis=0, keepdims=True)
        dcw_ref[1:2, :] += jnp.sum(dconv * zm1, axis=0, keepdims=True)
        dcw_ref[2:3, :] += jnp.sum(dconv * z, axis=0, keepdims=True)
        nxt = dcn_ref[...]
        row = lax.broadcasted_iota(jnp.int32, dconv.shape, 0)
        dcp1 = jnp.where(row == TM - 1, nxt[0:1, :], pltpu.roll(dconv, TM - 1, 0))
        dcp2 = jnp.where(row == TM - 1, nxt[1:2, :],
                         jnp.where(row == TM - 2, nxt[0:1, :], pltpu.roll(dconv, TM - 2, 0)))
        dz = cw2 * dconv + cw1 * dcp1 + cw0 * dcp2
        dcn_ref[...] = dconv[0:8, :]
        dh_ref[:, 2 * aw:3 * aw] = (dyb * conv).astype(BF16)
        dh_ref[:, 3 * aw:4 * aw] = (dz * val).astype(BF16)
        dh_ref[:, 4 * aw:5 * aw] = (dz * g_c).astype(BF16)
        dxn = jnp.zeros((TM, d), F32)
        for k in range(N_CHIPS):
            dxn = dxn + _dot_nt(dh_ref[:, k * ns:(k + 1) * ns], win_ref[k])
        g1v = g1_ref[...]
        _, xh1, r1 = _rms_fwd(x_ref[...], g1v)
        dxr, dg1 = _rms_bwd(dxn, xh1, r1, g1v)
        dg1_ref[...] += dg1
        dx_ref[...] = dx1v + dxr

    ins = (x, dx1, h, h, g1, lng, lnb, wsp, b_t, cw, ga, gb, w_out, w_in)
    acc_shapes = [(1, d), (1, aw), (1, aw), wsp.shape, (CHUNK, CHUNK), cw.shape, (1, aw), (1, aw)]
    return _host_call(
        body, "mixer_bwd", (n,),
        [rev(d), rev(d), rev(nh), hprev] + [_full(a.shape) for a in ins[4:]],
        [rev(d), rev(nh)] + [_acc(sh) for sh in acc_shapes],
        (jax.ShapeDtypeStruct((s, d), F32), jax.ShapeDtypeStruct((s, nh), BF16))
        + tuple(jax.ShapeDtypeStruct(sh, F32) for sh in acc_shapes),
        [pltpu.VMEM((TM, aw), F32), pltpu.VMEM((TM, aw), F32), pltpu.VMEM((8, aw), F32)],
        ins, ("arbitrary",), hosted)


def _weight_grad(a, b, name, tm, tn, col_sharded, hosted=None):
    t, m = a.shape
    n = b.shape[1]

    def body(a_ref, b_ref, o_ref):
        o_ref[...] = _dot_tn(a_ref[...].astype(BF16), b_ref[...].astype(BF16)).astype(BF16)

    if col_sharded:
        ns = n // N_CHIPS
        per = ns // tn
        out_shape = jax.ShapeDtypeStruct((N_CHIPS, m, ns), BF16)
        out_spec = pl.BlockSpec((None, tm, tn), lambda i, j: (j // per, i, j % per))
    else:
        out_shape = jax.ShapeDtypeStruct((m, n), BF16)
        out_spec = pl.BlockSpec((tm, tn), lambda i, j: (i, j))
    (out,), extra = _host_call(
        body, name, (m // tm, n // tn),
        [pl.BlockSpec((t, tm), lambda i, j: (0, i)), pl.BlockSpec((t, tn), lambda i, j: (0, j))],
        [out_spec], (out_shape,), [], (a, b), ("parallel", "parallel"), hosted)
    return (out if col_sharded else out.reshape(N_CHIPS, m // N_CHIPS, n)), extra


def _row_tile(rows, cap=256):
    best = None
    for t in range(16, min(rows, cap) + 1, 16):
        if rows % t == 0:
            best = t
    return best if best is not None else rows


def _adamw_math(w, g, m, v):
    m2 = ADAM_B1 * m + (1.0 - ADAM_B1) * g
    v2 = ADAM_B2 * v + (1.0 - ADAM_B2) * (g * g)
    m_hat = m2 / (1.0 - ADAM_B1 ** ADAM_STEP)
    v_hat = v2 / (1.0 - ADAM_B2 ** ADAM_STEP)
    delta = -ADAM_LR * (m_hat / (jnp.sqrt(v_hat) + ADAM_EPS) + ADAM_WD * w)
    return delta, m2, v2


def _adamw(w, g, m, v, name):
    r, c = w.shape
    tr = _row_tile(r) if r >= 16 else r

    def body(w_ref, g_ref, m_ref, v_ref, d_ref, m2_ref, v2_ref):
        d_ref[...], m2_ref[...], v2_ref[...] = _adamw_math(w_ref[...], g_ref[...], m_ref[...], v_ref[...])

    sh = jax.ShapeDtypeStruct((r, c), F32)
    return pl.pallas_call(
        body, name=name, grid=(r // tr,),
        in_specs=[_rows(tr, c)] * 4, out_specs=[_rows(tr, c)] * 3, out_shape=(sh, sh, sh),
        compiler_params=_params("parallel"),
    )(w, g, m, v)


def _finalize(part, slots, w, m, v, place, name, hosted=None):
    r, c = w.shape
    tr = _row_tile(r)

    def body(place_ref, own_ref, s0_ref, s1_ref, s2_ref, w_ref, m_ref, v_ref, g_ref, d_ref, m2_ref, v2_ref):
        g = own_ref[...].astype(F32) + s0_ref[...].astype(F32)
        g = (g + s1_ref[...].astype(F32)) + s2_ref[...].astype(F32)
        g_ref[...] = g
        d_ref[...], m2_ref[...], v2_ref[...] = _adamw_math(w_ref[...], g, m_ref[...], v_ref[...])

    def slot(k):
        return pl.BlockSpec((None, tr, c), lambda i, pref: (k, i, 0))

    rows = pl.BlockSpec((tr, c), lambda i, pref: (i, 0))
    sh = jax.ShapeDtypeStruct((r, c), F32)
    return _host_call(
        body, name, (r // tr,),
        [pl.BlockSpec((None, tr, c), lambda i, pref: (pref[1], i, 0)), slot(0), slot(1), slot(2), rows, rows, rows],
        [rows] * 4, (sh,) * 4, [], (part, slots, slots, slots, w, m, v), ("parallel",), hosted, prefetch=(place,))


def _small_sum_adamw(parts, w, m, v):
    nd, r, c = parts.shape

    def body(p_ref, w_ref, m_ref, v_ref, g_ref, d_ref, m2_ref, v2_ref):
        g = p_ref[0]
        for k in range(1, nd):
            g = g + p_ref[k]
        g_ref[...] = g
        d_ref[...], m2_ref[...], v2_ref[...] = _adamw_math(w_ref[...], g, m_ref[...], v_ref[...])

    sh = jax.ShapeDtypeStruct((r, c), F32)
    return pl.pallas_call(
        body, name="small_sum_adamw", out_shape=(sh, sh, sh, sh),
        compiler_params=pltpu.CompilerParams(vmem_limit_bytes=VMEM_LIMIT),
    )(parts, w, m, v)


def _place():
    x, y, c = lax.axis_index("x"), lax.axis_index("y"), lax.axis_index("c")
    chips = [(1 - x, y), (x, 1 - y), (1 - x, 1 - y)]
    return x, y, c, 2 * x + y, chips


def _remote(src, dst, send_sem, recv_sem, to):
    return pltpu.make_async_remote_copy(src_ref=src, dst_ref=dst, send_sem=send_sem, recv_sem=recv_sem,
                                        device_id=to, device_id_type=MESH)


class _Exchange:
    def __init__(self, ins, out_shapes, sem_shape, start, finish, middle=None, in_place=False):
        self.ins, self.out_shapes, self.sem_shape = tuple(ins), tuple(out_shapes), sem_shape
        self.start, self.finish, self.middle = start, finish, middle
        self.in_place = in_place


def _run_exchange(ex, name):
    n_in, n_out = len(ex.ins), len(ex.out_shapes)

    def body(*refs):
        ins, outs = refs[:n_in], refs[n_in:n_in + n_out]
        send_sems, recv_sems = refs[n_in + n_out:]
        ex.start(ins, outs, send_sems, recv_sems)
        if ex.middle is not None:
            ex.middle(ins, outs, send_sems, recv_sems)
        ex.finish(ins, outs, send_sems, recv_sems)

    sem = pltpu.SemaphoreType.DMA(ex.sem_shape)
    return pl.pallas_call(
        body, name=name, out_shape=ex.out_shapes, in_specs=[ANY] * n_in, out_specs=[ANY] * n_out,
        input_output_aliases={k: k for k in range(n_in)} if ex.in_place else {}, scratch_shapes=[sem, sem],
    )(*ex.ins)


def _host_call(body, name, grid, in_specs, out_specs, out_shape, scratch_shapes, args, semantics, hosted,
               prefetch=()):
    hosted = [] if hosted is None else (list(hosted) if isinstance(hosted, (list, tuple)) else [hosted])
    n_pre, n_in, n_out, n_scr = len(prefetch), len(in_specs), len(out_specs), len(scratch_shapes)
    h_ins = [a for ex in hosted for a in ex.ins]
    h_outs = [s for ex in hosted for s in ex.out_shapes]
    h_in, h_out = len(h_ins), len(h_outs)

    def wrapped(*refs):
        pre, refs = refs[:n_pre], refs[n_pre:]
        a, hi = refs[:n_in], refs[n_in:n_in + h_in]
        o = refs[n_in + h_in:n_in + h_in + n_out]
        ho = refs[n_in + h_in + n_out:n_in + h_in + n_out + h_out]
        scr = refs[n_in + h_in + n_out + h_out:]

        def run(phase):
            i0 = o0 = 0
            for k, ex in enumerate(hosted):
                fn = getattr(ex, phase)
                if fn is not None:
                    fn(hi[i0:i0 + len(ex.ins)], ho[o0:o0 + len(ex.out_shapes)], scr[n_scr + 2 * k],
                       scr[n_scr + 2 * k + 1])
                i0, o0 = i0 + len(ex.ins), o0 + len(ex.out_shapes)

        if hosted:
            first = functools.reduce(jnp.logical_and, [pl.program_id(k) == 0 for k in range(len(grid))])

            @pl.when(first)
            def _():
                run("start")

        if any(ex.middle is not None for ex in hosted):
            half_way = functools.reduce(jnp.logical_and, [
                pl.program_id(0) == max(1, grid[0] * MIDDLE_STEP_16THS // 16)] + [
                pl.program_id(k) == 0 for k in range(1, len(grid))])

            @pl.when(half_way)
            def _():
                run("middle")

        body(*pre, *a, *o, *scr[:n_scr])

        if hosted:
            last = functools.reduce(jnp.logical_and, [pl.program_id(k) == grid[k] - 1 for k in range(len(grid))])

            @pl.when(last)
            def _():
                run("finish")

    sems = [pltpu.SemaphoreType.DMA(ex.sem_shape) for ex in hosted for _ in range(2)]
    aliases, i0, o0 = {}, n_pre + n_in, n_out
    for ex in hosted:
        if ex.in_place:
            aliases.update({i0 + k: o0 + k for k in range(len(ex.ins))})
        i0, o0 = i0 + len(ex.ins), o0 + len(ex.out_shapes)
    all_in, all_out = list(in_specs) + [ANY] * h_in, list(out_specs) + [ANY] * h_out
    all_scr = list(scratch_shapes) + sems
    params = _params(*(["arbitrary"] * len(grid) if hosted else semantics))
    shapes = tuple(out_shape) + tuple(h_outs)
    if n_pre:
        call = pl.pallas_call(
            wrapped, name=name, out_shape=shapes, input_output_aliases=aliases, compiler_params=params,
            grid_spec=pltpu.PrefetchScalarGridSpec(num_scalar_prefetch=n_pre, grid=grid, in_specs=all_in,
                                                   out_specs=all_out, scratch_shapes=all_scr))
    else:
        call = pl.pallas_call(
            wrapped, name=name, grid=grid, in_specs=all_in, out_specs=all_out, out_shape=shapes,
            scratch_shapes=all_scr, input_output_aliases=aliases, compiler_params=params)
    res = call(*prefetch, *args, *h_ins)
    return res[:n_out], res[n_out:]


def _all_gather(shards, small=()):
    items = tuple(shards) + tuple(small)
    nw = len(shards)

    def place():
        x, y, c, me, _ = _place()
        first = (x + (1 - c) * (1 - 2 * x), y + c * (1 - 2 * y))
        second = (x + c * (1 - 2 * x), y + (1 - c) * (1 - 2 * y))
        diag = (1 - x, 1 - y)
        return x, y, c, me, (first, second, diag)

    def halves(w, c):
        rh = items[w].shape[0] // 2
        return pl.ds(c * rh, rh), pl.ds((1 - c) * rh, rh)

    def start(ins, outs, ss, rs):
        x, y, c, me, chips = place()
        for w in range(len(items)):
            _remote(ins[w], outs[w].at[me], ss.at[w, 6], rs.at[w, 6], (x, y, 1 - c)).start()
            if w < nw:
                mine, _ = halves(w, c)
                _remote(ins[w].at[mine], outs[w].at[me, mine], ss.at[w, 0], rs.at[w, 0], (*chips[0], c)).start()
            else:
                for k in range(3):
                    _remote(ins[w], outs[w].at[me], ss.at[w, k], rs.at[w, k], (*chips[k], c)).start()

    def onward(outs, ss, rs, w, k, x, y, c, chips):
        mine, _ = halves(w, c)
        pk = 2 * chips[k][0] + chips[k][1]
        got = outs[w].at[pk, mine]
        src = chips[1] if k == 2 else chips[k]
        _remote(got, got, ss.at[w, k], rs.at[w, k], (*src, c)).wait_recv()
        if k == 0:
            _remote(got, got, ss.at[w, 2], rs.at[w, 2], (*chips[1], c)).start()
        _remote(got, got, ss.at[w, 3 + k], rs.at[w, 3 + k], (x, y, 1 - c)).start()

    def middle(ins, outs, ss, rs):
        x, y, c, me, chips = place()
        for w in range(nw):
            mine, _ = halves(w, c)
            _remote(ins[w].at[mine], outs[w].at[me, mine], ss.at[w, 1], rs.at[w, 1], (*chips[1], c)).start()
        for w in range(nw):
            onward(outs, ss, rs, w, 0, x, y, c, chips)

    def finish(ins, outs, ss, rs):
        x, y, c, me, chips = place()
        sib = (x, y, 1 - c)
        for k in (1, 2):
            for w in range(nw):
                onward(outs, ss, rs, w, k, x, y, c, chips)
        for w in range(len(items)):
            if w < nw:
                mine, theirs = halves(w, c)
                for k, chip in ((3, chips[1]), (4, chips[0]), (5, chips[2])):
                    oth = outs[w].at[2 * chip[0] + chip[1], theirs]
                    _remote(oth, oth, ss.at[w, k], rs.at[w, k], sib).wait_recv()
                own = ins[w].at[mine]
                for k in range(6):
                    _remote(own, own, ss.at[w, k], rs.at[w, k], sib).wait_send()
            else:
                for k in range(3):
                    got = outs[w].at[2 * chips[k][0] + chips[k][1]]
                    _remote(got, got, ss.at[w, k], rs.at[w, k], (*chips[k], c)).wait_recv()
                    _remote(ins[w], ins[w], ss.at[w, k], rs.at[w, k], sib).wait_send()
            _remote(ins[w], outs[w].at[me], ss.at[w, 6], rs.at[w, 6], sib).wait()

    out_shapes = tuple(jax.ShapeDtypeStruct((N_CHIPS,) + a.shape, a.dtype) for a in items)
    return _Exchange(items, out_shapes, (len(items), 7), start, finish, middle if nw else None)


def _chip_reduce(grads, name):
    nw = len(grads)
    step = 64

    def body(*refs):
        ins, outs = refs[:nw], refs[nw:2 * nw]
        own, got = refs[2 * nw:3 * nw], refs[3 * nw:4 * nw]
        send_sems, recv_sems, local_sems = refs[4 * nw:]
        x, y, c, _, _ = _place()
        moves = []
        for w in range(nw):
            rh = grads[w].shape[1] // 2
            away = _remote(ins[w].at[:, pl.ds((1 - c) * rh, rh), :], got[w], send_sems.at[w], recv_sems.at[w],
                           (x, y, 1 - c))
            mine = pltpu.make_async_copy(ins[w].at[:, pl.ds(c * rh, rh), :], own[w], local_sems.at[w])
            away.start()
            mine.start()
            moves.append((away, mine))
        back = []
        for w, (away, mine) in enumerate(moves):
            nb, rh, _ = own[w].shape
            mine.wait()
            away.wait()
            for k in range(nb):
                def add(i, carry, w=w, k=k):
                    rows = pl.ds(pl.multiple_of(i * step, step), step)
                    own[w][k, rows, :] = (own[w][k, rows, :].astype(F32) + got[w][k, rows, :].astype(F32)).astype(BF16)
                    return carry
                lax.fori_loop(0, rh // step, add, 0)
                tail = rh % step
                if tail:
                    rows = slice(rh - tail, rh)
                    own[w][k, rows, :] = (own[w][k, rows, :].astype(F32) + got[w][k, rows, :].astype(F32)).astype(BF16)
            wb = pltpu.make_async_copy(own[w], outs[w].at[:, pl.ds(c * rh, rh), :], local_sems.at[w])
            wb.start()
            back.append(wb)
        for wb in back:
            wb.wait()

    halves = [pltpu.VMEM((g.shape[0], g.shape[1] // 2, g.shape[2]), BF16) for g in grads]
    sem = pltpu.SemaphoreType.DMA((nw,))
    return pl.pallas_call(
        body, name=name, out_shape=tuple(jax.ShapeDtypeStruct(g.shape, BF16) for g in grads),
        in_specs=[ANY] * nw, out_specs=[ANY] * nw, scratch_shapes=halves + halves + [sem, sem, sem],
        compiler_params=pltpu.CompilerParams(vmem_limit_bytes=VMEM_LIMIT),
    )(*grads)


def _scatter_partials(parts):
    nw = len(parts)

    def copies(ins, outs, ss, rs):
        _, _, c, _, chips = _place()
        res = []
        for r, (px, py) in enumerate(chips):
            for w in range(nw):
                rh = parts[w].shape[1] // 2
                rows = pl.ds(c * rh, rh)
                res.append(_remote(ins[w].at[2 * px + py, rows], outs[w].at[r, rows], ss.at[w, r], rs.at[w, r],
                                   (px, py, c)))
        return res

    def start(ins, outs, ss, rs):
        for cp in copies(ins, outs, ss, rs):
            cp.start()

    def finish(ins, outs, ss, rs):
        for cp in copies(ins, outs, ss, rs):
            cp.wait()

    out_shapes = tuple(jax.ShapeDtypeStruct((3,) + p.shape[1:], p.dtype) for p in parts)
    return _Exchange(parts, out_shapes, (nw, 3), start, finish)


def _join_partials(parts, slots):
    nw = len(parts)

    def copies(outs, ss, rs, mine):
        x, y, c, me, _ = _place()
        res = []
        for w in range(nw):
            rh = parts[w].shape[1] // 2
            rows = pl.ds((c if mine else 1 - c) * rh, rh)
            own = outs[w].at[me, rows]
            got = outs[nw + w].at[:, rows, :]
            res.append(_remote(own, own, ss.at[w, 0], rs.at[w, 0], (x, y, 1 - c)))
            res.append(_remote(got, got, ss.at[w, 1], rs.at[w, 1], (x, y, 1 - c)))
        return res

    def start(ins, outs, ss, rs):
        for cp in copies(outs, ss, rs, True):
            cp.start()

    def finish(ins, outs, ss, rs):
        for cp in copies(outs, ss, rs, True):
            cp.wait_send()
        for cp in copies(outs, ss, rs, False):
            cp.wait_recv()

    arrays = tuple(parts) + tuple(slots)
    return _Exchange(arrays, tuple(jax.ShapeDtypeStruct(a.shape, a.dtype) for a in arrays), (nw, 2), start, finish,
                     in_place=True)


def _gather_small(slab):
    def copies(ins, outs, ss, rs):
        x, y, c, _, _ = _place()
        me = 4 * x + 2 * y + c
        own = pltpu.make_async_copy(ins[0], outs[0].at[me], ss.at[7])
        out, arrivals = [], []
        for k in range(1, 8):
            px = 1 - x if k & 4 else x
            py = 1 - y if k & 2 else y
            pc = 1 - c if k & 1 else c
            out.append(_remote(ins[0], outs[0].at[me], ss.at[k - 1], rs.at[k - 1], (px, py, pc)))
            theirs = outs[0].at[4 * px + 2 * py + pc]
            arrivals.append(_remote(theirs, theirs, ss.at[k - 1], rs.at[k - 1], (px, py, pc)))
        return own, out, arrivals

    def start(ins, outs, ss, rs):
        own, out, _ = copies(ins, outs, ss, rs)
        own.start()
        for cp in out:
            cp.start()

    def finish(ins, outs, ss, rs):
        own, out, arrivals = copies(ins, outs, ss, rs)
        for cp in out:
            cp.wait_send()
        for cp in arrivals:
            cp.wait_recv()
        own.wait()

    return _Exchange((slab,), (jax.ShapeDtypeStruct((8,) + slab.shape, slab.dtype),), (8,), start, finish)


_SMALL_VECS = ("ln_mix_g", "ln_attn_g", "ln_mem_g", "ln_ffn_g", "ln_final_g")


def _pack_small(p, extra, conv):
    d = p["ln_mix_g"].shape[-1]
    top = [p[k].reshape(1, d) for k in _SMALL_VECS]
    top.append(jnp.concatenate([p["sgu_ln_g"].reshape(-1), p["sgu_ln_b"].reshape(-1)]).reshape(1, d))
    top.append(jnp.concatenate([p["grp_norm_a"].reshape(-1), p["grp_norm_b"].reshape(-1)]).reshape(1, d))
    top.append(jnp.concatenate([p["b_spatial"].reshape(-1), extra]).reshape(1, d))
    mid = jnp.zeros((8, d), F32)
    if conv is not None:
        mid = jnp.pad(conv, ((0, 5), (0, d - conv.shape[1])))
    return jnp.concatenate([jnp.concatenate(top, axis=0), mid, p["w_spatial"].reshape(-1, d)], axis=0)


def _unpack_small(slab):
    d = slab.shape[1]
    hw = d // 2
    out = {k: slab[i] for i, k in enumerate(_SMALL_VECS)}
    out["sgu_ln_g"], out["sgu_ln_b"] = slab[5, :hw], slab[5, hw:]
    out["grp_norm_a"], out["grp_norm_b"] = slab[6, :hw], slab[6, hw:]
    out["b_spatial"] = slab[7, :hw].reshape(HEADS, CHUNK)
    out["w_spatial"] = slab[16:].reshape(HEADS, CHUNK, CHUNK)
    return out


_BIG = ("w_in", "w_kv", "w_gate_up", "w_out", "w_q", "w_o", "w_down")
_WEIGHTS = ("ln_mix_g", "w_in", "sgu_ln_g", "sgu_ln_b", "w_spatial", "b_spatial", "conv_w", "grp_norm_a",
            "grp_norm_b", "w_out", "ln_attn_g", "ln_mem_g", "w_q", "w_kv", "w_o", "ln_ffn_g", "w_gate_up",
            "w_down", "ln_final_g")


def _step(p, m_, v_, x, mem, target):
    s, d = x.shape
    hw = d // 2
    row = lambda a: a.reshape(1, -1)
    x_, y_, c_ = lax.axis_index("x"), lax.axis_index("y"), lax.axis_index("c")
    chip = 2 * x_ + y_

    bf = {k: p[k].astype(BF16) for k in _BIG}
    conv8 = jnp.pad(p["conv_w"], ((0, 5), (0, 0)))
    w_in, w_out4, conv4 = _run_exchange(_all_gather([bf["w_in"], bf["w_out"]], [conv8]), "all_gather_mixer")
    cw = jnp.transpose(conv4[:, :3, :], (1, 0, 2)).reshape(3, hw)
    b_t = jnp.pad(jnp.transpose(p["b_spatial"]), ((0, 0), (0, CHUNK - HEADS)))
    g1, g2, gm, g3, gf = (row(p[k]) for k in _SMALL_VECS)
    lng, lnb, ga, gb = row(p["sgu_ln_g"]), row(p["sgu_ln_b"]), row(p["grp_norm_a"]), row(p["grp_norm_b"])
    wsp = p["w_spatial"]
    w_out = w_out4.reshape(-1, d)

    (h, x1, ycat, xn1), (w_kv, w_q4, w_o4, w_down4) = _mixer_fwd(
        x, g1, w_in, lng, lnb, wsp, b_t, cw, ga, gb, w_out,
        hosted=_all_gather([bf[k] for k in ("w_kv", "w_q", "w_o", "w_down")]))
    w_q, w_o, w_down = (a.reshape(-1, d) for a in (w_q4, w_o4, w_down4))
    memn, kv = _kv_fwd(mem, gm, w_kv)
    (x2, o), (w_gu,) = _attn_fwd(x1, g2, w_q, kv, w_o, hosted=_all_gather([bf["w_gate_up"]]))
    dx2, act, dgu, xn3, dx3, loss, dgf, dg3 = _ffn_fwd_bwd(x2, g3, gf, target, w_gu, w_down)

    place = jnp.stack([c_, chip]).astype(jnp.int32)

    def chip_partials(names, grads, tag):
        return list(_chip_reduce(grads, "chip_reduce_" + tag))

    names_d = ("w_down",)
    parts_d = chip_partials(names_d, (_weight_grad(act, dx3, "grad_w_down", 1408, 512, False)[0],), "down")
    g_gu, slots_d = _weight_grad(xn3, dgu, "grad_w_gate_up", 512, 1408, True, hosted=_scatter_partials(parts_d))
    names_a = ("w_gate_up",)
    parts_a = chip_partials(names_a, (g_gu,), "ffn")
    (dx1, dkv, dg2, g_o, g_q, g_out), slots_a = _attn_bwd(x1, dx2, o, ycat, g2, w_q, kv, w_o,
                                                           hosted=_scatter_partials(parts_a))
    g_kv, dgm = _kv_bwd(dkv, mem, memn, gm, w_kv)
    names_b = ("w_o", "w_out", "w_q", "w_kv")
    shard_major = lambda g: g.astype(BF16).reshape(N_CHIPS, -1, d)
    parts_b = chip_partials(names_b, (shard_major(g_o), shard_major(g_out), shard_major(g_q), g_kv), "attn")
    (dx, dh, dg1, dlng, dlnb, dwsp, dbt, dcw, dga, dgb), slots_b = _mixer_bwd(
        x, dx1, h, g1, lng, lnb, wsp, b_t, cw, ga, gb, w_out, w_in, hosted=_scatter_partials(parts_b))
    small = {"ln_mix_g": dg1, "ln_attn_g": dg2, "ln_mem_g": dgm, "ln_ffn_g": dg3, "ln_final_g": dgf,
             "sgu_ln_g": dlng, "sgu_ln_b": dlnb, "grp_norm_a": dga, "grp_norm_b": dgb,
             "b_spatial": jnp.transpose(dbt[:, :HEADS]), "w_spatial": dwsp}
    loss_vec = jnp.pad(loss.reshape(1), (0, hw - 1))
    names = names_d + names_a + names_b
    n_done = len(names)
    g_in, extra = _weight_grad(
        xn1, dh, "grad_w_in", 1024, 640, True,
        hosted=[_gather_small(_pack_small(small, loss_vec, dcw)),
                _join_partials(parts_d + parts_a + parts_b, slots_d + slots_a + slots_b)])
    parts, whole = extra[0], dict(zip(names, zip(extra[1:1 + n_done], extra[1 + n_done:])))
    (part_in,) = chip_partials(("w_in",), (g_in,), "mixer")
    out_g, out_d, out_m, out_v = {}, {}, {}, {}

    def finalize(k, hosted=None):
        (out_g[k], out_d[k], out_m[k], out_v[k]), res = _finalize(
            whole[k][0], whole[k][1], p[k], m_[k], v_[k], place, "finalize_" + k, hosted)
        return res

    (slots_in,) = finalize("w_gate_up", _scatter_partials([part_in]))
    for k in names:
        if k != "w_gate_up":
            finalize(k)
    whole["w_in"] = _run_exchange(_join_partials([part_in], [slots_in]), "rs_join_mixer")
    finalize("w_in")

    zeros = jnp.zeros((hw,), F32)
    sg, sd, sm, sv = _small_sum_adamw(parts, _pack_small(p, zeros, None), _pack_small(m_, zeros, None),
                                      _pack_small(v_, zeros, None))
    for tree, slab in zip((out_g, out_d, out_m, out_v), (sg, sd, sm, sv)):
        tree.update(_unpack_small(slab))
    loss_out = sg[7, hw]
    g_conv = lax.dynamic_slice(sg[8:11, :hw], (0, chip * (hw // N_CHIPS)), (3, hw // N_CHIPS))
    out_g["conv_w"] = g_conv
    out_d["conv_w"], out_m["conv_w"], out_v["conv_w"] = _adamw(p["conv_w"], g_conv, m_["conv_w"], v_["conv_w"],
                                                                "adamw_conv_w")
    return loss_out, dx, out_g, out_d, out_m, out_v


def kernel(x, mem, ln_mix_g, w_in, sgu_ln_g, sgu_ln_b, w_spatial, b_spatial, conv_w, grp_norm_a, grp_norm_b, w_out, ln_attn_g, ln_mem_g, w_q, w_kv, w_o, ln_ffn_g, w_gate_up, w_down, ln_final_g, loss_target, m_ln_mix_g, m_w_in, m_sgu_ln_g, m_sgu_ln_b, m_w_spatial, m_b_spatial, m_conv_w, m_grp_norm_a, m_grp_norm_b, m_w_out, m_ln_attn_g, m_ln_mem_g, m_w_q, m_w_kv, m_w_o, m_ln_ffn_g, m_w_gate_up, m_w_down, m_ln_final_g, v_ln_mix_g, v_w_in, v_sgu_ln_g, v_sgu_ln_b, v_w_spatial, v_b_spatial, v_conv_w, v_grp_norm_a, v_grp_norm_b, v_w_out, v_ln_attn_g, v_ln_mem_g, v_w_q, v_w_kv, v_w_o, v_ln_ffn_g, v_w_gate_up, v_w_down, v_ln_final_g):
    p = dict(ln_mix_g=ln_mix_g, w_in=w_in, sgu_ln_g=sgu_ln_g, sgu_ln_b=sgu_ln_b, w_spatial=w_spatial,
             b_spatial=b_spatial, conv_w=conv_w, grp_norm_a=grp_norm_a, grp_norm_b=grp_norm_b, w_out=w_out,
             ln_attn_g=ln_attn_g, ln_mem_g=ln_mem_g, w_q=w_q, w_kv=w_kv, w_o=w_o, ln_ffn_g=ln_ffn_g,
             w_gate_up=w_gate_up, w_down=w_down, ln_final_g=ln_final_g)
    m_ = dict(ln_mix_g=m_ln_mix_g, w_in=m_w_in, sgu_ln_g=m_sgu_ln_g, sgu_ln_b=m_sgu_ln_b, w_spatial=m_w_spatial,
              b_spatial=m_b_spatial, conv_w=m_conv_w, grp_norm_a=m_grp_norm_a, grp_norm_b=m_grp_norm_b,
              w_out=m_w_out, ln_attn_g=m_ln_attn_g, ln_mem_g=m_ln_mem_g, w_q=m_w_q, w_kv=m_w_kv, w_o=m_w_o,
              ln_ffn_g=m_ln_ffn_g, w_gate_up=m_w_gate_up, w_down=m_w_down, ln_final_g=m_ln_final_g)
    v_ = dict(ln_mix_g=v_ln_mix_g, w_in=v_w_in, sgu_ln_g=v_sgu_ln_g, sgu_ln_b=v_sgu_ln_b, w_spatial=v_w_spatial,
              b_spatial=v_b_spatial, conv_w=v_conv_w, grp_norm_a=v_grp_norm_a, grp_norm_b=v_grp_norm_b,
              w_out=v_w_out, ln_attn_g=v_ln_attn_g, ln_mem_g=v_ln_mem_g, w_q=v_w_q, w_kv=v_w_kv, w_o=v_w_o,
              ln_ffn_g=v_ln_ffn_g, w_gate_up=v_w_gate_up, w_down=v_w_down, ln_final_g=v_ln_final_g)
    s, d = x.shape[-2], x.shape[-1]
    loss, dx, g, dl, nm, nv = _step(p, m_, v_, x.reshape(s, d), mem.reshape(-1, d), loss_target.reshape(s, d))
    outs = [loss, dx.reshape(x.shape)]
    for tree in (g, dl, nm, nv):
        outs += [tree[k].reshape(p[k].shape) for k in _WEIGHTS]
    return tuple(outs)
```

```python
import functools
import math

import jax
import jax.numpy as jnp
from jax import lax
from jax.experimental import pallas as pl
from jax.experimental.pallas import tpu as pltpu

F32 = jnp.float32
BF16 = jnp.bfloat16
EPS = 1e-6
CHUNK = 128
HEADS = 4
N_CHIPS = 4
TM = 512
TM_ATTN = 512
TM_FFN = 256
ADAM_LR, ADAM_B1, ADAM_B2, ADAM_EPS, ADAM_WD, ADAM_STEP = 0.001, 0.9, 0.999, 1e-08, 0.01, 10
GELU_C = math.sqrt(2.0 / math.pi)
GELU_K = 0.044715
SMALL_ROWS = 80
VMEM_LIMIT = 56 * 1024 * 1024
MIDDLE_STEP_16THS = 7
MESH = pl.DeviceIdType.MESH
ANY = pl.BlockSpec(memory_space=pl.ANY)


def _params(*sem):
    return pltpu.CompilerParams(dimension_semantics=sem, vmem_limit_bytes=VMEM_LIMIT)


def _dot(a, b):
    return jnp.dot(a, b, preferred_element_type=F32)


def _dot_nt(a, b):
    return lax.dot_general(a, b, (((1,), (1,)), ((), ())), preferred_element_type=F32)


def _dot_tn(a, b):
    return lax.dot_general(a, b, (((0,), (0,)), ((), ())), preferred_element_type=F32)


def _rms_fwd(x, g):
    r = lax.rsqrt(jnp.mean(x * x, axis=-1, keepdims=True) + EPS)
    xh = x * r
    return xh * g, xh, r


def _rms_bwd(dy, xh, r, g):
    dxh = dy * g
    dx = r * (dxh - xh * jnp.mean(dxh * xh, axis=-1, keepdims=True))
    return dx, jnp.sum(dy * xh, axis=0, keepdims=True)


def _full(shape):
    nd = len(shape)
    return pl.BlockSpec(shape, lambda *_: (0,) * nd, pipeline_mode=pl.Buffered(1))


def _acc(shape):
    nd = len(shape)
    return pl.BlockSpec(shape, lambda *_: (0,) * nd)


def _rows(tm, cols):
    return pl.BlockSpec((tm, cols), lambda i: (i, 0))


def _tril_weights(wsp_ref):
    row = lax.broadcasted_iota(jnp.int32, (CHUNK, CHUNK), 0)
    col = lax.broadcasted_iota(jnp.int32, (CHUNK, CHUNK), 1)
    return [jnp.where(row >= col, wsp_ref[hd], 0.0).astype(BF16) for hd in range(HEADS)]


def _shift_rows(z, zp):
    row = lax.broadcasted_iota(jnp.int32, z.shape, 0)
    zm1 = jnp.where(row == 0, zp[7:8, :], pltpu.roll(z, 1, 0))
    zm2 = jnp.where(row == 0, zp[6:7, :], jnp.where(row == 1, zp[7:8, :], pltpu.roll(z, 2, 0)))
    return zm1, zm2


def _gelu_parts(x):
    t = jnp.tanh(GELU_C * (x + GELU_K * (x * x * x)))
    return 0.5 * x * (1.0 + t), t


def _layer_norm_parts(v, g, b):
    mu = jnp.mean(v, axis=-1, keepdims=True)
    vc = v - mu
    rs = lax.rsqrt(jnp.mean(vc * vc, axis=-1, keepdims=True) + EPS)
    vhat = vc * rs
    return vhat * g + b, vhat, rs


def _kv_fwd(mem, g_mem, w_kv):
    m, d = mem.shape
    ns = w_kv.shape[2]

    def body(mem_ref, g_ref, w_ref, memn_ref, kv_ref):
        y, _, _ = _rms_fwd(mem_ref[...], g_ref[...])
        yb = y.astype(BF16)
        memn_ref[...] = yb
        for k in range(N_CHIPS):
            kv_ref[:, k * ns:(k + 1) * ns] = _dot(yb, w_ref[k]).astype(BF16)

    return pl.pallas_call(
        body, name="kv_fwd",
        out_shape=(jax.ShapeDtypeStruct((m, d), BF16), jax.ShapeDtypeStruct((m, N_CHIPS * ns), BF16)),
        compiler_params=pltpu.CompilerParams(vmem_limit_bytes=VMEM_LIMIT),
    )(mem, g_mem, w_kv)


def _mixer_fwd(x, g1, w_in, lng, lnb, wsp, b_t, cw, ga, gb, w_out, hosted=None):
    s, d = x.shape
    n = s // TM
    nch = TM // CHUNK
    ns = w_in.shape[2]
    nh = N_CHIPS * ns
    aw = d // 2
    hd_w = aw // HEADS

    def body(x_ref, g1_ref, win_ref, lng_ref, lnb_ref, wsp_ref, bt_ref, cw_ref, ga_ref, gb_ref, wout_ref,
             h_ref, x1_ref, y_ref, xn_ref, mix_ref, zp_ref):
        i = pl.program_id(0)

        @pl.when(i == 0)
        def _():
            zp_ref[...] = jnp.zeros_like(zp_ref)

        x = x_ref[...]
        xn, _, _ = _rms_fwd(x, g1_ref[...])
        xnb = xn.astype(BF16)
        xn_ref[...] = xnb
        for k in range(N_CHIPS):
            h_ref[:, k * ns:(k + 1) * ns] = _dot(xnb, win_ref[k])
        a, _ = _gelu_parts(h_ref[:, 0:2 * aw])
        u = a[:, :aw]
        vn, _, _ = _layer_norm_parts(a[:, aw:], lng_ref[...], lnb_ref[...])
        vnb = vn.astype(BF16)
        wm = _tril_weights(wsp_ref)
        for c in range(nch):
            for hd in range(HEADS):
                blk = vnb[c * CHUNK:(c + 1) * CHUNK, hd * hd_w:(hd + 1) * hd_w]
                mix_ref[c * CHUNK:(c + 1) * CHUNK, hd * hd_w:(hd + 1) * hd_w] = _dot(wm[hd], blk) + bt_ref[:, hd:hd + 1]
        ya, _, _ = _rms_fwd(u * mix_ref[...], ga_ref[...])
        g_b = h_ref[:, 2 * aw:3 * aw]
        z = h_ref[:, 3 * aw:4 * aw] * h_ref[:, 4 * aw:5 * aw]
        zm1, zm2 = _shift_rows(z, zp_ref[...])
        conv = cw_ref[0:1, :] * zm2 + cw_ref[1:2, :] * zm1 + cw_ref[2:3, :] * z
        yb, _, _ = _rms_fwd(g_b * conv, gb_ref[...])
        zp_ref[...] = z[TM - 8:TM, :]
        ycat = jnp.concatenate([ya, yb], axis=-1).astype(BF16)
        y_ref[...] = ycat
        x1_ref[...] = x + _dot(ycat, wout_ref[...])

    return _host_call(
        body, "mixer_fwd", (n,),
        [_rows(TM, d), _full(g1.shape), _full(w_in.shape), _full(lng.shape), _full(lnb.shape),
         _full(wsp.shape), _full(b_t.shape), _full(cw.shape), _full(ga.shape), _full(gb.shape),
         _full(w_out.shape)],
        [_rows(TM, nh), _rows(TM, d), _rows(TM, d), _rows(TM, d)],
        (jax.ShapeDtypeStruct((s, nh), F32), jax.ShapeDtypeStruct((s, d), F32),
         jax.ShapeDtypeStruct((s, d), BF16), jax.ShapeDtypeStruct((s, d), BF16)),
        [pltpu.VMEM((TM, aw), F32), pltpu.VMEM((8, aw), F32)],
        (x, g1, w_in, lng, lnb, wsp, b_t, cw, ga, gb, w_out), ("arbitrary",), hosted)


def _attn_probs(qb, kv_ref, hd, dh, d, scale):
    kh = kv_ref[:, hd * dh:(hd + 1) * dh]
    vh = kv_ref[:, d + hd * dh:d + (hd + 1) * dh]
    sc = _dot_nt(qb, kh) * scale
    sc = sc - jnp.max(sc, axis=-1, keepdims=True)
    e = jnp.exp(sc)
    return e / jnp.sum(e, axis=-1, keepdims=True), kh, vh


def _attn_fwd(x1, g2, w_q, kv, w_o, hosted=None):
    s, d = x1.shape
    tm = min(TM_ATTN, s)
    n = s // tm
    dh = d // HEADS
    scale = dh ** -0.5

    def body(x1_ref, g2_ref, wq_ref, kv_ref, wo_ref, x2_ref, o_ref):
        x1v = x1_ref[...]
        xn, _, _ = _rms_fwd(x1v, g2_ref[...])
        q = _dot(xn.astype(BF16), wq_ref[...])
        for hd in range(HEADS):
            qb = q[:, hd * dh:(hd + 1) * dh].astype(BF16)
            p, _, vh = _attn_probs(qb, kv_ref, hd, dh, d, scale)
            o_ref[:, hd * dh:(hd + 1) * dh] = _dot(p.astype(BF16), vh).astype(BF16)
        x2_ref[...] = x1v + _dot(o_ref[...], wo_ref[...])

    return _host_call(
        body, "attn_fwd", (n,),
        [_rows(tm, d), _full(g2.shape), _full(w_q.shape), _full(kv.shape), _full(w_o.shape)],
        [_rows(tm, d), _rows(tm, d)],
        (jax.ShapeDtypeStruct((s, d), F32), jax.ShapeDtypeStruct((s, d), BF16)),
        [], (x1, g2, w_q, kv, w_o), ("parallel",), hosted)


def _ffn_fwd_bwd(x2, g3, gf, target, w_gu, w_down):
    s, d = x2.shape
    tm = min(TM_FFN, s)
    n = s // tm
    ns = w_gu.shape[2]
    ff = 2 * ns

    def body(x2_ref, g3_ref, gf_ref, t_ref, wgu_ref, wd_ref,
             dx2_ref, act_ref, dgu_ref, xn_ref, dx3_ref, loss_ref, dgf_ref, dg3_ref):
        i = pl.program_id(0)

        @pl.when(i == 0)
        def _():
            loss_ref[...] = jnp.zeros_like(loss_ref)
            dgf_ref[...] = jnp.zeros_like(dgf_ref)
            dg3_ref[...] = jnp.zeros_like(dg3_ref)

        x2v = x2_ref[...]
        xn, xh3, r3 = _rms_fwd(x2v, g3_ref[...])
        xnb = xn.astype(BF16)
        xn_ref[...] = xnb
        x3 = x2v
        saved = []
        for j in range(2):
            g = _dot(xnb, wgu_ref[j])
            u = _dot(xnb, wgu_ref[2 + j])
            sg = 1.0 / (1.0 + jnp.exp(-g))
            sl = g * sg
            actb = (sl * u).astype(BF16)
            act_ref[:, j * ns:(j + 1) * ns] = actb
            x3 = x3 + _dot(actb, wd_ref[j * ns:(j + 1) * ns, :])
            saved.append((u, sl, sg * (1.0 + g * (1.0 - sg))))
        gfv = gf_ref[...]
        y, xhf, rf = _rms_fwd(x3, gfv)
        e = y - t_ref[...]
        loss_ref[...] += 0.5 * jnp.sum(jnp.sum(e * e, axis=-1, keepdims=True), axis=0, keepdims=True) / d
        dx3, dgf = _rms_bwd(e / d, xhf, rf, gfv)
        dgf_ref[...] += dgf
        dx3b = dx3.astype(BF16)
        dx3_ref[...] = dx3b
        dxn = jnp.zeros_like(x2v)
        for j in range(2):
            u, sl, dsl = saved[j]
            dact = _dot_nt(dx3b, wd_ref[j * ns:(j + 1) * ns, :])
            dgb = (dact * u * dsl).astype(BF16)
            dub = (dact * sl).astype(BF16)
            dgu_ref[:, j * ns:(j + 1) * ns] = dgb
            dgu_ref[:, ff + j * ns:ff + (j + 1) * ns] = dub
            dxn = dxn + _dot_nt(dgb, wgu_ref[j]) + _dot_nt(dub, wgu_ref[2 + j])
        dxr, dg3 = _rms_bwd(dxn, xh3, r3, g3_ref[...])
        dg3_ref[...] += dg3
        dx2_ref[...] = dx3 + dxr

    vec = jax.ShapeDtypeStruct((1, d), F32)
    return pl.pallas_call(
        body, name="ffn_fwd_bwd", grid=(n,),
        in_specs=[_rows(tm, d), _full(g3.shape), _full(gf.shape), _rows(tm, d), _full(w_gu.shape),
                  _full(w_down.shape)],
        out_specs=[_rows(tm, d), _rows(tm, ff), _rows(tm, 2 * ff), _rows(tm, d), _rows(tm, d),
                   _acc((1, 1)), _acc((1, d)), _acc((1, d))],
        out_shape=(jax.ShapeDtypeStruct((s, d), F32), jax.ShapeDtypeStruct((s, ff), BF16),
                   jax.ShapeDtypeStruct((s, 2 * ff), BF16), jax.ShapeDtypeStruct((s, d), BF16),
                   jax.ShapeDtypeStruct((s, d), BF16), jax.ShapeDtypeStruct((1, 1), F32), vec, vec),
        compiler_params=_params("arbitrary"),
    )(x2, g3, gf, target, w_gu, w_down)


def _attn_bwd(x1, dx2, o, ycat, g2, w_q, kv, w_o, hosted=None):
    s, d = x1.shape
    tm = min(TM_ATTN, s)
    n = s // tm
    dh = d // HEADS
    scale = dh ** -0.5
    m = kv.shape[0]

    def body(x1_ref, dx2_ref, o_ref, y_ref, g2_ref, wq_ref, kv_ref, wo_ref,
             dx1_ref, dkv_ref, dg2_ref, gwo_out, gwq_out, gwout_out, dq_ref, gwo_ref, gwq_ref, gwout_ref):
        i = pl.program_id(0)

        @pl.when(i == 0)
        def _():
            for r in (dkv_ref, dg2_ref, gwo_ref, gwq_ref, gwout_ref):
                r[...] = jnp.zeros_like(r)

        xn, xh2, r2 = _rms_fwd(x1_ref[...], g2_ref[...])
        xnb = xn.astype(BF16)
        q = _dot(xnb, wq_ref[...])
        dx2v = dx2_ref[...]
        dx2b = dx2v.astype(BF16)
        gwo_ref[...] += _dot_tn(o_ref[...], dx2b)
        do = _dot_nt(dx2b, wo_ref[...])
        for hd in range(HEADS):
            qb = q[:, hd * dh:(hd + 1) * dh].astype(BF16)
            p, kh, vh = _attn_probs(qb, kv_ref, hd, dh, d, scale)
            dob = do[:, hd * dh:(hd + 1) * dh].astype(BF16)
            dp = _dot_nt(dob, vh)
            ds = p * (dp - jnp.sum(dp * p, axis=-1, keepdims=True))
            dsb = (ds * scale).astype(BF16)
            dq_ref[:, hd * dh:(hd + 1) * dh] = _dot(dsb, kh).astype(BF16)
            dkv_ref[:, hd * dh:(hd + 1) * dh] += _dot_tn(dsb, qb)
            dkv_ref[:, d + hd * dh:d + (hd + 1) * dh] += _dot_tn(p.astype(BF16), dob)
        dqb = dq_ref[...]
        gwq_ref[...] += _dot_tn(xnb, dqb)
        dxn = _dot_nt(dqb, wq_ref[...])
        dxr, dg2 = _rms_bwd(dxn, xh2, r2, g2_ref[...])
        dg2_ref[...] += dg2
        dx1 = dx2v + dxr
        dx1_ref[...] = dx1
        gwout_ref[...] += _dot_tn(y_ref[...], dx1.astype(BF16))

        @pl.when(i == n - 1)
        def _():
            for acc, out in ((gwo_ref, gwo_out), (gwq_ref, gwq_out), (gwout_ref, gwout_out)):
                out[...] = acc[...].astype(BF16)

    sq = jax.ShapeDtypeStruct((d, d), BF16)
    return _host_call(
        body, "attn_bwd", (n,),
        [_rows(tm, d), _rows(tm, d), _rows(tm, d), _rows(tm, d), _full(g2.shape), _full(w_q.shape),
         _full(kv.shape), _full(w_o.shape)],
        [_rows(tm, d), _acc((m, 2 * d)), _acc((1, d)), _acc((d, d)), _acc((d, d)), _acc((d, d))],
        (jax.ShapeDtypeStruct((s, d), F32), jax.ShapeDtypeStruct((m, 2 * d), F32),
         jax.ShapeDtypeStruct((1, d), F32), sq, sq, sq),
        [pltpu.VMEM((tm, d), BF16)] + [pltpu.VMEM((d, d), F32)] * 3,
        (x1, dx2, o, ycat, g2, w_q, kv, w_o), ("arbitrary",), hosted)


def _kv_bwd(dkv, mem, memn, g_mem, w_kv):
    m, d = mem.shape
    ns = w_kv.shape[2]

    def body(dkv_ref, mem_ref, memn_ref, g_ref, w_ref, gw_ref, dg_ref):
        _, xh, _ = _rms_fwd(mem_ref[...], g_ref[...])
        dmemn = jnp.zeros((m, d), F32)
        for k in range(N_CHIPS):
            dkb = dkv_ref[:, k * ns:(k + 1) * ns].astype(BF16)
            gw_ref[k] = _dot_tn(memn_ref[...], dkb).astype(BF16)
            dmemn = dmemn + _dot_nt(dkb, w_ref[k])
        dg_ref[...] = jnp.sum(dmemn * xh, axis=0, keepdims=True)

    return pl.pallas_call(
        body, name="kv_bwd",
        out_shape=(jax.ShapeDtypeStruct((N_CHIPS, d, ns), BF16), jax.ShapeDtypeStruct((1, d), F32)),
        compiler_params=pltpu.CompilerParams(vmem_limit_bytes=VMEM_LIMIT),
    )(dkv, mem, memn, g_mem, w_kv)


def _mixer_bwd(x, dx1, h, g1, lng, lnb, wsp, b_t, cw, ga, gb, w_out, w_in, hosted=None):
    s, d = x.shape
    n = s // TM
    nch = TM // CHUNK
    ns = w_in.shape[2]
    nh = N_CHIPS * ns
    aw = d // 2
    hd_w = aw // HEADS

    def rev(cols):
        return pl.BlockSpec((TM, cols), lambda i: (n - 1 - i, 0))

    hprev = pl.BlockSpec((8, nh), lambda i: (jnp.maximum((n - 1 - i) * (TM // 8) - 1, 0), 0))

    def body(x_ref, dx1_ref, h_ref, hp_ref, g1_ref, lng_ref, lnb_ref, wsp_ref, bt_ref, cw_ref, ga_ref, gb_ref,
             wout_ref, win_ref,
             dx_ref, dh_ref, dg1_ref, dlng_ref, dlnb_ref, dwsp_ref, dbt_ref, dcw_ref, dga_ref, dgb_ref,
             mix_ref, dvn_ref, dcn_ref):
        i = pl.program_id(0)

        @pl.when(i == 0)
        def _():
            for r in (dg1_ref, dlng_ref, dlnb_ref, dwsp_ref, dbt_ref, dcw_ref, dga_ref, dgb_ref, dcn_ref):
                r[...] = jnp.zeros_like(r)

        dx1v = dx1_ref[...]
        dycat = _dot_nt(dx1v.astype(BF16), wout_ref[...])
        ha = h_ref[:, 0:2 * aw]
        a, th = _gelu_parts(ha)
        u = a[:, :aw]
        lngv = lng_ref[...]
        vn, vhat, rs = _layer_norm_parts(a[:, aw:], lngv, lnb_ref[...])
        vnb = vn.astype(BF16)
        wm = _tril_weights(wsp_ref)
        for c in range(nch):
            for hd in range(HEADS):
                blk = vnb[c * CHUNK:(c + 1) * CHUNK, hd * hd_w:(hd + 1) * hd_w]
                mix_ref[c * CHUNK:(c + 1) * CHUNK, hd * hd_w:(hd + 1) * hd_w] = _dot(wm[hd], blk) + bt_ref[:, hd:hd + 1]
        mixed = mix_ref[...]
        gav = ga_ref[...]
        _, yah, ra = _rms_fwd(u * mixed, gav)
        dya, dga = _rms_bwd(dycat[:, :aw], yah, ra, gav)
        dga_ref[...] += dga
        du = dya * mixed
        dmix = dya * u
        dmb = dmix.astype(BF16)
        tri = lax.broadcasted_iota(jnp.int32, (CHUNK, CHUNK), 0) >= lax.broadcasted_iota(jnp.int32, (CHUNK, CHUNK), 1)
        for hd in range(HEADS):
            dw = jnp.zeros((CHUNK, CHUNK), F32)
            db = jnp.zeros((CHUNK, 1), F32)
            for c in range(nch):
                rows = slice(c * CHUNK, (c + 1) * CHUNK)
                cols = slice(hd * hd_w, (hd + 1) * hd_w)
                dvn_ref[rows, cols] = _dot_tn(wm[hd], dmb[rows, cols])
                dw = dw + _dot_nt(dmb[rows, cols], vnb[rows, cols])
                db = db + jnp.sum(dmix[rows, cols], axis=1, keepdims=True)
            dwsp_ref[hd] += jnp.where(tri, dw, 0.0)
            dbt_ref[:, hd:hd + 1] += db
        dvn = dvn_ref[...]
        dlng_ref[...] += jnp.sum(dvn * vhat, axis=0, keepdims=True)
        dlnb_ref[...] += jnp.sum(dvn, axis=0, keepdims=True)
        dvh = dvn * lngv
        dv = rs * (dvh - jnp.mean(dvh, axis=-1, keepdims=True) - vhat * jnp.mean(dvh * vhat, axis=-1, keepdims=True))
        gprime = 0.5 * (1.0 + th) + 0.5 * ha * (1.0 - th * th) * (GELU_C * (1.0 + 3.0 * GELU_K * (ha * ha)))
        dh_ref[:, 0:2 * aw] = (jnp.concatenate([du, dv], axis=-1) * gprime).astype(BF16)
        g_b = h_ref[:, 2 * aw:3 * aw]
        g_c = h_ref[:, 3 * aw:4 * aw]
        val = h_ref[:, 4 * aw:5 * aw]
        z = g_c * val
        zp = jnp.where(i == n - 1, 0.0, hp_ref[:, 3 * aw:4 * aw] * hp_ref[:, 4 * aw:5 * aw])
        zm1, zm2 = _shift_rows(z, zp)
        cw0, cw1, cw2 = cw_ref[0:1, :], cw_ref[1:2, :], cw_ref[2:3, :]
        conv = cw0 * zm2 + cw1 * zm1 + cw2 * z
        gbv = gb_ref[...]
        _, ybh, rb = _rms_fwd(g_b * conv, gbv)
        dyb, dgb = _rms_bwd(dycat[:, aw:], ybh, rb, gbv)
        dgb_ref[...] += dgb
        dconv = dyb * g_b
        dcw_ref[0:1, :] += jnp.sum(dconv * zm2, axis=0, keepdims=True)
        dcw_ref[1:2, :] += jnp.sum(dconv * zm1, axis=0, keepdims=True)
        dcw_ref[2:3, :] += jnp.sum(dconv * z, axis=0, keepdims=True)
        nxt = dcn_ref[...]
        row = lax.broadcasted_iota(jnp.int32, dconv.shape, 0)
        dcp1 = jnp.where(row == TM - 1, nxt[0:1, :], pltpu.roll(dconv, TM - 1, 0))
        dcp2 = jnp.where(row == TM - 1, nxt[1:2, :],
                         jnp.where(row == TM - 2, nxt[0:1, :], pltpu.roll(dconv, TM - 2, 0)))
        dz = cw2 * dconv + cw1 * dcp1 + cw0 * dcp2
        dcn_ref[...] = dconv[0:8, :]
        dh_ref[:, 2 * aw:3 * aw] = (dyb * conv).astype(BF16)
        dh_ref[:, 3 * aw:4 * aw] = (dz * val).astype(BF16)
        dh_ref[:, 4 * aw:5 * aw] = (dz * g_c).astype(BF16)
        dxn = jnp.zeros((TM, d), F32)
        for k in range(N_CHIPS):
            dxn = dxn + _dot_nt(dh_ref[:, k * ns:(k + 1) * ns], win_ref[k])
        g1v = g1_ref[...]
        _, xh1, r1 = _rms_fwd(x_ref[...], g1v)
        dxr, dg1 = _rms_bwd(dxn, xh1, r1, g1v)
        dg1_ref[...] += dg1
        dx_ref[...] = dx1v + dxr

    ins = (x, dx1, h, h, g1, lng, lnb, wsp, b_t, cw, ga, gb, w_out, w_in)
    acc_shapes = [(1, d), (1, aw), (1, aw), wsp.shape, (CHUNK, CHUNK), cw.shape, (1, aw), (1, aw)]
    return _host_call(
        body, "mixer_bwd", (n,),
        [rev(d), rev(d), rev(nh), hprev] + [_full(a.shape) for a in ins[4:]],
        [rev(d), rev(nh)] + [_acc(sh) for sh in acc_shapes],
        (jax.ShapeDtypeStruct((s, d), F32), jax.ShapeDtypeStruct((s, nh), BF16))
        + tuple(jax.ShapeDtypeStruct(sh, F32) for sh in acc_shapes),
        [pltpu.VMEM((TM, aw), F32), pltpu.VMEM((TM, aw), F32), pltpu.VMEM((8, aw), F32)],
        ins, ("arbitrary",), hosted)


def _weight_grad(a, b, name, tm, tn, col_sharded, hosted=None):
    t, m = a.shape
    n = b.shape[1]

    def body(a_ref, b_ref, o_ref):
        o_ref[...] = _dot_tn(a_ref[...].astype(BF16), b_ref[...].astype(BF16)).astype(BF16)

    if col_sharded:
        ns = n // N_CHIPS
        per = ns // tn
        out_shape = jax.ShapeDtypeStruct((N_CHIPS, m, ns), BF16)
        out_spec = pl.BlockSpec((None, tm, tn), lambda i, j: (j // per, i, j % per))
    else:
        out_shape = jax.ShapeDtypeStruct((m, n), BF16)
        out_spec = pl.BlockSpec((tm, tn), lambda i, j: (i, j))
    (out,), extra = _host_call(
        body, name, (m // tm, n // tn),
        [pl.BlockSpec((t, tm), lambda i, j: (0, i)), pl.BlockSpec((t, tn), lambda i, j: (0, j))],
        [out_spec], (out_shape,), [], (a, b), ("parallel", "parallel"), hosted)
    return (out if col_sharded else out.reshape(N_CHIPS, m // N_CHIPS, n)), extra


def _row_tile(rows, cap=256):
    best = None
    for t in range(16, min(rows, cap) + 1, 16):
        if rows % t == 0:
            best = t
    return best if best is not None else rows


def _adamw_math(w, g, m, v):
    m2 = ADAM_B1 * m + (1.0 - ADAM_B1) * g
    v2 = ADAM_B2 * v + (1.0 - ADAM_B2) * (g * g)
    m_hat = m2 / (1.0 - ADAM_B1 ** ADAM_STEP)
    v_hat = v2 / (1.0 - ADAM_B2 ** ADAM_STEP)
    delta = -ADAM_LR * (m_hat / (jnp.sqrt(v_hat) + ADAM_EPS) + ADAM_WD * w)
    return delta, m2, v2


def _adamw(w, g, m, v, name):
    r, c = w.shape
    tr = _row_tile(r) if r >= 16 else r

    def body(w_ref, g_ref, m_ref, v_ref, d_ref, m2_ref, v2_ref):
        d_ref[...], m2_ref[...], v2_ref[...] = _adamw_math(w_ref[...], g_ref[...], m_ref[...], v_ref[...])

    sh = jax.ShapeDtypeStruct((r, c), F32)
    return pl.pallas_call(
        body, name=name, grid=(r // tr,),
        in_specs=[_rows(tr, c)] * 4, out_specs=[_rows(tr, c)] * 3, out_shape=(sh, sh, sh),
        compiler_params=_params("parallel"),
    )(w, g, m, v)


def _finalize(part, slots, w, m, v, place, name, hosted=None):
    r, c = w.shape
    tr = _row_tile(r)

    def body(place_ref, own_ref, s0_ref, s1_ref, s2_ref, w_ref, m_ref, v_ref, g_ref, d_ref, m2_ref, v2_ref):
        g = own_ref[...].astype(F32) + s0_ref[...].astype(F32)
        g = (g + s1_ref[...].astype(F32)) + s2_ref[...].astype(F32)
        g_ref[...] = g
        d_ref[...], m2_ref[...], v2_ref[...] = _adamw_math(w_ref[...], g, m_ref[...], v_ref[...])

    def slot(k):
        return pl.BlockSpec((None, tr, c), lambda i, pref: (k, i, 0))

    rows = pl.BlockSpec((tr, c), lambda i, pref: (i, 0))
    sh = jax.ShapeDtypeStruct((r, c), F32)
    return _host_call(
        body, name, (r // tr,),
        [pl.BlockSpec((None, tr, c), lambda i, pref: (pref[1], i, 0)), slot(0), slot(1), slot(2), rows, rows, rows],
        [rows] * 4, (sh,) * 4, [], (part, slots, slots, slots, w, m, v), ("parallel",), hosted, prefetch=(place,))


def _small_sum_adamw(parts, w, m, v):
    nd, r, c = parts.shape

    def body(p_ref, w_ref, m_ref, v_ref, g_ref, d_ref, m2_ref, v2_ref):
        g = p_ref[0]
        for k in range(1, nd):
            g = g + p_ref[k]
        g_ref[...] = g
        d_ref[...], m2_ref[...], v2_ref[...] = _adamw_math(w_ref[...], g, m_ref[...], v_ref[...])

    sh = jax.ShapeDtypeStruct((r, c), F32)
    return pl.pallas_call(
        body, name="small_sum_adamw", out_shape=(sh, sh, sh, sh),
        compiler_params=pltpu.CompilerParams(vmem_limit_bytes=VMEM_LIMIT),
    )(parts, w, m, v)


def _place():
    x, y, c = lax.axis_index("x"), lax.axis_index("y"), lax.axis_index("c")
    chips = [(1 - x, y), (x, 1 - y), (1 - x, 1 - y)]
    return x, y, c, 2 * x + y, chips


def _remote(src, dst, send_sem, recv_sem, to):
    return pltpu.make_async_remote_copy(src_ref=src, dst_ref=dst, send_sem=send_sem, recv_sem=recv_sem,
                                        device_id=to, device_id_type=MESH)


class _Exchange:
    def __init__(self, ins, out_shapes, sem_shape, start, finish, middle=None, in_place=False):
        self.ins, self.out_shapes, self.sem_shape = tuple(ins), tuple(out_shapes), sem_shape
        self.start, self.finish, self.middle = start, finish, middle
        self.in_place = in_place


def _run_exchange(ex, name):
    n_in, n_out = len(ex.ins), len(ex.out_shapes)

    def body(*refs):
        ins, outs = refs[:n_in], refs[n_in:n_in + n_out]
        send_sems, recv_sems = refs[n_in + n_out:]
        ex.start(ins, outs, send_sems, recv_sems)
        if ex.middle is not None:
            ex.middle(ins, outs, send_sems, recv_sems)
        ex.finish(ins, outs, send_sems, recv_sems)

    sem = pltpu.SemaphoreType.DMA(ex.sem_shape)
    return pl.pallas_call(
        body, name=name, out_shape=ex.out_shapes, in_specs=[ANY] * n_in, out_specs=[ANY] * n_out,
        input_output_aliases={k: k for k in range(n_in)} if ex.in_place else {}, scratch_shapes=[sem, sem],
    )(*ex.ins)


def _host_call(body, name, grid, in_specs, out_specs, out_shape, scratch_shapes, args, semantics, hosted,
               prefetch=()):
    hosted = [] if hosted is None else (list(hosted) if isinstance(hosted, (list, tuple)) else [hosted])
    n_pre, n_in, n_out, n_scr = len(prefetch), len(in_specs), len(out_specs), len(scratch_shapes)
    h_ins = [a for ex in hosted for a in ex.ins]
    h_outs = [s for ex in hosted for s in ex.out_shapes]
    h_in, h_out = len(h_ins), len(h_outs)

    def wrapped(*refs):
        pre, refs = refs[:n_pre], refs[n_pre:]
        a, hi = refs[:n_in], refs[n_in:n_in + h_in]
        o = refs[n_in + h_in:n_in + h_in + n_out]
        ho = refs[n_in + h_in + n_out:n_in + h_in + n_out + h_out]
        scr = refs[n_in + h_in + n_out + h_out:]

        def run(phase):
            i0 = o0 = 0
            for k, ex in enumerate(hosted):
                fn = getattr(ex, phase)
                if fn is not None:
                    fn(hi[i0:i0 + len(ex.ins)], ho[o0:o0 + len(ex.out_shapes)], scr[n_scr + 2 * k],
                       scr[n_scr + 2 * k + 1])
                i0, o0 = i0 + len(ex.ins), o0 + len(ex.out_shapes)

        if hosted:
            first = functools.reduce(jnp.logical_and, [pl.program_id(k) == 0 for k in range(len(grid))])

            @pl.when(first)
            def _():
                run("start")

        if any(ex.middle is not None for ex in hosted):
            half_way = functools.reduce(jnp.logical_and, [
                pl.program_id(0) == max(1, grid[0] * MIDDLE_STEP_16THS // 16)] + [
                pl.program_id(k) == 0 for k in range(1, len(grid))])

            @pl.when(half_way)
            def _():
                run("middle")

        body(*pre, *a, *o, *scr[:n_scr])

        if hosted:
            last = functools.reduce(jnp.logical_and, [pl.program_id(k) == grid[k] - 1 for k in range(len(grid))])

            @pl.when(last)
            def _():
                run("finish")

    sems = [pltpu.SemaphoreType.DMA(ex.sem_shape) for ex in hosted for _ in range(2)]
    aliases, i0, o0 = {}, n_pre + n_in, n_out
    for ex in hosted:
        if ex.in_place:
            aliases.update({i0 + k: o0 + k for k in range(len(ex.ins))})
        i0, o0 = i0 + len(ex.ins), o0 + len(ex.out_shapes)
    all_in, all_out = list(in_specs) + [ANY] * h_in, list(out_specs) + [ANY] * h_out
    all_scr = list(scratch_shapes) + sems
    params = _params(*(["arbitrary"] * len(grid) if hosted else semantics))
    shapes = tuple(out_shape) + tuple(h_outs)
    if n_pre:
        call = pl.pallas_call(
            wrapped, name=name, out_shape=shapes, input_output_aliases=aliases, compiler_params=params,
            grid_spec=pltpu.PrefetchScalarGridSpec(num_scalar_prefetch=n_pre, grid=grid, in_specs=all_in,
                                                   out_specs=all_out, scratch_shapes=all_scr))
    else:
        call = pl.pallas_call(
            wrapped, name=name, grid=grid, in_specs=all_in, out_specs=all_out, out_shape=shapes,
            scratch_shapes=all_scr, input_output_aliases=aliases, compiler_params=params)
    res = call(*prefetch, *args, *h_ins)
    return res[:n_out], res[n_out:]


def _all_gather(shards, small=()):
    items = tuple(shards) + tuple(small)
    nw = len(shards)

    def place():
        x, y, c, me, _ = _place()
        first = (x + (1 - c) * (1 - 2 * x), y + c * (1 - 2 * y))
        second = (x + c * (1 - 2 * x), y + (1 - c) * (1 - 2 * y))
        diag = (1 - x, 1 - y)
        return x, y, c, me, (first, second, diag)

    def halves(w, c):
        rh = items[w].shape[0] // 2
        return pl.ds(c * rh, rh), pl.ds((1 - c) * rh, rh)

    def start(ins, outs, ss, rs):
        x, y, c, me, chips = place()
        for w in range(len(items)):
            _remote(ins[w], outs[w].at[me], ss.at[w, 6], rs.at[w, 6], (x, y, 1 - c)).start()
            if w < nw:
                mine, _ = halves(w, c)
                _remote(ins[w].at[mine], outs[w].at[me, mine], ss.at[w, 0], rs.at[w, 0], (*chips[0], c)).start()
            else:
                for k in range(3):
                    _remote(ins[w], outs[w].at[me], ss.at[w, k], rs.at[w, k], (*chips[k], c)).start()

    def onward(outs, ss, rs, w, k, x, y, c, chips):
        mine, _ = halves(w, c)
        pk = 2 * chips[k][0] + chips[k][1]
        got = outs[w].at[pk, mine]
        src = chips[1] if k == 2 else chips[k]
        _remote(got, got, ss.at[w, k], rs.at[w, k], (*src, c)).wait_recv()
        if k == 0:
            _remote(got, got, ss.at[w, 2], rs.at[w, 2], (*chips[1], c)).start()
        _remote(got, got, ss.at[w, 3 + k], rs.at[w, 3 + k], (x, y, 1 - c)).start()

    def middle(ins, outs, ss, rs):
        x, y, c, me, chips = place()
        for w in range(nw):
            mine, _ = halves(w, c)
            _remote(ins[w].at[mine], outs[w].at[me, mine], ss.at[w, 1], rs.at[w, 1], (*chips[1], c)).start()
        for w in range(nw):
            onward(outs, ss, rs, w, 0, x, y, c, chips)

    def finish(ins, outs, ss, rs):
        x, y, c, me, chips = place()
        sib = (x, y, 1 - c)
        for k in (1, 2):
            for w in range(nw):
                onward(outs, ss, rs, w, k, x, y, c, chips)
        for w in range(len(items)):
            if w < nw:
                mine, theirs = halves(w, c)
                for k, chip in ((3, chips[1]), (4, chips[0]), (5, chips[2])):
                    oth = outs[w].at[2 * chip[0] + chip[1], theirs]
                    _remote(oth, oth, ss.at[w, k], rs.at[w, k], sib).wait_recv()
                own = ins[w].at[mine]
                for k in range(6):
                    _remote(own, own, ss.at[w, k], rs.at[w, k], sib).wait_send()
            else:
                for k in range(3):
                    got = outs[w].at[2 * chips[k][0] + chips[k][1]]
                    _remote(got, got, ss.at[w, k], rs.at[w, k], (*chips[k], c)).wait_recv()
                    _remote(ins[w], ins[w], ss.at[w, k], rs.at[w, k], sib).wait_send()
            _remote(ins[w], outs[w].at[me], ss.at[w, 6], rs.at[w, 6], sib).wait()

    out_shapes = tuple(jax.ShapeDtypeStruct((N_CHIPS,) + a.shape, a.dtype) for a in items)
    return _Exchange(items, out_shapes, (len(items), 7), start, finish, middle if nw else None)


def _chip_reduce(grads, name):
    nw = len(grads)
    step = 64

    def body(*refs):
        ins, outs = refs[:nw], refs[nw:2 * nw]
        own, got = refs[2 * nw:3 * nw], refs[3 * nw:4 * nw]
        send_sems, recv_sems, local_sems = refs[4 * nw:]
        x, y, c, _, _ = _place()
        moves = []
        for w in range(nw):
            rh = grads[w].shape[1] // 2
            away = _remote(ins[w].at[:, pl.ds((1 - c) * rh, rh), :], got[w], send_sems.at[w], recv_sems.at[w],
                           (x, y, 1 - c))
            mine = pltpu.make_async_copy(ins[w].at[:, pl.ds(c * rh, rh), :], own[w], local_sems.at[w])
            away.start()
            mine.start()
            moves.append((away, mine))
        back = []
        for w, (away, mine) in enumerate(moves):
            nb, rh, _ = own[w].shape
            mine.wait()
            away.wait()
            for k in range(nb):
                def add(i, carry, w=w, k=k):
                    rows = pl.ds(pl.multiple_of(i * step, step), step)
                    own[w][k, rows, :] = (own[w][k, rows, :].astype(F32) + got[w][k, rows, :].astype(F32)).astype(BF16)
                    return carry
                lax.fori_loop(0, rh // step, add, 0)
                tail = rh % step
                if tail:
                    rows = slice(rh - tail, rh)
                    own[w][k, rows, :] = (own[w][k, rows, :].astype(F32) + got[w][k, rows, :].astype(F32)).astype(BF16)
            wb = pltpu.make_async_copy(own[w], outs[w].at[:, pl.ds(c * rh, rh), :], local_sems.at[w])
            wb.start()
            back.append(wb)
        for wb in back:
            wb.wait()

    halves = [pltpu.VMEM((g.shape[0], g.shape[1] // 2, g.shape[2]), BF16) for g in grads]
    sem = pltpu.SemaphoreType.DMA((nw,))
    return pl.pallas_call(
        body, name=name, out_shape=tuple(jax.ShapeDtypeStruct(g.shape, BF16) for g in grads),
        in_specs=[ANY] * nw, out_specs=[ANY] * nw, scratch_shapes=halves + halves + [sem, sem, sem],
        compiler_params=pltpu.CompilerParams(vmem_limit_bytes=VMEM_LIMIT),
    )(*grads)


def _scatter_partials(parts):
    nw = len(parts)

    def copies(ins, outs, ss, rs):
        _, _, c, _, chips = _place()
        res = []
        for r, (px, py) in enumerate(chips):
            for w in range(nw):
                rh = parts[w].shape[1] // 2
                rows = pl.ds(c * rh, rh)
                res.append(_remote(ins[w].at[2 * px + py, rows], outs[w].at[r, rows], ss.at[w, r], rs.at[w, r],
                                   (px, py, c)))
        return res

    def start(ins, outs, ss, rs):
        for cp in copies(ins, outs, ss, rs):
            cp.start()

    def finish(ins, outs, ss, rs):
        for cp in copies(ins, outs, ss, rs):
            cp.wait()

    out_shapes = tuple(jax.ShapeDtypeStruct((3,) + p.shape[1:], p.dtype) for p in parts)
    return _Exchange(parts, out_shapes, (nw, 3), start, finish)


def _join_partials(parts, slots):
    nw = len(parts)

    def copies(outs, ss, rs, mine):
        x, y, c, me, _ = _place()
        res = []
        for w in range(nw):
            rh = parts[w].shape[1] // 2
            rows = pl.ds((c if mine else 1 - c) * rh, rh)
            own = outs[w].at[me, rows]
            got = outs[nw + w].at[:, rows, :]
            res.append(_remote(own, own, ss.at[w, 0], rs.at[w, 0], (x, y, 1 - c)))
            res.append(_remote(got, got, ss.at[w, 1], rs.at[w, 1], (x, y, 1 - c)))
        return res

    def start(ins, outs, ss, rs):
        for cp in copies(outs, ss, rs, True):
            cp.start()

    def finish(ins, outs, ss, rs):
        for cp in copies(outs, ss, rs, True):
            cp.wait_send()
        for cp in copies(outs, ss, rs, False):
            cp.wait_recv()

    arrays = tuple(parts) + tuple(slots)
    return _Exchange(arrays, tuple(jax.ShapeDtypeStruct(a.shape, a.dtype) for a in arrays), (nw, 2), start, finish,
                     in_place=True)


def _gather_small(slab):
    def copies(ins, outs, ss, rs):
        x, y, c, _, _ = _place()
        me = 4 * x + 2 * y + c
        own = pltpu.make_async_copy(ins[0], outs[0].at[me], ss.at[7])
        out, arrivals = [], []
        for k in range(1, 8):
            px = 1 - x if k & 4 else x
            py = 1 - y if k & 2 else y
            pc = 1 - c if k & 1 else c
            out.append(_remote(ins[0], outs[0].at[me], ss.at[k - 1], rs.at[k - 1], (px, py, pc)))
            theirs = outs[0].at[4 * px + 2 * py + pc]
            arrivals.append(_remote(theirs, theirs, ss.at[k - 1], rs.at[k - 1], (px, py, pc)))
        return own, out, arrivals

    def start(ins, outs, ss, rs):
        own, out, _ = copies(ins, outs, ss, rs)
        own.start()
        for cp in out:
            cp.start()

    def finish(ins, outs, ss, rs):
        own, out, arrivals = copies(ins, outs, ss, rs)
        for cp in out:
            cp.wait_send()
        for cp in arrivals:
            cp.wait_recv()
        own.wait()

    return _Exchange((slab,), (jax.ShapeDtypeStruct((8,) + slab.shape, slab.dtype),), (8,), start, finish)


_SMALL_VECS = ("ln_mix_g", "ln_attn_g", "ln_mem_g", "ln_ffn_g", "ln_final_g")


def _pack_small(p, extra, conv):
    d = p["ln_mix_g"].shape[-1]
    top = [p[k].reshape(1, d) for k in _SMALL_VECS]
    top.append(jnp.concatenate([p["sgu_ln_g"].reshape(-1), p["sgu_ln_b"].reshape(-1)]).reshape(1, d))
    top.append(jnp.concatenate([p["grp_norm_a"].reshape(-1), p["grp_norm_b"].reshape(-1)]).reshape(1, d))
    top.append(jnp.concatenate([p["b_spatial"].reshape(-1), extra]).reshape(1, d))
    mid = jnp.zeros((8, d), F32)
    if conv is not None:
        mid = jnp.pad(conv, ((0, 5), (0, d - conv.shape[1])))
    return jnp.concatenate([jnp.concatenate(top, axis=0), mid, p["w_spatial"].reshape(-1, d)], axis=0)


def _unpack_small(slab):
    d = slab.shape[1]
    hw = d // 2
    out = {k: slab[i] for i, k in enumerate(_SMALL_VECS)}
    out["sgu_ln_g"], out["sgu_ln_b"] = slab[5, :hw], slab[5, hw:]
    out["grp_norm_a"], out["grp_norm_b"] = slab[6, :hw], slab[6, hw:]
    out["b_spatial"] = slab[7, :hw].reshape(HEADS, CHUNK)
    out["w_spatial"] = slab[16:].reshape(HEADS, CHUNK, CHUNK)
    return out


_BIG = ("w_in", "w_kv", "w_gate_up", "w_out", "w_q", "w_o", "w_down")
_WEIGHTS = ("ln_mix_g", "w_in", "sgu_ln_g", "sgu_ln_b", "w_spatial", "b_spatial", "conv_w", "grp_norm_a",
            "grp_norm_b", "w_out", "ln_attn_g", "ln_mem_g", "w_q", "w_kv", "w_o", "ln_ffn_g", "w_gate_up",
            "w_down", "ln_final_g")


def _step(p, m_, v_, x, mem, target):
    s, d = x.shape
    hw = d // 2
    row = lambda a: a.reshape(1, -1)
    x_, y_, c_ = lax.axis_index("x"), lax.axis_index("y"), lax.axis_index("c")
    chip = 2 * x_ + y_

    bf = {k: p[k].astype(BF16) for k in _BIG}
    conv8 = jnp.pad(p["conv_w"], ((0, 5), (0, 0)))
    w_in, w_out4, conv4 = _run_exchange(_all_gather([bf["w_in"], bf["w_out"]], [conv8]), "all_gather_mixer")
    cw = jnp.transpose(conv4[:, :3, :], (1, 0, 2)).reshape(3, hw)
    b_t = jnp.pad(jnp.transpose(p["b_spatial"]), ((0, 0), (0, CHUNK - HEADS)))
    g1, g2, gm, g3, gf = (row(p[k]) for k in _SMALL_VECS)
    lng, lnb, ga, gb = row(p["sgu_ln_g"]), row(p["sgu_ln_b"]), row(p["grp_norm_a"]), row(p["grp_norm_b"])
    wsp = p["w_spatial"]
    w_out = w_out4.reshape(-1, d)

    (h, x1, ycat, xn1), (w_kv, w_q4, w_o4, w_down4) = _mixer_fwd(
        x, g1, w_in, lng, lnb, wsp, b_t, cw, ga, gb, w_out,
        hosted=_all_gather([bf[k] for k in ("w_kv", "w_q", "w_o", "w_down")]))
    w_q, w_o, w_down = (a.reshape(-1, d) for a in (w_q4, w_o4, w_down4))
    memn, kv = _kv_fwd(mem, gm, w_kv)
    (x2, o), (w_gu,) = _attn_fwd(x1, g2, w_q, kv, w_o, hosted=_all_gather([bf["w_gate_up"]]))
    dx2, act, dgu, xn3, dx3, loss, dgf, dg3 = _ffn_fwd_bwd(x2, g3, gf, target, w_gu, w_down)

    place = jnp.stack([c_, chip]).astype(jnp.int32)

    def chip_partials(names, grads, tag):
        return list(_chip_reduce(grads, "chip_reduce_" + tag))

    names_d = ("w_down",)
    parts_d = chip_partials(names_d, (_weight_grad(act, dx3, "grad_w_down", 1408, 512, False)[0],), "down")
    g_gu, slots_d = _weight_grad(xn3, dgu, "grad_w_gate_up", 512, 1408, True, hosted=_scatter_partials(parts_d))
    names_a = ("w_gate_up",)
    parts_a = chip_partials(names_a, (g_gu,), "ffn")
    (dx1, dkv, dg2, g_o, g_q, g_out), slots_a = _attn_bwd(x1, dx2, o, ycat, g2, w_q, kv, w_o,
                                                           hosted=_scatter_partials(parts_a))
    g_kv, dgm = _kv_bwd(dkv, mem, memn, gm, w_kv)
    names_b = ("w_o", "w_out", "w_q", "w_kv")
    shard_major = lambda g: g.reshape(N_CHIPS, -1, d)
    parts_b = chip_partials(names_b, (shard_major(g_o), shard_major(g_out), shard_major(g_q), g_kv), "attn")
    (dx, dh, dg1, dlng, dlnb, dwsp, dbt, dcw, dga, dgb), slots_b = _mixer_bwd(
        x, dx1, h, g1, lng, lnb, wsp, b_t, cw, ga, gb, w_out, w_in, hosted=_scatter_partials(parts_b))
    small = {"ln_mix_g": dg1, "ln_attn_g": dg2, "ln_mem_g": dgm, "ln_ffn_g": dg3, "ln_final_g": dgf,
             "sgu_ln_g": dlng, "sgu_ln_b": dlnb, "grp_norm_a": dga, "grp_norm_b": dgb,
             "b_spatial": jnp.transpose(dbt[:, :HEADS]), "w_spatial": dwsp}
    loss_vec = jnp.pad(loss.reshape(1), (0, hw - 1))
    names = names_d + names_a + names_b
    n_done = len(names)
    g_in, extra = _weight_grad(
        xn1, dh, "grad_w_in", 1024, 640, True,
        hosted=[_gather_small(_pack_small(small, loss_vec, dcw)),
                _join_partials(parts_d + parts_a + parts_b, slots_d + slots_a + slots_b)])
    parts, whole = extra[0], dict(zip(names, zip(extra[1:1 + n_done], extra[1 + n_done:])))
    (part_in,) = chip_partials(("w_in",), (g_in,), "mixer")
    out_g, out_d, out_m, out_v = {}, {}, {}, {}

    def finalize(k, hosted=None):
        (out_g[k], out_d[k], out_m[k], out_v[k]), res = _finalize(
            whole[k][0], whole[k][1], p[k], m_[k], v_[k], place, "finalize_" + k, hosted)
        return res

    (slots_in,) = finalize("w_down", _scatter_partials([part_in]))
    for k in names:
        if k != "w_down":
            finalize(k)
    whole["w_in"] = _run_exchange(_join_partials([part_in], [slots_in]), "rs_join_mixer")
    finalize("w_in")

    zeros = jnp.zeros((hw,), F32)
    sg, sd, sm, sv = _small_sum_adamw(parts, _pack_small(p, zeros, None), _pack_small(m_, zeros, None),
                                      _pack_small(v_, zeros, None))
    for tree, slab in zip((out_g, out_d, out_m, out_v), (sg, sd, sm, sv)):
        tree.update(_unpack_small(slab))
    loss_out = sg[7, hw]
    g_conv = lax.dynamic_slice(sg[8:11, :hw], (0, chip * (hw // N_CHIPS)), (3, hw // N_CHIPS))
    out_g["conv_w"] = g_conv
    out_d["conv_w"], out_m["conv_w"], out_v["conv_w"] = _adamw(p["conv_w"], g_conv, m_["conv_w"], v_["conv_w"],
                                                                "adamw_conv_w")
    return loss_out, dx, out_g, out_d, out_m, out_v


def kernel(x, mem, ln_mix_g, w_in, sgu_ln_g, sgu_ln_b, w_spatial, b_spatial, conv_w, grp_norm_a, grp_norm_b, w_out, ln_attn_g, ln_mem_g, w_q, w_kv, w_o, ln_ffn_g, w_gate_up, w_down, ln_final_g, loss_target, m_ln_mix_g, m_w_in, m_sgu_ln_g, m_sgu_ln_b, m_w_spatial, m_b_spatial, m_conv_w, m_grp_norm_a, m_grp_norm_b, m_w_out, m_ln_attn_g, m_ln_mem_g, m_w_q, m_w_kv, m_w_o, m_ln_ffn_g, m_w_gate_up, m_w_down, m_ln_final_g, v_ln_mix_g, v_w_in, v_sgu_ln_g, v_sgu_ln_b, v_w_spatial, v_b_spatial, v_conv_w, v_grp_norm_a, v_grp_norm_b, v_w_out, v_ln_attn_g, v_ln_mem_g, v_w_q, v_w_kv, v_w_o, v_ln_ffn_g, v_w_gate_up, v_w_down, v_ln_final_g):
    p = dict(ln_mix_g=ln_mix_g, w_in=w_in, sgu_ln_g=sgu_ln_g, sgu_ln_b=sgu_ln_b, w_spatial=w_spatial,
             b_spatial=b_spatial, conv_w=conv_w, grp_norm_a=grp_norm_a, grp_norm_b=grp_norm_b, w_out=w_out,
             ln_attn_g=ln_attn_g, ln_mem_g=ln_mem_g, w_q=w_q, w_kv=w_kv, w_o=w_o, ln_ffn_g=ln_ffn_g,
             w_gate_up=w_gate_up, w_down=w_down, ln_final_g=ln_final_g)
    m_ = dict(ln_mix_g=m_ln_mix_g, w_in=m_w_in, sgu_ln_g=m_sgu_ln_g, sgu_ln_b=m_sgu_ln_b, w_spatial=m_w_spatial,
              b_spatial=m_b_spatial, conv_w=m_conv_w, grp_norm_a=m_grp_norm_a, grp_norm_b=m_grp_norm_b,
              w_out=m_w_out, ln_attn_g=m_ln_attn_g, ln_mem_g=m_ln_mem_g, w_q=m_w_q, w_kv=m_w_kv, w_o=m_w_o,
              ln_ffn_g=m_ln_ffn_g, w_gate_up=m_w_gate_up, w_down=m_w_down, ln_final_g=m_ln_final_g)
    v_ = dict(ln_mix_g=v_ln_mix_g, w_in=v_w_in, sgu_ln_g=v_sgu_ln_g, sgu_ln_b=v_sgu_ln_b, w_spatial=v_w_spatial,
              b_spatial=v_b_spatial, conv_w=v_conv_w, grp_norm_a=v_grp_norm_a, grp_norm_b=v_grp_norm_b,
              w_out=v_w_out, ln_attn_g=v_ln_attn_g, ln_mem_g=v_ln_mem_g, w_q=v_w_q, w_kv=v_w_kv, w_o=v_w_o,
              ln_ffn_g=v_ln_ffn_g, w_gate_up=v_w_gate_up, w_down=v_w_down, ln_final_g=v_ln_final_g)
    s, d = x.shape[-2], x.shape[-1]
    loss, dx, g, dl, nm, nv = _step(p, m_, v_, x.reshape(s, d), mem.reshape(-1, d), loss_target.reshape(s, d))
    outs = [loss, dx.reshape(x.shape)]
    for tree in (g, dl, nm, nv):
        outs += [tree[k].reshape(p[k].shape) for k in _WEIGHTS]
    return tuple(outs)
```

```python
import functools
import math

import jax
import jax.numpy as jnp
from jax import lax
from jax.experimental import pallas as pl
from jax.experimental.pallas import tpu as pltpu

F32 = jnp.float32
BF16 = jnp.bfloat16
EPS = 1e-6
CHUNK = 128
HEADS = 4
N_CHIPS = 4
TM = 512
TM_ATTN = 512
TM_FFN = 256
ADAM_LR, ADAM_B1, ADAM_B2, ADAM_EPS, ADAM_WD, ADAM_STEP = 0.001, 0.9, 0.999, 1e-08, 0.01, 10
GELU_C = math.sqrt(2.0 / math.pi)
GELU_K = 0.044715
SMALL_ROWS = 80
VMEM_LIMIT = 56 * 1024 * 1024
MIDDLE_STEP_16THS = 7
MESH = pl.DeviceIdType.MESH
ANY = pl.BlockSpec(memory_space=pl.ANY)


def _params(*sem):
    return pltpu.CompilerParams(dimension_semantics=sem, vmem_limit_bytes=VMEM_LIMIT)


def _dot(a, b):
    return jnp.dot(a, b, preferred_element_type=F32)


def _dot_nt(a, b):
    return lax.dot_general(a, b, (((1,), (1,)), ((), ())), preferred_element_type=F32)


def _dot_tn(a, b):
    return lax.dot_general(a, b, (((0,), (0,)), ((), ())), preferred_element_type=F32)


def _rms_fwd(x, g):
    r = lax.rsqrt(jnp.mean(x * x, axis=-1, keepdims=True) + EPS)
    xh = x * r
    return xh * g, xh, r


def _rms_bwd(dy, xh, r, g):
    dxh = dy * g
    dx = r * (dxh - xh * jnp.mean(dxh * xh, axis=-1, keepdims=True))
    return dx, jnp.sum(dy * xh, axis=0, keepdims=True)


def _full(shape):
    nd = len(shape)
    return pl.BlockSpec(shape, lambda *_: (0,) * nd, pipeline_mode=pl.Buffered(1))


def _acc(shape):
    nd = len(shape)
    return pl.BlockSpec(shape, lambda *_: (0,) * nd)


def _rows(tm, cols):
    return pl.BlockSpec((tm, cols), lambda i: (i, 0))


def _tril_weights(wsp_ref):
    row = lax.broadcasted_iota(jnp.int32, (CHUNK, CHUNK), 0)
    col = lax.broadcasted_iota(jnp.int32, (CHUNK, CHUNK), 1)
    return [jnp.where(row >= col, wsp_ref[hd], 0.0).astype(BF16) for hd in range(HEADS)]


def _shift_rows(z, zp):
    row = lax.broadcasted_iota(jnp.int32, z.shape, 0)
    zm1 = jnp.where(row == 0, zp[7:8, :], pltpu.roll(z, 1, 0))
    zm2 = jnp.where(row == 0, zp[6:7, :], jnp.where(row == 1, zp[7:8, :], pltpu.roll(z, 2, 0)))
    return zm1, zm2


def _gelu_parts(x):
    t = jnp.tanh(GELU_C * (x + GELU_K * (x * x * x)))
    return 0.5 * x * (1.0 + t), t


def _layer_norm_parts(v, g, b):
    mu = jnp.mean(v, axis=-1, keepdims=True)
    vc = v - mu
    rs = lax.rsqrt(jnp.mean(vc * vc, axis=-1, keepdims=True) + EPS)
    vhat = vc * rs
    return vhat * g + b, vhat, rs


def _kv_fwd(mem, g_mem, w_kv):
    m, d = mem.shape
    ns = w_kv.shape[2]

    def body(mem_ref, g_ref, w_ref, memn_ref, kv_ref):
        y, _, _ = _rms_fwd(mem_ref[...], g_ref[...])
        yb = y.astype(BF16)
        memn_ref[...] = yb
        for k in range(N_CHIPS):
            kv_ref[:, k * ns:(k + 1) * ns] = _dot(yb, w_ref[k]).astype(BF16)

    return pl.pallas_call(
        body, name="kv_fwd",
        out_shape=(jax.ShapeDtypeStruct((m, d), BF16), jax.ShapeDtypeStruct((m, N_CHIPS * ns), BF16)),
        compiler_params=pltpu.CompilerParams(vmem_limit_bytes=VMEM_LIMIT),
    )(mem, g_mem, w_kv)


def _mixer_fwd(x, g1, w_in, lng, lnb, wsp, b_t, cw, ga, gb, w_out, hosted=None):
    s, d = x.shape
    n = s // TM
    nch = TM // CHUNK
    ns = w_in.shape[2]
    nh = N_CHIPS * ns
    aw = d // 2
    hd_w = aw // HEADS

    def body(x_ref, g1_ref, win_ref, lng_ref, lnb_ref, wsp_ref, bt_ref, cw_ref, ga_ref, gb_ref, wout_ref,
             h_ref, x1_ref, y_ref, xn_ref, mix_ref, th_ref, zp_ref):
        i = pl.program_id(0)

        @pl.when(i == 0)
        def _():
            zp_ref[...] = jnp.zeros_like(zp_ref)

        x = x_ref[...]
        xn, _, _ = _rms_fwd(x, g1_ref[...])
        xnb = xn.astype(BF16)
        xn_ref[...] = xnb
        for k in range(N_CHIPS):
            h_ref[:, k * ns:(k + 1) * ns] = _dot(xnb, win_ref[k])
        a, th = _gelu_parts(h_ref[:, 0:2 * aw])
        th_ref[...] = th
        u = a[:, :aw]
        vn, _, _ = _layer_norm_parts(a[:, aw:], lng_ref[...], lnb_ref[...])
        vnb = vn.astype(BF16)
        wm = _tril_weights(wsp_ref)
        for c in range(nch):
            for hd in range(HEADS):
                blk = vnb[c * CHUNK:(c + 1) * CHUNK, hd * hd_w:(hd + 1) * hd_w]
                mix_ref[c * CHUNK:(c + 1) * CHUNK, hd * hd_w:(hd + 1) * hd_w] = _dot(wm[hd], blk) + bt_ref[:, hd:hd + 1]
        ya, _, _ = _rms_fwd(u * mix_ref[...], ga_ref[...])
        g_b = h_ref[:, 2 * aw:3 * aw]
        z = h_ref[:, 3 * aw:4 * aw] * h_ref[:, 4 * aw:5 * aw]
        zm1, zm2 = _shift_rows(z, zp_ref[...])
        conv = cw_ref[0:1, :] * zm2 + cw_ref[1:2, :] * zm1 + cw_ref[2:3, :] * z
        yb, _, _ = _rms_fwd(g_b * conv, gb_ref[...])
        zp_ref[...] = z[TM - 8:TM, :]
        ycat = jnp.concatenate([ya, yb], axis=-1).astype(BF16)
        y_ref[...] = ycat
        x1_ref[...] = x + _dot(ycat, wout_ref[...])

    return _host_call(
        body, "mixer_fwd", (n,),
        [_rows(TM, d), _full(g1.shape), _full(w_in.shape), _full(lng.shape), _full(lnb.shape),
         _full(wsp.shape), _full(b_t.shape), _full(cw.shape), _full(ga.shape), _full(gb.shape),
         _full(w_out.shape)],
        [_rows(TM, nh), _rows(TM, d), _rows(TM, d), _rows(TM, d), _rows(TM, aw), _rows(TM, d)],
        (jax.ShapeDtypeStruct((s, nh), F32), jax.ShapeDtypeStruct((s, d), F32),
         jax.ShapeDtypeStruct((s, d), BF16), jax.ShapeDtypeStruct((s, d), BF16),
         jax.ShapeDtypeStruct((s, aw), F32), jax.ShapeDtypeStruct((s, d), F32)),
        [pltpu.VMEM((8, aw), F32)],
        (x, g1, w_in, lng, lnb, wsp, b_t, cw, ga, gb, w_out), ("arbitrary",), hosted)


def _attn_fwd(x1, g2, w_q, kv, w_o, hosted=None):
    s, d = x1.shape
    tm = min(TM_ATTN, s)
    n = s // tm
    dh = d // HEADS
    m = kv.shape[0]
    scale = dh ** -0.5

    def body(x1_ref, g2_ref, wq_ref, kv_ref, wo_ref, x2_ref, o_ref, q_ref, p_ref):
        x1v = x1_ref[...]
        xn, _, _ = _rms_fwd(x1v, g2_ref[...])
        q_ref[...] = _dot(xn.astype(BF16), wq_ref[...]).astype(BF16)
        for hd in range(HEADS):
            kh = kv_ref[:, hd * dh:(hd + 1) * dh]
            vh = kv_ref[:, d + hd * dh:d + (hd + 1) * dh]
            sc = _dot_nt(q_ref[:, hd * dh:(hd + 1) * dh], kh) * scale
            e = jnp.exp(sc - jnp.max(sc, axis=-1, keepdims=True))
            p = e / jnp.sum(e, axis=-1, keepdims=True)
            p_ref[:, hd * m:(hd + 1) * m] = p
            o_ref[:, hd * dh:(hd + 1) * dh] = _dot(p.astype(BF16), vh).astype(BF16)
        x2_ref[...] = x1v + _dot(o_ref[...], wo_ref[...])

    return _host_call(
        body, "attn_fwd", (n,),
        [_rows(tm, d), _full(g2.shape), _full(w_q.shape), _full(kv.shape), _full(w_o.shape)],
        [_rows(tm, d), _rows(tm, d), _rows(tm, d), _rows(tm, HEADS * m)],
        (jax.ShapeDtypeStruct((s, d), F32), jax.ShapeDtypeStruct((s, d), BF16), jax.ShapeDtypeStruct((s, d), BF16),
         jax.ShapeDtypeStruct((s, HEADS * m), F32)),
        [], (x1, g2, w_q, kv, w_o), ("parallel",), hosted)


def _ffn_fwd_bwd(x2, g3, gf, target, w_gu, w_down):
    s, d = x2.shape
    tm = min(TM_FFN, s)
    n = s // tm
    ns = w_gu.shape[2]
    ff = 2 * ns

    def body(x2_ref, g3_ref, gf_ref, t_ref, wgu_ref, wd_ref,
             dx2_ref, act_ref, dgu_ref, xn_ref, dx3_ref, loss_ref, dgf_ref, dg3_ref):
        i = pl.program_id(0)

        @pl.when(i == 0)
        def _():
            loss_ref[...] = jnp.zeros_like(loss_ref)
            dgf_ref[...] = jnp.zeros_like(dgf_ref)
            dg3_ref[...] = jnp.zeros_like(dg3_ref)

        x2v = x2_ref[...]
        xn, xh3, r3 = _rms_fwd(x2v, g3_ref[...])
        xnb = xn.astype(BF16)
        xn_ref[...] = xnb
        x3 = x2v
        saved = []
        for j in range(2):
            g = _dot(xnb, wgu_ref[j])
            u = _dot(xnb, wgu_ref[2 + j])
            sg = 1.0 / (1.0 + jnp.exp(-g))
            sl = g * sg
            actb = (sl * u).astype(BF16)
            act_ref[:, j * ns:(j + 1) * ns] = actb
            x3 = x3 + _dot(actb, wd_ref[j * ns:(j + 1) * ns, :])
            saved.append((u, sl, sg * (1.0 + g * (1.0 - sg))))
        gfv = gf_ref[...]
        y, xhf, rf = _rms_fwd(x3, gfv)
        e = y - t_ref[...]
        loss_ref[...] += 0.5 * jnp.sum(jnp.sum(e * e, axis=-1, keepdims=True), axis=0, keepdims=True) / d
        dx3, dgf = _rms_bwd(e / d, xhf, rf, gfv)
        dgf_ref[...] += dgf
        dx3b = dx3.astype(BF16)
        dx3_ref[...] = dx3b
        dxn = jnp.zeros_like(x2v)
        for j in range(2):
            u, sl, dsl = saved[j]
            dact = _dot_nt(dx3b, wd_ref[j * ns:(j + 1) * ns, :])
            dgb = (dact * u * dsl).astype(BF16)
            dub = (dact * sl).astype(BF16)
            dgu_ref[:, j * ns:(j + 1) * ns] = dgb
            dgu_ref[:, ff + j * ns:ff + (j + 1) * ns] = dub
            dxn = dxn + _dot_nt(dgb, wgu_ref[j]) + _dot_nt(dub, wgu_ref[2 + j])
        dxr, dg3 = _rms_bwd(dxn, xh3, r3, g3_ref[...])
        dg3_ref[...] += dg3
        dx2_ref[...] = dx3 + dxr

    vec = jax.ShapeDtypeStruct((1, d), F32)
    return pl.pallas_call(
        body, name="ffn_fwd_bwd", grid=(n,),
        in_specs=[_rows(tm, d), _full(g3.shape), _full(gf.shape), _rows(tm, d), _full(w_gu.shape),
                  _full(w_down.shape)],
        out_specs=[_rows(tm, d), _rows(tm, ff), _rows(tm, 2 * ff), _rows(tm, d), _rows(tm, d),
                   _acc((1, 1)), _acc((1, d)), _acc((1, d))],
        out_shape=(jax.ShapeDtypeStruct((s, d), F32), jax.ShapeDtypeStruct((s, ff), BF16),
                   jax.ShapeDtypeStruct((s, 2 * ff), BF16), jax.ShapeDtypeStruct((s, d), BF16),
                   jax.ShapeDtypeStruct((s, d), BF16), jax.ShapeDtypeStruct((1, 1), F32), vec, vec),
        compiler_params=_params("arbitrary"),
    )(x2, g3, gf, target, w_gu, w_down)


def _attn_bwd(x1, dx2, o, ycat, qs, probs, g2, w_q, kv, w_o, hosted=None):
    s, d = x1.shape
    tm = min(TM_ATTN, s)
    n = s // tm
    dh = d // HEADS
    scale = dh ** -0.5
    m = kv.shape[0]

    def body(x1_ref, dx2_ref, o_ref, y_ref, q_ref, p_ref, g2_ref, wq_ref, kv_ref, wo_ref,
             dx1_ref, dkv_ref, dg2_ref, gwo_out, gwq_out, gwout_out, dq_ref, gwo_ref, gwq_ref, gwout_ref):
        i = pl.program_id(0)

        @pl.when(i == 0)
        def _():
            for r in (dkv_ref, dg2_ref, gwo_ref, gwq_ref, gwout_ref):
                r[...] = jnp.zeros_like(r)

        xn, xh2, r2 = _rms_fwd(x1_ref[...], g2_ref[...])
        xnb = xn.astype(BF16)
        dx2v = dx2_ref[...]
        dx2b = dx2v.astype(BF16)
        gwo_ref[...] += _dot_tn(o_ref[...], dx2b)
        do = _dot_nt(dx2b, wo_ref[...])
        for hd in range(HEADS):
            qb = q_ref[:, hd * dh:(hd + 1) * dh]
            p = p_ref[:, hd * m:(hd + 1) * m]
            kh = kv_ref[:, hd * dh:(hd + 1) * dh]
            vh = kv_ref[:, d + hd * dh:d + (hd + 1) * dh]
            dob = do[:, hd * dh:(hd + 1) * dh].astype(BF16)
            dp = _dot_nt(dob, vh)
            ds = p * (dp - jnp.sum(dp * p, axis=-1, keepdims=True))
            dsb = (ds * scale).astype(BF16)
            dq_ref[:, hd * dh:(hd + 1) * dh] = _dot(dsb, kh).astype(BF16)
            dkv_ref[:, hd * dh:(hd + 1) * dh] += _dot_tn(dsb, qb)
            dkv_ref[:, d + hd * dh:d + (hd + 1) * dh] += _dot_tn(p.astype(BF16), dob)
        dqb = dq_ref[...]
        gwq_ref[...] += _dot_tn(xnb, dqb)
        dxn = _dot_nt(dqb, wq_ref[...])
        dxr, dg2 = _rms_bwd(dxn, xh2, r2, g2_ref[...])
        dg2_ref[...] += dg2
        dx1 = dx2v + dxr
        dx1_ref[...] = dx1
        gwout_ref[...] += _dot_tn(y_ref[...], dx1.astype(BF16))

        @pl.when(i == n - 1)
        def _():
            for acc, out in ((gwo_ref, gwo_out), (gwq_ref, gwq_out), (gwout_ref, gwout_out)):
                out[...] = acc[...].astype(BF16)

    sq = jax.ShapeDtypeStruct((d, d), BF16)
    return _host_call(
        body, "attn_bwd", (n,),
        [_rows(tm, d), _rows(tm, d), _rows(tm, d), _rows(tm, d), _rows(tm, d), _rows(tm, HEADS * m),
         _full(g2.shape), _full(w_q.shape), _full(kv.shape), _full(w_o.shape)],
        [_rows(tm, d), _acc((m, 2 * d)), _acc((1, d)), _acc((d, d)), _acc((d, d)), _acc((d, d))],
        (jax.ShapeDtypeStruct((s, d), F32), jax.ShapeDtypeStruct((m, 2 * d), F32),
         jax.ShapeDtypeStruct((1, d), F32), sq, sq, sq),
        [pltpu.VMEM((tm, d), BF16)] + [pltpu.VMEM((d, d), F32)] * 3,
        (x1, dx2, o, ycat, qs, probs, g2, w_q, kv, w_o), ("arbitrary",), hosted)


def _kv_bwd(dkv, mem, memn, g_mem, w_kv):
    m, d = mem.shape
    ns = w_kv.shape[2]

    def body(dkv_ref, mem_ref, memn_ref, g_ref, w_ref, gw_ref, dg_ref):
        _, xh, _ = _rms_fwd(mem_ref[...], g_ref[...])
        dmemn = jnp.zeros((m, d), F32)
        for k in range(N_CHIPS):
            dkb = dkv_ref[:, k * ns:(k + 1) * ns].astype(BF16)
            gw_ref[k] = _dot_tn(memn_ref[...], dkb).astype(BF16)
            dmemn = dmemn + _dot_nt(dkb, w_ref[k])
        dg_ref[...] = jnp.sum(dmemn * xh, axis=0, keepdims=True)

    return pl.pallas_call(
        body, name="kv_bwd",
        out_shape=(jax.ShapeDtypeStruct((N_CHIPS, d, ns), BF16), jax.ShapeDtypeStruct((1, d), F32)),
        compiler_params=pltpu.CompilerParams(vmem_limit_bytes=VMEM_LIMIT),
    )(dkv, mem, memn, g_mem, w_kv)


def _mixer_bwd(x, dx1, h, mixed_all, th_all, g1, lng, lnb, wsp, b_t, cw, ga, gb, w_out, w_in, hosted=None):
    s, d = x.shape
    n = s // TM
    nch = TM // CHUNK
    ns = w_in.shape[2]
    nh = N_CHIPS * ns
    aw = d // 2
    hd_w = aw // HEADS

    def rev(cols):
        return pl.BlockSpec((TM, cols), lambda i: (n - 1 - i, 0))

    hprev = pl.BlockSpec((8, nh), lambda i: (jnp.maximum((n - 1 - i) * (TM // 8) - 1, 0), 0))

    def body(x_ref, dx1_ref, h_ref, hp_ref, mix_ref, th_ref, g1_ref, lng_ref, lnb_ref, wsp_ref, bt_ref, cw_ref,
             ga_ref, gb_ref, wout_ref, win_ref,
             dx_ref, dh_ref, dg1_ref, dlng_ref, dlnb_ref, dwsp_ref, dbt_ref, dcw_ref, dga_ref, dgb_ref,
             dvn_ref, dcn_ref):
        i = pl.program_id(0)

        @pl.when(i == 0)
        def _():
            for r in (dg1_ref, dlng_ref, dlnb_ref, dwsp_ref, dbt_ref, dcw_ref, dga_ref, dgb_ref, dcn_ref):
                r[...] = jnp.zeros_like(r)

        dx1v = dx1_ref[...]
        dycat = _dot_nt(dx1v.astype(BF16), wout_ref[...])
        ha = h_ref[:, 0:2 * aw]
        th = th_ref[...]
        a = 0.5 * ha * (1.0 + th)
        u = a[:, :aw]
        lngv = lng_ref[...]
        vn, vhat, rs = _layer_norm_parts(a[:, aw:], lngv, lnb_ref[...])
        vnb = vn.astype(BF16)
        wm = _tril_weights(wsp_ref)
        mixed = mix_ref[...]
        gav = ga_ref[...]
        _, yah, ra = _rms_fwd(u * mixed, gav)
        dya, dga = _rms_bwd(dycat[:, :aw], yah, ra, gav)
        dga_ref[...] += dga
        du = dya * mixed
        dmix = dya * u
        dmb = dmix.astype(BF16)
        tri = lax.broadcasted_iota(jnp.int32, (CHUNK, CHUNK), 0) >= lax.broadcasted_iota(jnp.int32, (CHUNK, CHUNK), 1)
        for hd in range(HEADS):
            dw = jnp.zeros((CHUNK, CHUNK), F32)
            db = jnp.zeros((CHUNK, 1), F32)
            for c in range(nch):
                rows = slice(c * CHUNK, (c + 1) * CHUNK)
                cols = slice(hd * hd_w, (hd + 1) * hd_w)
                dvn_ref[rows, cols] = _dot_tn(wm[hd], dmb[rows, cols])
                dw = dw + _dot_nt(dmb[rows, cols], vnb[rows, cols])
                db = db + jnp.sum(dmix[rows, cols], axis=1, keepdims=True)
            dwsp_ref[hd] += jnp.where(tri, dw, 0.0)
            dbt_ref[:, hd:hd + 1] += db
        dvn = dvn_ref[...]
        dlng_ref[...] += jnp.sum(dvn * vhat, axis=0, keepdims=True)
        dlnb_ref[...] += jnp.sum(dvn, axis=0, keepdims=True)
        dvh = dvn * lngv
        dv = rs * (dvh - jnp.mean(dvh, axis=-1, keepdims=True) - vhat * jnp.mean(dvh * vhat, axis=-1, keepdims=True))
        gprime = 0.5 * (1.0 + th) + 0.5 * ha * (1.0 - th * th) * (GELU_C * (1.0 + 3.0 * GELU_K * (ha * ha)))
        dh_ref[:, 0:2 * aw] = (jnp.concatenate([du, dv], axis=-1) * gprime).astype(BF16)
        g_b = h_ref[:, 2 * aw:3 * aw]
        g_c = h_ref[:, 3 * aw:4 * aw]
        val = h_ref[:, 4 * aw:5 * aw]
        z = g_c * val
        zp = jnp.where(i == n - 1, 0.0, hp_ref[:, 3 * aw:4 * aw] * hp_ref[:, 4 * aw:5 * aw])
        zm1, zm2 = _shift_rows(z, zp)
        cw0, cw1, cw2 = cw_ref[0:1, :], cw_ref[1:2, :], cw_ref[2:3, :]
        conv = cw0 * zm2 + cw1 * zm1 + cw2 * z
        gbv = gb_ref[...]
        _, ybh, rb = _rms_fwd(g_b * conv, gbv)
        dyb, dgb = _rms_bwd(dycat[:, aw:], ybh, rb, gbv)
        dgb_ref[...] += dgb
        dconv = dyb * g_b
        dcw_ref[0:1, :] += jnp.sum(dconv * zm2, axis=0, keepdims=True)
        dcw_ref[1:2, :] += jnp.sum(dconv * zm1, axis=0, keepdims=True)
        dcw_ref[2:3, :] += jnp.sum(dconv * z, axis=0, keepdims=True)
        nxt = dcn_ref[...]
        row = lax.broadcasted_iota(jnp.int32, dconv.shape, 0)
        dcp1 = jnp.where(row == TM - 1, nxt[0:1, :], pltpu.roll(dconv, TM - 1, 0))
        dcp2 = jnp.where(row == TM - 1, nxt[1:2, :],
                         jnp.where(row == TM - 2, nxt[0:1, :], pltpu.roll(dconv, TM - 2, 0)))
        dz = cw2 * dconv + cw1 * dcp1 + cw0 * dcp2
        dcn_ref[...] = dconv[0:8, :]
        dh_ref[:, 2 * aw:3 * aw] = (dyb * conv).astype(BF16)
        dh_ref[:, 3 * aw:4 * aw] = (dz * val).astype(BF16)
        dh_ref[:, 4 * aw:5 * aw] = (dz * g_c).astype(BF16)
        dxn = jnp.zeros((TM, d), F32)
        for k in range(N_CHIPS):
            dxn = dxn + _dot_nt(dh_ref[:, k * ns:(k + 1) * ns], win_ref[k])
        g1v = g1_ref[...]
        _, xh1, r1 = _rms_fwd(x_ref[...], g1v)
        dxr, dg1 = _rms_bwd(dxn, xh1, r1, g1v)
        dg1_ref[...] += dg1
        dx_ref[...] = dx1v + dxr

    ins = (x, dx1, h, h, mixed_all, th_all, g1, lng, lnb, wsp, b_t, cw, ga, gb, w_out, w_in)
    acc_shapes = [(1, d), (1, aw), (1, aw), wsp.shape, (CHUNK, CHUNK), cw.shape, (1, aw), (1, aw)]
    return _host_call(
        body, "mixer_bwd", (n,),
        [rev(d), rev(d), rev(nh), hprev, rev(aw), rev(d)] + [_full(a.shape) for a in ins[6:]],
        [rev(d), rev(nh)] + [_acc(sh) for sh in acc_shapes],
        (jax.ShapeDtypeStruct((s, d), F32), jax.ShapeDtypeStruct((s, nh), BF16))
        + tuple(jax.ShapeDtypeStruct(sh, F32) for sh in acc_shapes),
        [pltpu.VMEM((TM, aw), F32), pltpu.VMEM((8, aw), F32)],
        ins, ("arbitrary",), hosted)


def _weight_grad(a, b, name, tm, tn, col_sharded, hosted=None):
    t, m = a.shape
    n = b.shape[1]

    def body(a_ref, b_ref, o_ref):
        o_ref[...] = _dot_tn(a_ref[...].astype(BF16), b_ref[...].astype(BF16)).astype(BF16)

    if col_sharded:
        ns = n // N_CHIPS
        per = ns // tn
        out_shape = jax.ShapeDtypeStruct((N_CHIPS, m, ns), BF16)
        out_spec = pl.BlockSpec((None, tm, tn), lambda i, j: (j // per, i, j % per))
    else:
        out_shape = jax.ShapeDtypeStruct((m, n), BF16)
        out_spec = pl.BlockSpec((tm, tn), lambda i, j: (i, j))
    (out,), extra = _host_call(
        body, name, (m // tm, n // tn),
        [pl.BlockSpec((t, tm), lambda i, j: (0, i)), pl.BlockSpec((t, tn), lambda i, j: (0, j))],
        [out_spec], (out_shape,), [], (a, b), ("parallel", "parallel"), hosted)
    return (out if col_sharded else out.reshape(N_CHIPS, m // N_CHIPS, n)), extra


def _row_tile(rows, cap=256):
    best = None
    for t in range(16, min(rows, cap) + 1, 16):
        if rows % t == 0:
            best = t
    return best if best is not None else rows


def _adamw_math(w, g, m, v):
    m2 = ADAM_B1 * m + (1.0 - ADAM_B1) * g
    v2 = ADAM_B2 * v + (1.0 - ADAM_B2) * (g * g)
    m_hat = m2 / (1.0 - ADAM_B1 ** ADAM_STEP)
    v_hat = v2 / (1.0 - ADAM_B2 ** ADAM_STEP)
    delta = -ADAM_LR * (m_hat / (jnp.sqrt(v_hat) + ADAM_EPS) + ADAM_WD * w)
    return delta, m2, v2


def _adamw(w, g, m, v, name):
    r, c = w.shape
    tr = _row_tile(r) if r >= 16 else r

    def body(w_ref, g_ref, m_ref, v_ref, d_ref, m2_ref, v2_ref):
        d_ref[...], m2_ref[...], v2_ref[...] = _adamw_math(w_ref[...], g_ref[...], m_ref[...], v_ref[...])

    sh = jax.ShapeDtypeStruct((r, c), F32)
    return pl.pallas_call(
        body, name=name, grid=(r // tr,),
        in_specs=[_rows(tr, c)] * 4, out_specs=[_rows(tr, c)] * 3, out_shape=(sh, sh, sh),
        compiler_params=_params("parallel"),
    )(w, g, m, v)


def _finalize(part, slots, w, m, v, place, name, hosted=None):
    r, c = w.shape
    tr = _row_tile(r)

    def body(place_ref, own_ref, s0_ref, s1_ref, s2_ref, w_ref, m_ref, v_ref, g_ref, d_ref, m2_ref, v2_ref):
        g = own_ref[...].astype(F32) + s0_ref[...].astype(F32)
        g = (g + s1_ref[...].astype(F32)) + s2_ref[...].astype(F32)
        g_ref[...] = g
        d_ref[...], m2_ref[...], v2_ref[...] = _adamw_math(w_ref[...], g, m_ref[...], v_ref[...])

    def slot(k):
        return pl.BlockSpec((None, tr, c), lambda i, pref: (k, i, 0))

    rows = pl.BlockSpec((tr, c), lambda i, pref: (i, 0))
    sh = jax.ShapeDtypeStruct((r, c), F32)
    return _host_call(
        body, name, (r // tr,),
        [pl.BlockSpec((None, tr, c), lambda i, pref: (pref[1], i, 0)), slot(0), slot(1), slot(2), rows, rows, rows],
        [rows] * 4, (sh,) * 4, [], (part, slots, slots, slots, w, m, v), ("parallel",), hosted, prefetch=(place,))


def _small_sum_adamw(parts, w, m, v):
    nd, r, c = parts.shape

    def body(p_ref, w_ref, m_ref, v_ref, g_ref, d_ref, m2_ref, v2_ref):
        g = p_ref[0]
        for k in range(1, nd):
            g = g + p_ref[k]
        g_ref[...] = g
        d_ref[...], m2_ref[...], v2_ref[...] = _adamw_math(w_ref[...], g, m_ref[...], v_ref[...])

    sh = jax.ShapeDtypeStruct((r, c), F32)
    return pl.pallas_call(
        body, name="small_sum_adamw", out_shape=(sh, sh, sh, sh),
        compiler_params=pltpu.CompilerParams(vmem_limit_bytes=VMEM_LIMIT),
    )(parts, w, m, v)


def _place():
    x, y, c = lax.axis_index("x"), lax.axis_index("y"), lax.axis_index("c")
    chips = [(1 - x, y), (x, 1 - y), (1 - x, 1 - y)]
    return x, y, c, 2 * x + y, chips


def _remote(src, dst, send_sem, recv_sem, to):
    return pltpu.make_async_remote_copy(src_ref=src, dst_ref=dst, send_sem=send_sem, recv_sem=recv_sem,
                                        device_id=to, device_id_type=MESH)


class _Exchange:
    def __init__(self, ins, out_shapes, sem_shape, start, finish, middle=None, in_place=False):
        self.ins, self.out_shapes, self.sem_shape = tuple(ins), tuple(out_shapes), sem_shape
        self.start, self.finish, self.middle = start, finish, middle
        self.in_place = in_place


def _run_exchange(ex, name):
    n_in, n_out = len(ex.ins), len(ex.out_shapes)

    def body(*refs):
        ins, outs = refs[:n_in], refs[n_in:n_in + n_out]
        send_sems, recv_sems = refs[n_in + n_out:]
        ex.start(ins, outs, send_sems, recv_sems)
        if ex.middle is not None:
            ex.middle(ins, outs, send_sems, recv_sems)
        ex.finish(ins, outs, send_sems, recv_sems)

    sem = pltpu.SemaphoreType.DMA(ex.sem_shape)
    return pl.pallas_call(
        body, name=name, out_shape=ex.out_shapes, in_specs=[ANY] * n_in, out_specs=[ANY] * n_out,
        input_output_aliases={k: k for k in range(n_in)} if ex.in_place else {}, scratch_shapes=[sem, sem],
    )(*ex.ins)


def _host_call(body, name, grid, in_specs, out_specs, out_shape, scratch_shapes, args, semantics, hosted,
               prefetch=()):
    hosted = [] if hosted is None else (list(hosted) if isinstance(hosted, (list, tuple)) else [hosted])
    n_pre, n_in, n_out, n_scr = len(prefetch), len(in_specs), len(out_specs), len(scratch_shapes)
    h_ins = [a for ex in hosted for a in ex.ins]
    h_outs = [s for ex in hosted for s in ex.out_shapes]
    h_in, h_out = len(h_ins), len(h_outs)

    def wrapped(*refs):
        pre, refs = refs[:n_pre], refs[n_pre:]
        a, hi = refs[:n_in], refs[n_in:n_in + h_in]
        o = refs[n_in + h_in:n_in + h_in + n_out]
        ho = refs[n_in + h_in + n_out:n_in + h_in + n_out + h_out]
        scr = refs[n_in + h_in + n_out + h_out:]

        def run(phase):
            i0 = o0 = 0
            for k, ex in enumerate(hosted):
                fn = getattr(ex, phase)
                if fn is not None:
                    fn(hi[i0:i0 + len(ex.ins)], ho[o0:o0 + len(ex.out_shapes)], scr[n_scr + 2 * k],
                       scr[n_scr + 2 * k + 1])
                i0, o0 = i0 + len(ex.ins), o0 + len(ex.out_shapes)

        if hosted:
            first = functools.reduce(jnp.logical_and, [pl.program_id(k) == 0 for k in range(len(grid))])

            @pl.when(first)
            def _():
                run("start")

        if any(ex.middle is not None for ex in hosted):
            half_way = functools.reduce(jnp.logical_and, [
                pl.program_id(0) == max(1, grid[0] * MIDDLE_STEP_16THS // 16)] + [
                pl.program_id(k) == 0 for k in range(1, len(grid))])

            @pl.when(half_way)
            def _():
                run("middle")

        body(*pre, *a, *o, *scr[:n_scr])

        if hosted:
            last = functools.reduce(jnp.logical_and, [pl.program_id(k) == grid[k] - 1 for k in range(len(grid))])

            @pl.when(last)
            def _():
                run("finish")

    sems = [pltpu.SemaphoreType.DMA(ex.sem_shape) for ex in hosted for _ in range(2)]
    aliases, i0, o0 = {}, n_pre + n_in, n_out
    for ex in hosted:
        if ex.in_place:
            aliases.update({i0 + k: o0 + k for k in range(len(ex.ins))})
        i0, o0 = i0 + len(ex.ins), o0 + len(ex.out_shapes)
    all_in, all_out = list(in_specs) + [ANY] * h_in, list(out_specs) + [ANY] * h_out
    all_scr = list(scratch_shapes) + sems
    params = _params(*(["arbitrary"] * len(grid) if hosted else semantics))
    shapes = tuple(out_shape) + tuple(h_outs)
    if n_pre:
        call = pl.pallas_call(
            wrapped, name=name, out_shape=shapes, input_output_aliases=aliases, compiler_params=params,
            grid_spec=pltpu.PrefetchScalarGridSpec(num_scalar_prefetch=n_pre, grid=grid, in_specs=all_in,
                                                   out_specs=all_out, scratch_shapes=all_scr))
    else:
        call = pl.pallas_call(
            wrapped, name=name, grid=grid, in_specs=all_in, out_specs=all_out, out_shape=shapes,
            scratch_shapes=all_scr, input_output_aliases=aliases, compiler_params=params)
    res = call(*prefetch, *args, *h_ins)
    return res[:n_out], res[n_out:]


def _all_gather(shards, small=()):
    items = tuple(shards) + tuple(small)
    nw = len(shards)

    def place():
        x, y, c, me, _ = _place()
        first = (x + (1 - c) * (1 - 2 * x), y + c * (1 - 2 * y))
        second = (x + c * (1 - 2 * x), y + (1 - c) * (1 - 2 * y))
        diag = (1 - x, 1 - y)
        return x, y, c, me, (first, second, diag)

    def halves(w, c):
        rh = items[w].shape[0] // 2
        return pl.ds(c * rh, rh), pl.ds((1 - c) * rh, rh)

    def start(ins, outs, ss, rs):
        x, y, c, me, chips = place()
        for w in range(len(items)):
            _remote(ins[w], outs[w].at[me], ss.at[w, 6], rs.at[w, 6], (x, y, 1 - c)).start()
            if w < nw:
                mine, _ = halves(w, c)
                _remote(ins[w].at[mine], outs[w].at[me, mine], ss.at[w, 0], rs.at[w, 0], (*chips[0], c)).start()
            else:
                for k in range(3):
                    _remote(ins[w], outs[w].at[me], ss.at[w, k], rs.at[w, k], (*chips[k], c)).start()

    def onward(outs, ss, rs, w, k, x, y, c, chips):
        mine, _ = halves(w, c)
        pk = 2 * chips[k][0] + chips[k][1]
        got = outs[w].at[pk, mine]
        src = chips[1] if k == 2 else chips[k]
        _remote(got, got, ss.at[w, k], rs.at[w, k], (*src, c)).wait_recv()
        if k == 0:
            _remote(got, got, ss.at[w, 2], rs.at[w, 2], (*chips[1], c)).start()
        _remote(got, got, ss.at[w, 3 + k], rs.at[w, 3 + k], (x, y, 1 - c)).start()

    def middle(ins, outs, ss, rs):
        x, y, c, me, chips = place()
        for w in range(nw):
            mine, _ = halves(w, c)
            _remote(ins[w].at[mine], outs[w].at[me, mine], ss.at[w, 1], rs.at[w, 1], (*chips[1], c)).start()
        for w in range(nw):
            onward(outs, ss, rs, w, 0, x, y, c, chips)

    def finish(ins, outs, ss, rs):
        x, y, c, me, chips = place()
        sib = (x, y, 1 - c)
        for k in (1, 2):
            for w in range(nw):
                onward(outs, ss, rs, w, k, x, y, c, chips)
        for w in range(len(items)):
            if w < nw:
                mine, theirs = halves(w, c)
                for k, chip in ((3, chips[1]), (4, chips[0]), (5, chips[2])):
                    oth = outs[w].at[2 * chip[0] + chip[1], theirs]
                    _remote(oth, oth, ss.at[w, k], rs.at[w, k], sib).wait_recv()
                own = ins[w].at[mine]
                for k in range(6):
                    _remote(own, own, ss.at[w, k], rs.at[w, k], sib).wait_send()
            else:
                for k in range(3):
                    got = outs[w].at[2 * chips[k][0] + chips[k][1]]
                    _remote(got, got, ss.at[w, k], rs.at[w, k], (*chips[k], c)).wait_recv()
                    _remote(ins[w], ins[w], ss.at[w, k], rs.at[w, k], sib).wait_send()
            _remote(ins[w], outs[w].at[me], ss.at[w, 6], rs.at[w, 6], sib).wait()

    out_shapes = tuple(jax.ShapeDtypeStruct((N_CHIPS,) + a.shape, a.dtype) for a in items)
    return _Exchange(items, out_shapes, (len(items), 7), start, finish, middle if nw else None)


def _chip_reduce(grads, name):
    nw = len(grads)
    step = 64

    def body(*refs):
        ins, outs = refs[:nw], refs[nw:2 * nw]
        own, got = refs[2 * nw:3 * nw], refs[3 * nw:4 * nw]
        send_sems, recv_sems, local_sems = refs[4 * nw:]
        x, y, c, _, _ = _place()
        moves = []
        for w in range(nw):
            rh = grads[w].shape[1] // 2
            away = _remote(ins[w].at[:, pl.ds((1 - c) * rh, rh), :], got[w], send_sems.at[w], recv_sems.at[w],
                           (x, y, 1 - c))
            mine = pltpu.make_async_copy(ins[w].at[:, pl.ds(c * rh, rh), :], own[w], local_sems.at[w])
            away.start()
            mine.start()
            moves.append((away, mine))
        back = []
        for w, (away, mine) in enumerate(moves):
            nb, rh, _ = own[w].shape
            mine.wait()
            away.wait()
            for k in range(nb):
                def add(i, carry, w=w, k=k):
                    rows = pl.ds(pl.multiple_of(i * step, step), step)
                    own[w][k, rows, :] = (own[w][k, rows, :].astype(F32) + got[w][k, rows, :].astype(F32)).astype(BF16)
                    return carry
                lax.fori_loop(0, rh // step, add, 0)
                tail = rh % step
                if tail:
                    rows = slice(rh - tail, rh)
                    own[w][k, rows, :] = (own[w][k, rows, :].astype(F32) + got[w][k, rows, :].astype(F32)).astype(BF16)
            wb = pltpu.make_async_copy(own[w], outs[w].at[:, pl.ds(c * rh, rh), :], local_sems.at[w])
            wb.start()
            back.append(wb)
        for wb in back:
            wb.wait()

    halves = [pltpu.VMEM((g.shape[0], g.shape[1] // 2, g.shape[2]), BF16) for g in grads]
    sem = pltpu.SemaphoreType.DMA((nw,))
    return pl.pallas_call(
        body, name=name, out_shape=tuple(jax.ShapeDtypeStruct(g.shape, BF16) for g in grads),
        in_specs=[ANY] * nw, out_specs=[ANY] * nw, scratch_shapes=halves + halves + [sem, sem, sem],
        compiler_params=pltpu.CompilerParams(vmem_limit_bytes=VMEM_LIMIT),
    )(*grads)


def _scatter_partials(parts):
    nw = len(parts)

    def copies(ins, outs, ss, rs):
        _, _, c, _, chips = _place()
        res = []
        for r, (px, py) in enumerate(chips):
            for w in range(nw):
                rh = parts[w].shape[1] // 2
                rows = pl.ds(c * rh, rh)
                res.append(_remote(ins[w].at[2 * px + py, rows], outs[w].at[r, rows], ss.at[w, r], rs.at[w, r],
                                   (px, py, c)))
        return res

    def start(ins, outs, ss, rs):
        for cp in copies(ins, outs, ss, rs):
            cp.start()

    def finish(ins, outs, ss, rs):
        for cp in copies(ins, outs, ss, rs):
            cp.wait()

    out_shapes = tuple(jax.ShapeDtypeStruct((3,) + p.shape[1:], p.dtype) for p in parts)
    return _Exchange(parts, out_shapes, (nw, 3), start, finish)


def _join_partials(parts, slots):
    nw = len(parts)

    def copies(outs, ss, rs, mine):
        x, y, c, me, _ = _place()
        res = []
        for w in range(nw):
            rh = parts[w].shape[1] // 2
            rows = pl.ds((c if mine else 1 - c) * rh, rh)
            own = outs[w].at[me, rows]
            got = outs[nw + w].at[:, rows, :]
            res.append(_remote(own, own, ss.at[w, 0], rs.at[w, 0], (x, y, 1 - c)))
            res.append(_remote(got, got, ss.at[w, 1], rs.at[w, 1], (x, y, 1 - c)))
        return res

    def start(ins, outs, ss, rs):
        for cp in copies(outs, ss, rs, True):
            cp.start()

    def finish(ins, outs, ss, rs):
        for cp in copies(outs, ss, rs, True):
            cp.wait_send()
        for cp in copies(outs, ss, rs, False):
            cp.wait_recv()

    arrays = tuple(parts) + tuple(slots)
    return _Exchange(arrays, tuple(jax.ShapeDtypeStruct(a.shape, a.dtype) for a in arrays), (nw, 2), start, finish,
                     in_place=True)


def _gather_small(slab):
    def copies(ins, outs, ss, rs):
        x, y, c, _, _ = _place()
        me = 4 * x + 2 * y + c
        own = pltpu.make_async_copy(ins[0], outs[0].at[me], ss.at[7])
        out, arrivals = [], []
        for k in range(1, 8):
            px = 1 - x if k & 4 else x
            py = 1 - y if k & 2 else y
            pc = 1 - c if k & 1 else c
            out.append(_remote(ins[0], outs[0].at[me], ss.at[k - 1], rs.at[k - 1], (px, py, pc)))
            theirs = outs[0].at[4 * px + 2 * py + pc]
            arrivals.append(_remote(theirs, theirs, ss.at[k - 1], rs.at[k - 1], (px, py, pc)))
        return own, out, arrivals

    def start(ins, outs, ss, rs):
        own, out, _ = copies(ins, outs, ss, rs)
        own.start()
        for cp in out:
            cp.start()

    def finish(ins, outs, ss, rs):
        own, out, arrivals = copies(ins, outs, ss, rs)
        for cp in out:
            cp.wait_send()
        for cp in arrivals:
            cp.wait_recv()
        own.wait()

    return _Exchange((slab,), (jax.ShapeDtypeStruct((8,) + slab.shape, slab.dtype),), (8,), start, finish)


_SMALL_VECS = ("ln_mix_g", "ln_attn_g", "ln_mem_g", "ln_ffn_g", "ln_final_g")


def _pack_small(p, extra, conv):
    d = p["ln_mix_g"].shape[-1]
    top = [p[k].reshape(1, d) for k in _SMALL_VECS]
    top.append(jnp.concatenate([p["sgu_ln_g"].reshape(-1), p["sgu_ln_b"].reshape(-1)]).reshape(1, d))
    top.append(jnp.concatenate([p["grp_norm_a"].reshape(-1), p["grp_norm_b"].reshape(-1)]).reshape(1, d))
    top.append(jnp.concatenate([p["b_spatial"].reshape(-1), extra]).reshape(1, d))
    mid = jnp.zeros((8, d), F32)
    if conv is not None:
        mid = jnp.pad(conv, ((0, 5), (0, d - conv.shape[1])))
    return jnp.concatenate([jnp.concatenate(top, axis=0), mid, p["w_spatial"].reshape(-1, d)], axis=0)


def _unpack_small(slab):
    d = slab.shape[1]
    hw = d // 2
    out = {k: slab[i] for i, k in enumerate(_SMALL_VECS)}
    out["sgu_ln_g"], out["sgu_ln_b"] = slab[5, :hw], slab[5, hw:]
    out["grp_norm_a"], out["grp_norm_b"] = slab[6, :hw], slab[6, hw:]
    out["b_spatial"] = slab[7, :hw].reshape(HEADS, CHUNK)
    out["w_spatial"] = slab[16:].reshape(HEADS, CHUNK, CHUNK)
    return out


_BIG = ("w_in", "w_kv", "w_gate_up", "w_out", "w_q", "w_o", "w_down")
_WEIGHTS = ("ln_mix_g", "w_in", "sgu_ln_g", "sgu_ln_b", "w_spatial", "b_spatial", "conv_w", "grp_norm_a",
            "grp_norm_b", "w_out", "ln_attn_g", "ln_mem_g", "w_q", "w_kv", "w_o", "ln_ffn_g", "w_gate_up",
            "w_down", "ln_final_g")


def _step(p, m_, v_, x, mem, target):
    s, d = x.shape
    hw = d // 2
    row = lambda a: a.reshape(1, -1)
    x_, y_, c_ = lax.axis_index("x"), lax.axis_index("y"), lax.axis_index("c")
    chip = 2 * x_ + y_

    bf = {k: p[k].astype(BF16) for k in _BIG}
    conv8 = jnp.pad(p["conv_w"], ((0, 5), (0, 0)))
    w_in, w_out4, conv4 = _run_exchange(_all_gather([bf["w_in"], bf["w_out"]], [conv8]), "all_gather_mixer")
    cw = jnp.transpose(conv4[:, :3, :], (1, 0, 2)).reshape(3, hw)
    b_t = jnp.pad(jnp.transpose(p["b_spatial"]), ((0, 0), (0, CHUNK - HEADS)))
    g1, g2, gm, g3, gf = (row(p[k]) for k in _SMALL_VECS)
    lng, lnb, ga, gb = row(p["sgu_ln_g"]), row(p["sgu_ln_b"]), row(p["grp_norm_a"]), row(p["grp_norm_b"])
    wsp = p["w_spatial"]
    w_out = w_out4.reshape(-1, d)

    (h, x1, ycat, xn1, mixed, th), (w_kv, w_q4, w_o4, w_down4) = _mixer_fwd(
        x, g1, w_in, lng, lnb, wsp, b_t, cw, ga, gb, w_out,
        hosted=_all_gather([bf[k] for k in ("w_kv", "w_q", "w_o", "w_down")]))
    w_q, w_o, w_down = (a.reshape(-1, d) for a in (w_q4, w_o4, w_down4))
    memn, kv = _kv_fwd(mem, gm, w_kv)
    (x2, o, qs, probs), (w_gu,) = _attn_fwd(x1, g2, w_q, kv, w_o, hosted=_all_gather([bf["w_gate_up"]]))
    dx2, act, dgu, xn3, dx3, loss, dgf, dg3 = _ffn_fwd_bwd(x2, g3, gf, target, w_gu, w_down)

    place = jnp.stack([c_, chip]).astype(jnp.int32)

    def chip_partials(names, grads, tag):
        return list(_chip_reduce(grads, "chip_reduce_" + tag))

    names_d = ("w_down",)
    parts_d = chip_partials(names_d, (_weight_grad(act, dx3, "grad_w_down", 1408, 512, False)[0],), "down")
    g_gu, slots_d = _weight_grad(xn3, dgu, "grad_w_gate_up", 512, 1408, True, hosted=_scatter_partials(parts_d))
    names_a = ("w_gate_up",)
    parts_a = chip_partials(names_a, (g_gu,), "ffn")
    (dx1, dkv, dg2, g_o, g_q, g_out), slots_a = _attn_bwd(x1, dx2, o, ycat, qs, probs, g2, w_q, kv, w_o,
                                                           hosted=_scatter_partials(parts_a))
    g_kv, dgm = _kv_bwd(dkv, mem, memn, gm, w_kv)
    names_b = ("w_o", "w_out", "w_q", "w_kv")
    shard_major = lambda g: g.reshape(N_CHIPS, -1, d)
    parts_b = chip_partials(names_b, (shard_major(g_o), shard_major(g_out), shard_major(g_q), g_kv), "attn")
    (dx, dh, dg1, dlng, dlnb, dwsp, dbt, dcw, dga, dgb), slots_b = _mixer_bwd(
        x, dx1, h, mixed, th, g1, lng, lnb, wsp, b_t, cw, ga, gb, w_out, w_in, hosted=_scatter_partials(parts_b))
    small = {"ln_mix_g": dg1, "ln_attn_g": dg2, "ln_mem_g": dgm, "ln_ffn_g": dg3, "ln_final_g": dgf,
             "sgu_ln_g": dlng, "sgu_ln_b": dlnb, "grp_norm_a": dga, "grp_norm_b": dgb,
             "b_spatial": jnp.transpose(dbt[:, :HEADS]), "w_spatial": dwsp}
    loss_vec = jnp.pad(loss.reshape(1), (0, hw - 1))
    names = names_d + names_a + names_b
    n_done = len(names)
    g_in, extra = _weight_grad(
        xn1, dh, "grad_w_in", 1024, 640, True,
        hosted=[_gather_small(_pack_small(small, loss_vec, dcw)),
                _join_partials(parts_d + parts_a + parts_b, slots_d + slots_a + slots_b)])
    parts, whole = extra[0], dict(zip(names, zip(extra[1:1 + n_done], extra[1 + n_done:])))
    (part_in,) = chip_partials(("w_in",), (g_in,), "mixer")
    out_g, out_d, out_m, out_v = {}, {}, {}, {}

    def finalize(k, hosted=None):
        (out_g[k], out_d[k], out_m[k], out_v[k]), res = _finalize(
            whole[k][0], whole[k][1], p[k], m_[k], v_[k], place, "finalize_" + k, hosted)
        return res

    (slots_in,) = finalize("w_down", _scatter_partials([part_in]))
    for k in names:
        if k != "w_down":
            finalize(k)
    whole["w_in"] = _run_exchange(_join_partials([part_in], [slots_in]), "rs_join_mixer")
    finalize("w_in")

    zeros = jnp.zeros((hw,), F32)
    sg, sd, sm, sv = _small_sum_adamw(parts, _pack_small(p, zeros, None), _pack_small(m_, zeros, None),
                                      _pack_small(v_, zeros, None))
    for tree, slab in zip((out_g, out_d, out_m, out_v), (sg, sd, sm, sv)):
        tree.update(_unpack_small(slab))
    loss_out = sg[7, hw]
    g_conv = lax.dynamic_slice(sg[8:11, :hw], (0, chip * (hw // N_CHIPS)), (3, hw // N_CHIPS))
    out_g["conv_w"] = g_conv
    out_d["conv_w"], out_m["conv_w"], out_v["conv_w"] = _adamw(p["conv_w"], g_conv, m_["conv_w"], v_["conv_w"],
                                                                "adamw_conv_w")
    return loss_out, dx, out_g, out_d, out_m, out_v


def kernel(x, mem, ln_mix_g, w_in, sgu_ln_g, sgu_ln_b, w_spatial, b_spatial, conv_w, grp_norm_a, grp_norm_b, w_out, ln_attn_g, ln_mem_g, w_q, w_kv, w_o, ln_ffn_g, w_gate_up, w_down, ln_final_g, loss_target, m_ln_mix_g, m_w_in, m_sgu_ln_g, m_sgu_ln_b, m_w_spatial, m_b_spatial, m_conv_w, m_grp_norm_a, m_grp_norm_b, m_w_out, m_ln_attn_g, m_ln_mem_g, m_w_q, m_w_kv, m_w_o, m_ln_ffn_g, m_w_gate_up, m_w_down, m_ln_final_g, v_ln_mix_g, v_w_in, v_sgu_ln_g, v_sgu_ln_b, v_w_spatial, v_b_spatial, v_conv_w, v_grp_norm_a, v_grp_norm_b, v_w_out, v_ln_attn_g, v_ln_mem_g, v_w_q, v_w_kv, v_w_o, v_ln_ffn_g, v_w_gate_up, v_w_down, v_ln_final_g):
    p = dict(ln_mix_g=ln_mix_g, w_in=w_in, sgu_ln_g=sgu_ln_g, sgu_ln_b=sgu_ln_b, w_spatial=w_spatial,
             b_spatial=b_spatial, conv_w=conv_w, grp_norm_a=grp_norm_a, grp_norm_b=grp_norm_b, w_out=w_out,
             ln_attn_g=ln_attn_g, ln_mem_g=ln_mem_g, w_q=w_q, w_kv=w_kv, w_o=w_o, ln_ffn_g=ln_ffn_g,
             w_gate_up=w_gate_up, w_down=w_down, ln_final_g=ln_final_g)
    m_ = dict(ln_mix_g=m_ln_mix_g, w_in=m_w_in, sgu_ln_g=m_sgu_ln_g, sgu_ln_b=m_sgu_ln_b, w_spatial=m_w_spatial,
              b_spatial=m_b_spatial, conv_w=m_conv_w, grp_norm_a=m_grp_norm_a, grp_norm_b=m_grp_norm_b,
              w_out=m_w_out, ln_attn_g=m_ln_attn_g, ln_mem_g=m_ln_mem_g, w_q=m_w_q, w_kv=m_w_kv, w_o=m_w_o,
              ln_ffn_g=m_ln_ffn_g, w_gate_up=m_w_gate_up, w_down=m_w_down, ln_final_g=m_ln_final_g)
    v_ = dict(ln_mix_g=v_ln_mix_g, w_in=v_w_in, sgu_ln_g=v_sgu_ln_g, sgu_ln_b=v_sgu_ln_b, w_spatial=v_w_spatial,
              b_spatial=v_b_spatial, conv_w=v_conv_w, grp_norm_a=v_grp_norm_a, grp_norm_b=v_grp_norm_b,
              w_out=v_w_out, ln_attn_g=v_ln_attn_g, ln_mem_g=v_ln_mem_g, w_q=v_w_q, w_kv=v_w_kv, w_o=v_w_o,
              ln_ffn_g=v_ln_ffn_g, w_gate_up=v_w_gate_up, w_down=v_w_down, ln_final_g=v_ln_final_g)
    s, d = x.shape[-2], x.shape[-1]
    loss, dx, g, dl, nm, nv = _step(p, m_, v_, x.reshape(s, d), mem.reshape(-1, d), loss_target.reshape(s, d))
    outs = [loss, dx.reshape(x.shape)]
    for tree in (g, dl, nm, nv):
        outs += [tree[k].reshape(p[k].shape) for k in _WEIGHTS]
    return tuple(outs)
```

```python
import functools
import math

import jax
import jax.numpy as jnp
from jax import lax
from jax.experimental import pallas as pl
from jax.experimental.pallas import tpu as pltpu

F32 = jnp.float32
BF16 = jnp.bfloat16
EPS = 1e-6
CHUNK = 128
HEADS = 4
N_CHIPS = 4
TM = 512
TM_ATTN = 512
TM_FFN = 256
ADAM_LR, ADAM_B1, ADAM_B2, ADAM_EPS, ADAM_WD, ADAM_STEP = 0.001, 0.9, 0.999, 1e-08, 0.01, 10
GELU_C = math.sqrt(2.0 / math.pi)
GELU_K = 0.044715
SMALL_ROWS = 80
VMEM_LIMIT = 56 * 1024 * 1024
MIDDLE_STEP_16THS = 7
MESH = pl.DeviceIdType.MESH
ANY = pl.BlockSpec(memory_space=pl.ANY)


def _params(*sem):
    return pltpu.CompilerParams(dimension_semantics=sem, vmem_limit_bytes=VMEM_LIMIT)


def _dot(a, b):
    return jnp.dot(a, b, preferred_element_type=F32)


def _dot_nt(a, b):
    return lax.dot_general(a, b, (((1,), (1,)), ((), ())), preferred_element_type=F32)


def _dot_tn(a, b):
    return lax.dot_general(a, b, (((0,), (0,)), ((), ())), preferred_element_type=F32)


def _rms_fwd(x, g):
    r = lax.rsqrt(jnp.mean(x * x, axis=-1, keepdims=True) + EPS)
    xh = x * r
    return xh * g, xh, r


def _rms_bwd(dy, xh, r, g):
    dxh = dy * g
    dx = r * (dxh - xh * jnp.mean(dxh * xh, axis=-1, keepdims=True))
    return dx, jnp.sum(dy * xh, axis=0, keepdims=True)


def _full(shape):
    nd = len(shape)
    return pl.BlockSpec(shape, lambda *_: (0,) * nd, pipeline_mode=pl.Buffered(1))


def _acc(shape):
    nd = len(shape)
    return pl.BlockSpec(shape, lambda *_: (0,) * nd)


def _rows(tm, cols):
    return pl.BlockSpec((tm, cols), lambda i: (i, 0))


def _tril_weights(wsp_ref):
    row = lax.broadcasted_iota(jnp.int32, (CHUNK, CHUNK), 0)
    col = lax.broadcasted_iota(jnp.int32, (CHUNK, CHUNK), 1)
    return [jnp.where(row >= col, wsp_ref[hd], 0.0).astype(BF16) for hd in range(HEADS)]


def _shift_rows(z, zp):
    row = lax.broadcasted_iota(jnp.int32, z.shape, 0)
    zm1 = jnp.where(row == 0, zp[7:8, :], pltpu.roll(z, 1, 0))
    zm2 = jnp.where(row == 0, zp[6:7, :], jnp.where(row == 1, zp[7:8, :], pltpu.roll(z, 2, 0)))
    return zm1, zm2


def _gelu_parts(x):
    t = jnp.tanh(GELU_C * (x + GELU_K * (x * x * x)))
    return 0.5 * x * (1.0 + t), t


def _layer_norm_parts(v, g, b):
    mu = jnp.mean(v, axis=-1, keepdims=True)
    vc = v - mu
    rs = lax.rsqrt(jnp.mean(vc * vc, axis=-1, keepdims=True) + EPS)
    vhat = vc * rs
    return vhat * g + b, vhat, rs


def _kv_fwd(mem, g_mem, w_kv):
    m, d = mem.shape
    ns = w_kv.shape[2]

    def body(mem_ref, g_ref, w_ref, memn_ref, kv_ref):
        y, _, _ = _rms_fwd(mem_ref[...], g_ref[...])
        yb = y.astype(BF16)
        memn_ref[...] = yb
        for k in range(N_CHIPS):
            kv_ref[:, k * ns:(k + 1) * ns] = _dot(yb, w_ref[k]).astype(BF16)

    return pl.pallas_call(
        body, name="kv_fwd",
        out_shape=(jax.ShapeDtypeStruct((m, d), BF16), jax.ShapeDtypeStruct((m, N_CHIPS * ns), BF16)),
        compiler_params=pltpu.CompilerParams(vmem_limit_bytes=VMEM_LIMIT),
    )(mem, g_mem, w_kv)


def _mixer_fwd(x, g1, w_in, lng, lnb, wsp, b_t, cw, ga, gb, w_out, hosted=None):
    s, d = x.shape
    n = s // TM
    nch = TM // CHUNK
    ns = w_in.shape[2]
    nh = N_CHIPS * ns
    aw = d // 2
    hd_w = aw // HEADS

    def body(x_ref, g1_ref, win_ref, lng_ref, lnb_ref, wsp_ref, bt_ref, cw_ref, ga_ref, gb_ref, wout_ref,
             h_ref, x1_ref, y_ref, xn_ref, mix_ref, th_ref, zp_ref):
        i = pl.program_id(0)

        @pl.when(i == 0)
        def _():
            zp_ref[...] = jnp.zeros_like(zp_ref)

        x = x_ref[...]
        xn, _, _ = _rms_fwd(x, g1_ref[...])
        xnb = xn.astype(BF16)
        xn_ref[...] = xnb
        for k in range(N_CHIPS):
            h_ref[:, k * ns:(k + 1) * ns] = _dot(xnb, win_ref[k])
        a, th = _gelu_parts(h_ref[:, 0:2 * aw])
        th_ref[...] = th
        u = a[:, :aw]
        vn, _, _ = _layer_norm_parts(a[:, aw:], lng_ref[...], lnb_ref[...])
        vnb = vn.astype(BF16)
        wm = _tril_weights(wsp_ref)
        for c in range(nch):
            for hd in range(HEADS):
                blk = vnb[c * CHUNK:(c + 1) * CHUNK, hd * hd_w:(hd + 1) * hd_w]
                mix_ref[c * CHUNK:(c + 1) * CHUNK, hd * hd_w:(hd + 1) * hd_w] = _dot(wm[hd], blk) + bt_ref[:, hd:hd + 1]
        ya, _, _ = _rms_fwd(u * mix_ref[...], ga_ref[...])
        g_b = h_ref[:, 2 * aw:3 * aw]
        z = h_ref[:, 3 * aw:4 * aw] * h_ref[:, 4 * aw:5 * aw]
        zm1, zm2 = _shift_rows(z, zp_ref[...])
        conv = cw_ref[0:1, :] * zm2 + cw_ref[1:2, :] * zm1 + cw_ref[2:3, :] * z
        yb, _, _ = _rms_fwd(g_b * conv, gb_ref[...])
        zp_ref[...] = z[TM - 8:TM, :]
        ycat = jnp.concatenate([ya, yb], axis=-1).astype(BF16)
        y_ref[...] = ycat
        x1_ref[...] = x + _dot(ycat, wout_ref[...])

    return _host_call(
        body, "mixer_fwd", (n,),
        [_rows(TM, d), _full(g1.shape), _full(w_in.shape), _full(lng.shape), _full(lnb.shape),
         _full(wsp.shape), _full(b_t.shape), _full(cw.shape), _full(ga.shape), _full(gb.shape),
         _full(w_out.shape)],
        [_rows(TM, nh), _rows(TM, d), _rows(TM, d), _rows(TM, d), _rows(TM, aw), _rows(TM, d)],
        (jax.ShapeDtypeStruct((s, nh), F32), jax.ShapeDtypeStruct((s, d), F32),
         jax.ShapeDtypeStruct((s, d), BF16), jax.ShapeDtypeStruct((s, d), BF16),
         jax.ShapeDtypeStruct((s, aw), F32), jax.ShapeDtypeStruct((s, d), F32)),
        [pltpu.VMEM((8, aw), F32)],
        (x, g1, w_in, lng, lnb, wsp, b_t, cw, ga, gb, w_out), ("arbitrary",), hosted)


def _attn_fwd(x1, g2, w_q, kv, w_o, hosted=None):
    s, d = x1.shape
    tm = min(TM_ATTN, s)
    n = s // tm
    dh = d // HEADS
    m = kv.shape[0]
    scale = dh ** -0.5

    def body(x1_ref, g2_ref, wq_ref, kv_ref, wo_ref, x2_ref, o_ref, q_ref, p_ref):
        x1v = x1_ref[...]
        xn, _, _ = _rms_fwd(x1v, g2_ref[...])
        q_ref[...] = _dot(xn.astype(BF16), wq_ref[...]).astype(BF16)
        for hd in range(HEADS):
            kh = kv_ref[:, hd * dh:(hd + 1) * dh]
            vh = kv_ref[:, d + hd * dh:d + (hd + 1) * dh]
            sc = _dot_nt(q_ref[:, hd * dh:(hd + 1) * dh], kh) * scale
            e = jnp.exp(sc - jnp.max(sc, axis=-1, keepdims=True))
            p = e / jnp.sum(e, axis=-1, keepdims=True)
            p_ref[:, hd * m:(hd + 1) * m] = p
            o_ref[:, hd * dh:(hd + 1) * dh] = _dot(p.astype(BF16), vh).astype(BF16)
        x2_ref[...] = x1v + _dot(o_ref[...], wo_ref[...])

    return _host_call(
        body, "attn_fwd", (n,),
        [_rows(tm, d), _full(g2.shape), _full(w_q.shape), _full(kv.shape), _full(w_o.shape)],
        [_rows(tm, d), _rows(tm, d), _rows(tm, d), _rows(tm, HEADS * m)],
        (jax.ShapeDtypeStruct((s, d), F32), jax.ShapeDtypeStruct((s, d), BF16), jax.ShapeDtypeStruct((s, d), BF16),
         jax.ShapeDtypeStruct((s, HEADS * m), F32)),
        [], (x1, g2, w_q, kv, w_o), ("parallel",), hosted)


def _ffn_fwd_bwd(x2, g3, gf, target, w_gu, w_down):
    s, d = x2.shape
    tm = min(TM_FFN, s)
    n = s // tm
    ns = w_gu.shape[2]
    ff = 2 * ns

    def body(x2_ref, g3_ref, gf_ref, t_ref, wgu_ref, wd_ref,
             dx2_ref, act_ref, dgu_ref, xn_ref, dx3_ref, loss_ref, dgf_ref, dg3_ref):
        i = pl.program_id(0)

        @pl.when(i == 0)
        def _():
            loss_ref[...] = jnp.zeros_like(loss_ref)
            dgf_ref[...] = jnp.zeros_like(dgf_ref)
            dg3_ref[...] = jnp.zeros_like(dg3_ref)

        x2v = x2_ref[...]
        xn, xh3, r3 = _rms_fwd(x2v, g3_ref[...])
        xnb = xn.astype(BF16)
        xn_ref[...] = xnb
        x3 = x2v
        saved = []
        for j in range(2):
            g = _dot(xnb, wgu_ref[j])
            u = _dot(xnb, wgu_ref[2 + j])
            sg = 1.0 / (1.0 + jnp.exp(-g))
            sl = g * sg
            actb = (sl * u).astype(BF16)
            act_ref[:, j * ns:(j + 1) * ns] = actb
            x3 = x3 + _dot(actb, wd_ref[j * ns:(j + 1) * ns, :])
            saved.append((u, sl, sg * (1.0 + g * (1.0 - sg))))
        gfv = gf_ref[...]
        y, xhf, rf = _rms_fwd(x3, gfv)
        e = y - t_ref[...]
        loss_ref[...] += 0.5 * jnp.sum(jnp.sum(e * e, axis=-1, keepdims=True), axis=0, keepdims=True) / d
        dx3, dgf = _rms_bwd(e / d, xhf, rf, gfv)
        dgf_ref[...] += dgf
        dx3b = dx3.astype(BF16)
        dx3_ref[...] = dx3b
        dxn = jnp.zeros_like(x2v)
        for j in range(2):
            u, sl, dsl = saved[j]
            dact = _dot_nt(dx3b, wd_ref[j * ns:(j + 1) * ns, :])
            dgb = (dact * u * dsl).astype(BF16)
            dub = (dact * sl).astype(BF16)
            dgu_ref[:, j * ns:(j + 1) * ns] = dgb
            dgu_ref[:, ff + j * ns:ff + (j + 1) * ns] = dub
            dxn = dxn + _dot_nt(dgb, wgu_ref[j]) + _dot_nt(dub, wgu_ref[2 + j])
        dxr, dg3 = _rms_bwd(dxn, xh3, r3, g3_ref[...])
        dg3_ref[...] += dg3
        dx2_ref[...] = dx3 + dxr

    vec = jax.ShapeDtypeStruct((1, d), F32)
    return pl.pallas_call(
        body, name="ffn_fwd_bwd", grid=(n,),
        in_specs=[_rows(tm, d), _full(g3.shape), _full(gf.shape), _rows(tm, d), _full(w_gu.shape),
                  _full(w_down.shape)],
        out_specs=[_rows(tm, d), _rows(tm, ff), _rows(tm, 2 * ff), _rows(tm, d), _rows(tm, d),
                   _acc((1, 1)), _acc((1, d)), _acc((1, d))],
        out_shape=(jax.ShapeDtypeStruct((s, d), F32), jax.ShapeDtypeStruct((s, ff), BF16),
                   jax.ShapeDtypeStruct((s, 2 * ff), BF16), jax.ShapeDtypeStruct((s, d), BF16),
                   jax.ShapeDtypeStruct((s, d), BF16), jax.ShapeDtypeStruct((1, 1), F32), vec, vec),
        compiler_params=_params("arbitrary"),
    )(x2, g3, gf, target, w_gu, w_down)


def _attn_bwd(x1, dx2, o, ycat, qs, probs, g2, w_q, kv, w_o, hosted=None):
    s, d = x1.shape
    tm = min(TM_ATTN, s)
    n = s // tm
    dh = d // HEADS
    scale = dh ** -0.5
    m = kv.shape[0]

    def body(x1_ref, dx2_ref, o_ref, y_ref, q_ref, p_ref, g2_ref, wq_ref, kv_ref, wo_ref,
             dx1_ref, dkv_ref, dg2_ref, gwo_out, gwq_out, gwout_out, dq_ref, gwo_ref, gwq_ref, gwout_ref):
        i = pl.program_id(0)

        @pl.when(i == 0)
        def _():
            for r in (dkv_ref, dg2_ref, gwo_ref, gwq_ref, gwout_ref):
                r[...] = jnp.zeros_like(r)

        xn, xh2, r2 = _rms_fwd(x1_ref[...], g2_ref[...])
        xnb = xn.astype(BF16)
        dx2v = dx2_ref[...]
        dx2b = dx2v.astype(BF16)
        gwo_ref[...] += _dot_tn(o_ref[...], dx2b)
        do = _dot_nt(dx2b, wo_ref[...])
        for hd in range(HEADS):
            qb = q_ref[:, hd * dh:(hd + 1) * dh]
            p = p_ref[:, hd * m:(hd + 1) * m]
            kh = kv_ref[:, hd * dh:(hd + 1) * dh]
            vh = kv_ref[:, d + hd * dh:d + (hd + 1) * dh]
            dob = do[:, hd * dh:(hd + 1) * dh].astype(BF16)
            dp = _dot_nt(dob, vh)
            ds = p * (dp - jnp.sum(dp * p, axis=-1, keepdims=True))
            dsb = (ds * scale).astype(BF16)
            dq_ref[:, hd * dh:(hd + 1) * dh] = _dot(dsb, kh).astype(BF16)
            dkv_ref[:, hd * dh:(hd + 1) * dh] += _dot_tn(dsb, qb)
            dkv_ref[:, d + hd * dh:d + (hd + 1) * dh] += _dot_tn(p.astype(BF16), dob)
        dqb = dq_ref[...]
        gwq_ref[...] += _dot_tn(xnb, dqb)
        dxn = _dot_nt(dqb, wq_ref[...])
        dxr, dg2 = _rms_bwd(dxn, xh2, r2, g2_ref[...])
        dg2_ref[...] += dg2
        dx1 = dx2v + dxr
        dx1_ref[...] = dx1
        gwout_ref[...] += _dot_tn(y_ref[...], dx1.astype(BF16))

        @pl.when(i == n - 1)
        def _():
            for acc, out in ((gwo_ref, gwo_out), (gwq_ref, gwq_out), (gwout_ref, gwout_out)):
                out[...] = acc[...].astype(BF16)

    sq = jax.ShapeDtypeStruct((d, d), BF16)
    return _host_call(
        body, "attn_bwd", (n,),
        [_rows(tm, d), _rows(tm, d), _rows(tm, d), _rows(tm, d), _rows(tm, d), _rows(tm, HEADS * m),
         _full(g2.shape), _full(w_q.shape), _full(kv.shape), _full(w_o.shape)],
        [_rows(tm, d), _acc((m, 2 * d)), _acc((1, d)), _acc((d, d)), _acc((d, d)), _acc((d, d))],
        (jax.ShapeDtypeStruct((s, d), F32), jax.ShapeDtypeStruct((m, 2 * d), F32),
         jax.ShapeDtypeStruct((1, d), F32), sq, sq, sq),
        [pltpu.VMEM((tm, d), BF16)] + [pltpu.VMEM((d, d), F32)] * 3,
        (x1, dx2, o, ycat, qs, probs, g2, w_q, kv, w_o), ("arbitrary",), hosted)


def _kv_bwd(dkv, mem, memn, g_mem, w_kv):
    m, d = mem.shape
    ns = w_kv.shape[2]

    def body(dkv_ref, mem_ref, memn_ref, g_ref, w_ref, gw_ref, dg_ref):
        _, xh, _ = _rms_fwd(mem_ref[...], g_ref[...])
        dmemn = jnp.zeros((m, d), F32)
        for k in range(N_CHIPS):
            dkb = dkv_ref[:, k * ns:(k + 1) * ns].astype(BF16)
            gw_ref[k] = _dot_tn(memn_ref[...], dkb).astype(BF16)
            dmemn = dmemn + _dot_nt(dkb, w_ref[k])
        dg_ref[...] = jnp.sum(dmemn * xh, axis=0, keepdims=True)

    return pl.pallas_call(
        body, name="kv_bwd",
        out_shape=(jax.ShapeDtypeStruct((N_CHIPS, d, ns), BF16), jax.ShapeDtypeStruct((1, d), F32)),
        compiler_params=pltpu.CompilerParams(vmem_limit_bytes=VMEM_LIMIT),
    )(dkv, mem, memn, g_mem, w_kv)


def _mixer_bwd(x, dx1, h, mixed_all, th_all, g1, lng, lnb, wsp, b_t, cw, ga, gb, w_out, w_in, hosted=None):
    s, d = x.shape
    n = s // TM
    nch = TM // CHUNK
    ns = w_in.shape[2]
    nh = N_CHIPS * ns
    aw = d // 2
    hd_w = aw // HEADS

    def rev(cols):
        return pl.BlockSpec((TM, cols), lambda i: (n - 1 - i, 0))

    hprev = pl.BlockSpec((8, nh), lambda i: (jnp.maximum((n - 1 - i) * (TM // 8) - 1, 0), 0))

    def body(x_ref, dx1_ref, h_ref, hp_ref, mix_ref, th_ref, g1_ref, lng_ref, lnb_ref, wsp_ref, bt_ref, cw_ref,
             ga_ref, gb_ref, wout_ref, win_ref,
             dx_ref, dh_ref, dg1_ref, dlng_ref, dlnb_ref, dwsp_ref, dbt_ref, dcw_ref, dga_ref, dgb_ref,
             dvn_ref, dcn_ref):
        i = pl.program_id(0)

        @pl.when(i == 0)
        def _():
            for r in (dg1_ref, dlng_ref, dlnb_ref, dwsp_ref, dbt_ref, dcw_ref, dga_ref, dgb_ref, dcn_ref):
                r[...] = jnp.zeros_like(r)

        dx1v = dx1_ref[...]
        dycat = _dot_nt(dx1v.astype(BF16), wout_ref[...])
        ha = h_ref[:, 0:2 * aw]
        th = th_ref[...]
        a = 0.5 * ha * (1.0 + th)
        u = a[:, :aw]
        lngv = lng_ref[...]
        vn, vhat, rs = _layer_norm_parts(a[:, aw:], lngv, lnb_ref[...])
        vnb = vn.astype(BF16)
        wm = _tril_weights(wsp_ref)
        mixed = mix_ref[...]
        gav = ga_ref[...]
        _, yah, ra = _rms_fwd(u * mixed, gav)
        dya, dga = _rms_bwd(dycat[:, :aw], yah, ra, gav)
        dga_ref[...] += dga
        du = dya * mixed
        dmix = dya * u
        dmb = dmix.astype(BF16)
        tri = lax.broadcasted_iota(jnp.int32, (CHUNK, CHUNK), 0) >= lax.broadcasted_iota(jnp.int32, (CHUNK, CHUNK), 1)
        for hd in range(HEADS):
            dw = jnp.zeros((CHUNK, CHUNK), F32)
            db = jnp.zeros((CHUNK, 1), F32)
            for c in range(nch):
                rows = slice(c * CHUNK, (c + 1) * CHUNK)
                cols = slice(hd * hd_w, (hd + 1) * hd_w)
                dvn_ref[rows, cols] = _dot_tn(wm[hd], dmb[rows, cols])
                dw = dw + _dot_nt(dmb[rows, cols], vnb[rows, cols])
                db = db + jnp.sum(dmix[rows, cols], axis=1, keepdims=True)
            dwsp_ref[hd] += jnp.where(tri, dw, 0.0)
            dbt_ref[:, hd:hd + 1] += db
        dvn = dvn_ref[...]
        dlng_ref[...] += jnp.sum(dvn * vhat, axis=0, keepdims=True)
        dlnb_ref[...] += jnp.sum(dvn, axis=0, keepdims=True)
        dvh = dvn * lngv
        dv = rs * (dvh - jnp.mean(dvh, axis=-1, keepdims=True) - vhat * jnp.mean(dvh * vhat, axis=-1, keepdims=True))
        gprime = 0.5 * (1.0 + th) + 0.5 * ha * (1.0 - th * th) * (GELU_C * (1.0 + 3.0 * GELU_K * (ha * ha)))
        dh_ref[:, 0:2 * aw] = (jnp.concatenate([du, dv], axis=-1) * gprime).astype(BF16)
        g_b = h_ref[:, 2 * aw:3 * aw]
        g_c = h_ref[:, 3 * aw:4 * aw]
        val = h_ref[:, 4 * aw:5 * aw]
        z = g_c * val
        zp = jnp.where(i == n - 1, 0.0, hp_ref[:, 3 * aw:4 * aw] * hp_ref[:, 4 * aw:5 * aw])
        zm1, zm2 = _shift_rows(z, zp)
        cw0, cw1, cw2 = cw_ref[0:1, :], cw_ref[1:2, :], cw_ref[2:3, :]
        conv = cw0 * zm2 + cw1 * zm1 + cw2 * z
        gbv = gb_ref[...]
        _, ybh, rb = _rms_fwd(g_b * conv, gbv)
        dyb, dgb = _rms_bwd(dycat[:, aw:], ybh, rb, gbv)
        dgb_ref[...] += dgb
        dconv = dyb * g_b
        dcw_ref[0:1, :] += jnp.sum(dconv * zm2, axis=0, keepdims=True)
        dcw_ref[1:2, :] += jnp.sum(dconv * zm1, axis=0, keepdims=True)
        dcw_ref[2:3, :] += jnp.sum(dconv * z, axis=0, keepdims=True)
        nxt = dcn_ref[...]
        row = lax.broadcasted_iota(jnp.int32, dconv.shape, 0)
        dcp1 = jnp.where(row == TM - 1, nxt[0:1, :], pltpu.roll(dconv, TM - 1, 0))
        dcp2 = jnp.where(row == TM - 1, nxt[1:2, :],
                         jnp.where(row == TM - 2, nxt[0:1, :], pltpu.roll(dconv, TM - 2, 0)))
        dz = cw2 * dconv + cw1 * dcp1 + cw0 * dcp2
        dcn_ref[...] = dconv[0:8, :]
        dh_ref[:, 2 * aw:3 * aw] = (dyb * conv).astype(BF16)
        dh_ref[:, 3 * aw:4 * aw] = (dz * val).astype(BF16)
        dh_ref[:, 4 * aw:5 * aw] = (dz * g_c).astype(BF16)
        dxn = jnp.zeros((TM, d), F32)
        for k in range(N_CHIPS):
            dxn = dxn + _dot_nt(dh_ref[:, k * ns:(k + 1) * ns], win_ref[k])
        g1v = g1_ref[...]
        _, xh1, r1 = _rms_fwd(x_ref[...], g1v)
        dxr, dg1 = _rms_bwd(dxn, xh1, r1, g1v)
        dg1_ref[...] += dg1
        dx_ref[...] = dx1v + dxr

    ins = (x, dx1, h, h, mixed_all, th_all, g1, lng, lnb, wsp, b_t, cw, ga, gb, w_out, w_in)
    acc_shapes = [(1, d), (1, aw), (1, aw), wsp.shape, (CHUNK, CHUNK), cw.shape, (1, aw), (1, aw)]
    return _host_call(
        body, "mixer_bwd", (n,),
        [rev(d), rev(d), rev(nh), hprev, rev(aw), rev(d)] + [_full(a.shape) for a in ins[6:]],
        [rev(d), rev(nh)] + [_acc(sh) for sh in acc_shapes],
        (jax.ShapeDtypeStruct((s, d), F32), jax.ShapeDtypeStruct((s, nh), BF16))
        + tuple(jax.ShapeDtypeStruct(sh, F32) for sh in acc_shapes),
        [pltpu.VMEM((TM, aw), F32), pltpu.VMEM((8, aw), F32)],
        ins, ("arbitrary",), hosted)


def _weight_grad(a, b, name, tm, tn, col_sharded, hosted=None):
    t, m = a.shape
    n = b.shape[1]

    def body(a_ref, b_ref, o_ref):
        o_ref[...] = _dot_tn(a_ref[...].astype(BF16), b_ref[...].astype(BF16)).astype(BF16)

    if col_sharded:
        ns = n // N_CHIPS
        per = ns // tn
        out_shape = jax.ShapeDtypeStruct((N_CHIPS, m, ns), BF16)
        out_spec = pl.BlockSpec((None, tm, tn), lambda i, j: (j // per, i, j % per))
    else:
        out_shape = jax.ShapeDtypeStruct((m, n), BF16)
        out_spec = pl.BlockSpec((tm, tn), lambda i, j: (i, j))
    (out,), extra = _host_call(
        body, name, (m // tm, n // tn),
        [pl.BlockSpec((t, tm), lambda i, j: (0, i)), pl.BlockSpec((t, tn), lambda i, j: (0, j))],
        [out_spec], (out_shape,), [], (a, b), ("parallel", "parallel"), hosted)
    return (out if col_sharded else out.reshape(N_CHIPS, m // N_CHIPS, n)), extra


def _row_tile(rows, cap=256):
    best = None
    for t in range(16, min(rows, cap) + 1, 16):
        if rows % t == 0:
            best = t
    return best if best is not None else rows


def _adamw_math(w, g, m, v):
    m2 = ADAM_B1 * m + (1.0 - ADAM_B1) * g
    v2 = ADAM_B2 * v + (1.0 - ADAM_B2) * (g * g)
    m_hat = m2 / (1.0 - ADAM_B1 ** ADAM_STEP)
    v_hat = v2 / (1.0 - ADAM_B2 ** ADAM_STEP)
    delta = -ADAM_LR * (m_hat / (jnp.sqrt(v_hat) + ADAM_EPS) + ADAM_WD * w)
    return delta, m2, v2


def _adamw(w, g, m, v, name):
    r, c = w.shape
    tr = _row_tile(r) if r >= 16 else r

    def body(w_ref, g_ref, m_ref, v_ref, d_ref, m2_ref, v2_ref):
        d_ref[...], m2_ref[...], v2_ref[...] = _adamw_math(w_ref[...], g_ref[...], m_ref[...], v_ref[...])

    sh = jax.ShapeDtypeStruct((r, c), F32)
    return pl.pallas_call(
        body, name=name, grid=(r // tr,),
        in_specs=[_rows(tr, c)] * 4, out_specs=[_rows(tr, c)] * 3, out_shape=(sh, sh, sh),
        compiler_params=_params("parallel"),
    )(w, g, m, v)


def _finalize(items, place, name, hosted=None):
    r, c = items[0][2].shape
    tr = _row_tile(r)
    nw = len(items)

    def body(place_ref, *refs):
        ins, outs = refs[:7 * nw], refs[7 * nw:]
        for k in range(nw):
            own_ref, s0_ref, s1_ref, s2_ref, w_ref, m_ref, v_ref = ins[7 * k:7 * k + 7]
            g_ref, d_ref, m2_ref, v2_ref = outs[4 * k:4 * k + 4]
            g = own_ref[...].astype(F32) + s0_ref[...].astype(F32)
            g = (g + s1_ref[...].astype(F32)) + s2_ref[...].astype(F32)
            g_ref[...] = g
            d_ref[...], m2_ref[...], v2_ref[...] = _adamw_math(w_ref[...], g, m_ref[...], v_ref[...])

    def slot(k):
        return pl.BlockSpec((None, tr, c), lambda i, pref: (k, i, 0))

    rows = pl.BlockSpec((tr, c), lambda i, pref: (i, 0))
    sh = jax.ShapeDtypeStruct((r, c), F32)
    one = [pl.BlockSpec((None, tr, c), lambda i, pref: (pref[1], i, 0)), slot(0), slot(1), slot(2), rows, rows, rows]
    args = [a for part, slots, w, m, v in items for a in (part, slots, slots, slots, w, m, v)]
    res, extra = _host_call(body, name, (r // tr,), one * nw, [rows] * (4 * nw), (sh,) * (4 * nw), [], args,
                            ("parallel",), hosted, prefetch=(place,))
    return [res[4 * k:4 * k + 4] for k in range(nw)], extra


def _small_sum_adamw(parts, w, m, v):
    nd, r, c = parts.shape

    def body(p_ref, w_ref, m_ref, v_ref, g_ref, d_ref, m2_ref, v2_ref):
        g = p_ref[0]
        for k in range(1, nd):
            g = g + p_ref[k]
        g_ref[...] = g
        d_ref[...], m2_ref[...], v2_ref[...] = _adamw_math(w_ref[...], g, m_ref[...], v_ref[...])

    sh = jax.ShapeDtypeStruct((r, c), F32)
    return pl.pallas_call(
        body, name="small_sum_adamw", out_shape=(sh, sh, sh, sh),
        compiler_params=pltpu.CompilerParams(vmem_limit_bytes=VMEM_LIMIT),
    )(parts, w, m, v)


def _place():
    x, y, c = lax.axis_index("x"), lax.axis_index("y"), lax.axis_index("c")
    chips = [(1 - x, y), (x, 1 - y), (1 - x, 1 - y)]
    return x, y, c, 2 * x + y, chips


def _remote(src, dst, send_sem, recv_sem, to):
    return pltpu.make_async_remote_copy(src_ref=src, dst_ref=dst, send_sem=send_sem, recv_sem=recv_sem,
                                        device_id=to, device_id_type=MESH)


class _Exchange:
    def __init__(self, ins, out_shapes, sem_shape, start, finish, middle=None, in_place=False):
        self.ins, self.out_shapes, self.sem_shape = tuple(ins), tuple(out_shapes), sem_shape
        self.start, self.finish, self.middle = start, finish, middle
        self.in_place = in_place


def _run_exchange(ex, name):
    n_in, n_out = len(ex.ins), len(ex.out_shapes)

    def body(*refs):
        ins, outs = refs[:n_in], refs[n_in:n_in + n_out]
        send_sems, recv_sems = refs[n_in + n_out:]
        ex.start(ins, outs, send_sems, recv_sems)
        if ex.middle is not None:
            ex.middle(ins, outs, send_sems, recv_sems)
        ex.finish(ins, outs, send_sems, recv_sems)

    sem = pltpu.SemaphoreType.DMA(ex.sem_shape)
    return pl.pallas_call(
        body, name=name, out_shape=ex.out_shapes, in_specs=[ANY] * n_in, out_specs=[ANY] * n_out,
        input_output_aliases={k: k for k in range(n_in)} if ex.in_place else {}, scratch_shapes=[sem, sem],
    )(*ex.ins)


def _host_call(body, name, grid, in_specs, out_specs, out_shape, scratch_shapes, args, semantics, hosted,
               prefetch=()):
    hosted = [] if hosted is None else (list(hosted) if isinstance(hosted, (list, tuple)) else [hosted])
    n_pre, n_in, n_out, n_scr = len(prefetch), len(in_specs), len(out_specs), len(scratch_shapes)
    h_ins = [a for ex in hosted for a in ex.ins]
    h_outs = [s for ex in hosted for s in ex.out_shapes]
    h_in, h_out = len(h_ins), len(h_outs)

    def wrapped(*refs):
        pre, refs = refs[:n_pre], refs[n_pre:]
        a, hi = refs[:n_in], refs[n_in:n_in + h_in]
        o = refs[n_in + h_in:n_in + h_in + n_out]
        ho = refs[n_in + h_in + n_out:n_in + h_in + n_out + h_out]
        scr = refs[n_in + h_in + n_out + h_out:]

        def run(phase):
            i0 = o0 = 0
            for k, ex in enumerate(hosted):
                fn = getattr(ex, phase)
                if fn is not None:
                    fn(hi[i0:i0 + len(ex.ins)], ho[o0:o0 + len(ex.out_shapes)], scr[n_scr + 2 * k],
                       scr[n_scr + 2 * k + 1])
                i0, o0 = i0 + len(ex.ins), o0 + len(ex.out_shapes)

        if hosted:
            first = functools.reduce(jnp.logical_and, [pl.program_id(k) == 0 for k in range(len(grid))])

            @pl.when(first)
            def _():
                run("start")

        if any(ex.middle is not None for ex in hosted):
            half_way = functools.reduce(jnp.logical_and, [
                pl.program_id(0) == max(1, grid[0] * MIDDLE_STEP_16THS // 16)] + [
                pl.program_id(k) == 0 for k in range(1, len(grid))])

            @pl.when(half_way)
            def _():
                run("middle")

        body(*pre, *a, *o, *scr[:n_scr])

        if hosted:
            last = functools.reduce(jnp.logical_and, [pl.program_id(k) == grid[k] - 1 for k in range(len(grid))])

            @pl.when(last)
            def _():
                run("finish")

    sems = [pltpu.SemaphoreType.DMA(ex.sem_shape) for ex in hosted for _ in range(2)]
    aliases, i0, o0 = {}, n_pre + n_in, n_out
    for ex in hosted:
        if ex.in_place:
            aliases.update({i0 + k: o0 + k for k in range(len(ex.ins))})
        i0, o0 = i0 + len(ex.ins), o0 + len(ex.out_shapes)
    all_in, all_out = list(in_specs) + [ANY] * h_in, list(out_specs) + [ANY] * h_out
    all_scr = list(scratch_shapes) + sems
    params = _params(*(["arbitrary"] * len(grid) if hosted else semantics))
    shapes = tuple(out_shape) + tuple(h_outs)
    if n_pre:
        call = pl.pallas_call(
            wrapped, name=name, out_shape=shapes, input_output_aliases=aliases, compiler_params=params,
            grid_spec=pltpu.PrefetchScalarGridSpec(num_scalar_prefetch=n_pre, grid=grid, in_specs=all_in,
                                                   out_specs=all_out, scratch_shapes=all_scr))
    else:
        call = pl.pallas_call(
            wrapped, name=name, grid=grid, in_specs=all_in, out_specs=all_out, out_shape=shapes,
            scratch_shapes=all_scr, input_output_aliases=aliases, compiler_params=params)
    res = call(*prefetch, *args, *h_ins)
    return res[:n_out], res[n_out:]


def _all_gather(shards, small=()):
    items = tuple(shards) + tuple(small)
    nw = len(shards)

    def place():
        x, y, c, me, _ = _place()
        first = (x + (1 - c) * (1 - 2 * x), y + c * (1 - 2 * y))
        second = (x + c * (1 - 2 * x), y + (1 - c) * (1 - 2 * y))
        diag = (1 - x, 1 - y)
        return x, y, c, me, (first, second, diag)

    def halves(w, c):
        rh = items[w].shape[0] // 2
        return pl.ds(c * rh, rh), pl.ds((1 - c) * rh, rh)

    def start(ins, outs, ss, rs):
        x, y, c, me, chips = place()
        for w in range(len(items)):
            _remote(ins[w], outs[w].at[me], ss.at[w, 6], rs.at[w, 6], (x, y, 1 - c)).start()
            if w < nw:
                mine, _ = halves(w, c)
                _remote(ins[w].at[mine], outs[w].at[me, mine], ss.at[w, 0], rs.at[w, 0], (*chips[0], c)).start()
            else:
                for k in range(3):
                    _remote(ins[w], outs[w].at[me], ss.at[w, k], rs.at[w, k], (*chips[k], c)).start()

    def onward(outs, ss, rs, w, k, x, y, c, chips):
        mine, _ = halves(w, c)
        pk = 2 * chips[k][0] + chips[k][1]
        got = outs[w].at[pk, mine]
        src = chips[1] if k == 2 else chips[k]
        _remote(got, got, ss.at[w, k], rs.at[w, k], (*src, c)).wait_recv()
        if k == 0:
            _remote(got, got, ss.at[w, 2], rs.at[w, 2], (*chips[1], c)).start()
        _remote(got, got, ss.at[w, 3 + k], rs.at[w, 3 + k], (x, y, 1 - c)).start()

    def middle(ins, outs, ss, rs):
        x, y, c, me, chips = place()
        for w in range(nw):
            mine, _ = halves(w, c)
            _remote(ins[w].at[mine], outs[w].at[me, mine], ss.at[w, 1], rs.at[w, 1], (*chips[1], c)).start()
        for w in range(nw):
            onward(outs, ss, rs, w, 0, x, y, c, chips)

    def finish(ins, outs, ss, rs):
        x, y, c, me, chips = place()
        sib = (x, y, 1 - c)
        for k in (1, 2):
            for w in range(nw):
                onward(outs, ss, rs, w, k, x, y, c, chips)
        for w in range(len(items)):
            if w < nw:
                mine, theirs = halves(w, c)
                for k, chip in ((3, chips[1]), (4, chips[0]), (5, chips[2])):
                    oth = outs[w].at[2 * chip[0] + chip[1], theirs]
                    _remote(oth, oth, ss.at[w, k], rs.at[w, k], sib).wait_recv()
                own = ins[w].at[mine]
                for k in range(6):
                    _remote(own, own, ss.at[w, k], rs.at[w, k], sib).wait_send()
            else:
                for k in range(3):
                    got = outs[w].at[2 * chips[k][0] + chips[k][1]]
                    _remote(got, got, ss.at[w, k], rs.at[w, k], (*chips[k], c)).wait_recv()
                    _remote(ins[w], ins[w], ss.at[w, k], rs.at[w, k], sib).wait_send()
            _remote(ins[w], outs[w].at[me], ss.at[w, 6], rs.at[w, 6], sib).wait()

    out_shapes = tuple(jax.ShapeDtypeStruct((N_CHIPS,) + a.shape, a.dtype) for a in items)
    return _Exchange(items, out_shapes, (len(items), 7), start, finish, middle if nw else None)


def _chip_reduce(grads, name):
    nw = len(grads)
    step = 64

    def body(*refs):
        ins, outs = refs[:nw], refs[nw:2 * nw]
        own, got = refs[2 * nw:3 * nw], refs[3 * nw:4 * nw]
        send_sems, recv_sems, local_sems = refs[4 * nw:]
        x, y, c, _, _ = _place()
        moves = []
        for w in range(nw):
            rh = grads[w].shape[1] // 2
            away = _remote(ins[w].at[:, pl.ds((1 - c) * rh, rh), :], got[w], send_sems.at[w], recv_sems.at[w],
                           (x, y, 1 - c))
            mine = pltpu.make_async_copy(ins[w].at[:, pl.ds(c * rh, rh), :], own[w], local_sems.at[w])
            away.start()
            mine.start()
            moves.append((away, mine))
        back = []
        for w, (away, mine) in enumerate(moves):
            nb, rh, _ = own[w].shape
            mine.wait()
            away.wait()
            for k in range(nb):
                def add(i, carry, w=w, k=k):
                    rows = pl.ds(pl.multiple_of(i * step, step), step)
                    own[w][k, rows, :] = (own[w][k, rows, :].astype(F32) + got[w][k, rows, :].astype(F32)).astype(BF16)
                    return carry
                lax.fori_loop(0, rh // step, add, 0)
                tail = rh % step
                if tail:
                    rows = slice(rh - tail, rh)
                    own[w][k, rows, :] = (own[w][k, rows, :].astype(F32) + got[w][k, rows, :].astype(F32)).astype(BF16)
            wb = pltpu.make_async_copy(own[w], outs[w].at[:, pl.ds(c * rh, rh), :], local_sems.at[w])
            wb.start()
            back.append(wb)
        for wb in back:
            wb.wait()

    halves = [pltpu.VMEM((g.shape[0], g.shape[1] // 2, g.shape[2]), BF16) for g in grads]
    sem = pltpu.SemaphoreType.DMA((nw,))
    return pl.pallas_call(
        body, name=name, out_shape=tuple(jax.ShapeDtypeStruct(g.shape, BF16) for g in grads),
        in_specs=[ANY] * nw, out_specs=[ANY] * nw, scratch_shapes=halves + halves + [sem, sem, sem],
        compiler_params=pltpu.CompilerParams(vmem_limit_bytes=VMEM_LIMIT),
    )(*grads)


def _scatter_partials(parts):
    nw = len(parts)

    def copies(ins, outs, ss, rs):
        _, _, c, _, chips = _place()
        res = []
        for r, (px, py) in enumerate(chips):
            for w in range(nw):
                rh = parts[w].shape[1] // 2
                rows = pl.ds(c * rh, rh)
                res.append(_remote(ins[w].at[2 * px + py, rows], outs[w].at[r, rows], ss.at[w, r], rs.at[w, r],
                                   (px, py, c)))
        return res

    def start(ins, outs, ss, rs):
        for cp in copies(ins, outs, ss, rs):
            cp.start()

    def finish(ins, outs, ss, rs):
        for cp in copies(ins, outs, ss, rs):
            cp.wait()

    out_shapes = tuple(jax.ShapeDtypeStruct((3,) + p.shape[1:], p.dtype) for p in parts)
    return _Exchange(parts, out_shapes, (nw, 3), start, finish)


def _join_partials(parts, slots):
    nw = len(parts)

    def copies(outs, ss, rs, mine):
        x, y, c, me, _ = _place()
        res = []
        for w in range(nw):
            rh = parts[w].shape[1] // 2
            rows = pl.ds((c if mine else 1 - c) * rh, rh)
            own = outs[w].at[me, rows]
            got = outs[nw + w].at[:, rows, :]
            res.append(_remote(own, own, ss.at[w, 0], rs.at[w, 0], (x, y, 1 - c)))
            res.append(_remote(got, got, ss.at[w, 1], rs.at[w, 1], (x, y, 1 - c)))
        return res

    def start(ins, outs, ss, rs):
        for cp in copies(outs, ss, rs, True):
            cp.start()

    def finish(ins, outs, ss, rs):
        for cp in copies(outs, ss, rs, True):
            cp.wait_send()
        for cp in copies(outs, ss, rs, False):
            cp.wait_recv()

    arrays = tuple(parts) + tuple(slots)
    return _Exchange(arrays, tuple(jax.ShapeDtypeStruct(a.shape, a.dtype) for a in arrays), (nw, 2), start, finish,
                     in_place=True)


def _gather_small(slab):
    def start(ins, outs, ss, rs):
        x, y, c, _, chips = _place()
        me = 4 * x + 2 * y + c
        pltpu.make_async_copy(ins[0], outs[0].at[me], ss.at[7]).start()
        _remote(ins[0], outs[0].at[me], ss.at[0], rs.at[0], (x, y, 1 - c)).start()
        for r, (px, py) in enumerate(chips):
            _remote(ins[0], outs[0].at[me], ss.at[1 + r], rs.at[1 + r], (px, py, c)).start()

    def finish(ins, outs, ss, rs):
        x, y, c, _, chips = _place()
        me, sib = 4 * x + 2 * y + c, (x, y, 1 - c)
        for r, (px, py) in enumerate(chips):
            got = outs[0].at[4 * px + 2 * py + c]
            _remote(got, got, ss.at[1 + r], rs.at[1 + r], (px, py, c)).wait_recv()
            _remote(got, got, ss.at[4 + r], rs.at[4 + r], sib).start()
        theirs = outs[0].at[4 * x + 2 * y + (1 - c)]
        _remote(theirs, theirs, ss.at[0], rs.at[0], sib).wait_recv()
        for r, (px, py) in enumerate(chips):
            got = outs[0].at[4 * px + 2 * py + (1 - c)]
            _remote(got, got, ss.at[4 + r], rs.at[4 + r], sib).wait_recv()
        for k in range(7):
            _remote(ins[0], ins[0], ss.at[k], rs.at[k], sib).wait_send()
        pltpu.make_async_copy(ins[0], outs[0].at[me], ss.at[7]).wait()

    return _Exchange((slab,), (jax.ShapeDtypeStruct((8,) + slab.shape, slab.dtype),), (8,), start, finish)


_SMALL_VECS = ("ln_mix_g", "ln_attn_g", "ln_mem_g", "ln_ffn_g", "ln_final_g")


def _pack_small(p, extra, conv):
    d = p["ln_mix_g"].shape[-1]
    top = [p[k].reshape(1, d) for k in _SMALL_VECS]
    top.append(jnp.concatenate([p["sgu_ln_g"].reshape(-1), p["sgu_ln_b"].reshape(-1)]).reshape(1, d))
    top.append(jnp.concatenate([p["grp_norm_a"].reshape(-1), p["grp_norm_b"].reshape(-1)]).reshape(1, d))
    top.append(jnp.concatenate([p["b_spatial"].reshape(-1), extra]).reshape(1, d))
    mid = jnp.zeros((8, d), F32)
    if conv is not None:
        mid = jnp.pad(conv, ((0, 5), (0, d - conv.shape[1])))
    return jnp.concatenate([jnp.concatenate(top, axis=0), mid, p["w_spatial"].reshape(-1, d)], axis=0)


def _unpack_small(slab):
    d = slab.shape[1]
    hw = d // 2
    out = {k: slab[i] for i, k in enumerate(_SMALL_VECS)}
    out["sgu_ln_g"], out["sgu_ln_b"] = slab[5, :hw], slab[5, hw:]
    out["grp_norm_a"], out["grp_norm_b"] = slab[6, :hw], slab[6, hw:]
    out["b_spatial"] = slab[7, :hw].reshape(HEADS, CHUNK)
    out["w_spatial"] = slab[16:].reshape(HEADS, CHUNK, CHUNK)
    return out


_BIG = ("w_in", "w_kv", "w_gate_up", "w_out", "w_q", "w_o", "w_down")
_WEIGHTS = ("ln_mix_g", "w_in", "sgu_ln_g", "sgu_ln_b", "w_spatial", "b_spatial", "conv_w", "grp_norm_a",
            "grp_norm_b", "w_out", "ln_attn_g", "ln_mem_g", "w_q", "w_kv", "w_o", "ln_ffn_g", "w_gate_up",
            "w_down", "ln_final_g")


def _step(p, m_, v_, x, mem, target):
    s, d = x.shape
    hw = d // 2
    row = lambda a: a.reshape(1, -1)
    x_, y_, c_ = lax.axis_index("x"), lax.axis_index("y"), lax.axis_index("c")
    chip = 2 * x_ + y_

    bf = {k: p[k].astype(BF16) for k in _BIG}
    conv8 = jnp.pad(p["conv_w"], ((0, 5), (0, 0)))
    w_in, w_out4, conv4 = _run_exchange(_all_gather([bf["w_in"], bf["w_out"]], [conv8]), "all_gather_mixer")
    cw = jnp.transpose(conv4[:, :3, :], (1, 0, 2)).reshape(3, hw)
    b_t = jnp.pad(jnp.transpose(p["b_spatial"]), ((0, 0), (0, CHUNK - HEADS)))
    g1, g2, gm, g3, gf = (row(p[k]) for k in _SMALL_VECS)
    lng, lnb, ga, gb = row(p["sgu_ln_g"]), row(p["sgu_ln_b"]), row(p["grp_norm_a"]), row(p["grp_norm_b"])
    wsp = p["w_spatial"]
    w_out = w_out4.reshape(-1, d)

    (h, x1, ycat, xn1, mixed, th), (w_kv, w_q4, w_o4, w_down4) = _mixer_fwd(
        x, g1, w_in, lng, lnb, wsp, b_t, cw, ga, gb, w_out,
        hosted=_all_gather([bf[k] for k in ("w_kv", "w_q", "w_o", "w_down")]))
    w_q, w_o, w_down = (a.reshape(-1, d) for a in (w_q4, w_o4, w_down4))
    memn, kv = _kv_fwd(mem, gm, w_kv)
    (x2, o, qs, probs), (w_gu,) = _attn_fwd(x1, g2, w_q, kv, w_o, hosted=_all_gather([bf["w_gate_up"]]))
    dx2, act, dgu, xn3, dx3, loss, dgf, dg3 = _ffn_fwd_bwd(x2, g3, gf, target, w_gu, w_down)

    place = jnp.stack([c_, chip]).astype(jnp.int32)

    def chip_partials(names, grads, tag):
        return list(_chip_reduce(grads, "chip_reduce_" + tag))

    names_d = ("w_down",)
    parts_d = chip_partials(names_d, (_weight_grad(act, dx3, "grad_w_down", 1408, 512, False)[0],), "down")
    g_gu, slots_d = _weight_grad(xn3, dgu, "grad_w_gate_up", 512, 1408, True, hosted=_scatter_partials(parts_d))
    names_a = ("w_gate_up",)
    parts_a = chip_partials(names_a, (g_gu,), "ffn")
    (dx1, dkv, dg2, g_o, g_q, g_out), slots_a = _attn_bwd(x1, dx2, o, ycat, qs, probs, g2, w_q, kv, w_o,
                                                           hosted=_scatter_partials(parts_a))
    g_kv, dgm = _kv_bwd(dkv, mem, memn, gm, w_kv)
    names_b = ("w_o", "w_out", "w_q", "w_kv")
    shard_major = lambda g: g.reshape(N_CHIPS, -1, d)
    parts_b = chip_partials(names_b, (shard_major(g_o), shard_major(g_out), shard_major(g_q), g_kv), "attn")
    (dx, dh, dg1, dlng, dlnb, dwsp, dbt, dcw, dga, dgb), slots_b = _mixer_bwd(
        x, dx1, h, mixed, th, g1, lng, lnb, wsp, b_t, cw, ga, gb, w_out, w_in, hosted=_scatter_partials(parts_b))
    small = {"ln_mix_g": dg1, "ln_attn_g": dg2, "ln_mem_g": dgm, "ln_ffn_g": dg3, "ln_final_g": dgf,
             "sgu_ln_g": dlng, "sgu_ln_b": dlnb, "grp_norm_a": dga, "grp_norm_b": dgb,
             "b_spatial": jnp.transpose(dbt[:, :HEADS]), "w_spatial": dwsp}
    loss_vec = jnp.pad(loss.reshape(1), (0, hw - 1))
    names = names_d + names_a + names_b
    n_done = len(names)
    g_in, extra = _weight_grad(
        xn1, dh, "grad_w_in", 1024, 640, True,
        hosted=[_gather_small(_pack_small(small, loss_vec, dcw)),
                _join_partials(parts_d + parts_a + parts_b, slots_d + slots_a + slots_b)])
    parts, whole = extra[0], dict(zip(names, zip(extra[1:1 + n_done], extra[1 + n_done:])))
    (part_in,) = chip_partials(("w_in",), (g_in,), "mixer")
    out_g, out_d, out_m, out_v = {}, {}, {}, {}

    def finalize(ks, tag, hosted=None):
        done, res = _finalize([(whole[k][0], whole[k][1], p[k], m_[k], v_[k]) for k in ks], place,
                              "finalize_" + tag, hosted)
        for k, (g, dl, nm, nv) in zip(ks, done):
            out_g[k], out_d[k], out_m[k], out_v[k] = g, dl, nm, nv
        return res

    (slots_in,) = finalize(("w_o", "w_out", "w_q"), "attn", _scatter_partials([part_in]))
    for k in ("w_down", "w_gate_up", "w_kv"):
        finalize((k,), k)
    whole["w_in"] = _run_exchange(_join_partials([part_in], [slots_in]), "rs_join_mixer")
    finalize(("w_in",), "w_in")

    zeros = jnp.zeros((hw,), F32)
    sg, sd, sm, sv = _small_sum_adamw(parts, _pack_small(p, zeros, None), _pack_small(m_, zeros, None),
                                      _pack_small(v_, zeros, None))
    for tree, slab in zip((out_g, out_d, out_m, out_v), (sg, sd, sm, sv)):
        tree.update(_unpack_small(slab))
    loss_out = sg[7, hw]
    g_conv = lax.dynamic_slice(sg[8:11, :hw], (0, chip * (hw // N_CHIPS)), (3, hw // N_CHIPS))
    out_g["conv_w"] = g_conv
    out_d["conv_w"], out_m["conv_w"], out_v["conv_w"] = _adamw(p["conv_w"], g_conv, m_["conv_w"], v_["conv_w"],
                                                                "adamw_conv_w")
    return loss_out, dx, out_g, out_d, out_m, out_v


def kernel(x, mem, ln_mix_g, w_in, sgu_ln_g, sgu_ln_b, w_spatial, b_spatial, conv_w, grp_norm_a, grp_norm_b, w_out, ln_attn_g, ln_mem_g, w_q, w_kv, w_o, ln_ffn_g, w_gate_up, w_down, ln_final_g, loss_target, m_ln_mix_g, m_w_in, m_sgu_ln_g, m_sgu_ln_b, m_w_spatial, m_b_spatial, m_conv_w, m_grp_norm_a, m_grp_norm_b, m_w_out, m_ln_attn_g, m_ln_mem_g, m_w_q, m_w_kv, m_w_o, m_ln_ffn_g, m_w_gate_up, m_w_down, m_ln_final_g, v_ln_mix_g, v_w_in, v_sgu_ln_g, v_sgu_ln_b, v_w_spatial, v_b_spatial, v_conv_w, v_grp_norm_a, v_grp_norm_b, v_w_out, v_ln_attn_g, v_ln_mem_g, v_w_q, v_w_kv, v_w_o, v_ln_ffn_g, v_w_gate_up, v_w_down, v_ln_final_g):
    p = dict(ln_mix_g=ln_mix_g, w_in=w_in, sgu_ln_g=sgu_ln_g, sgu_ln_b=sgu_ln_b, w_spatial=w_spatial,
             b_spatial=b_spatial, conv_w=conv_w, grp_norm_a=grp_norm_a, grp_norm_b=grp_norm_b, w_out=w_out,
             ln_attn_g=ln_attn_g, ln_mem_g=ln_mem_g, w_q=w_q, w_kv=w_kv, w_o=w_o, ln_ffn_g=ln_ffn_g,
             w_gate_up=w_gate_up, w_down=w_down, ln_final_g=ln_final_g)
    m_ = dict(ln_mix_g=m_ln_mix_g, w_in=m_w_in, sgu_ln_g=m_sgu_ln_g, sgu_ln_b=m_sgu_ln_b, w_spatial=m_w_spatial,
              b_spatial=m_b_spatial, conv_w=m_conv_w, grp_norm_a=m_grp_norm_a, grp_norm_b=m_grp_norm_b,
              w_out=m_w_out, ln_attn_g=m_ln_attn_g, ln_mem_g=m_ln_mem_g, w_q=m_w_q, w_kv=m_w_kv, w_o=m_w_o,
              ln_ffn_g=m_ln_ffn_g, w_gate_up=m_w_gate_up, w_down=m_w_down, ln_final_g=m_ln_final_g)
    v_ = dict(ln_mix_g=v_ln_mix_g, w_in=v_w_in, sgu_ln_g=v_sgu_ln_g, sgu_ln_b=v_sgu_ln_b, w_spatial=v_w_spatial,
              b_spatial=v_b_spatial, conv_w=v_conv_w, grp_norm_a=v_grp_norm_a, grp_norm_b=v_grp_norm_b,
              w_out=v_w_out, ln_attn_g=v_ln_attn_g, ln_mem_g=v_ln_mem_g, w_q=v_w_q, w_kv=v_w_kv, w_o=v_w_o,
              ln_ffn_g=v_ln_ffn_g, w_gate_up=v_w_gate_up, w_down=v_w_down, ln_final_g=v_ln_final_g)
    s, d = x.shape[-2], x.shape[-1]
    loss, dx, g, dl, nm, nv = _step(p, m_, v_, x.reshape(s, d), mem.reshape(-1, d), loss_target.reshape(s, d))
    outs = [loss, dx.reshape(x.shape)]
    for tree in (g, dl, nm, nv):
        outs += [tree[k].reshape(p[k].shape) for k in _WEIGHTS]
    return tuple(outs)
```

```python
import functools
import math

import jax
import jax.numpy as jnp
from jax import lax
from jax.experimental import pallas as pl
from jax.experimental.pallas import tpu as pltpu

F32 = jnp.float32
BF16 = jnp.bfloat16
EPS = 1e-6
CHUNK = 128
HEADS = 4
N_CHIPS = 4
TM = 512
TM_ATTN = 512
TM_FFN = 256
ADAM_LR, ADAM_B1, ADAM_B2, ADAM_EPS, ADAM_WD, ADAM_STEP = 0.001, 0.9, 0.999, 1e-08, 0.01, 10
GELU_C = math.sqrt(2.0 / math.pi)
GELU_K = 0.044715
SMALL_ROWS = 80
VMEM_LIMIT = 56 * 1024 * 1024
MIDDLE_STEP_16THS = 7
MESH = pl.DeviceIdType.MESH
ANY = pl.BlockSpec(memory_space=pl.ANY)


def _params(*sem):
    return pltpu.CompilerParams(dimension_semantics=sem, vmem_limit_bytes=VMEM_LIMIT)


def _dot(a, b):
    return jnp.dot(a, b, preferred_element_type=F32)


def _dot_nt(a, b):
    return lax.dot_general(a, b, (((1,), (1,)), ((), ())), preferred_element_type=F32)


def _dot_tn(a, b):
    return lax.dot_general(a, b, (((0,), (0,)), ((), ())), preferred_element_type=F32)


def _rms_fwd(x, g):
    r = lax.rsqrt(jnp.mean(x * x, axis=-1, keepdims=True) + EPS)
    xh = x * r
    return xh * g, xh, r


def _rms_bwd(dy, xh, r, g):
    dxh = dy * g
    dx = r * (dxh - xh * jnp.mean(dxh * xh, axis=-1, keepdims=True))
    return dx, jnp.sum(dy * xh, axis=0, keepdims=True)


def _full(shape):
    nd = len(shape)
    return pl.BlockSpec(shape, lambda *_: (0,) * nd, pipeline_mode=pl.Buffered(1))


def _acc(shape):
    nd = len(shape)
    return pl.BlockSpec(shape, lambda *_: (0,) * nd)


def _rows(tm, cols):
    return pl.BlockSpec((tm, cols), lambda i: (i, 0))


def _tril_weights(wsp_ref):
    row = lax.broadcasted_iota(jnp.int32, (CHUNK, CHUNK), 0)
    col = lax.broadcasted_iota(jnp.int32, (CHUNK, CHUNK), 1)
    return [jnp.where(row >= col, wsp_ref[hd], 0.0).astype(BF16) for hd in range(HEADS)]


def _shift_rows(z, zp):
    row = lax.broadcasted_iota(jnp.int32, z.shape, 0)
    zm1 = jnp.where(row == 0, zp[7:8, :], pltpu.roll(z, 1, 0))
    zm2 = jnp.where(row == 0, zp[6:7, :], jnp.where(row == 1, zp[7:8, :], pltpu.roll(z, 2, 0)))
    return zm1, zm2


def _gelu_parts(x):
    t = jnp.tanh(GELU_C * (x + GELU_K * (x * x * x)))
    return 0.5 * x * (1.0 + t), t


def _layer_norm_parts(v, g, b):
    mu = jnp.mean(v, axis=-1, keepdims=True)
    vc = v - mu
    rs = lax.rsqrt(jnp.mean(vc * vc, axis=-1, keepdims=True) + EPS)
    vhat = vc * rs
    return vhat * g + b, vhat, rs


def _kv_fwd(mem, g_mem, w_kv):
    m, d = mem.shape
    ns = w_kv.shape[2]

    def body(mem_ref, g_ref, w_ref, memn_ref, kv_ref):
        y, _, _ = _rms_fwd(mem_ref[...], g_ref[...])
        yb = y.astype(BF16)
        memn_ref[...] = yb
        for k in range(N_CHIPS):
            kv_ref[:, k * ns:(k + 1) * ns] = _dot(yb, w_ref[k]).astype(BF16)

    return pl.pallas_call(
        body, name="kv_fwd",
        out_shape=(jax.ShapeDtypeStruct((m, d), BF16), jax.ShapeDtypeStruct((m, N_CHIPS * ns), BF16)),
        compiler_params=pltpu.CompilerParams(vmem_limit_bytes=VMEM_LIMIT),
    )(mem, g_mem, w_kv)


def _mixer_fwd(x, g1, w_in, lng, lnb, wsp, b_t, cw, ga, gb, w_out, hosted=None):
    s, d = x.shape
    n = s // TM
    nch = TM // CHUNK
    ns = w_in.shape[2]
    nh = N_CHIPS * ns
    aw = d // 2
    hd_w = aw // HEADS

    def body(x_ref, g1_ref, win_ref, lng_ref, lnb_ref, wsp_ref, bt_ref, cw_ref, ga_ref, gb_ref, wout_ref,
             h_ref, x1_ref, y_ref, xn_ref, mix_ref, th_ref, zp_ref):
        i = pl.program_id(0)

        @pl.when(i == 0)
        def _():
            zp_ref[...] = jnp.zeros_like(zp_ref)

        x = x_ref[...]
        xn, _, _ = _rms_fwd(x, g1_ref[...])
        xnb = xn.astype(BF16)
        xn_ref[...] = xnb
        for k in range(N_CHIPS):
            h_ref[:, k * ns:(k + 1) * ns] = _dot(xnb, win_ref[k])
        a, th = _gelu_parts(h_ref[:, 0:2 * aw])
        th_ref[...] = th
        u = a[:, :aw]
        vn, _, _ = _layer_norm_parts(a[:, aw:], lng_ref[...], lnb_ref[...])
        vnb = vn.astype(BF16)
        wm = _tril_weights(wsp_ref)
        for c in range(nch):
            for hd in range(HEADS):
                blk = vnb[c * CHUNK:(c + 1) * CHUNK, hd * hd_w:(hd + 1) * hd_w]
                mix_ref[c * CHUNK:(c + 1) * CHUNK, hd * hd_w:(hd + 1) * hd_w] = _dot(wm[hd], blk) + bt_ref[:, hd:hd + 1]
        ya, _, _ = _rms_fwd(u * mix_ref[...], ga_ref[...])
        g_b = h_ref[:, 2 * aw:3 * aw]
        z = h_ref[:, 3 * aw:4 * aw] * h_ref[:, 4 * aw:5 * aw]
        zm1, zm2 = _shift_rows(z, zp_ref[...])
        conv = cw_ref[0:1, :] * zm2 + cw_ref[1:2, :] * zm1 + cw_ref[2:3, :] * z
        yb, _, _ = _rms_fwd(g_b * conv, gb_ref[...])
        zp_ref[...] = z[TM - 8:TM, :]
        ycat = jnp.concatenate([ya, yb], axis=-1).astype(BF16)
        y_ref[...] = ycat
        x1_ref[...] = x + _dot(ycat, wout_ref[...])

    return _host_call(
        body, "mixer_fwd", (n,),
        [_rows(TM, d), _full(g1.shape), _full(w_in.shape), _full(lng.shape), _full(lnb.shape),
         _full(wsp.shape), _full(b_t.shape), _full(cw.shape), _full(ga.shape), _full(gb.shape),
         _full(w_out.shape)],
        [_rows(TM, nh), _rows(TM, d), _rows(TM, d), _rows(TM, d), _rows(TM, aw), _rows(TM, d)],
        (jax.ShapeDtypeStruct((s, nh), F32), jax.ShapeDtypeStruct((s, d), F32),
         jax.ShapeDtypeStruct((s, d), BF16), jax.ShapeDtypeStruct((s, d), BF16),
         jax.ShapeDtypeStruct((s, aw), F32), jax.ShapeDtypeStruct((s, d), F32)),
        [pltpu.VMEM((8, aw), F32)],
        (x, g1, w_in, lng, lnb, wsp, b_t, cw, ga, gb, w_out), ("arbitrary",), hosted)


def _attn_fwd(x1, g2, w_q, kv, w_o, hosted=None):
    s, d = x1.shape
    tm = min(TM_ATTN, s)
    n = s // tm
    dh = d // HEADS
    m = kv.shape[0]
    scale = dh ** -0.5

    def body(x1_ref, g2_ref, wq_ref, kv_ref, wo_ref, x2_ref, o_ref, q_ref, p_ref):
        x1v = x1_ref[...]
        xn, _, _ = _rms_fwd(x1v, g2_ref[...])
        q_ref[...] = _dot(xn.astype(BF16), wq_ref[...]).astype(BF16)
        for hd in range(HEADS):
            kh = kv_ref[:, hd * dh:(hd + 1) * dh]
            vh = kv_ref[:, d + hd * dh:d + (hd + 1) * dh]
            sc = _dot_nt(q_ref[:, hd * dh:(hd + 1) * dh], kh) * scale
            e = jnp.exp(sc - jnp.max(sc, axis=-1, keepdims=True))
            p = e / jnp.sum(e, axis=-1, keepdims=True)
            p_ref[:, hd * m:(hd + 1) * m] = p
            o_ref[:, hd * dh:(hd + 1) * dh] = _dot(p.astype(BF16), vh).astype(BF16)
        x2_ref[...] = x1v + _dot(o_ref[...], wo_ref[...])

    return _host_call(
        body, "attn_fwd", (n,),
        [_rows(tm, d), _full(g2.shape), _full(w_q.shape), _full(kv.shape), _full(w_o.shape)],
        [_rows(tm, d), _rows(tm, d), _rows(tm, d), _rows(tm, HEADS * m)],
        (jax.ShapeDtypeStruct((s, d), F32), jax.ShapeDtypeStruct((s, d), BF16), jax.ShapeDtypeStruct((s, d), BF16),
         jax.ShapeDtypeStruct((s, HEADS * m), F32)),
        [], (x1, g2, w_q, kv, w_o), ("parallel",), hosted)


def _ffn_fwd_bwd(x2, g3, gf, target, w_gu, w_down):
    s, d = x2.shape
    tm = min(TM_FFN, s)
    n = s // tm
    ns = w_gu.shape[2]
    ff = 2 * ns

    def body(x2_ref, g3_ref, gf_ref, t_ref, wgu_ref, wd_ref,
             dx2_ref, act_ref, dgu_ref, xn_ref, dx3_ref, loss_ref, dgf_ref, dg3_ref):
        i = pl.program_id(0)

        @pl.when(i == 0)
        def _():
            loss_ref[...] = jnp.zeros_like(loss_ref)
            dgf_ref[...] = jnp.zeros_like(dgf_ref)
            dg3_ref[...] = jnp.zeros_like(dg3_ref)

        x2v = x2_ref[...]
        xn, xh3, r3 = _rms_fwd(x2v, g3_ref[...])
        xnb = xn.astype(BF16)
        xn_ref[...] = xnb
        x3 = x2v
        saved = []
        for j in range(2):
            g = _dot(xnb, wgu_ref[j])
            u = _dot(xnb, wgu_ref[2 + j])
            sg = 1.0 / (1.0 + jnp.exp(-g))
            sl = g * sg
            actb = (sl * u).astype(BF16)
            act_ref[:, j * ns:(j + 1) * ns] = actb
            x3 = x3 + _dot(actb, wd_ref[j * ns:(j + 1) * ns, :])
            saved.append((u, sl, sg * (1.0 + g * (1.0 - sg))))
        gfv = gf_ref[...]
        y, xhf, rf = _rms_fwd(x3, gfv)
        e = y - t_ref[...]
        loss_ref[...] += 0.5 * jnp.sum(jnp.sum(e * e, axis=-1, keepdims=True), axis=0, keepdims=True) / d
        dx3, dgf = _rms_bwd(e / d, xhf, rf, gfv)
        dgf_ref[...] += dgf
        dx3b = dx3.astype(BF16)
        dx3_ref[...] = dx3b
        dxn = jnp.zeros_like(x2v)
        for j in range(2):
            u, sl, dsl = saved[j]
            dact = _dot_nt(dx3b, wd_ref[j * ns:(j + 1) * ns, :])
            dgb = (dact * u * dsl).astype(BF16)
            dub = (dact * sl).astype(BF16)
            dgu_ref[:, j * ns:(j + 1) * ns] = dgb
            dgu_ref[:, ff + j * ns:ff + (j + 1) * ns] = dub
            dxn = dxn + _dot_nt(dgb, wgu_ref[j]) + _dot_nt(dub, wgu_ref[2 + j])
        dxr, dg3 = _rms_bwd(dxn, xh3, r3, g3_ref[...])
        dg3_ref[...] += dg3
        dx2_ref[...] = dx3 + dxr

    vec = jax.ShapeDtypeStruct((1, d), F32)
    return pl.pallas_call(
        body, name="ffn_fwd_bwd", grid=(n,),
        in_specs=[_rows(tm, d), _full(g3.shape), _full(gf.shape), _rows(tm, d), _full(w_gu.shape),
                  _full(w_down.shape)],
        out_specs=[_rows(tm, d), _rows(tm, ff), _rows(tm, 2 * ff), _rows(tm, d), _rows(tm, d),
                   _acc((1, 1)), _acc((1, d)), _acc((1, d))],
        out_shape=(jax.ShapeDtypeStruct((s, d), F32), jax.ShapeDtypeStruct((s, ff), BF16),
                   jax.ShapeDtypeStruct((s, 2 * ff), BF16), jax.ShapeDtypeStruct((s, d), BF16),
                   jax.ShapeDtypeStruct((s, d), BF16), jax.ShapeDtypeStruct((1, 1), F32), vec, vec),
        compiler_params=_params("arbitrary"),
    )(x2, g3, gf, target, w_gu, w_down)


def _attn_bwd(x1, dx2, o, ycat, qs, probs, g2, w_q, kv, w_o, hosted=None):
    s, d = x1.shape
    tm = min(TM_ATTN, s)
    n = s // tm
    dh = d // HEADS
    scale = dh ** -0.5
    m = kv.shape[0]

    def body(x1_ref, dx2_ref, o_ref, y_ref, q_ref, p_ref, g2_ref, wq_ref, kv_ref, wo_ref,
             dx1_ref, dkv_ref, dg2_ref, gwo_out, gwq_out, gwout_out, dq_ref, gwo_ref, gwq_ref, gwout_ref):
        i = pl.program_id(0)

        @pl.when(i == 0)
        def _():
            for r in (dkv_ref, dg2_ref, gwo_ref, gwq_ref, gwout_ref):
                r[...] = jnp.zeros_like(r)

        xn, xh2, r2 = _rms_fwd(x1_ref[...], g2_ref[...])
        xnb = xn.astype(BF16)
        dx2v = dx2_ref[...]
        dx2b = dx2v.astype(BF16)
        gwo_ref[...] += _dot_tn(o_ref[...], dx2b)
        do = _dot_nt(dx2b, wo_ref[...])
        for hd in range(HEADS):
            qb = q_ref[:, hd * dh:(hd + 1) * dh]
            p = p_ref[:, hd * m:(hd + 1) * m]
            kh = kv_ref[:, hd * dh:(hd + 1) * dh]
            vh = kv_ref[:, d + hd * dh:d + (hd + 1) * dh]
            dob = do[:, hd * dh:(hd + 1) * dh].astype(BF16)
            dp = _dot_nt(dob, vh)
            ds = p * (dp - jnp.sum(dp * p, axis=-1, keepdims=True))
            dsb = (ds * scale).astype(BF16)
            dq_ref[:, hd * dh:(hd + 1) * dh] = _dot(dsb, kh).astype(BF16)
            dkv_ref[:, hd * dh:(hd + 1) * dh] += _dot_tn(dsb, qb)
            dkv_ref[:, d + hd * dh:d + (hd + 1) * dh] += _dot_tn(p.astype(BF16), dob)
        dqb = dq_ref[...]
        gwq_ref[...] += _dot_tn(xnb, dqb)
        dxn = _dot_nt(dqb, wq_ref[...])
        dxr, dg2 = _rms_bwd(dxn, xh2, r2, g2_ref[...])
        dg2_ref[...] += dg2
        dx1 = dx2v + dxr
        dx1_ref[...] = dx1
        gwout_ref[...] += _dot_tn(y_ref[...], dx1.astype(BF16))

        @pl.when(i == n - 1)
        def _():
            for acc, out in ((gwo_ref, gwo_out), (gwq_ref, gwq_out), (gwout_ref, gwout_out)):
                out[...] = acc[...].astype(BF16)

    sq = jax.ShapeDtypeStruct((d, d), BF16)
    return _host_call(
        body, "attn_bwd", (n,),
        [_rows(tm, d), _rows(tm, d), _rows(tm, d), _rows(tm, d), _rows(tm, d), _rows(tm, HEADS * m),
         _full(g2.shape), _full(w_q.shape), _full(kv.shape), _full(w_o.shape)],
        [_rows(tm, d), _acc((m, 2 * d)), _acc((1, d)), _acc((d, d)), _acc((d, d)), _acc((d, d))],
        (jax.ShapeDtypeStruct((s, d), F32), jax.ShapeDtypeStruct((m, 2 * d), F32),
         jax.ShapeDtypeStruct((1, d), F32), sq, sq, sq),
        [pltpu.VMEM((tm, d), BF16)] + [pltpu.VMEM((d, d), F32)] * 3,
        (x1, dx2, o, ycat, qs, probs, g2, w_q, kv, w_o), ("arbitrary",), hosted)


def _kv_bwd(dkv, mem, memn, g_mem, w_kv):
    m, d = mem.shape
    ns = w_kv.shape[2]

    def body(dkv_ref, mem_ref, memn_ref, g_ref, w_ref, gw_ref, dg_ref):
        _, xh, _ = _rms_fwd(mem_ref[...], g_ref[...])
        dmemn = jnp.zeros((m, d), F32)
        for k in range(N_CHIPS):
            dkb = dkv_ref[:, k * ns:(k + 1) * ns].astype(BF16)
            gw_ref[k] = _dot_tn(memn_ref[...], dkb).astype(BF16)
            dmemn = dmemn + _dot_nt(dkb, w_ref[k])
        dg_ref[...] = jnp.sum(dmemn * xh, axis=0, keepdims=True)

    return pl.pallas_call(
        body, name="kv_bwd",
        out_shape=(jax.ShapeDtypeStruct((N_CHIPS, d, ns), BF16), jax.ShapeDtypeStruct((1, d), F32)),
        compiler_params=pltpu.CompilerParams(vmem_limit_bytes=VMEM_LIMIT),
    )(dkv, mem, memn, g_mem, w_kv)


def _mixer_bwd(x, dx1, h, mixed_all, th_all, g1, lng, lnb, wsp, b_t, cw, ga, gb, w_out, w_in, hosted=None):
    s, d = x.shape
    n = s // TM
    nch = TM // CHUNK
    ns = w_in.shape[2]
    nh = N_CHIPS * ns
    aw = d // 2
    hd_w = aw // HEADS

    def rev(cols):
        return pl.BlockSpec((TM, cols), lambda i: (n - 1 - i, 0))

    hprev = pl.BlockSpec((8, nh), lambda i: (jnp.maximum((n - 1 - i) * (TM // 8) - 1, 0), 0))

    def body(x_ref, dx1_ref, h_ref, hp_ref, mix_ref, th_ref, g1_ref, lng_ref, lnb_ref, wsp_ref, bt_ref, cw_ref,
             ga_ref, gb_ref, wout_ref, win_ref,
             dx_ref, dh_ref, dg1_ref, dlng_ref, dlnb_ref, dwsp_ref, dbt_ref, dcw_ref, dga_ref, dgb_ref,
             dvn_ref, dcn_ref):
        i = pl.program_id(0)

        @pl.when(i == 0)
        def _():
            for r in (dg1_ref, dlng_ref, dlnb_ref, dwsp_ref, dbt_ref, dcw_ref, dga_ref, dgb_ref, dcn_ref):
                r[...] = jnp.zeros_like(r)

        dx1v = dx1_ref[...]
        dycat = _dot_nt(dx1v.astype(BF16), wout_ref[...])
        ha = h_ref[:, 0:2 * aw]
        th = th_ref[...]
        a = 0.5 * ha * (1.0 + th)
        u = a[:, :aw]
        lngv = lng_ref[...]
        vn, vhat, rs = _layer_norm_parts(a[:, aw:], lngv, lnb_ref[...])
        vnb = vn.astype(BF16)
        wm = _tril_weights(wsp_ref)
        mixed = mix_ref[...]
        gav = ga_ref[...]
        _, yah, ra = _rms_fwd(u * mixed, gav)
        dya, dga = _rms_bwd(dycat[:, :aw], yah, ra, gav)
        dga_ref[...] += dga
        du = dya * mixed
        dmix = dya * u
        dmb = dmix.astype(BF16)
        tri = lax.broadcasted_iota(jnp.int32, (CHUNK, CHUNK), 0) >= lax.broadcasted_iota(jnp.int32, (CHUNK, CHUNK), 1)
        for hd in range(HEADS):
            dw = jnp.zeros((CHUNK, CHUNK), F32)
            db = jnp.zeros((CHUNK, 1), F32)
            for c in range(nch):
                rows = slice(c * CHUNK, (c + 1) * CHUNK)
                cols = slice(hd * hd_w, (hd + 1) * hd_w)
                dvn_ref[rows, cols] = _dot_tn(wm[hd], dmb[rows, cols])
                dw = dw + _dot_nt(dmb[rows, cols], vnb[rows, cols])
                db = db + jnp.sum(dmix[rows, cols], axis=1, keepdims=True)
            dwsp_ref[hd] += jnp.where(tri, dw, 0.0)
            dbt_ref[:, hd:hd + 1] += db
        dvn = dvn_ref[...]
        dlng_ref[...] += jnp.sum(dvn * vhat, axis=0, keepdims=True)
        dlnb_ref[...] += jnp.sum(dvn, axis=0, keepdims=True)
        dvh = dvn * lngv
        dv = rs * (dvh - jnp.mean(dvh, axis=-1, keepdims=True) - vhat * jnp.mean(dvh * vhat, axis=-1, keepdims=True))
        gprime = 0.5 * (1.0 + th) + 0.5 * ha * (1.0 - th * th) * (GELU_C * (1.0 + 3.0 * GELU_K * (ha * ha)))
        dh_ref[:, 0:2 * aw] = (jnp.concatenate([du, dv], axis=-1) * gprime).astype(BF16)
        g_b = h_ref[:, 2 * aw:3 * aw]
        g_c = h_ref[:, 3 * aw:4 * aw]
        val = h_ref[:, 4 * aw:5 * aw]
        z = g_c * val
        zp = jnp.where(i == n - 1, 0.0, hp_ref[:, 3 * aw:4 * aw] * hp_ref[:, 4 * aw:5 * aw])
        zm1, zm2 = _shift_rows(z, zp)
        cw0, cw1, cw2 = cw_ref[0:1, :], cw_ref[1:2, :], cw_ref[2:3, :]
        conv = cw0 * zm2 + cw1 * zm1 + cw2 * z
        gbv = gb_ref[...]
        _, ybh, rb = _rms_fwd(g_b * conv, gbv)
        dyb, dgb = _rms_bwd(dycat[:, aw:], ybh, rb, gbv)
        dgb_ref[...] += dgb
        dconv = dyb * g_b
        dcw_ref[0:1, :] += jnp.sum(dconv * zm2, axis=0, keepdims=True)
        dcw_ref[1:2, :] += jnp.sum(dconv * zm1, axis=0, keepdims=True)
        dcw_ref[2:3, :] += jnp.sum(dconv * z, axis=0, keepdims=True)
        nxt = dcn_ref[...]
        row = lax.broadcasted_iota(jnp.int32, dconv.shape, 0)
        dcp1 = jnp.where(row == TM - 1, nxt[0:1, :], pltpu.roll(dconv, TM - 1, 0))
        dcp2 = jnp.where(row == TM - 1, nxt[1:2, :],
                         jnp.where(row == TM - 2, nxt[0:1, :], pltpu.roll(dconv, TM - 2, 0)))
        dz = cw2 * dconv + cw1 * dcp1 + cw0 * dcp2
        dcn_ref[...] = dconv[0:8, :]
        dh_ref[:, 2 * aw:3 * aw] = (dyb * conv).astype(BF16)
        dh_ref[:, 3 * aw:4 * aw] = (dz * val).astype(BF16)
        dh_ref[:, 4 * aw:5 * aw] = (dz * g_c).astype(BF16)
        dxn = jnp.zeros((TM, d), F32)
        for k in range(N_CHIPS):
            dxn = dxn + _dot_nt(dh_ref[:, k * ns:(k + 1) * ns], win_ref[k])
        g1v = g1_ref[...]
        _, xh1, r1 = _rms_fwd(x_ref[...], g1v)
        dxr, dg1 = _rms_bwd(dxn, xh1, r1, g1v)
        dg1_ref[...] += dg1
        dx_ref[...] = dx1v + dxr

    ins = (x, dx1, h, h, mixed_all, th_all, g1, lng, lnb, wsp, b_t, cw, ga, gb, w_out, w_in)
    acc_shapes = [(1, d), (1, aw), (1, aw), wsp.shape, (CHUNK, CHUNK), cw.shape, (1, aw), (1, aw)]
    return _host_call(
        body, "mixer_bwd", (n,),
        [rev(d), rev(d), rev(nh), hprev, rev(aw), rev(d)] + [_full(a.shape) for a in ins[6:]],
        [rev(d), rev(nh)] + [_acc(sh) for sh in acc_shapes],
        (jax.ShapeDtypeStruct((s, d), F32), jax.ShapeDtypeStruct((s, nh), BF16))
        + tuple(jax.ShapeDtypeStruct(sh, F32) for sh in acc_shapes),
        [pltpu.VMEM((TM, aw), F32), pltpu.VMEM((8, aw), F32)],
        ins, ("arbitrary",), hosted)


def _weight_grad(a, b, name, tm, tn, col_sharded, hosted=None):
    t, m = a.shape
    n = b.shape[1]

    def body(a_ref, b_ref, o_ref):
        o_ref[...] = _dot_tn(a_ref[...].astype(BF16), b_ref[...].astype(BF16)).astype(BF16)

    if col_sharded:
        ns = n // N_CHIPS
        per = ns // tn
        out_shape = jax.ShapeDtypeStruct((N_CHIPS, m, ns), BF16)
        out_spec = pl.BlockSpec((None, tm, tn), lambda i, j: (j // per, i, j % per))
    else:
        out_shape = jax.ShapeDtypeStruct((m, n), BF16)
        out_spec = pl.BlockSpec((tm, tn), lambda i, j: (i, j))
    (out,), extra = _host_call(
        body, name, (m // tm, n // tn),
        [pl.BlockSpec((t, tm), lambda i, j: (0, i)), pl.BlockSpec((t, tn), lambda i, j: (0, j))],
        [out_spec], (out_shape,), [], (a, b), ("parallel", "parallel"), hosted)
    return (out if col_sharded else out.reshape(N_CHIPS, m // N_CHIPS, n)), extra


def _row_tile(rows, cap=256):
    best = None
    for t in range(16, min(rows, cap) + 1, 16):
        if rows % t == 0:
            best = t
    return best if best is not None else rows


def _adamw_math(w, g, m, v):
    m2 = ADAM_B1 * m + (1.0 - ADAM_B1) * g
    v2 = ADAM_B2 * v + (1.0 - ADAM_B2) * (g * g)
    m_hat = m2 / (1.0 - ADAM_B1 ** ADAM_STEP)
    v_hat = v2 / (1.0 - ADAM_B2 ** ADAM_STEP)
    delta = -ADAM_LR * (m_hat / (jnp.sqrt(v_hat) + ADAM_EPS) + ADAM_WD * w)
    return delta, m2, v2


def _adamw(w, g, m, v, name):
    r, c = w.shape
    tr = _row_tile(r) if r >= 16 else r

    def body(w_ref, g_ref, m_ref, v_ref, d_ref, m2_ref, v2_ref):
        d_ref[...], m2_ref[...], v2_ref[...] = _adamw_math(w_ref[...], g_ref[...], m_ref[...], v_ref[...])

    sh = jax.ShapeDtypeStruct((r, c), F32)
    return pl.pallas_call(
        body, name=name, grid=(r // tr,),
        in_specs=[_rows(tr, c)] * 4, out_specs=[_rows(tr, c)] * 3, out_shape=(sh, sh, sh),
        compiler_params=_params("parallel"),
    )(w, g, m, v)


def _finalize(items, place, name, hosted=None):
    r, c = items[0][2].shape
    tr = _row_tile(r)
    nw = len(items)

    def body(place_ref, *refs):
        ins, outs = refs[:7 * nw], refs[7 * nw:]
        for k in range(nw):
            own_ref, s0_ref, s1_ref, s2_ref, w_ref, m_ref, v_ref = ins[7 * k:7 * k + 7]
            g_ref, d_ref, m2_ref, v2_ref = outs[4 * k:4 * k + 4]
            g = own_ref[...].astype(F32) + s0_ref[...].astype(F32)
            g = (g + s1_ref[...].astype(F32)) + s2_ref[...].astype(F32)
            g_ref[...] = g
            d_ref[...], m2_ref[...], v2_ref[...] = _adamw_math(w_ref[...], g, m_ref[...], v_ref[...])

    def slot(k):
        return pl.BlockSpec((None, tr, c), lambda i, pref: (k, i, 0))

    rows = pl.BlockSpec((tr, c), lambda i, pref: (i, 0))
    sh = jax.ShapeDtypeStruct((r, c), F32)
    one = [pl.BlockSpec((None, tr, c), lambda i, pref: (pref[1], i, 0)), slot(0), slot(1), slot(2), rows, rows, rows]
    args = [a for part, slots, w, m, v in items for a in (part, slots, slots, slots, w, m, v)]
    res, extra = _host_call(body, name, (r // tr,), one * nw, [rows] * (4 * nw), (sh,) * (4 * nw), [], args,
                            ("parallel",), hosted, prefetch=(place,))
    return [res[4 * k:4 * k + 4] for k in range(nw)], extra


def _small_sum_adamw(parts, w, m, v):
    nd, r, c = parts.shape

    def body(p_ref, w_ref, m_ref, v_ref, g_ref, d_ref, m2_ref, v2_ref):
        g = p_ref[0]
        for k in range(1, nd):
            g = g + p_ref[k]
        g_ref[...] = g
        d_ref[...], m2_ref[...], v2_ref[...] = _adamw_math(w_ref[...], g, m_ref[...], v_ref[...])

    sh = jax.ShapeDtypeStruct((r, c), F32)
    return pl.pallas_call(
        body, name="small_sum_adamw", out_shape=(sh, sh, sh, sh),
        compiler_params=pltpu.CompilerParams(vmem_limit_bytes=VMEM_LIMIT),
    )(parts, w, m, v)


def _place():
    x, y, c = lax.axis_index("x"), lax.axis_index("y"), lax.axis_index("c")
    chips = [(1 - x, y), (x, 1 - y), (1 - x, 1 - y)]
    return x, y, c, 2 * x + y, chips


def _remote(src, dst, send_sem, recv_sem, to):
    return pltpu.make_async_remote_copy(src_ref=src, dst_ref=dst, send_sem=send_sem, recv_sem=recv_sem,
                                        device_id=to, device_id_type=MESH)


class _Exchange:
    def __init__(self, ins, out_shapes, sem_shape, start, finish, middle=None, in_place=False):
        self.ins, self.out_shapes, self.sem_shape = tuple(ins), tuple(out_shapes), sem_shape
        self.start, self.finish, self.middle = start, finish, middle
        self.in_place = in_place


def _run_exchange(ex, name):
    n_in, n_out = len(ex.ins), len(ex.out_shapes)

    def body(*refs):
        ins, outs = refs[:n_in], refs[n_in:n_in + n_out]
        send_sems, recv_sems = refs[n_in + n_out:]
        ex.start(ins, outs, send_sems, recv_sems)
        if ex.middle is not None:
            ex.middle(ins, outs, send_sems, recv_sems)
        ex.finish(ins, outs, send_sems, recv_sems)

    sem = pltpu.SemaphoreType.DMA(ex.sem_shape)
    return pl.pallas_call(
        body, name=name, out_shape=ex.out_shapes, in_specs=[ANY] * n_in, out_specs=[ANY] * n_out,
        input_output_aliases={k: k for k in range(n_in)} if ex.in_place else {}, scratch_shapes=[sem, sem],
    )(*ex.ins)


def _host_call(body, name, grid, in_specs, out_specs, out_shape, scratch_shapes, args, semantics, hosted,
               prefetch=()):
    hosted = [] if hosted is None else (list(hosted) if isinstance(hosted, (list, tuple)) else [hosted])
    n_pre, n_in, n_out, n_scr = len(prefetch), len(in_specs), len(out_specs), len(scratch_shapes)
    h_ins = [a for ex in hosted for a in ex.ins]
    h_outs = [s for ex in hosted for s in ex.out_shapes]
    h_in, h_out = len(h_ins), len(h_outs)

    def wrapped(*refs):
        pre, refs = refs[:n_pre], refs[n_pre:]
        a, hi = refs[:n_in], refs[n_in:n_in + h_in]
        o = refs[n_in + h_in:n_in + h_in + n_out]
        ho = refs[n_in + h_in + n_out:n_in + h_in + n_out + h_out]
        scr = refs[n_in + h_in + n_out + h_out:]

        def run(phase):
            i0 = o0 = 0
            for k, ex in enumerate(hosted):
                fn = getattr(ex, phase)
                if fn is not None:
                    fn(hi[i0:i0 + len(ex.ins)], ho[o0:o0 + len(ex.out_shapes)], scr[n_scr + 2 * k],
                       scr[n_scr + 2 * k + 1])
                i0, o0 = i0 + len(ex.ins), o0 + len(ex.out_shapes)

        if hosted:
            first = functools.reduce(jnp.logical_and, [pl.program_id(k) == 0 for k in range(len(grid))])

            @pl.when(first)
            def _():
                run("start")

        if any(ex.middle is not None for ex in hosted):
            half_way = functools.reduce(jnp.logical_and, [
                pl.program_id(0) == max(1, grid[0] * MIDDLE_STEP_16THS // 16)] + [
                pl.program_id(k) == 0 for k in range(1, len(grid))])

            @pl.when(half_way)
            def _():
                run("middle")

        body(*pre, *a, *o, *scr[:n_scr])

        if hosted:
            last = functools.reduce(jnp.logical_and, [pl.program_id(k) == grid[k] - 1 for k in range(len(grid))])

            @pl.when(last)
            def _():
                run("finish")

    sems = [pltpu.SemaphoreType.DMA(ex.sem_shape) for ex in hosted for _ in range(2)]
    aliases, i0, o0 = {}, n_pre + n_in, n_out
    for ex in hosted:
        if ex.in_place:
            aliases.update({i0 + k: o0 + k for k in range(len(ex.ins))})
        i0, o0 = i0 + len(ex.ins), o0 + len(ex.out_shapes)
    all_in, all_out = list(in_specs) + [ANY] * h_in, list(out_specs) + [ANY] * h_out
    all_scr = list(scratch_shapes) + sems
    params = _params(*(["arbitrary"] * len(grid) if hosted else semantics))
    shapes = tuple(out_shape) + tuple(h_outs)
    if n_pre:
        call = pl.pallas_call(
            wrapped, name=name, out_shape=shapes, input_output_aliases=aliases, compiler_params=params,
            grid_spec=pltpu.PrefetchScalarGridSpec(num_scalar_prefetch=n_pre, grid=grid, in_specs=all_in,
                                                   out_specs=all_out, scratch_shapes=all_scr))
    else:
        call = pl.pallas_call(
            wrapped, name=name, grid=grid, in_specs=all_in, out_specs=all_out, out_shape=shapes,
            scratch_shapes=all_scr, input_output_aliases=aliases, compiler_params=params)
    res = call(*prefetch, *args, *h_ins)
    return res[:n_out], res[n_out:]


def _all_gather(shards, small=()):
    items = tuple(shards) + tuple(small)
    nw = len(shards)

    def place():
        x, y, c, me, _ = _place()
        first = (x + (1 - c) * (1 - 2 * x), y + c * (1 - 2 * y))
        second = (x + c * (1 - 2 * x), y + (1 - c) * (1 - 2 * y))
        diag = (1 - x, 1 - y)
        return x, y, c, me, (first, second, diag)

    def halves(w, c):
        rh = items[w].shape[0] // 2
        return pl.ds(c * rh, rh), pl.ds((1 - c) * rh, rh)

    def start(ins, outs, ss, rs):
        x, y, c, me, chips = place()
        for w in range(len(items)):
            _remote(ins[w], outs[w].at[me], ss.at[w, 6], rs.at[w, 6], (x, y, 1 - c)).start()
            if w < nw:
                mine, _ = halves(w, c)
                _remote(ins[w].at[mine], outs[w].at[me, mine], ss.at[w, 0], rs.at[w, 0], (*chips[0], c)).start()
            else:
                for k in range(3):
                    _remote(ins[w], outs[w].at[me], ss.at[w, k], rs.at[w, k], (*chips[k], c)).start()

    def onward(outs, ss, rs, w, k, x, y, c, chips):
        mine, _ = halves(w, c)
        pk = 2 * chips[k][0] + chips[k][1]
        got = outs[w].at[pk, mine]
        src = chips[1] if k == 2 else chips[k]
        _remote(got, got, ss.at[w, k], rs.at[w, k], (*src, c)).wait_recv()
        if k == 0:
            _remote(got, got, ss.at[w, 2], rs.at[w, 2], (*chips[1], c)).start()
        _remote(got, got, ss.at[w, 3 + k], rs.at[w, 3 + k], (x, y, 1 - c)).start()

    def middle(ins, outs, ss, rs):
        x, y, c, me, chips = place()
        for w in range(nw):
            mine, _ = halves(w, c)
            _remote(ins[w].at[mine], outs[w].at[me, mine], ss.at[w, 1], rs.at[w, 1], (*chips[1], c)).start()
        for w in range(nw):
            onward(outs, ss, rs, w, 0, x, y, c, chips)

    def finish(ins, outs, ss, rs):
        x, y, c, me, chips = place()
        sib = (x, y, 1 - c)
        for k in (1, 2):
            for w in range(nw):
                onward(outs, ss, rs, w, k, x, y, c, chips)
        for w in range(len(items)):
            if w < nw:
                mine, theirs = halves(w, c)
                for k, chip in ((3, chips[1]), (4, chips[0]), (5, chips[2])):
                    oth = outs[w].at[2 * chip[0] + chip[1], theirs]
                    _remote(oth, oth, ss.at[w, k], rs.at[w, k], sib).wait_recv()
                own = ins[w].at[mine]
                for k in range(6):
                    _remote(own, own, ss.at[w, k], rs.at[w, k], sib).wait_send()
            else:
                for k in range(3):
                    got = outs[w].at[2 * chips[k][0] + chips[k][1]]
                    _remote(got, got, ss.at[w, k], rs.at[w, k], (*chips[k], c)).wait_recv()
                    _remote(ins[w], ins[w], ss.at[w, k], rs.at[w, k], sib).wait_send()
            _remote(ins[w], outs[w].at[me], ss.at[w, 6], rs.at[w, 6], sib).wait()

    out_shapes = tuple(jax.ShapeDtypeStruct((N_CHIPS,) + a.shape, a.dtype) for a in items)
    return _Exchange(items, out_shapes, (len(items), 7), start, finish, middle if nw else None)


def _chip_reduce(grads, name):
    nw = len(grads)
    step = 64

    def body(*refs):
        ins, outs = refs[:nw], refs[nw:2 * nw]
        own, got = refs[2 * nw:3 * nw], refs[3 * nw:4 * nw]
        send_sems, recv_sems, local_sems = refs[4 * nw:]
        x, y, c, _, _ = _place()
        moves = []
        for w in range(nw):
            rh = grads[w].shape[1] // 2
            away = _remote(ins[w].at[:, pl.ds((1 - c) * rh, rh), :], got[w], send_sems.at[w], recv_sems.at[w],
                           (x, y, 1 - c))
            mine = pltpu.make_async_copy(ins[w].at[:, pl.ds(c * rh, rh), :], own[w], local_sems.at[w])
            away.start()
            mine.start()
            moves.append((away, mine))
        back = []
        for w, (away, mine) in enumerate(moves):
            nb, rh, _ = own[w].shape
            mine.wait()
            away.wait()
            for k in range(nb):
                def add(i, carry, w=w, k=k):
                    rows = pl.ds(pl.multiple_of(i * step, step), step)
                    own[w][k, rows, :] = (own[w][k, rows, :].astype(F32) + got[w][k, rows, :].astype(F32)).astype(BF16)
                    return carry
                lax.fori_loop(0, rh // step, add, 0)
                tail = rh % step
                if tail:
                    rows = slice(rh - tail, rh)
                    own[w][k, rows, :] = (own[w][k, rows, :].astype(F32) + got[w][k, rows, :].astype(F32)).astype(BF16)
            wb = pltpu.make_async_copy(own[w], outs[w].at[:, pl.ds(c * rh, rh), :], local_sems.at[w])
            wb.start()
            back.append(wb)
        for wb in back:
            wb.wait()

    halves = [pltpu.VMEM((g.shape[0], g.shape[1] // 2, g.shape[2]), BF16) for g in grads]
    sem = pltpu.SemaphoreType.DMA((nw,))
    return pl.pallas_call(
        body, name=name, out_shape=tuple(jax.ShapeDtypeStruct(g.shape, BF16) for g in grads),
        in_specs=[ANY] * nw, out_specs=[ANY] * nw, scratch_shapes=halves + halves + [sem, sem, sem],
        compiler_params=pltpu.CompilerParams(vmem_limit_bytes=VMEM_LIMIT),
    )(*grads)


def _scatter_partials(parts):
    nw = len(parts)

    def copies(ins, outs, ss, rs):
        _, _, c, _, chips = _place()
        res = []
        for r, (px, py) in enumerate(chips):
            for w in range(nw):
                rh = parts[w].shape[1] // 2
                rows = pl.ds(c * rh, rh)
                res.append(_remote(ins[w].at[2 * px + py, rows], outs[w].at[r, rows], ss.at[w, r], rs.at[w, r],
                                   (px, py, c)))
        return res

    def start(ins, outs, ss, rs):
        for cp in copies(ins, outs, ss, rs):
            cp.start()

    def finish(ins, outs, ss, rs):
        for cp in copies(ins, outs, ss, rs):
            cp.wait()

    out_shapes = tuple(jax.ShapeDtypeStruct((3,) + p.shape[1:], p.dtype) for p in parts)
    return _Exchange(parts, out_shapes, (nw, 3), start, finish)


def _join_partials(parts, slots):
    nw = len(parts)

    def copies(outs, ss, rs, mine):
        x, y, c, me, _ = _place()
        res = []
        for w in range(nw):
            rh = parts[w].shape[1] // 2
            rows = pl.ds((c if mine else 1 - c) * rh, rh)
            own = outs[w].at[me, rows]
            got = outs[nw + w].at[:, rows, :]
            res.append(_remote(own, own, ss.at[w, 0], rs.at[w, 0], (x, y, 1 - c)))
            res.append(_remote(got, got, ss.at[w, 1], rs.at[w, 1], (x, y, 1 - c)))
        return res

    def start(ins, outs, ss, rs):
        for cp in copies(outs, ss, rs, True):
            cp.start()

    def finish(ins, outs, ss, rs):
        for cp in copies(outs, ss, rs, True):
            cp.wait_send()
        for cp in copies(outs, ss, rs, False):
            cp.wait_recv()

    arrays = tuple(parts) + tuple(slots)
    return _Exchange(arrays, tuple(jax.ShapeDtypeStruct(a.shape, a.dtype) for a in arrays), (nw, 2), start, finish,
                     in_place=True)


def _gather_small(slab):
    def copies(ins, outs, ss, rs):
        x, y, c, _, _ = _place()
        me = 4 * x + 2 * y + c
        out, arrivals = [], []
        for k in range(1, 8):
            px = 1 - x if k & 4 else x
            py = 1 - y if k & 2 else y
            pc = 1 - c if k & 1 else c
            out.append(_remote(ins[0], outs[0].at[me], ss.at[k - 1], rs.at[k - 1], (px, py, pc)))
            theirs = outs[0].at[4 * px + 2 * py + pc]
            arrivals.append((theirs, k - 1, (px, py, pc)))
        return pltpu.make_async_copy(ins[0], outs[0].at[me], ss.at[7]), out, arrivals

    def start(ins, outs, ss, rs):
        own, out, _ = copies(ins, outs, ss, rs)
        own.start()
        for cp in out:
            cp.start()

    def finish(ins, outs, ss, rs):
        own, out, arrivals = copies(ins, outs, ss, rs)
        for cp in out:
            cp.wait_send()
        for theirs, k, peer in arrivals:
            _remote(theirs, theirs, ss.at[k], rs.at[k], peer).wait_recv()
        own.wait()

    return _Exchange((slab,), (jax.ShapeDtypeStruct((8,) + slab.shape, slab.dtype),), (8,), start, finish)


_SMALL_VECS = ("ln_mix_g", "ln_attn_g", "ln_mem_g", "ln_ffn_g", "ln_final_g")


def _pack_small(p, extra, conv):
    d = p["ln_mix_g"].shape[-1]
    top = [p[k].reshape(1, d) for k in _SMALL_VECS]
    top.append(jnp.concatenate([p["sgu_ln_g"].reshape(-1), p["sgu_ln_b"].reshape(-1)]).reshape(1, d))
    top.append(jnp.concatenate([p["grp_norm_a"].reshape(-1), p["grp_norm_b"].reshape(-1)]).reshape(1, d))
    top.append(jnp.concatenate([p["b_spatial"].reshape(-1), extra]).reshape(1, d))
    mid = jnp.zeros((8, d), F32)
    if conv is not None:
        mid = jnp.pad(conv, ((0, 5), (0, d - conv.shape[1])))
    return jnp.concatenate([jnp.concatenate(top, axis=0), mid, p["w_spatial"].reshape(-1, d)], axis=0)


def _unpack_small(slab):
    d = slab.shape[1]
    hw = d // 2
    out = {k: slab[i] for i, k in enumerate(_SMALL_VECS)}
    out["sgu_ln_g"], out["sgu_ln_b"] = slab[5, :hw], slab[5, hw:]
    out["grp_norm_a"], out["grp_norm_b"] = slab[6, :hw], slab[6, hw:]
    out["b_spatial"] = slab[7, :hw].reshape(HEADS, CHUNK)
    out["w_spatial"] = slab[16:].reshape(HEADS, CHUNK, CHUNK)
    return out


_BIG = ("w_in", "w_kv", "w_gate_up", "w_out", "w_q", "w_o", "w_down")
_WEIGHTS = ("ln_mix_g", "w_in", "sgu_ln_g", "sgu_ln_b", "w_spatial", "b_spatial", "conv_w", "grp_norm_a",
            "grp_norm_b", "w_out", "ln_attn_g", "ln_mem_g", "w_q", "w_kv", "w_o", "ln_ffn_g", "w_gate_up",
            "w_down", "ln_final_g")


def _step(p, m_, v_, x, mem, target):
    s, d = x.shape
    hw = d // 2
    row = lambda a: a.reshape(1, -1)
    x_, y_, c_ = lax.axis_index("x"), lax.axis_index("y"), lax.axis_index("c")
    chip = 2 * x_ + y_

    bf = {k: p[k].astype(BF16) for k in _BIG}
    conv8 = jnp.pad(p["conv_w"], ((0, 5), (0, 0)))
    w_in, w_out4, conv4 = _run_exchange(_all_gather([bf["w_in"], bf["w_out"]], [conv8]), "all_gather_mixer")
    cw = jnp.transpose(conv4[:, :3, :], (1, 0, 2)).reshape(3, hw)
    b_t = jnp.pad(jnp.transpose(p["b_spatial"]), ((0, 0), (0, CHUNK - HEADS)))
    g1, g2, gm, g3, gf = (row(p[k]) for k in _SMALL_VECS)
    lng, lnb, ga, gb = row(p["sgu_ln_g"]), row(p["sgu_ln_b"]), row(p["grp_norm_a"]), row(p["grp_norm_b"])
    wsp = p["w_spatial"]
    w_out = w_out4.reshape(-1, d)

    (h, x1, ycat, xn1, mixed, th), (w_kv, w_q4, w_o4, w_down4) = _mixer_fwd(
        x, g1, w_in, lng, lnb, wsp, b_t, cw, ga, gb, w_out,
        hosted=_all_gather([bf[k] for k in ("w_kv", "w_q", "w_o", "w_down")]))
    w_q, w_o, w_down = (a.reshape(-1, d) for a in (w_q4, w_o4, w_down4))
    memn, kv = _kv_fwd(mem, gm, w_kv)
    (x2, o, qs, probs), (w_gu,) = _attn_fwd(x1, g2, w_q, kv, w_o, hosted=_all_gather([bf["w_gate_up"]]))
    dx2, act, dgu, xn3, dx3, loss, dgf, dg3 = _ffn_fwd_bwd(x2, g3, gf, target, w_gu, w_down)

    place = jnp.stack([c_, chip]).astype(jnp.int32)

    def chip_partials(names, grads, tag):
        return list(_chip_reduce(grads, "chip_reduce_" + tag))

    names_d = ("w_down",)
    parts_d = chip_partials(names_d, (_weight_grad(act, dx3, "grad_w_down", 1408, 512, False)[0],), "down")
    g_gu, slots_d = _weight_grad(xn3, dgu, "grad_w_gate_up", 512, 1408, True, hosted=_scatter_partials(parts_d))
    names_a = ("w_gate_up",)
    parts_a = chip_partials(names_a, (g_gu,), "ffn")
    (dx1, dkv, dg2, g_o, g_q, g_out), slots_a = _attn_bwd(x1, dx2, o, ycat, qs, probs, g2, w_q, kv, w_o,
                                                           hosted=_scatter_partials(parts_a))
    g_kv, dgm = _kv_bwd(dkv, mem, memn, gm, w_kv)
    names_b = ("w_o", "w_out", "w_q", "w_kv")
    shard_major = lambda g: g.reshape(N_CHIPS, -1, d)
    parts_b = chip_partials(names_b, (shard_major(g_o), shard_major(g_out), shard_major(g_q), g_kv), "attn")
    names_da = names_d + names_a
    (dx, dh, dg1, dlng, dlnb, dwsp, dbt, dcw, dga, dgb), extra = _mixer_bwd(
        x, dx1, h, mixed, th, g1, lng, lnb, wsp, b_t, cw, ga, gb, w_out, w_in,
        hosted=[_scatter_partials(parts_b), _join_partials(parts_d + parts_a, slots_d + slots_a)])
    slots_b, joined = extra[:len(names_b)], extra[len(names_b):]
    whole = dict(zip(names_da, zip(joined[:len(names_da)], joined[len(names_da):])))
    small = {"ln_mix_g": dg1, "ln_attn_g": dg2, "ln_mem_g": dgm, "ln_ffn_g": dg3, "ln_final_g": dgf,
             "sgu_ln_g": dlng, "sgu_ln_b": dlnb, "grp_norm_a": dga, "grp_norm_b": dgb,
             "b_spatial": jnp.transpose(dbt[:, :HEADS]), "w_spatial": dwsp}
    loss_vec = jnp.pad(loss.reshape(1), (0, hw - 1))
    g_in, extra = _weight_grad(
        xn1, dh, "grad_w_in", 1024, 640, True,
        hosted=[_gather_small(_pack_small(small, loss_vec, dcw)), _join_partials(parts_b, slots_b)])
    parts = extra[0]
    whole.update(zip(names_b, zip(extra[1:1 + len(names_b)], extra[1 + len(names_b):])))
    (part_in,) = chip_partials(("w_in",), (g_in,), "mixer")
    out_g, out_d, out_m, out_v = {}, {}, {}, {}

    def finalize(ks, tag, hosted=None):
        done, res = _finalize([(whole[k][0], whole[k][1], p[k], m_[k], v_[k]) for k in ks], place,
                              "finalize_" + tag, hosted)
        for k, (g, dl, nm, nv) in zip(ks, done):
            out_g[k], out_d[k], out_m[k], out_v[k] = g, dl, nm, nv
        return res

    (slots_in,) = finalize(("w_down",), "w_down", _scatter_partials([part_in]))
    finalize(("w_o", "w_out", "w_q"), "attn")
    for k in ("w_gate_up", "w_kv"):
        finalize((k,), k)
    whole["w_in"] = _run_exchange(_join_partials([part_in], [slots_in]), "rs_join_mixer")
    finalize(("w_in",), "w_in")

    zeros = jnp.zeros((hw,), F32)
    sg, sd, sm, sv = _small_sum_adamw(parts, _pack_small(p, zeros, None), _pack_small(m_, zeros, None),
                                      _pack_small(v_, zeros, None))
    for tree, slab in zip((out_g, out_d, out_m, out_v), (sg, sd, sm, sv)):
        tree.update(_unpack_small(slab))
    loss_out = sg[7, hw]
    g_conv = lax.dynamic_slice(sg[8:11, :hw], (0, chip * (hw // N_CHIPS)), (3, hw // N_CHIPS))
    out_g["conv_w"] = g_conv
    out_d["conv_w"], out_m["conv_w"], out_v["conv_w"] = _adamw(p["conv_w"], g_conv, m_["conv_w"], v_["conv_w"],
                                                                "adamw_conv_w")
    return loss_out, dx, out_g, out_d, out_m, out_v


def kernel(x, mem, ln_mix_g, w_in, sgu_ln_g, sgu_ln_b, w_spatial, b_spatial, conv_w, grp_norm_a, grp_norm_b, w_out, ln_attn_g, ln_mem_g, w_q, w_kv, w_o, ln_ffn_g, w_gate_up, w_down, ln_final_g, loss_target, m_ln_mix_g, m_w_in, m_sgu_ln_g, m_sgu_ln_b, m_w_spatial, m_b_spatial, m_conv_w, m_grp_norm_a, m_grp_norm_b, m_w_out, m_ln_attn_g, m_ln_mem_g, m_w_q, m_w_kv, m_w_o, m_ln_ffn_g, m_w_gate_up, m_w_down, m_ln_final_g, v_ln_mix_g, v_w_in, v_sgu_ln_g, v_sgu_ln_b, v_w_spatial, v_b_spatial, v_conv_w, v_grp_norm_a, v_grp_norm_b, v_w_out, v_ln_attn_g, v_ln_mem_g, v_w_q, v_w_kv, v_w_o, v_ln_ffn_g, v_w_gate_up, v_w_down, v_ln_final_g):
    p = dict(ln_mix_g=ln_mix_g, w_in=w_in, sgu_ln_g=sgu_ln_g, sgu_ln_b=sgu_ln_b, w_spatial=w_spatial,
             b_spatial=b_spatial, conv_w=conv_w, grp_norm_a=grp_norm_a, grp_norm_b=grp_norm_b, w_out=w_out,
             ln_attn_g=ln_attn_g, ln_mem_g=ln_mem_g, w_q=w_q, w_kv=w_kv, w_o=w_o, ln_ffn_g=ln_ffn_g,
             w_gate_up=w_gate_up, w_down=w_down, ln_final_g=ln_final_g)
    m_ = dict(ln_mix_g=m_ln_mix_g, w_in=m_w_in, sgu_ln_g=m_sgu_ln_g, sgu_ln_b=m_sgu_ln_b, w_spatial=m_w_spatial,
              b_spatial=m_b_spatial, conv_w=m_conv_w, grp_norm_a=m_grp_norm_a, grp_norm_b=m_grp_norm_b,
              w_out=m_w_out, ln_attn_g=m_ln_attn_g, ln_mem_g=m_ln_mem_g, w_q=m_w_q, w_kv=m_w_kv, w_o=m_w_o,
              ln_ffn_g=m_ln_ffn_g, w_gate_up=m_w_gate_up, w_down=m_w_down, ln_final_g=m_ln_final_g)
    v_ = dict(ln_mix_g=v_ln_mix_g, w_in=v_w_in, sgu_ln_g=v_sgu_ln_g, sgu_ln_b=v_sgu_ln_b, w_spatial=v_w_spatial,
              b_spatial=v_b_spatial, conv_w=v_conv_w, grp_norm_a=v_grp_norm_a, grp_norm_b=v_grp_norm_b,
              w_out=v_w_out, ln_attn_g=v_ln_attn_g, ln_mem_g=v_ln_mem_g, w_q=v_w_q, w_kv=v_w_kv, w_o=v_w_o,
              ln_ffn_g=v_ln_ffn_g, w_gate_up=v_w_gate_up, w_down=v_w_down, ln_final_g=v_ln_final_g)
    s, d = x.shape[-2], x.shape[-1]
    loss, dx, g, dl, nm, nv = _step(p, m_, v_, x.reshape(s, d), mem.reshape(-1, d), loss_target.reshape(s, d))
    outs = [loss, dx.reshape(x.shape)]
    for tree in (g, dl, nm, nv):
        outs += [tree[k].reshape(p[k].shape) for k in _WEIGHTS]
    return tuple(outs)
```

```python
import functools
import math

import jax
import jax.numpy as jnp
from jax import lax
from jax.experimental import pallas as pl
from jax.experimental.pallas import tpu as pltpu

F32 = jnp.float32
BF16 = jnp.bfloat16
EPS = 1e-6
CHUNK = 128
HEADS = 4
N_CHIPS = 4
TM = 512
TM_ATTN = 512
TM_FFN = 256
ADAM_LR, ADAM_B1, ADAM_B2, ADAM_EPS, ADAM_WD, ADAM_STEP = 0.001, 0.9, 0.999, 1e-08, 0.01, 10
GELU_C = math.sqrt(2.0 / math.pi)
GELU_K = 0.044715
SMALL_ROWS = 80
VMEM_LIMIT = 56 * 1024 * 1024
MIDDLE_STEP_16THS = 7
MESH = pl.DeviceIdType.MESH
ANY = pl.BlockSpec(memory_space=pl.ANY)


def _params(*sem):
    return pltpu.CompilerParams(dimension_semantics=sem, vmem_limit_bytes=VMEM_LIMIT)


def _dot(a, b):
    return jnp.dot(a, b, preferred_element_type=F32)


def _dot_nt(a, b):
    return lax.dot_general(a, b, (((1,), (1,)), ((), ())), preferred_element_type=F32)


def _dot_tn(a, b):
    return lax.dot_general(a, b, (((0,), (0,)), ((), ())), preferred_element_type=F32)


def _rms_fwd(x, g):
    r = lax.rsqrt(jnp.mean(x * x, axis=-1, keepdims=True) + EPS)
    xh = x * r
    return xh * g, xh, r


def _rms_bwd(dy, xh, r, g):
    dxh = dy * g
    dx = r * (dxh - xh * jnp.mean(dxh * xh, axis=-1, keepdims=True))
    return dx, jnp.sum(dy * xh, axis=0, keepdims=True)


def _full(shape):
    nd = len(shape)
    return pl.BlockSpec(shape, lambda *_: (0,) * nd, pipeline_mode=pl.Buffered(1))


def _acc(shape):
    nd = len(shape)
    return pl.BlockSpec(shape, lambda *_: (0,) * nd)


def _rows(tm, cols):
    return pl.BlockSpec((tm, cols), lambda i: (i, 0))


def _tril_weights(wsp_ref):
    row = lax.broadcasted_iota(jnp.int32, (CHUNK, CHUNK), 0)
    col = lax.broadcasted_iota(jnp.int32, (CHUNK, CHUNK), 1)
    return [jnp.where(row >= col, wsp_ref[hd], 0.0).astype(BF16) for hd in range(HEADS)]


def _shift_rows(z, zp):
    row = lax.broadcasted_iota(jnp.int32, z.shape, 0)
    zm1 = jnp.where(row == 0, zp[7:8, :], pltpu.roll(z, 1, 0))
    zm2 = jnp.where(row == 0, zp[6:7, :], jnp.where(row == 1, zp[7:8, :], pltpu.roll(z, 2, 0)))
    return zm1, zm2


def _gelu_parts(x):
    t = jnp.tanh(GELU_C * (x + GELU_K * (x * x * x)))
    return 0.5 * x * (1.0 + t), t


def _layer_norm_parts(v, g, b):
    mu = jnp.mean(v, axis=-1, keepdims=True)
    vc = v - mu
    rs = lax.rsqrt(jnp.mean(vc * vc, axis=-1, keepdims=True) + EPS)
    vhat = vc * rs
    return vhat * g + b, vhat, rs


def _cast_bf16(arrs, hosted=None):
    n = len(arrs)
    step = 64

    def body(*refs):
        for src, dst in zip(refs[:n], refs[n:]):
            def chunk(i, carry, src=src, dst=dst):
                rows = pl.ds(pl.multiple_of(i * step, step), step)
                dst[rows, :] = src[rows, :].astype(BF16)
                return carry
            lax.fori_loop(0, src.shape[0] // step, chunk, 0)

    return _host_call(
        body, "cast_weights", (1,), [_full(a.shape) for a in arrs], [_acc(a.shape) for a in arrs],
        tuple(jax.ShapeDtypeStruct(a.shape, BF16) for a in arrs), [], tuple(arrs), ("arbitrary",), hosted)


def _kv_fwd(mem, g_mem, w_kv):
    m, d = mem.shape
    ns = w_kv.shape[2]

    def body(mem_ref, g_ref, w_ref, memn_ref, kv_ref):
        y, _, _ = _rms_fwd(mem_ref[...], g_ref[...])
        yb = y.astype(BF16)
        memn_ref[...] = yb
        for k in range(N_CHIPS):
            kv_ref[:, k * ns:(k + 1) * ns] = _dot(yb, w_ref[k]).astype(BF16)

    return pl.pallas_call(
        body, name="kv_fwd",
        out_shape=(jax.ShapeDtypeStruct((m, d), BF16), jax.ShapeDtypeStruct((m, N_CHIPS * ns), BF16)),
        compiler_params=pltpu.CompilerParams(vmem_limit_bytes=VMEM_LIMIT),
    )(mem, g_mem, w_kv)


def _mixer_fwd(x, g1, w_in, lng, lnb, wsp, b_t, cw, ga, gb, w_out, hosted=None):
    s, d = x.shape
    n = s // TM
    nch = TM // CHUNK
    ns = w_in.shape[2]
    nh = N_CHIPS * ns
    aw = d // 2
    hd_w = aw // HEADS

    def body(x_ref, g1_ref, win_ref, lng_ref, lnb_ref, wsp_ref, bt_ref, cw_ref, ga_ref, gb_ref, wout_ref,
             h_ref, x1_ref, y_ref, xn_ref, mix_ref, th_ref, zp_ref):
        i = pl.program_id(0)

        @pl.when(i == 0)
        def _():
            zp_ref[...] = jnp.zeros_like(zp_ref)

        x = x_ref[...]
        xn, _, _ = _rms_fwd(x, g1_ref[...])
        xnb = xn.astype(BF16)
        xn_ref[...] = xnb
        for k in range(N_CHIPS):
            h_ref[:, k * ns:(k + 1) * ns] = _dot(xnb, win_ref[k])
        a, th = _gelu_parts(h_ref[:, 0:2 * aw])
        th_ref[...] = th
        u = a[:, :aw]
        vn, _, _ = _layer_norm_parts(a[:, aw:], lng_ref[...], lnb_ref[...])
        vnb = vn.astype(BF16)
        wm = _tril_weights(wsp_ref)
        for c in range(nch):
            for hd in range(HEADS):
                blk = vnb[c * CHUNK:(c + 1) * CHUNK, hd * hd_w:(hd + 1) * hd_w]
                mix_ref[c * CHUNK:(c + 1) * CHUNK, hd * hd_w:(hd + 1) * hd_w] = _dot(wm[hd], blk) + bt_ref[:, hd:hd + 1]
        ya, _, _ = _rms_fwd(u * mix_ref[...], ga_ref[...])
        g_b = h_ref[:, 2 * aw:3 * aw]
        z = h_ref[:, 3 * aw:4 * aw] * h_ref[:, 4 * aw:5 * aw]
        zm1, zm2 = _shift_rows(z, zp_ref[...])
        conv = cw_ref[0:1, :] * zm2 + cw_ref[1:2, :] * zm1 + cw_ref[2:3, :] * z
        yb, _, _ = _rms_fwd(g_b * conv, gb_ref[...])
        zp_ref[...] = z[TM - 8:TM, :]
        ycat = jnp.concatenate([ya, yb], axis=-1).astype(BF16)
        y_ref[...] = ycat
        x1_ref[...] = x + _dot(ycat, wout_ref[...])

    return _host_call(
        body, "mixer_fwd", (n,),
        [_rows(TM, d), _full(g1.shape), _full(w_in.shape), _full(lng.shape), _full(lnb.shape),
         _full(wsp.shape), _full(b_t.shape), _full(cw.shape), _full(ga.shape), _full(gb.shape),
         _full(w_out.shape)],
        [_rows(TM, nh), _rows(TM, d), _rows(TM, d), _rows(TM, d), _rows(TM, aw), _rows(TM, d)],
        (jax.ShapeDtypeStruct((s, nh), F32), jax.ShapeDtypeStruct((s, d), F32),
         jax.ShapeDtypeStruct((s, d), BF16), jax.ShapeDtypeStruct((s, d), BF16),
         jax.ShapeDtypeStruct((s, aw), F32), jax.ShapeDtypeStruct((s, d), F32)),
        [pltpu.VMEM((8, aw), F32)],
        (x, g1, w_in, lng, lnb, wsp, b_t, cw, ga, gb, w_out), ("arbitrary",), hosted)


def _attn_fwd(x1, g2, w_q, kv, w_o, hosted=None):
    s, d = x1.shape
    tm = min(TM_ATTN, s)
    n = s // tm
    dh = d // HEADS
    m = kv.shape[0]
    scale = dh ** -0.5

    def body(x1_ref, g2_ref, wq_ref, kv_ref, wo_ref, x2_ref, o_ref, q_ref, p_ref):
        x1v = x1_ref[...]
        xn, _, _ = _rms_fwd(x1v, g2_ref[...])
        q_ref[...] = _dot(xn.astype(BF16), wq_ref[...]).astype(BF16)
        for hd in range(HEADS):
            kh = kv_ref[:, hd * dh:(hd + 1) * dh]
            vh = kv_ref[:, d + hd * dh:d + (hd + 1) * dh]
            sc = _dot_nt(q_ref[:, hd * dh:(hd + 1) * dh], kh) * scale
            e = jnp.exp(sc - jnp.max(sc, axis=-1, keepdims=True))
            p = e / jnp.sum(e, axis=-1, keepdims=True)
            p_ref[:, hd * m:(hd + 1) * m] = p
            o_ref[:, hd * dh:(hd + 1) * dh] = _dot(p.astype(BF16), vh).astype(BF16)
        x2_ref[...] = x1v + _dot(o_ref[...], wo_ref[...])

    return _host_call(
        body, "attn_fwd", (n,),
        [_rows(tm, d), _full(g2.shape), _full(w_q.shape), _full(kv.shape), _full(w_o.shape)],
        [_rows(tm, d), _rows(tm, d), _rows(tm, d), _rows(tm, HEADS * m)],
        (jax.ShapeDtypeStruct((s, d), F32), jax.ShapeDtypeStruct((s, d), BF16), jax.ShapeDtypeStruct((s, d), BF16),
         jax.ShapeDtypeStruct((s, HEADS * m), F32)),
        [], (x1, g2, w_q, kv, w_o), ("parallel",), hosted)


def _ffn_fwd_bwd(x2, g3, gf, target, w_gu, w_down):
    s, d = x2.shape
    tm = min(TM_FFN, s)
    n = s // tm
    ns = w_gu.shape[2]
    ff = 2 * ns

    def body(x2_ref, g3_ref, gf_ref, t_ref, wgu_ref, wd_ref,
             dx2_ref, act_ref, dgu_ref, xn_ref, dx3_ref, loss_ref, dgf_ref, dg3_ref):
        i = pl.program_id(0)

        @pl.when(i == 0)
        def _():
            loss_ref[...] = jnp.zeros_like(loss_ref)
            dgf_ref[...] = jnp.zeros_like(dgf_ref)
            dg3_ref[...] = jnp.zeros_like(dg3_ref)

        x2v = x2_ref[...]
        xn, xh3, r3 = _rms_fwd(x2v, g3_ref[...])
        xnb = xn.astype(BF16)
        xn_ref[...] = xnb
        x3 = x2v
        saved = []
        for j in range(2):
            g = _dot(xnb, wgu_ref[j])
            u = _dot(xnb, wgu_ref[2 + j])
            sg = 1.0 / (1.0 + jnp.exp(-g))
            sl = g * sg
            actb = (sl * u).astype(BF16)
            act_ref[:, j * ns:(j + 1) * ns] = actb
            x3 = x3 + _dot(actb, wd_ref[j * ns:(j + 1) * ns, :])
            saved.append((u, sl, sg * (1.0 + g * (1.0 - sg))))
        gfv = gf_ref[...]
        y, xhf, rf = _rms_fwd(x3, gfv)
        e = y - t_ref[...]
        loss_ref[...] += 0.5 * jnp.sum(jnp.sum(e * e, axis=-1, keepdims=True), axis=0, keepdims=True) / d
        dx3, dgf = _rms_bwd(e / d, xhf, rf, gfv)
        dgf_ref[...] += dgf
        dx3b = dx3.astype(BF16)
        dx3_ref[...] = dx3b
        dxn = jnp.zeros_like(x2v)
        for j in range(2):
            u, sl, dsl = saved[j]
            dact = _dot_nt(dx3b, wd_ref[j * ns:(j + 1) * ns, :])
            dgb = (dact * u * dsl).astype(BF16)
            dub = (dact * sl).astype(BF16)
            dgu_ref[:, j * ns:(j + 1) * ns] = dgb
            dgu_ref[:, ff + j * ns:ff + (j + 1) * ns] = dub
            dxn = dxn + _dot_nt(dgb, wgu_ref[j]) + _dot_nt(dub, wgu_ref[2 + j])
        dxr, dg3 = _rms_bwd(dxn, xh3, r3, g3_ref[...])
        dg3_ref[...] += dg3
        dx2_ref[...] = dx3 + dxr

    vec = jax.ShapeDtypeStruct((1, d), F32)
    return pl.pallas_call(
        body, name="ffn_fwd_bwd", grid=(n,),
        in_specs=[_rows(tm, d), _full(g3.shape), _full(gf.shape), _rows(tm, d), _full(w_gu.shape),
                  _full(w_down.shape)],
        out_specs=[_rows(tm, d), _rows(tm, ff), _rows(tm, 2 * ff), _rows(tm, d), _rows(tm, d),
                   _acc((1, 1)), _acc((1, d)), _acc((1, d))],
        out_shape=(jax.ShapeDtypeStruct((s, d), F32), jax.ShapeDtypeStruct((s, ff), BF16),
                   jax.ShapeDtypeStruct((s, 2 * ff), BF16), jax.ShapeDtypeStruct((s, d), BF16),
                   jax.ShapeDtypeStruct((s, d), BF16), jax.ShapeDtypeStruct((1, 1), F32), vec, vec),
        compiler_params=_params("arbitrary"),
    )(x2, g3, gf, target, w_gu, w_down)


def _attn_bwd(x1, dx2, o, ycat, qs, probs, g2, w_q, kv, w_o, hosted=None):
    s, d = x1.shape
    tm = min(TM_ATTN, s)
    n = s // tm
    dh = d // HEADS
    scale = dh ** -0.5
    m = kv.shape[0]

    def body(x1_ref, dx2_ref, o_ref, y_ref, q_ref, p_ref, g2_ref, wq_ref, kv_ref, wo_ref,
             dx1_ref, dkv_ref, dg2_ref, gwo_out, gwq_out, gwout_out, dq_ref, gwo_ref, gwq_ref, gwout_ref):
        i = pl.program_id(0)

        @pl.when(i == 0)
        def _():
            for r in (dkv_ref, dg2_ref, gwo_ref, gwq_ref, gwout_ref):
                r[...] = jnp.zeros_like(r)

        xn, xh2, r2 = _rms_fwd(x1_ref[...], g2_ref[...])
        xnb = xn.astype(BF16)
        dx2v = dx2_ref[...]
        dx2b = dx2v.astype(BF16)
        gwo_ref[...] += _dot_tn(o_ref[...], dx2b)
        do = _dot_nt(dx2b, wo_ref[...])
        for hd in range(HEADS):
            qb = q_ref[:, hd * dh:(hd + 1) * dh]
            p = p_ref[:, hd * m:(hd + 1) * m]
            kh = kv_ref[:, hd * dh:(hd + 1) * dh]
            vh = kv_ref[:, d + hd * dh:d + (hd + 1) * dh]
            dob = do[:, hd * dh:(hd + 1) * dh].astype(BF16)
            dp = _dot_nt(dob, vh)
            ds = p * (dp - jnp.sum(dp * p, axis=-1, keepdims=True))
            dsb = (ds * scale).astype(BF16)
            dq_ref[:, hd * dh:(hd + 1) * dh] = _dot(dsb, kh).astype(BF16)
            dkv_ref[:, hd * dh:(hd + 1) * dh] += _dot_tn(dsb, qb)
            dkv_ref[:, d + hd * dh:d + (hd + 1) * dh] += _dot_tn(p.astype(BF16), dob)
        dqb = dq_ref[...]
        gwq_ref[...] += _dot_tn(xnb, dqb)
        dxn = _dot_nt(dqb, wq_ref[...])
        dxr, dg2 = _rms_bwd(dxn, xh2, r2, g2_ref[...])
        dg2_ref[...] += dg2
        dx1 = dx2v + dxr
        dx1_ref[...] = dx1
        gwout_ref[...] += _dot_tn(y_ref[...], dx1.astype(BF16))

        @pl.when(i == n - 1)
        def _():
            for acc, out in ((gwo_ref, gwo_out), (gwq_ref, gwq_out), (gwout_ref, gwout_out)):
                out[...] = acc[...].astype(BF16)

    sq = jax.ShapeDtypeStruct((d, d), BF16)
    return _host_call(
        body, "attn_bwd", (n,),
        [_rows(tm, d), _rows(tm, d), _rows(tm, d), _rows(tm, d), _rows(tm, d), _rows(tm, HEADS * m),
         _full(g2.shape), _full(w_q.shape), _full(kv.shape), _full(w_o.shape)],
        [_rows(tm, d), _acc((m, 2 * d)), _acc((1, d)), _acc((d, d)), _acc((d, d)), _acc((d, d))],
        (jax.ShapeDtypeStruct((s, d), F32), jax.ShapeDtypeStruct((m, 2 * d), F32),
         jax.ShapeDtypeStruct((1, d), F32), sq, sq, sq),
        [pltpu.VMEM((tm, d), BF16)] + [pltpu.VMEM((d, d), F32)] * 3,
        (x1, dx2, o, ycat, qs, probs, g2, w_q, kv, w_o), ("arbitrary",), hosted)


def _kv_bwd(dkv, mem, memn, g_mem, w_kv):
    m, d = mem.shape
    ns = w_kv.shape[2]

    def body(dkv_ref, mem_ref, memn_ref, g_ref, w_ref, gw_ref, dg_ref):
        _, xh, _ = _rms_fwd(mem_ref[...], g_ref[...])
        dmemn = jnp.zeros((m, d), F32)
        for k in range(N_CHIPS):
            dkb = dkv_ref[:, k * ns:(k + 1) * ns].astype(BF16)
            gw_ref[k] = _dot_tn(memn_ref[...], dkb).astype(BF16)
            dmemn = dmemn + _dot_nt(dkb, w_ref[k])
        dg_ref[...] = jnp.sum(dmemn * xh, axis=0, keepdims=True)

    return pl.pallas_call(
        body, name="kv_bwd",
        out_shape=(jax.ShapeDtypeStruct((N_CHIPS, d, ns), BF16), jax.ShapeDtypeStruct((1, d), F32)),
        compiler_params=pltpu.CompilerParams(vmem_limit_bytes=VMEM_LIMIT),
    )(dkv, mem, memn, g_mem, w_kv)


def _mixer_bwd(x, dx1, h, mixed_all, th_all, g1, lng, lnb, wsp, b_t, cw, ga, gb, w_out, w_in, hosted=None):
    s, d = x.shape
    n = s // TM
    nch = TM // CHUNK
    ns = w_in.shape[2]
    nh = N_CHIPS * ns
    aw = d // 2
    hd_w = aw // HEADS

    def rev(cols):
        return pl.BlockSpec((TM, cols), lambda i: (n - 1 - i, 0))

    hprev = pl.BlockSpec((8, nh), lambda i: (jnp.maximum((n - 1 - i) * (TM // 8) - 1, 0), 0))

    def body(x_ref, dx1_ref, h_ref, hp_ref, mix_ref, th_ref, g1_ref, lng_ref, lnb_ref, wsp_ref, bt_ref, cw_ref,
             ga_ref, gb_ref, wout_ref, win_ref,
             dx_ref, dh_ref, dg1_ref, dlng_ref, dlnb_ref, dwsp_ref, dbt_ref, dcw_ref, dga_ref, dgb_ref,
             dvn_ref, dcn_ref):
        i = pl.program_id(0)

        @pl.when(i == 0)
        def _():
            for r in (dg1_ref, dlng_ref, dlnb_ref, dwsp_ref, dbt_ref, dcw_ref, dga_ref, dgb_ref, dcn_ref):
                r[...] = jnp.zeros_like(r)

        dx1v = dx1_ref[...]
        dycat = _dot_nt(dx1v.astype(BF16), wout_ref[...])
        ha = h_ref[:, 0:2 * aw]
        th = th_ref[...]
        a = 0.5 * ha * (1.0 + th)
        u = a[:, :aw]
        lngv = lng_ref[...]
        vn, vhat, rs = _layer_norm_parts(a[:, aw:], lngv, lnb_ref[...])
        vnb = vn.astype(BF16)
        wm = _tril_weights(wsp_ref)
        mixed = mix_ref[...]
        gav = ga_ref[...]
        _, yah, ra = _rms_fwd(u * mixed, gav)
        dya, dga = _rms_bwd(dycat[:, :aw], yah, ra, gav)
        dga_ref[...] += dga
        du = dya * mixed
        dmix = dya * u
        dmb = dmix.astype(BF16)
        tri = lax.broadcasted_iota(jnp.int32, (CHUNK, CHUNK), 0) >= lax.broadcasted_iota(jnp.int32, (CHUNK, CHUNK), 1)
        for hd in range(HEADS):
            dw = jnp.zeros((CHUNK, CHUNK), F32)
            db = jnp.zeros((CHUNK, 1), F32)
            for c in range(nch):
                rows = slice(c * CHUNK, (c + 1) * CHUNK)
                cols = slice(hd * hd_w, (hd + 1) * hd_w)
                dvn_ref[rows, cols] = _dot_tn(wm[hd], dmb[rows, cols])
                dw = dw + _dot_nt(dmb[rows, cols], vnb[rows, cols])
                db = db + jnp.sum(dmix[rows, cols], axis=1, keepdims=True)
            dwsp_ref[hd] += jnp.where(tri, dw, 0.0)
            dbt_ref[:, hd:hd + 1] += db
        dvn = dvn_ref[...]
        dlng_ref[...] += jnp.sum(dvn * vhat, axis=0, keepdims=True)
        dlnb_ref[...] += jnp.sum(dvn, axis=0, keepdims=True)
        dvh = dvn * lngv
        dv = rs * (dvh - jnp.mean(dvh, axis=-1, keepdims=True) - vhat * jnp.mean(dvh * vhat, axis=-1, keepdims=True))
        gprime = 0.5 * (1.0 + th) + 0.5 * ha * (1.0 - th * th) * (GELU_C * (1.0 + 3.0 * GELU_K * (ha * ha)))
        dh_ref[:, 0:2 * aw] = (jnp.concatenate([du, dv], axis=-1) * gprime).astype(BF16)
        g_b = h_ref[:, 2 * aw:3 * aw]
        g_c = h_ref[:, 3 * aw:4 * aw]
        val = h_ref[:, 4 * aw:5 * aw]
        z = g_c * val
        zp = jnp.where(i == n - 1, 0.0, hp_ref[:, 3 * aw:4 * aw] * hp_ref[:, 4 * aw:5 * aw])
        zm1, zm2 = _shift_rows(z, zp)
        cw0, cw1, cw2 = cw_ref[0:1, :], cw_ref[1:2, :], cw_ref[2:3, :]
        conv = cw0 * zm2 + cw1 * zm1 + cw2 * z
        gbv = gb_ref[...]
        _, ybh, rb = _rms_fwd(g_b * conv, gbv)
        dyb, dgb = _rms_bwd(dycat[:, aw:], ybh, rb, gbv)
        dgb_ref[...] += dgb
        dconv = dyb * g_b
        dcw_ref[0:1, :] += jnp.sum(dconv * zm2, axis=0, keepdims=True)
        dcw_ref[1:2, :] += jnp.sum(dconv * zm1, axis=0, keepdims=True)
        dcw_ref[2:3, :] += jnp.sum(dconv * z, axis=0, keepdims=True)
        nxt = dcn_ref[...]
        row = lax.broadcasted_iota(jnp.int32, dconv.shape, 0)
        dcp1 = jnp.where(row == TM - 1, nxt[0:1, :], pltpu.roll(dconv, TM - 1, 0))
        dcp2 = jnp.where(row == TM - 1, nxt[1:2, :],
                         jnp.where(row == TM - 2, nxt[0:1, :], pltpu.roll(dconv, TM - 2, 0)))
        dz = cw2 * dconv + cw1 * dcp1 + cw0 * dcp2
        dcn_ref[...] = dconv[0:8, :]
        dh_ref[:, 2 * aw:3 * aw] = (dyb * conv).astype(BF16)
        dh_ref[:, 3 * aw:4 * aw] = (dz * val).astype(BF16)
        dh_ref[:, 4 * aw:5 * aw] = (dz * g_c).astype(BF16)
        dxn = jnp.zeros((TM, d), F32)
        for k in range(N_CHIPS):
            dxn = dxn + _dot_nt(dh_ref[:, k * ns:(k + 1) * ns], win_ref[k])
        g1v = g1_ref[...]
        _, xh1, r1 = _rms_fwd(x_ref[...], g1v)
        dxr, dg1 = _rms_bwd(dxn, xh1, r1, g1v)
        dg1_ref[...] += dg1
        dx_ref[...] = dx1v + dxr

    ins = (x, dx1, h, h, mixed_all, th_all, g1, lng, lnb, wsp, b_t, cw, ga, gb, w_out, w_in)
    acc_shapes = [(1, d), (1, aw), (1, aw), wsp.shape, (CHUNK, CHUNK), cw.shape, (1, aw), (1, aw)]
    return _host_call(
        body, "mixer_bwd", (n,),
        [rev(d), rev(d), rev(nh), hprev, rev(aw), rev(d)] + [_full(a.shape) for a in ins[6:]],
        [rev(d), rev(nh)] + [_acc(sh) for sh in acc_shapes],
        (jax.ShapeDtypeStruct((s, d), F32), jax.ShapeDtypeStruct((s, nh), BF16))
        + tuple(jax.ShapeDtypeStruct(sh, F32) for sh in acc_shapes),
        [pltpu.VMEM((TM, aw), F32), pltpu.VMEM((8, aw), F32)],
        ins, ("arbitrary",), hosted)


def _weight_grad(a, b, name, tm, tn, col_sharded, hosted=None):
    t, m = a.shape
    n = b.shape[1]

    def body(a_ref, b_ref, o_ref):
        o_ref[...] = _dot_tn(a_ref[...].astype(BF16), b_ref[...].astype(BF16)).astype(BF16)

    if col_sharded:
        ns = n // N_CHIPS
        per = ns // tn
        out_shape = jax.ShapeDtypeStruct((N_CHIPS, m, ns), BF16)
        out_spec = pl.BlockSpec((None, tm, tn), lambda i, j: (j // per, i, j % per))
    else:
        out_shape = jax.ShapeDtypeStruct((m, n), BF16)
        out_spec = pl.BlockSpec((tm, tn), lambda i, j: (i, j))
    (out,), extra = _host_call(
        body, name, (m // tm, n // tn),
        [pl.BlockSpec((t, tm), lambda i, j: (0, i)), pl.BlockSpec((t, tn), lambda i, j: (0, j))],
        [out_spec], (out_shape,), [], (a, b), ("parallel", "parallel"), hosted)
    return (out if col_sharded else out.reshape(N_CHIPS, m // N_CHIPS, n)), extra


def _row_tile(rows, cap=256):
    best = None
    for t in range(16, min(rows, cap) + 1, 16):
        if rows % t == 0:
            best = t
    return best if best is not None else rows


def _adamw_math(w, g, m, v):
    m2 = ADAM_B1 * m + (1.0 - ADAM_B1) * g
    v2 = ADAM_B2 * v + (1.0 - ADAM_B2) * (g * g)
    m_hat = m2 / (1.0 - ADAM_B1 ** ADAM_STEP)
    v_hat = v2 / (1.0 - ADAM_B2 ** ADAM_STEP)
    delta = -ADAM_LR * (m_hat / (jnp.sqrt(v_hat) + ADAM_EPS) + ADAM_WD * w)
    return delta, m2, v2


def _adamw(w, g, m, v, name):
    r, c = w.shape
    tr = _row_tile(r) if r >= 16 else r

    def body(w_ref, g_ref, m_ref, v_ref, d_ref, m2_ref, v2_ref):
        d_ref[...], m2_ref[...], v2_ref[...] = _adamw_math(w_ref[...], g_ref[...], m_ref[...], v_ref[...])

    sh = jax.ShapeDtypeStruct((r, c), F32)
    return pl.pallas_call(
        body, name=name, grid=(r // tr,),
        in_specs=[_rows(tr, c)] * 4, out_specs=[_rows(tr, c)] * 3, out_shape=(sh, sh, sh),
        compiler_params=_params("parallel"),
    )(w, g, m, v)


def _finalize(items, place, name, hosted=None):
    r, c = items[0][2].shape
    tr = _row_tile(r)
    nw = len(items)

    def body(place_ref, *refs):
        ins, outs = refs[:7 * nw], refs[7 * nw:]
        for k in range(nw):
            own_ref, s0_ref, s1_ref, s2_ref, w_ref, m_ref, v_ref = ins[7 * k:7 * k + 7]
            g_ref, d_ref, m2_ref, v2_ref = outs[4 * k:4 * k + 4]
            g = own_ref[...].astype(F32) + s0_ref[...].astype(F32)
            g = (g + s1_ref[...].astype(F32)) + s2_ref[...].astype(F32)
            g_ref[...] = g
            d_ref[...], m2_ref[...], v2_ref[...] = _adamw_math(w_ref[...], g, m_ref[...], v_ref[...])

    def slot(k):
        return pl.BlockSpec((None, tr, c), lambda i, pref: (k, i, 0))

    rows = pl.BlockSpec((tr, c), lambda i, pref: (i, 0))
    sh = jax.ShapeDtypeStruct((r, c), F32)
    one = [pl.BlockSpec((None, tr, c), lambda i, pref: (pref[1], i, 0)), slot(0), slot(1), slot(2), rows, rows, rows]
    args = [a for part, slots, w, m, v in items for a in (part, slots, slots, slots, w, m, v)]
    res, extra = _host_call(body, name, (r // tr,), one * nw, [rows] * (4 * nw), (sh,) * (4 * nw), [], args,
                            ("parallel",), hosted, prefetch=(place,))
    return [res[4 * k:4 * k + 4] for k in range(nw)], extra


def _small_sum_adamw(parts, w, m, v):
    nd, r, c = parts.shape

    def body(p_ref, w_ref, m_ref, v_ref, g_ref, d_ref, m2_ref, v2_ref):
        g = p_ref[0]
        for k in range(1, nd):
            g = g + p_ref[k]
        g_ref[...] = g
        d_ref[...], m2_ref[...], v2_ref[...] = _adamw_math(w_ref[...], g, m_ref[...], v_ref[...])

    sh = jax.ShapeDtypeStruct((r, c), F32)
    return pl.pallas_call(
        body, name="small_sum_adamw", out_shape=(sh, sh, sh, sh),
        compiler_params=pltpu.CompilerParams(vmem_limit_bytes=VMEM_LIMIT),
    )(parts, w, m, v)


def _place():
    x, y, c = lax.axis_index("x"), lax.axis_index("y"), lax.axis_index("c")
    chips = [(1 - x, y), (x, 1 - y), (1 - x, 1 - y)]
    return x, y, c, 2 * x + y, chips


def _remote(src, dst, send_sem, recv_sem, to):
    return pltpu.make_async_remote_copy(src_ref=src, dst_ref=dst, send_sem=send_sem, recv_sem=recv_sem,
                                        device_id=to, device_id_type=MESH)


class _Exchange:
    def __init__(self, ins, out_shapes, sem_shape, start, finish, middle=None, in_place=False):
        self.ins, self.out_shapes, self.sem_shape = tuple(ins), tuple(out_shapes), sem_shape
        self.start, self.finish, self.middle = start, finish, middle
        self.in_place = in_place


def _run_exchange(ex, name):
    n_in, n_out = len(ex.ins), len(ex.out_shapes)

    def body(*refs):
        ins, outs = refs[:n_in], refs[n_in:n_in + n_out]
        send_sems, recv_sems = refs[n_in + n_out:]
        ex.start(ins, outs, send_sems, recv_sems)
        if ex.middle is not None:
            ex.middle(ins, outs, send_sems, recv_sems)
        ex.finish(ins, outs, send_sems, recv_sems)

    sem = pltpu.SemaphoreType.DMA(ex.sem_shape)
    return pl.pallas_call(
        body, name=name, out_shape=ex.out_shapes, in_specs=[ANY] * n_in, out_specs=[ANY] * n_out,
        input_output_aliases={k: k for k in range(n_in)} if ex.in_place else {}, scratch_shapes=[sem, sem],
    )(*ex.ins)


def _host_call(body, name, grid, in_specs, out_specs, out_shape, scratch_shapes, args, semantics, hosted,
               prefetch=()):
    hosted = [] if hosted is None else (list(hosted) if isinstance(hosted, (list, tuple)) else [hosted])
    n_pre, n_in, n_out, n_scr = len(prefetch), len(in_specs), len(out_specs), len(scratch_shapes)
    h_ins = [a for ex in hosted for a in ex.ins]
    h_outs = [s for ex in hosted for s in ex.out_shapes]
    h_in, h_out = len(h_ins), len(h_outs)

    def wrapped(*refs):
        pre, refs = refs[:n_pre], refs[n_pre:]
        a, hi = refs[:n_in], refs[n_in:n_in + h_in]
        o = refs[n_in + h_in:n_in + h_in + n_out]
        ho = refs[n_in + h_in + n_out:n_in + h_in + n_out + h_out]
        scr = refs[n_in + h_in + n_out + h_out:]

        def run(phase):
            i0 = o0 = 0
            for k, ex in enumerate(hosted):
                fn = getattr(ex, phase)
                if fn is not None:
                    fn(hi[i0:i0 + len(ex.ins)], ho[o0:o0 + len(ex.out_shapes)], scr[n_scr + 2 * k],
                       scr[n_scr + 2 * k + 1])
                i0, o0 = i0 + len(ex.ins), o0 + len(ex.out_shapes)

        if hosted and tuple(grid) == (1,):
            run("start")
            body(*pre, *a, *o, *scr[:n_scr])
            run("middle")
            run("finish")
            return

        if hosted:
            first = functools.reduce(jnp.logical_and, [pl.program_id(k) == 0 for k in range(len(grid))])

            @pl.when(first)
            def _():
                run("start")

        if any(ex.middle is not None for ex in hosted):
            half_way = functools.reduce(jnp.logical_and, [
                pl.program_id(0) == max(1, grid[0] * MIDDLE_STEP_16THS // 16)] + [
                pl.program_id(k) == 0 for k in range(1, len(grid))])

            @pl.when(half_way)
            def _():
                run("middle")

        body(*pre, *a, *o, *scr[:n_scr])

        if hosted:
            last = functools.reduce(jnp.logical_and, [pl.program_id(k) == grid[k] - 1 for k in range(len(grid))])

            @pl.when(last)
            def _():
                run("finish")

    sems = [pltpu.SemaphoreType.DMA(ex.sem_shape) for ex in hosted for _ in range(2)]
    aliases, i0, o0 = {}, n_pre + n_in, n_out
    for ex in hosted:
        if ex.in_place:
            aliases.update({i0 + k: o0 + k for k in range(len(ex.ins))})
        i0, o0 = i0 + len(ex.ins), o0 + len(ex.out_shapes)
    all_in, all_out = list(in_specs) + [ANY] * h_in, list(out_specs) + [ANY] * h_out
    all_scr = list(scratch_shapes) + sems
    params = _params(*(["arbitrary"] * len(grid) if hosted else semantics))
    shapes = tuple(out_shape) + tuple(h_outs)
    if n_pre:
        call = pl.pallas_call(
            wrapped, name=name, out_shape=shapes, input_output_aliases=aliases, compiler_params=params,
            grid_spec=pltpu.PrefetchScalarGridSpec(num_scalar_prefetch=n_pre, grid=grid, in_specs=all_in,
                                                   out_specs=all_out, scratch_shapes=all_scr))
    else:
        call = pl.pallas_call(
            wrapped, name=name, grid=grid, in_specs=all_in, out_specs=all_out, out_shape=shapes,
            scratch_shapes=all_scr, input_output_aliases=aliases, compiler_params=params)
    res = call(*prefetch, *args, *h_ins)
    return res[:n_out], res[n_out:]


def _all_gather(shards, small=()):
    items = tuple(shards) + tuple(small)
    nw = len(shards)

    def place():
        x, y, c, me, _ = _place()
        first = (x + (1 - c) * (1 - 2 * x), y + c * (1 - 2 * y))
        second = (x + c * (1 - 2 * x), y + (1 - c) * (1 - 2 * y))
        diag = (1 - x, 1 - y)
        return x, y, c, me, (first, second, diag)

    def halves(w, c):
        rh = items[w].shape[0] // 2
        return pl.ds(c * rh, rh), pl.ds((1 - c) * rh, rh)

    def start(ins, outs, ss, rs):
        x, y, c, me, chips = place()
        for w in range(len(items)):
            _remote(ins[w], outs[w].at[me], ss.at[w, 6], rs.at[w, 6], (x, y, 1 - c)).start()
            if w < nw:
                mine, _ = halves(w, c)
                _remote(ins[w].at[mine], outs[w].at[me, mine], ss.at[w, 0], rs.at[w, 0], (*chips[0], c)).start()
            else:
                for k in range(3):
                    _remote(ins[w], outs[w].at[me], ss.at[w, k], rs.at[w, k], (*chips[k], c)).start()

    def onward(outs, ss, rs, w, k, x, y, c, chips):
        mine, _ = halves(w, c)
        pk = 2 * chips[k][0] + chips[k][1]
        got = outs[w].at[pk, mine]
        src = chips[1] if k == 2 else chips[k]
        _remote(got, got, ss.at[w, k], rs.at[w, k], (*src, c)).wait_recv()
        if k == 0:
            _remote(got, got, ss.at[w, 2], rs.at[w, 2], (*chips[1], c)).start()
        _remote(got, got, ss.at[w, 3 + k], rs.at[w, 3 + k], (x, y, 1 - c)).start()

    def middle(ins, outs, ss, rs):
        x, y, c, me, chips = place()
        for w in range(nw):
            mine, _ = halves(w, c)
            _remote(ins[w].at[mine], outs[w].at[me, mine], ss.at[w, 1], rs.at[w, 1], (*chips[1], c)).start()
        for w in range(nw):
            onward(outs, ss, rs, w, 0, x, y, c, chips)

    def finish(ins, outs, ss, rs):
        x, y, c, me, chips = place()
        sib = (x, y, 1 - c)
        for k in (1, 2):
            for w in range(nw):
                onward(outs, ss, rs, w, k, x, y, c, chips)
        for w in range(len(items)):
            if w < nw:
                mine, theirs = halves(w, c)
                for k, chip in ((3, chips[1]), (4, chips[0]), (5, chips[2])):
                    oth = outs[w].at[2 * chip[0] + chip[1], theirs]
                    _remote(oth, oth, ss.at[w, k], rs.at[w, k], sib).wait_recv()
                own = ins[w].at[mine]
                for k in range(6):
                    _remote(own, own, ss.at[w, k], rs.at[w, k], sib).wait_send()
            else:
                for k in range(3):
                    got = outs[w].at[2 * chips[k][0] + chips[k][1]]
                    _remote(got, got, ss.at[w, k], rs.at[w, k], (*chips[k], c)).wait_recv()
                    _remote(ins[w], ins[w], ss.at[w, k], rs.at[w, k], sib).wait_send()
            _remote(ins[w], outs[w].at[me], ss.at[w, 6], rs.at[w, 6], sib).wait()

    out_shapes = tuple(jax.ShapeDtypeStruct((N_CHIPS,) + a.shape, a.dtype) for a in items)
    return _Exchange(items, out_shapes, (len(items), 7), start, finish, middle if nw else None)


def _chip_reduce(grads, name):
    nw = len(grads)
    step = 64

    def body(*refs):
        ins, outs = refs[:nw], refs[nw:2 * nw]
        own, got = refs[2 * nw:3 * nw], refs[3 * nw:4 * nw]
        send_sems, recv_sems, local_sems = refs[4 * nw:]
        x, y, c, _, _ = _place()
        moves = []
        for w in range(nw):
            rh = grads[w].shape[1] // 2
            away = _remote(ins[w].at[:, pl.ds((1 - c) * rh, rh), :], got[w], send_sems.at[w], recv_sems.at[w],
                           (x, y, 1 - c))
            mine = pltpu.make_async_copy(ins[w].at[:, pl.ds(c * rh, rh), :], own[w], local_sems.at[w])
            away.start()
            mine.start()
            moves.append((away, mine))
        back = []
        for w, (away, mine) in enumerate(moves):
            nb, rh, _ = own[w].shape
            mine.wait()
            away.wait()
            for k in range(nb):
                def add(i, carry, w=w, k=k):
                    rows = pl.ds(pl.multiple_of(i * step, step), step)
                    own[w][k, rows, :] = (own[w][k, rows, :].astype(F32) + got[w][k, rows, :].astype(F32)).astype(BF16)
                    return carry
                lax.fori_loop(0, rh // step, add, 0)
                tail = rh % step
                if tail:
                    rows = slice(rh - tail, rh)
                    own[w][k, rows, :] = (own[w][k, rows, :].astype(F32) + got[w][k, rows, :].astype(F32)).astype(BF16)
            wb = pltpu.make_async_copy(own[w], outs[w].at[:, pl.ds(c * rh, rh), :], local_sems.at[w])
            wb.start()
            back.append(wb)
        for wb in back:
            wb.wait()

    halves = [pltpu.VMEM((g.shape[0], g.shape[1] // 2, g.shape[2]), BF16) for g in grads]
    sem = pltpu.SemaphoreType.DMA((nw,))
    return pl.pallas_call(
        body, name=name, out_shape=tuple(jax.ShapeDtypeStruct(g.shape, BF16) for g in grads),
        in_specs=[ANY] * nw, out_specs=[ANY] * nw, scratch_shapes=halves + halves + [sem, sem, sem],
        compiler_params=pltpu.CompilerParams(vmem_limit_bytes=VMEM_LIMIT),
    )(*grads)


def _scatter_partials(parts):
    nw = len(parts)

    def copies(ins, outs, ss, rs):
        _, _, c, _, chips = _place()
        res = []
        for r, (px, py) in enumerate(chips):
            for w in range(nw):
                rh = parts[w].shape[1] // 2
                rows = pl.ds(c * rh, rh)
                res.append(_remote(ins[w].at[2 * px + py, rows], outs[w].at[r, rows], ss.at[w, r], rs.at[w, r],
                                   (px, py, c)))
        return res

    def start(ins, outs, ss, rs):
        for cp in copies(ins, outs, ss, rs):
            cp.start()

    def finish(ins, outs, ss, rs):
        for cp in copies(ins, outs, ss, rs):
            cp.wait()

    out_shapes = tuple(jax.ShapeDtypeStruct((3,) + p.shape[1:], p.dtype) for p in parts)
    return _Exchange(parts, out_shapes, (nw, 3), start, finish)


def _join_partials(parts, slots):
    nw = len(parts)

    def copies(outs, ss, rs, mine):
        x, y, c, me, _ = _place()
        res = []
        for w in range(nw):
            rh = parts[w].shape[1] // 2
            rows = pl.ds((c if mine else 1 - c) * rh, rh)
            own = outs[w].at[me, rows]
            got = outs[nw + w].at[:, rows, :]
            res.append(_remote(own, own, ss.at[w, 0], rs.at[w, 0], (x, y, 1 - c)))
            res.append(_remote(got, got, ss.at[w, 1], rs.at[w, 1], (x, y, 1 - c)))
        return res

    def start(ins, outs, ss, rs):
        for cp in copies(outs, ss, rs, True):
            cp.start()

    def finish(ins, outs, ss, rs):
        for cp in copies(outs, ss, rs, True):
            cp.wait_send()
        for cp in copies(outs, ss, rs, False):
            cp.wait_recv()

    arrays = tuple(parts) + tuple(slots)
    return _Exchange(arrays, tuple(jax.ShapeDtypeStruct(a.shape, a.dtype) for a in arrays), (nw, 2), start, finish,
                     in_place=True)


def _gather_small(slab):
    def copies(ins, outs, ss, rs):
        x, y, c, _, _ = _place()
        me = 4 * x + 2 * y + c
        out, arrivals = [], []
        for k in range(1, 8):
            px = 1 - x if k & 4 else x
            py = 1 - y if k & 2 else y
            pc = 1 - c if k & 1 else c
            out.append(_remote(ins[0], outs[0].at[me], ss.at[k - 1], rs.at[k - 1], (px, py, pc)))
            theirs = outs[0].at[4 * px + 2 * py + pc]
            arrivals.append((theirs, k - 1, (px, py, pc)))
        return pltpu.make_async_copy(ins[0], outs[0].at[me], ss.at[7]), out, arrivals

    def start(ins, outs, ss, rs):
        own, out, _ = copies(ins, outs, ss, rs)
        own.start()
        for cp in out:
            cp.start()

    def finish(ins, outs, ss, rs):
        own, out, arrivals = copies(ins, outs, ss, rs)
        for cp in out:
            cp.wait_send()
        for theirs, k, peer in arrivals:
            _remote(theirs, theirs, ss.at[k], rs.at[k], peer).wait_recv()
        own.wait()

    return _Exchange((slab,), (jax.ShapeDtypeStruct((8,) + slab.shape, slab.dtype),), (8,), start, finish)


_SMALL_VECS = ("ln_mix_g", "ln_attn_g", "ln_mem_g", "ln_ffn_g", "ln_final_g")


def _pack_small(p, extra, conv):
    d = p["ln_mix_g"].shape[-1]
    top = [p[k].reshape(1, d) for k in _SMALL_VECS]
    top.append(jnp.concatenate([p["sgu_ln_g"].reshape(-1), p["sgu_ln_b"].reshape(-1)]).reshape(1, d))
    top.append(jnp.concatenate([p["grp_norm_a"].reshape(-1), p["grp_norm_b"].reshape(-1)]).reshape(1, d))
    top.append(jnp.concatenate([p["b_spatial"].reshape(-1), extra]).reshape(1, d))
    mid = jnp.zeros((8, d), F32)
    if conv is not None:
        mid = jnp.pad(conv, ((0, 5), (0, d - conv.shape[1])))
    return jnp.concatenate([jnp.concatenate(top, axis=0), mid, p["w_spatial"].reshape(-1, d)], axis=0)


def _unpack_small(slab):
    d = slab.shape[1]
    hw = d // 2
    out = {k: slab[i] for i, k in enumerate(_SMALL_VECS)}
    out["sgu_ln_g"], out["sgu_ln_b"] = slab[5, :hw], slab[5, hw:]
    out["grp_norm_a"], out["grp_norm_b"] = slab[6, :hw], slab[6, hw:]
    out["b_spatial"] = slab[7, :hw].reshape(HEADS, CHUNK)
    out["w_spatial"] = slab[16:].reshape(HEADS, CHUNK, CHUNK)
    return out


_BIG = ("w_in", "w_kv", "w_gate_up", "w_out", "w_q", "w_o", "w_down")
_WEIGHTS = ("ln_mix_g", "w_in", "sgu_ln_g", "sgu_ln_b", "w_spatial", "b_spatial", "conv_w", "grp_norm_a",
            "grp_norm_b", "w_out", "ln_attn_g", "ln_mem_g", "w_q", "w_kv", "w_o", "ln_ffn_g", "w_gate_up",
            "w_down", "ln_final_g")


def _step(p, m_, v_, x, mem, target):
    s, d = x.shape
    hw = d // 2
    row = lambda a: a.reshape(1, -1)
    x_, y_, c_ = lax.axis_index("x"), lax.axis_index("y"), lax.axis_index("c")
    chip = 2 * x_ + y_

    conv8 = jnp.pad(p["conv_w"], ((0, 5), (0, 0)))
    later = ("w_kv", "w_q", "w_o", "w_down", "w_gate_up")
    casts, (w_in, w_out4, conv4) = _cast_bf16(
        [p[k] for k in later], hosted=_all_gather([p["w_in"].astype(BF16), p["w_out"].astype(BF16)], [conv8]))
    bf = dict(zip(later, casts))
    cw = jnp.transpose(conv4[:, :3, :], (1, 0, 2)).reshape(3, hw)
    b_t = jnp.pad(jnp.transpose(p["b_spatial"]), ((0, 0), (0, CHUNK - HEADS)))
    g1, g2, gm, g3, gf = (row(p[k]) for k in _SMALL_VECS)
    lng, lnb, ga, gb = row(p["sgu_ln_g"]), row(p["sgu_ln_b"]), row(p["grp_norm_a"]), row(p["grp_norm_b"])
    wsp = p["w_spatial"]
    w_out = w_out4.reshape(-1, d)

    (h, x1, ycat, xn1, mixed, th), (w_kv, w_q4, w_o4, w_down4) = _mixer_fwd(
        x, g1, w_in, lng, lnb, wsp, b_t, cw, ga, gb, w_out,
        hosted=_all_gather([bf[k] for k in ("w_kv", "w_q", "w_o", "w_down")]))
    w_q, w_o, w_down = (a.reshape(-1, d) for a in (w_q4, w_o4, w_down4))
    memn, kv = _kv_fwd(mem, gm, w_kv)
    (x2, o, qs, probs), (w_gu,) = _attn_fwd(x1, g2, w_q, kv, w_o, hosted=_all_gather([bf["w_gate_up"]]))
    dx2, act, dgu, xn3, dx3, loss, dgf, dg3 = _ffn_fwd_bwd(x2, g3, gf, target, w_gu, w_down)

    place = jnp.stack([c_, chip]).astype(jnp.int32)

    def chip_partials(names, grads, tag):
        return list(_chip_reduce(grads, "chip_reduce_" + tag))

    names_d = ("w_down",)
    parts_d = chip_partials(names_d, (_weight_grad(act, dx3, "grad_w_down", 1408, 512, False)[0],), "down")
    g_gu, slots_d = _weight_grad(xn3, dgu, "grad_w_gate_up", 512, 1408, True, hosted=_scatter_partials(parts_d))
    names_a = ("w_gate_up",)
    parts_a = chip_partials(names_a, (g_gu,), "ffn")
    (dx1, dkv, dg2, g_o, g_q, g_out), slots_a = _attn_bwd(x1, dx2, o, ycat, qs, probs, g2, w_q, kv, w_o,
                                                           hosted=_scatter_partials(parts_a))
    g_kv, dgm = _kv_bwd(dkv, mem, memn, gm, w_kv)
    names_b = ("w_o", "w_out", "w_q", "w_kv")
    shard_major = lambda g: g.reshape(N_CHIPS, -1, d)
    parts_b = chip_partials(names_b, (shard_major(g_o), shard_major(g_out), shard_major(g_q), g_kv), "attn")
    names_da = names_d + names_a
    (dx, dh, dg1, dlng, dlnb, dwsp, dbt, dcw, dga, dgb), extra = _mixer_bwd(
        x, dx1, h, mixed, th, g1, lng, lnb, wsp, b_t, cw, ga, gb, w_out, w_in,
        hosted=[_scatter_partials(parts_b), _join_partials(parts_d + parts_a, slots_d + slots_a)])
    slots_b, joined = extra[:len(names_b)], extra[len(names_b):]
    whole = dict(zip(names_da, zip(joined[:len(names_da)], joined[len(names_da):])))
    small = {"ln_mix_g": dg1, "ln_attn_g": dg2, "ln_mem_g": dgm, "ln_ffn_g": dg3, "ln_final_g": dgf,
             "sgu_ln_g": dlng, "sgu_ln_b": dlnb, "grp_norm_a": dga, "grp_norm_b": dgb,
             "b_spatial": jnp.transpose(dbt[:, :HEADS]), "w_spatial": dwsp}
    loss_vec = jnp.pad(loss.reshape(1), (0, hw - 1))
    g_in, extra = _weight_grad(
        xn1, dh, "grad_w_in", 1024, 640, True,
        hosted=[_gather_small(_pack_small(small, loss_vec, dcw)), _join_partials(parts_b, slots_b)])
    parts = extra[0]
    whole.update(zip(names_b, zip(extra[1:1 + len(names_b)], extra[1 + len(names_b):])))
    (part_in,) = chip_partials(("w_in",), (g_in,), "mixer")
    out_g, out_d, out_m, out_v = {}, {}, {}, {}

    def finalize(ks, tag, hosted=None):
        done, res = _finalize([(whole[k][0], whole[k][1], p[k], m_[k], v_[k]) for k in ks], place,
                              "finalize_" + tag, hosted)
        for k, (g, dl, nm, nv) in zip(ks, done):
            out_g[k], out_d[k], out_m[k], out_v[k] = g, dl, nm, nv
        return res

    (slots_in,) = finalize(("w_down",), "w_down", _scatter_partials([part_in]))
    finalize(("w_o", "w_out", "w_q"), "attn")
    for k in ("w_gate_up", "w_kv"):
        finalize((k,), k)
    whole["w_in"] = _run_exchange(_join_partials([part_in], [slots_in]), "rs_join_mixer")
    finalize(("w_in",), "w_in")

    zeros = jnp.zeros((hw,), F32)
    sg, sd, sm, sv = _small_sum_adamw(parts, _pack_small(p, zeros, None), _pack_small(m_, zeros, None),
                                      _pack_small(v_, zeros, None))
    for tree, slab in zip((out_g, out_d, out_m, out_v), (sg, sd, sm, sv)):
        tree.update(_unpack_small(slab))
    loss_out = sg[7, hw]
    g_conv = lax.dynamic_slice(sg[8:11, :hw], (0, chip * (hw // N_CHIPS)), (3, hw // N_CHIPS))
    out_g["conv_w"] = g_conv
    out_d["conv_w"], out_m["conv_w"], out_v["conv_w"] = _adamw(p["conv_w"], g_conv, m_["conv_w"], v_["conv_w"],
                                                                "adamw_conv_w")
    return loss_out, dx, out_g, out_d, out_m, out_v


def kernel(x, mem, ln_mix_g, w_in, sgu_ln_g, sgu_ln_b, w_spatial, b_spatial, conv_w, grp_norm_a, grp_norm_b, w_out, ln_attn_g, ln_mem_g, w_q, w_kv, w_o, ln_ffn_g, w_gate_up, w_down, ln_final_g, loss_target, m_ln_mix_g, m_w_in, m_sgu_ln_g, m_sgu_ln_b, m_w_spatial, m_b_spatial, m_conv_w, m_grp_norm_a, m_grp_norm_b, m_w_out, m_ln_attn_g, m_ln_mem_g, m_w_q, m_w_kv, m_w_o, m_ln_ffn_g, m_w_gate_up, m_w_down, m_ln_final_g, v_ln_mix_g, v_w_in, v_sgu_ln_g, v_sgu_ln_b, v_w_spatial, v_b_spatial, v_conv_w, v_grp_norm_a, v_grp_norm_b, v_w_out, v_ln_attn_g, v_ln_mem_g, v_w_q, v_w_kv, v_w_o, v_ln_ffn_g, v_w_gate_up, v_w_down, v_ln_final_g):
    p = dict(ln_mix_g=ln_mix_g, w_in=w_in, sgu_ln_g=sgu_ln_g, sgu_ln_b=sgu_ln_b, w_spatial=w_spatial,
             b_spatial=b_spatial, conv_w=conv_w, grp_norm_a=grp_norm_a, grp_norm_b=grp_norm_b, w_out=w_out,
             ln_attn_g=ln_attn_g, ln_mem_g=ln_mem_g, w_q=w_q, w_kv=w_kv, w_o=w_o, ln_ffn_g=ln_ffn_g,
             w_gate_up=w_gate_up, w_down=w_down, ln_final_g=ln_final_g)
    m_ = dict(ln_mix_g=m_ln_mix_g, w_in=m_w_in, sgu_ln_g=m_sgu_ln_g, sgu_ln_b=m_sgu_ln_b, w_spatial=m_w_spatial,
              b_spatial=m_b_spatial, conv_w=m_conv_w, grp_norm_a=m_grp_norm_a, grp_norm_b=m_grp_norm_b,
              w_out=m_w_out, ln_attn_g=m_ln_attn_g, ln_mem_g=m_ln_mem_g, w_q=m_w_q, w_kv=m_w_kv, w_o=m_w_o,
              ln_ffn_g=m_ln_ffn_g, w_gate_up=m_w_gate_up, w_down=m_w_down, ln_final_g=m_ln_final_g)
    v_ = dict(ln_mix_g=v_ln_mix_g, w_in=v_w_in, sgu_ln_g=v_sgu_ln_g, sgu_ln_b=v_sgu_ln_b, w_spatial=v_w_spatial,
              b_spatial=v_b_spatial, conv_w=v_conv_w, grp_norm_a=v_grp_norm_a, grp_norm_b=v_grp_norm_b,
              w_out=v_w_out, ln_attn_g=v_ln_attn_g, ln_mem_g=v_ln_mem_g, w_q=v_w_q, w_kv=v_w_kv, w_o=v_w_o,
              ln_ffn_g=v_ln_ffn_g, w_gate_up=v_w_gate_up, w_down=v_w_down, ln_final_g=v_ln_final_g)
    s, d = x.shape[-2], x.shape[-1]
    loss, dx, g, dl, nm, nv = _step(p, m_, v_, x.reshape(s, d), mem.reshape(-1, d), loss_target.reshape(s, d))
    outs = [loss, dx.reshape(x.shape)]
    for tree in (g, dl, nm, nv):
        outs += [tree[k].reshape(p[k].shape) for k in _WEIGHTS]
    return tuple(outs)
```

```python
import functools
import math

import jax
import jax.numpy as jnp
from jax import lax
from jax.experimental import pallas as pl
from jax.experimental.pallas import tpu as pltpu

F32 = jnp.float32
BF16 = jnp.bfloat16
EPS = 1e-6
CHUNK = 128
HEADS = 4
N_CHIPS = 4
TM = 512
TM_ATTN = 512
TM_FFN = 256
ADAM_LR, ADAM_B1, ADAM_B2, ADAM_EPS, ADAM_WD, ADAM_STEP = 0.001, 0.9, 0.999, 1e-08, 0.01, 10
GELU_C = math.sqrt(2.0 / math.pi)
GELU_K = 0.044715
SMALL_ROWS = 80
VMEM_LIMIT = 56 * 1024 * 1024
MIDDLE_STEP_16THS = 7
MESH = pl.DeviceIdType.MESH
ANY = pl.BlockSpec(memory_space=pl.ANY)


def _params(*sem):
    return pltpu.CompilerParams(dimension_semantics=sem, vmem_limit_bytes=VMEM_LIMIT)


def _dot(a, b):
    return jnp.dot(a, b, preferred_element_type=F32)


def _dot_nt(a, b):
    return lax.dot_general(a, b, (((1,), (1,)), ((), ())), preferred_element_type=F32)


def _dot_tn(a, b):
    return lax.dot_general(a, b, (((0,), (0,)), ((), ())), preferred_element_type=F32)


def _rms_fwd(x, g):
    r = lax.rsqrt(jnp.mean(x * x, axis=-1, keepdims=True) + EPS)
    xh = x * r
    return xh * g, xh, r


def _rms_bwd(dy, xh, r, g):
    dxh = dy * g
    dx = r * (dxh - xh * jnp.mean(dxh * xh, axis=-1, keepdims=True))
    return dx, jnp.sum(dy * xh, axis=0, keepdims=True)


def _full(shape):
    nd = len(shape)
    return pl.BlockSpec(shape, lambda *_: (0,) * nd, pipeline_mode=pl.Buffered(1))


def _acc(shape):
    nd = len(shape)
    return pl.BlockSpec(shape, lambda *_: (0,) * nd)


def _rows(tm, cols):
    return pl.BlockSpec((tm, cols), lambda i: (i, 0))


def _tril_weights(wsp_ref):
    row = lax.broadcasted_iota(jnp.int32, (CHUNK, CHUNK), 0)
    col = lax.broadcasted_iota(jnp.int32, (CHUNK, CHUNK), 1)
    return [jnp.where(row >= col, wsp_ref[hd], 0.0).astype(BF16) for hd in range(HEADS)]


def _shift_rows(z, zp):
    row = lax.broadcasted_iota(jnp.int32, z.shape, 0)
    zm1 = jnp.where(row == 0, zp[7:8, :], pltpu.roll(z, 1, 0))
    zm2 = jnp.where(row == 0, zp[6:7, :], jnp.where(row == 1, zp[7:8, :], pltpu.roll(z, 2, 0)))
    return zm1, zm2


def _gelu_parts(x):
    t = jnp.tanh(GELU_C * (x + GELU_K * (x * x * x)))
    return 0.5 * x * (1.0 + t), t


def _layer_norm_parts(v, g, b):
    mu = jnp.mean(v, axis=-1, keepdims=True)
    vc = v - mu
    rs = lax.rsqrt(jnp.mean(vc * vc, axis=-1, keepdims=True) + EPS)
    vhat = vc * rs
    return vhat * g + b, vhat, rs


def _kv_fwd(mem, g_mem, w_kv):
    m, d = mem.shape
    ns = w_kv.shape[2]

    def body(mem_ref, g_ref, w_ref, memn_ref, kv_ref):
        y, _, _ = _rms_fwd(mem_ref[...], g_ref[...])
        yb = y.astype(BF16)
        memn_ref[...] = yb
        for k in range(N_CHIPS):
            kv_ref[:, k * ns:(k + 1) * ns] = _dot(yb, w_ref[k]).astype(BF16)

    return pl.pallas_call(
        body, name="kv_fwd",
        out_shape=(jax.ShapeDtypeStruct((m, d), BF16), jax.ShapeDtypeStruct((m, N_CHIPS * ns), BF16)),
        compiler_params=pltpu.CompilerParams(vmem_limit_bytes=VMEM_LIMIT),
    )(mem, g_mem, w_kv)


def _mixer_fwd(x, g1, w_in, lng, lnb, wsp, b_t, cw, ga, gb, w_out, hosted=None):
    s, d = x.shape
    n = s // TM
    nch = TM // CHUNK
    ns = w_in.shape[2]
    nh = N_CHIPS * ns
    aw = d // 2
    hd_w = aw // HEADS

    def body(x_ref, g1_ref, win_ref, lng_ref, lnb_ref, wsp_ref, bt_ref, cw_ref, ga_ref, gb_ref, wout_ref,
             h_ref, x1_ref, y_ref, xn_ref, mix_ref, th_ref, zp_ref):
        i = pl.program_id(0)

        @pl.when(i == 0)
        def _():
            zp_ref[...] = jnp.zeros_like(zp_ref)

        x = x_ref[...]
        xn, _, _ = _rms_fwd(x, g1_ref[...])
        xnb = xn.astype(BF16)
        xn_ref[...] = xnb
        for k in range(N_CHIPS):
            h_ref[:, k * ns:(k + 1) * ns] = _dot(xnb, win_ref[k])
        a, th = _gelu_parts(h_ref[:, 0:2 * aw])
        th_ref[...] = th
        u = a[:, :aw]
        vn, _, _ = _layer_norm_parts(a[:, aw:], lng_ref[...], lnb_ref[...])
        vnb = vn.astype(BF16)
        wm = _tril_weights(wsp_ref)
        for c in range(nch):
            for hd in range(HEADS):
                blk = vnb[c * CHUNK:(c + 1) * CHUNK, hd * hd_w:(hd + 1) * hd_w]
                mix_ref[c * CHUNK:(c + 1) * CHUNK, hd * hd_w:(hd + 1) * hd_w] = _dot(wm[hd], blk) + bt_ref[:, hd:hd + 1]
        ya, _, _ = _rms_fwd(u * mix_ref[...], ga_ref[...])
        g_b = h_ref[:, 2 * aw:3 * aw]
        z = h_ref[:, 3 * aw:4 * aw] * h_ref[:, 4 * aw:5 * aw]
        zm1, zm2 = _shift_rows(z, zp_ref[...])
        conv = cw_ref[0:1, :] * zm2 + cw_ref[1:2, :] * zm1 + cw_ref[2:3, :] * z
        yb, _, _ = _rms_fwd(g_b * conv, gb_ref[...])
        zp_ref[...] = z[TM - 8:TM, :]
        ycat = jnp.concatenate([ya, yb], axis=-1).astype(BF16)
        y_ref[...] = ycat
        x1_ref[...] = x + _dot(ycat, wout_ref[...])

    return _host_call(
        body, "mixer_fwd", (n,),
        [_rows(TM, d), _full(g1.shape), _full(w_in.shape), _full(lng.shape), _full(lnb.shape),
         _full(wsp.shape), _full(b_t.shape), _full(cw.shape), _full(ga.shape), _full(gb.shape),
         _full(w_out.shape)],
        [_rows(TM, nh), _rows(TM, d), _rows(TM, d), _rows(TM, d), _rows(TM, aw), _rows(TM, d)],
        (jax.ShapeDtypeStruct((s, nh), F32), jax.ShapeDtypeStruct((s, d), F32),
         jax.ShapeDtypeStruct((s, d), BF16), jax.ShapeDtypeStruct((s, d), BF16),
         jax.ShapeDtypeStruct((s, aw), F32), jax.ShapeDtypeStruct((s, d), F32)),
        [pltpu.VMEM((8, aw), F32)],
        (x, g1, w_in, lng, lnb, wsp, b_t, cw, ga, gb, w_out), ("arbitrary",), hosted)


def _attn_fwd(x1, g2, w_q, kv, w_o, hosted=None):
    s, d = x1.shape
    tm = min(TM_ATTN, s)
    n = s // tm
    dh = d // HEADS
    m = kv.shape[0]
    scale = dh ** -0.5

    def body(x1_ref, g2_ref, wq_ref, kv_ref, wo_ref, x2_ref, o_ref, q_ref, p_ref):
        x1v = x1_ref[...]
        xn, _, _ = _rms_fwd(x1v, g2_ref[...])
        q_ref[...] = _dot(xn.astype(BF16), wq_ref[...]).astype(BF16)
        for hd in range(HEADS):
            kh = kv_ref[:, hd * dh:(hd + 1) * dh]
            vh = kv_ref[:, d + hd * dh:d + (hd + 1) * dh]
            sc = _dot_nt(q_ref[:, hd * dh:(hd + 1) * dh], kh) * scale
            e = jnp.exp(sc - jnp.max(sc, axis=-1, keepdims=True))
            p = e / jnp.sum(e, axis=-1, keepdims=True)
            p_ref[:, hd * m:(hd + 1) * m] = p
            o_ref[:, hd * dh:(hd + 1) * dh] = _dot(p.astype(BF16), vh).astype(BF16)
        x2_ref[...] = x1v + _dot(o_ref[...], wo_ref[...])

    return _host_call(
        body, "attn_fwd", (n,),
        [_rows(tm, d), _full(g2.shape), _full(w_q.shape), _full(kv.shape), _full(w_o.shape)],
        [_rows(tm, d), _rows(tm, d), _rows(tm, d), _rows(tm, HEADS * m)],
        (jax.ShapeDtypeStruct((s, d), F32), jax.ShapeDtypeStruct((s, d), BF16), jax.ShapeDtypeStruct((s, d), BF16),
         jax.ShapeDtypeStruct((s, HEADS * m), F32)),
        [], (x1, g2, w_q, kv, w_o), ("parallel",), hosted)


def _ffn_fwd_bwd(x2, g3, gf, target, w_gu, w_down):
    s, d = x2.shape
    tm = min(TM_FFN, s)
    n = s // tm
    ns = w_gu.shape[2]
    ff = 2 * ns

    def body(x2_ref, g3_ref, gf_ref, t_ref, wgu_ref, wd_ref,
             dx2_ref, act_ref, dgu_ref, xn_ref, dx3_ref, loss_ref, dgf_ref, dg3_ref):
        i = pl.program_id(0)

        @pl.when(i == 0)
        def _():
            loss_ref[...] = jnp.zeros_like(loss_ref)
            dgf_ref[...] = jnp.zeros_like(dgf_ref)
            dg3_ref[...] = jnp.zeros_like(dg3_ref)

        x2v = x2_ref[...]
        xn, xh3, r3 = _rms_fwd(x2v, g3_ref[...])
        xnb = xn.astype(BF16)
        xn_ref[...] = xnb
        x3 = x2v
        saved = []
        for j in range(2):
            g = _dot(xnb, wgu_ref[j])
            u = _dot(xnb, wgu_ref[2 + j])
            sg = 1.0 / (1.0 + jnp.exp(-g))
            sl = g * sg
            actb = (sl * u).astype(BF16)
            act_ref[:, j * ns:(j + 1) * ns] = actb
            x3 = x3 + _dot(actb, wd_ref[j * ns:(j + 1) * ns, :])
            saved.append((u, sl, sg * (1.0 + g * (1.0 - sg))))
        gfv = gf_ref[...]
        y, xhf, rf = _rms_fwd(x3, gfv)
        e = y - t_ref[...]
        loss_ref[...] += 0.5 * jnp.sum(jnp.sum(e * e, axis=-1, keepdims=True), axis=0, keepdims=True) / d
        dx3, dgf = _rms_bwd(e / d, xhf, rf, gfv)
        dgf_ref[...] += dgf
        dx3b = dx3.astype(BF16)
        dx3_ref[...] = dx3b
        dxn = jnp.zeros_like(x2v)
        for j in range(2):
            u, sl, dsl = saved[j]
            dact = _dot_nt(dx3b, wd_ref[j * ns:(j + 1) * ns, :])
            dgb = (dact * u * dsl).astype(BF16)
            dub = (dact * sl).astype(BF16)
            dgu_ref[:, j * ns:(j + 1) * ns] = dgb
            dgu_ref[:, ff + j * ns:ff + (j + 1) * ns] = dub
            dxn = dxn + _dot_nt(dgb, wgu_ref[j]) + _dot_nt(dub, wgu_ref[2 + j])
        dxr, dg3 = _rms_bwd(dxn, xh3, r3, g3_ref[...])
        dg3_ref[...] += dg3
        dx2_ref[...] = dx3 + dxr

    vec = jax.ShapeDtypeStruct((1, d), F32)
    return pl.pallas_call(
        body, name="ffn_fwd_bwd", grid=(n,),
        in_specs=[_rows(tm, d), _full(g3.shape), _full(gf.shape), _rows(tm, d), _full(w_gu.shape),
                  _full(w_down.shape)],
        out_specs=[_rows(tm, d), _rows(tm, ff), _rows(tm, 2 * ff), _rows(tm, d), _rows(tm, d),
                   _acc((1, 1)), _acc((1, d)), _acc((1, d))],
        out_shape=(jax.ShapeDtypeStruct((s, d), F32), jax.ShapeDtypeStruct((s, ff), BF16),
                   jax.ShapeDtypeStruct((s, 2 * ff), BF16), jax.ShapeDtypeStruct((s, d), BF16),
                   jax.ShapeDtypeStruct((s, d), BF16), jax.ShapeDtypeStruct((1, 1), F32), vec, vec),
        compiler_params=_params("arbitrary"),
    )(x2, g3, gf, target, w_gu, w_down)


def _attn_bwd(x1, dx2, o, ycat, qs, probs, g2, w_q, kv, w_o, hosted=None):
    s, d = x1.shape
    tm = min(TM_ATTN, s)
    n = s // tm
    dh = d // HEADS
    scale = dh ** -0.5
    m = kv.shape[0]

    def body(x1_ref, dx2_ref, o_ref, y_ref, q_ref, p_ref, g2_ref, wq_ref, kv_ref, wo_ref,
             dx1_ref, dkv_ref, dg2_ref, gwo_out, gwq_out, gwout_out, dq_ref, gwo_ref, gwq_ref, gwout_ref):
        i = pl.program_id(0)

        @pl.when(i == 0)
        def _():
            for r in (dkv_ref, dg2_ref, gwo_ref, gwq_ref, gwout_ref):
                r[...] = jnp.zeros_like(r)

        xn, xh2, r2 = _rms_fwd(x1_ref[...], g2_ref[...])
        xnb = xn.astype(BF16)
        dx2v = dx2_ref[...]
        dx2b = dx2v.astype(BF16)
        gwo_ref[...] += _dot_tn(o_ref[...], dx2b)
        do = _dot_nt(dx2b, wo_ref[...])
        for hd in range(HEADS):
            qb = q_ref[:, hd * dh:(hd + 1) * dh]
            p = p_ref[:, hd * m:(hd + 1) * m]
            kh = kv_ref[:, hd * dh:(hd + 1) * dh]
            vh = kv_ref[:, d + hd * dh:d + (hd + 1) * dh]
            dob = do[:, hd * dh:(hd + 1) * dh].astype(BF16)
            dp = _dot_nt(dob, vh)
            ds = p * (dp - jnp.sum(dp * p, axis=-1, keepdims=True))
            dsb = (ds * scale).astype(BF16)
            dq_ref[:, hd * dh:(hd + 1) * dh] = _dot(dsb, kh).astype(BF16)
            dkv_ref[:, hd * dh:(hd + 1) * dh] += _dot_tn(dsb, qb)
            dkv_ref[:, d + hd * dh:d + (hd + 1) * dh] += _dot_tn(p.astype(BF16), dob)
        dqb = dq_ref[...]
        gwq_ref[...] += _dot_tn(xnb, dqb)
        dxn = _dot_nt(dqb, wq_ref[...])
        dxr, dg2 = _rms_bwd(dxn, xh2, r2, g2_ref[...])
        dg2_ref[...] += dg2
        dx1 = dx2v + dxr
        dx1_ref[...] = dx1
        gwout_ref[...] += _dot_tn(y_ref[...], dx1.astype(BF16))

        @pl.when(i == n - 1)
        def _():
            for acc, out in ((gwo_ref, gwo_out), (gwq_ref, gwq_out), (gwout_ref, gwout_out)):
                out[...] = acc[...].astype(BF16)

    sq = jax.ShapeDtypeStruct((d, d), BF16)
    return _host_call(
        body, "attn_bwd", (n,),
        [_rows(tm, d), _rows(tm, d), _rows(tm, d), _rows(tm, d), _rows(tm, d), _rows(tm, HEADS * m),
         _full(g2.shape), _full(w_q.shape), _full(kv.shape), _full(w_o.shape)],
        [_rows(tm, d), _acc((m, 2 * d)), _acc((1, d)), _acc((d, d)), _acc((d, d)), _acc((d, d))],
        (jax.ShapeDtypeStruct((s, d), F32), jax.ShapeDtypeStruct((m, 2 * d), F32),
         jax.ShapeDtypeStruct((1, d), F32), sq, sq, sq),
        [pltpu.VMEM((tm, d), BF16)] + [pltpu.VMEM((d, d), F32)] * 3,
        (x1, dx2, o, ycat, qs, probs, g2, w_q, kv, w_o), ("arbitrary",), hosted)


def _kv_bwd(dkv, mem, memn, g_mem, w_kv):
    m, d = mem.shape
    ns = w_kv.shape[2]

    def body(dkv_ref, mem_ref, memn_ref, g_ref, w_ref, gw_ref, dg_ref):
        _, xh, _ = _rms_fwd(mem_ref[...], g_ref[...])
        dmemn = jnp.zeros((m, d), F32)
        for k in range(N_CHIPS):
            dkb = dkv_ref[:, k * ns:(k + 1) * ns].astype(BF16)
            gw_ref[k] = _dot_tn(memn_ref[...], dkb).astype(BF16)
            dmemn = dmemn + _dot_nt(dkb, w_ref[k])
        dg_ref[...] = jnp.sum(dmemn * xh, axis=0, keepdims=True)

    return pl.pallas_call(
        body, name="kv_bwd",
        out_shape=(jax.ShapeDtypeStruct((N_CHIPS, d, ns), BF16), jax.ShapeDtypeStruct((1, d), F32)),
        compiler_params=pltpu.CompilerParams(vmem_limit_bytes=VMEM_LIMIT),
    )(dkv, mem, memn, g_mem, w_kv)


def _mixer_bwd(x, dx1, h, mixed_all, th_all, g1, lng, lnb, wsp, b_t, cw, ga, gb, w_out, w_in, hosted=None):
    s, d = x.shape
    n = s // TM
    nch = TM // CHUNK
    ns = w_in.shape[2]
    nh = N_CHIPS * ns
    aw = d // 2
    hd_w = aw // HEADS

    def rev(cols):
        return pl.BlockSpec((TM, cols), lambda i: (n - 1 - i, 0))

    hprev = pl.BlockSpec((8, nh), lambda i: (jnp.maximum((n - 1 - i) * (TM // 8) - 1, 0), 0))

    def body(x_ref, dx1_ref, h_ref, hp_ref, mix_ref, th_ref, g1_ref, lng_ref, lnb_ref, wsp_ref, bt_ref, cw_ref,
             ga_ref, gb_ref, wout_ref, win_ref,
             dx_ref, dh_ref, dg1_ref, dlng_ref, dlnb_ref, dwsp_ref, dbt_ref, dcw_ref, dga_ref, dgb_ref,
             dvn_ref, dcn_ref):
        i = pl.program_id(0)

        @pl.when(i == 0)
        def _():
            for r in (dg1_ref, dlng_ref, dlnb_ref, dwsp_ref, dbt_ref, dcw_ref, dga_ref, dgb_ref, dcn_ref):
                r[...] = jnp.zeros_like(r)

        dx1v = dx1_ref[...]
        dycat = _dot_nt(dx1v.astype(BF16), wout_ref[...])
        ha = h_ref[:, 0:2 * aw]
        th = th_ref[...]
        a = 0.5 * ha * (1.0 + th)
        u = a[:, :aw]
        lngv = lng_ref[...]
        vn, vhat, rs = _layer_norm_parts(a[:, aw:], lngv, lnb_ref[...])
        vnb = vn.astype(BF16)
        wm = _tril_weights(wsp_ref)
        mixed = mix_ref[...]
        gav = ga_ref[...]
        _, yah, ra = _rms_fwd(u * mixed, gav)
        dya, dga = _rms_bwd(dycat[:, :aw], yah, ra, gav)
        dga_ref[...] += dga
        du = dya * mixed
        dmix = dya * u
        dmb = dmix.astype(BF16)
        tri = lax.broadcasted_iota(jnp.int32, (CHUNK, CHUNK), 0) >= lax.broadcasted_iota(jnp.int32, (CHUNK, CHUNK), 1)
        for hd in range(HEADS):
            dw = jnp.zeros((CHUNK, CHUNK), F32)
            db = jnp.zeros((CHUNK, 1), F32)
            for c in range(nch):
                rows = slice(c * CHUNK, (c + 1) * CHUNK)
                cols = slice(hd * hd_w, (hd + 1) * hd_w)
                dvn_ref[rows, cols] = _dot_tn(wm[hd], dmb[rows, cols])
                dw = dw + _dot_nt(dmb[rows, cols], vnb[rows, cols])
                db = db + jnp.sum(dmix[rows, cols], axis=1, keepdims=True)
            dwsp_ref[hd] += jnp.where(tri, dw, 0.0)
            dbt_ref[:, hd:hd + 1] += db
        dvn = dvn_ref[...]
        dlng_ref[...] += jnp.sum(dvn * vhat, axis=0, keepdims=True)
        dlnb_ref[...] += jnp.sum(dvn, axis=0, keepdims=True)
        dvh = dvn * lngv
        dv = rs * (dvh - jnp.mean(dvh, axis=-1, keepdims=True) - vhat * jnp.mean(dvh * vhat, axis=-1, keepdims=True))
        gprime = 0.5 * (1.0 + th) + 0.5 * ha * (1.0 - th * th) * (GELU_C * (1.0 + 3.0 * GELU_K * (ha * ha)))
        dh_ref[:, 0:2 * aw] = (jnp.concatenate([du, dv], axis=-1) * gprime).astype(BF16)
        g_b = h_ref[:, 2 * aw:3 * aw]
        g_c = h_ref[:, 3 * aw:4 * aw]
        val = h_ref[:, 4 * aw:5 * aw]
        z = g_c * val
        zp = jnp.where(i == n - 1, 0.0, hp_ref[:, 3 * aw:4 * aw] * hp_ref[:, 4 * aw:5 * aw])
        zm1, zm2 = _shift_rows(z, zp)
        cw0, cw1, cw2 = cw_ref[0:1, :], cw_ref[1:2, :], cw_ref[2:3, :]
        conv = cw0 * zm2 + cw1 * zm1 + cw2 * z
        gbv = gb_ref[...]
        _, ybh, rb = _rms_fwd(g_b * conv, gbv)
        dyb, dgb = _rms_bwd(dycat[:, aw:], ybh, rb, gbv)
        dgb_ref[...] += dgb
        dconv = dyb * g_b
        dcw_ref[0:1, :] += jnp.sum(dconv * zm2, axis=0, keepdims=True)
        dcw_ref[1:2, :] += jnp.sum(dconv * zm1, axis=0, keepdims=True)
        dcw_ref[2:3, :] += jnp.sum(dconv * z, axis=0, keepdims=True)
        nxt = dcn_ref[...]
        row = lax.broadcasted_iota(jnp.int32, dconv.shape, 0)
        dcp1 = jnp.where(row == TM - 1, nxt[0:1, :], pltpu.roll(dconv, TM - 1, 0))
        dcp2 = jnp.where(row == TM - 1, nxt[1:2, :],
                         jnp.where(row == TM - 2, nxt[0:1, :], pltpu.roll(dconv, TM - 2, 0)))
        dz = cw2 * dconv + cw1 * dcp1 + cw0 * dcp2
        dcn_ref[...] = dconv[0:8, :]
        dh_ref[:, 2 * aw:3 * aw] = (dyb * conv).astype(BF16)
        dh_ref[:, 3 * aw:4 * aw] = (dz * val).astype(BF16)
        dh_ref[:, 4 * aw:5 * aw] = (dz * g_c).astype(BF16)
        dxn = jnp.zeros((TM, d), F32)
        for k in range(N_CHIPS):
            dxn = dxn + _dot_nt(dh_ref[:, k * ns:(k + 1) * ns], win_ref[k])
        g1v = g1_ref[...]
        _, xh1, r1 = _rms_fwd(x_ref[...], g1v)
        dxr, dg1 = _rms_bwd(dxn, xh1, r1, g1v)
        dg1_ref[...] += dg1
        dx_ref[...] = dx1v + dxr

    ins = (x, dx1, h, h, mixed_all, th_all, g1, lng, lnb, wsp, b_t, cw, ga, gb, w_out, w_in)
    acc_shapes = [(1, d), (1, aw), (1, aw), wsp.shape, (CHUNK, CHUNK), cw.shape, (1, aw), (1, aw)]
    return _host_call(
        body, "mixer_bwd", (n,),
        [rev(d), rev(d), rev(nh), hprev, rev(aw), rev(d)] + [_full(a.shape) for a in ins[6:]],
        [rev(d), rev(nh)] + [_acc(sh) for sh in acc_shapes],
        (jax.ShapeDtypeStruct((s, d), F32), jax.ShapeDtypeStruct((s, nh), BF16))
        + tuple(jax.ShapeDtypeStruct(sh, F32) for sh in acc_shapes),
        [pltpu.VMEM((TM, aw), F32), pltpu.VMEM((8, aw), F32)],
        ins, ("arbitrary",), hosted)


def _weight_grad(a, b, name, tm, tn, col_sharded, hosted=None):
    t, m = a.shape
    n = b.shape[1]

    def body(a_ref, b_ref, o_ref):
        o_ref[...] = _dot_tn(a_ref[...].astype(BF16), b_ref[...].astype(BF16)).astype(BF16)

    if col_sharded:
        ns = n // N_CHIPS
        per = ns // tn
        out_shape = jax.ShapeDtypeStruct((N_CHIPS, m, ns), BF16)
        out_spec = pl.BlockSpec((None, tm, tn), lambda i, j: (j // per, i, j % per))
    else:
        out_shape = jax.ShapeDtypeStruct((m, n), BF16)
        out_spec = pl.BlockSpec((tm, tn), lambda i, j: (i, j))
    (out,), extra = _host_call(
        body, name, (m // tm, n // tn),
        [pl.BlockSpec((t, tm), lambda i, j: (0, i)), pl.BlockSpec((t, tn), lambda i, j: (0, j))],
        [out_spec], (out_shape,), [], (a, b), ("parallel", "parallel"), hosted)
    return (out if col_sharded else out.reshape(N_CHIPS, m // N_CHIPS, n)), extra


def _row_tile(rows, cap=256):
    best = None
    for t in range(16, min(rows, cap) + 1, 16):
        if rows % t == 0:
            best = t
    return best if best is not None else rows


def _adamw_math(w, g, m, v):
    m2 = ADAM_B1 * m + (1.0 - ADAM_B1) * g
    v2 = ADAM_B2 * v + (1.0 - ADAM_B2) * (g * g)
    m_hat = m2 / (1.0 - ADAM_B1 ** ADAM_STEP)
    v_hat = v2 / (1.0 - ADAM_B2 ** ADAM_STEP)
    delta = -ADAM_LR * (m_hat / (jnp.sqrt(v_hat) + ADAM_EPS) + ADAM_WD * w)
    return delta, m2, v2


def _adamw(w, g, m, v, name):
    r, c = w.shape
    tr = _row_tile(r) if r >= 16 else r

    def body(w_ref, g_ref, m_ref, v_ref, d_ref, m2_ref, v2_ref):
        d_ref[...], m2_ref[...], v2_ref[...] = _adamw_math(w_ref[...], g_ref[...], m_ref[...], v_ref[...])

    sh = jax.ShapeDtypeStruct((r, c), F32)
    return pl.pallas_call(
        body, name=name, grid=(r // tr,),
        in_specs=[_rows(tr, c)] * 4, out_specs=[_rows(tr, c)] * 3, out_shape=(sh, sh, sh),
        compiler_params=_params("parallel"),
    )(w, g, m, v)


def _finalize(items, place, name, hosted=None):
    r, c = items[0][2].shape
    tr = _row_tile(r)
    nw = len(items)

    def body(place_ref, *refs):
        ins, outs = refs[:7 * nw], refs[7 * nw:]
        for k in range(nw):
            own_ref, s0_ref, s1_ref, s2_ref, w_ref, m_ref, v_ref = ins[7 * k:7 * k + 7]
            g_ref, d_ref, m2_ref, v2_ref = outs[4 * k:4 * k + 4]
            g = own_ref[...].astype(F32) + s0_ref[...].astype(F32)
            g = (g + s1_ref[...].astype(F32)) + s2_ref[...].astype(F32)
            g_ref[...] = g
            d_ref[...], m2_ref[...], v2_ref[...] = _adamw_math(w_ref[...], g, m_ref[...], v_ref[...])

    def slot(k):
        return pl.BlockSpec((None, tr, c), lambda i, pref: (k, i, 0))

    rows = pl.BlockSpec((tr, c), lambda i, pref: (i, 0))
    sh = jax.ShapeDtypeStruct((r, c), F32)
    one = [pl.BlockSpec((None, tr, c), lambda i, pref: (pref[1], i, 0)), slot(0), slot(1), slot(2), rows, rows, rows]
    args = [a for part, slots, w, m, v in items for a in (part, slots, slots, slots, w, m, v)]
    res, extra = _host_call(body, name, (r // tr,), one * nw, [rows] * (4 * nw), (sh,) * (4 * nw), [], args,
                            ("parallel",), hosted, prefetch=(place,))
    return [res[4 * k:4 * k + 4] for k in range(nw)], extra


def _small_sum_adamw(parts, w, m, v):
    nd, r, c = parts.shape

    def body(p_ref, w_ref, m_ref, v_ref, g_ref, d_ref, m2_ref, v2_ref):
        g = p_ref[0]
        for k in range(1, nd):
            g = g + p_ref[k]
        g_ref[...] = g
        d_ref[...], m2_ref[...], v2_ref[...] = _adamw_math(w_ref[...], g, m_ref[...], v_ref[...])

    sh = jax.ShapeDtypeStruct((r, c), F32)
    return pl.pallas_call(
        body, name="small_sum_adamw", out_shape=(sh, sh, sh, sh),
        compiler_params=pltpu.CompilerParams(vmem_limit_bytes=VMEM_LIMIT),
    )(parts, w, m, v)


def _place():
    x, y, c = lax.axis_index("x"), lax.axis_index("y"), lax.axis_index("c")
    chips = [(1 - x, y), (x, 1 - y), (1 - x, 1 - y)]
    return x, y, c, 2 * x + y, chips


def _remote(src, dst, send_sem, recv_sem, to):
    return pltpu.make_async_remote_copy(src_ref=src, dst_ref=dst, send_sem=send_sem, recv_sem=recv_sem,
                                        device_id=to, device_id_type=MESH)


class _Exchange:
    def __init__(self, ins, out_shapes, sem_shape, start, finish, middle=None, in_place=False):
        self.ins, self.out_shapes, self.sem_shape = tuple(ins), tuple(out_shapes), sem_shape
        self.start, self.finish, self.middle = start, finish, middle
        self.in_place = in_place


def _run_exchange(ex, name):
    n_in, n_out = len(ex.ins), len(ex.out_shapes)

    def body(*refs):
        ins, outs = refs[:n_in], refs[n_in:n_in + n_out]
        send_sems, recv_sems = refs[n_in + n_out:]
        ex.start(ins, outs, send_sems, recv_sems)
        if ex.middle is not None:
            ex.middle(ins, outs, send_sems, recv_sems)
        ex.finish(ins, outs, send_sems, recv_sems)

    sem = pltpu.SemaphoreType.DMA(ex.sem_shape)
    return pl.pallas_call(
        body, name=name, out_shape=ex.out_shapes, in_specs=[ANY] * n_in, out_specs=[ANY] * n_out,
        input_output_aliases={k: k for k in range(n_in)} if ex.in_place else {}, scratch_shapes=[sem, sem],
    )(*ex.ins)


def _host_call(body, name, grid, in_specs, out_specs, out_shape, scratch_shapes, args, semantics, hosted,
               prefetch=()):
    hosted = [] if hosted is None else (list(hosted) if isinstance(hosted, (list, tuple)) else [hosted])
    n_pre, n_in, n_out, n_scr = len(prefetch), len(in_specs), len(out_specs), len(scratch_shapes)
    h_ins = [a for ex in hosted for a in ex.ins]
    h_outs = [s for ex in hosted for s in ex.out_shapes]
    h_in, h_out = len(h_ins), len(h_outs)

    def wrapped(*refs):
        pre, refs = refs[:n_pre], refs[n_pre:]
        a, hi = refs[:n_in], refs[n_in:n_in + h_in]
        o = refs[n_in + h_in:n_in + h_in + n_out]
        ho = refs[n_in + h_in + n_out:n_in + h_in + n_out + h_out]
        scr = refs[n_in + h_in + n_out + h_out:]

        def run(phase):
            i0 = o0 = 0
            for k, ex in enumerate(hosted):
                fn = getattr(ex, phase)
                if fn is not None:
                    fn(hi[i0:i0 + len(ex.ins)], ho[o0:o0 + len(ex.out_shapes)], scr[n_scr + 2 * k],
                       scr[n_scr + 2 * k + 1])
                i0, o0 = i0 + len(ex.ins), o0 + len(ex.out_shapes)

        if hosted:
            first = functools.reduce(jnp.logical_and, [pl.program_id(k) == 0 for k in range(len(grid))])

            @pl.when(first)
            def _():
                run("start")

        if any(ex.middle is not None for ex in hosted):
            half_way = functools.reduce(jnp.logical_and, [
                pl.program_id(0) == max(1, grid[0] * MIDDLE_STEP_16THS // 16)] + [
                pl.program_id(k) == 0 for k in range(1, len(grid))])

            @pl.when(half_way)
            def _():
                run("middle")

        body(*pre, *a, *o, *scr[:n_scr])

        if hosted:
            last = functools.reduce(jnp.logical_and, [pl.program_id(k) == grid[k] - 1 for k in range(len(grid))])

            @pl.when(last)
            def _():
                run("finish")

    sems = [pltpu.SemaphoreType.DMA(ex.sem_shape) for ex in hosted for _ in range(2)]
    aliases, i0, o0 = {}, n_pre + n_in, n_out
    for ex in hosted:
        if ex.in_place:
            aliases.update({i0 + k: o0 + k for k in range(len(ex.ins))})
        i0, o0 = i0 + len(ex.ins), o0 + len(ex.out_shapes)
    all_in, all_out = list(in_specs) + [ANY] * h_in, list(out_specs) + [ANY] * h_out
    all_scr = list(scratch_shapes) + sems
    params = _params(*(["arbitrary"] * len(grid) if hosted else semantics))
    shapes = tuple(out_shape) + tuple(h_outs)
    if n_pre:
        call = pl.pallas_call(
            wrapped, name=name, out_shape=shapes, input_output_aliases=aliases, compiler_params=params,
            grid_spec=pltpu.PrefetchScalarGridSpec(num_scalar_prefetch=n_pre, grid=grid, in_specs=all_in,
                                                   out_specs=all_out, scratch_shapes=all_scr))
    else:
        call = pl.pallas_call(
            wrapped, name=name, grid=grid, in_specs=all_in, out_specs=all_out, out_shape=shapes,
            scratch_shapes=all_scr, input_output_aliases=aliases, compiler_params=params)
    res = call(*prefetch, *args, *h_ins)
    return res[:n_out], res[n_out:]


def _all_gather(shards, small=()):
    items = tuple(shards) + tuple(small)
    nw = len(shards)

    def place():
        x, y, c, me, _ = _place()
        first = (x + (1 - c) * (1 - 2 * x), y + c * (1 - 2 * y))
        second = (x + c * (1 - 2 * x), y + (1 - c) * (1 - 2 * y))
        diag = (1 - x, 1 - y)
        return x, y, c, me, (first, second, diag)

    def halves(w, c):
        rh = items[w].shape[0] // 2
        return pl.ds(c * rh, rh), pl.ds((1 - c) * rh, rh)

    def start(ins, outs, ss, rs):
        x, y, c, me, chips = place()
        for w in range(len(items)):
            _remote(ins[w], outs[w].at[me], ss.at[w, 6], rs.at[w, 6], (x, y, 1 - c)).start()
            if w < nw:
                mine, _ = halves(w, c)
                _remote(ins[w].at[mine], outs[w].at[me, mine], ss.at[w, 0], rs.at[w, 0], (*chips[0], c)).start()
            else:
                for k in range(3):
                    _remote(ins[w], outs[w].at[me], ss.at[w, k], rs.at[w, k], (*chips[k], c)).start()

    def onward(outs, ss, rs, w, k, x, y, c, chips):
        mine, _ = halves(w, c)
        pk = 2 * chips[k][0] + chips[k][1]
        got = outs[w].at[pk, mine]
        src = chips[1] if k == 2 else chips[k]
        _remote(got, got, ss.at[w, k], rs.at[w, k], (*src, c)).wait_recv()
        if k == 0:
            _remote(got, got, ss.at[w, 2], rs.at[w, 2], (*chips[1], c)).start()
        _remote(got, got, ss.at[w, 3 + k], rs.at[w, 3 + k], (x, y, 1 - c)).start()

    def middle(ins, outs, ss, rs):
        x, y, c, me, chips = place()
        for w in range(nw):
            mine, _ = halves(w, c)
            _remote(ins[w].at[mine], outs[w].at[me, mine], ss.at[w, 1], rs.at[w, 1], (*chips[1], c)).start()
        for w in range(nw):
            onward(outs, ss, rs, w, 0, x, y, c, chips)

    def finish(ins, outs, ss, rs):
        x, y, c, me, chips = place()
        sib = (x, y, 1 - c)
        for k in (1, 2):
            for w in range(nw):
                onward(outs, ss, rs, w, k, x, y, c, chips)
        for w in range(len(items)):
            if w < nw:
                mine, theirs = halves(w, c)
                for k, chip in ((3, chips[1]), (4, chips[0]), (5, chips[2])):
                    oth = outs[w].at[2 * chip[0] + chip[1], theirs]
                    _remote(oth, oth, ss.at[w, k], rs.at[w, k], sib).wait_recv()
                own = ins[w].at[mine]
                for k in range(6):
                    _remote(own, own, ss.at[w, k], rs.at[w, k], sib).wait_send()
            else:
                for k in range(3):
                    got = outs[w].at[2 * chips[k][0] + chips[k][1]]
                    _remote(got, got, ss.at[w, k], rs.at[w, k], (*chips[k], c)).wait_recv()
                    _remote(ins[w], ins[w], ss.at[w, k], rs.at[w, k], sib).wait_send()
            _remote(ins[w], outs[w].at[me], ss.at[w, 6], rs.at[w, 6], sib).wait()

    out_shapes = tuple(jax.ShapeDtypeStruct((N_CHIPS,) + a.shape, a.dtype) for a in items)
    return _Exchange(items, out_shapes, (len(items), 7), start, finish, middle if nw else None)


def _chip_reduce(grads, name, collective_id):
    nw = len(grads)
    step = 64

    def body(*refs):
        ins, outs = refs[:nw], refs[nw:2 * nw]
        own, got = refs[2 * nw:3 * nw], refs[3 * nw:4 * nw]
        send_sems, recv_sems, local_sems = refs[4 * nw:]
        x, y, c, _, _ = _place()
        barrier = pltpu.get_barrier_semaphore()
        pl.semaphore_signal(barrier, inc=1, device_id=(x, y, 1 - c), device_id_type=MESH)
        pl.semaphore_wait(barrier, 1)
        moves = []
        for w in range(nw):
            rh = grads[w].shape[1] // 2
            away = _remote(ins[w].at[:, pl.ds((1 - c) * rh, rh), :], got[w], send_sems.at[w], recv_sems.at[w],
                           (x, y, 1 - c))
            mine = pltpu.make_async_copy(ins[w].at[:, pl.ds(c * rh, rh), :], own[w], local_sems.at[w])
            away.start()
            mine.start()
            moves.append((away, mine))
        back = []
        for w, (away, mine) in enumerate(moves):
            nb, rh, _ = own[w].shape
            mine.wait()
            away.wait()
            for k in range(nb):
                def add(i, carry, w=w, k=k):
                    rows = pl.ds(pl.multiple_of(i * step, step), step)
                    own[w][k, rows, :] = (own[w][k, rows, :].astype(F32) + got[w][k, rows, :].astype(F32)).astype(BF16)
                    return carry
                lax.fori_loop(0, rh // step, add, 0)
                tail = rh % step
                if tail:
                    rows = slice(rh - tail, rh)
                    own[w][k, rows, :] = (own[w][k, rows, :].astype(F32) + got[w][k, rows, :].astype(F32)).astype(BF16)
            wb = pltpu.make_async_copy(own[w], outs[w].at[:, pl.ds(c * rh, rh), :], local_sems.at[w])
            wb.start()
            back.append(wb)
        for wb in back:
            wb.wait()

    halves = [pltpu.VMEM((g.shape[0], g.shape[1] // 2, g.shape[2]), BF16) for g in grads]
    sem = pltpu.SemaphoreType.DMA((nw,))
    return pl.pallas_call(
        body, name=name, out_shape=tuple(jax.ShapeDtypeStruct(g.shape, BF16) for g in grads),
        in_specs=[ANY] * nw, out_specs=[ANY] * nw, scratch_shapes=halves + halves + [sem, sem, sem],
        compiler_params=pltpu.CompilerParams(vmem_limit_bytes=VMEM_LIMIT, collective_id=collective_id),
    )(*grads)


def _scatter_partials(parts):
    nw = len(parts)

    def copies(ins, outs, ss, rs):
        _, _, c, _, chips = _place()
        res = []
        for r, (px, py) in enumerate(chips):
            for w in range(nw):
                rh = parts[w].shape[1] // 2
                rows = pl.ds(c * rh, rh)
                res.append(_remote(ins[w].at[2 * px + py, rows], outs[w].at[r, rows], ss.at[w, r], rs.at[w, r],
                                   (px, py, c)))
        return res

    def start(ins, outs, ss, rs):
        for cp in copies(ins, outs, ss, rs):
            cp.start()

    def finish(ins, outs, ss, rs):
        for cp in copies(ins, outs, ss, rs):
            cp.wait()

    out_shapes = tuple(jax.ShapeDtypeStruct((3,) + p.shape[1:], p.dtype) for p in parts)
    return _Exchange(parts, out_shapes, (nw, 3), start, finish)


def _join_partials(parts, slots):
    nw = len(parts)

    def copies(outs, ss, rs, mine):
        x, y, c, me, _ = _place()
        res = []
        for w in range(nw):
            rh = parts[w].shape[1] // 2
            rows = pl.ds((c if mine else 1 - c) * rh, rh)
            own = outs[w].at[me, rows]
            got = outs[nw + w].at[:, rows, :]
            res.append(_remote(own, own, ss.at[w, 0], rs.at[w, 0], (x, y, 1 - c)))
            res.append(_remote(got, got, ss.at[w, 1], rs.at[w, 1], (x, y, 1 - c)))
        return res

    def start(ins, outs, ss, rs):
        for cp in copies(outs, ss, rs, True):
            cp.start()

    def finish(ins, outs, ss, rs):
        for cp in copies(outs, ss, rs, True):
            cp.wait_send()
        for cp in copies(outs, ss, rs, False):
            cp.wait_recv()

    arrays = tuple(parts) + tuple(slots)
    return _Exchange(arrays, tuple(jax.ShapeDtypeStruct(a.shape, a.dtype) for a in arrays), (nw, 2), start, finish,
                     in_place=True)


def _gather_small(slab):
    def copies(ins, outs, ss, rs):
        x, y, c, _, _ = _place()
        me = 4 * x + 2 * y + c
        out, arrivals = [], []
        for k in range(1, 8):
            px = 1 - x if k & 4 else x
            py = 1 - y if k & 2 else y
            pc = 1 - c if k & 1 else c
            out.append(_remote(ins[0], outs[0].at[me], ss.at[k - 1], rs.at[k - 1], (px, py, pc)))
            theirs = outs[0].at[4 * px + 2 * py + pc]
            arrivals.append((theirs, k - 1, (px, py, pc)))
        return pltpu.make_async_copy(ins[0], outs[0].at[me], ss.at[7]), out, arrivals

    def start(ins, outs, ss, rs):
        own, out, _ = copies(ins, outs, ss, rs)
        own.start()
        for cp in out:
            cp.start()

    def finish(ins, outs, ss, rs):
        own, out, arrivals = copies(ins, outs, ss, rs)
        for cp in out:
            cp.wait_send()
        for theirs, k, peer in arrivals:
            _remote(theirs, theirs, ss.at[k], rs.at[k], peer).wait_recv()
        own.wait()

    return _Exchange((slab,), (jax.ShapeDtypeStruct((8,) + slab.shape, slab.dtype),), (8,), start, finish)


_SMALL_VECS = ("ln_mix_g", "ln_attn_g", "ln_mem_g", "ln_ffn_g", "ln_final_g")


def _pack_small(p, extra, conv):
    d = p["ln_mix_g"].shape[-1]
    top = [p[k].reshape(1, d) for k in _SMALL_VECS]
    top.append(jnp.concatenate([p["sgu_ln_g"].reshape(-1), p["sgu_ln_b"].reshape(-1)]).reshape(1, d))
    top.append(jnp.concatenate([p["grp_norm_a"].reshape(-1), p["grp_norm_b"].reshape(-1)]).reshape(1, d))
    top.append(jnp.concatenate([p["b_spatial"].reshape(-1), extra]).reshape(1, d))
    mid = jnp.zeros((8, d), F32)
    if conv is not None:
        mid = jnp.pad(conv, ((0, 5), (0, d - conv.shape[1])))
    return jnp.concatenate([jnp.concatenate(top, axis=0), mid, p["w_spatial"].reshape(-1, d)], axis=0)


def _unpack_small(slab):
    d = slab.shape[1]
    hw = d // 2
    out = {k: slab[i] for i, k in enumerate(_SMALL_VECS)}
    out["sgu_ln_g"], out["sgu_ln_b"] = slab[5, :hw], slab[5, hw:]
    out["grp_norm_a"], out["grp_norm_b"] = slab[6, :hw], slab[6, hw:]
    out["b_spatial"] = slab[7, :hw].reshape(HEADS, CHUNK)
    out["w_spatial"] = slab[16:].reshape(HEADS, CHUNK, CHUNK)
    return out


_BIG = ("w_in", "w_kv", "w_gate_up", "w_out", "w_q", "w_o", "w_down")
_WEIGHTS = ("ln_mix_g", "w_in", "sgu_ln_g", "sgu_ln_b", "w_spatial", "b_spatial", "conv_w", "grp_norm_a",
            "grp_norm_b", "w_out", "ln_attn_g", "ln_mem_g", "w_q", "w_kv", "w_o", "ln_ffn_g", "w_gate_up",
            "w_down", "ln_final_g")


def _step(p, m_, v_, x, mem, target):
    s, d = x.shape
    hw = d // 2
    row = lambda a: a.reshape(1, -1)
    x_, y_, c_ = lax.axis_index("x"), lax.axis_index("y"), lax.axis_index("c")
    chip = 2 * x_ + y_

    bf = {k: p[k].astype(BF16) for k in _BIG}
    conv8 = jnp.pad(p["conv_w"], ((0, 5), (0, 0)))
    w_in, w_out4, conv4 = _run_exchange(_all_gather([bf["w_in"], bf["w_out"]], [conv8]), "all_gather_mixer")
    cw = jnp.transpose(conv4[:, :3, :], (1, 0, 2)).reshape(3, hw)
    b_t = jnp.pad(jnp.transpose(p["b_spatial"]), ((0, 0), (0, CHUNK - HEADS)))
    g1, g2, gm, g3, gf = (row(p[k]) for k in _SMALL_VECS)
    lng, lnb, ga, gb = row(p["sgu_ln_g"]), row(p["sgu_ln_b"]), row(p["grp_norm_a"]), row(p["grp_norm_b"])
    wsp = p["w_spatial"]
    w_out = w_out4.reshape(-1, d)

    (h, x1, ycat, xn1, mixed, th), (w_kv, w_q4, w_o4, w_down4) = _mixer_fwd(
        x, g1, w_in, lng, lnb, wsp, b_t, cw, ga, gb, w_out,
        hosted=_all_gather([bf[k] for k in ("w_kv", "w_q", "w_o", "w_down")]))
    w_q, w_o, w_down = (a.reshape(-1, d) for a in (w_q4, w_o4, w_down4))
    memn, kv = _kv_fwd(mem, gm, w_kv)
    (x2, o, qs, probs), (w_gu,) = _attn_fwd(x1, g2, w_q, kv, w_o, hosted=_all_gather([bf["w_gate_up"]]))
    dx2, act, dgu, xn3, dx3, loss, dgf, dg3 = _ffn_fwd_bwd(x2, g3, gf, target, w_gu, w_down)

    place = jnp.stack([c_, chip]).astype(jnp.int32)

    def chip_partials(names, grads, tag):
        return list(_chip_reduce(grads, "chip_reduce_" + tag, ("down", "ffn", "attn", "mixer").index(tag)))

    names_d = ("w_down",)
    parts_d = chip_partials(names_d, (_weight_grad(act, dx3, "grad_w_down", 1408, 512, False)[0],), "down")
    g_gu, slots_d = _weight_grad(xn3, dgu, "grad_w_gate_up", 512, 1408, True, hosted=_scatter_partials(parts_d))
    names_a = ("w_gate_up",)
    parts_a = chip_partials(names_a, (g_gu,), "ffn")
    (dx1, dkv, dg2, g_o, g_q, g_out), slots_a = _attn_bwd(x1, dx2, o, ycat, qs, probs, g2, w_q, kv, w_o,
                                                           hosted=_scatter_partials(parts_a))
    g_kv, dgm = _kv_bwd(dkv, mem, memn, gm, w_kv)
    names_b = ("w_o", "w_out", "w_q", "w_kv")
    shard_major = lambda g: g.reshape(N_CHIPS, -1, d)
    parts_b = chip_partials(names_b, (shard_major(g_o), shard_major(g_out), shard_major(g_q), g_kv), "attn")
    names_da = names_d + names_a
    (dx, dh, dg1, dlng, dlnb, dwsp, dbt, dcw, dga, dgb), extra = _mixer_bwd(
        x, dx1, h, mixed, th, g1, lng, lnb, wsp, b_t, cw, ga, gb, w_out, w_in,
        hosted=[_scatter_partials(parts_b), _join_partials(parts_d + parts_a, slots_d + slots_a)])
    slots_b, joined = extra[:len(names_b)], extra[len(names_b):]
    whole = dict(zip(names_da, zip(joined[:len(names_da)], joined[len(names_da):])))
    small = {"ln_mix_g": dg1, "ln_attn_g": dg2, "ln_mem_g": dgm, "ln_ffn_g": dg3, "ln_final_g": dgf,
             "sgu_ln_g": dlng, "sgu_ln_b": dlnb, "grp_norm_a": dga, "grp_norm_b": dgb,
             "b_spatial": jnp.transpose(dbt[:, :HEADS]), "w_spatial": dwsp}
    loss_vec = jnp.pad(loss.reshape(1), (0, hw - 1))
    g_in, extra = _weight_grad(
        xn1, dh, "grad_w_in", 1024, 640, True,
        hosted=[_gather_small(_pack_small(small, loss_vec, dcw)), _join_partials(parts_b, slots_b)])
    parts = extra[0]
    whole.update(zip(names_b, zip(extra[1:1 + len(names_b)], extra[1 + len(names_b):])))
    (part_in,) = chip_partials(("w_in",), (g_in,), "mixer")
    out_g, out_d, out_m, out_v = {}, {}, {}, {}

    def finalize(ks, tag, hosted=None):
        done, res = _finalize([(whole[k][0], whole[k][1], p[k], m_[k], v_[k]) for k in ks], place,
                              "finalize_" + tag, hosted)
        for k, (g, dl, nm, nv) in zip(ks, done):
            out_g[k], out_d[k], out_m[k], out_v[k] = g, dl, nm, nv
        return res

    (slots_in,) = finalize(("w_down",), "w_down", _scatter_partials([part_in]))
    finalize(("w_o", "w_out", "w_q"), "attn")
    for k in ("w_gate_up", "w_kv"):
        finalize((k,), k)
    whole["w_in"] = _run_exchange(_join_partials([part_in], [slots_in]), "rs_join_mixer")
    finalize(("w_in",), "w_in")

    zeros = jnp.zeros((hw,), F32)
    sg, sd, sm, sv = _small_sum_adamw(parts, _pack_small(p, zeros, None), _pack_small(m_, zeros, None),
                                      _pack_small(v_, zeros, None))
    for tree, slab in zip((out_g, out_d, out_m, out_v), (sg, sd, sm, sv)):
        tree.update(_unpack_small(slab))
    loss_out = sg[7, hw]
    g_conv = lax.dynamic_slice(sg[8:11, :hw], (0, chip * (hw // N_CHIPS)), (3, hw // N_CHIPS))
    out_g["conv_w"] = g_conv
    out_d["conv_w"], out_m["conv_w"], out_v["conv_w"] = _adamw(p["conv_w"], g_conv, m_["conv_w"], v_["conv_w"],
                                                                "adamw_conv_w")
    return loss_out, dx, out_g, out_d, out_m, out_v


def kernel(x, mem, ln_mix_g, w_in, sgu_ln_g, sgu_ln_b, w_spatial, b_spatial, conv_w, grp_norm_a, grp_norm_b, w_out, ln_attn_g, ln_mem_g, w_q, w_kv, w_o, ln_ffn_g, w_gate_up, w_down, ln_final_g, loss_target, m_ln_mix_g, m_w_in, m_sgu_ln_g, m_sgu_ln_b, m_w_spatial, m_b_spatial, m_conv_w, m_grp_norm_a, m_grp_norm_b, m_w_out, m_ln_attn_g, m_ln_mem_g, m_w_q, m_w_kv, m_w_o, m_ln_ffn_g, m_w_gate_up, m_w_down, m_ln_final_g, v_ln_mix_g, v_w_in, v_sgu_ln_g, v_sgu_ln_b, v_w_spatial, v_b_spatial, v_conv_w, v_grp_norm_a, v_grp_norm_b, v_w_out, v_ln_attn_g, v_ln_mem_g, v_w_q, v_w_kv, v_w_o, v_ln_ffn_g, v_w_gate_up, v_w_down, v_ln_final_g):
    p = dict(ln_mix_g=ln_mix_g, w_in=w_in, sgu_ln_g=sgu_ln_g, sgu_ln_b=sgu_ln_b, w_spatial=w_spatial,
             b_spatial=b_spatial, conv_w=conv_w, grp_norm_a=grp_norm_a, grp_norm_b=grp_norm_b, w_out=w_out,
             ln_attn_g=ln_attn_g, ln_mem_g=ln_mem_g, w_q=w_q, w_kv=w_kv, w_o=w_o, ln_ffn_g=ln_ffn_g,
             w_gate_up=w_gate_up, w_down=w_down, ln_final_g=ln_final_g)
    m_ = dict(ln_mix_g=m_ln_mix_g, w_in=m_w_in, sgu_ln_g=m_sgu_ln_g, sgu_ln_b=m_sgu_ln_b, w_spatial=m_w_spatial,
              b_spatial=m_b_spatial, conv_w=m_conv_w, grp_norm_a=m_grp_norm_a, grp_norm_b=m_grp_norm_b,
              w_out=m_w_out, ln_attn_g=m_ln_attn_g, ln_mem_g=m_ln_mem_g, w_q=m_w_q, w_kv=m_w_kv, w_o=m_w_o,
              ln_ffn_g=m_ln_ffn_g, w_gate_up=m_w_gate_up, w_down=m_w_down, ln_final_g=m_ln_final_g)
    v_ = dict(ln_mix_g=v_ln_mix_g, w_in=v_w_in, sgu_ln_g=v_sgu_ln_g, sgu_ln_b=v_sgu_ln_b, w_spatial=v_w_spatial,
              b_spatial=v_b_spatial, conv_w=v_conv_w, grp_norm_a=v_grp_norm_a, grp_norm_b=v_grp_norm_b,
              w_out=v_w_out, ln_attn_g=v_ln_attn_g, ln_mem_g=v_ln_mem_g, w_q=v_w_q, w_kv=v_w_kv, w_o=v_w_o,
              ln_ffn_g=v_ln_ffn_g, w_gate_up=v_w_gate_up, w_down=v_w_down, ln_final_g=v_ln_final_g)
    s, d = x.shape[-2], x.shape[-1]
    loss, dx, g, dl, nm, nv = _step(p, m_, v_, x.reshape(s, d), mem.reshape(-1, d), loss_target.reshape(s, d))
    outs = [loss, dx.reshape(x.shape)]
    for tree in (g, dl, nm, nv):
        outs += [tree[k].reshape(p[k].shape) for k in _WEIGHTS]
    return tuple(outs)
```

```python
import functools
import math

import jax
import jax.numpy as jnp
from jax import lax
from jax.experimental import pallas as pl
from jax.experimental.pallas import tpu as pltpu

F32 = jnp.float32
BF16 = jnp.bfloat16
EPS = 1e-6
CHUNK = 128
HEADS = 4
N_CHIPS = 4
TM = 512
TM_ATTN = 512
TM_FFN = 256
ADAM_LR, ADAM_B1, ADAM_B2, ADAM_EPS, ADAM_WD, ADAM_STEP = 0.001, 0.9, 0.999, 1e-08, 0.01, 10
GELU_C = math.sqrt(2.0 / math.pi)
GELU_K = 0.044715
SMALL_ROWS = 80
VMEM_LIMIT = 56 * 1024 * 1024
MIDDLE_STEP_16THS = 7
MESH = pl.DeviceIdType.MESH
ANY = pl.BlockSpec(memory_space=pl.ANY)


def _params(*sem, collective_id=None):
    return pltpu.CompilerParams(dimension_semantics=sem, vmem_limit_bytes=VMEM_LIMIT, collective_id=collective_id)


def _dot(a, b):
    return jnp.dot(a, b, preferred_element_type=F32)


def _dot_nt(a, b):
    return lax.dot_general(a, b, (((1,), (1,)), ((), ())), preferred_element_type=F32)


def _dot_tn(a, b):
    return lax.dot_general(a, b, (((0,), (0,)), ((), ())), preferred_element_type=F32)


def _rms_fwd(x, g):
    r = lax.rsqrt(jnp.mean(x * x, axis=-1, keepdims=True) + EPS)
    xh = x * r
    return xh * g, xh, r


def _rms_bwd(dy, xh, r, g):
    dxh = dy * g
    dx = r * (dxh - xh * jnp.mean(dxh * xh, axis=-1, keepdims=True))
    return dx, jnp.sum(dy * xh, axis=0, keepdims=True)


def _full(shape):
    nd = len(shape)
    return pl.BlockSpec(shape, lambda *_: (0,) * nd, pipeline_mode=pl.Buffered(1))


def _acc(shape):
    nd = len(shape)
    return pl.BlockSpec(shape, lambda *_: (0,) * nd)


def _rows(tm, cols):
    return pl.BlockSpec((tm, cols), lambda i: (i, 0))


def _tril_weights(wsp_ref):
    row = lax.broadcasted_iota(jnp.int32, (CHUNK, CHUNK), 0)
    col = lax.broadcasted_iota(jnp.int32, (CHUNK, CHUNK), 1)
    return [jnp.where(row >= col, wsp_ref[hd], 0.0).astype(BF16) for hd in range(HEADS)]


def _shift_rows(z, zp):
    row = lax.broadcasted_iota(jnp.int32, z.shape, 0)
    zm1 = jnp.where(row == 0, zp[7:8, :], pltpu.roll(z, 1, 0))
    zm2 = jnp.where(row == 0, zp[6:7, :], jnp.where(row == 1, zp[7:8, :], pltpu.roll(z, 2, 0)))
    return zm1, zm2


def _gelu_parts(x):
    t = jnp.tanh(GELU_C * (x + GELU_K * (x * x * x)))
    return 0.5 * x * (1.0 + t), t


def _layer_norm_parts(v, g, b):
    mu = jnp.mean(v, axis=-1, keepdims=True)
    vc = v - mu
    rs = lax.rsqrt(jnp.mean(vc * vc, axis=-1, keepdims=True) + EPS)
    vhat = vc * rs
    return vhat * g + b, vhat, rs


def _kv_fwd(mem, g_mem, w_kv):
    m, d = mem.shape
    ns = w_kv.shape[2]

    def body(mem_ref, g_ref, w_ref, memn_ref, kv_ref):
        y, _, _ = _rms_fwd(mem_ref[...], g_ref[...])
        yb = y.astype(BF16)
        memn_ref[...] = yb
        for k in range(N_CHIPS):
            kv_ref[:, k * ns:(k + 1) * ns] = _dot(yb, w_ref[k]).astype(BF16)

    return pl.pallas_call(
        body, name="kv_fwd",
        out_shape=(jax.ShapeDtypeStruct((m, d), BF16), jax.ShapeDtypeStruct((m, N_CHIPS * ns), BF16)),
        compiler_params=pltpu.CompilerParams(vmem_limit_bytes=VMEM_LIMIT),
    )(mem, g_mem, w_kv)


def _mixer_fwd(x, g1, w_in, lng, lnb, wsp, b_t, cw, ga, gb, w_out, hosted=None):
    s, d = x.shape
    n = s // TM
    nch = TM // CHUNK
    ns = w_in.shape[2]
    nh = N_CHIPS * ns
    aw = d // 2
    hd_w = aw // HEADS

    def body(x_ref, g1_ref, win_ref, lng_ref, lnb_ref, wsp_ref, bt_ref, cw_ref, ga_ref, gb_ref, wout_ref,
             h_ref, x1_ref, y_ref, xn_ref, mix_ref, th_ref, zp_ref):
        i = pl.program_id(0)

        @pl.when(i == 0)
        def _():
            zp_ref[...] = jnp.zeros_like(zp_ref)

        x = x_ref[...]
        xn, _, _ = _rms_fwd(x, g1_ref[...])
        xnb = xn.astype(BF16)
        xn_ref[...] = xnb
        for k in range(N_CHIPS):
            h_ref[:, k * ns:(k + 1) * ns] = _dot(xnb, win_ref[k])
        a, th = _gelu_parts(h_ref[:, 0:2 * aw])
        th_ref[...] = th
        u = a[:, :aw]
        vn, _, _ = _layer_norm_parts(a[:, aw:], lng_ref[...], lnb_ref[...])
        vnb = vn.astype(BF16)
        wm = _tril_weights(wsp_ref)
        for c in range(nch):
            for hd in range(HEADS):
                blk = vnb[c * CHUNK:(c + 1) * CHUNK, hd * hd_w:(hd + 1) * hd_w]
                mix_ref[c * CHUNK:(c + 1) * CHUNK, hd * hd_w:(hd + 1) * hd_w] = _dot(wm[hd], blk) + bt_ref[:, hd:hd + 1]
        ya, _, _ = _rms_fwd(u * mix_ref[...], ga_ref[...])
        g_b = h_ref[:, 2 * aw:3 * aw]
        z = h_ref[:, 3 * aw:4 * aw] * h_ref[:, 4 * aw:5 * aw]
        zm1, zm2 = _shift_rows(z, zp_ref[...])
        conv = cw_ref[0:1, :] * zm2 + cw_ref[1:2, :] * zm1 + cw_ref[2:3, :] * z
        yb, _, _ = _rms_fwd(g_b * conv, gb_ref[...])
        zp_ref[...] = z[TM - 8:TM, :]
        ycat = jnp.concatenate([ya, yb], axis=-1).astype(BF16)
        y_ref[...] = ycat
        x1_ref[...] = x + _dot(ycat, wout_ref[...])

    return _host_call(
        body, "mixer_fwd", (n,),
        [_rows(TM, d), _full(g1.shape), _full(w_in.shape), _full(lng.shape), _full(lnb.shape),
         _full(wsp.shape), _full(b_t.shape), _full(cw.shape), _full(ga.shape), _full(gb.shape),
         _full(w_out.shape)],
        [_rows(TM, nh), _rows(TM, d), _rows(TM, d), _rows(TM, d), _rows(TM, aw), _rows(TM, d)],
        (jax.ShapeDtypeStruct((s, nh), F32), jax.ShapeDtypeStruct((s, d), F32),
         jax.ShapeDtypeStruct((s, d), BF16), jax.ShapeDtypeStruct((s, d), BF16),
         jax.ShapeDtypeStruct((s, aw), F32), jax.ShapeDtypeStruct((s, d), F32)),
        [pltpu.VMEM((8, aw), F32)],
        (x, g1, w_in, lng, lnb, wsp, b_t, cw, ga, gb, w_out), ("arbitrary",), hosted)


def _attn_fwd(x1, g2, w_q, kv, w_o, hosted=None):
    s, d = x1.shape
    tm = min(TM_ATTN, s)
    n = s // tm
    dh = d // HEADS
    m = kv.shape[0]
    scale = dh ** -0.5

    def body(x1_ref, g2_ref, wq_ref, kv_ref, wo_ref, x2_ref, o_ref, q_ref, p_ref):
        x1v = x1_ref[...]
        xn, _, _ = _rms_fwd(x1v, g2_ref[...])
        q_ref[...] = _dot(xn.astype(BF16), wq_ref[...]).astype(BF16)
        for hd in range(HEADS):
            kh = kv_ref[:, hd * dh:(hd + 1) * dh]
            vh = kv_ref[:, d + hd * dh:d + (hd + 1) * dh]
            sc = _dot_nt(q_ref[:, hd * dh:(hd + 1) * dh], kh) * scale
            e = jnp.exp(sc - jnp.max(sc, axis=-1, keepdims=True))
            p = e / jnp.sum(e, axis=-1, keepdims=True)
            p_ref[:, hd * m:(hd + 1) * m] = p
            o_ref[:, hd * dh:(hd + 1) * dh] = _dot(p.astype(BF16), vh).astype(BF16)
        x2_ref[...] = x1v + _dot(o_ref[...], wo_ref[...])

    return _host_call(
        body, "attn_fwd", (n,),
        [_rows(tm, d), _full(g2.shape), _full(w_q.shape), _full(kv.shape), _full(w_o.shape)],
        [_rows(tm, d), _rows(tm, d), _rows(tm, d), _rows(tm, HEADS * m)],
        (jax.ShapeDtypeStruct((s, d), F32), jax.ShapeDtypeStruct((s, d), BF16), jax.ShapeDtypeStruct((s, d), BF16),
         jax.ShapeDtypeStruct((s, HEADS * m), F32)),
        [], (x1, g2, w_q, kv, w_o), ("parallel",), hosted)


def _ffn_fwd_bwd(x2, g3, gf, target, w_gu, w_down):
    s, d = x2.shape
    tm = min(TM_FFN, s)
    n = s // tm
    ns = w_gu.shape[2]
    ff = 2 * ns

    def body(x2_ref, g3_ref, gf_ref, t_ref, wgu_ref, wd_ref,
             dx2_ref, act_ref, dgu_ref, xn_ref, dx3_ref, loss_ref, dgf_ref, dg3_ref):
        i = pl.program_id(0)

        @pl.when(i == 0)
        def _():
            loss_ref[...] = jnp.zeros_like(loss_ref)
            dgf_ref[...] = jnp.zeros_like(dgf_ref)
            dg3_ref[...] = jnp.zeros_like(dg3_ref)

        x2v = x2_ref[...]
        xn, xh3, r3 = _rms_fwd(x2v, g3_ref[...])
        xnb = xn.astype(BF16)
        xn_ref[...] = xnb
        x3 = x2v
        saved = []
        for j in range(2):
            g = _dot(xnb, wgu_ref[j])
            u = _dot(xnb, wgu_ref[2 + j])
            sg = 1.0 / (1.0 + jnp.exp(-g))
            sl = g * sg
            actb = (sl * u).astype(BF16)
            act_ref[:, j * ns:(j + 1) * ns] = actb
            x3 = x3 + _dot(actb, wd_ref[j * ns:(j + 1) * ns, :])
            saved.append((u, sl, sg * (1.0 + g * (1.0 - sg))))
        gfv = gf_ref[...]
        y, xhf, rf = _rms_fwd(x3, gfv)
        e = y - t_ref[...]
        loss_ref[...] += 0.5 * jnp.sum(jnp.sum(e * e, axis=-1, keepdims=True), axis=0, keepdims=True) / d
        dx3, dgf = _rms_bwd(e / d, xhf, rf, gfv)
        dgf_ref[...] += dgf
        dx3b = dx3.astype(BF16)
        dx3_ref[...] = dx3b
        dxn = jnp.zeros_like(x2v)
        for j in range(2):
            u, sl, dsl = saved[j]
            dact = _dot_nt(dx3b, wd_ref[j * ns:(j + 1) * ns, :])
            dgb = (dact * u * dsl).astype(BF16)
            dub = (dact * sl).astype(BF16)
            dgu_ref[:, j * ns:(j + 1) * ns] = dgb
            dgu_ref[:, ff + j * ns:ff + (j + 1) * ns] = dub
            dxn = dxn + _dot_nt(dgb, wgu_ref[j]) + _dot_nt(dub, wgu_ref[2 + j])
        dxr, dg3 = _rms_bwd(dxn, xh3, r3, g3_ref[...])
        dg3_ref[...] += dg3
        dx2_ref[...] = dx3 + dxr

    vec = jax.ShapeDtypeStruct((1, d), F32)
    return pl.pallas_call(
        body, name="ffn_fwd_bwd", grid=(n,),
        in_specs=[_rows(tm, d), _full(g3.shape), _full(gf.shape), _rows(tm, d), _full(w_gu.shape),
                  _full(w_down.shape)],
        out_specs=[_rows(tm, d), _rows(tm, ff), _rows(tm, 2 * ff), _rows(tm, d), _rows(tm, d),
                   _acc((1, 1)), _acc((1, d)), _acc((1, d))],
        out_shape=(jax.ShapeDtypeStruct((s, d), F32), jax.ShapeDtypeStruct((s, ff), BF16),
                   jax.ShapeDtypeStruct((s, 2 * ff), BF16), jax.ShapeDtypeStruct((s, d), BF16),
                   jax.ShapeDtypeStruct((s, d), BF16), jax.ShapeDtypeStruct((1, 1), F32), vec, vec),
        compiler_params=_params("arbitrary"),
    )(x2, g3, gf, target, w_gu, w_down)


def _attn_bwd(x1, dx2, o, ycat, qs, probs, g2, w_q, kv, w_o, hosted=None):
    s, d = x1.shape
    tm = min(TM_ATTN, s)
    n = s // tm
    dh = d // HEADS
    scale = dh ** -0.5
    m = kv.shape[0]

    def body(x1_ref, dx2_ref, o_ref, y_ref, q_ref, p_ref, g2_ref, wq_ref, kv_ref, wo_ref,
             dx1_ref, dkv_ref, dg2_ref, gwo_out, gwq_out, gwout_out, dq_ref, gwo_ref, gwq_ref, gwout_ref):
        i = pl.program_id(0)

        @pl.when(i == 0)
        def _():
            for r in (dkv_ref, dg2_ref, gwo_ref, gwq_ref, gwout_ref):
                r[...] = jnp.zeros_like(r)

        xn, xh2, r2 = _rms_fwd(x1_ref[...], g2_ref[...])
        xnb = xn.astype(BF16)
        dx2v = dx2_ref[...]
        dx2b = dx2v.astype(BF16)
        gwo_ref[...] += _dot_tn(o_ref[...], dx2b)
        do = _dot_nt(dx2b, wo_ref[...])
        for hd in range(HEADS):
            qb = q_ref[:, hd * dh:(hd + 1) * dh]
            p = p_ref[:, hd * m:(hd + 1) * m]
            kh = kv_ref[:, hd * dh:(hd + 1) * dh]
            vh = kv_ref[:, d + hd * dh:d + (hd + 1) * dh]
            dob = do[:, hd * dh:(hd + 1) * dh].astype(BF16)
            dp = _dot_nt(dob, vh)
            ds = p * (dp - jnp.sum(dp * p, axis=-1, keepdims=True))
            dsb = (ds * scale).astype(BF16)
            dq_ref[:, hd * dh:(hd + 1) * dh] = _dot(dsb, kh).astype(BF16)
            dkv_ref[:, hd * dh:(hd + 1) * dh] += _dot_tn(dsb, qb)
            dkv_ref[:, d + hd * dh:d + (hd + 1) * dh] += _dot_tn(p.astype(BF16), dob)
        dqb = dq_ref[...]
        gwq_ref[...] += _dot_tn(xnb, dqb)
        dxn = _dot_nt(dqb, wq_ref[...])
        dxr, dg2 = _rms_bwd(dxn, xh2, r2, g2_ref[...])
        dg2_ref[...] += dg2
        dx1 = dx2v + dxr
        dx1_ref[...] = dx1
        gwout_ref[...] += _dot_tn(y_ref[...], dx1.astype(BF16))

        @pl.when(i == n - 1)
        def _():
            for acc, out in ((gwo_ref, gwo_out), (gwq_ref, gwq_out), (gwout_ref, gwout_out)):
                out[...] = acc[...].astype(BF16)

    sq = jax.ShapeDtypeStruct((d, d), BF16)
    return _host_call(
        body, "attn_bwd", (n,),
        [_rows(tm, d), _rows(tm, d), _rows(tm, d), _rows(tm, d), _rows(tm, d), _rows(tm, HEADS * m),
         _full(g2.shape), _full(w_q.shape), _full(kv.shape), _full(w_o.shape)],
        [_rows(tm, d), _acc((m, 2 * d)), _acc((1, d)), _acc((d, d)), _acc((d, d)), _acc((d, d))],
        (jax.ShapeDtypeStruct((s, d), F32), jax.ShapeDtypeStruct((m, 2 * d), F32),
         jax.ShapeDtypeStruct((1, d), F32), sq, sq, sq),
        [pltpu.VMEM((tm, d), BF16)] + [pltpu.VMEM((d, d), F32)] * 3,
        (x1, dx2, o, ycat, qs, probs, g2, w_q, kv, w_o), ("arbitrary",), hosted)


def _kv_bwd(dkv, mem, memn, g_mem, w_kv):
    m, d = mem.shape
    ns = w_kv.shape[2]

    def body(dkv_ref, mem_ref, memn_ref, g_ref, w_ref, gw_ref, dg_ref):
        _, xh, _ = _rms_fwd(mem_ref[...], g_ref[...])
        dmemn = jnp.zeros((m, d), F32)
        for k in range(N_CHIPS):
            dkb = dkv_ref[:, k * ns:(k + 1) * ns].astype(BF16)
            gw_ref[k] = _dot_tn(memn_ref[...], dkb).astype(BF16)
            dmemn = dmemn + _dot_nt(dkb, w_ref[k])
        dg_ref[...] = jnp.sum(dmemn * xh, axis=0, keepdims=True)

    return pl.pallas_call(
        body, name="kv_bwd",
        out_shape=(jax.ShapeDtypeStruct((N_CHIPS, d, ns), BF16), jax.ShapeDtypeStruct((1, d), F32)),
        compiler_params=pltpu.CompilerParams(vmem_limit_bytes=VMEM_LIMIT),
    )(dkv, mem, memn, g_mem, w_kv)


def _mixer_bwd(x, dx1, h, mixed_all, th_all, g1, lng, lnb, wsp, b_t, cw, ga, gb, w_out, w_in, hosted=None):
    s, d = x.shape
    n = s // TM
    nch = TM // CHUNK
    ns = w_in.shape[2]
    nh = N_CHIPS * ns
    aw = d // 2
    hd_w = aw // HEADS

    def rev(cols):
        return pl.BlockSpec((TM, cols), lambda i: (n - 1 - i, 0))

    hprev = pl.BlockSpec((8, nh), lambda i: (jnp.maximum((n - 1 - i) * (TM // 8) - 1, 0), 0))

    def body(x_ref, dx1_ref, h_ref, hp_ref, mix_ref, th_ref, g1_ref, lng_ref, lnb_ref, wsp_ref, bt_ref, cw_ref,
             ga_ref, gb_ref, wout_ref, win_ref,
             dx_ref, dh_ref, dg1_ref, dlng_ref, dlnb_ref, dwsp_ref, dbt_ref, dcw_ref, dga_ref, dgb_ref,
             dvn_ref, dcn_ref):
        i = pl.program_id(0)

        @pl.when(i == 0)
        def _():
            for r in (dg1_ref, dlng_ref, dlnb_ref, dwsp_ref, dbt_ref, dcw_ref, dga_ref, dgb_ref, dcn_ref):
                r[...] = jnp.zeros_like(r)

        dx1v = dx1_ref[...]
        dycat = _dot_nt(dx1v.astype(BF16), wout_ref[...])
        ha = h_ref[:, 0:2 * aw]
        th = th_ref[...]
        a = 0.5 * ha * (1.0 + th)
        u = a[:, :aw]
        lngv = lng_ref[...]
        vn, vhat, rs = _layer_norm_parts(a[:, aw:], lngv, lnb_ref[...])
        vnb = vn.astype(BF16)
        wm = _tril_weights(wsp_ref)
        mixed = mix_ref[...]
        gav = ga_ref[...]
        _, yah, ra = _rms_fwd(u * mixed, gav)
        dya, dga = _rms_bwd(dycat[:, :aw], yah, ra, gav)
        dga_ref[...] += dga
        du = dya * mixed
        dmix = dya * u
        dmb = dmix.astype(BF16)
        tri = lax.broadcasted_iota(jnp.int32, (CHUNK, CHUNK), 0) >= lax.broadcasted_iota(jnp.int32, (CHUNK, CHUNK), 1)
        for hd in range(HEADS):
            dw = jnp.zeros((CHUNK, CHUNK), F32)
            db = jnp.zeros((CHUNK, 1), F32)
            for c in range(nch):
                rows = slice(c * CHUNK, (c + 1) * CHUNK)
                cols = slice(hd * hd_w, (hd + 1) * hd_w)
                dvn_ref[rows, cols] = _dot_tn(wm[hd], dmb[rows, cols])
                dw = dw + _dot_nt(dmb[rows, cols], vnb[rows, cols])
                db = db + jnp.sum(dmix[rows, cols], axis=1, keepdims=True)
            dwsp_ref[hd] += jnp.where(tri, dw, 0.0)
            dbt_ref[:, hd:hd + 1] += db
        dvn = dvn_ref[...]
        dlng_ref[...] += jnp.sum(dvn * vhat, axis=0, keepdims=True)
        dlnb_ref[...] += jnp.sum(dvn, axis=0, keepdims=True)
        dvh = dvn * lngv
        dv = rs * (dvh - jnp.mean(dvh, axis=-1, keepdims=True) - vhat * jnp.mean(dvh * vhat, axis=-1, keepdims=True))
        gprime = 0.5 * (1.0 + th) + 0.5 * ha * (1.0 - th * th) * (GELU_C * (1.0 + 3.0 * GELU_K * (ha * ha)))
        dh_ref[:, 0:2 * aw] = (jnp.concatenate([du, dv], axis=-1) * gprime).astype(BF16)
        g_b = h_ref[:, 2 * aw:3 * aw]
        g_c = h_ref[:, 3 * aw:4 * aw]
        val = h_ref[:, 4 * aw:5 * aw]
        z = g_c * val
        zp = jnp.where(i == n - 1, 0.0, hp_ref[:, 3 * aw:4 * aw] * hp_ref[:, 4 * aw:5 * aw])
        zm1, zm2 = _shift_rows(z, zp)
        cw0, cw1, cw2 = cw_ref[0:1, :], cw_ref[1:2, :], cw_ref[2:3, :]
        conv = cw0 * zm2 + cw1 * zm1 + cw2 * z
        gbv = gb_ref[...]
        _, ybh, rb = _rms_fwd(g_b * conv, gbv)
        dyb, dgb = _rms_bwd(dycat[:, aw:], ybh, rb, gbv)
        dgb_ref[...] += dgb
        dconv = dyb * g_b
        dcw_ref[0:1, :] += jnp.sum(dconv * zm2, axis=0, keepdims=True)
        dcw_ref[1:2, :] += jnp.sum(dconv * zm1, axis=0, keepdims=True)
        dcw_ref[2:3, :] += jnp.sum(dconv * z, axis=0, keepdims=True)
        nxt = dcn_ref[...]
        row = lax.broadcasted_iota(jnp.int32, dconv.shape, 0)
        dcp1 = jnp.where(row == TM - 1, nxt[0:1, :], pltpu.roll(dconv, TM - 1, 0))
        dcp2 = jnp.where(row == TM - 1, nxt[1:2, :],
                         jnp.where(row == TM - 2, nxt[0:1, :], pltpu.roll(dconv, TM - 2, 0)))
        dz = cw2 * dconv + cw1 * dcp1 + cw0 * dcp2
        dcn_ref[...] = dconv[0:8, :]
        dh_ref[:, 2 * aw:3 * aw] = (dyb * conv).astype(BF16)
        dh_ref[:, 3 * aw:4 * aw] = (dz * val).astype(BF16)
        dh_ref[:, 4 * aw:5 * aw] = (dz * g_c).astype(BF16)
        dxn = jnp.zeros((TM, d), F32)
        for k in range(N_CHIPS):
            dxn = dxn + _dot_nt(dh_ref[:, k * ns:(k + 1) * ns], win_ref[k])
        g1v = g1_ref[...]
        _, xh1, r1 = _rms_fwd(x_ref[...], g1v)
        dxr, dg1 = _rms_bwd(dxn, xh1, r1, g1v)
        dg1_ref[...] += dg1
        dx_ref[...] = dx1v + dxr

    ins = (x, dx1, h, h, mixed_all, th_all, g1, lng, lnb, wsp, b_t, cw, ga, gb, w_out, w_in)
    acc_shapes = [(1, d), (1, aw), (1, aw), wsp.shape, (CHUNK, CHUNK), cw.shape, (1, aw), (1, aw)]
    return _host_call(
        body, "mixer_bwd", (n,),
        [rev(d), rev(d), rev(nh), hprev, rev(aw), rev(d)] + [_full(a.shape) for a in ins[6:]],
        [rev(d), rev(nh)] + [_acc(sh) for sh in acc_shapes],
        (jax.ShapeDtypeStruct((s, d), F32), jax.ShapeDtypeStruct((s, nh), BF16))
        + tuple(jax.ShapeDtypeStruct(sh, F32) for sh in acc_shapes),
        [pltpu.VMEM((TM, aw), F32), pltpu.VMEM((8, aw), F32)],
        ins, ("arbitrary",), hosted)


def _weight_grad(a, b, name, tm, tn, col_sharded, hosted=None):
    t, m = a.shape
    n = b.shape[1]

    def body(a_ref, b_ref, o_ref):
        o_ref[...] = _dot_tn(a_ref[...].astype(BF16), b_ref[...].astype(BF16)).astype(BF16)

    if col_sharded:
        ns = n // N_CHIPS
        per = ns // tn
        out_shape = jax.ShapeDtypeStruct((N_CHIPS, m, ns), BF16)
        out_spec = pl.BlockSpec((None, tm, tn), lambda i, j: (j // per, i, j % per))
    else:
        out_shape = jax.ShapeDtypeStruct((m, n), BF16)
        out_spec = pl.BlockSpec((tm, tn), lambda i, j: (i, j))
    (out,), extra = _host_call(
        body, name, (m // tm, n // tn),
        [pl.BlockSpec((t, tm), lambda i, j: (0, i)), pl.BlockSpec((t, tn), lambda i, j: (0, j))],
        [out_spec], (out_shape,), [], (a, b), ("parallel", "parallel"), hosted)
    return (out if col_sharded else out.reshape(N_CHIPS, m // N_CHIPS, n)), extra


def _row_tile(rows, cap=256):
    best = None
    for t in range(16, min(rows, cap) + 1, 16):
        if rows % t == 0:
            best = t
    return best if best is not None else rows


def _adamw_math(w, g, m, v):
    m2 = ADAM_B1 * m + (1.0 - ADAM_B1) * g
    v2 = ADAM_B2 * v + (1.0 - ADAM_B2) * (g * g)
    m_hat = m2 / (1.0 - ADAM_B1 ** ADAM_STEP)
    v_hat = v2 / (1.0 - ADAM_B2 ** ADAM_STEP)
    delta = -ADAM_LR * (m_hat / (jnp.sqrt(v_hat) + ADAM_EPS) + ADAM_WD * w)
    return delta, m2, v2


def _adamw(w, g, m, v, name):
    r, c = w.shape
    tr = _row_tile(r) if r >= 16 else r

    def body(w_ref, g_ref, m_ref, v_ref, d_ref, m2_ref, v2_ref):
        d_ref[...], m2_ref[...], v2_ref[...] = _adamw_math(w_ref[...], g_ref[...], m_ref[...], v_ref[...])

    sh = jax.ShapeDtypeStruct((r, c), F32)
    return pl.pallas_call(
        body, name=name, grid=(r // tr,),
        in_specs=[_rows(tr, c)] * 4, out_specs=[_rows(tr, c)] * 3, out_shape=(sh, sh, sh),
        compiler_params=_params("parallel"),
    )(w, g, m, v)


def _finalize(items, place, name, hosted=None):
    r, c = items[0][2].shape
    tr = _row_tile(r)
    nw = len(items)

    def body(place_ref, *refs):
        ins, outs = refs[:7 * nw], refs[7 * nw:]
        for k in range(nw):
            own_ref, s0_ref, s1_ref, s2_ref, w_ref, m_ref, v_ref = ins[7 * k:7 * k + 7]
            g_ref, d_ref, m2_ref, v2_ref = outs[4 * k:4 * k + 4]
            g = own_ref[...].astype(F32) + s0_ref[...].astype(F32)
            g = (g + s1_ref[...].astype(F32)) + s2_ref[...].astype(F32)
            g_ref[...] = g
            d_ref[...], m2_ref[...], v2_ref[...] = _adamw_math(w_ref[...], g, m_ref[...], v_ref[...])

    def slot(k):
        return pl.BlockSpec((None, tr, c), lambda i, pref: (k, i, 0))

    rows = pl.BlockSpec((tr, c), lambda i, pref: (i, 0))
    sh = jax.ShapeDtypeStruct((r, c), F32)
    one = [pl.BlockSpec((None, tr, c), lambda i, pref: (pref[1], i, 0)), slot(0), slot(1), slot(2), rows, rows, rows]
    args = [a for part, slots, w, m, v in items for a in (part, slots, slots, slots, w, m, v)]
    res, extra = _host_call(body, name, (r // tr,), one * nw, [rows] * (4 * nw), (sh,) * (4 * nw), [], args,
                            ("parallel",), hosted, prefetch=(place,))
    return [res[4 * k:4 * k + 4] for k in range(nw)], extra


def _small_sum_adamw(parts, w, m, v):
    nd, r, c = parts.shape

    def body(p_ref, w_ref, m_ref, v_ref, g_ref, d_ref, m2_ref, v2_ref):
        g = p_ref[0]
        for k in range(1, nd):
            g = g + p_ref[k]
        g_ref[...] = g
        d_ref[...], m2_ref[...], v2_ref[...] = _adamw_math(w_ref[...], g, m_ref[...], v_ref[...])

    sh = jax.ShapeDtypeStruct((r, c), F32)
    return pl.pallas_call(
        body, name="small_sum_adamw", out_shape=(sh, sh, sh, sh),
        compiler_params=pltpu.CompilerParams(vmem_limit_bytes=VMEM_LIMIT),
    )(parts, w, m, v)


def _place():
    x, y, c = lax.axis_index("x"), lax.axis_index("y"), lax.axis_index("c")
    chips = [(1 - x, y), (x, 1 - y), (1 - x, 1 - y)]
    return x, y, c, 2 * x + y, chips


def _remote(src, dst, send_sem, recv_sem, to):
    return pltpu.make_async_remote_copy(src_ref=src, dst_ref=dst, send_sem=send_sem, recv_sem=recv_sem,
                                        device_id=to, device_id_type=MESH)


class _Exchange:
    def __init__(self, ins, out_shapes, sem_shape, start, finish, middle=None, in_place=False, peers=()):
        self.ins, self.out_shapes, self.sem_shape = tuple(ins), tuple(out_shapes), sem_shape
        self.start, self.finish, self.middle = start, finish, middle
        self.in_place = in_place
        self.peers = frozenset(peers)
        self.collective_id = None

    def with_id(self, collective_id):
        self.collective_id = collective_id
        return self


_FLIPS = {"c": (0, 0, 1), "x": (1, 0, 0), "y": (0, 1, 0), "xy": (1, 1, 0)}


def _handshake(peers):
    x, y, c, _, _ = _place()
    barrier = pltpu.get_barrier_semaphore()
    for name in sorted(peers):
        fx, fy, fc = _FLIPS[name]
        peer = (1 - x if fx else x, 1 - y if fy else y, 1 - c if fc else c)
        pl.semaphore_signal(barrier, inc=1, device_id=peer, device_id_type=MESH)
    pl.semaphore_wait(barrier, len(peers))


def _run_exchange(ex, name, collective_id=None):
    n_in, n_out = len(ex.ins), len(ex.out_shapes)

    def body(*refs):
        ins, outs = refs[:n_in], refs[n_in:n_in + n_out]
        send_sems, recv_sems = refs[n_in + n_out:]
        if collective_id is not None:
            _handshake(ex.peers)
        ex.start(ins, outs, send_sems, recv_sems)
        if ex.middle is not None:
            ex.middle(ins, outs, send_sems, recv_sems)
        ex.finish(ins, outs, send_sems, recv_sems)

    assert collective_id is None or ex.peers
    sem = pltpu.SemaphoreType.DMA(ex.sem_shape)
    return pl.pallas_call(
        body, name=name, out_shape=ex.out_shapes, in_specs=[ANY] * n_in, out_specs=[ANY] * n_out,
        input_output_aliases={k: k for k in range(n_in)} if ex.in_place else {}, scratch_shapes=[sem, sem],
        compiler_params=pltpu.CompilerParams(collective_id=collective_id),
    )(*ex.ins)


def _host_call(body, name, grid, in_specs, out_specs, out_shape, scratch_shapes, args, semantics, hosted,
               prefetch=()):
    hosted = [] if hosted is None else (list(hosted) if isinstance(hosted, (list, tuple)) else [hosted])
    collective_id = hosted[0].collective_id if hosted else None
    peers = frozenset().union(*[ex.peers for ex in hosted]) if hosted else frozenset()
    assert collective_id is None or all(ex.peers for ex in hosted)
    n_pre, n_in, n_out, n_scr = len(prefetch), len(in_specs), len(out_specs), len(scratch_shapes)
    h_ins = [a for ex in hosted for a in ex.ins]
    h_outs = [s for ex in hosted for s in ex.out_shapes]
    h_in, h_out = len(h_ins), len(h_outs)

    def wrapped(*refs):
        pre, refs = refs[:n_pre], refs[n_pre:]
        a, hi = refs[:n_in], refs[n_in:n_in + h_in]
        o = refs[n_in + h_in:n_in + h_in + n_out]
        ho = refs[n_in + h_in + n_out:n_in + h_in + n_out + h_out]
        scr = refs[n_in + h_in + n_out + h_out:]

        def run(phase):
            i0 = o0 = 0
            for k, ex in enumerate(hosted):
                fn = getattr(ex, phase)
                if fn is not None:
                    fn(hi[i0:i0 + len(ex.ins)], ho[o0:o0 + len(ex.out_shapes)], scr[n_scr + 2 * k],
                       scr[n_scr + 2 * k + 1])
                i0, o0 = i0 + len(ex.ins), o0 + len(ex.out_shapes)

        if hosted:
            first = functools.reduce(jnp.logical_and, [pl.program_id(k) == 0 for k in range(len(grid))])

            @pl.when(first)
            def _():
                if collective_id is not None:
                    _handshake(peers)
                run("start")

        if any(ex.middle is not None for ex in hosted):
            half_way = functools.reduce(jnp.logical_and, [
                pl.program_id(0) == max(1, grid[0] * MIDDLE_STEP_16THS // 16)] + [
                pl.program_id(k) == 0 for k in range(1, len(grid))])

            @pl.when(half_way)
            def _():
                run("middle")

        body(*pre, *a, *o, *scr[:n_scr])

        if hosted:
            last = functools.reduce(jnp.logical_and, [pl.program_id(k) == grid[k] - 1 for k in range(len(grid))])

            @pl.when(last)
            def _():
                run("finish")

    sems = [pltpu.SemaphoreType.DMA(ex.sem_shape) for ex in hosted for _ in range(2)]
    aliases, i0, o0 = {}, n_pre + n_in, n_out
    for ex in hosted:
        if ex.in_place:
            aliases.update({i0 + k: o0 + k for k in range(len(ex.ins))})
        i0, o0 = i0 + len(ex.ins), o0 + len(ex.out_shapes)
    all_in, all_out = list(in_specs) + [ANY] * h_in, list(out_specs) + [ANY] * h_out
    all_scr = list(scratch_shapes) + sems
    params = _params(*(["arbitrary"] * len(grid) if hosted else semantics), collective_id=collective_id)
    shapes = tuple(out_shape) + tuple(h_outs)
    if n_pre:
        call = pl.pallas_call(
            wrapped, name=name, out_shape=shapes, input_output_aliases=aliases, compiler_params=params,
            grid_spec=pltpu.PrefetchScalarGridSpec(num_scalar_prefetch=n_pre, grid=grid, in_specs=all_in,
                                                   out_specs=all_out, scratch_shapes=all_scr))
    else:
        call = pl.pallas_call(
            wrapped, name=name, grid=grid, in_specs=all_in, out_specs=all_out, out_shape=shapes,
            scratch_shapes=all_scr, input_output_aliases=aliases, compiler_params=params)
    res = call(*prefetch, *args, *h_ins)
    return res[:n_out], res[n_out:]


def _all_gather(shards, small=()):
    items = tuple(shards) + tuple(small)
    nw = len(shards)

    def place():
        x, y, c, me, _ = _place()
        first = (x + (1 - c) * (1 - 2 * x), y + c * (1 - 2 * y))
        second = (x + c * (1 - 2 * x), y + (1 - c) * (1 - 2 * y))
        diag = (1 - x, 1 - y)
        return x, y, c, me, (first, second, diag)

    def halves(w, c):
        rh = items[w].shape[0] // 2
        return pl.ds(c * rh, rh), pl.ds((1 - c) * rh, rh)

    def start(ins, outs, ss, rs):
        x, y, c, me, chips = place()
        for w in range(len(items)):
            _remote(ins[w], outs[w].at[me], ss.at[w, 6], rs.at[w, 6], (x, y, 1 - c)).start()
            if w < nw:
                mine, _ = halves(w, c)
                _remote(ins[w].at[mine], outs[w].at[me, mine], ss.at[w, 0], rs.at[w, 0], (*chips[0], c)).start()
            else:
                for k in range(3):
                    _remote(ins[w], outs[w].at[me], ss.at[w, k], rs.at[w, k], (*chips[k], c)).start()

    def onward(outs, ss, rs, w, k, x, y, c, chips):
        mine, _ = halves(w, c)
        pk = 2 * chips[k][0] + chips[k][1]
        got = outs[w].at[pk, mine]
        src = chips[1] if k == 2 else chips[k]
        _remote(got, got, ss.at[w, k], rs.at[w, k], (*src, c)).wait_recv()
        if k == 0:
            _remote(got, got, ss.at[w, 2], rs.at[w, 2], (*chips[1], c)).start()
        _remote(got, got, ss.at[w, 3 + k], rs.at[w, 3 + k], (x, y, 1 - c)).start()

    def middle(ins, outs, ss, rs):
        x, y, c, me, chips = place()
        for w in range(nw):
            mine, _ = halves(w, c)
            _remote(ins[w].at[mine], outs[w].at[me, mine], ss.at[w, 1], rs.at[w, 1], (*chips[1], c)).start()
        for w in range(nw):
            onward(outs, ss, rs, w, 0, x, y, c, chips)

    def finish(ins, outs, ss, rs):
        x, y, c, me, chips = place()
        sib = (x, y, 1 - c)
        for k in (1, 2):
            for w in range(nw):
                onward(outs, ss, rs, w, k, x, y, c, chips)
        for w in range(len(items)):
            if w < nw:
                mine, theirs = halves(w, c)
                for k, chip in ((3, chips[1]), (4, chips[0]), (5, chips[2])):
                    oth = outs[w].at[2 * chip[0] + chip[1], theirs]
                    _remote(oth, oth, ss.at[w, k], rs.at[w, k], sib).wait_recv()
                own = ins[w].at[mine]
                for k in range(6):
                    _remote(own, own, ss.at[w, k], rs.at[w, k], sib).wait_send()
            else:
                for k in range(3):
                    got = outs[w].at[2 * chips[k][0] + chips[k][1]]
                    _remote(got, got, ss.at[w, k], rs.at[w, k], (*chips[k], c)).wait_recv()
                    _remote(ins[w], ins[w], ss.at[w, k], rs.at[w, k], sib).wait_send()
            _remote(ins[w], outs[w].at[me], ss.at[w, 6], rs.at[w, 6], sib).wait()

    out_shapes = tuple(jax.ShapeDtypeStruct((N_CHIPS,) + a.shape, a.dtype) for a in items)
    return _Exchange(items, out_shapes, (len(items), 7), start, finish, middle if nw else None,
                     peers=("c", "x", "y", "xy") if small else ("c", "x", "y"))


def _chip_reduce(grads, name, collective_id):
    nw = len(grads)
    step = 64

    def body(*refs):
        ins, outs = refs[:nw], refs[nw:2 * nw]
        own, got = refs[2 * nw:3 * nw], refs[3 * nw:4 * nw]
        send_sems, recv_sems, local_sems = refs[4 * nw:]
        x, y, c, _, _ = _place()
        barrier = pltpu.get_barrier_semaphore()
        pl.semaphore_signal(barrier, inc=1, device_id=(x, y, 1 - c), device_id_type=MESH)
        pl.semaphore_wait(barrier, 1)
        moves = []
        for w in range(nw):
            rh = grads[w].shape[1] // 2
            away = _remote(ins[w].at[:, pl.ds((1 - c) * rh, rh), :], got[w], send_sems.at[w], recv_sems.at[w],
                           (x, y, 1 - c))
            mine = pltpu.make_async_copy(ins[w].at[:, pl.ds(c * rh, rh), :], own[w], local_sems.at[w])
            away.start()
            mine.start()
            moves.append((away, mine))
        back = []
        for w, (away, mine) in enumerate(moves):
            nb, rh, _ = own[w].shape
            mine.wait()
            away.wait()
            for k in range(nb):
                def add(i, carry, w=w, k=k):
                    rows = pl.ds(pl.multiple_of(i * step, step), step)
                    own[w][k, rows, :] = (own[w][k, rows, :].astype(F32) + got[w][k, rows, :].astype(F32)).astype(BF16)
                    return carry
                lax.fori_loop(0, rh // step, add, 0)
                tail = rh % step
                if tail:
                    rows = slice(rh - tail, rh)
                    own[w][k, rows, :] = (own[w][k, rows, :].astype(F32) + got[w][k, rows, :].astype(F32)).astype(BF16)
            wb = pltpu.make_async_copy(own[w], outs[w].at[:, pl.ds(c * rh, rh), :], local_sems.at[w])
            wb.start()
            back.append(wb)
        for wb in back:
            wb.wait()

    halves = [pltpu.VMEM((g.shape[0], g.shape[1] // 2, g.shape[2]), BF16) for g in grads]
    sem = pltpu.SemaphoreType.DMA((nw,))
    return pl.pallas_call(
        body, name=name, out_shape=tuple(jax.ShapeDtypeStruct(g.shape, BF16) for g in grads),
        in_specs=[ANY] * nw, out_specs=[ANY] * nw, scratch_shapes=halves + halves + [sem, sem, sem],
        compiler_params=pltpu.CompilerParams(vmem_limit_bytes=VMEM_LIMIT, collective_id=collective_id),
    )(*grads)


def _scatter_partials(parts):
    nw = len(parts)

    def copies(ins, outs, ss, rs):
        _, _, c, _, chips = _place()
        res = []
        for r, (px, py) in enumerate(chips):
            for w in range(nw):
                rh = parts[w].shape[1] // 2
                rows = pl.ds(c * rh, rh)
                res.append(_remote(ins[w].at[2 * px + py, rows], outs[w].at[r, rows], ss.at[w, r], rs.at[w, r],
                                   (px, py, c)))
        return res

    def start(ins, outs, ss, rs):
        for cp in copies(ins, outs, ss, rs):
            cp.start()

    def finish(ins, outs, ss, rs):
        for cp in copies(ins, outs, ss, rs):
            cp.wait()

    out_shapes = tuple(jax.ShapeDtypeStruct((3,) + p.shape[1:], p.dtype) for p in parts)
    return _Exchange(parts, out_shapes, (nw, 3), start, finish, peers=("x", "y", "xy"))


def _join_partials(parts, slots):
    nw = len(parts)

    def copies(outs, ss, rs, mine):
        x, y, c, me, _ = _place()
        res = []
        for w in range(nw):
            rh = parts[w].shape[1] // 2
            rows = pl.ds((c if mine else 1 - c) * rh, rh)
            own = outs[w].at[me, rows]
            got = outs[nw + w].at[:, rows, :]
            res.append(_remote(own, own, ss.at[w, 0], rs.at[w, 0], (x, y, 1 - c)))
            res.append(_remote(got, got, ss.at[w, 1], rs.at[w, 1], (x, y, 1 - c)))
        return res

    def start(ins, outs, ss, rs):
        for cp in copies(outs, ss, rs, True):
            cp.start()

    def finish(ins, outs, ss, rs):
        for cp in copies(outs, ss, rs, True):
            cp.wait_send()
        for cp in copies(outs, ss, rs, False):
            cp.wait_recv()

    arrays = tuple(parts) + tuple(slots)
    return _Exchange(arrays, tuple(jax.ShapeDtypeStruct(a.shape, a.dtype) for a in arrays), (nw, 2), start, finish,
                     in_place=True, peers=("c",))


def _gather_small(slab):
    def copies(ins, outs, ss, rs):
        x, y, c, _, _ = _place()
        me = 4 * x + 2 * y + c
        out, arrivals = [], []
        for k in range(1, 8):
            px = 1 - x if k & 4 else x
            py = 1 - y if k & 2 else y
            pc = 1 - c if k & 1 else c
            out.append(_remote(ins[0], outs[0].at[me], ss.at[k - 1], rs.at[k - 1], (px, py, pc)))
            theirs = outs[0].at[4 * px + 2 * py + pc]
            arrivals.append((theirs, k - 1, (px, py, pc)))
        return pltpu.make_async_copy(ins[0], outs[0].at[me], ss.at[7]), out, arrivals

    def start(ins, outs, ss, rs):
        own, out, _ = copies(ins, outs, ss, rs)
        own.start()
        for cp in out:
            cp.start()

    def finish(ins, outs, ss, rs):
        own, out, arrivals = copies(ins, outs, ss, rs)
        for cp in out:
            cp.wait_send()
        for theirs, k, peer in arrivals:
            _remote(theirs, theirs, ss.at[k], rs.at[k], peer).wait_recv()
        own.wait()

    return _Exchange((slab,), (jax.ShapeDtypeStruct((8,) + slab.shape, slab.dtype),), (8,), start, finish)


_SMALL_VECS = ("ln_mix_g", "ln_attn_g", "ln_mem_g", "ln_ffn_g", "ln_final_g")


def _pack_small(p, extra, conv):
    d = p["ln_mix_g"].shape[-1]
    top = [p[k].reshape(1, d) for k in _SMALL_VECS]
    top.append(jnp.concatenate([p["sgu_ln_g"].reshape(-1), p["sgu_ln_b"].reshape(-1)]).reshape(1, d))
    top.append(jnp.concatenate([p["grp_norm_a"].reshape(-1), p["grp_norm_b"].reshape(-1)]).reshape(1, d))
    top.append(jnp.concatenate([p["b_spatial"].reshape(-1), extra]).reshape(1, d))
    mid = jnp.zeros((8, d), F32)
    if conv is not None:
        mid = jnp.pad(conv, ((0, 5), (0, d - conv.shape[1])))
    return jnp.concatenate([jnp.concatenate(top, axis=0), mid, p["w_spatial"].reshape(-1, d)], axis=0)


def _unpack_small(slab):
    d = slab.shape[1]
    hw = d // 2
    out = {k: slab[i] for i, k in enumerate(_SMALL_VECS)}
    out["sgu_ln_g"], out["sgu_ln_b"] = slab[5, :hw], slab[5, hw:]
    out["grp_norm_a"], out["grp_norm_b"] = slab[6, :hw], slab[6, hw:]
    out["b_spatial"] = slab[7, :hw].reshape(HEADS, CHUNK)
    out["w_spatial"] = slab[16:].reshape(HEADS, CHUNK, CHUNK)
    return out


_BIG = ("w_in", "w_kv", "w_gate_up", "w_out", "w_q", "w_o", "w_down")
_WEIGHTS = ("ln_mix_g", "w_in", "sgu_ln_g", "sgu_ln_b", "w_spatial", "b_spatial", "conv_w", "grp_norm_a",
            "grp_norm_b", "w_out", "ln_attn_g", "ln_mem_g", "w_q", "w_kv", "w_o", "ln_ffn_g", "w_gate_up",
            "w_down", "ln_final_g")


def _step(p, m_, v_, x, mem, target):
    s, d = x.shape
    hw = d // 2
    row = lambda a: a.reshape(1, -1)
    x_, y_, c_ = lax.axis_index("x"), lax.axis_index("y"), lax.axis_index("c")
    chip = 2 * x_ + y_

    bf = {k: p[k].astype(BF16) for k in _BIG}
    conv8 = jnp.pad(p["conv_w"], ((0, 5), (0, 0)))
    w_in, w_out4, conv4 = _run_exchange(_all_gather([bf["w_in"], bf["w_out"]], [conv8]), "all_gather_mixer",
                                        collective_id=4)
    cw = jnp.transpose(conv4[:, :3, :], (1, 0, 2)).reshape(3, hw)
    b_t = jnp.pad(jnp.transpose(p["b_spatial"]), ((0, 0), (0, CHUNK - HEADS)))
    g1, g2, gm, g3, gf = (row(p[k]) for k in _SMALL_VECS)
    lng, lnb, ga, gb = row(p["sgu_ln_g"]), row(p["sgu_ln_b"]), row(p["grp_norm_a"]), row(p["grp_norm_b"])
    wsp = p["w_spatial"]
    w_out = w_out4.reshape(-1, d)

    (h, x1, ycat, xn1, mixed, th), (w_kv, w_q4, w_o4, w_down4) = _mixer_fwd(
        x, g1, w_in, lng, lnb, wsp, b_t, cw, ga, gb, w_out,
        hosted=_all_gather([bf[k] for k in ("w_kv", "w_q", "w_o", "w_down")]).with_id(5))
    w_q, w_o, w_down = (a.reshape(-1, d) for a in (w_q4, w_o4, w_down4))
    memn, kv = _kv_fwd(mem, gm, w_kv)
    (x2, o, qs, probs), (w_gu,) = _attn_fwd(x1, g2, w_q, kv, w_o,
                                            hosted=_all_gather([bf["w_gate_up"]]).with_id(6))
    dx2, act, dgu, xn3, dx3, loss, dgf, dg3 = _ffn_fwd_bwd(x2, g3, gf, target, w_gu, w_down)

    place = jnp.stack([c_, chip]).astype(jnp.int32)

    def chip_partials(names, grads, tag):
        return list(_chip_reduce(grads, "chip_reduce_" + tag, ("down", "ffn", "attn", "mixer").index(tag)))

    names_d = ("w_down",)
    parts_d = chip_partials(names_d, (_weight_grad(act, dx3, "grad_w_down", 1408, 512, False)[0],), "down")
    g_gu, slots_d = _weight_grad(xn3, dgu, "grad_w_gate_up", 512, 1408, True,
                                 hosted=_scatter_partials(parts_d).with_id(7))
    names_a = ("w_gate_up",)
    parts_a = chip_partials(names_a, (g_gu,), "ffn")
    (dx1, dkv, dg2, g_o, g_q, g_out), slots_a = _attn_bwd(x1, dx2, o, ycat, qs, probs, g2, w_q, kv, w_o,
                                                           hosted=_scatter_partials(parts_a).with_id(8))
    g_kv, dgm = _kv_bwd(dkv, mem, memn, gm, w_kv)
    names_b = ("w_o", "w_out", "w_q", "w_kv")
    shard_major = lambda g: g.reshape(N_CHIPS, -1, d)
    parts_b = chip_partials(names_b, (shard_major(g_o), shard_major(g_out), shard_major(g_q), g_kv), "attn")
    names_da = names_d + names_a
    (dx, dh, dg1, dlng, dlnb, dwsp, dbt, dcw, dga, dgb), extra = _mixer_bwd(
        x, dx1, h, mixed, th, g1, lng, lnb, wsp, b_t, cw, ga, gb, w_out, w_in,
        hosted=[_scatter_partials(parts_b).with_id(9), _join_partials(parts_d + parts_a, slots_d + slots_a)])
    slots_b, joined = extra[:len(names_b)], extra[len(names_b):]
    whole = dict(zip(names_da, zip(joined[:len(names_da)], joined[len(names_da):])))
    small = {"ln_mix_g": dg1, "ln_attn_g": dg2, "ln_mem_g": dgm, "ln_ffn_g": dg3, "ln_final_g": dgf,
             "sgu_ln_g": dlng, "sgu_ln_b": dlnb, "grp_norm_a": dga, "grp_norm_b": dgb,
             "b_spatial": jnp.transpose(dbt[:, :HEADS]), "w_spatial": dwsp}
    loss_vec = jnp.pad(loss.reshape(1), (0, hw - 1))
    g_in, extra = _weight_grad(
        xn1, dh, "grad_w_in", 1024, 640, True,
        hosted=[_gather_small(_pack_small(small, loss_vec, dcw)), _join_partials(parts_b, slots_b)])
    parts = extra[0]
    whole.update(zip(names_b, zip(extra[1:1 + len(names_b)], extra[1 + len(names_b):])))
    (part_in,) = chip_partials(("w_in",), (g_in,), "mixer")
    out_g, out_d, out_m, out_v = {}, {}, {}, {}

    def finalize(ks, tag, hosted=None):
        done, res = _finalize([(whole[k][0], whole[k][1], p[k], m_[k], v_[k]) for k in ks], place,
                              "finalize_" + tag, hosted)
        for k, (g, dl, nm, nv) in zip(ks, done):
            out_g[k], out_d[k], out_m[k], out_v[k] = g, dl, nm, nv
        return res

    (slots_in,) = finalize(("w_down",), "w_down", _scatter_partials([part_in]).with_id(10))
    finalize(("w_o", "w_out", "w_q"), "attn")
    for k in ("w_gate_up", "w_kv"):
        finalize((k,), k)
    whole["w_in"] = _run_exchange(_join_partials([part_in], [slots_in]), "rs_join_mixer", collective_id=11)
    finalize(("w_in",), "w_in")

    zeros = jnp.zeros((hw,), F32)
    sg, sd, sm, sv = _small_sum_adamw(parts, _pack_small(p, zeros, None), _pack_small(m_, zeros, None),
                                      _pack_small(v_, zeros, None))
    for tree, slab in zip((out_g, out_d, out_m, out_v), (sg, sd, sm, sv)):
        tree.update(_unpack_small(slab))
    loss_out = sg[7, hw]
    g_conv = lax.dynamic_slice(sg[8:11, :hw], (0, chip * (hw // N_CHIPS)), (3, hw // N_CHIPS))
    out_g["conv_w"] = g_conv
    out_d["conv_w"], out_m["conv_w"], out_v["conv_w"] = _adamw(p["conv_w"], g_conv, m_["conv_w"], v_["conv_w"],
                                                                "adamw_conv_w")
    return loss_out, dx, out_g, out_d, out_m, out_v


def kernel(x, mem, ln_mix_g, w_in, sgu_ln_g, sgu_ln_b, w_spatial, b_spatial, conv_w, grp_norm_a, grp_norm_b, w_out, ln_attn_g, ln_mem_g, w_q, w_kv, w_o, ln_ffn_g, w_gate_up, w_down, ln_final_g, loss_target, m_ln_mix_g, m_w_in, m_sgu_ln_g, m_sgu_ln_b, m_w_spatial, m_b_spatial, m_conv_w, m_grp_norm_a, m_grp_norm_b, m_w_out, m_ln_attn_g, m_ln_mem_g, m_w_q, m_w_kv, m_w_o, m_ln_ffn_g, m_w_gate_up, m_w_down, m_ln_final_g, v_ln_mix_g, v_w_in, v_sgu_ln_g, v_sgu_ln_b, v_w_spatial, v_b_spatial, v_conv_w, v_grp_norm_a, v_grp_norm_b, v_w_out, v_ln_attn_g, v_ln_mem_g, v_w_q, v_w_kv, v_w_o, v_ln_ffn_g, v_w_gate_up, v_w_down, v_ln_final_g):
    p = dict(ln_mix_g=ln_mix_g, w_in=w_in, sgu_ln_g=sgu_ln_g, sgu_ln_b=sgu_ln_b, w_spatial=w_spatial,
             b_spatial=b_spatial, conv_w=conv_w, grp_norm_a=grp_norm_a, grp_norm_b=grp_norm_b, w_out=w_out,
             ln_attn_g=ln_attn_g, ln_mem_g=ln_mem_g, w_q=w_q, w_kv=w_kv, w_o=w_o, ln_ffn_g=ln_ffn_g,
             w_gate_up=w_gate_up, w_down=w_down, ln_final_g=ln_final_g)
    m_ = dict(ln_mix_g=m_ln_mix_g, w_in=m_w_in, sgu_ln_g=m_sgu_ln_g, sgu_ln_b=m_sgu_ln_b, w_spatial=m_w_spatial,
              b_spatial=m_b_spatial, conv_w=m_conv_w, grp_norm_a=m_grp_norm_a, grp_norm_b=m_grp_norm_b,
              w_out=m_w_out, ln_attn_g=m_ln_attn_g, ln_mem_g=m_ln_mem_g, w_q=m_w_q, w_kv=m_w_kv, w_o=m_w_o,
              ln_ffn_g=m_ln_ffn_g, w_gate_up=m_w_gate_up, w_down=m_w_down, ln_final_g=m_ln_final_g)
    v_ = dict(ln_mix_g=v_ln_mix_g, w_in=v_w_in, sgu_ln_g=v_sgu_ln_g, sgu_ln_b=v_sgu_ln_b, w_spatial=v_w_spatial,
              b_spatial=v_b_spatial, conv_w=v_conv_w, grp_norm_a=v_grp_norm_a, grp_norm_b=v_grp_norm_b,
              w_out=v_w_out, ln_attn_g=v_ln_attn_g, ln_mem_g=v_ln_mem_g, w_q=v_w_q, w_kv=v_w_kv, w_o=v_w_o,
              ln_ffn_g=v_ln_ffn_g, w_gate_up=v_w_gate_up, w_down=v_w_down, ln_final_g=v_ln_final_g)
    s, d = x.shape[-2], x.shape[-1]
    loss, dx, g, dl, nm, nv = _step(p, m_, v_, x.reshape(s, d), mem.reshape(-1, d), loss_target.reshape(s, d))
    outs = [loss, dx.reshape(x.shape)]
    for tree in (g, dl, nm, nv):
        outs += [tree[k].reshape(p[k].shape) for k in _WEIGHTS]
    return tuple(outs)
```

```python
import functools
import math

import jax
import jax.numpy as jnp
from jax import lax
from jax.experimental import pallas as pl
from jax.experimental.pallas import tpu as pltpu

F32 = jnp.float32
BF16 = jnp.bfloat16
EPS = 1e-6
CHUNK = 128
HEADS = 4
N_CHIPS = 4
TM = 512
TM_ATTN = 512
TM_FFN = 256
ADAM_LR, ADAM_B1, ADAM_B2, ADAM_EPS, ADAM_WD, ADAM_STEP = 0.001, 0.9, 0.999, 1e-08, 0.01, 10
GELU_C = math.sqrt(2.0 / math.pi)
GELU_K = 0.044715
SMALL_ROWS = 80
VMEM_LIMIT = 56 * 1024 * 1024
MIDDLE_STEP_16THS = 7
MESH = pl.DeviceIdType.MESH
ANY = pl.BlockSpec(memory_space=pl.ANY)


def _params(*sem, collective_id=None):
    return pltpu.CompilerParams(dimension_semantics=sem, vmem_limit_bytes=VMEM_LIMIT, collective_id=collective_id)


def _dot(a, b):
    return jnp.dot(a, b, preferred_element_type=F32)


def _dot_nt(a, b):
    return lax.dot_general(a, b, (((1,), (1,)), ((), ())), preferred_element_type=F32)


def _dot_tn(a, b):
    return lax.dot_general(a, b, (((0,), (0,)), ((), ())), preferred_element_type=F32)


def _rms_fwd(x, g):
    r = lax.rsqrt(jnp.mean(x * x, axis=-1, keepdims=True) + EPS)
    xh = x * r
    return xh * g, xh, r


def _rms_bwd(dy, xh, r, g):
    dxh = dy * g
    dx = r * (dxh - xh * jnp.mean(dxh * xh, axis=-1, keepdims=True))
    return dx, jnp.sum(dy * xh, axis=0, keepdims=True)


def _full(shape):
    nd = len(shape)
    return pl.BlockSpec(shape, lambda *_: (0,) * nd, pipeline_mode=pl.Buffered(1))


def _acc(shape):
    nd = len(shape)
    return pl.BlockSpec(shape, lambda *_: (0,) * nd)


def _rows(tm, cols):
    return pl.BlockSpec((tm, cols), lambda i: (i, 0))


def _tril_weights(wsp_ref):
    row = lax.broadcasted_iota(jnp.int32, (CHUNK, CHUNK), 0)
    col = lax.broadcasted_iota(jnp.int32, (CHUNK, CHUNK), 1)
    return [jnp.where(row >= col, wsp_ref[hd], 0.0).astype(BF16) for hd in range(HEADS)]


def _shift_rows(z, zp):
    row = lax.broadcasted_iota(jnp.int32, z.shape, 0)
    zm1 = jnp.where(row == 0, zp[7:8, :], pltpu.roll(z, 1, 0))
    zm2 = jnp.where(row == 0, zp[6:7, :], jnp.where(row == 1, zp[7:8, :], pltpu.roll(z, 2, 0)))
    return zm1, zm2


def _gelu_parts(x):
    t = jnp.tanh(GELU_C * (x + GELU_K * (x * x * x)))
    return 0.5 * x * (1.0 + t), t


def _layer_norm_parts(v, g, b):
    mu = jnp.mean(v, axis=-1, keepdims=True)
    vc = v - mu
    rs = lax.rsqrt(jnp.mean(vc * vc, axis=-1, keepdims=True) + EPS)
    vhat = vc * rs
    return vhat * g + b, vhat, rs


def _kv_fwd(mem, g_mem, w_kv):
    m, d = mem.shape
    ns = w_kv.shape[2]

    def body(mem_ref, g_ref, w_ref, memn_ref, kv_ref):
        y, _, _ = _rms_fwd(mem_ref[...], g_ref[...])
        yb = y.astype(BF16)
        memn_ref[...] = yb
        for k in range(N_CHIPS):
            kv_ref[:, k * ns:(k + 1) * ns] = _dot(yb, w_ref[k]).astype(BF16)

    return pl.pallas_call(
        body, name="kv_fwd",
        out_shape=(jax.ShapeDtypeStruct((m, d), BF16), jax.ShapeDtypeStruct((m, N_CHIPS * ns), BF16)),
        compiler_params=pltpu.CompilerParams(vmem_limit_bytes=VMEM_LIMIT),
    )(mem, g_mem, w_kv)


def _mixer_fwd(x, g1, w_in, lng, lnb, wsp, b_t, cw, ga, gb, w_out, hosted=None):
    s, d = x.shape
    n = s // TM
    nch = TM // CHUNK
    ns = w_in.shape[2]
    nh = N_CHIPS * ns
    aw = d // 2
    hd_w = aw // HEADS

    def body(x_ref, g1_ref, win_ref, lng_ref, lnb_ref, wsp_ref, bt_ref, cw_ref, ga_ref, gb_ref, wout_ref,
             h_ref, x1_ref, y_ref, xn_ref, mix_ref, th_ref, zp_ref):
        i = pl.program_id(0)

        @pl.when(i == 0)
        def _():
            zp_ref[...] = jnp.zeros_like(zp_ref)

        x = x_ref[...]
        xn, _, _ = _rms_fwd(x, g1_ref[...])
        xnb = xn.astype(BF16)
        xn_ref[...] = xnb
        for k in range(N_CHIPS):
            h_ref[:, k * ns:(k + 1) * ns] = _dot(xnb, win_ref[k])
        a, th = _gelu_parts(h_ref[:, 0:2 * aw])
        th_ref[...] = th
        u = a[:, :aw]
        vn, _, _ = _layer_norm_parts(a[:, aw:], lng_ref[...], lnb_ref[...])
        vnb = vn.astype(BF16)
        wm = _tril_weights(wsp_ref)
        for c in range(nch):
            for hd in range(HEADS):
                blk = vnb[c * CHUNK:(c + 1) * CHUNK, hd * hd_w:(hd + 1) * hd_w]
                mix_ref[c * CHUNK:(c + 1) * CHUNK, hd * hd_w:(hd + 1) * hd_w] = _dot(wm[hd], blk) + bt_ref[:, hd:hd + 1]
        ya, _, _ = _rms_fwd(u * mix_ref[...], ga_ref[...])
        g_b = h_ref[:, 2 * aw:3 * aw]
        z = h_ref[:, 3 * aw:4 * aw] * h_ref[:, 4 * aw:5 * aw]
        zm1, zm2 = _shift_rows(z, zp_ref[...])
        conv = cw_ref[0:1, :] * zm2 + cw_ref[1:2, :] * zm1 + cw_ref[2:3, :] * z
        yb, _, _ = _rms_fwd(g_b * conv, gb_ref[...])
        zp_ref[...] = z[TM - 8:TM, :]
        ycat = jnp.concatenate([ya, yb], axis=-1).astype(BF16)
        y_ref[...] = ycat
        x1_ref[...] = x + _dot(ycat, wout_ref[...])

    return _host_call(
        body, "mixer_fwd", (n,),
        [_rows(TM, d), _full(g1.shape), _full(w_in.shape), _full(lng.shape), _full(lnb.shape),
         _full(wsp.shape), _full(b_t.shape), _full(cw.shape), _full(ga.shape), _full(gb.shape),
         _full(w_out.shape)],
        [_rows(TM, nh), _rows(TM, d), _rows(TM, d), _rows(TM, d), _rows(TM, aw), _rows(TM, d)],
        (jax.ShapeDtypeStruct((s, nh), F32), jax.ShapeDtypeStruct((s, d), F32),
         jax.ShapeDtypeStruct((s, d), BF16), jax.ShapeDtypeStruct((s, d), BF16),
         jax.ShapeDtypeStruct((s, aw), F32), jax.ShapeDtypeStruct((s, d), F32)),
        [pltpu.VMEM((8, aw), F32)],
        (x, g1, w_in, lng, lnb, wsp, b_t, cw, ga, gb, w_out), ("arbitrary",), hosted)


def _attn_fwd(x1, g2, w_q, kv, w_o, hosted=None):
    s, d = x1.shape
    tm = min(TM_ATTN, s)
    n = s // tm
    dh = d // HEADS
    m = kv.shape[0]
    scale = dh ** -0.5

    def body(x1_ref, g2_ref, wq_ref, kv_ref, wo_ref, x2_ref, o_ref, q_ref, p_ref):
        x1v = x1_ref[...]
        xn, _, _ = _rms_fwd(x1v, g2_ref[...])
        q_ref[...] = _dot(xn.astype(BF16), wq_ref[...]).astype(BF16)
        for hd in range(HEADS):
            kh = kv_ref[:, hd * dh:(hd + 1) * dh]
            vh = kv_ref[:, d + hd * dh:d + (hd + 1) * dh]
            sc = _dot_nt(q_ref[:, hd * dh:(hd + 1) * dh], kh) * scale
            e = jnp.exp(sc - jnp.max(sc, axis=-1, keepdims=True))
            p = e / jnp.sum(e, axis=-1, keepdims=True)
            p_ref[:, hd * m:(hd + 1) * m] = p
            o_ref[:, hd * dh:(hd + 1) * dh] = _dot(p.astype(BF16), vh).astype(BF16)
        x2_ref[...] = x1v + _dot(o_ref[...], wo_ref[...])

    return _host_call(
        body, "attn_fwd", (n,),
        [_rows(tm, d), _full(g2.shape), _full(w_q.shape), _full(kv.shape), _full(w_o.shape)],
        [_rows(tm, d), _rows(tm, d), _rows(tm, d), _rows(tm, HEADS * m)],
        (jax.ShapeDtypeStruct((s, d), F32), jax.ShapeDtypeStruct((s, d), BF16), jax.ShapeDtypeStruct((s, d), BF16),
         jax.ShapeDtypeStruct((s, HEADS * m), F32)),
        [], (x1, g2, w_q, kv, w_o), ("parallel",), hosted)


def _ffn_fwd_bwd(x2, g3, gf, target, w_gu, w_down):
    s, d = x2.shape
    tm = min(TM_FFN, s)
    n = s // tm
    ns = w_gu.shape[2]
    ff = 2 * ns

    def body(x2_ref, g3_ref, gf_ref, t_ref, wgu_ref, wd_ref,
             dx2_ref, act_ref, dgu_ref, xn_ref, dx3_ref, loss_ref, dgf_ref, dg3_ref):
        i = pl.program_id(0)

        @pl.when(i == 0)
        def _():
            loss_ref[...] = jnp.zeros_like(loss_ref)
            dgf_ref[...] = jnp.zeros_like(dgf_ref)
            dg3_ref[...] = jnp.zeros_like(dg3_ref)

        x2v = x2_ref[...]
        xn, xh3, r3 = _rms_fwd(x2v, g3_ref[...])
        xnb = xn.astype(BF16)
        xn_ref[...] = xnb
        x3 = x2v
        saved = []
        for j in range(2):
            g = _dot(xnb, wgu_ref[j])
            u = _dot(xnb, wgu_ref[2 + j])
            sg = 1.0 / (1.0 + jnp.exp(-g))
            sl = g * sg
            actb = (sl * u).astype(BF16)
            act_ref[:, j * ns:(j + 1) * ns] = actb
            x3 = x3 + _dot(actb, wd_ref[j * ns:(j + 1) * ns, :])
            saved.append((u, sl, sg * (1.0 + g * (1.0 - sg))))
        gfv = gf_ref[...]
        y, xhf, rf = _rms_fwd(x3, gfv)
        e = y - t_ref[...]
        loss_ref[...] += 0.5 * jnp.sum(jnp.sum(e * e, axis=-1, keepdims=True), axis=0, keepdims=True) / d
        dx3, dgf = _rms_bwd(e / d, xhf, rf, gfv)
        dgf_ref[...] += dgf
        dx3b = dx3.astype(BF16)
        dx3_ref[...] = dx3b
        dxn = jnp.zeros_like(x2v)
        for j in range(2):
            u, sl, dsl = saved[j]
            dact = _dot_nt(dx3b, wd_ref[j * ns:(j + 1) * ns, :])
            dgb = (dact * u * dsl).astype(BF16)
            dub = (dact * sl).astype(BF16)
            dgu_ref[:, j * ns:(j + 1) * ns] = dgb
            dgu_ref[:, ff + j * ns:ff + (j + 1) * ns] = dub
            dxn = dxn + _dot_nt(dgb, wgu_ref[j]) + _dot_nt(dub, wgu_ref[2 + j])
        dxr, dg3 = _rms_bwd(dxn, xh3, r3, g3_ref[...])
        dg3_ref[...] += dg3
        dx2_ref[...] = dx3 + dxr

    vec = jax.ShapeDtypeStruct((1, d), F32)
    return pl.pallas_call(
        body, name="ffn_fwd_bwd", grid=(n,),
        in_specs=[_rows(tm, d), _full(g3.shape), _full(gf.shape), _rows(tm, d), _full(w_gu.shape),
                  _full(w_down.shape)],
        out_specs=[_rows(tm, d), _rows(tm, ff), _rows(tm, 2 * ff), _rows(tm, d), _rows(tm, d),
                   _acc((1, 1)), _acc((1, d)), _acc((1, d))],
        out_shape=(jax.ShapeDtypeStruct((s, d), F32), jax.ShapeDtypeStruct((s, ff), BF16),
                   jax.ShapeDtypeStruct((s, 2 * ff), BF16), jax.ShapeDtypeStruct((s, d), BF16),
                   jax.ShapeDtypeStruct((s, d), BF16), jax.ShapeDtypeStruct((1, 1), F32), vec, vec),
        compiler_params=_params("arbitrary"),
    )(x2, g3, gf, target, w_gu, w_down)


def _attn_bwd(x1, dx2, o, ycat, qs, probs, g2, w_q, kv, w_o, hosted=None):
    s, d = x1.shape
    tm = min(TM_ATTN, s)
    n = s // tm
    dh = d // HEADS
    scale = dh ** -0.5
    m = kv.shape[0]

    def body(x1_ref, dx2_ref, o_ref, y_ref, q_ref, p_ref, g2_ref, wq_ref, kv_ref, wo_ref,
             dx1_ref, dkv_ref, dg2_ref, gwo_out, gwq_out, gwout_out, dq_ref, gwo_ref, gwq_ref, gwout_ref):
        i = pl.program_id(0)

        @pl.when(i == 0)
        def _():
            for r in (dkv_ref, dg2_ref, gwo_ref, gwq_ref, gwout_ref):
                r[...] = jnp.zeros_like(r)

        xn, xh2, r2 = _rms_fwd(x1_ref[...], g2_ref[...])
        xnb = xn.astype(BF16)
        dx2v = dx2_ref[...]
        dx2b = dx2v.astype(BF16)
        gwo_ref[...] += _dot_tn(o_ref[...], dx2b)
        do = _dot_nt(dx2b, wo_ref[...])
        for hd in range(HEADS):
            qb = q_ref[:, hd * dh:(hd + 1) * dh]
            p = p_ref[:, hd * m:(hd + 1) * m]
            kh = kv_ref[:, hd * dh:(hd + 1) * dh]
            vh = kv_ref[:, d + hd * dh:d + (hd + 1) * dh]
            dob = do[:, hd * dh:(hd + 1) * dh].astype(BF16)
            dp = _dot_nt(dob, vh)
            ds = p * (dp - jnp.sum(dp * p, axis=-1, keepdims=True))
            dsb = (ds * scale).astype(BF16)
            dq_ref[:, hd * dh:(hd + 1) * dh] = _dot(dsb, kh).astype(BF16)
            dkv_ref[:, hd * dh:(hd + 1) * dh] += _dot_tn(dsb, qb)
            dkv_ref[:, d + hd * dh:d + (hd + 1) * dh] += _dot_tn(p.astype(BF16), dob)
        dqb = dq_ref[...]
        gwq_ref[...] += _dot_tn(xnb, dqb)
        dxn = _dot_nt(dqb, wq_ref[...])
        dxr, dg2 = _rms_bwd(dxn, xh2, r2, g2_ref[...])
        dg2_ref[...] += dg2
        dx1 = dx2v + dxr
        dx1_ref[...] = dx1
        gwout_ref[...] += _dot_tn(y_ref[...], dx1.astype(BF16))

        @pl.when(i == n - 1)
        def _():
            for acc, out in ((gwo_ref, gwo_out), (gwq_ref, gwq_out), (gwout_ref, gwout_out)):
                out[...] = acc[...].astype(BF16)

    sq = jax.ShapeDtypeStruct((d, d), BF16)
    return _host_call(
        body, "attn_bwd", (n,),
        [_rows(tm, d), _rows(tm, d), _rows(tm, d), _rows(tm, d), _rows(tm, d), _rows(tm, HEADS * m),
         _full(g2.shape), _full(w_q.shape), _full(kv.shape), _full(w_o.shape)],
        [_rows(tm, d), _acc((m, 2 * d)), _acc((1, d)), _acc((d, d)), _acc((d, d)), _acc((d, d))],
        (jax.ShapeDtypeStruct((s, d), F32), jax.ShapeDtypeStruct((m, 2 * d), F32),
         jax.ShapeDtypeStruct((1, d), F32), sq, sq, sq),
        [pltpu.VMEM((tm, d), BF16)] + [pltpu.VMEM((d, d), F32)] * 3,
        (x1, dx2, o, ycat, qs, probs, g2, w_q, kv, w_o), ("arbitrary",), hosted)


def _kv_bwd(dkv, mem, memn, g_mem, w_kv):
    m, d = mem.shape
    ns = w_kv.shape[2]

    def body(dkv_ref, mem_ref, memn_ref, g_ref, w_ref, gw_ref, dg_ref):
        _, xh, _ = _rms_fwd(mem_ref[...], g_ref[...])
        dmemn = jnp.zeros((m, d), F32)
        for k in range(N_CHIPS):
            dkb = dkv_ref[:, k * ns:(k + 1) * ns].astype(BF16)
            gw_ref[k] = _dot_tn(memn_ref[...], dkb).astype(BF16)
            dmemn = dmemn + _dot_nt(dkb, w_ref[k])
        dg_ref[...] = jnp.sum(dmemn * xh, axis=0, keepdims=True)

    return pl.pallas_call(
        body, name="kv_bwd",
        out_shape=(jax.ShapeDtypeStruct((N_CHIPS, d, ns), BF16), jax.ShapeDtypeStruct((1, d), F32)),
        compiler_params=pltpu.CompilerParams(vmem_limit_bytes=VMEM_LIMIT),
    )(dkv, mem, memn, g_mem, w_kv)


def _mixer_bwd(x, dx1, h, mixed_all, th_all, g1, lng, lnb, wsp, b_t, cw, ga, gb, w_out, w_in, hosted=None):
    s, d = x.shape
    n = s // TM
    nch = TM // CHUNK
    ns = w_in.shape[2]
    nh = N_CHIPS * ns
    aw = d // 2
    hd_w = aw // HEADS

    def rev(cols):
        return pl.BlockSpec((TM, cols), lambda i: (n - 1 - i, 0))

    hprev = pl.BlockSpec((8, nh), lambda i: (jnp.maximum((n - 1 - i) * (TM // 8) - 1, 0), 0))

    def body(x_ref, dx1_ref, h_ref, hp_ref, mix_ref, th_ref, g1_ref, lng_ref, lnb_ref, wsp_ref, bt_ref, cw_ref,
             ga_ref, gb_ref, wout_ref, win_ref,
             dx_ref, dh_ref, dg1_ref, dlng_ref, dlnb_ref, dwsp_ref, dbt_ref, dcw_ref, dga_ref, dgb_ref,
             dvn_ref, dcn_ref):
        i = pl.program_id(0)

        @pl.when(i == 0)
        def _():
            for r in (dg1_ref, dlng_ref, dlnb_ref, dwsp_ref, dbt_ref, dcw_ref, dga_ref, dgb_ref, dcn_ref):
                r[...] = jnp.zeros_like(r)

        dx1v = dx1_ref[...]
        dycat = _dot_nt(dx1v.astype(BF16), wout_ref[...])
        ha = h_ref[:, 0:2 * aw]
        th = th_ref[...]
        a = 0.5 * ha * (1.0 + th)
        u = a[:, :aw]
        lngv = lng_ref[...]
        vn, vhat, rs = _layer_norm_parts(a[:, aw:], lngv, lnb_ref[...])
        vnb = vn.astype(BF16)
        wm = _tril_weights(wsp_ref)
        mixed = mix_ref[...]
        gav = ga_ref[...]
        _, yah, ra = _rms_fwd(u * mixed, gav)
        dya, dga = _rms_bwd(dycat[:, :aw], yah, ra, gav)
        dga_ref[...] += dga
        du = dya * mixed
        dmix = dya * u
        dmb = dmix.astype(BF16)
        tri = lax.broadcasted_iota(jnp.int32, (CHUNK, CHUNK), 0) >= lax.broadcasted_iota(jnp.int32, (CHUNK, CHUNK), 1)
        for hd in range(HEADS):
            dw = jnp.zeros((CHUNK, CHUNK), F32)
            db = jnp.zeros((CHUNK, 1), F32)
            for c in range(nch):
                rows = slice(c * CHUNK, (c + 1) * CHUNK)
                cols = slice(hd * hd_w, (hd + 1) * hd_w)
                dvn_ref[rows, cols] = _dot_tn(wm[hd], dmb[rows, cols])
                dw = dw + _dot_nt(dmb[rows, cols], vnb[rows, cols])
                db = db + jnp.sum(dmix[rows, cols], axis=1, keepdims=True)
            dwsp_ref[hd] += jnp.where(tri, dw, 0.0)
            dbt_ref[:, hd:hd + 1] += db
        dvn = dvn_ref[...]
        dlng_ref[...] += jnp.sum(dvn * vhat, axis=0, keepdims=True)
        dlnb_ref[...] += jnp.sum(dvn, axis=0, keepdims=True)
        dvh = dvn * lngv
        dv = rs * (dvh - jnp.mean(dvh, axis=-1, keepdims=True) - vhat * jnp.mean(dvh * vhat, axis=-1, keepdims=True))
        gprime = 0.5 * (1.0 + th) + 0.5 * ha * (1.0 - th * th) * (GELU_C * (1.0 + 3.0 * GELU_K * (ha * ha)))
        dh_ref[:, 0:2 * aw] = (jnp.concatenate([du, dv], axis=-1) * gprime).astype(BF16)
        g_b = h_ref[:, 2 * aw:3 * aw]
        g_c = h_ref[:, 3 * aw:4 * aw]
        val = h_ref[:, 4 * aw:5 * aw]
        z = g_c * val
        zp = jnp.where(i == n - 1, 0.0, hp_ref[:, 3 * aw:4 * aw] * hp_ref[:, 4 * aw:5 * aw])
        zm1, zm2 = _shift_rows(z, zp)
        cw0, cw1, cw2 = cw_ref[0:1, :], cw_ref[1:2, :], cw_ref[2:3, :]
        conv = cw0 * zm2 + cw1 * zm1 + cw2 * z
        gbv = gb_ref[...]
        _, ybh, rb = _rms_fwd(g_b * conv, gbv)
        dyb, dgb = _rms_bwd(dycat[:, aw:], ybh, rb, gbv)
        dgb_ref[...] += dgb
        dconv = dyb * g_b
        dcw_ref[0:1, :] += jnp.sum(dconv * zm2, axis=0, keepdims=True)
        dcw_ref[1:2, :] += jnp.sum(dconv * zm1, axis=0, keepdims=True)
        dcw_ref[2:3, :] += jnp.sum(dconv * z, axis=0, keepdims=True)
        nxt = dcn_ref[...]
        row = lax.broadcasted_iota(jnp.int32, dconv.shape, 0)
        dcp1 = jnp.where(row == TM - 1, nxt[0:1, :], pltpu.roll(dconv, TM - 1, 0))
        dcp2 = jnp.where(row == TM - 1, nxt[1:2, :],
                         jnp.where(row == TM - 2, nxt[0:1, :], pltpu.roll(dconv, TM - 2, 0)))
        dz = cw2 * dconv + cw1 * dcp1 + cw0 * dcp2
        dcn_ref[...] = dconv[0:8, :]
        dh_ref[:, 2 * aw:3 * aw] = (dyb * conv).astype(BF16)
        dh_ref[:, 3 * aw:4 * aw] = (dz * val).astype(BF16)
        dh_ref[:, 4 * aw:5 * aw] = (dz * g_c).astype(BF16)
        dxn = jnp.zeros((TM, d), F32)
        for k in range(N_CHIPS):
            dxn = dxn + _dot_nt(dh_ref[:, k * ns:(k + 1) * ns], win_ref[k])
        g1v = g1_ref[...]
        _, xh1, r1 = _rms_fwd(x_ref[...], g1v)
        dxr, dg1 = _rms_bwd(dxn, xh1, r1, g1v)
        dg1_ref[...] += dg1
        dx_ref[...] = dx1v + dxr

    ins = (x, dx1, h, h, mixed_all, th_all, g1, lng, lnb, wsp, b_t, cw, ga, gb, w_out, w_in)
    acc_shapes = [(1, d), (1, aw), (1, aw), wsp.shape, (CHUNK, CHUNK), cw.shape, (1, aw), (1, aw)]
    return _host_call(
        body, "mixer_bwd", (n,),
        [rev(d), rev(d), rev(nh), hprev, rev(aw), rev(d)] + [_full(a.shape) for a in ins[6:]],
        [rev(d), rev(nh)] + [_acc(sh) for sh in acc_shapes],
        (jax.ShapeDtypeStruct((s, d), F32), jax.ShapeDtypeStruct((s, nh), BF16))
        + tuple(jax.ShapeDtypeStruct(sh, F32) for sh in acc_shapes),
        [pltpu.VMEM((TM, aw), F32), pltpu.VMEM((8, aw), F32)],
        ins, ("arbitrary",), hosted)


def _weight_grad(a, b, name, tm, tn, col_sharded, hosted=None):
    t, m = a.shape
    n = b.shape[1]

    def body(a_ref, b_ref, o_ref):
        o_ref[...] = _dot_tn(a_ref[...].astype(BF16), b_ref[...].astype(BF16)).astype(BF16)

    if col_sharded:
        ns = n // N_CHIPS
        per = ns // tn
        out_shape = jax.ShapeDtypeStruct((N_CHIPS, m, ns), BF16)
        out_spec = pl.BlockSpec((None, tm, tn), lambda i, j: (j // per, i, j % per))
    else:
        out_shape = jax.ShapeDtypeStruct((m, n), BF16)
        out_spec = pl.BlockSpec((tm, tn), lambda i, j: (i, j))
    (out,), extra = _host_call(
        body, name, (m // tm, n // tn),
        [pl.BlockSpec((t, tm), lambda i, j: (0, i)), pl.BlockSpec((t, tn), lambda i, j: (0, j))],
        [out_spec], (out_shape,), [], (a, b), ("parallel", "parallel"), hosted)
    return (out if col_sharded else out.reshape(N_CHIPS, m // N_CHIPS, n)), extra


def _row_tile(rows, cap=256):
    best = None
    for t in range(16, min(rows, cap) + 1, 16):
        if rows % t == 0:
            best = t
    return best if best is not None else rows


def _adamw_math(w, g, m, v):
    m2 = ADAM_B1 * m + (1.0 - ADAM_B1) * g
    v2 = ADAM_B2 * v + (1.0 - ADAM_B2) * (g * g)
    m_hat = m2 / (1.0 - ADAM_B1 ** ADAM_STEP)
    v_hat = v2 / (1.0 - ADAM_B2 ** ADAM_STEP)
    delta = -ADAM_LR * (m_hat / (jnp.sqrt(v_hat) + ADAM_EPS) + ADAM_WD * w)
    return delta, m2, v2


def _adamw(w, g, m, v, name):
    r, c = w.shape
    tr = _row_tile(r) if r >= 16 else r

    def body(w_ref, g_ref, m_ref, v_ref, d_ref, m2_ref, v2_ref):
        d_ref[...], m2_ref[...], v2_ref[...] = _adamw_math(w_ref[...], g_ref[...], m_ref[...], v_ref[...])

    sh = jax.ShapeDtypeStruct((r, c), F32)
    return pl.pallas_call(
        body, name=name, grid=(r // tr,),
        in_specs=[_rows(tr, c)] * 4, out_specs=[_rows(tr, c)] * 3, out_shape=(sh, sh, sh),
        compiler_params=_params("parallel"),
    )(w, g, m, v)


def _finalize(items, place, name, hosted=None):
    r, c = items[0][2].shape
    tr = _row_tile(r)
    nw = len(items)

    def body(place_ref, *refs):
        ins, outs = refs[:7 * nw], refs[7 * nw:]
        for k in range(nw):
            own_ref, s0_ref, s1_ref, s2_ref, w_ref, m_ref, v_ref = ins[7 * k:7 * k + 7]
            g_ref, d_ref, m2_ref, v2_ref = outs[4 * k:4 * k + 4]
            g = own_ref[...].astype(F32) + s0_ref[...].astype(F32)
            g = (g + s1_ref[...].astype(F32)) + s2_ref[...].astype(F32)
            g_ref[...] = g
            d_ref[...], m2_ref[...], v2_ref[...] = _adamw_math(w_ref[...], g, m_ref[...], v_ref[...])

    def slot(k):
        return pl.BlockSpec((None, tr, c), lambda i, pref: (k, i, 0))

    rows = pl.BlockSpec((tr, c), lambda i, pref: (i, 0))
    sh = jax.ShapeDtypeStruct((r, c), F32)
    one = [pl.BlockSpec((None, tr, c), lambda i, pref: (pref[1], i, 0)), slot(0), slot(1), slot(2), rows, rows, rows]
    args = [a for part, slots, w, m, v in items for a in (part, slots, slots, slots, w, m, v)]
    res, extra = _host_call(body, name, (r // tr,), one * nw, [rows] * (4 * nw), (sh,) * (4 * nw), [], args,
                            ("parallel",), hosted, prefetch=(place,))
    return [res[4 * k:4 * k + 4] for k in range(nw)], extra


def _small_sum_adamw(parts, w, m, v):
    nd, r, c = parts.shape

    def body(p_ref, w_ref, m_ref, v_ref, g_ref, d_ref, m2_ref, v2_ref):
        g = p_ref[0]
        for k in range(1, nd):
            g = g + p_ref[k]
        g_ref[...] = g
        d_ref[...], m2_ref[...], v2_ref[...] = _adamw_math(w_ref[...], g, m_ref[...], v_ref[...])

    sh = jax.ShapeDtypeStruct((r, c), F32)
    return pl.pallas_call(
        body, name="small_sum_adamw", out_shape=(sh, sh, sh, sh),
        compiler_params=pltpu.CompilerParams(vmem_limit_bytes=VMEM_LIMIT),
    )(parts, w, m, v)


def _place():
    x, y, c = lax.axis_index("x"), lax.axis_index("y"), lax.axis_index("c")
    chips = [(1 - x, y), (x, 1 - y), (1 - x, 1 - y)]
    return x, y, c, 2 * x + y, chips


def _remote(src, dst, send_sem, recv_sem, to):
    return pltpu.make_async_remote_copy(src_ref=src, dst_ref=dst, send_sem=send_sem, recv_sem=recv_sem,
                                        device_id=to, device_id_type=MESH)


class _Exchange:
    def __init__(self, ins, out_shapes, sem_shape, start, finish, middle=None, in_place=False, peers=()):
        self.ins, self.out_shapes, self.sem_shape = tuple(ins), tuple(out_shapes), sem_shape
        self.start, self.finish, self.middle = start, finish, middle
        self.in_place = in_place
        self.peers = frozenset(peers)
        self.collective_id = None

    def with_id(self, collective_id):
        self.collective_id = collective_id
        return self


_FLIPS = {"c": (0, 0, 1), "x": (1, 0, 0), "y": (0, 1, 0), "xy": (1, 1, 0)}


def _handshake(peers):
    x, y, c, _, _ = _place()
    barrier = pltpu.get_barrier_semaphore()
    for name in sorted(peers):
        fx, fy, fc = _FLIPS[name]
        peer = (1 - x if fx else x, 1 - y if fy else y, 1 - c if fc else c)
        pl.semaphore_signal(barrier, inc=1, device_id=peer, device_id_type=MESH)
    pl.semaphore_wait(barrier, len(peers))


def _run_exchange(ex, name, collective_id=None):
    n_in, n_out = len(ex.ins), len(ex.out_shapes)

    def body(*refs):
        ins, outs = refs[:n_in], refs[n_in:n_in + n_out]
        send_sems, recv_sems = refs[n_in + n_out:]
        if collective_id is not None:
            _handshake(ex.peers)
        ex.start(ins, outs, send_sems, recv_sems)
        if ex.middle is not None:
            ex.middle(ins, outs, send_sems, recv_sems)
        ex.finish(ins, outs, send_sems, recv_sems)

    assert collective_id is None or ex.peers
    sem = pltpu.SemaphoreType.DMA(ex.sem_shape)
    return pl.pallas_call(
        body, name=name, out_shape=ex.out_shapes, in_specs=[ANY] * n_in, out_specs=[ANY] * n_out,
        input_output_aliases={k: k for k in range(n_in)} if ex.in_place else {}, scratch_shapes=[sem, sem],
        compiler_params=pltpu.CompilerParams(collective_id=collective_id),
    )(*ex.ins)


def _host_call(body, name, grid, in_specs, out_specs, out_shape, scratch_shapes, args, semantics, hosted,
               prefetch=()):
    hosted = [] if hosted is None else (list(hosted) if isinstance(hosted, (list, tuple)) else [hosted])
    collective_id = hosted[0].collective_id if hosted else None
    peers = frozenset().union(*[ex.peers for ex in hosted]) if hosted else frozenset()
    assert collective_id is None or all(ex.peers for ex in hosted)
    n_pre, n_in, n_out, n_scr = len(prefetch), len(in_specs), len(out_specs), len(scratch_shapes)
    h_ins = [a for ex in hosted for a in ex.ins]
    h_outs = [s for ex in hosted for s in ex.out_shapes]
    h_in, h_out = len(h_ins), len(h_outs)

    def wrapped(*refs):
        pre, refs = refs[:n_pre], refs[n_pre:]
        a, hi = refs[:n_in], refs[n_in:n_in + h_in]
        o = refs[n_in + h_in:n_in + h_in + n_out]
        ho = refs[n_in + h_in + n_out:n_in + h_in + n_out + h_out]
        scr = refs[n_in + h_in + n_out + h_out:]

        def run(phase):
            i0 = o0 = 0
            for k, ex in enumerate(hosted):
                fn = getattr(ex, phase)
                if fn is not None:
                    fn(hi[i0:i0 + len(ex.ins)], ho[o0:o0 + len(ex.out_shapes)], scr[n_scr + 2 * k],
                       scr[n_scr + 2 * k + 1])
                i0, o0 = i0 + len(ex.ins), o0 + len(ex.out_shapes)

        if hosted:
            first = functools.reduce(jnp.logical_and, [pl.program_id(k) == 0 for k in range(len(grid))])

            @pl.when(first)
            def _():
                if collective_id is not None:
                    _handshake(peers)
                run("start")

        if any(ex.middle is not None for ex in hosted):
            half_way = functools.reduce(jnp.logical_and, [
                pl.program_id(0) == max(1, grid[0] * MIDDLE_STEP_16THS // 16)] + [
                pl.program_id(k) == 0 for k in range(1, len(grid))])

            @pl.when(half_way)
            def _():
                run("middle")

        body(*pre, *a, *o, *scr[:n_scr])

        if hosted:
            last = functools.reduce(jnp.logical_and, [pl.program_id(k) == grid[k] - 1 for k in range(len(grid))])

            @pl.when(last)
            def _():
                run("finish")

    sems = [pltpu.SemaphoreType.DMA(ex.sem_shape) for ex in hosted for _ in range(2)]
    aliases, i0, o0 = {}, n_pre + n_in, n_out
    for ex in hosted:
        if ex.in_place:
            aliases.update({i0 + k: o0 + k for k in range(len(ex.ins))})
        i0, o0 = i0 + len(ex.ins), o0 + len(ex.out_shapes)
    all_in, all_out = list(in_specs) + [ANY] * h_in, list(out_specs) + [ANY] * h_out
    all_scr = list(scratch_shapes) + sems
    params = _params(*(["arbitrary"] * len(grid) if hosted else semantics), collective_id=collective_id)
    shapes = tuple(out_shape) + tuple(h_outs)
    if n_pre:
        call = pl.pallas_call(
            wrapped, name=name, out_shape=shapes, input_output_aliases=aliases, compiler_params=params,
            grid_spec=pltpu.PrefetchScalarGridSpec(num_scalar_prefetch=n_pre, grid=grid, in_specs=all_in,
                                                   out_specs=all_out, scratch_shapes=all_scr))
    else:
        call = pl.pallas_call(
            wrapped, name=name, grid=grid, in_specs=all_in, out_specs=all_out, out_shape=shapes,
            scratch_shapes=all_scr, input_output_aliases=aliases, compiler_params=params)
    res = call(*prefetch, *args, *h_ins)
    return res[:n_out], res[n_out:]


def _all_gather(shards, small=()):
    items = tuple(shards) + tuple(small)
    nw = len(shards)

    def place():
        x, y, c, me, _ = _place()
        first = (x + (1 - c) * (1 - 2 * x), y + c * (1 - 2 * y))
        second = (x + c * (1 - 2 * x), y + (1 - c) * (1 - 2 * y))
        diag = (1 - x, 1 - y)
        return x, y, c, me, (first, second, diag)

    def halves(w, c):
        rh = items[w].shape[0] // 2
        return pl.ds(c * rh, rh), pl.ds((1 - c) * rh, rh)

    def start(ins, outs, ss, rs):
        x, y, c, me, chips = place()
        for w in range(len(items)):
            _remote(ins[w], outs[w].at[me], ss.at[w, 6], rs.at[w, 6], (x, y, 1 - c)).start()
            if w < nw:
                mine, _ = halves(w, c)
                _remote(ins[w].at[mine], outs[w].at[me, mine], ss.at[w, 0], rs.at[w, 0], (*chips[0], c)).start()
            else:
                for k in range(3):
                    _remote(ins[w], outs[w].at[me], ss.at[w, k], rs.at[w, k], (*chips[k], c)).start()

    def onward(outs, ss, rs, w, k, x, y, c, chips):
        mine, _ = halves(w, c)
        pk = 2 * chips[k][0] + chips[k][1]
        got = outs[w].at[pk, mine]
        src = chips[1] if k == 2 else chips[k]
        _remote(got, got, ss.at[w, k], rs.at[w, k], (*src, c)).wait_recv()
        if k == 0:
            _remote(got, got, ss.at[w, 2], rs.at[w, 2], (*chips[1], c)).start()
        _remote(got, got, ss.at[w, 3 + k], rs.at[w, 3 + k], (x, y, 1 - c)).start()

    def middle(ins, outs, ss, rs):
        x, y, c, me, chips = place()
        for w in range(nw):
            mine, _ = halves(w, c)
            _remote(ins[w].at[mine], outs[w].at[me, mine], ss.at[w, 1], rs.at[w, 1], (*chips[1], c)).start()
        for w in range(nw):
            onward(outs, ss, rs, w, 0, x, y, c, chips)

    def finish(ins, outs, ss, rs):
        x, y, c, me, chips = place()
        sib = (x, y, 1 - c)
        for k in (1, 2):
            for w in range(nw):
                onward(outs, ss, rs, w, k, x, y, c, chips)
        for w in range(len(items)):
            if w < nw:
                mine, theirs = halves(w, c)
                for k, chip in ((3, chips[1]), (4, chips[0]), (5, chips[2])):
                    oth = outs[w].at[2 * chip[0] + chip[1], theirs]
                    _remote(oth, oth, ss.at[w, k], rs.at[w, k], sib).wait_recv()
                own = ins[w].at[mine]
                for k in range(6):
                    _remote(own, own, ss.at[w, k], rs.at[w, k], sib).wait_send()
            else:
                for k in range(3):
                    got = outs[w].at[2 * chips[k][0] + chips[k][1]]
                    _remote(got, got, ss.at[w, k], rs.at[w, k], (*chips[k], c)).wait_recv()
                    _remote(ins[w], ins[w], ss.at[w, k], rs.at[w, k], sib).wait_send()
            _remote(ins[w], outs[w].at[me], ss.at[w, 6], rs.at[w, 6], sib).wait()

    out_shapes = tuple(jax.ShapeDtypeStruct((N_CHIPS,) + a.shape, a.dtype) for a in items)
    return _Exchange(items, out_shapes, (len(items), 7), start, finish, middle if nw else None,
                     peers=("c", "x", "y", "xy") if small else ("c", "x", "y"))


def _chip_reduce(grads, name, collective_id):
    nw = len(grads)
    step = 64

    def body(*refs):
        ins, outs = refs[:nw], refs[nw:2 * nw]
        own, got = refs[2 * nw:3 * nw], refs[3 * nw:4 * nw]
        send_sems, recv_sems, local_sems = refs[4 * nw:]
        x, y, c, _, _ = _place()
        barrier = pltpu.get_barrier_semaphore()
        pl.semaphore_signal(barrier, inc=1, device_id=(x, y, 1 - c), device_id_type=MESH)
        pl.semaphore_wait(barrier, 1)
        moves = []
        for w in range(nw):
            rh = grads[w].shape[1] // 2
            away = _remote(ins[w].at[:, pl.ds((1 - c) * rh, rh), :], got[w], send_sems.at[w], recv_sems.at[w],
                           (x, y, 1 - c))
            mine = pltpu.make_async_copy(ins[w].at[:, pl.ds(c * rh, rh), :], own[w], local_sems.at[w])
            away.start()
            mine.start()
            moves.append((away, mine))
        back = []
        for w, (away, mine) in enumerate(moves):
            nb, rh, _ = own[w].shape
            mine.wait()
            away.wait()
            for k in range(nb):
                def add(i, carry, w=w, k=k):
                    rows = pl.ds(pl.multiple_of(i * step, step), step)
                    own[w][k, rows, :] = (own[w][k, rows, :].astype(F32) + got[w][k, rows, :].astype(F32)).astype(BF16)
                    return carry
                lax.fori_loop(0, rh // step, add, 0)
                tail = rh % step
                if tail:
                    rows = slice(rh - tail, rh)
                    own[w][k, rows, :] = (own[w][k, rows, :].astype(F32) + got[w][k, rows, :].astype(F32)).astype(BF16)
            wb = pltpu.make_async_copy(own[w], outs[w].at[:, pl.ds(c * rh, rh), :], local_sems.at[w])
            wb.start()
            back.append(wb)
        for wb in back:
            wb.wait()

    halves = [pltpu.VMEM((g.shape[0], g.shape[1] // 2, g.shape[2]), BF16) for g in grads]
    sem = pltpu.SemaphoreType.DMA((nw,))
    return pl.pallas_call(
        body, name=name, out_shape=tuple(jax.ShapeDtypeStruct(g.shape, BF16) for g in grads),
        in_specs=[ANY] * nw, out_specs=[ANY] * nw, scratch_shapes=halves + halves + [sem, sem, sem],
        compiler_params=pltpu.CompilerParams(vmem_limit_bytes=VMEM_LIMIT, collective_id=collective_id),
    )(*grads)


def _scatter_partials(parts):
    nw = len(parts)

    def copies(ins, outs, ss, rs):
        _, _, c, _, chips = _place()
        res = []
        for r, (px, py) in enumerate(chips):
            for w in range(nw):
                rh = parts[w].shape[1] // 2
                rows = pl.ds(c * rh, rh)
                res.append(_remote(ins[w].at[2 * px + py, rows], outs[w].at[r, rows], ss.at[w, r], rs.at[w, r],
                                   (px, py, c)))
        return res

    def start(ins, outs, ss, rs):
        for cp in copies(ins, outs, ss, rs):
            cp.start()

    def finish(ins, outs, ss, rs):
        for cp in copies(ins, outs, ss, rs):
            cp.wait()

    out_shapes = tuple(jax.ShapeDtypeStruct((3,) + p.shape[1:], p.dtype) for p in parts)
    return _Exchange(parts, out_shapes, (nw, 3), start, finish, peers=("x", "y", "xy"))


def _join_partials(parts, slots):
    nw = len(parts)

    def copies(outs, ss, rs, mine):
        x, y, c, me, _ = _place()
        res = []
        for w in range(nw):
            rh = parts[w].shape[1] // 2
            rows = pl.ds((c if mine else 1 - c) * rh, rh)
            own = outs[w].at[me, rows]
            got = outs[nw + w].at[:, rows, :]
            res.append(_remote(own, own, ss.at[w, 0], rs.at[w, 0], (x, y, 1 - c)))
            res.append(_remote(got, got, ss.at[w, 1], rs.at[w, 1], (x, y, 1 - c)))
        return res

    def start(ins, outs, ss, rs):
        for cp in copies(outs, ss, rs, True):
            cp.start()

    def finish(ins, outs, ss, rs):
        for cp in copies(outs, ss, rs, True):
            cp.wait_send()
        for cp in copies(outs, ss, rs, False):
            cp.wait_recv()

    arrays = tuple(parts) + tuple(slots)
    return _Exchange(arrays, tuple(jax.ShapeDtypeStruct(a.shape, a.dtype) for a in arrays), (nw, 2), start, finish,
                     in_place=True, peers=("c",))


def _gather_small(slab):
    def copies(ins, outs, ss, rs):
        x, y, c, _, _ = _place()
        me = 4 * x + 2 * y + c
        out, arrivals = [], []
        for k in range(1, 8):
            px = 1 - x if k & 4 else x
            py = 1 - y if k & 2 else y
            pc = 1 - c if k & 1 else c
            out.append(_remote(ins[0], outs[0].at[me], ss.at[k - 1], rs.at[k - 1], (px, py, pc)))
            theirs = outs[0].at[4 * px + 2 * py + pc]
            arrivals.append((theirs, k - 1, (px, py, pc)))
        return pltpu.make_async_copy(ins[0], outs[0].at[me], ss.at[7]), out, arrivals

    def start(ins, outs, ss, rs):
        own, out, _ = copies(ins, outs, ss, rs)
        own.start()
        for cp in out:
            cp.start()

    def finish(ins, outs, ss, rs):
        own, out, arrivals = copies(ins, outs, ss, rs)
        for cp in out:
            cp.wait_send()
        for theirs, k, peer in arrivals:
            _remote(theirs, theirs, ss.at[k], rs.at[k], peer).wait_recv()
        own.wait()

    return _Exchange((slab,), (jax.ShapeDtypeStruct((8,) + slab.shape, slab.dtype),), (8,), start, finish)


_SMALL_VECS = ("ln_mix_g", "ln_attn_g", "ln_mem_g", "ln_ffn_g", "ln_final_g")


def _pack_small(p, extra, conv):
    d = p["ln_mix_g"].shape[-1]
    top = [p[k].reshape(1, d) for k in _SMALL_VECS]
    top.append(jnp.concatenate([p["sgu_ln_g"].reshape(-1), p["sgu_ln_b"].reshape(-1)]).reshape(1, d))
    top.append(jnp.concatenate([p["grp_norm_a"].reshape(-1), p["grp_norm_b"].reshape(-1)]).reshape(1, d))
    top.append(jnp.concatenate([p["b_spatial"].reshape(-1), extra]).reshape(1, d))
    mid = jnp.zeros((8, d), F32)
    if conv is not None:
        mid = jnp.pad(conv, ((0, 5), (0, d - conv.shape[1])))
    return jnp.concatenate([jnp.concatenate(top, axis=0), mid, p["w_spatial"].reshape(-1, d)], axis=0)


def _unpack_small(slab):
    d = slab.shape[1]
    hw = d // 2
    out = {k: slab[i] for i, k in enumerate(_SMALL_VECS)}
    out["sgu_ln_g"], out["sgu_ln_b"] = slab[5, :hw], slab[5, hw:]
    out["grp_norm_a"], out["grp_norm_b"] = slab[6, :hw], slab[6, hw:]
    out["b_spatial"] = slab[7, :hw].reshape(HEADS, CHUNK)
    out["w_spatial"] = slab[16:].reshape(HEADS, CHUNK, CHUNK)
    return out


_BIG = ("w_in", "w_kv", "w_gate_up", "w_out", "w_q", "w_o", "w_down")
_WEIGHTS = ("ln_mix_g", "w_in", "sgu_ln_g", "sgu_ln_b", "w_spatial", "b_spatial", "conv_w", "grp_norm_a",
            "grp_norm_b", "w_out", "ln_attn_g", "ln_mem_g", "w_q", "w_kv", "w_o", "ln_ffn_g", "w_gate_up",
            "w_down", "ln_final_g")


def _step(p, m_, v_, x, mem, target):
    s, d = x.shape
    hw = d // 2
    row = lambda a: a.reshape(1, -1)
    x_, y_, c_ = lax.axis_index("x"), lax.axis_index("y"), lax.axis_index("c")
    chip = 2 * x_ + y_

    bf = {k: p[k].astype(BF16) for k in _BIG}
    conv8 = jnp.pad(p["conv_w"], ((0, 5), (0, 0)))
    w_in, w_out4, w_kv, w_q4, w_o4, conv4 = _run_exchange(
        _all_gather([bf[k] for k in ("w_in", "w_out", "w_kv", "w_q", "w_o")], [conv8]), "all_gather_first",
        collective_id=4)
    cw = jnp.transpose(conv4[:, :3, :], (1, 0, 2)).reshape(3, hw)
    b_t = jnp.pad(jnp.transpose(p["b_spatial"]), ((0, 0), (0, CHUNK - HEADS)))
    g1, g2, gm, g3, gf = (row(p[k]) for k in _SMALL_VECS)
    lng, lnb, ga, gb = row(p["sgu_ln_g"]), row(p["sgu_ln_b"]), row(p["grp_norm_a"]), row(p["grp_norm_b"])
    wsp = p["w_spatial"]
    w_out = w_out4.reshape(-1, d)

    (h, x1, ycat, xn1, mixed, th), (w_gu,) = _mixer_fwd(
        x, g1, w_in, lng, lnb, wsp, b_t, cw, ga, gb, w_out, hosted=_all_gather([bf["w_gate_up"]]).with_id(5))
    w_q, w_o = w_q4.reshape(-1, d), w_o4.reshape(-1, d)
    memn, kv = _kv_fwd(mem, gm, w_kv)
    (x2, o, qs, probs), (w_down4,) = _attn_fwd(x1, g2, w_q, kv, w_o, hosted=_all_gather([bf["w_down"]]).with_id(6))
    w_down = w_down4.reshape(-1, d)
    dx2, act, dgu, xn3, dx3, loss, dgf, dg3 = _ffn_fwd_bwd(x2, g3, gf, target, w_gu, w_down)

    place = jnp.stack([c_, chip]).astype(jnp.int32)

    def chip_partials(names, grads, tag):
        return list(_chip_reduce(grads, "chip_reduce_" + tag, ("down", "ffn", "attn", "mixer").index(tag)))

    names_d = ("w_down",)
    parts_d = chip_partials(names_d, (_weight_grad(act, dx3, "grad_w_down", 1408, 512, False)[0],), "down")
    g_gu, slots_d = _weight_grad(xn3, dgu, "grad_w_gate_up", 512, 1408, True,
                                 hosted=_scatter_partials(parts_d).with_id(7))
    names_a = ("w_gate_up",)
    parts_a = chip_partials(names_a, (g_gu,), "ffn")
    (dx1, dkv, dg2, g_o, g_q, g_out), slots_a = _attn_bwd(x1, dx2, o, ycat, qs, probs, g2, w_q, kv, w_o,
                                                           hosted=_scatter_partials(parts_a).with_id(8))
    g_kv, dgm = _kv_bwd(dkv, mem, memn, gm, w_kv)
    names_b = ("w_o", "w_out", "w_q", "w_kv")
    shard_major = lambda g: g.reshape(N_CHIPS, -1, d)
    parts_b = chip_partials(names_b, (shard_major(g_o), shard_major(g_out), shard_major(g_q), g_kv), "attn")
    names_da = names_d + names_a
    (dx, dh, dg1, dlng, dlnb, dwsp, dbt, dcw, dga, dgb), extra = _mixer_bwd(
        x, dx1, h, mixed, th, g1, lng, lnb, wsp, b_t, cw, ga, gb, w_out, w_in,
        hosted=[_scatter_partials(parts_b).with_id(9), _join_partials(parts_d + parts_a, slots_d + slots_a)])
    slots_b, joined = extra[:len(names_b)], extra[len(names_b):]
    whole = dict(zip(names_da, zip(joined[:len(names_da)], joined[len(names_da):])))
    small = {"ln_mix_g": dg1, "ln_attn_g": dg2, "ln_mem_g": dgm, "ln_ffn_g": dg3, "ln_final_g": dgf,
             "sgu_ln_g": dlng, "sgu_ln_b": dlnb, "grp_norm_a": dga, "grp_norm_b": dgb,
             "b_spatial": jnp.transpose(dbt[:, :HEADS]), "w_spatial": dwsp}
    loss_vec = jnp.pad(loss.reshape(1), (0, hw - 1))
    g_in, extra = _weight_grad(
        xn1, dh, "grad_w_in", 1024, 640, True,
        hosted=[_gather_small(_pack_small(small, loss_vec, dcw)), _join_partials(parts_b, slots_b)])
    parts = extra[0]
    whole.update(zip(names_b, zip(extra[1:1 + len(names_b)], extra[1 + len(names_b):])))
    (part_in,) = chip_partials(("w_in",), (g_in,), "mixer")
    out_g, out_d, out_m, out_v = {}, {}, {}, {}

    def finalize(ks, tag, hosted=None):
        done, res = _finalize([(whole[k][0], whole[k][1], p[k], m_[k], v_[k]) for k in ks], place,
                              "finalize_" + tag, hosted)
        for k, (g, dl, nm, nv) in zip(ks, done):
            out_g[k], out_d[k], out_m[k], out_v[k] = g, dl, nm, nv
        return res

    (slots_in,) = finalize(("w_down",), "w_down", _scatter_partials([part_in]).with_id(10))
    finalize(("w_o", "w_out", "w_q"), "attn")
    for k in ("w_gate_up", "w_kv"):
        finalize((k,), k)
    whole["w_in"] = _run_exchange(_join_partials([part_in], [slots_in]), "rs_join_mixer", collective_id=11)
    finalize(("w_in",), "w_in")

    zeros = jnp.zeros((hw,), F32)
    sg, sd, sm, sv = _small_sum_adamw(parts, _pack_small(p, zeros, None), _pack_small(m_, zeros, None),
                                      _pack_small(v_, zeros, None))
    for tree, slab in zip((out_g, out_d, out_m, out_v), (sg, sd, sm, sv)):
        tree.update(_unpack_small(slab))
    loss_out = sg[7, hw]
    g_conv = lax.dynamic_slice(sg[8:11, :hw], (0, chip * (hw // N_CHIPS)), (3, hw // N_CHIPS))
    out_g["conv_w"] = g_conv
    out_d["conv_w"], out_m["conv_w"], out_v["conv_w"] = _adamw(p["conv_w"], g_conv, m_["conv_w"], v_["conv_w"],
                                                                "adamw_conv_w")
    return loss_out, dx, out_g, out_d, out_m, out_v


def kernel(x, mem, ln_mix_g, w_in, sgu_ln_g, sgu_ln_b, w_spatial, b_spatial, conv_w, grp_norm_a, grp_norm_b, w_out, ln_attn_g, ln_mem_g, w_q, w_kv, w_o, ln_ffn_g, w_gate_up, w_down, ln_final_g, loss_target, m_ln_mix_g, m_w_in, m_sgu_ln_g, m_sgu_ln_b, m_w_spatial, m_b_spatial, m_conv_w, m_grp_norm_a, m_grp_norm_b, m_w_out, m_ln_attn_g, m_ln_mem_g, m_w_q, m_w_kv, m_w_o, m_ln_ffn_g, m_w_gate_up, m_w_down, m_ln_final_g, v_ln_mix_g, v_w_in, v_sgu_ln_g, v_sgu_ln_b, v_w_spatial, v_b_spatial, v_conv_w, v_grp_norm_a, v_grp_norm_b, v_w_out, v_ln_attn_g, v_ln_mem_g, v_w_q, v_w_kv, v_w_o, v_ln_ffn_g, v_w_gate_up, v_w_down, v_ln_final_g):
    p = dict(ln_mix_g=ln_mix_g, w_in=w_in, sgu_ln_g=sgu_ln_g, sgu_ln_b=sgu_ln_b, w_spatial=w_spatial,
             b_spatial=b_spatial, conv_w=conv_w, grp_norm_a=grp_norm_a, grp_norm_b=grp_norm_b, w_out=w_out,
             ln_attn_g=ln_attn_g, ln_mem_g=ln_mem_g, w_q=w_q, w_kv=w_kv, w_o=w_o, ln_ffn_g=ln_ffn_g,
             w_gate_up=w_gate_up, w_down=w_down, ln_final_g=ln_final_g)
    m_ = dict(ln_mix_g=m_ln_mix_g, w_in=m_w_in, sgu_ln_g=m_sgu_ln_g, sgu_ln_b=m_sgu_ln_b, w_spatial=m_w_spatial,
              b_spatial=m_b_spatial, conv_w=m_conv_w, grp_norm_a=m_grp_norm_a, grp_norm_b=m_grp_norm_b,
              w_out=m_w_out, ln_attn_g=m_ln_attn_g, ln_mem_g=m_ln_mem_g, w_q=m_w_q, w_kv=m_w_kv, w_o=m_w_o,
              ln_ffn_g=m_ln_ffn_g, w_gate_up=m_w_gate_up, w_down=m_w_down, ln_final_g=m_ln_final_g)
    v_ = dict(ln_mix_g=v_ln_mix_g, w_in=v_w_in, sgu_ln_g=v_sgu_ln_g, sgu_ln_b=v_sgu_ln_b, w_spatial=v_w_spatial,
              b_spatial=v_b_spatial, conv_w=v_conv_w, grp_norm_a=v_grp_norm_a, grp_norm_b=v_grp_norm_b,
              w_out=v_w_out, ln_attn_g=v_ln_attn_g, ln_mem_g=v_ln_mem_g, w_q=v_w_q, w_kv=v_w_kv, w_o=v_w_o,
              ln_ffn_g=v_ln_ffn_g, w_gate_up=v_w_gate_up, w_down=v_w_down, ln_final_g=v_ln_final_g)
    s, d = x.shape[-2], x.shape[-1]
    loss, dx, g, dl, nm, nv = _step(p, m_, v_, x.reshape(s, d), mem.reshape(-1, d), loss_target.reshape(s, d))
    outs = [loss, dx.reshape(x.shape)]
    for tree in (g, dl, nm, nv):
        outs += [tree[k].reshape(p[k].shape) for k in _WEIGHTS]
    return tuple(outs)
```

```python
import functools
import math

import jax
import jax.numpy as jnp
from jax import lax
from jax.experimental import pallas as pl
from jax.experimental.pallas import tpu as pltpu

F32 = jnp.float32
BF16 = jnp.bfloat16
EPS = 1e-6
CHUNK = 128
HEADS = 4
N_CHIPS = 4
TM = 512
TM_ATTN = 512
TM_FFN = 256
ADAM_LR, ADAM_B1, ADAM_B2, ADAM_EPS, ADAM_WD, ADAM_STEP = 0.001, 0.9, 0.999, 1e-08, 0.01, 10
GELU_C = math.sqrt(2.0 / math.pi)
GELU_K = 0.044715
SMALL_ROWS = 80
VMEM_LIMIT = 56 * 1024 * 1024
MIDDLE_STEP_16THS = 7
MESH = pl.DeviceIdType.MESH
ANY = pl.BlockSpec(memory_space=pl.ANY)


def _params(*sem, collective_id=None):
    return pltpu.CompilerParams(dimension_semantics=sem, vmem_limit_bytes=VMEM_LIMIT, collective_id=collective_id)


def _dot(a, b):
    return jnp.dot(a, b, preferred_element_type=F32)


def _dot_nt(a, b):
    return lax.dot_general(a, b, (((1,), (1,)), ((), ())), preferred_element_type=F32)


def _dot_tn(a, b):
    return lax.dot_general(a, b, (((0,), (0,)), ((), ())), preferred_element_type=F32)


def _rms_fwd(x, g):
    r = lax.rsqrt(jnp.mean(x * x, axis=-1, keepdims=True) + EPS)
    xh = x * r
    return xh * g, xh, r


def _rms_bwd(dy, xh, r, g):
    dxh = dy * g
    dx = r * (dxh - xh * jnp.mean(dxh * xh, axis=-1, keepdims=True))
    return dx, jnp.sum(dy * xh, axis=0, keepdims=True)


def _full(shape):
    nd = len(shape)
    return pl.BlockSpec(shape, lambda *_: (0,) * nd, pipeline_mode=pl.Buffered(1))


def _acc(shape):
    nd = len(shape)
    return pl.BlockSpec(shape, lambda *_: (0,) * nd)


def _rows(tm, cols):
    return pl.BlockSpec((tm, cols), lambda i: (i, 0))


def _tril_weights(wsp_ref):
    row = lax.broadcasted_iota(jnp.int32, (CHUNK, CHUNK), 0)
    col = lax.broadcasted_iota(jnp.int32, (CHUNK, CHUNK), 1)
    return [jnp.where(row >= col, wsp_ref[hd], 0.0).astype(BF16) for hd in range(HEADS)]


def _shift_rows(z, zp):
    row = lax.broadcasted_iota(jnp.int32, z.shape, 0)
    zm1 = jnp.where(row == 0, zp[7:8, :], pltpu.roll(z, 1, 0))
    zm2 = jnp.where(row == 0, zp[6:7, :], jnp.where(row == 1, zp[7:8, :], pltpu.roll(z, 2, 0)))
    return zm1, zm2


def _gelu_parts(x):
    t = jnp.tanh(GELU_C * (x + GELU_K * (x * x * x)))
    return 0.5 * x * (1.0 + t), t


def _layer_norm_parts(v, g, b):
    mu = jnp.mean(v, axis=-1, keepdims=True)
    vc = v - mu
    rs = lax.rsqrt(jnp.mean(vc * vc, axis=-1, keepdims=True) + EPS)
    vhat = vc * rs
    return vhat * g + b, vhat, rs


def _kv_fwd(mem, g_mem, w_kv):
    m, d = mem.shape
    ns = w_kv.shape[2]

    def body(mem_ref, g_ref, w_ref, memn_ref, kv_ref):
        y, _, _ = _rms_fwd(mem_ref[...], g_ref[...])
        yb = y.astype(BF16)
        memn_ref[...] = yb
        for k in range(N_CHIPS):
            kv_ref[:, k * ns:(k + 1) * ns] = _dot(yb, w_ref[k]).astype(BF16)

    return pl.pallas_call(
        body, name="kv_fwd",
        out_shape=(jax.ShapeDtypeStruct((m, d), BF16), jax.ShapeDtypeStruct((m, N_CHIPS * ns), BF16)),
        compiler_params=pltpu.CompilerParams(vmem_limit_bytes=VMEM_LIMIT),
    )(mem, g_mem, w_kv)


def _mixer_fwd(x, g1, w_in, lng, lnb, wsp, b_t, cw, ga, gb, w_out, hosted=None):
    s, d = x.shape
    n = s // TM
    nch = TM // CHUNK
    ns = w_in.shape[2]
    nh = N_CHIPS * ns
    aw = d // 2
    hd_w = aw // HEADS

    def body(x_ref, g1_ref, win_ref, lng_ref, lnb_ref, wsp_ref, bt_ref, cw_ref, ga_ref, gb_ref, wout_ref,
             h_ref, x1_ref, y_ref, xn_ref, mix_ref, th_ref, zp_ref):
        i = pl.program_id(0)

        @pl.when(i == 0)
        def _():
            zp_ref[...] = jnp.zeros_like(zp_ref)

        x = x_ref[...]
        xn, _, _ = _rms_fwd(x, g1_ref[...])
        xnb = xn.astype(BF16)
        xn_ref[...] = xnb
        for k in range(N_CHIPS):
            h_ref[:, k * ns:(k + 1) * ns] = _dot(xnb, win_ref[k])
        a, th = _gelu_parts(h_ref[:, 0:2 * aw])
        th_ref[...] = th
        u = a[:, :aw]
        vn, _, _ = _layer_norm_parts(a[:, aw:], lng_ref[...], lnb_ref[...])
        vnb = vn.astype(BF16)
        wm = _tril_weights(wsp_ref)
        for c in range(nch):
            for hd in range(HEADS):
                blk = vnb[c * CHUNK:(c + 1) * CHUNK, hd * hd_w:(hd + 1) * hd_w]
                mix_ref[c * CHUNK:(c + 1) * CHUNK, hd * hd_w:(hd + 1) * hd_w] = _dot(wm[hd], blk) + bt_ref[:, hd:hd + 1]
        ya, _, _ = _rms_fwd(u * mix_ref[...], ga_ref[...])
        g_b = h_ref[:, 2 * aw:3 * aw]
        z = h_ref[:, 3 * aw:4 * aw] * h_ref[:, 4 * aw:5 * aw]
        zm1, zm2 = _shift_rows(z, zp_ref[...])
        conv = cw_ref[0:1, :] * zm2 + cw_ref[1:2, :] * zm1 + cw_ref[2:3, :] * z
        yb, _, _ = _rms_fwd(g_b * conv, gb_ref[...])
        zp_ref[...] = z[TM - 8:TM, :]
        ycat = jnp.concatenate([ya, yb], axis=-1).astype(BF16)
        y_ref[...] = ycat
        x1_ref[...] = x + _dot(ycat, wout_ref[...])

    return _host_call(
        body, "mixer_fwd", (n,),
        [_rows(TM, d), _full(g1.shape), _full(w_in.shape), _full(lng.shape), _full(lnb.shape),
         _full(wsp.shape), _full(b_t.shape), _full(cw.shape), _full(ga.shape), _full(gb.shape),
         _full(w_out.shape)],
        [_rows(TM, nh), _rows(TM, d), _rows(TM, d), _rows(TM, d), _rows(TM, aw), _rows(TM, d)],
        (jax.ShapeDtypeStruct((s, nh), F32), jax.ShapeDtypeStruct((s, d), F32),
         jax.ShapeDtypeStruct((s, d), BF16), jax.ShapeDtypeStruct((s, d), BF16),
         jax.ShapeDtypeStruct((s, aw), F32), jax.ShapeDtypeStruct((s, d), F32)),
        [pltpu.VMEM((8, aw), F32)],
        (x, g1, w_in, lng, lnb, wsp, b_t, cw, ga, gb, w_out), ("arbitrary",), hosted)


def _attn_fwd(x1, g2, w_q, kv, w_o, hosted=None):
    s, d = x1.shape
    tm = min(TM_ATTN, s)
    n = s // tm
    dh = d // HEADS
    m = kv.shape[0]
    scale = dh ** -0.5

    def body(x1_ref, g2_ref, wq_ref, kv_ref, wo_ref, x2_ref, o_ref, q_ref, p_ref):
        x1v = x1_ref[...]
        xn, _, _ = _rms_fwd(x1v, g2_ref[...])
        q_ref[...] = _dot(xn.astype(BF16), wq_ref[...]).astype(BF16)
        for hd in range(HEADS):
            kh = kv_ref[:, hd * dh:(hd + 1) * dh]
            vh = kv_ref[:, d + hd * dh:d + (hd + 1) * dh]
            sc = _dot_nt(q_ref[:, hd * dh:(hd + 1) * dh], kh) * scale
            e = jnp.exp(sc - jnp.max(sc, axis=-1, keepdims=True))
            p = e / jnp.sum(e, axis=-1, keepdims=True)
            p_ref[:, hd * m:(hd + 1) * m] = p
            o_ref[:, hd * dh:(hd + 1) * dh] = _dot(p.astype(BF16), vh).astype(BF16)
        x2_ref[...] = x1v + _dot(o_ref[...], wo_ref[...])

    return _host_call(
        body, "attn_fwd", (n,),
        [_rows(tm, d), _full(g2.shape), _full(w_q.shape), _full(kv.shape), _full(w_o.shape)],
        [_rows(tm, d), _rows(tm, d), _rows(tm, d), _rows(tm, HEADS * m)],
        (jax.ShapeDtypeStruct((s, d), F32), jax.ShapeDtypeStruct((s, d), BF16), jax.ShapeDtypeStruct((s, d), BF16),
         jax.ShapeDtypeStruct((s, HEADS * m), F32)),
        [], (x1, g2, w_q, kv, w_o), ("parallel",), hosted)


def _ffn_fwd_bwd(x2, g3, gf, target, w_gu, w_down):
    s, d = x2.shape
    tm = min(TM_FFN, s)
    n = s // tm
    ns = w_gu.shape[2]
    ff = 2 * ns

    def body(x2_ref, g3_ref, gf_ref, t_ref, wgu_ref, wd_ref,
             dx2_ref, act_ref, dgu_ref, xn_ref, dx3_ref, loss_ref, dgf_ref, dg3_ref):
        i = pl.program_id(0)

        @pl.when(i == 0)
        def _():
            loss_ref[...] = jnp.zeros_like(loss_ref)
            dgf_ref[...] = jnp.zeros_like(dgf_ref)
            dg3_ref[...] = jnp.zeros_like(dg3_ref)

        x2v = x2_ref[...]
        xn, xh3, r3 = _rms_fwd(x2v, g3_ref[...])
        xnb = xn.astype(BF16)
        xn_ref[...] = xnb
        x3 = x2v
        saved = []
        for j in range(2):
            g = _dot(xnb, wgu_ref[j])
            u = _dot(xnb, wgu_ref[2 + j])
            sg = 1.0 / (1.0 + jnp.exp(-g))
            sl = g * sg
            actb = (sl * u).astype(BF16)
            act_ref[:, j * ns:(j + 1) * ns] = actb
            x3 = x3 + _dot(actb, wd_ref[j * ns:(j + 1) * ns, :])
            saved.append((u, sl, sg * (1.0 + g * (1.0 - sg))))
        gfv = gf_ref[...]
        y, xhf, rf = _rms_fwd(x3, gfv)
        e = y - t_ref[...]
        loss_ref[...] += 0.5 * jnp.sum(jnp.sum(e * e, axis=-1, keepdims=True), axis=0, keepdims=True) / d
        dx3, dgf = _rms_bwd(e / d, xhf, rf, gfv)
        dgf_ref[...] += dgf
        dx3b = dx3.astype(BF16)
        dx3_ref[...] = dx3b
        dxn = jnp.zeros_like(x2v)
        for j in range(2):
            u, sl, dsl = saved[j]
            dact = _dot_nt(dx3b, wd_ref[j * ns:(j + 1) * ns, :])
            dgb = (dact * u * dsl).astype(BF16)
            dub = (dact * sl).astype(BF16)
            dgu_ref[:, j * ns:(j + 1) * ns] = dgb
            dgu_ref[:, ff + j * ns:ff + (j + 1) * ns] = dub
            dxn = dxn + _dot_nt(dgb, wgu_ref[j]) + _dot_nt(dub, wgu_ref[2 + j])
        dxr, dg3 = _rms_bwd(dxn, xh3, r3, g3_ref[...])
        dg3_ref[...] += dg3
        dx2_ref[...] = dx3 + dxr

    vec = jax.ShapeDtypeStruct((1, d), F32)
    return pl.pallas_call(
        body, name="ffn_fwd_bwd", grid=(n,),
        in_specs=[_rows(tm, d), _full(g3.shape), _full(gf.shape), _rows(tm, d), _full(w_gu.shape),
                  _full(w_down.shape)],
        out_specs=[_rows(tm, d), _rows(tm, ff), _rows(tm, 2 * ff), _rows(tm, d), _rows(tm, d),
                   _acc((1, 1)), _acc((1, d)), _acc((1, d))],
        out_shape=(jax.ShapeDtypeStruct((s, d), F32), jax.ShapeDtypeStruct((s, ff), BF16),
                   jax.ShapeDtypeStruct((s, 2 * ff), BF16), jax.ShapeDtypeStruct((s, d), BF16),
                   jax.ShapeDtypeStruct((s, d), BF16), jax.ShapeDtypeStruct((1, 1), F32), vec, vec),
        compiler_params=_params("arbitrary"),
    )(x2, g3, gf, target, w_gu, w_down)


def _attn_bwd(x1, dx2, o, ycat, qs, probs, g2, w_q, kv, w_o, hosted=None):
    s, d = x1.shape
    tm = min(TM_ATTN, s)
    n = s // tm
    dh = d // HEADS
    scale = dh ** -0.5
    m = kv.shape[0]

    def body(x1_ref, dx2_ref, o_ref, y_ref, q_ref, p_ref, g2_ref, wq_ref, kv_ref, wo_ref,
             dx1_ref, dkv_ref, dg2_ref, gwo_out, gwq_out, gwout_out, dq_ref, gwo_ref, gwq_ref, gwout_ref):
        i = pl.program_id(0)

        @pl.when(i == 0)
        def _():
            for r in (dkv_ref, dg2_ref, gwo_ref, gwq_ref, gwout_ref):
                r[...] = jnp.zeros_like(r)

        xn, xh2, r2 = _rms_fwd(x1_ref[...], g2_ref[...])
        xnb = xn.astype(BF16)
        dx2v = dx2_ref[...]
        dx2b = dx2v.astype(BF16)
        gwo_ref[...] += _dot_tn(o_ref[...], dx2b)
        do = _dot_nt(dx2b, wo_ref[...])
        for hd in range(HEADS):
            qb = q_ref[:, hd * dh:(hd + 1) * dh]
            p = p_ref[:, hd * m:(hd + 1) * m]
            kh = kv_ref[:, hd * dh:(hd + 1) * dh]
            vh = kv_ref[:, d + hd * dh:d + (hd + 1) * dh]
            dob = do[:, hd * dh:(hd + 1) * dh].astype(BF16)
            dp = _dot_nt(dob, vh)
            ds = p * (dp - jnp.sum(dp * p, axis=-1, keepdims=True))
            dsb = (ds * scale).astype(BF16)
            dq_ref[:, hd * dh:(hd + 1) * dh] = _dot(dsb, kh).astype(BF16)
            dkv_ref[:, hd * dh:(hd + 1) * dh] += _dot_tn(dsb, qb)
            dkv_ref[:, d + hd * dh:d + (hd + 1) * dh] += _dot_tn(p.astype(BF16), dob)
        dqb = dq_ref[...]
        gwq_ref[...] += _dot_tn(xnb, dqb)
        dxn = _dot_nt(dqb, wq_ref[...])
        dxr, dg2 = _rms_bwd(dxn, xh2, r2, g2_ref[...])
        dg2_ref[...] += dg2
        dx1 = dx2v + dxr
        dx1_ref[...] = dx1
        gwout_ref[...] += _dot_tn(y_ref[...], dx1.astype(BF16))

        @pl.when(i == n - 1)
        def _():
            for acc, out in ((gwo_ref, gwo_out), (gwq_ref, gwq_out), (gwout_ref, gwout_out)):
                out[...] = acc[...].astype(BF16)

    sq = jax.ShapeDtypeStruct((d, d), BF16)
    return _host_call(
        body, "attn_bwd", (n,),
        [_rows(tm, d), _rows(tm, d), _rows(tm, d), _rows(tm, d), _rows(tm, d), _rows(tm, HEADS * m),
         _full(g2.shape), _full(w_q.shape), _full(kv.shape), _full(w_o.shape)],
        [_rows(tm, d), _acc((m, 2 * d)), _acc((1, d)), _acc((d, d)), _acc((d, d)), _acc((d, d))],
        (jax.ShapeDtypeStruct((s, d), F32), jax.ShapeDtypeStruct((m, 2 * d), F32),
         jax.ShapeDtypeStruct((1, d), F32), sq, sq, sq),
        [pltpu.VMEM((tm, d), BF16)] + [pltpu.VMEM((d, d), F32)] * 3,
        (x1, dx2, o, ycat, qs, probs, g2, w_q, kv, w_o), ("arbitrary",), hosted)


def _kv_bwd(dkv, mem, memn, g_mem, w_kv):
    m, d = mem.shape
    ns = w_kv.shape[2]

    def body(dkv_ref, mem_ref, memn_ref, g_ref, w_ref, gw_ref, dg_ref):
        _, xh, _ = _rms_fwd(mem_ref[...], g_ref[...])
        dmemn = jnp.zeros((m, d), F32)
        for k in range(N_CHIPS):
            dkb = dkv_ref[:, k * ns:(k + 1) * ns].astype(BF16)
            gw_ref[k] = _dot_tn(memn_ref[...], dkb).astype(BF16)
            dmemn = dmemn + _dot_nt(dkb, w_ref[k])
        dg_ref[...] = jnp.sum(dmemn * xh, axis=0, keepdims=True)

    return pl.pallas_call(
        body, name="kv_bwd",
        out_shape=(jax.ShapeDtypeStruct((N_CHIPS, d, ns), BF16), jax.ShapeDtypeStruct((1, d), F32)),
        compiler_params=pltpu.CompilerParams(vmem_limit_bytes=VMEM_LIMIT),
    )(dkv, mem, memn, g_mem, w_kv)


def _mixer_bwd(x, dx1, h, mixed_all, th_all, g1, lng, lnb, wsp, b_t, cw, ga, gb, w_out, w_in, hosted=None):
    s, d = x.shape
    n = s // TM
    nch = TM // CHUNK
    ns = w_in.shape[2]
    nh = N_CHIPS * ns
    aw = d // 2
    hd_w = aw // HEADS

    def rev(cols):
        return pl.BlockSpec((TM, cols), lambda i: (n - 1 - i, 0))

    hprev = pl.BlockSpec((8, nh), lambda i: (jnp.maximum((n - 1 - i) * (TM // 8) - 1, 0), 0))

    def body(x_ref, dx1_ref, h_ref, hp_ref, mix_ref, th_ref, g1_ref, lng_ref, lnb_ref, wsp_ref, bt_ref, cw_ref,
             ga_ref, gb_ref, wout_ref, win_ref,
             dx_ref, dh_ref, dg1_ref, dlng_ref, dlnb_ref, dwsp_ref, dbt_ref, dcw_ref, dga_ref, dgb_ref,
             dvn_ref, dcn_ref):
        i = pl.program_id(0)

        @pl.when(i == 0)
        def _():
            for r in (dg1_ref, dlng_ref, dlnb_ref, dwsp_ref, dbt_ref, dcw_ref, dga_ref, dgb_ref, dcn_ref):
                r[...] = jnp.zeros_like(r)

        dx1v = dx1_ref[...]
        dycat = _dot_nt(dx1v.astype(BF16), wout_ref[...])
        ha = h_ref[:, 0:2 * aw]
        th = th_ref[...]
        a = 0.5 * ha * (1.0 + th)
        u = a[:, :aw]
        lngv = lng_ref[...]
        vn, vhat, rs = _layer_norm_parts(a[:, aw:], lngv, lnb_ref[...])
        vnb = vn.astype(BF16)
        wm = _tril_weights(wsp_ref)
        mixed = mix_ref[...]
        gav = ga_ref[...]
        _, yah, ra = _rms_fwd(u * mixed, gav)
        dya, dga = _rms_bwd(dycat[:, :aw], yah, ra, gav)
        dga_ref[...] += dga
        du = dya * mixed
        dmix = dya * u
        dmb = dmix.astype(BF16)
        tri = lax.broadcasted_iota(jnp.int32, (CHUNK, CHUNK), 0) >= lax.broadcasted_iota(jnp.int32, (CHUNK, CHUNK), 1)
        for hd in range(HEADS):
            dw = jnp.zeros((CHUNK, CHUNK), F32)
            db = jnp.zeros((CHUNK, 1), F32)
            for c in range(nch):
                rows = slice(c * CHUNK, (c + 1) * CHUNK)
                cols = slice(hd * hd_w, (hd + 1) * hd_w)
                dvn_ref[rows, cols] = _dot_tn(wm[hd], dmb[rows, cols])
                dw = dw + _dot_nt(dmb[rows, cols], vnb[rows, cols])
                db = db + jnp.sum(dmix[rows, cols], axis=1, keepdims=True)
            dwsp_ref[hd] += jnp.where(tri, dw, 0.0)
            dbt_ref[:, hd:hd + 1] += db
        dvn = dvn_ref[...]
        dlng_ref[...] += jnp.sum(dvn * vhat, axis=0, keepdims=True)
        dlnb_ref[...] += jnp.sum(dvn, axis=0, keepdims=True)
        dvh = dvn * lngv
        dv = rs * (dvh - jnp.mean(dvh, axis=-1, keepdims=True) - vhat * jnp.mean(dvh * vhat, axis=-1, keepdims=True))
        gprime = 0.5 * (1.0 + th) + 0.5 * ha * (1.0 - th * th) * (GELU_C * (1.0 + 3.0 * GELU_K * (ha * ha)))
        dh_ref[:, 0:2 * aw] = (jnp.concatenate([du, dv], axis=-1) * gprime).astype(BF16)
        g_b = h_ref[:, 2 * aw:3 * aw]
        g_c = h_ref[:, 3 * aw:4 * aw]
        val = h_ref[:, 4 * aw:5 * aw]
        z = g_c * val
        zp = jnp.where(i == n - 1, 0.0, hp_ref[:, 3 * aw:4 * aw] * hp_ref[:, 4 * aw:5 * aw])
        zm1, zm2 = _shift_rows(z, zp)
        cw0, cw1, cw2 = cw_ref[0:1, :], cw_ref[1:2, :], cw_ref[2:3, :]
        conv = cw0 * zm2 + cw1 * zm1 + cw2 * z
        gbv = gb_ref[...]
        _, ybh, rb = _rms_fwd(g_b * conv, gbv)
        dyb, dgb = _rms_bwd(dycat[:, aw:], ybh, rb, gbv)
        dgb_ref[...] += dgb
        dconv = dyb * g_b
        dcw_ref[0:1, :] += jnp.sum(dconv * zm2, axis=0, keepdims=True)
        dcw_ref[1:2, :] += jnp.sum(dconv * zm1, axis=0, keepdims=True)
        dcw_ref[2:3, :] += jnp.sum(dconv * z, axis=0, keepdims=True)
        nxt = dcn_ref[...]
        row = lax.broadcasted_iota(jnp.int32, dconv.shape, 0)
        dcp1 = jnp.where(row == TM - 1, nxt[0:1, :], pltpu.roll(dconv, TM - 1, 0))
        dcp2 = jnp.where(row == TM - 1, nxt[1:2, :],
                         jnp.where(row == TM - 2, nxt[0:1, :], pltpu.roll(dconv, TM - 2, 0)))
        dz = cw2 * dconv + cw1 * dcp1 + cw0 * dcp2
        dcn_ref[...] = dconv[0:8, :]
        dh_ref[:, 2 * aw:3 * aw] = (dyb * conv).astype(BF16)
        dh_ref[:, 3 * aw:4 * aw] = (dz * val).astype(BF16)
        dh_ref[:, 4 * aw:5 * aw] = (dz * g_c).astype(BF16)
        dxn = jnp.zeros((TM, d), F32)
        for k in range(N_CHIPS):
            dxn = dxn + _dot_nt(dh_ref[:, k * ns:(k + 1) * ns], win_ref[k])
        g1v = g1_ref[...]
        _, xh1, r1 = _rms_fwd(x_ref[...], g1v)
        dxr, dg1 = _rms_bwd(dxn, xh1, r1, g1v)
        dg1_ref[...] += dg1
        dx_ref[...] = dx1v + dxr

    ins = (x, dx1, h, h, mixed_all, th_all, g1, lng, lnb, wsp, b_t, cw, ga, gb, w_out, w_in)
    acc_shapes = [(1, d), (1, aw), (1, aw), wsp.shape, (CHUNK, CHUNK), cw.shape, (1, aw), (1, aw)]
    return _host_call(
        body, "mixer_bwd", (n,),
        [rev(d), rev(d), rev(nh), hprev, rev(aw), rev(d)] + [_full(a.shape) for a in ins[6:]],
        [rev(d), rev(nh)] + [_acc(sh) for sh in acc_shapes],
        (jax.ShapeDtypeStruct((s, d), F32), jax.ShapeDtypeStruct((s, nh), BF16))
        + tuple(jax.ShapeDtypeStruct(sh, F32) for sh in acc_shapes),
        [pltpu.VMEM((TM, aw), F32), pltpu.VMEM((8, aw), F32)],
        ins, ("arbitrary",), hosted)


def _weight_grad(a, b, name, tm, tn, col_sharded, hosted=None):
    t, m = a.shape
    n = b.shape[1]

    def body(a_ref, b_ref, o_ref):
        o_ref[...] = _dot_tn(a_ref[...].astype(BF16), b_ref[...].astype(BF16)).astype(BF16)

    if col_sharded:
        ns = n // N_CHIPS
        per = ns // tn
        out_shape = jax.ShapeDtypeStruct((N_CHIPS, m, ns), BF16)
        out_spec = pl.BlockSpec((None, tm, tn), lambda i, j: (j // per, i, j % per))
    else:
        out_shape = jax.ShapeDtypeStruct((m, n), BF16)
        out_spec = pl.BlockSpec((tm, tn), lambda i, j: (i, j))
    (out,), extra = _host_call(
        body, name, (m // tm, n // tn),
        [pl.BlockSpec((t, tm), lambda i, j: (0, i)), pl.BlockSpec((t, tn), lambda i, j: (0, j))],
        [out_spec], (out_shape,), [], (a, b), ("parallel", "parallel"), hosted)
    return (out if col_sharded else out.reshape(N_CHIPS, m // N_CHIPS, n)), extra


def _row_tile(rows, cap=256):
    best = None
    for t in range(16, min(rows, cap) + 1, 16):
        if rows % t == 0:
            best = t
    return best if best is not None else rows


def _adamw_math(w, g, m, v):
    m2 = ADAM_B1 * m + (1.0 - ADAM_B1) * g
    v2 = ADAM_B2 * v + (1.0 - ADAM_B2) * (g * g)
    m_hat = m2 / (1.0 - ADAM_B1 ** ADAM_STEP)
    v_hat = v2 / (1.0 - ADAM_B2 ** ADAM_STEP)
    delta = -ADAM_LR * (m_hat / (jnp.sqrt(v_hat) + ADAM_EPS) + ADAM_WD * w)
    return delta, m2, v2


def _adamw(w, g, m, v, name):
    r, c = w.shape
    tr = _row_tile(r) if r >= 16 else r

    def body(w_ref, g_ref, m_ref, v_ref, d_ref, m2_ref, v2_ref):
        d_ref[...], m2_ref[...], v2_ref[...] = _adamw_math(w_ref[...], g_ref[...], m_ref[...], v_ref[...])

    sh = jax.ShapeDtypeStruct((r, c), F32)
    return pl.pallas_call(
        body, name=name, grid=(r // tr,),
        in_specs=[_rows(tr, c)] * 4, out_specs=[_rows(tr, c)] * 3, out_shape=(sh, sh, sh),
        compiler_params=_params("parallel"),
    )(w, g, m, v)


def _finalize(items, place, name, hosted=None):
    r, c = items[0][2].shape
    tr = _row_tile(r)
    nw = len(items)

    def body(place_ref, *refs):
        ins, outs = refs[:7 * nw], refs[7 * nw:]
        for k in range(nw):
            own_ref, s0_ref, s1_ref, s2_ref, w_ref, m_ref, v_ref = ins[7 * k:7 * k + 7]
            g_ref, d_ref, m2_ref, v2_ref = outs[4 * k:4 * k + 4]
            g = own_ref[...].astype(F32) + s0_ref[...].astype(F32)
            g = (g + s1_ref[...].astype(F32)) + s2_ref[...].astype(F32)
            g_ref[...] = g
            d_ref[...], m2_ref[...], v2_ref[...] = _adamw_math(w_ref[...], g, m_ref[...], v_ref[...])

    def slot(k):
        return pl.BlockSpec((None, tr, c), lambda i, pref: (k, i, 0))

    rows = pl.BlockSpec((tr, c), lambda i, pref: (i, 0))
    sh = jax.ShapeDtypeStruct((r, c), F32)
    one = [pl.BlockSpec((None, tr, c), lambda i, pref: (pref[1], i, 0)), slot(0), slot(1), slot(2), rows, rows, rows]
    args = [a for part, slots, w, m, v in items for a in (part, slots, slots, slots, w, m, v)]
    res, extra = _host_call(body, name, (r // tr,), one * nw, [rows] * (4 * nw), (sh,) * (4 * nw), [], args,
                            ("parallel",), hosted, prefetch=(place,))
    return [res[4 * k:4 * k + 4] for k in range(nw)], extra


def _small_sum_adamw(parts, w, m, v):
    nd, r, c = parts.shape

    def body(p_ref, w_ref, m_ref, v_ref, g_ref, d_ref, m2_ref, v2_ref):
        g = p_ref[0]
        for k in range(1, nd):
            g = g + p_ref[k]
        g_ref[...] = g
        d_ref[...], m2_ref[...], v2_ref[...] = _adamw_math(w_ref[...], g, m_ref[...], v_ref[...])

    sh = jax.ShapeDtypeStruct((r, c), F32)
    return pl.pallas_call(
        body, name="small_sum_adamw", out_shape=(sh, sh, sh, sh),
        compiler_params=pltpu.CompilerParams(vmem_limit_bytes=VMEM_LIMIT),
    )(parts, w, m, v)


def _place():
    x, y, c = lax.axis_index("x"), lax.axis_index("y"), lax.axis_index("c")
    chips = [(1 - x, y), (x, 1 - y), (1 - x, 1 - y)]
    return x, y, c, 2 * x + y, chips


def _remote(src, dst, send_sem, recv_sem, to):
    return pltpu.make_async_remote_copy(src_ref=src, dst_ref=dst, send_sem=send_sem, recv_sem=recv_sem,
                                        device_id=to, device_id_type=MESH)


class _Exchange:
    def __init__(self, ins, out_shapes, sem_shape, start, finish, middle=None, in_place=False, peers=()):
        self.ins, self.out_shapes, self.sem_shape = tuple(ins), tuple(out_shapes), sem_shape
        self.start, self.finish, self.middle = start, finish, middle
        self.in_place = in_place
        self.peers = frozenset(peers)
        self.collective_id = None

    def with_id(self, collective_id):
        self.collective_id = collective_id
        return self


_FLIPS = {"c": (0, 0, 1), "x": (1, 0, 0), "y": (0, 1, 0), "xy": (1, 1, 0)}


def _handshake(peers):
    x, y, c, _, _ = _place()
    barrier = pltpu.get_barrier_semaphore()
    for name in sorted(peers):
        fx, fy, fc = _FLIPS[name]
        peer = (1 - x if fx else x, 1 - y if fy else y, 1 - c if fc else c)
        pl.semaphore_signal(barrier, inc=1, device_id=peer, device_id_type=MESH)
    pl.semaphore_wait(barrier, len(peers))


def _run_exchange(ex, name, collective_id=None, casts=()):
    n_in, n_out, n_cast = len(ex.ins), len(ex.out_shapes), len(casts)
    step = 64

    def body(*refs):
        ins, srcs = refs[:n_in], refs[n_in:n_in + n_cast]
        outs = refs[n_in + n_cast:n_in + n_cast + n_out]
        dsts = refs[n_in + n_cast + n_out:n_in + 2 * n_cast + n_out]
        rest = refs[n_in + 2 * n_cast + n_out:]
        wide, narrow = rest[:n_cast], rest[n_cast:2 * n_cast]
        send_sems, recv_sems = rest[2 * n_cast], rest[2 * n_cast + 1]
        if collective_id is not None:
            _handshake(ex.peers)
        ex.start(ins, outs, send_sems, recv_sems)
        loads = [pltpu.make_async_copy(srcs[k], wide[k], rest[2 * n_cast + 2].at[k]) for k in range(n_cast)]
        for cp in loads:
            cp.start()
        if ex.middle is not None:
            ex.middle(ins, outs, send_sems, recv_sems)
        stores = []
        for k in range(n_cast):
            loads[k].wait()

            def chunk(i, carry, k=k):
                rows = pl.ds(pl.multiple_of(i * step, step), step)
                narrow[k][rows, :] = wide[k][rows, :].astype(BF16)
                return carry
            lax.fori_loop(0, casts[k].shape[0] // step, chunk, 0)
            stores.append(pltpu.make_async_copy(narrow[k], dsts[k], rest[2 * n_cast + 2].at[k]))
            stores[-1].start()
        ex.finish(ins, outs, send_sems, recv_sems)
        for cp in stores:
            cp.wait()

    assert collective_id is None or ex.peers
    sem = pltpu.SemaphoreType.DMA(ex.sem_shape)
    scratch = ([pltpu.VMEM(a.shape, F32) for a in casts] + [pltpu.VMEM(a.shape, BF16) for a in casts] + [sem, sem]
               + ([pltpu.SemaphoreType.DMA((n_cast,))] if n_cast else []))
    return pl.pallas_call(
        body, name=name,
        out_shape=ex.out_shapes + tuple(jax.ShapeDtypeStruct(a.shape, BF16) for a in casts),
        in_specs=[ANY] * (n_in + n_cast), out_specs=[ANY] * (n_out + n_cast),
        input_output_aliases={k: k for k in range(n_in)} if ex.in_place else {}, scratch_shapes=scratch,
        compiler_params=pltpu.CompilerParams(collective_id=collective_id, vmem_limit_bytes=VMEM_LIMIT),
    )(*ex.ins, *casts)


def _host_call(body, name, grid, in_specs, out_specs, out_shape, scratch_shapes, args, semantics, hosted,
               prefetch=()):
    hosted = [] if hosted is None else (list(hosted) if isinstance(hosted, (list, tuple)) else [hosted])
    collective_id = hosted[0].collective_id if hosted else None
    peers = frozenset().union(*[ex.peers for ex in hosted]) if hosted else frozenset()
    assert collective_id is None or all(ex.peers for ex in hosted)
    n_pre, n_in, n_out, n_scr = len(prefetch), len(in_specs), len(out_specs), len(scratch_shapes)
    h_ins = [a for ex in hosted for a in ex.ins]
    h_outs = [s for ex in hosted for s in ex.out_shapes]
    h_in, h_out = len(h_ins), len(h_outs)

    def wrapped(*refs):
        pre, refs = refs[:n_pre], refs[n_pre:]
        a, hi = refs[:n_in], refs[n_in:n_in + h_in]
        o = refs[n_in + h_in:n_in + h_in + n_out]
        ho = refs[n_in + h_in + n_out:n_in + h_in + n_out + h_out]
        scr = refs[n_in + h_in + n_out + h_out:]

        def run(phase):
            i0 = o0 = 0
            for k, ex in enumerate(hosted):
                fn = getattr(ex, phase)
                if fn is not None:
                    fn(hi[i0:i0 + len(ex.ins)], ho[o0:o0 + len(ex.out_shapes)], scr[n_scr + 2 * k],
                       scr[n_scr + 2 * k + 1])
                i0, o0 = i0 + len(ex.ins), o0 + len(ex.out_shapes)

        if hosted:
            first = functools.reduce(jnp.logical_and, [pl.program_id(k) == 0 for k in range(len(grid))])

            @pl.when(first)
            def _():
                if collective_id is not None:
                    _handshake(peers)
                run("start")

        if any(ex.middle is not None for ex in hosted):
            half_way = functools.reduce(jnp.logical_and, [
                pl.program_id(0) == max(1, grid[0] * MIDDLE_STEP_16THS // 16)] + [
                pl.program_id(k) == 0 for k in range(1, len(grid))])

            @pl.when(half_way)
            def _():
                run("middle")

        body(*pre, *a, *o, *scr[:n_scr])

        if hosted:
            last = functools.reduce(jnp.logical_and, [pl.program_id(k) == grid[k] - 1 for k in range(len(grid))])

            @pl.when(last)
            def _():
                run("finish")

    sems = [pltpu.SemaphoreType.DMA(ex.sem_shape) for ex in hosted for _ in range(2)]
    aliases, i0, o0 = {}, n_pre + n_in, n_out
    for ex in hosted:
        if ex.in_place:
            aliases.update({i0 + k: o0 + k for k in range(len(ex.ins))})
        i0, o0 = i0 + len(ex.ins), o0 + len(ex.out_shapes)
    all_in, all_out = list(in_specs) + [ANY] * h_in, list(out_specs) + [ANY] * h_out
    all_scr = list(scratch_shapes) + sems
    params = _params(*(["arbitrary"] * len(grid) if hosted else semantics), collective_id=collective_id)
    shapes = tuple(out_shape) + tuple(h_outs)
    if n_pre:
        call = pl.pallas_call(
            wrapped, name=name, out_shape=shapes, input_output_aliases=aliases, compiler_params=params,
            grid_spec=pltpu.PrefetchScalarGridSpec(num_scalar_prefetch=n_pre, grid=grid, in_specs=all_in,
                                                   out_specs=all_out, scratch_shapes=all_scr))
    else:
        call = pl.pallas_call(
            wrapped, name=name, grid=grid, in_specs=all_in, out_specs=all_out, out_shape=shapes,
            scratch_shapes=all_scr, input_output_aliases=aliases, compiler_params=params)
    res = call(*prefetch, *args, *h_ins)
    return res[:n_out], res[n_out:]


def _all_gather(shards, small=()):
    items = tuple(shards) + tuple(small)
    nw = len(shards)

    def place():
        x, y, c, me, _ = _place()
        first = (x + (1 - c) * (1 - 2 * x), y + c * (1 - 2 * y))
        second = (x + c * (1 - 2 * x), y + (1 - c) * (1 - 2 * y))
        diag = (1 - x, 1 - y)
        return x, y, c, me, (first, second, diag)

    def halves(w, c):
        rh = items[w].shape[0] // 2
        return pl.ds(c * rh, rh), pl.ds((1 - c) * rh, rh)

    def start(ins, outs, ss, rs):
        x, y, c, me, chips = place()
        for w in range(len(items)):
            _remote(ins[w], outs[w].at[me], ss.at[w, 6], rs.at[w, 6], (x, y, 1 - c)).start()
            if w < nw:
                mine, _ = halves(w, c)
                _remote(ins[w].at[mine], outs[w].at[me, mine], ss.at[w, 0], rs.at[w, 0], (*chips[0], c)).start()
            else:
                for k in range(3):
                    _remote(ins[w], outs[w].at[me], ss.at[w, k], rs.at[w, k], (*chips[k], c)).start()

    def onward(outs, ss, rs, w, k, x, y, c, chips):
        mine, _ = halves(w, c)
        pk = 2 * chips[k][0] + chips[k][1]
        got = outs[w].at[pk, mine]
        src = chips[1] if k == 2 else chips[k]
        _remote(got, got, ss.at[w, k], rs.at[w, k], (*src, c)).wait_recv()
        if k == 0:
            _remote(got, got, ss.at[w, 2], rs.at[w, 2], (*chips[1], c)).start()
        _remote(got, got, ss.at[w, 3 + k], rs.at[w, 3 + k], (x, y, 1 - c)).start()

    def middle(ins, outs, ss, rs):
        x, y, c, me, chips = place()
        for w in range(nw):
            mine, _ = halves(w, c)
            _remote(ins[w].at[mine], outs[w].at[me, mine], ss.at[w, 1], rs.at[w, 1], (*chips[1], c)).start()
        for w in range(nw):
            onward(outs, ss, rs, w, 0, x, y, c, chips)

    def finish(ins, outs, ss, rs):
        x, y, c, me, chips = place()
        sib = (x, y, 1 - c)
        for k in (1, 2):
            for w in range(nw):
                onward(outs, ss, rs, w, k, x, y, c, chips)
        for w in range(len(items)):
            if w < nw:
                mine, theirs = halves(w, c)
                for k, chip in ((3, chips[1]), (4, chips[0]), (5, chips[2])):
                    oth = outs[w].at[2 * chip[0] + chip[1], theirs]
                    _remote(oth, oth, ss.at[w, k], rs.at[w, k], sib).wait_recv()
                own = ins[w].at[mine]
                for k in range(6):
                    _remote(own, own, ss.at[w, k], rs.at[w, k], sib).wait_send()
            else:
                for k in range(3):
                    got = outs[w].at[2 * chips[k][0] + chips[k][1]]
                    _remote(got, got, ss.at[w, k], rs.at[w, k], (*chips[k], c)).wait_recv()
                    _remote(ins[w], ins[w], ss.at[w, k], rs.at[w, k], sib).wait_send()
            _remote(ins[w], outs[w].at[me], ss.at[w, 6], rs.at[w, 6], sib).wait()

    out_shapes = tuple(jax.ShapeDtypeStruct((N_CHIPS,) + a.shape, a.dtype) for a in items)
    return _Exchange(items, out_shapes, (len(items), 7), start, finish, middle if nw else None,
                     peers=("c", "x", "y", "xy") if small else ("c", "x", "y"))


def _chip_reduce(grads, name, collective_id):
    nw = len(grads)
    step = 64

    def body(*refs):
        ins, outs = refs[:nw], refs[nw:2 * nw]
        own, got = refs[2 * nw:3 * nw], refs[3 * nw:4 * nw]
        send_sems, recv_sems, local_sems = refs[4 * nw:]
        x, y, c, _, _ = _place()
        barrier = pltpu.get_barrier_semaphore()
        pl.semaphore_signal(barrier, inc=1, device_id=(x, y, 1 - c), device_id_type=MESH)
        pl.semaphore_wait(barrier, 1)
        moves = []
        for w in range(nw):
            rh = grads[w].shape[1] // 2
            away = _remote(ins[w].at[:, pl.ds((1 - c) * rh, rh), :], got[w], send_sems.at[w], recv_sems.at[w],
                           (x, y, 1 - c))
            mine = pltpu.make_async_copy(ins[w].at[:, pl.ds(c * rh, rh), :], own[w], local_sems.at[w])
            away.start()
            mine.start()
            moves.append((away, mine))
        back = []
        for w, (away, mine) in enumerate(moves):
            nb, rh, _ = own[w].shape
            mine.wait()
            away.wait()
            for k in range(nb):
                def add(i, carry, w=w, k=k):
                    rows = pl.ds(pl.multiple_of(i * step, step), step)
                    own[w][k, rows, :] = (own[w][k, rows, :].astype(F32) + got[w][k, rows, :].astype(F32)).astype(BF16)
                    return carry
                lax.fori_loop(0, rh // step, add, 0)
                tail = rh % step
                if tail:
                    rows = slice(rh - tail, rh)
                    own[w][k, rows, :] = (own[w][k, rows, :].astype(F32) + got[w][k, rows, :].astype(F32)).astype(BF16)
            wb = pltpu.make_async_copy(own[w], outs[w].at[:, pl.ds(c * rh, rh), :], local_sems.at[w])
            wb.start()
            back.append(wb)
        for wb in back:
            wb.wait()

    halves = [pltpu.VMEM((g.shape[0], g.shape[1] // 2, g.shape[2]), BF16) for g in grads]
    sem = pltpu.SemaphoreType.DMA((nw,))
    return pl.pallas_call(
        body, name=name, out_shape=tuple(jax.ShapeDtypeStruct(g.shape, BF16) for g in grads),
        in_specs=[ANY] * nw, out_specs=[ANY] * nw, scratch_shapes=halves + halves + [sem, sem, sem],
        compiler_params=pltpu.CompilerParams(vmem_limit_bytes=VMEM_LIMIT, collective_id=collective_id),
    )(*grads)


def _scatter_partials(parts):
    nw = len(parts)

    def copies(ins, outs, ss, rs):
        _, _, c, _, chips = _place()
        res = []
        for r, (px, py) in enumerate(chips):
            for w in range(nw):
                rh = parts[w].shape[1] // 2
                rows = pl.ds(c * rh, rh)
                res.append(_remote(ins[w].at[2 * px + py, rows], outs[w].at[r, rows], ss.at[w, r], rs.at[w, r],
                                   (px, py, c)))
        return res

    def start(ins, outs, ss, rs):
        for cp in copies(ins, outs, ss, rs):
            cp.start()

    def finish(ins, outs, ss, rs):
        for cp in copies(ins, outs, ss, rs):
            cp.wait()

    out_shapes = tuple(jax.ShapeDtypeStruct((3,) + p.shape[1:], p.dtype) for p in parts)
    return _Exchange(parts, out_shapes, (nw, 3), start, finish, peers=("x", "y", "xy"))


def _join_partials(parts, slots):
    nw = len(parts)

    def copies(outs, ss, rs, mine):
        x, y, c, me, _ = _place()
        res = []
        for w in range(nw):
            rh = parts[w].shape[1] // 2
            rows = pl.ds((c if mine else 1 - c) * rh, rh)
            own = outs[w].at[me, rows]
            got = outs[nw + w].at[:, rows, :]
            res.append(_remote(own, own, ss.at[w, 0], rs.at[w, 0], (x, y, 1 - c)))
            res.append(_remote(got, got, ss.at[w, 1], rs.at[w, 1], (x, y, 1 - c)))
        return res

    def start(ins, outs, ss, rs):
        for cp in copies(outs, ss, rs, True):
            cp.start()

    def finish(ins, outs, ss, rs):
        for cp in copies(outs, ss, rs, True):
            cp.wait_send()
        for cp in copies(outs, ss, rs, False):
            cp.wait_recv()

    arrays = tuple(parts) + tuple(slots)
    return _Exchange(arrays, tuple(jax.ShapeDtypeStruct(a.shape, a.dtype) for a in arrays), (nw, 2), start, finish,
                     in_place=True, peers=("c",))


def _gather_small(slab):
    def copies(ins, outs, ss, rs):
        x, y, c, _, _ = _place()
        me = 4 * x + 2 * y + c
        out, arrivals = [], []
        for k in range(1, 8):
            px = 1 - x if k & 4 else x
            py = 1 - y if k & 2 else y
            pc = 1 - c if k & 1 else c
            out.append(_remote(ins[0], outs[0].at[me], ss.at[k - 1], rs.at[k - 1], (px, py, pc)))
            theirs = outs[0].at[4 * px + 2 * py + pc]
            arrivals.append((theirs, k - 1, (px, py, pc)))
        return pltpu.make_async_copy(ins[0], outs[0].at[me], ss.at[7]), out, arrivals

    def start(ins, outs, ss, rs):
        own, out, _ = copies(ins, outs, ss, rs)
        own.start()
        for cp in out:
            cp.start()

    def finish(ins, outs, ss, rs):
        own, out, arrivals = copies(ins, outs, ss, rs)
        for cp in out:
            cp.wait_send()
        for theirs, k, peer in arrivals:
            _remote(theirs, theirs, ss.at[k], rs.at[k], peer).wait_recv()
        own.wait()

    return _Exchange((slab,), (jax.ShapeDtypeStruct((8,) + slab.shape, slab.dtype),), (8,), start, finish)


_SMALL_VECS = ("ln_mix_g", "ln_attn_g", "ln_mem_g", "ln_ffn_g", "ln_final_g")


def _pack_small(p, extra, conv):
    d = p["ln_mix_g"].shape[-1]
    top = [p[k].reshape(1, d) for k in _SMALL_VECS]
    top.append(jnp.concatenate([p["sgu_ln_g"].reshape(-1), p["sgu_ln_b"].reshape(-1)]).reshape(1, d))
    top.append(jnp.concatenate([p["grp_norm_a"].reshape(-1), p["grp_norm_b"].reshape(-1)]).reshape(1, d))
    top.append(jnp.concatenate([p["b_spatial"].reshape(-1), extra]).reshape(1, d))
    mid = jnp.zeros((8, d), F32)
    if conv is not None:
        mid = jnp.pad(conv, ((0, 5), (0, d - conv.shape[1])))
    return jnp.concatenate([jnp.concatenate(top, axis=0), mid, p["w_spatial"].reshape(-1, d)], axis=0)


def _unpack_small(slab):
    d = slab.shape[1]
    hw = d // 2
    out = {k: slab[i] for i, k in enumerate(_SMALL_VECS)}
    out["sgu_ln_g"], out["sgu_ln_b"] = slab[5, :hw], slab[5, hw:]
    out["grp_norm_a"], out["grp_norm_b"] = slab[6, :hw], slab[6, hw:]
    out["b_spatial"] = slab[7, :hw].reshape(HEADS, CHUNK)
    out["w_spatial"] = slab[16:].reshape(HEADS, CHUNK, CHUNK)
    return out


_BIG = ("w_in", "w_kv", "w_gate_up", "w_out", "w_q", "w_o", "w_down")
_WEIGHTS = ("ln_mix_g", "w_in", "sgu_ln_g", "sgu_ln_b", "w_spatial", "b_spatial", "conv_w", "grp_norm_a",
            "grp_norm_b", "w_out", "ln_attn_g", "ln_mem_g", "w_q", "w_kv", "w_o", "ln_ffn_g", "w_gate_up",
            "w_down", "ln_final_g")


def _step(p, m_, v_, x, mem, target):
    s, d = x.shape
    hw = d // 2
    row = lambda a: a.reshape(1, -1)
    x_, y_, c_ = lax.axis_index("x"), lax.axis_index("y"), lax.axis_index("c")
    chip = 2 * x_ + y_

    conv8 = jnp.pad(p["conv_w"], ((0, 5), (0, 0)))
    later = ("w_kv", "w_q", "w_o", "w_down", "w_gate_up")
    first = _run_exchange(_all_gather([p["w_in"].astype(BF16), p["w_out"].astype(BF16)], [conv8]),
                          "all_gather_mixer", collective_id=4, casts=[p[k] for k in later])
    (w_in, w_out4, conv4), bf = first[:3], dict(zip(later, first[3:]))
    cw = jnp.transpose(conv4[:, :3, :], (1, 0, 2)).reshape(3, hw)
    b_t = jnp.pad(jnp.transpose(p["b_spatial"]), ((0, 0), (0, CHUNK - HEADS)))
    g1, g2, gm, g3, gf = (row(p[k]) for k in _SMALL_VECS)
    lng, lnb, ga, gb = row(p["sgu_ln_g"]), row(p["sgu_ln_b"]), row(p["grp_norm_a"]), row(p["grp_norm_b"])
    wsp = p["w_spatial"]
    w_out = w_out4.reshape(-1, d)

    (h, x1, ycat, xn1, mixed, th), (w_kv, w_q4, w_o4, w_down4) = _mixer_fwd(
        x, g1, w_in, lng, lnb, wsp, b_t, cw, ga, gb, w_out,
        hosted=_all_gather([bf[k] for k in ("w_kv", "w_q", "w_o", "w_down")]).with_id(5))
    w_q, w_o, w_down = (a.reshape(-1, d) for a in (w_q4, w_o4, w_down4))
    memn, kv = _kv_fwd(mem, gm, w_kv)
    (x2, o, qs, probs), (w_gu,) = _attn_fwd(x1, g2, w_q, kv, w_o,
                                            hosted=_all_gather([bf["w_gate_up"]]).with_id(6))
    dx2, act, dgu, xn3, dx3, loss, dgf, dg3 = _ffn_fwd_bwd(x2, g3, gf, target, w_gu, w_down)

    place = jnp.stack([c_, chip]).astype(jnp.int32)

    def chip_partials(names, grads, tag):
        return list(_chip_reduce(grads, "chip_reduce_" + tag, ("down", "ffn", "attn", "mixer").index(tag)))

    names_d = ("w_down",)
    parts_d = chip_partials(names_d, (_weight_grad(act, dx3, "grad_w_down", 1408, 512, False)[0],), "down")
    g_gu, slots_d = _weight_grad(xn3, dgu, "grad_w_gate_up", 512, 1408, True,
                                 hosted=_scatter_partials(parts_d).with_id(7))
    names_a = ("w_gate_up",)
    parts_a = chip_partials(names_a, (g_gu,), "ffn")
    (dx1, dkv, dg2, g_o, g_q, g_out), slots_a = _attn_bwd(x1, dx2, o, ycat, qs, probs, g2, w_q, kv, w_o,
                                                           hosted=_scatter_partials(parts_a).with_id(8))
    g_kv, dgm = _kv_bwd(dkv, mem, memn, gm, w_kv)
    names_b = ("w_o", "w_out", "w_q", "w_kv")
    shard_major = lambda g: g.reshape(N_CHIPS, -1, d)
    parts_b = chip_partials(names_b, (shard_major(g_o), shard_major(g_out), shard_major(g_q), g_kv), "attn")
    names_da = names_d + names_a
    (dx, dh, dg1, dlng, dlnb, dwsp, dbt, dcw, dga, dgb), extra = _mixer_bwd(
        x, dx1, h, mixed, th, g1, lng, lnb, wsp, b_t, cw, ga, gb, w_out, w_in,
        hosted=[_scatter_partials(parts_b).with_id(9), _join_partials(parts_d + parts_a, slots_d + slots_a)])
    slots_b, joined = extra[:len(names_b)], extra[len(names_b):]
    whole = dict(zip(names_da, zip(joined[:len(names_da)], joined[len(names_da):])))
    small = {"ln_mix_g": dg1, "ln_attn_g": dg2, "ln_mem_g": dgm, "ln_ffn_g": dg3, "ln_final_g": dgf,
             "sgu_ln_g": dlng, "sgu_ln_b": dlnb, "grp_norm_a": dga, "grp_norm_b": dgb,
             "b_spatial": jnp.transpose(dbt[:, :HEADS]), "w_spatial": dwsp}
    loss_vec = jnp.pad(loss.reshape(1), (0, hw - 1))
    g_in, extra = _weight_grad(
        xn1, dh, "grad_w_in", 1024, 640, True,
        hosted=[_gather_small(_pack_small(small, loss_vec, dcw)), _join_partials(parts_b, slots_b)])
    parts = extra[0]
    whole.update(zip(names_b, zip(extra[1:1 + len(names_b)], extra[1 + len(names_b):])))
    (part_in,) = chip_partials(("w_in",), (g_in,), "mixer")
    out_g, out_d, out_m, out_v = {}, {}, {}, {}

    def finalize(ks, tag, hosted=None):
        done, res = _finalize([(whole[k][0], whole[k][1], p[k], m_[k], v_[k]) for k in ks], place,
                              "finalize_" + tag, hosted)
        for k, (g, dl, nm, nv) in zip(ks, done):
            out_g[k], out_d[k], out_m[k], out_v[k] = g, dl, nm, nv
        return res

    (slots_in,) = finalize(("w_down",), "w_down", _scatter_partials([part_in]).with_id(10))
    finalize(("w_o", "w_out", "w_q"), "attn")
    for k in ("w_gate_up", "w_kv"):
        finalize((k,), k)
    whole["w_in"] = _run_exchange(_join_partials([part_in], [slots_in]), "rs_join_mixer", collective_id=11)
    finalize(("w_in",), "w_in")

    zeros = jnp.zeros((hw,), F32)
    sg, sd, sm, sv = _small_sum_adamw(parts, _pack_small(p, zeros, None), _pack_small(m_, zeros, None),
                                      _pack_small(v_, zeros, None))
    for tree, slab in zip((out_g, out_d, out_m, out_v), (sg, sd, sm, sv)):
        tree.update(_unpack_small(slab))
    loss_out = sg[7, hw]
    g_conv = lax.dynamic_slice(sg[8:11, :hw], (0, chip * (hw // N_CHIPS)), (3, hw // N_CHIPS))
    out_g["conv_w"] = g_conv
    out_d["conv_w"], out_m["conv_w"], out_v["conv_w"] = _adamw(p["conv_w"], g_conv, m_["conv_w"], v_["conv_w"],
                                                                "adamw_conv_w")
    return loss_out, dx, out_g, out_d, out_m, out_v


def kernel(x, mem, ln_mix_g, w_in, sgu_ln_g, sgu_ln_b, w_spatial, b_spatial, conv_w, grp_norm_a, grp_norm_b, w_out, ln_attn_g, ln_mem_g, w_q, w_kv, w_o, ln_ffn_g, w_gate_up, w_down, ln_final_g, loss_target, m_ln_mix_g, m_w_in, m_sgu_ln_g, m_sgu_ln_b, m_w_spatial, m_b_spatial, m_conv_w, m_grp_norm_a, m_grp_norm_b, m_w_out, m_ln_attn_g, m_ln_mem_g, m_w_q, m_w_kv, m_w_o, m_ln_ffn_g, m_w_gate_up, m_w_down, m_ln_final_g, v_ln_mix_g, v_w_in, v_sgu_ln_g, v_sgu_ln_b, v_w_spatial, v_b_spatial, v_conv_w, v_grp_norm_a, v_grp_norm_b, v_w_out, v_ln_attn_g, v_ln_mem_g, v_w_q, v_w_kv, v_w_o, v_ln_ffn_g, v_w_gate_up, v_w_down, v_ln_final_g):
    p = dict(ln_mix_g=ln_mix_g, w_in=w_in, sgu_ln_g=sgu_ln_g, sgu_ln_b=sgu_ln_b, w_spatial=w_spatial,
             b_spatial=b_spatial, conv_w=conv_w, grp_norm_a=grp_norm_a, grp_norm_b=grp_norm_b, w_out=w_out,
             ln_attn_g=ln_attn_g, ln_mem_g=ln_mem_g, w_q=w_q, w_kv=w_kv, w_o=w_o, ln_ffn_g=ln_ffn_g,
             w_gate_up=w_gate_up, w_down=w_down, ln_final_g=ln_final_g)
    m_ = dict(ln_mix_g=m_ln_mix_g, w_in=m_w_in, sgu_ln_g=m_sgu_ln_g, sgu_ln_b=m_sgu_ln_b, w_spatial=m_w_spatial,
              b_spatial=m_b_spatial, conv_w=m_conv_w, grp_norm_a=m_grp_norm_a, grp_norm_b=m_grp_norm_b,
              w_out=m_w_out, ln_attn_g=m_ln_attn_g, ln_mem_g=m_ln_mem_g, w_q=m_w_q, w_kv=m_w_kv, w_o=m_w_o,
              ln_ffn_g=m_ln_ffn_g, w_gate_up=m_w_gate_up, w_down=m_w_down, ln_final_g=m_ln_final_g)
    v_ = dict(ln_mix_g=v_ln_mix_g, w_in=v_w_in, sgu_ln_g=v_sgu_ln_g, sgu_ln_b=v_sgu_ln_b, w_spatial=v_w_spatial,
              b_spatial=v_b_spatial, conv_w=v_conv_w, grp_norm_a=v_grp_norm_a, grp_norm_b=v_grp_norm_b,
              w_out=v_w_out, ln_attn_g=v_ln_attn_g, ln_mem_g=v_ln_mem_g, w_q=v_w_q, w_kv=v_w_kv, w_o=v_w_o,
              ln_ffn_g=v_ln_ffn_g, w_gate_up=v_w_gate_up, w_down=v_w_down, ln_final_g=v_ln_final_g)
    s, d = x.shape[-2], x.shape[-1]
    loss, dx, g, dl, nm, nv = _step(p, m_, v_, x.reshape(s, d), mem.reshape(-1, d), loss_target.reshape(s, d))
    outs = [loss, dx.reshape(x.shape)]
    for tree in (g, dl, nm, nv):
        outs += [tree[k].reshape(p[k].shape) for k in _WEIGHTS]
    return tuple(outs)
```

```python
import functools
import math

import jax
import jax.numpy as jnp
from jax import lax
from jax.experimental import pallas as pl
from jax.experimental.pallas import tpu as pltpu

F32 = jnp.float32
BF16 = jnp.bfloat16
EPS = 1e-6
CHUNK = 128
HEADS = 4
N_CHIPS = 4
TM = 512
TM_ATTN = 512
TM_FFN = 256
ADAM_LR, ADAM_B1, ADAM_B2, ADAM_EPS, ADAM_WD, ADAM_STEP = 0.001, 0.9, 0.999, 1e-08, 0.01, 10
GELU_C = math.sqrt(2.0 / math.pi)
GELU_K = 0.044715
SMALL_ROWS = 80
VMEM_LIMIT = 56 * 1024 * 1024
MIDDLE_STEP_16THS = 7
MESH = pl.DeviceIdType.MESH
ANY = pl.BlockSpec(memory_space=pl.ANY)


def _params(*sem, collective_id=None):
    return pltpu.CompilerParams(dimension_semantics=sem, vmem_limit_bytes=VMEM_LIMIT, collective_id=collective_id)


def _dot(a, b):
    return jnp.dot(a, b, preferred_element_type=F32)


def _dot_nt(a, b):
    return lax.dot_general(a, b, (((1,), (1,)), ((), ())), preferred_element_type=F32)


def _dot_tn(a, b):
    return lax.dot_general(a, b, (((0,), (0,)), ((), ())), preferred_element_type=F32)


def _rms_fwd(x, g):
    r = lax.rsqrt(jnp.mean(x * x, axis=-1, keepdims=True) + EPS)
    xh = x * r
    return xh * g, xh, r


def _rms_bwd(dy, xh, r, g):
    dxh = dy * g
    dx = r * (dxh - xh * jnp.mean(dxh * xh, axis=-1, keepdims=True))
    return dx, jnp.sum(dy * xh, axis=0, keepdims=True)


def _full(shape):
    nd = len(shape)
    return pl.BlockSpec(shape, lambda *_: (0,) * nd, pipeline_mode=pl.Buffered(1))


def _acc(shape):
    nd = len(shape)
    return pl.BlockSpec(shape, lambda *_: (0,) * nd)


def _rows(tm, cols):
    return pl.BlockSpec((tm, cols), lambda i: (i, 0))


def _tril_weights(wsp_ref):
    row = lax.broadcasted_iota(jnp.int32, (CHUNK, CHUNK), 0)
    col = lax.broadcasted_iota(jnp.int32, (CHUNK, CHUNK), 1)
    return [jnp.where(row >= col, wsp_ref[hd], 0.0).astype(BF16) for hd in range(HEADS)]


def _shift_rows(z, zp):
    row = lax.broadcasted_iota(jnp.int32, z.shape, 0)
    zm1 = jnp.where(row == 0, zp[7:8, :], pltpu.roll(z, 1, 0))
    zm2 = jnp.where(row == 0, zp[6:7, :], jnp.where(row == 1, zp[7:8, :], pltpu.roll(z, 2, 0)))
    return zm1, zm2


def _gelu_parts(x):
    t = jnp.tanh(GELU_C * (x + GELU_K * (x * x * x)))
    return 0.5 * x * (1.0 + t), t


def _layer_norm_parts(v, g, b):
    mu = jnp.mean(v, axis=-1, keepdims=True)
    vc = v - mu
    rs = lax.rsqrt(jnp.mean(vc * vc, axis=-1, keepdims=True) + EPS)
    vhat = vc * rs
    return vhat * g + b, vhat, rs


def _kv_fwd(mem, g_mem, w_kv):
    m, d = mem.shape
    ns = w_kv.shape[2]

    def body(mem_ref, g_ref, w_ref, memn_ref, kv_ref):
        y, _, _ = _rms_fwd(mem_ref[...], g_ref[...])
        yb = y.astype(BF16)
        memn_ref[...] = yb
        for k in range(N_CHIPS):
            kv_ref[:, k * ns:(k + 1) * ns] = _dot(yb, w_ref[k]).astype(BF16)

    return pl.pallas_call(
        body, name="kv_fwd",
        out_shape=(jax.ShapeDtypeStruct((m, d), BF16), jax.ShapeDtypeStruct((m, N_CHIPS * ns), BF16)),
        compiler_params=pltpu.CompilerParams(vmem_limit_bytes=VMEM_LIMIT),
    )(mem, g_mem, w_kv)


def _mixer_fwd(x, g1, w_in, lng, lnb, wsp, b_t, cw, ga, gb, w_out, hosted=None):
    s, d = x.shape
    n = s // TM
    nch = TM // CHUNK
    ns = w_in.shape[2]
    nh = N_CHIPS * ns
    aw = d // 2
    hd_w = aw // HEADS

    def body(x_ref, g1_ref, win_ref, lng_ref, lnb_ref, wsp_ref, bt_ref, cw_ref, ga_ref, gb_ref, wout_ref,
             h_ref, x1_ref, y_ref, xn_ref, mix_ref, th_ref, zp_ref):
        i = pl.program_id(0)

        @pl.when(i == 0)
        def _():
            zp_ref[...] = jnp.zeros_like(zp_ref)

        x = x_ref[...]
        xn, _, _ = _rms_fwd(x, g1_ref[...])
        xnb = xn.astype(BF16)
        xn_ref[...] = xnb
        for k in range(N_CHIPS):
            h_ref[:, k * ns:(k + 1) * ns] = _dot(xnb, win_ref[k])
        a, th = _gelu_parts(h_ref[:, 0:2 * aw])
        th_ref[...] = th
        u = a[:, :aw]
        vn, _, _ = _layer_norm_parts(a[:, aw:], lng_ref[...], lnb_ref[...])
        vnb = vn.astype(BF16)
        wm = _tril_weights(wsp_ref)
        for c in range(nch):
            for hd in range(HEADS):
                blk = vnb[c * CHUNK:(c + 1) * CHUNK, hd * hd_w:(hd + 1) * hd_w]
                mix_ref[c * CHUNK:(c + 1) * CHUNK, hd * hd_w:(hd + 1) * hd_w] = _dot(wm[hd], blk) + bt_ref[:, hd:hd + 1]
        ya, _, _ = _rms_fwd(u * mix_ref[...], ga_ref[...])
        g_b = h_ref[:, 2 * aw:3 * aw]
        z = h_ref[:, 3 * aw:4 * aw] * h_ref[:, 4 * aw:5 * aw]
        zm1, zm2 = _shift_rows(z, zp_ref[...])
        conv = cw_ref[0:1, :] * zm2 + cw_ref[1:2, :] * zm1 + cw_ref[2:3, :] * z
        yb, _, _ = _rms_fwd(g_b * conv, gb_ref[...])
        zp_ref[...] = z[TM - 8:TM, :]
        ycat = jnp.concatenate([ya, yb], axis=-1).astype(BF16)
        y_ref[...] = ycat
        x1_ref[...] = x + _dot(ycat, wout_ref[...])

    return _host_call(
        body, "mixer_fwd", (n,),
        [_rows(TM, d), _full(g1.shape), _full(w_in.shape), _full(lng.shape), _full(lnb.shape),
         _full(wsp.shape), _full(b_t.shape), _full(cw.shape), _full(ga.shape), _full(gb.shape),
         _full(w_out.shape)],
        [_rows(TM, nh), _rows(TM, d), _rows(TM, d), _rows(TM, d), _rows(TM, aw), _rows(TM, d)],
        (jax.ShapeDtypeStruct((s, nh), F32), jax.ShapeDtypeStruct((s, d), F32),
         jax.ShapeDtypeStruct((s, d), BF16), jax.ShapeDtypeStruct((s, d), BF16),
         jax.ShapeDtypeStruct((s, aw), F32), jax.ShapeDtypeStruct((s, d), F32)),
        [pltpu.VMEM((8, aw), F32)],
        (x, g1, w_in, lng, lnb, wsp, b_t, cw, ga, gb, w_out), ("arbitrary",), hosted)


def _attn_fwd(x1, g2, w_q, kv, w_o, hosted=None):
    s, d = x1.shape
    tm = min(TM_ATTN, s)
    n = s // tm
    dh = d // HEADS
    m = kv.shape[0]
    scale = dh ** -0.5

    def body(x1_ref, g2_ref, wq_ref, kv_ref, wo_ref, x2_ref, o_ref, q_ref, p_ref):
        x1v = x1_ref[...]
        xn, _, _ = _rms_fwd(x1v, g2_ref[...])
        q_ref[...] = _dot(xn.astype(BF16), wq_ref[...]).astype(BF16)
        for hd in range(HEADS):
            kh = kv_ref[:, hd * dh:(hd + 1) * dh]
            vh = kv_ref[:, d + hd * dh:d + (hd + 1) * dh]
            sc = _dot_nt(q_ref[:, hd * dh:(hd + 1) * dh], kh) * scale
            e = jnp.exp(sc - jnp.max(sc, axis=-1, keepdims=True))
            p = e / jnp.sum(e, axis=-1, keepdims=True)
            p_ref[:, hd * m:(hd + 1) * m] = p
            o_ref[:, hd * dh:(hd + 1) * dh] = _dot(p.astype(BF16), vh).astype(BF16)
        x2_ref[...] = x1v + _dot(o_ref[...], wo_ref[...])

    return _host_call(
        body, "attn_fwd", (n,),
        [_rows(tm, d), _full(g2.shape), _full(w_q.shape), _full(kv.shape), _full(w_o.shape)],
        [_rows(tm, d), _rows(tm, d), _rows(tm, d), _rows(tm, HEADS * m)],
        (jax.ShapeDtypeStruct((s, d), F32), jax.ShapeDtypeStruct((s, d), BF16), jax.ShapeDtypeStruct((s, d), BF16),
         jax.ShapeDtypeStruct((s, HEADS * m), F32)),
        [], (x1, g2, w_q, kv, w_o), ("parallel",), hosted)


def _ffn_fwd_bwd(x2, g3, gf, target, w_gu, w_down):
    s, d = x2.shape
    tm = min(TM_FFN, s)
    n = s // tm
    ns = w_gu.shape[2]
    ff = 2 * ns

    def body(x2_ref, g3_ref, gf_ref, t_ref, wgu_ref, wd_ref,
             dx2_ref, act_ref, dgu_ref, xn_ref, dx3_ref, loss_ref, dgf_ref, dg3_ref):
        i = pl.program_id(0)

        @pl.when(i == 0)
        def _():
            loss_ref[...] = jnp.zeros_like(loss_ref)
            dgf_ref[...] = jnp.zeros_like(dgf_ref)
            dg3_ref[...] = jnp.zeros_like(dg3_ref)

        x2v = x2_ref[...]
        xn, xh3, r3 = _rms_fwd(x2v, g3_ref[...])
        xnb = xn.astype(BF16)
        xn_ref[...] = xnb
        x3 = x2v
        saved = []
        for j in range(2):
            g = _dot(xnb, wgu_ref[j])
            u = _dot(xnb, wgu_ref[2 + j])
            sg = 1.0 / (1.0 + jnp.exp(-g))
            sl = g * sg
            actb = (sl * u).astype(BF16)
            act_ref[:, j * ns:(j + 1) * ns] = actb
            x3 = x3 + _dot(actb, wd_ref[j * ns:(j + 1) * ns, :])
            saved.append((u, sl, sg * (1.0 + g * (1.0 - sg))))
        gfv = gf_ref[...]
        y, xhf, rf = _rms_fwd(x3, gfv)
        e = y - t_ref[...]
        loss_ref[...] += 0.5 * jnp.sum(jnp.sum(e * e, axis=-1, keepdims=True), axis=0, keepdims=True) / d
        dx3, dgf = _rms_bwd(e / d, xhf, rf, gfv)
        dgf_ref[...] += dgf
        dx3b = dx3.astype(BF16)
        dx3_ref[...] = dx3b
        dxn = jnp.zeros_like(x2v)
        for j in range(2):
            u, sl, dsl = saved[j]
            dact = _dot_nt(dx3b, wd_ref[j * ns:(j + 1) * ns, :])
            dgb = (dact * u * dsl).astype(BF16)
            dub = (dact * sl).astype(BF16)
            dgu_ref[:, j * ns:(j + 1) * ns] = dgb
            dgu_ref[:, ff + j * ns:ff + (j + 1) * ns] = dub
            dxn = dxn + _dot_nt(dgb, wgu_ref[j]) + _dot_nt(dub, wgu_ref[2 + j])
        dxr, dg3 = _rms_bwd(dxn, xh3, r3, g3_ref[...])
        dg3_ref[...] += dg3
        dx2_ref[...] = dx3 + dxr

    vec = jax.ShapeDtypeStruct((1, d), F32)
    return pl.pallas_call(
        body, name="ffn_fwd_bwd", grid=(n,),
        in_specs=[_rows(tm, d), _full(g3.shape), _full(gf.shape), _rows(tm, d), _full(w_gu.shape),
                  _full(w_down.shape)],
        out_specs=[_rows(tm, d), _rows(tm, ff), _rows(tm, 2 * ff), _rows(tm, d), _rows(tm, d),
                   _acc((1, 1)), _acc((1, d)), _acc((1, d))],
        out_shape=(jax.ShapeDtypeStruct((s, d), F32), jax.ShapeDtypeStruct((s, ff), BF16),
                   jax.ShapeDtypeStruct((s, 2 * ff), BF16), jax.ShapeDtypeStruct((s, d), BF16),
                   jax.ShapeDtypeStruct((s, d), BF16), jax.ShapeDtypeStruct((1, 1), F32), vec, vec),
        compiler_params=_params("arbitrary"),
    )(x2, g3, gf, target, w_gu, w_down)


def _attn_bwd(x1, dx2, o, ycat, qs, probs, g2, w_q, kv, w_o, hosted=None):
    s, d = x1.shape
    tm = min(TM_ATTN, s)
    n = s // tm
    dh = d // HEADS
    scale = dh ** -0.5
    m = kv.shape[0]

    def body(x1_ref, dx2_ref, o_ref, y_ref, q_ref, p_ref, g2_ref, wq_ref, kv_ref, wo_ref,
             dx1_ref, dkv_ref, dg2_ref, gwo_out, gwq_out, gwout_out, dq_ref, gwo_ref, gwq_ref, gwout_ref):
        i = pl.program_id(0)

        @pl.when(i == 0)
        def _():
            for r in (dkv_ref, dg2_ref, gwo_ref, gwq_ref, gwout_ref):
                r[...] = jnp.zeros_like(r)

        xn, xh2, r2 = _rms_fwd(x1_ref[...], g2_ref[...])
        xnb = xn.astype(BF16)
        dx2v = dx2_ref[...]
        dx2b = dx2v.astype(BF16)
        gwo_ref[...] += _dot_tn(o_ref[...], dx2b)
        do = _dot_nt(dx2b, wo_ref[...])
        for hd in range(HEADS):
            qb = q_ref[:, hd * dh:(hd + 1) * dh]
            p = p_ref[:, hd * m:(hd + 1) * m]
            kh = kv_ref[:, hd * dh:(hd + 1) * dh]
            vh = kv_ref[:, d + hd * dh:d + (hd + 1) * dh]
            dob = do[:, hd * dh:(hd + 1) * dh].astype(BF16)
            dp = _dot_nt(dob, vh)
            ds = p * (dp - jnp.sum(dp * p, axis=-1, keepdims=True))
            dsb = (ds * scale).astype(BF16)
            dq_ref[:, hd * dh:(hd + 1) * dh] = _dot(dsb, kh).astype(BF16)
            dkv_ref[:, hd * dh:(hd + 1) * dh] += _dot_tn(dsb, qb)
            dkv_ref[:, d + hd * dh:d + (hd + 1) * dh] += _dot_tn(p.astype(BF16), dob)
        dqb = dq_ref[...]
        gwq_ref[...] += _dot_tn(xnb, dqb)
        dxn = _dot_nt(dqb, wq_ref[...])
        dxr, dg2 = _rms_bwd(dxn, xh2, r2, g2_ref[...])
        dg2_ref[...] += dg2
        dx1 = dx2v + dxr
        dx1_ref[...] = dx1
        gwout_ref[...] += _dot_tn(y_ref[...], dx1.astype(BF16))

        @pl.when(i == n - 1)
        def _():
            for acc, out in ((gwo_ref, gwo_out), (gwq_ref, gwq_out), (gwout_ref, gwout_out)):
                out[...] = acc[...].astype(BF16)

    sq = jax.ShapeDtypeStruct((d, d), BF16)
    return _host_call(
        body, "attn_bwd", (n,),
        [_rows(tm, d), _rows(tm, d), _rows(tm, d), _rows(tm, d), _rows(tm, d), _rows(tm, HEADS * m),
         _full(g2.shape), _full(w_q.shape), _full(kv.shape), _full(w_o.shape)],
        [_rows(tm, d), _acc((m, 2 * d)), _acc((1, d)), _acc((d, d)), _acc((d, d)), _acc((d, d))],
        (jax.ShapeDtypeStruct((s, d), F32), jax.ShapeDtypeStruct((m, 2 * d), F32),
         jax.ShapeDtypeStruct((1, d), F32), sq, sq, sq),
        [pltpu.VMEM((tm, d), BF16)] + [pltpu.VMEM((d, d), F32)] * 3,
        (x1, dx2, o, ycat, qs, probs, g2, w_q, kv, w_o), ("arbitrary",), hosted)


def _kv_bwd(dkv, mem, memn, g_mem, w_kv):
    m, d = mem.shape
    ns = w_kv.shape[2]

    def body(dkv_ref, mem_ref, memn_ref, g_ref, w_ref, gw_ref, dg_ref):
        _, xh, _ = _rms_fwd(mem_ref[...], g_ref[...])
        dmemn = jnp.zeros((m, d), F32)
        for k in range(N_CHIPS):
            dkb = dkv_ref[:, k * ns:(k + 1) * ns].astype(BF16)
            gw_ref[k] = _dot_tn(memn_ref[...], dkb).astype(BF16)
            dmemn = dmemn + _dot_nt(dkb, w_ref[k])
        dg_ref[...] = jnp.sum(dmemn * xh, axis=0, keepdims=True)

    return pl.pallas_call(
        body, name="kv_bwd",
        out_shape=(jax.ShapeDtypeStruct((N_CHIPS, d, ns), BF16), jax.ShapeDtypeStruct((1, d), F32)),
        compiler_params=pltpu.CompilerParams(vmem_limit_bytes=VMEM_LIMIT),
    )(dkv, mem, memn, g_mem, w_kv)


def _mixer_bwd(x, dx1, h, mixed_all, th_all, g1, lng, lnb, wsp, b_t, cw, ga, gb, w_out, w_in, hosted=None):
    s, d = x.shape
    n = s // TM
    nch = TM // CHUNK
    ns = w_in.shape[2]
    nh = N_CHIPS * ns
    aw = d // 2
    hd_w = aw // HEADS

    def rev(cols):
        return pl.BlockSpec((TM, cols), lambda i: (n - 1 - i, 0))

    hprev = pl.BlockSpec((8, nh), lambda i: (jnp.maximum((n - 1 - i) * (TM // 8) - 1, 0), 0))

    def body(x_ref, dx1_ref, h_ref, hp_ref, mix_ref, th_ref, g1_ref, lng_ref, lnb_ref, wsp_ref, bt_ref, cw_ref,
             ga_ref, gb_ref, wout_ref, win_ref,
             dx_ref, dh_ref, dg1_ref, dlng_ref, dlnb_ref, dwsp_ref, dbt_ref, dcw_ref, dga_ref, dgb_ref,
             dvn_ref, dcn_ref):
        i = pl.program_id(0)

        @pl.when(i == 0)
        def _():
            for r in (dg1_ref, dlng_ref, dlnb_ref, dwsp_ref, dbt_ref, dcw_ref, dga_ref, dgb_ref, dcn_ref):
                r[...] = jnp.zeros_like(r)

        dx1v = dx1_ref[...]
        dycat = _dot_nt(dx1v.astype(BF16), wout_ref[...])
        ha = h_ref[:, 0:2 * aw]
        th = th_ref[...]
        a = 0.5 * ha * (1.0 + th)
        u = a[:, :aw]
        lngv = lng_ref[...]
        vn, vhat, rs = _layer_norm_parts(a[:, aw:], lngv, lnb_ref[...])
        vnb = vn.astype(BF16)
        wm = _tril_weights(wsp_ref)
        mixed = mix_ref[...]
        gav = ga_ref[...]
        _, yah, ra = _rms_fwd(u * mixed, gav)
        dya, dga = _rms_bwd(dycat[:, :aw], yah, ra, gav)
        dga_ref[...] += dga
        du = dya * mixed
        dmix = dya * u
        dmb = dmix.astype(BF16)
        tri = lax.broadcasted_iota(jnp.int32, (CHUNK, CHUNK), 0) >= lax.broadcasted_iota(jnp.int32, (CHUNK, CHUNK), 1)
        for hd in range(HEADS):
            dw = jnp.zeros((CHUNK, CHUNK), F32)
            db = jnp.zeros((CHUNK, 1), F32)
            for c in range(nch):
                rows = slice(c * CHUNK, (c + 1) * CHUNK)
                cols = slice(hd * hd_w, (hd + 1) * hd_w)
                dvn_ref[rows, cols] = _dot_tn(wm[hd], dmb[rows, cols])
                dw = dw + _dot_nt(dmb[rows, cols], vnb[rows, cols])
                db = db + jnp.sum(dmix[rows, cols], axis=1, keepdims=True)
            dwsp_ref[hd] += jnp.where(tri, dw, 0.0)
            dbt_ref[:, hd:hd + 1] += db
        dvn = dvn_ref[...]
        dlng_ref[...] += jnp.sum(dvn * vhat, axis=0, keepdims=True)
        dlnb_ref[...] += jnp.sum(dvn, axis=0, keepdims=True)
        dvh = dvn * lngv
        dv = rs * (dvh - jnp.mean(dvh, axis=-1, keepdims=True) - vhat * jnp.mean(dvh * vhat, axis=-1, keepdims=True))
        gprime = 0.5 * (1.0 + th) + 0.5 * ha * (1.0 - th * th) * (GELU_C * (1.0 + 3.0 * GELU_K * (ha * ha)))
        dh_ref[:, 0:2 * aw] = (jnp.concatenate([du, dv], axis=-1) * gprime).astype(BF16)
        g_b = h_ref[:, 2 * aw:3 * aw]
        g_c = h_ref[:, 3 * aw:4 * aw]
        val = h_ref[:, 4 * aw:5 * aw]
        z = g_c * val
        zp = jnp.where(i == n - 1, 0.0, hp_ref[:, 3 * aw:4 * aw] * hp_ref[:, 4 * aw:5 * aw])
        zm1, zm2 = _shift_rows(z, zp)
        cw0, cw1, cw2 = cw_ref[0:1, :], cw_ref[1:2, :], cw_ref[2:3, :]
        conv = cw0 * zm2 + cw1 * zm1 + cw2 * z
        gbv = gb_ref[...]
        _, ybh, rb = _rms_fwd(g_b * conv, gbv)
        dyb, dgb = _rms_bwd(dycat[:, aw:], ybh, rb, gbv)
        dgb_ref[...] += dgb
        dconv = dyb * g_b
        dcw_ref[0:1, :] += jnp.sum(dconv * zm2, axis=0, keepdims=True)
        dcw_ref[1:2, :] += jnp.sum(dconv * zm1, axis=0, keepdims=True)
        dcw_ref[2:3, :] += jnp.sum(dconv * z, axis=0, keepdims=True)
        nxt = dcn_ref[...]
        row = lax.broadcasted_iota(jnp.int32, dconv.shape, 0)
        dcp1 = jnp.where(row == TM - 1, nxt[0:1, :], pltpu.roll(dconv, TM - 1, 0))
        dcp2 = jnp.where(row == TM - 1, nxt[1:2, :],
                         jnp.where(row == TM - 2, nxt[0:1, :], pltpu.roll(dconv, TM - 2, 0)))
        dz = cw2 * dconv + cw1 * dcp1 + cw0 * dcp2
        dcn_ref[...] = dconv[0:8, :]
        dh_ref[:, 2 * aw:3 * aw] = (dyb * conv).astype(BF16)
        dh_ref[:, 3 * aw:4 * aw] = (dz * val).astype(BF16)
        dh_ref[:, 4 * aw:5 * aw] = (dz * g_c).astype(BF16)
        dxn = jnp.zeros((TM, d), F32)
        for k in range(N_CHIPS):
            dxn = dxn + _dot_nt(dh_ref[:, k * ns:(k + 1) * ns], win_ref[k])
        g1v = g1_ref[...]
        _, xh1, r1 = _rms_fwd(x_ref[...], g1v)
        dxr, dg1 = _rms_bwd(dxn, xh1, r1, g1v)
        dg1_ref[...] += dg1
        dx_ref[...] = dx1v + dxr

    ins = (x, dx1, h, h, mixed_all, th_all, g1, lng, lnb, wsp, b_t, cw, ga, gb, w_out, w_in)
    acc_shapes = [(1, d), (1, aw), (1, aw), wsp.shape, (CHUNK, CHUNK), cw.shape, (1, aw), (1, aw)]
    return _host_call(
        body, "mixer_bwd", (n,),
        [rev(d), rev(d), rev(nh), hprev, rev(aw), rev(d)] + [_full(a.shape) for a in ins[6:]],
        [rev(d), rev(nh)] + [_acc(sh) for sh in acc_shapes],
        (jax.ShapeDtypeStruct((s, d), F32), jax.ShapeDtypeStruct((s, nh), BF16))
        + tuple(jax.ShapeDtypeStruct(sh, F32) for sh in acc_shapes),
        [pltpu.VMEM((TM, aw), F32), pltpu.VMEM((8, aw), F32)],
        ins, ("arbitrary",), hosted)


def _weight_grad(a, b, name, tm, tn, col_sharded, hosted=None):
    t, m = a.shape
    n = b.shape[1]

    def body(a_ref, b_ref, o_ref):
        o_ref[...] = _dot_tn(a_ref[...].astype(BF16), b_ref[...].astype(BF16)).astype(BF16)

    if col_sharded:
        ns = n // N_CHIPS
        per = ns // tn
        out_shape = jax.ShapeDtypeStruct((N_CHIPS, m, ns), BF16)
        out_spec = pl.BlockSpec((None, tm, tn), lambda i, j: (j // per, i, j % per))
    else:
        out_shape = jax.ShapeDtypeStruct((m, n), BF16)
        out_spec = pl.BlockSpec((tm, tn), lambda i, j: (i, j))
    (out,), extra = _host_call(
        body, name, (m // tm, n // tn),
        [pl.BlockSpec((t, tm), lambda i, j: (0, i)), pl.BlockSpec((t, tn), lambda i, j: (0, j))],
        [out_spec], (out_shape,), [], (a, b), ("parallel", "parallel"), hosted)
    return (out if col_sharded else out.reshape(N_CHIPS, m // N_CHIPS, n)), extra


def _row_tile(rows, cap=256):
    best = None
    for t in range(16, min(rows, cap) + 1, 16):
        if rows % t == 0:
            best = t
    return best if best is not None else rows


def _adamw_math(w, g, m, v):
    m2 = ADAM_B1 * m + (1.0 - ADAM_B1) * g
    v2 = ADAM_B2 * v + (1.0 - ADAM_B2) * (g * g)
    m_hat = m2 / (1.0 - ADAM_B1 ** ADAM_STEP)
    v_hat = v2 / (1.0 - ADAM_B2 ** ADAM_STEP)
    delta = -ADAM_LR * (m_hat / (jnp.sqrt(v_hat) + ADAM_EPS) + ADAM_WD * w)
    return delta, m2, v2


def _adamw(w, g, m, v, name):
    r, c = w.shape
    tr = _row_tile(r) if r >= 16 else r

    def body(w_ref, g_ref, m_ref, v_ref, d_ref, m2_ref, v2_ref):
        d_ref[...], m2_ref[...], v2_ref[...] = _adamw_math(w_ref[...], g_ref[...], m_ref[...], v_ref[...])

    sh = jax.ShapeDtypeStruct((r, c), F32)
    return pl.pallas_call(
        body, name=name, grid=(r // tr,),
        in_specs=[_rows(tr, c)] * 4, out_specs=[_rows(tr, c)] * 3, out_shape=(sh, sh, sh),
        compiler_params=_params("parallel"),
    )(w, g, m, v)


def _finalize(items, place, name, hosted=None):
    r, c = items[0][2].shape
    tr = _row_tile(r)
    nw = len(items)

    def body(place_ref, *refs):
        ins, outs = refs[:7 * nw], refs[7 * nw:]
        for k in range(nw):
            own_ref, s0_ref, s1_ref, s2_ref, w_ref, m_ref, v_ref = ins[7 * k:7 * k + 7]
            g_ref, d_ref, m2_ref, v2_ref = outs[4 * k:4 * k + 4]
            g = own_ref[...].astype(F32) + s0_ref[...].astype(F32)
            g = (g + s1_ref[...].astype(F32)) + s2_ref[...].astype(F32)
            g_ref[...] = g
            d_ref[...], m2_ref[...], v2_ref[...] = _adamw_math(w_ref[...], g, m_ref[...], v_ref[...])

    def slot(k):
        return pl.BlockSpec((None, tr, c), lambda i, pref: (k, i, 0))

    rows = pl.BlockSpec((tr, c), lambda i, pref: (i, 0))
    sh = jax.ShapeDtypeStruct((r, c), F32)
    one = [pl.BlockSpec((None, tr, c), lambda i, pref: (pref[1], i, 0)), slot(0), slot(1), slot(2), rows, rows, rows]
    args = [a for part, slots, w, m, v in items for a in (part, slots, slots, slots, w, m, v)]
    res, extra = _host_call(body, name, (r // tr,), one * nw, [rows] * (4 * nw), (sh,) * (4 * nw), [], args,
                            ("parallel",), hosted, prefetch=(place,))
    return [res[4 * k:4 * k + 4] for k in range(nw)], extra


def _small_sum_adamw(parts, w, m, v):
    nd, r, c = parts.shape

    def body(p_ref, w_ref, m_ref, v_ref, g_ref, d_ref, m2_ref, v2_ref):
        g = p_ref[0]
        for k in range(1, nd):
            g = g + p_ref[k]
        g_ref[...] = g
        d_ref[...], m2_ref[...], v2_ref[...] = _adamw_math(w_ref[...], g, m_ref[...], v_ref[...])

    sh = jax.ShapeDtypeStruct((r, c), F32)
    return pl.pallas_call(
        body, name="small_sum_adamw", out_shape=(sh, sh, sh, sh),
        compiler_params=pltpu.CompilerParams(vmem_limit_bytes=VMEM_LIMIT),
    )(parts, w, m, v)


def _place():
    x, y, c = lax.axis_index("x"), lax.axis_index("y"), lax.axis_index("c")
    chips = [(1 - x, y), (x, 1 - y), (1 - x, 1 - y)]
    return x, y, c, 2 * x + y, chips


def _remote(src, dst, send_sem, recv_sem, to):
    return pltpu.make_async_remote_copy(src_ref=src, dst_ref=dst, send_sem=send_sem, recv_sem=recv_sem,
                                        device_id=to, device_id_type=MESH)


class _Exchange:
    def __init__(self, ins, out_shapes, sem_shape, start, finish, middle=None, in_place=False, peers=()):
        self.ins, self.out_shapes, self.sem_shape = tuple(ins), tuple(out_shapes), sem_shape
        self.start, self.finish, self.middle = start, finish, middle
        self.in_place = in_place
        self.peers = frozenset(peers)
        self.collective_id = None

    def with_id(self, collective_id):
        self.collective_id = collective_id
        return self


_FLIPS = {"c": (0, 0, 1), "x": (1, 0, 0), "y": (0, 1, 0), "xy": (1, 1, 0)}


def _handshake(peers):
    x, y, c, _, _ = _place()
    barrier = pltpu.get_barrier_semaphore()
    for name in sorted(peers):
        fx, fy, fc = _FLIPS[name]
        peer = (1 - x if fx else x, 1 - y if fy else y, 1 - c if fc else c)
        pl.semaphore_signal(barrier, inc=1, device_id=peer, device_id_type=MESH)
    pl.semaphore_wait(barrier, len(peers))


def _run_exchange(ex, name, collective_id=None, casts=()):
    n_in, n_out, n_cast = len(ex.ins), len(ex.out_shapes), len(casts)
    step = 64

    def body(*refs):
        ins, srcs = refs[:n_in], refs[n_in:n_in + n_cast]
        outs = refs[n_in + n_cast:n_in + n_cast + n_out]
        dsts = refs[n_in + n_cast + n_out:n_in + 2 * n_cast + n_out]
        rest = refs[n_in + 2 * n_cast + n_out:]
        wide, narrow = rest[:n_cast], rest[n_cast:2 * n_cast]
        send_sems, recv_sems = rest[2 * n_cast], rest[2 * n_cast + 1]
        if collective_id is not None:
            _handshake(ex.peers)
        ex.start(ins, outs, send_sems, recv_sems)
        loads = [pltpu.make_async_copy(srcs[k], wide[k], rest[2 * n_cast + 2].at[k]) for k in range(n_cast)]
        for cp in loads:
            cp.start()
        if ex.middle is not None:
            ex.middle(ins, outs, send_sems, recv_sems)
        stores = []
        for k in range(n_cast):
            loads[k].wait()

            def chunk(i, carry, k=k):
                rows = pl.ds(pl.multiple_of(i * step, step), step)
                narrow[k][rows, :] = wide[k][rows, :].astype(BF16)
                return carry
            lax.fori_loop(0, casts[k].shape[0] // step, chunk, 0)
            stores.append(pltpu.make_async_copy(narrow[k], dsts[k], rest[2 * n_cast + 2].at[k]))
            stores[-1].start()
        ex.finish(ins, outs, send_sems, recv_sems)
        for cp in stores:
            cp.wait()

    assert collective_id is None or ex.peers
    sem = pltpu.SemaphoreType.DMA(ex.sem_shape)
    scratch = ([pltpu.VMEM(a.shape, F32) for a in casts] + [pltpu.VMEM(a.shape, BF16) for a in casts] + [sem, sem]
               + ([pltpu.SemaphoreType.DMA((n_cast,))] if n_cast else []))
    return pl.pallas_call(
        body, name=name,
        out_shape=ex.out_shapes + tuple(jax.ShapeDtypeStruct(a.shape, BF16) for a in casts),
        in_specs=[ANY] * (n_in + n_cast), out_specs=[ANY] * (n_out + n_cast),
        input_output_aliases={k: k for k in range(n_in)} if ex.in_place else {}, scratch_shapes=scratch,
        compiler_params=pltpu.CompilerParams(collective_id=collective_id, vmem_limit_bytes=VMEM_LIMIT),
    )(*ex.ins, *casts)


def _host_call(body, name, grid, in_specs, out_specs, out_shape, scratch_shapes, args, semantics, hosted,
               prefetch=()):
    hosted = [] if hosted is None else (list(hosted) if isinstance(hosted, (list, tuple)) else [hosted])
    collective_id = hosted[0].collective_id if hosted else None
    peers = frozenset().union(*[ex.peers for ex in hosted]) if hosted else frozenset()
    assert collective_id is None or all(ex.peers for ex in hosted)
    n_pre, n_in, n_out, n_scr = len(prefetch), len(in_specs), len(out_specs), len(scratch_shapes)
    h_ins = [a for ex in hosted for a in ex.ins]
    h_outs = [s for ex in hosted for s in ex.out_shapes]
    h_in, h_out = len(h_ins), len(h_outs)

    def wrapped(*refs):
        pre, refs = refs[:n_pre], refs[n_pre:]
        a, hi = refs[:n_in], refs[n_in:n_in + h_in]
        o = refs[n_in + h_in:n_in + h_in + n_out]
        ho = refs[n_in + h_in + n_out:n_in + h_in + n_out + h_out]
        scr = refs[n_in + h_in + n_out + h_out:]

        def run(phase):
            i0 = o0 = 0
            for k, ex in enumerate(hosted):
                fn = getattr(ex, phase)
                if fn is not None:
                    fn(hi[i0:i0 + len(ex.ins)], ho[o0:o0 + len(ex.out_shapes)], scr[n_scr + 2 * k],
                       scr[n_scr + 2 * k + 1])
                i0, o0 = i0 + len(ex.ins), o0 + len(ex.out_shapes)

        if hosted:
            first = functools.reduce(jnp.logical_and, [pl.program_id(k) == 0 for k in range(len(grid))])

            @pl.when(first)
            def _():
                if collective_id is not None:
                    _handshake(peers)
                run("start")

        if any(ex.middle is not None for ex in hosted):
            half_way = functools.reduce(jnp.logical_and, [
                pl.program_id(0) == max(1, grid[0] * MIDDLE_STEP_16THS // 16)] + [
                pl.program_id(k) == 0 for k in range(1, len(grid))])

            @pl.when(half_way)
            def _():
                run("middle")

        body(*pre, *a, *o, *scr[:n_scr])

        if hosted:
            last = functools.reduce(jnp.logical_and, [pl.program_id(k) == grid[k] - 1 for k in range(len(grid))])

            @pl.when(last)
            def _():
                run("finish")

    sems = [pltpu.SemaphoreType.DMA(ex.sem_shape) for ex in hosted for _ in range(2)]
    aliases, i0, o0 = {}, n_pre + n_in, n_out
    for ex in hosted:
        if ex.in_place:
            aliases.update({i0 + k: o0 + k for k in range(len(ex.ins))})
        i0, o0 = i0 + len(ex.ins), o0 + len(ex.out_shapes)
    all_in, all_out = list(in_specs) + [ANY] * h_in, list(out_specs) + [ANY] * h_out
    all_scr = list(scratch_shapes) + sems
    params = _params(*(["arbitrary"] * len(grid) if hosted else semantics), collective_id=collective_id)
    shapes = tuple(out_shape) + tuple(h_outs)
    if n_pre:
        call = pl.pallas_call(
            wrapped, name=name, out_shape=shapes, input_output_aliases=aliases, compiler_params=params,
            grid_spec=pltpu.PrefetchScalarGridSpec(num_scalar_prefetch=n_pre, grid=grid, in_specs=all_in,
                                                   out_specs=all_out, scratch_shapes=all_scr))
    else:
        call = pl.pallas_call(
            wrapped, name=name, grid=grid, in_specs=all_in, out_specs=all_out, out_shape=shapes,
            scratch_shapes=all_scr, input_output_aliases=aliases, compiler_params=params)
    res = call(*prefetch, *args, *h_ins)
    return res[:n_out], res[n_out:]


def _all_gather(shards, small=()):
    items = tuple(shards) + tuple(small)
    nw = len(shards)

    def place():
        x, y, c, me, _ = _place()
        first = (x + (1 - c) * (1 - 2 * x), y + c * (1 - 2 * y))
        second = (x + c * (1 - 2 * x), y + (1 - c) * (1 - 2 * y))
        diag = (1 - x, 1 - y)
        return x, y, c, me, (first, second, diag)

    def halves(w, c):
        rh = items[w].shape[0] // 2
        return pl.ds(c * rh, rh), pl.ds((1 - c) * rh, rh)

    def start(ins, outs, ss, rs):
        x, y, c, me, chips = place()
        for w in range(len(items)):
            _remote(ins[w], outs[w].at[me], ss.at[w, 6], rs.at[w, 6], (x, y, 1 - c)).start()
            if w < nw:
                mine, _ = halves(w, c)
                _remote(ins[w].at[mine], outs[w].at[me, mine], ss.at[w, 0], rs.at[w, 0], (*chips[0], c)).start()
            else:
                for k in range(3):
                    _remote(ins[w], outs[w].at[me], ss.at[w, k], rs.at[w, k], (*chips[k], c)).start()

    def onward(outs, ss, rs, w, k, x, y, c, chips):
        mine, _ = halves(w, c)
        pk = 2 * chips[k][0] + chips[k][1]
        got = outs[w].at[pk, mine]
        src = chips[1] if k == 2 else chips[k]
        _remote(got, got, ss.at[w, k], rs.at[w, k], (*src, c)).wait_recv()
        if k == 0:
            _remote(got, got, ss.at[w, 2], rs.at[w, 2], (*chips[1], c)).start()
        _remote(got, got, ss.at[w, 3 + k], rs.at[w, 3 + k], (x, y, 1 - c)).start()

    def middle(ins, outs, ss, rs):
        x, y, c, me, chips = place()
        for w in range(nw):
            mine, _ = halves(w, c)
            _remote(ins[w].at[mine], outs[w].at[me, mine], ss.at[w, 1], rs.at[w, 1], (*chips[1], c)).start()
        for w in range(nw):
            onward(outs, ss, rs, w, 0, x, y, c, chips)

    def finish(ins, outs, ss, rs):
        x, y, c, me, chips = place()
        sib = (x, y, 1 - c)
        for k in (1, 2):
            for w in range(nw):
                onward(outs, ss, rs, w, k, x, y, c, chips)
        for w in range(len(items)):
            if w < nw:
                mine, theirs = halves(w, c)
                for k, chip in ((3, chips[1]), (4, chips[0]), (5, chips[2])):
                    oth = outs[w].at[2 * chip[0] + chip[1], theirs]
                    _remote(oth, oth, ss.at[w, k], rs.at[w, k], sib).wait_recv()
                own = ins[w].at[mine]
                for k in range(6):
                    _remote(own, own, ss.at[w, k], rs.at[w, k], sib).wait_send()
            else:
                for k in range(3):
                    got = outs[w].at[2 * chips[k][0] + chips[k][1]]
                    _remote(got, got, ss.at[w, k], rs.at[w, k], (*chips[k], c)).wait_recv()
                    _remote(ins[w], ins[w], ss.at[w, k], rs.at[w, k], sib).wait_send()
            _remote(ins[w], outs[w].at[me], ss.at[w, 6], rs.at[w, 6], sib).wait()

    out_shapes = tuple(jax.ShapeDtypeStruct((N_CHIPS,) + a.shape, a.dtype) for a in items)
    return _Exchange(items, out_shapes, (len(items), 7), start, finish, middle if nw else None,
                     peers=("c", "x", "y", "xy") if small else ("c", "x", "y"))


def _chip_reduce(grads, name, collective_id):
    nw = len(grads)
    step = 64

    def body(*refs):
        ins, outs = refs[:nw], refs[nw:2 * nw]
        own, got = refs[2 * nw:3 * nw], refs[3 * nw:4 * nw]
        send_sems, recv_sems, local_sems = refs[4 * nw:]
        x, y, c, _, _ = _place()
        barrier = pltpu.get_barrier_semaphore()
        pl.semaphore_signal(barrier, inc=1, device_id=(x, y, 1 - c), device_id_type=MESH)
        pl.semaphore_wait(barrier, 1)
        moves = []
        for w in range(nw):
            nb, rh = grads[w].shape[0], grads[w].shape[1] // 2
            for k in range(nb):
                away = _remote(ins[w].at[k, pl.ds((1 - c) * rh, rh), :], got[w].at[k], send_sems.at[w, k],
                               recv_sems.at[w, k], (x, y, 1 - c))
                mine = pltpu.make_async_copy(ins[w].at[k, pl.ds(c * rh, rh), :], own[w].at[k], local_sems.at[w, k])
                away.start()
                mine.start()
                moves.append((w, k, away, mine))
        back = []
        for w, k, away, mine in moves:
            rh = own[w].shape[1]
            mine.wait()
            away.wait()

            def add(i, carry, w=w, k=k):
                rows = pl.ds(pl.multiple_of(i * step, step), step)
                own[w][k, rows, :] = (own[w][k, rows, :].astype(F32) + got[w][k, rows, :].astype(F32)).astype(BF16)
                return carry
            lax.fori_loop(0, rh // step, add, 0)
            tail = rh % step
            if tail:
                rows = slice(rh - tail, rh)
                own[w][k, rows, :] = (own[w][k, rows, :].astype(F32) + got[w][k, rows, :].astype(F32)).astype(BF16)
            wb = pltpu.make_async_copy(own[w].at[k], outs[w].at[k, pl.ds(c * rh, rh), :], local_sems.at[w, k])
            wb.start()
            back.append(wb)
        for wb in back:
            wb.wait()

    halves = [pltpu.VMEM((g.shape[0], g.shape[1] // 2, g.shape[2]), BF16) for g in grads]
    sem = pltpu.SemaphoreType.DMA((nw, N_CHIPS))
    return pl.pallas_call(
        body, name=name, out_shape=tuple(jax.ShapeDtypeStruct(g.shape, BF16) for g in grads),
        in_specs=[ANY] * nw, out_specs=[ANY] * nw, scratch_shapes=halves + halves + [sem, sem, sem],
        compiler_params=pltpu.CompilerParams(vmem_limit_bytes=VMEM_LIMIT, collective_id=collective_id),
    )(*grads)


def _scatter_partials(parts):
    nw = len(parts)

    def copies(ins, outs, ss, rs):
        _, _, c, _, chips = _place()
        res = []
        for r, (px, py) in enumerate(chips):
            for w in range(nw):
                rh = parts[w].shape[1] // 2
                rows = pl.ds(c * rh, rh)
                res.append(_remote(ins[w].at[2 * px + py, rows], outs[w].at[r, rows], ss.at[w, r], rs.at[w, r],
                                   (px, py, c)))
        return res

    def start(ins, outs, ss, rs):
        for cp in copies(ins, outs, ss, rs):
            cp.start()

    def finish(ins, outs, ss, rs):
        for cp in copies(ins, outs, ss, rs):
            cp.wait()

    out_shapes = tuple(jax.ShapeDtypeStruct((3,) + p.shape[1:], p.dtype) for p in parts)
    return _Exchange(parts, out_shapes, (nw, 3), start, finish, peers=("x", "y", "xy"))


def _join_partials(parts, slots):
    nw = len(parts)

    def copies(outs, ss, rs, mine):
        x, y, c, me, _ = _place()
        res = []
        for w in range(nw):
            rh = parts[w].shape[1] // 2
            rows = pl.ds((c if mine else 1 - c) * rh, rh)
            own = outs[w].at[me, rows]
            got = outs[nw + w].at[:, rows, :]
            res.append(_remote(own, own, ss.at[w, 0], rs.at[w, 0], (x, y, 1 - c)))
            res.append(_remote(got, got, ss.at[w, 1], rs.at[w, 1], (x, y, 1 - c)))
        return res

    def start(ins, outs, ss, rs):
        for cp in copies(outs, ss, rs, True):
            cp.start()

    def finish(ins, outs, ss, rs):
        for cp in copies(outs, ss, rs, True):
            cp.wait_send()
        for cp in copies(outs, ss, rs, False):
            cp.wait_recv()

    arrays = tuple(parts) + tuple(slots)
    return _Exchange(arrays, tuple(jax.ShapeDtypeStruct(a.shape, a.dtype) for a in arrays), (nw, 2), start, finish,
                     in_place=True, peers=("c",))


def _gather_small(slab):
    def copies(ins, outs, ss, rs):
        x, y, c, _, _ = _place()
        me = 4 * x + 2 * y + c
        out, arrivals = [], []
        for k in range(1, 8):
            px = 1 - x if k & 4 else x
            py = 1 - y if k & 2 else y
            pc = 1 - c if k & 1 else c
            out.append(_remote(ins[0], outs[0].at[me], ss.at[k - 1], rs.at[k - 1], (px, py, pc)))
            theirs = outs[0].at[4 * px + 2 * py + pc]
            arrivals.append((theirs, k - 1, (px, py, pc)))
        return pltpu.make_async_copy(ins[0], outs[0].at[me], ss.at[7]), out, arrivals

    def start(ins, outs, ss, rs):
        own, out, _ = copies(ins, outs, ss, rs)
        own.start()
        for cp in out:
            cp.start()

    def finish(ins, outs, ss, rs):
        own, out, arrivals = copies(ins, outs, ss, rs)
        for cp in out:
            cp.wait_send()
        for theirs, k, peer in arrivals:
            _remote(theirs, theirs, ss.at[k], rs.at[k], peer).wait_recv()
        own.wait()

    return _Exchange((slab,), (jax.ShapeDtypeStruct((8,) + slab.shape, slab.dtype),), (8,), start, finish)


_SMALL_VECS = ("ln_mix_g", "ln_attn_g", "ln_mem_g", "ln_ffn_g", "ln_final_g")


def _pack_small(p, extra, conv):
    d = p["ln_mix_g"].shape[-1]
    top = [p[k].reshape(1, d) for k in _SMALL_VECS]
    top.append(jnp.concatenate([p["sgu_ln_g"].reshape(-1), p["sgu_ln_b"].reshape(-1)]).reshape(1, d))
    top.append(jnp.concatenate([p["grp_norm_a"].reshape(-1), p["grp_norm_b"].reshape(-1)]).reshape(1, d))
    top.append(jnp.concatenate([p["b_spatial"].reshape(-1), extra]).reshape(1, d))
    mid = jnp.zeros((8, d), F32)
    if conv is not None:
        mid = jnp.pad(conv, ((0, 5), (0, d - conv.shape[1])))
    return jnp.concatenate([jnp.concatenate(top, axis=0), mid, p["w_spatial"].reshape(-1, d)], axis=0)


def _unpack_small(slab):
    d = slab.shape[1]
    hw = d // 2
    out = {k: slab[i] for i, k in enumerate(_SMALL_VECS)}
    out["sgu_ln_g"], out["sgu_ln_b"] = slab[5, :hw], slab[5, hw:]
    out["grp_norm_a"], out["grp_norm_b"] = slab[6, :hw], slab[6, hw:]
    out["b_spatial"] = slab[7, :hw].reshape(HEADS, CHUNK)
    out["w_spatial"] = slab[16:].reshape(HEADS, CHUNK, CHUNK)
    return out


_BIG = ("w_in", "w_kv", "w_gate_up", "w_out", "w_q", "w_o", "w_down")
_WEIGHTS = ("ln_mix_g", "w_in", "sgu_ln_g", "sgu_ln_b", "w_spatial", "b_spatial", "conv_w", "grp_norm_a",
            "grp_norm_b", "w_out", "ln_attn_g", "ln_mem_g", "w_q", "w_kv", "w_o", "ln_ffn_g", "w_gate_up",
            "w_down", "ln_final_g")


def _step(p, m_, v_, x, mem, target):
    s, d = x.shape
    hw = d // 2
    row = lambda a: a.reshape(1, -1)
    x_, y_, c_ = lax.axis_index("x"), lax.axis_index("y"), lax.axis_index("c")
    chip = 2 * x_ + y_

    conv8 = jnp.pad(p["conv_w"], ((0, 5), (0, 0)))
    later = ("w_kv", "w_q", "w_o", "w_down", "w_gate_up")
    first = _run_exchange(_all_gather([p["w_in"].astype(BF16), p["w_out"].astype(BF16)], [conv8]),
                          "all_gather_mixer", collective_id=4, casts=[p[k] for k in later])
    (w_in, w_out4, conv4), bf = first[:3], dict(zip(later, first[3:]))
    cw = jnp.transpose(conv4[:, :3, :], (1, 0, 2)).reshape(3, hw)
    b_t = jnp.pad(jnp.transpose(p["b_spatial"]), ((0, 0), (0, CHUNK - HEADS)))
    g1, g2, gm, g3, gf = (row(p[k]) for k in _SMALL_VECS)
    lng, lnb, ga, gb = row(p["sgu_ln_g"]), row(p["sgu_ln_b"]), row(p["grp_norm_a"]), row(p["grp_norm_b"])
    wsp = p["w_spatial"]
    w_out = w_out4.reshape(-1, d)

    (h, x1, ycat, xn1, mixed, th), (w_kv, w_q4, w_o4, w_down4) = _mixer_fwd(
        x, g1, w_in, lng, lnb, wsp, b_t, cw, ga, gb, w_out,
        hosted=_all_gather([bf[k] for k in ("w_kv", "w_q", "w_o", "w_down")]).with_id(5))
    w_q, w_o, w_down = (a.reshape(-1, d) for a in (w_q4, w_o4, w_down4))
    memn, kv = _kv_fwd(mem, gm, w_kv)
    (x2, o, qs, probs), (w_gu,) = _attn_fwd(x1, g2, w_q, kv, w_o,
                                            hosted=_all_gather([bf["w_gate_up"]]).with_id(6))
    dx2, act, dgu, xn3, dx3, loss, dgf, dg3 = _ffn_fwd_bwd(x2, g3, gf, target, w_gu, w_down)

    place = jnp.stack([c_, chip]).astype(jnp.int32)

    def chip_partials(names, grads, tag):
        return list(_chip_reduce(grads, "chip_reduce_" + tag, ("down", "ffn", "attn", "mixer").index(tag)))

    names_d = ("w_down",)
    parts_d = chip_partials(names_d, (_weight_grad(act, dx3, "grad_w_down", 1408, 512, False)[0],), "down")
    g_gu, slots_d = _weight_grad(xn3, dgu, "grad_w_gate_up", 512, 1408, True,
                                 hosted=_scatter_partials(parts_d).with_id(7))
    names_a = ("w_gate_up",)
    parts_a = chip_partials(names_a, (g_gu,), "ffn")
    (dx1, dkv, dg2, g_o, g_q, g_out), slots_a = _attn_bwd(x1, dx2, o, ycat, qs, probs, g2, w_q, kv, w_o,
                                                           hosted=_scatter_partials(parts_a).with_id(8))
    g_kv, dgm = _kv_bwd(dkv, mem, memn, gm, w_kv)
    names_b = ("w_o", "w_out", "w_q", "w_kv")
    shard_major = lambda g: g.reshape(N_CHIPS, -1, d)
    parts_b = chip_partials(names_b, (shard_major(g_o), shard_major(g_out), shard_major(g_q), g_kv), "attn")
    names_da = names_d + names_a
    (dx, dh, dg1, dlng, dlnb, dwsp, dbt, dcw, dga, dgb), extra = _mixer_bwd(
        x, dx1, h, mixed, th, g1, lng, lnb, wsp, b_t, cw, ga, gb, w_out, w_in,
        hosted=[_scatter_partials(parts_b).with_id(9), _join_partials(parts_d + parts_a, slots_d + slots_a)])
    slots_b, joined = extra[:len(names_b)], extra[len(names_b):]
    whole = dict(zip(names_da, zip(joined[:len(names_da)], joined[len(names_da):])))
    small = {"ln_mix_g": dg1, "ln_attn_g": dg2, "ln_mem_g": dgm, "ln_ffn_g": dg3, "ln_final_g": dgf,
             "sgu_ln_g": dlng, "sgu_ln_b": dlnb, "grp_norm_a": dga, "grp_norm_b": dgb,
             "b_spatial": jnp.transpose(dbt[:, :HEADS]), "w_spatial": dwsp}
    loss_vec = jnp.pad(loss.reshape(1), (0, hw - 1))
    g_in, extra = _weight_grad(
        xn1, dh, "grad_w_in", 1024, 640, True,
        hosted=[_gather_small(_pack_small(small, loss_vec, dcw)), _join_partials(parts_b, slots_b)])
    parts = extra[0]
    whole.update(zip(names_b, zip(extra[1:1 + len(names_b)], extra[1 + len(names_b):])))
    (part_in,) = chip_partials(("w_in",), (g_in,), "mixer")
    out_g, out_d, out_m, out_v = {}, {}, {}, {}

    def finalize(ks, tag, hosted=None):
        done, res = _finalize([(whole[k][0], whole[k][1], p[k], m_[k], v_[k]) for k in ks], place,
                              "finalize_" + tag, hosted)
        for k, (g, dl, nm, nv) in zip(ks, done):
            out_g[k], out_d[k], out_m[k], out_v[k] = g, dl, nm, nv
        return res

    (slots_in,) = finalize(("w_down",), "w_down", _scatter_partials([part_in]).with_id(10))
    finalize(("w_o", "w_out", "w_q"), "attn")
    for k in ("w_gate_up", "w_kv"):
        finalize((k,), k)
    whole["w_in"] = _run_exchange(_join_partials([part_in], [slots_in]), "rs_join_mixer", collective_id=11)
    finalize(("w_in",), "w_in")

    zeros = jnp.zeros((hw,), F32)
    sg, sd, sm, sv = _small_sum_adamw(parts, _pack_small(p, zeros, None), _pack_small(m_, zeros, None),
                                      _pack_small(v_, zeros, None))
    for tree, slab in zip((out_g, out_d, out_m, out_v), (sg, sd, sm, sv)):
        tree.update(_unpack_small(slab))
    loss_out = sg[7, hw]
    g_conv = lax.dynamic_slice(sg[8:11, :hw], (0, chip * (hw // N_CHIPS)), (3, hw // N_CHIPS))
    out_g["conv_w"] = g_conv
    out_d["conv_w"], out_m["conv_w"], out_v["conv_w"] = _adamw(p["conv_w"], g_conv, m_["conv_w"], v_["conv_w"],
                                                                "adamw_conv_w")
    return loss_out, dx, out_g, out_d, out_m, out_v


def kernel(x, mem, ln_mix_g, w_in, sgu_ln_g, sgu_ln_b, w_spatial, b_spatial, conv_w, grp_norm_a, grp_norm_b, w_out, ln_attn_g, ln_mem_g, w_q, w_kv, w_o, ln_ffn_g, w_gate_up, w_down, ln_final_g, loss_target, m_ln_mix_g, m_w_in, m_sgu_ln_g, m_sgu_ln_b, m_w_spatial, m_b_spatial, m_conv_w, m_grp_norm_a, m_grp_norm_b, m_w_out, m_ln_attn_g, m_ln_mem_g, m_w_q, m_w_kv, m_w_o, m_ln_ffn_g, m_w_gate_up, m_w_down, m_ln_final_g, v_ln_mix_g, v_w_in, v_sgu_ln_g, v_sgu_ln_b, v_w_spatial, v_b_spatial, v_conv_w, v_grp_norm_a, v_grp_norm_b, v_w_out, v_ln_attn_g, v_ln_mem_g, v_w_q, v_w_kv, v_w_o, v_ln_ffn_g, v_w_gate_up, v_w_down, v_ln_final_g):
    p = dict(ln_mix_g=ln_mix_g, w_in=w_in, sgu_ln_g=sgu_ln_g, sgu_ln_b=sgu_ln_b, w_spatial=w_spatial,
             b_spatial=b_spatial, conv_w=conv_w, grp_norm_a=grp_norm_a, grp_norm_b=grp_norm_b, w_out=w_out,
             ln_attn_g=ln_attn_g, ln_mem_g=ln_mem_g, w_q=w_q, w_kv=w_kv, w_o=w_o, ln_ffn_g=ln_ffn_g,
             w_gate_up=w_gate_up, w_down=w_down, ln_final_g=ln_final_g)
    m_ = dict(ln_mix_g=m_ln_mix_g, w_in=m_w_in, sgu_ln_g=m_sgu_ln_g, sgu_ln_b=m_sgu_ln_b, w_spatial=m_w_spatial,
              b_spatial=m_b_spatial, conv_w=m_conv_w, grp_norm_a=m_grp_norm_a, grp_norm_b=m_grp_norm_b,
              w_out=m_w_out, ln_attn_g=m_ln_attn_g, ln_mem_g=m_ln_mem_g, w_q=m_w_q, w_kv=m_w_kv, w_o=m_w_o,
              ln_ffn_g=m_ln_ffn_g, w_gate_up=m_w_gate_up, w_down=m_w_down, ln_final_g=m_ln_final_g)
    v_ = dict(ln_mix_g=v_ln_mix_g, w_in=v_w_in, sgu_ln_g=v_sgu_ln_g, sgu_ln_b=v_sgu_ln_b, w_spatial=v_w_spatial,
              b_spatial=v_b_spatial, conv_w=v_conv_w, grp_norm_a=v_grp_norm_a, grp_norm_b=v_grp_norm_b,
              w_out=v_w_out, ln_attn_g=v_ln_attn_g, ln_mem_g=v_ln_mem_g, w_q=v_w_q, w_kv=v_w_kv, w_o=v_w_o,
              ln_ffn_g=v_ln_ffn_g, w_gate_up=v_w_gate_up, w_down=v_w_down, ln_final_g=v_ln_final_g)
    s, d = x.shape[-2], x.shape[-1]
    loss, dx, g, dl, nm, nv = _step(p, m_, v_, x.reshape(s, d), mem.reshape(-1, d), loss_target.reshape(s, d))
    outs = [loss, dx.reshape(x.shape)]
    for tree in (g, dl, nm, nv):
        outs += [tree[k].reshape(p[k].shape) for k in _WEIGHTS]
    return tuple(outs)
```

```python
import functools
import math

import jax
import jax.numpy as jnp
from jax import lax
from jax.experimental import pallas as pl
from jax.experimental.pallas import tpu as pltpu

F32 = jnp.float32
BF16 = jnp.bfloat16
EPS = 1e-6
CHUNK = 128
HEADS = 4
N_CHIPS = 4
TM = 512
TM_ATTN = 512
TM_FFN = 256
ADAM_LR, ADAM_B1, ADAM_B2, ADAM_EPS, ADAM_WD, ADAM_STEP = 0.001, 0.9, 0.999, 1e-08, 0.01, 10
GELU_C = math.sqrt(2.0 / math.pi)
GELU_K = 0.044715
SMALL_ROWS = 80
VMEM_LIMIT = 56 * 1024 * 1024
MIDDLE_STEP_16THS = 7
LATE_STEP_16THS = 13
MESH = pl.DeviceIdType.MESH
ANY = pl.BlockSpec(memory_space=pl.ANY)


def _params(*sem, collective_id=None):
    return pltpu.CompilerParams(dimension_semantics=sem, vmem_limit_bytes=VMEM_LIMIT, collective_id=collective_id)


def _dot(a, b):
    return jnp.dot(a, b, preferred_element_type=F32)


def _dot_nt(a, b):
    return lax.dot_general(a, b, (((1,), (1,)), ((), ())), preferred_element_type=F32)


def _dot_tn(a, b):
    return lax.dot_general(a, b, (((0,), (0,)), ((), ())), preferred_element_type=F32)


def _rms_fwd(x, g):
    r = lax.rsqrt(jnp.mean(x * x, axis=-1, keepdims=True) + EPS)
    xh = x * r
    return xh * g, xh, r


def _rms_bwd(dy, xh, r, g):
    dxh = dy * g
    dx = r * (dxh - xh * jnp.mean(dxh * xh, axis=-1, keepdims=True))
    return dx, jnp.sum(dy * xh, axis=0, keepdims=True)


def _full(shape):
    nd = len(shape)
    return pl.BlockSpec(shape, lambda *_: (0,) * nd, pipeline_mode=pl.Buffered(1))


def _acc(shape):
    nd = len(shape)
    return pl.BlockSpec(shape, lambda *_: (0,) * nd)


def _rows(tm, cols):
    return pl.BlockSpec((tm, cols), lambda i: (i, 0))


def _tril_weights(wsp_ref):
    row = lax.broadcasted_iota(jnp.int32, (CHUNK, CHUNK), 0)
    col = lax.broadcasted_iota(jnp.int32, (CHUNK, CHUNK), 1)
    return [jnp.where(row >= col, wsp_ref[hd], 0.0).astype(BF16) for hd in range(HEADS)]


def _shift_rows(z, zp):
    row = lax.broadcasted_iota(jnp.int32, z.shape, 0)
    zm1 = jnp.where(row == 0, zp[7:8, :], pltpu.roll(z, 1, 0))
    zm2 = jnp.where(row == 0, zp[6:7, :], jnp.where(row == 1, zp[7:8, :], pltpu.roll(z, 2, 0)))
    return zm1, zm2


def _gelu_parts(x):
    t = jnp.tanh(GELU_C * (x + GELU_K * (x * x * x)))
    return 0.5 * x * (1.0 + t), t


def _layer_norm_parts(v, g, b):
    mu = jnp.mean(v, axis=-1, keepdims=True)
    vc = v - mu
    rs = lax.rsqrt(jnp.mean(vc * vc, axis=-1, keepdims=True) + EPS)
    vhat = vc * rs
    return vhat * g + b, vhat, rs


def _kv_fwd(mem, g_mem, w_kv):
    m, d = mem.shape
    ns = w_kv.shape[2]

    def body(mem_ref, g_ref, w_ref, memn_ref, kv_ref):
        y, _, _ = _rms_fwd(mem_ref[...], g_ref[...])
        yb = y.astype(BF16)
        memn_ref[...] = yb
        for k in range(N_CHIPS):
            kv_ref[:, k * ns:(k + 1) * ns] = _dot(yb, w_ref[k]).astype(BF16)

    return pl.pallas_call(
        body, name="kv_fwd",
        out_shape=(jax.ShapeDtypeStruct((m, d), BF16), jax.ShapeDtypeStruct((m, N_CHIPS * ns), BF16)),
        compiler_params=pltpu.CompilerParams(vmem_limit_bytes=VMEM_LIMIT),
    )(mem, g_mem, w_kv)


def _mixer_fwd(x, g1, w_in, lng, lnb, wsp, b_t, cw, ga, gb, w_out, hosted=None):
    s, d = x.shape
    n = s // TM
    nch = TM // CHUNK
    ns = w_in.shape[2]
    nh = N_CHIPS * ns
    aw = d // 2
    hd_w = aw // HEADS

    def body(x_ref, g1_ref, win_ref, lng_ref, lnb_ref, wsp_ref, bt_ref, cw_ref, ga_ref, gb_ref, wout_ref,
             h_ref, x1_ref, y_ref, xn_ref, mix_ref, th_ref, zp_ref):
        i = pl.program_id(0)

        @pl.when(i == 0)
        def _():
            zp_ref[...] = jnp.zeros_like(zp_ref)

        x = x_ref[...]
        xn, _, _ = _rms_fwd(x, g1_ref[...])
        xnb = xn.astype(BF16)
        xn_ref[...] = xnb
        for k in range(N_CHIPS):
            h_ref[:, k * ns:(k + 1) * ns] = _dot(xnb, win_ref[k])
        a, th = _gelu_parts(h_ref[:, 0:2 * aw])
        th_ref[...] = th
        u = a[:, :aw]
        vn, _, _ = _layer_norm_parts(a[:, aw:], lng_ref[...], lnb_ref[...])
        vnb = vn.astype(BF16)
        wm = _tril_weights(wsp_ref)
        for c in range(nch):
            for hd in range(HEADS):
                blk = vnb[c * CHUNK:(c + 1) * CHUNK, hd * hd_w:(hd + 1) * hd_w]
                mix_ref[c * CHUNK:(c + 1) * CHUNK, hd * hd_w:(hd + 1) * hd_w] = _dot(wm[hd], blk) + bt_ref[:, hd:hd + 1]
        ya, _, _ = _rms_fwd(u * mix_ref[...], ga_ref[...])
        g_b = h_ref[:, 2 * aw:3 * aw]
        z = h_ref[:, 3 * aw:4 * aw] * h_ref[:, 4 * aw:5 * aw]
        zm1, zm2 = _shift_rows(z, zp_ref[...])
        conv = cw_ref[0:1, :] * zm2 + cw_ref[1:2, :] * zm1 + cw_ref[2:3, :] * z
        yb, _, _ = _rms_fwd(g_b * conv, gb_ref[...])
        zp_ref[...] = z[TM - 8:TM, :]
        ycat = jnp.concatenate([ya, yb], axis=-1).astype(BF16)
        y_ref[...] = ycat
        x1_ref[...] = x + _dot(ycat, wout_ref[...])

    return _host_call(
        body, "mixer_fwd", (n,),
        [_rows(TM, d), _full(g1.shape), _full(w_in.shape), _full(lng.shape), _full(lnb.shape),
         _full(wsp.shape), _full(b_t.shape), _full(cw.shape), _full(ga.shape), _full(gb.shape),
         _full(w_out.shape)],
        [_rows(TM, nh), _rows(TM, d), _rows(TM, d), _rows(TM, d), _rows(TM, aw), _rows(TM, d)],
        (jax.ShapeDtypeStruct((s, nh), F32), jax.ShapeDtypeStruct((s, d), F32),
         jax.ShapeDtypeStruct((s, d), BF16), jax.ShapeDtypeStruct((s, d), BF16),
         jax.ShapeDtypeStruct((s, aw), F32), jax.ShapeDtypeStruct((s, d), F32)),
        [pltpu.VMEM((8, aw), F32)],
        (x, g1, w_in, lng, lnb, wsp, b_t, cw, ga, gb, w_out), ("arbitrary",), hosted)


def _attn_fwd(x1, g2, w_q, kv, w_o, hosted=None):
    s, d = x1.shape
    tm = min(TM_ATTN, s)
    n = s // tm
    dh = d // HEADS
    m = kv.shape[0]
    scale = dh ** -0.5

    def body(x1_ref, g2_ref, wq_ref, kv_ref, wo_ref, x2_ref, o_ref, q_ref, p_ref):
        x1v = x1_ref[...]
        xn, _, _ = _rms_fwd(x1v, g2_ref[...])
        q_ref[...] = _dot(xn.astype(BF16), wq_ref[...]).astype(BF16)
        for hd in range(HEADS):
            kh = kv_ref[:, hd * dh:(hd + 1) * dh]
            vh = kv_ref[:, d + hd * dh:d + (hd + 1) * dh]
            sc = _dot_nt(q_ref[:, hd * dh:(hd + 1) * dh], kh) * scale
            e = jnp.exp(sc - jnp.max(sc, axis=-1, keepdims=True))
            p = e / jnp.sum(e, axis=-1, keepdims=True)
            p_ref[:, hd * m:(hd + 1) * m] = p
            o_ref[:, hd * dh:(hd + 1) * dh] = _dot(p.astype(BF16), vh).astype(BF16)
        x2_ref[...] = x1v + _dot(o_ref[...], wo_ref[...])

    return _host_call(
        body, "attn_fwd", (n,),
        [_rows(tm, d), _full(g2.shape), _full(w_q.shape), _full(kv.shape), _full(w_o.shape)],
        [_rows(tm, d), _rows(tm, d), _rows(tm, d), _rows(tm, HEADS * m)],
        (jax.ShapeDtypeStruct((s, d), F32), jax.ShapeDtypeStruct((s, d), BF16), jax.ShapeDtypeStruct((s, d), BF16),
         jax.ShapeDtypeStruct((s, HEADS * m), F32)),
        [], (x1, g2, w_q, kv, w_o), ("parallel",), hosted)


def _ffn_fwd_bwd(x2, g3, gf, target, w_gu, w_down):
    s, d = x2.shape
    tm = min(TM_FFN, s)
    n = s // tm
    ns = w_gu.shape[2]
    ff = 2 * ns

    def body(x2_ref, g3_ref, gf_ref, t_ref, wgu_ref, wd_ref,
             dx2_ref, act_ref, dgu_ref, xn_ref, dx3_ref, loss_ref, dgf_ref, dg3_ref):
        i = pl.program_id(0)

        @pl.when(i == 0)
        def _():
            loss_ref[...] = jnp.zeros_like(loss_ref)
            dgf_ref[...] = jnp.zeros_like(dgf_ref)
            dg3_ref[...] = jnp.zeros_like(dg3_ref)

        x2v = x2_ref[...]
        xn, xh3, r3 = _rms_fwd(x2v, g3_ref[...])
        xnb = xn.astype(BF16)
        xn_ref[...] = xnb
        x3 = x2v
        saved = []
        for j in range(2):
            g = _dot(xnb, wgu_ref[j])
            u = _dot(xnb, wgu_ref[2 + j])
            sg = 1.0 / (1.0 + jnp.exp(-g))
            sl = g * sg
            actb = (sl * u).astype(BF16)
            act_ref[:, j * ns:(j + 1) * ns] = actb
            x3 = x3 + _dot(actb, wd_ref[j * ns:(j + 1) * ns, :])
            saved.append((u, sl, sg * (1.0 + g * (1.0 - sg))))
        gfv = gf_ref[...]
        y, xhf, rf = _rms_fwd(x3, gfv)
        e = y - t_ref[...]
        loss_ref[...] += 0.5 * jnp.sum(jnp.sum(e * e, axis=-1, keepdims=True), axis=0, keepdims=True) / d
        dx3, dgf = _rms_bwd(e / d, xhf, rf, gfv)
        dgf_ref[...] += dgf
        dx3b = dx3.astype(BF16)
        dx3_ref[...] = dx3b
        dxn = jnp.zeros_like(x2v)
        for j in range(2):
            u, sl, dsl = saved[j]
            dact = _dot_nt(dx3b, wd_ref[j * ns:(j + 1) * ns, :])
            dgb = (dact * u * dsl).astype(BF16)
            dub = (dact * sl).astype(BF16)
            dgu_ref[:, j * ns:(j + 1) * ns] = dgb
            dgu_ref[:, ff + j * ns:ff + (j + 1) * ns] = dub
            dxn = dxn + _dot_nt(dgb, wgu_ref[j]) + _dot_nt(dub, wgu_ref[2 + j])
        dxr, dg3 = _rms_bwd(dxn, xh3, r3, g3_ref[...])
        dg3_ref[...] += dg3
        dx2_ref[...] = dx3 + dxr

    vec = jax.ShapeDtypeStruct((1, d), F32)
    return pl.pallas_call(
        body, name="ffn_fwd_bwd", grid=(n,),
        in_specs=[_rows(tm, d), _full(g3.shape), _full(gf.shape), _rows(tm, d), _full(w_gu.shape),
                  _full(w_down.shape)],
        out_specs=[_rows(tm, d), _rows(tm, ff), _rows(tm, 2 * ff), _rows(tm, d), _rows(tm, d),
                   _acc((1, 1)), _acc((1, d)), _acc((1, d))],
        out_shape=(jax.ShapeDtypeStruct((s, d), F32), jax.ShapeDtypeStruct((s, ff), BF16),
                   jax.ShapeDtypeStruct((s, 2 * ff), BF16), jax.ShapeDtypeStruct((s, d), BF16),
                   jax.ShapeDtypeStruct((s, d), BF16), jax.ShapeDtypeStruct((1, 1), F32), vec, vec),
        compiler_params=_params("arbitrary"),
    )(x2, g3, gf, target, w_gu, w_down)


def _attn_bwd(x1, dx2, o, ycat, qs, probs, g2, w_q, kv, w_o, hosted=None):
    s, d = x1.shape
    tm = min(TM_ATTN, s)
    n = s // tm
    dh = d // HEADS
    scale = dh ** -0.5
    m = kv.shape[0]

    def body(x1_ref, dx2_ref, o_ref, y_ref, q_ref, p_ref, g2_ref, wq_ref, kv_ref, wo_ref,
             dx1_ref, dkv_ref, dg2_ref, gwo_out, gwq_out, gwout_out, dq_ref, gwo_ref, gwq_ref, gwout_ref):
        i = pl.program_id(0)

        @pl.when(i == 0)
        def _():
            for r in (dkv_ref, dg2_ref, gwo_ref, gwq_ref, gwout_ref):
                r[...] = jnp.zeros_like(r)

        xn, xh2, r2 = _rms_fwd(x1_ref[...], g2_ref[...])
        xnb = xn.astype(BF16)
        dx2v = dx2_ref[...]
        dx2b = dx2v.astype(BF16)
        gwo_ref[...] += _dot_tn(o_ref[...], dx2b)
        do = _dot_nt(dx2b, wo_ref[...])
        for hd in range(HEADS):
            qb = q_ref[:, hd * dh:(hd + 1) * dh]
            p = p_ref[:, hd * m:(hd + 1) * m]
            kh = kv_ref[:, hd * dh:(hd + 1) * dh]
            vh = kv_ref[:, d + hd * dh:d + (hd + 1) * dh]
            dob = do[:, hd * dh:(hd + 1) * dh].astype(BF16)
            dp = _dot_nt(dob, vh)
            ds = p * (dp - jnp.sum(dp * p, axis=-1, keepdims=True))
            dsb = (ds * scale).astype(BF16)
            dq_ref[:, hd * dh:(hd + 1) * dh] = _dot(dsb, kh).astype(BF16)
            dkv_ref[:, hd * dh:(hd + 1) * dh] += _dot_tn(dsb, qb)
            dkv_ref[:, d + hd * dh:d + (hd + 1) * dh] += _dot_tn(p.astype(BF16), dob)
        dqb = dq_ref[...]
        gwq_ref[...] += _dot_tn(xnb, dqb)
        dxn = _dot_nt(dqb, wq_ref[...])
        dxr, dg2 = _rms_bwd(dxn, xh2, r2, g2_ref[...])
        dg2_ref[...] += dg2
        dx1 = dx2v + dxr
        dx1_ref[...] = dx1
        gwout_ref[...] += _dot_tn(y_ref[...], dx1.astype(BF16))

        @pl.when(i == n - 1)
        def _():
            for acc, out in ((gwo_ref, gwo_out), (gwq_ref, gwq_out), (gwout_ref, gwout_out)):
                out[...] = acc[...].astype(BF16)

    sq = jax.ShapeDtypeStruct((d, d), BF16)
    return _host_call(
        body, "attn_bwd", (n,),
        [_rows(tm, d), _rows(tm, d), _rows(tm, d), _rows(tm, d), _rows(tm, d), _rows(tm, HEADS * m),
         _full(g2.shape), _full(w_q.shape), _full(kv.shape), _full(w_o.shape)],
        [_rows(tm, d), _acc((m, 2 * d)), _acc((1, d)), _acc((d, d)), _acc((d, d)), _acc((d, d))],
        (jax.ShapeDtypeStruct((s, d), F32), jax.ShapeDtypeStruct((m, 2 * d), F32),
         jax.ShapeDtypeStruct((1, d), F32), sq, sq, sq),
        [pltpu.VMEM((tm, d), BF16)] + [pltpu.VMEM((d, d), F32)] * 3,
        (x1, dx2, o, ycat, qs, probs, g2, w_q, kv, w_o), ("arbitrary",), hosted)


def _kv_bwd(dkv, mem, memn, g_mem, w_kv):
    m, d = mem.shape
    ns = w_kv.shape[2]

    def body(dkv_ref, mem_ref, memn_ref, g_ref, w_ref, gw_ref, dg_ref):
        _, xh, _ = _rms_fwd(mem_ref[...], g_ref[...])
        dmemn = jnp.zeros((m, d), F32)
        for k in range(N_CHIPS):
            dkb = dkv_ref[:, k * ns:(k + 1) * ns].astype(BF16)
            gw_ref[k] = _dot_tn(memn_ref[...], dkb).astype(BF16)
            dmemn = dmemn + _dot_nt(dkb, w_ref[k])
        dg_ref[...] = jnp.sum(dmemn * xh, axis=0, keepdims=True)

    return pl.pallas_call(
        body, name="kv_bwd",
        out_shape=(jax.ShapeDtypeStruct((N_CHIPS, d, ns), BF16), jax.ShapeDtypeStruct((1, d), F32)),
        compiler_params=pltpu.CompilerParams(vmem_limit_bytes=VMEM_LIMIT),
    )(dkv, mem, memn, g_mem, w_kv)


def _mixer_bwd(x, dx1, h, mixed_all, th_all, g1, lng, lnb, wsp, b_t, cw, ga, gb, w_out, w_in, hosted=None):
    s, d = x.shape
    n = s // TM
    nch = TM // CHUNK
    ns = w_in.shape[2]
    nh = N_CHIPS * ns
    aw = d // 2
    hd_w = aw // HEADS

    def rev(cols):
        return pl.BlockSpec((TM, cols), lambda i: (n - 1 - i, 0))

    hprev = pl.BlockSpec((8, nh), lambda i: (jnp.maximum((n - 1 - i) * (TM // 8) - 1, 0), 0))

    def body(x_ref, dx1_ref, h_ref, hp_ref, mix_ref, th_ref, g1_ref, lng_ref, lnb_ref, wsp_ref, bt_ref, cw_ref,
             ga_ref, gb_ref, wout_ref, win_ref,
             dx_ref, dh_ref, dg1_ref, dlng_ref, dlnb_ref, dwsp_ref, dbt_ref, dcw_ref, dga_ref, dgb_ref,
             dvn_ref, dcn_ref):
        i = pl.program_id(0)

        @pl.when(i == 0)
        def _():
            for r in (dg1_ref, dlng_ref, dlnb_ref, dwsp_ref, dbt_ref, dcw_ref, dga_ref, dgb_ref, dcn_ref):
                r[...] = jnp.zeros_like(r)

        dx1v = dx1_ref[...]
        dycat = _dot_nt(dx1v.astype(BF16), wout_ref[...])
        ha = h_ref[:, 0:2 * aw]
        th = th_ref[...]
        a = 0.5 * ha * (1.0 + th)
        u = a[:, :aw]
        lngv = lng_ref[...]
        vn, vhat, rs = _layer_norm_parts(a[:, aw:], lngv, lnb_ref[...])
        vnb = vn.astype(BF16)
        wm = _tril_weights(wsp_ref)
        mixed = mix_ref[...]
        gav = ga_ref[...]
        _, yah, ra = _rms_fwd(u * mixed, gav)
        dya, dga = _rms_bwd(dycat[:, :aw], yah, ra, gav)
        dga_ref[...] += dga
        du = dya * mixed
        dmix = dya * u
        dmb = dmix.astype(BF16)
        tri = lax.broadcasted_iota(jnp.int32, (CHUNK, CHUNK), 0) >= lax.broadcasted_iota(jnp.int32, (CHUNK, CHUNK), 1)
        for hd in range(HEADS):
            dw = jnp.zeros((CHUNK, CHUNK), F32)
            db = jnp.zeros((CHUNK, 1), F32)
            for c in range(nch):
                rows = slice(c * CHUNK, (c + 1) * CHUNK)
                cols = slice(hd * hd_w, (hd + 1) * hd_w)
                dvn_ref[rows, cols] = _dot_tn(wm[hd], dmb[rows, cols])
                dw = dw + _dot_nt(dmb[rows, cols], vnb[rows, cols])
                db = db + jnp.sum(dmix[rows, cols], axis=1, keepdims=True)
            dwsp_ref[hd] += jnp.where(tri, dw, 0.0)
            dbt_ref[:, hd:hd + 1] += db
        dvn = dvn_ref[...]
        dlng_ref[...] += jnp.sum(dvn * vhat, axis=0, keepdims=True)
        dlnb_ref[...] += jnp.sum(dvn, axis=0, keepdims=True)
        dvh = dvn * lngv
        dv = rs * (dvh - jnp.mean(dvh, axis=-1, keepdims=True) - vhat * jnp.mean(dvh * vhat, axis=-1, keepdims=True))
        gprime = 0.5 * (1.0 + th) + 0.5 * ha * (1.0 - th * th) * (GELU_C * (1.0 + 3.0 * GELU_K * (ha * ha)))
        dh_ref[:, 0:2 * aw] = (jnp.concatenate([du, dv], axis=-1) * gprime).astype(BF16)
        g_b = h_ref[:, 2 * aw:3 * aw]
        g_c = h_ref[:, 3 * aw:4 * aw]
        val = h_ref[:, 4 * aw:5 * aw]
        z = g_c * val
        zp = jnp.where(i == n - 1, 0.0, hp_ref[:, 3 * aw:4 * aw] * hp_ref[:, 4 * aw:5 * aw])
        zm1, zm2 = _shift_rows(z, zp)
        cw0, cw1, cw2 = cw_ref[0:1, :], cw_ref[1:2, :], cw_ref[2:3, :]
        conv = cw0 * zm2 + cw1 * zm1 + cw2 * z
        gbv = gb_ref[...]
        _, ybh, rb = _rms_fwd(g_b * conv, gbv)
        dyb, dgb = _rms_bwd(dycat[:, aw:], ybh, rb, gbv)
        dgb_ref[...] += dgb
        dconv = dyb * g_b
        dcw_ref[0:1, :] += jnp.sum(dconv * zm2, axis=0, keepdims=True)
        dcw_ref[1:2, :] += jnp.sum(dconv * zm1, axis=0, keepdims=True)
        dcw_ref[2:3, :] += jnp.sum(dconv * z, axis=0, keepdims=True)
        nxt = dcn_ref[...]
        row = lax.broadcasted_iota(jnp.int32, dconv.shape, 0)
        dcp1 = jnp.where(row == TM - 1, nxt[0:1, :], pltpu.roll(dconv, TM - 1, 0))
        dcp2 = jnp.where(row == TM - 1, nxt[1:2, :],
                         jnp.where(row == TM - 2, nxt[0:1, :], pltpu.roll(dconv, TM - 2, 0)))
        dz = cw2 * dconv + cw1 * dcp1 + cw0 * dcp2
        dcn_ref[...] = dconv[0:8, :]
        dh_ref[:, 2 * aw:3 * aw] = (dyb * conv).astype(BF16)
        dh_ref[:, 3 * aw:4 * aw] = (dz * val).astype(BF16)
        dh_ref[:, 4 * aw:5 * aw] = (dz * g_c).astype(BF16)
        dxn = jnp.zeros((TM, d), F32)
        for k in range(N_CHIPS):
            dxn = dxn + _dot_nt(dh_ref[:, k * ns:(k + 1) * ns], win_ref[k])
        g1v = g1_ref[...]
        _, xh1, r1 = _rms_fwd(x_ref[...], g1v)
        dxr, dg1 = _rms_bwd(dxn, xh1, r1, g1v)
        dg1_ref[...] += dg1
        dx_ref[...] = dx1v + dxr

    ins = (x, dx1, h, h, mixed_all, th_all, g1, lng, lnb, wsp, b_t, cw, ga, gb, w_out, w_in)
    acc_shapes = [(1, d), (1, aw), (1, aw), wsp.shape, (CHUNK, CHUNK), cw.shape, (1, aw), (1, aw)]
    return _host_call(
        body, "mixer_bwd", (n,),
        [rev(d), rev(d), rev(nh), hprev, rev(aw), rev(d)] + [_full(a.shape) for a in ins[6:]],
        [rev(d), rev(nh)] + [_acc(sh) for sh in acc_shapes],
        (jax.ShapeDtypeStruct((s, d), F32), jax.ShapeDtypeStruct((s, nh), BF16))
        + tuple(jax.ShapeDtypeStruct(sh, F32) for sh in acc_shapes),
        [pltpu.VMEM((TM, aw), F32), pltpu.VMEM((8, aw), F32)],
        ins, ("arbitrary",), hosted)


def _weight_grad(a, b, name, tm, tn, col_sharded, hosted=None):
    t, m = a.shape
    n = b.shape[1]

    def body(a_ref, b_ref, o_ref):
        o_ref[...] = _dot_tn(a_ref[...].astype(BF16), b_ref[...].astype(BF16)).astype(BF16)

    if col_sharded:
        ns = n // N_CHIPS
        per = ns // tn
        out_shape = jax.ShapeDtypeStruct((N_CHIPS, m, ns), BF16)
        out_spec = pl.BlockSpec((None, tm, tn), lambda i, j: (j // per, i, j % per))
    else:
        out_shape = jax.ShapeDtypeStruct((m, n), BF16)
        out_spec = pl.BlockSpec((tm, tn), lambda i, j: (i, j))
    (out,), extra = _host_call(
        body, name, (m // tm, n // tn),
        [pl.BlockSpec((t, tm), lambda i, j: (0, i)), pl.BlockSpec((t, tn), lambda i, j: (0, j))],
        [out_spec], (out_shape,), [], (a, b), ("parallel", "parallel"), hosted)
    return (out if col_sharded else out.reshape(N_CHIPS, m // N_CHIPS, n)), extra


def _row_tile(rows, cap=256):
    best = None
    for t in range(16, min(rows, cap) + 1, 16):
        if rows % t == 0:
            best = t
    return best if best is not None else rows


def _adamw_math(w, g, m, v):
    m2 = ADAM_B1 * m + (1.0 - ADAM_B1) * g
    v2 = ADAM_B2 * v + (1.0 - ADAM_B2) * (g * g)
    m_hat = m2 / (1.0 - ADAM_B1 ** ADAM_STEP)
    v_hat = v2 / (1.0 - ADAM_B2 ** ADAM_STEP)
    delta = -ADAM_LR * (m_hat / (jnp.sqrt(v_hat) + ADAM_EPS) + ADAM_WD * w)
    return delta, m2, v2


def _adamw(w, g, m, v, name):
    r, c = w.shape
    tr = _row_tile(r) if r >= 16 else r

    def body(w_ref, g_ref, m_ref, v_ref, d_ref, m2_ref, v2_ref):
        d_ref[...], m2_ref[...], v2_ref[...] = _adamw_math(w_ref[...], g_ref[...], m_ref[...], v_ref[...])

    sh = jax.ShapeDtypeStruct((r, c), F32)
    return pl.pallas_call(
        body, name=name, grid=(r // tr,),
        in_specs=[_rows(tr, c)] * 4, out_specs=[_rows(tr, c)] * 3, out_shape=(sh, sh, sh),
        compiler_params=_params("parallel"),
    )(w, g, m, v)


def _finalize(items, place, name, hosted=None):
    r, c = items[0][2].shape
    tr = _row_tile(r)
    nw = len(items)

    def body(place_ref, *refs):
        ins, outs = refs[:7 * nw], refs[7 * nw:]
        for k in range(nw):
            own_ref, s0_ref, s1_ref, s2_ref, w_ref, m_ref, v_ref = ins[7 * k:7 * k + 7]
            g_ref, d_ref, m2_ref, v2_ref = outs[4 * k:4 * k + 4]
            g = own_ref[...].astype(F32) + s0_ref[...].astype(F32)
            g = (g + s1_ref[...].astype(F32)) + s2_ref[...].astype(F32)
            g_ref[...] = g
            d_ref[...], m2_ref[...], v2_ref[...] = _adamw_math(w_ref[...], g, m_ref[...], v_ref[...])

    def slot(k):
        return pl.BlockSpec((None, tr, c), lambda i, pref: (k, i, 0))

    rows = pl.BlockSpec((tr, c), lambda i, pref: (i, 0))
    sh = jax.ShapeDtypeStruct((r, c), F32)
    one = [pl.BlockSpec((None, tr, c), lambda i, pref: (pref[1], i, 0)), slot(0), slot(1), slot(2), rows, rows, rows]
    args = [a for part, slots, w, m, v in items for a in (part, slots, slots, slots, w, m, v)]
    res, extra = _host_call(body, name, (r // tr,), one * nw, [rows] * (4 * nw), (sh,) * (4 * nw), [], args,
                            ("parallel",), hosted, prefetch=(place,))
    return [res[4 * k:4 * k + 4] for k in range(nw)], extra


def _small_sum_adamw(parts, w, m, v):
    nd, r, c = parts.shape

    def body(p_ref, w_ref, m_ref, v_ref, g_ref, d_ref, m2_ref, v2_ref):
        g = p_ref[0]
        for k in range(1, nd):
            g = g + p_ref[k]
        g_ref[...] = g
        d_ref[...], m2_ref[...], v2_ref[...] = _adamw_math(w_ref[...], g, m_ref[...], v_ref[...])

    sh = jax.ShapeDtypeStruct((r, c), F32)
    return pl.pallas_call(
        body, name="small_sum_adamw", out_shape=(sh, sh, sh, sh),
        compiler_params=pltpu.CompilerParams(vmem_limit_bytes=VMEM_LIMIT),
    )(parts, w, m, v)


def _place():
    x, y, c = lax.axis_index("x"), lax.axis_index("y"), lax.axis_index("c")
    chips = [(1 - x, y), (x, 1 - y), (1 - x, 1 - y)]
    return x, y, c, 2 * x + y, chips


def _remote(src, dst, send_sem, recv_sem, to):
    return pltpu.make_async_remote_copy(src_ref=src, dst_ref=dst, send_sem=send_sem, recv_sem=recv_sem,
                                        device_id=to, device_id_type=MESH)


class _Exchange:
    def __init__(self, ins, out_shapes, sem_shape, start, finish, middle=None, in_place=False, peers=(), late=None):
        self.ins, self.out_shapes, self.sem_shape = tuple(ins), tuple(out_shapes), sem_shape
        self.start, self.finish, self.middle, self.late = start, finish, middle, late
        self.in_place = in_place
        self.peers = frozenset(peers)
        self.collective_id = None

    def with_id(self, collective_id):
        self.collective_id = collective_id
        return self


_FLIPS = {"c": (0, 0, 1), "x": (1, 0, 0), "y": (0, 1, 0), "xy": (1, 1, 0)}


def _handshake(peers):
    x, y, c, _, _ = _place()
    barrier = pltpu.get_barrier_semaphore()
    for name in sorted(peers):
        fx, fy, fc = _FLIPS[name]
        peer = (1 - x if fx else x, 1 - y if fy else y, 1 - c if fc else c)
        pl.semaphore_signal(barrier, inc=1, device_id=peer, device_id_type=MESH)
    pl.semaphore_wait(barrier, len(peers))


def _run_exchange(ex, name, collective_id=None, casts=()):
    n_in, n_out, n_cast = len(ex.ins), len(ex.out_shapes), len(casts)
    step = 64

    def body(*refs):
        ins, srcs = refs[:n_in], refs[n_in:n_in + n_cast]
        outs = refs[n_in + n_cast:n_in + n_cast + n_out]
        dsts = refs[n_in + n_cast + n_out:n_in + 2 * n_cast + n_out]
        rest = refs[n_in + 2 * n_cast + n_out:]
        wide, narrow = rest[:n_cast], rest[n_cast:2 * n_cast]
        send_sems, recv_sems = rest[2 * n_cast], rest[2 * n_cast + 1]
        if collective_id is not None:
            _handshake(ex.peers)
        ex.start(ins, outs, send_sems, recv_sems)
        loads = [pltpu.make_async_copy(srcs[k], wide[k], rest[2 * n_cast + 2].at[k]) for k in range(n_cast)]
        for cp in loads:
            cp.start()
        if ex.middle is not None:
            ex.middle(ins, outs, send_sems, recv_sems)
        if ex.late is not None:
            ex.late(ins, outs, send_sems, recv_sems)
        stores = []
        for k in range(n_cast):
            loads[k].wait()

            def chunk(i, carry, k=k):
                rows = pl.ds(pl.multiple_of(i * step, step), step)
                narrow[k][rows, :] = wide[k][rows, :].astype(BF16)
                return carry
            lax.fori_loop(0, casts[k].shape[0] // step, chunk, 0)
            stores.append(pltpu.make_async_copy(narrow[k], dsts[k], rest[2 * n_cast + 2].at[k]))
            stores[-1].start()
        ex.finish(ins, outs, send_sems, recv_sems)
        for cp in stores:
            cp.wait()

    assert collective_id is None or ex.peers
    sem = pltpu.SemaphoreType.DMA(ex.sem_shape)
    scratch = ([pltpu.VMEM(a.shape, F32) for a in casts] + [pltpu.VMEM(a.shape, BF16) for a in casts] + [sem, sem]
               + ([pltpu.SemaphoreType.DMA((n_cast,))] if n_cast else []))
    return pl.pallas_call(
        body, name=name,
        out_shape=ex.out_shapes + tuple(jax.ShapeDtypeStruct(a.shape, BF16) for a in casts),
        in_specs=[ANY] * (n_in + n_cast), out_specs=[ANY] * (n_out + n_cast),
        input_output_aliases={k: k for k in range(n_in)} if ex.in_place else {}, scratch_shapes=scratch,
        compiler_params=pltpu.CompilerParams(collective_id=collective_id, vmem_limit_bytes=VMEM_LIMIT),
    )(*ex.ins, *casts)


def _host_call(body, name, grid, in_specs, out_specs, out_shape, scratch_shapes, args, semantics, hosted,
               prefetch=()):
    hosted = [] if hosted is None else (list(hosted) if isinstance(hosted, (list, tuple)) else [hosted])
    collective_id = hosted[0].collective_id if hosted else None
    peers = frozenset().union(*[ex.peers for ex in hosted]) if hosted else frozenset()
    assert collective_id is None or all(ex.peers for ex in hosted)
    n_pre, n_in, n_out, n_scr = len(prefetch), len(in_specs), len(out_specs), len(scratch_shapes)
    h_ins = [a for ex in hosted for a in ex.ins]
    h_outs = [s for ex in hosted for s in ex.out_shapes]
    h_in, h_out = len(h_ins), len(h_outs)

    def wrapped(*refs):
        pre, refs = refs[:n_pre], refs[n_pre:]
        a, hi = refs[:n_in], refs[n_in:n_in + h_in]
        o = refs[n_in + h_in:n_in + h_in + n_out]
        ho = refs[n_in + h_in + n_out:n_in + h_in + n_out + h_out]
        scr = refs[n_in + h_in + n_out + h_out:]

        def run(phase):
            i0 = o0 = 0
            for k, ex in enumerate(hosted):
                fn = getattr(ex, phase)
                if fn is not None:
                    fn(hi[i0:i0 + len(ex.ins)], ho[o0:o0 + len(ex.out_shapes)], scr[n_scr + 2 * k],
                       scr[n_scr + 2 * k + 1])
                i0, o0 = i0 + len(ex.ins), o0 + len(ex.out_shapes)

        if hosted:
            first = functools.reduce(jnp.logical_and, [pl.program_id(k) == 0 for k in range(len(grid))])

            @pl.when(first)
            def _():
                if collective_id is not None:
                    _handshake(peers)
                run("start")

        if any(ex.middle is not None for ex in hosted):
            half_way = functools.reduce(jnp.logical_and, [
                pl.program_id(0) == max(1, grid[0] * MIDDLE_STEP_16THS // 16)] + [
                pl.program_id(k) == 0 for k in range(1, len(grid))])

            @pl.when(half_way)
            def _():
                run("middle")

        if any(ex.late is not None for ex in hosted):
            late_step = min(grid[0] - 1, max(2, grid[0] * LATE_STEP_16THS // 16))
            near_end = functools.reduce(jnp.logical_and, [pl.program_id(0) == late_step] + [
                pl.program_id(k) == 0 for k in range(1, len(grid))])

            @pl.when(near_end)
            def _():
                run("late")

        body(*pre, *a, *o, *scr[:n_scr])

        if hosted:
            last = functools.reduce(jnp.logical_and, [pl.program_id(k) == grid[k] - 1 for k in range(len(grid))])

            @pl.when(last)
            def _():
                run("finish")

    sems = [pltpu.SemaphoreType.DMA(ex.sem_shape) for ex in hosted for _ in range(2)]
    aliases, i0, o0 = {}, n_pre + n_in, n_out
    for ex in hosted:
        if ex.in_place:
            aliases.update({i0 + k: o0 + k for k in range(len(ex.ins))})
        i0, o0 = i0 + len(ex.ins), o0 + len(ex.out_shapes)
    all_in, all_out = list(in_specs) + [ANY] * h_in, list(out_specs) + [ANY] * h_out
    all_scr = list(scratch_shapes) + sems
    params = _params(*(["arbitrary"] * len(grid) if hosted else semantics), collective_id=collective_id)
    shapes = tuple(out_shape) + tuple(h_outs)
    if n_pre:
        call = pl.pallas_call(
            wrapped, name=name, out_shape=shapes, input_output_aliases=aliases, compiler_params=params,
            grid_spec=pltpu.PrefetchScalarGridSpec(num_scalar_prefetch=n_pre, grid=grid, in_specs=all_in,
                                                   out_specs=all_out, scratch_shapes=all_scr))
    else:
        call = pl.pallas_call(
            wrapped, name=name, grid=grid, in_specs=all_in, out_specs=all_out, out_shape=shapes,
            scratch_shapes=all_scr, input_output_aliases=aliases, compiler_params=params)
    res = call(*prefetch, *args, *h_ins)
    return res[:n_out], res[n_out:]


def _all_gather(shards, small=()):
    items = tuple(shards) + tuple(small)
    nw = len(shards)

    def place():
        x, y, c, me, _ = _place()
        first = (x + (1 - c) * (1 - 2 * x), y + c * (1 - 2 * y))
        second = (x + c * (1 - 2 * x), y + (1 - c) * (1 - 2 * y))
        diag = (1 - x, 1 - y)
        return x, y, c, me, (first, second, diag)

    def halves(w, c):
        rh = items[w].shape[0] // 2
        return pl.ds(c * rh, rh), pl.ds((1 - c) * rh, rh)

    def start(ins, outs, ss, rs):
        x, y, c, me, chips = place()
        for w in range(len(items)):
            _remote(ins[w], outs[w].at[me], ss.at[w, 6], rs.at[w, 6], (x, y, 1 - c)).start()
            if w < nw:
                mine, _ = halves(w, c)
                _remote(ins[w].at[mine], outs[w].at[me, mine], ss.at[w, 0], rs.at[w, 0], (*chips[0], c)).start()
            else:
                for k in range(3):
                    _remote(ins[w], outs[w].at[me], ss.at[w, k], rs.at[w, k], (*chips[k], c)).start()

    def onward(outs, ss, rs, w, k, x, y, c, chips):
        mine, _ = halves(w, c)
        pk = 2 * chips[k][0] + chips[k][1]
        got = outs[w].at[pk, mine]
        src = chips[1] if k == 2 else chips[k]
        _remote(got, got, ss.at[w, k], rs.at[w, k], (*src, c)).wait_recv()
        if k == 0:
            _remote(got, got, ss.at[w, 2], rs.at[w, 2], (*chips[1], c)).start()
        _remote(got, got, ss.at[w, 3 + k], rs.at[w, 3 + k], (x, y, 1 - c)).start()

    def middle(ins, outs, ss, rs):
        x, y, c, me, chips = place()
        for w in range(nw):
            mine, _ = halves(w, c)
            _remote(ins[w].at[mine], outs[w].at[me, mine], ss.at[w, 1], rs.at[w, 1], (*chips[1], c)).start()
        for w in range(nw):
            onward(outs, ss, rs, w, 0, x, y, c, chips)

    def late(ins, outs, ss, rs):
        x, y, c, _, chips = place()
        for w in range(nw):
            onward(outs, ss, rs, w, 1, x, y, c, chips)

    def finish(ins, outs, ss, rs):
        x, y, c, me, chips = place()
        sib = (x, y, 1 - c)
        for w in range(nw):
            onward(outs, ss, rs, w, 2, x, y, c, chips)
        for w in range(len(items)):
            if w < nw:
                mine, theirs = halves(w, c)
                for k, chip in ((3, chips[1]), (4, chips[0]), (5, chips[2])):
                    oth = outs[w].at[2 * chip[0] + chip[1], theirs]
                    _remote(oth, oth, ss.at[w, k], rs.at[w, k], sib).wait_recv()
                own = ins[w].at[mine]
                for k in range(6):
                    _remote(own, own, ss.at[w, k], rs.at[w, k], sib).wait_send()
            else:
                for k in range(3):
                    got = outs[w].at[2 * chips[k][0] + chips[k][1]]
                    _remote(got, got, ss.at[w, k], rs.at[w, k], (*chips[k], c)).wait_recv()
                    _remote(ins[w], ins[w], ss.at[w, k], rs.at[w, k], sib).wait_send()
            _remote(ins[w], outs[w].at[me], ss.at[w, 6], rs.at[w, 6], sib).wait()

    out_shapes = tuple(jax.ShapeDtypeStruct((N_CHIPS,) + a.shape, a.dtype) for a in items)
    return _Exchange(items, out_shapes, (len(items), 7), start, finish, middle if nw else None,
                     peers=("c", "x", "y", "xy") if small else ("c", "x", "y"), late=late if nw else None)


def _chip_reduce(grads, name, collective_id):
    nw = len(grads)
    step = 64

    def body(*refs):
        ins, outs = refs[:nw], refs[nw:2 * nw]
        own, got = refs[2 * nw:3 * nw], refs[3 * nw:4 * nw]
        send_sems, recv_sems, local_sems = refs[4 * nw:]
        x, y, c, _, _ = _place()
        barrier = pltpu.get_barrier_semaphore()
        pl.semaphore_signal(barrier, inc=1, device_id=(x, y, 1 - c), device_id_type=MESH)
        pl.semaphore_wait(barrier, 1)
        moves = []
        for w in range(nw):
            nb, rh = grads[w].shape[0], grads[w].shape[1] // 2
            for k in range(nb):
                away = _remote(ins[w].at[k, pl.ds((1 - c) * rh, rh), :], got[w].at[k], send_sems.at[w, k],
                               recv_sems.at[w, k], (x, y, 1 - c))
                mine = pltpu.make_async_copy(ins[w].at[k, pl.ds(c * rh, rh), :], own[w].at[k], local_sems.at[w, k])
                away.start()
                mine.start()
                moves.append((w, k, away, mine))
        back = []
        for w, k, away, mine in moves:
            rh = own[w].shape[1]
            mine.wait()
            away.wait()

            def add(i, carry, w=w, k=k):
                rows = pl.ds(pl.multiple_of(i * step, step), step)
                own[w][k, rows, :] = (own[w][k, rows, :].astype(F32) + got[w][k, rows, :].astype(F32)).astype(BF16)
                return carry
            lax.fori_loop(0, rh // step, add, 0)
            tail = rh % step
            if tail:
                rows = slice(rh - tail, rh)
                own[w][k, rows, :] = (own[w][k, rows, :].astype(F32) + got[w][k, rows, :].astype(F32)).astype(BF16)
            wb = pltpu.make_async_copy(own[w].at[k], outs[w].at[k, pl.ds(c * rh, rh), :], local_sems.at[w, k])
            wb.start()
            back.append(wb)
        for wb in back:
            wb.wait()

    halves = [pltpu.VMEM((g.shape[0], g.shape[1] // 2, g.shape[2]), BF16) for g in grads]
    sem = pltpu.SemaphoreType.DMA((nw, N_CHIPS))
    return pl.pallas_call(
        body, name=name, out_shape=tuple(jax.ShapeDtypeStruct(g.shape, BF16) for g in grads),
        in_specs=[ANY] * nw, out_specs=[ANY] * nw, scratch_shapes=halves + halves + [sem, sem, sem],
        compiler_params=pltpu.CompilerParams(vmem_limit_bytes=VMEM_LIMIT, collective_id=collective_id),
    )(*grads)


def _scatter_partials(parts):
    nw = len(parts)

    def copies(ins, outs, ss, rs):
        _, _, c, _, chips = _place()
        res = []
        for r, (px, py) in enumerate(chips):
            for w in range(nw):
                rh = parts[w].shape[1] // 2
                rows = pl.ds(c * rh, rh)
                res.append(_remote(ins[w].at[2 * px + py, rows], outs[w].at[r, rows], ss.at[w, r], rs.at[w, r],
                                   (px, py, c)))
        return res

    def start(ins, outs, ss, rs):
        for cp in copies(ins, outs, ss, rs):
            cp.start()

    def finish(ins, outs, ss, rs):
        for cp in copies(ins, outs, ss, rs):
            cp.wait()

    out_shapes = tuple(jax.ShapeDtypeStruct((3,) + p.shape[1:], p.dtype) for p in parts)
    return _Exchange(parts, out_shapes, (nw, 3), start, finish, peers=("x", "y", "xy"))


def _join_partials(parts, slots):
    nw = len(parts)

    def copies(outs, ss, rs, mine):
        x, y, c, me, _ = _place()
        res = []
        for w in range(nw):
            rh = parts[w].shape[1] // 2
            rows = pl.ds((c if mine else 1 - c) * rh, rh)
            own = outs[w].at[me, rows]
            got = outs[nw + w].at[:, rows, :]
            res.append(_remote(own, own, ss.at[w, 0], rs.at[w, 0], (x, y, 1 - c)))
            res.append(_remote(got, got, ss.at[w, 1], rs.at[w, 1], (x, y, 1 - c)))
        return res

    def start(ins, outs, ss, rs):
        for cp in copies(outs, ss, rs, True):
            cp.start()

    def finish(ins, outs, ss, rs):
        for cp in copies(outs, ss, rs, True):
            cp.wait_send()
        for cp in copies(outs, ss, rs, False):
            cp.wait_recv()

    arrays = tuple(parts) + tuple(slots)
    return _Exchange(arrays, tuple(jax.ShapeDtypeStruct(a.shape, a.dtype) for a in arrays), (nw, 2), start, finish,
                     in_place=True, peers=("c",))


def _gather_small(slab):
    def copies(ins, outs, ss, rs):
        x, y, c, _, _ = _place()
        me = 4 * x + 2 * y + c
        out, arrivals = [], []
        for k in range(1, 8):
            px = 1 - x if k & 4 else x
            py = 1 - y if k & 2 else y
            pc = 1 - c if k & 1 else c
            out.append(_remote(ins[0], outs[0].at[me], ss.at[k - 1], rs.at[k - 1], (px, py, pc)))
            theirs = outs[0].at[4 * px + 2 * py + pc]
            arrivals.append((theirs, k - 1, (px, py, pc)))
        return pltpu.make_async_copy(ins[0], outs[0].at[me], ss.at[7]), out, arrivals

    def start(ins, outs, ss, rs):
        own, out, _ = copies(ins, outs, ss, rs)
        own.start()
        for cp in out:
            cp.start()

    def finish(ins, outs, ss, rs):
        own, out, arrivals = copies(ins, outs, ss, rs)
        for cp in out:
            cp.wait_send()
        for theirs, k, peer in arrivals:
            _remote(theirs, theirs, ss.at[k], rs.at[k], peer).wait_recv()
        own.wait()

    return _Exchange((slab,), (jax.ShapeDtypeStruct((8,) + slab.shape, slab.dtype),), (8,), start, finish)


_SMALL_VECS = ("ln_mix_g", "ln_attn_g", "ln_mem_g", "ln_ffn_g", "ln_final_g")


def _pack_small(p, extra, conv):
    d = p["ln_mix_g"].shape[-1]
    top = [p[k].reshape(1, d) for k in _SMALL_VECS]
    top.append(jnp.concatenate([p["sgu_ln_g"].reshape(-1), p["sgu_ln_b"].reshape(-1)]).reshape(1, d))
    top.append(jnp.concatenate([p["grp_norm_a"].reshape(-1), p["grp_norm_b"].reshape(-1)]).reshape(1, d))
    top.append(jnp.concatenate([p["b_spatial"].reshape(-1), extra]).reshape(1, d))
    mid = jnp.zeros((8, d), F32)
    if conv is not None:
        mid = jnp.pad(conv, ((0, 5), (0, d - conv.shape[1])))
    return jnp.concatenate([jnp.concatenate(top, axis=0), mid, p["w_spatial"].reshape(-1, d)], axis=0)


def _unpack_small(slab):
    d = slab.shape[1]
    hw = d // 2
    out = {k: slab[i] for i, k in enumerate(_SMALL_VECS)}
    out["sgu_ln_g"], out["sgu_ln_b"] = slab[5, :hw], slab[5, hw:]
    out["grp_norm_a"], out["grp_norm_b"] = slab[6, :hw], slab[6, hw:]
    out["b_spatial"] = slab[7, :hw].reshape(HEADS, CHUNK)
    out["w_spatial"] = slab[16:].reshape(HEADS, CHUNK, CHUNK)
    return out


_BIG = ("w_in", "w_kv", "w_gate_up", "w_out", "w_q", "w_o", "w_down")
_WEIGHTS = ("ln_mix_g", "w_in", "sgu_ln_g", "sgu_ln_b", "w_spatial", "b_spatial", "conv_w", "grp_norm_a",
            "grp_norm_b", "w_out", "ln_attn_g", "ln_mem_g", "w_q", "w_kv", "w_o", "ln_ffn_g", "w_gate_up",
            "w_down", "ln_final_g")


def _step(p, m_, v_, x, mem, target):
    s, d = x.shape
    hw = d // 2
    row = lambda a: a.reshape(1, -1)
    x_, y_, c_ = lax.axis_index("x"), lax.axis_index("y"), lax.axis_index("c")
    chip = 2 * x_ + y_

    conv8 = jnp.pad(p["conv_w"], ((0, 5), (0, 0)))
    later = ("w_kv", "w_q", "w_o", "w_down", "w_gate_up")
    first = _run_exchange(_all_gather([p["w_in"].astype(BF16), p["w_out"].astype(BF16)], [conv8]),
                          "all_gather_mixer", collective_id=4, casts=[p[k] for k in later])
    (w_in, w_out4, conv4), bf = first[:3], dict(zip(later, first[3:]))
    cw = jnp.transpose(conv4[:, :3, :], (1, 0, 2)).reshape(3, hw)
    b_t = jnp.pad(jnp.transpose(p["b_spatial"]), ((0, 0), (0, CHUNK - HEADS)))
    g1, g2, gm, g3, gf = (row(p[k]) for k in _SMALL_VECS)
    lng, lnb, ga, gb = row(p["sgu_ln_g"]), row(p["sgu_ln_b"]), row(p["grp_norm_a"]), row(p["grp_norm_b"])
    wsp = p["w_spatial"]
    w_out = w_out4.reshape(-1, d)

    (h, x1, ycat, xn1, mixed, th), (w_kv, w_q4, w_o4, w_down4) = _mixer_fwd(
        x, g1, w_in, lng, lnb, wsp, b_t, cw, ga, gb, w_out,
        hosted=_all_gather([bf[k] for k in ("w_kv", "w_q", "w_o", "w_down")]).with_id(5))
    w_q, w_o, w_down = (a.reshape(-1, d) for a in (w_q4, w_o4, w_down4))
    memn, kv = _kv_fwd(mem, gm, w_kv)
    (x2, o, qs, probs), (w_gu,) = _attn_fwd(x1, g2, w_q, kv, w_o,
                                            hosted=_all_gather([bf["w_gate_up"]]).with_id(6))
    dx2, act, dgu, xn3, dx3, loss, dgf, dg3 = _ffn_fwd_bwd(x2, g3, gf, target, w_gu, w_down)

    place = jnp.stack([c_, chip]).astype(jnp.int32)

    def chip_partials(names, grads, tag):
        return list(_chip_reduce(grads, "chip_reduce_" + tag, ("down", "ffn", "attn", "mixer").index(tag)))

    names_d = ("w_down",)
    parts_d = chip_partials(names_d, (_weight_grad(act, dx3, "grad_w_down", 1408, 512, False)[0],), "down")
    g_gu, slots_d = _weight_grad(xn3, dgu, "grad_w_gate_up", 512, 1408, True,
                                 hosted=_scatter_partials(parts_d).with_id(7))
    names_a = ("w_gate_up",)
    parts_a = chip_partials(names_a, (g_gu,), "ffn")
    (dx1, dkv, dg2, g_o, g_q, g_out), slots_a = _attn_bwd(x1, dx2, o, ycat, qs, probs, g2, w_q, kv, w_o,
                                                           hosted=_scatter_partials(parts_a).with_id(8))
    g_kv, dgm = _kv_bwd(dkv, mem, memn, gm, w_kv)
    names_b = ("w_o", "w_out", "w_q", "w_kv")
    shard_major = lambda g: g.reshape(N_CHIPS, -1, d)
    parts_b = chip_partials(names_b, (shard_major(g_o), shard_major(g_out), shard_major(g_q), g_kv), "attn")
    names_da = names_d + names_a
    (dx, dh, dg1, dlng, dlnb, dwsp, dbt, dcw, dga, dgb), extra = _mixer_bwd(
        x, dx1, h, mixed, th, g1, lng, lnb, wsp, b_t, cw, ga, gb, w_out, w_in,
        hosted=[_scatter_partials(parts_b).with_id(9), _join_partials(parts_d + parts_a, slots_d + slots_a)])
    slots_b, joined = extra[:len(names_b)], extra[len(names_b):]
    whole = dict(zip(names_da, zip(joined[:len(names_da)], joined[len(names_da):])))
    small = {"ln_mix_g": dg1, "ln_attn_g": dg2, "ln_mem_g": dgm, "ln_ffn_g": dg3, "ln_final_g": dgf,
             "sgu_ln_g": dlng, "sgu_ln_b": dlnb, "grp_norm_a": dga, "grp_norm_b": dgb,
             "b_spatial": jnp.transpose(dbt[:, :HEADS]), "w_spatial": dwsp}
    loss_vec = jnp.pad(loss.reshape(1), (0, hw - 1))
    g_in, extra = _weight_grad(
        xn1, dh, "grad_w_in", 1024, 640, True,
        hosted=[_gather_small(_pack_small(small, loss_vec, dcw)), _join_partials(parts_b, slots_b)])
    parts = extra[0]
    whole.update(zip(names_b, zip(extra[1:1 + len(names_b)], extra[1 + len(names_b):])))
    (part_in,) = chip_partials(("w_in",), (g_in,), "mixer")
    out_g, out_d, out_m, out_v = {}, {}, {}, {}

    def finalize(ks, tag, hosted=None):
        done, res = _finalize([(whole[k][0], whole[k][1], p[k], m_[k], v_[k]) for k in ks], place,
                              "finalize_" + tag, hosted)
        for k, (g, dl, nm, nv) in zip(ks, done):
            out_g[k], out_d[k], out_m[k], out_v[k] = g, dl, nm, nv
        return res

    (slots_in,) = finalize(("w_down",), "w_down", _scatter_partials([part_in]).with_id(10))
    finalize(("w_o", "w_out", "w_q"), "attn")
    for k in ("w_gate_up", "w_kv"):
        finalize((k,), k)
    whole["w_in"] = _run_exchange(_join_partials([part_in], [slots_in]), "rs_join_mixer", collective_id=11)
    finalize(("w_in",), "w_in")

    zeros = jnp.zeros((hw,), F32)
    sg, sd, sm, sv = _small_sum_adamw(parts, _pack_small(p, zeros, None), _pack_small(m_, zeros, None),
                                      _pack_small(v_, zeros, None))
    for tree, slab in zip((out_g, out_d, out_m, out_v), (sg, sd, sm, sv)):
        tree.update(_unpack_small(slab))
    loss_out = sg[7, hw]
    g_conv = lax.dynamic_slice(sg[8:11, :hw], (0, chip * (hw // N_CHIPS)), (3, hw // N_CHIPS))
    out_g["conv_w"] = g_conv
    out_d["conv_w"], out_m["conv_w"], out_v["conv_w"] = _adamw(p["conv_w"], g_conv, m_["conv_w"], v_["conv_w"],
                                                                "adamw_conv_w")
    return loss_out, dx, out_g, out_d, out_m, out_v


def kernel(x, mem, ln_mix_g, w_in, sgu_ln_g, sgu_ln_b, w_spatial, b_spatial, conv_w, grp_norm_a, grp_norm_b, w_out, ln_attn_g, ln_mem_g, w_q, w_kv, w_o, ln_ffn_g, w_gate_up, w_down, ln_final_g, loss_target, m_ln_mix_g, m_w_in, m_sgu_ln_g, m_sgu_ln_b, m_w_spatial, m_b_spatial, m_conv_w, m_grp_norm_a, m_grp_norm_b, m_w_out, m_ln_attn_g, m_ln_mem_g, m_w_q, m_w_kv, m_w_o, m_ln_ffn_g, m_w_gate_up, m_w_down, m_ln_final_g, v_ln_mix_g, v_w_in, v_sgu_ln_g, v_sgu_ln_b, v_w_spatial, v_b_spatial, v_conv_w, v_grp_norm_a, v_grp_norm_b, v_w_out, v_ln_attn_g, v_ln_mem_g, v_w_q, v_w_kv, v_w_o, v_ln_ffn_g, v_w_gate_up, v_w_down, v_ln_final_g):
    p = dict(ln_mix_g=ln_mix_g, w_in=w_in, sgu_ln_g=sgu_ln_g, sgu_ln_b=sgu_ln_b, w_spatial=w_spatial,
             b_spatial=b_spatial, conv_w=conv_w, grp_norm_a=grp_norm_a, grp_norm_b=grp_norm_b, w_out=w_out,
             ln_attn_g=ln_attn_g, ln_mem_g=ln_mem_g, w_q=w_q, w_kv=w_kv, w_o=w_o, ln_ffn_g=ln_ffn_g,
             w_gate_up=w_gate_up, w_down=w_down, ln_final_g=ln_final_g)
    m_ = dict(ln_mix_g=m_ln_mix_g, w_in=m_w_in, sgu_ln_g=m_sgu_ln_g, sgu_ln_b=m_sgu_ln_b, w_spatial=m_w_spatial,
              b_spatial=m_b_spatial, conv_w=m_conv_w, grp_norm_a=m_grp_norm_a, grp_norm_b=m_grp_norm_b,
              w_out=m_w_out, ln_attn_g=m_ln_attn_g, ln_mem_g=m_ln_mem_g, w_q=m_w_q, w_kv=m_w_kv, w_o=m_w_o,
              ln_ffn_g=m_ln_ffn_g, w_gate_up=m_w_gate_up, w_down=m_w_down, ln_final_g=m_ln_final_g)
    v_ = dict(ln_mix_g=v_ln_mix_g, w_in=v_w_in, sgu_ln_g=v_sgu_ln_g, sgu_ln_b=v_sgu_ln_b, w_spatial=v_w_spatial,
              b_spatial=v_b_spatial, conv_w=v_conv_w, grp_norm_a=v_grp_norm_a, grp_norm_b=v_grp_norm_b,
              w_out=v_w_out, ln_attn_g=v_ln_attn_g, ln_mem_g=v_ln_mem_g, w_q=v_w_q, w_kv=v_w_kv, w_o=v_w_o,
              ln_ffn_g=v_ln_ffn_g, w_gate_up=v_w_gate_up, w_down=v_w_down, ln_final_g=v_ln_final_g)
    s, d = x.shape[-2], x.shape[-1]
    loss, dx, g, dl, nm, nv = _step(p, m_, v_, x.reshape(s, d), mem.reshape(-1, d), loss_target.reshape(s, d))
    outs = [loss, dx.reshape(x.shape)]
    for tree in (g, dl, nm, nv):
        outs += [tree[k].reshape(p[k].shape) for k in _WEIGHTS]
    return tuple(outs)
```

```python
import functools
import math

import jax
import jax.numpy as jnp
from jax import lax
from jax.experimental import pallas as pl
from jax.experimental.pallas import tpu as pltpu

F32 = jnp.float32
BF16 = jnp.bfloat16
EPS = 1e-6
CHUNK = 128
HEADS = 4
N_CHIPS = 4
TM = 512
TM_ATTN = 512
TM_FFN = 256
ADAM_LR, ADAM_B1, ADAM_B2, ADAM_EPS, ADAM_WD, ADAM_STEP = 0.001, 0.9, 0.999, 1e-08, 0.01, 10
GELU_C = math.sqrt(2.0 / math.pi)
GELU_K = 0.044715
SMALL_ROWS = 80
VMEM_LIMIT = 56 * 1024 * 1024
MIDDLE_STEP_16THS = 7
MESH = pl.DeviceIdType.MESH
ANY = pl.BlockSpec(memory_space=pl.ANY)


def _params(*sem, collective_id=None):
    return pltpu.CompilerParams(dimension_semantics=sem, vmem_limit_bytes=VMEM_LIMIT, collective_id=collective_id)


def _dot(a, b):
    return jnp.dot(a, b, preferred_element_type=F32)


def _dot_nt(a, b):
    return lax.dot_general(a, b, (((1,), (1,)), ((), ())), preferred_element_type=F32)


def _dot_tn(a, b):
    return lax.dot_general(a, b, (((0,), (0,)), ((), ())), preferred_element_type=F32)


def _rms_fwd(x, g):
    r = lax.rsqrt(jnp.mean(x * x, axis=-1, keepdims=True) + EPS)
    xh = x * r
    return xh * g, xh, r


def _rms_bwd(dy, xh, r, g):
    dxh = dy * g
    dx = r * (dxh - xh * jnp.mean(dxh * xh, axis=-1, keepdims=True))
    return dx, jnp.sum(dy * xh, axis=0, keepdims=True)


def _full(shape):
    nd = len(shape)
    return pl.BlockSpec(shape, lambda *_: (0,) * nd, pipeline_mode=pl.Buffered(1))


def _acc(shape):
    nd = len(shape)
    return pl.BlockSpec(shape, lambda *_: (0,) * nd)


def _rows(tm, cols):
    return pl.BlockSpec((tm, cols), lambda i: (i, 0))


def _tril_weights(wsp_ref):
    row = lax.broadcasted_iota(jnp.int32, (CHUNK, CHUNK), 0)
    col = lax.broadcasted_iota(jnp.int32, (CHUNK, CHUNK), 1)
    return [jnp.where(row >= col, wsp_ref[hd], 0.0).astype(BF16) for hd in range(HEADS)]


def _shift_rows(z, zp):
    row = lax.broadcasted_iota(jnp.int32, z.shape, 0)
    zm1 = jnp.where(row == 0, zp[7:8, :], pltpu.roll(z, 1, 0))
    zm2 = jnp.where(row == 0, zp[6:7, :], jnp.where(row == 1, zp[7:8, :], pltpu.roll(z, 2, 0)))
    return zm1, zm2


def _gelu_parts(x):
    t = jnp.tanh(GELU_C * (x + GELU_K * (x * x * x)))
    return 0.5 * x * (1.0 + t), t


def _layer_norm_parts(v, g, b):
    mu = jnp.mean(v, axis=-1, keepdims=True)
    vc = v - mu
    rs = lax.rsqrt(jnp.mean(vc * vc, axis=-1, keepdims=True) + EPS)
    vhat = vc * rs
    return vhat * g + b, vhat, rs


def _mixer_fwd(x, g1, w_in, lng, lnb, wsp, b_t, cw, ga, gb, w_out, hosted=None):
    s, d = x.shape
    n = s // TM
    nch = TM // CHUNK
    ns = w_in.shape[2]
    nh = N_CHIPS * ns
    aw = d // 2
    hd_w = aw // HEADS

    def body(x_ref, g1_ref, win_ref, lng_ref, lnb_ref, wsp_ref, bt_ref, cw_ref, ga_ref, gb_ref, wout_ref,
             h_ref, x1_ref, y_ref, xn_ref, mix_ref, th_ref, zp_ref):
        i = pl.program_id(0)

        @pl.when(i == 0)
        def _():
            zp_ref[...] = jnp.zeros_like(zp_ref)

        x = x_ref[...]
        xn, _, _ = _rms_fwd(x, g1_ref[...])
        xnb = xn.astype(BF16)
        xn_ref[...] = xnb
        for k in range(N_CHIPS):
            h_ref[:, k * ns:(k + 1) * ns] = _dot(xnb, win_ref[k])
        a, th = _gelu_parts(h_ref[:, 0:2 * aw])
        th_ref[...] = th
        u = a[:, :aw]
        vn, _, _ = _layer_norm_parts(a[:, aw:], lng_ref[...], lnb_ref[...])
        vnb = vn.astype(BF16)
        wm = _tril_weights(wsp_ref)
        for c in range(nch):
            for hd in range(HEADS):
                blk = vnb[c * CHUNK:(c + 1) * CHUNK, hd * hd_w:(hd + 1) * hd_w]
                mix_ref[c * CHUNK:(c + 1) * CHUNK, hd * hd_w:(hd + 1) * hd_w] = _dot(wm[hd], blk) + bt_ref[:, hd:hd + 1]
        ya, _, _ = _rms_fwd(u * mix_ref[...], ga_ref[...])
        g_b = h_ref[:, 2 * aw:3 * aw]
        z = h_ref[:, 3 * aw:4 * aw] * h_ref[:, 4 * aw:5 * aw]
        zm1, zm2 = _shift_rows(z, zp_ref[...])
        conv = cw_ref[0:1, :] * zm2 + cw_ref[1:2, :] * zm1 + cw_ref[2:3, :] * z
        yb, _, _ = _rms_fwd(g_b * conv, gb_ref[...])
        zp_ref[...] = z[TM - 8:TM, :]
        ycat = jnp.concatenate([ya, yb], axis=-1).astype(BF16)
        y_ref[...] = ycat
        x1_ref[...] = x + _dot(ycat, wout_ref[...])

    return _host_call(
        body, "mixer_fwd", (n,),
        [_rows(TM, d), _full(g1.shape), _full(w_in.shape), _full(lng.shape), _full(lnb.shape),
         _full(wsp.shape), _full(b_t.shape), _full(cw.shape), _full(ga.shape), _full(gb.shape),
         _full(w_out.shape)],
        [_rows(TM, nh), _rows(TM, d), _rows(TM, d), _rows(TM, d), _rows(TM, aw), _rows(TM, d)],
        (jax.ShapeDtypeStruct((s, nh), F32), jax.ShapeDtypeStruct((s, d), F32),
         jax.ShapeDtypeStruct((s, d), BF16), jax.ShapeDtypeStruct((s, d), BF16),
         jax.ShapeDtypeStruct((s, aw), F32), jax.ShapeDtypeStruct((s, d), F32)),
        [pltpu.VMEM((8, aw), F32)],
        (x, g1, w_in, lng, lnb, wsp, b_t, cw, ga, gb, w_out), ("arbitrary",), hosted)


def _attn_fwd(x1, mem, g2, g_mem, w_q, w_kv, w_o, hosted=None):
    s, d = x1.shape
    tm = min(TM_ATTN, s)
    n = s // tm
    dh = d // HEADS
    m = mem.shape[0]
    ns = w_kv.shape[2]
    scale = dh ** -0.5

    def body(x1_ref, mem_ref, g2_ref, gm_ref, wq_ref, wkv_ref, wo_ref, x2_ref, o_ref, q_ref, p_ref, memn_ref, kv_ref):
        @pl.when(pl.program_id(0) == 0)
        def _():
            y, _, _ = _rms_fwd(mem_ref[...], gm_ref[...])
            yb = y.astype(BF16)
            memn_ref[...] = yb
            for k in range(N_CHIPS):
                kv_ref[:, k * ns:(k + 1) * ns] = _dot(yb, wkv_ref[k]).astype(BF16)

        x1v = x1_ref[...]
        xn, _, _ = _rms_fwd(x1v, g2_ref[...])
        q_ref[...] = _dot(xn.astype(BF16), wq_ref[...]).astype(BF16)
        for hd in range(HEADS):
            kh = kv_ref[:, hd * dh:(hd + 1) * dh]
            vh = kv_ref[:, d + hd * dh:d + (hd + 1) * dh]
            sc = _dot_nt(q_ref[:, hd * dh:(hd + 1) * dh], kh) * scale
            e = jnp.exp(sc - jnp.max(sc, axis=-1, keepdims=True))
            p = e / jnp.sum(e, axis=-1, keepdims=True)
            p_ref[:, hd * m:(hd + 1) * m] = p
            o_ref[:, hd * dh:(hd + 1) * dh] = _dot(p.astype(BF16), vh).astype(BF16)
        x2_ref[...] = x1v + _dot(o_ref[...], wo_ref[...])

    return _host_call(
        body, "attn_fwd", (n,),
        [_rows(tm, d), _full(mem.shape), _full(g2.shape), _full(g_mem.shape), _full(w_q.shape), _full(w_kv.shape),
         _full(w_o.shape)],
        [_rows(tm, d), _rows(tm, d), _rows(tm, d), _rows(tm, HEADS * m), _acc((m, d)), _acc((m, 2 * d))],
        (jax.ShapeDtypeStruct((s, d), F32), jax.ShapeDtypeStruct((s, d), BF16), jax.ShapeDtypeStruct((s, d), BF16),
         jax.ShapeDtypeStruct((s, HEADS * m), F32), jax.ShapeDtypeStruct((m, d), BF16),
         jax.ShapeDtypeStruct((m, 2 * d), BF16)),
        [], (x1, mem, g2, g_mem, w_q, w_kv, w_o), ("arbitrary",), hosted)


def _ffn_fwd_bwd(x2, g3, gf, target, w_gu, w_down):
    s, d = x2.shape
    tm = min(TM_FFN, s)
    n = s // tm
    ns = w_gu.shape[2]
    ff = 2 * ns

    def body(x2_ref, g3_ref, gf_ref, t_ref, wgu_ref, wd_ref,
             dx2_ref, act_ref, dgu_ref, xn_ref, dx3_ref, loss_ref, dgf_ref, dg3_ref):
        i = pl.program_id(0)

        @pl.when(i == 0)
        def _():
            loss_ref[...] = jnp.zeros_like(loss_ref)
            dgf_ref[...] = jnp.zeros_like(dgf_ref)
            dg3_ref[...] = jnp.zeros_like(dg3_ref)

        x2v = x2_ref[...]
        xn, xh3, r3 = _rms_fwd(x2v, g3_ref[...])
        xnb = xn.astype(BF16)
        xn_ref[...] = xnb
        x3 = x2v
        saved = []
        for j in range(2):
            g = _dot(xnb, wgu_ref[j])
            u = _dot(xnb, wgu_ref[2 + j])
            sg = 1.0 / (1.0 + jnp.exp(-g))
            sl = g * sg
            actb = (sl * u).astype(BF16)
            act_ref[:, j * ns:(j + 1) * ns] = actb
            x3 = x3 + _dot(actb, wd_ref[j * ns:(j + 1) * ns, :])
            saved.append((u, sl, sg * (1.0 + g * (1.0 - sg))))
        gfv = gf_ref[...]
        y, xhf, rf = _rms_fwd(x3, gfv)
        e = y - t_ref[...]
        loss_ref[...] += 0.5 * jnp.sum(jnp.sum(e * e, axis=-1, keepdims=True), axis=0, keepdims=True) / d
        dx3, dgf = _rms_bwd(e / d, xhf, rf, gfv)
        dgf_ref[...] += dgf
        dx3b = dx3.astype(BF16)
        dx3_ref[...] = dx3b
        dxn = jnp.zeros_like(x2v)
        for j in range(2):
            u, sl, dsl = saved[j]
            dact = _dot_nt(dx3b, wd_ref[j * ns:(j + 1) * ns, :])
            dgb = (dact * u * dsl).astype(BF16)
            dub = (dact * sl).astype(BF16)
            dgu_ref[:, j * ns:(j + 1) * ns] = dgb
            dgu_ref[:, ff + j * ns:ff + (j + 1) * ns] = dub
            dxn = dxn + _dot_nt(dgb, wgu_ref[j]) + _dot_nt(dub, wgu_ref[2 + j])
        dxr, dg3 = _rms_bwd(dxn, xh3, r3, g3_ref[...])
        dg3_ref[...] += dg3
        dx2_ref[...] = dx3 + dxr

    vec = jax.ShapeDtypeStruct((1, d), F32)
    return pl.pallas_call(
        body, name="ffn_fwd_bwd", grid=(n,),
        in_specs=[_rows(tm, d), _full(g3.shape), _full(gf.shape), _rows(tm, d), _full(w_gu.shape),
                  _full(w_down.shape)],
        out_specs=[_rows(tm, d), _rows(tm, ff), _rows(tm, 2 * ff), _rows(tm, d), _rows(tm, d),
                   _acc((1, 1)), _acc((1, d)), _acc((1, d))],
        out_shape=(jax.ShapeDtypeStruct((s, d), F32), jax.ShapeDtypeStruct((s, ff), BF16),
                   jax.ShapeDtypeStruct((s, 2 * ff), BF16), jax.ShapeDtypeStruct((s, d), BF16),
                   jax.ShapeDtypeStruct((s, d), BF16), jax.ShapeDtypeStruct((1, 1), F32), vec, vec),
        compiler_params=_params("arbitrary"),
    )(x2, g3, gf, target, w_gu, w_down)


def _attn_bwd(x1, dx2, o, ycat, qs, probs, g2, w_q, kv, w_o, hosted=None):
    s, d = x1.shape
    tm = min(TM_ATTN, s)
    n = s // tm
    dh = d // HEADS
    scale = dh ** -0.5
    m = kv.shape[0]

    def body(x1_ref, dx2_ref, o_ref, y_ref, q_ref, p_ref, g2_ref, wq_ref, kv_ref, wo_ref,
             dx1_ref, dkv_ref, dg2_ref, gwo_out, gwq_out, gwout_out, dq_ref, gwo_ref, gwq_ref, gwout_ref):
        i = pl.program_id(0)

        @pl.when(i == 0)
        def _():
            for r in (dkv_ref, dg2_ref, gwo_ref, gwq_ref, gwout_ref):
                r[...] = jnp.zeros_like(r)

        xn, xh2, r2 = _rms_fwd(x1_ref[...], g2_ref[...])
        xnb = xn.astype(BF16)
        dx2v = dx2_ref[...]
        dx2b = dx2v.astype(BF16)
        gwo_ref[...] += _dot_tn(o_ref[...], dx2b)
        do = _dot_nt(dx2b, wo_ref[...])
        for hd in range(HEADS):
            qb = q_ref[:, hd * dh:(hd + 1) * dh]
            p = p_ref[:, hd * m:(hd + 1) * m]
            kh = kv_ref[:, hd * dh:(hd + 1) * dh]
            vh = kv_ref[:, d + hd * dh:d + (hd + 1) * dh]
            dob = do[:, hd * dh:(hd + 1) * dh].astype(BF16)
            dp = _dot_nt(dob, vh)
            ds = p * (dp - jnp.sum(dp * p, axis=-1, keepdims=True))
            dsb = (ds * scale).astype(BF16)
            dq_ref[:, hd * dh:(hd + 1) * dh] = _dot(dsb, kh).astype(BF16)
            dkv_ref[:, hd * dh:(hd + 1) * dh] += _dot_tn(dsb, qb)
            dkv_ref[:, d + hd * dh:d + (hd + 1) * dh] += _dot_tn(p.astype(BF16), dob)
        dqb = dq_ref[...]
        gwq_ref[...] += _dot_tn(xnb, dqb)
        dxn = _dot_nt(dqb, wq_ref[...])
        dxr, dg2 = _rms_bwd(dxn, xh2, r2, g2_ref[...])
        dg2_ref[...] += dg2
        dx1 = dx2v + dxr
        dx1_ref[...] = dx1
        gwout_ref[...] += _dot_tn(y_ref[...], dx1.astype(BF16))

        @pl.when(i == n - 1)
        def _():
            for acc, out in ((gwo_ref, gwo_out), (gwq_ref, gwq_out), (gwout_ref, gwout_out)):
                out[...] = acc[...].astype(BF16)

    sq = jax.ShapeDtypeStruct((d, d), BF16)
    return _host_call(
        body, "attn_bwd", (n,),
        [_rows(tm, d), _rows(tm, d), _rows(tm, d), _rows(tm, d), _rows(tm, d), _rows(tm, HEADS * m),
         _full(g2.shape), _full(w_q.shape), _full(kv.shape), _full(w_o.shape)],
        [_rows(tm, d), _acc((m, 2 * d)), _acc((1, d)), _acc((d, d)), _acc((d, d)), _acc((d, d))],
        (jax.ShapeDtypeStruct((s, d), F32), jax.ShapeDtypeStruct((m, 2 * d), F32),
         jax.ShapeDtypeStruct((1, d), F32), sq, sq, sq),
        [pltpu.VMEM((tm, d), BF16)] + [pltpu.VMEM((d, d), F32)] * 3,
        (x1, dx2, o, ycat, qs, probs, g2, w_q, kv, w_o), ("arbitrary",), hosted)


def _kv_bwd(dkv, mem, memn, g_mem, w_kv):
    m, d = mem.shape
    ns = w_kv.shape[2]

    def body(dkv_ref, mem_ref, memn_ref, g_ref, w_ref, gw_ref, dg_ref):
        _, xh, _ = _rms_fwd(mem_ref[...], g_ref[...])
        dmemn = jnp.zeros((m, d), F32)
        for k in range(N_CHIPS):
            dkb = dkv_ref[:, k * ns:(k + 1) * ns].astype(BF16)
            gw_ref[k] = _dot_tn(memn_ref[...], dkb).astype(BF16)
            dmemn = dmemn + _dot_nt(dkb, w_ref[k])
        dg_ref[...] = jnp.sum(dmemn * xh, axis=0, keepdims=True)

    return pl.pallas_call(
        body, name="kv_bwd",
        out_shape=(jax.ShapeDtypeStruct((N_CHIPS, d, ns), BF16), jax.ShapeDtypeStruct((1, d), F32)),
        compiler_params=pltpu.CompilerParams(vmem_limit_bytes=VMEM_LIMIT),
    )(dkv, mem, memn, g_mem, w_kv)


def _mixer_bwd(x, dx1, h, mixed_all, th_all, g1, lng, lnb, wsp, b_t, cw, ga, gb, w_out, w_in, hosted=None):
    s, d = x.shape
    n = s // TM
    nch = TM // CHUNK
    ns = w_in.shape[2]
    nh = N_CHIPS * ns
    aw = d // 2
    hd_w = aw // HEADS

    def rev(cols):
        return pl.BlockSpec((TM, cols), lambda i: (n - 1 - i, 0))

    hprev = pl.BlockSpec((8, nh), lambda i: (jnp.maximum((n - 1 - i) * (TM // 8) - 1, 0), 0))

    def body(x_ref, dx1_ref, h_ref, hp_ref, mix_ref, th_ref, g1_ref, lng_ref, lnb_ref, wsp_ref, bt_ref, cw_ref,
             ga_ref, gb_ref, wout_ref, win_ref,
             dx_ref, dh_ref, dg1_ref, dlng_ref, dlnb_ref, dwsp_ref, dbt_ref, dcw_ref, dga_ref, dgb_ref,
             dvn_ref, dcn_ref):
        i = pl.program_id(0)

        @pl.when(i == 0)
        def _():
            for r in (dg1_ref, dlng_ref, dlnb_ref, dwsp_ref, dbt_ref, dcw_ref, dga_ref, dgb_ref, dcn_ref):
                r[...] = jnp.zeros_like(r)

        dx1v = dx1_ref[...]
        dycat = _dot_nt(dx1v.astype(BF16), wout_ref[...])
        ha = h_ref[:, 0:2 * aw]
        th = th_ref[...]
        a = 0.5 * ha * (1.0 + th)
        u = a[:, :aw]
        lngv = lng_ref[...]
        vn, vhat, rs = _layer_norm_parts(a[:, aw:], lngv, lnb_ref[...])
        vnb = vn.astype(BF16)
        wm = _tril_weights(wsp_ref)
        mixed = mix_ref[...]
        gav = ga_ref[...]
        _, yah, ra = _rms_fwd(u * mixed, gav)
        dya, dga = _rms_bwd(dycat[:, :aw], yah, ra, gav)
        dga_ref[...] += dga
        du = dya * mixed
        dmix = dya * u
        dmb = dmix.astype(BF16)
        tri = lax.broadcasted_iota(jnp.int32, (CHUNK, CHUNK), 0) >= lax.broadcasted_iota(jnp.int32, (CHUNK, CHUNK), 1)
        for hd in range(HEADS):
            dw = jnp.zeros((CHUNK, CHUNK), F32)
            db = jnp.zeros((CHUNK, 1), F32)
            for c in range(nch):
                rows = slice(c * CHUNK, (c + 1) * CHUNK)
                cols = slice(hd * hd_w, (hd + 1) * hd_w)
                dvn_ref[rows, cols] = _dot_tn(wm[hd], dmb[rows, cols])
                dw = dw + _dot_nt(dmb[rows, cols], vnb[rows, cols])
                db = db + jnp.sum(dmix[rows, cols], axis=1, keepdims=True)
            dwsp_ref[hd] += jnp.where(tri, dw, 0.0)
            dbt_ref[:, hd:hd + 1] += db
        dvn = dvn_ref[...]
        dlng_ref[...] += jnp.sum(dvn * vhat, axis=0, keepdims=True)
        dlnb_ref[...] += jnp.sum(dvn, axis=0, keepdims=True)
        dvh = dvn * lngv
        dv = rs * (dvh - jnp.mean(dvh, axis=-1, keepdims=True) - vhat * jnp.mean(dvh * vhat, axis=-1, keepdims=True))
        gprime = 0.5 * (1.0 + th) + 0.5 * ha * (1.0 - th * th) * (GELU_C * (1.0 + 3.0 * GELU_K * (ha * ha)))
        dh_ref[:, 0:2 * aw] = (jnp.concatenate([du, dv], axis=-1) * gprime).astype(BF16)
        g_b = h_ref[:, 2 * aw:3 * aw]
        g_c = h_ref[:, 3 * aw:4 * aw]
        val = h_ref[:, 4 * aw:5 * aw]
        z = g_c * val
        zp = jnp.where(i == n - 1, 0.0, hp_ref[:, 3 * aw:4 * aw] * hp_ref[:, 4 * aw:5 * aw])
        zm1, zm2 = _shift_rows(z, zp)
        cw0, cw1, cw2 = cw_ref[0:1, :], cw_ref[1:2, :], cw_ref[2:3, :]
        conv = cw0 * zm2 + cw1 * zm1 + cw2 * z
        gbv = gb_ref[...]
        _, ybh, rb = _rms_fwd(g_b * conv, gbv)
        dyb, dgb = _rms_bwd(dycat[:, aw:], ybh, rb, gbv)
        dgb_ref[...] += dgb
        dconv = dyb * g_b
        dcw_ref[0:1, :] += jnp.sum(dconv * zm2, axis=0, keepdims=True)
        dcw_ref[1:2, :] += jnp.sum(dconv * zm1, axis=0, keepdims=True)
        dcw_ref[2:3, :] += jnp.sum(dconv * z, axis=0, keepdims=True)
        nxt = dcn_ref[...]
        row = lax.broadcasted_iota(jnp.int32, dconv.shape, 0)
        dcp1 = jnp.where(row == TM - 1, nxt[0:1, :], pltpu.roll(dconv, TM - 1, 0))
        dcp2 = jnp.where(row == TM - 1, nxt[1:2, :],
                         jnp.where(row == TM - 2, nxt[0:1, :], pltpu.roll(dconv, TM - 2, 0)))
        dz = cw2 * dconv + cw1 * dcp1 + cw0 * dcp2
        dcn_ref[...] = dconv[0:8, :]
        dh_ref[:, 2 * aw:3 * aw] = (dyb * conv).astype(BF16)
        dh_ref[:, 3 * aw:4 * aw] = (dz * val).astype(BF16)
        dh_ref[:, 4 * aw:5 * aw] = (dz * g_c).astype(BF16)
        dxn = jnp.zeros((TM, d), F32)
        for k in range(N_CHIPS):
            dxn = dxn + _dot_nt(dh_ref[:, k * ns:(k + 1) * ns], win_ref[k])
        g1v = g1_ref[...]
        _, xh1, r1 = _rms_fwd(x_ref[...], g1v)
        dxr, dg1 = _rms_bwd(dxn, xh1, r1, g1v)
        dg1_ref[...] += dg1
        dx_ref[...] = dx1v + dxr

    ins = (x, dx1, h, h, mixed_all, th_all, g1, lng, lnb, wsp, b_t, cw, ga, gb, w_out, w_in)
    acc_shapes = [(1, d), (1, aw), (1, aw), wsp.shape, (CHUNK, CHUNK), cw.shape, (1, aw), (1, aw)]
    return _host_call(
        body, "mixer_bwd", (n,),
        [rev(d), rev(d), rev(nh), hprev, rev(aw), rev(d)] + [_full(a.shape) for a in ins[6:]],
        [rev(d), rev(nh)] + [_acc(sh) for sh in acc_shapes],
        (jax.ShapeDtypeStruct((s, d), F32), jax.ShapeDtypeStruct((s, nh), BF16))
        + tuple(jax.ShapeDtypeStruct(sh, F32) for sh in acc_shapes),
        [pltpu.VMEM((TM, aw), F32), pltpu.VMEM((8, aw), F32)],
        ins, ("arbitrary",), hosted)


def _weight_grad(a, b, name, tm, tn, col_sharded, hosted=None):
    t, m = a.shape
    n = b.shape[1]

    def body(a_ref, b_ref, o_ref):
        o_ref[...] = _dot_tn(a_ref[...].astype(BF16), b_ref[...].astype(BF16)).astype(BF16)

    if col_sharded:
        ns = n // N_CHIPS
        per = ns // tn
        out_shape = jax.ShapeDtypeStruct((N_CHIPS, m, ns), BF16)
        out_spec = pl.BlockSpec((None, tm, tn), lambda i, j: (j // per, i, j % per))
    else:
        out_shape = jax.ShapeDtypeStruct((m, n), BF16)
        out_spec = pl.BlockSpec((tm, tn), lambda i, j: (i, j))
    (out,), extra = _host_call(
        body, name, (m // tm, n // tn),
        [pl.BlockSpec((t, tm), lambda i, j: (0, i)), pl.BlockSpec((t, tn), lambda i, j: (0, j))],
        [out_spec], (out_shape,), [], (a, b), ("parallel", "parallel"), hosted)
    return (out if col_sharded else out.reshape(N_CHIPS, m // N_CHIPS, n)), extra


def _row_tile(rows, cap=256):
    best = None
    for t in range(16, min(rows, cap) + 1, 16):
        if rows % t == 0:
            best = t
    return best if best is not None else rows


def _adamw_math(w, g, m, v):
    m2 = ADAM_B1 * m + (1.0 - ADAM_B1) * g
    v2 = ADAM_B2 * v + (1.0 - ADAM_B2) * (g * g)
    m_hat = m2 / (1.0 - ADAM_B1 ** ADAM_STEP)
    v_hat = v2 / (1.0 - ADAM_B2 ** ADAM_STEP)
    delta = -ADAM_LR * (m_hat / (jnp.sqrt(v_hat) + ADAM_EPS) + ADAM_WD * w)
    return delta, m2, v2


def _adamw(w, g, m, v, name):
    r, c = w.shape
    tr = _row_tile(r) if r >= 16 else r

    def body(w_ref, g_ref, m_ref, v_ref, d_ref, m2_ref, v2_ref):
        d_ref[...], m2_ref[...], v2_ref[...] = _adamw_math(w_ref[...], g_ref[...], m_ref[...], v_ref[...])

    sh = jax.ShapeDtypeStruct((r, c), F32)
    return pl.pallas_call(
        body, name=name, grid=(r // tr,),
        in_specs=[_rows(tr, c)] * 4, out_specs=[_rows(tr, c)] * 3, out_shape=(sh, sh, sh),
        compiler_params=_params("parallel"),
    )(w, g, m, v)


def _finalize(items, place, name, hosted=None):
    r, c = items[0][2].shape
    tr = _row_tile(r)
    nw = len(items)

    def body(place_ref, *refs):
        ins, outs = refs[:7 * nw], refs[7 * nw:]
        for k in range(nw):
            own_ref, s0_ref, s1_ref, s2_ref, w_ref, m_ref, v_ref = ins[7 * k:7 * k + 7]
            g_ref, d_ref, m2_ref, v2_ref = outs[4 * k:4 * k + 4]
            g = own_ref[...].astype(F32) + s0_ref[...].astype(F32)
            g = (g + s1_ref[...].astype(F32)) + s2_ref[...].astype(F32)
            g_ref[...] = g
            d_ref[...], m2_ref[...], v2_ref[...] = _adamw_math(w_ref[...], g, m_ref[...], v_ref[...])

    def slot(k):
        return pl.BlockSpec((None, tr, c), lambda i, pref: (k, i, 0))

    rows = pl.BlockSpec((tr, c), lambda i, pref: (i, 0))
    sh = jax.ShapeDtypeStruct((r, c), F32)
    one = [pl.BlockSpec((None, tr, c), lambda i, pref: (pref[1], i, 0)), slot(0), slot(1), slot(2), rows, rows, rows]
    args = [a for part, slots, w, m, v in items for a in (part, slots, slots, slots, w, m, v)]
    res, extra = _host_call(body, name, (r // tr,), one * nw, [rows] * (4 * nw), (sh,) * (4 * nw), [], args,
                            ("parallel",), hosted, prefetch=(place,))
    return [res[4 * k:4 * k + 4] for k in range(nw)], extra


def _small_sum_adamw(parts, w, m, v):
    nd, r, c = parts.shape

    def body(p_ref, w_ref, m_ref, v_ref, g_ref, d_ref, m2_ref, v2_ref):
        g = p_ref[0]
        for k in range(1, nd):
            g = g + p_ref[k]
        g_ref[...] = g
        d_ref[...], m2_ref[...], v2_ref[...] = _adamw_math(w_ref[...], g, m_ref[...], v_ref[...])

    sh = jax.ShapeDtypeStruct((r, c), F32)
    return pl.pallas_call(
        body, name="small_sum_adamw", out_shape=(sh, sh, sh, sh),
        compiler_params=pltpu.CompilerParams(vmem_limit_bytes=VMEM_LIMIT),
    )(parts, w, m, v)


def _place():
    x, y, c = lax.axis_index("x"), lax.axis_index("y"), lax.axis_index("c")
    chips = [(1 - x, y), (x, 1 - y), (1 - x, 1 - y)]
    return x, y, c, 2 * x + y, chips


def _remote(src, dst, send_sem, recv_sem, to):
    return pltpu.make_async_remote_copy(src_ref=src, dst_ref=dst, send_sem=send_sem, recv_sem=recv_sem,
                                        device_id=to, device_id_type=MESH)


class _Exchange:
    def __init__(self, ins, out_shapes, sem_shape, start, finish, middle=None, in_place=False, peers=()):
        self.ins, self.out_shapes, self.sem_shape = tuple(ins), tuple(out_shapes), sem_shape
        self.start, self.finish, self.middle = start, finish, middle
        self.in_place = in_place
        self.peers = frozenset(peers)
        self.collective_id = None

    def with_id(self, collective_id):
        self.collective_id = collective_id
        return self


_FLIPS = {"c": (0, 0, 1), "x": (1, 0, 0), "y": (0, 1, 0), "xy": (1, 1, 0)}


def _handshake(peers):
    x, y, c, _, _ = _place()
    barrier = pltpu.get_barrier_semaphore()
    for name in sorted(peers):
        fx, fy, fc = _FLIPS[name]
        peer = (1 - x if fx else x, 1 - y if fy else y, 1 - c if fc else c)
        pl.semaphore_signal(barrier, inc=1, device_id=peer, device_id_type=MESH)
    pl.semaphore_wait(barrier, len(peers))


def _run_exchange(ex, name, collective_id=None, casts=()):
    n_in, n_out, n_cast = len(ex.ins), len(ex.out_shapes), len(casts)
    step = 64

    def body(*refs):
        ins, srcs = refs[:n_in], refs[n_in:n_in + n_cast]
        outs = refs[n_in + n_cast:n_in + n_cast + n_out]
        dsts = refs[n_in + n_cast + n_out:n_in + 2 * n_cast + n_out]
        rest = refs[n_in + 2 * n_cast + n_out:]
        wide, narrow = rest[:n_cast], rest[n_cast:2 * n_cast]
        send_sems, recv_sems = rest[2 * n_cast], rest[2 * n_cast + 1]
        if collective_id is not None:
            _handshake(ex.peers)
        ex.start(ins, outs, send_sems, recv_sems)
        loads = [pltpu.make_async_copy(srcs[k], wide[k], rest[2 * n_cast + 2].at[k]) for k in range(n_cast)]
        for cp in loads:
            cp.start()
        if ex.middle is not None:
            ex.middle(ins, outs, send_sems, recv_sems)
        stores = []
        for k in range(n_cast):
            loads[k].wait()

            def chunk(i, carry, k=k):
                rows = pl.ds(pl.multiple_of(i * step, step), step)
                narrow[k][rows, :] = wide[k][rows, :].astype(BF16)
                return carry
            lax.fori_loop(0, casts[k].shape[0] // step, chunk, 0)
            stores.append(pltpu.make_async_copy(narrow[k], dsts[k], rest[2 * n_cast + 2].at[k]))
            stores[-1].start()
        ex.finish(ins, outs, send_sems, recv_sems)
        for cp in stores:
            cp.wait()

    assert collective_id is None or ex.peers
    sem = pltpu.SemaphoreType.DMA(ex.sem_shape)
    scratch = ([pltpu.VMEM(a.shape, F32) for a in casts] + [pltpu.VMEM(a.shape, BF16) for a in casts] + [sem, sem]
               + ([pltpu.SemaphoreType.DMA((n_cast,))] if n_cast else []))
    return pl.pallas_call(
        body, name=name,
        out_shape=ex.out_shapes + tuple(jax.ShapeDtypeStruct(a.shape, BF16) for a in casts),
        in_specs=[ANY] * (n_in + n_cast), out_specs=[ANY] * (n_out + n_cast),
        input_output_aliases={k: k for k in range(n_in)} if ex.in_place else {}, scratch_shapes=scratch,
        compiler_params=pltpu.CompilerParams(collective_id=collective_id, vmem_limit_bytes=VMEM_LIMIT),
    )(*ex.ins, *casts)


def _host_call(body, name, grid, in_specs, out_specs, out_shape, scratch_shapes, args, semantics, hosted,
               prefetch=()):
    hosted = [] if hosted is None else (list(hosted) if isinstance(hosted, (list, tuple)) else [hosted])
    collective_id = hosted[0].collective_id if hosted else None
    peers = frozenset().union(*[ex.peers for ex in hosted]) if hosted else frozenset()
    assert collective_id is None or all(ex.peers for ex in hosted)
    n_pre, n_in, n_out, n_scr = len(prefetch), len(in_specs), len(out_specs), len(scratch_shapes)
    h_ins = [a for ex in hosted for a in ex.ins]
    h_outs = [s for ex in hosted for s in ex.out_shapes]
    h_in, h_out = len(h_ins), len(h_outs)

    def wrapped(*refs):
        pre, refs = refs[:n_pre], refs[n_pre:]
        a, hi = refs[:n_in], refs[n_in:n_in + h_in]
        o = refs[n_in + h_in:n_in + h_in + n_out]
        ho = refs[n_in + h_in + n_out:n_in + h_in + n_out + h_out]
        scr = refs[n_in + h_in + n_out + h_out:]

        def run(phase):
            i0 = o0 = 0
            for k, ex in enumerate(hosted):
                fn = getattr(ex, phase)
                if fn is not None:
                    fn(hi[i0:i0 + len(ex.ins)], ho[o0:o0 + len(ex.out_shapes)], scr[n_scr + 2 * k],
                       scr[n_scr + 2 * k + 1])
                i0, o0 = i0 + len(ex.ins), o0 + len(ex.out_shapes)

        if hosted:
            first = functools.reduce(jnp.logical_and, [pl.program_id(k) == 0 for k in range(len(grid))])

            @pl.when(first)
            def _():
                if collective_id is not None:
                    _handshake(peers)
                run("start")

        if any(ex.middle is not None for ex in hosted):
            half_way = functools.reduce(jnp.logical_and, [
                pl.program_id(0) == max(1, grid[0] * MIDDLE_STEP_16THS // 16)] + [
                pl.program_id(k) == 0 for k in range(1, len(grid))])

            @pl.when(half_way)
            def _():
                run("middle")

        body(*pre, *a, *o, *scr[:n_scr])

        if hosted:
            last = functools.reduce(jnp.logical_and, [pl.program_id(k) == grid[k] - 1 for k in range(len(grid))])

            @pl.when(last)
            def _():
                run("finish")

    sems = [pltpu.SemaphoreType.DMA(ex.sem_shape) for ex in hosted for _ in range(2)]
    aliases, i0, o0 = {}, n_pre + n_in, n_out
    for ex in hosted:
        if ex.in_place:
            aliases.update({i0 + k: o0 + k for k in range(len(ex.ins))})
        i0, o0 = i0 + len(ex.ins), o0 + len(ex.out_shapes)
    all_in, all_out = list(in_specs) + [ANY] * h_in, list(out_specs) + [ANY] * h_out
    all_scr = list(scratch_shapes) + sems
    params = _params(*(["arbitrary"] * len(grid) if hosted else semantics), collective_id=collective_id)
    shapes = tuple(out_shape) + tuple(h_outs)
    if n_pre:
        call = pl.pallas_call(
            wrapped, name=name, out_shape=shapes, input_output_aliases=aliases, compiler_params=params,
            grid_spec=pltpu.PrefetchScalarGridSpec(num_scalar_prefetch=n_pre, grid=grid, in_specs=all_in,
                                                   out_specs=all_out, scratch_shapes=all_scr))
    else:
        call = pl.pallas_call(
            wrapped, name=name, grid=grid, in_specs=all_in, out_specs=all_out, out_shape=shapes,
            scratch_shapes=all_scr, input_output_aliases=aliases, compiler_params=params)
    res = call(*prefetch, *args, *h_ins)
    return res[:n_out], res[n_out:]


def _all_gather(shards, small=()):
    items = tuple(shards) + tuple(small)
    nw = len(shards)

    def place():
        x, y, c, me, _ = _place()
        first = (x + (1 - c) * (1 - 2 * x), y + c * (1 - 2 * y))
        second = (x + c * (1 - 2 * x), y + (1 - c) * (1 - 2 * y))
        diag = (1 - x, 1 - y)
        return x, y, c, me, (first, second, diag)

    def halves(w, c):
        rh = items[w].shape[0] // 2
        return pl.ds(c * rh, rh), pl.ds((1 - c) * rh, rh)

    def start(ins, outs, ss, rs):
        x, y, c, me, chips = place()
        for w in range(len(items)):
            _remote(ins[w], outs[w].at[me], ss.at[w, 6], rs.at[w, 6], (x, y, 1 - c)).start()
            if w < nw:
                mine, _ = halves(w, c)
                _remote(ins[w].at[mine], outs[w].at[me, mine], ss.at[w, 0], rs.at[w, 0], (*chips[0], c)).start()
            else:
                for k in range(3):
                    _remote(ins[w], outs[w].at[me], ss.at[w, k], rs.at[w, k], (*chips[k], c)).start()

    def onward(outs, ss, rs, w, k, x, y, c, chips):
        mine, _ = halves(w, c)
        pk = 2 * chips[k][0] + chips[k][1]
        got = outs[w].at[pk, mine]
        src = chips[1] if k == 2 else chips[k]
        _remote(got, got, ss.at[w, k], rs.at[w, k], (*src, c)).wait_recv()
        if k == 0:
            _remote(got, got, ss.at[w, 2], rs.at[w, 2], (*chips[1], c)).start()
        _remote(got, got, ss.at[w, 3 + k], rs.at[w, 3 + k], (x, y, 1 - c)).start()

    def middle(ins, outs, ss, rs):
        x, y, c, me, chips = place()
        for w in range(nw):
            mine, _ = halves(w, c)
            _remote(ins[w].at[mine], outs[w].at[me, mine], ss.at[w, 1], rs.at[w, 1], (*chips[1], c)).start()
        for w in range(nw):
            onward(outs, ss, rs, w, 0, x, y, c, chips)

    def finish(ins, outs, ss, rs):
        x, y, c, me, chips = place()
        sib = (x, y, 1 - c)
        for k in (1, 2):
            for w in range(nw):
                onward(outs, ss, rs, w, k, x, y, c, chips)
        for w in range(len(items)):
            if w < nw:
                mine, theirs = halves(w, c)
                for k, chip in ((3, chips[1]), (4, chips[0]), (5, chips[2])):
                    oth = outs[w].at[2 * chip[0] + chip[1], theirs]
                    _remote(oth, oth, ss.at[w, k], rs.at[w, k], sib).wait_recv()
                own = ins[w].at[mine]
                for k in range(6):
                    _remote(own, own, ss.at[w, k], rs.at[w, k], sib).wait_send()
            else:
                for k in range(3):
                    got = outs[w].at[2 * chips[k][0] + chips[k][1]]
                    _remote(got, got, ss.at[w, k], rs.at[w, k], (*chips[k], c)).wait_recv()
                    _remote(ins[w], ins[w], ss.at[w, k], rs.at[w, k], sib).wait_send()
            _remote(ins[w], outs[w].at[me], ss.at[w, 6], rs.at[w, 6], sib).wait()

    out_shapes = tuple(jax.ShapeDtypeStruct((N_CHIPS,) + a.shape, a.dtype) for a in items)
    return _Exchange(items, out_shapes, (len(items), 7), start, finish, middle if nw else None,
                     peers=("c", "x", "y", "xy") if small else ("c", "x", "y"))


def _chip_reduce(grads, name, collective_id):
    nw = len(grads)
    step = 64

    def body(*refs):
        ins, outs = refs[:nw], refs[nw:2 * nw]
        own, got = refs[2 * nw:3 * nw], refs[3 * nw:4 * nw]
        send_sems, recv_sems, local_sems = refs[4 * nw:]
        x, y, c, _, _ = _place()
        barrier = pltpu.get_barrier_semaphore()
        pl.semaphore_signal(barrier, inc=1, device_id=(x, y, 1 - c), device_id_type=MESH)
        pl.semaphore_wait(barrier, 1)
        moves = []
        for w in range(nw):
            nb, rh = grads[w].shape[0], grads[w].shape[1] // 2
            for k in range(nb):
                away = _remote(ins[w].at[k, pl.ds((1 - c) * rh, rh), :], got[w].at[k], send_sems.at[w, k],
                               recv_sems.at[w, k], (x, y, 1 - c))
                mine = pltpu.make_async_copy(ins[w].at[k, pl.ds(c * rh, rh), :], own[w].at[k], local_sems.at[w, k])
                away.start()
                mine.start()
                moves.append((w, k, away, mine))
        back = []
        for w, k, away, mine in moves:
            rh = own[w].shape[1]
            mine.wait()
            away.wait()

            def add(i, carry, w=w, k=k):
                rows = pl.ds(pl.multiple_of(i * step, step), step)
                own[w][k, rows, :] = (own[w][k, rows, :].astype(F32) + got[w][k, rows, :].astype(F32)).astype(BF16)
                return carry
            lax.fori_loop(0, rh // step, add, 0)
            tail = rh % step
            if tail:
                rows = slice(rh - tail, rh)
                own[w][k, rows, :] = (own[w][k, rows, :].astype(F32) + got[w][k, rows, :].astype(F32)).astype(BF16)
            wb = pltpu.make_async_copy(own[w].at[k], outs[w].at[k, pl.ds(c * rh, rh), :], local_sems.at[w, k])
            wb.start()
            back.append(wb)
        for wb in back:
            wb.wait()

    halves = [pltpu.VMEM((g.shape[0], g.shape[1] // 2, g.shape[2]), BF16) for g in grads]
    sem = pltpu.SemaphoreType.DMA((nw, N_CHIPS))
    return pl.pallas_call(
        body, name=name, out_shape=tuple(jax.ShapeDtypeStruct(g.shape, BF16) for g in grads),
        in_specs=[ANY] * nw, out_specs=[ANY] * nw, scratch_shapes=halves + halves + [sem, sem, sem],
        compiler_params=pltpu.CompilerParams(vmem_limit_bytes=VMEM_LIMIT, collective_id=collective_id),
    )(*grads)


def _scatter_partials(parts):
    nw = len(parts)

    def copies(ins, outs, ss, rs):
        _, _, c, _, chips = _place()
        res = []
        for r, (px, py) in enumerate(chips):
            for w in range(nw):
                rh = parts[w].shape[1] // 2
                rows = pl.ds(c * rh, rh)
                res.append(_remote(ins[w].at[2 * px + py, rows], outs[w].at[r, rows], ss.at[w, r], rs.at[w, r],
                                   (px, py, c)))
        return res

    def start(ins, outs, ss, rs):
        for cp in copies(ins, outs, ss, rs):
            cp.start()

    def finish(ins, outs, ss, rs):
        for cp in copies(ins, outs, ss, rs):
            cp.wait()

    out_shapes = tuple(jax.ShapeDtypeStruct((3,) + p.shape[1:], p.dtype) for p in parts)
    return _Exchange(parts, out_shapes, (nw, 3), start, finish, peers=("x", "y", "xy"))


def _join_partials(parts, slots):
    nw = len(parts)

    def copies(outs, ss, rs, mine):
        x, y, c, me, _ = _place()
        res = []
        for w in range(nw):
            rh = parts[w].shape[1] // 2
            rows = pl.ds((c if mine else 1 - c) * rh, rh)
            own = outs[w].at[me, rows]
            got = outs[nw + w].at[:, rows, :]
            res.append(_remote(own, own, ss.at[w, 0], rs.at[w, 0], (x, y, 1 - c)))
            res.append(_remote(got, got, ss.at[w, 1], rs.at[w, 1], (x, y, 1 - c)))
        return res

    def start(ins, outs, ss, rs):
        for cp in copies(outs, ss, rs, True):
            cp.start()

    def finish(ins, outs, ss, rs):
        for cp in copies(outs, ss, rs, True):
            cp.wait_send()
        for cp in copies(outs, ss, rs, False):
            cp.wait_recv()

    arrays = tuple(parts) + tuple(slots)
    return _Exchange(arrays, tuple(jax.ShapeDtypeStruct(a.shape, a.dtype) for a in arrays), (nw, 2), start, finish,
                     in_place=True, peers=("c",))


def _gather_small(slab):
    def copies(ins, outs, ss, rs):
        x, y, c, _, _ = _place()
        me = 4 * x + 2 * y + c
        out, arrivals = [], []
        for k in range(1, 8):
            px = 1 - x if k & 4 else x
            py = 1 - y if k & 2 else y
            pc = 1 - c if k & 1 else c
            out.append(_remote(ins[0], outs[0].at[me], ss.at[k - 1], rs.at[k - 1], (px, py, pc)))
            theirs = outs[0].at[4 * px + 2 * py + pc]
            arrivals.append((theirs, k - 1, (px, py, pc)))
        return pltpu.make_async_copy(ins[0], outs[0].at[me], ss.at[7]), out, arrivals

    def start(ins, outs, ss, rs):
        own, out, _ = copies(ins, outs, ss, rs)
        own.start()
        for cp in out:
            cp.start()

    def finish(ins, outs, ss, rs):
        own, out, arrivals = copies(ins, outs, ss, rs)
        for cp in out:
            cp.wait_send()
        for theirs, k, peer in arrivals:
            _remote(theirs, theirs, ss.at[k], rs.at[k], peer).wait_recv()
        own.wait()

    return _Exchange((slab,), (jax.ShapeDtypeStruct((8,) + slab.shape, slab.dtype),), (8,), start, finish)


_SMALL_VECS = ("ln_mix_g", "ln_attn_g", "ln_mem_g", "ln_ffn_g", "ln_final_g")


def _pack_small(p, extra, conv):
    d = p["ln_mix_g"].shape[-1]
    top = [p[k].reshape(1, d) for k in _SMALL_VECS]
    top.append(jnp.concatenate([p["sgu_ln_g"].reshape(-1), p["sgu_ln_b"].reshape(-1)]).reshape(1, d))
    top.append(jnp.concatenate([p["grp_norm_a"].reshape(-1), p["grp_norm_b"].reshape(-1)]).reshape(1, d))
    top.append(jnp.concatenate([p["b_spatial"].reshape(-1), extra]).reshape(1, d))
    mid = jnp.zeros((8, d), F32)
    if conv is not None:
        mid = jnp.pad(conv, ((0, 5), (0, d - conv.shape[1])))
    return jnp.concatenate([jnp.concatenate(top, axis=0), mid, p["w_spatial"].reshape(-1, d)], axis=0)


def _unpack_small(slab):
    d = slab.shape[1]
    hw = d // 2
    out = {k: slab[i] for i, k in enumerate(_SMALL_VECS)}
    out["sgu_ln_g"], out["sgu_ln_b"] = slab[5, :hw], slab[5, hw:]
    out["grp_norm_a"], out["grp_norm_b"] = slab[6, :hw], slab[6, hw:]
    out["b_spatial"] = slab[7, :hw].reshape(HEADS, CHUNK)
    out["w_spatial"] = slab[16:].reshape(HEADS, CHUNK, CHUNK)
    return out


_BIG = ("w_in", "w_kv", "w_gate_up", "w_out", "w_q", "w_o", "w_down")
_WEIGHTS = ("ln_mix_g", "w_in", "sgu_ln_g", "sgu_ln_b", "w_spatial", "b_spatial", "conv_w", "grp_norm_a",
            "grp_norm_b", "w_out", "ln_attn_g", "ln_mem_g", "w_q", "w_kv", "w_o", "ln_ffn_g", "w_gate_up",
            "w_down", "ln_final_g")


def _step(p, m_, v_, x, mem, target):
    s, d = x.shape
    hw = d // 2
    row = lambda a: a.reshape(1, -1)
    x_, y_, c_ = lax.axis_index("x"), lax.axis_index("y"), lax.axis_index("c")
    chip = 2 * x_ + y_

    conv8 = jnp.pad(p["conv_w"], ((0, 5), (0, 0)))
    later = ("w_kv", "w_q", "w_o", "w_down", "w_gate_up")
    first = _run_exchange(_all_gather([p["w_in"].astype(BF16), p["w_out"].astype(BF16)], [conv8]),
                          "all_gather_mixer", collective_id=4, casts=[p[k] for k in later])
    (w_in, w_out4, conv4), bf = first[:3], dict(zip(later, first[3:]))
    cw = jnp.transpose(conv4[:, :3, :], (1, 0, 2)).reshape(3, hw)
    b_t = jnp.pad(jnp.transpose(p["b_spatial"]), ((0, 0), (0, CHUNK - HEADS)))
    g1, g2, gm, g3, gf = (row(p[k]) for k in _SMALL_VECS)
    lng, lnb, ga, gb = row(p["sgu_ln_g"]), row(p["sgu_ln_b"]), row(p["grp_norm_a"]), row(p["grp_norm_b"])
    wsp = p["w_spatial"]
    w_out = w_out4.reshape(-1, d)

    (h, x1, ycat, xn1, mixed, th), (w_kv, w_q4, w_o4, w_down4) = _mixer_fwd(
        x, g1, w_in, lng, lnb, wsp, b_t, cw, ga, gb, w_out,
        hosted=_all_gather([bf[k] for k in ("w_kv", "w_q", "w_o", "w_down")]).with_id(5))
    w_q, w_o, w_down = (a.reshape(-1, d) for a in (w_q4, w_o4, w_down4))
    (x2, o, qs, probs, memn, kv), (w_gu,) = _attn_fwd(x1, mem, g2, gm, w_q, w_kv, w_o,
                                                      hosted=_all_gather([bf["w_gate_up"]]).with_id(6))
    dx2, act, dgu, xn3, dx3, loss, dgf, dg3 = _ffn_fwd_bwd(x2, g3, gf, target, w_gu, w_down)

    place = jnp.stack([c_, chip]).astype(jnp.int32)

    def chip_partials(names, grads, tag):
        return list(_chip_reduce(grads, "chip_reduce_" + tag, ("down", "ffn", "attn", "mixer").index(tag)))

    names_d = ("w_down",)
    parts_d = chip_partials(names_d, (_weight_grad(act, dx3, "grad_w_down", 1408, 512, False)[0],), "down")
    g_gu, slots_d = _weight_grad(xn3, dgu, "grad_w_gate_up", 512, 1408, True,
                                 hosted=_scatter_partials(parts_d).with_id(7))
    names_a = ("w_gate_up",)
    parts_a = chip_partials(names_a, (g_gu,), "ffn")
    (dx1, dkv, dg2, g_o, g_q, g_out), slots_a = _attn_bwd(x1, dx2, o, ycat, qs, probs, g2, w_q, kv, w_o,
                                                           hosted=_scatter_partials(parts_a).with_id(8))
    g_kv, dgm = _kv_bwd(dkv, mem, memn, gm, w_kv)
    names_b = ("w_o", "w_out", "w_q", "w_kv")
    shard_major = lambda g: g.reshape(N_CHIPS, -1, d)
    parts_b = chip_partials(names_b, (shard_major(g_o), shard_major(g_out), shard_major(g_q), g_kv), "attn")
    names_da = names_d + names_a
    (dx, dh, dg1, dlng, dlnb, dwsp, dbt, dcw, dga, dgb), extra = _mixer_bwd(
        x, dx1, h, mixed, th, g1, lng, lnb, wsp, b_t, cw, ga, gb, w_out, w_in,
        hosted=[_scatter_partials(parts_b).with_id(9), _join_partials(parts_d + parts_a, slots_d + slots_a)])
    slots_b, joined = extra[:len(names_b)], extra[len(names_b):]
    whole = dict(zip(names_da, zip(joined[:len(names_da)], joined[len(names_da):])))
    small = {"ln_mix_g": dg1, "ln_attn_g": dg2, "ln_mem_g": dgm, "ln_ffn_g": dg3, "ln_final_g": dgf,
             "sgu_ln_g": dlng, "sgu_ln_b": dlnb, "grp_norm_a": dga, "grp_norm_b": dgb,
             "b_spatial": jnp.transpose(dbt[:, :HEADS]), "w_spatial": dwsp}
    loss_vec = jnp.pad(loss.reshape(1), (0, hw - 1))
    g_in, extra = _weight_grad(
        xn1, dh, "grad_w_in", 1024, 640, True,
        hosted=[_gather_small(_pack_small(small, loss_vec, dcw)), _join_partials(parts_b, slots_b)])
    parts = extra[0]
    whole.update(zip(names_b, zip(extra[1:1 + len(names_b)], extra[1 + len(names_b):])))
    (part_in,) = chip_partials(("w_in",), (g_in,), "mixer")
    out_g, out_d, out_m, out_v = {}, {}, {}, {}

    def finalize(ks, tag, hosted=None):
        done, res = _finalize([(whole[k][0], whole[k][1], p[k], m_[k], v_[k]) for k in ks], place,
                              "finalize_" + tag, hosted)
        for k, (g, dl, nm, nv) in zip(ks, done):
            out_g[k], out_d[k], out_m[k], out_v[k] = g, dl, nm, nv
        return res

    (slots_in,) = finalize(("w_down",), "w_down", _scatter_partials([part_in]).with_id(10))
    finalize(("w_o", "w_out", "w_q"), "attn")
    for k in ("w_gate_up", "w_kv"):
        finalize((k,), k)
    whole["w_in"] = _run_exchange(_join_partials([part_in], [slots_in]), "rs_join_mixer", collective_id=11)
    finalize(("w_in",), "w_in")

    zeros = jnp.zeros((hw,), F32)
    sg, sd, sm, sv = _small_sum_adamw(parts, _pack_small(p, zeros, None), _pack_small(m_, zeros, None),
                                      _pack_small(v_, zeros, None))
    for tree, slab in zip((out_g, out_d, out_m, out_v), (sg, sd, sm, sv)):
        tree.update(_unpack_small(slab))
    loss_out = sg[7, hw]
    g_conv = lax.dynamic_slice(sg[8:11, :hw], (0, chip * (hw // N_CHIPS)), (3, hw // N_CHIPS))
    out_g["conv_w"] = g_conv
    out_d["conv_w"], out_m["conv_w"], out_v["conv_w"] = _adamw(p["conv_w"], g_conv, m_["conv_w"], v_["conv_w"],
                                                                "adamw_conv_w")
    return loss_out, dx, out_g, out_d, out_m, out_v


def kernel(x, mem, ln_mix_g, w_in, sgu_ln_g, sgu_ln_b, w_spatial, b_spatial, conv_w, grp_norm_a, grp_norm_b, w_out, ln_attn_g, ln_mem_g, w_q, w_kv, w_o, ln_ffn_g, w_gate_up, w_down, ln_final_g, loss_target, m_ln_mix_g, m_w_in, m_sgu_ln_g, m_sgu_ln_b, m_w_spatial, m_b_spatial, m_conv_w, m_grp_norm_a, m_grp_norm_b, m_w_out, m_ln_attn_g, m_ln_mem_g, m_w_q, m_w_kv, m_w_o, m_ln_ffn_g, m_w_gate_up, m_w_down, m_ln_final_g, v_ln_mix_g, v_w_in, v_sgu_ln_g, v_sgu_ln_b, v_w_spatial, v_b_spatial, v_conv_w, v_grp_norm_a, v_grp_norm_b, v_w_out, v_ln_attn_g, v_ln_mem_g, v_w_q, v_w_kv, v_w_o, v_ln_ffn_g, v_w_gate_up, v_w_down, v_ln_final_g):
    p = dict(ln_mix_g=ln_mix_g, w_in=w_in, sgu_ln_g=sgu_ln_g, sgu_ln_b=sgu_ln_b, w_spatial=w_spatial,
             b_spatial=b_spatial, conv_w=conv_w, grp_norm_a=grp_norm_a, grp_norm_b=grp_norm_b, w_out=w_out,
             ln_attn_g=ln_attn_g, ln_mem_g=ln_mem_g, w_q=w_q, w_kv=w_kv, w_o=w_o, ln_ffn_g=ln_ffn_g,
             w_gate_up=w_gate_up, w_down=w_down, ln_final_g=ln_final_g)
    m_ = dict(ln_mix_g=m_ln_mix_g, w_in=m_w_in, sgu_ln_g=m_sgu_ln_g, sgu_ln_b=m_sgu_ln_b, w_spatial=m_w_spatial,
              b_spatial=m_b_spatial, conv_w=m_conv_w, grp_norm_a=m_grp_norm_a, grp_norm_b=m_grp_norm_b,
              w_out=m_w_out, ln_attn_g=m_ln_attn_g, ln_mem_g=m_ln_mem_g, w_q=m_w_q, w_kv=m_w_kv, w_o=m_w_o,
              ln_ffn_g=m_ln_ffn_g, w_gate_up=m_w_gate_up, w_down=m_w_down, ln_final_g=m_ln_final_g)
    v_ = dict(ln_mix_g=v_ln_mix_g, w_in=v_w_in, sgu_ln_g=v_sgu_ln_g, sgu_ln_b=v_sgu_ln_b, w_spatial=v_w_spatial,
              b_spatial=v_b_spatial, conv_w=v_conv_w, grp_norm_a=v_grp_norm_a, grp_norm_b=v_grp_norm_b,
              w_out=v_w_out, ln_attn_g=v_ln_attn_g, ln_mem_g=v_ln_mem_g, w_q=v_w_q, w_kv=v_w_kv, w_o=v_w_o,
              ln_ffn_g=v_ln_ffn_g, w_gate_up=v_w_gate_up, w_down=v_w_down, ln_final_g=v_ln_final_g)
    s, d = x.shape[-2], x.shape[-1]
    loss, dx, g, dl, nm, nv = _step(p, m_, v_, x.reshape(s, d), mem.reshape(-1, d), loss_target.reshape(s, d))
    outs = [loss, dx.reshape(x.shape)]
    for tree in (g, dl, nm, nv):
        outs += [tree[k].reshape(p[k].shape) for k in _WEIGHTS]
    return tuple(outs)
```

```python
import functools
import math

import jax
import jax.numpy as jnp
from jax import lax
from jax.experimental import pallas as pl
from jax.experimental.pallas import tpu as pltpu

F32 = jnp.float32
BF16 = jnp.bfloat16
EPS = 1e-6
CHUNK = 128
HEADS = 4
N_CHIPS = 4
TM = 512
TM_ATTN = 512
TM_FFN = 256
ADAM_LR, ADAM_B1, ADAM_B2, ADAM_EPS, ADAM_WD, ADAM_STEP = 0.001, 0.9, 0.999, 1e-08, 0.01, 10
GELU_C = math.sqrt(2.0 / math.pi)
GELU_K = 0.044715
SMALL_ROWS = 80
VMEM_LIMIT = 56 * 1024 * 1024
MIDDLE_STEP_16THS = 7
MESH = pl.DeviceIdType.MESH
ANY = pl.BlockSpec(memory_space=pl.ANY)


def _params(*sem, collective_id=None):
    return pltpu.CompilerParams(dimension_semantics=sem, vmem_limit_bytes=VMEM_LIMIT, collective_id=collective_id)


def _dot(a, b):
    return jnp.dot(a, b, preferred_element_type=F32)


def _dot_nt(a, b):
    return lax.dot_general(a, b, (((1,), (1,)), ((), ())), preferred_element_type=F32)


def _dot_tn(a, b):
    return lax.dot_general(a, b, (((0,), (0,)), ((), ())), preferred_element_type=F32)


def _rms_fwd(x, g):
    r = lax.rsqrt(jnp.mean(x * x, axis=-1, keepdims=True) + EPS)
    xh = x * r
    return xh * g, xh, r


def _rms_bwd(dy, xh, r, g):
    dxh = dy * g
    dx = r * (dxh - xh * jnp.mean(dxh * xh, axis=-1, keepdims=True))
    return dx, jnp.sum(dy * xh, axis=0, keepdims=True)


def _full(shape):
    nd = len(shape)
    return pl.BlockSpec(shape, lambda *_: (0,) * nd, pipeline_mode=pl.Buffered(1))


def _acc(shape):
    nd = len(shape)
    return pl.BlockSpec(shape, lambda *_: (0,) * nd)


def _rows(tm, cols):
    return pl.BlockSpec((tm, cols), lambda i: (i, 0))


def _tril_weights(wsp_ref):
    row = lax.broadcasted_iota(jnp.int32, (CHUNK, CHUNK), 0)
    col = lax.broadcasted_iota(jnp.int32, (CHUNK, CHUNK), 1)
    return [jnp.where(row >= col, wsp_ref[hd], 0.0).astype(BF16) for hd in range(HEADS)]


def _shift_rows(z, zp):
    row = lax.broadcasted_iota(jnp.int32, z.shape, 0)
    zm1 = jnp.where(row == 0, zp[7:8, :], pltpu.roll(z, 1, 0))
    zm2 = jnp.where(row == 0, zp[6:7, :], jnp.where(row == 1, zp[7:8, :], pltpu.roll(z, 2, 0)))
    return zm1, zm2


def _gelu_parts(x):
    t = jnp.tanh(GELU_C * (x + GELU_K * (x * x * x)))
    return 0.5 * x * (1.0 + t), t


def _layer_norm_parts(v, g, b):
    mu = jnp.mean(v, axis=-1, keepdims=True)
    vc = v - mu
    rs = lax.rsqrt(jnp.mean(vc * vc, axis=-1, keepdims=True) + EPS)
    vhat = vc * rs
    return vhat * g + b, vhat, rs


def _mixer_fwd(x, g1, w_in, lng, lnb, wsp, b_t, cw, ga, gb, w_out, hosted=None):
    s, d = x.shape
    n = s // TM
    nch = TM // CHUNK
    ns = w_in.shape[2]
    nh = N_CHIPS * ns
    aw = d // 2
    hd_w = aw // HEADS

    def body(x_ref, g1_ref, win_ref, lng_ref, lnb_ref, wsp_ref, bt_ref, cw_ref, ga_ref, gb_ref, wout_ref,
             h_ref, x1_ref, y_ref, xn_ref, mix_ref, th_ref, zp_ref):
        i = pl.program_id(0)

        @pl.when(i == 0)
        def _():
            zp_ref[...] = jnp.zeros_like(zp_ref)

        x = x_ref[...]
        xn, _, _ = _rms_fwd(x, g1_ref[...])
        xnb = xn.astype(BF16)
        xn_ref[...] = xnb
        for k in range(N_CHIPS):
            h_ref[:, k * ns:(k + 1) * ns] = _dot(xnb, win_ref[k])
        a, th = _gelu_parts(h_ref[:, 0:2 * aw])
        th_ref[...] = th
        u = a[:, :aw]
        vn, _, _ = _layer_norm_parts(a[:, aw:], lng_ref[...], lnb_ref[...])
        vnb = vn.astype(BF16)
        wm = _tril_weights(wsp_ref)
        for c in range(nch):
            for hd in range(HEADS):
                blk = vnb[c * CHUNK:(c + 1) * CHUNK, hd * hd_w:(hd + 1) * hd_w]
                mix_ref[c * CHUNK:(c + 1) * CHUNK, hd * hd_w:(hd + 1) * hd_w] = _dot(wm[hd], blk) + bt_ref[:, hd:hd + 1]
        ya, _, _ = _rms_fwd(u * mix_ref[...], ga_ref[...])
        g_b = h_ref[:, 2 * aw:3 * aw]
        z = h_ref[:, 3 * aw:4 * aw] * h_ref[:, 4 * aw:5 * aw]
        zm1, zm2 = _shift_rows(z, zp_ref[...])
        conv = cw_ref[0:1, :] * zm2 + cw_ref[1:2, :] * zm1 + cw_ref[2:3, :] * z
        yb, _, _ = _rms_fwd(g_b * conv, gb_ref[...])
        zp_ref[...] = z[TM - 8:TM, :]
        ycat = jnp.concatenate([ya, yb], axis=-1).astype(BF16)
        y_ref[...] = ycat
        x1_ref[...] = x + _dot(ycat, wout_ref[...])

    return _host_call(
        body, "mixer_fwd", (n,),
        [_rows(TM, d), _full(g1.shape), _full(w_in.shape), _full(lng.shape), _full(lnb.shape),
         _full(wsp.shape), _full(b_t.shape), _full(cw.shape), _full(ga.shape), _full(gb.shape),
         _full(w_out.shape)],
        [_rows(TM, nh), _rows(TM, d), _rows(TM, d), _rows(TM, d), _rows(TM, aw), _rows(TM, d)],
        (jax.ShapeDtypeStruct((s, nh), F32), jax.ShapeDtypeStruct((s, d), F32),
         jax.ShapeDtypeStruct((s, d), BF16), jax.ShapeDtypeStruct((s, d), BF16),
         jax.ShapeDtypeStruct((s, aw), F32), jax.ShapeDtypeStruct((s, d), F32)),
        [pltpu.VMEM((8, aw), F32)],
        (x, g1, w_in, lng, lnb, wsp, b_t, cw, ga, gb, w_out), ("arbitrary",), hosted)


def _attn_fwd(x1, mem, g2, g_mem, w_q, w_kv, w_o, hosted=None):
    s, d = x1.shape
    tm = min(TM_ATTN, s)
    n = s // tm
    dh = d // HEADS
    m = mem.shape[0]
    ns = w_kv.shape[2]
    scale = dh ** -0.5

    def body(x1_ref, mem_ref, g2_ref, gm_ref, wq_ref, wkv_ref, wo_ref, x2_ref, o_ref, q_ref, p_ref, memn_ref, kv_ref):
        @pl.when(pl.program_id(0) == 0)
        def _():
            y, _, _ = _rms_fwd(mem_ref[...], gm_ref[...])
            yb = y.astype(BF16)
            memn_ref[...] = yb
            for k in range(N_CHIPS):
                kv_ref[:, k * ns:(k + 1) * ns] = _dot(yb, wkv_ref[k]).astype(BF16)

        x1v = x1_ref[...]
        xn, _, _ = _rms_fwd(x1v, g2_ref[...])
        q_ref[...] = _dot(xn.astype(BF16), wq_ref[...]).astype(BF16)
        for hd in range(HEADS):
            kh = kv_ref[:, hd * dh:(hd + 1) * dh]
            vh = kv_ref[:, d + hd * dh:d + (hd + 1) * dh]
            sc = _dot_nt(q_ref[:, hd * dh:(hd + 1) * dh], kh) * scale
            e = jnp.exp(sc - jnp.max(sc, axis=-1, keepdims=True))
            p = e / jnp.sum(e, axis=-1, keepdims=True)
            p_ref[:, hd * m:(hd + 1) * m] = p
            o_ref[:, hd * dh:(hd + 1) * dh] = _dot(p.astype(BF16), vh).astype(BF16)
        x2_ref[...] = x1v + _dot(o_ref[...], wo_ref[...])

    return _host_call(
        body, "attn_fwd", (n,),
        [_rows(tm, d), _full(mem.shape), _full(g2.shape), _full(g_mem.shape), _full(w_q.shape), _full(w_kv.shape),
         _full(w_o.shape)],
        [_rows(tm, d), _rows(tm, d), _rows(tm, d), _rows(tm, HEADS * m), _acc((m, d)), _acc((m, 2 * d))],
        (jax.ShapeDtypeStruct((s, d), F32), jax.ShapeDtypeStruct((s, d), BF16), jax.ShapeDtypeStruct((s, d), BF16),
         jax.ShapeDtypeStruct((s, HEADS * m), F32), jax.ShapeDtypeStruct((m, d), BF16),
         jax.ShapeDtypeStruct((m, 2 * d), BF16)),
        [], (x1, mem, g2, g_mem, w_q, w_kv, w_o), ("arbitrary",), hosted)


def _ffn_fwd_bwd(x2, g3, gf, target, w_gu, w_down):
    s, d = x2.shape
    tm = min(TM_FFN, s)
    n = s // tm
    ns = w_gu.shape[2]
    ff = 2 * ns

    def body(x2_ref, g3_ref, gf_ref, t_ref, wgu_ref, wd_ref,
             dx2_ref, act_ref, dgu_ref, xn_ref, dx3_ref, loss_ref, dgf_ref, dg3_ref):
        i = pl.program_id(0)

        @pl.when(i == 0)
        def _():
            loss_ref[...] = jnp.zeros_like(loss_ref)
            dgf_ref[...] = jnp.zeros_like(dgf_ref)
            dg3_ref[...] = jnp.zeros_like(dg3_ref)

        x2v = x2_ref[...]
        xn, xh3, r3 = _rms_fwd(x2v, g3_ref[...])
        xnb = xn.astype(BF16)
        xn_ref[...] = xnb
        x3 = x2v
        saved = []
        for j in range(2):
            g = _dot(xnb, wgu_ref[j])
            u = _dot(xnb, wgu_ref[2 + j])
            sg = 1.0 / (1.0 + jnp.exp(-g))
            sl = g * sg
            actb = (sl * u).astype(BF16)
            act_ref[:, j * ns:(j + 1) * ns] = actb
            x3 = x3 + _dot(actb, wd_ref[j * ns:(j + 1) * ns, :])
            saved.append((u, sl, sg * (1.0 + g * (1.0 - sg))))
        gfv = gf_ref[...]
        y, xhf, rf = _rms_fwd(x3, gfv)
        e = y - t_ref[...]
        loss_ref[...] += 0.5 * jnp.sum(jnp.sum(e * e, axis=-1, keepdims=True), axis=0, keepdims=True) / d
        dx3, dgf = _rms_bwd(e / d, xhf, rf, gfv)
        dgf_ref[...] += dgf
        dx3b = dx3.astype(BF16)
        dx3_ref[...] = dx3b
        dxn = jnp.zeros_like(x2v)
        for j in range(2):
            u, sl, dsl = saved[j]
            dact = _dot_nt(dx3b, wd_ref[j * ns:(j + 1) * ns, :])
            dgb = (dact * u * dsl).astype(BF16)
            dub = (dact * sl).astype(BF16)
            dgu_ref[:, j * ns:(j + 1) * ns] = dgb
            dgu_ref[:, ff + j * ns:ff + (j + 1) * ns] = dub
            dxn = dxn + _dot_nt(dgb, wgu_ref[j]) + _dot_nt(dub, wgu_ref[2 + j])
        dxr, dg3 = _rms_bwd(dxn, xh3, r3, g3_ref[...])
        dg3_ref[...] += dg3
        dx2_ref[...] = dx3 + dxr

    vec = jax.ShapeDtypeStruct((1, d), F32)
    return pl.pallas_call(
        body, name="ffn_fwd_bwd", grid=(n,),
        in_specs=[_rows(tm, d), _full(g3.shape), _full(gf.shape), _rows(tm, d), _full(w_gu.shape),
                  _full(w_down.shape)],
        out_specs=[_rows(tm, d), _rows(tm, ff), _rows(tm, 2 * ff), _rows(tm, d), _rows(tm, d),
                   _acc((1, 1)), _acc((1, d)), _acc((1, d))],
        out_shape=(jax.ShapeDtypeStruct((s, d), F32), jax.ShapeDtypeStruct((s, ff), BF16),
                   jax.ShapeDtypeStruct((s, 2 * ff), BF16), jax.ShapeDtypeStruct((s, d), BF16),
                   jax.ShapeDtypeStruct((s, d), BF16), jax.ShapeDtypeStruct((1, 1), F32), vec, vec),
        compiler_params=_params("arbitrary"),
    )(x2, g3, gf, target, w_gu, w_down)


def _attn_bwd(x1, dx2, o, ycat, qs, probs, g2, w_q, kv, w_o, hosted=None):
    s, d = x1.shape
    tm = min(TM_ATTN, s)
    n = s // tm
    dh = d // HEADS
    scale = dh ** -0.5
    m = kv.shape[0]

    def body(x1_ref, dx2_ref, o_ref, y_ref, q_ref, p_ref, g2_ref, wq_ref, kv_ref, wo_ref,
             dx1_ref, dkv_ref, dg2_ref, gwo_out, gwq_out, gwout_out, dq_ref, gwo_ref, gwq_ref, gwout_ref):
        i = pl.program_id(0)

        @pl.when(i == 0)
        def _():
            for r in (dkv_ref, dg2_ref, gwo_ref, gwq_ref, gwout_ref):
                r[...] = jnp.zeros_like(r)

        xn, xh2, r2 = _rms_fwd(x1_ref[...], g2_ref[...])
        xnb = xn.astype(BF16)
        dx2v = dx2_ref[...]
        dx2b = dx2v.astype(BF16)
        gwo_ref[...] += _dot_tn(o_ref[...], dx2b)
        do = _dot_nt(dx2b, wo_ref[...])
        for hd in range(HEADS):
            qb = q_ref[:, hd * dh:(hd + 1) * dh]
            p = p_ref[:, hd * m:(hd + 1) * m]
            kh = kv_ref[:, hd * dh:(hd + 1) * dh]
            vh = kv_ref[:, d + hd * dh:d + (hd + 1) * dh]
            dob = do[:, hd * dh:(hd + 1) * dh].astype(BF16)
            dp = _dot_nt(dob, vh)
            ds = p * (dp - jnp.sum(dp * p, axis=-1, keepdims=True))
            dsb = (ds * scale).astype(BF16)
            dq_ref[:, hd * dh:(hd + 1) * dh] = _dot(dsb, kh).astype(BF16)
            dkv_ref[:, hd * dh:(hd + 1) * dh] += _dot_tn(dsb, qb)
            dkv_ref[:, d + hd * dh:d + (hd + 1) * dh] += _dot_tn(p.astype(BF16), dob)
        dqb = dq_ref[...]
        gwq_ref[...] += _dot_tn(xnb, dqb)
        dxn = _dot_nt(dqb, wq_ref[...])
        dxr, dg2 = _rms_bwd(dxn, xh2, r2, g2_ref[...])
        dg2_ref[...] += dg2
        dx1 = dx2v + dxr
        dx1_ref[...] = dx1
        gwout_ref[...] += _dot_tn(y_ref[...], dx1.astype(BF16))

        @pl.when(i == n - 1)
        def _():
            for acc, out in ((gwo_ref, gwo_out), (gwq_ref, gwq_out), (gwout_ref, gwout_out)):
                out[...] = acc[...].astype(BF16)

    sq = jax.ShapeDtypeStruct((d, d), BF16)
    return _host_call(
        body, "attn_bwd", (n,),
        [_rows(tm, d), _rows(tm, d), _rows(tm, d), _rows(tm, d), _rows(tm, d), _rows(tm, HEADS * m),
         _full(g2.shape), _full(w_q.shape), _full(kv.shape), _full(w_o.shape)],
        [_rows(tm, d), _acc((m, 2 * d)), _acc((1, d)), _acc((d, d)), _acc((d, d)), _acc((d, d))],
        (jax.ShapeDtypeStruct((s, d), F32), jax.ShapeDtypeStruct((m, 2 * d), F32),
         jax.ShapeDtypeStruct((1, d), F32), sq, sq, sq),
        [pltpu.VMEM((tm, d), BF16)] + [pltpu.VMEM((d, d), F32)] * 3,
        (x1, dx2, o, ycat, qs, probs, g2, w_q, kv, w_o), ("arbitrary",), hosted)


def _kv_bwd(dkv, mem, memn, g_mem, w_kv):
    m, d = mem.shape
    ns = w_kv.shape[2]

    def body(dkv_ref, mem_ref, memn_ref, g_ref, w_ref, gw_ref, dg_ref):
        _, xh, _ = _rms_fwd(mem_ref[...], g_ref[...])
        dmemn = jnp.zeros((m, d), F32)
        for k in range(N_CHIPS):
            dkb = dkv_ref[:, k * ns:(k + 1) * ns].astype(BF16)
            gw_ref[k] = _dot_tn(memn_ref[...], dkb).astype(BF16)
            dmemn = dmemn + _dot_nt(dkb, w_ref[k])
        dg_ref[...] = jnp.sum(dmemn * xh, axis=0, keepdims=True)

    return pl.pallas_call(
        body, name="kv_bwd",
        out_shape=(jax.ShapeDtypeStruct((N_CHIPS, d, ns), BF16), jax.ShapeDtypeStruct((1, d), F32)),
        compiler_params=pltpu.CompilerParams(vmem_limit_bytes=VMEM_LIMIT),
    )(dkv, mem, memn, g_mem, w_kv)


def _mixer_bwd(x, dx1, h, mixed_all, th_all, g1, lng, lnb, wsp, b_t, cw, ga, gb, w_out, w_in, hosted=None):
    s, d = x.shape
    n = s // TM
    nch = TM // CHUNK
    ns = w_in.shape[2]
    nh = N_CHIPS * ns
    aw = d // 2
    hd_w = aw // HEADS

    def rev(cols):
        return pl.BlockSpec((TM, cols), lambda i: (n - 1 - i, 0))

    hprev = pl.BlockSpec((8, nh), lambda i: (jnp.maximum((n - 1 - i) * (TM // 8) - 1, 0), 0))

    def body(x_ref, dx1_ref, h_ref, hp_ref, mix_ref, th_ref, g1_ref, lng_ref, lnb_ref, wsp_ref, bt_ref, cw_ref,
             ga_ref, gb_ref, wout_ref, win_ref,
             dx_ref, dh_ref, dg1_ref, dlng_ref, dlnb_ref, dwsp_ref, dbt_ref, dcw_ref, dga_ref, dgb_ref,
             dvn_ref, dcn_ref):
        i = pl.program_id(0)

        @pl.when(i == 0)
        def _():
            for r in (dg1_ref, dlng_ref, dlnb_ref, dwsp_ref, dbt_ref, dcw_ref, dga_ref, dgb_ref, dcn_ref):
                r[...] = jnp.zeros_like(r)

        dx1v = dx1_ref[...]
        dycat = _dot_nt(dx1v.astype(BF16), wout_ref[...])
        ha = h_ref[:, 0:2 * aw]
        th = th_ref[...]
        a = 0.5 * ha * (1.0 + th)
        u = a[:, :aw]
        lngv = lng_ref[...]
        vn, vhat, rs = _layer_norm_parts(a[:, aw:], lngv, lnb_ref[...])
        vnb = vn.astype(BF16)
        wm = _tril_weights(wsp_ref)
        mixed = mix_ref[...]
        gav = ga_ref[...]
        _, yah, ra = _rms_fwd(u * mixed, gav)
        dya, dga = _rms_bwd(dycat[:, :aw], yah, ra, gav)
        dga_ref[...] += dga
        du = dya * mixed
        dmix = dya * u
        dmb = dmix.astype(BF16)
        tri = lax.broadcasted_iota(jnp.int32, (CHUNK, CHUNK), 0) >= lax.broadcasted_iota(jnp.int32, (CHUNK, CHUNK), 1)
        for hd in range(HEADS):
            dw = jnp.zeros((CHUNK, CHUNK), F32)
            db = jnp.zeros((CHUNK, 1), F32)
            for c in range(nch):
                rows = slice(c * CHUNK, (c + 1) * CHUNK)
                cols = slice(hd * hd_w, (hd + 1) * hd_w)
                dvn_ref[rows, cols] = _dot_tn(wm[hd], dmb[rows, cols])
                dw = dw + _dot_nt(dmb[rows, cols], vnb[rows, cols])
                db = db + jnp.sum(dmix[rows, cols], axis=1, keepdims=True)
            dwsp_ref[hd] += jnp.where(tri, dw, 0.0)
            dbt_ref[:, hd:hd + 1] += db
        dvn = dvn_ref[...]
        dlng_ref[...] += jnp.sum(dvn * vhat, axis=0, keepdims=True)
        dlnb_ref[...] += jnp.sum(dvn, axis=0, keepdims=True)
        dvh = dvn * lngv
        dv = rs * (dvh - jnp.mean(dvh, axis=-1, keepdims=True) - vhat * jnp.mean(dvh * vhat, axis=-1, keepdims=True))
        gprime = 0.5 * (1.0 + th) + 0.5 * ha * (1.0 - th * th) * (GELU_C * (1.0 + 3.0 * GELU_K * (ha * ha)))
        dh_ref[:, 0:2 * aw] = (jnp.concatenate([du, dv], axis=-1) * gprime).astype(BF16)
        g_b = h_ref[:, 2 * aw:3 * aw]
        g_c = h_ref[:, 3 * aw:4 * aw]
        val = h_ref[:, 4 * aw:5 * aw]
        z = g_c * val
        zp = jnp.where(i == n - 1, 0.0, hp_ref[:, 3 * aw:4 * aw] * hp_ref[:, 4 * aw:5 * aw])
        zm1, zm2 = _shift_rows(z, zp)
        cw0, cw1, cw2 = cw_ref[0:1, :], cw_ref[1:2, :], cw_ref[2:3, :]
        conv = cw0 * zm2 + cw1 * zm1 + cw2 * z
        gbv = gb_ref[...]
        _, ybh, rb = _rms_fwd(g_b * conv, gbv)
        dyb, dgb = _rms_bwd(dycat[:, aw:], ybh, rb, gbv)
        dgb_ref[...] += dgb
        dconv = dyb * g_b
        dcw_ref[0:1, :] += jnp.sum(dconv * zm2, axis=0, keepdims=True)
        dcw_ref[1:2, :] += jnp.sum(dconv * zm1, axis=0, keepdims=True)
        dcw_ref[2:3, :] += jnp.sum(dconv * z, axis=0, keepdims=True)
        nxt = dcn_ref[...]
        row = lax.broadcasted_iota(jnp.int32, dconv.shape, 0)
        dcp1 = jnp.where(row == TM - 1, nxt[0:1, :], pltpu.roll(dconv, TM - 1, 0))
        dcp2 = jnp.where(row == TM - 1, nxt[1:2, :],
                         jnp.where(row == TM - 2, nxt[0:1, :], pltpu.roll(dconv, TM - 2, 0)))
        dz = cw2 * dconv + cw1 * dcp1 + cw0 * dcp2
        dcn_ref[...] = dconv[0:8, :]
        dh_ref[:, 2 * aw:3 * aw] = (dyb * conv).astype(BF16)
        dh_ref[:, 3 * aw:4 * aw] = (dz * val).astype(BF16)
        dh_ref[:, 4 * aw:5 * aw] = (dz * g_c).astype(BF16)
        dxn = jnp.zeros((TM, d), F32)
        for k in range(N_CHIPS):
            dxn = dxn + _dot_nt(dh_ref[:, k * ns:(k + 1) * ns], win_ref[k])
        g1v = g1_ref[...]
        _, xh1, r1 = _rms_fwd(x_ref[...], g1v)
        dxr, dg1 = _rms_bwd(dxn, xh1, r1, g1v)
        dg1_ref[...] += dg1
        dx_ref[...] = dx1v + dxr

    ins = (x, dx1, h, h, mixed_all, th_all, g1, lng, lnb, wsp, b_t, cw, ga, gb, w_out, w_in)
    acc_shapes = [(1, d), (1, aw), (1, aw), wsp.shape, (CHUNK, CHUNK), cw.shape, (1, aw), (1, aw)]
    return _host_call(
        body, "mixer_bwd", (n,),
        [rev(d), rev(d), rev(nh), hprev, rev(aw), rev(d)] + [_full(a.shape) for a in ins[6:]],
        [rev(d), rev(nh)] + [_acc(sh) for sh in acc_shapes],
        (jax.ShapeDtypeStruct((s, d), F32), jax.ShapeDtypeStruct((s, nh), BF16))
        + tuple(jax.ShapeDtypeStruct(sh, F32) for sh in acc_shapes),
        [pltpu.VMEM((TM, aw), F32), pltpu.VMEM((8, aw), F32)],
        ins, ("arbitrary",), hosted)


def _weight_grad(a, b, name, tm, tn, col_sharded, hosted=None):
    t, m = a.shape
    n = b.shape[1]

    def body(a_ref, b_ref, o_ref):
        o_ref[...] = _dot_tn(a_ref[...].astype(BF16), b_ref[...].astype(BF16)).astype(BF16)

    if col_sharded:
        ns = n // N_CHIPS
        per = ns // tn
        out_shape = jax.ShapeDtypeStruct((N_CHIPS, m, ns), BF16)
        out_spec = pl.BlockSpec((None, tm, tn), lambda i, j: (j // per, i, j % per))
    else:
        out_shape = jax.ShapeDtypeStruct((m, n), BF16)
        out_spec = pl.BlockSpec((tm, tn), lambda i, j: (i, j))
    (out,), extra = _host_call(
        body, name, (m // tm, n // tn),
        [pl.BlockSpec((t, tm), lambda i, j: (0, i)), pl.BlockSpec((t, tn), lambda i, j: (0, j))],
        [out_spec], (out_shape,), [], (a, b), ("parallel", "parallel"), hosted)
    return (out if col_sharded else out.reshape(N_CHIPS, m // N_CHIPS, n)), extra


def _row_tile(rows, cap=256):
    best = None
    for t in range(16, min(rows, cap) + 1, 16):
        if rows % t == 0:
            best = t
    return best if best is not None else rows


def _adamw_math(w, g, m, v):
    m2 = ADAM_B1 * m + (1.0 - ADAM_B1) * g
    v2 = ADAM_B2 * v + (1.0 - ADAM_B2) * (g * g)
    m_hat = m2 / (1.0 - ADAM_B1 ** ADAM_STEP)
    v_hat = v2 / (1.0 - ADAM_B2 ** ADAM_STEP)
    delta = -ADAM_LR * (m_hat / (jnp.sqrt(v_hat) + ADAM_EPS) + ADAM_WD * w)
    return delta, m2, v2


def _adamw(w, g, m, v, name):
    r, c = w.shape
    tr = _row_tile(r) if r >= 16 else r

    def body(w_ref, g_ref, m_ref, v_ref, d_ref, m2_ref, v2_ref):
        d_ref[...], m2_ref[...], v2_ref[...] = _adamw_math(w_ref[...], g_ref[...], m_ref[...], v_ref[...])

    sh = jax.ShapeDtypeStruct((r, c), F32)
    return pl.pallas_call(
        body, name=name, grid=(r // tr,),
        in_specs=[_rows(tr, c)] * 4, out_specs=[_rows(tr, c)] * 3, out_shape=(sh, sh, sh),
        compiler_params=_params("parallel"),
    )(w, g, m, v)


def _finalize(items, place, name, hosted=None, after=None):
    r, c = items[0][2].shape
    tr = _row_tile(r)
    nw = len(items)

    n_after = 0 if after is None else 1

    def body(place_ref, *refs):
        ins, outs = refs[:7 * nw], refs[7 * nw + n_after:]
        for k in range(nw):
            own_ref, s0_ref, s1_ref, s2_ref, w_ref, m_ref, v_ref = ins[7 * k:7 * k + 7]
            g_ref, d_ref, m2_ref, v2_ref = outs[4 * k:4 * k + 4]
            g = own_ref[...].astype(F32) + s0_ref[...].astype(F32)
            g = (g + s1_ref[...].astype(F32)) + s2_ref[...].astype(F32)
            g_ref[...] = g
            d_ref[...], m2_ref[...], v2_ref[...] = _adamw_math(w_ref[...], g, m_ref[...], v_ref[...])

    def slot(k):
        return pl.BlockSpec((None, tr, c), lambda i, pref: (k, i, 0))

    rows = pl.BlockSpec((tr, c), lambda i, pref: (i, 0))
    sh = jax.ShapeDtypeStruct((r, c), F32)
    one = [pl.BlockSpec((None, tr, c), lambda i, pref: (pref[1], i, 0)), slot(0), slot(1), slot(2), rows, rows, rows]
    args = [a for part, slots, w, m, v in items for a in (part, slots, slots, slots, w, m, v)]
    args += [] if after is None else [after]
    res, extra = _host_call(body, name, (r // tr,), one * nw + [ANY] * n_after, [rows] * (4 * nw), (sh,) * (4 * nw),
                            [], args, ("parallel",), hosted, prefetch=(place,))
    return [res[4 * k:4 * k + 4] for k in range(nw)], extra


def _small_sum_adamw(parts, w, m, v):
    nd, r, c = parts.shape

    def body(p_ref, w_ref, m_ref, v_ref, g_ref, d_ref, m2_ref, v2_ref):
        g = p_ref[0]
        for k in range(1, nd):
            g = g + p_ref[k]
        g_ref[...] = g
        d_ref[...], m2_ref[...], v2_ref[...] = _adamw_math(w_ref[...], g, m_ref[...], v_ref[...])

    sh = jax.ShapeDtypeStruct((r, c), F32)
    return pl.pallas_call(
        body, name="small_sum_adamw", out_shape=(sh, sh, sh, sh),
        compiler_params=pltpu.CompilerParams(vmem_limit_bytes=VMEM_LIMIT),
    )(parts, w, m, v)


def _place():
    x, y, c = lax.axis_index("x"), lax.axis_index("y"), lax.axis_index("c")
    chips = [(1 - x, y), (x, 1 - y), (1 - x, 1 - y)]
    return x, y, c, 2 * x + y, chips


def _remote(src, dst, send_sem, recv_sem, to):
    return pltpu.make_async_remote_copy(src_ref=src, dst_ref=dst, send_sem=send_sem, recv_sem=recv_sem,
                                        device_id=to, device_id_type=MESH)


class _Exchange:
    def __init__(self, ins, out_shapes, sem_shape, start, finish, middle=None, in_place=False, peers=()):
        self.ins, self.out_shapes, self.sem_shape = tuple(ins), tuple(out_shapes), sem_shape
        self.start, self.finish, self.middle = start, finish, middle
        self.in_place = in_place
        self.peers = frozenset(peers)
        self.collective_id = None

    def with_id(self, collective_id):
        self.collective_id = collective_id
        return self


_FLIPS = {"c": (0, 0, 1), "x": (1, 0, 0), "y": (0, 1, 0), "xy": (1, 1, 0)}


def _handshake(peers):
    x, y, c, _, _ = _place()
    barrier = pltpu.get_barrier_semaphore()
    for name in sorted(peers):
        fx, fy, fc = _FLIPS[name]
        peer = (1 - x if fx else x, 1 - y if fy else y, 1 - c if fc else c)
        pl.semaphore_signal(barrier, inc=1, device_id=peer, device_id_type=MESH)
    pl.semaphore_wait(barrier, len(peers))


def _run_exchange(ex, name, collective_id=None, casts=()):
    n_in, n_out, n_cast = len(ex.ins), len(ex.out_shapes), len(casts)
    step = 64

    def body(*refs):
        ins, srcs = refs[:n_in], refs[n_in:n_in + n_cast]
        outs = refs[n_in + n_cast:n_in + n_cast + n_out]
        dsts = refs[n_in + n_cast + n_out:n_in + 2 * n_cast + n_out]
        rest = refs[n_in + 2 * n_cast + n_out:]
        wide, narrow = rest[:n_cast], rest[n_cast:2 * n_cast]
        send_sems, recv_sems = rest[2 * n_cast], rest[2 * n_cast + 1]
        if collective_id is not None:
            _handshake(ex.peers)
        ex.start(ins, outs, send_sems, recv_sems)
        loads = [pltpu.make_async_copy(srcs[k], wide[k], rest[2 * n_cast + 2].at[k]) for k in range(n_cast)]
        for cp in loads:
            cp.start()
        if ex.middle is not None:
            ex.middle(ins, outs, send_sems, recv_sems)
        stores = []
        for k in range(n_cast):
            loads[k].wait()

            def chunk(i, carry, k=k):
                rows = pl.ds(pl.multiple_of(i * step, step), step)
                narrow[k][rows, :] = wide[k][rows, :].astype(BF16)
                return carry
            lax.fori_loop(0, casts[k].shape[0] // step, chunk, 0)
            stores.append(pltpu.make_async_copy(narrow[k], dsts[k], rest[2 * n_cast + 2].at[k]))
            stores[-1].start()
        ex.finish(ins, outs, send_sems, recv_sems)
        for cp in stores:
            cp.wait()

    assert collective_id is None or ex.peers
    sem = pltpu.SemaphoreType.DMA(ex.sem_shape)
    scratch = ([pltpu.VMEM(a.shape, F32) for a in casts] + [pltpu.VMEM(a.shape, BF16) for a in casts] + [sem, sem]
               + ([pltpu.SemaphoreType.DMA((n_cast,))] if n_cast else []))
    return pl.pallas_call(
        body, name=name,
        out_shape=ex.out_shapes + tuple(jax.ShapeDtypeStruct(a.shape, BF16) for a in casts),
        in_specs=[ANY] * (n_in + n_cast), out_specs=[ANY] * (n_out + n_cast),
        input_output_aliases={k: k for k in range(n_in)} if ex.in_place else {}, scratch_shapes=scratch,
        compiler_params=pltpu.CompilerParams(collective_id=collective_id, vmem_limit_bytes=VMEM_LIMIT),
    )(*ex.ins, *casts)


def _host_call(body, name, grid, in_specs, out_specs, out_shape, scratch_shapes, args, semantics, hosted,
               prefetch=()):
    hosted = [] if hosted is None else (list(hosted) if isinstance(hosted, (list, tuple)) else [hosted])
    collective_id = hosted[0].collective_id if hosted else None
    peers = frozenset().union(*[ex.peers for ex in hosted]) if hosted else frozenset()
    assert collective_id is None or all(ex.peers for ex in hosted)
    n_pre, n_in, n_out, n_scr = len(prefetch), len(in_specs), len(out_specs), len(scratch_shapes)
    h_ins = [a for ex in hosted for a in ex.ins]
    h_outs = [s for ex in hosted for s in ex.out_shapes]
    h_in, h_out = len(h_ins), len(h_outs)

    def wrapped(*refs):
        pre, refs = refs[:n_pre], refs[n_pre:]
        a, hi = refs[:n_in], refs[n_in:n_in + h_in]
        o = refs[n_in + h_in:n_in + h_in + n_out]
        ho = refs[n_in + h_in + n_out:n_in + h_in + n_out + h_out]
        scr = refs[n_in + h_in + n_out + h_out:]

        def run(phase):
            i0 = o0 = 0
            for k, ex in enumerate(hosted):
                fn = getattr(ex, phase)
                if fn is not None:
                    fn(hi[i0:i0 + len(ex.ins)], ho[o0:o0 + len(ex.out_shapes)], scr[n_scr + 2 * k],
                       scr[n_scr + 2 * k + 1])
                i0, o0 = i0 + len(ex.ins), o0 + len(ex.out_shapes)

        if hosted:
            first = functools.reduce(jnp.logical_and, [pl.program_id(k) == 0 for k in range(len(grid))])

            @pl.when(first)
            def _():
                if collective_id is not None:
                    _handshake(peers)
                run("start")

        if any(ex.middle is not None for ex in hosted):
            half_way = functools.reduce(jnp.logical_and, [
                pl.program_id(0) == max(1, grid[0] * MIDDLE_STEP_16THS // 16)] + [
                pl.program_id(k) == 0 for k in range(1, len(grid))])

            @pl.when(half_way)
            def _():
                run("middle")

        body(*pre, *a, *o, *scr[:n_scr])

        if hosted:
            last = functools.reduce(jnp.logical_and, [pl.program_id(k) == grid[k] - 1 for k in range(len(grid))])

            @pl.when(last)
            def _():
                run("finish")

    sems = [pltpu.SemaphoreType.DMA(ex.sem_shape) for ex in hosted for _ in range(2)]
    aliases, i0, o0 = {}, n_pre + n_in, n_out
    for ex in hosted:
        if ex.in_place:
            aliases.update({i0 + k: o0 + k for k in range(len(ex.ins))})
        i0, o0 = i0 + len(ex.ins), o0 + len(ex.out_shapes)
    all_in, all_out = list(in_specs) + [ANY] * h_in, list(out_specs) + [ANY] * h_out
    all_scr = list(scratch_shapes) + sems
    params = _params(*(["arbitrary"] * len(grid) if hosted else semantics), collective_id=collective_id)
    shapes = tuple(out_shape) + tuple(h_outs)
    if n_pre:
        call = pl.pallas_call(
            wrapped, name=name, out_shape=shapes, input_output_aliases=aliases, compiler_params=params,
            grid_spec=pltpu.PrefetchScalarGridSpec(num_scalar_prefetch=n_pre, grid=grid, in_specs=all_in,
                                                   out_specs=all_out, scratch_shapes=all_scr))
    else:
        call = pl.pallas_call(
            wrapped, name=name, grid=grid, in_specs=all_in, out_specs=all_out, out_shape=shapes,
            scratch_shapes=all_scr, input_output_aliases=aliases, compiler_params=params)
    res = call(*prefetch, *args, *h_ins)
    return res[:n_out], res[n_out:]


def _all_gather(shards, small=()):
    items = tuple(shards) + tuple(small)
    nw = len(shards)

    def place():
        x, y, c, me, _ = _place()
        first = (x + (1 - c) * (1 - 2 * x), y + c * (1 - 2 * y))
        second = (x + c * (1 - 2 * x), y + (1 - c) * (1 - 2 * y))
        diag = (1 - x, 1 - y)
        return x, y, c, me, (first, second, diag)

    def halves(w, c):
        rh = items[w].shape[0] // 2
        return pl.ds(c * rh, rh), pl.ds((1 - c) * rh, rh)

    def start(ins, outs, ss, rs):
        x, y, c, me, chips = place()
        for w in range(len(items)):
            _remote(ins[w], outs[w].at[me], ss.at[w, 6], rs.at[w, 6], (x, y, 1 - c)).start()
            if w < nw:
                mine, _ = halves(w, c)
                _remote(ins[w].at[mine], outs[w].at[me, mine], ss.at[w, 0], rs.at[w, 0], (*chips[0], c)).start()
            else:
                for k in range(3):
                    _remote(ins[w], outs[w].at[me], ss.at[w, k], rs.at[w, k], (*chips[k], c)).start()

    def onward(outs, ss, rs, w, k, x, y, c, chips):
        mine, _ = halves(w, c)
        pk = 2 * chips[k][0] + chips[k][1]
        got = outs[w].at[pk, mine]
        src = chips[1] if k == 2 else chips[k]
        _remote(got, got, ss.at[w, k], rs.at[w, k], (*src, c)).wait_recv()
        if k == 0:
            _remote(got, got, ss.at[w, 2], rs.at[w, 2], (*chips[1], c)).start()
        _remote(got, got, ss.at[w, 3 + k], rs.at[w, 3 + k], (x, y, 1 - c)).start()

    def middle(ins, outs, ss, rs):
        x, y, c, me, chips = place()
        for w in range(nw):
            mine, _ = halves(w, c)
            _remote(ins[w].at[mine], outs[w].at[me, mine], ss.at[w, 1], rs.at[w, 1], (*chips[1], c)).start()
        for w in range(nw):
            onward(outs, ss, rs, w, 0, x, y, c, chips)

    def finish(ins, outs, ss, rs):
        x, y, c, me, chips = place()
        sib = (x, y, 1 - c)
        for k in (1, 2):
            for w in range(nw):
                onward(outs, ss, rs, w, k, x, y, c, chips)
        for w in range(len(items)):
            if w < nw:
                mine, theirs = halves(w, c)
                for k, chip in ((3, chips[1]), (4, chips[0]), (5, chips[2])):
                    oth = outs[w].at[2 * chip[0] + chip[1], theirs]
                    _remote(oth, oth, ss.at[w, k], rs.at[w, k], sib).wait_recv()
                own = ins[w].at[mine]
                for k in range(6):
                    _remote(own, own, ss.at[w, k], rs.at[w, k], sib).wait_send()
            else:
                for k in range(3):
                    got = outs[w].at[2 * chips[k][0] + chips[k][1]]
                    _remote(got, got, ss.at[w, k], rs.at[w, k], (*chips[k], c)).wait_recv()
                    _remote(ins[w], ins[w], ss.at[w, k], rs.at[w, k], sib).wait_send()
            _remote(ins[w], outs[w].at[me], ss.at[w, 6], rs.at[w, 6], sib).wait()

    out_shapes = tuple(jax.ShapeDtypeStruct((N_CHIPS,) + a.shape, a.dtype) for a in items)
    return _Exchange(items, out_shapes, (len(items), 7), start, finish, middle if nw else None,
                     peers=("c", "x", "y", "xy") if small else ("c", "x", "y"))


def _chip_reduce(grads, name, collective_id):
    nw = len(grads)
    step = 64

    def body(*refs):
        ins, outs = refs[:nw], refs[nw:2 * nw]
        own, got = refs[2 * nw:3 * nw], refs[3 * nw:4 * nw]
        send_sems, recv_sems, local_sems = refs[4 * nw:]
        x, y, c, _, _ = _place()
        barrier = pltpu.get_barrier_semaphore()
        pl.semaphore_signal(barrier, inc=1, device_id=(x, y, 1 - c), device_id_type=MESH)
        pl.semaphore_wait(barrier, 1)
        moves = []
        for w in range(nw):
            nb, rh = grads[w].shape[0], grads[w].shape[1] // 2
            for k in range(nb):
                away = _remote(ins[w].at[k, pl.ds((1 - c) * rh, rh), :], got[w].at[k], send_sems.at[w, k],
                               recv_sems.at[w, k], (x, y, 1 - c))
                mine = pltpu.make_async_copy(ins[w].at[k, pl.ds(c * rh, rh), :], own[w].at[k], local_sems.at[w, k])
                away.start()
                mine.start()
                moves.append((w, k, away, mine))
        back = []
        for w, k, away, mine in moves:
            rh = own[w].shape[1]
            mine.wait()
            away.wait()

            def add(i, carry, w=w, k=k):
                rows = pl.ds(pl.multiple_of(i * step, step), step)
                own[w][k, rows, :] = (own[w][k, rows, :].astype(F32) + got[w][k, rows, :].astype(F32)).astype(BF16)
                return carry
            lax.fori_loop(0, rh // step, add, 0)
            tail = rh % step
            if tail:
                rows = slice(rh - tail, rh)
                own[w][k, rows, :] = (own[w][k, rows, :].astype(F32) + got[w][k, rows, :].astype(F32)).astype(BF16)
            wb = pltpu.make_async_copy(own[w].at[k], outs[w].at[k, pl.ds(c * rh, rh), :], local_sems.at[w, k])
            wb.start()
            back.append(wb)
        for wb in back:
            wb.wait()

    halves = [pltpu.VMEM((g.shape[0], g.shape[1] // 2, g.shape[2]), BF16) for g in grads]
    sem = pltpu.SemaphoreType.DMA((nw, N_CHIPS))
    return pl.pallas_call(
        body, name=name, out_shape=tuple(jax.ShapeDtypeStruct(g.shape, BF16) for g in grads),
        in_specs=[ANY] * nw, out_specs=[ANY] * nw, scratch_shapes=halves + halves + [sem, sem, sem],
        compiler_params=pltpu.CompilerParams(vmem_limit_bytes=VMEM_LIMIT, collective_id=collective_id),
    )(*grads)


def _scatter_partials(parts):
    nw = len(parts)

    def copies(ins, outs, ss, rs):
        _, _, c, _, chips = _place()
        res = []
        for r, (px, py) in enumerate(chips):
            for w in range(nw):
                rh = parts[w].shape[1] // 2
                rows = pl.ds(c * rh, rh)
                res.append(_remote(ins[w].at[2 * px + py, rows], outs[w].at[r, rows], ss.at[w, r], rs.at[w, r],
                                   (px, py, c)))
        return res

    def start(ins, outs, ss, rs):
        for cp in copies(ins, outs, ss, rs):
            cp.start()

    def finish(ins, outs, ss, rs):
        for cp in copies(ins, outs, ss, rs):
            cp.wait()

    out_shapes = tuple(jax.ShapeDtypeStruct((3,) + p.shape[1:], p.dtype) for p in parts)
    return _Exchange(parts, out_shapes, (nw, 3), start, finish, peers=("x", "y", "xy"))


_HBM = pl.BlockSpec(memory_space=pltpu.HBM)
_SEM = pl.BlockSpec(memory_space=pltpu.SEMAPHORE)
_EFFECT = pltpu.SideEffectType.DATAFLOW_SIDE_EFFECTING


def _scatter_copies(part_ref, land_ref, sems):
    _, _, c, _, chips = _place()
    rh = part_ref.shape[1] // 2
    rows = pl.ds(c * rh, rh)
    return [pltpu.make_async_remote_copy(src_ref=part_ref.at[2 * px + py, rows], dst_ref=land_ref.at[k, rows],
                                         send_sem=sems[k], recv_sem=sems[3 + k], device_id=(px, py, c),
                                         device_id_type=MESH)
            for k, (px, py) in enumerate(chips)]


def _scatter_start(part, collective_id):
    land = lax.empty((3,) + part.shape[1:], part.dtype)

    def body(part_ref, land_ref, *rest):
        _handshake(("x", "y", "xy"))
        for cp in _scatter_copies(part_ref, land_ref, rest[:6]):
            cp.start()
        rest[8][...] = jnp.zeros_like(rest[8])

    sem = pltpu.SemaphoreType.DMA(())
    res = pl.pallas_call(
        body, name="rs_scatter_in_start",
        out_shape=(sem,) * 6 + (pltpu.HBM(part.shape, part.dtype), pltpu.HBM(land.shape, land.dtype),
                                jax.ShapeDtypeStruct((8, 128), F32)),
        in_specs=(_HBM, _HBM), out_specs=(_SEM,) * 6 + (_HBM, _HBM, pl.BlockSpec(memory_space=pltpu.VMEM)),
        input_output_aliases={0: 6, 1: 7},
        compiler_params=pltpu.CompilerParams(has_side_effects=_EFFECT, collective_id=collective_id),
    )(pltpu.with_memory_space_constraint(part, pltpu.HBM), pltpu.with_memory_space_constraint(land, pltpu.HBM))
    return res[:6], res[6], res[7], res[8]


def _scatter_wait(sems, part_thru, land_thru, after):
    def body(part_ref, land_ref, *rest):
        for cp in _scatter_copies(part_ref, land_ref, rest[:6]):
            cp.wait_send()
            cp.wait_recv()

    return pl.pallas_call(
        body, name="rs_scatter_in_wait",
        out_shape=(pltpu.HBM(part_thru.shape, part_thru.dtype), pltpu.HBM(land_thru.shape, land_thru.dtype)),
        in_specs=(_HBM, _HBM) + (_SEM,) * 6 + (ANY,), out_specs=(_HBM, _HBM), input_output_aliases={0: 0, 1: 1},
        compiler_params=pltpu.CompilerParams(has_side_effects=_EFFECT),
    )(part_thru, land_thru, *sems, after)


def _join_partials(parts, slots):
    nw = len(parts)

    def copies(outs, ss, rs, mine):
        x, y, c, me, _ = _place()
        res = []
        for w in range(nw):
            rh = parts[w].shape[1] // 2
            rows = pl.ds((c if mine else 1 - c) * rh, rh)
            own = outs[w].at[me, rows]
            got = outs[nw + w].at[:, rows, :]
            res.append(_remote(own, own, ss.at[w, 0], rs.at[w, 0], (x, y, 1 - c)))
            res.append(_remote(got, got, ss.at[w, 1], rs.at[w, 1], (x, y, 1 - c)))
        return res

    def start(ins, outs, ss, rs):
        for cp in copies(outs, ss, rs, True):
            cp.start()

    def finish(ins, outs, ss, rs):
        for cp in copies(outs, ss, rs, True):
            cp.wait_send()
        for cp in copies(outs, ss, rs, False):
            cp.wait_recv()

    arrays = tuple(parts) + tuple(slots)
    return _Exchange(arrays, tuple(jax.ShapeDtypeStruct(a.shape, a.dtype) for a in arrays), (nw, 2), start, finish,
                     in_place=True, peers=("c",))


def _gather_small(slab):
    def copies(ins, outs, ss, rs):
        x, y, c, _, _ = _place()
        me = 4 * x + 2 * y + c
        out, arrivals = [], []
        for k in range(1, 8):
            px = 1 - x if k & 4 else x
            py = 1 - y if k & 2 else y
            pc = 1 - c if k & 1 else c
            out.append(_remote(ins[0], outs[0].at[me], ss.at[k - 1], rs.at[k - 1], (px, py, pc)))
            theirs = outs[0].at[4 * px + 2 * py + pc]
            arrivals.append((theirs, k - 1, (px, py, pc)))
        return pltpu.make_async_copy(ins[0], outs[0].at[me], ss.at[7]), out, arrivals

    def start(ins, outs, ss, rs):
        own, out, _ = copies(ins, outs, ss, rs)
        own.start()
        for cp in out:
            cp.start()

    def finish(ins, outs, ss, rs):
        own, out, arrivals = copies(ins, outs, ss, rs)
        for cp in out:
            cp.wait_send()
        for theirs, k, peer in arrivals:
            _remote(theirs, theirs, ss.at[k], rs.at[k], peer).wait_recv()
        own.wait()

    return _Exchange((slab,), (jax.ShapeDtypeStruct((8,) + slab.shape, slab.dtype),), (8,), start, finish)


_SMALL_VECS = ("ln_mix_g", "ln_attn_g", "ln_mem_g", "ln_ffn_g", "ln_final_g")


def _pack_small(p, extra, conv):
    d = p["ln_mix_g"].shape[-1]
    top = [p[k].reshape(1, d) for k in _SMALL_VECS]
    top.append(jnp.concatenate([p["sgu_ln_g"].reshape(-1), p["sgu_ln_b"].reshape(-1)]).reshape(1, d))
    top.append(jnp.concatenate([p["grp_norm_a"].reshape(-1), p["grp_norm_b"].reshape(-1)]).reshape(1, d))
    top.append(jnp.concatenate([p["b_spatial"].reshape(-1), extra]).reshape(1, d))
    mid = jnp.zeros((8, d), F32)
    if conv is not None:
        mid = jnp.pad(conv, ((0, 5), (0, d - conv.shape[1])))
    return jnp.concatenate([jnp.concatenate(top, axis=0), mid, p["w_spatial"].reshape(-1, d)], axis=0)


def _unpack_small(slab):
    d = slab.shape[1]
    hw = d // 2
    out = {k: slab[i] for i, k in enumerate(_SMALL_VECS)}
    out["sgu_ln_g"], out["sgu_ln_b"] = slab[5, :hw], slab[5, hw:]
    out["grp_norm_a"], out["grp_norm_b"] = slab[6, :hw], slab[6, hw:]
    out["b_spatial"] = slab[7, :hw].reshape(HEADS, CHUNK)
    out["w_spatial"] = slab[16:].reshape(HEADS, CHUNK, CHUNK)
    return out


_BIG = ("w_in", "w_kv", "w_gate_up", "w_out", "w_q", "w_o", "w_down")
_WEIGHTS = ("ln_mix_g", "w_in", "sgu_ln_g", "sgu_ln_b", "w_spatial", "b_spatial", "conv_w", "grp_norm_a",
            "grp_norm_b", "w_out", "ln_attn_g", "ln_mem_g", "w_q", "w_kv", "w_o", "ln_ffn_g", "w_gate_up",
            "w_down", "ln_final_g")


def _step(p, m_, v_, x, mem, target):
    s, d = x.shape
    hw = d // 2
    row = lambda a: a.reshape(1, -1)
    x_, y_, c_ = lax.axis_index("x"), lax.axis_index("y"), lax.axis_index("c")
    chip = 2 * x_ + y_

    conv8 = jnp.pad(p["conv_w"], ((0, 5), (0, 0)))
    later = ("w_kv", "w_q", "w_o", "w_down", "w_gate_up")
    first = _run_exchange(_all_gather([p["w_in"].astype(BF16), p["w_out"].astype(BF16)], [conv8]),
                          "all_gather_mixer", collective_id=4, casts=[p[k] for k in later])
    (w_in, w_out4, conv4), bf = first[:3], dict(zip(later, first[3:]))
    cw = jnp.transpose(conv4[:, :3, :], (1, 0, 2)).reshape(3, hw)
    b_t = jnp.pad(jnp.transpose(p["b_spatial"]), ((0, 0), (0, CHUNK - HEADS)))
    g1, g2, gm, g3, gf = (row(p[k]) for k in _SMALL_VECS)
    lng, lnb, ga, gb = row(p["sgu_ln_g"]), row(p["sgu_ln_b"]), row(p["grp_norm_a"]), row(p["grp_norm_b"])
    wsp = p["w_spatial"]
    w_out = w_out4.reshape(-1, d)

    (h, x1, ycat, xn1, mixed, th), (w_kv, w_q4, w_o4, w_down4) = _mixer_fwd(
        x, g1, w_in, lng, lnb, wsp, b_t, cw, ga, gb, w_out,
        hosted=_all_gather([bf[k] for k in ("w_kv", "w_q", "w_o", "w_down")]).with_id(5))
    w_q, w_o, w_down = (a.reshape(-1, d) for a in (w_q4, w_o4, w_down4))
    (x2, o, qs, probs, memn, kv), (w_gu,) = _attn_fwd(x1, mem, g2, gm, w_q, w_kv, w_o,
                                                      hosted=_all_gather([bf["w_gate_up"]]).with_id(6))
    dx2, act, dgu, xn3, dx3, loss, dgf, dg3 = _ffn_fwd_bwd(x2, g3, gf, target, w_gu, w_down)

    place = jnp.stack([c_, chip]).astype(jnp.int32)

    def chip_partials(names, grads, tag):
        return list(_chip_reduce(grads, "chip_reduce_" + tag, ("down", "ffn", "attn", "mixer").index(tag)))

    names_d = ("w_down",)
    parts_d = chip_partials(names_d, (_weight_grad(act, dx3, "grad_w_down", 1408, 512, False)[0],), "down")
    g_gu, slots_d = _weight_grad(xn3, dgu, "grad_w_gate_up", 512, 1408, True,
                                 hosted=_scatter_partials(parts_d).with_id(7))
    names_a = ("w_gate_up",)
    parts_a = chip_partials(names_a, (g_gu,), "ffn")
    (dx1, dkv, dg2, g_o, g_q, g_out), slots_a = _attn_bwd(x1, dx2, o, ycat, qs, probs, g2, w_q, kv, w_o,
                                                           hosted=_scatter_partials(parts_a).with_id(8))
    g_kv, dgm = _kv_bwd(dkv, mem, memn, gm, w_kv)
    names_b = ("w_o", "w_out", "w_q", "w_kv")
    shard_major = lambda g: g.reshape(N_CHIPS, -1, d)
    parts_b = chip_partials(names_b, (shard_major(g_o), shard_major(g_out), shard_major(g_q), g_kv), "attn")
    names_da = names_d + names_a
    (dx, dh, dg1, dlng, dlnb, dwsp, dbt, dcw, dga, dgb), extra = _mixer_bwd(
        x, dx1, h, mixed, th, g1, lng, lnb, wsp, b_t, cw, ga, gb, w_out, w_in,
        hosted=[_scatter_partials(parts_b).with_id(9), _join_partials(parts_d + parts_a, slots_d + slots_a)])
    slots_b, joined = extra[:len(names_b)], extra[len(names_b):]
    whole = dict(zip(names_da, zip(joined[:len(names_da)], joined[len(names_da):])))
    small = {"ln_mix_g": dg1, "ln_attn_g": dg2, "ln_mem_g": dgm, "ln_ffn_g": dg3, "ln_final_g": dgf,
             "sgu_ln_g": dlng, "sgu_ln_b": dlnb, "grp_norm_a": dga, "grp_norm_b": dgb,
             "b_spatial": jnp.transpose(dbt[:, :HEADS]), "w_spatial": dwsp}
    loss_vec = jnp.pad(loss.reshape(1), (0, hw - 1))
    g_in, extra = _weight_grad(
        xn1, dh, "grad_w_in", 1024, 640, True,
        hosted=[_gather_small(_pack_small(small, loss_vec, dcw)), _join_partials(parts_b, slots_b)])
    parts = extra[0]
    whole.update(zip(names_b, zip(extra[1:1 + len(names_b)], extra[1 + len(names_b):])))
    (part_in,) = chip_partials(("w_in",), (g_in,), "mixer")
    out_g, out_d, out_m, out_v = {}, {}, {}, {}

    def finalize(ks, tag, after=None):
        done, _ = _finalize([(whole[k][0], whole[k][1], p[k], m_[k], v_[k]) for k in ks], place,
                            "finalize_" + tag, after=after)
        for k, (g, dl, nm, nv) in zip(ks, done):
            out_g[k], out_d[k], out_m[k], out_v[k] = g, dl, nm, nv

    sems, part_thru, land_thru, token = _scatter_start(part_in, 10)
    finalize(("w_down",), "w_down", after=token)
    finalize(("w_o", "w_out", "w_q"), "attn", after=token)
    finalize(("w_gate_up",), "w_gate_up", after=token)
    finalize(("w_kv",), "w_kv", after=token)
    part_in, slots_in = _scatter_wait(sems, part_thru, land_thru, out_v["w_kv"])
    whole["w_in"] = _run_exchange(_join_partials([part_in], [slots_in]), "rs_join_mixer", collective_id=11)
    finalize(("w_in",), "w_in")

    zeros = jnp.zeros((hw,), F32)
    sg, sd, sm, sv = _small_sum_adamw(parts, _pack_small(p, zeros, None), _pack_small(m_, zeros, None),
                                      _pack_small(v_, zeros, None))
    for tree, slab in zip((out_g, out_d, out_m, out_v), (sg, sd, sm, sv)):
        tree.update(_unpack_small(slab))
    loss_out = sg[7, hw]
    g_conv = lax.dynamic_slice(sg[8:11, :hw], (0, chip * (hw // N_CHIPS)), (3, hw // N_CHIPS))
    out_g["conv_w"] = g_conv
    out_d["conv_w"], out_m["conv_w"], out_v["conv_w"] = _adamw(p["conv_w"], g_conv, m_["conv_w"], v_["conv_w"],
                                                                "adamw_conv_w")
    return loss_out, dx, out_g, out_d, out_m, out_v


def kernel(x, mem, ln_mix_g, w_in, sgu_ln_g, sgu_ln_b, w_spatial, b_spatial, conv_w, grp_norm_a, grp_norm_b, w_out, ln_attn_g, ln_mem_g, w_q, w_kv, w_o, ln_ffn_g, w_gate_up, w_down, ln_final_g, loss_target, m_ln_mix_g, m_w_in, m_sgu_ln_g, m_sgu_ln_b, m_w_spatial, m_b_spatial, m_conv_w, m_grp_norm_a, m_grp_norm_b, m_w_out, m_ln_attn_g, m_ln_mem_g, m_w_q, m_w_kv, m_w_o, m_ln_ffn_g, m_w_gate_up, m_w_down, m_ln_final_g, v_ln_mix_g, v_w_in, v_sgu_ln_g, v_sgu_ln_b, v_w_spatial, v_b_spatial, v_conv_w, v_grp_norm_a, v_grp_norm_b, v_w_out, v_ln_attn_g, v_ln_mem_g, v_w_q, v_w_kv, v_w_o, v_ln_ffn_g, v_w_gate_up, v_w_down, v_ln_final_g):
    p = dict(ln_mix_g=ln_mix_g, w_in=w_in, sgu_ln_g=sgu_ln_g, sgu_ln_b=sgu_ln_b, w_spatial=w_spatial,
             b_spatial=b_spatial, conv_w=conv_w, grp_norm_a=grp_norm_a, grp_norm_b=grp_norm_b, w_out=w_out,
             ln_attn_g=ln_attn_g, ln_mem_g=ln_mem_g, w_q=w_q, w_kv=w_kv, w_o=w_o, ln_ffn_g=ln_ffn_g,
             w_gate_up=w_gate_up, w_down=w_down, ln_final_g=ln_final_g)
    m_ = dict(ln_mix_g=m_ln_mix_g, w_in=m_w_in, sgu_ln_g=m_sgu_ln_g, sgu_ln_b=m_sgu_ln_b, w_spatial=m_w_spatial,
              b_spatial=m_b_spatial, conv_w=m_conv_w, grp_norm_a=m_grp_norm_a, grp_norm_b=m_grp_norm_b,
              w_out=m_w_out, ln_attn_g=m_ln_attn_g, ln_mem_g=m_ln_mem_g, w_q=m_w_q, w_kv=m_w_kv, w_o=m_w_o,
              ln_ffn_g=m_ln_ffn_g, w_gate_up=m_w_gate_up, w_down=m_w_down, ln_final_g=m_ln_final_g)
    v_ = dict(ln_mix_g=v_ln_mix_g, w_in=v_w_in, sgu_ln_g=v_sgu_ln_g, sgu_ln_b=v_sgu_ln_b, w_spatial=v_w_spatial,
              b_spatial=v_b_spatial, conv_w=v_conv_w, grp_norm_a=v_grp_norm_a, grp_norm_b=v_grp_norm_b,
              w_out=v_w_out, ln_attn_g=v_ln_attn_g, ln_mem_g=v_ln_mem_g, w_q=v_w_q, w_kv=v_w_kv, w_o=v_w_o,
              ln_ffn_g=v_ln_ffn_g, w_gate_up=v_w_gate_up, w_down=v_w_down, ln_final_g=v_ln_final_g)
    s, d = x.shape[-2], x.shape[-1]
    loss, dx, g, dl, nm, nv = _step(p, m_, v_, x.reshape(s, d), mem.reshape(-1, d), loss_target.reshape(s, d))
    outs = [loss, dx.reshape(x.shape)]
    for tree in (g, dl, nm, nv):
        outs += [tree[k].reshape(p[k].shape) for k in _WEIGHTS]
    return tuple(outs)
```

```python
import functools
import math

import jax
import jax.numpy as jnp
from jax import lax
from jax.experimental import pallas as pl
from jax.experimental.pallas import tpu as pltpu

F32 = jnp.float32
BF16 = jnp.bfloat16
EPS = 1e-6
CHUNK = 128
HEADS = 4
N_CHIPS = 4
TM = 512
TM_ATTN = 512
TM_FFN = 256
ADAM_LR, ADAM_B1, ADAM_B2, ADAM_EPS, ADAM_WD, ADAM_STEP = 0.001, 0.9, 0.999, 1e-08, 0.01, 10
GELU_C = math.sqrt(2.0 / math.pi)
GELU_K = 0.044715
SMALL_ROWS = 80
VMEM_LIMIT = 56 * 1024 * 1024
MIDDLE_STEP_16THS = 7
MESH = pl.DeviceIdType.MESH
ANY = pl.BlockSpec(memory_space=pl.ANY)


def _params(*sem, collective_id=None):
    return pltpu.CompilerParams(dimension_semantics=sem, vmem_limit_bytes=VMEM_LIMIT, collective_id=collective_id)


def _dot(a, b):
    return jnp.dot(a, b, preferred_element_type=F32)


def _dot_nt(a, b):
    return lax.dot_general(a, b, (((1,), (1,)), ((), ())), preferred_element_type=F32)


def _dot_tn(a, b):
    return lax.dot_general(a, b, (((0,), (0,)), ((), ())), preferred_element_type=F32)


def _rms_fwd(x, g):
    r = lax.rsqrt(jnp.mean(x * x, axis=-1, keepdims=True) + EPS)
    xh = x * r
    return xh * g, xh, r


def _rms_bwd(dy, xh, r, g):
    dxh = dy * g
    dx = r * (dxh - xh * jnp.mean(dxh * xh, axis=-1, keepdims=True))
    return dx, jnp.sum(dy * xh, axis=0, keepdims=True)


def _full(shape):
    nd = len(shape)
    return pl.BlockSpec(shape, lambda *_: (0,) * nd, pipeline_mode=pl.Buffered(1))


def _acc(shape):
    nd = len(shape)
    return pl.BlockSpec(shape, lambda *_: (0,) * nd)


def _rows(tm, cols):
    return pl.BlockSpec((tm, cols), lambda i: (i, 0))


def _tril_weights(wsp_ref):
    row = lax.broadcasted_iota(jnp.int32, (CHUNK, CHUNK), 0)
    col = lax.broadcasted_iota(jnp.int32, (CHUNK, CHUNK), 1)
    return [jnp.where(row >= col, wsp_ref[hd], 0.0).astype(BF16) for hd in range(HEADS)]


def _shift_rows(z, zp):
    row = lax.broadcasted_iota(jnp.int32, z.shape, 0)
    zm1 = jnp.where(row == 0, zp[7:8, :], pltpu.roll(z, 1, 0))
    zm2 = jnp.where(row == 0, zp[6:7, :], jnp.where(row == 1, zp[7:8, :], pltpu.roll(z, 2, 0)))
    return zm1, zm2


def _gelu_parts(x):
    t = jnp.tanh(GELU_C * (x + GELU_K * (x * x * x)))
    return 0.5 * x * (1.0 + t), t


def _layer_norm_parts(v, g, b):
    mu = jnp.mean(v, axis=-1, keepdims=True)
    vc = v - mu
    rs = lax.rsqrt(jnp.mean(vc * vc, axis=-1, keepdims=True) + EPS)
    vhat = vc * rs
    return vhat * g + b, vhat, rs


def _mixer_fwd(x, g1, w_in, lng, lnb, wsp, b_t, cw, ga, gb, w_out, hosted=None):
    s, d = x.shape
    n = s // TM
    nch = TM // CHUNK
    ns = w_in.shape[2]
    nh = N_CHIPS * ns
    aw = d // 2
    hd_w = aw // HEADS

    def body(x_ref, g1_ref, win_ref, lng_ref, lnb_ref, wsp_ref, bt_ref, cw_ref, ga_ref, gb_ref, wout_ref,
             h_ref, x1_ref, y_ref, xn_ref, mix_ref, th_ref, zp_ref):
        i = pl.program_id(0)

        @pl.when(i == 0)
        def _():
            zp_ref[...] = jnp.zeros_like(zp_ref)

        x = x_ref[...]
        xn, _, _ = _rms_fwd(x, g1_ref[...])
        xnb = xn.astype(BF16)
        xn_ref[...] = xnb
        for k in range(N_CHIPS):
            h_ref[:, k * ns:(k + 1) * ns] = _dot(xnb, win_ref[k])
        a, th = _gelu_parts(h_ref[:, 0:2 * aw])
        th_ref[...] = th
        u = a[:, :aw]
        vn, _, _ = _layer_norm_parts(a[:, aw:], lng_ref[...], lnb_ref[...])
        vnb = vn.astype(BF16)
        wm = _tril_weights(wsp_ref)
        for c in range(nch):
            for hd in range(HEADS):
                blk = vnb[c * CHUNK:(c + 1) * CHUNK, hd * hd_w:(hd + 1) * hd_w]
                mix_ref[c * CHUNK:(c + 1) * CHUNK, hd * hd_w:(hd + 1) * hd_w] = _dot(wm[hd], blk) + bt_ref[:, hd:hd + 1]
        ya, _, _ = _rms_fwd(u * mix_ref[...], ga_ref[...])
        g_b = h_ref[:, 2 * aw:3 * aw]
        z = h_ref[:, 3 * aw:4 * aw] * h_ref[:, 4 * aw:5 * aw]
        zm1, zm2 = _shift_rows(z, zp_ref[...])
        conv = cw_ref[0:1, :] * zm2 + cw_ref[1:2, :] * zm1 + cw_ref[2:3, :] * z
        yb, _, _ = _rms_fwd(g_b * conv, gb_ref[...])
        zp_ref[...] = z[TM - 8:TM, :]
        ycat = jnp.concatenate([ya, yb], axis=-1).astype(BF16)
        y_ref[...] = ycat
        x1_ref[...] = x + _dot(ycat, wout_ref[...])

    return _host_call(
        body, "mixer_fwd", (n,),
        [_rows(TM, d), _full(g1.shape), _full(w_in.shape), _full(lng.shape), _full(lnb.shape),
         _full(wsp.shape), _full(b_t.shape), _full(cw.shape), _full(ga.shape), _full(gb.shape),
         _full(w_out.shape)],
        [_rows(TM, nh), _rows(TM, d), _rows(TM, d), _rows(TM, d), _rows(TM, aw), _rows(TM, d)],
        (jax.ShapeDtypeStruct((s, nh), F32), jax.ShapeDtypeStruct((s, d), F32),
         jax.ShapeDtypeStruct((s, d), BF16), jax.ShapeDtypeStruct((s, d), BF16),
         jax.ShapeDtypeStruct((s, aw), F32), jax.ShapeDtypeStruct((s, d), F32)),
        [pltpu.VMEM((8, aw), F32)],
        (x, g1, w_in, lng, lnb, wsp, b_t, cw, ga, gb, w_out), ("arbitrary",), hosted)


def _attn_fwd(x1, mem, g2, g_mem, w_q, w_kv, w_o, hosted=None):
    s, d = x1.shape
    tm = min(TM_ATTN, s)
    n = s // tm
    dh = d // HEADS
    m = mem.shape[0]
    ns = w_kv.shape[2]
    scale = dh ** -0.5

    def body(x1_ref, mem_ref, g2_ref, gm_ref, wq_ref, wkv_ref, wo_ref, x2_ref, o_ref, q_ref, p_ref, memn_ref, kv_ref):
        @pl.when(pl.program_id(0) == 0)
        def _():
            y, _, _ = _rms_fwd(mem_ref[...], gm_ref[...])
            yb = y.astype(BF16)
            memn_ref[...] = yb
            for k in range(N_CHIPS):
                kv_ref[:, k * ns:(k + 1) * ns] = _dot(yb, wkv_ref[k]).astype(BF16)

        x1v = x1_ref[...]
        xn, _, _ = _rms_fwd(x1v, g2_ref[...])
        q_ref[...] = _dot(xn.astype(BF16), wq_ref[...]).astype(BF16)
        for hd in range(HEADS):
            kh = kv_ref[:, hd * dh:(hd + 1) * dh]
            vh = kv_ref[:, d + hd * dh:d + (hd + 1) * dh]
            sc = _dot_nt(q_ref[:, hd * dh:(hd + 1) * dh], kh) * scale
            e = jnp.exp(sc - jnp.max(sc, axis=-1, keepdims=True))
            p = e / jnp.sum(e, axis=-1, keepdims=True)
            p_ref[:, hd * m:(hd + 1) * m] = p
            o_ref[:, hd * dh:(hd + 1) * dh] = _dot(p.astype(BF16), vh).astype(BF16)
        x2_ref[...] = x1v + _dot(o_ref[...], wo_ref[...])

    return _host_call(
        body, "attn_fwd", (n,),
        [_rows(tm, d), _full(mem.shape), _full(g2.shape), _full(g_mem.shape), _full(w_q.shape), _full(w_kv.shape),
         _full(w_o.shape)],
        [_rows(tm, d), _rows(tm, d), _rows(tm, d), _rows(tm, HEADS * m), _acc((m, d)), _acc((m, 2 * d))],
        (jax.ShapeDtypeStruct((s, d), F32), jax.ShapeDtypeStruct((s, d), BF16), jax.ShapeDtypeStruct((s, d), BF16),
         jax.ShapeDtypeStruct((s, HEADS * m), F32), jax.ShapeDtypeStruct((m, d), BF16),
         jax.ShapeDtypeStruct((m, 2 * d), BF16)),
        [], (x1, mem, g2, g_mem, w_q, w_kv, w_o), ("arbitrary",), hosted)


def _ffn_fwd_bwd(x2, g3, gf, target, w_gu, w_down):
    s, d = x2.shape
    tm = min(TM_FFN, s)
    n = s // tm
    ns = w_gu.shape[2]
    ff = 2 * ns

    def body(x2_ref, g3_ref, gf_ref, t_ref, wgu_ref, wd_ref,
             dx2_ref, act_ref, dgu_ref, xn_ref, dx3_ref, loss_ref, dgf_ref, dg3_ref):
        i = pl.program_id(0)

        @pl.when(i == 0)
        def _():
            loss_ref[...] = jnp.zeros_like(loss_ref)
            dgf_ref[...] = jnp.zeros_like(dgf_ref)
            dg3_ref[...] = jnp.zeros_like(dg3_ref)

        x2v = x2_ref[...]
        xn, xh3, r3 = _rms_fwd(x2v, g3_ref[...])
        xnb = xn.astype(BF16)
        xn_ref[...] = xnb
        x3 = x2v
        saved = []
        for j in range(2):
            g = _dot(xnb, wgu_ref[j])
            u = _dot(xnb, wgu_ref[2 + j])
            sg = 1.0 / (1.0 + jnp.exp(-g))
            sl = g * sg
            actb = (sl * u).astype(BF16)
            act_ref[:, j * ns:(j + 1) * ns] = actb
            x3 = x3 + _dot(actb, wd_ref[j * ns:(j + 1) * ns, :])
            saved.append((u, sl, sg * (1.0 + g * (1.0 - sg))))
        gfv = gf_ref[...]
        y, xhf, rf = _rms_fwd(x3, gfv)
        e = y - t_ref[...]
        loss_ref[...] += 0.5 * jnp.sum(jnp.sum(e * e, axis=-1, keepdims=True), axis=0, keepdims=True) / d
        dx3, dgf = _rms_bwd(e / d, xhf, rf, gfv)
        dgf_ref[...] += dgf
        dx3b = dx3.astype(BF16)
        dx3_ref[...] = dx3b
        dxn = jnp.zeros_like(x2v)
        for j in range(2):
            u, sl, dsl = saved[j]
            dact = _dot_nt(dx3b, wd_ref[j * ns:(j + 1) * ns, :])
            dgb = (dact * u * dsl).astype(BF16)
            dub = (dact * sl).astype(BF16)
            dgu_ref[:, j * ns:(j + 1) * ns] = dgb
            dgu_ref[:, ff + j * ns:ff + (j + 1) * ns] = dub
            dxn = dxn + _dot_nt(dgb, wgu_ref[j]) + _dot_nt(dub, wgu_ref[2 + j])
        dxr, dg3 = _rms_bwd(dxn, xh3, r3, g3_ref[...])
        dg3_ref[...] += dg3
        dx2_ref[...] = dx3 + dxr

    vec = jax.ShapeDtypeStruct((1, d), F32)
    return pl.pallas_call(
        body, name="ffn_fwd_bwd", grid=(n,),
        in_specs=[_rows(tm, d), _full(g3.shape), _full(gf.shape), _rows(tm, d), _full(w_gu.shape),
                  _full(w_down.shape)],
        out_specs=[_rows(tm, d), _rows(tm, ff), _rows(tm, 2 * ff), _rows(tm, d), _rows(tm, d),
                   _acc((1, 1)), _acc((1, d)), _acc((1, d))],
        out_shape=(jax.ShapeDtypeStruct((s, d), F32), jax.ShapeDtypeStruct((s, ff), BF16),
                   jax.ShapeDtypeStruct((s, 2 * ff), BF16), jax.ShapeDtypeStruct((s, d), BF16),
                   jax.ShapeDtypeStruct((s, d), BF16), jax.ShapeDtypeStruct((1, 1), F32), vec, vec),
        compiler_params=_params("arbitrary"),
    )(x2, g3, gf, target, w_gu, w_down)


def _attn_bwd(x1, dx2, o, ycat, qs, probs, g2, w_q, kv, w_o, hosted=None):
    s, d = x1.shape
    tm = min(TM_ATTN, s)
    n = s // tm
    dh = d // HEADS
    scale = dh ** -0.5
    m = kv.shape[0]

    def body(x1_ref, dx2_ref, o_ref, y_ref, q_ref, p_ref, g2_ref, wq_ref, kv_ref, wo_ref,
             dx1_ref, dkv_ref, dg2_ref, gwo_out, gwq_out, gwout_out, dq_ref, gwo_ref, gwq_ref, gwout_ref):
        i = pl.program_id(0)

        @pl.when(i == 0)
        def _():
            for r in (dkv_ref, dg2_ref, gwo_ref, gwq_ref, gwout_ref):
                r[...] = jnp.zeros_like(r)

        xn, xh2, r2 = _rms_fwd(x1_ref[...], g2_ref[...])
        xnb = xn.astype(BF16)
        dx2v = dx2_ref[...]
        dx2b = dx2v.astype(BF16)
        gwo_ref[...] += _dot_tn(o_ref[...], dx2b)
        do = _dot_nt(dx2b, wo_ref[...])
        for hd in range(HEADS):
            qb = q_ref[:, hd * dh:(hd + 1) * dh]
            p = p_ref[:, hd * m:(hd + 1) * m]
            kh = kv_ref[:, hd * dh:(hd + 1) * dh]
            vh = kv_ref[:, d + hd * dh:d + (hd + 1) * dh]
            dob = do[:, hd * dh:(hd + 1) * dh].astype(BF16)
            dp = _dot_nt(dob, vh)
            ds = p * (dp - jnp.sum(dp * p, axis=-1, keepdims=True))
            dsb = (ds * scale).astype(BF16)
            dq_ref[:, hd * dh:(hd + 1) * dh] = _dot(dsb, kh).astype(BF16)
            dkv_ref[:, hd * dh:(hd + 1) * dh] += _dot_tn(dsb, qb)
            dkv_ref[:, d + hd * dh:d + (hd + 1) * dh] += _dot_tn(p.astype(BF16), dob)
        dqb = dq_ref[...]
        gwq_ref[...] += _dot_tn(xnb, dqb)
        dxn = _dot_nt(dqb, wq_ref[...])
        dxr, dg2 = _rms_bwd(dxn, xh2, r2, g2_ref[...])
        dg2_ref[...] += dg2
        dx1 = dx2v + dxr
        dx1_ref[...] = dx1
        gwout_ref[...] += _dot_tn(y_ref[...], dx1.astype(BF16))

        @pl.when(i == n - 1)
        def _():
            for acc, out in ((gwo_ref, gwo_out), (gwq_ref, gwq_out), (gwout_ref, gwout_out)):
                out[...] = acc[...].astype(BF16)

    sq = jax.ShapeDtypeStruct((d, d), BF16)
    return _host_call(
        body, "attn_bwd", (n,),
        [_rows(tm, d), _rows(tm, d), _rows(tm, d), _rows(tm, d), _rows(tm, d), _rows(tm, HEADS * m),
         _full(g2.shape), _full(w_q.shape), _full(kv.shape), _full(w_o.shape)],
        [_rows(tm, d), _acc((m, 2 * d)), _acc((1, d)), _acc((d, d)), _acc((d, d)), _acc((d, d))],
        (jax.ShapeDtypeStruct((s, d), F32), jax.ShapeDtypeStruct((m, 2 * d), F32),
         jax.ShapeDtypeStruct((1, d), F32), sq, sq, sq),
        [pltpu.VMEM((tm, d), BF16)] + [pltpu.VMEM((d, d), F32)] * 3,
        (x1, dx2, o, ycat, qs, probs, g2, w_q, kv, w_o), ("arbitrary",), hosted)


def _kv_bwd(dkv, mem, memn, g_mem, w_kv):
    m, d = mem.shape
    ns = w_kv.shape[2]

    def body(dkv_ref, mem_ref, memn_ref, g_ref, w_ref, gw_ref, dg_ref):
        _, xh, _ = _rms_fwd(mem_ref[...], g_ref[...])
        dmemn = jnp.zeros((m, d), F32)
        for k in range(N_CHIPS):
            dkb = dkv_ref[:, k * ns:(k + 1) * ns].astype(BF16)
            gw_ref[k] = _dot_tn(memn_ref[...], dkb).astype(BF16)
            dmemn = dmemn + _dot_nt(dkb, w_ref[k])
        dg_ref[...] = jnp.sum(dmemn * xh, axis=0, keepdims=True)

    return pl.pallas_call(
        body, name="kv_bwd",
        out_shape=(jax.ShapeDtypeStruct((N_CHIPS, d, ns), BF16), jax.ShapeDtypeStruct((1, d), F32)),
        compiler_params=pltpu.CompilerParams(vmem_limit_bytes=VMEM_LIMIT),
    )(dkv, mem, memn, g_mem, w_kv)


def _mixer_bwd(x, dx1, h, mixed_all, th_all, g1, lng, lnb, wsp, b_t, cw, ga, gb, w_out, w_in, hosted=None):
    s, d = x.shape
    n = s // TM
    nch = TM // CHUNK
    ns = w_in.shape[2]
    nh = N_CHIPS * ns
    aw = d // 2
    hd_w = aw // HEADS

    def rev(cols):
        return pl.BlockSpec((TM, cols), lambda i: (n - 1 - i, 0))

    hprev = pl.BlockSpec((8, nh), lambda i: (jnp.maximum((n - 1 - i) * (TM // 8) - 1, 0), 0))

    def body(x_ref, dx1_ref, h_ref, hp_ref, mix_ref, th_ref, g1_ref, lng_ref, lnb_ref, wsp_ref, bt_ref, cw_ref,
             ga_ref, gb_ref, wout_ref, win_ref,
             dx_ref, dh_ref, dg1_ref, dlng_ref, dlnb_ref, dwsp_ref, dbt_ref, dcw_ref, dga_ref, dgb_ref,
             dvn_ref, dcn_ref):
        i = pl.program_id(0)

        @pl.when(i == 0)
        def _():
            for r in (dg1_ref, dlng_ref, dlnb_ref, dwsp_ref, dbt_ref, dcw_ref, dga_ref, dgb_ref, dcn_ref):
                r[...] = jnp.zeros_like(r)

        dx1v = dx1_ref[...]
        dycat = _dot_nt(dx1v.astype(BF16), wout_ref[...])
        ha = h_ref[:, 0:2 * aw]
        th = th_ref[...]
        a = 0.5 * ha * (1.0 + th)
        u = a[:, :aw]
        lngv = lng_ref[...]
        vn, vhat, rs = _layer_norm_parts(a[:, aw:], lngv, lnb_ref[...])
        vnb = vn.astype(BF16)
        wm = _tril_weights(wsp_ref)
        mixed = mix_ref[...]
        gav = ga_ref[...]
        _, yah, ra = _rms_fwd(u * mixed, gav)
        dya, dga = _rms_bwd(dycat[:, :aw], yah, ra, gav)
        dga_ref[...] += dga
        du = dya * mixed
        dmix = dya * u
        dmb = dmix.astype(BF16)
        tri = lax.broadcasted_iota(jnp.int32, (CHUNK, CHUNK), 0) >= lax.broadcasted_iota(jnp.int32, (CHUNK, CHUNK), 1)
        for hd in range(HEADS):
            dw = jnp.zeros((CHUNK, CHUNK), F32)
            db = jnp.zeros((CHUNK, 1), F32)
            for c in range(nch):
                rows = slice(c * CHUNK, (c + 1) * CHUNK)
                cols = slice(hd * hd_w, (hd + 1) * hd_w)
                dvn_ref[rows, cols] = _dot_tn(wm[hd], dmb[rows, cols])
                dw = dw + _dot_nt(dmb[rows, cols], vnb[rows, cols])
                db = db + jnp.sum(dmix[rows, cols], axis=1, keepdims=True)
            dwsp_ref[hd] += jnp.where(tri, dw, 0.0)
            dbt_ref[:, hd:hd + 1] += db
        dvn = dvn_ref[...]
        dlng_ref[...] += jnp.sum(dvn * vhat, axis=0, keepdims=True)
        dlnb_ref[...] += jnp.sum(dvn, axis=0, keepdims=True)
        dvh = dvn * lngv
        dv = rs * (dvh - jnp.mean(dvh, axis=-1, keepdims=True) - vhat * jnp.mean(dvh * vhat, axis=-1, keepdims=True))
        gprime = 0.5 * (1.0 + th) + 0.5 * ha * (1.0 - th * th) * (GELU_C * (1.0 + 3.0 * GELU_K * (ha * ha)))
        dh_ref[:, 0:2 * aw] = (jnp.concatenate([du, dv], axis=-1) * gprime).astype(BF16)
        g_b = h_ref[:, 2 * aw:3 * aw]
        g_c = h_ref[:, 3 * aw:4 * aw]
        val = h_ref[:, 4 * aw:5 * aw]
        z = g_c * val
        zp = jnp.where(i == n - 1, 0.0, hp_ref[:, 3 * aw:4 * aw] * hp_ref[:, 4 * aw:5 * aw])
        zm1, zm2 = _shift_rows(z, zp)
        cw0, cw1, cw2 = cw_ref[0:1, :], cw_ref[1:2, :], cw_ref[2:3, :]
        conv = cw0 * zm2 + cw1 * zm1 + cw2 * z
        gbv = gb_ref[...]
        _, ybh, rb = _rms_fwd(g_b * conv, gbv)
        dyb, dgb = _rms_bwd(dycat[:, aw:], ybh, rb, gbv)
        dgb_ref[...] += dgb
        dconv = dyb * g_b
        dcw_ref[0:1, :] += jnp.sum(dconv * zm2, axis=0, keepdims=True)
        dcw_ref[1:2, :] += jnp.sum(dconv * zm1, axis=0, keepdims=True)
        dcw_ref[2:3, :] += jnp.sum(dconv * z, axis=0, keepdims=True)
        nxt = dcn_ref[...]
        row = lax.broadcasted_iota(jnp.int32, dconv.shape, 0)
        dcp1 = jnp.where(row == TM - 1, nxt[0:1, :], pltpu.roll(dconv, TM - 1, 0))
        dcp2 = jnp.where(row == TM - 1, nxt[1:2, :],
                         jnp.where(row == TM - 2, nxt[0:1, :], pltpu.roll(dconv, TM - 2, 0)))
        dz = cw2 * dconv + cw1 * dcp1 + cw0 * dcp2
        dcn_ref[...] = dconv[0:8, :]
        dh_ref[:, 2 * aw:3 * aw] = (dyb * conv).astype(BF16)
        dh_ref[:, 3 * aw:4 * aw] = (dz * val).astype(BF16)
        dh_ref[:, 4 * aw:5 * aw] = (dz * g_c).astype(BF16)
        dxn = jnp.zeros((TM, d), F32)
        for k in range(N_CHIPS):
            dxn = dxn + _dot_nt(dh_ref[:, k * ns:(k + 1) * ns], win_ref[k])
        g1v = g1_ref[...]
        _, xh1, r1 = _rms_fwd(x_ref[...], g1v)
        dxr, dg1 = _rms_bwd(dxn, xh1, r1, g1v)
        dg1_ref[...] += dg1
        dx_ref[...] = dx1v + dxr

    ins = (x, dx1, h, h, mixed_all, th_all, g1, lng, lnb, wsp, b_t, cw, ga, gb, w_out, w_in)
    acc_shapes = [(1, d), (1, aw), (1, aw), wsp.shape, (CHUNK, CHUNK), cw.shape, (1, aw), (1, aw)]
    return _host_call(
        body, "mixer_bwd", (n,),
        [rev(d), rev(d), rev(nh), hprev, rev(aw), rev(d)] + [_full(a.shape) for a in ins[6:]],
        [rev(d), rev(nh)] + [_acc(sh) for sh in acc_shapes],
        (jax.ShapeDtypeStruct((s, d), F32), jax.ShapeDtypeStruct((s, nh), BF16))
        + tuple(jax.ShapeDtypeStruct(sh, F32) for sh in acc_shapes),
        [pltpu.VMEM((TM, aw), F32), pltpu.VMEM((8, aw), F32)],
        ins, ("arbitrary",), hosted)


def _weight_grad(a, b, name, tm, tn, col_sharded, hosted=None, after=None):
    t, m = a.shape
    n = b.shape[1]
    extra_in = [] if after is None else [after]

    def body(a_ref, b_ref, *rest):
        rest[-1][...] = _dot_tn(a_ref[...].astype(BF16), b_ref[...].astype(BF16)).astype(BF16)

    if col_sharded:
        ns = n // N_CHIPS
        per = ns // tn
        out_shape = jax.ShapeDtypeStruct((N_CHIPS, m, ns), BF16)
        out_spec = pl.BlockSpec((None, tm, tn), lambda i, j: (j // per, i, j % per))
    else:
        out_shape = jax.ShapeDtypeStruct((m, n), BF16)
        out_spec = pl.BlockSpec((tm, tn), lambda i, j: (i, j))
    (out,), extra = _host_call(
        body, name, (m // tm, n // tn),
        [pl.BlockSpec((t, tm), lambda i, j: (0, i)), pl.BlockSpec((t, tn), lambda i, j: (0, j))] + [ANY] * len(extra_in),
        [out_spec], (out_shape,), [], (a, b, *extra_in), ("parallel", "parallel"), hosted)
    return (out if col_sharded else out.reshape(N_CHIPS, m // N_CHIPS, n)), extra


def _row_tile(rows, cap=256):
    best = None
    for t in range(16, min(rows, cap) + 1, 16):
        if rows % t == 0:
            best = t
    return best if best is not None else rows


def _adamw_math(w, g, m, v):
    m2 = ADAM_B1 * m + (1.0 - ADAM_B1) * g
    v2 = ADAM_B2 * v + (1.0 - ADAM_B2) * (g * g)
    m_hat = m2 / (1.0 - ADAM_B1 ** ADAM_STEP)
    v_hat = v2 / (1.0 - ADAM_B2 ** ADAM_STEP)
    delta = -ADAM_LR * (m_hat / (jnp.sqrt(v_hat) + ADAM_EPS) + ADAM_WD * w)
    return delta, m2, v2


def _adamw(w, g, m, v, name):
    r, c = w.shape
    tr = _row_tile(r) if r >= 16 else r

    def body(w_ref, g_ref, m_ref, v_ref, d_ref, m2_ref, v2_ref):
        d_ref[...], m2_ref[...], v2_ref[...] = _adamw_math(w_ref[...], g_ref[...], m_ref[...], v_ref[...])

    sh = jax.ShapeDtypeStruct((r, c), F32)
    return pl.pallas_call(
        body, name=name, grid=(r // tr,),
        in_specs=[_rows(tr, c)] * 4, out_specs=[_rows(tr, c)] * 3, out_shape=(sh, sh, sh),
        compiler_params=_params("parallel"),
    )(w, g, m, v)


def _finalize(items, place, name, hosted=None, after=None):
    r, c = items[0][2].shape
    tr = _row_tile(r)
    nw = len(items)

    n_after = 0 if after is None else 1

    def body(place_ref, *refs):
        ins, outs = refs[:7 * nw], refs[7 * nw + n_after:]
        for k in range(nw):
            own_ref, s0_ref, s1_ref, s2_ref, w_ref, m_ref, v_ref = ins[7 * k:7 * k + 7]
            g_ref, d_ref, m2_ref, v2_ref = outs[4 * k:4 * k + 4]
            g = own_ref[...].astype(F32) + s0_ref[...].astype(F32)
            g = (g + s1_ref[...].astype(F32)) + s2_ref[...].astype(F32)
            g_ref[...] = g
            d_ref[...], m2_ref[...], v2_ref[...] = _adamw_math(w_ref[...], g, m_ref[...], v_ref[...])

    def slot(k):
        return pl.BlockSpec((None, tr, c), lambda i, pref: (k, i, 0))

    rows = pl.BlockSpec((tr, c), lambda i, pref: (i, 0))
    sh = jax.ShapeDtypeStruct((r, c), F32)
    one = [pl.BlockSpec((None, tr, c), lambda i, pref: (pref[1], i, 0)), slot(0), slot(1), slot(2), rows, rows, rows]
    args = [a for part, slots, w, m, v in items for a in (part, slots, slots, slots, w, m, v)]
    args += [] if after is None else [after]
    res, extra = _host_call(body, name, (r // tr,), one * nw + [ANY] * n_after, [rows] * (4 * nw), (sh,) * (4 * nw),
                            [], args, ("parallel",), hosted, prefetch=(place,))
    return [res[4 * k:4 * k + 4] for k in range(nw)], extra


def _small_sum_adamw(parts, own, me, w, m, v):
    nd, r, c = parts.shape

    def body(me_ref, p_ref, own_ref, w_ref, m_ref, v_ref, g_ref, d_ref, m2_ref, v2_ref):
        g_ref[...] = jnp.zeros((r, c), F32)
        for k in range(nd):
            @pl.when(me_ref[0] == k)
            def _():
                g_ref[...] += own_ref[...]

            @pl.when(me_ref[0] != k)
            def _(k=k):
                g_ref[...] += p_ref[k]
        d_ref[...], m2_ref[...], v2_ref[...] = _adamw_math(w_ref[...], g_ref[...], m_ref[...], v_ref[...])

    sh = jax.ShapeDtypeStruct((r, c), F32)
    slab = pl.BlockSpec((r, c), lambda i, pref: (0, 0))
    return pl.pallas_call(
        body, name="small_sum_adamw", out_shape=(sh, sh, sh, sh),
        grid_spec=pltpu.PrefetchScalarGridSpec(
            num_scalar_prefetch=1, grid=(1,),
            in_specs=[pl.BlockSpec((nd, r, c), lambda i, pref: (0, 0, 0)), slab, slab, slab, slab],
            out_specs=[slab] * 4),
        compiler_params=pltpu.CompilerParams(vmem_limit_bytes=VMEM_LIMIT),
    )(me, parts, own, w, m, v)


def _place():
    x, y, c = lax.axis_index("x"), lax.axis_index("y"), lax.axis_index("c")
    chips = [(1 - x, y), (x, 1 - y), (1 - x, 1 - y)]
    return x, y, c, 2 * x + y, chips


def _remote(src, dst, send_sem, recv_sem, to):
    return pltpu.make_async_remote_copy(src_ref=src, dst_ref=dst, send_sem=send_sem, recv_sem=recv_sem,
                                        device_id=to, device_id_type=MESH)


class _Exchange:
    def __init__(self, ins, out_shapes, sem_shape, start, finish, middle=None, in_place=False, peers=()):
        self.ins, self.out_shapes, self.sem_shape = tuple(ins), tuple(out_shapes), sem_shape
        self.start, self.finish, self.middle = start, finish, middle
        self.in_place = in_place
        self.peers = frozenset(peers)
        self.collective_id = None

    def with_id(self, collective_id):
        self.collective_id = collective_id
        return self


_FLIPS = {"c": (0, 0, 1), "x": (1, 0, 0), "y": (0, 1, 0), "xy": (1, 1, 0),
          "xc": (1, 0, 1), "yc": (0, 1, 1), "xyc": (1, 1, 1)}


def _handshake(peers):
    x, y, c, _, _ = _place()
    barrier = pltpu.get_barrier_semaphore()
    for name in sorted(peers):
        fx, fy, fc = _FLIPS[name]
        peer = (1 - x if fx else x, 1 - y if fy else y, 1 - c if fc else c)
        pl.semaphore_signal(barrier, inc=1, device_id=peer, device_id_type=MESH)
    pl.semaphore_wait(barrier, len(peers))


def _run_exchange(ex, name, collective_id=None, casts=()):
    n_in, n_out, n_cast = len(ex.ins), len(ex.out_shapes), len(casts)
    step = 64

    def body(*refs):
        ins, srcs = refs[:n_in], refs[n_in:n_in + n_cast]
        outs = refs[n_in + n_cast:n_in + n_cast + n_out]
        dsts = refs[n_in + n_cast + n_out:n_in + 2 * n_cast + n_out]
        rest = refs[n_in + 2 * n_cast + n_out:]
        wide, narrow = rest[:n_cast], rest[n_cast:2 * n_cast]
        send_sems, recv_sems = rest[2 * n_cast], rest[2 * n_cast + 1]
        if collective_id is not None:
            _handshake(ex.peers)
        ex.start(ins, outs, send_sems, recv_sems)
        loads = [pltpu.make_async_copy(srcs[k], wide[k], rest[2 * n_cast + 2].at[k]) for k in range(n_cast)]
        for cp in loads:
            cp.start()
        if ex.middle is not None:
            ex.middle(ins, outs, send_sems, recv_sems)
        stores = []
        for k in range(n_cast):
            loads[k].wait()

            def chunk(i, carry, k=k):
                rows = pl.ds(pl.multiple_of(i * step, step), step)
                narrow[k][rows, :] = wide[k][rows, :].astype(BF16)
                return carry
            lax.fori_loop(0, casts[k].shape[0] // step, chunk, 0)
            stores.append(pltpu.make_async_copy(narrow[k], dsts[k], rest[2 * n_cast + 2].at[k]))
            stores[-1].start()
        ex.finish(ins, outs, send_sems, recv_sems)
        for cp in stores:
            cp.wait()

    assert collective_id is None or ex.peers
    sem = pltpu.SemaphoreType.DMA(ex.sem_shape)
    scratch = ([pltpu.VMEM(a.shape, F32) for a in casts] + [pltpu.VMEM(a.shape, BF16) for a in casts] + [sem, sem]
               + ([pltpu.SemaphoreType.DMA((n_cast,))] if n_cast else []))
    return pl.pallas_call(
        body, name=name,
        out_shape=ex.out_shapes + tuple(jax.ShapeDtypeStruct(a.shape, BF16) for a in casts),
        in_specs=[ANY] * (n_in + n_cast), out_specs=[ANY] * (n_out + n_cast),
        input_output_aliases={k: k for k in range(n_in)} if ex.in_place else {}, scratch_shapes=scratch,
        compiler_params=pltpu.CompilerParams(collective_id=collective_id, vmem_limit_bytes=VMEM_LIMIT),
    )(*ex.ins, *casts)


def _host_call(body, name, grid, in_specs, out_specs, out_shape, scratch_shapes, args, semantics, hosted,
               prefetch=()):
    hosted = [] if hosted is None else (list(hosted) if isinstance(hosted, (list, tuple)) else [hosted])
    collective_id = hosted[0].collective_id if hosted else None
    peers = frozenset().union(*[ex.peers for ex in hosted]) if hosted else frozenset()
    assert collective_id is None or all(ex.peers for ex in hosted)
    n_pre, n_in, n_out, n_scr = len(prefetch), len(in_specs), len(out_specs), len(scratch_shapes)
    h_ins = [a for ex in hosted for a in ex.ins]
    h_outs = [s for ex in hosted for s in ex.out_shapes]
    h_in, h_out = len(h_ins), len(h_outs)

    def wrapped(*refs):
        pre, refs = refs[:n_pre], refs[n_pre:]
        a, hi = refs[:n_in], refs[n_in:n_in + h_in]
        o = refs[n_in + h_in:n_in + h_in + n_out]
        ho = refs[n_in + h_in + n_out:n_in + h_in + n_out + h_out]
        scr = refs[n_in + h_in + n_out + h_out:]

        def run(phase):
            i0 = o0 = 0
            for k, ex in enumerate(hosted):
                fn = getattr(ex, phase)
                if fn is not None:
                    fn(hi[i0:i0 + len(ex.ins)], ho[o0:o0 + len(ex.out_shapes)], scr[n_scr + 2 * k],
                       scr[n_scr + 2 * k + 1])
                i0, o0 = i0 + len(ex.ins), o0 + len(ex.out_shapes)

        if hosted:
            first = functools.reduce(jnp.logical_and, [pl.program_id(k) == 0 for k in range(len(grid))])

            @pl.when(first)
            def _():
                if collective_id is not None:
                    _handshake(peers)
                run("start")

        if any(ex.middle is not None for ex in hosted):
            half_way = functools.reduce(jnp.logical_and, [
                pl.program_id(0) == max(1, grid[0] * MIDDLE_STEP_16THS // 16)] + [
                pl.program_id(k) == 0 for k in range(1, len(grid))])

            @pl.when(half_way)
            def _():
                run("middle")

        body(*pre, *a, *o, *scr[:n_scr])

        if hosted:
            last = functools.reduce(jnp.logical_and, [pl.program_id(k) == grid[k] - 1 for k in range(len(grid))])

            @pl.when(last)
            def _():
                run("finish")

    sems = [pltpu.SemaphoreType.DMA(ex.sem_shape) for ex in hosted for _ in range(2)]
    aliases, i0, o0 = {}, n_pre + n_in, n_out
    for ex in hosted:
        if ex.in_place:
            aliases.update({i0 + k: o0 + k for k in range(len(ex.ins))})
        i0, o0 = i0 + len(ex.ins), o0 + len(ex.out_shapes)
    all_in, all_out = list(in_specs) + [ANY] * h_in, list(out_specs) + [ANY] * h_out
    all_scr = list(scratch_shapes) + sems
    params = _params(*(["arbitrary"] * len(grid) if hosted else semantics), collective_id=collective_id)
    shapes = tuple(out_shape) + tuple(h_outs)
    if n_pre:
        call = pl.pallas_call(
            wrapped, name=name, out_shape=shapes, input_output_aliases=aliases, compiler_params=params,
            grid_spec=pltpu.PrefetchScalarGridSpec(num_scalar_prefetch=n_pre, grid=grid, in_specs=all_in,
                                                   out_specs=all_out, scratch_shapes=all_scr))
    else:
        call = pl.pallas_call(
            wrapped, name=name, grid=grid, in_specs=all_in, out_specs=all_out, out_shape=shapes,
            scratch_shapes=all_scr, input_output_aliases=aliases, compiler_params=params)
    res = call(*prefetch, *args, *h_ins)
    return res[:n_out], res[n_out:]


def _all_gather(shards, small=()):
    items = tuple(shards) + tuple(small)
    nw = len(shards)

    def place():
        x, y, c, me, _ = _place()
        first = (x + (1 - c) * (1 - 2 * x), y + c * (1 - 2 * y))
        second = (x + c * (1 - 2 * x), y + (1 - c) * (1 - 2 * y))
        diag = (1 - x, 1 - y)
        return x, y, c, me, (first, second, diag)

    def halves(w, c):
        rh = items[w].shape[0] // 2
        return pl.ds(c * rh, rh), pl.ds((1 - c) * rh, rh)

    def start(ins, outs, ss, rs):
        x, y, c, me, chips = place()
        for w in range(len(items)):
            _remote(ins[w], outs[w].at[me], ss.at[w, 6], rs.at[w, 6], (x, y, 1 - c)).start()
            if w < nw:
                mine, _ = halves(w, c)
                _remote(ins[w].at[mine], outs[w].at[me, mine], ss.at[w, 0], rs.at[w, 0], (*chips[0], c)).start()
            else:
                for k in range(3):
                    _remote(ins[w], outs[w].at[me], ss.at[w, k], rs.at[w, k], (*chips[k], c)).start()

    def onward(outs, ss, rs, w, k, x, y, c, chips):
        mine, _ = halves(w, c)
        pk = 2 * chips[k][0] + chips[k][1]
        got = outs[w].at[pk, mine]
        src = chips[1] if k == 2 else chips[k]
        _remote(got, got, ss.at[w, k], rs.at[w, k], (*src, c)).wait_recv()
        if k == 0:
            _remote(got, got, ss.at[w, 2], rs.at[w, 2], (*chips[1], c)).start()
        _remote(got, got, ss.at[w, 3 + k], rs.at[w, 3 + k], (x, y, 1 - c)).start()

    def middle(ins, outs, ss, rs):
        x, y, c, me, chips = place()
        for w in range(nw):
            mine, _ = halves(w, c)
            _remote(ins[w].at[mine], outs[w].at[me, mine], ss.at[w, 1], rs.at[w, 1], (*chips[1], c)).start()
        for w in range(nw):
            onward(outs, ss, rs, w, 0, x, y, c, chips)

    def finish(ins, outs, ss, rs):
        x, y, c, me, chips = place()
        sib = (x, y, 1 - c)
        for k in (1, 2):
            for w in range(nw):
                onward(outs, ss, rs, w, k, x, y, c, chips)
        for w in range(len(items)):
            if w < nw:
                mine, theirs = halves(w, c)
                for k, chip in ((3, chips[1]), (4, chips[0]), (5, chips[2])):
                    oth = outs[w].at[2 * chip[0] + chip[1], theirs]
                    _remote(oth, oth, ss.at[w, k], rs.at[w, k], sib).wait_recv()
                own = ins[w].at[mine]
                for k in range(6):
                    _remote(own, own, ss.at[w, k], rs.at[w, k], sib).wait_send()
            else:
                for k in range(3):
                    got = outs[w].at[2 * chips[k][0] + chips[k][1]]
                    _remote(got, got, ss.at[w, k], rs.at[w, k], (*chips[k], c)).wait_recv()
                    _remote(ins[w], ins[w], ss.at[w, k], rs.at[w, k], sib).wait_send()
            _remote(ins[w], outs[w].at[me], ss.at[w, 6], rs.at[w, 6], sib).wait()

    out_shapes = tuple(jax.ShapeDtypeStruct((N_CHIPS,) + a.shape, a.dtype) for a in items)
    return _Exchange(items, out_shapes, (len(items), 7), start, finish, middle if nw else None,
                     peers=("c", "x", "y", "xy") if small else ("c", "x", "y"))


def _chip_reduce(grads, name, collective_id):
    nw = len(grads)
    step = 64

    def body(*refs):
        ins, outs = refs[:nw], refs[nw:2 * nw]
        own, got = refs[2 * nw:3 * nw], refs[3 * nw:4 * nw]
        send_sems, recv_sems, local_sems = refs[4 * nw:]
        x, y, c, _, _ = _place()
        barrier = pltpu.get_barrier_semaphore()
        pl.semaphore_signal(barrier, inc=1, device_id=(x, y, 1 - c), device_id_type=MESH)
        pl.semaphore_wait(barrier, 1)
        moves = []
        for w in range(nw):
            nb, rh = grads[w].shape[0], grads[w].shape[1] // 2
            for k in range(nb):
                away = _remote(ins[w].at[k, pl.ds((1 - c) * rh, rh), :], got[w].at[k], send_sems.at[w, k],
                               recv_sems.at[w, k], (x, y, 1 - c))
                mine = pltpu.make_async_copy(ins[w].at[k, pl.ds(c * rh, rh), :], own[w].at[k], local_sems.at[w, k])
                away.start()
                mine.start()
                moves.append((w, k, away, mine))
        back = []
        for w, k, away, mine in moves:
            rh = own[w].shape[1]
            mine.wait()
            away.wait()

            def add(i, carry, w=w, k=k):
                rows = pl.ds(pl.multiple_of(i * step, step), step)
                own[w][k, rows, :] = (own[w][k, rows, :].astype(F32) + got[w][k, rows, :].astype(F32)).astype(BF16)
                return carry
            lax.fori_loop(0, rh // step, add, 0)
            tail = rh % step
            if tail:
                rows = slice(rh - tail, rh)
                own[w][k, rows, :] = (own[w][k, rows, :].astype(F32) + got[w][k, rows, :].astype(F32)).astype(BF16)
            wb = pltpu.make_async_copy(own[w].at[k], outs[w].at[k, pl.ds(c * rh, rh), :], local_sems.at[w, k])
            wb.start()
            back.append(wb)
        for wb in back:
            wb.wait()

    halves = [pltpu.VMEM((g.shape[0], g.shape[1] // 2, g.shape[2]), BF16) for g in grads]
    sem = pltpu.SemaphoreType.DMA((nw, N_CHIPS))
    return pl.pallas_call(
        body, name=name, out_shape=tuple(jax.ShapeDtypeStruct(g.shape, BF16) for g in grads),
        in_specs=[ANY] * nw, out_specs=[ANY] * nw, scratch_shapes=halves + halves + [sem, sem, sem],
        compiler_params=pltpu.CompilerParams(vmem_limit_bytes=VMEM_LIMIT, collective_id=collective_id),
    )(*grads)


def _scatter_partials(parts):
    nw = len(parts)

    def copies(ins, outs, ss, rs):
        _, _, c, _, chips = _place()
        res = []
        for r, (px, py) in enumerate(chips):
            for w in range(nw):
                rh = parts[w].shape[1] // 2
                rows = pl.ds(c * rh, rh)
                res.append(_remote(ins[w].at[2 * px + py, rows], outs[w].at[r, rows], ss.at[w, r], rs.at[w, r],
                                   (px, py, c)))
        return res

    def start(ins, outs, ss, rs):
        for cp in copies(ins, outs, ss, rs):
            cp.start()

    def finish(ins, outs, ss, rs):
        for cp in copies(ins, outs, ss, rs):
            cp.wait()

    out_shapes = tuple(jax.ShapeDtypeStruct((3,) + p.shape[1:], p.dtype) for p in parts)
    return _Exchange(parts, out_shapes, (nw, 3), start, finish, peers=("x", "y", "xy"))


_HBM = pl.BlockSpec(memory_space=pltpu.HBM)
_SEM = pl.BlockSpec(memory_space=pltpu.SEMAPHORE)
_EFFECT = pltpu.SideEffectType.DATAFLOW_SIDE_EFFECTING


def _scatter_copies(part_ref, land_ref, sems):
    _, _, c, _, chips = _place()
    rh = part_ref.shape[1] // 2
    rows = pl.ds(c * rh, rh)
    return [pltpu.make_async_remote_copy(src_ref=part_ref.at[2 * px + py, rows], dst_ref=land_ref.at[k, rows],
                                         send_sem=sems[k], recv_sem=sems[3 + k], device_id=(px, py, c),
                                         device_id_type=MESH)
            for k, (px, py) in enumerate(chips)]


def _scatter_start(part, collective_id):
    land = lax.empty((3,) + part.shape[1:], part.dtype)

    def body(part_ref, land_ref, *rest):
        _handshake(("x", "y", "xy"))
        for cp in _scatter_copies(part_ref, land_ref, rest[:6]):
            cp.start()
        rest[8][...] = jnp.zeros_like(rest[8])

    sem = pltpu.SemaphoreType.DMA(())
    res = pl.pallas_call(
        body, name="rs_scatter_in_start",
        out_shape=(sem,) * 6 + (pltpu.HBM(part.shape, part.dtype), pltpu.HBM(land.shape, land.dtype),
                                jax.ShapeDtypeStruct((8, 128), F32)),
        in_specs=(_HBM, _HBM), out_specs=(_SEM,) * 6 + (_HBM, _HBM, pl.BlockSpec(memory_space=pltpu.VMEM)),
        input_output_aliases={0: 6, 1: 7},
        compiler_params=pltpu.CompilerParams(has_side_effects=_EFFECT, collective_id=collective_id),
    )(pltpu.with_memory_space_constraint(part, pltpu.HBM), pltpu.with_memory_space_constraint(land, pltpu.HBM))
    return res[:6], res[6], res[7], res[8]


def _scatter_wait(sems, part_thru, land_thru, after):
    def body(part_ref, land_ref, *rest):
        for cp in _scatter_copies(part_ref, land_ref, rest[:6]):
            cp.wait_send()
            cp.wait_recv()

    return pl.pallas_call(
        body, name="rs_scatter_in_wait",
        out_shape=(pltpu.HBM(part_thru.shape, part_thru.dtype), pltpu.HBM(land_thru.shape, land_thru.dtype)),
        in_specs=(_HBM, _HBM) + (_SEM,) * 6 + (ANY,), out_specs=(_HBM, _HBM), input_output_aliases={0: 0, 1: 1},
        compiler_params=pltpu.CompilerParams(has_side_effects=_EFFECT),
    )(part_thru, land_thru, *sems, after)


def _small_copies(slab_ref, land_ref, sems):
    x, y, c, _, _ = _place()
    me = 4 * x + 2 * y + c
    res = []
    for k in range(1, 8):
        px = 1 - x if k & 4 else x
        py = 1 - y if k & 2 else y
        pc = 1 - c if k & 1 else c
        res.append((pltpu.make_async_remote_copy(src_ref=slab_ref, dst_ref=land_ref.at[me], send_sem=sems[k - 1],
                                                 recv_sem=sems[7 + k - 1], device_id=(px, py, pc),
                                                 device_id_type=MESH), 4 * px + 2 * py + pc))
    return res


def _small_start(slab, collective_id):
    land = lax.empty((8,) + slab.shape, slab.dtype)

    def body(slab_ref, land_ref, *rest):
        _handshake(tuple(_FLIPS))
        for cp, _ in _small_copies(slab_ref, land_ref, rest[:14]):
            cp.start()
        rest[16][...] = jnp.zeros_like(rest[16])

    sem = pltpu.SemaphoreType.DMA(())
    res = pl.pallas_call(
        body, name="small_gather_start",
        out_shape=(sem,) * 14 + (pltpu.HBM(slab.shape, slab.dtype), pltpu.HBM(land.shape, land.dtype),
                                 jax.ShapeDtypeStruct((8, 128), F32)),
        in_specs=(_HBM, _HBM), out_specs=(_SEM,) * 14 + (_HBM, _HBM, pl.BlockSpec(memory_space=pltpu.VMEM)),
        input_output_aliases={0: 14, 1: 15},
        compiler_params=pltpu.CompilerParams(has_side_effects=_EFFECT, collective_id=collective_id),
    )(pltpu.with_memory_space_constraint(slab, pltpu.HBM), pltpu.with_memory_space_constraint(land, pltpu.HBM))
    return res[:14], res[14], res[15], res[16]


def _small_wait(sems, slab_thru, land_thru, after):
    def body(slab_ref, land_ref, *rest):
        for cp, _ in _small_copies(slab_ref, land_ref, rest[:14]):
            cp.wait_send()
            cp.wait_recv()

    return pl.pallas_call(
        body, name="small_gather_wait",
        out_shape=(pltpu.HBM(slab_thru.shape, slab_thru.dtype), pltpu.HBM(land_thru.shape, land_thru.dtype)),
        in_specs=(_HBM, _HBM) + (_SEM,) * 14 + (ANY,), out_specs=(_HBM, _HBM), input_output_aliases={0: 0, 1: 1},
        compiler_params=pltpu.CompilerParams(has_side_effects=_EFFECT),
    )(slab_thru, land_thru, *sems, after)


def _join_partials(parts, slots):
    nw = len(parts)

    def copies(outs, ss, rs, mine):
        x, y, c, me, _ = _place()
        res = []
        for w in range(nw):
            rh = parts[w].shape[1] // 2
            rows = pl.ds((c if mine else 1 - c) * rh, rh)
            own = outs[w].at[me, rows]
            got = outs[nw + w].at[:, rows, :]
            res.append(_remote(own, own, ss.at[w, 0], rs.at[w, 0], (x, y, 1 - c)))
            res.append(_remote(got, got, ss.at[w, 1], rs.at[w, 1], (x, y, 1 - c)))
        return res

    def start(ins, outs, ss, rs):
        for cp in copies(outs, ss, rs, True):
            cp.start()

    def finish(ins, outs, ss, rs):
        for cp in copies(outs, ss, rs, True):
            cp.wait_send()
        for cp in copies(outs, ss, rs, False):
            cp.wait_recv()

    arrays = tuple(parts) + tuple(slots)
    return _Exchange(arrays, tuple(jax.ShapeDtypeStruct(a.shape, a.dtype) for a in arrays), (nw, 2), start, finish,
                     in_place=True, peers=("c",))


def _gather_small(slab):
    def copies(ins, outs, ss, rs):
        x, y, c, _, _ = _place()
        me = 4 * x + 2 * y + c
        out, arrivals = [], []
        for k in range(1, 8):
            px = 1 - x if k & 4 else x
            py = 1 - y if k & 2 else y
            pc = 1 - c if k & 1 else c
            out.append(_remote(ins[0], outs[0].at[me], ss.at[k - 1], rs.at[k - 1], (px, py, pc)))
            theirs = outs[0].at[4 * px + 2 * py + pc]
            arrivals.append((theirs, k - 1, (px, py, pc)))
        return pltpu.make_async_copy(ins[0], outs[0].at[me], ss.at[7]), out, arrivals

    def start(ins, outs, ss, rs):
        own, out, _ = copies(ins, outs, ss, rs)
        own.start()
        for cp in out:
            cp.start()

    def finish(ins, outs, ss, rs):
        own, out, arrivals = copies(ins, outs, ss, rs)
        for cp in out:
            cp.wait_send()
        for theirs, k, peer in arrivals:
            _remote(theirs, theirs, ss.at[k], rs.at[k], peer).wait_recv()
        own.wait()

    return _Exchange((slab,), (jax.ShapeDtypeStruct((8,) + slab.shape, slab.dtype),), (8,), start, finish)


_SMALL_VECS = ("ln_mix_g", "ln_attn_g", "ln_mem_g", "ln_ffn_g", "ln_final_g")


def _pack_small(p, extra, conv):
    d = p["ln_mix_g"].shape[-1]
    top = [p[k].reshape(1, d) for k in _SMALL_VECS]
    top.append(jnp.concatenate([p["sgu_ln_g"].reshape(-1), p["sgu_ln_b"].reshape(-1)]).reshape(1, d))
    top.append(jnp.concatenate([p["grp_norm_a"].reshape(-1), p["grp_norm_b"].reshape(-1)]).reshape(1, d))
    top.append(jnp.concatenate([p["b_spatial"].reshape(-1), extra]).reshape(1, d))
    mid = jnp.zeros((8, d), F32)
    if conv is not None:
        mid = jnp.pad(conv, ((0, 5), (0, d - conv.shape[1])))
    return jnp.concatenate([jnp.concatenate(top, axis=0), mid, p["w_spatial"].reshape(-1, d)], axis=0)


def _unpack_small(slab):
    d = slab.shape[1]
    hw = d // 2
    out = {k: slab[i] for i, k in enumerate(_SMALL_VECS)}
    out["sgu_ln_g"], out["sgu_ln_b"] = slab[5, :hw], slab[5, hw:]
    out["grp_norm_a"], out["grp_norm_b"] = slab[6, :hw], slab[6, hw:]
    out["b_spatial"] = slab[7, :hw].reshape(HEADS, CHUNK)
    out["w_spatial"] = slab[16:].reshape(HEADS, CHUNK, CHUNK)
    return out


_BIG = ("w_in", "w_kv", "w_gate_up", "w_out", "w_q", "w_o", "w_down")
_WEIGHTS = ("ln_mix_g", "w_in", "sgu_ln_g", "sgu_ln_b", "w_spatial", "b_spatial", "conv_w", "grp_norm_a",
            "grp_norm_b", "w_out", "ln_attn_g", "ln_mem_g", "w_q", "w_kv", "w_o", "ln_ffn_g", "w_gate_up",
            "w_down", "ln_final_g")


def _step(p, m_, v_, x, mem, target):
    s, d = x.shape
    hw = d // 2
    row = lambda a: a.reshape(1, -1)
    x_, y_, c_ = lax.axis_index("x"), lax.axis_index("y"), lax.axis_index("c")
    chip = 2 * x_ + y_

    conv8 = jnp.pad(p["conv_w"], ((0, 5), (0, 0)))
    later = ("w_kv", "w_q", "w_o", "w_down", "w_gate_up")
    first = _run_exchange(_all_gather([p["w_in"].astype(BF16), p["w_out"].astype(BF16)], [conv8]),
                          "all_gather_mixer", collective_id=4, casts=[p[k] for k in later])
    (w_in, w_out4, conv4), bf = first[:3], dict(zip(later, first[3:]))
    cw = jnp.transpose(conv4[:, :3, :], (1, 0, 2)).reshape(3, hw)
    b_t = jnp.pad(jnp.transpose(p["b_spatial"]), ((0, 0), (0, CHUNK - HEADS)))
    g1, g2, gm, g3, gf = (row(p[k]) for k in _SMALL_VECS)
    lng, lnb, ga, gb = row(p["sgu_ln_g"]), row(p["sgu_ln_b"]), row(p["grp_norm_a"]), row(p["grp_norm_b"])
    wsp = p["w_spatial"]
    w_out = w_out4.reshape(-1, d)

    (h, x1, ycat, xn1, mixed, th), (w_kv, w_q4, w_o4, w_down4) = _mixer_fwd(
        x, g1, w_in, lng, lnb, wsp, b_t, cw, ga, gb, w_out,
        hosted=_all_gather([bf[k] for k in ("w_kv", "w_q", "w_o", "w_down")]).with_id(5))
    w_q, w_o, w_down = (a.reshape(-1, d) for a in (w_q4, w_o4, w_down4))
    (x2, o, qs, probs, memn, kv), (w_gu,) = _attn_fwd(x1, mem, g2, gm, w_q, w_kv, w_o,
                                                      hosted=_all_gather([bf["w_gate_up"]]).with_id(6))
    dx2, act, dgu, xn3, dx3, loss, dgf, dg3 = _ffn_fwd_bwd(x2, g3, gf, target, w_gu, w_down)

    place = jnp.stack([c_, chip]).astype(jnp.int32)

    def chip_partials(names, grads, tag):
        return list(_chip_reduce(grads, "chip_reduce_" + tag, ("down", "ffn", "attn", "mixer").index(tag)))

    names_d = ("w_down",)
    parts_d = chip_partials(names_d, (_weight_grad(act, dx3, "grad_w_down", 1408, 512, False)[0],), "down")
    g_gu, slots_d = _weight_grad(xn3, dgu, "grad_w_gate_up", 512, 1408, True,
                                 hosted=_scatter_partials(parts_d).with_id(7))
    names_a = ("w_gate_up",)
    parts_a = chip_partials(names_a, (g_gu,), "ffn")
    (dx1, dkv, dg2, g_o, g_q, g_out), slots_a = _attn_bwd(x1, dx2, o, ycat, qs, probs, g2, w_q, kv, w_o,
                                                           hosted=_scatter_partials(parts_a).with_id(8))
    g_kv, dgm = _kv_bwd(dkv, mem, memn, gm, w_kv)
    names_b = ("w_o", "w_out", "w_q", "w_kv")
    shard_major = lambda g: g.reshape(N_CHIPS, -1, d)
    parts_b = chip_partials(names_b, (shard_major(g_o), shard_major(g_out), shard_major(g_q), g_kv), "attn")
    names_da = names_d + names_a
    (dx, dh, dg1, dlng, dlnb, dwsp, dbt, dcw, dga, dgb), extra = _mixer_bwd(
        x, dx1, h, mixed, th, g1, lng, lnb, wsp, b_t, cw, ga, gb, w_out, w_in,
        hosted=[_scatter_partials(parts_b).with_id(9), _join_partials(parts_d + parts_a, slots_d + slots_a)])
    slots_b, joined = extra[:len(names_b)], extra[len(names_b):]
    whole = dict(zip(names_da, zip(joined[:len(names_da)], joined[len(names_da):])))
    small = {"ln_mix_g": dg1, "ln_attn_g": dg2, "ln_mem_g": dgm, "ln_ffn_g": dg3, "ln_final_g": dgf,
             "sgu_ln_g": dlng, "sgu_ln_b": dlnb, "grp_norm_a": dga, "grp_norm_b": dgb,
             "b_spatial": jnp.transpose(dbt[:, :HEADS]), "w_spatial": dwsp}
    loss_vec = jnp.pad(loss.reshape(1), (0, hw - 1))
    small_sems, slab_thru, small_land, small_token = _small_start(_pack_small(small, loss_vec, dcw), 12)
    g_in, extra = _weight_grad(xn1, dh, "grad_w_in", 1024, 640, True,
                               hosted=_join_partials(parts_b, slots_b).with_id(13), after=small_token)
    whole.update(zip(names_b, zip(extra[:len(names_b)], extra[len(names_b):])))
    (part_in,) = chip_partials(("w_in",), (g_in,), "mixer")
    out_g, out_d, out_m, out_v = {}, {}, {}, {}

    def finalize(ks, tag, after=None):
        done, _ = _finalize([(whole[k][0], whole[k][1], p[k], m_[k], v_[k]) for k in ks], place,
                            "finalize_" + tag, after=after)
        for k, (g, dl, nm, nv) in zip(ks, done):
            out_g[k], out_d[k], out_m[k], out_v[k] = g, dl, nm, nv

    sems, part_thru, land_thru, token = _scatter_start(part_in, 10)
    finalize(("w_down",), "w_down", after=token)
    finalize(("w_o", "w_out", "w_q"), "attn", after=token)
    finalize(("w_gate_up",), "w_gate_up", after=token)
    finalize(("w_kv",), "w_kv", after=token)
    part_in, slots_in = _scatter_wait(sems, part_thru, land_thru, out_v["w_kv"])
    whole["w_in"] = _run_exchange(_join_partials([part_in], [slots_in]), "rs_join_mixer", collective_id=11)
    finalize(("w_in",), "w_in")

    zeros = jnp.zeros((hw,), F32)
    own_slab, parts = _small_wait(small_sems, slab_thru, small_land, out_v["w_in"])
    me = (4 * x_ + 2 * y_ + c_).astype(jnp.int32).reshape(1)
    sg, sd, sm, sv = _small_sum_adamw(parts, own_slab, me, _pack_small(p, zeros, None),
                                      _pack_small(m_, zeros, None), _pack_small(v_, zeros, None))
    for tree, slab in zip((out_g, out_d, out_m, out_v), (sg, sd, sm, sv)):
        tree.update(_unpack_small(slab))
    loss_out = sg[7, hw]
    g_conv = lax.dynamic_slice(sg[8:11, :hw], (0, chip * (hw // N_CHIPS)), (3, hw // N_CHIPS))
    out_g["conv_w"] = g_conv
    out_d["conv_w"], out_m["conv_w"], out_v["conv_w"] = _adamw(p["conv_w"], g_conv, m_["conv_w"], v_["conv_w"],
                                                                "adamw_conv_w")
    return loss_out, dx, out_g, out_d, out_m, out_v


def kernel(x, mem, ln_mix_g, w_in, sgu_ln_g, sgu_ln_b, w_spatial, b_spatial, conv_w, grp_norm_a, grp_norm_b, w_out, ln_attn_g, ln_mem_g, w_q, w_kv, w_o, ln_ffn_g, w_gate_up, w_down, ln_final_g, loss_target, m_ln_mix_g, m_w_in, m_sgu_ln_g, m_sgu_ln_b, m_w_spatial, m_b_spatial, m_conv_w, m_grp_norm_a, m_grp_norm_b, m_w_out, m_ln_attn_g, m_ln_mem_g, m_w_q, m_w_kv, m_w_o, m_ln_ffn_g, m_w_gate_up, m_w_down, m_ln_final_g, v_ln_mix_g, v_w_in, v_sgu_ln_g, v_sgu_ln_b, v_w_spatial, v_b_spatial, v_conv_w, v_grp_norm_a, v_grp_norm_b, v_w_out, v_ln_attn_g, v_ln_mem_g, v_w_q, v_w_kv, v_w_o, v_ln_ffn_g, v_w_gate_up, v_w_down, v_ln_final_g):
    p = dict(ln_mix_g=ln_mix_g, w_in=w_in, sgu_ln_g=sgu_ln_g, sgu_ln_b=sgu_ln_b, w_spatial=w_spatial,
             b_spatial=b_spatial, conv_w=conv_w, grp_norm_a=grp_norm_a, grp_norm_b=grp_norm_b, w_out=w_out,
             ln_attn_g=ln_attn_g, ln_mem_g=ln_mem_g, w_q=w_q, w_kv=w_kv, w_o=w_o, ln_ffn_g=ln_ffn_g,
             w_gate_up=w_gate_up, w_down=w_down, ln_final_g=ln_final_g)
    m_ = dict(ln_mix_g=m_ln_mix_g, w_in=m_w_in, sgu_ln_g=m_sgu_ln_g, sgu_ln_b=m_sgu_ln_b, w_spatial=m_w_spatial,
              b_spatial=m_b_spatial, conv_w=m_conv_w, grp_norm_a=m_grp_norm_a, grp_norm_b=m_grp_norm_b,
              w_out=m_w_out, ln_attn_g=m_ln_attn_g, ln_mem_g=m_ln_mem_g, w_q=m_w_q, w_kv=m_w_kv, w_o=m_w_o,
              ln_ffn_g=m_ln_ffn_g, w_gate_up=m_w_gate_up, w_down=m_w_down, ln_final_g=m_ln_final_g)
    v_ = dict(ln_mix_g=v_ln_mix_g, w_in=v_w_in, sgu_ln_g=v_sgu_ln_g, sgu_ln_b=v_sgu_ln_b, w_spatial=v_w_spatial,
              b_spatial=v_b_spatial, conv_w=v_conv_w, grp_norm_a=v_grp_norm_a, grp_norm_b=v_grp_norm_b,
              w_out=v_w_out, ln_attn_g=v_ln_attn_g, ln_mem_g=v_ln_mem_g, w_q=v_w_q, w_kv=v_w_kv, w_o=v_w_o,
              ln_ffn_g=v_ln_ffn_g, w_gate_up=v_w_gate_up, w_down=v_w_down, ln_final_g=v_ln_final_g)
    s, d = x.shape[-2], x.shape[-1]
    loss, dx, g, dl, nm, nv = _step(p, m_, v_, x.reshape(s, d), mem.reshape(-1, d), loss_target.reshape(s, d))
    outs = [loss, dx.reshape(x.shape)]
    for tree in (g, dl, nm, nv):
        outs += [tree[k].reshape(p[k].shape) for k in _WEIGHTS]
    return tuple(outs)
```

```python
import functools
import math

import jax
import jax.numpy as jnp
from jax import lax
from jax.experimental import pallas as pl
from jax.experimental.pallas import tpu as pltpu

F32 = jnp.float32
BF16 = jnp.bfloat16
EPS = 1e-6
CHUNK = 128
HEADS = 4
N_CHIPS = 4
TM = 512
TM_ATTN = 512
TM_FFN = 256
ADAM_LR, ADAM_B1, ADAM_B2, ADAM_EPS, ADAM_WD, ADAM_STEP = 0.001, 0.9, 0.999, 1e-08, 0.01, 10
GELU_C = math.sqrt(2.0 / math.pi)
GELU_K = 0.044715
SMALL_ROWS = 80
VMEM_LIMIT = 56 * 1024 * 1024
MIDDLE_STEP_16THS = 7
MESH = pl.DeviceIdType.MESH
ANY = pl.BlockSpec(memory_space=pl.ANY)


def _params(*sem, collective_id=None):
    return pltpu.CompilerParams(dimension_semantics=sem, vmem_limit_bytes=VMEM_LIMIT, collective_id=collective_id)


def _dot(a, b):
    return jnp.dot(a, b, preferred_element_type=F32)


def _dot_nt(a, b):
    return lax.dot_general(a, b, (((1,), (1,)), ((), ())), preferred_element_type=F32)


def _dot_tn(a, b):
    return lax.dot_general(a, b, (((0,), (0,)), ((), ())), preferred_element_type=F32)


def _rms_fwd(x, g):
    r = lax.rsqrt(jnp.mean(x * x, axis=-1, keepdims=True) + EPS)
    xh = x * r
    return xh * g, xh, r


def _rms_bwd(dy, xh, r, g):
    dxh = dy * g
    dx = r * (dxh - xh * jnp.mean(dxh * xh, axis=-1, keepdims=True))
    return dx, jnp.sum(dy * xh, axis=0, keepdims=True)


def _full(shape):
    nd = len(shape)
    return pl.BlockSpec(shape, lambda *_: (0,) * nd, pipeline_mode=pl.Buffered(1))


def _acc(shape):
    nd = len(shape)
    return pl.BlockSpec(shape, lambda *_: (0,) * nd)


def _rows(tm, cols):
    return pl.BlockSpec((tm, cols), lambda i: (i, 0))


def _tril_weights(wsp_ref):
    row = lax.broadcasted_iota(jnp.int32, (CHUNK, CHUNK), 0)
    col = lax.broadcasted_iota(jnp.int32, (CHUNK, CHUNK), 1)
    return [jnp.where(row >= col, wsp_ref[hd], 0.0).astype(BF16) for hd in range(HEADS)]


def _shift_rows(z, zp):
    row = lax.broadcasted_iota(jnp.int32, z.shape, 0)
    zm1 = jnp.where(row == 0, zp[7:8, :], pltpu.roll(z, 1, 0))
    zm2 = jnp.where(row == 0, zp[6:7, :], jnp.where(row == 1, zp[7:8, :], pltpu.roll(z, 2, 0)))
    return zm1, zm2


def _gelu_parts(x):
    t = jnp.tanh(GELU_C * (x + GELU_K * (x * x * x)))
    return 0.5 * x * (1.0 + t), t


def _layer_norm_parts(v, g, b):
    mu = jnp.mean(v, axis=-1, keepdims=True)
    vc = v - mu
    rs = lax.rsqrt(jnp.mean(vc * vc, axis=-1, keepdims=True) + EPS)
    vhat = vc * rs
    return vhat * g + b, vhat, rs


def _mixer_fwd(x, g1, w_in, lng, lnb, wsp, b_t, cw, ga, gb, w_out, hosted=None):
    s, d = x.shape
    n = s // TM
    nch = TM // CHUNK
    ns = w_in.shape[2]
    nh = N_CHIPS * ns
    aw = d // 2
    hd_w = aw // HEADS

    def body(x_ref, g1_ref, win_ref, lng_ref, lnb_ref, wsp_ref, bt_ref, cw_ref, ga_ref, gb_ref, wout_ref,
             h_ref, x1_ref, y_ref, xn_ref, mix_ref, th_ref, zp_ref):
        i = pl.program_id(0)

        @pl.when(i == 0)
        def _():
            zp_ref[...] = jnp.zeros_like(zp_ref)

        x = x_ref[...]
        xn, _, _ = _rms_fwd(x, g1_ref[...])
        xnb = xn.astype(BF16)
        xn_ref[...] = xnb
        for k in range(N_CHIPS):
            h_ref[:, k * ns:(k + 1) * ns] = _dot(xnb, win_ref[k])
        a, th = _gelu_parts(h_ref[:, 0:2 * aw])
        th_ref[...] = th
        u = a[:, :aw]
        vn, _, _ = _layer_norm_parts(a[:, aw:], lng_ref[...], lnb_ref[...])
        vnb = vn.astype(BF16)
        wm = _tril_weights(wsp_ref)
        for c in range(nch):
            for hd in range(HEADS):
                blk = vnb[c * CHUNK:(c + 1) * CHUNK, hd * hd_w:(hd + 1) * hd_w]
                mix_ref[c * CHUNK:(c + 1) * CHUNK, hd * hd_w:(hd + 1) * hd_w] = _dot(wm[hd], blk) + bt_ref[:, hd:hd + 1]
        ya, _, _ = _rms_fwd(u * mix_ref[...], ga_ref[...])
        g_b = h_ref[:, 2 * aw:3 * aw]
        z = h_ref[:, 3 * aw:4 * aw] * h_ref[:, 4 * aw:5 * aw]
        zm1, zm2 = _shift_rows(z, zp_ref[...])
        conv = cw_ref[0:1, :] * zm2 + cw_ref[1:2, :] * zm1 + cw_ref[2:3, :] * z
        yb, _, _ = _rms_fwd(g_b * conv, gb_ref[...])
        zp_ref[...] = z[TM - 8:TM, :]
        ycat = jnp.concatenate([ya, yb], axis=-1).astype(BF16)
        y_ref[...] = ycat
        x1_ref[...] = x + _dot(ycat, wout_ref[...])

    return _host_call(
        body, "mixer_fwd", (n,),
        [_rows(TM, d), _full(g1.shape), _full(w_in.shape), _full(lng.shape), _full(lnb.shape),
         _full(wsp.shape), _full(b_t.shape), _full(cw.shape), _full(ga.shape), _full(gb.shape),
         _full(w_out.shape)],
        [_rows(TM, nh), _rows(TM, d), _rows(TM, d), _rows(TM, d), _rows(TM, aw), _rows(TM, d)],
        (jax.ShapeDtypeStruct((s, nh), F32), jax.ShapeDtypeStruct((s, d), F32),
         jax.ShapeDtypeStruct((s, d), BF16), jax.ShapeDtypeStruct((s, d), BF16),
         jax.ShapeDtypeStruct((s, aw), F32), jax.ShapeDtypeStruct((s, d), F32)),
        [pltpu.VMEM((8, aw), F32)],
        (x, g1, w_in, lng, lnb, wsp, b_t, cw, ga, gb, w_out), ("arbitrary",), hosted)


def _attn_fwd(x1, mem, g2, g_mem, w_q, w_kv, w_o, hosted=None):
    s, d = x1.shape
    tm = min(TM_ATTN, s)
    n = s // tm
    dh = d // HEADS
    m = mem.shape[0]
    ns = w_kv.shape[2]
    scale = dh ** -0.5

    def body(x1_ref, mem_ref, g2_ref, gm_ref, wq_ref, wkv_ref, wo_ref, x2_ref, o_ref, q_ref, p_ref, memn_ref, kv_ref):
        @pl.when(pl.program_id(0) == 0)
        def _():
            y, _, _ = _rms_fwd(mem_ref[...], gm_ref[...])
            yb = y.astype(BF16)
            memn_ref[...] = yb
            for k in range(N_CHIPS):
                kv_ref[:, k * ns:(k + 1) * ns] = _dot(yb, wkv_ref[k]).astype(BF16)

        x1v = x1_ref[...]
        xn, _, _ = _rms_fwd(x1v, g2_ref[...])
        q_ref[...] = _dot(xn.astype(BF16), wq_ref[...]).astype(BF16)
        for hd in range(HEADS):
            kh = kv_ref[:, hd * dh:(hd + 1) * dh]
            vh = kv_ref[:, d + hd * dh:d + (hd + 1) * dh]
            sc = _dot_nt(q_ref[:, hd * dh:(hd + 1) * dh], kh) * scale
            e = jnp.exp(sc - jnp.max(sc, axis=-1, keepdims=True))
            p = e / jnp.sum(e, axis=-1, keepdims=True)
            p_ref[:, hd * m:(hd + 1) * m] = p
            o_ref[:, hd * dh:(hd + 1) * dh] = _dot(p.astype(BF16), vh).astype(BF16)
        x2_ref[...] = x1v + _dot(o_ref[...], wo_ref[...])

    return _host_call(
        body, "attn_fwd", (n,),
        [_rows(tm, d), _full(mem.shape), _full(g2.shape), _full(g_mem.shape), _full(w_q.shape), _full(w_kv.shape),
         _full(w_o.shape)],
        [_rows(tm, d), _rows(tm, d), _rows(tm, d), _rows(tm, HEADS * m), _acc((m, d)), _acc((m, 2 * d))],
        (jax.ShapeDtypeStruct((s, d), F32), jax.ShapeDtypeStruct((s, d), BF16), jax.ShapeDtypeStruct((s, d), BF16),
         jax.ShapeDtypeStruct((s, HEADS * m), F32), jax.ShapeDtypeStruct((m, d), BF16),
         jax.ShapeDtypeStruct((m, 2 * d), BF16)),
        [], (x1, mem, g2, g_mem, w_q, w_kv, w_o), ("arbitrary",), hosted)


def _ffn_fwd_bwd(x2, g3, gf, target, w_gu, w_down):
    s, d = x2.shape
    tm = min(TM_FFN, s)
    n = s // tm
    ns = w_gu.shape[2]
    ff = 2 * ns

    def body(x2_ref, g3_ref, gf_ref, t_ref, wgu_ref, wd_ref,
             dx2_ref, act_ref, dgu_ref, xn_ref, dx3_ref, loss_ref, dgf_ref, dg3_ref):
        i = pl.program_id(0)

        @pl.when(i == 0)
        def _():
            loss_ref[...] = jnp.zeros_like(loss_ref)
            dgf_ref[...] = jnp.zeros_like(dgf_ref)
            dg3_ref[...] = jnp.zeros_like(dg3_ref)

        x2v = x2_ref[...]
        xn, xh3, r3 = _rms_fwd(x2v, g3_ref[...])
        xnb = xn.astype(BF16)
        xn_ref[...] = xnb
        x3 = x2v
        saved = []
        for j in range(2):
            g = _dot(xnb, wgu_ref[j])
            u = _dot(xnb, wgu_ref[2 + j])
            sg = 1.0 / (1.0 + jnp.exp(-g))
            sl = g * sg
            actb = (sl * u).astype(BF16)
            act_ref[:, j * ns:(j + 1) * ns] = actb
            x3 = x3 + _dot(actb, wd_ref[j * ns:(j + 1) * ns, :])
            saved.append((u, sl, sg * (1.0 + g * (1.0 - sg))))
        gfv = gf_ref[...]
        y, xhf, rf = _rms_fwd(x3, gfv)
        e = y - t_ref[...]
        loss_ref[...] += 0.5 * jnp.sum(jnp.sum(e * e, axis=-1, keepdims=True), axis=0, keepdims=True) / d
        dx3, dgf = _rms_bwd(e / d, xhf, rf, gfv)
        dgf_ref[...] += dgf
        dx3b = dx3.astype(BF16)
        dx3_ref[...] = dx3b
        dxn = jnp.zeros_like(x2v)
        for j in range(2):
            u, sl, dsl = saved[j]
            dact = _dot_nt(dx3b, wd_ref[j * ns:(j + 1) * ns, :])
            dgb = (dact * u * dsl).astype(BF16)
            dub = (dact * sl).astype(BF16)
            dgu_ref[:, j * ns:(j + 1) * ns] = dgb
            dgu_ref[:, ff + j * ns:ff + (j + 1) * ns] = dub
            dxn = dxn + _dot_nt(dgb, wgu_ref[j]) + _dot_nt(dub, wgu_ref[2 + j])
        dxr, dg3 = _rms_bwd(dxn, xh3, r3, g3_ref[...])
        dg3_ref[...] += dg3
        dx2_ref[...] = dx3 + dxr

    vec = jax.ShapeDtypeStruct((1, d), F32)
    return pl.pallas_call(
        body, name="ffn_fwd_bwd", grid=(n,),
        in_specs=[_rows(tm, d), _full(g3.shape), _full(gf.shape), _rows(tm, d), _full(w_gu.shape),
                  _full(w_down.shape)],
        out_specs=[_rows(tm, d), _rows(tm, ff), _rows(tm, 2 * ff), _rows(tm, d), _rows(tm, d),
                   _acc((1, 1)), _acc((1, d)), _acc((1, d))],
        out_shape=(jax.ShapeDtypeStruct((s, d), F32), jax.ShapeDtypeStruct((s, ff), BF16),
                   jax.ShapeDtypeStruct((s, 2 * ff), BF16), jax.ShapeDtypeStruct((s, d), BF16),
                   jax.ShapeDtypeStruct((s, d), BF16), jax.ShapeDtypeStruct((1, 1), F32), vec, vec),
        compiler_params=_params("arbitrary"),
    )(x2, g3, gf, target, w_gu, w_down)


def _attn_bwd(x1, dx2, o, ycat, qs, probs, g2, w_q, kv, w_o, hosted=None):
    s, d = x1.shape
    tm = min(TM_ATTN, s)
    n = s // tm
    dh = d // HEADS
    scale = dh ** -0.5
    m = kv.shape[0]

    def body(x1_ref, dx2_ref, o_ref, y_ref, q_ref, p_ref, g2_ref, wq_ref, kv_ref, wo_ref,
             dx1_ref, dkv_ref, dg2_ref, gwo_out, gwq_out, gwout_out, dq_ref, gwo_ref, gwq_ref, gwout_ref):
        i = pl.program_id(0)

        @pl.when(i == 0)
        def _():
            for r in (dkv_ref, dg2_ref, gwo_ref, gwq_ref, gwout_ref):
                r[...] = jnp.zeros_like(r)

        xn, xh2, r2 = _rms_fwd(x1_ref[...], g2_ref[...])
        xnb = xn.astype(BF16)
        dx2v = dx2_ref[...]
        dx2b = dx2v.astype(BF16)
        gwo_ref[...] += _dot_tn(o_ref[...], dx2b)
        do = _dot_nt(dx2b, wo_ref[...])
        for hd in range(HEADS):
            qb = q_ref[:, hd * dh:(hd + 1) * dh]
            p = p_ref[:, hd * m:(hd + 1) * m]
            kh = kv_ref[:, hd * dh:(hd + 1) * dh]
            vh = kv_ref[:, d + hd * dh:d + (hd + 1) * dh]
            dob = do[:, hd * dh:(hd + 1) * dh].astype(BF16)
            dp = _dot_nt(dob, vh)
            ds = p * (dp - jnp.sum(dp * p, axis=-1, keepdims=True))
            dsb = (ds * scale).astype(BF16)
            dq_ref[:, hd * dh:(hd + 1) * dh] = _dot(dsb, kh).astype(BF16)
            dkv_ref[:, hd * dh:(hd + 1) * dh] += _dot_tn(dsb, qb)
            dkv_ref[:, d + hd * dh:d + (hd + 1) * dh] += _dot_tn(p.astype(BF16), dob)
        dqb = dq_ref[...]
        gwq_ref[...] += _dot_tn(xnb, dqb)
        dxn = _dot_nt(dqb, wq_ref[...])
        dxr, dg2 = _rms_bwd(dxn, xh2, r2, g2_ref[...])
        dg2_ref[...] += dg2
        dx1 = dx2v + dxr
        dx1_ref[...] = dx1
        gwout_ref[...] += _dot_tn(y_ref[...], dx1.astype(BF16))

        @pl.when(i == n - 1)
        def _():
            for acc, out in ((gwo_ref, gwo_out), (gwq_ref, gwq_out), (gwout_ref, gwout_out)):
                out[...] = acc[...].astype(BF16)

    sq = jax.ShapeDtypeStruct((d, d), BF16)
    return _host_call(
        body, "attn_bwd", (n,),
        [_rows(tm, d), _rows(tm, d), _rows(tm, d), _rows(tm, d), _rows(tm, d), _rows(tm, HEADS * m),
         _full(g2.shape), _full(w_q.shape), _full(kv.shape), _full(w_o.shape)],
        [_rows(tm, d), _acc((m, 2 * d)), _acc((1, d)), _acc((d, d)), _acc((d, d)), _acc((d, d))],
        (jax.ShapeDtypeStruct((s, d), F32), jax.ShapeDtypeStruct((m, 2 * d), F32),
         jax.ShapeDtypeStruct((1, d), F32), sq, sq, sq),
        [pltpu.VMEM((tm, d), BF16)] + [pltpu.VMEM((d, d), F32)] * 3,
        (x1, dx2, o, ycat, qs, probs, g2, w_q, kv, w_o), ("arbitrary",), hosted)


def _kv_bwd(dkv, mem, memn, g_mem, w_kv):
    m, d = mem.shape
    ns = w_kv.shape[2]

    def body(dkv_ref, mem_ref, memn_ref, g_ref, w_ref, gw_ref, dg_ref):
        _, xh, _ = _rms_fwd(mem_ref[...], g_ref[...])
        dmemn = jnp.zeros((m, d), F32)
        for k in range(N_CHIPS):
            dkb = dkv_ref[:, k * ns:(k + 1) * ns].astype(BF16)
            gw_ref[k] = _dot_tn(memn_ref[...], dkb).astype(BF16)
            dmemn = dmemn + _dot_nt(dkb, w_ref[k])
        dg_ref[...] = jnp.sum(dmemn * xh, axis=0, keepdims=True)

    return pl.pallas_call(
        body, name="kv_bwd",
        out_shape=(jax.ShapeDtypeStruct((N_CHIPS, d, ns), BF16), jax.ShapeDtypeStruct((1, d), F32)),
        compiler_params=pltpu.CompilerParams(vmem_limit_bytes=VMEM_LIMIT),
    )(dkv, mem, memn, g_mem, w_kv)


def _mixer_bwd(x, dx1, h, mixed_all, th_all, g1, lng, lnb, wsp, b_t, cw, ga, gb, w_out, w_in, after):
    s, d = x.shape
    n = s // TM
    nch = TM // CHUNK
    ns = w_in.shape[2]
    nh = N_CHIPS * ns
    aw = d // 2
    hd_w = aw // HEADS

    def rev(cols):
        return pl.BlockSpec((TM, cols), lambda i: (n - 1 - i, 0))

    hprev = pl.BlockSpec((8, nh), lambda i: (jnp.maximum((n - 1 - i) * (TM // 8) - 1, 0), 0))

    def body(x_ref, dx1_ref, h_ref, hp_ref, mix_ref, th_ref, g1_ref, lng_ref, lnb_ref, wsp_ref, bt_ref, cw_ref,
             ga_ref, gb_ref, wout_ref, win_ref, after_ref,
             dx_ref, dh_ref, dg1_ref, dlng_ref, dlnb_ref, dwsp_ref, dbt_ref, dcw_ref, dga_ref, dgb_ref,
             dvn_ref, dcn_ref):
        i = pl.program_id(0)

        @pl.when(i == 0)
        def _():
            for r in (dg1_ref, dlng_ref, dlnb_ref, dwsp_ref, dbt_ref, dcw_ref, dga_ref, dgb_ref, dcn_ref):
                r[...] = jnp.zeros_like(r)

        dx1v = dx1_ref[...]
        dycat = _dot_nt(dx1v.astype(BF16), wout_ref[...])
        ha = h_ref[:, 0:2 * aw]
        th = th_ref[...]
        a = 0.5 * ha * (1.0 + th)
        u = a[:, :aw]
        lngv = lng_ref[...]
        vn, vhat, rs = _layer_norm_parts(a[:, aw:], lngv, lnb_ref[...])
        vnb = vn.astype(BF16)
        wm = _tril_weights(wsp_ref)
        mixed = mix_ref[...]
        gav = ga_ref[...]
        _, yah, ra = _rms_fwd(u * mixed, gav)
        dya, dga = _rms_bwd(dycat[:, :aw], yah, ra, gav)
        dga_ref[...] += dga
        du = dya * mixed
        dmix = dya * u
        dmb = dmix.astype(BF16)
        tri = lax.broadcasted_iota(jnp.int32, (CHUNK, CHUNK), 0) >= lax.broadcasted_iota(jnp.int32, (CHUNK, CHUNK), 1)
        for hd in range(HEADS):
            dw = jnp.zeros((CHUNK, CHUNK), F32)
            db = jnp.zeros((CHUNK, 1), F32)
            for c in range(nch):
                rows = slice(c * CHUNK, (c + 1) * CHUNK)
                cols = slice(hd * hd_w, (hd + 1) * hd_w)
                dvn_ref[rows, cols] = _dot_tn(wm[hd], dmb[rows, cols])
                dw = dw + _dot_nt(dmb[rows, cols], vnb[rows, cols])
                db = db + jnp.sum(dmix[rows, cols], axis=1, keepdims=True)
            dwsp_ref[hd] += jnp.where(tri, dw, 0.0)
            dbt_ref[:, hd:hd + 1] += db
        dvn = dvn_ref[...]
        dlng_ref[...] += jnp.sum(dvn * vhat, axis=0, keepdims=True)
        dlnb_ref[...] += jnp.sum(dvn, axis=0, keepdims=True)
        dvh = dvn * lngv
        dv = rs * (dvh - jnp.mean(dvh, axis=-1, keepdims=True) - vhat * jnp.mean(dvh * vhat, axis=-1, keepdims=True))
        gprime = 0.5 * (1.0 + th) + 0.5 * ha * (1.0 - th * th) * (GELU_C * (1.0 + 3.0 * GELU_K * (ha * ha)))
        dh_ref[:, 0:2 * aw] = (jnp.concatenate([du, dv], axis=-1) * gprime).astype(BF16)
        g_b = h_ref[:, 2 * aw:3 * aw]
        g_c = h_ref[:, 3 * aw:4 * aw]
        val = h_ref[:, 4 * aw:5 * aw]
        z = g_c * val
        zp = jnp.where(i == n - 1, 0.0, hp_ref[:, 3 * aw:4 * aw] * hp_ref[:, 4 * aw:5 * aw])
        zm1, zm2 = _shift_rows(z, zp)
        cw0, cw1, cw2 = cw_ref[0:1, :], cw_ref[1:2, :], cw_ref[2:3, :]
        conv = cw0 * zm2 + cw1 * zm1 + cw2 * z
        gbv = gb_ref[...]
        _, ybh, rb = _rms_fwd(g_b * conv, gbv)
        dyb, dgb = _rms_bwd(dycat[:, aw:], ybh, rb, gbv)
        dgb_ref[...] += dgb
        dconv = dyb * g_b
        dcw_ref[0:1, :] += jnp.sum(dconv * zm2, axis=0, keepdims=True)
        dcw_ref[1:2, :] += jnp.sum(dconv * zm1, axis=0, keepdims=True)
        dcw_ref[2:3, :] += jnp.sum(dconv * z, axis=0, keepdims=True)
        nxt = dcn_ref[...]
        row = lax.broadcasted_iota(jnp.int32, dconv.shape, 0)
        dcp1 = jnp.where(row == TM - 1, nxt[0:1, :], pltpu.roll(dconv, TM - 1, 0))
        dcp2 = jnp.where(row == TM - 1, nxt[1:2, :],
                         jnp.where(row == TM - 2, nxt[0:1, :], pltpu.roll(dconv, TM - 2, 0)))
        dz = cw2 * dconv + cw1 * dcp1 + cw0 * dcp2
        dcn_ref[...] = dconv[0:8, :]
        dh_ref[:, 2 * aw:3 * aw] = (dyb * conv).astype(BF16)
        dh_ref[:, 3 * aw:4 * aw] = (dz * val).astype(BF16)
        dh_ref[:, 4 * aw:5 * aw] = (dz * g_c).astype(BF16)
        dxn = jnp.zeros((TM, d), F32)
        for k in range(N_CHIPS):
            dxn = dxn + _dot_nt(dh_ref[:, k * ns:(k + 1) * ns], win_ref[k])
        g1v = g1_ref[...]
        _, xh1, r1 = _rms_fwd(x_ref[...], g1v)
        dxr, dg1 = _rms_bwd(dxn, xh1, r1, g1v)
        dg1_ref[...] += dg1
        dx_ref[...] = dx1v + dxr

    ins = (x, dx1, h, h, mixed_all, th_all, g1, lng, lnb, wsp, b_t, cw, ga, gb, w_out, w_in)
    acc_shapes = [(1, d), (1, aw), (1, aw), wsp.shape, (CHUNK, CHUNK), cw.shape, (1, aw), (1, aw)]
    return pl.pallas_call(
        body, name="mixer_bwd", grid=(n,),
        in_specs=[rev(d), rev(d), rev(nh), hprev, rev(aw), rev(d)] + [_full(a.shape) for a in ins[6:]] + [ANY],
        out_specs=[rev(d), rev(nh)] + [_acc(sh) for sh in acc_shapes],
        out_shape=(jax.ShapeDtypeStruct((s, d), F32), jax.ShapeDtypeStruct((s, nh), BF16))
        + tuple(jax.ShapeDtypeStruct(sh, F32) for sh in acc_shapes),
        scratch_shapes=[pltpu.VMEM((TM, aw), F32), pltpu.VMEM((8, aw), F32)],
        compiler_params=_params("arbitrary"),
    )(*ins, after)


def _weight_grad(a, b, name, tm, tn, col_sharded, hosted=None):
    t, m = a.shape
    n = b.shape[1]

    def body(a_ref, b_ref, o_ref):
        o_ref[...] = _dot_tn(a_ref[...].astype(BF16), b_ref[...].astype(BF16)).astype(BF16)

    if col_sharded:
        ns = n // N_CHIPS
        per = ns // tn
        out_shape = jax.ShapeDtypeStruct((N_CHIPS, m, ns), BF16)
        out_spec = pl.BlockSpec((None, tm, tn), lambda i, j: (j // per, i, j % per))
    else:
        out_shape = jax.ShapeDtypeStruct((m, n), BF16)
        out_spec = pl.BlockSpec((tm, tn), lambda i, j: (i, j))
    (out,), extra = _host_call(
        body, name, (m // tm, n // tn),
        [pl.BlockSpec((t, tm), lambda i, j: (0, i)), pl.BlockSpec((t, tn), lambda i, j: (0, j))],
        [out_spec], (out_shape,), [], (a, b), ("parallel", "parallel"), hosted)
    return (out if col_sharded else out.reshape(N_CHIPS, m // N_CHIPS, n)), extra


def _row_tile(rows, cap=256):
    best = None
    for t in range(16, min(rows, cap) + 1, 16):
        if rows % t == 0:
            best = t
    return best if best is not None else rows


def _adamw_math(w, g, m, v):
    m2 = ADAM_B1 * m + (1.0 - ADAM_B1) * g
    v2 = ADAM_B2 * v + (1.0 - ADAM_B2) * (g * g)
    m_hat = m2 / (1.0 - ADAM_B1 ** ADAM_STEP)
    v_hat = v2 / (1.0 - ADAM_B2 ** ADAM_STEP)
    delta = -ADAM_LR * (m_hat / (jnp.sqrt(v_hat) + ADAM_EPS) + ADAM_WD * w)
    return delta, m2, v2


def _adamw(w, g, m, v, name):
    r, c = w.shape
    tr = _row_tile(r) if r >= 16 else r

    def body(w_ref, g_ref, m_ref, v_ref, d_ref, m2_ref, v2_ref):
        d_ref[...], m2_ref[...], v2_ref[...] = _adamw_math(w_ref[...], g_ref[...], m_ref[...], v_ref[...])

    sh = jax.ShapeDtypeStruct((r, c), F32)
    return pl.pallas_call(
        body, name=name, grid=(r // tr,),
        in_specs=[_rows(tr, c)] * 4, out_specs=[_rows(tr, c)] * 3, out_shape=(sh, sh, sh),
        compiler_params=_params("parallel"),
    )(w, g, m, v)


def _finalize(items, place, name, hosted=None, after=None):
    r, c = items[0][2].shape
    tr = _row_tile(r)
    nw = len(items)

    n_after = 0 if after is None else 1

    def body(place_ref, *refs):
        ins, outs = refs[:7 * nw], refs[7 * nw + n_after:]
        for k in range(nw):
            own_ref, s0_ref, s1_ref, s2_ref, w_ref, m_ref, v_ref = ins[7 * k:7 * k + 7]
            g_ref, d_ref, m2_ref, v2_ref = outs[4 * k:4 * k + 4]
            g = own_ref[...].astype(F32) + s0_ref[...].astype(F32)
            g = (g + s1_ref[...].astype(F32)) + s2_ref[...].astype(F32)
            g_ref[...] = g
            d_ref[...], m2_ref[...], v2_ref[...] = _adamw_math(w_ref[...], g, m_ref[...], v_ref[...])

    def slot(k):
        return pl.BlockSpec((None, tr, c), lambda i, pref: (k, i, 0))

    rows = pl.BlockSpec((tr, c), lambda i, pref: (i, 0))
    sh = jax.ShapeDtypeStruct((r, c), F32)
    one = [pl.BlockSpec((None, tr, c), lambda i, pref: (pref[1], i, 0)), slot(0), slot(1), slot(2), rows, rows, rows]
    args = [a for part, slots, w, m, v in items for a in (part, slots, slots, slots, w, m, v)]
    args += [] if after is None else [after]
    res, extra = _host_call(body, name, (r // tr,), one * nw + [ANY] * n_after, [rows] * (4 * nw), (sh,) * (4 * nw),
                            [], args, ("parallel",), hosted, prefetch=(place,))
    return [res[4 * k:4 * k + 4] for k in range(nw)], extra


def _small_sum_adamw(parts, w, m, v):
    nd, r, c = parts.shape

    def body(p_ref, w_ref, m_ref, v_ref, g_ref, d_ref, m2_ref, v2_ref):
        g = p_ref[0]
        for k in range(1, nd):
            g = g + p_ref[k]
        g_ref[...] = g
        d_ref[...], m2_ref[...], v2_ref[...] = _adamw_math(w_ref[...], g, m_ref[...], v_ref[...])

    sh = jax.ShapeDtypeStruct((r, c), F32)
    return pl.pallas_call(
        body, name="small_sum_adamw", out_shape=(sh, sh, sh, sh),
        compiler_params=pltpu.CompilerParams(vmem_limit_bytes=VMEM_LIMIT),
    )(parts, w, m, v)


def _place():
    x, y, c = lax.axis_index("x"), lax.axis_index("y"), lax.axis_index("c")
    chips = [(1 - x, y), (x, 1 - y), (1 - x, 1 - y)]
    return x, y, c, 2 * x + y, chips


def _remote(src, dst, send_sem, recv_sem, to):
    return pltpu.make_async_remote_copy(src_ref=src, dst_ref=dst, send_sem=send_sem, recv_sem=recv_sem,
                                        device_id=to, device_id_type=MESH)


class _Exchange:
    def __init__(self, ins, out_shapes, sem_shape, start, finish, middle=None, in_place=False, peers=()):
        self.ins, self.out_shapes, self.sem_shape = tuple(ins), tuple(out_shapes), sem_shape
        self.start, self.finish, self.middle = start, finish, middle
        self.in_place = in_place
        self.peers = frozenset(peers)
        self.collective_id = None

    def with_id(self, collective_id):
        self.collective_id = collective_id
        return self


_FLIPS = {"c": (0, 0, 1), "x": (1, 0, 0), "y": (0, 1, 0), "xy": (1, 1, 0)}


def _handshake(peers):
    x, y, c, _, _ = _place()
    barrier = pltpu.get_barrier_semaphore()
    for name in sorted(peers):
        fx, fy, fc = _FLIPS[name]
        peer = (1 - x if fx else x, 1 - y if fy else y, 1 - c if fc else c)
        pl.semaphore_signal(barrier, inc=1, device_id=peer, device_id_type=MESH)
    pl.semaphore_wait(barrier, len(peers))


def _run_exchange(ex, name, collective_id=None, casts=()):
    n_in, n_out, n_cast = len(ex.ins), len(ex.out_shapes), len(casts)
    step = 64

    def body(*refs):
        ins, srcs = refs[:n_in], refs[n_in:n_in + n_cast]
        outs = refs[n_in + n_cast:n_in + n_cast + n_out]
        dsts = refs[n_in + n_cast + n_out:n_in + 2 * n_cast + n_out]
        rest = refs[n_in + 2 * n_cast + n_out:]
        wide, narrow = rest[:n_cast], rest[n_cast:2 * n_cast]
        send_sems, recv_sems = rest[2 * n_cast], rest[2 * n_cast + 1]
        if collective_id is not None:
            _handshake(ex.peers)
        ex.start(ins, outs, send_sems, recv_sems)
        loads = [pltpu.make_async_copy(srcs[k], wide[k], rest[2 * n_cast + 2].at[k]) for k in range(n_cast)]
        for cp in loads:
            cp.start()
        if ex.middle is not None:
            ex.middle(ins, outs, send_sems, recv_sems)
        stores = []
        for k in range(n_cast):
            loads[k].wait()

            def chunk(i, carry, k=k):
                rows = pl.ds(pl.multiple_of(i * step, step), step)
                narrow[k][rows, :] = wide[k][rows, :].astype(BF16)
                return carry
            lax.fori_loop(0, casts[k].shape[0] // step, chunk, 0)
            stores.append(pltpu.make_async_copy(narrow[k], dsts[k], rest[2 * n_cast + 2].at[k]))
            stores[-1].start()
        ex.finish(ins, outs, send_sems, recv_sems)
        for cp in stores:
            cp.wait()

    assert collective_id is None or ex.peers
    sem = pltpu.SemaphoreType.DMA(ex.sem_shape)
    scratch = ([pltpu.VMEM(a.shape, F32) for a in casts] + [pltpu.VMEM(a.shape, BF16) for a in casts] + [sem, sem]
               + ([pltpu.SemaphoreType.DMA((n_cast,))] if n_cast else []))
    return pl.pallas_call(
        body, name=name,
        out_shape=ex.out_shapes + tuple(jax.ShapeDtypeStruct(a.shape, BF16) for a in casts),
        in_specs=[ANY] * (n_in + n_cast), out_specs=[ANY] * (n_out + n_cast),
        input_output_aliases={k: k for k in range(n_in)} if ex.in_place else {}, scratch_shapes=scratch,
        compiler_params=pltpu.CompilerParams(collective_id=collective_id, vmem_limit_bytes=VMEM_LIMIT),
    )(*ex.ins, *casts)


def _host_call(body, name, grid, in_specs, out_specs, out_shape, scratch_shapes, args, semantics, hosted,
               prefetch=()):
    hosted = [] if hosted is None else (list(hosted) if isinstance(hosted, (list, tuple)) else [hosted])
    collective_id = hosted[0].collective_id if hosted else None
    peers = frozenset().union(*[ex.peers for ex in hosted]) if hosted else frozenset()
    assert collective_id is None or all(ex.peers for ex in hosted)
    n_pre, n_in, n_out, n_scr = len(prefetch), len(in_specs), len(out_specs), len(scratch_shapes)
    h_ins = [a for ex in hosted for a in ex.ins]
    h_outs = [s for ex in hosted for s in ex.out_shapes]
    h_in, h_out = len(h_ins), len(h_outs)

    def wrapped(*refs):
        pre, refs = refs[:n_pre], refs[n_pre:]
        a, hi = refs[:n_in], refs[n_in:n_in + h_in]
        o = refs[n_in + h_in:n_in + h_in + n_out]
        ho = refs[n_in + h_in + n_out:n_in + h_in + n_out + h_out]
        scr = refs[n_in + h_in + n_out + h_out:]

        def run(phase):
            i0 = o0 = 0
            for k, ex in enumerate(hosted):
                fn = getattr(ex, phase)
                if fn is not None:
                    fn(hi[i0:i0 + len(ex.ins)], ho[o0:o0 + len(ex.out_shapes)], scr[n_scr + 2 * k],
                       scr[n_scr + 2 * k + 1])
                i0, o0 = i0 + len(ex.ins), o0 + len(ex.out_shapes)

        if hosted:
            first = functools.reduce(jnp.logical_and, [pl.program_id(k) == 0 for k in range(len(grid))])

            @pl.when(first)
            def _():
                if collective_id is not None:
                    _handshake(peers)
                run("start")

        if any(ex.middle is not None for ex in hosted):
            half_way = functools.reduce(jnp.logical_and, [
                pl.program_id(0) == max(1, grid[0] * MIDDLE_STEP_16THS // 16)] + [
                pl.program_id(k) == 0 for k in range(1, len(grid))])

            @pl.when(half_way)
            def _():
                run("middle")

        body(*pre, *a, *o, *scr[:n_scr])

        if hosted:
            last = functools.reduce(jnp.logical_and, [pl.program_id(k) == grid[k] - 1 for k in range(len(grid))])

            @pl.when(last)
            def _():
                run("finish")

    sems = [pltpu.SemaphoreType.DMA(ex.sem_shape) for ex in hosted for _ in range(2)]
    aliases, i0, o0 = {}, n_pre + n_in, n_out
    for ex in hosted:
        if ex.in_place:
            aliases.update({i0 + k: o0 + k for k in range(len(ex.ins))})
        i0, o0 = i0 + len(ex.ins), o0 + len(ex.out_shapes)
    all_in, all_out = list(in_specs) + [ANY] * h_in, list(out_specs) + [ANY] * h_out
    all_scr = list(scratch_shapes) + sems
    params = _params(*(["arbitrary"] * len(grid) if hosted else semantics), collective_id=collective_id)
    shapes = tuple(out_shape) + tuple(h_outs)
    if n_pre:
        call = pl.pallas_call(
            wrapped, name=name, out_shape=shapes, input_output_aliases=aliases, compiler_params=params,
            grid_spec=pltpu.PrefetchScalarGridSpec(num_scalar_prefetch=n_pre, grid=grid, in_specs=all_in,
                                                   out_specs=all_out, scratch_shapes=all_scr))
    else:
        call = pl.pallas_call(
            wrapped, name=name, grid=grid, in_specs=all_in, out_specs=all_out, out_shape=shapes,
            scratch_shapes=all_scr, input_output_aliases=aliases, compiler_params=params)
    res = call(*prefetch, *args, *h_ins)
    return res[:n_out], res[n_out:]


def _all_gather(shards, small=()):
    items = tuple(shards) + tuple(small)
    nw = len(shards)

    def place():
        x, y, c, me, _ = _place()
        first = (x + (1 - c) * (1 - 2 * x), y + c * (1 - 2 * y))
        second = (x + c * (1 - 2 * x), y + (1 - c) * (1 - 2 * y))
        diag = (1 - x, 1 - y)
        return x, y, c, me, (first, second, diag)

    def halves(w, c):
        rh = items[w].shape[0] // 2
        return pl.ds(c * rh, rh), pl.ds((1 - c) * rh, rh)

    def start(ins, outs, ss, rs):
        x, y, c, me, chips = place()
        for w in range(len(items)):
            _remote(ins[w], outs[w].at[me], ss.at[w, 6], rs.at[w, 6], (x, y, 1 - c)).start()
            if w < nw:
                mine, _ = halves(w, c)
                _remote(ins[w].at[mine], outs[w].at[me, mine], ss.at[w, 0], rs.at[w, 0], (*chips[0], c)).start()
            else:
                for k in range(3):
                    _remote(ins[w], outs[w].at[me], ss.at[w, k], rs.at[w, k], (*chips[k], c)).start()

    def onward(outs, ss, rs, w, k, x, y, c, chips):
        mine, _ = halves(w, c)
        pk = 2 * chips[k][0] + chips[k][1]
        got = outs[w].at[pk, mine]
        src = chips[1] if k == 2 else chips[k]
        _remote(got, got, ss.at[w, k], rs.at[w, k], (*src, c)).wait_recv()
        if k == 0:
            _remote(got, got, ss.at[w, 2], rs.at[w, 2], (*chips[1], c)).start()
        _remote(got, got, ss.at[w, 3 + k], rs.at[w, 3 + k], (x, y, 1 - c)).start()

    def middle(ins, outs, ss, rs):
        x, y, c, me, chips = place()
        for w in range(nw):
            mine, _ = halves(w, c)
            _remote(ins[w].at[mine], outs[w].at[me, mine], ss.at[w, 1], rs.at[w, 1], (*chips[1], c)).start()
        for w in range(nw):
            onward(outs, ss, rs, w, 0, x, y, c, chips)

    def finish(ins, outs, ss, rs):
        x, y, c, me, chips = place()
        sib = (x, y, 1 - c)
        for k in (1, 2):
            for w in range(nw):
                onward(outs, ss, rs, w, k, x, y, c, chips)
        for w in range(len(items)):
            if w < nw:
                mine, theirs = halves(w, c)
                for k, chip in ((3, chips[1]), (4, chips[0]), (5, chips[2])):
                    oth = outs[w].at[2 * chip[0] + chip[1], theirs]
                    _remote(oth, oth, ss.at[w, k], rs.at[w, k], sib).wait_recv()
                own = ins[w].at[mine]
                for k in range(6):
                    _remote(own, own, ss.at[w, k], rs.at[w, k], sib).wait_send()
            else:
                for k in range(3):
                    got = outs[w].at[2 * chips[k][0] + chips[k][1]]
                    _remote(got, got, ss.at[w, k], rs.at[w, k], (*chips[k], c)).wait_recv()
                    _remote(ins[w], ins[w], ss.at[w, k], rs.at[w, k], sib).wait_send()
            _remote(ins[w], outs[w].at[me], ss.at[w, 6], rs.at[w, 6], sib).wait()

    out_shapes = tuple(jax.ShapeDtypeStruct((N_CHIPS,) + a.shape, a.dtype) for a in items)
    return _Exchange(items, out_shapes, (len(items), 7), start, finish, middle if nw else None,
                     peers=("c", "x", "y", "xy") if small else ("c", "x", "y"))


def _chip_reduce(grads, name, collective_id):
    nw = len(grads)
    step = 64

    def body(*refs):
        ins, outs = refs[:nw], refs[nw:2 * nw]
        own, got = refs[2 * nw:3 * nw], refs[3 * nw:4 * nw]
        send_sems, recv_sems, local_sems = refs[4 * nw:]
        x, y, c, _, _ = _place()
        barrier = pltpu.get_barrier_semaphore()
        pl.semaphore_signal(barrier, inc=1, device_id=(x, y, 1 - c), device_id_type=MESH)
        pl.semaphore_wait(barrier, 1)
        moves = []
        for w in range(nw):
            nb, rh = grads[w].shape[0], grads[w].shape[1] // 2
            for k in range(nb):
                away = _remote(ins[w].at[k, pl.ds((1 - c) * rh, rh), :], got[w].at[k], send_sems.at[w, k],
                               recv_sems.at[w, k], (x, y, 1 - c))
                mine = pltpu.make_async_copy(ins[w].at[k, pl.ds(c * rh, rh), :], own[w].at[k], local_sems.at[w, k])
                away.start()
                mine.start()
                moves.append((w, k, away, mine))
        back = []
        for w, k, away, mine in moves:
            rh = own[w].shape[1]
            mine.wait()
            away.wait()

            def add(i, carry, w=w, k=k):
                rows = pl.ds(pl.multiple_of(i * step, step), step)
                own[w][k, rows, :] = (own[w][k, rows, :].astype(F32) + got[w][k, rows, :].astype(F32)).astype(BF16)
                return carry
            lax.fori_loop(0, rh // step, add, 0)
            tail = rh % step
            if tail:
                rows = slice(rh - tail, rh)
                own[w][k, rows, :] = (own[w][k, rows, :].astype(F32) + got[w][k, rows, :].astype(F32)).astype(BF16)
            wb = pltpu.make_async_copy(own[w].at[k], outs[w].at[k, pl.ds(c * rh, rh), :], local_sems.at[w, k])
            wb.start()
            back.append(wb)
        for wb in back:
            wb.wait()

    halves = [pltpu.VMEM((g.shape[0], g.shape[1] // 2, g.shape[2]), BF16) for g in grads]
    sem = pltpu.SemaphoreType.DMA((nw, N_CHIPS))
    return pl.pallas_call(
        body, name=name, out_shape=tuple(jax.ShapeDtypeStruct(g.shape, BF16) for g in grads),
        in_specs=[ANY] * nw, out_specs=[ANY] * nw, scratch_shapes=halves + halves + [sem, sem, sem],
        compiler_params=pltpu.CompilerParams(vmem_limit_bytes=VMEM_LIMIT, collective_id=collective_id),
    )(*grads)


def _scatter_partials(parts):
    nw = len(parts)

    def copies(ins, outs, ss, rs):
        _, _, c, _, chips = _place()
        res = []
        for r, (px, py) in enumerate(chips):
            for w in range(nw):
                rh = parts[w].shape[1] // 2
                rows = pl.ds(c * rh, rh)
                res.append(_remote(ins[w].at[2 * px + py, rows], outs[w].at[r, rows], ss.at[w, r], rs.at[w, r],
                                   (px, py, c)))
        return res

    def start(ins, outs, ss, rs):
        for cp in copies(ins, outs, ss, rs):
            cp.start()

    def finish(ins, outs, ss, rs):
        for cp in copies(ins, outs, ss, rs):
            cp.wait()

    out_shapes = tuple(jax.ShapeDtypeStruct((3,) + p.shape[1:], p.dtype) for p in parts)
    return _Exchange(parts, out_shapes, (nw, 3), start, finish, peers=("x", "y", "xy"))


_HBM = pl.BlockSpec(memory_space=pltpu.HBM)
_SEM = pl.BlockSpec(memory_space=pltpu.SEMAPHORE)
_EFFECT = pltpu.SideEffectType.DATAFLOW_SIDE_EFFECTING


class _SemGrid:
    def __init__(self, refs, shape):
        assert len(refs) == math.prod(shape)
        self.refs, self.shape, self.at = refs, shape, self

    def __getitem__(self, idx):
        flat = 0
        for i, n in zip(idx, self.shape, strict=True):
            flat = flat * n + i
        return self.refs[flat]


def _split_phase(exs, arrays, sems, phase):
    a0 = s0 = 0
    for ex in exs:
        n_in, n_sem = len(ex.ins), math.prod(ex.sem_shape)
        n_arr = n_in if ex.in_place else n_in + len(ex.out_shapes)
        ins = arrays[a0:a0 + n_in]
        outs = ins if ex.in_place else arrays[a0 + n_in:a0 + n_arr]
        getattr(ex, phase)(ins, outs, _SemGrid(sems[s0:s0 + n_sem], ex.sem_shape),
                           _SemGrid(sems[s0 + n_sem:s0 + 2 * n_sem], ex.sem_shape))
        a0, s0 = a0 + n_arr, s0 + 2 * n_sem


def _split_start(exs, name, collective_id):
    assert all(ex.middle is None and ex.peers for ex in exs)
    arrays = [a for ex in exs for a in ex.ins + (() if ex.in_place else tuple(
        lax.empty(s.shape, s.dtype) for s in ex.out_shapes))]
    n_arr, n_sem = len(arrays), 2 * sum(math.prod(ex.sem_shape) for ex in exs)
    peers = frozenset().union(*[ex.peers for ex in exs])

    def body(*refs):
        _handshake(peers)
        _split_phase(exs, refs[:n_arr], refs[n_arr:n_arr + n_sem], "start")
        refs[-1][...] = jnp.zeros_like(refs[-1])

    res = pl.pallas_call(
        body, name=name + "_start",
        out_shape=(pltpu.SemaphoreType.DMA(()),) * n_sem + tuple(pltpu.HBM(a.shape, a.dtype) for a in arrays)
        + (jax.ShapeDtypeStruct((8, 128), F32),),
        in_specs=(_HBM,) * n_arr,
        out_specs=(_SEM,) * n_sem + (_HBM,) * n_arr + (pl.BlockSpec(memory_space=pltpu.VMEM),),
        input_output_aliases={k: n_sem + k for k in range(n_arr)},
        compiler_params=pltpu.CompilerParams(has_side_effects=_EFFECT, collective_id=collective_id),
    )(*[pltpu.with_memory_space_constraint(a, pltpu.HBM) for a in arrays])
    return res[:n_sem], res[n_sem:n_sem + n_arr], res[-1]


def _split_wait(exs, name, sems, thru, after):
    n_arr, n_sem = len(thru), len(sems)

    def body(*refs):
        _split_phase(exs, refs[:n_arr], refs[n_arr:n_arr + n_sem], "finish")

    return pl.pallas_call(
        body, name=name + "_wait", out_shape=tuple(pltpu.HBM(a.shape, a.dtype) for a in thru),
        in_specs=(_HBM,) * n_arr + (_SEM,) * n_sem + (ANY,) * len(after), out_specs=(_HBM,) * n_arr,
        input_output_aliases={k: k for k in range(n_arr)},
        compiler_params=pltpu.CompilerParams(has_side_effects=_EFFECT),
    )(*thru, *sems, *after)


def _join_partials(parts, slots):
    nw = len(parts)

    def copies(outs, ss, rs, mine):
        x, y, c, me, _ = _place()
        res = []
        for w in range(nw):
            rh = parts[w].shape[1] // 2
            rows = pl.ds((c if mine else 1 - c) * rh, rh)
            own = outs[w].at[me, rows]
            got = outs[nw + w].at[:, rows, :]
            res.append(_remote(own, own, ss.at[w, 0], rs.at[w, 0], (x, y, 1 - c)))
            res.append(_remote(got, got, ss.at[w, 1], rs.at[w, 1], (x, y, 1 - c)))
        return res

    def start(ins, outs, ss, rs):
        for cp in copies(outs, ss, rs, True):
            cp.start()

    def finish(ins, outs, ss, rs):
        for cp in copies(outs, ss, rs, True):
            cp.wait_send()
        for cp in copies(outs, ss, rs, False):
            cp.wait_recv()

    arrays = tuple(parts) + tuple(slots)
    return _Exchange(arrays, tuple(jax.ShapeDtypeStruct(a.shape, a.dtype) for a in arrays), (nw, 2), start, finish,
                     in_place=True, peers=("c",))


def _gather_small(slab):
    def copies(ins, outs, ss, rs):
        x, y, c, _, _ = _place()
        me = 4 * x + 2 * y + c
        out, arrivals = [], []
        for k in range(1, 8):
            px = 1 - x if k & 4 else x
            py = 1 - y if k & 2 else y
            pc = 1 - c if k & 1 else c
            out.append(_remote(ins[0], outs[0].at[me], ss.at[k - 1], rs.at[k - 1], (px, py, pc)))
            theirs = outs[0].at[4 * px + 2 * py + pc]
            arrivals.append((theirs, k - 1, (px, py, pc)))
        return pltpu.make_async_copy(ins[0], outs[0].at[me], ss.at[7]), out, arrivals

    def start(ins, outs, ss, rs):
        own, out, _ = copies(ins, outs, ss, rs)
        own.start()
        for cp in out:
            cp.start()

    def finish(ins, outs, ss, rs):
        own, out, arrivals = copies(ins, outs, ss, rs)
        for cp in out:
            cp.wait_send()
        for theirs, k, peer in arrivals:
            _remote(theirs, theirs, ss.at[k], rs.at[k], peer).wait_recv()
        own.wait()

    return _Exchange((slab,), (jax.ShapeDtypeStruct((8,) + slab.shape, slab.dtype),), (8,), start, finish)


_SMALL_VECS = ("ln_mix_g", "ln_attn_g", "ln_mem_g", "ln_ffn_g", "ln_final_g")


def _pack_small(p, extra, conv):
    d = p["ln_mix_g"].shape[-1]
    top = [p[k].reshape(1, d) for k in _SMALL_VECS]
    top.append(jnp.concatenate([p["sgu_ln_g"].reshape(-1), p["sgu_ln_b"].reshape(-1)]).reshape(1, d))
    top.append(jnp.concatenate([p["grp_norm_a"].reshape(-1), p["grp_norm_b"].reshape(-1)]).reshape(1, d))
    top.append(jnp.concatenate([p["b_spatial"].reshape(-1), extra]).reshape(1, d))
    mid = jnp.zeros((8, d), F32)
    if conv is not None:
        mid = jnp.pad(conv, ((0, 5), (0, d - conv.shape[1])))
    return jnp.concatenate([jnp.concatenate(top, axis=0), mid, p["w_spatial"].reshape(-1, d)], axis=0)


def _unpack_small(slab):
    d = slab.shape[1]
    hw = d // 2
    out = {k: slab[i] for i, k in enumerate(_SMALL_VECS)}
    out["sgu_ln_g"], out["sgu_ln_b"] = slab[5, :hw], slab[5, hw:]
    out["grp_norm_a"], out["grp_norm_b"] = slab[6, :hw], slab[6, hw:]
    out["b_spatial"] = slab[7, :hw].reshape(HEADS, CHUNK)
    out["w_spatial"] = slab[16:].reshape(HEADS, CHUNK, CHUNK)
    return out


_BIG = ("w_in", "w_kv", "w_gate_up", "w_out", "w_q", "w_o", "w_down")
_WEIGHTS = ("ln_mix_g", "w_in", "sgu_ln_g", "sgu_ln_b", "w_spatial", "b_spatial", "conv_w", "grp_norm_a",
            "grp_norm_b", "w_out", "ln_attn_g", "ln_mem_g", "w_q", "w_kv", "w_o", "ln_ffn_g", "w_gate_up",
            "w_down", "ln_final_g")


def _step(p, m_, v_, x, mem, target):
    s, d = x.shape
    hw = d // 2
    row = lambda a: a.reshape(1, -1)
    x_, y_, c_ = lax.axis_index("x"), lax.axis_index("y"), lax.axis_index("c")
    chip = 2 * x_ + y_

    conv8 = jnp.pad(p["conv_w"], ((0, 5), (0, 0)))
    later = ("w_kv", "w_q", "w_o", "w_down", "w_gate_up")
    first = _run_exchange(_all_gather([p["w_in"].astype(BF16), p["w_out"].astype(BF16)], [conv8]),
                          "all_gather_mixer", collective_id=4, casts=[p[k] for k in later])
    (w_in, w_out4, conv4), bf = first[:3], dict(zip(later, first[3:]))
    cw = jnp.transpose(conv4[:, :3, :], (1, 0, 2)).reshape(3, hw)
    b_t = jnp.pad(jnp.transpose(p["b_spatial"]), ((0, 0), (0, CHUNK - HEADS)))
    g1, g2, gm, g3, gf = (row(p[k]) for k in _SMALL_VECS)
    lng, lnb, ga, gb = row(p["sgu_ln_g"]), row(p["sgu_ln_b"]), row(p["grp_norm_a"]), row(p["grp_norm_b"])
    wsp = p["w_spatial"]
    w_out = w_out4.reshape(-1, d)

    (h, x1, ycat, xn1, mixed, th), (w_kv, w_q4, w_o4, w_down4) = _mixer_fwd(
        x, g1, w_in, lng, lnb, wsp, b_t, cw, ga, gb, w_out,
        hosted=_all_gather([bf[k] for k in ("w_kv", "w_q", "w_o", "w_down")]).with_id(5))
    w_q, w_o, w_down = (a.reshape(-1, d) for a in (w_q4, w_o4, w_down4))
    (x2, o, qs, probs, memn, kv), (w_gu,) = _attn_fwd(x1, mem, g2, gm, w_q, w_kv, w_o,
                                                      hosted=_all_gather([bf["w_gate_up"]]).with_id(6))
    dx2, act, dgu, xn3, dx3, loss, dgf, dg3 = _ffn_fwd_bwd(x2, g3, gf, target, w_gu, w_down)

    place = jnp.stack([c_, chip]).astype(jnp.int32)

    def chip_partials(names, grads, tag):
        return list(_chip_reduce(grads, "chip_reduce_" + tag, ("down", "ffn", "attn", "mixer").index(tag)))

    names_d = ("w_down",)
    parts_d = chip_partials(names_d, (_weight_grad(act, dx3, "grad_w_down", 1408, 512, False)[0],), "down")
    g_gu, slots_d = _weight_grad(xn3, dgu, "grad_w_gate_up", 512, 1408, True,
                                 hosted=_scatter_partials(parts_d).with_id(7))
    names_a = ("w_gate_up",)
    parts_a = chip_partials(names_a, (g_gu,), "ffn")
    (dx1, dkv, dg2, g_o, g_q, g_out), slots_a = _attn_bwd(x1, dx2, o, ycat, qs, probs, g2, w_q, kv, w_o,
                                                           hosted=_scatter_partials(parts_a).with_id(8))
    g_kv, dgm = _kv_bwd(dkv, mem, memn, gm, w_kv)
    names_b = ("w_o", "w_out", "w_q", "w_kv")
    shard_major = lambda g: g.reshape(N_CHIPS, -1, d)
    parts_b = chip_partials(names_b, (shard_major(g_o), shard_major(g_out), shard_major(g_q), g_kv), "attn")
    names_da = names_d + names_a
    exs_b = [_scatter_partials(parts_b), _join_partials(parts_d + parts_a, slots_d + slots_a)]
    sems_b, thru_b, token_b = _split_start(exs_b, "rs_scatter_attn", 9)
    dx, dh, dg1, dlng, dlnb, dwsp, dbt, dcw, dga, dgb = _mixer_bwd(
        x, dx1, h, mixed, th, g1, lng, lnb, wsp, b_t, cw, ga, gb, w_out, w_in, after=token_b)
    thru_b = _split_wait(exs_b, "rs_scatter_attn", sems_b, thru_b, [dh])
    nb = len(names_b)
    parts_b, slots_b, joined = thru_b[:nb], thru_b[nb:2 * nb], thru_b[2 * nb:]
    whole = dict(zip(names_da, zip(joined[:len(names_da)], joined[len(names_da):])))
    small = {"ln_mix_g": dg1, "ln_attn_g": dg2, "ln_mem_g": dgm, "ln_ffn_g": dg3, "ln_final_g": dgf,
             "sgu_ln_g": dlng, "sgu_ln_b": dlnb, "grp_norm_a": dga, "grp_norm_b": dgb,
             "b_spatial": jnp.transpose(dbt[:, :HEADS]), "w_spatial": dwsp}
    loss_vec = jnp.pad(loss.reshape(1), (0, hw - 1))
    g_in, extra = _weight_grad(
        xn1, dh, "grad_w_in", 1024, 640, True,
        hosted=[_gather_small(_pack_small(small, loss_vec, dcw)), _join_partials(parts_b, slots_b)])
    parts = extra[0]
    whole.update(zip(names_b, zip(extra[1:1 + len(names_b)], extra[1 + len(names_b):])))
    (part_in,) = chip_partials(("w_in",), (g_in,), "mixer")
    out_g, out_d, out_m, out_v = {}, {}, {}, {}

    def finalize(ks, tag, after=None):
        done, _ = _finalize([(whole[k][0], whole[k][1], p[k], m_[k], v_[k]) for k in ks], place,
                            "finalize_" + tag, after=after)
        for k, (g, dl, nm, nv) in zip(ks, done):
            out_g[k], out_d[k], out_m[k], out_v[k] = g, dl, nm, nv

    exs_in = [_scatter_partials([part_in])]
    sems, thru_in, token = _split_start(exs_in, "rs_scatter_in", 10)
    finalize(("w_down",), "w_down", after=token)
    finalize(("w_o", "w_out", "w_q"), "attn", after=token)
    finalize(("w_gate_up",), "w_gate_up", after=token)
    finalize(("w_kv",), "w_kv", after=token)
    part_in, slots_in = _split_wait(exs_in, "rs_scatter_in", sems, thru_in,
                                    [out_v[k] for k in ("w_down", "w_o", "w_gate_up", "w_kv")])
    whole["w_in"] = _run_exchange(_join_partials([part_in], [slots_in]), "rs_join_mixer", collective_id=11)
    finalize(("w_in",), "w_in")

    zeros = jnp.zeros((hw,), F32)
    sg, sd, sm, sv = _small_sum_adamw(parts, _pack_small(p, zeros, None), _pack_small(m_, zeros, None),
                                      _pack_small(v_, zeros, None))
    for tree, slab in zip((out_g, out_d, out_m, out_v), (sg, sd, sm, sv)):
        tree.update(_unpack_small(slab))
    loss_out = sg[7, hw]
    g_conv = lax.dynamic_slice(sg[8:11, :hw], (0, chip * (hw // N_CHIPS)), (3, hw // N_CHIPS))
    out_g["conv_w"] = g_conv
    out_d["conv_w"], out_m["conv_w"], out_v["conv_w"] = _adamw(p["conv_w"], g_conv, m_["conv_w"], v_["conv_w"],
                                                                "adamw_conv_w")
    return loss_out, dx, out_g, out_d, out_m, out_v


def kernel(x, mem, ln_mix_g, w_in, sgu_ln_g, sgu_ln_b, w_spatial, b_spatial, conv_w, grp_norm_a, grp_norm_b, w_out, ln_attn_g, ln_mem_g, w_q, w_kv, w_o, ln_ffn_g, w_gate_up, w_down, ln_final_g, loss_target, m_ln_mix_g, m_w_in, m_sgu_ln_g, m_sgu_ln_b, m_w_spatial, m_b_spatial, m_conv_w, m_grp_norm_a, m_grp_norm_b, m_w_out, m_ln_attn_g, m_ln_mem_g, m_w_q, m_w_kv, m_w_o, m_ln_ffn_g, m_w_gate_up, m_w_down, m_ln_final_g, v_ln_mix_g, v_w_in, v_sgu_ln_g, v_sgu_ln_b, v_w_spatial, v_b_spatial, v_conv_w, v_grp_norm_a, v_grp_norm_b, v_w_out, v_ln_attn_g, v_ln_mem_g, v_w_q, v_w_kv, v_w_o, v_ln_ffn_g, v_w_gate_up, v_w_down, v_ln_final_g):
    p = dict(ln_mix_g=ln_mix_g, w_in=w_in, sgu_ln_g=sgu_ln_g, sgu_ln_b=sgu_ln_b, w_spatial=w_spatial,
             b_spatial=b_spatial, conv_w=conv_w, grp_norm_a=grp_norm_a, grp_norm_b=grp_norm_b, w_out=w_out,
             ln_attn_g=ln_attn_g, ln_mem_g=ln_mem_g, w_q=w_q, w_kv=w_kv, w_o=w_o, ln_ffn_g=ln_ffn_g,
             w_gate_up=w_gate_up, w_down=w_down, ln_final_g=ln_final_g)
    m_ = dict(ln_mix_g=m_ln_mix_g, w_in=m_w_in, sgu_ln_g=m_sgu_ln_g, sgu_ln_b=m_sgu_ln_b, w_spatial=m_w_spatial,
              b_spatial=m_b_spatial, conv_w=m_conv_w, grp_norm_a=m_grp_norm_a, grp_norm_b=m_grp_norm_b,
              w_out=m_w_out, ln_attn_g=m_ln_attn_g, ln_mem_g=m_ln_mem_g, w_q=m_w_q, w_kv=m_w_kv, w_o=m_w_o,
              ln_ffn_g=m_ln_ffn_g, w_gate_up=m_w_gate_up, w_down=m_w_down, ln_final_g=m_ln_final_g)
    v_ = dict(ln_mix_g=v_ln_mix_g, w_in=v_w_in, sgu_ln_g=v_sgu_ln_g, sgu_ln_b=v_sgu_ln_b, w_spatial=v_w_spatial,
              b_spatial=v_b_spatial, conv_w=v_conv_w, grp_norm_a=v_grp_norm_a, grp_norm_b=v_grp_norm_b,
              w_out=v_w_out, ln_attn_g=v_ln_attn_g, ln_mem_g=v_ln_mem_g, w_q=v_w_q, w_kv=v_w_kv, w_o=v_w_o,
              ln_ffn_g=v_ln_ffn_g, w_gate_up=v_w_gate_up, w_down=v_w_down, ln_final_g=v_ln_final_g)
    s, d = x.shape[-2], x.shape[-1]
    loss, dx, g, dl, nm, nv = _step(p, m_, v_, x.reshape(s, d), mem.reshape(-1, d), loss_target.reshape(s, d))
    outs = [loss, dx.reshape(x.shape)]
    for tree in (g, dl, nm, nv):
        outs += [tree[k].reshape(p[k].shape) for k in _WEIGHTS]
    return tuple(outs)
```

```python
import functools
import math

import jax
import jax.numpy as jnp
from jax import lax
from jax.experimental import pallas as pl
from jax.experimental.pallas import tpu as pltpu

F32 = jnp.float32
BF16 = jnp.bfloat16
EPS = 1e-6
CHUNK = 128
HEADS = 4
N_CHIPS = 4
TM = 512
TM_ATTN = 512
TM_FFN = 256
ADAM_LR, ADAM_B1, ADAM_B2, ADAM_EPS, ADAM_WD, ADAM_STEP = 0.001, 0.9, 0.999, 1e-08, 0.01, 10
GELU_C = math.sqrt(2.0 / math.pi)
GELU_K = 0.044715
SMALL_ROWS = 80
VMEM_LIMIT = 56 * 1024 * 1024
MIDDLE_STEP_16THS = 7
MESH = pl.DeviceIdType.MESH
ANY = pl.BlockSpec(memory_space=pl.ANY)


def _params(*sem, collective_id=None):
    return pltpu.CompilerParams(dimension_semantics=sem, vmem_limit_bytes=VMEM_LIMIT, collective_id=collective_id)


def _dot(a, b):
    return jnp.dot(a, b, preferred_element_type=F32)


def _dot_nt(a, b):
    return lax.dot_general(a, b, (((1,), (1,)), ((), ())), preferred_element_type=F32)


def _dot_tn(a, b):
    return lax.dot_general(a, b, (((0,), (0,)), ((), ())), preferred_element_type=F32)


def _rms_fwd(x, g):
    r = lax.rsqrt(jnp.mean(x * x, axis=-1, keepdims=True) + EPS)
    xh = x * r
    return xh * g, xh, r


def _rms_bwd(dy, xh, r, g):
    dxh = dy * g
    dx = r * (dxh - xh * jnp.mean(dxh * xh, axis=-1, keepdims=True))
    return dx, jnp.sum(dy * xh, axis=0, keepdims=True)


def _full(shape):
    nd = len(shape)
    return pl.BlockSpec(shape, lambda *_: (0,) * nd, pipeline_mode=pl.Buffered(1))


def _acc(shape):
    nd = len(shape)
    return pl.BlockSpec(shape, lambda *_: (0,) * nd)


def _rows(tm, cols):
    return pl.BlockSpec((tm, cols), lambda i: (i, 0))


def _tril_weights(wsp_ref):
    row = lax.broadcasted_iota(jnp.int32, (CHUNK, CHUNK), 0)
    col = lax.broadcasted_iota(jnp.int32, (CHUNK, CHUNK), 1)
    return [jnp.where(row >= col, wsp_ref[hd], 0.0).astype(BF16) for hd in range(HEADS)]


def _shift_rows(z, zp):
    row = lax.broadcasted_iota(jnp.int32, z.shape, 0)
    zm1 = jnp.where(row == 0, zp[7:8, :], pltpu.roll(z, 1, 0))
    zm2 = jnp.where(row == 0, zp[6:7, :], jnp.where(row == 1, zp[7:8, :], pltpu.roll(z, 2, 0)))
    return zm1, zm2


def _gelu_parts(x):
    t = jnp.tanh(GELU_C * (x + GELU_K * (x * x * x)))
    return 0.5 * x * (1.0 + t), t


def _layer_norm_parts(v, g, b):
    mu = jnp.mean(v, axis=-1, keepdims=True)
    vc = v - mu
    rs = lax.rsqrt(jnp.mean(vc * vc, axis=-1, keepdims=True) + EPS)
    vhat = vc * rs
    return vhat * g + b, vhat, rs


def _mixer_fwd(x, g1, w_in, lng, lnb, wsp, b_t, cw, ga, gb, w_out, hosted=None):
    s, d = x.shape
    n = s // TM
    nch = TM // CHUNK
    ns = w_in.shape[2]
    nh = N_CHIPS * ns
    aw = d // 2
    hd_w = aw // HEADS

    def body(x_ref, g1_ref, win_ref, lng_ref, lnb_ref, wsp_ref, bt_ref, cw_ref, ga_ref, gb_ref, wout_ref,
             h_ref, x1_ref, y_ref, xn_ref, mix_ref, th_ref, zp_ref):
        i = pl.program_id(0)

        @pl.when(i == 0)
        def _():
            zp_ref[...] = jnp.zeros_like(zp_ref)

        x = x_ref[...]
        xn, _, _ = _rms_fwd(x, g1_ref[...])
        xnb = xn.astype(BF16)
        xn_ref[...] = xnb
        for k in range(N_CHIPS):
            h_ref[:, k * ns:(k + 1) * ns] = _dot(xnb, win_ref[k])
        a, th = _gelu_parts(h_ref[:, 0:2 * aw])
        th_ref[...] = th
        u = a[:, :aw]
        vn, _, _ = _layer_norm_parts(a[:, aw:], lng_ref[...], lnb_ref[...])
        vnb = vn.astype(BF16)
        wm = _tril_weights(wsp_ref)
        for c in range(nch):
            for hd in range(HEADS):
                blk = vnb[c * CHUNK:(c + 1) * CHUNK, hd * hd_w:(hd + 1) * hd_w]
                mix_ref[c * CHUNK:(c + 1) * CHUNK, hd * hd_w:(hd + 1) * hd_w] = _dot(wm[hd], blk) + bt_ref[:, hd:hd + 1]
        ya, _, _ = _rms_fwd(u * mix_ref[...], ga_ref[...])
        g_b = h_ref[:, 2 * aw:3 * aw]
        z = h_ref[:, 3 * aw:4 * aw] * h_ref[:, 4 * aw:5 * aw]
        zm1, zm2 = _shift_rows(z, zp_ref[...])
        conv = cw_ref[0:1, :] * zm2 + cw_ref[1:2, :] * zm1 + cw_ref[2:3, :] * z
        yb, _, _ = _rms_fwd(g_b * conv, gb_ref[...])
        zp_ref[...] = z[TM - 8:TM, :]
        ycat = jnp.concatenate([ya, yb], axis=-1).astype(BF16)
        y_ref[...] = ycat
        x1_ref[...] = x + _dot(ycat, wout_ref[...])

    return _host_call(
        body, "mixer_fwd", (n,),
        [_rows(TM, d), _full(g1.shape), _full(w_in.shape), _full(lng.shape), _full(lnb.shape),
         _full(wsp.shape), _full(b_t.shape), _full(cw.shape), _full(ga.shape), _full(gb.shape),
         _full(w_out.shape)],
        [_rows(TM, nh), _rows(TM, d), _rows(TM, d), _rows(TM, d), _rows(TM, aw), _rows(TM, d)],
        (jax.ShapeDtypeStruct((s, nh), F32), jax.ShapeDtypeStruct((s, d), F32),
         jax.ShapeDtypeStruct((s, d), BF16), jax.ShapeDtypeStruct((s, d), BF16),
         jax.ShapeDtypeStruct((s, aw), F32), jax.ShapeDtypeStruct((s, d), F32)),
        [pltpu.VMEM((8, aw), F32)],
        (x, g1, w_in, lng, lnb, wsp, b_t, cw, ga, gb, w_out), ("arbitrary",), hosted)


def _attn_fwd(x1, mem, g2, g_mem, w_q, w_kv, w_o, hosted=None):
    s, d = x1.shape
    tm = min(TM_ATTN, s)
    n = s // tm
    dh = d // HEADS
    m = mem.shape[0]
    ns = w_kv.shape[2]
    scale = dh ** -0.5

    def body(x1_ref, mem_ref, g2_ref, gm_ref, wq_ref, wkv_ref, wo_ref, x2_ref, o_ref, q_ref, p_ref, memn_ref, kv_ref):
        @pl.when(pl.program_id(0) == 0)
        def _():
            y, _, _ = _rms_fwd(mem_ref[...], gm_ref[...])
            yb = y.astype(BF16)
            memn_ref[...] = yb
            for k in range(N_CHIPS):
                kv_ref[:, k * ns:(k + 1) * ns] = _dot(yb, wkv_ref[k]).astype(BF16)

        x1v = x1_ref[...]
        xn, _, _ = _rms_fwd(x1v, g2_ref[...])
        q_ref[...] = _dot(xn.astype(BF16), wq_ref[...]).astype(BF16)
        for hd in range(HEADS):
            kh = kv_ref[:, hd * dh:(hd + 1) * dh]
            vh = kv_ref[:, d + hd * dh:d + (hd + 1) * dh]
            sc = _dot_nt(q_ref[:, hd * dh:(hd + 1) * dh], kh) * scale
            e = jnp.exp(sc - jnp.max(sc, axis=-1, keepdims=True))
            p = e / jnp.sum(e, axis=-1, keepdims=True)
            p_ref[:, hd * m:(hd + 1) * m] = p
            o_ref[:, hd * dh:(hd + 1) * dh] = _dot(p.astype(BF16), vh).astype(BF16)
        x2_ref[...] = x1v + _dot(o_ref[...], wo_ref[...])

    return _host_call(
        body, "attn_fwd", (n,),
        [_rows(tm, d), _full(mem.shape), _full(g2.shape), _full(g_mem.shape), _full(w_q.shape), _full(w_kv.shape),
         _full(w_o.shape)],
        [_rows(tm, d), _rows(tm, d), _rows(tm, d), _rows(tm, HEADS * m), _acc((m, d)), _acc((m, 2 * d))],
        (jax.ShapeDtypeStruct((s, d), F32), jax.ShapeDtypeStruct((s, d), BF16), jax.ShapeDtypeStruct((s, d), BF16),
         jax.ShapeDtypeStruct((s, HEADS * m), F32), jax.ShapeDtypeStruct((m, d), BF16),
         jax.ShapeDtypeStruct((m, 2 * d), BF16)),
        [], (x1, mem, g2, g_mem, w_q, w_kv, w_o), ("arbitrary",), hosted)


def _ffn_fwd_bwd(x2, g3, gf, target, w_gu, w_down):
    s, d = x2.shape
    tm = min(TM_FFN, s)
    n = s // tm
    ns = w_gu.shape[2]
    ff = 2 * ns

    def body(x2_ref, g3_ref, gf_ref, t_ref, wgu_ref, wd_ref,
             dx2_ref, act_ref, dgu_ref, xn_ref, dx3_ref, loss_ref, dgf_ref, dg3_ref):
        i = pl.program_id(0)

        @pl.when(i == 0)
        def _():
            loss_ref[...] = jnp.zeros_like(loss_ref)
            dgf_ref[...] = jnp.zeros_like(dgf_ref)
            dg3_ref[...] = jnp.zeros_like(dg3_ref)

        x2v = x2_ref[...]
        xn, xh3, r3 = _rms_fwd(x2v, g3_ref[...])
        xnb = xn.astype(BF16)
        xn_ref[...] = xnb
        x3 = x2v
        saved = []
        for j in range(2):
            g = _dot(xnb, wgu_ref[j])
            u = _dot(xnb, wgu_ref[2 + j])
            sg = 1.0 / (1.0 + jnp.exp(-g))
            sl = g * sg
            actb = (sl * u).astype(BF16)
            act_ref[:, j * ns:(j + 1) * ns] = actb
            x3 = x3 + _dot(actb, wd_ref[j * ns:(j + 1) * ns, :])
            saved.append((u, sl, sg * (1.0 + g * (1.0 - sg))))
        gfv = gf_ref[...]
        y, xhf, rf = _rms_fwd(x3, gfv)
        e = y - t_ref[...]
        loss_ref[...] += 0.5 * jnp.sum(jnp.sum(e * e, axis=-1, keepdims=True), axis=0, keepdims=True) / d
        dx3, dgf = _rms_bwd(e / d, xhf, rf, gfv)
        dgf_ref[...] += dgf
        dx3b = dx3.astype(BF16)
        dx3_ref[...] = dx3b
        dxn = jnp.zeros_like(x2v)
        for j in range(2):
            u, sl, dsl = saved[j]
            dact = _dot_nt(dx3b, wd_ref[j * ns:(j + 1) * ns, :])
            dgb = (dact * u * dsl).astype(BF16)
            dub = (dact * sl).astype(BF16)
            dgu_ref[:, j * ns:(j + 1) * ns] = dgb
            dgu_ref[:, ff + j * ns:ff + (j + 1) * ns] = dub
            dxn = dxn + _dot_nt(dgb, wgu_ref[j]) + _dot_nt(dub, wgu_ref[2 + j])
        dxr, dg3 = _rms_bwd(dxn, xh3, r3, g3_ref[...])
        dg3_ref[...] += dg3
        dx2_ref[...] = dx3 + dxr

    vec = jax.ShapeDtypeStruct((1, d), F32)
    return pl.pallas_call(
        body, name="ffn_fwd_bwd", grid=(n,),
        in_specs=[_rows(tm, d), _full(g3.shape), _full(gf.shape), _rows(tm, d), _full(w_gu.shape),
                  _full(w_down.shape)],
        out_specs=[_rows(tm, d), _rows(tm, ff), _rows(tm, 2 * ff), _rows(tm, d), _rows(tm, d),
                   _acc((1, 1)), _acc((1, d)), _acc((1, d))],
        out_shape=(jax.ShapeDtypeStruct((s, d), F32), jax.ShapeDtypeStruct((s, ff), BF16),
                   jax.ShapeDtypeStruct((s, 2 * ff), BF16), jax.ShapeDtypeStruct((s, d), BF16),
                   jax.ShapeDtypeStruct((s, d), BF16), jax.ShapeDtypeStruct((1, 1), F32), vec, vec),
        compiler_params=_params("arbitrary"),
    )(x2, g3, gf, target, w_gu, w_down)


def _attn_bwd(x1, dx2, o, ycat, qs, probs, g2, w_q, kv, w_o, hosted=None):
    s, d = x1.shape
    tm = min(TM_ATTN, s)
    n = s // tm
    dh = d // HEADS
    scale = dh ** -0.5
    m = kv.shape[0]

    def body(x1_ref, dx2_ref, o_ref, y_ref, q_ref, p_ref, g2_ref, wq_ref, kv_ref, wo_ref,
             dx1_ref, dkv_ref, dg2_ref, gwo_out, gwq_out, gwout_out, dq_ref, gwo_ref, gwq_ref, gwout_ref):
        i = pl.program_id(0)

        @pl.when(i == 0)
        def _():
            for r in (dkv_ref, dg2_ref, gwo_ref, gwq_ref, gwout_ref):
                r[...] = jnp.zeros_like(r)

        xn, xh2, r2 = _rms_fwd(x1_ref[...], g2_ref[...])
        xnb = xn.astype(BF16)
        dx2v = dx2_ref[...]
        dx2b = dx2v.astype(BF16)
        gwo_ref[...] += _dot_tn(o_ref[...], dx2b)
        do = _dot_nt(dx2b, wo_ref[...])
        for hd in range(HEADS):
            qb = q_ref[:, hd * dh:(hd + 1) * dh]
            p = p_ref[:, hd * m:(hd + 1) * m]
            kh = kv_ref[:, hd * dh:(hd + 1) * dh]
            vh = kv_ref[:, d + hd * dh:d + (hd + 1) * dh]
            dob = do[:, hd * dh:(hd + 1) * dh].astype(BF16)
            dp = _dot_nt(dob, vh)
            ds = p * (dp - jnp.sum(dp * p, axis=-1, keepdims=True))
            dsb = (ds * scale).astype(BF16)
            dq_ref[:, hd * dh:(hd + 1) * dh] = _dot(dsb, kh).astype(BF16)
            dkv_ref[:, hd * dh:(hd + 1) * dh] += _dot_tn(dsb, qb)
            dkv_ref[:, d + hd * dh:d + (hd + 1) * dh] += _dot_tn(p.astype(BF16), dob)
        dqb = dq_ref[...]
        gwq_ref[...] += _dot_tn(xnb, dqb)
        dxn = _dot_nt(dqb, wq_ref[...])
        dxr, dg2 = _rms_bwd(dxn, xh2, r2, g2_ref[...])
        dg2_ref[...] += dg2
        dx1 = dx2v + dxr
        dx1_ref[...] = dx1
        gwout_ref[...] += _dot_tn(y_ref[...], dx1.astype(BF16))

        @pl.when(i == n - 1)
        def _():
            for acc, out in ((gwo_ref, gwo_out), (gwq_ref, gwq_out), (gwout_ref, gwout_out)):
                out[...] = acc[...].astype(BF16)

    sq = jax.ShapeDtypeStruct((d, d), BF16)
    return _host_call(
        body, "attn_bwd", (n,),
        [_rows(tm, d), _rows(tm, d), _rows(tm, d), _rows(tm, d), _rows(tm, d), _rows(tm, HEADS * m),
         _full(g2.shape), _full(w_q.shape), _full(kv.shape), _full(w_o.shape)],
        [_rows(tm, d), _acc((m, 2 * d)), _acc((1, d)), _acc((d, d)), _acc((d, d)), _acc((d, d))],
        (jax.ShapeDtypeStruct((s, d), F32), jax.ShapeDtypeStruct((m, 2 * d), F32),
         jax.ShapeDtypeStruct((1, d), F32), sq, sq, sq),
        [pltpu.VMEM((tm, d), BF16)] + [pltpu.VMEM((d, d), F32)] * 3,
        (x1, dx2, o, ycat, qs, probs, g2, w_q, kv, w_o), ("arbitrary",), hosted)


def _kv_bwd(dkv, mem, memn, g_mem, w_kv):
    m, d = mem.shape
    ns = w_kv.shape[2]

    def body(dkv_ref, mem_ref, memn_ref, g_ref, w_ref, gw_ref, dg_ref):
        _, xh, _ = _rms_fwd(mem_ref[...], g_ref[...])
        dmemn = jnp.zeros((m, d), F32)
        for k in range(N_CHIPS):
            dkb = dkv_ref[:, k * ns:(k + 1) * ns].astype(BF16)
            gw_ref[k] = _dot_tn(memn_ref[...], dkb).astype(BF16)
            dmemn = dmemn + _dot_nt(dkb, w_ref[k])
        dg_ref[...] = jnp.sum(dmemn * xh, axis=0, keepdims=True)

    return pl.pallas_call(
        body, name="kv_bwd",
        out_shape=(jax.ShapeDtypeStruct((N_CHIPS, d, ns), BF16), jax.ShapeDtypeStruct((1, d), F32)),
        compiler_params=pltpu.CompilerParams(vmem_limit_bytes=VMEM_LIMIT),
    )(dkv, mem, memn, g_mem, w_kv)


def _mixer_bwd(x, dx1, h, mixed_all, th_all, g1, lng, lnb, wsp, b_t, cw, ga, gb, w_out, w_in, after):
    s, d = x.shape
    n = s // TM
    nch = TM // CHUNK
    ns = w_in.shape[2]
    nh = N_CHIPS * ns
    aw = d // 2
    hd_w = aw // HEADS

    def rev(cols):
        return pl.BlockSpec((TM, cols), lambda i: (n - 1 - i, 0))

    hprev = pl.BlockSpec((8, nh), lambda i: (jnp.maximum((n - 1 - i) * (TM // 8) - 1, 0), 0))

    def body(x_ref, dx1_ref, h_ref, hp_ref, mix_ref, th_ref, g1_ref, lng_ref, lnb_ref, wsp_ref, bt_ref, cw_ref,
             ga_ref, gb_ref, wout_ref, win_ref, after_ref,
             dx_ref, dh_ref, dg1_ref, dlng_ref, dlnb_ref, dwsp_ref, dbt_ref, dcw_ref, dga_ref, dgb_ref,
             dvn_ref, dcn_ref):
        i = pl.program_id(0)

        @pl.when(i == 0)
        def _():
            for r in (dg1_ref, dlng_ref, dlnb_ref, dwsp_ref, dbt_ref, dcw_ref, dga_ref, dgb_ref, dcn_ref):
                r[...] = jnp.zeros_like(r)

        dx1v = dx1_ref[...]
        dycat = _dot_nt(dx1v.astype(BF16), wout_ref[...])
        ha = h_ref[:, 0:2 * aw]
        th = th_ref[...]
        a = 0.5 * ha * (1.0 + th)
        u = a[:, :aw]
        lngv = lng_ref[...]
        vn, vhat, rs = _layer_norm_parts(a[:, aw:], lngv, lnb_ref[...])
        vnb = vn.astype(BF16)
        wm = _tril_weights(wsp_ref)
        mixed = mix_ref[...]
        gav = ga_ref[...]
        _, yah, ra = _rms_fwd(u * mixed, gav)
        dya, dga = _rms_bwd(dycat[:, :aw], yah, ra, gav)
        dga_ref[...] += dga
        du = dya * mixed
        dmix = dya * u
        dmb = dmix.astype(BF16)
        tri = lax.broadcasted_iota(jnp.int32, (CHUNK, CHUNK), 0) >= lax.broadcasted_iota(jnp.int32, (CHUNK, CHUNK), 1)
        for hd in range(HEADS):
            dw = jnp.zeros((CHUNK, CHUNK), F32)
            db = jnp.zeros((CHUNK, 1), F32)
            for c in range(nch):
                rows = slice(c * CHUNK, (c + 1) * CHUNK)
                cols = slice(hd * hd_w, (hd + 1) * hd_w)
                dvn_ref[rows, cols] = _dot_tn(wm[hd], dmb[rows, cols])
                dw = dw + _dot_nt(dmb[rows, cols], vnb[rows, cols])
                db = db + jnp.sum(dmix[rows, cols], axis=1, keepdims=True)
            dwsp_ref[hd] += jnp.where(tri, dw, 0.0)
            dbt_ref[:, hd:hd + 1] += db
        dvn = dvn_ref[...]
        dlng_ref[...] += jnp.sum(dvn * vhat, axis=0, keepdims=True)
        dlnb_ref[...] += jnp.sum(dvn, axis=0, keepdims=True)
        dvh = dvn * lngv
        dv = rs * (dvh - jnp.mean(dvh, axis=-1, keepdims=True) - vhat * jnp.mean(dvh * vhat, axis=-1, keepdims=True))
        gprime = 0.5 * (1.0 + th) + 0.5 * ha * (1.0 - th * th) * (GELU_C * (1.0 + 3.0 * GELU_K * (ha * ha)))
        dh_ref[:, 0:2 * aw] = (jnp.concatenate([du, dv], axis=-1) * gprime).astype(BF16)
        g_b = h_ref[:, 2 * aw:3 * aw]
        g_c = h_ref[:, 3 * aw:4 * aw]
        val = h_ref[:, 4 * aw:5 * aw]
        z = g_c * val
        zp = jnp.where(i == n - 1, 0.0, hp_ref[:, 3 * aw:4 * aw] * hp_ref[:, 4 * aw:5 * aw])
        zm1, zm2 = _shift_rows(z, zp)
        cw0, cw1, cw2 = cw_ref[0:1, :], cw_ref[1:2, :], cw_ref[2:3, :]
        conv = cw0 * zm2 + cw1 * zm1 + cw2 * z
        gbv = gb_ref[...]
        _, ybh, rb = _rms_fwd(g_b * conv, gbv)
        dyb, dgb = _rms_bwd(dycat[:, aw:], ybh, rb, gbv)
        dgb_ref[...] += dgb
        dconv = dyb * g_b
        dcw_ref[0:1, :] += jnp.sum(dconv * zm2, axis=0, keepdims=True)
        dcw_ref[1:2, :] += jnp.sum(dconv * zm1, axis=0, keepdims=True)
        dcw_ref[2:3, :] += jnp.sum(dconv * z, axis=0, keepdims=True)
        nxt = dcn_ref[...]
        row = lax.broadcasted_iota(jnp.int32, dconv.shape, 0)
        dcp1 = jnp.where(row == TM - 1, nxt[0:1, :], pltpu.roll(dconv, TM - 1, 0))
        dcp2 = jnp.where(row == TM - 1, nxt[1:2, :],
                         jnp.where(row == TM - 2, nxt[0:1, :], pltpu.roll(dconv, TM - 2, 0)))
        dz = cw2 * dconv + cw1 * dcp1 + cw0 * dcp2
        dcn_ref[...] = dconv[0:8, :]
        dh_ref[:, 2 * aw:3 * aw] = (dyb * conv).astype(BF16)
        dh_ref[:, 3 * aw:4 * aw] = (dz * val).astype(BF16)
        dh_ref[:, 4 * aw:5 * aw] = (dz * g_c).astype(BF16)
        dxn = jnp.zeros((TM, d), F32)
        for k in range(N_CHIPS):
            dxn = dxn + _dot_nt(dh_ref[:, k * ns:(k + 1) * ns], win_ref[k])
        g1v = g1_ref[...]
        _, xh1, r1 = _rms_fwd(x_ref[...], g1v)
        dxr, dg1 = _rms_bwd(dxn, xh1, r1, g1v)
        dg1_ref[...] += dg1
        dx_ref[...] = dx1v + dxr

    ins = (x, dx1, h, h, mixed_all, th_all, g1, lng, lnb, wsp, b_t, cw, ga, gb, w_out, w_in)
    acc_shapes = [(1, d), (1, aw), (1, aw), wsp.shape, (CHUNK, CHUNK), cw.shape, (1, aw), (1, aw)]
    return pl.pallas_call(
        body, name="mixer_bwd", grid=(n,),
        in_specs=[rev(d), rev(d), rev(nh), hprev, rev(aw), rev(d)] + [_full(a.shape) for a in ins[6:]] + [ANY],
        out_specs=[rev(d), rev(nh)] + [_acc(sh) for sh in acc_shapes],
        out_shape=(jax.ShapeDtypeStruct((s, d), F32), jax.ShapeDtypeStruct((s, nh), BF16))
        + tuple(jax.ShapeDtypeStruct(sh, F32) for sh in acc_shapes),
        scratch_shapes=[pltpu.VMEM((TM, aw), F32), pltpu.VMEM((8, aw), F32)],
        compiler_params=_params("arbitrary"),
    )(*ins, after)


def _weight_grad(a, b, name, tm, tn, col_sharded, hosted=None):
    t, m = a.shape
    n = b.shape[1]

    def body(a_ref, b_ref, o_ref):
        o_ref[...] = _dot_tn(a_ref[...].astype(BF16), b_ref[...].astype(BF16)).astype(BF16)

    if col_sharded:
        ns = n // N_CHIPS
        per = ns // tn
        out_shape = jax.ShapeDtypeStruct((N_CHIPS, m, ns), BF16)
        out_spec = pl.BlockSpec((None, tm, tn), lambda i, j: (j // per, i, j % per))
    else:
        out_shape = jax.ShapeDtypeStruct((m, n), BF16)
        out_spec = pl.BlockSpec((tm, tn), lambda i, j: (i, j))
    (out,), extra = _host_call(
        body, name, (m // tm, n // tn),
        [pl.BlockSpec((t, tm), lambda i, j: (0, i)), pl.BlockSpec((t, tn), lambda i, j: (0, j))],
        [out_spec], (out_shape,), [], (a, b), ("parallel", "parallel"), hosted)
    return (out if col_sharded else out.reshape(N_CHIPS, m // N_CHIPS, n)), extra


def _row_tile(rows, cap=32):
    best = None
    for t in range(16, min(rows, cap) + 1, 16):
        if rows % t == 0:
            best = t
    return best if best is not None else rows


def _adamw_math(w, g, m, v):
    m2 = ADAM_B1 * m + (1.0 - ADAM_B1) * g
    v2 = ADAM_B2 * v + (1.0 - ADAM_B2) * (g * g)
    m_hat = m2 / (1.0 - ADAM_B1 ** ADAM_STEP)
    v_hat = v2 / (1.0 - ADAM_B2 ** ADAM_STEP)
    delta = -ADAM_LR * (m_hat / (jnp.sqrt(v_hat) + ADAM_EPS) + ADAM_WD * w)
    return delta, m2, v2


def _adamw(w, g, m, v, name):
    r, c = w.shape
    tr = _row_tile(r) if r >= 16 else r

    def body(w_ref, g_ref, m_ref, v_ref, d_ref, m2_ref, v2_ref):
        d_ref[...], m2_ref[...], v2_ref[...] = _adamw_math(w_ref[...], g_ref[...], m_ref[...], v_ref[...])

    sh = jax.ShapeDtypeStruct((r, c), F32)
    return pl.pallas_call(
        body, name=name, grid=(r // tr,),
        in_specs=[_rows(tr, c)] * 4, out_specs=[_rows(tr, c)] * 3, out_shape=(sh, sh, sh),
        compiler_params=_params("parallel"),
    )(w, g, m, v)


def _finalize(items, place, name, hosted=None, after=None):
    r, c = items[0][2].shape
    tr = _row_tile(r)
    nw = len(items)

    n_after = 0 if after is None else 1

    def body(place_ref, *refs):
        ins, outs = refs[:7 * nw], refs[7 * nw + n_after:]
        for k in range(nw):
            own_ref, s0_ref, s1_ref, s2_ref, w_ref, m_ref, v_ref = ins[7 * k:7 * k + 7]
            g_ref, d_ref, m2_ref, v2_ref = outs[4 * k:4 * k + 4]
            g = own_ref[...].astype(F32) + s0_ref[...].astype(F32)
            g = (g + s1_ref[...].astype(F32)) + s2_ref[...].astype(F32)
            g_ref[...] = g
            d_ref[...], m2_ref[...], v2_ref[...] = _adamw_math(w_ref[...], g, m_ref[...], v_ref[...])

    def slot(k):
        return pl.BlockSpec((None, tr, c), lambda i, pref: (k, i, 0))

    rows = pl.BlockSpec((tr, c), lambda i, pref: (i, 0))
    sh = jax.ShapeDtypeStruct((r, c), F32)
    one = [pl.BlockSpec((None, tr, c), lambda i, pref: (pref[1], i, 0)), slot(0), slot(1), slot(2), rows, rows, rows]
    args = [a for part, slots, w, m, v in items for a in (part, slots, slots, slots, w, m, v)]
    args += [] if after is None else [after]
    res, extra = _host_call(body, name, (r // tr,), one * nw + [ANY] * n_after, [rows] * (4 * nw), (sh,) * (4 * nw),
                            [], args, ("parallel",), hosted, prefetch=(place,))
    return [res[4 * k:4 * k + 4] for k in range(nw)], extra


def _small_sum_adamw(parts, w, m, v):
    nd, r, c = parts.shape

    def body(p_ref, w_ref, m_ref, v_ref, g_ref, d_ref, m2_ref, v2_ref):
        g = p_ref[0]
        for k in range(1, nd):
            g = g + p_ref[k]
        g_ref[...] = g
        d_ref[...], m2_ref[...], v2_ref[...] = _adamw_math(w_ref[...], g, m_ref[...], v_ref[...])

    sh = jax.ShapeDtypeStruct((r, c), F32)
    return pl.pallas_call(
        body, name="small_sum_adamw", out_shape=(sh, sh, sh, sh),
        compiler_params=pltpu.CompilerParams(vmem_limit_bytes=VMEM_LIMIT),
    )(parts, w, m, v)


def _place():
    x, y, c = lax.axis_index("x"), lax.axis_index("y"), lax.axis_index("c")
    chips = [(1 - x, y), (x, 1 - y), (1 - x, 1 - y)]
    return x, y, c, 2 * x + y, chips


def _remote(src, dst, send_sem, recv_sem, to):
    return pltpu.make_async_remote_copy(src_ref=src, dst_ref=dst, send_sem=send_sem, recv_sem=recv_sem,
                                        device_id=to, device_id_type=MESH)


class _Exchange:
    def __init__(self, ins, out_shapes, sem_shape, start, finish, middle=None, in_place=False, peers=()):
        self.ins, self.out_shapes, self.sem_shape = tuple(ins), tuple(out_shapes), sem_shape
        self.start, self.finish, self.middle = start, finish, middle
        self.in_place = in_place
        self.peers = frozenset(peers)
        self.collective_id = None

    def with_id(self, collective_id):
        self.collective_id = collective_id
        return self


_FLIPS = {"c": (0, 0, 1), "x": (1, 0, 0), "y": (0, 1, 0), "xy": (1, 1, 0)}


def _handshake(peers):
    x, y, c, _, _ = _place()
    barrier = pltpu.get_barrier_semaphore()
    for name in sorted(peers):
        fx, fy, fc = _FLIPS[name]
        peer = (1 - x if fx else x, 1 - y if fy else y, 1 - c if fc else c)
        pl.semaphore_signal(barrier, inc=1, device_id=peer, device_id_type=MESH)
    pl.semaphore_wait(barrier, len(peers))


def _run_exchange(ex, name, collective_id=None, casts=()):
    n_in, n_out, n_cast = len(ex.ins), len(ex.out_shapes), len(casts)
    step = 64

    def body(*refs):
        ins, srcs = refs[:n_in], refs[n_in:n_in + n_cast]
        outs = refs[n_in + n_cast:n_in + n_cast + n_out]
        dsts = refs[n_in + n_cast + n_out:n_in + 2 * n_cast + n_out]
        rest = refs[n_in + 2 * n_cast + n_out:]
        wide, narrow = rest[:n_cast], rest[n_cast:2 * n_cast]
        send_sems, recv_sems = rest[2 * n_cast], rest[2 * n_cast + 1]
        if collective_id is not None:
            _handshake(ex.peers)
        ex.start(ins, outs, send_sems, recv_sems)
        loads = [pltpu.make_async_copy(srcs[k], wide[k], rest[2 * n_cast + 2].at[k]) for k in range(n_cast)]
        for cp in loads:
            cp.start()
        if ex.middle is not None:
            ex.middle(ins, outs, send_sems, recv_sems)
        stores = []
        for k in range(n_cast):
            loads[k].wait()

            def chunk(i, carry, k=k):
                rows = pl.ds(pl.multiple_of(i * step, step), step)
                narrow[k][rows, :] = wide[k][rows, :].astype(BF16)
                return carry
            lax.fori_loop(0, casts[k].shape[0] // step, chunk, 0)
            stores.append(pltpu.make_async_copy(narrow[k], dsts[k], rest[2 * n_cast + 2].at[k]))
            stores[-1].start()
        ex.finish(ins, outs, send_sems, recv_sems)
        for cp in stores:
            cp.wait()

    assert collective_id is None or ex.peers
    sem = pltpu.SemaphoreType.DMA(ex.sem_shape)
    scratch = ([pltpu.VMEM(a.shape, F32) for a in casts] + [pltpu.VMEM(a.shape, BF16) for a in casts] + [sem, sem]
               + ([pltpu.SemaphoreType.DMA((n_cast,))] if n_cast else []))
    return pl.pallas_call(
        body, name=name,
        out_shape=ex.out_shapes + tuple(jax.ShapeDtypeStruct(a.shape, BF16) for a in casts),
        in_specs=[ANY] * (n_in + n_cast), out_specs=[ANY] * (n_out + n_cast),
        input_output_aliases={k: k for k in range(n_in)} if ex.in_place else {}, scratch_shapes=scratch,
        compiler_params=pltpu.CompilerParams(collective_id=collective_id, vmem_limit_bytes=VMEM_LIMIT),
    )(*ex.ins, *casts)


def _host_call(body, name, grid, in_specs, out_specs, out_shape, scratch_shapes, args, semantics, hosted,
               prefetch=()):
    hosted = [] if hosted is None else (list(hosted) if isinstance(hosted, (list, tuple)) else [hosted])
    collective_id = hosted[0].collective_id if hosted else None
    peers = frozenset().union(*[ex.peers for ex in hosted]) if hosted else frozenset()
    assert collective_id is None or all(ex.peers for ex in hosted)
    n_pre, n_in, n_out, n_scr = len(prefetch), len(in_specs), len(out_specs), len(scratch_shapes)
    h_ins = [a for ex in hosted for a in ex.ins]
    h_outs = [s for ex in hosted for s in ex.out_shapes]
    h_in, h_out = len(h_ins), len(h_outs)

    def wrapped(*refs):
        pre, refs = refs[:n_pre], refs[n_pre:]
        a, hi = refs[:n_in], refs[n_in:n_in + h_in]
        o = refs[n_in + h_in:n_in + h_in + n_out]
        ho = refs[n_in + h_in + n_out:n_in + h_in + n_out + h_out]
        scr = refs[n_in + h_in + n_out + h_out:]

        def run(phase):
            i0 = o0 = 0
            for k, ex in enumerate(hosted):
                fn = getattr(ex, phase)
                if fn is not None:
                    fn(hi[i0:i0 + len(ex.ins)], ho[o0:o0 + len(ex.out_shapes)], scr[n_scr + 2 * k],
                       scr[n_scr + 2 * k + 1])
                i0, o0 = i0 + len(ex.ins), o0 + len(ex.out_shapes)

        if hosted:
            first = functools.reduce(jnp.logical_and, [pl.program_id(k) == 0 for k in range(len(grid))])

            @pl.when(first)
            def _():
                if collective_id is not None:
                    _handshake(peers)
                run("start")

        if any(ex.middle is not None for ex in hosted):
            half_way = functools.reduce(jnp.logical_and, [
                pl.program_id(0) == max(1, grid[0] * MIDDLE_STEP_16THS // 16)] + [
                pl.program_id(k) == 0 for k in range(1, len(grid))])

            @pl.when(half_way)
            def _():
                run("middle")

        body(*pre, *a, *o, *scr[:n_scr])

        if hosted:
            last = functools.reduce(jnp.logical_and, [pl.program_id(k) == grid[k] - 1 for k in range(len(grid))])

            @pl.when(last)
            def _():
                run("finish")

    sems = [pltpu.SemaphoreType.DMA(ex.sem_shape) for ex in hosted for _ in range(2)]
    aliases, i0, o0 = {}, n_pre + n_in, n_out
    for ex in hosted:
        if ex.in_place:
            aliases.update({i0 + k: o0 + k for k in range(len(ex.ins))})
        i0, o0 = i0 + len(ex.ins), o0 + len(ex.out_shapes)
    all_in, all_out = list(in_specs) + [ANY] * h_in, list(out_specs) + [ANY] * h_out
    all_scr = list(scratch_shapes) + sems
    params = _params(*(["arbitrary"] * len(grid) if hosted else semantics), collective_id=collective_id)
    shapes = tuple(out_shape) + tuple(h_outs)
    if n_pre:
        call = pl.pallas_call(
            wrapped, name=name, out_shape=shapes, input_output_aliases=aliases, compiler_params=params,
            grid_spec=pltpu.PrefetchScalarGridSpec(num_scalar_prefetch=n_pre, grid=grid, in_specs=all_in,
                                                   out_specs=all_out, scratch_shapes=all_scr))
    else:
        call = pl.pallas_call(
            wrapped, name=name, grid=grid, in_specs=all_in, out_specs=all_out, out_shape=shapes,
            scratch_shapes=all_scr, input_output_aliases=aliases, compiler_params=params)
    res = call(*prefetch, *args, *h_ins)
    return res[:n_out], res[n_out:]


def _all_gather(shards, small=()):
    items = tuple(shards) + tuple(small)
    nw = len(shards)

    def place():
        x, y, c, me, _ = _place()
        first = (x + (1 - c) * (1 - 2 * x), y + c * (1 - 2 * y))
        second = (x + c * (1 - 2 * x), y + (1 - c) * (1 - 2 * y))
        diag = (1 - x, 1 - y)
        return x, y, c, me, (first, second, diag)

    def halves(w, c):
        rh = items[w].shape[0] // 2
        return pl.ds(c * rh, rh), pl.ds((1 - c) * rh, rh)

    def start(ins, outs, ss, rs):
        x, y, c, me, chips = place()
        for w in range(len(items)):
            _remote(ins[w], outs[w].at[me], ss.at[w, 6], rs.at[w, 6], (x, y, 1 - c)).start()
            if w < nw:
                mine, _ = halves(w, c)
                _remote(ins[w].at[mine], outs[w].at[me, mine], ss.at[w, 0], rs.at[w, 0], (*chips[0], c)).start()
            else:
                for k in range(3):
                    _remote(ins[w], outs[w].at[me], ss.at[w, k], rs.at[w, k], (*chips[k], c)).start()

    def onward(outs, ss, rs, w, k, x, y, c, chips):
        mine, _ = halves(w, c)
        pk = 2 * chips[k][0] + chips[k][1]
        got = outs[w].at[pk, mine]
        src = chips[1] if k == 2 else chips[k]
        _remote(got, got, ss.at[w, k], rs.at[w, k], (*src, c)).wait_recv()
        if k == 0:
            _remote(got, got, ss.at[w, 2], rs.at[w, 2], (*chips[1], c)).start()
        _remote(got, got, ss.at[w, 3 + k], rs.at[w, 3 + k], (x, y, 1 - c)).start()

    def middle(ins, outs, ss, rs):
        x, y, c, me, chips = place()
        for w in range(nw):
            mine, _ = halves(w, c)
            _remote(ins[w].at[mine], outs[w].at[me, mine], ss.at[w, 1], rs.at[w, 1], (*chips[1], c)).start()
        for w in range(nw):
            onward(outs, ss, rs, w, 0, x, y, c, chips)

    def finish(ins, outs, ss, rs):
        x, y, c, me, chips = place()
        sib = (x, y, 1 - c)
        for k in (1, 2):
            for w in range(nw):
                onward(outs, ss, rs, w, k, x, y, c, chips)
        for w in range(len(items)):
            if w < nw:
                mine, theirs = halves(w, c)
                for k, chip in ((3, chips[1]), (4, chips[0]), (5, chips[2])):
                    oth = outs[w].at[2 * chip[0] + chip[1], theirs]
                    _remote(oth, oth, ss.at[w, k], rs.at[w, k], sib).wait_recv()
                own = ins[w].at[mine]
                for k in range(6):
                    _remote(own, own, ss.at[w, k], rs.at[w, k], sib).wait_send()
            else:
                for k in range(3):
                    got = outs[w].at[2 * chips[k][0] + chips[k][1]]
                    _remote(got, got, ss.at[w, k], rs.at[w, k], (*chips[k], c)).wait_recv()
                    _remote(ins[w], ins[w], ss.at[w, k], rs.at[w, k], sib).wait_send()
            _remote(ins[w], outs[w].at[me], ss.at[w, 6], rs.at[w, 6], sib).wait()

    out_shapes = tuple(jax.ShapeDtypeStruct((N_CHIPS,) + a.shape, a.dtype) for a in items)
    return _Exchange(items, out_shapes, (len(items), 7), start, finish, middle if nw else None,
                     peers=("c", "x", "y", "xy") if small else ("c", "x", "y"))


def _chip_reduce(grads, name, collective_id):
    nw = len(grads)
    step = 64

    def body(*refs):
        ins, outs = refs[:nw], refs[nw:2 * nw]
        own, got = refs[2 * nw:3 * nw], refs[3 * nw:4 * nw]
        send_sems, recv_sems, local_sems = refs[4 * nw:]
        x, y, c, _, _ = _place()
        barrier = pltpu.get_barrier_semaphore()
        pl.semaphore_signal(barrier, inc=1, device_id=(x, y, 1 - c), device_id_type=MESH)
        pl.semaphore_wait(barrier, 1)
        moves = []
        for w in range(nw):
            nb, rh = grads[w].shape[0], grads[w].shape[1] // 2
            for k in range(nb):
                away = _remote(ins[w].at[k, pl.ds((1 - c) * rh, rh), :], got[w].at[k], send_sems.at[w, k],
                               recv_sems.at[w, k], (x, y, 1 - c))
                mine = pltpu.make_async_copy(ins[w].at[k, pl.ds(c * rh, rh), :], own[w].at[k], local_sems.at[w, k])
                away.start()
                mine.start()
                moves.append((w, k, away, mine))
        back = []
        for w, k, away, mine in moves:
            rh = own[w].shape[1]
            mine.wait()
            away.wait()

            def add(i, carry, w=w, k=k):
                rows = pl.ds(pl.multiple_of(i * step, step), step)
                own[w][k, rows, :] = (own[w][k, rows, :].astype(F32) + got[w][k, rows, :].astype(F32)).astype(BF16)
                return carry
            lax.fori_loop(0, rh // step, add, 0)
            tail = rh % step
            if tail:
                rows = slice(rh - tail, rh)
                own[w][k, rows, :] = (own[w][k, rows, :].astype(F32) + got[w][k, rows, :].astype(F32)).astype(BF16)
            wb = pltpu.make_async_copy(own[w].at[k], outs[w].at[k, pl.ds(c * rh, rh), :], local_sems.at[w, k])
            wb.start()
            back.append(wb)
        for wb in back:
            wb.wait()

    halves = [pltpu.VMEM((g.shape[0], g.shape[1] // 2, g.shape[2]), BF16) for g in grads]
    sem = pltpu.SemaphoreType.DMA((nw, N_CHIPS))
    return pl.pallas_call(
        body, name=name, out_shape=tuple(jax.ShapeDtypeStruct(g.shape, BF16) for g in grads),
        in_specs=[ANY] * nw, out_specs=[ANY] * nw, scratch_shapes=halves + halves + [sem, sem, sem],
        compiler_params=pltpu.CompilerParams(vmem_limit_bytes=VMEM_LIMIT, collective_id=collective_id),
    )(*grads)


def _scatter_partials(parts):
    nw = len(parts)

    def copies(ins, outs, ss, rs):
        _, _, c, _, chips = _place()
        res = []
        for r, (px, py) in enumerate(chips):
            for w in range(nw):
                rh = parts[w].shape[1] // 2
                rows = pl.ds(c * rh, rh)
                res.append(_remote(ins[w].at[2 * px + py, rows], outs[w].at[r, rows], ss.at[w, r], rs.at[w, r],
                                   (px, py, c)))
        return res

    def start(ins, outs, ss, rs):
        for cp in copies(ins, outs, ss, rs):
            cp.start()

    def finish(ins, outs, ss, rs):
        for cp in copies(ins, outs, ss, rs):
            cp.wait()

    out_shapes = tuple(jax.ShapeDtypeStruct((3,) + p.shape[1:], p.dtype) for p in parts)
    return _Exchange(parts, out_shapes, (nw, 3), start, finish, peers=("x", "y", "xy"))


_HBM = pl.BlockSpec(memory_space=pltpu.HBM)
_SEM = pl.BlockSpec(memory_space=pltpu.SEMAPHORE)
_EFFECT = pltpu.SideEffectType.DATAFLOW_SIDE_EFFECTING


class _SemGrid:
    def __init__(self, refs, shape):
        assert len(refs) == math.prod(shape)
        self.refs, self.shape, self.at = refs, shape, self

    def __getitem__(self, idx):
        flat = 0
        for i, n in zip(idx, self.shape, strict=True):
            flat = flat * n + i
        return self.refs[flat]


def _split_phase(exs, arrays, sems, phase):
    a0 = s0 = 0
    for ex in exs:
        n_in, n_sem = len(ex.ins), math.prod(ex.sem_shape)
        n_arr = n_in if ex.in_place else n_in + len(ex.out_shapes)
        ins = arrays[a0:a0 + n_in]
        outs = ins if ex.in_place else arrays[a0 + n_in:a0 + n_arr]
        getattr(ex, phase)(ins, outs, _SemGrid(sems[s0:s0 + n_sem], ex.sem_shape),
                           _SemGrid(sems[s0 + n_sem:s0 + 2 * n_sem], ex.sem_shape))
        a0, s0 = a0 + n_arr, s0 + 2 * n_sem


def _split_start(exs, name, collective_id):
    assert all(ex.middle is None and ex.peers for ex in exs)
    arrays = [a for ex in exs for a in ex.ins + (() if ex.in_place else tuple(
        lax.empty(s.shape, s.dtype) for s in ex.out_shapes))]
    n_arr, n_sem = len(arrays), 2 * sum(math.prod(ex.sem_shape) for ex in exs)
    peers = frozenset().union(*[ex.peers for ex in exs])

    def body(*refs):
        _handshake(peers)
        _split_phase(exs, refs[:n_arr], refs[n_arr:n_arr + n_sem], "start")
        refs[-1][...] = jnp.zeros_like(refs[-1])

    res = pl.pallas_call(
        body, name=name + "_start",
        out_shape=(pltpu.SemaphoreType.DMA(()),) * n_sem + tuple(pltpu.HBM(a.shape, a.dtype) for a in arrays)
        + (jax.ShapeDtypeStruct((8, 128), F32),),
        in_specs=(_HBM,) * n_arr,
        out_specs=(_SEM,) * n_sem + (_HBM,) * n_arr + (pl.BlockSpec(memory_space=pltpu.VMEM),),
        input_output_aliases={k: n_sem + k for k in range(n_arr)},
        compiler_params=pltpu.CompilerParams(has_side_effects=_EFFECT, collective_id=collective_id),
    )(*[pltpu.with_memory_space_constraint(a, pltpu.HBM) for a in arrays])
    return res[:n_sem], res[n_sem:n_sem + n_arr], res[-1]


def _split_wait(exs, name, sems, thru, after):
    n_arr, n_sem = len(thru), len(sems)

    def body(*refs):
        _split_phase(exs, refs[:n_arr], refs[n_arr:n_arr + n_sem], "finish")

    return pl.pallas_call(
        body, name=name + "_wait", out_shape=tuple(pltpu.HBM(a.shape, a.dtype) for a in thru),
        in_specs=(_HBM,) * n_arr + (_SEM,) * n_sem + (ANY,) * len(after), out_specs=(_HBM,) * n_arr,
        input_output_aliases={k: k for k in range(n_arr)},
        compiler_params=pltpu.CompilerParams(has_side_effects=_EFFECT),
    )(*thru, *sems, *after)


def _join_partials(parts, slots):
    nw = len(parts)

    def copies(outs, ss, rs, mine):
        x, y, c, me, _ = _place()
        res = []
        for w in range(nw):
            rh = parts[w].shape[1] // 2
            rows = pl.ds((c if mine else 1 - c) * rh, rh)
            own = outs[w].at[me, rows]
            got = outs[nw + w].at[:, rows, :]
            res.append(_remote(own, own, ss.at[w, 0], rs.at[w, 0], (x, y, 1 - c)))
            res.append(_remote(got, got, ss.at[w, 1], rs.at[w, 1], (x, y, 1 - c)))
        return res

    def start(ins, outs, ss, rs):
        for cp in copies(outs, ss, rs, True):
            cp.start()

    def finish(ins, outs, ss, rs):
        for cp in copies(outs, ss, rs, True):
            cp.wait_send()
        for cp in copies(outs, ss, rs, False):
            cp.wait_recv()

    arrays = tuple(parts) + tuple(slots)
    return _Exchange(arrays, tuple(jax.ShapeDtypeStruct(a.shape, a.dtype) for a in arrays), (nw, 2), start, finish,
                     in_place=True, peers=("c",))


def _gather_small(slab):
    def copies(ins, outs, ss, rs):
        x, y, c, _, _ = _place()
        me = 4 * x + 2 * y + c
        out, arrivals = [], []
        for k in range(1, 8):
            px = 1 - x if k & 4 else x
            py = 1 - y if k & 2 else y
            pc = 1 - c if k & 1 else c
            out.append(_remote(ins[0], outs[0].at[me], ss.at[k - 1], rs.at[k - 1], (px, py, pc)))
            theirs = outs[0].at[4 * px + 2 * py + pc]
            arrivals.append((theirs, k - 1, (px, py, pc)))
        return pltpu.make_async_copy(ins[0], outs[0].at[me], ss.at[7]), out, arrivals

    def start(ins, outs, ss, rs):
        own, out, _ = copies(ins, outs, ss, rs)
        own.start()
        for cp in out:
            cp.start()

    def finish(ins, outs, ss, rs):
        own, out, arrivals = copies(ins, outs, ss, rs)
        for cp in out:
            cp.wait_send()
        for theirs, k, peer in arrivals:
            _remote(theirs, theirs, ss.at[k], rs.at[k], peer).wait_recv()
        own.wait()

    return _Exchange((slab,), (jax.ShapeDtypeStruct((8,) + slab.shape, slab.dtype),), (8,), start, finish)


_SMALL_VECS = ("ln_mix_g", "ln_attn_g", "ln_mem_g", "ln_ffn_g", "ln_final_g")


def _pack_small(p, extra, conv):
    d = p["ln_mix_g"].shape[-1]
    top = [p[k].reshape(1, d) for k in _SMALL_VECS]
    top.append(jnp.concatenate([p["sgu_ln_g"].reshape(-1), p["sgu_ln_b"].reshape(-1)]).reshape(1, d))
    top.append(jnp.concatenate([p["grp_norm_a"].reshape(-1), p["grp_norm_b"].reshape(-1)]).reshape(1, d))
    top.append(jnp.concatenate([p["b_spatial"].reshape(-1), extra]).reshape(1, d))
    mid = jnp.zeros((8, d), F32)
    if conv is not None:
        mid = jnp.pad(conv, ((0, 5), (0, d - conv.shape[1])))
    return jnp.concatenate([jnp.concatenate(top, axis=0), mid, p["w_spatial"].reshape(-1, d)], axis=0)


def _unpack_small(slab):
    d = slab.shape[1]
    hw = d // 2
    out = {k: slab[i] for i, k in enumerate(_SMALL_VECS)}
    out["sgu_ln_g"], out["sgu_ln_b"] = slab[5, :hw], slab[5, hw:]
    out["grp_norm_a"], out["grp_norm_b"] = slab[6, :hw], slab[6, hw:]
    out["b_spatial"] = slab[7, :hw].reshape(HEADS, CHUNK)
    out["w_spatial"] = slab[16:].reshape(HEADS, CHUNK, CHUNK)
    return out


_BIG = ("w_in", "w_kv", "w_gate_up", "w_out", "w_q", "w_o", "w_down")
_WEIGHTS = ("ln_mix_g", "w_in", "sgu_ln_g", "sgu_ln_b", "w_spatial", "b_spatial", "conv_w", "grp_norm_a",
            "grp_norm_b", "w_out", "ln_attn_g", "ln_mem_g", "w_q", "w_kv", "w_o", "ln_ffn_g", "w_gate_up",
            "w_down", "ln_final_g")


def _step(p, m_, v_, x, mem, target):
    s, d = x.shape
    hw = d // 2
    row = lambda a: a.reshape(1, -1)
    x_, y_, c_ = lax.axis_index("x"), lax.axis_index("y"), lax.axis_index("c")
    chip = 2 * x_ + y_

    conv8 = jnp.pad(p["conv_w"], ((0, 5), (0, 0)))
    later = ("w_kv", "w_q", "w_o", "w_down", "w_gate_up")
    first = _run_exchange(_all_gather([p["w_in"].astype(BF16), p["w_out"].astype(BF16)], [conv8]),
                          "all_gather_mixer", collective_id=4, casts=[p[k] for k in later])
    (w_in, w_out4, conv4), bf = first[:3], dict(zip(later, first[3:]))
    cw = jnp.transpose(conv4[:, :3, :], (1, 0, 2)).reshape(3, hw)
    b_t = jnp.pad(jnp.transpose(p["b_spatial"]), ((0, 0), (0, CHUNK - HEADS)))
    g1, g2, gm, g3, gf = (row(p[k]) for k in _SMALL_VECS)
    lng, lnb, ga, gb = row(p["sgu_ln_g"]), row(p["sgu_ln_b"]), row(p["grp_norm_a"]), row(p["grp_norm_b"])
    wsp = p["w_spatial"]
    w_out = w_out4.reshape(-1, d)

    (h, x1, ycat, xn1, mixed, th), (w_kv, w_q4, w_o4, w_down4) = _mixer_fwd(
        x, g1, w_in, lng, lnb, wsp, b_t, cw, ga, gb, w_out,
        hosted=_all_gather([bf[k] for k in ("w_kv", "w_q", "w_o", "w_down")]).with_id(5))
    w_q, w_o, w_down = (a.reshape(-1, d) for a in (w_q4, w_o4, w_down4))
    (x2, o, qs, probs, memn, kv), (w_gu,) = _attn_fwd(x1, mem, g2, gm, w_q, w_kv, w_o,
                                                      hosted=_all_gather([bf["w_gate_up"]]).with_id(6))
    dx2, act, dgu, xn3, dx3, loss, dgf, dg3 = _ffn_fwd_bwd(x2, g3, gf, target, w_gu, w_down)

    place = jnp.stack([c_, chip]).astype(jnp.int32)

    def chip_partials(names, grads, tag):
        return list(_chip_reduce(grads, "chip_reduce_" + tag, ("down", "ffn", "attn", "mixer").index(tag)))

    names_d = ("w_down",)
    parts_d = chip_partials(names_d, (_weight_grad(act, dx3, "grad_w_down", 1408, 512, False)[0],), "down")
    g_gu, slots_d = _weight_grad(xn3, dgu, "grad_w_gate_up", 512, 1408, True,
                                 hosted=_scatter_partials(parts_d).with_id(7))
    names_a = ("w_gate_up",)
    parts_a = chip_partials(names_a, (g_gu,), "ffn")
    (dx1, dkv, dg2, g_o, g_q, g_out), slots_a = _attn_bwd(x1, dx2, o, ycat, qs, probs, g2, w_q, kv, w_o,
                                                           hosted=_scatter_partials(parts_a).with_id(8))
    g_kv, dgm = _kv_bwd(dkv, mem, memn, gm, w_kv)
    names_b = ("w_o", "w_out", "w_q", "w_kv")
    shard_major = lambda g: g.reshape(N_CHIPS, -1, d)
    parts_b = chip_partials(names_b, (shard_major(g_o), shard_major(g_out), shard_major(g_q), g_kv), "attn")
    names_da = names_d + names_a
    exs_b = [_scatter_partials(parts_b), _join_partials(parts_d + parts_a, slots_d + slots_a)]
    sems_b, thru_b, token_b = _split_start(exs_b, "rs_scatter_attn", 9)
    dx, dh, dg1, dlng, dlnb, dwsp, dbt, dcw, dga, dgb = _mixer_bwd(
        x, dx1, h, mixed, th, g1, lng, lnb, wsp, b_t, cw, ga, gb, w_out, w_in, after=token_b)
    thru_b = _split_wait(exs_b, "rs_scatter_attn", sems_b, thru_b, [dh])
    nb = len(names_b)
    parts_b, slots_b, joined = thru_b[:nb], thru_b[nb:2 * nb], thru_b[2 * nb:]
    whole = dict(zip(names_da, zip(joined[:len(names_da)], joined[len(names_da):])))
    small = {"ln_mix_g": dg1, "ln_attn_g": dg2, "ln_mem_g": dgm, "ln_ffn_g": dg3, "ln_final_g": dgf,
             "sgu_ln_g": dlng, "sgu_ln_b": dlnb, "grp_norm_a": dga, "grp_norm_b": dgb,
             "b_spatial": jnp.transpose(dbt[:, :HEADS]), "w_spatial": dwsp}
    loss_vec = jnp.pad(loss.reshape(1), (0, hw - 1))
    g_in, extra = _weight_grad(
        xn1, dh, "grad_w_in", 1024, 640, True,
        hosted=[_gather_small(_pack_small(small, loss_vec, dcw)), _join_partials(parts_b, slots_b)])
    parts = extra[0]
    whole.update(zip(names_b, zip(extra[1:1 + len(names_b)], extra[1 + len(names_b):])))
    (part_in,) = chip_partials(("w_in",), (g_in,), "mixer")
    out_g, out_d, out_m, out_v = {}, {}, {}, {}

    def finalize(ks, tag, after=None):
        done, _ = _finalize([(whole[k][0], whole[k][1], p[k], m_[k], v_[k]) for k in ks], place,
                            "finalize_" + tag, after=after)
        for k, (g, dl, nm, nv) in zip(ks, done):
            out_g[k], out_d[k], out_m[k], out_v[k] = g, dl, nm, nv

    exs_in = [_scatter_partials([part_in])]
    sems, thru_in, token = _split_start(exs_in, "rs_scatter_in", 10)
    finalize(("w_down",), "w_down", after=token)
    finalize(("w_o", "w_out", "w_q"), "attn", after=token)
    finalize(("w_gate_up",), "w_gate_up", after=token)
    finalize(("w_kv",), "w_kv", after=token)
    part_in, slots_in = _split_wait(exs_in, "rs_scatter_in", sems, thru_in,
                                    [out_v[k] for k in ("w_down", "w_o", "w_gate_up", "w_kv")])
    whole["w_in"] = _run_exchange(_join_partials([part_in], [slots_in]), "rs_join_mixer", collective_id=11)
    finalize(("w_in",), "w_in")

    zeros = jnp.zeros((hw,), F32)
    sg, sd, sm, sv = _small_sum_adamw(parts, _pack_small(p, zeros, None), _pack_small(m_, zeros, None),
                                      _pack_small(v_, zeros, None))
    for tree, slab in zip((out_g, out_d, out_m, out_v), (sg, sd, sm, sv)):
        tree.update(_unpack_small(slab))
    loss_out = sg[7, hw]
    g_conv = lax.dynamic_slice(sg[8:11, :hw], (0, chip * (hw // N_CHIPS)), (3, hw // N_CHIPS))
    out_g["conv_w"] = g_conv
    out_d["conv_w"], out_m["conv_w"], out_v["conv_w"] = _adamw(p["conv_w"], g_conv, m_["conv_w"], v_["conv_w"],
                                                                "adamw_conv_w")
    return loss_out, dx, out_g, out_d, out_m, out_v


def kernel(x, mem, ln_mix_g, w_in, sgu_ln_g, sgu_ln_b, w_spatial, b_spatial, conv_w, grp_norm_a, grp_norm_b, w_out, ln_attn_g, ln_mem_g, w_q, w_kv, w_o, ln_ffn_g, w_gate_up, w_down, ln_final_g, loss_target, m_ln_mix_g, m_w_in, m_sgu_ln_g, m_sgu_ln_b, m_w_spatial, m_b_spatial, m_conv_w, m_grp_norm_a, m_grp_norm_b, m_w_out, m_ln_attn_g, m_ln_mem_g, m_w_q, m_w_kv, m_w_o, m_ln_ffn_g, m_w_gate_up, m_w_down, m_ln_final_g, v_ln_mix_g, v_w_in, v_sgu_ln_g, v_sgu_ln_b, v_w_spatial, v_b_spatial, v_conv_w, v_grp_norm_a, v_grp_norm_b, v_w_out, v_ln_attn_g, v_ln_mem_g, v_w_q, v_w_kv, v_w_o, v_ln_ffn_g, v_w_gate_up, v_w_down, v_ln_final_g):
    p = dict(ln_mix_g=ln_mix_g, w_in=w_in, sgu_ln_g=sgu_ln_g, sgu_ln_b=sgu_ln_b, w_spatial=w_spatial,
             b_spatial=b_spatial, conv_w=conv_w, grp_norm_a=grp_norm_a, grp_norm_b=grp_norm_b, w_out=w_out,
             ln_attn_g=ln_attn_g, ln_mem_g=ln_mem_g, w_q=w_q, w_kv=w_kv, w_o=w_o, ln_ffn_g=ln_ffn_g,
             w_gate_up=w_gate_up, w_down=w_down, ln_final_g=ln_final_g)
    m_ = dict(ln_mix_g=m_ln_mix_g, w_in=m_w_in, sgu_ln_g=m_sgu_ln_g, sgu_ln_b=m_sgu_ln_b, w_spatial=m_w_spatial,
              b_spatial=m_b_spatial, conv_w=m_conv_w, grp_norm_a=m_grp_norm_a, grp_norm_b=m_grp_norm_b,
              w_out=m_w_out, ln_attn_g=m_ln_attn_g, ln_mem_g=m_ln_mem_g, w_q=m_w_q, w_kv=m_w_kv, w_o=m_w_o,
              ln_ffn_g=m_ln_ffn_g, w_gate_up=m_w_gate_up, w_down=m_w_down, ln_final_g=m_ln_final_g)
    v_ = dict(ln_mix_g=v_ln_mix_g, w_in=v_w_in, sgu_ln_g=v_sgu_ln_g, sgu_ln_b=v_sgu_ln_b, w_spatial=v_w_spatial,
              b_spatial=v_b_spatial, conv_w=v_conv_w, grp_norm_a=v_grp_norm_a, grp_norm_b=v_grp_norm_b,
              w_out=v_w_out, ln_attn_g=v_ln_attn_g, ln_mem_g=v_ln_mem_g, w_q=v_w_q, w_kv=v_w_kv, w_o=v_w_o,
              ln_ffn_g=v_ln_ffn_g, w_gate_up=v_w_gate_up, w_down=v_w_down, ln_final_g=v_ln_final_g)
    s, d = x.shape[-2], x.shape[-1]
    loss, dx, g, dl, nm, nv = _step(p, m_, v_, x.reshape(s, d), mem.reshape(-1, d), loss_target.reshape(s, d))
    outs = [loss, dx.reshape(x.shape)]
    for tree in (g, dl, nm, nv):
        outs += [tree[k].reshape(p[k].shape) for k in _WEIGHTS]
    return tuple(outs)
```

```python
import functools
import math

import jax
import jax.numpy as jnp
from jax import lax
from jax.experimental import pallas as pl
from jax.experimental.pallas import tpu as pltpu

F32 = jnp.float32
BF16 = jnp.bfloat16
EPS = 1e-6
CHUNK = 128
HEADS = 4
N_CHIPS = 4
TM = 512
TM_ATTN = 512
TM_FFN = 256
ADAM_LR, ADAM_B1, ADAM_B2, ADAM_EPS, ADAM_WD, ADAM_STEP = 0.001, 0.9, 0.999, 1e-08, 0.01, 10
GELU_C = math.sqrt(2.0 / math.pi)
GELU_K = 0.044715
SMALL_ROWS = 80
VMEM_LIMIT = 56 * 1024 * 1024
MIDDLE_STEP_16THS = 5
MESH = pl.DeviceIdType.MESH
ANY = pl.BlockSpec(memory_space=pl.ANY)


def _params(*sem, collective_id=None):
    return pltpu.CompilerParams(dimension_semantics=sem, vmem_limit_bytes=VMEM_LIMIT, collective_id=collective_id)


def _dot(a, b):
    return jnp.dot(a, b, preferred_element_type=F32)


def _dot_nt(a, b):
    return lax.dot_general(a, b, (((1,), (1,)), ((), ())), preferred_element_type=F32)


def _dot_tn(a, b):
    return lax.dot_general(a, b, (((0,), (0,)), ((), ())), preferred_element_type=F32)


def _rms_fwd(x, g):
    r = lax.rsqrt(jnp.mean(x * x, axis=-1, keepdims=True) + EPS)
    xh = x * r
    return xh * g, xh, r


def _rms_bwd(dy, xh, r, g):
    dxh = dy * g
    dx = r * (dxh - xh * jnp.mean(dxh * xh, axis=-1, keepdims=True))
    return dx, jnp.sum(dy * xh, axis=0, keepdims=True)


def _full(shape):
    nd = len(shape)
    return pl.BlockSpec(shape, lambda *_: (0,) * nd, pipeline_mode=pl.Buffered(1))


def _acc(shape):
    nd = len(shape)
    return pl.BlockSpec(shape, lambda *_: (0,) * nd)


def _rows(tm, cols):
    return pl.BlockSpec((tm, cols), lambda i: (i, 0))


def _tril_weights(wsp_ref):
    row = lax.broadcasted_iota(jnp.int32, (CHUNK, CHUNK), 0)
    col = lax.broadcasted_iota(jnp.int32, (CHUNK, CHUNK), 1)
    return [jnp.where(row >= col, wsp_ref[hd], 0.0).astype(BF16) for hd in range(HEADS)]


def _shift_rows(z, zp):
    row = lax.broadcasted_iota(jnp.int32, z.shape, 0)
    zm1 = jnp.where(row == 0, zp[7:8, :], pltpu.roll(z, 1, 0))
    zm2 = jnp.where(row == 0, zp[6:7, :], jnp.where(row == 1, zp[7:8, :], pltpu.roll(z, 2, 0)))
    return zm1, zm2


def _gelu_parts(x):
    t = jnp.tanh(GELU_C * (x + GELU_K * (x * x * x)))
    return 0.5 * x * (1.0 + t), t


def _layer_norm_parts(v, g, b):
    mu = jnp.mean(v, axis=-1, keepdims=True)
    vc = v - mu
    rs = lax.rsqrt(jnp.mean(vc * vc, axis=-1, keepdims=True) + EPS)
    vhat = vc * rs
    return vhat * g + b, vhat, rs


def _mixer_fwd(x, g1, w_in, lng, lnb, wsp, b_t, cw, ga, gb, w_out, hosted=None):
    s, d = x.shape
    n = s // TM
    nch = TM // CHUNK
    ns = w_in.shape[2]
    nh = N_CHIPS * ns
    aw = d // 2
    hd_w = aw // HEADS

    def body(x_ref, g1_ref, win_ref, lng_ref, lnb_ref, wsp_ref, bt_ref, cw_ref, ga_ref, gb_ref, wout_ref,
             h_ref, x1_ref, y_ref, xn_ref, mix_ref, th_ref, zp_ref):
        i = pl.program_id(0)

        @pl.when(i == 0)
        def _():
            zp_ref[...] = jnp.zeros_like(zp_ref)

        x = x_ref[...]
        xn, _, _ = _rms_fwd(x, g1_ref[...])
        xnb = xn.astype(BF16)
        xn_ref[...] = xnb
        for k in range(N_CHIPS):
            h_ref[:, k * ns:(k + 1) * ns] = _dot(xnb, win_ref[k])
        a, th = _gelu_parts(h_ref[:, 0:2 * aw])
        th_ref[...] = th
        u = a[:, :aw]
        vn, _, _ = _layer_norm_parts(a[:, aw:], lng_ref[...], lnb_ref[...])
        vnb = vn.astype(BF16)
        wm = _tril_weights(wsp_ref)
        for c in range(nch):
            for hd in range(HEADS):
                blk = vnb[c * CHUNK:(c + 1) * CHUNK, hd * hd_w:(hd + 1) * hd_w]
                mix_ref[c * CHUNK:(c + 1) * CHUNK, hd * hd_w:(hd + 1) * hd_w] = _dot(wm[hd], blk) + bt_ref[:, hd:hd + 1]
        ya, _, _ = _rms_fwd(u * mix_ref[...], ga_ref[...])
        g_b = h_ref[:, 2 * aw:3 * aw]
        z = h_ref[:, 3 * aw:4 * aw] * h_ref[:, 4 * aw:5 * aw]
        zm1, zm2 = _shift_rows(z, zp_ref[...])
        conv = cw_ref[0:1, :] * zm2 + cw_ref[1:2, :] * zm1 + cw_ref[2:3, :] * z
        yb, _, _ = _rms_fwd(g_b * conv, gb_ref[...])
        zp_ref[...] = z[TM - 8:TM, :]
        ycat = jnp.concatenate([ya, yb], axis=-1).astype(BF16)
        y_ref[...] = ycat
        x1_ref[...] = x + _dot(ycat, wout_ref[...])

    return _host_call(
        body, "mixer_fwd", (n,),
        [_rows(TM, d), _full(g1.shape), _full(w_in.shape), _full(lng.shape), _full(lnb.shape),
         _full(wsp.shape), _full(b_t.shape), _full(cw.shape), _full(ga.shape), _full(gb.shape),
         _full(w_out.shape)],
        [_rows(TM, nh), _rows(TM, d), _rows(TM, d), _rows(TM, d), _rows(TM, aw), _rows(TM, d)],
        (jax.ShapeDtypeStruct((s, nh), F32), jax.ShapeDtypeStruct((s, d), F32),
         jax.ShapeDtypeStruct((s, d), BF16), jax.ShapeDtypeStruct((s, d), BF16),
         jax.ShapeDtypeStruct((s, aw), F32), jax.ShapeDtypeStruct((s, d), F32)),
        [pltpu.VMEM((8, aw), F32)],
        (x, g1, w_in, lng, lnb, wsp, b_t, cw, ga, gb, w_out), ("arbitrary",), hosted)


def _attn_fwd(x1, mem, g2, g_mem, w_q, w_kv, w_o, hosted=None):
    s, d = x1.shape
    tm = min(TM_ATTN, s)
    n = s // tm
    dh = d // HEADS
    m = mem.shape[0]
    ns = w_kv.shape[2]
    scale = dh ** -0.5

    def body(x1_ref, mem_ref, g2_ref, gm_ref, wq_ref, wkv_ref, wo_ref, x2_ref, o_ref, q_ref, p_ref, memn_ref, kv_ref):
        @pl.when(pl.program_id(0) == 0)
        def _():
            y, _, _ = _rms_fwd(mem_ref[...], gm_ref[...])
            yb = y.astype(BF16)
            memn_ref[...] = yb
            for k in range(N_CHIPS):
                kv_ref[:, k * ns:(k + 1) * ns] = _dot(yb, wkv_ref[k]).astype(BF16)

        x1v = x1_ref[...]
        xn, _, _ = _rms_fwd(x1v, g2_ref[...])
        q_ref[...] = _dot(xn.astype(BF16), wq_ref[...]).astype(BF16)
        for hd in range(HEADS):
            kh = kv_ref[:, hd * dh:(hd + 1) * dh]
            vh = kv_ref[:, d + hd * dh:d + (hd + 1) * dh]
            sc = _dot_nt(q_ref[:, hd * dh:(hd + 1) * dh], kh) * scale
            e = jnp.exp(sc - jnp.max(sc, axis=-1, keepdims=True))
            p = e / jnp.sum(e, axis=-1, keepdims=True)
            p_ref[:, hd * m:(hd + 1) * m] = p
            o_ref[:, hd * dh:(hd + 1) * dh] = _dot(p.astype(BF16), vh).astype(BF16)
        x2_ref[...] = x1v + _dot(o_ref[...], wo_ref[...])

    return _host_call(
        body, "attn_fwd", (n,),
        [_rows(tm, d), _full(mem.shape), _full(g2.shape), _full(g_mem.shape), _full(w_q.shape), _full(w_kv.shape),
         _full(w_o.shape)],
        [_rows(tm, d), _rows(tm, d), _rows(tm, d), _rows(tm, HEADS * m), _acc((m, d)), _acc((m, 2 * d))],
        (jax.ShapeDtypeStruct((s, d), F32), jax.ShapeDtypeStruct((s, d), BF16), jax.ShapeDtypeStruct((s, d), BF16),
         jax.ShapeDtypeStruct((s, HEADS * m), F32), jax.ShapeDtypeStruct((m, d), BF16),
         jax.ShapeDtypeStruct((m, 2 * d), BF16)),
        [], (x1, mem, g2, g_mem, w_q, w_kv, w_o), ("arbitrary",), hosted)


def _ffn_fwd_bwd(x2, g3, gf, target, w_gu, w_down):
    s, d = x2.shape
    tm = min(TM_FFN, s)
    n = s // tm
    ns = w_gu.shape[2]
    ff = 2 * ns

    def body(x2_ref, g3_ref, gf_ref, t_ref, wgu_ref, wd_ref,
             dx2_ref, act_ref, dgu_ref, xn_ref, dx3_ref, loss_ref, dgf_ref, dg3_ref):
        i = pl.program_id(0)

        @pl.when(i == 0)
        def _():
            loss_ref[...] = jnp.zeros_like(loss_ref)
            dgf_ref[...] = jnp.zeros_like(dgf_ref)
            dg3_ref[...] = jnp.zeros_like(dg3_ref)

        x2v = x2_ref[...]
        xn, xh3, r3 = _rms_fwd(x2v, g3_ref[...])
        xnb = xn.astype(BF16)
        xn_ref[...] = xnb
        x3 = x2v
        saved = []
        for j in range(2):
            g = _dot(xnb, wgu_ref[j])
            u = _dot(xnb, wgu_ref[2 + j])
            sg = 1.0 / (1.0 + jnp.exp(-g))
            sl = g * sg
            actb = (sl * u).astype(BF16)
            act_ref[:, j * ns:(j + 1) * ns] = actb
            x3 = x3 + _dot(actb, wd_ref[j * ns:(j + 1) * ns, :])
            saved.append((u, sl, sg * (1.0 + g * (1.0 - sg))))
        gfv = gf_ref[...]
        y, xhf, rf = _rms_fwd(x3, gfv)
        e = y - t_ref[...]
        loss_ref[...] += 0.5 * jnp.sum(jnp.sum(e * e, axis=-1, keepdims=True), axis=0, keepdims=True) / d
        dx3, dgf = _rms_bwd(e / d, xhf, rf, gfv)
        dgf_ref[...] += dgf
        dx3b = dx3.astype(BF16)
        dx3_ref[...] = dx3b
        dxn = jnp.zeros_like(x2v)
        for j in range(2):
            u, sl, dsl = saved[j]
            dact = _dot_nt(dx3b, wd_ref[j * ns:(j + 1) * ns, :])
            dgb = (dact * u * dsl).astype(BF16)
            dub = (dact * sl).astype(BF16)
            dgu_ref[:, j * ns:(j + 1) * ns] = dgb
            dgu_ref[:, ff + j * ns:ff + (j + 1) * ns] = dub
            dxn = dxn + _dot_nt(dgb, wgu_ref[j]) + _dot_nt(dub, wgu_ref[2 + j])
        dxr, dg3 = _rms_bwd(dxn, xh3, r3, g3_ref[...])
        dg3_ref[...] += dg3
        dx2_ref[...] = dx3 + dxr

    vec = jax.ShapeDtypeStruct((1, d), F32)
    return pl.pallas_call(
        body, name="ffn_fwd_bwd", grid=(n,),
        in_specs=[_rows(tm, d), _full(g3.shape), _full(gf.shape), _rows(tm, d), _full(w_gu.shape),
                  _full(w_down.shape)],
        out_specs=[_rows(tm, d), _rows(tm, ff), _rows(tm, 2 * ff), _rows(tm, d), _rows(tm, d),
                   _acc((1, 1)), _acc((1, d)), _acc((1, d))],
        out_shape=(jax.ShapeDtypeStruct((s, d), F32), jax.ShapeDtypeStruct((s, ff), BF16),
                   jax.ShapeDtypeStruct((s, 2 * ff), BF16), jax.ShapeDtypeStruct((s, d), BF16),
                   jax.ShapeDtypeStruct((s, d), BF16), jax.ShapeDtypeStruct((1, 1), F32), vec, vec),
        compiler_params=_params("arbitrary"),
    )(x2, g3, gf, target, w_gu, w_down)


def _attn_bwd(x1, dx2, o, ycat, qs, probs, g2, w_q, kv, w_o, hosted=None):
    s, d = x1.shape
    tm = min(TM_ATTN, s)
    n = s // tm
    dh = d // HEADS
    scale = dh ** -0.5
    m = kv.shape[0]

    def body(x1_ref, dx2_ref, o_ref, y_ref, q_ref, p_ref, g2_ref, wq_ref, kv_ref, wo_ref,
             dx1_ref, dkv_ref, dg2_ref, gwo_out, gwq_out, gwout_out, dq_ref, gwo_ref, gwq_ref, gwout_ref):
        i = pl.program_id(0)

        @pl.when(i == 0)
        def _():
            for r in (dkv_ref, dg2_ref, gwo_ref, gwq_ref, gwout_ref):
                r[...] = jnp.zeros_like(r)

        xn, xh2, r2 = _rms_fwd(x1_ref[...], g2_ref[...])
        xnb = xn.astype(BF16)
        dx2v = dx2_ref[...]
        dx2b = dx2v.astype(BF16)
        gwo_ref[...] += _dot_tn(o_ref[...], dx2b)
        do = _dot_nt(dx2b, wo_ref[...])
        for hd in range(HEADS):
            qb = q_ref[:, hd * dh:(hd + 1) * dh]
            p = p_ref[:, hd * m:(hd + 1) * m]
            kh = kv_ref[:, hd * dh:(hd + 1) * dh]
            vh = kv_ref[:, d + hd * dh:d + (hd + 1) * dh]
            dob = do[:, hd * dh:(hd + 1) * dh].astype(BF16)
            dp = _dot_nt(dob, vh)
            ds = p * (dp - jnp.sum(dp * p, axis=-1, keepdims=True))
            dsb = (ds * scale).astype(BF16)
            dq_ref[:, hd * dh:(hd + 1) * dh] = _dot(dsb, kh).astype(BF16)
            dkv_ref[:, hd * dh:(hd + 1) * dh] += _dot_tn(dsb, qb)
            dkv_ref[:, d + hd * dh:d + (hd + 1) * dh] += _dot_tn(p.astype(BF16), dob)
        dqb = dq_ref[...]
        gwq_ref[...] += _dot_tn(xnb, dqb)
        dxn = _dot_nt(dqb, wq_ref[...])
        dxr, dg2 = _rms_bwd(dxn, xh2, r2, g2_ref[...])
        dg2_ref[...] += dg2
        dx1 = dx2v + dxr
        dx1_ref[...] = dx1
        gwout_ref[...] += _dot_tn(y_ref[...], dx1.astype(BF16))

        @pl.when(i == n - 1)
        def _():
            for acc, out in ((gwo_ref, gwo_out), (gwq_ref, gwq_out), (gwout_ref, gwout_out)):
                out[...] = acc[...].astype(BF16)

    sq = jax.ShapeDtypeStruct((d, d), BF16)
    return _host_call(
        body, "attn_bwd", (n,),
        [_rows(tm, d), _rows(tm, d), _rows(tm, d), _rows(tm, d), _rows(tm, d), _rows(tm, HEADS * m),
         _full(g2.shape), _full(w_q.shape), _full(kv.shape), _full(w_o.shape)],
        [_rows(tm, d), _acc((m, 2 * d)), _acc((1, d)), _acc((d, d)), _acc((d, d)), _acc((d, d))],
        (jax.ShapeDtypeStruct((s, d), F32), jax.ShapeDtypeStruct((m, 2 * d), F32),
         jax.ShapeDtypeStruct((1, d), F32), sq, sq, sq),
        [pltpu.VMEM((tm, d), BF16)] + [pltpu.VMEM((d, d), F32)] * 3,
        (x1, dx2, o, ycat, qs, probs, g2, w_q, kv, w_o), ("arbitrary",), hosted)


def _kv_bwd(dkv, mem, memn, g_mem, w_kv):
    m, d = mem.shape
    ns = w_kv.shape[2]

    def body(dkv_ref, mem_ref, memn_ref, g_ref, w_ref, gw_ref, dg_ref):
        _, xh, _ = _rms_fwd(mem_ref[...], g_ref[...])
        dmemn = jnp.zeros((m, d), F32)
        for k in range(N_CHIPS):
            dkb = dkv_ref[:, k * ns:(k + 1) * ns].astype(BF16)
            gw_ref[k] = _dot_tn(memn_ref[...], dkb).astype(BF16)
            dmemn = dmemn + _dot_nt(dkb, w_ref[k])
        dg_ref[...] = jnp.sum(dmemn * xh, axis=0, keepdims=True)

    return pl.pallas_call(
        body, name="kv_bwd",
        out_shape=(jax.ShapeDtypeStruct((N_CHIPS, d, ns), BF16), jax.ShapeDtypeStruct((1, d), F32)),
        compiler_params=pltpu.CompilerParams(vmem_limit_bytes=VMEM_LIMIT),
    )(dkv, mem, memn, g_mem, w_kv)


def _mixer_bwd(x, dx1, h, mixed_all, th_all, g1, lng, lnb, wsp, b_t, cw, ga, gb, w_out, w_in, after):
    s, d = x.shape
    n = s // TM
    nch = TM // CHUNK
    ns = w_in.shape[2]
    nh = N_CHIPS * ns
    aw = d // 2
    hd_w = aw // HEADS

    def rev(cols):
        return pl.BlockSpec((TM, cols), lambda i: (n - 1 - i, 0))

    hprev = pl.BlockSpec((8, nh), lambda i: (jnp.maximum((n - 1 - i) * (TM // 8) - 1, 0), 0))

    def body(x_ref, dx1_ref, h_ref, hp_ref, mix_ref, th_ref, g1_ref, lng_ref, lnb_ref, wsp_ref, bt_ref, cw_ref,
             ga_ref, gb_ref, wout_ref, win_ref, after_ref,
             dx_ref, dh_ref, dg1_ref, dlng_ref, dlnb_ref, dwsp_ref, dbt_ref, dcw_ref, dga_ref, dgb_ref,
             dvn_ref, dcn_ref):
        i = pl.program_id(0)

        @pl.when(i == 0)
        def _():
            for r in (dg1_ref, dlng_ref, dlnb_ref, dwsp_ref, dbt_ref, dcw_ref, dga_ref, dgb_ref, dcn_ref):
                r[...] = jnp.zeros_like(r)

        dx1v = dx1_ref[...]
        dycat = _dot_nt(dx1v.astype(BF16), wout_ref[...])
        ha = h_ref[:, 0:2 * aw]
        th = th_ref[...]
        a = 0.5 * ha * (1.0 + th)
        u = a[:, :aw]
        lngv = lng_ref[...]
        vn, vhat, rs = _layer_norm_parts(a[:, aw:], lngv, lnb_ref[...])
        vnb = vn.astype(BF16)
        wm = _tril_weights(wsp_ref)
        mixed = mix_ref[...]
        gav = ga_ref[...]
        _, yah, ra = _rms_fwd(u * mixed, gav)
        dya, dga = _rms_bwd(dycat[:, :aw], yah, ra, gav)
        dga_ref[...] += dga
        du = dya * mixed
        dmix = dya * u
        dmb = dmix.astype(BF16)
        tri = lax.broadcasted_iota(jnp.int32, (CHUNK, CHUNK), 0) >= lax.broadcasted_iota(jnp.int32, (CHUNK, CHUNK), 1)
        for hd in range(HEADS):
            dw = jnp.zeros((CHUNK, CHUNK), F32)
            db = jnp.zeros((CHUNK, 1), F32)
            for c in range(nch):
                rows = slice(c * CHUNK, (c + 1) * CHUNK)
                cols = slice(hd * hd_w, (hd + 1) * hd_w)
                dvn_ref[rows, cols] = _dot_tn(wm[hd], dmb[rows, cols])
                dw = dw + _dot_nt(dmb[rows, cols], vnb[rows, cols])
                db = db + jnp.sum(dmix[rows, cols], axis=1, keepdims=True)
            dwsp_ref[hd] += jnp.where(tri, dw, 0.0)
            dbt_ref[:, hd:hd + 1] += db
        dvn = dvn_ref[...]
        dlng_ref[...] += jnp.sum(dvn * vhat, axis=0, keepdims=True)
        dlnb_ref[...] += jnp.sum(dvn, axis=0, keepdims=True)
        dvh = dvn * lngv
        dv = rs * (dvh - jnp.mean(dvh, axis=-1, keepdims=True) - vhat * jnp.mean(dvh * vhat, axis=-1, keepdims=True))
        gprime = 0.5 * (1.0 + th) + 0.5 * ha * (1.0 - th * th) * (GELU_C * (1.0 + 3.0 * GELU_K * (ha * ha)))
        dh_ref[:, 0:2 * aw] = (jnp.concatenate([du, dv], axis=-1) * gprime).astype(BF16)
        g_b = h_ref[:, 2 * aw:3 * aw]
        g_c = h_ref[:, 3 * aw:4 * aw]
        val = h_ref[:, 4 * aw:5 * aw]
        z = g_c * val
        zp = jnp.where(i == n - 1, 0.0, hp_ref[:, 3 * aw:4 * aw] * hp_ref[:, 4 * aw:5 * aw])
        zm1, zm2 = _shift_rows(z, zp)
        cw0, cw1, cw2 = cw_ref[0:1, :], cw_ref[1:2, :], cw_ref[2:3, :]
        conv = cw0 * zm2 + cw1 * zm1 + cw2 * z
        gbv = gb_ref[...]
        _, ybh, rb = _rms_fwd(g_b * conv, gbv)
        dyb, dgb = _rms_bwd(dycat[:, aw:], ybh, rb, gbv)
        dgb_ref[...] += dgb
        dconv = dyb * g_b
        dcw_ref[0:1, :] += jnp.sum(dconv * zm2, axis=0, keepdims=True)
        dcw_ref[1:2, :] += jnp.sum(dconv * zm1, axis=0, keepdims=True)
        dcw_ref[2:3, :] += jnp.sum(dconv * z, axis=0, keepdims=True)
        nxt = dcn_ref[...]
        row = lax.broadcasted_iota(jnp.int32, dconv.shape, 0)
        dcp1 = jnp.where(row == TM - 1, nxt[0:1, :], pltpu.roll(dconv, TM - 1, 0))
        dcp2 = jnp.where(row == TM - 1, nxt[1:2, :],
                         jnp.where(row == TM - 2, nxt[0:1, :], pltpu.roll(dconv, TM - 2, 0)))
        dz = cw2 * dconv + cw1 * dcp1 + cw0 * dcp2
        dcn_ref[...] = dconv[0:8, :]
        dh_ref[:, 2 * aw:3 * aw] = (dyb * conv).astype(BF16)
        dh_ref[:, 3 * aw:4 * aw] = (dz * val).astype(BF16)
        dh_ref[:, 4 * aw:5 * aw] = (dz * g_c).astype(BF16)
        dxn = jnp.zeros((TM, d), F32)
        for k in range(N_CHIPS):
            dxn = dxn + _dot_nt(dh_ref[:, k * ns:(k + 1) * ns], win_ref[k])
        g1v = g1_ref[...]
        _, xh1, r1 = _rms_fwd(x_ref[...], g1v)
        dxr, dg1 = _rms_bwd(dxn, xh1, r1, g1v)
        dg1_ref[...] += dg1
        dx_ref[...] = dx1v + dxr

    ins = (x, dx1, h, h, mixed_all, th_all, g1, lng, lnb, wsp, b_t, cw, ga, gb, w_out, w_in)
    acc_shapes = [(1, d), (1, aw), (1, aw), wsp.shape, (CHUNK, CHUNK), cw.shape, (1, aw), (1, aw)]
    return pl.pallas_call(
        body, name="mixer_bwd", grid=(n,),
        in_specs=[rev(d), rev(d), rev(nh), hprev, rev(aw), rev(d)] + [_full(a.shape) for a in ins[6:]] + [ANY],
        out_specs=[rev(d), rev(nh)] + [_acc(sh) for sh in acc_shapes],
        out_shape=(jax.ShapeDtypeStruct((s, d), F32), jax.ShapeDtypeStruct((s, nh), BF16))
        + tuple(jax.ShapeDtypeStruct(sh, F32) for sh in acc_shapes),
        scratch_shapes=[pltpu.VMEM((TM, aw), F32), pltpu.VMEM((8, aw), F32)],
        compiler_params=_params("arbitrary"),
    )(*ins, after)


def _weight_grad(a, b, name, tm, tn, col_sharded, hosted=None):
    t, m = a.shape
    n = b.shape[1]

    def body(a_ref, b_ref, o_ref):
        o_ref[...] = _dot_tn(a_ref[...].astype(BF16), b_ref[...].astype(BF16)).astype(BF16)

    if col_sharded:
        ns = n // N_CHIPS
        per = ns // tn
        out_shape = jax.ShapeDtypeStruct((N_CHIPS, m, ns), BF16)
        out_spec = pl.BlockSpec((None, tm, tn), lambda i, j: (j // per, i, j % per))
    else:
        out_shape = jax.ShapeDtypeStruct((m, n), BF16)
        out_spec = pl.BlockSpec((tm, tn), lambda i, j: (i, j))
    (out,), extra = _host_call(
        body, name, (m // tm, n // tn),
        [pl.BlockSpec((t, tm), lambda i, j: (0, i)), pl.BlockSpec((t, tn), lambda i, j: (0, j))],
        [out_spec], (out_shape,), [], (a, b), ("parallel", "parallel"), hosted)
    return (out if col_sharded else out.reshape(N_CHIPS, m // N_CHIPS, n)), extra


def _row_tile(rows, cap=256):
    best = None
    for t in range(16, min(rows, cap) + 1, 16):
        if rows % t == 0:
            best = t
    return best if best is not None else rows


def _adamw_math(w, g, m, v):
    m2 = ADAM_B1 * m + (1.0 - ADAM_B1) * g
    v2 = ADAM_B2 * v + (1.0 - ADAM_B2) * (g * g)
    m_hat = m2 / (1.0 - ADAM_B1 ** ADAM_STEP)
    v_hat = v2 / (1.0 - ADAM_B2 ** ADAM_STEP)
    delta = -ADAM_LR * (m_hat / (jnp.sqrt(v_hat) + ADAM_EPS) + ADAM_WD * w)
    return delta, m2, v2


def _adamw(w, g, m, v, name):
    r, c = w.shape
    tr = _row_tile(r) if r >= 16 else r

    def body(w_ref, g_ref, m_ref, v_ref, d_ref, m2_ref, v2_ref):
        d_ref[...], m2_ref[...], v2_ref[...] = _adamw_math(w_ref[...], g_ref[...], m_ref[...], v_ref[...])

    sh = jax.ShapeDtypeStruct((r, c), F32)
    return pl.pallas_call(
        body, name=name, grid=(r // tr,),
        in_specs=[_rows(tr, c)] * 4, out_specs=[_rows(tr, c)] * 3, out_shape=(sh, sh, sh),
        compiler_params=_params("parallel"),
    )(w, g, m, v)


def _finalize(items, place, name, hosted=None, after=None):
    r, c = items[0][2].shape
    tr = _row_tile(r)
    nw = len(items)

    n_after = 0 if after is None else 1

    def body(place_ref, *refs):
        ins, outs = refs[:7 * nw], refs[7 * nw + n_after:]
        for k in range(nw):
            own_ref, s0_ref, s1_ref, s2_ref, w_ref, m_ref, v_ref = ins[7 * k:7 * k + 7]
            g_ref, d_ref, m2_ref, v2_ref = outs[4 * k:4 * k + 4]
            g = own_ref[...].astype(F32) + s0_ref[...].astype(F32)
            g = (g + s1_ref[...].astype(F32)) + s2_ref[...].astype(F32)
            g_ref[...] = g
            d_ref[...], m2_ref[...], v2_ref[...] = _adamw_math(w_ref[...], g, m_ref[...], v_ref[...])

    def slot(k):
        return pl.BlockSpec((None, tr, c), lambda i, pref: (k, i, 0))

    rows = pl.BlockSpec((tr, c), lambda i, pref: (i, 0))
    sh = jax.ShapeDtypeStruct((r, c), F32)
    one = [pl.BlockSpec((None, tr, c), lambda i, pref: (pref[1], i, 0)), slot(0), slot(1), slot(2), rows, rows, rows]
    args = [a for part, slots, w, m, v in items for a in (part, slots, slots, slots, w, m, v)]
    args += [] if after is None else [after]
    res, extra = _host_call(body, name, (r // tr,), one * nw + [ANY] * n_after, [rows] * (4 * nw), (sh,) * (4 * nw),
                            [], args, ("parallel",), hosted, prefetch=(place,))
    return [res[4 * k:4 * k + 4] for k in range(nw)], extra


def _small_sum_adamw(parts, w, m, v):
    nd, r, c = parts.shape

    def body(p_ref, w_ref, m_ref, v_ref, g_ref, d_ref, m2_ref, v2_ref):
        g = p_ref[0]
        for k in range(1, nd):
            g = g + p_ref[k]
        g_ref[...] = g
        d_ref[...], m2_ref[...], v2_ref[...] = _adamw_math(w_ref[...], g, m_ref[...], v_ref[...])

    sh = jax.ShapeDtypeStruct((r, c), F32)
    return pl.pallas_call(
        body, name="small_sum_adamw", out_shape=(sh, sh, sh, sh),
        compiler_params=pltpu.CompilerParams(vmem_limit_bytes=VMEM_LIMIT),
    )(parts, w, m, v)


def _place():
    x, y, c = lax.axis_index("x"), lax.axis_index("y"), lax.axis_index("c")
    chips = [(1 - x, y), (x, 1 - y), (1 - x, 1 - y)]
    return x, y, c, 2 * x + y, chips


def _remote(src, dst, send_sem, recv_sem, to):
    return pltpu.make_async_remote_copy(src_ref=src, dst_ref=dst, send_sem=send_sem, recv_sem=recv_sem,
                                        device_id=to, device_id_type=MESH)


class _Exchange:
    def __init__(self, ins, out_shapes, sem_shape, start, finish, middle=None, in_place=False, peers=()):
        self.ins, self.out_shapes, self.sem_shape = tuple(ins), tuple(out_shapes), sem_shape
        self.start, self.finish, self.middle = start, finish, middle
        self.in_place = in_place
        self.peers = frozenset(peers)
        self.collective_id = None

    def with_id(self, collective_id):
        self.collective_id = collective_id
        return self


_FLIPS = {"c": (0, 0, 1), "x": (1, 0, 0), "y": (0, 1, 0), "xy": (1, 1, 0)}


def _handshake(peers):
    x, y, c, _, _ = _place()
    barrier = pltpu.get_barrier_semaphore()
    for name in sorted(peers):
        fx, fy, fc = _FLIPS[name]
        peer = (1 - x if fx else x, 1 - y if fy else y, 1 - c if fc else c)
        pl.semaphore_signal(barrier, inc=1, device_id=peer, device_id_type=MESH)
    pl.semaphore_wait(barrier, len(peers))


def _run_exchange(ex, name, collective_id=None, casts=()):
    n_in, n_out, n_cast = len(ex.ins), len(ex.out_shapes), len(casts)
    step = 64

    def body(*refs):
        ins, srcs = refs[:n_in], refs[n_in:n_in + n_cast]
        outs = refs[n_in + n_cast:n_in + n_cast + n_out]
        dsts = refs[n_in + n_cast + n_out:n_in + 2 * n_cast + n_out]
        rest = refs[n_in + 2 * n_cast + n_out:]
        wide, narrow = rest[:n_cast], rest[n_cast:2 * n_cast]
        send_sems, recv_sems = rest[2 * n_cast], rest[2 * n_cast + 1]
        if collective_id is not None:
            _handshake(ex.peers)
        ex.start(ins, outs, send_sems, recv_sems)
        loads = [pltpu.make_async_copy(srcs[k], wide[k], rest[2 * n_cast + 2].at[k]) for k in range(n_cast)]
        for cp in loads:
            cp.start()
        if ex.middle is not None:
            ex.middle(ins, outs, send_sems, recv_sems)
        stores = []
        for k in range(n_cast):
            loads[k].wait()

            def chunk(i, carry, k=k):
                rows = pl.ds(pl.multiple_of(i * step, step), step)
                narrow[k][rows, :] = wide[k][rows, :].astype(BF16)
                return carry
            lax.fori_loop(0, casts[k].shape[0] // step, chunk, 0)
            stores.append(pltpu.make_async_copy(narrow[k], dsts[k], rest[2 * n_cast + 2].at[k]))
            stores[-1].start()
        ex.finish(ins, outs, send_sems, recv_sems)
        for cp in stores:
            cp.wait()

    assert collective_id is None or ex.peers
    sem = pltpu.SemaphoreType.DMA(ex.sem_shape)
    scratch = ([pltpu.VMEM(a.shape, F32) for a in casts] + [pltpu.VMEM(a.shape, BF16) for a in casts] + [sem, sem]
               + ([pltpu.SemaphoreType.DMA((n_cast,))] if n_cast else []))
    return pl.pallas_call(
        body, name=name,
        out_shape=ex.out_shapes + tuple(jax.ShapeDtypeStruct(a.shape, BF16) for a in casts),
        in_specs=[ANY] * (n_in + n_cast), out_specs=[ANY] * (n_out + n_cast),
        input_output_aliases={k: k for k in range(n_in)} if ex.in_place else {}, scratch_shapes=scratch,
        compiler_params=pltpu.CompilerParams(collective_id=collective_id, vmem_limit_bytes=VMEM_LIMIT),
    )(*ex.ins, *casts)


def _host_call(body, name, grid, in_specs, out_specs, out_shape, scratch_shapes, args, semantics, hosted,
               prefetch=()):
    hosted = [] if hosted is None else (list(hosted) if isinstance(hosted, (list, tuple)) else [hosted])
    collective_id = hosted[0].collective_id if hosted else None
    peers = frozenset().union(*[ex.peers for ex in hosted]) if hosted else frozenset()
    assert collective_id is None or all(ex.peers for ex in hosted)
    n_pre, n_in, n_out, n_scr = len(prefetch), len(in_specs), len(out_specs), len(scratch_shapes)
    h_ins = [a for ex in hosted for a in ex.ins]
    h_outs = [s for ex in hosted for s in ex.out_shapes]
    h_in, h_out = len(h_ins), len(h_outs)

    def wrapped(*refs):
        pre, refs = refs[:n_pre], refs[n_pre:]
        a, hi = refs[:n_in], refs[n_in:n_in + h_in]
        o = refs[n_in + h_in:n_in + h_in + n_out]
        ho = refs[n_in + h_in + n_out:n_in + h_in + n_out + h_out]
        scr = refs[n_in + h_in + n_out + h_out:]

        def run(phase):
            i0 = o0 = 0
            for k, ex in enumerate(hosted):
                fn = getattr(ex, phase)
                if fn is not None:
                    fn(hi[i0:i0 + len(ex.ins)], ho[o0:o0 + len(ex.out_shapes)], scr[n_scr + 2 * k],
                       scr[n_scr + 2 * k + 1])
                i0, o0 = i0 + len(ex.ins), o0 + len(ex.out_shapes)

        if hosted:
            first = functools.reduce(jnp.logical_and, [pl.program_id(k) == 0 for k in range(len(grid))])

            @pl.when(first)
            def _():
                if collective_id is not None:
                    _handshake(peers)
                run("start")

        if any(ex.middle is not None for ex in hosted):
            half_way = functools.reduce(jnp.logical_and, [
                pl.program_id(0) == max(1, grid[0] * MIDDLE_STEP_16THS // 16)] + [
                pl.program_id(k) == 0 for k in range(1, len(grid))])

            @pl.when(half_way)
            def _():
                run("middle")

        body(*pre, *a, *o, *scr[:n_scr])

        if hosted:
            last = functools.reduce(jnp.logical_and, [pl.program_id(k) == grid[k] - 1 for k in range(len(grid))])

            @pl.when(last)
            def _():
                run("finish")

    sems = [pltpu.SemaphoreType.DMA(ex.sem_shape) for ex in hosted for _ in range(2)]
    aliases, i0, o0 = {}, n_pre + n_in, n_out
    for ex in hosted:
        if ex.in_place:
            aliases.update({i0 + k: o0 + k for k in range(len(ex.ins))})
        i0, o0 = i0 + len(ex.ins), o0 + len(ex.out_shapes)
    all_in, all_out = list(in_specs) + [ANY] * h_in, list(out_specs) + [ANY] * h_out
    all_scr = list(scratch_shapes) + sems
    params = _params(*(["arbitrary"] * len(grid) if hosted else semantics), collective_id=collective_id)
    shapes = tuple(out_shape) + tuple(h_outs)
    if n_pre:
        call = pl.pallas_call(
            wrapped, name=name, out_shape=shapes, input_output_aliases=aliases, compiler_params=params,
            grid_spec=pltpu.PrefetchScalarGridSpec(num_scalar_prefetch=n_pre, grid=grid, in_specs=all_in,
                                                   out_specs=all_out, scratch_shapes=all_scr))
    else:
        call = pl.pallas_call(
            wrapped, name=name, grid=grid, in_specs=all_in, out_specs=all_out, out_shape=shapes,
            scratch_shapes=all_scr, input_output_aliases=aliases, compiler_params=params)
    res = call(*prefetch, *args, *h_ins)
    return res[:n_out], res[n_out:]


def _all_gather(shards, small=()):
    items = tuple(shards) + tuple(small)
    nw = len(shards)

    def place():
        x, y, c, me, _ = _place()
        first = (x + (1 - c) * (1 - 2 * x), y + c * (1 - 2 * y))
        second = (x + c * (1 - 2 * x), y + (1 - c) * (1 - 2 * y))
        diag = (1 - x, 1 - y)
        return x, y, c, me, (first, second, diag)

    def halves(w, c):
        rh = items[w].shape[0] // 2
        return pl.ds(c * rh, rh), pl.ds((1 - c) * rh, rh)

    def start(ins, outs, ss, rs):
        x, y, c, me, chips = place()
        for w in range(len(items)):
            _remote(ins[w], outs[w].at[me], ss.at[w, 6], rs.at[w, 6], (x, y, 1 - c)).start()
            if w < nw:
                mine, _ = halves(w, c)
                _remote(ins[w].at[mine], outs[w].at[me, mine], ss.at[w, 0], rs.at[w, 0], (*chips[0], c)).start()
            else:
                for k in range(3):
                    _remote(ins[w], outs[w].at[me], ss.at[w, k], rs.at[w, k], (*chips[k], c)).start()

    def onward(outs, ss, rs, w, k, x, y, c, chips):
        mine, _ = halves(w, c)
        pk = 2 * chips[k][0] + chips[k][1]
        got = outs[w].at[pk, mine]
        src = chips[1] if k == 2 else chips[k]
        _remote(got, got, ss.at[w, k], rs.at[w, k], (*src, c)).wait_recv()
        if k == 0:
            _remote(got, got, ss.at[w, 2], rs.at[w, 2], (*chips[1], c)).start()
        _remote(got, got, ss.at[w, 3 + k], rs.at[w, 3 + k], (x, y, 1 - c)).start()

    def middle(ins, outs, ss, rs):
        x, y, c, me, chips = place()
        for w in range(nw):
            mine, _ = halves(w, c)
            _remote(ins[w].at[mine], outs[w].at[me, mine], ss.at[w, 1], rs.at[w, 1], (*chips[1], c)).start()
        for w in range(nw):
            onward(outs, ss, rs, w, 0, x, y, c, chips)

    def finish(ins, outs, ss, rs):
        x, y, c, me, chips = place()
        sib = (x, y, 1 - c)
        for k in (1, 2):
            for w in range(nw):
                onward(outs, ss, rs, w, k, x, y, c, chips)
        for w in range(len(items)):
            if w < nw:
                mine, theirs = halves(w, c)
                for k, chip in ((3, chips[1]), (4, chips[0]), (5, chips[2])):
                    oth = outs[w].at[2 * chip[0] + chip[1], theirs]
                    _remote(oth, oth, ss.at[w, k], rs.at[w, k], sib).wait_recv()
                own = ins[w].at[mine]
                for k in range(6):
                    _remote(own, own, ss.at[w, k], rs.at[w, k], sib).wait_send()
            else:
                for k in range(3):
                    got = outs[w].at[2 * chips[k][0] + chips[k][1]]
                    _remote(got, got, ss.at[w, k], rs.at[w, k], (*chips[k], c)).wait_recv()
                    _remote(ins[w], ins[w], ss.at[w, k], rs.at[w, k], sib).wait_send()
            _remote(ins[w], outs[w].at[me], ss.at[w, 6], rs.at[w, 6], sib).wait()

    out_shapes = tuple(jax.ShapeDtypeStruct((N_CHIPS,) + a.shape, a.dtype) for a in items)
    return _Exchange(items, out_shapes, (len(items), 7), start, finish, middle if nw else None,
                     peers=("c", "x", "y", "xy") if small else ("c", "x", "y"))


def _chip_reduce(grads, name, collective_id):
    nw = len(grads)
    step = 64

    def body(*refs):
        ins, outs = refs[:nw], refs[nw:2 * nw]
        own, got = refs[2 * nw:3 * nw], refs[3 * nw:4 * nw]
        send_sems, recv_sems, local_sems = refs[4 * nw:]
        x, y, c, _, _ = _place()
        barrier = pltpu.get_barrier_semaphore()
        pl.semaphore_signal(barrier, inc=1, device_id=(x, y, 1 - c), device_id_type=MESH)
        pl.semaphore_wait(barrier, 1)
        moves = []
        for w in range(nw):
            nb, rh = grads[w].shape[0], grads[w].shape[1] // 2
            for k in range(nb):
                away = _remote(ins[w].at[k, pl.ds((1 - c) * rh, rh), :], got[w].at[k], send_sems.at[w, k],
                               recv_sems.at[w, k], (x, y, 1 - c))
                mine = pltpu.make_async_copy(ins[w].at[k, pl.ds(c * rh, rh), :], own[w].at[k], local_sems.at[w, k])
                away.start()
                mine.start()
                moves.append((w, k, away, mine))
        back = []
        for w, k, away, mine in moves:
            rh = own[w].shape[1]
            mine.wait()
            away.wait()

            def add(i, carry, w=w, k=k):
                rows = pl.ds(pl.multiple_of(i * step, step), step)
                own[w][k, rows, :] = (own[w][k, rows, :].astype(F32) + got[w][k, rows, :].astype(F32)).astype(BF16)
                return carry
            lax.fori_loop(0, rh // step, add, 0)
            tail = rh % step
            if tail:
                rows = slice(rh - tail, rh)
                own[w][k, rows, :] = (own[w][k, rows, :].astype(F32) + got[w][k, rows, :].astype(F32)).astype(BF16)
            wb = pltpu.make_async_copy(own[w].at[k], outs[w].at[k, pl.ds(c * rh, rh), :], local_sems.at[w, k])
            wb.start()
            back.append(wb)
        for wb in back:
            wb.wait()

    halves = [pltpu.VMEM((g.shape[0], g.shape[1] // 2, g.shape[2]), BF16) for g in grads]
    sem = pltpu.SemaphoreType.DMA((nw, N_CHIPS))
    return pl.pallas_call(
        body, name=name, out_shape=tuple(jax.ShapeDtypeStruct(g.shape, BF16) for g in grads),
        in_specs=[ANY] * nw, out_specs=[ANY] * nw, scratch_shapes=halves + halves + [sem, sem, sem],
        compiler_params=pltpu.CompilerParams(vmem_limit_bytes=VMEM_LIMIT, collective_id=collective_id),
    )(*grads)


def _scatter_partials(parts):
    nw = len(parts)

    def copies(ins, outs, ss, rs):
        _, _, c, _, chips = _place()
        res = []
        for r, (px, py) in enumerate(chips):
            for w in range(nw):
                rh = parts[w].shape[1] // 2
                rows = pl.ds(c * rh, rh)
                res.append(_remote(ins[w].at[2 * px + py, rows], outs[w].at[r, rows], ss.at[w, r], rs.at[w, r],
                                   (px, py, c)))
        return res

    def start(ins, outs, ss, rs):
        for cp in copies(ins, outs, ss, rs):
            cp.start()

    def finish(ins, outs, ss, rs):
        for cp in copies(ins, outs, ss, rs):
            cp.wait()

    out_shapes = tuple(jax.ShapeDtypeStruct((3,) + p.shape[1:], p.dtype) for p in parts)
    return _Exchange(parts, out_shapes, (nw, 3), start, finish, peers=("x", "y", "xy"))


_HBM = pl.BlockSpec(memory_space=pltpu.HBM)
_SEM = pl.BlockSpec(memory_space=pltpu.SEMAPHORE)
_EFFECT = pltpu.SideEffectType.DATAFLOW_SIDE_EFFECTING


class _SemGrid:
    def __init__(self, refs, shape):
        assert len(refs) == math.prod(shape)
        self.refs, self.shape, self.at = refs, shape, self

    def __getitem__(self, idx):
        flat = 0
        for i, n in zip(idx, self.shape, strict=True):
            flat = flat * n + i
        return self.refs[flat]


def _split_phase(exs, arrays, sems, phase):
    a0 = s0 = 0
    for ex in exs:
        n_in, n_sem = len(ex.ins), math.prod(ex.sem_shape)
        n_arr = n_in if ex.in_place else n_in + len(ex.out_shapes)
        ins = arrays[a0:a0 + n_in]
        outs = ins if ex.in_place else arrays[a0 + n_in:a0 + n_arr]
        getattr(ex, phase)(ins, outs, _SemGrid(sems[s0:s0 + n_sem], ex.sem_shape),
                           _SemGrid(sems[s0 + n_sem:s0 + 2 * n_sem], ex.sem_shape))
        a0, s0 = a0 + n_arr, s0 + 2 * n_sem


def _split_start(exs, name, collective_id):
    assert all(ex.middle is None and ex.peers for ex in exs)
    arrays = [a for ex in exs for a in ex.ins + (() if ex.in_place else tuple(
        lax.empty(s.shape, s.dtype) for s in ex.out_shapes))]
    n_arr, n_sem = len(arrays), 2 * sum(math.prod(ex.sem_shape) for ex in exs)
    peers = frozenset().union(*[ex.peers for ex in exs])

    def body(*refs):
        _handshake(peers)
        _split_phase(exs, refs[:n_arr], refs[n_arr:n_arr + n_sem], "start")
        refs[-1][...] = jnp.zeros_like(refs[-1])

    res = pl.pallas_call(
        body, name=name + "_start",
        out_shape=(pltpu.SemaphoreType.DMA(()),) * n_sem + tuple(pltpu.HBM(a.shape, a.dtype) for a in arrays)
        + (jax.ShapeDtypeStruct((8, 128), F32),),
        in_specs=(_HBM,) * n_arr,
        out_specs=(_SEM,) * n_sem + (_HBM,) * n_arr + (pl.BlockSpec(memory_space=pltpu.VMEM),),
        input_output_aliases={k: n_sem + k for k in range(n_arr)},
        compiler_params=pltpu.CompilerParams(has_side_effects=_EFFECT, collective_id=collective_id),
    )(*[pltpu.with_memory_space_constraint(a, pltpu.HBM) for a in arrays])
    return res[:n_sem], res[n_sem:n_sem + n_arr], res[-1]


def _split_wait(exs, name, sems, thru, after):
    n_arr, n_sem = len(thru), len(sems)

    def body(*refs):
        _split_phase(exs, refs[:n_arr], refs[n_arr:n_arr + n_sem], "finish")

    return pl.pallas_call(
        body, name=name + "_wait", out_shape=tuple(pltpu.HBM(a.shape, a.dtype) for a in thru),
        in_specs=(_HBM,) * n_arr + (_SEM,) * n_sem + (ANY,) * len(after), out_specs=(_HBM,) * n_arr,
        input_output_aliases={k: k for k in range(n_arr)},
        compiler_params=pltpu.CompilerParams(has_side_effects=_EFFECT),
    )(*thru, *sems, *after)


def _join_partials(parts, slots):
    nw = len(parts)

    def copies(outs, ss, rs, mine):
        x, y, c, me, _ = _place()
        res = []
        for w in range(nw):
            rh = parts[w].shape[1] // 2
            rows = pl.ds((c if mine else 1 - c) * rh, rh)
            own = outs[w].at[me, rows]
            got = outs[nw + w].at[:, rows, :]
            res.append(_remote(own, own, ss.at[w, 0], rs.at[w, 0], (x, y, 1 - c)))
            res.append(_remote(got, got, ss.at[w, 1], rs.at[w, 1], (x, y, 1 - c)))
        return res

    def start(ins, outs, ss, rs):
        for cp in copies(outs, ss, rs, True):
            cp.start()

    def finish(ins, outs, ss, rs):
        for cp in copies(outs, ss, rs, True):
            cp.wait_send()
        for cp in copies(outs, ss, rs, False):
            cp.wait_recv()

    arrays = tuple(parts) + tuple(slots)
    return _Exchange(arrays, tuple(jax.ShapeDtypeStruct(a.shape, a.dtype) for a in arrays), (nw, 2), start, finish,
                     in_place=True, peers=("c",))


def _gather_small(slab):
    def copies(ins, outs, ss, rs):
        x, y, c, _, _ = _place()
        me = 4 * x + 2 * y + c
        out, arrivals = [], []
        for k in range(1, 8):
            px = 1 - x if k & 4 else x
            py = 1 - y if k & 2 else y
            pc = 1 - c if k & 1 else c
            out.append(_remote(ins[0], outs[0].at[me], ss.at[k - 1], rs.at[k - 1], (px, py, pc)))
            theirs = outs[0].at[4 * px + 2 * py + pc]
            arrivals.append((theirs, k - 1, (px, py, pc)))
        return pltpu.make_async_copy(ins[0], outs[0].at[me], ss.at[7]), out, arrivals

    def start(ins, outs, ss, rs):
        own, out, _ = copies(ins, outs, ss, rs)
        own.start()
        for cp in out:
            cp.start()

    def finish(ins, outs, ss, rs):
        own, out, arrivals = copies(ins, outs, ss, rs)
        for cp in out:
            cp.wait_send()
        for theirs, k, peer in arrivals:
            _remote(theirs, theirs, ss.at[k], rs.at[k], peer).wait_recv()
        own.wait()

    return _Exchange((slab,), (jax.ShapeDtypeStruct((8,) + slab.shape, slab.dtype),), (8,), start, finish)


_SMALL_VECS = ("ln_mix_g", "ln_attn_g", "ln_mem_g", "ln_ffn_g", "ln_final_g")


def _pack_small(p, extra, conv):
    d = p["ln_mix_g"].shape[-1]
    top = [p[k].reshape(1, d) for k in _SMALL_VECS]
    top.append(jnp.concatenate([p["sgu_ln_g"].reshape(-1), p["sgu_ln_b"].reshape(-1)]).reshape(1, d))
    top.append(jnp.concatenate([p["grp_norm_a"].reshape(-1), p["grp_norm_b"].reshape(-1)]).reshape(1, d))
    top.append(jnp.concatenate([p["b_spatial"].reshape(-1), extra]).reshape(1, d))
    mid = jnp.zeros((8, d), F32)
    if conv is not None:
        mid = jnp.pad(conv, ((0, 5), (0, d - conv.shape[1])))
    return jnp.concatenate([jnp.concatenate(top, axis=0), mid, p["w_spatial"].reshape(-1, d)], axis=0)


def _unpack_small(slab):
    d = slab.shape[1]
    hw = d // 2
    out = {k: slab[i] for i, k in enumerate(_SMALL_VECS)}
    out["sgu_ln_g"], out["sgu_ln_b"] = slab[5, :hw], slab[5, hw:]
    out["grp_norm_a"], out["grp_norm_b"] = slab[6, :hw], slab[6, hw:]
    out["b_spatial"] = slab[7, :hw].reshape(HEADS, CHUNK)
    out["w_spatial"] = slab[16:].reshape(HEADS, CHUNK, CHUNK)
    return out


_BIG = ("w_in", "w_kv", "w_gate_up", "w_out", "w_q", "w_o", "w_down")
_WEIGHTS = ("ln_mix_g", "w_in", "sgu_ln_g", "sgu_ln_b", "w_spatial", "b_spatial", "conv_w", "grp_norm_a",
            "grp_norm_b", "w_out", "ln_attn_g", "ln_mem_g", "w_q", "w_kv", "w_o", "ln_ffn_g", "w_gate_up",
            "w_down", "ln_final_g")


def _step(p, m_, v_, x, mem, target):
    s, d = x.shape
    hw = d // 2
    row = lambda a: a.reshape(1, -1)
    x_, y_, c_ = lax.axis_index("x"), lax.axis_index("y"), lax.axis_index("c")
    chip = 2 * x_ + y_

    conv8 = jnp.pad(p["conv_w"], ((0, 5), (0, 0)))
    later = ("w_kv", "w_q", "w_o", "w_down", "w_gate_up")
    first = _run_exchange(_all_gather([p["w_in"].astype(BF16), p["w_out"].astype(BF16)], [conv8]),
                          "all_gather_mixer", collective_id=4, casts=[p[k] for k in later])
    (w_in, w_out4, conv4), bf = first[:3], dict(zip(later, first[3:]))
    cw = jnp.transpose(conv4[:, :3, :], (1, 0, 2)).reshape(3, hw)
    b_t = jnp.pad(jnp.transpose(p["b_spatial"]), ((0, 0), (0, CHUNK - HEADS)))
    g1, g2, gm, g3, gf = (row(p[k]) for k in _SMALL_VECS)
    lng, lnb, ga, gb = row(p["sgu_ln_g"]), row(p["sgu_ln_b"]), row(p["grp_norm_a"]), row(p["grp_norm_b"])
    wsp = p["w_spatial"]
    w_out = w_out4.reshape(-1, d)

    (h, x1, ycat, xn1, mixed, th), (w_kv, w_q4, w_o4, w_down4) = _mixer_fwd(
        x, g1, w_in, lng, lnb, wsp, b_t, cw, ga, gb, w_out,
        hosted=_all_gather([bf[k] for k in ("w_kv", "w_q", "w_o", "w_down")]).with_id(5))
    w_q, w_o, w_down = (a.reshape(-1, d) for a in (w_q4, w_o4, w_down4))
    (x2, o, qs, probs, memn, kv), (w_gu,) = _attn_fwd(x1, mem, g2, gm, w_q, w_kv, w_o,
                                                      hosted=_all_gather([bf["w_gate_up"]]).with_id(6))
    dx2, act, dgu, xn3, dx3, loss, dgf, dg3 = _ffn_fwd_bwd(x2, g3, gf, target, w_gu, w_down)

    place = jnp.stack([c_, chip]).astype(jnp.int32)

    def chip_partials(names, grads, tag):
        return list(_chip_reduce(grads, "chip_reduce_" + tag, ("down", "ffn", "attn", "mixer").index(tag)))

    names_d = ("w_down",)
    parts_d = chip_partials(names_d, (_weight_grad(act, dx3, "grad_w_down", 1408, 512, False)[0],), "down")
    g_gu, slots_d = _weight_grad(xn3, dgu, "grad_w_gate_up", 512, 1408, True,
                                 hosted=_scatter_partials(parts_d).with_id(7))
    names_a = ("w_gate_up",)
    parts_a = chip_partials(names_a, (g_gu,), "ffn")
    (dx1, dkv, dg2, g_o, g_q, g_out), slots_a = _attn_bwd(x1, dx2, o, ycat, qs, probs, g2, w_q, kv, w_o,
                                                           hosted=_scatter_partials(parts_a).with_id(8))
    g_kv, dgm = _kv_bwd(dkv, mem, memn, gm, w_kv)
    names_b = ("w_o", "w_out", "w_q", "w_kv")
    shard_major = lambda g: g.reshape(N_CHIPS, -1, d)
    parts_b = chip_partials(names_b, (shard_major(g_o), shard_major(g_out), shard_major(g_q), g_kv), "attn")
    names_da = names_d + names_a
    exs_b = [_scatter_partials(parts_b), _join_partials(parts_d + parts_a, slots_d + slots_a)]
    sems_b, thru_b, token_b = _split_start(exs_b, "rs_scatter_attn", 9)
    dx, dh, dg1, dlng, dlnb, dwsp, dbt, dcw, dga, dgb = _mixer_bwd(
        x, dx1, h, mixed, th, g1, lng, lnb, wsp, b_t, cw, ga, gb, w_out, w_in, after=token_b)
    thru_b = _split_wait(exs_b, "rs_scatter_attn", sems_b, thru_b, [dh])
    nb = len(names_b)
    parts_b, slots_b, joined = thru_b[:nb], thru_b[nb:2 * nb], thru_b[2 * nb:]
    whole = dict(zip(names_da, zip(joined[:len(names_da)], joined[len(names_da):])))
    small = {"ln_mix_g": dg1, "ln_attn_g": dg2, "ln_mem_g": dgm, "ln_ffn_g": dg3, "ln_final_g": dgf,
             "sgu_ln_g": dlng, "sgu_ln_b": dlnb, "grp_norm_a": dga, "grp_norm_b": dgb,
             "b_spatial": jnp.transpose(dbt[:, :HEADS]), "w_spatial": dwsp}
    loss_vec = jnp.pad(loss.reshape(1), (0, hw - 1))
    g_in, extra = _weight_grad(
        xn1, dh, "grad_w_in", 1024, 640, True,
        hosted=[_gather_small(_pack_small(small, loss_vec, dcw)), _join_partials(parts_b, slots_b)])
    parts = extra[0]
    whole.update(zip(names_b, zip(extra[1:1 + len(names_b)], extra[1 + len(names_b):])))
    (part_in,) = chip_partials(("w_in",), (g_in,), "mixer")
    out_g, out_d, out_m, out_v = {}, {}, {}, {}

    def finalize(ks, tag, after=None):
        done, _ = _finalize([(whole[k][0], whole[k][1], p[k], m_[k], v_[k]) for k in ks], place,
                            "finalize_" + tag, after=after)
        for k, (g, dl, nm, nv) in zip(ks, done):
            out_g[k], out_d[k], out_m[k], out_v[k] = g, dl, nm, nv

    exs_in = [_scatter_partials([part_in])]
    sems, thru_in, token = _split_start(exs_in, "rs_scatter_in", 10)
    finalize(("w_down",), "w_down", after=token)
    finalize(("w_o", "w_out", "w_q"), "attn", after=token)
    finalize(("w_gate_up",), "w_gate_up", after=token)
    finalize(("w_kv",), "w_kv", after=token)
    part_in, slots_in = _split_wait(exs_in, "rs_scatter_in", sems, thru_in,
                                    [out_v[k] for k in ("w_down", "w_o", "w_gate_up", "w_kv")])
    whole["w_in"] = _run_exchange(_join_partials([part_in], [slots_in]), "rs_join_mixer", collective_id=11)
    finalize(("w_in",), "w_in")

    zeros = jnp.zeros((hw,), F32)
    sg, sd, sm, sv = _small_sum_adamw(parts, _pack_small(p, zeros, None), _pack_small(m_, zeros, None),
                                      _pack_small(v_, zeros, None))
    for tree, slab in zip((out_g, out_d, out_m, out_v), (sg, sd, sm, sv)):
        tree.update(_unpack_small(slab))
    loss_out = sg[7, hw]
    g_conv = lax.dynamic_slice(sg[8:11, :hw], (0, chip * (hw // N_CHIPS)), (3, hw // N_CHIPS))
    out_g["conv_w"] = g_conv
    out_d["conv_w"], out_m["conv_w"], out_v["conv_w"] = _adamw(p["conv_w"], g_conv, m_["conv_w"], v_["conv_w"],
                                                                "adamw_conv_w")
    return loss_out, dx, out_g, out_d, out_m, out_v


def kernel(x, mem, ln_mix_g, w_in, sgu_ln_g, sgu_ln_b, w_spatial, b_spatial, conv_w, grp_norm_a, grp_norm_b, w_out, ln_attn_g, ln_mem_g, w_q, w_kv, w_o, ln_ffn_g, w_gate_up, w_down, ln_final_g, loss_target, m_ln_mix_g, m_w_in, m_sgu_ln_g, m_sgu_ln_b, m_w_spatial, m_b_spatial, m_conv_w, m_grp_norm_a, m_grp_norm_b, m_w_out, m_ln_attn_g, m_ln_mem_g, m_w_q, m_w_kv, m_w_o, m_ln_ffn_g, m_w_gate_up, m_w_down, m_ln_final_g, v_ln_mix_g, v_w_in, v_sgu_ln_g, v_sgu_ln_b, v_w_spatial, v_b_spatial, v_conv_w, v_grp_norm_a, v_grp_norm_b, v_w_out, v_ln_attn_g, v_ln_mem_g, v_w_q, v_w_kv, v_w_o, v_ln_ffn_g, v_w_gate_up, v_w_down, v_ln_final_g):
    p = dict(ln_mix_g=ln_mix_g, w_in=w_in, sgu_ln_g=sgu_ln_g, sgu_ln_b=sgu_ln_b, w_spatial=w_spatial,
             b_spatial=b_spatial, conv_w=conv_w, grp_norm_a=grp_norm_a, grp_norm_b=grp_norm_b, w_out=w_out,
             ln_attn_g=ln_attn_g, ln_mem_g=ln_mem_g, w_q=w_q, w_kv=w_kv, w_o=w_o, ln_ffn_g=ln_ffn_g,
             w_gate_up=w_gate_up, w_down=w_down, ln_final_g=ln_final_g)
    m_ = dict(ln_mix_g=m_ln_mix_g, w_in=m_w_in, sgu_ln_g=m_sgu_ln_g, sgu_ln_b=m_sgu_ln_b, w_spatial=m_w_spatial,
              b_spatial=m_b_spatial, conv_w=m_conv_w, grp_norm_a=m_grp_norm_a, grp_norm_b=m_grp_norm_b,
              w_out=m_w_out, ln_attn_g=m_ln_attn_g, ln_mem_g=m_ln_mem_g, w_q=m_w_q, w_kv=m_w_kv, w_o=m_w_o,
              ln_ffn_g=m_ln_ffn_g, w_gate_up=m_w_gate_up, w_down=m_w_down, ln_final_g=m_ln_final_g)
    v_ = dict(ln_mix_g=v_ln_mix_g, w_in=v_w_in, sgu_ln_g=v_sgu_ln_g, sgu_ln_b=v_sgu_ln_b, w_spatial=v_w_spatial,
              b_spatial=v_b_spatial, conv_w=v_conv_w, grp_norm_a=v_grp_norm_a, grp_norm_b=v_grp_norm_b,
              w_out=v_w_out, ln_attn_g=v_ln_attn_g, ln_mem_g=v_ln_mem_g, w_q=v_w_q, w_kv=v_w_kv, w_o=v_w_o,
              ln_ffn_g=v_ln_ffn_g, w_gate_up=v_w_gate_up, w_down=v_w_down, ln_final_g=v_ln_final_g)
    s, d = x.shape[-2], x.shape[-1]
    loss, dx, g, dl, nm, nv = _step(p, m_, v_, x.reshape(s, d), mem.reshape(-1, d), loss_target.reshape(s, d))
    outs = [loss, dx.reshape(x.shape)]
    for tree in (g, dl, nm, nv):
        outs += [tree[k].reshape(p[k].shape) for k in _WEIGHTS]
    return tuple(outs)
```

```python
import functools
import math

import jax
import jax.numpy as jnp
from jax import lax
from jax.experimental import pallas as pl
from jax.experimental.pallas import tpu as pltpu

F32 = jnp.float32
BF16 = jnp.bfloat16
EPS = 1e-6
CHUNK = 128
HEADS = 4
N_CHIPS = 4
TM = 512
TM_ATTN = 512
TM_FFN = 256
ADAM_LR, ADAM_B1, ADAM_B2, ADAM_EPS, ADAM_WD, ADAM_STEP = 0.001, 0.9, 0.999, 1e-08, 0.01, 10
GELU_C = math.sqrt(2.0 / math.pi)
GELU_K = 0.044715
SMALL_ROWS = 80
VMEM_LIMIT = 56 * 1024 * 1024
MIDDLE_STEP_16THS = 7
MESH = pl.DeviceIdType.MESH
ANY = pl.BlockSpec(memory_space=pl.ANY)


def _params(*sem, collective_id=None):
    return pltpu.CompilerParams(dimension_semantics=sem, vmem_limit_bytes=VMEM_LIMIT, collective_id=collective_id)


def _dot(a, b):
    return jnp.dot(a, b, preferred_element_type=F32)


def _dot_nt(a, b):
    return lax.dot_general(a, b, (((1,), (1,)), ((), ())), preferred_element_type=F32)


def _dot_tn(a, b):
    return lax.dot_general(a, b, (((0,), (0,)), ((), ())), preferred_element_type=F32)


def _rms_fwd(x, g):
    r = lax.rsqrt(jnp.mean(x * x, axis=-1, keepdims=True) + EPS)
    xh = x * r
    return xh * g, xh, r


def _rms_bwd(dy, xh, r, g):
    dxh = dy * g
    dx = r * (dxh - xh * jnp.mean(dxh * xh, axis=-1, keepdims=True))
    return dx, jnp.sum(dy * xh, axis=0, keepdims=True)


def _full(shape):
    nd = len(shape)
    return pl.BlockSpec(shape, lambda *_: (0,) * nd, pipeline_mode=pl.Buffered(1))


def _acc(shape):
    nd = len(shape)
    return pl.BlockSpec(shape, lambda *_: (0,) * nd)


def _rows(tm, cols):
    return pl.BlockSpec((tm, cols), lambda i: (i, 0))


def _tril_weights(wsp_ref):
    row = lax.broadcasted_iota(jnp.int32, (CHUNK, CHUNK), 0)
    col = lax.broadcasted_iota(jnp.int32, (CHUNK, CHUNK), 1)
    return [jnp.where(row >= col, wsp_ref[hd], 0.0).astype(BF16) for hd in range(HEADS)]


def _shift_rows(z, zp):
    row = lax.broadcasted_iota(jnp.int32, z.shape, 0)
    zm1 = jnp.where(row == 0, zp[7:8, :], pltpu.roll(z, 1, 0))
    zm2 = jnp.where(row == 0, zp[6:7, :], jnp.where(row == 1, zp[7:8, :], pltpu.roll(z, 2, 0)))
    return zm1, zm2


def _gelu_parts(x):
    t = jnp.tanh(GELU_C * (x + GELU_K * (x * x * x)))
    return 0.5 * x * (1.0 + t), t


def _layer_norm_parts(v, g, b):
    mu = jnp.mean(v, axis=-1, keepdims=True)
    vc = v - mu
    rs = lax.rsqrt(jnp.mean(vc * vc, axis=-1, keepdims=True) + EPS)
    vhat = vc * rs
    return vhat * g + b, vhat, rs


def _mixer_fwd(x, g1, w_in, lng, lnb, wsp, b_t, cw, ga, gb, w_out, hosted=None):
    s, d = x.shape
    n = s // TM
    nch = TM // CHUNK
    ns = w_in.shape[2]
    nh = N_CHIPS * ns
    aw = d // 2
    hd_w = aw // HEADS

    def body(x_ref, g1_ref, win_ref, lng_ref, lnb_ref, wsp_ref, bt_ref, cw_ref, ga_ref, gb_ref, wout_ref,
             h_ref, x1_ref, y_ref, xn_ref, mix_ref, th_ref, zp_ref):
        i = pl.program_id(0)

        @pl.when(i == 0)
        def _():
            zp_ref[...] = jnp.zeros_like(zp_ref)

        x = x_ref[...]
        xn, _, _ = _rms_fwd(x, g1_ref[...])
        xnb = xn.astype(BF16)
        xn_ref[...] = xnb
        for k in range(N_CHIPS):
            h_ref[:, k * ns:(k + 1) * ns] = _dot(xnb, win_ref[k])
        a, th = _gelu_parts(h_ref[:, 0:2 * aw])
        th_ref[...] = th
        u = a[:, :aw]
        vn, _, _ = _layer_norm_parts(a[:, aw:], lng_ref[...], lnb_ref[...])
        vnb = vn.astype(BF16)
        wm = _tril_weights(wsp_ref)
        for c in range(nch):
            for hd in range(HEADS):
                blk = vnb[c * CHUNK:(c + 1) * CHUNK, hd * hd_w:(hd + 1) * hd_w]
                mix_ref[c * CHUNK:(c + 1) * CHUNK, hd * hd_w:(hd + 1) * hd_w] = _dot(wm[hd], blk) + bt_ref[:, hd:hd + 1]
        ya, _, _ = _rms_fwd(u * mix_ref[...], ga_ref[...])
        g_b = h_ref[:, 2 * aw:3 * aw]
        z = h_ref[:, 3 * aw:4 * aw] * h_ref[:, 4 * aw:5 * aw]
        zm1, zm2 = _shift_rows(z, zp_ref[...])
        conv = cw_ref[0:1, :] * zm2 + cw_ref[1:2, :] * zm1 + cw_ref[2:3, :] * z
        yb, _, _ = _rms_fwd(g_b * conv, gb_ref[...])
        zp_ref[...] = z[TM - 8:TM, :]
        ycat = jnp.concatenate([ya, yb], axis=-1).astype(BF16)
        y_ref[...] = ycat
        x1_ref[...] = x + _dot(ycat, wout_ref[...])

    return _host_call(
        body, "mixer_fwd", (n,),
        [_rows(TM, d), _full(g1.shape), _full(w_in.shape), _full(lng.shape), _full(lnb.shape),
         _full(wsp.shape), _full(b_t.shape), _full(cw.shape), _full(ga.shape), _full(gb.shape),
         _full(w_out.shape)],
        [_rows(TM, nh), _rows(TM, d), _rows(TM, d), _rows(TM, d), _rows(TM, aw), _rows(TM, d)],
        (jax.ShapeDtypeStruct((s, nh), F32), jax.ShapeDtypeStruct((s, d), F32),
         jax.ShapeDtypeStruct((s, d), BF16), jax.ShapeDtypeStruct((s, d), BF16),
         jax.ShapeDtypeStruct((s, aw), F32), jax.ShapeDtypeStruct((s, d), F32)),
        [pltpu.VMEM((8, aw), F32)],
        (x, g1, w_in, lng, lnb, wsp, b_t, cw, ga, gb, w_out), ("arbitrary",), hosted)


def _attn_fwd(x1, mem, g2, g_mem, w_q, w_kv, w_o, hosted=None):
    s, d = x1.shape
    tm = min(TM_ATTN, s)
    n = s // tm
    dh = d // HEADS
    m = mem.shape[0]
    ns = w_kv.shape[2]
    scale = dh ** -0.5

    def body(x1_ref, mem_ref, g2_ref, gm_ref, wq_ref, wkv_ref, wo_ref, x2_ref, o_ref, q_ref, p_ref, memn_ref, kv_ref):
        @pl.when(pl.program_id(0) == 0)
        def _():
            y, _, _ = _rms_fwd(mem_ref[...], gm_ref[...])
            yb = y.astype(BF16)
            memn_ref[...] = yb
            for k in range(N_CHIPS):
                kv_ref[:, k * ns:(k + 1) * ns] = _dot(yb, wkv_ref[k]).astype(BF16)

        x1v = x1_ref[...]
        xn, _, _ = _rms_fwd(x1v, g2_ref[...])
        q_ref[...] = _dot(xn.astype(BF16), wq_ref[...]).astype(BF16)
        for hd in range(HEADS):
            kh = kv_ref[:, hd * dh:(hd + 1) * dh]
            vh = kv_ref[:, d + hd * dh:d + (hd + 1) * dh]
            sc = _dot_nt(q_ref[:, hd * dh:(hd + 1) * dh], kh) * scale
            e = jnp.exp(sc - jnp.max(sc, axis=-1, keepdims=True))
            p = e / jnp.sum(e, axis=-1, keepdims=True)
            p_ref[:, hd * m:(hd + 1) * m] = p
            o_ref[:, hd * dh:(hd + 1) * dh] = _dot(p.astype(BF16), vh).astype(BF16)
        x2_ref[...] = x1v + _dot(o_ref[...], wo_ref[...])

    return _host_call(
        body, "attn_fwd", (n,),
        [_rows(tm, d), _full(mem.shape), _full(g2.shape), _full(g_mem.shape), _full(w_q.shape), _full(w_kv.shape),
         _full(w_o.shape)],
        [_rows(tm, d), _rows(tm, d), _rows(tm, d), _rows(tm, HEADS * m), _acc((m, d)), _acc((m, 2 * d))],
        (jax.ShapeDtypeStruct((s, d), F32), jax.ShapeDtypeStruct((s, d), BF16), jax.ShapeDtypeStruct((s, d), BF16),
         jax.ShapeDtypeStruct((s, HEADS * m), F32), jax.ShapeDtypeStruct((m, d), BF16),
         jax.ShapeDtypeStruct((m, 2 * d), BF16)),
        [], (x1, mem, g2, g_mem, w_q, w_kv, w_o), ("arbitrary",), hosted)


def _ffn_fwd_bwd(x2, g3, gf, target, w_gu, w_down):
    s, d = x2.shape
    tm = min(TM_FFN, s)
    n = s // tm
    ns = w_gu.shape[2]
    ff = 2 * ns

    def body(x2_ref, g3_ref, gf_ref, t_ref, wgu_ref, wd_ref,
             dx2_ref, act_ref, dgu_ref, xn_ref, dx3_ref, loss_ref, dgf_ref, dg3_ref):
        i = pl.program_id(0)

        @pl.when(i == 0)
        def _():
            loss_ref[...] = jnp.zeros_like(loss_ref)
            dgf_ref[...] = jnp.zeros_like(dgf_ref)
            dg3_ref[...] = jnp.zeros_like(dg3_ref)

        x2v = x2_ref[...]
        xn, xh3, r3 = _rms_fwd(x2v, g3_ref[...])
        xnb = xn.astype(BF16)
        xn_ref[...] = xnb
        x3 = x2v
        saved = []
        for j in range(2):
            g = _dot(xnb, wgu_ref[j])
            u = _dot(xnb, wgu_ref[2 + j])
            sg = 1.0 / (1.0 + jnp.exp(-g))
            sl = g * sg
            actb = (sl * u).astype(BF16)
            act_ref[:, j * ns:(j + 1) * ns] = actb
            x3 = x3 + _dot(actb, wd_ref[j * ns:(j + 1) * ns, :])
            saved.append((u, sl, sg * (1.0 + g * (1.0 - sg))))
        gfv = gf_ref[...]
        y, xhf, rf = _rms_fwd(x3, gfv)
        e = y - t_ref[...]
        loss_ref[...] += 0.5 * jnp.sum(jnp.sum(e * e, axis=-1, keepdims=True), axis=0, keepdims=True) / d
        dx3, dgf = _rms_bwd(e / d, xhf, rf, gfv)
        dgf_ref[...] += dgf
        dx3b = dx3.astype(BF16)
        dx3_ref[...] = dx3b
        dxn = jnp.zeros_like(x2v)
        for j in range(2):
            u, sl, dsl = saved[j]
            dact = _dot_nt(dx3b, wd_ref[j * ns:(j + 1) * ns, :])
            dgb = (dact * u * dsl).astype(BF16)
            dub = (dact * sl).astype(BF16)
            dgu_ref[:, j * ns:(j + 1) * ns] = dgb
            dgu_ref[:, ff + j * ns:ff + (j + 1) * ns] = dub
            dxn = dxn + _dot_nt(dgb, wgu_ref[j]) + _dot_nt(dub, wgu_ref[2 + j])
        dxr, dg3 = _rms_bwd(dxn, xh3, r3, g3_ref[...])
        dg3_ref[...] += dg3
        dx2_ref[...] = dx3 + dxr

    vec = jax.ShapeDtypeStruct((1, d), F32)
    return pl.pallas_call(
        body, name="ffn_fwd_bwd", grid=(n,),
        in_specs=[_rows(tm, d), _full(g3.shape), _full(gf.shape), _rows(tm, d), _full(w_gu.shape),
                  _full(w_down.shape)],
        out_specs=[_rows(tm, d), _rows(tm, ff), _rows(tm, 2 * ff), _rows(tm, d), _rows(tm, d),
                   _acc((1, 1)), _acc((1, d)), _acc((1, d))],
        out_shape=(jax.ShapeDtypeStruct((s, d), F32), jax.ShapeDtypeStruct((s, ff), BF16),
                   jax.ShapeDtypeStruct((s, 2 * ff), BF16), jax.ShapeDtypeStruct((s, d), BF16),
                   jax.ShapeDtypeStruct((s, d), BF16), jax.ShapeDtypeStruct((1, 1), F32), vec, vec),
        compiler_params=_params("arbitrary"),
    )(x2, g3, gf, target, w_gu, w_down)


def _attn_bwd(x1, dx2, o, ycat, qs, probs, g2, w_q, kv, w_o, hosted=None):
    s, d = x1.shape
    tm = min(TM_ATTN, s)
    n = s // tm
    dh = d // HEADS
    scale = dh ** -0.5
    m = kv.shape[0]

    def body(x1_ref, dx2_ref, o_ref, y_ref, q_ref, p_ref, g2_ref, wq_ref, kv_ref, wo_ref,
             dx1_ref, dkv_ref, dg2_ref, gwo_out, gwq_out, gwout_out, dq_ref, gwo_ref, gwq_ref, gwout_ref):
        i = pl.program_id(0)

        @pl.when(i == 0)
        def _():
            for r in (dkv_ref, dg2_ref, gwo_ref, gwq_ref, gwout_ref):
                r[...] = jnp.zeros_like(r)

        xn, xh2, r2 = _rms_fwd(x1_ref[...], g2_ref[...])
        xnb = xn.astype(BF16)
        dx2v = dx2_ref[...]
        dx2b = dx2v.astype(BF16)
        gwo_ref[...] += _dot_tn(o_ref[...], dx2b)
        do = _dot_nt(dx2b, wo_ref[...])
        for hd in range(HEADS):
            qb = q_ref[:, hd * dh:(hd + 1) * dh]
            p = p_ref[:, hd * m:(hd + 1) * m]
            kh = kv_ref[:, hd * dh:(hd + 1) * dh]
            vh = kv_ref[:, d + hd * dh:d + (hd + 1) * dh]
            dob = do[:, hd * dh:(hd + 1) * dh].astype(BF16)
            dp = _dot_nt(dob, vh)
            ds = p * (dp - jnp.sum(dp * p, axis=-1, keepdims=True))
            dsb = (ds * scale).astype(BF16)
            dq_ref[:, hd * dh:(hd + 1) * dh] = _dot(dsb, kh).astype(BF16)
            dkv_ref[:, hd * dh:(hd + 1) * dh] += _dot_tn(dsb, qb)
            dkv_ref[:, d + hd * dh:d + (hd + 1) * dh] += _dot_tn(p.astype(BF16), dob)
        dqb = dq_ref[...]
        gwq_ref[...] += _dot_tn(xnb, dqb)
        dxn = _dot_nt(dqb, wq_ref[...])
        dxr, dg2 = _rms_bwd(dxn, xh2, r2, g2_ref[...])
        dg2_ref[...] += dg2
        dx1 = dx2v + dxr
        dx1_ref[...] = dx1
        gwout_ref[...] += _dot_tn(y_ref[...], dx1.astype(BF16))

        @pl.when(i == n - 1)
        def _():
            for acc, out in ((gwo_ref, gwo_out), (gwq_ref, gwq_out), (gwout_ref, gwout_out)):
                out[...] = acc[...].astype(BF16)

    sq = jax.ShapeDtypeStruct((d, d), BF16)
    return _host_call(
        body, "attn_bwd", (n,),
        [_rows(tm, d), _rows(tm, d), _rows(tm, d), _rows(tm, d), _rows(tm, d), _rows(tm, HEADS * m),
         _full(g2.shape), _full(w_q.shape), _full(kv.shape), _full(w_o.shape)],
        [_rows(tm, d), _acc((m, 2 * d)), _acc((1, d)), _acc((d, d)), _acc((d, d)), _acc((d, d))],
        (jax.ShapeDtypeStruct((s, d), F32), jax.ShapeDtypeStruct((m, 2 * d), F32),
         jax.ShapeDtypeStruct((1, d), F32), sq, sq, sq),
        [pltpu.VMEM((tm, d), BF16)] + [pltpu.VMEM((d, d), F32)] * 3,
        (x1, dx2, o, ycat, qs, probs, g2, w_q, kv, w_o), ("arbitrary",), hosted)


def _kv_bwd(dkv, mem, memn, g_mem, w_kv):
    m, d = mem.shape
    ns = w_kv.shape[2]

    def body(dkv_ref, mem_ref, memn_ref, g_ref, w_ref, gw_ref, dg_ref):
        _, xh, _ = _rms_fwd(mem_ref[...], g_ref[...])
        dmemn = jnp.zeros((m, d), F32)
        for k in range(N_CHIPS):
            dkb = dkv_ref[:, k * ns:(k + 1) * ns].astype(BF16)
            gw_ref[k] = _dot_tn(memn_ref[...], dkb).astype(BF16)
            dmemn = dmemn + _dot_nt(dkb, w_ref[k])
        dg_ref[...] = jnp.sum(dmemn * xh, axis=0, keepdims=True)

    return pl.pallas_call(
        body, name="kv_bwd",
        out_shape=(jax.ShapeDtypeStruct((N_CHIPS, d, ns), BF16), jax.ShapeDtypeStruct((1, d), F32)),
        compiler_params=pltpu.CompilerParams(vmem_limit_bytes=VMEM_LIMIT),
    )(dkv, mem, memn, g_mem, w_kv)


def _mixer_bwd(x, dx1, h, mixed_all, th_all, g1, lng, lnb, wsp, b_t, cw, ga, gb, w_out, w_in, after):
    s, d = x.shape
    n = s // TM
    nch = TM // CHUNK
    ns = w_in.shape[2]
    nh = N_CHIPS * ns
    aw = d // 2
    hd_w = aw // HEADS

    def rev(cols):
        return pl.BlockSpec((TM, cols), lambda i: (n - 1 - i, 0))

    hprev = pl.BlockSpec((8, nh), lambda i: (jnp.maximum((n - 1 - i) * (TM // 8) - 1, 0), 0))

    def in_proj_bwd(dhv, col0, win_ref):
        out, c, end = None, col0, col0 + dhv.shape[1]
        while c < end:
            k, off = divmod(c, ns)
            w = min(ns - off, end - c)
            part = _dot_nt(dhv[:, c - col0:c - col0 + w], win_ref[k, :, off:off + w])
            out = part if out is None else out + part
            c += w
        return out

    def body(x_ref, dx1_ref, h_ref, hp_ref, mix_ref, th_ref, g1_ref, lng_ref, lnb_ref, wsp_ref, bt_ref, cw_ref,
             ga_ref, gb_ref, wout_ref, win_ref, after_ref,
             dx_ref, dh_ref, dg1_ref, dlng_ref, dlnb_ref, dwsp_ref, dbt_ref, dcw_ref, dga_ref, dgb_ref,
             dvn_ref, dcn_ref):
        i = pl.program_id(0)

        @pl.when(i == 0)
        def _():
            for r in (dg1_ref, dlng_ref, dlnb_ref, dwsp_ref, dbt_ref, dcw_ref, dga_ref, dgb_ref, dcn_ref):
                r[...] = jnp.zeros_like(r)

        dx1v = dx1_ref[...]
        dycat = _dot_nt(dx1v.astype(BF16), wout_ref[...])
        ha = h_ref[:, 0:2 * aw]
        th = th_ref[...]
        a = 0.5 * ha * (1.0 + th)
        u = a[:, :aw]
        lngv = lng_ref[...]
        vn, vhat, rs = _layer_norm_parts(a[:, aw:], lngv, lnb_ref[...])
        vnb = vn.astype(BF16)
        wm = _tril_weights(wsp_ref)
        mixed = mix_ref[...]
        gav = ga_ref[...]
        _, yah, ra = _rms_fwd(u * mixed, gav)
        dya, dga = _rms_bwd(dycat[:, :aw], yah, ra, gav)
        dga_ref[...] += dga
        du = dya * mixed
        dmix = dya * u
        dmb = dmix.astype(BF16)
        tri = lax.broadcasted_iota(jnp.int32, (CHUNK, CHUNK), 0) >= lax.broadcasted_iota(jnp.int32, (CHUNK, CHUNK), 1)
        for hd in range(HEADS):
            dw = jnp.zeros((CHUNK, CHUNK), F32)
            db = jnp.zeros((CHUNK, 1), F32)
            for c in range(nch):
                rows = slice(c * CHUNK, (c + 1) * CHUNK)
                cols = slice(hd * hd_w, (hd + 1) * hd_w)
                dvn_ref[rows, cols] = _dot_tn(wm[hd], dmb[rows, cols])
                dw = dw + _dot_nt(dmb[rows, cols], vnb[rows, cols])
                db = db + jnp.sum(dmix[rows, cols], axis=1, keepdims=True)
            dwsp_ref[hd] += jnp.where(tri, dw, 0.0)
            dbt_ref[:, hd:hd + 1] += db
        dvn = dvn_ref[...]
        dlng_ref[...] += jnp.sum(dvn * vhat, axis=0, keepdims=True)
        dlnb_ref[...] += jnp.sum(dvn, axis=0, keepdims=True)
        dvh = dvn * lngv
        dv = rs * (dvh - jnp.mean(dvh, axis=-1, keepdims=True) - vhat * jnp.mean(dvh * vhat, axis=-1, keepdims=True))
        gprime = 0.5 * (1.0 + th) + 0.5 * ha * (1.0 - th * th) * (GELU_C * (1.0 + 3.0 * GELU_K * (ha * ha)))
        dha = (jnp.concatenate([du, dv], axis=-1) * gprime).astype(BF16)
        dh_ref[:, 0:2 * aw] = dha
        dxn = in_proj_bwd(dha, 0, win_ref)
        g_b = h_ref[:, 2 * aw:3 * aw]
        g_c = h_ref[:, 3 * aw:4 * aw]
        val = h_ref[:, 4 * aw:5 * aw]
        z = g_c * val
        zp = jnp.where(i == n - 1, 0.0, hp_ref[:, 3 * aw:4 * aw] * hp_ref[:, 4 * aw:5 * aw])
        zm1, zm2 = _shift_rows(z, zp)
        cw0, cw1, cw2 = cw_ref[0:1, :], cw_ref[1:2, :], cw_ref[2:3, :]
        conv = cw0 * zm2 + cw1 * zm1 + cw2 * z
        gbv = gb_ref[...]
        _, ybh, rb = _rms_fwd(g_b * conv, gbv)
        dyb, dgb = _rms_bwd(dycat[:, aw:], ybh, rb, gbv)
        dgb_ref[...] += dgb
        dconv = dyb * g_b
        dcw_ref[0:1, :] += jnp.sum(dconv * zm2, axis=0, keepdims=True)
        dcw_ref[1:2, :] += jnp.sum(dconv * zm1, axis=0, keepdims=True)
        dcw_ref[2:3, :] += jnp.sum(dconv * z, axis=0, keepdims=True)
        nxt = dcn_ref[...]
        row = lax.broadcasted_iota(jnp.int32, dconv.shape, 0)
        dcp1 = jnp.where(row == TM - 1, nxt[0:1, :], pltpu.roll(dconv, TM - 1, 0))
        dcp2 = jnp.where(row == TM - 1, nxt[1:2, :],
                         jnp.where(row == TM - 2, nxt[0:1, :], pltpu.roll(dconv, TM - 2, 0)))
        dz = cw2 * dconv + cw1 * dcp1 + cw0 * dcp2
        dcn_ref[...] = dconv[0:8, :]
        for k, dhb in enumerate((dyb * conv, dz * val, dz * g_c)):
            dhb = dhb.astype(BF16)
            dh_ref[:, (2 + k) * aw:(3 + k) * aw] = dhb
            dxn = dxn + in_proj_bwd(dhb, (2 + k) * aw, win_ref)
        g1v = g1_ref[...]
        _, xh1, r1 = _rms_fwd(x_ref[...], g1v)
        dxr, dg1 = _rms_bwd(dxn, xh1, r1, g1v)
        dg1_ref[...] += dg1
        dx_ref[...] = dx1v + dxr

    ins = (x, dx1, h, h, mixed_all, th_all, g1, lng, lnb, wsp, b_t, cw, ga, gb, w_out, w_in)
    acc_shapes = [(1, d), (1, aw), (1, aw), wsp.shape, (CHUNK, CHUNK), cw.shape, (1, aw), (1, aw)]
    return pl.pallas_call(
        body, name="mixer_bwd", grid=(n,),
        in_specs=[rev(d), rev(d), rev(nh), hprev, rev(aw), rev(d)] + [_full(a.shape) for a in ins[6:]] + [ANY],
        out_specs=[rev(d), rev(nh)] + [_acc(sh) for sh in acc_shapes],
        out_shape=(jax.ShapeDtypeStruct((s, d), F32), jax.ShapeDtypeStruct((s, nh), BF16))
        + tuple(jax.ShapeDtypeStruct(sh, F32) for sh in acc_shapes),
        scratch_shapes=[pltpu.VMEM((TM, aw), F32), pltpu.VMEM((8, aw), F32)],
        compiler_params=_params("arbitrary"),
    )(*ins, after)


def _weight_grad(a, b, name, tm, tn, col_sharded, hosted=None):
    t, m = a.shape
    n = b.shape[1]

    def body(a_ref, b_ref, o_ref):
        o_ref[...] = _dot_tn(a_ref[...].astype(BF16), b_ref[...].astype(BF16)).astype(BF16)

    if col_sharded:
        ns = n // N_CHIPS
        per = ns // tn
        out_shape = jax.ShapeDtypeStruct((N_CHIPS, m, ns), BF16)
        out_spec = pl.BlockSpec((None, tm, tn), lambda i, j: (j // per, i, j % per))
    else:
        out_shape = jax.ShapeDtypeStruct((m, n), BF16)
        out_spec = pl.BlockSpec((tm, tn), lambda i, j: (i, j))
    (out,), extra = _host_call(
        body, name, (m // tm, n // tn),
        [pl.BlockSpec((t, tm), lambda i, j: (0, i)), pl.BlockSpec((t, tn), lambda i, j: (0, j))],
        [out_spec], (out_shape,), [], (a, b), ("parallel", "parallel"), hosted)
    return (out if col_sharded else out.reshape(N_CHIPS, m // N_CHIPS, n)), extra


def _row_tile(rows, cap=256):
    best = None
    for t in range(16, min(rows, cap) + 1, 16):
        if rows % t == 0:
            best = t
    return best if best is not None else rows


def _adamw_math(w, g, m, v):
    m2 = ADAM_B1 * m + (1.0 - ADAM_B1) * g
    v2 = ADAM_B2 * v + (1.0 - ADAM_B2) * (g * g)
    m_hat = m2 / (1.0 - ADAM_B1 ** ADAM_STEP)
    v_hat = v2 / (1.0 - ADAM_B2 ** ADAM_STEP)
    delta = -ADAM_LR * (m_hat / (jnp.sqrt(v_hat) + ADAM_EPS) + ADAM_WD * w)
    return delta, m2, v2


def _adamw(w, g, m, v, name):
    r, c = w.shape
    tr = _row_tile(r) if r >= 16 else r

    def body(w_ref, g_ref, m_ref, v_ref, d_ref, m2_ref, v2_ref):
        d_ref[...], m2_ref[...], v2_ref[...] = _adamw_math(w_ref[...], g_ref[...], m_ref[...], v_ref[...])

    sh = jax.ShapeDtypeStruct((r, c), F32)
    return pl.pallas_call(
        body, name=name, grid=(r // tr,),
        in_specs=[_rows(tr, c)] * 4, out_specs=[_rows(tr, c)] * 3, out_shape=(sh, sh, sh),
        compiler_params=_params("parallel"),
    )(w, g, m, v)


def _finalize(items, place, name, hosted=None, after=None):
    r, c = items[0][2].shape
    tr = _row_tile(r)
    nw = len(items)

    n_after = 0 if after is None else 1

    def body(place_ref, *refs):
        ins, outs = refs[:7 * nw], refs[7 * nw + n_after:]
        for k in range(nw):
            own_ref, s0_ref, s1_ref, s2_ref, w_ref, m_ref, v_ref = ins[7 * k:7 * k + 7]
            g_ref, d_ref, m2_ref, v2_ref = outs[4 * k:4 * k + 4]
            g = own_ref[...].astype(F32) + s0_ref[...].astype(F32)
            g = (g + s1_ref[...].astype(F32)) + s2_ref[...].astype(F32)
            g_ref[...] = g
            d_ref[...], m2_ref[...], v2_ref[...] = _adamw_math(w_ref[...], g, m_ref[...], v_ref[...])

    def slot(k):
        return pl.BlockSpec((None, tr, c), lambda i, pref: (k, i, 0))

    rows = pl.BlockSpec((tr, c), lambda i, pref: (i, 0))
    sh = jax.ShapeDtypeStruct((r, c), F32)
    one = [pl.BlockSpec((None, tr, c), lambda i, pref: (pref[1], i, 0)), slot(0), slot(1), slot(2), rows, rows, rows]
    args = [a for part, slots, w, m, v in items for a in (part, slots, slots, slots, w, m, v)]
    args += [] if after is None else [after]
    res, extra = _host_call(body, name, (r // tr,), one * nw + [ANY] * n_after, [rows] * (4 * nw), (sh,) * (4 * nw),
                            [], args, ("parallel",), hosted, prefetch=(place,))
    return [res[4 * k:4 * k + 4] for k in range(nw)], extra


def _small_sum_adamw(parts, w, m, v):
    nd, r, c = parts.shape

    def body(p_ref, w_ref, m_ref, v_ref, g_ref, d_ref, m2_ref, v2_ref):
        g = p_ref[0]
        for k in range(1, nd):
            g = g + p_ref[k]
        g_ref[...] = g
        d_ref[...], m2_ref[...], v2_ref[...] = _adamw_math(w_ref[...], g, m_ref[...], v_ref[...])

    sh = jax.ShapeDtypeStruct((r, c), F32)
    return pl.pallas_call(
        body, name="small_sum_adamw", out_shape=(sh, sh, sh, sh),
        compiler_params=pltpu.CompilerParams(vmem_limit_bytes=VMEM_LIMIT),
    )(parts, w, m, v)


def _place():
    x, y, c = lax.axis_index("x"), lax.axis_index("y"), lax.axis_index("c")
    chips = [(1 - x, y), (x, 1 - y), (1 - x, 1 - y)]
    return x, y, c, 2 * x + y, chips


def _remote(src, dst, send_sem, recv_sem, to):
    return pltpu.make_async_remote_copy(src_ref=src, dst_ref=dst, send_sem=send_sem, recv_sem=recv_sem,
                                        device_id=to, device_id_type=MESH)


class _Exchange:
    def __init__(self, ins, out_shapes, sem_shape, start, finish, middle=None, in_place=False, peers=()):
        self.ins, self.out_shapes, self.sem_shape = tuple(ins), tuple(out_shapes), sem_shape
        self.start, self.finish, self.middle = start, finish, middle
        self.in_place = in_place
        self.peers = frozenset(peers)
        self.collective_id = None

    def with_id(self, collective_id):
        self.collective_id = collective_id
        return self


_FLIPS = {"c": (0, 0, 1), "x": (1, 0, 0), "y": (0, 1, 0), "xy": (1, 1, 0)}


def _handshake(peers):
    x, y, c, _, _ = _place()
    barrier = pltpu.get_barrier_semaphore()
    for name in sorted(peers):
        fx, fy, fc = _FLIPS[name]
        peer = (1 - x if fx else x, 1 - y if fy else y, 1 - c if fc else c)
        pl.semaphore_signal(barrier, inc=1, device_id=peer, device_id_type=MESH)
    pl.semaphore_wait(barrier, len(peers))


def _run_exchange(ex, name, collective_id=None, casts=()):
    n_in, n_out, n_cast = len(ex.ins), len(ex.out_shapes), len(casts)
    step = 64

    def body(*refs):
        ins, srcs = refs[:n_in], refs[n_in:n_in + n_cast]
        outs = refs[n_in + n_cast:n_in + n_cast + n_out]
        dsts = refs[n_in + n_cast + n_out:n_in + 2 * n_cast + n_out]
        rest = refs[n_in + 2 * n_cast + n_out:]
        wide, narrow = rest[:n_cast], rest[n_cast:2 * n_cast]
        send_sems, recv_sems = rest[2 * n_cast], rest[2 * n_cast + 1]
        if collective_id is not None:
            _handshake(ex.peers)
        ex.start(ins, outs, send_sems, recv_sems)
        loads = [pltpu.make_async_copy(srcs[k], wide[k], rest[2 * n_cast + 2].at[k]) for k in range(n_cast)]
        for cp in loads:
            cp.start()
        if ex.middle is not None:
            ex.middle(ins, outs, send_sems, recv_sems)
        stores = []
        for k in range(n_cast):
            loads[k].wait()

            def chunk(i, carry, k=k):
                rows = pl.ds(pl.multiple_of(i * step, step), step)
                narrow[k][rows, :] = wide[k][rows, :].astype(BF16)
                return carry
            lax.fori_loop(0, casts[k].shape[0] // step, chunk, 0)
            stores.append(pltpu.make_async_copy(narrow[k], dsts[k], rest[2 * n_cast + 2].at[k]))
            stores[-1].start()
        ex.finish(ins, outs, send_sems, recv_sems)
        for cp in stores:
            cp.wait()

    assert collective_id is None or ex.peers
    sem = pltpu.SemaphoreType.DMA(ex.sem_shape)
    scratch = ([pltpu.VMEM(a.shape, F32) for a in casts] + [pltpu.VMEM(a.shape, BF16) for a in casts] + [sem, sem]
               + ([pltpu.SemaphoreType.DMA((n_cast,))] if n_cast else []))
    return pl.pallas_call(
        body, name=name,
        out_shape=ex.out_shapes + tuple(jax.ShapeDtypeStruct(a.shape, BF16) for a in casts),
        in_specs=[ANY] * (n_in + n_cast), out_specs=[ANY] * (n_out + n_cast),
        input_output_aliases={k: k for k in range(n_in)} if ex.in_place else {}, scratch_shapes=scratch,
        compiler_params=pltpu.CompilerParams(collective_id=collective_id, vmem_limit_bytes=VMEM_LIMIT),
    )(*ex.ins, *casts)


def _host_call(body, name, grid, in_specs, out_specs, out_shape, scratch_shapes, args, semantics, hosted,
               prefetch=()):
    hosted = [] if hosted is None else (list(hosted) if isinstance(hosted, (list, tuple)) else [hosted])
    collective_id = hosted[0].collective_id if hosted else None
    peers = frozenset().union(*[ex.peers for ex in hosted]) if hosted else frozenset()
    assert collective_id is None or all(ex.peers for ex in hosted)
    n_pre, n_in, n_out, n_scr = len(prefetch), len(in_specs), len(out_specs), len(scratch_shapes)
    h_ins = [a for ex in hosted for a in ex.ins]
    h_outs = [s for ex in hosted for s in ex.out_shapes]
    h_in, h_out = len(h_ins), len(h_outs)

    def wrapped(*refs):
        pre, refs = refs[:n_pre], refs[n_pre:]
        a, hi = refs[:n_in], refs[n_in:n_in + h_in]
        o = refs[n_in + h_in:n_in + h_in + n_out]
        ho = refs[n_in + h_in + n_out:n_in + h_in + n_out + h_out]
        scr = refs[n_in + h_in + n_out + h_out:]

        def run(phase):
            i0 = o0 = 0
            for k, ex in enumerate(hosted):
                fn = getattr(ex, phase)
                if fn is not None:
                    fn(hi[i0:i0 + len(ex.ins)], ho[o0:o0 + len(ex.out_shapes)], scr[n_scr + 2 * k],
                       scr[n_scr + 2 * k + 1])
                i0, o0 = i0 + len(ex.ins), o0 + len(ex.out_shapes)

        if hosted:
            first = functools.reduce(jnp.logical_and, [pl.program_id(k) == 0 for k in range(len(grid))])

            @pl.when(first)
            def _():
                if collective_id is not None:
                    _handshake(peers)
                run("start")

        if any(ex.middle is not None for ex in hosted):
            half_way = functools.reduce(jnp.logical_and, [
                pl.program_id(0) == max(1, grid[0] * MIDDLE_STEP_16THS // 16)] + [
                pl.program_id(k) == 0 for k in range(1, len(grid))])

            @pl.when(half_way)
            def _():
                run("middle")

        body(*pre, *a, *o, *scr[:n_scr])

        if hosted:
            last = functools.reduce(jnp.logical_and, [pl.program_id(k) == grid[k] - 1 for k in range(len(grid))])

            @pl.when(last)
            def _():
                run("finish")

    sems = [pltpu.SemaphoreType.DMA(ex.sem_shape) for ex in hosted for _ in range(2)]
    aliases, i0, o0 = {}, n_pre + n_in, n_out
    for ex in hosted:
        if ex.in_place:
            aliases.update({i0 + k: o0 + k for k in range(len(ex.ins))})
        i0, o0 = i0 + len(ex.ins), o0 + len(ex.out_shapes)
    all_in, all_out = list(in_specs) + [ANY] * h_in, list(out_specs) + [ANY] * h_out
    all_scr = list(scratch_shapes) + sems
    params = _params(*(["arbitrary"] * len(grid) if hosted else semantics), collective_id=collective_id)
    shapes = tuple(out_shape) + tuple(h_outs)
    if n_pre:
        call = pl.pallas_call(
            wrapped, name=name, out_shape=shapes, input_output_aliases=aliases, compiler_params=params,
            grid_spec=pltpu.PrefetchScalarGridSpec(num_scalar_prefetch=n_pre, grid=grid, in_specs=all_in,
                                                   out_specs=all_out, scratch_shapes=all_scr))
    else:
        call = pl.pallas_call(
            wrapped, name=name, grid=grid, in_specs=all_in, out_specs=all_out, out_shape=shapes,
            scratch_shapes=all_scr, input_output_aliases=aliases, compiler_params=params)
    res = call(*prefetch, *args, *h_ins)
    return res[:n_out], res[n_out:]


def _all_gather(shards, small=()):
    items = tuple(shards) + tuple(small)
    nw = len(shards)

    def place():
        x, y, c, me, _ = _place()
        first = (x + (1 - c) * (1 - 2 * x), y + c * (1 - 2 * y))
        second = (x + c * (1 - 2 * x), y + (1 - c) * (1 - 2 * y))
        diag = (1 - x, 1 - y)
        return x, y, c, me, (first, second, diag)

    def halves(w, c):
        rh = items[w].shape[0] // 2
        return pl.ds(c * rh, rh), pl.ds((1 - c) * rh, rh)

    def start(ins, outs, ss, rs):
        x, y, c, me, chips = place()
        for w in range(len(items)):
            _remote(ins[w], outs[w].at[me], ss.at[w, 6], rs.at[w, 6], (x, y, 1 - c)).start()
            if w < nw:
                mine, _ = halves(w, c)
                _remote(ins[w].at[mine], outs[w].at[me, mine], ss.at[w, 0], rs.at[w, 0], (*chips[0], c)).start()
            else:
                for k in range(3):
                    _remote(ins[w], outs[w].at[me], ss.at[w, k], rs.at[w, k], (*chips[k], c)).start()

    def onward(outs, ss, rs, w, k, x, y, c, chips):
        mine, _ = halves(w, c)
        pk = 2 * chips[k][0] + chips[k][1]
        got = outs[w].at[pk, mine]
        src = chips[1] if k == 2 else chips[k]
        _remote(got, got, ss.at[w, k], rs.at[w, k], (*src, c)).wait_recv()
        if k == 0:
            _remote(got, got, ss.at[w, 2], rs.at[w, 2], (*chips[1], c)).start()
        _remote(got, got, ss.at[w, 3 + k], rs.at[w, 3 + k], (x, y, 1 - c)).start()

    def middle(ins, outs, ss, rs):
        x, y, c, me, chips = place()
        for w in range(nw):
            mine, _ = halves(w, c)
            _remote(ins[w].at[mine], outs[w].at[me, mine], ss.at[w, 1], rs.at[w, 1], (*chips[1], c)).start()
        for w in range(nw):
            onward(outs, ss, rs, w, 0, x, y, c, chips)

    def finish(ins, outs, ss, rs):
        x, y, c, me, chips = place()
        sib = (x, y, 1 - c)
        for k in (1, 2):
            for w in range(nw):
                onward(outs, ss, rs, w, k, x, y, c, chips)
        for w in range(len(items)):
            if w < nw:
                mine, theirs = halves(w, c)
                for k, chip in ((3, chips[1]), (4, chips[0]), (5, chips[2])):
                    oth = outs[w].at[2 * chip[0] + chip[1], theirs]
                    _remote(oth, oth, ss.at[w, k], rs.at[w, k], sib).wait_recv()
                own = ins[w].at[mine]
                for k in range(6):
                    _remote(own, own, ss.at[w, k], rs.at[w, k], sib).wait_send()
            else:
                for k in range(3):
                    got = outs[w].at[2 * chips[k][0] + chips[k][1]]
                    _remote(got, got, ss.at[w, k], rs.at[w, k], (*chips[k], c)).wait_recv()
                    _remote(ins[w], ins[w], ss.at[w, k], rs.at[w, k], sib).wait_send()
            _remote(ins[w], outs[w].at[me], ss.at[w, 6], rs.at[w, 6], sib).wait()

    out_shapes = tuple(jax.ShapeDtypeStruct((N_CHIPS,) + a.shape, a.dtype) for a in items)
    return _Exchange(items, out_shapes, (len(items), 7), start, finish, middle if nw else None,
                     peers=("c", "x", "y", "xy") if small else ("c", "x", "y"))


def _chip_reduce(grads, name, collective_id):
    nw = len(grads)
    step = 64

    def body(*refs):
        ins, outs = refs[:nw], refs[nw:2 * nw]
        own, got = refs[2 * nw:3 * nw], refs[3 * nw:4 * nw]
        send_sems, recv_sems, local_sems = refs[4 * nw:]
        x, y, c, _, _ = _place()
        barrier = pltpu.get_barrier_semaphore()
        pl.semaphore_signal(barrier, inc=1, device_id=(x, y, 1 - c), device_id_type=MESH)
        pl.semaphore_wait(barrier, 1)
        moves = []
        for w in range(nw):
            nb, rh = grads[w].shape[0], grads[w].shape[1] // 2
            for k in range(nb):
                away = _remote(ins[w].at[k, pl.ds((1 - c) * rh, rh), :], got[w].at[k], send_sems.at[w, k],
                               recv_sems.at[w, k], (x, y, 1 - c))
                mine = pltpu.make_async_copy(ins[w].at[k, pl.ds(c * rh, rh), :], own[w].at[k], local_sems.at[w, k])
                away.start()
                mine.start()
                moves.append((w, k, away, mine))
        back = []
        for w, k, away, mine in moves:
            rh = own[w].shape[1]
            mine.wait()
            away.wait()

            def add(i, carry, w=w, k=k):
                rows = pl.ds(pl.multiple_of(i * step, step), step)
                own[w][k, rows, :] = (own[w][k, rows, :].astype(F32) + got[w][k, rows, :].astype(F32)).astype(BF16)
                return carry
            lax.fori_loop(0, rh // step, add, 0)
            tail = rh % step
            if tail:
                rows = slice(rh - tail, rh)
                own[w][k, rows, :] = (own[w][k, rows, :].astype(F32) + got[w][k, rows, :].astype(F32)).astype(BF16)
            wb = pltpu.make_async_copy(own[w].at[k], outs[w].at[k, pl.ds(c * rh, rh), :], local_sems.at[w, k])
            wb.start()
            back.append(wb)
        for wb in back:
            wb.wait()

    halves = [pltpu.VMEM((g.shape[0], g.shape[1] // 2, g.shape[2]), BF16) for g in grads]
    sem = pltpu.SemaphoreType.DMA((nw, N_CHIPS))
    return pl.pallas_call(
        body, name=name, out_shape=tuple(jax.ShapeDtypeStruct(g.shape, BF16) for g in grads),
        in_specs=[ANY] * nw, out_specs=[ANY] * nw, scratch_shapes=halves + halves + [sem, sem, sem],
        compiler_params=pltpu.CompilerParams(vmem_limit_bytes=VMEM_LIMIT, collective_id=collective_id),
    )(*grads)


def _scatter_partials(parts):
    nw = len(parts)

    def copies(ins, outs, ss, rs):
        _, _, c, _, chips = _place()
        res = []
        for r, (px, py) in enumerate(chips):
            for w in range(nw):
                rh = parts[w].shape[1] // 2
                rows = pl.ds(c * rh, rh)
                res.append(_remote(ins[w].at[2 * px + py, rows], outs[w].at[r, rows], ss.at[w, r], rs.at[w, r],
                                   (px, py, c)))
        return res

    def start(ins, outs, ss, rs):
        for cp in copies(ins, outs, ss, rs):
            cp.start()

    def finish(ins, outs, ss, rs):
        for cp in copies(ins, outs, ss, rs):
            cp.wait()

    out_shapes = tuple(jax.ShapeDtypeStruct((3,) + p.shape[1:], p.dtype) for p in parts)
    return _Exchange(parts, out_shapes, (nw, 3), start, finish, peers=("x", "y", "xy"))


_HBM = pl.BlockSpec(memory_space=pltpu.HBM)
_SEM = pl.BlockSpec(memory_space=pltpu.SEMAPHORE)
_EFFECT = pltpu.SideEffectType.DATAFLOW_SIDE_EFFECTING


class _SemGrid:
    def __init__(self, refs, shape):
        assert len(refs) == math.prod(shape)
        self.refs, self.shape, self.at = refs, shape, self

    def __getitem__(self, idx):
        flat = 0
        for i, n in zip(idx, self.shape, strict=True):
            flat = flat * n + i
        return self.refs[flat]


def _split_phase(exs, arrays, sems, phase):
    a0 = s0 = 0
    for ex in exs:
        n_in, n_sem = len(ex.ins), math.prod(ex.sem_shape)
        n_arr = n_in if ex.in_place else n_in + len(ex.out_shapes)
        ins = arrays[a0:a0 + n_in]
        outs = ins if ex.in_place else arrays[a0 + n_in:a0 + n_arr]
        getattr(ex, phase)(ins, outs, _SemGrid(sems[s0:s0 + n_sem], ex.sem_shape),
                           _SemGrid(sems[s0 + n_sem:s0 + 2 * n_sem], ex.sem_shape))
        a0, s0 = a0 + n_arr, s0 + 2 * n_sem


def _split_start(exs, name, collective_id):
    assert all(ex.middle is None and ex.peers for ex in exs)
    arrays = [a for ex in exs for a in ex.ins + (() if ex.in_place else tuple(
        lax.empty(s.shape, s.dtype) for s in ex.out_shapes))]
    n_arr, n_sem = len(arrays), 2 * sum(math.prod(ex.sem_shape) for ex in exs)
    peers = frozenset().union(*[ex.peers for ex in exs])

    def body(*refs):
        _handshake(peers)
        _split_phase(exs, refs[:n_arr], refs[n_arr:n_arr + n_sem], "start")
        refs[-1][...] = jnp.zeros_like(refs[-1])

    res = pl.pallas_call(
        body, name=name + "_start",
        out_shape=(pltpu.SemaphoreType.DMA(()),) * n_sem + tuple(pltpu.HBM(a.shape, a.dtype) for a in arrays)
        + (jax.ShapeDtypeStruct((8, 128), F32),),
        in_specs=(_HBM,) * n_arr,
        out_specs=(_SEM,) * n_sem + (_HBM,) * n_arr + (pl.BlockSpec(memory_space=pltpu.VMEM),),
        input_output_aliases={k: n_sem + k for k in range(n_arr)},
        compiler_params=pltpu.CompilerParams(has_side_effects=_EFFECT, collective_id=collective_id),
    )(*[pltpu.with_memory_space_constraint(a, pltpu.HBM) for a in arrays])
    return res[:n_sem], res[n_sem:n_sem + n_arr], res[-1]


def _split_wait(exs, name, sems, thru, after):
    n_arr, n_sem = len(thru), len(sems)

    def body(*refs):
        _split_phase(exs, refs[:n_arr], refs[n_arr:n_arr + n_sem], "finish")

    return pl.pallas_call(
        body, name=name + "_wait", out_shape=tuple(pltpu.HBM(a.shape, a.dtype) for a in thru),
        in_specs=(_HBM,) * n_arr + (_SEM,) * n_sem + (ANY,) * len(after), out_specs=(_HBM,) * n_arr,
        input_output_aliases={k: k for k in range(n_arr)},
        compiler_params=pltpu.CompilerParams(has_side_effects=_EFFECT),
    )(*thru, *sems, *after)


def _join_partials(parts, slots):
    nw = len(parts)

    def copies(outs, ss, rs, mine):
        x, y, c, me, _ = _place()
        res = []
        for w in range(nw):
            rh = parts[w].shape[1] // 2
            rows = pl.ds((c if mine else 1 - c) * rh, rh)
            own = outs[w].at[me, rows]
            got = outs[nw + w].at[:, rows, :]
            res.append(_remote(own, own, ss.at[w, 0], rs.at[w, 0], (x, y, 1 - c)))
            res.append(_remote(got, got, ss.at[w, 1], rs.at[w, 1], (x, y, 1 - c)))
        return res

    def start(ins, outs, ss, rs):
        for cp in copies(outs, ss, rs, True):
            cp.start()

    def finish(ins, outs, ss, rs):
        for cp in copies(outs, ss, rs, True):
            cp.wait_send()
        for cp in copies(outs, ss, rs, False):
            cp.wait_recv()

    arrays = tuple(parts) + tuple(slots)
    return _Exchange(arrays, tuple(jax.ShapeDtypeStruct(a.shape, a.dtype) for a in arrays), (nw, 2), start, finish,
                     in_place=True, peers=("c",))


def _gather_small(slab):
    def copies(ins, outs, ss, rs):
        x, y, c, _, _ = _place()
        me = 4 * x + 2 * y + c
        out, arrivals = [], []
        for k in range(1, 8):
            px = 1 - x if k & 4 else x
            py = 1 - y if k & 2 else y
            pc = 1 - c if k & 1 else c
            out.append(_remote(ins[0], outs[0].at[me], ss.at[k - 1], rs.at[k - 1], (px, py, pc)))
            theirs = outs[0].at[4 * px + 2 * py + pc]
            arrivals.append((theirs, k - 1, (px, py, pc)))
        return pltpu.make_async_copy(ins[0], outs[0].at[me], ss.at[7]), out, arrivals

    def start(ins, outs, ss, rs):
        own, out, _ = copies(ins, outs, ss, rs)
        own.start()
        for cp in out:
            cp.start()

    def finish(ins, outs, ss, rs):
        own, out, arrivals = copies(ins, outs, ss, rs)
        for cp in out:
            cp.wait_send()
        for theirs, k, peer in arrivals:
            _remote(theirs, theirs, ss.at[k], rs.at[k], peer).wait_recv()
        own.wait()

    return _Exchange((slab,), (jax.ShapeDtypeStruct((8,) + slab.shape, slab.dtype),), (8,), start, finish)


_SMALL_VECS = ("ln_mix_g", "ln_attn_g", "ln_mem_g", "ln_ffn_g", "ln_final_g")


def _pack_small(p, extra, conv):
    d = p["ln_mix_g"].shape[-1]
    top = [p[k].reshape(1, d) for k in _SMALL_VECS]
    top.append(jnp.concatenate([p["sgu_ln_g"].reshape(-1), p["sgu_ln_b"].reshape(-1)]).reshape(1, d))
    top.append(jnp.concatenate([p["grp_norm_a"].reshape(-1), p["grp_norm_b"].reshape(-1)]).reshape(1, d))
    top.append(jnp.concatenate([p["b_spatial"].reshape(-1), extra]).reshape(1, d))
    mid = jnp.zeros((8, d), F32)
    if conv is not None:
        mid = jnp.pad(conv, ((0, 5), (0, d - conv.shape[1])))
    return jnp.concatenate([jnp.concatenate(top, axis=0), mid, p["w_spatial"].reshape(-1, d)], axis=0)


def _unpack_small(slab):
    d = slab.shape[1]
    hw = d // 2
    out = {k: slab[i] for i, k in enumerate(_SMALL_VECS)}
    out["sgu_ln_g"], out["sgu_ln_b"] = slab[5, :hw], slab[5, hw:]
    out["grp_norm_a"], out["grp_norm_b"] = slab[6, :hw], slab[6, hw:]
    out["b_spatial"] = slab[7, :hw].reshape(HEADS, CHUNK)
    out["w_spatial"] = slab[16:].reshape(HEADS, CHUNK, CHUNK)
    return out


_BIG = ("w_in", "w_kv", "w_gate_up", "w_out", "w_q", "w_o", "w_down")
_WEIGHTS = ("ln_mix_g", "w_in", "sgu_ln_g", "sgu_ln_b", "w_spatial", "b_spatial", "conv_w", "grp_norm_a",
            "grp_norm_b", "w_out", "ln_attn_g", "ln_mem_g", "w_q", "w_kv", "w_o", "ln_ffn_g", "w_gate_up",
            "w_down", "ln_final_g")


def _step(p, m_, v_, x, mem, target):
    s, d = x.shape
    hw = d // 2
    row = lambda a: a.reshape(1, -1)
    x_, y_, c_ = lax.axis_index("x"), lax.axis_index("y"), lax.axis_index("c")
    chip = 2 * x_ + y_

    conv8 = jnp.pad(p["conv_w"], ((0, 5), (0, 0)))
    later = ("w_kv", "w_q", "w_o", "w_down", "w_gate_up")
    first = _run_exchange(_all_gather([p["w_in"].astype(BF16), p["w_out"].astype(BF16)], [conv8]),
                          "all_gather_mixer", collective_id=4, casts=[p[k] for k in later])
    (w_in, w_out4, conv4), bf = first[:3], dict(zip(later, first[3:]))
    cw = jnp.transpose(conv4[:, :3, :], (1, 0, 2)).reshape(3, hw)
    b_t = jnp.pad(jnp.transpose(p["b_spatial"]), ((0, 0), (0, CHUNK - HEADS)))
    g1, g2, gm, g3, gf = (row(p[k]) for k in _SMALL_VECS)
    lng, lnb, ga, gb = row(p["sgu_ln_g"]), row(p["sgu_ln_b"]), row(p["grp_norm_a"]), row(p["grp_norm_b"])
    wsp = p["w_spatial"]
    w_out = w_out4.reshape(-1, d)

    (h, x1, ycat, xn1, mixed, th), (w_kv, w_q4, w_o4, w_down4) = _mixer_fwd(
        x, g1, w_in, lng, lnb, wsp, b_t, cw, ga, gb, w_out,
        hosted=_all_gather([bf[k] for k in ("w_kv", "w_q", "w_o", "w_down")]).with_id(5))
    w_q, w_o, w_down = (a.reshape(-1, d) for a in (w_q4, w_o4, w_down4))
    (x2, o, qs, probs, memn, kv), (w_gu,) = _attn_fwd(x1, mem, g2, gm, w_q, w_kv, w_o,
                                                      hosted=_all_gather([bf["w_gate_up"]]).with_id(6))
    dx2, act, dgu, xn3, dx3, loss, dgf, dg3 = _ffn_fwd_bwd(x2, g3, gf, target, w_gu, w_down)

    place = jnp.stack([c_, chip]).astype(jnp.int32)

    def chip_partials(names, grads, tag):
        return list(_chip_reduce(grads, "chip_reduce_" + tag, ("down", "ffn", "attn", "mixer").index(tag)))

    names_d = ("w_down",)
    parts_d = chip_partials(names_d, (_weight_grad(act, dx3, "grad_w_down", 1408, 512, False)[0],), "down")
    g_gu, slots_d = _weight_grad(xn3, dgu, "grad_w_gate_up", 512, 1408, True,
                                 hosted=_scatter_partials(parts_d).with_id(7))
    names_a = ("w_gate_up",)
    parts_a = chip_partials(names_a, (g_gu,), "ffn")
    (dx1, dkv, dg2, g_o, g_q, g_out), slots_a = _attn_bwd(x1, dx2, o, ycat, qs, probs, g2, w_q, kv, w_o,
                                                           hosted=_scatter_partials(parts_a).with_id(8))
    g_kv, dgm = _kv_bwd(dkv, mem, memn, gm, w_kv)
    names_b = ("w_o", "w_out", "w_q", "w_kv")
    shard_major = lambda g: g.reshape(N_CHIPS, -1, d)
    parts_b = chip_partials(names_b, (shard_major(g_o), shard_major(g_out), shard_major(g_q), g_kv), "attn")
    names_da = names_d + names_a
    exs_b = [_scatter_partials(parts_b), _join_partials(parts_d + parts_a, slots_d + slots_a)]
    sems_b, thru_b, token_b = _split_start(exs_b, "rs_scatter_attn", 9)
    dx, dh, dg1, dlng, dlnb, dwsp, dbt, dcw, dga, dgb = _mixer_bwd(
        x, dx1, h, mixed, th, g1, lng, lnb, wsp, b_t, cw, ga, gb, w_out, w_in, after=token_b)
    thru_b = _split_wait(exs_b, "rs_scatter_attn", sems_b, thru_b, [dh])
    nb = len(names_b)
    parts_b, slots_b, joined = thru_b[:nb], thru_b[nb:2 * nb], thru_b[2 * nb:]
    whole = dict(zip(names_da, zip(joined[:len(names_da)], joined[len(names_da):])))
    small = {"ln_mix_g": dg1, "ln_attn_g": dg2, "ln_mem_g": dgm, "ln_ffn_g": dg3, "ln_final_g": dgf,
             "sgu_ln_g": dlng, "sgu_ln_b": dlnb, "grp_norm_a": dga, "grp_norm_b": dgb,
             "b_spatial": jnp.transpose(dbt[:, :HEADS]), "w_spatial": dwsp}
    loss_vec = jnp.pad(loss.reshape(1), (0, hw - 1))
    g_in, extra = _weight_grad(
        xn1, dh, "grad_w_in", 1024, 640, True,
        hosted=[_gather_small(_pack_small(small, loss_vec, dcw)), _join_partials(parts_b, slots_b)])
    parts = extra[0]
    whole.update(zip(names_b, zip(extra[1:1 + len(names_b)], extra[1 + len(names_b):])))
    (part_in,) = chip_partials(("w_in",), (g_in,), "mixer")
    out_g, out_d, out_m, out_v = {}, {}, {}, {}

    def finalize(ks, tag, after=None):
        done, _ = _finalize([(whole[k][0], whole[k][1], p[k], m_[k], v_[k]) for k in ks], place,
                            "finalize_" + tag, after=after)
        for k, (g, dl, nm, nv) in zip(ks, done):
            out_g[k], out_d[k], out_m[k], out_v[k] = g, dl, nm, nv

    exs_in = [_scatter_partials([part_in])]
    sems, thru_in, token = _split_start(exs_in, "rs_scatter_in", 10)
    finalize(("w_down",), "w_down", after=token)
    finalize(("w_o", "w_out", "w_q"), "attn", after=token)
    finalize(("w_gate_up",), "w_gate_up", after=token)
    finalize(("w_kv",), "w_kv", after=token)
    part_in, slots_in = _split_wait(exs_in, "rs_scatter_in", sems, thru_in,
                                    [out_v[k] for k in ("w_down", "w_o", "w_gate_up", "w_kv")])
    whole["w_in"] = _run_exchange(_join_partials([part_in], [slots_in]), "rs_join_mixer", collective_id=11)
    finalize(("w_in",), "w_in")

    zeros = jnp.zeros((hw,), F32)
    sg, sd, sm, sv = _small_sum_adamw(parts, _pack_small(p, zeros, None), _pack_small(m_, zeros, None),
                                      _pack_small(v_, zeros, None))
    for tree, slab in zip((out_g, out_d, out_m, out_v), (sg, sd, sm, sv)):
        tree.update(_unpack_small(slab))
    loss_out = sg[7, hw]
    g_conv = lax.dynamic_slice(sg[8:11, :hw], (0, chip * (hw // N_CHIPS)), (3, hw // N_CHIPS))
    out_g["conv_w"] = g_conv
    out_d["conv_w"], out_m["conv_w"], out_v["conv_w"] = _adamw(p["conv_w"], g_conv, m_["conv_w"], v_["conv_w"],
                                                                "adamw_conv_w")
    return loss_out, dx, out_g, out_d, out_m, out_v


def kernel(x, mem, ln_mix_g, w_in, sgu_ln_g, sgu_ln_b, w_spatial, b_spatial, conv_w, grp_norm_a, grp_norm_b, w_out, ln_attn_g, ln_mem_g, w_q, w_kv, w_o, ln_ffn_g, w_gate_up, w_down, ln_final_g, loss_target, m_ln_mix_g, m_w_in, m_sgu_ln_g, m_sgu_ln_b, m_w_spatial, m_b_spatial, m_conv_w, m_grp_norm_a, m_grp_norm_b, m_w_out, m_ln_attn_g, m_ln_mem_g, m_w_q, m_w_kv, m_w_o, m_ln_ffn_g, m_w_gate_up, m_w_down, m_ln_final_g, v_ln_mix_g, v_w_in, v_sgu_ln_g, v_sgu_ln_b, v_w_spatial, v_b_spatial, v_conv_w, v_grp_norm_a, v_grp_norm_b, v_w_out, v_ln_attn_g, v_ln_mem_g, v_w_q, v_w_kv, v_w_o, v_ln_ffn_g, v_w_gate_up, v_w_down, v_ln_final_g):
    p = dict(ln_mix_g=ln_mix_g, w_in=w_in, sgu_ln_g=sgu_ln_g, sgu_ln_b=sgu_ln_b, w_spatial=w_spatial,
             b_spatial=b_spatial, conv_w=conv_w, grp_norm_a=grp_norm_a, grp_norm_b=grp_norm_b, w_out=w_out,
             ln_attn_g=ln_attn_g, ln_mem_g=ln_mem_g, w_q=w_q, w_kv=w_kv, w_o=w_o, ln_ffn_g=ln_ffn_g,
             w_gate_up=w_gate_up, w_down=w_down, ln_final_g=ln_final_g)
    m_ = dict(ln_mix_g=m_ln_mix_g, w_in=m_w_in, sgu_ln_g=m_sgu_ln_g, sgu_ln_b=m_sgu_ln_b, w_spatial=m_w_spatial,
              b_spatial=m_b_spatial, conv_w=m_conv_w, grp_norm_a=m_grp_norm_a, grp_norm_b=m_grp_norm_b,
              w_out=m_w_out, ln_attn_g=m_ln_attn_g, ln_mem_g=m_ln_mem_g, w_q=m_w_q, w_kv=m_w_kv, w_o=m_w_o,
              ln_ffn_g=m_ln_ffn_g, w_gate_up=m_w_gate_up, w_down=m_w_down, ln_final_g=m_ln_final_g)
    v_ = dict(ln_mix_g=v_ln_mix_g, w_in=v_w_in, sgu_ln_g=v_sgu_ln_g, sgu_ln_b=v_sgu_ln_b, w_spatial=v_w_spatial,
              b_spatial=v_b_spatial, conv_w=v_conv_w, grp_norm_a=v_grp_norm_a, grp_norm_b=v_grp_norm_b,
              w_out=v_w_out, ln_attn_g=v_ln_attn_g, ln_mem_g=v_ln_mem_g, w_q=v_w_q, w_kv=v_w_kv, w_o=v_w_o,
              ln_ffn_g=v_ln_ffn_g, w_gate_up=v_w_gate_up, w_down=v_w_down, ln_final_g=v_ln_final_g)
    s, d = x.shape[-2], x.shape[-1]
    loss, dx, g, dl, nm, nv = _step(p, m_, v_, x.reshape(s, d), mem.reshape(-1, d), loss_target.reshape(s, d))
    outs = [loss, dx.reshape(x.shape)]
    for tree in (g, dl, nm, nv):
        outs += [tree[k].reshape(p[k].shape) for k in _WEIGHTS]
    return tuple(outs)
```

```python
import functools
import math

import jax
import jax.numpy as jnp
from jax import lax
from jax.experimental import pallas as pl
from jax.experimental.pallas import tpu as pltpu

F32 = jnp.float32
BF16 = jnp.bfloat16
EPS = 1e-6
CHUNK = 128
HEADS = 4
N_CHIPS = 4
TM = 512
TM_ATTN = 512
TM_FFN = 256
ADAM_LR, ADAM_B1, ADAM_B2, ADAM_EPS, ADAM_WD, ADAM_STEP = 0.001, 0.9, 0.999, 1e-08, 0.01, 10
GELU_C = math.sqrt(2.0 / math.pi)
GELU_K = 0.044715
SMALL_ROWS = 80
VMEM_LIMIT = 56 * 1024 * 1024
MIDDLE_STEP_16THS = 7
MESH = pl.DeviceIdType.MESH
ANY = pl.BlockSpec(memory_space=pl.ANY)


def _params(*sem, collective_id=None):
    return pltpu.CompilerParams(dimension_semantics=sem, vmem_limit_bytes=VMEM_LIMIT, collective_id=collective_id)


def _dot(a, b):
    return jnp.dot(a, b, preferred_element_type=F32)


def _dot_nt(a, b):
    return lax.dot_general(a, b, (((1,), (1,)), ((), ())), preferred_element_type=F32)


def _dot_tn(a, b):
    return lax.dot_general(a, b, (((0,), (0,)), ((), ())), preferred_element_type=F32)


def _rms_fwd(x, g):
    r = lax.rsqrt(jnp.mean(x * x, axis=-1, keepdims=True) + EPS)
    xh = x * r
    return xh * g, xh, r


def _rms_bwd(dy, xh, r, g):
    dxh = dy * g
    dx = r * (dxh - xh * jnp.mean(dxh * xh, axis=-1, keepdims=True))
    return dx, jnp.sum(dy * xh, axis=0, keepdims=True)


def _full(shape):
    nd = len(shape)
    return pl.BlockSpec(shape, lambda *_: (0,) * nd, pipeline_mode=pl.Buffered(1))


def _acc(shape):
    nd = len(shape)
    return pl.BlockSpec(shape, lambda *_: (0,) * nd)


def _rows(tm, cols):
    return pl.BlockSpec((tm, cols), lambda i: (i, 0))


def _tril_weights(wsp_ref):
    row = lax.broadcasted_iota(jnp.int32, (CHUNK, CHUNK), 0)
    col = lax.broadcasted_iota(jnp.int32, (CHUNK, CHUNK), 1)
    return [jnp.where(row >= col, wsp_ref[hd], 0.0).astype(BF16) for hd in range(HEADS)]


def _shift_rows(z, zp):
    row = lax.broadcasted_iota(jnp.int32, z.shape, 0)
    zm1 = jnp.where(row == 0, zp[7:8, :], pltpu.roll(z, 1, 0))
    zm2 = jnp.where(row == 0, zp[6:7, :], jnp.where(row == 1, zp[7:8, :], pltpu.roll(z, 2, 0)))
    return zm1, zm2


def _gelu_parts(x):
    t = jnp.tanh(GELU_C * (x + GELU_K * (x * x * x)))
    return 0.5 * x * (1.0 + t), t


def _layer_norm_parts(v, g, b):
    mu = jnp.mean(v, axis=-1, keepdims=True)
    vc = v - mu
    rs = lax.rsqrt(jnp.mean(vc * vc, axis=-1, keepdims=True) + EPS)
    vhat = vc * rs
    return vhat * g + b, vhat, rs


def _mixer_fwd(x, g1, w_in, lng, lnb, wsp, b_t, cw, ga, gb, w_out, hosted=None):
    s, d = x.shape
    n = s // TM
    nch = TM // CHUNK
    ns = w_in.shape[2]
    nh = N_CHIPS * ns
    aw = d // 2
    hd_w = aw // HEADS

    def body(x_ref, g1_ref, win_ref, lng_ref, lnb_ref, wsp_ref, bt_ref, cw_ref, ga_ref, gb_ref, wout_ref,
             h_ref, x1_ref, y_ref, xn_ref, mix_ref, th_ref, zp_ref):
        i = pl.program_id(0)

        @pl.when(i == 0)
        def _():
            zp_ref[...] = jnp.zeros_like(zp_ref)

        x = x_ref[...]
        xn, _, _ = _rms_fwd(x, g1_ref[...])
        xnb = xn.astype(BF16)
        xn_ref[...] = xnb
        for k in range(N_CHIPS):
            h_ref[:, k * ns:(k + 1) * ns] = _dot(xnb, win_ref[k])
        a, th = _gelu_parts(h_ref[:, 0:2 * aw])
        th_ref[...] = th
        u = a[:, :aw]
        vn, _, _ = _layer_norm_parts(a[:, aw:], lng_ref[...], lnb_ref[...])
        vnb = vn.astype(BF16)
        wm = _tril_weights(wsp_ref)
        for c in range(nch):
            for hd in range(HEADS):
                blk = vnb[c * CHUNK:(c + 1) * CHUNK, hd * hd_w:(hd + 1) * hd_w]
                mix_ref[c * CHUNK:(c + 1) * CHUNK, hd * hd_w:(hd + 1) * hd_w] = _dot(wm[hd], blk) + bt_ref[:, hd:hd + 1]
        ya, _, _ = _rms_fwd(u * mix_ref[...], ga_ref[...])
        g_b = h_ref[:, 2 * aw:3 * aw]
        z = h_ref[:, 3 * aw:4 * aw] * h_ref[:, 4 * aw:5 * aw]
        zm1, zm2 = _shift_rows(z, zp_ref[...])
        conv = cw_ref[0:1, :] * zm2 + cw_ref[1:2, :] * zm1 + cw_ref[2:3, :] * z
        yb, _, _ = _rms_fwd(g_b * conv, gb_ref[...])
        zp_ref[...] = z[TM - 8:TM, :]
        ycat = jnp.concatenate([ya, yb], axis=-1).astype(BF16)
        y_ref[...] = ycat
        x1_ref[...] = x + _dot(ycat, wout_ref[...])

    return _host_call(
        body, "mixer_fwd", (n,),
        [_rows(TM, d), _full(g1.shape), _full(w_in.shape), _full(lng.shape), _full(lnb.shape),
         _full(wsp.shape), _full(b_t.shape), _full(cw.shape), _full(ga.shape), _full(gb.shape),
         _full(w_out.shape)],
        [_rows(TM, nh), _rows(TM, d), _rows(TM, d), _rows(TM, d), _rows(TM, aw), _rows(TM, d)],
        (jax.ShapeDtypeStruct((s, nh), F32), jax.ShapeDtypeStruct((s, d), F32),
         jax.ShapeDtypeStruct((s, d), BF16), jax.ShapeDtypeStruct((s, d), BF16),
         jax.ShapeDtypeStruct((s, aw), F32), jax.ShapeDtypeStruct((s, d), F32)),
        [pltpu.VMEM((8, aw), F32)],
        (x, g1, w_in, lng, lnb, wsp, b_t, cw, ga, gb, w_out), ("arbitrary",), hosted)


def _attn_fwd(x1, mem, g2, g_mem, w_q, w_kv, w_o, hosted=None):
    s, d = x1.shape
    tm = min(TM_ATTN, s)
    n = s // tm
    dh = d // HEADS
    m = mem.shape[0]
    ns = w_kv.shape[2]
    scale = dh ** -0.5

    def body(x1_ref, mem_ref, g2_ref, gm_ref, wq_ref, wkv_ref, wo_ref, x2_ref, o_ref, q_ref, p_ref, memn_ref, kv_ref):
        @pl.when(pl.program_id(0) == 0)
        def _():
            y, _, _ = _rms_fwd(mem_ref[...], gm_ref[...])
            yb = y.astype(BF16)
            memn_ref[...] = yb
            for k in range(N_CHIPS):
                kv_ref[:, k * ns:(k + 1) * ns] = _dot(yb, wkv_ref[k]).astype(BF16)

        x1v = x1_ref[...]
        xn, _, _ = _rms_fwd(x1v, g2_ref[...])
        q_ref[...] = _dot(xn.astype(BF16), wq_ref[...]).astype(BF16)
        for hd in range(HEADS):
            kh = kv_ref[:, hd * dh:(hd + 1) * dh]
            vh = kv_ref[:, d + hd * dh:d + (hd + 1) * dh]
            sc = _dot_nt(q_ref[:, hd * dh:(hd + 1) * dh], kh) * scale
            e = jnp.exp(sc - jnp.max(sc, axis=-1, keepdims=True))
            p = e / jnp.sum(e, axis=-1, keepdims=True)
            p_ref[:, hd * m:(hd + 1) * m] = p
            o_ref[:, hd * dh:(hd + 1) * dh] = _dot(p.astype(BF16), vh).astype(BF16)
        x2_ref[...] = x1v + _dot(o_ref[...], wo_ref[...])

    return _host_call(
        body, "attn_fwd", (n,),
        [_rows(tm, d), _full(mem.shape), _full(g2.shape), _full(g_mem.shape), _full(w_q.shape), _full(w_kv.shape),
         _full(w_o.shape)],
        [_rows(tm, d), _rows(tm, d), _rows(tm, d), _rows(tm, HEADS * m), _acc((m, d)), _acc((m, 2 * d))],
        (jax.ShapeDtypeStruct((s, d), F32), jax.ShapeDtypeStruct((s, d), BF16), jax.ShapeDtypeStruct((s, d), BF16),
         jax.ShapeDtypeStruct((s, HEADS * m), F32), jax.ShapeDtypeStruct((m, d), BF16),
         jax.ShapeDtypeStruct((m, 2 * d), BF16)),
        [], (x1, mem, g2, g_mem, w_q, w_kv, w_o), ("arbitrary",), hosted)


def _ffn_fwd_bwd(x2, g3, gf, target, w_gu, w_down):
    s, d = x2.shape
    tm = min(TM_FFN, s)
    n = s // tm
    ns = w_gu.shape[2]
    ff = 2 * ns

    def body(x2_ref, g3_ref, gf_ref, t_ref, wgu_ref, wd_ref,
             dx2_ref, act_ref, dgu_ref, xn_ref, dx3_ref, loss_ref, dgf_ref, dg3_ref):
        i = pl.program_id(0)

        @pl.when(i == 0)
        def _():
            loss_ref[...] = jnp.zeros_like(loss_ref)
            dgf_ref[...] = jnp.zeros_like(dgf_ref)
            dg3_ref[...] = jnp.zeros_like(dg3_ref)

        x2v = x2_ref[...]
        xn, xh3, r3 = _rms_fwd(x2v, g3_ref[...])
        xnb = xn.astype(BF16)
        xn_ref[...] = xnb
        x3 = x2v
        saved = []
        for j in range(2):
            g = _dot(xnb, wgu_ref[j])
            u = _dot(xnb, wgu_ref[2 + j])
            sg = 1.0 / (1.0 + jnp.exp(-g))
            sl = g * sg
            actb = (sl * u).astype(BF16)
            act_ref[:, j * ns:(j + 1) * ns] = actb
            x3 = x3 + _dot(actb, wd_ref[j * ns:(j + 1) * ns, :])
            saved.append((u, sl, sg * (1.0 + g * (1.0 - sg))))
        gfv = gf_ref[...]
        y, xhf, rf = _rms_fwd(x3, gfv)
        e = y - t_ref[...]
        loss_ref[...] += 0.5 * jnp.sum(jnp.sum(e * e, axis=-1, keepdims=True), axis=0, keepdims=True) / d
        dx3, dgf = _rms_bwd(e / d, xhf, rf, gfv)
        dgf_ref[...] += dgf
        dx3b = dx3.astype(BF16)
        dx3_ref[...] = dx3b
        dxn = jnp.zeros_like(x2v)
        for j in range(2):
            u, sl, dsl = saved[j]
            dact = _dot_nt(dx3b, wd_ref[j * ns:(j + 1) * ns, :])
            dgb = (dact * u * dsl).astype(BF16)
            dub = (dact * sl).astype(BF16)
            dgu_ref[:, j * ns:(j + 1) * ns] = dgb
            dgu_ref[:, ff + j * ns:ff + (j + 1) * ns] = dub
            dxn = dxn + _dot_nt(dgb, wgu_ref[j]) + _dot_nt(dub, wgu_ref[2 + j])
        dxr, dg3 = _rms_bwd(dxn, xh3, r3, g3_ref[...])
        dg3_ref[...] += dg3
        dx2_ref[...] = dx3 + dxr

    vec = jax.ShapeDtypeStruct((1, d), F32)
    return pl.pallas_call(
        body, name="ffn_fwd_bwd", grid=(n,),
        in_specs=[_rows(tm, d), _full(g3.shape), _full(gf.shape), _rows(tm, d), _full(w_gu.shape),
                  _full(w_down.shape)],
        out_specs=[_rows(tm, d), _rows(tm, ff), _rows(tm, 2 * ff), _rows(tm, d), _rows(tm, d),
                   _acc((1, 1)), _acc((1, d)), _acc((1, d))],
        out_shape=(jax.ShapeDtypeStruct((s, d), F32), jax.ShapeDtypeStruct((s, ff), BF16),
                   jax.ShapeDtypeStruct((s, 2 * ff), BF16), jax.ShapeDtypeStruct((s, d), BF16),
                   jax.ShapeDtypeStruct((s, d), BF16), jax.ShapeDtypeStruct((1, 1), F32), vec, vec),
        compiler_params=_params("arbitrary"),
    )(x2, g3, gf, target, w_gu, w_down)


def _attn_bwd(x1, dx2, o, ycat, qs, probs, g2, w_q, kv, w_o, hosted=None):
    s, d = x1.shape
    tm = min(TM_ATTN, s)
    n = s // tm
    dh = d // HEADS
    scale = dh ** -0.5
    m = kv.shape[0]

    def body(x1_ref, dx2_ref, o_ref, y_ref, q_ref, p_ref, g2_ref, wq_ref, kv_ref, wo_ref,
             dx1_ref, dkv_ref, dg2_ref, gwo_out, gwq_out, gwout_out, gwo_ref, gwq_ref, gwout_ref):
        i = pl.program_id(0)

        @pl.when(i == 0)
        def _():
            for r in (dkv_ref, dg2_ref, gwo_ref, gwq_ref, gwout_ref):
                r[...] = jnp.zeros_like(r)

        xn, xh2, r2 = _rms_fwd(x1_ref[...], g2_ref[...])
        xnb = xn.astype(BF16)
        dx2v = dx2_ref[...]
        dx2b = dx2v.astype(BF16)
        gwo_ref[...] += _dot_tn(o_ref[...], dx2b)
        do = _dot_nt(dx2b, wo_ref[...])
        dxn = jnp.zeros_like(dx2v)
        for hd in range(HEADS):
            qb = q_ref[:, hd * dh:(hd + 1) * dh]
            p = p_ref[:, hd * m:(hd + 1) * m]
            kh = kv_ref[:, hd * dh:(hd + 1) * dh]
            vh = kv_ref[:, d + hd * dh:d + (hd + 1) * dh]
            dob = do[:, hd * dh:(hd + 1) * dh].astype(BF16)
            dp = _dot_nt(dob, vh)
            ds = p * (dp - jnp.sum(dp * p, axis=-1, keepdims=True))
            dsb = (ds * scale).astype(BF16)
            dqh = _dot(dsb, kh).astype(BF16)
            gwq_ref[:, hd * dh:(hd + 1) * dh] += _dot_tn(xnb, dqh)
            dxn = dxn + _dot_nt(dqh, wq_ref[:, hd * dh:(hd + 1) * dh])
            dkv_ref[:, hd * dh:(hd + 1) * dh] += _dot_tn(dsb, qb)
            dkv_ref[:, d + hd * dh:d + (hd + 1) * dh] += _dot_tn(p.astype(BF16), dob)
        dxr, dg2 = _rms_bwd(dxn, xh2, r2, g2_ref[...])
        dg2_ref[...] += dg2
        dx1 = dx2v + dxr
        dx1_ref[...] = dx1
        gwout_ref[...] += _dot_tn(y_ref[...], dx1.astype(BF16))

        @pl.when(i == n - 1)
        def _():
            for acc, out in ((gwo_ref, gwo_out), (gwq_ref, gwq_out), (gwout_ref, gwout_out)):
                out[...] = acc[...].astype(BF16)

    sq = jax.ShapeDtypeStruct((d, d), BF16)
    return _host_call(
        body, "attn_bwd", (n,),
        [_rows(tm, d), _rows(tm, d), _rows(tm, d), _rows(tm, d), _rows(tm, d), _rows(tm, HEADS * m),
         _full(g2.shape), _full(w_q.shape), _full(kv.shape), _full(w_o.shape)],
        [_rows(tm, d), _acc((m, 2 * d)), _acc((1, d)), _acc((d, d)), _acc((d, d)), _acc((d, d))],
        (jax.ShapeDtypeStruct((s, d), F32), jax.ShapeDtypeStruct((m, 2 * d), F32),
         jax.ShapeDtypeStruct((1, d), F32), sq, sq, sq),
        [pltpu.VMEM((d, d), F32)] * 3,
        (x1, dx2, o, ycat, qs, probs, g2, w_q, kv, w_o), ("arbitrary",), hosted)


def _kv_bwd(dkv, mem, memn, g_mem, w_kv):
    m, d = mem.shape
    ns = w_kv.shape[2]

    def body(dkv_ref, mem_ref, memn_ref, g_ref, w_ref, gw_ref, dg_ref):
        _, xh, _ = _rms_fwd(mem_ref[...], g_ref[...])
        dmemn = jnp.zeros((m, d), F32)
        for k in range(N_CHIPS):
            dkb = dkv_ref[:, k * ns:(k + 1) * ns].astype(BF16)
            gw_ref[k] = _dot_tn(memn_ref[...], dkb).astype(BF16)
            dmemn = dmemn + _dot_nt(dkb, w_ref[k])
        dg_ref[...] = jnp.sum(dmemn * xh, axis=0, keepdims=True)

    return pl.pallas_call(
        body, name="kv_bwd",
        out_shape=(jax.ShapeDtypeStruct((N_CHIPS, d, ns), BF16), jax.ShapeDtypeStruct((1, d), F32)),
        compiler_params=pltpu.CompilerParams(vmem_limit_bytes=VMEM_LIMIT),
    )(dkv, mem, memn, g_mem, w_kv)


def _mixer_bwd(x, dx1, h, mixed_all, th_all, g1, lng, lnb, wsp, b_t, cw, ga, gb, w_out, w_in, after):
    s, d = x.shape
    n = s // TM
    nch = TM // CHUNK
    ns = w_in.shape[2]
    nh = N_CHIPS * ns
    aw = d // 2
    hd_w = aw // HEADS

    def rev(cols):
        return pl.BlockSpec((TM, cols), lambda i: (n - 1 - i, 0))

    hprev = pl.BlockSpec((8, nh), lambda i: (jnp.maximum((n - 1 - i) * (TM // 8) - 1, 0), 0))

    def in_proj_bwd(dhv, col0, win_ref):
        out, c, end = None, col0, col0 + dhv.shape[1]
        while c < end:
            k, off = divmod(c, ns)
            w = min(ns - off, end - c)
            part = _dot_nt(dhv[:, c - col0:c - col0 + w], win_ref[k, :, off:off + w])
            out = part if out is None else out + part
            c += w
        return out

    def body(x_ref, dx1_ref, h_ref, hp_ref, mix_ref, th_ref, g1_ref, lng_ref, lnb_ref, wsp_ref, bt_ref, cw_ref,
             ga_ref, gb_ref, wout_ref, win_ref, after_ref,
             dx_ref, dh_ref, dg1_ref, dlng_ref, dlnb_ref, dwsp_ref, dbt_ref, dcw_ref, dga_ref, dgb_ref,
             dvn_ref, dcn_ref):
        i = pl.program_id(0)

        @pl.when(i == 0)
        def _():
            for r in (dg1_ref, dlng_ref, dlnb_ref, dwsp_ref, dbt_ref, dcw_ref, dga_ref, dgb_ref, dcn_ref):
                r[...] = jnp.zeros_like(r)

        dx1v = dx1_ref[...]
        dycat = _dot_nt(dx1v.astype(BF16), wout_ref[...])
        ha = h_ref[:, 0:2 * aw]
        th = th_ref[...]
        a = 0.5 * ha * (1.0 + th)
        u = a[:, :aw]
        lngv = lng_ref[...]
        vn, vhat, rs = _layer_norm_parts(a[:, aw:], lngv, lnb_ref[...])
        vnb = vn.astype(BF16)
        wm = _tril_weights(wsp_ref)
        mixed = mix_ref[...]
        gav = ga_ref[...]
        _, yah, ra = _rms_fwd(u * mixed, gav)
        dya, dga = _rms_bwd(dycat[:, :aw], yah, ra, gav)
        dga_ref[...] += dga
        du = dya * mixed
        dmix = dya * u
        dmb = dmix.astype(BF16)
        tri = lax.broadcasted_iota(jnp.int32, (CHUNK, CHUNK), 0) >= lax.broadcasted_iota(jnp.int32, (CHUNK, CHUNK), 1)
        for hd in range(HEADS):
            dw = jnp.zeros((CHUNK, CHUNK), F32)
            db = jnp.zeros((CHUNK, 1), F32)
            for c in range(nch):
                rows = slice(c * CHUNK, (c + 1) * CHUNK)
                cols = slice(hd * hd_w, (hd + 1) * hd_w)
                dvn_ref[rows, cols] = _dot_tn(wm[hd], dmb[rows, cols])
                dw = dw + _dot_nt(dmb[rows, cols], vnb[rows, cols])
                db = db + jnp.sum(dmix[rows, cols], axis=1, keepdims=True)
            dwsp_ref[hd] += jnp.where(tri, dw, 0.0)
            dbt_ref[:, hd:hd + 1] += db
        dvn = dvn_ref[...]
        dlng_ref[...] += jnp.sum(dvn * vhat, axis=0, keepdims=True)
        dlnb_ref[...] += jnp.sum(dvn, axis=0, keepdims=True)
        dvh = dvn * lngv
        dv = rs * (dvh - jnp.mean(dvh, axis=-1, keepdims=True) - vhat * jnp.mean(dvh * vhat, axis=-1, keepdims=True))
        gprime = 0.5 * (1.0 + th) + 0.5 * ha * (1.0 - th * th) * (GELU_C * (1.0 + 3.0 * GELU_K * (ha * ha)))
        dha = (jnp.concatenate([du, dv], axis=-1) * gprime).astype(BF16)
        dh_ref[:, 0:2 * aw] = dha
        dxn = in_proj_bwd(dha, 0, win_ref)
        g_b = h_ref[:, 2 * aw:3 * aw]
        g_c = h_ref[:, 3 * aw:4 * aw]
        val = h_ref[:, 4 * aw:5 * aw]
        z = g_c * val
        zp = jnp.where(i == n - 1, 0.0, hp_ref[:, 3 * aw:4 * aw] * hp_ref[:, 4 * aw:5 * aw])
        zm1, zm2 = _shift_rows(z, zp)
        cw0, cw1, cw2 = cw_ref[0:1, :], cw_ref[1:2, :], cw_ref[2:3, :]
        conv = cw0 * zm2 + cw1 * zm1 + cw2 * z
        gbv = gb_ref[...]
        _, ybh, rb = _rms_fwd(g_b * conv, gbv)
        dyb, dgb = _rms_bwd(dycat[:, aw:], ybh, rb, gbv)
        dgb_ref[...] += dgb
        dconv = dyb * g_b
        dcw_ref[0:1, :] += jnp.sum(dconv * zm2, axis=0, keepdims=True)
        dcw_ref[1:2, :] += jnp.sum(dconv * zm1, axis=0, keepdims=True)
        dcw_ref[2:3, :] += jnp.sum(dconv * z, axis=0, keepdims=True)
        nxt = dcn_ref[...]
        row = lax.broadcasted_iota(jnp.int32, dconv.shape, 0)
        dcp1 = jnp.where(row == TM - 1, nxt[0:1, :], pltpu.roll(dconv, TM - 1, 0))
        dcp2 = jnp.where(row == TM - 1, nxt[1:2, :],
                         jnp.where(row == TM - 2, nxt[0:1, :], pltpu.roll(dconv, TM - 2, 0)))
        dz = cw2 * dconv + cw1 * dcp1 + cw0 * dcp2
        dcn_ref[...] = dconv[0:8, :]
        for k, dhb in enumerate((dyb * conv, dz * val, dz * g_c)):
            dhb = dhb.astype(BF16)
            dh_ref[:, (2 + k) * aw:(3 + k) * aw] = dhb
            dxn = dxn + in_proj_bwd(dhb, (2 + k) * aw, win_ref)
        g1v = g1_ref[...]
        _, xh1, r1 = _rms_fwd(x_ref[...], g1v)
        dxr, dg1 = _rms_bwd(dxn, xh1, r1, g1v)
        dg1_ref[...] += dg1
        dx_ref[...] = dx1v + dxr

    ins = (x, dx1, h, h, mixed_all, th_all, g1, lng, lnb, wsp, b_t, cw, ga, gb, w_out, w_in)
    acc_shapes = [(1, d), (1, aw), (1, aw), wsp.shape, (CHUNK, CHUNK), cw.shape, (1, aw), (1, aw)]
    return pl.pallas_call(
        body, name="mixer_bwd", grid=(n,),
        in_specs=[rev(d), rev(d), rev(nh), hprev, rev(aw), rev(d)] + [_full(a.shape) for a in ins[6:]] + [ANY],
        out_specs=[rev(d), rev(nh)] + [_acc(sh) for sh in acc_shapes],
        out_shape=(jax.ShapeDtypeStruct((s, d), F32), jax.ShapeDtypeStruct((s, nh), BF16))
        + tuple(jax.ShapeDtypeStruct(sh, F32) for sh in acc_shapes),
        scratch_shapes=[pltpu.VMEM((TM, aw), F32), pltpu.VMEM((8, aw), F32)],
        compiler_params=_params("arbitrary"),
    )(*ins, after)


def _weight_grad(a, b, name, tm, tn, col_sharded, hosted=None):
    t, m = a.shape
    n = b.shape[1]

    def body(a_ref, b_ref, o_ref):
        o_ref[...] = _dot_tn(a_ref[...].astype(BF16), b_ref[...].astype(BF16)).astype(BF16)

    if col_sharded:
        ns = n // N_CHIPS
        per = ns // tn
        out_shape = jax.ShapeDtypeStruct((N_CHIPS, m, ns), BF16)
        out_spec = pl.BlockSpec((None, tm, tn), lambda i, j: (j // per, i, j % per))
    else:
        out_shape = jax.ShapeDtypeStruct((m, n), BF16)
        out_spec = pl.BlockSpec((tm, tn), lambda i, j: (i, j))
    (out,), extra = _host_call(
        body, name, (m // tm, n // tn),
        [pl.BlockSpec((t, tm), lambda i, j: (0, i)), pl.BlockSpec((t, tn), lambda i, j: (0, j))],
        [out_spec], (out_shape,), [], (a, b), ("parallel", "parallel"), hosted)
    return (out if col_sharded else out.reshape(N_CHIPS, m // N_CHIPS, n)), extra


def _row_tile(rows, cap=256):
    best = None
    for t in range(16, min(rows, cap) + 1, 16):
        if rows % t == 0:
            best = t
    return best if best is not None else rows


def _adamw_math(w, g, m, v):
    m2 = ADAM_B1 * m + (1.0 - ADAM_B1) * g
    v2 = ADAM_B2 * v + (1.0 - ADAM_B2) * (g * g)
    m_hat = m2 / (1.0 - ADAM_B1 ** ADAM_STEP)
    v_hat = v2 / (1.0 - ADAM_B2 ** ADAM_STEP)
    delta = -ADAM_LR * (m_hat / (jnp.sqrt(v_hat) + ADAM_EPS) + ADAM_WD * w)
    return delta, m2, v2


def _adamw(w, g, m, v, name):
    r, c = w.shape
    tr = _row_tile(r) if r >= 16 else r

    def body(w_ref, g_ref, m_ref, v_ref, d_ref, m2_ref, v2_ref):
        d_ref[...], m2_ref[...], v2_ref[...] = _adamw_math(w_ref[...], g_ref[...], m_ref[...], v_ref[...])

    sh = jax.ShapeDtypeStruct((r, c), F32)
    return pl.pallas_call(
        body, name=name, grid=(r // tr,),
        in_specs=[_rows(tr, c)] * 4, out_specs=[_rows(tr, c)] * 3, out_shape=(sh, sh, sh),
        compiler_params=_params("parallel"),
    )(w, g, m, v)


def _finalize(items, place, name, hosted=None, after=None):
    r, c = items[0][2].shape
    tr = _row_tile(r)
    nw = len(items)

    n_after = 0 if after is None else 1

    def body(place_ref, *refs):
        ins, outs = refs[:7 * nw], refs[7 * nw + n_after:]
        for k in range(nw):
            own_ref, s0_ref, s1_ref, s2_ref, w_ref, m_ref, v_ref = ins[7 * k:7 * k + 7]
            g_ref, d_ref, m2_ref, v2_ref = outs[4 * k:4 * k + 4]
            g = own_ref[...].astype(F32) + s0_ref[...].astype(F32)
            g = (g + s1_ref[...].astype(F32)) + s2_ref[...].astype(F32)
            g_ref[...] = g
            d_ref[...], m2_ref[...], v2_ref[...] = _adamw_math(w_ref[...], g, m_ref[...], v_ref[...])

    def slot(k):
        return pl.BlockSpec((None, tr, c), lambda i, pref: (k, i, 0))

    rows = pl.BlockSpec((tr, c), lambda i, pref: (i, 0))
    sh = jax.ShapeDtypeStruct((r, c), F32)
    one = [pl.BlockSpec((None, tr, c), lambda i, pref: (pref[1], i, 0)), slot(0), slot(1), slot(2), rows, rows, rows]
    args = [a for part, slots, w, m, v in items for a in (part, slots, slots, slots, w, m, v)]
    args += [] if after is None else [after]
    res, extra = _host_call(body, name, (r // tr,), one * nw + [ANY] * n_after, [rows] * (4 * nw), (sh,) * (4 * nw),
                            [], args, ("parallel",), hosted, prefetch=(place,))
    return [res[4 * k:4 * k + 4] for k in range(nw)], extra


def _small_sum_adamw(parts, w, m, v):
    nd, r, c = parts.shape

    def body(p_ref, w_ref, m_ref, v_ref, g_ref, d_ref, m2_ref, v2_ref):
        g = p_ref[0]
        for k in range(1, nd):
            g = g + p_ref[k]
        g_ref[...] = g
        d_ref[...], m2_ref[...], v2_ref[...] = _adamw_math(w_ref[...], g, m_ref[...], v_ref[...])

    sh = jax.ShapeDtypeStruct((r, c), F32)
    return pl.pallas_call(
        body, name="small_sum_adamw", out_shape=(sh, sh, sh, sh),
        compiler_params=pltpu.CompilerParams(vmem_limit_bytes=VMEM_LIMIT),
    )(parts, w, m, v)


def _place():
    x, y, c = lax.axis_index("x"), lax.axis_index("y"), lax.axis_index("c")
    chips = [(1 - x, y), (x, 1 - y), (1 - x, 1 - y)]
    return x, y, c, 2 * x + y, chips


def _remote(src, dst, send_sem, recv_sem, to):
    return pltpu.make_async_remote_copy(src_ref=src, dst_ref=dst, send_sem=send_sem, recv_sem=recv_sem,
                                        device_id=to, device_id_type=MESH)


class _Exchange:
    def __init__(self, ins, out_shapes, sem_shape, start, finish, middle=None, in_place=False, peers=()):
        self.ins, self.out_shapes, self.sem_shape = tuple(ins), tuple(out_shapes), sem_shape
        self.start, self.finish, self.middle = start, finish, middle
        self.in_place = in_place
        self.peers = frozenset(peers)
        self.collective_id = None

    def with_id(self, collective_id):
        self.collective_id = collective_id
        return self


_FLIPS = {"c": (0, 0, 1), "x": (1, 0, 0), "y": (0, 1, 0), "xy": (1, 1, 0)}


def _handshake(peers):
    x, y, c, _, _ = _place()
    barrier = pltpu.get_barrier_semaphore()
    for name in sorted(peers):
        fx, fy, fc = _FLIPS[name]
        peer = (1 - x if fx else x, 1 - y if fy else y, 1 - c if fc else c)
        pl.semaphore_signal(barrier, inc=1, device_id=peer, device_id_type=MESH)
    pl.semaphore_wait(barrier, len(peers))


def _run_exchange(ex, name, collective_id=None, casts=()):
    n_in, n_out, n_cast = len(ex.ins), len(ex.out_shapes), len(casts)
    step = 64

    def body(*refs):
        ins, srcs = refs[:n_in], refs[n_in:n_in + n_cast]
        outs = refs[n_in + n_cast:n_in + n_cast + n_out]
        dsts = refs[n_in + n_cast + n_out:n_in + 2 * n_cast + n_out]
        rest = refs[n_in + 2 * n_cast + n_out:]
        wide, narrow = rest[:n_cast], rest[n_cast:2 * n_cast]
        send_sems, recv_sems = rest[2 * n_cast], rest[2 * n_cast + 1]
        if collective_id is not None:
            _handshake(ex.peers)
        ex.start(ins, outs, send_sems, recv_sems)
        loads = [pltpu.make_async_copy(srcs[k], wide[k], rest[2 * n_cast + 2].at[k]) for k in range(n_cast)]
        for cp in loads:
            cp.start()
        if ex.middle is not None:
            ex.middle(ins, outs, send_sems, recv_sems)
        stores = []
        for k in range(n_cast):
            loads[k].wait()

            def chunk(i, carry, k=k):
                rows = pl.ds(pl.multiple_of(i * step, step), step)
                narrow[k][rows, :] = wide[k][rows, :].astype(BF16)
                return carry
            lax.fori_loop(0, casts[k].shape[0] // step, chunk, 0)
            stores.append(pltpu.make_async_copy(narrow[k], dsts[k], rest[2 * n_cast + 2].at[k]))
            stores[-1].start()
        ex.finish(ins, outs, send_sems, recv_sems)
        for cp in stores:
            cp.wait()

    assert collective_id is None or ex.peers
    sem = pltpu.SemaphoreType.DMA(ex.sem_shape)
    scratch = ([pltpu.VMEM(a.shape, F32) for a in casts] + [pltpu.VMEM(a.shape, BF16) for a in casts] + [sem, sem]
               + ([pltpu.SemaphoreType.DMA((n_cast,))] if n_cast else []))
    return pl.pallas_call(
        body, name=name,
        out_shape=ex.out_shapes + tuple(jax.ShapeDtypeStruct(a.shape, BF16) for a in casts),
        in_specs=[ANY] * (n_in + n_cast), out_specs=[ANY] * (n_out + n_cast),
        input_output_aliases={k: k for k in range(n_in)} if ex.in_place else {}, scratch_shapes=scratch,
        compiler_params=pltpu.CompilerParams(collective_id=collective_id, vmem_limit_bytes=VMEM_LIMIT),
    )(*ex.ins, *casts)


def _host_call(body, name, grid, in_specs, out_specs, out_shape, scratch_shapes, args, semantics, hosted,
               prefetch=()):
    hosted = [] if hosted is None else (list(hosted) if isinstance(hosted, (list, tuple)) else [hosted])
    collective_id = hosted[0].collective_id if hosted else None
    peers = frozenset().union(*[ex.peers for ex in hosted]) if hosted else frozenset()
    assert collective_id is None or all(ex.peers for ex in hosted)
    n_pre, n_in, n_out, n_scr = len(prefetch), len(in_specs), len(out_specs), len(scratch_shapes)
    h_ins = [a for ex in hosted for a in ex.ins]
    h_outs = [s for ex in hosted for s in ex.out_shapes]
    h_in, h_out = len(h_ins), len(h_outs)

    def wrapped(*refs):
        pre, refs = refs[:n_pre], refs[n_pre:]
        a, hi = refs[:n_in], refs[n_in:n_in + h_in]
        o = refs[n_in + h_in:n_in + h_in + n_out]
        ho = refs[n_in + h_in + n_out:n_in + h_in + n_out + h_out]
        scr = refs[n_in + h_in + n_out + h_out:]

        def run(phase):
            i0 = o0 = 0
            for k, ex in enumerate(hosted):
                fn = getattr(ex, phase)
                if fn is not None:
                    fn(hi[i0:i0 + len(ex.ins)], ho[o0:o0 + len(ex.out_shapes)], scr[n_scr + 2 * k],
                       scr[n_scr + 2 * k + 1])
                i0, o0 = i0 + len(ex.ins), o0 + len(ex.out_shapes)

        if hosted:
            first = functools.reduce(jnp.logical_and, [pl.program_id(k) == 0 for k in range(len(grid))])

            @pl.when(first)
            def _():
                if collective_id is not None:
                    _handshake(peers)
                run("start")

        if any(ex.middle is not None for ex in hosted):
            half_way = functools.reduce(jnp.logical_and, [
                pl.program_id(0) == max(1, grid[0] * MIDDLE_STEP_16THS // 16)] + [
                pl.program_id(k) == 0 for k in range(1, len(grid))])

            @pl.when(half_way)
            def _():
                run("middle")

        body(*pre, *a, *o, *scr[:n_scr])

        if hosted:
            last = functools.reduce(jnp.logical_and, [pl.program_id(k) == grid[k] - 1 for k in range(len(grid))])

            @pl.when(last)
            def _():
                run("finish")

    sems = [pltpu.SemaphoreType.DMA(ex.sem_shape) for ex in hosted for _ in range(2)]
    aliases, i0, o0 = {}, n_pre + n_in, n_out
    for ex in hosted:
        if ex.in_place:
            aliases.update({i0 + k: o0 + k for k in range(len(ex.ins))})
        i0, o0 = i0 + len(ex.ins), o0 + len(ex.out_shapes)
    all_in, all_out = list(in_specs) + [ANY] * h_in, list(out_specs) + [ANY] * h_out
    all_scr = list(scratch_shapes) + sems
    params = _params(*(["arbitrary"] * len(grid) if hosted else semantics), collective_id=collective_id)
    shapes = tuple(out_shape) + tuple(h_outs)
    if n_pre:
        call = pl.pallas_call(
            wrapped, name=name, out_shape=shapes, input_output_aliases=aliases, compiler_params=params,
            grid_spec=pltpu.PrefetchScalarGridSpec(num_scalar_prefetch=n_pre, grid=grid, in_specs=all_in,
                                                   out_specs=all_out, scratch_shapes=all_scr))
    else:
        call = pl.pallas_call(
            wrapped, name=name, grid=grid, in_specs=all_in, out_specs=all_out, out_shape=shapes,
            scratch_shapes=all_scr, input_output_aliases=aliases, compiler_params=params)
    res = call(*prefetch, *args, *h_ins)
    return res[:n_out], res[n_out:]


def _all_gather(shards, small=()):
    items = tuple(shards) + tuple(small)
    nw = len(shards)

    def place():
        x, y, c, me, _ = _place()
        first = (x + (1 - c) * (1 - 2 * x), y + c * (1 - 2 * y))
        second = (x + c * (1 - 2 * x), y + (1 - c) * (1 - 2 * y))
        diag = (1 - x, 1 - y)
        return x, y, c, me, (first, second, diag)

    def halves(w, c):
        rh = items[w].shape[0] // 2
        return pl.ds(c * rh, rh), pl.ds((1 - c) * rh, rh)

    def start(ins, outs, ss, rs):
        x, y, c, me, chips = place()
        for w in range(len(items)):
            _remote(ins[w], outs[w].at[me], ss.at[w, 6], rs.at[w, 6], (x, y, 1 - c)).start()
            if w < nw:
                mine, _ = halves(w, c)
                _remote(ins[w].at[mine], outs[w].at[me, mine], ss.at[w, 0], rs.at[w, 0], (*chips[0], c)).start()
            else:
                for k in range(3):
                    _remote(ins[w], outs[w].at[me], ss.at[w, k], rs.at[w, k], (*chips[k], c)).start()

    def onward(outs, ss, rs, w, k, x, y, c, chips):
        mine, _ = halves(w, c)
        pk = 2 * chips[k][0] + chips[k][1]
        got = outs[w].at[pk, mine]
        src = chips[1] if k == 2 else chips[k]
        _remote(got, got, ss.at[w, k], rs.at[w, k], (*src, c)).wait_recv()
        if k == 0:
            _remote(got, got, ss.at[w, 2], rs.at[w, 2], (*chips[1], c)).start()
        _remote(got, got, ss.at[w, 3 + k], rs.at[w, 3 + k], (x, y, 1 - c)).start()

    def middle(ins, outs, ss, rs):
        x, y, c, me, chips = place()
        for w in range(nw):
            mine, _ = halves(w, c)
            _remote(ins[w].at[mine], outs[w].at[me, mine], ss.at[w, 1], rs.at[w, 1], (*chips[1], c)).start()
        for w in range(nw):
            onward(outs, ss, rs, w, 0, x, y, c, chips)

    def finish(ins, outs, ss, rs):
        x, y, c, me, chips = place()
        sib = (x, y, 1 - c)
        for k in (1, 2):
            for w in range(nw):
                onward(outs, ss, rs, w, k, x, y, c, chips)
        for w in range(len(items)):
            if w < nw:
                mine, theirs = halves(w, c)
                for k, chip in ((3, chips[1]), (4, chips[0]), (5, chips[2])):
                    oth = outs[w].at[2 * chip[0] + chip[1], theirs]
                    _remote(oth, oth, ss.at[w, k], rs.at[w, k], sib).wait_recv()
                own = ins[w].at[mine]
                for k in range(6):
                    _remote(own, own, ss.at[w, k], rs.at[w, k], sib).wait_send()
            else:
                for k in range(3):
                    got = outs[w].at[2 * chips[k][0] + chips[k][1]]
                    _remote(got, got, ss.at[w, k], rs.at[w, k], (*chips[k], c)).wait_recv()
                    _remote(ins[w], ins[w], ss.at[w, k], rs.at[w, k], sib).wait_send()
            _remote(ins[w], outs[w].at[me], ss.at[w, 6], rs.at[w, 6], sib).wait()

    out_shapes = tuple(jax.ShapeDtypeStruct((N_CHIPS,) + a.shape, a.dtype) for a in items)
    return _Exchange(items, out_shapes, (len(items), 7), start, finish, middle if nw else None,
                     peers=("c", "x", "y", "xy") if small else ("c", "x", "y"))


def _chip_reduce(grads, name, collective_id):
    nw = len(grads)
    step = 64

    def body(*refs):
        ins, outs = refs[:nw], refs[nw:2 * nw]
        own, got = refs[2 * nw:3 * nw], refs[3 * nw:4 * nw]
        send_sems, recv_sems, local_sems = refs[4 * nw:]
        x, y, c, _, _ = _place()
        barrier = pltpu.get_barrier_semaphore()
        pl.semaphore_signal(barrier, inc=1, device_id=(x, y, 1 - c), device_id_type=MESH)
        pl.semaphore_wait(barrier, 1)
        moves = []
        for w in range(nw):
            nb, rh = grads[w].shape[0], grads[w].shape[1] // 2
            for k in range(nb):
                away = _remote(ins[w].at[k, pl.ds((1 - c) * rh, rh), :], got[w].at[k], send_sems.at[w, k],
                               recv_sems.at[w, k], (x, y, 1 - c))
                mine = pltpu.make_async_copy(ins[w].at[k, pl.ds(c * rh, rh), :], own[w].at[k], local_sems.at[w, k])
                away.start()
                mine.start()
                moves.append((w, k, away, mine))
        back = []
        for w, k, away, mine in moves:
            rh = own[w].shape[1]
            mine.wait()
            away.wait()

            def add(i, carry, w=w, k=k):
                rows = pl.ds(pl.multiple_of(i * step, step), step)
                own[w][k, rows, :] = (own[w][k, rows, :].astype(F32) + got[w][k, rows, :].astype(F32)).astype(BF16)
                return carry
            lax.fori_loop(0, rh // step, add, 0)
            tail = rh % step
            if tail:
                rows = slice(rh - tail, rh)
                own[w][k, rows, :] = (own[w][k, rows, :].astype(F32) + got[w][k, rows, :].astype(F32)).astype(BF16)
            wb = pltpu.make_async_copy(own[w].at[k], outs[w].at[k, pl.ds(c * rh, rh), :], local_sems.at[w, k])
            wb.start()
            back.append(wb)
        for wb in back:
            wb.wait()

    halves = [pltpu.VMEM((g.shape[0], g.shape[1] // 2, g.shape[2]), BF16) for g in grads]
    sem = pltpu.SemaphoreType.DMA((nw, N_CHIPS))
    return pl.pallas_call(
        body, name=name, out_shape=tuple(jax.ShapeDtypeStruct(g.shape, BF16) for g in grads),
        in_specs=[ANY] * nw, out_specs=[ANY] * nw, scratch_shapes=halves + halves + [sem, sem, sem],
        compiler_params=pltpu.CompilerParams(vmem_limit_bytes=VMEM_LIMIT, collective_id=collective_id),
    )(*grads)


def _scatter_partials(parts):
    nw = len(parts)

    def copies(ins, outs, ss, rs):
        _, _, c, _, chips = _place()
        res = []
        for r, (px, py) in enumerate(chips):
            for w in range(nw):
                rh = parts[w].shape[1] // 2
                rows = pl.ds(c * rh, rh)
                res.append(_remote(ins[w].at[2 * px + py, rows], outs[w].at[r, rows], ss.at[w, r], rs.at[w, r],
                                   (px, py, c)))
        return res

    def start(ins, outs, ss, rs):
        for cp in copies(ins, outs, ss, rs):
            cp.start()

    def finish(ins, outs, ss, rs):
        for cp in copies(ins, outs, ss, rs):
            cp.wait()

    out_shapes = tuple(jax.ShapeDtypeStruct((3,) + p.shape[1:], p.dtype) for p in parts)
    return _Exchange(parts, out_shapes, (nw, 3), start, finish, peers=("x", "y", "xy"))


_HBM = pl.BlockSpec(memory_space=pltpu.HBM)
_SEM = pl.BlockSpec(memory_space=pltpu.SEMAPHORE)
_EFFECT = pltpu.SideEffectType.DATAFLOW_SIDE_EFFECTING


class _SemGrid:
    def __init__(self, refs, shape):
        assert len(refs) == math.prod(shape)
        self.refs, self.shape, self.at = refs, shape, self

    def __getitem__(self, idx):
        flat = 0
        for i, n in zip(idx, self.shape, strict=True):
            flat = flat * n + i
        return self.refs[flat]


def _split_phase(exs, arrays, sems, phase):
    a0 = s0 = 0
    for ex in exs:
        n_in, n_sem = len(ex.ins), math.prod(ex.sem_shape)
        n_arr = n_in if ex.in_place else n_in + len(ex.out_shapes)
        ins = arrays[a0:a0 + n_in]
        outs = ins if ex.in_place else arrays[a0 + n_in:a0 + n_arr]
        getattr(ex, phase)(ins, outs, _SemGrid(sems[s0:s0 + n_sem], ex.sem_shape),
                           _SemGrid(sems[s0 + n_sem:s0 + 2 * n_sem], ex.sem_shape))
        a0, s0 = a0 + n_arr, s0 + 2 * n_sem


def _split_start(exs, name, collective_id):
    assert all(ex.middle is None and ex.peers for ex in exs)
    arrays = [a for ex in exs for a in ex.ins + (() if ex.in_place else tuple(
        lax.empty(s.shape, s.dtype) for s in ex.out_shapes))]
    n_arr, n_sem = len(arrays), 2 * sum(math.prod(ex.sem_shape) for ex in exs)
    peers = frozenset().union(*[ex.peers for ex in exs])

    def body(*refs):
        _handshake(peers)
        _split_phase(exs, refs[:n_arr], refs[n_arr:n_arr + n_sem], "start")
        refs[-1][...] = jnp.zeros_like(refs[-1])

    res = pl.pallas_call(
        body, name=name + "_start",
        out_shape=(pltpu.SemaphoreType.DMA(()),) * n_sem + tuple(pltpu.HBM(a.shape, a.dtype) for a in arrays)
        + (jax.ShapeDtypeStruct((8, 128), F32),),
        in_specs=(_HBM,) * n_arr,
        out_specs=(_SEM,) * n_sem + (_HBM,) * n_arr + (pl.BlockSpec(memory_space=pltpu.VMEM),),
        input_output_aliases={k: n_sem + k for k in range(n_arr)},
        compiler_params=pltpu.CompilerParams(has_side_effects=_EFFECT, collective_id=collective_id),
    )(*[pltpu.with_memory_space_constraint(a, pltpu.HBM) for a in arrays])
    return res[:n_sem], res[n_sem:n_sem + n_arr], res[-1]


def _split_wait(exs, name, sems, thru, after):
    n_arr, n_sem = len(thru), len(sems)

    def body(*refs):
        _split_phase(exs, refs[:n_arr], refs[n_arr:n_arr + n_sem], "finish")

    return pl.pallas_call(
        body, name=name + "_wait", out_shape=tuple(pltpu.HBM(a.shape, a.dtype) for a in thru),
        in_specs=(_HBM,) * n_arr + (_SEM,) * n_sem + (ANY,) * len(after), out_specs=(_HBM,) * n_arr,
        input_output_aliases={k: k for k in range(n_arr)},
        compiler_params=pltpu.CompilerParams(has_side_effects=_EFFECT),
    )(*thru, *sems, *after)


def _join_partials(parts, slots):
    nw = len(parts)

    def copies(outs, ss, rs, mine):
        x, y, c, me, _ = _place()
        res = []
        for w in range(nw):
            rh = parts[w].shape[1] // 2
            rows = pl.ds((c if mine else 1 - c) * rh, rh)
            own = outs[w].at[me, rows]
            got = outs[nw + w].at[:, rows, :]
            res.append(_remote(own, own, ss.at[w, 0], rs.at[w, 0], (x, y, 1 - c)))
            res.append(_remote(got, got, ss.at[w, 1], rs.at[w, 1], (x, y, 1 - c)))
        return res

    def start(ins, outs, ss, rs):
        for cp in copies(outs, ss, rs, True):
            cp.start()

    def finish(ins, outs, ss, rs):
        for cp in copies(outs, ss, rs, True):
            cp.wait_send()
        for cp in copies(outs, ss, rs, False):
            cp.wait_recv()

    arrays = tuple(parts) + tuple(slots)
    return _Exchange(arrays, tuple(jax.ShapeDtypeStruct(a.shape, a.dtype) for a in arrays), (nw, 2), start, finish,
                     in_place=True, peers=("c",))


def _gather_small(slab):
    def copies(ins, outs, ss, rs):
        x, y, c, _, _ = _place()
        me = 4 * x + 2 * y + c
        out, arrivals = [], []
        for k in range(1, 8):
            px = 1 - x if k & 4 else x
            py = 1 - y if k & 2 else y
            pc = 1 - c if k & 1 else c
            out.append(_remote(ins[0], outs[0].at[me], ss.at[k - 1], rs.at[k - 1], (px, py, pc)))
            theirs = outs[0].at[4 * px + 2 * py + pc]
            arrivals.append((theirs, k - 1, (px, py, pc)))
        return pltpu.make_async_copy(ins[0], outs[0].at[me], ss.at[7]), out, arrivals

    def start(ins, outs, ss, rs):
        own, out, _ = copies(ins, outs, ss, rs)
        own.start()
        for cp in out:
            cp.start()

    def finish(ins, outs, ss, rs):
        own, out, arrivals = copies(ins, outs, ss, rs)
        for cp in out:
            cp.wait_send()
        for theirs, k, peer in arrivals:
            _remote(theirs, theirs, ss.at[k], rs.at[k], peer).wait_recv()
        own.wait()

    return _Exchange((slab,), (jax.ShapeDtypeStruct((8,) + slab.shape, slab.dtype),), (8,), start, finish)


_SMALL_VECS = ("ln_mix_g", "ln_attn_g", "ln_mem_g", "ln_ffn_g", "ln_final_g")


def _pack_small(p, extra, conv):
    d = p["ln_mix_g"].shape[-1]
    top = [p[k].reshape(1, d) for k in _SMALL_VECS]
    top.append(jnp.concatenate([p["sgu_ln_g"].reshape(-1), p["sgu_ln_b"].reshape(-1)]).reshape(1, d))
    top.append(jnp.concatenate([p["grp_norm_a"].reshape(-1), p["grp_norm_b"].reshape(-1)]).reshape(1, d))
    top.append(jnp.concatenate([p["b_spatial"].reshape(-1), extra]).reshape(1, d))
    mid = jnp.zeros((8, d), F32)
    if conv is not None:
        mid = jnp.pad(conv, ((0, 5), (0, d - conv.shape[1])))
    return jnp.concatenate([jnp.concatenate(top, axis=0), mid, p["w_spatial"].reshape(-1, d)], axis=0)


def _unpack_small(slab):
    d = slab.shape[1]
    hw = d // 2
    out = {k: slab[i] for i, k in enumerate(_SMALL_VECS)}
    out["sgu_ln_g"], out["sgu_ln_b"] = slab[5, :hw], slab[5, hw:]
    out["grp_norm_a"], out["grp_norm_b"] = slab[6, :hw], slab[6, hw:]
    out["b_spatial"] = slab[7, :hw].reshape(HEADS, CHUNK)
    out["w_spatial"] = slab[16:].reshape(HEADS, CHUNK, CHUNK)
    return out


_BIG = ("w_in", "w_kv", "w_gate_up", "w_out", "w_q", "w_o", "w_down")
_WEIGHTS = ("ln_mix_g", "w_in", "sgu_ln_g", "sgu_ln_b", "w_spatial", "b_spatial", "conv_w", "grp_norm_a",
            "grp_norm_b", "w_out", "ln_attn_g", "ln_mem_g", "w_q", "w_kv", "w_o", "ln_ffn_g", "w_gate_up",
            "w_down", "ln_final_g")


def _step(p, m_, v_, x, mem, target):
    s, d = x.shape
    hw = d // 2
    row = lambda a: a.reshape(1, -1)
    x_, y_, c_ = lax.axis_index("x"), lax.axis_index("y"), lax.axis_index("c")
    chip = 2 * x_ + y_

    conv8 = jnp.pad(p["conv_w"], ((0, 5), (0, 0)))
    later = ("w_kv", "w_q", "w_o", "w_down", "w_gate_up")
    first = _run_exchange(_all_gather([p["w_in"].astype(BF16), p["w_out"].astype(BF16)], [conv8]),
                          "all_gather_mixer", collective_id=4, casts=[p[k] for k in later])
    (w_in, w_out4, conv4), bf = first[:3], dict(zip(later, first[3:]))
    cw = jnp.transpose(conv4[:, :3, :], (1, 0, 2)).reshape(3, hw)
    b_t = jnp.pad(jnp.transpose(p["b_spatial"]), ((0, 0), (0, CHUNK - HEADS)))
    g1, g2, gm, g3, gf = (row(p[k]) for k in _SMALL_VECS)
    lng, lnb, ga, gb = row(p["sgu_ln_g"]), row(p["sgu_ln_b"]), row(p["grp_norm_a"]), row(p["grp_norm_b"])
    wsp = p["w_spatial"]
    w_out = w_out4.reshape(-1, d)

    (h, x1, ycat, xn1, mixed, th), (w_kv, w_q4, w_o4, w_down4) = _mixer_fwd(
        x, g1, w_in, lng, lnb, wsp, b_t, cw, ga, gb, w_out,
        hosted=_all_gather([bf[k] for k in ("w_kv", "w_q", "w_o", "w_down")]).with_id(5))
    w_q, w_o, w_down = (a.reshape(-1, d) for a in (w_q4, w_o4, w_down4))
    (x2, o, qs, probs, memn, kv), (w_gu,) = _attn_fwd(x1, mem, g2, gm, w_q, w_kv, w_o,
                                                      hosted=_all_gather([bf["w_gate_up"]]).with_id(6))
    dx2, act, dgu, xn3, dx3, loss, dgf, dg3 = _ffn_fwd_bwd(x2, g3, gf, target, w_gu, w_down)

    place = jnp.stack([c_, chip]).astype(jnp.int32)

    def chip_partials(names, grads, tag):
        return list(_chip_reduce(grads, "chip_reduce_" + tag, ("down", "ffn", "attn", "mixer").index(tag)))

    names_d = ("w_down",)
    parts_d = chip_partials(names_d, (_weight_grad(act, dx3, "grad_w_down", 1408, 512, False)[0],), "down")
    g_gu, slots_d = _weight_grad(xn3, dgu, "grad_w_gate_up", 512, 1408, True,
                                 hosted=_scatter_partials(parts_d).with_id(7))
    names_a = ("w_gate_up",)
    parts_a = chip_partials(names_a, (g_gu,), "ffn")
    (dx1, dkv, dg2, g_o, g_q, g_out), slots_a = _attn_bwd(x1, dx2, o, ycat, qs, probs, g2, w_q, kv, w_o,
                                                           hosted=_scatter_partials(parts_a).with_id(8))
    g_kv, dgm = _kv_bwd(dkv, mem, memn, gm, w_kv)
    names_b = ("w_o", "w_out", "w_q", "w_kv")
    shard_major = lambda g: g.reshape(N_CHIPS, -1, d)
    parts_b = chip_partials(names_b, (shard_major(g_o), shard_major(g_out), shard_major(g_q), g_kv), "attn")
    names_da = names_d + names_a
    exs_b = [_scatter_partials(parts_b), _join_partials(parts_d + parts_a, slots_d + slots_a)]
    sems_b, thru_b, token_b = _split_start(exs_b, "rs_scatter_attn", 9)
    dx, dh, dg1, dlng, dlnb, dwsp, dbt, dcw, dga, dgb = _mixer_bwd(
        x, dx1, h, mixed, th, g1, lng, lnb, wsp, b_t, cw, ga, gb, w_out, w_in, after=token_b)
    thru_b = _split_wait(exs_b, "rs_scatter_attn", sems_b, thru_b, [dh])
    nb = len(names_b)
    parts_b, slots_b, joined = thru_b[:nb], thru_b[nb:2 * nb], thru_b[2 * nb:]
    whole = dict(zip(names_da, zip(joined[:len(names_da)], joined[len(names_da):])))
    small = {"ln_mix_g": dg1, "ln_attn_g": dg2, "ln_mem_g": dgm, "ln_ffn_g": dg3, "ln_final_g": dgf,
             "sgu_ln_g": dlng, "sgu_ln_b": dlnb, "grp_norm_a": dga, "grp_norm_b": dgb,
             "b_spatial": jnp.transpose(dbt[:, :HEADS]), "w_spatial": dwsp}
    loss_vec = jnp.pad(loss.reshape(1), (0, hw - 1))
    g_in, extra = _weight_grad(
        xn1, dh, "grad_w_in", 1024, 640, True,
        hosted=[_gather_small(_pack_small(small, loss_vec, dcw)), _join_partials(parts_b, slots_b)])
    parts = extra[0]
    whole.update(zip(names_b, zip(extra[1:1 + len(names_b)], extra[1 + len(names_b):])))
    (part_in,) = chip_partials(("w_in",), (g_in,), "mixer")
    out_g, out_d, out_m, out_v = {}, {}, {}, {}

    def finalize(ks, tag, after=None):
        done, _ = _finalize([(whole[k][0], whole[k][1], p[k], m_[k], v_[k]) for k in ks], place,
                            "finalize_" + tag, after=after)
        for k, (g, dl, nm, nv) in zip(ks, done):
            out_g[k], out_d[k], out_m[k], out_v[k] = g, dl, nm, nv

    exs_in = [_scatter_partials([part_in])]
    sems, thru_in, token = _split_start(exs_in, "rs_scatter_in", 10)
    finalize(("w_down",), "w_down", after=token)
    finalize(("w_o", "w_out", "w_q"), "attn", after=token)
    finalize(("w_gate_up",), "w_gate_up", after=token)
    finalize(("w_kv",), "w_kv", after=token)
    part_in, slots_in = _split_wait(exs_in, "rs_scatter_in", sems, thru_in,
                                    [out_v[k] for k in ("w_down", "w_o", "w_gate_up", "w_kv")])
    whole["w_in"] = _run_exchange(_join_partials([part_in], [slots_in]), "rs_join_mixer", collective_id=11)
    finalize(("w_in",), "w_in")

    zeros = jnp.zeros((hw,), F32)
    sg, sd, sm, sv = _small_sum_adamw(parts, _pack_small(p, zeros, None), _pack_small(m_, zeros, None),
                                      _pack_small(v_, zeros, None))
    for tree, slab in zip((out_g, out_d, out_m, out_v), (sg, sd, sm, sv)):
        tree.update(_unpack_small(slab))
    loss_out = sg[7, hw]
    g_conv = lax.dynamic_slice(sg[8:11, :hw], (0, chip * (hw // N_CHIPS)), (3, hw // N_CHIPS))
    out_g["conv_w"] = g_conv
    out_d["conv_w"], out_m["conv_w"], out_v["conv_w"] = _adamw(p["conv_w"], g_conv, m_["conv_w"], v_["conv_w"],
                                                                "adamw_conv_w")
    return loss_out, dx, out_g, out_d, out_m, out_v


def kernel(x, mem, ln_mix_g, w_in, sgu_ln_g, sgu_ln_b, w_spatial, b_spatial, conv_w, grp_norm_a, grp_norm_b, w_out, ln_attn_g, ln_mem_g, w_q, w_kv, w_o, ln_ffn_g, w_gate_up, w_down, ln_final_g, loss_target, m_ln_mix_g, m_w_in, m_sgu_ln_g, m_sgu_ln_b, m_w_spatial, m_b_spatial, m_conv_w, m_grp_norm_a, m_grp_norm_b, m_w_out, m_ln_attn_g, m_ln_mem_g, m_w_q, m_w_kv, m_w_o, m_ln_ffn_g, m_w_gate_up, m_w_down, m_ln_final_g, v_ln_mix_g, v_w_in, v_sgu_ln_g, v_sgu_ln_b, v_w_spatial, v_b_spatial, v_conv_w, v_grp_norm_a, v_grp_norm_b, v_w_out, v_ln_attn_g, v_ln_mem_g, v_w_q, v_w_kv, v_w_o, v_ln_ffn_g, v_w_gate_up, v_w_down, v_ln_final_g):
    p = dict(ln_mix_g=ln_mix_g, w_in=w_in, sgu_ln_g=sgu_ln_g, sgu_ln_b=sgu_ln_b, w_spatial=w_spatial,
             b_spatial=b_spatial, conv_w=conv_w, grp_norm_a=grp_norm_a, grp_norm_b=grp_norm_b, w_out=w_out,
             ln_attn_g=ln_attn_g, ln_mem_g=ln_mem_g, w_q=w_q, w_kv=w_kv, w_o=w_o, ln_ffn_g=ln_ffn_g,
             w_gate_up=w_gate_up, w_down=w_down, ln_final_g=ln_final_g)
    m_ = dict(ln_mix_g=m_ln_mix_g, w_in=m_w_in, sgu_ln_g=m_sgu_ln_g, sgu_ln_b=m_sgu_ln_b, w_spatial=m_w_spatial,
              b_spatial=m_b_spatial, conv_w=m_conv_w, grp_norm_a=m_grp_norm_a, grp_norm_b=m_grp_norm_b,
              w_out=m_w_out, ln_attn_g=m_ln_attn_g, ln_mem_g=m_ln_mem_g, w_q=m_w_q, w_kv=m_w_kv, w_o=m_w_o,
              ln_ffn_g=m_ln_ffn_g, w_gate_up=m_w_gate_up, w_down=m_w_down, ln_final_g=m_ln_final_g)
    v_ = dict(ln_mix_g=v_ln_mix_g, w_in=v_w_in, sgu_ln_g=v_sgu_ln_g, sgu_ln_b=v_sgu_ln_b, w_spatial=v_w_spatial,
              b_spatial=v_b_spatial, conv_w=v_conv_w, grp_norm_a=v_grp_norm_a, grp_norm_b=v_grp_norm_b,
              w_out=v_w_out, ln_attn_g=v_ln_attn_g, ln_mem_g=v_ln_mem_g, w_q=v_w_q, w_kv=v_w_kv, w_o=v_w_o,
              ln_ffn_g=v_ln_ffn_g, w_gate_up=v_w_gate_up, w_down=v_w_down, ln_final_g=v_ln_final_g)
    s, d = x.shape[-2], x.shape[-1]
    loss, dx, g, dl, nm, nv = _step(p, m_, v_, x.reshape(s, d), mem.reshape(-1, d), loss_target.reshape(s, d))
    outs = [loss, dx.reshape(x.shape)]
    for tree in (g, dl, nm, nv):
        outs += [tree[k].reshape(p[k].shape) for k in _WEIGHTS]
    return tuple(outs)
```
